```python
import jax, jax.numpy as jnp
from jax import lax
import numpy as np

D_MODEL = 1024
BATCH = 8
SEQ = 2048
DEPTH = 1

GRID_W = 64
CTX_LEN = 256
LRU_WIDTH = D_MODEL
LRU_HEADS = 8
LRU_HEAD_DIM = LRU_WIDTH // LRU_HEADS
CONV_WIDTH = 4
CONV_PAD_LEFT = 1
LRU_C = 8.0
SGU_WIDTH = D_MODEL
SGU_GROUPS = 8
SGU_GROUP_DIM = SGU_WIDTH // SGU_GROUPS
CHUNK = 128
D_MIX = LRU_WIDTH + SGU_WIDTH
D_IN = 2 * LRU_WIDTH + 3 * SGU_WIDTH
NORM_EPS = 1e-6
LN_EPS = 1e-5

kernel_name = "hybrid_rglru_chunk_sgu_dit_block"


def rmsnorm(x, g):
    xf = x.astype(jnp.float32)
    y = xf * lax.rsqrt(jnp.mean(xf * xf, axis=-1, keepdims=True) + NORM_EPS)
    return (y * g.astype(jnp.float32)).astype(x.dtype)


def ada_mod(cond, w, b):
    m = jax.nn.silu(cond) @ w + b
    return jnp.split(m, 3, axis=-1)


def project(h, shift, scale, norm_g, w_in):
    hn = rmsnorm(h, norm_g) * (1.0 + scale) + shift
    return hn @ w_in


def short_conv(xa, w, b):
    L = xa.shape[1]
    xp = jnp.pad(xa, ((0, 0), (CONV_PAD_LEFT, CONV_WIDTH - 1 - CONV_PAD_LEFT), (0, 0)))
    y = xp[:, 0:L] * w[0]
    for k in range(1, CONV_WIDTH):
        y = y + xp[:, k:k + L] * w[k]
    return y + b


def _lin_combine(e1, e2):
    a1, b1 = e1
    a2, b2 = e2
    return a1 * a2, a2 * b1 + b2


def rglru_direction(xc, h0, wa, ba, wx, bx, lam, reverse):
    Bn, L, _ = xc.shape
    xh = xc.reshape(Bn, L, LRU_HEADS, LRU_HEAD_DIM)
    r = jax.nn.sigmoid(jnp.einsum('blhi,hij->blhj', xh, wa) + ba).reshape(Bn, L, LRU_WIDTH)
    i = jax.nn.sigmoid(jnp.einsum('blhi,hij->blhj', xh, wx) + bx).reshape(Bn, L, LRU_WIDTH)
    log_a = -LRU_C * r * jax.nn.softplus(-lam.astype(jnp.float32))
    a = jnp.exp(log_a)
    u = jnp.sqrt(-jnp.expm1(2.0 * log_a)) * (i * xc)
    a_cum, h = lax.associative_scan(_lin_combine, (a, u), reverse=reverse, axis=1)
    h = h + a_cum * h0[:, None, :]
    final = h[:, 0] if reverse else h[:, -1]
    return h, final


def rglru_bidir(xc, h0f, h0b, wa, ba, wx, bx, lam):
    hf, ff = rglru_direction(xc, h0f, wa[0], ba[0], wx[0], bx[0], lam[0], False)
    hb, fb = rglru_direction(xc, h0b, wa[1], ba[1], wx[1], bx[1], lam[1], True)
    return hf + hb, ff, fb


def chunk_sgu(u, v, ln_g, ln_b, w_s, b_s, n_chunks):
    Bn = u.shape[0]
    vf = v.astype(jnp.float32)
    mu = jnp.mean(vf, axis=-1, keepdims=True)
    var = jnp.mean(jnp.square(vf - mu), axis=-1, keepdims=True)
    vn = (vf - mu) * lax.rsqrt(var + LN_EPS) * ln_g + ln_b
    vc = vn.reshape(Bn, n_chunks, CHUNK, SGU_GROUPS, SGU_GROUP_DIM)
    mixed = jnp.einsum('gpq,bnqgc->bnpgc', w_s, vc) + b_s.T[None, None, :, :, None]
    return u * mixed.reshape(Bn, n_chunks * CHUNK, SGU_WIDTH).astype(u.dtype)


def split_proj(z):
    W, S = LRU_WIDTH, SGU_WIDTH
    return (z[..., :W], z[..., W:2 * W], z[..., 2 * W:2 * W + S],
            z[..., 2 * W + S:2 * W + 2 * S], z[..., 2 * W + 2 * S:])


def mixer_out(y_lru, ga, y_sgu, gb, w_out):
    y = jnp.concatenate([y_lru * jax.nn.silu(ga), y_sgu * jax.nn.silu(gb)], axis=-1)
    return y @ w_out


def _fwd_setup_inputs(seed: int = 0) -> dict:
    key = jax.random.key(seed)
    ks = jax.random.split(key, 24)
    nrm = jax.random.normal
    a_c = jax.random.uniform(ks[13], (DEPTH, 2, LRU_WIDTH), minval=0.9, maxval=0.999)
    s = a_c ** (1.0 / LRU_C)
    return {
        "x": nrm(ks[0], (BATCH, SEQ, D_MODEL)),
        "c": nrm(ks[1], (BATCH, D_MODEL)),
        "ctx": nrm(ks[2], (BATCH, CTX_LEN, D_MODEL)),
        "c_ctx": nrm(ks[3], (D_MODEL,)),
        "ada_w": nrm(ks[4], (DEPTH, D_MODEL, 3 * D_MODEL)) * (0.5 * D_MODEL ** -0.5),
        "ada_b": 0.01 * nrm(ks[5], (DEPTH, 3 * D_MODEL)),
        "norm_g": 1.0 + 0.05 * nrm(ks[6], (DEPTH, D_MODEL)),
        "w_in": nrm(ks[7], (DEPTH, D_MODEL, D_IN)) * D_MODEL ** -0.5,
        "conv_w": nrm(ks[8], (DEPTH, CONV_WIDTH, LRU_WIDTH)) * CONV_WIDTH ** -0.5,
        "conv_b": 0.01 * nrm(ks[9], (DEPTH, LRU_WIDTH)),
        "lru_wa": nrm(ks[10], (DEPTH, 2, LRU_HEADS, LRU_HEAD_DIM, LRU_HEAD_DIM)) * LRU_HEAD_DIM ** -0.5,
        "lru_ba": 0.01 * nrm(ks[11], (DEPTH, 2, LRU_HEADS, LRU_HEAD_DIM)),
        "lru_wx": nrm(ks[12], (DEPTH, 2, LRU_HEADS, LRU_HEAD_DIM, LRU_HEAD_DIM)) * LRU_HEAD_DIM ** -0.5,
        "lru_bx": 0.01 * nrm(ks[14], (DEPTH, 2, LRU_HEADS, LRU_HEAD_DIM)),
        "lru_lambda": jnp.log(s) - jnp.log1p(-s),
        "sgu_ln_g": 1.0 + 0.05 * nrm(ks[15], (DEPTH, SGU_WIDTH)),
        "sgu_ln_b": 0.01 * nrm(ks[16], (DEPTH, SGU_WIDTH)),
        "sgu_w": nrm(ks[17], (DEPTH, SGU_GROUPS, CHUNK, CHUNK)) * (0.5 * CHUNK ** -0.5),
        "sgu_b": 1.0 + 0.1 * nrm(ks[18], (DEPTH, SGU_GROUPS, CHUNK)),
        "w_out": nrm(ks[19], (DEPTH, D_MIX, D_MODEL)) * D_MIX ** -0.5,
        "final_g": 1.0 + 0.05 * nrm(ks[20], (D_MODEL,)),
    }


def _fwd_reference(x, c, ctx, c_ctx, ada_w, ada_b, norm_g, w_in, conv_w, conv_b, lru_wa, lru_ba,
              lru_wx, lru_bx, lru_lambda, sgu_ln_g, sgu_ln_b, sgu_w, sgu_b, w_out, final_g):
    Bn, L, _ = x.shape
    rows = L // GRID_W
    n_chunks = rows * GRID_W // CHUNK
    n_ctx_chunks = ctx.shape[1] // CHUNK
    zeros = jnp.zeros((Bn, LRU_WIDTH), jnp.float32)
    for layer in range(DEPTH):
        sh_x, sc_x, g_x = ada_mod(c[:, None, :], ada_w[layer], ada_b[layer])
        sh_c, sc_c, g_c = ada_mod(c_ctx[None, None, :], ada_w[layer], ada_b[layer])
        lru_p = (lru_wa[layer], lru_ba[layer], lru_wx[layer], lru_bx[layer], lru_lambda[layer])
        last = layer == DEPTH - 1

        ctx_cols = LRU_WIDTH if last else D_IN
        zc = project(ctx, sh_c, sc_c, norm_g[layer], w_in[layer][:, :ctx_cols])
        xc_c = short_conv(zc[..., :LRU_WIDTH], conv_w[layer], conv_b[layer]).astype(jnp.float32)
        y_c, hf_c, hb_c = rglru_bidir(xc_c, zeros, zeros, *lru_p)

        zx = project(x, sh_x, sc_x, norm_g[layer], w_in[layer])
        xa_x, ga_x, u_x, v_x, gb_x = split_proj(zx)
        xc_x = short_conv(xa_x, conv_w[layer], conv_b[layer]).astype(jnp.float32)
        y_l, _, _ = rglru_bidir(xc_x, hf_c, hb_c, *lru_p)
        y_s = chunk_sgu(jax.nn.gelu(u_x), jax.nn.gelu(v_x), sgu_ln_g[layer], sgu_ln_b[layer],
                        sgu_w[layer], sgu_b[layer], n_chunks)
        x_new = x + g_x * mixer_out(y_l.astype(x.dtype), ga_x, y_s, gb_x, w_out[layer])

        if not last:
            _, ga_c, u_c, v_c, gb_c = split_proj(zc)
            y_sc = chunk_sgu(jax.nn.gelu(u_c), jax.nn.gelu(v_c), sgu_ln_g[layer], sgu_ln_b[layer],
                             sgu_w[layer], sgu_b[layer], n_ctx_chunks)
            ctx = ctx + g_c * mixer_out(y_c.astype(ctx.dtype), ga_c, y_sc, gb_c, w_out[layer])
        x = x_new
    return rmsnorm(x, final_g)


import jax as _jax
import jax.numpy as _jnp

TWIN_FORMAT = 'train_step'
FWD_PARAMS = ['x', 'c', 'ctx', 'c_ctx', 'ada_w', 'ada_b', 'norm_g', 'w_in', 'conv_w', 'conv_b', 'lru_wa', 'lru_ba', 'lru_wx', 'lru_bx', 'lru_lambda', 'sgu_ln_g', 'sgu_ln_b', 'sgu_w', 'sgu_b', 'w_out', 'final_g']
TWIN_WEIGHTS = ['c_ctx', 'ada_w', 'ada_b', 'norm_g', 'w_in', 'conv_w', 'conv_b', 'lru_wa', 'lru_ba', 'lru_wx', 'lru_bx', 'lru_lambda', 'sgu_ln_g', 'sgu_ln_b', 'sgu_w', 'sgu_b', 'w_out', 'final_g']
TWIN_DIFF_INPUT = 'x'
TWIN_INPUTS = ['x', 'c', 'ctx', 'c_ctx', 'ada_w', 'ada_b', 'norm_g', 'w_in', 'conv_w', 'conv_b', 'lru_wa', 'lru_ba', 'lru_wx', 'lru_bx', 'lru_lambda', 'sgu_ln_g', 'sgu_ln_b', 'sgu_w', 'sgu_b', 'w_out', 'final_g', 'loss_target', 'm_c_ctx', 'm_ada_w', 'm_ada_b', 'm_norm_g', 'm_w_in', 'm_conv_w', 'm_conv_b', 'm_lru_wa', 'm_lru_ba', 'm_lru_wx', 'm_lru_bx', 'm_lru_lambda', 'm_sgu_ln_g', 'm_sgu_ln_b', 'm_sgu_w', 'm_sgu_b', 'm_w_out', 'm_final_g', 'v_c_ctx', 'v_ada_w', 'v_ada_b', 'v_norm_g', 'v_w_in', 'v_conv_w', 'v_conv_b', 'v_lru_wa', 'v_lru_ba', 'v_lru_wx', 'v_lru_bx', 'v_lru_lambda', 'v_sgu_ln_g', 'v_sgu_ln_b', 'v_sgu_w', 'v_sgu_b', 'v_w_out', 'v_final_g']
TWIN_OUTPUTS = ['loss', 'grad_x', 'grad_c_ctx', 'grad_ada_w', 'grad_ada_b', 'grad_norm_g', 'grad_w_in', 'grad_conv_w', 'grad_conv_b', 'grad_lru_wa', 'grad_lru_ba', 'grad_lru_wx', 'grad_lru_bx', 'grad_lru_lambda', 'grad_sgu_ln_g', 'grad_sgu_ln_b', 'grad_sgu_w', 'grad_sgu_b', 'grad_w_out', 'grad_final_g', 'delta_c_ctx', 'delta_ada_w', 'delta_ada_b', 'delta_norm_g', 'delta_w_in', 'delta_conv_w', 'delta_conv_b', 'delta_lru_wa', 'delta_lru_ba', 'delta_lru_wx', 'delta_lru_bx', 'delta_lru_lambda', 'delta_sgu_ln_g', 'delta_sgu_ln_b', 'delta_sgu_w', 'delta_sgu_b', 'delta_w_out', 'delta_final_g', 'new_m_c_ctx', 'new_m_ada_w', 'new_m_ada_b', 'new_m_norm_g', 'new_m_w_in', 'new_m_conv_w', 'new_m_conv_b', 'new_m_lru_wa', 'new_m_lru_ba', 'new_m_lru_wx', 'new_m_lru_bx', 'new_m_lru_lambda', 'new_m_sgu_ln_g', 'new_m_sgu_ln_b', 'new_m_sgu_w', 'new_m_sgu_b', 'new_m_w_out', 'new_m_final_g', 'new_v_c_ctx', 'new_v_ada_w', 'new_v_ada_b', 'new_v_norm_g', 'new_v_w_in', 'new_v_conv_w', 'new_v_conv_b', 'new_v_lru_wa', 'new_v_lru_ba', 'new_v_lru_wx', 'new_v_lru_bx', 'new_v_lru_lambda', 'new_v_sgu_ln_g', 'new_v_sgu_ln_b', 'new_v_sgu_w', 'new_v_sgu_b', 'new_v_w_out', 'new_v_final_g']
TWIN_LEAF_KINDS = {'loss': 'loss', 'grad_x': 'grad_x', 'grad_c_ctx': 'grad_w', 'grad_ada_w': 'grad_w', 'grad_ada_b': 'grad_w', 'grad_norm_g': 'grad_w', 'grad_w_in': 'grad_w', 'grad_conv_w': 'grad_w', 'grad_conv_b': 'grad_w', 'grad_lru_wa': 'grad_w', 'grad_lru_ba': 'grad_w', 'grad_lru_wx': 'grad_w', 'grad_lru_bx': 'grad_w', 'grad_lru_lambda': 'grad_w', 'grad_sgu_ln_g': 'grad_w', 'grad_sgu_ln_b': 'grad_w', 'grad_sgu_w': 'grad_w', 'grad_sgu_b': 'grad_w', 'grad_w_out': 'grad_w', 'grad_final_g': 'grad_w', 'delta_c_ctx': 'delta_w', 'delta_ada_w': 'delta_w', 'delta_ada_b': 'delta_w', 'delta_norm_g': 'delta_w', 'delta_w_in': 'delta_w', 'delta_conv_w': 'delta_w', 'delta_conv_b': 'delta_w', 'delta_lru_wa': 'delta_w', 'delta_lru_ba': 'delta_w', 'delta_lru_wx': 'delta_w', 'delta_lru_bx': 'delta_w', 'delta_lru_lambda': 'delta_w', 'delta_sgu_ln_g': 'delta_w', 'delta_sgu_ln_b': 'delta_w', 'delta_sgu_w': 'delta_w', 'delta_sgu_b': 'delta_w', 'delta_w_out': 'delta_w', 'delta_final_g': 'delta_w', 'new_m_c_ctx': 'new_m', 'new_m_ada_w': 'new_m', 'new_m_ada_b': 'new_m', 'new_m_norm_g': 'new_m', 'new_m_w_in': 'new_m', 'new_m_conv_w': 'new_m', 'new_m_conv_b': 'new_m', 'new_m_lru_wa': 'new_m', 'new_m_lru_ba': 'new_m', 'new_m_lru_wx': 'new_m', 'new_m_lru_bx': 'new_m', 'new_m_lru_lambda': 'new_m', 'new_m_sgu_ln_g': 'new_m', 'new_m_sgu_ln_b': 'new_m', 'new_m_sgu_w': 'new_m', 'new_m_sgu_b': 'new_m', 'new_m_w_out': 'new_m', 'new_m_final_g': 'new_m', 'new_v_c_ctx': 'new_v', 'new_v_ada_w': 'new_v', 'new_v_ada_b': 'new_v', 'new_v_norm_g': 'new_v', 'new_v_w_in': 'new_v', 'new_v_conv_w': 'new_v', 'new_v_conv_b': 'new_v', 'new_v_lru_wa': 'new_v', 'new_v_lru_ba': 'new_v', 'new_v_lru_wx': 'new_v', 'new_v_lru_bx': 'new_v', 'new_v_lru_lambda': 'new_v', 'new_v_sgu_ln_g': 'new_v', 'new_v_sgu_ln_b': 'new_v', 'new_v_sgu_w': 'new_v', 'new_v_sgu_b': 'new_v', 'new_v_w_out': 'new_v', 'new_v_final_g': 'new_v'}


def _forward(args):
    return _fwd_reference(*[args[k] for k in FWD_PARAMS])


def _output_shape():
    out = _jax.eval_shape(lambda: _forward(_fwd_setup_inputs(0)))
    return out.shape, out.dtype

N_MICROBATCH = 1
ADAM_LR = 0.001
ADAM_B1 = 0.9
ADAM_B2 = 0.999
ADAM_EPS = 1e-08
ADAM_WD = 0.01
ADAM_STEP = 10
PER_EXAMPLE_BATCH_AXIS = {'x': 0, 'c': 0, 'ctx': 0, 'loss_target': 0}
SHARED_INPUTS = []
_WEIGHT_DTYPES = {'c_ctx': _jnp.float32, 'ada_w': _jnp.float32, 'ada_b': _jnp.float32, 'norm_g': _jnp.float32, 'w_in': _jnp.float32, 'conv_w': _jnp.float32, 'conv_b': _jnp.float32, 'lru_wa': _jnp.float32, 'lru_ba': _jnp.float32, 'lru_wx': _jnp.float32, 'lru_bx': _jnp.float32, 'lru_lambda': _jnp.float32, 'sgu_ln_g': _jnp.float32, 'sgu_ln_b': _jnp.float32, 'sgu_w': _jnp.float32, 'sgu_b': _jnp.float32, 'w_out': _jnp.float32, 'final_g': _jnp.float32}
MOMENT_SCALE = {'c_ctx': 1.686017e-02, 'ada_w': 1.871556e-01, 'ada_b': 3.345774e-01, 'norm_g': 6.965875e-02, 'w_in': 4.397714e-02, 'conv_w': 7.596534e-02, 'conv_b': 2.357405e-01, 'lru_wa': 4.097201e-03, 'lru_ba': 6.230832e-03, 'lru_wx': 8.502193e-03, 'lru_bx': 1.624558e-02, 'lru_lambda': 1.521435e-02, 'sgu_ln_g': 6.343277e-03, 'sgu_ln_b': 6.334371e-03, 'sgu_w': 1.233489e-02, 'sgu_b': 1.237540e-02, 'w_out': 6.971815e-02, 'final_g': 1.619384e+01}


def _to_microbatches(a, axis):
    t = _jnp.moveaxis(a, axis, 0)
    t = t.reshape((N_MICROBATCH, t.shape[0] // N_MICROBATCH) + t.shape[1:])
    return _jnp.moveaxis(t, 1, axis + 1)


def setup_inputs(seed: int = 0) -> dict:
    inp = _fwd_setup_inputs(seed)
    key = _jax.random.fold_in(_jax.random.key(seed), 7919)
    shape, _ = _output_shape()
    out = dict(inp)
    out["loss_target"] = _jax.random.normal(_jax.random.fold_in(key, 0), shape, _jnp.float32)
    for i, name in enumerate(TWIN_WEIGHTS):
        w = inp[name].astype(_jnp.float32)
        if MOMENT_SCALE is None:
            s = _jnp.sqrt(_jnp.mean(_jnp.square(w)) + 1e-30)
        else:
            s = MOMENT_SCALE[name]
        km, kv = _jax.random.split(_jax.random.fold_in(key, i + 1))
        out[name] = w
        out["m_" + name] = s * _jax.random.normal(km, w.shape, _jnp.float32)
        out["v_" + name] = (s * s) * _jax.random.uniform(kv, w.shape, _jnp.float32, 0.5, 1.5)
    if N_MICROBATCH > 1:
        for name, axis in PER_EXAMPLE_BATCH_AXIS.items():
            out[name] = _to_microbatches(out[name], axis)
    return {'x': out['x'], 'c': out['c'], 'ctx': out['ctx'], 'c_ctx': out['c_ctx'], 'ada_w': out['ada_w'], 'ada_b': out['ada_b'], 'norm_g': out['norm_g'], 'w_in': out['w_in'], 'conv_w': out['conv_w'], 'conv_b': out['conv_b'], 'lru_wa': out['lru_wa'], 'lru_ba': out['lru_ba'], 'lru_wx': out['lru_wx'], 'lru_bx': out['lru_bx'], 'lru_lambda': out['lru_lambda'], 'sgu_ln_g': out['sgu_ln_g'], 'sgu_ln_b': out['sgu_ln_b'], 'sgu_w': out['sgu_w'], 'sgu_b': out['sgu_b'], 'w_out': out['w_out'], 'final_g': out['final_g'], 'loss_target': out['loss_target'], 'm_c_ctx': out['m_c_ctx'], 'm_ada_w': out['m_ada_w'], 'm_ada_b': out['m_ada_b'], 'm_norm_g': out['m_norm_g'], 'm_w_in': out['m_w_in'], 'm_conv_w': out['m_conv_w'], 'm_conv_b': out['m_conv_b'], 'm_lru_wa': out['m_lru_wa'], 'm_lru_ba': out['m_lru_ba'], 'm_lru_wx': out['m_lru_wx'], 'm_lru_bx': out['m_lru_bx'], 'm_lru_lambda': out['m_lru_lambda'], 'm_sgu_ln_g': out['m_sgu_ln_g'], 'm_sgu_ln_b': out['m_sgu_ln_b'], 'm_sgu_w': out['m_sgu_w'], 'm_sgu_b': out['m_sgu_b'], 'm_w_out': out['m_w_out'], 'm_final_g': out['m_final_g'], 'v_c_ctx': out['v_c_ctx'], 'v_ada_w': out['v_ada_w'], 'v_ada_b': out['v_ada_b'], 'v_norm_g': out['v_norm_g'], 'v_w_in': out['v_w_in'], 'v_conv_w': out['v_conv_w'], 'v_conv_b': out['v_conv_b'], 'v_lru_wa': out['v_lru_wa'], 'v_lru_ba': out['v_lru_ba'], 'v_lru_wx': out['v_lru_wx'], 'v_lru_bx': out['v_lru_bx'], 'v_lru_lambda': out['v_lru_lambda'], 'v_sgu_ln_g': out['v_sgu_ln_g'], 'v_sgu_ln_b': out['v_sgu_ln_b'], 'v_sgu_w': out['v_sgu_w'], 'v_sgu_b': out['v_sgu_b'], 'v_w_out': out['v_w_out'], 'v_final_g': out['v_final_g']}


def _loss(weights, diff, rest, loss_target):
    with _jax.named_scope("forward"):
        args = {**rest, TWIN_DIFF_INPUT: diff, **{k: w.astype(_WEIGHT_DTYPES[k]) for k, w in weights.items()}}
        y = _forward(args)
    with _jax.named_scope("loss_head"):
        err = _jnp.square(y.astype(_jnp.float32) - loss_target)
        return 0.5 * _jnp.sum(_jnp.mean(err, axis=-1)) if err.ndim else 0.5 * err


def _adamw(w, g, m, v):
    m = ADAM_B1 * m + (1.0 - ADAM_B1) * g
    v = ADAM_B2 * v + (1.0 - ADAM_B2) * _jnp.square(g)
    m_hat = m / (1.0 - ADAM_B1 ** ADAM_STEP)
    v_hat = v / (1.0 - ADAM_B2 ** ADAM_STEP)
    delta = -ADAM_LR * (m_hat / (_jnp.sqrt(v_hat) + ADAM_EPS) + ADAM_WD * w)
    return delta, m, v


def reference(x, c, ctx, c_ctx, ada_w, ada_b, norm_g, w_in, conv_w, conv_b, lru_wa, lru_ba, lru_wx, lru_bx, lru_lambda, sgu_ln_g, sgu_ln_b, sgu_w, sgu_b, w_out, final_g, loss_target, m_c_ctx, m_ada_w, m_ada_b, m_norm_g, m_w_in, m_conv_w, m_conv_b, m_lru_wa, m_lru_ba, m_lru_wx, m_lru_bx, m_lru_lambda, m_sgu_ln_g, m_sgu_ln_b, m_sgu_w, m_sgu_b, m_w_out, m_final_g, v_c_ctx, v_ada_w, v_ada_b, v_norm_g, v_w_in, v_conv_w, v_conv_b, v_lru_wa, v_lru_ba, v_lru_wx, v_lru_bx, v_lru_lambda, v_sgu_ln_g, v_sgu_ln_b, v_sgu_w, v_sgu_b, v_w_out, v_final_g):
    given = dict(x=x, c=c, ctx=ctx, c_ctx=c_ctx, ada_w=ada_w, ada_b=ada_b, norm_g=norm_g, w_in=w_in, conv_w=conv_w, conv_b=conv_b, lru_wa=lru_wa, lru_ba=lru_ba, lru_wx=lru_wx, lru_bx=lru_bx, lru_lambda=lru_lambda, sgu_ln_g=sgu_ln_g, sgu_ln_b=sgu_ln_b, sgu_w=sgu_w, sgu_b=sgu_b, w_out=w_out, final_g=final_g, loss_target=loss_target, m_c_ctx=m_c_ctx, m_ada_w=m_ada_w, m_ada_b=m_ada_b, m_norm_g=m_norm_g, m_w_in=m_w_in, m_conv_w=m_conv_w, m_conv_b=m_conv_b, m_lru_wa=m_lru_wa, m_lru_ba=m_lru_ba, m_lru_wx=m_lru_wx, m_lru_bx=m_lru_bx, m_lru_lambda=m_lru_lambda, m_sgu_ln_g=m_sgu_ln_g, m_sgu_ln_b=m_sgu_ln_b, m_sgu_w=m_sgu_w, m_sgu_b=m_sgu_b, m_w_out=m_w_out, m_final_g=m_final_g, v_c_ctx=v_c_ctx, v_ada_w=v_ada_w, v_ada_b=v_ada_b, v_norm_g=v_norm_g, v_w_in=v_w_in, v_conv_w=v_conv_w, v_conv_b=v_conv_b, v_lru_wa=v_lru_wa, v_lru_ba=v_lru_ba, v_lru_wx=v_lru_wx, v_lru_bx=v_lru_bx, v_lru_lambda=v_lru_lambda, v_sgu_ln_g=v_sgu_ln_g, v_sgu_ln_b=v_sgu_ln_b, v_sgu_w=v_sgu_w, v_sgu_b=v_sgu_b, v_w_out=v_w_out, v_final_g=v_final_g)
    weights = {n: given[n] for n in TWIN_WEIGHTS}
    shared = {n: given[n] for n in SHARED_INPUTS}
    per_example = {n: given[n] for n in ['x', 'c', 'ctx']}
    grad_fn = _jax.value_and_grad(_loss, argnums=(0, 1))

    def one_microbatch(ex, loss_target):
        ex = dict(ex)
        diff = ex.pop(TWIN_DIFF_INPUT)
        return grad_fn(weights, diff, {**shared, **ex}, loss_target)

    if N_MICROBATCH == 1:
        loss, (grad_w, grad_x) = one_microbatch(per_example, given["loss_target"])
    else:
        def body(carry, xs):
            loss_sum, grad_sum = carry
            l_k, (gw_k, gx_k) = one_microbatch(xs[0], xs[1])
            with _jax.named_scope("update"):
                return (loss_sum + l_k, _jax.tree.map(_jnp.add, grad_sum, gw_k)), gx_k

        init = (_jnp.zeros((), _jnp.float32), _jax.tree.map(_jnp.zeros_like, weights))
        (loss, grad_w), grad_x = _jax.lax.scan(body, init, (per_example, given["loss_target"]))
    with _jax.named_scope("update"):
        delta_w, new_m, new_v = {}, {}, {}
        for n in TWIN_WEIGHTS:
            delta_w[n], new_m[n], new_v[n] = _adamw(weights[n], grad_w[n], given["m_" + n], given["v_" + n])
    return (loss, grad_x, *[grad_w[n] for n in TWIN_WEIGHTS], *[delta_w[n] for n in TWIN_WEIGHTS],
            *[new_m[n] for n in TWIN_WEIGHTS], *[new_v[n] for n in TWIN_WEIGHTS])
```

```python
import functools

import jax
import jax.numpy as jnp
from jax import lax
from jax.experimental import pallas as pl
from jax.experimental.pallas import tpu as pltpu

F32 = jnp.float32
BF16 = jnp.bfloat16

D = 1024
NH = 8
HD = 128
NCHIP = 4
T = 256
NORM_EPS = 1e-6
LN_EPS = 1e-5
LRU_C = 8.0
ADAM_LR = 0.001
ADAM_B1 = 0.9
ADAM_B2 = 0.999
ADAM_EPS = 1e-08
ADAM_WD = 0.01
ADAM_STEP = 10

VMEM = pl.BlockSpec(memory_space=pltpu.VMEM)
ANY = pl.BlockSpec(memory_space=pl.ANY)
MESH = pl.DeviceIdType.MESH


def _cp(n_grid=0, vmem_mb=None):
    kw = {}
    if n_grid:
        kw["dimension_semantics"] = ("arbitrary",) * n_grid
    if vmem_mb:
        kw["vmem_limit_bytes"] = vmem_mb << 20
    return pltpu.CompilerParams(**kw)


def _sigmoid(x):
    return 1.0 / (1.0 + jnp.exp(-x))


def _silu_and_grad(x):
    s = _sigmoid(x)
    return x * s, s * (1.0 + x * (1.0 - s))


_GELU_K = 0.7978845608028654
_GELU_C = 0.044715


def _gelu_and_grad(x):
    x2 = x * x
    th = jnp.tanh(_GELU_K * (x + _GELU_C * x * x2))
    g = 0.5 * x * (1.0 + th)
    dg = 0.5 * (1.0 + th) + 0.5 * x * (1.0 - th * th) * (_GELU_K * (1.0 + 3.0 * _GELU_C * x2))
    return g, dg


def _softplus(x):
    return jnp.maximum(x, 0.0) + jnp.log1p(jnp.exp(-jnp.abs(x)))


def _neg_expm1(x):
    series = -x * (1.0 + x * (0.5 + x * (1.0 / 6.0 + x * (1.0 / 24.0))))
    return jnp.where(x > -0.03, series, 1.0 - jnp.exp(x))


def _lru_gate(pre, lam_row, d):
    r = _sigmoid(pre[:, 256 * d:256 * d + HD])
    gi = _sigmoid(pre[:, 256 * d + HD:256 * d + 2 * HD])
    lam = lam_row[:, HD * d:HD * d + HD]
    sp = _softplus(-lam)
    la = (-LRU_C) * r * sp
    a = jnp.exp(la)
    mult = jnp.sqrt(_neg_expm1(2.0 * la))
    return r, gi, lam, sp, a, mult


def _dot(a, b):
    return jnp.dot(a, b, preferred_element_type=F32)


def _dot_tn(a, b):
    return lax.dot_general(a, b, (((0,), (0,)), ((), ())), preferred_element_type=F32)


def _dot_nt(a, b):
    return lax.dot_general(a, b, (((1,), (1,)), ((), ())), preferred_element_type=F32)


def _mo(v, m):
    return v if isinstance(v, int) else pl.multiple_of(v, m)


def _zrows(h, n):
    return pl.ds(h, n, stride=NH)


def _gather_weights(w_in, w_out, ada_w, smalls):
    specs = [
        ((16, 256), (64, 256), F32,
         lambda r, jj, cc: r.at[pl.ds(_mo(16 * jj + 8 * cc, 8), 8), :]),
        ((D, 768), (D, 3072), BF16,
         lambda r, jj, cc: r.at[pl.ds(_mo(512 * cc, 16), 512), pl.ds(_mo(768 * jj, 128), 768)]),
        ((D, 1280), (D, 5120), BF16,
         lambda r, jj, cc: r.at[pl.ds(_mo(512 * cc, 16), 512), pl.ds(_mo(1280 * jj, 128), 1280)]),
        ((512, D), (2048, D), BF16,
         lambda r, jj, cc: r.at[pl.ds(_mo(512 * jj + 256 * cc, 16), 256), :]),
    ]
    halves = [
        lambda r, cc: r.at[pl.ds(_mo(8 * cc, 8), 8), :],
        lambda r, cc: r.at[pl.ds(_mo(512 * cc, 16), 512), :],
        lambda r, cc: r.at[pl.ds(_mo(512 * cc, 16), 512), :],
        lambda r, cc: r.at[pl.ds(_mo(256 * cc, 16), 256), :],
    ]
    na = len(specs)

    def body(sm_ref, ada_ref, win_ref, wout_ref, sm_o, ada_o, win_o, wout_o,
             s_ada, s_win, s_wout, send_sems, recv_sems, local_sems):
        x, y, c = lax.axis_index("x"), lax.axis_index("y"), lax.axis_index("c")
        j = 2 * x + y
        sib = (x, y, 1 - c)
        chips = [(1 - x, y), (x, 1 - y), (1 - x, 1 - y)]
        cj = [2 * cx + cy for cx, cy in chips]
        outs = [sm_o, ada_o, win_o, wout_o]
        s_ada[...] = ada_ref[...].astype(BF16)
        s_win[...] = win_ref[...].astype(BF16)
        s_wout[...] = wout_ref[...].astype(BF16)
        srcs = [sm_ref, s_ada, s_win, s_wout]

        def copy(a, k, src, dst, to):
            return pltpu.make_async_remote_copy(src_ref=src, dst_ref=dst, send_sem=send_sems.at[6 * a + k],
                                                recv_sem=recv_sems.at[6 * a + k], device_id=to, device_id_type=MESH)

        local = []
        first = []
        for a in range(na):
            for cc in range(2):
                lc = pltpu.make_async_copy(halves[a](srcs[a], cc), specs[a][3](outs[a], j, cc), local_sems.at[2 * a + cc])
                lc.start()
                local.append(lc)
        for k in range(3):
            for a in range(na):
                cp = copy(a, k, halves[a](srcs[a], c), specs[a][3](outs[a], j, c), (*chips[k], c))
                cp.start()
                first.append(cp)
        passed = []
        for k in range(3):
            for a in range(na):
                reg = specs[a][3](outs[a], cj[k], c)
                copy(a, k, reg, reg, sib).wait_recv()
                cp = copy(a, 3 + k, reg, reg, sib)
                cp.start()
                passed.append(cp)
        for k in range(3):
            for a in range(na):
                reg = specs[a][3](outs[a], cj[k], 1 - c)
                copy(a, 3 + k, reg, reg, sib).wait_recv()
        for cp in first + passed:
            cp.wait_send()
        for lc in local:
            lc.wait()

    out_shape = tuple(jax.ShapeDtypeStruct(s[1], s[2]) for s in specs)
    return pl.pallas_call(
        body, name="gather_weights", out_shape=out_shape,
        in_specs=[VMEM] * 4, out_specs=(VMEM,) * 4,
        scratch_shapes=[pltpu.VMEM((D, 768), BF16), pltpu.VMEM((D, 1280), BF16), pltpu.VMEM((512, D), BF16),
                        pltpu.SemaphoreType.DMA((6 * na,)), pltpu.SemaphoreType.DMA((6 * na,)),
                        pltpu.SemaphoreType.DMA((2 * na,))],
        compiler_params=_cp(vmem_mb=56),
    )(smalls, ada_w, w_in, w_out)


def _hier_reduce(g, mode, replicate, name):
    if mode == "A":
        rr, c4 = g.shape
        hr, cc_ = rr // 2, c4 // 4
        half_shape, piece_shape = (hr, c4), (hr, cc_)
        out_shape = (rr, c4) if replicate else (rr, cc_)
        half_of = lambda r, c: r.at[pl.ds(_mo(c * hr, 16), hr), :]
        piece_of = lambda r, jj: r.at[:, pl.ds(_mo(jj * cc_, 128), cc_)]
        if replicate:
            region = lambda r, jj, c: r.at[pl.ds(_mo(c * hr, 16), hr), pl.ds(_mo(jj * cc_, 128), cc_)]
        else:
            region = lambda r, jj, c: r.at[pl.ds(_mo(c * hr, 16), hr), :]
    else:
        r4, cn = g.shape
        rb, hc = r4 // 4, cn // 2
        half_shape, piece_shape = (r4, hc), (rb, hc)
        out_shape = (rb, cn)
        half_of = lambda r, c: r.at[:, pl.ds(_mo(c * hc, 128), hc)]
        piece_of = lambda r, jj: r.at[pl.ds(_mo(jj * rb, 16), rb), :]
        region = lambda r, jj, c: r.at[:, pl.ds(_mo(c * hc, 128), hc)]
    chunk = 32
    n_half_chunks = half_shape[0] // chunk
    n_piece_chunks = piece_shape[0] // chunk
    assert half_shape[0] % chunk == 0 and piece_shape[0] % chunk == 0

    def body(g_hbm, out_hbm, mine, recv_a, sendbuf, recv_b, outbuf, send_sems, recv_sems, local_sems):
        x, y, c = lax.axis_index("x"), lax.axis_index("y"), lax.axis_index("c")
        j = 2 * x + y
        sib = (x, y, 1 - c)
        chips = [(1 - x, y), (x, 1 - y), (1 - x, 1 - y)]
        cj = [2 * cx + cy for cx, cy in chips]

        def copy(k, src, dst, to):
            return pltpu.make_async_remote_copy(src_ref=src, dst_ref=dst, send_sem=send_sems.at[k],
                                                recv_sem=recv_sems.at[k], device_id=to, device_id_type=MESH)

        sends = []
        own_half = pltpu.make_async_copy(half_of(g_hbm, c), mine, local_sems.at[0])
        own_half.start()
        p1 = copy(0, half_of(g_hbm, 1 - c), recv_a, sib)
        p1.start()
        sends.append(p1)
        own_half.wait()
        p1.wait_recv()

        def pair_sum(i, carry):
            rows = pl.ds(_mo(i * chunk, chunk), chunk)
            s = mine[rows, :] + recv_a[rows, :]
            mine[rows, :] = s
            sendbuf[rows, :] = s.astype(BF16)
            return carry
        lax.fori_loop(0, n_half_chunks, pair_sum, 0)

        for k in range(3):
            cp = copy(1 + k, piece_of(sendbuf, cj[k]), recv_b.at[k], (*chips[k], c))
            cp.start()
            sends.append(cp)
        own_piece = pltpu.make_async_copy(piece_of(mine, j), outbuf, local_sems.at[1])
        own_piece.start()
        own_piece.wait()
        for k in range(3):
            copy(1 + k, piece_of(sendbuf, j), recv_b.at[k], sib).wait_recv()

        def chip_sum(i, carry):
            rows = pl.ds(_mo(i * chunk, chunk), chunk)
            outbuf[rows, :] = (outbuf[rows, :] + recv_b[0, rows, :].astype(F32)
                               + recv_b[1, rows, :].astype(F32) + recv_b[2, rows, :].astype(F32))
            return carry
        lax.fori_loop(0, n_piece_chunks, chip_sum, 0)

        to_out = pltpu.make_async_copy(outbuf, region(out_hbm, j, c), local_sems.at[2])
        to_out.start()
        p3 = copy(4, outbuf, region(out_hbm, j, c), sib)
        p3.start()
        sends.append(p3)
        if replicate:
            for k in range(3):
                cp = copy(5 + k, outbuf, region(out_hbm, j, c), (*chips[k], c))
                cp.start()
                sends.append(cp)
            for k in range(3):
                reg = region(out_hbm, cj[k], c)
                copy(5 + k, reg, reg, sib).wait_recv()
                cp = copy(8 + k, reg, reg, sib)
                cp.start()
                sends.append(cp)
        reg = region(out_hbm, j, 1 - c)
        copy(4, reg, reg, sib).wait_recv()
        if replicate:
            for k in range(3):
                reg = region(out_hbm, cj[k], 1 - c)
                copy(8 + k, reg, reg, sib).wait_recv()
        for cp in sends:
            cp.wait_send()
        to_out.wait()

    return pl.pallas_call(
        body, name=name, out_shape=jax.ShapeDtypeStruct(out_shape, F32),
        in_specs=[ANY], out_specs=ANY,
        scratch_shapes=[pltpu.VMEM(half_shape, F32), pltpu.VMEM(half_shape, F32), pltpu.VMEM(half_shape, BF16),
                        pltpu.VMEM((3,) + piece_shape, BF16), pltpu.VMEM(piece_shape, F32),
                        pltpu.SemaphoreType.DMA((11,)), pltpu.SemaphoreType.DMA((11,)), pltpu.SemaphoreType.DMA((3,))],
        compiler_params=_cp(vmem_mb=56),
    )(g)


def _ada_fwd(c, c_ctx, ada_w_full, ada_b):
    def body(c_ref, cc_ref, w_ref, b_ref, o_ref, lhs):
        lhs[...] = jnp.zeros_like(lhs)
        cv = c_ref[...]
        lhs[0:1, :] = cv * _sigmoid(cv)
        cc = cc_ref[...]
        lhs[1:2, :] = cc * _sigmoid(cc)
        o_ref[...] = _dot(lhs[...].astype(BF16), w_ref[...]) + b_ref[...]

    return pl.pallas_call(
        body, name="ada_fwd", out_shape=jax.ShapeDtypeStruct((8, 3 * D), F32),
        in_specs=[VMEM] * 4, out_specs=VMEM, scratch_shapes=[pltpu.VMEM((8, D), F32)],
        compiler_params=_cp(vmem_mb=32),
    )(c, c_ctx, ada_w_full, ada_b)


def _ada_bwd(c_all, dmx_all_j, dmc_j, dmx_all, dmc, c_ctx, ada_w_full):
    def body(c_ref, dmxj_ref, dmcj_ref, dmx_ref, dmc_ref, cc_ref, w_ref, gw_ref, gb_ref, gc_ref, lhs, rhs, dm8):
        lhs[...] = jnp.zeros_like(lhs)
        rhs[...] = jnp.zeros_like(rhs)
        cv = c_ref[...]
        lhs[0:8, :] = cv * _sigmoid(cv)
        cc = cc_ref[...]
        a_c, da_c = _silu_and_grad(cc)
        lhs[8:9, :] = a_c
        rhs[0:8, :] = dmxj_ref[...]
        rhs[8:9, :] = dmcj_ref[...]
        gw_ref[...] = _dot_tn(lhs[...].astype(BF16), rhs[...].astype(BF16))
        gb_ref[...] = jnp.sum(dmx_ref[...], axis=0, keepdims=True) + dmc_ref[...]
        dm8[...] = jnp.zeros_like(dm8)
        dm8[0:1, :] = dmc_ref[...]
        da = _dot_nt(dm8[...].astype(BF16), w_ref[...])
        gc_ref[...] = da[0:1, :] * da_c

    return pl.pallas_call(
        body, name="ada_bwd",
        out_shape=(jax.ShapeDtypeStruct((D, 768), F32), jax.ShapeDtypeStruct((1, 3 * D), F32),
                   jax.ShapeDtypeStruct((1, D), F32)),
        in_specs=[VMEM] * 7, out_specs=(VMEM,) * 3,
        scratch_shapes=[pltpu.VMEM((16, D), F32), pltpu.VMEM((16, 768), F32), pltpu.VMEM((8, 3 * D), F32)],
        compiler_params=_cp(vmem_mb=32),
    )(c_all, dmx_all_j, dmc_j, dmx_all, dmc, c_ctx, ada_w_full)


def _proj(x, mods, mrow, norm_g, w_full, nk, name):
    lx = x.shape[0]
    n = lx // T

    def body(x_ref, sh_ref, sc_ref, ng_ref, *rest):
        w_refs, hn_ref, z_refs = rest[:nk], rest[nk], rest[nk + 1:]
        xv = x_ref[...]
        r = lax.rsqrt(jnp.mean(xv * xv, axis=-1, keepdims=True) + NORM_EPS)
        hn = (xv * r) * ng_ref[...] * (1.0 + sc_ref[mrow:mrow + 1, :]) + sh_ref[mrow:mrow + 1, :]
        hb = hn.astype(BF16)
        hn_ref[...] = hb
        for k in range(nk):
            z_refs[k][...] = _dot(hb, w_refs[k][...])

    row = pl.BlockSpec((T, D), lambda i: (i, 0))
    in_specs = [row, pl.BlockSpec((8, D), lambda i: (0, 0)), pl.BlockSpec((8, D), lambda i: (0, 1)),
                pl.BlockSpec((1, D), lambda i: (0, 0))]
    in_specs += [pl.BlockSpec((D, D), lambda i, k=k: (0, k)) for k in range(nk)]
    out_shape = (jax.ShapeDtypeStruct((lx, D), BF16),) + tuple(jax.ShapeDtypeStruct((lx, D), F32) for _ in range(nk))
    return pl.pallas_call(
        body, name=name, grid=(n,), out_shape=out_shape, in_specs=in_specs, out_specs=(row,) * (nk + 1),
        compiler_params=_cp(1, vmem_mb=56),
    )(x, mods, mods, norm_g, *([w_full] * nk))


def _halo_specs(lx):
    last = lx // 8 - 1
    return [pl.BlockSpec((T, D), lambda i: (i, 0)),
            pl.BlockSpec((8, D), lambda i: (jnp.maximum(i * (T // 8) - 1, 0), 0)),
            pl.BlockSpec((8, D), lambda i: (jnp.minimum((i + 1) * (T // 8), last), 0))]


def _zhalo_specs(lx):
    last = lx // 8 - 1
    return [pl.BlockSpec((T * NH, HD), lambda i: (i, 0)),
            pl.BlockSpec((8 * NH, HD), lambda i: (jnp.maximum(i * (T // 8) - 1, 0), 0)),
            pl.BlockSpec((8 * NH, HD), lambda i: (jnp.minimum((i + 1) * (T // 8), last), 0))]


ZT = pl.BlockSpec((T * NH, HD), lambda i: (i, 0))
CONV_CHUNK = 32


def _lru_gates_fwd(xa, conv_wz, conv_bz, wcat, bcat, lamcat, name):
    lx = xa.shape[0]
    n = lx // T

    def body(xm, xp, xn, cw, cb, w_ref, b_ref, lam_ref, xaz_o, xcz_o, af_o, uf_o, ab_o, ub_o, pad):
        i = pl.program_id(0)
        pmask = jnp.where(i == 0, 0.0, 1.0)
        nmask = jnp.where(i == n - 1, 0.0, 1.0)
        for h in range(NH):
            cols = slice(HD * h, HD * h + HD)
            pad[_zrows(h, 8), :] = xp[:, cols] * pmask
            pad[pl.ds(8 * NH + h, T, stride=NH), :] = xm[:, cols]
            pad[pl.ds((T + 8) * NH + h, 8, stride=NH), :] = xn[:, cols] * nmask
        xaz_o[...] = pad[pl.ds(8 * NH, T * NH), :]

        def conv_chunk(ci, carry):
            base = pl.multiple_of(ci * (CONV_CHUNK * NH), CONV_CHUNK * NH)
            acc = None
            for k in range(4):
                sl = pad[pl.ds(base + (7 + k) * NH, CONV_CHUNK * NH), :].reshape(CONV_CHUNK, NH, HD)
                term = sl * cw[k][None]
                acc = term if acc is None else acc + term
            acc = acc + cb[...][None]
            xcz_o[pl.ds(base, CONV_CHUNK * NH), :] = acc.reshape(CONV_CHUNK * NH, HD)
            return carry
        lax.fori_loop(0, T // CONV_CHUNK, conv_chunk, 0)

        outs = ((af_o, uf_o), (ab_o, ub_o))
        for h in range(NH):
            xch = xcz_o[_zrows(h, T), :]
            pre = _dot(xch.astype(BF16), w_ref[h]) + b_ref[h:h + 1, :]
            for d in range(2):
                _, gi, _, _, a, mult = _lru_gate(pre, lam_ref[h:h + 1, :], d)
                outs[d][0][_zrows(h, T), :] = a
                outs[d][1][_zrows(h, T), :] = mult * gi * xch

    full = lambda shape: pl.BlockSpec(shape, lambda i: (0,) * len(shape))
    in_specs = _halo_specs(lx) + [full((4, NH, HD)), full((NH, HD)), full((NH, HD, 4 * HD)), full((NH, 4 * HD)),
                                  full((NH, 2 * HD))]
    zs = jax.ShapeDtypeStruct((lx * NH, HD), F32)
    return pl.pallas_call(
        body, name=name, grid=(n,), out_shape=(zs,) * 6, in_specs=in_specs, out_specs=(ZT,) * 6,
        scratch_shapes=[pltpu.VMEM(((T + 16) * NH, HD), F32)],
        compiler_params=_cp(1, vmem_mb=48),
    )(xa, xa, xa, conv_wz, conv_bz, wcat, bcat, lamcat)


def _scan(a_up, x_up, a_dn, x_dn, s_up, s_dn, post, name):
    lx = a_up.shape[0] // NH
    n = lx // T

    def body(au, xu, ad, xd, su0, sd0, *rest):
        if post:
            ou, od, fu, fd, carry = rest
        else:
            ou, od, pu, pd, fu, fd, carry = rest
        i = pl.program_id(0)

        @pl.when(i == 0)
        def _():
            carry[0] = su0[...]
            carry[1] = sd0[...]

        def step(k, s):
            su, sd = s
            ru = pl.ds(_mo(k * NH, NH), NH)
            rd = pl.ds(_mo((T - 1 - k) * NH, NH), NH)
            if post:
                vu = xu[ru, :] + su
                vd = xd[rd, :] + sd
                ou[ru, :] = vu
                od[rd, :] = vd
                return au[ru, :] * vu, ad[rd, :] * vd
            pu[ru, :] = su
            pd[rd, :] = sd
            vu = au[ru, :] * su + xu[ru, :]
            vd = ad[rd, :] * sd + xd[rd, :]
            ou[ru, :] = vu
            od[rd, :] = vd
            return vu, vd

        su, sd = lax.fori_loop(0, T, step, (carry[0], carry[1]), unroll=8)
        carry[0] = su
        carry[1] = sd
        fu[...] = su
        fd[...] = sd

    up = pl.BlockSpec((T * NH, HD), lambda i: (i, 0))
    dn = pl.BlockSpec((T * NH, HD), lambda i: (n - 1 - i, 0))
    st = pl.BlockSpec((NH, HD), lambda i: (0, 0))
    zs = jax.ShapeDtypeStruct((lx * NH, HD), F32)
    ss = jax.ShapeDtypeStruct((NH, HD), F32)
    if post:
        out_shape, out_specs = (zs, zs, ss, ss), (up, dn, st, st)
    else:
        out_shape, out_specs = (zs, zs, zs, zs, ss, ss), (up, dn, up, dn, st, st)
    return pl.pallas_call(
        body, name=name, grid=(n,), out_shape=out_shape, in_specs=[up, up, dn, dn, st, st], out_specs=out_specs,
        scratch_shapes=[pltpu.VMEM((2, NH, HD), F32)],
        compiler_params=_cp(1, vmem_mb=48),
    )(a_up, x_up, a_dn, x_dn, s_up, s_dn)


def _sgu_parts(u, v, lng, lnb, w_ref, bt_ref, mixed_s):
    ug, dug = _gelu_and_grad(u)
    vg, dvg = _gelu_and_grad(v)
    mu = jnp.mean(vg, axis=-1, keepdims=True)
    vc = vg - mu
    rstd = lax.rsqrt(jnp.mean(vc * vc, axis=-1, keepdims=True) + LN_EPS)
    vh = vc * rstd
    vn = (vh * lng + lnb).astype(BF16)
    for g in range(NH):
        cols = slice(HD * g, HD * g + HD)
        mixed_s[:, cols] = _dot(w_ref[g], vn[:, cols]) + bt_ref[:, g:g + 1]
    return ug, dug, dvg, rstd, vh, vn


def _sgu_fwd(u, v, ln_g, ln_b, sgu_w, sgu_bt):
    lx = u.shape[0]
    n = lx // HD

    def body(u_ref, v_ref, g_ref, b_ref, w_ref, bt_ref, y_ref, mixed_s):
        ug, _, _, _, _, _ = _sgu_parts(u_ref[...], v_ref[...], g_ref[...], b_ref[...], w_ref, bt_ref, mixed_s)
        y_ref[...] = ug * mixed_s[...]

    row = pl.BlockSpec((HD, D), lambda i: (i, 0))
    vec = pl.BlockSpec((1, D), lambda i: (0, 0))
    return pl.pallas_call(
        body, name="sgu_fwd", grid=(n,), out_shape=jax.ShapeDtypeStruct((lx, D), F32),
        in_specs=[row, row, vec, vec, pl.BlockSpec((NH, HD, HD), lambda i: (0, 0, 0)),
                  pl.BlockSpec((HD, NH), lambda i: (0, 0))],
        out_specs=row, scratch_shapes=[pltpu.VMEM((HD, D), F32)],
        compiler_params=_cp(1),
    )(u, v, ln_g, ln_b, sgu_w, sgu_bt)


def _sgu_bwd(u, v, dys, ln_g, ln_b, sgu_w, sgu_bt):
    lx = u.shape[0]
    n = lx // HD

    def body(u_ref, v_ref, dy_ref, g_ref, b_ref, w_ref, bt_ref, du_ref, dv_ref, dw_ref, db_ref, dg_ref, dbl_ref,
             mixed_s, dvn_s):
        i = pl.program_id(0)

        @pl.when(i == 0)
        def _():
            dw_ref[...] = jnp.zeros_like(dw_ref)
            db_ref[...] = jnp.zeros_like(db_ref)
            dg_ref[...] = jnp.zeros_like(dg_ref)
            dbl_ref[...] = jnp.zeros_like(dbl_ref)

        lng = g_ref[...]
        ug, dug, dvg, rstd, vh, vn = _sgu_parts(u_ref[...], v_ref[...], lng, b_ref[...], w_ref, bt_ref, mixed_s)
        dys_v = dy_ref[...]
        du_ref[...] = (dys_v * mixed_s[...] * dug).astype(BF16)
        dmix = dys_v * ug
        ones = jnp.ones((8, HD), BF16)
        for g in range(NH):
            cols = slice(HD * g, HD * g + HD)
            dm = dmix[:, cols]
            hi = dm.astype(BF16)
            lo = (dm - hi.astype(F32)).astype(BF16)
            dw_ref[g] += _dot_nt(hi, vn[:, cols])
            db_ref[g:g + 1, :] += (_dot_nt(ones, hi) + _dot_nt(ones, lo))[0:1, :]
            dvn_s[:, cols] = _dot_tn(w_ref[g], hi)
        dvn = dvn_s[...]
        dg_ref[...] += jnp.sum(dvn * vh, axis=0, keepdims=True)
        dbl_ref[...] += jnp.sum(dvn, axis=0, keepdims=True)
        dvh = dvn * lng
        dvg_in = rstd * (dvh - jnp.mean(dvh, axis=-1, keepdims=True)
                         - vh * jnp.mean(dvh * vh, axis=-1, keepdims=True))
        dv_ref[...] = (dvg_in * dvg).astype(BF16)

    row = pl.BlockSpec((HD, D), lambda i: (i, 0))
    vec = pl.BlockSpec((1, D), lambda i: (0, 0))
    wsp = pl.BlockSpec((NH, HD, HD), lambda i: (0, 0, 0))
    bsp = pl.BlockSpec((NH, HD), lambda i: (0, 0))
    return pl.pallas_call(
        body, name="sgu_bwd", grid=(n,),
        out_shape=(jax.ShapeDtypeStruct((lx, D), BF16), jax.ShapeDtypeStruct((lx, D), BF16),
                   jax.ShapeDtypeStruct((NH, HD, HD), F32), jax.ShapeDtypeStruct((NH, HD), F32),
                   jax.ShapeDtypeStruct((1, D), F32), jax.ShapeDtypeStruct((1, D), F32)),
        in_specs=[row, row, row, vec, vec, wsp, pl.BlockSpec((HD, NH), lambda i: (0, 0))],
        out_specs=(row, row, wsp, bsp, vec, vec),
        scratch_shapes=[pltpu.VMEM((HD, D), F32), pltpu.VMEM((HD, D), F32)],
        compiler_params=_cp(1),
    )(u, v, dys, ln_g, ln_b, sgu_w, sgu_bt)


def _out_fwd_bwd(hf_z, hb_z, ga, gb, ys, x, tgt, mods, final_g, w_out_full):
    lx = x.shape[0]
    n = lx // T

    def body(hf_ref, hb_ref, ga_ref, gb_ref, ys_ref, x_ref, t_ref, gx_ref, fg_ref, w_ref,
             loss_ref, dfg_ref, dgx_ref, dxn_ref, y_ref, do_ref, dga_ref, dgb_ref, dyl_ref, dys_ref, yl_s):
        i = pl.program_id(0)

        @pl.when(i == 0)
        def _():
            loss_ref[...] = jnp.zeros_like(loss_ref)
            dfg_ref[...] = jnp.zeros_like(dfg_ref)
            dgx_ref[...] = jnp.zeros_like(dgx_ref)

        for h in range(NH):
            yl_s[:, HD * h:HD * h + HD] = hf_ref[_zrows(h, T), :] + hb_ref[_zrows(h, T), :]
        yl = yl_s[...]
        gav = ga_ref[...]
        gbv = gb_ref[...]
        sa, dsa = _silu_and_grad(gav)
        sb, dsb = _silu_and_grad(gbv)
        ysv = ys_ref[...]
        y_ref[:, 0:D] = (yl * sa).astype(BF16)
        y_ref[:, D:2 * D] = (ysv * sb).astype(BF16)
        o = _dot(y_ref[...], w_ref[...])
        gx = gx_ref[0:1, :]
        xnew = x_ref[...] + gx * o
        r2 = lax.rsqrt(jnp.mean(xnew * xnew, axis=-1, keepdims=True) + NORM_EPS)
        xh = xnew * r2
        fg = fg_ref[...]
        err = xh * fg - t_ref[...]
        loss_ref[...] += 0.5 * jnp.sum(jnp.mean(err * err, axis=-1, keepdims=True), axis=0, keepdims=True)
        dout = err * (1.0 / D)
        dfg_ref[...] += jnp.sum(dout * xh, axis=0, keepdims=True)
        dxh = dout * fg
        dxn = r2 * (dxh - xh * jnp.mean(dxh * xh, axis=-1, keepdims=True))
        dxn_ref[...] = dxn
        dgx_ref[...] += jnp.sum(dxn * o, axis=0, keepdims=True)
        do = (dxn * gx).astype(BF16)
        do_ref[...] = do
        dy = _dot_nt(do, w_ref[...])
        dy1 = dy[:, 0:D]
        dy2 = dy[:, D:2 * D]
        dga_ref[...] = (dy1 * yl * dsa).astype(BF16)
        dgb_ref[...] = (dy2 * ysv * dsb).astype(BF16)
        dys_ref[...] = dy2 * sb
        yl_s[...] = dy1 * sa
        for h in range(NH):
            dyl_ref[_zrows(h, T), :] = yl_s[:, HD * h:HD * h + HD]

    row = pl.BlockSpec((T, D), lambda i: (i, 0))
    vec = pl.BlockSpec((1, D), lambda i: (0, 0))
    in_specs = [ZT, ZT, row, row, row, row, row, pl.BlockSpec((8, D), lambda i: (0, 2)), vec,
                pl.BlockSpec((2 * D, D), lambda i: (0, 0))]
    out_shape = (jax.ShapeDtypeStruct((1, 1), F32), jax.ShapeDtypeStruct((1, D), F32), jax.ShapeDtypeStruct((1, D), F32),
                 jax.ShapeDtypeStruct((lx, D), F32), jax.ShapeDtypeStruct((lx, 2 * D), BF16),
                 jax.ShapeDtypeStruct((lx, D), BF16), jax.ShapeDtypeStruct((lx, D), BF16),
                 jax.ShapeDtypeStruct((lx, D), BF16), jax.ShapeDtypeStruct((lx * NH, HD), F32),
                 jax.ShapeDtypeStruct((lx, D), F32))
    out_specs = (pl.BlockSpec((1, 1), lambda i: (0, 0)), vec, vec, row, pl.BlockSpec((T, 2 * D), lambda i: (i, 0)),
                 row, row, row, ZT, row)
    return pl.pallas_call(
        body, name="out_fwd_bwd", grid=(n,), out_shape=out_shape, in_specs=in_specs, out_specs=out_specs,
        scratch_shapes=[pltpu.VMEM((T, D), F32)],
        compiler_params=_cp(1, vmem_mb=56),
    )(hf_z, hb_z, ga, gb, ys, x, tgt, mods, final_g, w_out_full)


def _lru_gates_bwd(xc_z, lf_z, lb_z, pf_z, pb_z, wcat, bcat, lamcat, dw0, db0, dl0, name):
    lx = xc_z.shape[0] // NH
    n = lx // T

    def body(xc_ref, lf_ref, lb_ref, pf_ref, pb_ref, w_ref, b_ref, lam_ref, dw0_ref, db0_ref, dl0_ref,
             dxc_ref, dw_ref, db_ref, dl_ref, dpre_s):
        i = pl.program_id(0)

        @pl.when(i == 0)
        def _():
            dw_ref[...] = dw0_ref[...]
            db_ref[...] = db0_ref[...]
            dl_ref[...] = dl0_ref[...]

        lam_refs = (lf_ref, lb_ref)
        prev_refs = (pf_ref, pb_ref)
        for h in range(NH):
            xch = xc_ref[_zrows(h, T), :]
            xcb = xch.astype(BF16)
            pre = _dot(xcb, w_ref[h]) + b_ref[h:h + 1, :]
            dxc = jnp.zeros((T, HD), F32)
            for d in range(2):
                r, gi, lam, sp, a, mult = _lru_gate(pre, lam_ref[h:h + 1, :], d)
                du = lam_refs[d][_zrows(h, T), :]
                da = du * prev_refs[d][_zrows(h, T), :]
                dgi = du * mult * xch
                dxc = dxc + du * mult * gi
                dmult = du * gi * xch
                dla = da * a - dmult * (a * a) / mult
                dr = dla * ((-LRU_C) * sp)
                dsp = jnp.sum(dla * ((-LRU_C) * r), axis=0, keepdims=True)
                dl_ref[h:h + 1, HD * d:HD * d + HD] += dsp * (-_sigmoid(-lam))
                dpre_s[:, 256 * d:256 * d + HD] = dr * r * (1.0 - r)
                dpre_s[:, 256 * d + HD:256 * d + 2 * HD] = dgi * gi * (1.0 - gi)
            dpre = dpre_s[...]
            dpb = dpre.astype(BF16)
            dw_ref[h] += _dot_tn(xcb, dpb)
            db_ref[h:h + 1, :] += jnp.sum(dpre, axis=0, keepdims=True)
            dxc_ref[_zrows(h, T), :] = dxc + _dot_nt(dpb, w_ref[h])

    full = lambda shape: pl.BlockSpec(shape, lambda i: (0,) * len(shape))
    wsp, bsp, lsp = full((NH, HD, 4 * HD)), full((NH, 4 * HD)), full((NH, 2 * HD))
    return pl.pallas_call(
        body, name=name, grid=(n,),
        out_shape=(jax.ShapeDtypeStruct((lx * NH, HD), F32), jax.ShapeDtypeStruct((NH, HD, 4 * HD), F32),
                   jax.ShapeDtypeStruct((NH, 4 * HD), F32), jax.ShapeDtypeStruct((NH, 2 * HD), F32)),
        in_specs=[ZT] * 5 + [wsp, bsp, lsp, wsp, bsp, lsp], out_specs=(ZT, wsp, bsp, lsp),
        scratch_shapes=[pltpu.VMEM((T, 4 * HD), F32)],
        compiler_params=_cp(1, vmem_mb=48),
    )(xc_z, lf_z, lb_z, pf_z, pb_z, wcat, bcat, lamcat, dw0, db0, dl0)


def _conv_bwd(dxc_z, xa_z, conv_wz, dcw0, dcb0, name):
    lx = dxc_z.shape[0] // NH
    n = lx // T

    def body(dm, dp, dn, xa_ref, cw, dcw0_ref, dcb0_ref, dxa_ref, dcw_ref, dcb_ref, pad, dxa_s):
        i = pl.program_id(0)

        @pl.when(i == 0)
        def _():
            dcw_ref[...] = dcw0_ref[...]
            dcb_ref[...] = dcb0_ref[...]

        pmask = jnp.where(i == 0, 0.0, 1.0)
        nmask = jnp.where(i == n - 1, 0.0, 1.0)
        pad[pl.ds(0, 8 * NH), :] = dp[...] * pmask
        pad[pl.ds(8 * NH, T * NH), :] = dm[...]
        pad[pl.ds((T + 8) * NH, 8 * NH), :] = dn[...] * nmask

        def chunk(ci, carry):
            base = pl.multiple_of(ci * (CONV_CHUNK * NH), CONV_CHUNK * NH)
            xav = xa_ref[pl.ds(base, CONV_CHUNK * NH), :].reshape(CONV_CHUNK, NH, HD)
            acc = None
            for k in range(4):
                sl = pad[pl.ds(base + (9 - k) * NH, CONV_CHUNK * NH), :].reshape(CONV_CHUNK, NH, HD)
                term = sl * cw[k][None]
                acc = term if acc is None else acc + term
                dcw_ref[k] += jnp.sum(sl * xav, axis=0)
                if k == 1:
                    dcb_ref[...] += jnp.sum(sl, axis=0)
            dxa_s[pl.ds(base, CONV_CHUNK * NH), :] = acc.reshape(CONV_CHUNK * NH, HD)
            return carry
        lax.fori_loop(0, T // CONV_CHUNK, chunk, 0)
        for h in range(NH):
            dxa_ref[:, HD * h:HD * h + HD] = dxa_s[_zrows(h, T), :].astype(BF16)

    full = lambda shape: pl.BlockSpec(shape, lambda i: (0,) * len(shape))
    return pl.pallas_call(
        body, name=name, grid=(n,),
        out_shape=(jax.ShapeDtypeStruct((lx, D), BF16), jax.ShapeDtypeStruct((4, NH, HD), F32),
                   jax.ShapeDtypeStruct((NH, HD), F32)),
        in_specs=_zhalo_specs(lx) + [ZT, full((4, NH, HD)), full((4, NH, HD)), full((NH, HD))],
        out_specs=(pl.BlockSpec((T, D), lambda i: (i, 0)), full((4, NH, HD)), full((NH, HD))),
        scratch_shapes=[pltpu.VMEM(((T + 16) * NH, HD), F32), pltpu.VMEM((T * NH, HD), F32)],
        compiler_params=_cp(1, vmem_mb=48),
    )(dxc_z, dxc_z, dxc_z, xa_z, conv_wz, dcw0, dcb0)


def _proj_bwd(dzs, x, dxn, mods, mrow, norm_g, w_full, dng0, name):
    lx = x.shape[0]
    n = lx // T
    nk = len(dzs)
    has_x = dxn is not None

    def body(*refs):
        dz_refs = refs[:nk]
        w_refs = refs[nk:2 * nk]
        x_ref, sc_ref, ng_ref, dng0_ref = refs[2 * nk:2 * nk + 4]
        rest = refs[2 * nk + 4:]
        if has_x:
            dxn_ref, gx_ref, dng_ref, dsc_ref, dsh_ref = rest
        else:
            dng_ref, dsc_ref, dsh_ref = rest
        i = pl.program_id(0)

        @pl.when(i == 0)
        def _():
            dng_ref[...] = dng0_ref[...]
            dsc_ref[...] = jnp.zeros_like(dsc_ref)
            dsh_ref[...] = jnp.zeros_like(dsh_ref)

        dhn = _dot_nt(dz_refs[0][...], w_refs[0][...])
        for k in range(1, nk):
            dhn = dhn + _dot_nt(dz_refs[k][...], w_refs[k][...])
        xv = x_ref[...]
        r = lax.rsqrt(jnp.mean(xv * xv, axis=-1, keepdims=True) + NORM_EPS)
        xn = xv * r
        ng = ng_ref[...]
        sc1 = 1.0 + sc_ref[mrow:mrow + 1, :]
        t = dhn * xn
        dng_ref[...] += jnp.sum(t * sc1, axis=0, keepdims=True)
        dsc_ref[...] += jnp.sum(t * ng, axis=0, keepdims=True)
        dsh_ref[...] += jnp.sum(dhn, axis=0, keepdims=True)
        if has_x:
            dxh = dhn * (ng * sc1)
            gx_ref[...] = dxn_ref[...] + r * (dxh - xn * jnp.mean(dxh * xn, axis=-1, keepdims=True))

    row = pl.BlockSpec((T, D), lambda i: (i, 0))
    vec = pl.BlockSpec((1, D), lambda i: (0, 0))
    in_specs = [row] * nk + [pl.BlockSpec((D, D), lambda i, k=k: (0, k)) for k in range(nk)]
    in_specs += [row, pl.BlockSpec((8, D), lambda i: (0, 1)), vec, vec]
    args = list(dzs) + [w_full] * nk + [x, mods, norm_g, dng0]
    vs = jax.ShapeDtypeStruct((1, D), F32)
    out_shape, out_specs = (vs, vs, vs), (vec, vec, vec)
    if has_x:
        in_specs.append(row)
        args.append(dxn)
        out_shape = (jax.ShapeDtypeStruct((lx, D), F32),) + out_shape
        out_specs = (row,) + out_specs
    return pl.pallas_call(
        body, name=name, grid=(n,), out_shape=out_shape, in_specs=in_specs, out_specs=out_specs,
        compiler_params=_cp(1, vmem_mb=56),
    )(*args)


def _tn_matmul(a, bs, extra, name):
    lx, m = a.shape
    n = lx // T
    widths = [b.shape[1] for b in bs]
    nb = len(bs)

    def body(a_ref, *rest):
        b_refs = rest[:nb]
        rest = rest[nb:]
        if extra is not None:
            ea_ref, eb_ref = rest[:2]
            rest = rest[2:]
        out_ref, acc = rest
        i = pl.program_id(0)
        av = a_ref[...]
        off = 0
        for k in range(nb):
            cols = slice(off, off + widths[k])
            part = _dot_tn(av, b_refs[k][...])

            @pl.when(i == 0)
            def _():
                acc[:, cols] = part

            @pl.when(i > 0)
            def _():
                acc[:, cols] += part
            off += widths[k]

        @pl.when(i == n - 1)
        def _():
            if extra is not None:
                acc[:, 0:widths[0]] += _dot_tn(ea_ref[...], eb_ref[...])
            pltpu.sync_copy(acc, out_ref)

    in_specs = [pl.BlockSpec((T, m), lambda i: (i, 0))] + [pl.BlockSpec((T, w), lambda i: (i, 0)) for w in widths]
    args = [a] + list(bs)
    if extra is not None:
        in_specs += [VMEM, VMEM]
        args += list(extra)
    return pl.pallas_call(
        body, name=name, grid=(n,), out_shape=jax.ShapeDtypeStruct((m, sum(widths)), F32),
        in_specs=in_specs, out_specs=ANY, scratch_shapes=[pltpu.VMEM((m, sum(widths)), F32)],
        compiler_params=_cp(1, vmem_mb=48),
    )(*args)


def _adam_math(w, g, m, v):
    m = ADAM_B1 * m + (1.0 - ADAM_B1) * g
    v = ADAM_B2 * v + (1.0 - ADAM_B2) * (g * g)
    m_hat = m / (1.0 - ADAM_B1 ** ADAM_STEP)
    v_hat = v / (1.0 - ADAM_B2 ** ADAM_STEP)
    delta = -ADAM_LR * (m_hat / (jnp.sqrt(v_hat) + ADAM_EPS) + ADAM_WD * w)
    return delta, m, v


def _adam_big(w, g, m, v, name):
    rows, cols = w.shape
    tr = 256

    def body(w_ref, g_ref, m_ref, v_ref, d_o, m_o, v_o):
        d, mm, vv = _adam_math(w_ref[...], g_ref[...], m_ref[...], v_ref[...])
        d_o[...] = d
        m_o[...] = mm
        v_o[...] = vv

    blk = pl.BlockSpec((tr, cols), lambda i: (i, 0))
    s = jax.ShapeDtypeStruct((rows, cols), F32)
    return pl.pallas_call(
        body, name=name, grid=(rows // tr,), out_shape=(s, s, s), in_specs=[blk] * 4, out_specs=(blk,) * 3,
        compiler_params=_cp(1, vmem_mb=48),
    )(w, g, m, v)


def _adam_small(items):
    ni = len(items)

    def body(*refs):
        ins, outs = refs[:4 * ni], refs[4 * ni:]
        for k in range(ni):
            w_ref, g_ref, m_ref, v_ref = ins[4 * k:4 * k + 4]
            d, mm, vv = _adam_math(w_ref[...], g_ref[...], m_ref[...], v_ref[...])
            outs[3 * k][...] = d
            outs[3 * k + 1][...] = mm
            outs[3 * k + 2][...] = vv

    flat = [a for it in items for a in it]
    out_shape = tuple(jax.ShapeDtypeStruct(it[0].shape, F32) for it in items for _ in range(3))
    res = pl.pallas_call(
        body, name="adam_small", out_shape=out_shape, in_specs=[VMEM] * (4 * ni), out_specs=(VMEM,) * (3 * ni),
        compiler_params=_cp(vmem_mb=32),
    )(*flat)
    return [tuple(res[3 * k:3 * k + 3]) for k in range(ni)]


def kernel(x, c, ctx, c_ctx, ada_w, ada_b, norm_g, w_in, conv_w, conv_b, lru_wa, lru_ba, lru_wx, lru_bx, lru_lambda, sgu_ln_g, sgu_ln_b, sgu_w, sgu_b, w_out, final_g, loss_target, m_c_ctx, m_ada_w, m_ada_b, m_norm_g, m_w_in, m_conv_w, m_conv_b, m_lru_wa, m_lru_ba, m_lru_wx, m_lru_bx, m_lru_lambda, m_sgu_ln_g, m_sgu_ln_b, m_sgu_w, m_sgu_b, m_w_out, m_final_g, v_c_ctx, v_ada_w, v_ada_b, v_norm_g, v_w_in, v_conv_w, v_conv_b, v_lru_wa, v_lru_ba, v_lru_wx, v_lru_bx, v_lru_lambda, v_sgu_ln_g, v_sgu_ln_b, v_sgu_w, v_sgu_b, v_w_out, v_final_g):
    ix, iy, ic = lax.axis_index("x"), lax.axis_index("y"), lax.axis_index("c")
    chip = 2 * ix + iy
    dev = 2 * chip + ic
    lx = x.shape[1]
    lc = ctx.shape[1]

    smalls = jnp.concatenate([conv_w[0], lru_lambda[0], jnp.zeros((10, 256), F32)], axis=0)
    sm_all, ada_full, w_in_full, w_out_full = _gather_weights(w_in[0], w_out[0], ada_w[0], smalls)
    sm3 = sm_all.reshape(NCHIP, 16, 256)
    conv_w_full = sm3[:, 0:4, :].transpose(1, 0, 2).reshape(4, D)
    lam_full = sm3[:, 4:6, :].transpose(1, 0, 2).reshape(2, D)
    conv_wz = conv_w_full.reshape(4, NH, HD)
    conv_bz = conv_b.reshape(NH, HD)
    lamcat = lam_full.reshape(2, NH, HD).transpose(1, 0, 2).reshape(NH, 2 * HD)
    wa, wx, ba, bx = lru_wa[0], lru_wx[0], lru_ba[0], lru_bx[0]
    wcat = jnp.concatenate([wa[0], wx[0], wa[1], wx[1]], axis=-1).astype(BF16)
    bcat = jnp.concatenate([ba[0], bx[0], ba[1], bx[1]], axis=-1)
    sgu_wb = sgu_w[0].astype(BF16)
    sgu_bt = sgu_b[0].T
    c_ctx2 = c_ctx.reshape(1, D)
    final_g2 = final_g.reshape(1, D)

    mods = _ada_fwd(c, c_ctx2, ada_full, ada_b)
    zero_s = jnp.zeros((NH, HD), F32)
    hn_c, xa_c = _proj(ctx[0], mods, 1, norm_g, w_in_full, 1, "proj_ctx")
    xaz_c, xcz_c, af_c, uf_c, ab_c, ub_c = _lru_gates_fwd(xa_c, conv_wz, conv_bz, wcat, bcat, lamcat, "lru_gates_ctx")
    _, _, pf_c, pb_c, hf0, hb0 = _scan(af_c, uf_c, ab_c, ub_c, zero_s, zero_s, False, "scan_ctx")

    hn, xa, ga, u, v, gb = _proj(x[0], mods, 0, norm_g, w_in_full, 5, "proj")
    xaz, xcz, af, uf, ab, ub = _lru_gates_fwd(xa, conv_wz, conv_bz, wcat, bcat, lamcat, "lru_gates")
    hf, hb, pf, pb, _, _ = _scan(af, uf, ab, ub, hf0, hb0, False, "scan")
    ys = _sgu_fwd(u, v, sgu_ln_g, sgu_ln_b, sgu_wb, sgu_bt)

    (loss_part, dfg, dgx, dxn, y, do, dga, dgb, dyl_z, dys) = _out_fwd_bwd(
        hf, hb, ga, gb, ys, x[0], loss_target[0], mods, final_g2, w_out_full)
    g_w_out_part = _tn_matmul(y, [do], None, "grad_w_out")

    du, dv, d_sgu_w, d_sgu_b, d_ln_g, d_ln_b = _sgu_bwd(u, v, dys, sgu_ln_g, sgu_ln_b, sgu_wb, sgu_bt)
    lb, lf, dh0b, dh0f = _scan(ab, dyl_z, af, dyl_z, zero_s, zero_s, True, "scan_adj")
    zw = jnp.zeros((NH, HD, 4 * HD), F32)
    zb = jnp.zeros((NH, 4 * HD), F32)
    zl = jnp.zeros((NH, 2 * HD), F32)
    dxc_z, dwc, dbc, dlc = _lru_gates_bwd(xcz, lf, lb, pf, pb, wcat, bcat, lamcat, zw, zb, zl, "lru_gates_bwd")
    dxa, dcw, dcb = _conv_bwd(dxc_z, xaz, conv_wz, jnp.zeros((4, NH, HD), F32), zero_s, "conv_bwd")

    zc = jnp.zeros((lc * NH, HD), F32)
    dhf_c = lax.dynamic_update_slice(zc, dh0f, ((lc - 1) * NH, 0))
    dhb_c = lax.dynamic_update_slice(zc, dh0b, (0, 0))
    lb_c, lf_c, _, _ = _scan(ab_c, dhb_c, af_c, dhf_c, zero_s, zero_s, True, "scan_adj_ctx")
    dxc_zc, dwc, dbc, dlc = _lru_gates_bwd(xcz_c, lf_c, lb_c, pf_c, pb_c, wcat, bcat, lamcat, dwc, dbc, dlc,
                                           "lru_gates_bwd_ctx")
    dxa_c, dcw, dcb = _conv_bwd(dxc_zc, xaz_c, conv_wz, dcw, dcb, "conv_bwd_ctx")

    dzs = [dxa, dga, du, dv, dgb]
    grad_x, dng, dsc_x, dsh_x = _proj_bwd(dzs, x[0], dxn, mods, 0, norm_g, w_in_full, jnp.zeros((1, D), F32), "proj_bwd")
    dng, dsc_c, dsh_c = _proj_bwd([dxa_c], ctx[0], None, mods, 1, norm_g, w_in_full, dng, "proj_bwd_ctx")
    g_w_in_part = _tn_matmul(hn, dzs, (hn_c, dxa_c), "grad_w_in")

    g_w_in = _hier_reduce(g_w_in_part, "A", False, "reduce_w_in")
    g_w_out = _hier_reduce(g_w_out_part, "B", False, "reduce_w_out")

    dmx = jnp.concatenate([dsh_x, dsc_x, dgx], axis=0)
    dmc = jnp.concatenate([dsh_c, dsc_c, jnp.zeros((1, D), F32)], axis=0)
    slot = jnp.concatenate([dmx, c], axis=0)
    slots = lax.dynamic_update_slice(jnp.zeros((32, D), F32), slot, (4 * dev, 0))
    vecs = jnp.concatenate([dfg, dng, dcb.reshape(1, D), d_ln_g, d_ln_b, dcw.reshape(4, D), dmc,
                            jnp.zeros((4, D), F32), slots], axis=0)
    d_sgu_w4 = d_sgu_w.reshape(4, 256, HD).transpose(1, 0, 2).reshape(256, 4 * HD)
    pad8 = lambda a: jnp.pad(a, ((0, 8 - a.shape[0]), (0, 4 * HD - a.shape[1])))
    pack = jnp.concatenate([dwc.reshape(NH * HD, 4 * HD), pad8(dbc), pad8(dlc), d_sgu_w4, pad8(d_sgu_b),
                            vecs.reshape(96, 4 * HD), jnp.zeros((8, 4 * HD), F32)], axis=0)
    tot = _hier_reduce(pack, "A", True, "reduce_small")

    g_wc = tot[0:1024].reshape(NH, HD, 4 * HD)
    g_bc = tot[1024:1032]
    g_lc = tot[1032:1040, 0:2 * HD]
    g_sgu_w = tot[1040:1296].reshape(256, 4, HD).transpose(1, 0, 2).reshape(NH, HD, HD)
    g_sgu_b = tot[1296:1304, 0:HD]
    tv = tot[1304:1400].reshape(48, D)
    g_final_g, g_norm_g, g_conv_b, g_ln_g, g_ln_b = tv[0:1], tv[1:2], tv[2:3], tv[3:4], tv[4:5]
    g_conv_w_full = tv[5:9]
    dmc_tot = tv[9:12].reshape(1, 3 * D)
    slots_all = tv[16:48].reshape(8, 4, D)
    dmx_all = slots_all[:, 0:3, :].reshape(8, 3 * D)
    c_all = slots_all[:, 3, :]
    g_lru_wa = jnp.stack([g_wc[:, :, 0:HD], g_wc[:, :, 2 * HD:3 * HD]])
    g_lru_wx = jnp.stack([g_wc[:, :, HD:2 * HD], g_wc[:, :, 3 * HD:4 * HD]])
    g_lru_ba = jnp.stack([g_bc[:, 0:HD], g_bc[:, 2 * HD:3 * HD]])
    g_lru_bx = jnp.stack([g_bc[:, HD:2 * HD], g_bc[:, 3 * HD:4 * HD]])
    g_lam_full = jnp.stack([g_lc[:, 0:HD], g_lc[:, HD:2 * HD]]).reshape(2, D)
    g_conv_w = lax.dynamic_slice(g_conv_w_full, (0, 256 * chip), (4, 256))
    g_lam = lax.dynamic_slice(g_lam_full, (0, 256 * chip), (2, 256))
    dmx_all_j = lax.dynamic_slice(dmx_all, (0, 768 * chip), (8, 768))
    dmc_j = lax.dynamic_slice(dmc_tot, (0, 768 * chip), (1, 768))
    g_ada_w, g_ada_b, g_c_ctx = _ada_bwd(c_all, dmx_all_j, dmc_j, dmx_all, dmc_tot, c_ctx2, ada_full)

    big = {
        "ada_w": _adam_big(ada_w[0], g_ada_w, m_ada_w[0], v_ada_w[0], "adam_ada_w"),
        "w_in": _adam_big(w_in[0], g_w_in, m_w_in[0], v_w_in[0], "adam_w_in"),
        "w_out": _adam_big(w_out[0], g_w_out, m_w_out[0], v_w_out[0], "adam_w_out"),
    }
    small_in = {
        "c_ctx": (c_ctx, g_c_ctx, m_c_ctx, v_c_ctx, (1, D)),
        "ada_b": (ada_b, g_ada_b, m_ada_b, v_ada_b, (1, 3 * D)),
        "norm_g": (norm_g, g_norm_g, m_norm_g, v_norm_g, (1, D)),
        "conv_w": (conv_w, g_conv_w, m_conv_w, v_conv_w, (4, 256)),
        "conv_b": (conv_b, g_conv_b, m_conv_b, v_conv_b, (1, D)),
        "lru_wa": (lru_wa, g_lru_wa, m_lru_wa, v_lru_wa, (2 * NH * HD, HD)),
        "lru_ba": (lru_ba, g_lru_ba, m_lru_ba, v_lru_ba, (2 * NH, HD)),
        "lru_wx": (lru_wx, g_lru_wx, m_lru_wx, v_lru_wx, (2 * NH * HD, HD)),
        "lru_bx": (lru_bx, g_lru_bx, m_lru_bx, v_lru_bx, (2 * NH, HD)),
        "lru_lambda": (lru_lambda, g_lam, m_lru_lambda, v_lru_lambda, (2, 256)),
        "sgu_ln_g": (sgu_ln_g, g_ln_g, m_sgu_ln_g, v_sgu_ln_g, (1, D)),
        "sgu_ln_b": (sgu_ln_b, g_ln_b, m_sgu_ln_b, v_sgu_ln_b, (1, D)),
        "sgu_w": (sgu_w, g_sgu_w, m_sgu_w, v_sgu_w, (NH * HD, HD)),
        "sgu_b": (sgu_b, g_sgu_b, m_sgu_b, v_sgu_b, (NH, HD)),
        "final_g": (final_g, g_final_g, m_final_g, v_final_g, (1, D)),
    }
    names_small = list(small_in)
    res_small = _adam_small([tuple(a.reshape(small_in[k][4]) for a in small_in[k][:4]) for k in names_small])
    full_shapes = {"ada_w": ada_w.shape, "w_in": w_in.shape, "w_out": w_out.shape}
    grads, deltas, new_m, new_v = {}, {}, {}, {}
    for k in ("ada_w", "w_in", "w_out"):
        g = {"ada_w": g_ada_w, "w_in": g_w_in, "w_out": g_w_out}[k]
        grads[k] = g.reshape(full_shapes[k])
        deltas[k], new_m[k], new_v[k] = (a.reshape(full_shapes[k]) for a in big[k])
    for k, res in zip(names_small, res_small):
        shape = small_in[k][0].shape
        grads[k] = small_in[k][1].reshape(shape)
        deltas[k], new_m[k], new_v[k] = (a.reshape(shape) for a in res)

    loss = lax.psum(loss_part[0, 0], ("x", "y", "c"))
    order = ["c_ctx", "ada_w", "ada_b", "norm_g", "w_in", "conv_w", "conv_b", "lru_wa", "lru_ba", "lru_wx", "lru_bx",
             "lru_lambda", "sgu_ln_g", "sgu_ln_b", "sgu_w", "sgu_b", "w_out", "final_g"]
    return (loss, grad_x.reshape(x.shape), *[grads[k] for k in order], *[deltas[k] for k in order],
            *[new_m[k] for k in order], *[new_v[k] for k in order])
```

```python
import functools

import jax
import jax.numpy as jnp
from jax import lax
from jax.experimental import pallas as pl
from jax.experimental.pallas import tpu as pltpu

F32 = jnp.float32
BF16 = jnp.bfloat16

D = 1024
NH = 8
HD = 128
NCHIP = 4
T = 256
NORM_EPS = 1e-6
LN_EPS = 1e-5
LRU_C = 8.0
ADAM_LR = 0.001
ADAM_B1 = 0.9
ADAM_B2 = 0.999
ADAM_EPS = 1e-08
ADAM_WD = 0.01
ADAM_STEP = 10

VMEM = pl.BlockSpec(memory_space=pltpu.VMEM)
ANY = pl.BlockSpec(memory_space=pl.ANY)
MESH = pl.DeviceIdType.MESH


def _cp(n_grid=0, vmem_mb=None):
    kw = {}
    if n_grid:
        kw["dimension_semantics"] = ("arbitrary",) * n_grid
    if vmem_mb:
        kw["vmem_limit_bytes"] = vmem_mb << 20
    return pltpu.CompilerParams(**kw)


def _sigmoid(x):
    return 1.0 / (1.0 + jnp.exp(-x))


def _silu_and_grad(x):
    s = _sigmoid(x)
    return x * s, s * (1.0 + x * (1.0 - s))


_GELU_K = 0.7978845608028654
_GELU_C = 0.044715


def _gelu_and_grad(x):
    x2 = x * x
    th = jnp.tanh(_GELU_K * (x + _GELU_C * x * x2))
    g = 0.5 * x * (1.0 + th)
    dg = 0.5 * (1.0 + th) + 0.5 * x * (1.0 - th * th) * (_GELU_K * (1.0 + 3.0 * _GELU_C * x2))
    return g, dg


def _softplus(x):
    return jnp.maximum(x, 0.0) + jnp.log1p(jnp.exp(-jnp.abs(x)))


def _lru_gate(pre, lam_row, d):
    r = _sigmoid(pre[:, 256 * d:256 * d + HD])
    gi = _sigmoid(pre[:, 256 * d + HD:256 * d + 2 * HD])
    lam = lam_row[:, HD * d:HD * d + HD]
    sp = _softplus(-lam)
    la = (-LRU_C) * r * sp
    a = jnp.exp(la)
    x2 = 2.0 * la
    m2 = jnp.where(x2 > -1e-3, -x2 * (1.0 + 0.5 * x2), 1.0 - a * a)
    mult = jnp.sqrt(m2)
    return r, gi, lam, sp, a, mult


def _dot(a, b):
    return jnp.dot(a, b, preferred_element_type=F32)


def _dot_tn(a, b):
    return lax.dot_general(a, b, (((0,), (0,)), ((), ())), preferred_element_type=F32)


def _dot_nt(a, b):
    return lax.dot_general(a, b, (((1,), (1,)), ((), ())), preferred_element_type=F32)


def _mo(v, m):
    return v if isinstance(v, int) else pl.multiple_of(v, m)


def _zrows(h, n):
    return pl.ds(h, n, stride=NH)


def _gather_weights(w_in, w_out, ada_w, smalls):
    specs = [
        ((16, 256), (64, 256), F32,
         lambda r, jj, cc: r.at[pl.ds(_mo(16 * jj + 8 * cc, 8), 8), :]),
        ((D, 768), (D, 3072), BF16,
         lambda r, jj, cc: r.at[pl.ds(_mo(512 * cc, 16), 512), pl.ds(_mo(768 * jj, 128), 768)]),
        ((D, 1280), (D, 5120), BF16,
         lambda r, jj, cc: r.at[pl.ds(_mo(512 * cc, 16), 512), pl.ds(_mo(1280 * jj, 128), 1280)]),
        ((512, D), (2048, D), BF16,
         lambda r, jj, cc: r.at[pl.ds(_mo(512 * jj + 256 * cc, 16), 256), :]),
    ]
    halves = [
        lambda r, cc: r.at[pl.ds(_mo(8 * cc, 8), 8), :],
        lambda r, cc: r.at[pl.ds(_mo(512 * cc, 16), 512), :],
        lambda r, cc: r.at[pl.ds(_mo(512 * cc, 16), 512), :],
        lambda r, cc: r.at[pl.ds(_mo(256 * cc, 16), 256), :],
    ]
    na = len(specs)

    def body(sm_ref, ada_ref, win_ref, wout_ref, sm_o, ada_o, win_o, wout_o,
             s_ada, s_win, s_wout, send_sems, recv_sems, local_sems):
        x, y, c = lax.axis_index("x"), lax.axis_index("y"), lax.axis_index("c")
        j = 2 * x + y
        sib = (x, y, 1 - c)
        chips = [(1 - x, y), (x, 1 - y), (1 - x, 1 - y)]
        cj = [2 * cx + cy for cx, cy in chips]
        outs = [sm_o, ada_o, win_o, wout_o]
        s_ada[...] = ada_ref[...].astype(BF16)
        s_win[...] = win_ref[...].astype(BF16)
        s_wout[...] = wout_ref[...].astype(BF16)
        srcs = [sm_ref, s_ada, s_win, s_wout]

        def copy(a, k, src, dst, to):
            return pltpu.make_async_remote_copy(src_ref=src, dst_ref=dst, send_sem=send_sems.at[6 * a + k],
                                                recv_sem=recv_sems.at[6 * a + k], device_id=to, device_id_type=MESH)

        local = []
        first = []
        for a in range(na):
            for cc in range(2):
                lc = pltpu.make_async_copy(halves[a](srcs[a], cc), specs[a][3](outs[a], j, cc), local_sems.at[2 * a + cc])
                lc.start()
                local.append(lc)
        for k in range(3):
            for a in range(na):
                cp = copy(a, k, halves[a](srcs[a], c), specs[a][3](outs[a], j, c), (*chips[k], c))
                cp.start()
                first.append(cp)
        passed = []
        for k in range(3):
            for a in range(na):
                reg = specs[a][3](outs[a], cj[k], c)
                copy(a, k, reg, reg, sib).wait_recv()
                cp = copy(a, 3 + k, reg, reg, sib)
                cp.start()
                passed.append(cp)
        for k in range(3):
            for a in range(na):
                reg = specs[a][3](outs[a], cj[k], 1 - c)
                copy(a, 3 + k, reg, reg, sib).wait_recv()
        for cp in first + passed:
            cp.wait_send()
        for lc in local:
            lc.wait()

    out_shape = tuple(jax.ShapeDtypeStruct(s[1], s[2]) for s in specs)
    return pl.pallas_call(
        body, name="gather_weights", out_shape=out_shape,
        in_specs=[VMEM] * 4, out_specs=(VMEM,) * 4,
        scratch_shapes=[pltpu.VMEM((D, 768), BF16), pltpu.VMEM((D, 1280), BF16), pltpu.VMEM((512, D), BF16),
                        pltpu.SemaphoreType.DMA((6 * na,)), pltpu.SemaphoreType.DMA((6 * na,)),
                        pltpu.SemaphoreType.DMA((2 * na,))],
        compiler_params=_cp(vmem_mb=56),
    )(smalls, ada_w, w_in, w_out)


RCHUNK = 16


def _reduce_all(g_w_in, g_w_out, pack):
    rp = pack.shape[0]
    hp = rp // 2
    assert hp % RCHUNK == 0
    wi_w = 1280

    def body(wi_hbm, wo_hbm, pk_hbm, wi_out, wo_out, pk_out,
             wi_mine, wi_recv, wi_send, wi_rb, wo_mine, wo_recv, wo_send, wo_rb, wo_own,
             pk_mine, pk_recv, pk_send, pk_rb, pk_own, send_sems, recv_sems, local_sems):
        x, y, c = lax.axis_index("x"), lax.axis_index("y"), lax.axis_index("c")
        j = 2 * x + y
        sib = (x, y, 1 - c)
        chips = [(1 - x, y), (x, 1 - y), (1 - x, 1 - y)]
        cj = [2 * cx + cy for cx, cy in chips]
        slabs = cj + [j]

        def copy(k, src, dst, to):
            return pltpu.make_async_remote_copy(src_ref=src, dst_ref=dst, send_sem=send_sems.at[k],
                                                recv_sem=recv_sems.at[k], device_id=to, device_id_type=MESH)

        def local(k, src, dst):
            cp = pltpu.make_async_copy(src, dst, local_sems.at[k])
            cp.start()
            return cp

        rows_half = lambda r, cc, n: r.at[pl.ds(_mo(cc * n, 16), n), :]
        cols_half = lambda r, cc, n: r.at[:, pl.ds(_mo(cc * n, 128), n)]
        wi_slab = lambda r, cc, jj: r.at[pl.ds(_mo(cc * 512, 16), 512), pl.ds(_mo(jj * wi_w, 128), wi_w)]
        pk_piece = lambda r, cc, jj: r.at[pl.ds(_mo(cc * hp, 16), hp), pl.ds(_mo(jj * 128, 128), 128)]

        sends = []

        def start(cp):
            cp.start()
            sends.append(cp)

        l_pk = local(0, rows_half(pk_hbm, c, hp), pk_mine)
        start(copy(0, rows_half(pk_hbm, 1 - c, hp), pk_recv, sib))
        l_wo = local(1, cols_half(wo_hbm, c, 512), wo_mine)
        start(copy(1, cols_half(wo_hbm, 1 - c, 512), wo_recv, sib))
        l_wi = []
        for s in range(4):
            l_wi.append(local(2 + s, wi_slab(wi_hbm, c, slabs[s]), wi_mine.at[s]))
            start(copy(2 + s, wi_slab(wi_hbm, 1 - c, slabs[s]), wi_recv.at[s], sib))

        def pair_sum(mine, recv, send, nrows, keep):
            def step(i, carry):
                rows = pl.ds(_mo(i * RCHUNK, RCHUNK), RCHUNK)
                s = mine[rows, :] + recv[rows, :]
                if keep:
                    mine[rows, :] = s
                if send is not None:
                    send[rows, :] = s.astype(BF16)
                return carry
            lax.fori_loop(0, nrows // RCHUNK, step, 0)

        def chip_sum(own, rb, nrows):
            def step(i, carry):
                rows = pl.ds(_mo(i * RCHUNK, RCHUNK), RCHUNK)
                own[rows, :] = (own[rows, :] + rb[0, rows, :].astype(F32)
                                + rb[1, rows, :].astype(F32) + rb[2, rows, :].astype(F32))
                return carry
            lax.fori_loop(0, nrows // RCHUNK, step, 0)

        l_pk.wait()
        copy(0, pk_recv, pk_recv, sib).wait_recv()
        pair_sum(pk_mine, pk_recv, pk_send, hp, True)
        for k in range(3):
            start(copy(6 + k, pk_send.at[:, pl.ds(_mo(cj[k] * 128, 128), 128)], pk_rb.at[k], (*chips[k], c)))
        l_pk_own = local(6, pk_mine.at[:, pl.ds(_mo(j * 128, 128), 128)], pk_own)

        l_wo.wait()
        copy(1, wo_recv, wo_recv, sib).wait_recv()
        pair_sum(wo_mine, wo_recv, wo_send, 2048, True)
        for k in range(3):
            start(copy(9 + k, wo_send.at[pl.ds(_mo(cj[k] * 512, 16), 512), :], wo_rb.at[k], (*chips[k], c)))
        l_wo_own = local(7, wo_mine.at[pl.ds(_mo(j * 512, 16), 512), :], wo_own)

        l_pk_own.wait()
        for k in range(3):
            copy(6 + k, pk_rb.at[k], pk_rb.at[k], sib).wait_recv()
        chip_sum(pk_own, pk_rb, hp)
        l_pk_out = local(8, pk_own, pk_piece(pk_out, c, j))
        start(copy(15, pk_own, pk_piece(pk_out, c, j), sib))
        for k in range(3):
            start(copy(16 + k, pk_own, pk_piece(pk_out, c, j), (*chips[k], c)))

        for s in range(3):
            l_wi[s].wait()
            copy(2 + s, wi_recv.at[s], wi_recv.at[s], sib).wait_recv()
            pair_sum(wi_mine.at[s], wi_recv.at[s], wi_send.at[s], 512, False)
            start(copy(12 + s, wi_send.at[s], wi_rb.at[s], (*chips[s], c)))
        l_wi[3].wait()
        copy(5, wi_recv.at[3], wi_recv.at[3], sib).wait_recv()
        pair_sum(wi_mine.at[3], wi_recv.at[3], None, 512, True)

        l_wo_own.wait()
        for k in range(3):
            copy(9 + k, wo_rb.at[k], wo_rb.at[k], sib).wait_recv()
        chip_sum(wo_own, wo_rb, 512)
        l_wo_out = local(9, wo_own, cols_half(wo_out, c, 512))
        start(copy(22, wo_own, cols_half(wo_out, c, 512), sib))

        for k in range(3):
            reg = pk_piece(pk_out, c, cj[k])
            copy(16 + k, reg, reg, sib).wait_recv()
            start(copy(19 + k, reg, reg, sib))

        for k in range(3):
            copy(12 + k, wi_rb.at[k], wi_rb.at[k], sib).wait_recv()
        chip_sum(wi_mine.at[3], wi_rb, 512)
        l_wi_out = local(10, wi_mine.at[3], rows_half(wi_out, c, 512))
        start(copy(23, wi_mine.at[3], rows_half(wi_out, c, 512), sib))

        reg = pk_piece(pk_out, 1 - c, j)
        copy(15, reg, reg, sib).wait_recv()
        for k in range(3):
            reg = pk_piece(pk_out, 1 - c, cj[k])
            copy(19 + k, reg, reg, sib).wait_recv()
        reg = cols_half(wo_out, 1 - c, 512)
        copy(22, reg, reg, sib).wait_recv()
        reg = rows_half(wi_out, 1 - c, 512)
        copy(23, reg, reg, sib).wait_recv()
        for cp in sends:
            cp.wait_send()
        for cp in (l_pk_out, l_wo_out, l_wi_out):
            cp.wait()

    return pl.pallas_call(
        body, name="reduce_all",
        out_shape=(jax.ShapeDtypeStruct((D, wi_w), F32), jax.ShapeDtypeStruct((512, D), F32),
                   jax.ShapeDtypeStruct(pack.shape, F32)),
        in_specs=[ANY] * 3, out_specs=(ANY,) * 3,
        scratch_shapes=[
            pltpu.VMEM((4, 512, wi_w), F32), pltpu.VMEM((4, 512, wi_w), F32), pltpu.VMEM((3, 512, wi_w), BF16),
            pltpu.VMEM((3, 512, wi_w), BF16),
            pltpu.VMEM((2048, 512), F32), pltpu.VMEM((2048, 512), F32), pltpu.VMEM((2048, 512), BF16),
            pltpu.VMEM((3, 512, 512), BF16), pltpu.VMEM((512, 512), F32),
            pltpu.VMEM((hp, 512), F32), pltpu.VMEM((hp, 512), F32), pltpu.VMEM((hp, 512), BF16),
            pltpu.VMEM((3, hp, 128), BF16), pltpu.VMEM((hp, 128), F32),
            pltpu.SemaphoreType.DMA((24,)), pltpu.SemaphoreType.DMA((24,)), pltpu.SemaphoreType.DMA((11,))],
        compiler_params=_cp(vmem_mb=56),
    )(g_w_in, g_w_out, pack)


def _ada_fwd(c, c_ctx, ada_w_full, ada_b):
    def body(c_ref, cc_ref, w_ref, b_ref, o_ref, lhs):
        lhs[...] = jnp.zeros_like(lhs)
        cv = c_ref[...]
        lhs[0:1, :] = cv * _sigmoid(cv)
        cc = cc_ref[...]
        lhs[1:2, :] = cc * _sigmoid(cc)
        o_ref[...] = _dot(lhs[...].astype(BF16), w_ref[...]) + b_ref[...]

    return pl.pallas_call(
        body, name="ada_fwd", out_shape=jax.ShapeDtypeStruct((8, 3 * D), F32),
        in_specs=[VMEM] * 4, out_specs=VMEM, scratch_shapes=[pltpu.VMEM((8, D), F32)],
        compiler_params=_cp(vmem_mb=32),
    )(c, c_ctx, ada_w_full, ada_b)


def _ada_bwd(c_all, dmx_all_j, dmc_j, dmx_all, dmc, c_ctx, ada_w_full):
    def body(c_ref, dmxj_ref, dmcj_ref, dmx_ref, dmc_ref, cc_ref, w_ref, gw_ref, gb_ref, gc_ref, lhs, rhs, dm8):
        lhs[...] = jnp.zeros_like(lhs)
        rhs[...] = jnp.zeros_like(rhs)
        cv = c_ref[...]
        lhs[0:8, :] = cv * _sigmoid(cv)
        cc = cc_ref[...]
        a_c, da_c = _silu_and_grad(cc)
        lhs[8:9, :] = a_c
        rhs[0:8, :] = dmxj_ref[...]
        rhs[8:9, :] = dmcj_ref[...]
        gw_ref[...] = _dot_tn(lhs[...].astype(BF16), rhs[...].astype(BF16))
        gb_ref[...] = jnp.sum(dmx_ref[...], axis=0, keepdims=True) + dmc_ref[...]
        dm8[...] = jnp.zeros_like(dm8)
        dm8[0:1, :] = dmc_ref[...]
        da = _dot_nt(dm8[...].astype(BF16), w_ref[...])
        gc_ref[...] = da[0:1, :] * da_c

    return pl.pallas_call(
        body, name="ada_bwd",
        out_shape=(jax.ShapeDtypeStruct((D, 768), F32), jax.ShapeDtypeStruct((1, 3 * D), F32),
                   jax.ShapeDtypeStruct((1, D), F32)),
        in_specs=[VMEM] * 7, out_specs=(VMEM,) * 3,
        scratch_shapes=[pltpu.VMEM((16, D), F32), pltpu.VMEM((16, 768), F32), pltpu.VMEM((8, 3 * D), F32)],
        compiler_params=_cp(vmem_mb=32),
    )(c_all, dmx_all_j, dmc_j, dmx_all, dmc, c_ctx, ada_w_full)


def _proj(x, mods, mrow, norm_g, w_full, nk, name):
    lx = x.shape[0]
    n = lx // T

    def body(x_ref, sh_ref, sc_ref, ng_ref, *rest):
        w_refs, hn_ref, z_refs = rest[:nk], rest[nk], rest[nk + 1:]
        xv = x_ref[...]
        r = lax.rsqrt(jnp.mean(xv * xv, axis=-1, keepdims=True) + NORM_EPS)
        hn = (xv * r) * ng_ref[...] * (1.0 + sc_ref[mrow:mrow + 1, :]) + sh_ref[mrow:mrow + 1, :]
        hb = hn.astype(BF16)
        hn_ref[...] = hb
        for k in range(nk):
            z_refs[k][...] = _dot(hb, w_refs[k][...])

    row = pl.BlockSpec((T, D), lambda i: (i, 0))
    in_specs = [row, pl.BlockSpec((8, D), lambda i: (0, 0)), pl.BlockSpec((8, D), lambda i: (0, 1)),
                pl.BlockSpec((1, D), lambda i: (0, 0))]
    in_specs += [pl.BlockSpec((D, D), lambda i, k=k: (0, k)) for k in range(nk)]
    out_shape = (jax.ShapeDtypeStruct((lx, D), BF16),) + tuple(jax.ShapeDtypeStruct((lx, D), F32) for _ in range(nk))
    return pl.pallas_call(
        body, name=name, grid=(n,), out_shape=out_shape, in_specs=in_specs, out_specs=(row,) * (nk + 1),
        compiler_params=_cp(1, vmem_mb=56),
    )(x, mods, mods, norm_g, *([w_full] * nk))


def _halo_specs(lx):
    last = lx // 8 - 1
    return [pl.BlockSpec((T, D), lambda i: (i, 0)),
            pl.BlockSpec((8, D), lambda i: (jnp.maximum(i * (T // 8) - 1, 0), 0)),
            pl.BlockSpec((8, D), lambda i: (jnp.minimum((i + 1) * (T // 8), last), 0))]


def _zhalo_specs(lx):
    last = lx // 8 - 1
    return [pl.BlockSpec((T * NH, HD), lambda i: (i, 0)),
            pl.BlockSpec((8 * NH, HD), lambda i: (jnp.maximum(i * (T // 8) - 1, 0), 0)),
            pl.BlockSpec((8 * NH, HD), lambda i: (jnp.minimum((i + 1) * (T // 8), last), 0))]


ZT = pl.BlockSpec((T * NH, HD), lambda i: (i, 0))
CONV_CHUNK = 32


def _lru_gates_fwd(xa, conv_wz, conv_bz, wcat, bcat, lamcat, name):
    lx = xa.shape[0]
    n = lx // T

    def body(xm, xp, xn, cw, cb, w_ref, b_ref, lam_ref, xaz_o, xcz_o, af_o, uf_o, ab_o, ub_o, pad):
        i = pl.program_id(0)
        pmask = jnp.where(i == 0, 0.0, 1.0)
        nmask = jnp.where(i == n - 1, 0.0, 1.0)
        for h in range(NH):
            cols = slice(HD * h, HD * h + HD)
            pad[_zrows(h, 8), :] = xp[:, cols] * pmask
            pad[pl.ds(8 * NH + h, T, stride=NH), :] = xm[:, cols]
            pad[pl.ds((T + 8) * NH + h, 8, stride=NH), :] = xn[:, cols] * nmask
        xaz_o[...] = pad[pl.ds(8 * NH, T * NH), :]

        def conv_chunk(ci, carry):
            base = pl.multiple_of(ci * (CONV_CHUNK * NH), CONV_CHUNK * NH)
            acc = None
            for k in range(4):
                sl = pad[pl.ds(base + (7 + k) * NH, CONV_CHUNK * NH), :].reshape(CONV_CHUNK, NH, HD)
                term = sl * cw[k][None]
                acc = term if acc is None else acc + term
            acc = acc + cb[...][None]
            xcz_o[pl.ds(base, CONV_CHUNK * NH), :] = acc.reshape(CONV_CHUNK * NH, HD)
            return carry
        lax.fori_loop(0, T // CONV_CHUNK, conv_chunk, 0)

        outs = ((af_o, uf_o), (ab_o, ub_o))
        for h in range(NH):
            xch = xcz_o[_zrows(h, T), :]
            pre = _dot(xch.astype(BF16), w_ref[h]) + b_ref[h:h + 1, :]
            for d in range(2):
                _, gi, _, _, a, mult = _lru_gate(pre, lam_ref[h:h + 1, :], d)
                outs[d][0][_zrows(h, T), :] = a
                outs[d][1][_zrows(h, T), :] = mult * gi * xch

    full = lambda shape: pl.BlockSpec(shape, lambda i: (0,) * len(shape))
    in_specs = _halo_specs(lx) + [full((4, NH, HD)), full((NH, HD)), full((NH, HD, 4 * HD)), full((NH, 4 * HD)),
                                  full((NH, 2 * HD))]
    zs = jax.ShapeDtypeStruct((lx * NH, HD), F32)
    return pl.pallas_call(
        body, name=name, grid=(n,), out_shape=(zs,) * 6, in_specs=in_specs, out_specs=(ZT,) * 6,
        scratch_shapes=[pltpu.VMEM(((T + 16) * NH, HD), F32)],
        compiler_params=_cp(1, vmem_mb=48),
    )(xa, xa, xa, conv_wz, conv_bz, wcat, bcat, lamcat)


def _scan(a_up, x_up, a_dn, x_dn, s_up, s_dn, post, name):
    lx = a_up.shape[0] // NH
    n = lx // T

    def body(au, xu, ad, xd, su0, sd0, *rest):
        if post:
            ou, od, fu, fd, carry = rest
        else:
            ou, od, pu, pd, fu, fd, carry = rest
        i = pl.program_id(0)

        @pl.when(i == 0)
        def _():
            carry[0] = su0[...]
            carry[1] = sd0[...]

        def step(k, s):
            su, sd = s
            ru = pl.ds(_mo(k * NH, NH), NH)
            rd = pl.ds(_mo((T - 1 - k) * NH, NH), NH)
            if post:
                vu = xu[ru, :] + su
                vd = xd[rd, :] + sd
                ou[ru, :] = vu
                od[rd, :] = vd
                return au[ru, :] * vu, ad[rd, :] * vd
            pu[ru, :] = su
            pd[rd, :] = sd
            vu = au[ru, :] * su + xu[ru, :]
            vd = ad[rd, :] * sd + xd[rd, :]
            ou[ru, :] = vu
            od[rd, :] = vd
            return vu, vd

        su, sd = lax.fori_loop(0, T, step, (carry[0], carry[1]), unroll=8)
        carry[0] = su
        carry[1] = sd
        fu[...] = su
        fd[...] = sd

    up = pl.BlockSpec((T * NH, HD), lambda i: (i, 0))
    dn = pl.BlockSpec((T * NH, HD), lambda i: (n - 1 - i, 0))
    st = pl.BlockSpec((NH, HD), lambda i: (0, 0))
    zs = jax.ShapeDtypeStruct((lx * NH, HD), F32)
    ss = jax.ShapeDtypeStruct((NH, HD), F32)
    if post:
        out_shape, out_specs = (zs, zs, ss, ss), (up, dn, st, st)
    else:
        out_shape, out_specs = (zs, zs, zs, zs, ss, ss), (up, dn, up, dn, st, st)
    return pl.pallas_call(
        body, name=name, grid=(n,), out_shape=out_shape, in_specs=[up, up, dn, dn, st, st], out_specs=out_specs,
        scratch_shapes=[pltpu.VMEM((2, NH, HD), F32)],
        compiler_params=_cp(1, vmem_mb=48),
    )(a_up, x_up, a_dn, x_dn, s_up, s_dn)


def _sgu_parts(u, v, lng, lnb, w_ref, bt_ref, mixed_s):
    ug, dug = _gelu_and_grad(u)
    vg, dvg = _gelu_and_grad(v)
    mu = jnp.mean(vg, axis=-1, keepdims=True)
    vc = vg - mu
    rstd = lax.rsqrt(jnp.mean(vc * vc, axis=-1, keepdims=True) + LN_EPS)
    vh = vc * rstd
    vn = (vh * lng + lnb).astype(BF16)
    for g in range(NH):
        cols = slice(HD * g, HD * g + HD)
        mixed_s[:, cols] = _dot(w_ref[g], vn[:, cols]) + bt_ref[:, g:g + 1]
    return ug, dug, dvg, rstd, vh, vn


def _sgu_fwd(u, v, ln_g, ln_b, sgu_w, sgu_bt):
    lx = u.shape[0]
    n = lx // HD

    def body(u_ref, v_ref, g_ref, b_ref, w_ref, bt_ref, y_ref, mixed_s):
        ug, _, _, _, _, _ = _sgu_parts(u_ref[...], v_ref[...], g_ref[...], b_ref[...], w_ref, bt_ref, mixed_s)
        y_ref[...] = ug * mixed_s[...]

    row = pl.BlockSpec((HD, D), lambda i: (i, 0))
    vec = pl.BlockSpec((1, D), lambda i: (0, 0))
    return pl.pallas_call(
        body, name="sgu_fwd", grid=(n,), out_shape=jax.ShapeDtypeStruct((lx, D), F32),
        in_specs=[row, row, vec, vec, pl.BlockSpec((NH, HD, HD), lambda i: (0, 0, 0)),
                  pl.BlockSpec((HD, NH), lambda i: (0, 0))],
        out_specs=row, scratch_shapes=[pltpu.VMEM((HD, D), F32)],
        compiler_params=_cp(1),
    )(u, v, ln_g, ln_b, sgu_w, sgu_bt)


def _sgu_bwd(u, v, dys, ln_g, ln_b, sgu_w, sgu_bt):
    lx = u.shape[0]
    n = lx // HD

    def body(u_ref, v_ref, dy_ref, g_ref, b_ref, w_ref, bt_ref, du_ref, dv_ref, dw_ref, db_ref, dg_ref, dbl_ref,
             mixed_s, dvn_s):
        i = pl.program_id(0)

        @pl.when(i == 0)
        def _():
            dw_ref[...] = jnp.zeros_like(dw_ref)
            db_ref[...] = jnp.zeros_like(db_ref)
            dg_ref[...] = jnp.zeros_like(dg_ref)
            dbl_ref[...] = jnp.zeros_like(dbl_ref)

        lng = g_ref[...]
        ug, dug, dvg, rstd, vh, vn = _sgu_parts(u_ref[...], v_ref[...], lng, b_ref[...], w_ref, bt_ref, mixed_s)
        dys_v = dy_ref[...]
        du_ref[...] = (dys_v * mixed_s[...] * dug).astype(BF16)
        dmix = dys_v * ug
        ones = jnp.ones((8, HD), BF16)
        for g in range(NH):
            cols = slice(HD * g, HD * g + HD)
            dm = dmix[:, cols]
            hi = dm.astype(BF16)
            lo = (dm - hi.astype(F32)).astype(BF16)
            dw_ref[g] += _dot_nt(hi, vn[:, cols])
            db_ref[g:g + 1, :] += (_dot_nt(ones, hi) + _dot_nt(ones, lo))[0:1, :]
            dvn_s[:, cols] = _dot_tn(w_ref[g], hi)
        dvn = dvn_s[...]
        dg_ref[...] += jnp.sum(dvn * vh, axis=0, keepdims=True)
        dbl_ref[...] += jnp.sum(dvn, axis=0, keepdims=True)
        dvh = dvn * lng
        dvg_in = rstd * (dvh - jnp.mean(dvh, axis=-1, keepdims=True)
                         - vh * jnp.mean(dvh * vh, axis=-1, keepdims=True))
        dv_ref[...] = (dvg_in * dvg).astype(BF16)

    row = pl.BlockSpec((HD, D), lambda i: (i, 0))
    vec = pl.BlockSpec((1, D), lambda i: (0, 0))
    wsp = pl.BlockSpec((NH, HD, HD), lambda i: (0, 0, 0))
    bsp = pl.BlockSpec((NH, HD), lambda i: (0, 0))
    return pl.pallas_call(
        body, name="sgu_bwd", grid=(n,),
        out_shape=(jax.ShapeDtypeStruct((lx, D), BF16), jax.ShapeDtypeStruct((lx, D), BF16),
                   jax.ShapeDtypeStruct((NH, HD, HD), F32), jax.ShapeDtypeStruct((NH, HD), F32),
                   jax.ShapeDtypeStruct((1, D), F32), jax.ShapeDtypeStruct((1, D), F32)),
        in_specs=[row, row, row, vec, vec, wsp, pl.BlockSpec((HD, NH), lambda i: (0, 0))],
        out_specs=(row, row, wsp, bsp, vec, vec),
        scratch_shapes=[pltpu.VMEM((HD, D), F32), pltpu.VMEM((HD, D), F32)],
        compiler_params=_cp(1),
    )(u, v, dys, ln_g, ln_b, sgu_w, sgu_bt)


def _out_fwd_bwd(hf_z, hb_z, ga, gb, ys, x, tgt, mods, final_g, w_out_full):
    lx = x.shape[0]
    n = lx // T

    def body(hf_ref, hb_ref, ga_ref, gb_ref, ys_ref, x_ref, t_ref, gx_ref, fg_ref, w_ref,
             loss_ref, dfg_ref, dgx_ref, dxn_ref, y_ref, do_ref, dga_ref, dgb_ref, dyl_ref, dys_ref, yl_s):
        i = pl.program_id(0)

        @pl.when(i == 0)
        def _():
            loss_ref[...] = jnp.zeros_like(loss_ref)
            dfg_ref[...] = jnp.zeros_like(dfg_ref)
            dgx_ref[...] = jnp.zeros_like(dgx_ref)

        for h in range(NH):
            yl_s[:, HD * h:HD * h + HD] = hf_ref[_zrows(h, T), :] + hb_ref[_zrows(h, T), :]
        yl = yl_s[...]
        gav = ga_ref[...]
        gbv = gb_ref[...]
        sa, dsa = _silu_and_grad(gav)
        sb, dsb = _silu_and_grad(gbv)
        ysv = ys_ref[...]
        y_ref[:, 0:D] = (yl * sa).astype(BF16)
        y_ref[:, D:2 * D] = (ysv * sb).astype(BF16)
        o = _dot(y_ref[...], w_ref[...])
        gx = gx_ref[0:1, :]
        xnew = x_ref[...] + gx * o
        r2 = lax.rsqrt(jnp.mean(xnew * xnew, axis=-1, keepdims=True) + NORM_EPS)
        xh = xnew * r2
        fg = fg_ref[...]
        err = xh * fg - t_ref[...]
        loss_ref[...] += 0.5 * jnp.sum(jnp.mean(err * err, axis=-1, keepdims=True), axis=0, keepdims=True)
        dout = err * (1.0 / D)
        dfg_ref[...] += jnp.sum(dout * xh, axis=0, keepdims=True)
        dxh = dout * fg
        dxn = r2 * (dxh - xh * jnp.mean(dxh * xh, axis=-1, keepdims=True))
        dxn_ref[...] = dxn
        dgx_ref[...] += jnp.sum(dxn * o, axis=0, keepdims=True)
        do = (dxn * gx).astype(BF16)
        do_ref[...] = do
        dy = _dot_nt(do, w_ref[...])
        dy1 = dy[:, 0:D]
        dy2 = dy[:, D:2 * D]
        dga_ref[...] = (dy1 * yl * dsa).astype(BF16)
        dgb_ref[...] = (dy2 * ysv * dsb).astype(BF16)
        dys_ref[...] = dy2 * sb
        yl_s[...] = dy1 * sa
        for h in range(NH):
            dyl_ref[_zrows(h, T), :] = yl_s[:, HD * h:HD * h + HD]

    row = pl.BlockSpec((T, D), lambda i: (i, 0))
    vec = pl.BlockSpec((1, D), lambda i: (0, 0))
    in_specs = [ZT, ZT, row, row, row, row, row, pl.BlockSpec((8, D), lambda i: (0, 2)), vec,
                pl.BlockSpec((2 * D, D), lambda i: (0, 0))]
    out_shape = (jax.ShapeDtypeStruct((1, 1), F32), jax.ShapeDtypeStruct((1, D), F32), jax.ShapeDtypeStruct((1, D), F32),
                 jax.ShapeDtypeStruct((lx, D), F32), jax.ShapeDtypeStruct((lx, 2 * D), BF16),
                 jax.ShapeDtypeStruct((lx, D), BF16), jax.ShapeDtypeStruct((lx, D), BF16),
                 jax.ShapeDtypeStruct((lx, D), BF16), jax.ShapeDtypeStruct((lx * NH, HD), F32),
                 jax.ShapeDtypeStruct((lx, D), F32))
    out_specs = (pl.BlockSpec((1, 1), lambda i: (0, 0)), vec, vec, row, pl.BlockSpec((T, 2 * D), lambda i: (i, 0)),
                 row, row, row, ZT, row)
    return pl.pallas_call(
        body, name="out_fwd_bwd", grid=(n,), out_shape=out_shape, in_specs=in_specs, out_specs=out_specs,
        scratch_shapes=[pltpu.VMEM((T, D), F32)],
        compiler_params=_cp(1, vmem_mb=56),
    )(hf_z, hb_z, ga, gb, ys, x, tgt, mods, final_g, w_out_full)


def _lru_gates_bwd(xc_z, lf_z, lb_z, pf_z, pb_z, wcat, bcat, lamcat, dw0, db0, dl0, name):
    lx = xc_z.shape[0] // NH
    n = lx // T

    def body(xc_ref, lf_ref, lb_ref, pf_ref, pb_ref, w_ref, b_ref, lam_ref, dw0_ref, db0_ref, dl0_ref,
             dxc_ref, dw_ref, db_ref, dl_ref, dpre_s):
        i = pl.program_id(0)

        @pl.when(i == 0)
        def _():
            dw_ref[...] = dw0_ref[...]
            db_ref[...] = db0_ref[...]
            dl_ref[...] = dl0_ref[...]

        lam_refs = (lf_ref, lb_ref)
        prev_refs = (pf_ref, pb_ref)
        for h in range(NH):
            xch = xc_ref[_zrows(h, T), :]
            xcb = xch.astype(BF16)
            pre = _dot(xcb, w_ref[h]) + b_ref[h:h + 1, :]
            dxc = jnp.zeros((T, HD), F32)
            for d in range(2):
                r, gi, lam, sp, a, mult = _lru_gate(pre, lam_ref[h:h + 1, :], d)
                du = lam_refs[d][_zrows(h, T), :]
                da = du * prev_refs[d][_zrows(h, T), :]
                dgi = du * mult * xch
                dxc = dxc + du * mult * gi
                dmult = du * gi * xch
                dla = da * a - dmult * (a * a) / mult
                dr = dla * ((-LRU_C) * sp)
                dsp = jnp.sum(dla * ((-LRU_C) * r), axis=0, keepdims=True)
                dl_ref[h:h + 1, HD * d:HD * d + HD] += dsp * (-_sigmoid(-lam))
                dpre_s[:, 256 * d:256 * d + HD] = dr * r * (1.0 - r)
                dpre_s[:, 256 * d + HD:256 * d + 2 * HD] = dgi * gi * (1.0 - gi)
            dpre = dpre_s[...]
            dpb = dpre.astype(BF16)
            dw_ref[h] += _dot_tn(xcb, dpb)
            db_ref[h:h + 1, :] += jnp.sum(dpre, axis=0, keepdims=True)
            dxc_ref[_zrows(h, T), :] = dxc + _dot_nt(dpb, w_ref[h])

    full = lambda shape: pl.BlockSpec(shape, lambda i: (0,) * len(shape))
    wsp, bsp, lsp = full((NH, HD, 4 * HD)), full((NH, 4 * HD)), full((NH, 2 * HD))
    return pl.pallas_call(
        body, name=name, grid=(n,),
        out_shape=(jax.ShapeDtypeStruct((lx * NH, HD), F32), jax.ShapeDtypeStruct((NH, HD, 4 * HD), F32),
                   jax.ShapeDtypeStruct((NH, 4 * HD), F32), jax.ShapeDtypeStruct((NH, 2 * HD), F32)),
        in_specs=[ZT] * 5 + [wsp, bsp, lsp, wsp, bsp, lsp], out_specs=(ZT, wsp, bsp, lsp),
        scratch_shapes=[pltpu.VMEM((T, 4 * HD), F32)],
        compiler_params=_cp(1, vmem_mb=48),
    )(xc_z, lf_z, lb_z, pf_z, pb_z, wcat, bcat, lamcat, dw0, db0, dl0)


def _conv_bwd(dxc_z, xa_z, conv_wz, dcw0, dcb0, name):
    lx = dxc_z.shape[0] // NH
    n = lx // T

    def body(dm, dp, dn, xa_ref, cw, dcw0_ref, dcb0_ref, dxa_ref, dcw_ref, dcb_ref, pad, dxa_s):
        i = pl.program_id(0)

        @pl.when(i == 0)
        def _():
            dcw_ref[...] = dcw0_ref[...]
            dcb_ref[...] = dcb0_ref[...]

        pmask = jnp.where(i == 0, 0.0, 1.0)
        nmask = jnp.where(i == n - 1, 0.0, 1.0)
        pad[pl.ds(0, 8 * NH), :] = dp[...] * pmask
        pad[pl.ds(8 * NH, T * NH), :] = dm[...]
        pad[pl.ds((T + 8) * NH, 8 * NH), :] = dn[...] * nmask

        def chunk(ci, carry):
            base = pl.multiple_of(ci * (CONV_CHUNK * NH), CONV_CHUNK * NH)
            xav = xa_ref[pl.ds(base, CONV_CHUNK * NH), :].reshape(CONV_CHUNK, NH, HD)
            acc = None
            for k in range(4):
                sl = pad[pl.ds(base + (9 - k) * NH, CONV_CHUNK * NH), :].reshape(CONV_CHUNK, NH, HD)
                term = sl * cw[k][None]
                acc = term if acc is None else acc + term
                dcw_ref[k] += jnp.sum(sl * xav, axis=0)
                if k == 1:
                    dcb_ref[...] += jnp.sum(sl, axis=0)
            dxa_s[pl.ds(base, CONV_CHUNK * NH), :] = acc.reshape(CONV_CHUNK * NH, HD)
            return carry
        lax.fori_loop(0, T // CONV_CHUNK, chunk, 0)
        for h in range(NH):
            dxa_ref[:, HD * h:HD * h + HD] = dxa_s[_zrows(h, T), :].astype(BF16)

    full = lambda shape: pl.BlockSpec(shape, lambda i: (0,) * len(shape))
    return pl.pallas_call(
        body, name=name, grid=(n,),
        out_shape=(jax.ShapeDtypeStruct((lx, D), BF16), jax.ShapeDtypeStruct((4, NH, HD), F32),
                   jax.ShapeDtypeStruct((NH, HD), F32)),
        in_specs=_zhalo_specs(lx) + [ZT, full((4, NH, HD)), full((4, NH, HD)), full((NH, HD))],
        out_specs=(pl.BlockSpec((T, D), lambda i: (i, 0)), full((4, NH, HD)), full((NH, HD))),
        scratch_shapes=[pltpu.VMEM(((T + 16) * NH, HD), F32), pltpu.VMEM((T * NH, HD), F32)],
        compiler_params=_cp(1, vmem_mb=48),
    )(dxc_z, dxc_z, dxc_z, xa_z, conv_wz, dcw0, dcb0)


def _proj_bwd(dzs, x, dxn, mods, mrow, norm_g, w_full, dng0, name):
    lx = x.shape[0]
    n = lx // T
    nk = len(dzs)
    has_x = dxn is not None

    def body(*refs):
        dz_refs = refs[:nk]
        w_refs = refs[nk:2 * nk]
        x_ref, sc_ref, ng_ref, dng0_ref = refs[2 * nk:2 * nk + 4]
        rest = refs[2 * nk + 4:]
        if has_x:
            dxn_ref, gx_ref, dng_ref, dsc_ref, dsh_ref = rest
        else:
            dng_ref, dsc_ref, dsh_ref = rest
        i = pl.program_id(0)

        @pl.when(i == 0)
        def _():
            dng_ref[...] = dng0_ref[...]
            dsc_ref[...] = jnp.zeros_like(dsc_ref)
            dsh_ref[...] = jnp.zeros_like(dsh_ref)

        dhn = _dot_nt(dz_refs[0][...], w_refs[0][...])
        for k in range(1, nk):
            dhn = dhn + _dot_nt(dz_refs[k][...], w_refs[k][...])
        xv = x_ref[...]
        r = lax.rsqrt(jnp.mean(xv * xv, axis=-1, keepdims=True) + NORM_EPS)
        xn = xv * r
        ng = ng_ref[...]
        sc1 = 1.0 + sc_ref[mrow:mrow + 1, :]
        t = dhn * xn
        dng_ref[...] += jnp.sum(t * sc1, axis=0, keepdims=True)
        dsc_ref[...] += jnp.sum(t * ng, axis=0, keepdims=True)
        dsh_ref[...] += jnp.sum(dhn, axis=0, keepdims=True)
        if has_x:
            dxh = dhn * (ng * sc1)
            gx_ref[...] = dxn_ref[...] + r * (dxh - xn * jnp.mean(dxh * xn, axis=-1, keepdims=True))

    row = pl.BlockSpec((T, D), lambda i: (i, 0))
    vec = pl.BlockSpec((1, D), lambda i: (0, 0))
    in_specs = [row] * nk + [pl.BlockSpec((D, D), lambda i, k=k: (0, k)) for k in range(nk)]
    in_specs += [row, pl.BlockSpec((8, D), lambda i: (0, 1)), vec, vec]
    args = list(dzs) + [w_full] * nk + [x, mods, norm_g, dng0]
    vs = jax.ShapeDtypeStruct((1, D), F32)
    out_shape, out_specs = (vs, vs, vs), (vec, vec, vec)
    if has_x:
        in_specs.append(row)
        args.append(dxn)
        out_shape = (jax.ShapeDtypeStruct((lx, D), F32),) + out_shape
        out_specs = (row,) + out_specs
    return pl.pallas_call(
        body, name=name, grid=(n,), out_shape=out_shape, in_specs=in_specs, out_specs=out_specs,
        compiler_params=_cp(1, vmem_mb=56),
    )(*args)


def _tn_matmul(a, bs, extra, name):
    lx, m = a.shape
    tm = min(lx, 1024)
    n = lx // tm
    widths = [b.shape[1] for b in bs]
    nb = len(bs)

    def body(a_ref, *rest):
        b_refs = rest[:nb]
        rest = rest[nb:]
        if extra is not None:
            ea_ref, eb_ref = rest[:2]
            rest = rest[2:]
        out_ref, acc = rest
        i = pl.program_id(0)
        av = a_ref[...]
        off = 0
        for k in range(nb):
            cols = slice(off, off + widths[k])
            part = _dot_tn(av, b_refs[k][...])

            @pl.when(i == 0)
            def _():
                acc[:, cols] = part

            @pl.when(i > 0)
            def _():
                acc[:, cols] += part
            off += widths[k]

        @pl.when(i == n - 1)
        def _():
            if extra is not None:
                acc[:, 0:widths[0]] += _dot_tn(ea_ref[...], eb_ref[...])
            pltpu.sync_copy(acc, out_ref)

    in_specs = [pl.BlockSpec((tm, m), lambda i: (i, 0))] + [pl.BlockSpec((tm, w), lambda i: (i, 0)) for w in widths]
    args = [a] + list(bs)
    if extra is not None:
        in_specs += [VMEM, VMEM]
        args += list(extra)
    return pl.pallas_call(
        body, name=name, grid=(n,), out_shape=jax.ShapeDtypeStruct((m, sum(widths)), F32),
        in_specs=in_specs, out_specs=ANY, scratch_shapes=[pltpu.VMEM((m, sum(widths)), F32)],
        compiler_params=_cp(1, vmem_mb=56),
    )(*args)


def _adam_math(w, g, m, v):
    m = ADAM_B1 * m + (1.0 - ADAM_B1) * g
    v = ADAM_B2 * v + (1.0 - ADAM_B2) * (g * g)
    m_hat = m / (1.0 - ADAM_B1 ** ADAM_STEP)
    v_hat = v / (1.0 - ADAM_B2 ** ADAM_STEP)
    delta = -ADAM_LR * (m_hat / (jnp.sqrt(v_hat) + ADAM_EPS) + ADAM_WD * w)
    return delta, m, v


def _adam_big(w, g, m, v, name):
    rows, cols = w.shape
    tr = 256

    def body(w_ref, g_ref, m_ref, v_ref, d_o, m_o, v_o):
        d, mm, vv = _adam_math(w_ref[...], g_ref[...], m_ref[...], v_ref[...])
        d_o[...] = d
        m_o[...] = mm
        v_o[...] = vv

    blk = pl.BlockSpec((tr, cols), lambda i: (i, 0))
    s = jax.ShapeDtypeStruct((rows, cols), F32)
    return pl.pallas_call(
        body, name=name, grid=(rows // tr,), out_shape=(s, s, s), in_specs=[blk] * 4, out_specs=(blk,) * 3,
        compiler_params=_cp(1, vmem_mb=48),
    )(w, g, m, v)


def _adam_small(items):
    ni = len(items)

    def body(*refs):
        ins, outs = refs[:4 * ni], refs[4 * ni:]
        for k in range(ni):
            w_ref, g_ref, m_ref, v_ref = ins[4 * k:4 * k + 4]
            d, mm, vv = _adam_math(w_ref[...], g_ref[...], m_ref[...], v_ref[...])
            outs[3 * k][...] = d
            outs[3 * k + 1][...] = mm
            outs[3 * k + 2][...] = vv

    flat = [a for it in items for a in it]
    out_shape = tuple(jax.ShapeDtypeStruct(it[0].shape, F32) for it in items for _ in range(3))
    res = pl.pallas_call(
        body, name="adam_small", out_shape=out_shape, in_specs=[VMEM] * (4 * ni), out_specs=(VMEM,) * (3 * ni),
        compiler_params=_cp(vmem_mb=32),
    )(*flat)
    return [tuple(res[3 * k:3 * k + 3]) for k in range(ni)]


def kernel(x, c, ctx, c_ctx, ada_w, ada_b, norm_g, w_in, conv_w, conv_b, lru_wa, lru_ba, lru_wx, lru_bx, lru_lambda, sgu_ln_g, sgu_ln_b, sgu_w, sgu_b, w_out, final_g, loss_target, m_c_ctx, m_ada_w, m_ada_b, m_norm_g, m_w_in, m_conv_w, m_conv_b, m_lru_wa, m_lru_ba, m_lru_wx, m_lru_bx, m_lru_lambda, m_sgu_ln_g, m_sgu_ln_b, m_sgu_w, m_sgu_b, m_w_out, m_final_g, v_c_ctx, v_ada_w, v_ada_b, v_norm_g, v_w_in, v_conv_w, v_conv_b, v_lru_wa, v_lru_ba, v_lru_wx, v_lru_bx, v_lru_lambda, v_sgu_ln_g, v_sgu_ln_b, v_sgu_w, v_sgu_b, v_w_out, v_final_g):
    ix, iy, ic = lax.axis_index("x"), lax.axis_index("y"), lax.axis_index("c")
    chip = 2 * ix + iy
    dev = 2 * chip + ic
    lx = x.shape[1]
    lc = ctx.shape[1]

    smalls = jnp.concatenate([conv_w[0], lru_lambda[0], jnp.zeros((10, 256), F32)], axis=0)
    sm_all, ada_full, w_in_full, w_out_full = _gather_weights(w_in[0], w_out[0], ada_w[0], smalls)
    sm3 = sm_all.reshape(NCHIP, 16, 256)
    conv_w_full = sm3[:, 0:4, :].transpose(1, 0, 2).reshape(4, D)
    lam_full = sm3[:, 4:6, :].transpose(1, 0, 2).reshape(2, D)
    conv_wz = conv_w_full.reshape(4, NH, HD)
    conv_bz = conv_b.reshape(NH, HD)
    lamcat = lam_full.reshape(2, NH, HD).transpose(1, 0, 2).reshape(NH, 2 * HD)
    wa, wx, ba, bx = lru_wa[0], lru_wx[0], lru_ba[0], lru_bx[0]
    wcat = jnp.concatenate([wa[0], wx[0], wa[1], wx[1]], axis=-1).astype(BF16)
    bcat = jnp.concatenate([ba[0], bx[0], ba[1], bx[1]], axis=-1)
    sgu_wb = sgu_w[0].astype(BF16)
    sgu_bt = sgu_b[0].T
    c_ctx2 = c_ctx.reshape(1, D)
    final_g2 = final_g.reshape(1, D)

    mods = _ada_fwd(c, c_ctx2, ada_full, ada_b)
    zero_s = jnp.zeros((NH, HD), F32)
    hn_c, xa_c = _proj(ctx[0], mods, 1, norm_g, w_in_full, 1, "proj_ctx")
    xaz_c, xcz_c, af_c, uf_c, ab_c, ub_c = _lru_gates_fwd(xa_c, conv_wz, conv_bz, wcat, bcat, lamcat, "lru_gates_ctx")
    _, _, pf_c, pb_c, hf0, hb0 = _scan(af_c, uf_c, ab_c, ub_c, zero_s, zero_s, False, "scan_ctx")

    hn, xa, ga, u, v, gb = _proj(x[0], mods, 0, norm_g, w_in_full, 5, "proj")
    xaz, xcz, af, uf, ab, ub = _lru_gates_fwd(xa, conv_wz, conv_bz, wcat, bcat, lamcat, "lru_gates")
    hf, hb, pf, pb, _, _ = _scan(af, uf, ab, ub, hf0, hb0, False, "scan")
    ys = _sgu_fwd(u, v, sgu_ln_g, sgu_ln_b, sgu_wb, sgu_bt)

    (loss_part, dfg, dgx, dxn, y, do, dga, dgb, dyl_z, dys) = _out_fwd_bwd(
        hf, hb, ga, gb, ys, x[0], loss_target[0], mods, final_g2, w_out_full)
    g_w_out_part = _tn_matmul(y, [do], None, "grad_w_out")

    du, dv, d_sgu_w, d_sgu_b, d_ln_g, d_ln_b = _sgu_bwd(u, v, dys, sgu_ln_g, sgu_ln_b, sgu_wb, sgu_bt)
    lb, lf, dh0b, dh0f = _scan(ab, dyl_z, af, dyl_z, zero_s, zero_s, True, "scan_adj")
    zw = jnp.zeros((NH, HD, 4 * HD), F32)
    zb = jnp.zeros((NH, 4 * HD), F32)
    zl = jnp.zeros((NH, 2 * HD), F32)
    dxc_z, dwc, dbc, dlc = _lru_gates_bwd(xcz, lf, lb, pf, pb, wcat, bcat, lamcat, zw, zb, zl, "lru_gates_bwd")
    dxa, dcw, dcb = _conv_bwd(dxc_z, xaz, conv_wz, jnp.zeros((4, NH, HD), F32), zero_s, "conv_bwd")

    zc = jnp.zeros((lc * NH, HD), F32)
    dhf_c = lax.dynamic_update_slice(zc, dh0f, ((lc - 1) * NH, 0))
    dhb_c = lax.dynamic_update_slice(zc, dh0b, (0, 0))
    lb_c, lf_c, _, _ = _scan(ab_c, dhb_c, af_c, dhf_c, zero_s, zero_s, True, "scan_adj_ctx")
    dxc_zc, dwc, dbc, dlc = _lru_gates_bwd(xcz_c, lf_c, lb_c, pf_c, pb_c, wcat, bcat, lamcat, dwc, dbc, dlc,
                                           "lru_gates_bwd_ctx")
    dxa_c, dcw, dcb = _conv_bwd(dxc_zc, xaz_c, conv_wz, dcw, dcb, "conv_bwd_ctx")

    dzs = [dxa, dga, du, dv, dgb]
    grad_x, dng, dsc_x, dsh_x = _proj_bwd(dzs, x[0], dxn, mods, 0, norm_g, w_in_full, jnp.zeros((1, D), F32), "proj_bwd")
    dng, dsc_c, dsh_c = _proj_bwd([dxa_c], ctx[0], None, mods, 1, norm_g, w_in_full, dng, "proj_bwd_ctx")
    g_w_in_part = _tn_matmul(hn, dzs, (hn_c, dxa_c), "grad_w_in")

    dmx = jnp.concatenate([dsh_x, dsc_x, dgx], axis=0)
    dmc = jnp.concatenate([dsh_c, dsc_c, jnp.zeros((1, D), F32)], axis=0)
    slot = jnp.concatenate([dmx, c], axis=0)
    slots = lax.dynamic_update_slice(jnp.zeros((32, D), F32), slot, (4 * dev, 0))
    vecs = jnp.concatenate([dfg, dng, dcb.reshape(1, D), d_ln_g, d_ln_b, dcw.reshape(4, D), dmc,
                            jnp.zeros((4, D), F32), slots], axis=0)
    d_sgu_w4 = d_sgu_w.reshape(4, 256, HD).transpose(1, 0, 2).reshape(256, 4 * HD)
    pad8 = lambda a: jnp.pad(a, ((0, 8 - a.shape[0]), (0, 4 * HD - a.shape[1])))
    pack = jnp.concatenate([dwc.reshape(NH * HD, 4 * HD), pad8(dbc), pad8(dlc), d_sgu_w4, pad8(d_sgu_b),
                            vecs.reshape(96, 4 * HD), jnp.zeros((8, 4 * HD), F32)], axis=0)
    g_w_in, g_w_out, tot = _reduce_all(g_w_in_part, g_w_out_part, pack)

    g_wc = tot[0:1024].reshape(NH, HD, 4 * HD)
    g_bc = tot[1024:1032]
    g_lc = tot[1032:1040, 0:2 * HD]
    g_sgu_w = tot[1040:1296].reshape(256, 4, HD).transpose(1, 0, 2).reshape(NH, HD, HD)
    g_sgu_b = tot[1296:1304, 0:HD]
    tv = tot[1304:1400].reshape(48, D)
    g_final_g, g_norm_g, g_conv_b, g_ln_g, g_ln_b = tv[0:1], tv[1:2], tv[2:3], tv[3:4], tv[4:5]
    g_conv_w_full = tv[5:9]
    dmc_tot = tv[9:12].reshape(1, 3 * D)
    slots_all = tv[16:48].reshape(8, 4, D)
    dmx_all = slots_all[:, 0:3, :].reshape(8, 3 * D)
    c_all = slots_all[:, 3, :]
    g_lru_wa = jnp.stack([g_wc[:, :, 0:HD], g_wc[:, :, 2 * HD:3 * HD]])
    g_lru_wx = jnp.stack([g_wc[:, :, HD:2 * HD], g_wc[:, :, 3 * HD:4 * HD]])
    g_lru_ba = jnp.stack([g_bc[:, 0:HD], g_bc[:, 2 * HD:3 * HD]])
    g_lru_bx = jnp.stack([g_bc[:, HD:2 * HD], g_bc[:, 3 * HD:4 * HD]])
    g_lam_full = jnp.stack([g_lc[:, 0:HD], g_lc[:, HD:2 * HD]]).reshape(2, D)
    g_conv_w = lax.dynamic_slice(g_conv_w_full, (0, 256 * chip), (4, 256))
    g_lam = lax.dynamic_slice(g_lam_full, (0, 256 * chip), (2, 256))
    dmx_all_j = lax.dynamic_slice(dmx_all, (0, 768 * chip), (8, 768))
    dmc_j = lax.dynamic_slice(dmc_tot, (0, 768 * chip), (1, 768))
    g_ada_w, g_ada_b, g_c_ctx = _ada_bwd(c_all, dmx_all_j, dmc_j, dmx_all, dmc_tot, c_ctx2, ada_full)

    big = {
        "ada_w": _adam_big(ada_w[0], g_ada_w, m_ada_w[0], v_ada_w[0], "adam_ada_w"),
        "w_in": _adam_big(w_in[0], g_w_in, m_w_in[0], v_w_in[0], "adam_w_in"),
        "w_out": _adam_big(w_out[0], g_w_out, m_w_out[0], v_w_out[0], "adam_w_out"),
    }
    small_in = {
        "c_ctx": (c_ctx, g_c_ctx, m_c_ctx, v_c_ctx, (1, D)),
        "ada_b": (ada_b, g_ada_b, m_ada_b, v_ada_b, (1, 3 * D)),
        "norm_g": (norm_g, g_norm_g, m_norm_g, v_norm_g, (1, D)),
        "conv_w": (conv_w, g_conv_w, m_conv_w, v_conv_w, (4, 256)),
        "conv_b": (conv_b, g_conv_b, m_conv_b, v_conv_b, (1, D)),
        "lru_wa": (lru_wa, g_lru_wa, m_lru_wa, v_lru_wa, (2 * NH * HD, HD)),
        "lru_ba": (lru_ba, g_lru_ba, m_lru_ba, v_lru_ba, (2 * NH, HD)),
        "lru_wx": (lru_wx, g_lru_wx, m_lru_wx, v_lru_wx, (2 * NH * HD, HD)),
        "lru_bx": (lru_bx, g_lru_bx, m_lru_bx, v_lru_bx, (2 * NH, HD)),
        "lru_lambda": (lru_lambda, g_lam, m_lru_lambda, v_lru_lambda, (2, 256)),
        "sgu_ln_g": (sgu_ln_g, g_ln_g, m_sgu_ln_g, v_sgu_ln_g, (1, D)),
        "sgu_ln_b": (sgu_ln_b, g_ln_b, m_sgu_ln_b, v_sgu_ln_b, (1, D)),
        "sgu_w": (sgu_w, g_sgu_w, m_sgu_w, v_sgu_w, (NH * HD, HD)),
        "sgu_b": (sgu_b, g_sgu_b, m_sgu_b, v_sgu_b, (NH, HD)),
        "final_g": (final_g, g_final_g, m_final_g, v_final_g, (1, D)),
    }
    names_small = list(small_in)
    res_small = _adam_small([tuple(a.reshape(small_in[k][4]) for a in small_in[k][:4]) for k in names_small])
    full_shapes = {"ada_w": ada_w.shape, "w_in": w_in.shape, "w_out": w_out.shape}
    grads, deltas, new_m, new_v = {}, {}, {}, {}
    for k in ("ada_w", "w_in", "w_out"):
        g = {"ada_w": g_ada_w, "w_in": g_w_in, "w_out": g_w_out}[k]
        grads[k] = g.reshape(full_shapes[k])
        deltas[k], new_m[k], new_v[k] = (a.reshape(full_shapes[k]) for a in big[k])
    for k, res in zip(names_small, res_small):
        shape = small_in[k][0].shape
        grads[k] = small_in[k][1].reshape(shape)
        deltas[k], new_m[k], new_v[k] = (a.reshape(shape) for a in res)

    loss = lax.psum(loss_part[0, 0], ("x", "y", "c"))
    order = ["c_ctx", "ada_w", "ada_b", "norm_g", "w_in", "conv_w", "conv_b", "lru_wa", "lru_ba", "lru_wx", "lru_bx",
             "lru_lambda", "sgu_ln_g", "sgu_ln_b", "sgu_w", "sgu_b", "w_out", "final_g"]
    return (loss, grad_x.reshape(x.shape), *[grads[k] for k in order], *[deltas[k] for k in order],
            *[new_m[k] for k in order], *[new_v[k] for k in order])
```

```python
import functools

import jax
import jax.numpy as jnp
from jax import lax
from jax.experimental import pallas as pl
from jax.experimental.pallas import tpu as pltpu

F32 = jnp.float32
BF16 = jnp.bfloat16

D = 1024
NH = 8
HD = 128
NCHIP = 4
T = 256
NORM_EPS = 1e-6
LN_EPS = 1e-5
LRU_C = 8.0
ADAM_LR = 0.001
ADAM_B1 = 0.9
ADAM_B2 = 0.999
ADAM_EPS = 1e-08
ADAM_WD = 0.01
ADAM_STEP = 10

VMEM = pl.BlockSpec(memory_space=pltpu.VMEM)
ANY = pl.BlockSpec(memory_space=pl.ANY)
MESH = pl.DeviceIdType.MESH


def _cp(n_grid=0, vmem_mb=None):
    kw = {}
    if n_grid:
        kw["dimension_semantics"] = ("arbitrary",) * n_grid
    if vmem_mb:
        kw["vmem_limit_bytes"] = vmem_mb << 20
    return pltpu.CompilerParams(**kw)


def _sigmoid(x):
    return 1.0 / (1.0 + jnp.exp(-x))


def _silu_and_grad(x):
    s = _sigmoid(x)
    return x * s, s * (1.0 + x * (1.0 - s))


_GELU_K = 0.7978845608028654
_GELU_C = 0.044715


def _gelu_and_grad(x):
    x2 = x * x
    th = jnp.tanh(_GELU_K * (x + _GELU_C * x * x2))
    g = 0.5 * x * (1.0 + th)
    dg = 0.5 * (1.0 + th) + 0.5 * x * (1.0 - th * th) * (_GELU_K * (1.0 + 3.0 * _GELU_C * x2))
    return g, dg


def _softplus(x):
    return jnp.maximum(x, 0.0) + jnp.log1p(jnp.exp(-jnp.abs(x)))


def _lru_gate(pre, lam_row, d):
    r = _sigmoid(pre[:, 256 * d:256 * d + HD])
    gi = _sigmoid(pre[:, 256 * d + HD:256 * d + 2 * HD])
    lam = lam_row[:, HD * d:HD * d + HD]
    sp = _softplus(-lam)
    la = (-LRU_C) * r * sp
    a = jnp.exp(la)
    x2 = 2.0 * la
    m2 = jnp.where(x2 > -1e-3, -x2 * (1.0 + 0.5 * x2), 1.0 - a * a)
    mult = jnp.sqrt(m2)
    return r, gi, lam, sp, a, mult


def _dot(a, b):
    return jnp.dot(a, b, preferred_element_type=F32)


def _dot_tn(a, b):
    return lax.dot_general(a, b, (((0,), (0,)), ((), ())), preferred_element_type=F32)


def _dot_nt(a, b):
    return lax.dot_general(a, b, (((1,), (1,)), ((), ())), preferred_element_type=F32)


def _mo(v, m):
    return v if isinstance(v, int) else pl.multiple_of(v, m)


def _zrows(h, n):
    return pl.ds(h, n, stride=NH)


def _gather_in(c, c_ctx, ada_w, ada_b_j, w_in, w_out, smalls):
    specs = [
        ((64, 256), F32, lambda r, jj, cc: r.at[pl.ds(_mo(16 * jj + 8 * cc, 8), 8), :]),
        ((D, 5120), BF16, lambda r, jj, cc: r.at[pl.ds(_mo(512 * cc, 16), 512), pl.ds(_mo(1280 * jj, 128), 1280)]),
    ]
    halves = [lambda r, cc: r.at[pl.ds(_mo(8 * cc, 8), 8), :],
              lambda r, cc: r.at[pl.ds(_mo(512 * cc, 16), 512), :]]
    na = len(specs)
    n_sem = 6 * na + 10

    def body(c_ref, cc_ref, ada_ref, adab_ref, win_ref, wout_ref, sm_ref,
             mods_o, call_o, sm_o, win_o, wout_o, ada_o, s_win, cslot, lhs, mbuf, send_sems, recv_sems, local_sems):
        x, y, c = lax.axis_index("x"), lax.axis_index("y"), lax.axis_index("c")
        j = 2 * x + y
        dev = 2 * j + c
        sib = (x, y, 1 - c)
        chips = [(1 - x, y), (x, 1 - y), (1 - x, 1 - y)]
        cj = [2 * cx + cy for cx, cy in chips]
        outs = [sm_o, win_o]
        srcs = [sm_ref, s_win]

        def copy(idx, src, dst, to):
            return pltpu.make_async_remote_copy(src_ref=src, dst_ref=dst, send_sem=send_sems.at[idx],
                                                recv_sem=recv_sems.at[idx], device_id=to, device_id_type=MESH)

        sends = []

        def start(cp):
            cp.start()
            sends.append(cp)

        cslot[...] = jnp.zeros_like(cslot)
        cslot[0:1, :] = c_ref[...]
        my_slot = pl.ds(_mo(8 * dev, 8), 8)
        others = [sib] + [(*chips[k], c) for k in range(3)] + [(*chips[k], 1 - c) for k in range(3)]
        other_dev = [dev + 1 - 2 * c] + [2 * cj[k] + c for k in range(3)] + [2 * cj[k] + 1 - c for k in range(3)]
        base = 6 * na
        for r in range(7):
            start(copy(base + r, cslot, call_o.at[my_slot, :], others[r]))
        call_o[my_slot, :] = cslot[...]

        s_win[...] = win_ref[...].astype(BF16)
        local = []
        for a in range(na):
            for cc in range(2):
                lc = pltpu.make_async_copy(halves[a](srcs[a], cc), specs[a][2](outs[a], j, cc), local_sems.at[2 * a + cc])
                lc.start()
                local.append(lc)
        for k in range(3):
            for a in range(na):
                start(copy(6 * a + k, halves[a](srcs[a], c), specs[a][2](outs[a], j, c), (*chips[k], c)))
        wout_o[...] = wout_ref[...].astype(BF16)
        ada_o[...] = ada_ref[...].astype(BF16)

        for r in range(7):
            slot = call_o.at[pl.ds(_mo(8 * other_dev[r], 8), 8), :]
            copy(base + r, slot, slot, sib).wait_recv()
        lhs[...] = jnp.zeros_like(lhs)
        for b in range(8):
            cv = call_o[8 * b:8 * b + 1, :]
            lhs[b:b + 1, :] = cv * _sigmoid(cv)
        cv = cc_ref[...]
        lhs[8:9, :] = cv * _sigmoid(cv)
        mbuf[j] = _dot(lhs[...].astype(BF16), ada_o[...]) + adab_ref[...]
        for k in range(3):
            start(copy(base + 7 + k, mbuf.at[j], mbuf.at[j], (*chips[k], c)))
        for k in range(3):
            copy(base + 7 + k, mbuf.at[cj[k]], mbuf.at[cj[k]], sib).wait_recv()
        mods_o[...] = jnp.zeros_like(mods_o)
        for jj in range(NCHIP):
            mods_o[0:1, 768 * jj:768 * jj + 768] = mbuf[jj, pl.ds(dev, 1), :]
            mods_o[1:2, 768 * jj:768 * jj + 768] = mbuf[jj, 8:9, :]

        for k in range(3):
            for a in range(na):
                reg = specs[a][2](outs[a], cj[k], c)
                copy(6 * a + k, reg, reg, sib).wait_recv()
                start(copy(6 * a + 3 + k, reg, reg, sib))
        for k in range(3):
            for a in range(na):
                reg = specs[a][2](outs[a], cj[k], 1 - c)
                copy(6 * a + 3 + k, reg, reg, sib).wait_recv()
        for cp in sends:
            cp.wait_send()
        for lc in local:
            lc.wait()

    out_shape = (jax.ShapeDtypeStruct((8, 3 * D), F32), jax.ShapeDtypeStruct((64, D), F32),
                 jax.ShapeDtypeStruct(specs[0][0], F32), jax.ShapeDtypeStruct(specs[1][0], BF16),
                 jax.ShapeDtypeStruct((512, D), BF16), jax.ShapeDtypeStruct((D, 768), BF16))
    return pl.pallas_call(
        body, name="gather_in", out_shape=out_shape,
        in_specs=[VMEM] * 7, out_specs=(VMEM,) * 6,
        scratch_shapes=[pltpu.VMEM((D, 1280), BF16), pltpu.VMEM((8, D), F32), pltpu.VMEM((16, D), F32),
                        pltpu.VMEM((NCHIP, 16, 768), F32),
                        pltpu.SemaphoreType.DMA((n_sem,)), pltpu.SemaphoreType.DMA((n_sem,)),
                        pltpu.SemaphoreType.DMA((2 * na,))],
        compiler_params=_cp(vmem_mb=56),
    )(c, c_ctx, ada_w, ada_b_j, w_in, w_out, smalls)


HBM = pl.BlockSpec(memory_space=pltpu.HBM)
SEM = pl.BlockSpec(memory_space=pltpu.SEMAPHORE)


def _late_gather_regions(x, y, c):
    chips = [(1 - x, y), (x, 1 - y), (1 - x, 1 - y)]
    wo_reg = lambda r, jj, cc: r.at[pl.ds(_mo(512 * jj + 256 * cc, 16), 256), :]
    ada_reg = lambda r, jj, cc: r.at[pl.ds(_mo(512 * cc, 16), 512), pl.ds(_mo(768 * jj, 128), 768)]
    return chips, wo_reg, ada_reg


def _late_gather_start(wo_land, ada_land):
    def body(wol_ref, adal_ref, wo_ss, wo_rs, ada_ss, ada_rs, wol_thru, adal_thru, token):
        x, y, c = lax.axis_index("x"), lax.axis_index("y"), lax.axis_index("c")
        j = 2 * x + y
        chips, wo_reg, ada_reg = _late_gather_regions(x, y, c)
        for k in range(3):
            for cc in range(2):
                to = (*chips[k], cc)
                pltpu.make_async_remote_copy(src_ref=wo_reg(wol_ref, j, c), dst_ref=wo_reg(wol_ref, j, c),
                                             send_sem=wo_ss.at[2 * k + cc], recv_sem=wo_rs.at[2 * k + c],
                                             device_id=to, device_id_type=MESH).start()
                pltpu.make_async_remote_copy(src_ref=ada_reg(adal_ref, j, c), dst_ref=ada_reg(adal_ref, j, c),
                                             send_sem=ada_ss.at[2 * k + cc], recv_sem=ada_rs.at[2 * k + c],
                                             device_id=to, device_id_type=MESH).start()
        token[...] = jnp.zeros_like(token)

    sems = pltpu.SemaphoreType.DMA((6,))
    return pl.pallas_call(
        body, name="late_gather_start",
        out_shape=(sems, sems, sems, sems, pltpu.HBM(wo_land.shape, BF16), pltpu.HBM(ada_land.shape, BF16),
                   jax.ShapeDtypeStruct((8, 128), F32)),
        in_specs=(HBM, HBM), out_specs=(SEM, SEM, SEM, SEM, HBM, HBM, VMEM), input_output_aliases={0: 4, 1: 5},
        compiler_params=pltpu.CompilerParams(has_side_effects=pltpu.SideEffectType.DATAFLOW_SIDE_EFFECTING),
    )(pltpu.with_memory_space_constraint(wo_land, pltpu.HBM), pltpu.with_memory_space_constraint(ada_land, pltpu.HBM))


def _late_gather_wait(land, send_sems, recv_sems, which, after, name):
    def body(land_ref, ss, rs, after_ref, land_out):
        x, y, c = lax.axis_index("x"), lax.axis_index("y"), lax.axis_index("c")
        j = 2 * x + y
        chips, wo_reg, ada_reg = _late_gather_regions(x, y, c)
        reg = wo_reg if which == "w_out" else ada_reg
        for k in range(3):
            kj = 2 * chips[k][0] + chips[k][1]
            for cc in range(2):
                cp = pltpu.make_async_remote_copy(src_ref=reg(land_ref, j, c), dst_ref=reg(land_ref, kj, cc),
                                                  send_sem=ss.at[2 * k + cc], recv_sem=rs.at[2 * k + cc],
                                                  device_id=(*chips[k], cc), device_id_type=MESH)
                cp.wait_send()
                cp.wait_recv()

    return pl.pallas_call(
        body, name=name, out_shape=pltpu.HBM(land.shape, land.dtype),
        in_specs=(HBM, SEM, SEM, ANY), out_specs=HBM, input_output_aliases={0: 0},
        compiler_params=pltpu.CompilerParams(has_side_effects=pltpu.SideEffectType.DATAFLOW_SIDE_EFFECTING),
    )(land, send_sems, recv_sems, after)


RCHUNK = 16


def _reduce_all(g_w_in, g_w_out, pack):
    rp = pack.shape[0]
    hp = rp // 2
    assert hp % RCHUNK == 0
    wi_w = 1280

    def body(wi_hbm, wo_hbm, pk_hbm, wi_out, wo_out, pk_out,
             wi_mine, wi_recv, wi_send, wi_rb, wo_mine, wo_recv, wo_send, wo_rb, wo_own,
             pk_mine, pk_recv, pk_send, pk_rb, pk_own, send_sems, recv_sems, local_sems):
        x, y, c = lax.axis_index("x"), lax.axis_index("y"), lax.axis_index("c")
        j = 2 * x + y
        sib = (x, y, 1 - c)
        chips = [(1 - x, y), (x, 1 - y), (1 - x, 1 - y)]
        cj = [2 * cx + cy for cx, cy in chips]
        slabs = cj + [j]

        def copy(k, src, dst, to):
            return pltpu.make_async_remote_copy(src_ref=src, dst_ref=dst, send_sem=send_sems.at[k],
                                                recv_sem=recv_sems.at[k], device_id=to, device_id_type=MESH)

        def local(k, src, dst):
            cp = pltpu.make_async_copy(src, dst, local_sems.at[k])
            cp.start()
            return cp

        rows_half = lambda r, cc, n: r.at[pl.ds(_mo(cc * n, 16), n), :]
        cols_half = lambda r, cc, n: r.at[:, pl.ds(_mo(cc * n, 128), n)]
        wi_slab = lambda r, cc, jj: r.at[pl.ds(_mo(cc * 512, 16), 512), pl.ds(_mo(jj * wi_w, 128), wi_w)]
        pk_piece = lambda r, cc, jj: r.at[pl.ds(_mo(cc * hp, 16), hp), pl.ds(_mo(jj * 128, 128), 128)]

        sends = []

        def start(cp):
            cp.start()
            sends.append(cp)

        l_pk = local(0, rows_half(pk_hbm, c, hp), pk_mine)
        start(copy(0, rows_half(pk_hbm, 1 - c, hp), pk_recv, sib))
        l_wo = local(1, cols_half(wo_hbm, c, 512), wo_mine)
        start(copy(1, cols_half(wo_hbm, 1 - c, 512), wo_recv, sib))
        l_wi = []
        for s in range(4):
            l_wi.append(local(2 + s, wi_slab(wi_hbm, c, slabs[s]), wi_mine.at[s]))
            start(copy(2 + s, wi_slab(wi_hbm, 1 - c, slabs[s]), wi_recv.at[s], sib))

        def pair_sum(mine, recv, send, nrows, keep):
            def step(i, carry):
                rows = pl.ds(_mo(i * RCHUNK, RCHUNK), RCHUNK)
                s = mine[rows, :] + recv[rows, :]
                if keep:
                    mine[rows, :] = s
                if send is not None:
                    send[rows, :] = s.astype(BF16)
                return carry
            lax.fori_loop(0, nrows // RCHUNK, step, 0)

        def chip_sum(own, rb, nrows):
            def step(i, carry):
                rows = pl.ds(_mo(i * RCHUNK, RCHUNK), RCHUNK)
                own[rows, :] = (own[rows, :] + rb[0, rows, :].astype(F32)
                                + rb[1, rows, :].astype(F32) + rb[2, rows, :].astype(F32))
                return carry
            lax.fori_loop(0, nrows // RCHUNK, step, 0)

        l_pk.wait()
        copy(0, pk_recv, pk_recv, sib).wait_recv()
        pair_sum(pk_mine, pk_recv, pk_send, hp, True)
        for k in range(3):
            start(copy(6 + k, pk_send.at[:, pl.ds(_mo(cj[k] * 128, 128), 128)], pk_rb.at[k], (*chips[k], c)))
        l_pk_own = local(6, pk_mine.at[:, pl.ds(_mo(j * 128, 128), 128)], pk_own)

        l_wo.wait()
        copy(1, wo_recv, wo_recv, sib).wait_recv()
        pair_sum(wo_mine, wo_recv, wo_send, 2048, True)
        for k in range(3):
            start(copy(9 + k, wo_send.at[pl.ds(_mo(cj[k] * 512, 16), 512), :], wo_rb.at[k], (*chips[k], c)))
        l_wo_own = local(7, wo_mine.at[pl.ds(_mo(j * 512, 16), 512), :], wo_own)

        l_pk_own.wait()
        for k in range(3):
            copy(6 + k, pk_rb.at[k], pk_rb.at[k], sib).wait_recv()
        chip_sum(pk_own, pk_rb, hp)
        l_pk_out = local(8, pk_own, pk_piece(pk_out, c, j))
        start(copy(15, pk_own, pk_piece(pk_out, c, j), sib))
        for k in range(3):
            start(copy(16 + k, pk_own, pk_piece(pk_out, c, j), (*chips[k], c)))

        for s in range(3):
            l_wi[s].wait()
            copy(2 + s, wi_recv.at[s], wi_recv.at[s], sib).wait_recv()
            pair_sum(wi_mine.at[s], wi_recv.at[s], wi_send.at[s], 512, False)
            start(copy(12 + s, wi_send.at[s], wi_rb.at[s], (*chips[s], c)))
        l_wi[3].wait()
        copy(5, wi_recv.at[3], wi_recv.at[3], sib).wait_recv()
        pair_sum(wi_mine.at[3], wi_recv.at[3], None, 512, True)

        l_wo_own.wait()
        for k in range(3):
            copy(9 + k, wo_rb.at[k], wo_rb.at[k], sib).wait_recv()
        chip_sum(wo_own, wo_rb, 512)
        l_wo_out = local(9, wo_own, cols_half(wo_out, c, 512))
        start(copy(22, wo_own, cols_half(wo_out, c, 512), sib))

        for k in range(3):
            reg = pk_piece(pk_out, c, cj[k])
            copy(16 + k, reg, reg, sib).wait_recv()
            start(copy(19 + k, reg, reg, sib))

        for k in range(3):
            copy(12 + k, wi_rb.at[k], wi_rb.at[k], sib).wait_recv()
        chip_sum(wi_mine.at[3], wi_rb, 512)
        l_wi_out = local(10, wi_mine.at[3], rows_half(wi_out, c, 512))
        start(copy(23, wi_mine.at[3], rows_half(wi_out, c, 512), sib))

        reg = pk_piece(pk_out, 1 - c, j)
        copy(15, reg, reg, sib).wait_recv()
        for k in range(3):
            reg = pk_piece(pk_out, 1 - c, cj[k])
            copy(19 + k, reg, reg, sib).wait_recv()
        reg = cols_half(wo_out, 1 - c, 512)
        copy(22, reg, reg, sib).wait_recv()
        reg = rows_half(wi_out, 1 - c, 512)
        copy(23, reg, reg, sib).wait_recv()
        for cp in sends:
            cp.wait_send()
        for cp in (l_pk_out, l_wo_out, l_wi_out):
            cp.wait()

    return pl.pallas_call(
        body, name="reduce_all",
        out_shape=(jax.ShapeDtypeStruct((D, wi_w), F32), jax.ShapeDtypeStruct((512, D), F32),
                   jax.ShapeDtypeStruct(pack.shape, F32)),
        in_specs=[ANY] * 3, out_specs=(ANY,) * 3,
        scratch_shapes=[
            pltpu.VMEM((4, 512, wi_w), F32), pltpu.VMEM((4, 512, wi_w), F32), pltpu.VMEM((3, 512, wi_w), BF16),
            pltpu.VMEM((3, 512, wi_w), BF16),
            pltpu.VMEM((2048, 512), F32), pltpu.VMEM((2048, 512), F32), pltpu.VMEM((2048, 512), BF16),
            pltpu.VMEM((3, 512, 512), BF16), pltpu.VMEM((512, 512), F32),
            pltpu.VMEM((hp, 512), F32), pltpu.VMEM((hp, 512), F32), pltpu.VMEM((hp, 512), BF16),
            pltpu.VMEM((3, hp, 128), BF16), pltpu.VMEM((hp, 128), F32),
            pltpu.SemaphoreType.DMA((24,)), pltpu.SemaphoreType.DMA((24,)), pltpu.SemaphoreType.DMA((11,))],
        compiler_params=_cp(vmem_mb=56),
    )(g_w_in, g_w_out, pack)


def _ada_bwd(c_all, dmx_all_j, dmc_j, dmx_all, dmc, c_ctx, ada_w_full):
    def body(c_ref, dmxj_ref, dmcj_ref, dmx_ref, dmc_ref, cc_ref, w_ref, gw_ref, gb_ref, gc_ref, lhs, rhs, dm8):
        lhs[...] = jnp.zeros_like(lhs)
        rhs[...] = jnp.zeros_like(rhs)
        cv = c_ref[...]
        lhs[0:8, :] = cv * _sigmoid(cv)
        cc = cc_ref[...]
        a_c, da_c = _silu_and_grad(cc)
        lhs[8:9, :] = a_c
        rhs[0:8, :] = dmxj_ref[...]
        rhs[8:9, :] = dmcj_ref[...]
        gw_ref[...] = _dot_tn(lhs[...].astype(BF16), rhs[...].astype(BF16))
        gb_ref[...] = jnp.sum(dmx_ref[...], axis=0, keepdims=True) + dmc_ref[...]
        dm8[...] = jnp.zeros_like(dm8)
        dm8[0:1, :] = dmc_ref[...]
        da = _dot_nt(dm8[...].astype(BF16), w_ref[...])
        gc_ref[...] = da[0:1, :] * da_c

    return pl.pallas_call(
        body, name="ada_bwd",
        out_shape=(jax.ShapeDtypeStruct((D, 768), F32), jax.ShapeDtypeStruct((1, 3 * D), F32),
                   jax.ShapeDtypeStruct((1, D), F32)),
        in_specs=[VMEM] * 7, out_specs=(VMEM,) * 3,
        scratch_shapes=[pltpu.VMEM((16, D), F32), pltpu.VMEM((16, 768), F32), pltpu.VMEM((8, 3 * D), F32)],
        compiler_params=_cp(vmem_mb=32),
    )(c_all, dmx_all_j, dmc_j, dmx_all, dmc, c_ctx, ada_w_full)


def _proj(x, mods, mrow, norm_g, w_full, nk, name):
    lx = x.shape[0]
    n = lx // T

    def body(x_ref, sh_ref, sc_ref, ng_ref, *rest):
        w_refs, hn_ref, z_refs = rest[:nk], rest[nk], rest[nk + 1:]
        xv = x_ref[...]
        r = lax.rsqrt(jnp.mean(xv * xv, axis=-1, keepdims=True) + NORM_EPS)
        hn = (xv * r) * ng_ref[...] * (1.0 + sc_ref[mrow:mrow + 1, :]) + sh_ref[mrow:mrow + 1, :]
        hb = hn.astype(BF16)
        hn_ref[...] = hb
        for k in range(nk):
            z_refs[k][...] = _dot(hb, w_refs[k][...])

    row = pl.BlockSpec((T, D), lambda i: (i, 0))
    in_specs = [row, pl.BlockSpec((8, D), lambda i: (0, 0)), pl.BlockSpec((8, D), lambda i: (0, 1)),
                pl.BlockSpec((1, D), lambda i: (0, 0))]
    in_specs += [pl.BlockSpec((D, D), lambda i, k=k: (0, k)) for k in range(nk)]
    out_shape = (jax.ShapeDtypeStruct((lx, D), BF16),) + tuple(jax.ShapeDtypeStruct((lx, D), F32) for _ in range(nk))
    return pl.pallas_call(
        body, name=name, grid=(n,), out_shape=out_shape, in_specs=in_specs, out_specs=(row,) * (nk + 1),
        compiler_params=_cp(1, vmem_mb=56),
    )(x, mods, mods, norm_g, *([w_full] * nk))


def _halo_specs(lx):
    last = lx // 8 - 1
    return [pl.BlockSpec((T, D), lambda i: (i, 0)),
            pl.BlockSpec((8, D), lambda i: (jnp.maximum(i * (T // 8) - 1, 0), 0)),
            pl.BlockSpec((8, D), lambda i: (jnp.minimum((i + 1) * (T // 8), last), 0))]


def _zhalo_specs(lx):
    last = lx // 8 - 1
    return [pl.BlockSpec((T * NH, HD), lambda i: (i, 0)),
            pl.BlockSpec((8 * NH, HD), lambda i: (jnp.maximum(i * (T // 8) - 1, 0), 0)),
            pl.BlockSpec((8 * NH, HD), lambda i: (jnp.minimum((i + 1) * (T // 8), last), 0))]


ZT = pl.BlockSpec((T * NH, HD), lambda i: (i, 0))
CONV_CHUNK = 32


def _lru_gates_fwd(xa, conv_wz, conv_bz, wcat, bcat, lamcat, name):
    lx = xa.shape[0]
    n = lx // T

    def body(xm, xp, xn, cw, cb, w_ref, b_ref, lam_ref, xaz_o, xcz_o, af_o, uf_o, ab_o, ub_o, pad):
        i = pl.program_id(0)
        pmask = jnp.where(i == 0, 0.0, 1.0)
        nmask = jnp.where(i == n - 1, 0.0, 1.0)
        for h in range(NH):
            cols = slice(HD * h, HD * h + HD)
            pad[_zrows(h, 8), :] = xp[:, cols] * pmask
            pad[pl.ds(8 * NH + h, T, stride=NH), :] = xm[:, cols]
            pad[pl.ds((T + 8) * NH + h, 8, stride=NH), :] = xn[:, cols] * nmask
        xaz_o[...] = pad[pl.ds(8 * NH, T * NH), :]

        def conv_chunk(ci, carry):
            base = pl.multiple_of(ci * (CONV_CHUNK * NH), CONV_CHUNK * NH)
            acc = None
            for k in range(4):
                sl = pad[pl.ds(base + (7 + k) * NH, CONV_CHUNK * NH), :].reshape(CONV_CHUNK, NH, HD)
                term = sl * cw[k][None]
                acc = term if acc is None else acc + term
            acc = acc + cb[...][None]
            xcz_o[pl.ds(base, CONV_CHUNK * NH), :] = acc.reshape(CONV_CHUNK * NH, HD)
            return carry
        lax.fori_loop(0, T // CONV_CHUNK, conv_chunk, 0)

        outs = ((af_o, uf_o), (ab_o, ub_o))
        for h in range(NH):
            xch = xcz_o[_zrows(h, T), :]
            pre = _dot(xch.astype(BF16), w_ref[h]) + b_ref[h:h + 1, :]
            for d in range(2):
                _, gi, _, _, a, mult = _lru_gate(pre, lam_ref[h:h + 1, :], d)
                outs[d][0][_zrows(h, T), :] = a
                outs[d][1][_zrows(h, T), :] = mult * gi * xch

    full = lambda shape: pl.BlockSpec(shape, lambda i: (0,) * len(shape))
    in_specs = _halo_specs(lx) + [full((4, NH, HD)), full((NH, HD)), full((NH, HD, 4 * HD)), full((NH, 4 * HD)),
                                  full((NH, 2 * HD))]
    zs = jax.ShapeDtypeStruct((lx * NH, HD), F32)
    return pl.pallas_call(
        body, name=name, grid=(n,), out_shape=(zs,) * 6, in_specs=in_specs, out_specs=(ZT,) * 6,
        scratch_shapes=[pltpu.VMEM(((T + 16) * NH, HD), F32)],
        compiler_params=_cp(1, vmem_mb=48),
    )(xa, xa, xa, conv_wz, conv_bz, wcat, bcat, lamcat)


def _scan(a_up, x_up, a_dn, x_dn, s_up, s_dn, post, name):
    lx = a_up.shape[0] // NH
    n = lx // T

    def body(au, xu, ad, xd, su0, sd0, *rest):
        if post:
            ou, od, fu, fd, carry = rest
        else:
            ou, od, pu, pd, fu, fd, carry = rest
        i = pl.program_id(0)

        @pl.when(i == 0)
        def _():
            carry[0] = su0[...]
            carry[1] = sd0[...]

        def step(k, s):
            su, sd = s
            ru = pl.ds(_mo(k * NH, NH), NH)
            rd = pl.ds(_mo((T - 1 - k) * NH, NH), NH)
            if post:
                vu = xu[ru, :] + su
                vd = xd[rd, :] + sd
                ou[ru, :] = vu
                od[rd, :] = vd
                return au[ru, :] * vu, ad[rd, :] * vd
            pu[ru, :] = su
            pd[rd, :] = sd
            vu = au[ru, :] * su + xu[ru, :]
            vd = ad[rd, :] * sd + xd[rd, :]
            ou[ru, :] = vu
            od[rd, :] = vd
            return vu, vd

        su, sd = lax.fori_loop(0, T, step, (carry[0], carry[1]), unroll=8)
        carry[0] = su
        carry[1] = sd
        fu[...] = su
        fd[...] = sd

    up = pl.BlockSpec((T * NH, HD), lambda i: (i, 0))
    dn = pl.BlockSpec((T * NH, HD), lambda i: (n - 1 - i, 0))
    st = pl.BlockSpec((NH, HD), lambda i: (0, 0))
    zs = jax.ShapeDtypeStruct((lx * NH, HD), F32)
    ss = jax.ShapeDtypeStruct((NH, HD), F32)
    if post:
        out_shape, out_specs = (zs, zs, ss, ss), (up, dn, st, st)
    else:
        out_shape, out_specs = (zs, zs, zs, zs, ss, ss), (up, dn, up, dn, st, st)
    return pl.pallas_call(
        body, name=name, grid=(n,), out_shape=out_shape, in_specs=[up, up, dn, dn, st, st], out_specs=out_specs,
        scratch_shapes=[pltpu.VMEM((2, NH, HD), F32)],
        compiler_params=_cp(1, vmem_mb=48),
    )(a_up, x_up, a_dn, x_dn, s_up, s_dn)


def _sgu_parts(u, v, lng, lnb, w_ref, bt_ref, mixed_s):
    ug, dug = _gelu_and_grad(u)
    vg, dvg = _gelu_and_grad(v)
    mu = jnp.mean(vg, axis=-1, keepdims=True)
    vc = vg - mu
    rstd = lax.rsqrt(jnp.mean(vc * vc, axis=-1, keepdims=True) + LN_EPS)
    vh = vc * rstd
    vn = (vh * lng + lnb).astype(BF16)
    for g in range(NH):
        cols = slice(HD * g, HD * g + HD)
        mixed_s[:, cols] = _dot(w_ref[g], vn[:, cols]) + bt_ref[:, g:g + 1]
    return ug, dug, dvg, rstd, vh, vn


def _sgu_fwd(u, v, ln_g, ln_b, sgu_w, sgu_bt):
    lx = u.shape[0]
    n = lx // HD

    def body(u_ref, v_ref, g_ref, b_ref, w_ref, bt_ref, y_ref, mixed_s):
        ug, _, _, _, _, _ = _sgu_parts(u_ref[...], v_ref[...], g_ref[...], b_ref[...], w_ref, bt_ref, mixed_s)
        y_ref[...] = ug * mixed_s[...]

    row = pl.BlockSpec((HD, D), lambda i: (i, 0))
    vec = pl.BlockSpec((1, D), lambda i: (0, 0))
    return pl.pallas_call(
        body, name="sgu_fwd", grid=(n,), out_shape=jax.ShapeDtypeStruct((lx, D), F32),
        in_specs=[row, row, vec, vec, pl.BlockSpec((NH, HD, HD), lambda i: (0, 0, 0)),
                  pl.BlockSpec((HD, NH), lambda i: (0, 0))],
        out_specs=row, scratch_shapes=[pltpu.VMEM((HD, D), F32)],
        compiler_params=_cp(1),
    )(u, v, ln_g, ln_b, sgu_w, sgu_bt)


def _sgu_bwd(u, v, dys, ln_g, ln_b, sgu_w, sgu_bt):
    lx = u.shape[0]
    n = lx // HD

    def body(u_ref, v_ref, dy_ref, g_ref, b_ref, w_ref, bt_ref, du_ref, dv_ref, dw_ref, db_ref, dg_ref, dbl_ref,
             mixed_s, dvn_s):
        i = pl.program_id(0)

        @pl.when(i == 0)
        def _():
            dw_ref[...] = jnp.zeros_like(dw_ref)
            db_ref[...] = jnp.zeros_like(db_ref)
            dg_ref[...] = jnp.zeros_like(dg_ref)
            dbl_ref[...] = jnp.zeros_like(dbl_ref)

        lng = g_ref[...]
        ug, dug, dvg, rstd, vh, vn = _sgu_parts(u_ref[...], v_ref[...], lng, b_ref[...], w_ref, bt_ref, mixed_s)
        dys_v = dy_ref[...]
        du_ref[...] = (dys_v * mixed_s[...] * dug).astype(BF16)
        dmix = dys_v * ug
        ones = jnp.ones((8, HD), BF16)
        for g in range(NH):
            cols = slice(HD * g, HD * g + HD)
            dm = dmix[:, cols]
            hi = dm.astype(BF16)
            lo = (dm - hi.astype(F32)).astype(BF16)
            dw_ref[g] += _dot_nt(hi, vn[:, cols])
            db_ref[g:g + 1, :] += (_dot_nt(ones, hi) + _dot_nt(ones, lo))[0:1, :]
            dvn_s[:, cols] = _dot_tn(w_ref[g], hi)
        dvn = dvn_s[...]
        dg_ref[...] += jnp.sum(dvn * vh, axis=0, keepdims=True)
        dbl_ref[...] += jnp.sum(dvn, axis=0, keepdims=True)
        dvh = dvn * lng
        dvg_in = rstd * (dvh - jnp.mean(dvh, axis=-1, keepdims=True)
                         - vh * jnp.mean(dvh * vh, axis=-1, keepdims=True))
        dv_ref[...] = (dvg_in * dvg).astype(BF16)

    row = pl.BlockSpec((HD, D), lambda i: (i, 0))
    vec = pl.BlockSpec((1, D), lambda i: (0, 0))
    wsp = pl.BlockSpec((NH, HD, HD), lambda i: (0, 0, 0))
    bsp = pl.BlockSpec((NH, HD), lambda i: (0, 0))
    return pl.pallas_call(
        body, name="sgu_bwd", grid=(n,),
        out_shape=(jax.ShapeDtypeStruct((lx, D), BF16), jax.ShapeDtypeStruct((lx, D), BF16),
                   jax.ShapeDtypeStruct((NH, HD, HD), F32), jax.ShapeDtypeStruct((NH, HD), F32),
                   jax.ShapeDtypeStruct((1, D), F32), jax.ShapeDtypeStruct((1, D), F32)),
        in_specs=[row, row, row, vec, vec, wsp, pl.BlockSpec((HD, NH), lambda i: (0, 0))],
        out_specs=(row, row, wsp, bsp, vec, vec),
        scratch_shapes=[pltpu.VMEM((HD, D), F32), pltpu.VMEM((HD, D), F32)],
        compiler_params=_cp(1),
    )(u, v, dys, ln_g, ln_b, sgu_w, sgu_bt)


def _out_fwd_bwd(hf_z, hb_z, ga, gb, ys, x, tgt, mods, final_g, w_out_full):
    lx = x.shape[0]
    n = lx // T

    def body(hf_ref, hb_ref, ga_ref, gb_ref, ys_ref, x_ref, t_ref, gx_ref, fg_ref, w_ref,
             loss_ref, dfg_ref, dgx_ref, dxn_ref, y_ref, do_ref, dga_ref, dgb_ref, dyl_ref, dys_ref, yl_s):
        i = pl.program_id(0)

        @pl.when(i == 0)
        def _():
            loss_ref[...] = jnp.zeros_like(loss_ref)
            dfg_ref[...] = jnp.zeros_like(dfg_ref)
            dgx_ref[...] = jnp.zeros_like(dgx_ref)

        for h in range(NH):
            yl_s[:, HD * h:HD * h + HD] = hf_ref[_zrows(h, T), :] + hb_ref[_zrows(h, T), :]
        yl = yl_s[...]
        gav = ga_ref[...]
        gbv = gb_ref[...]
        sa, dsa = _silu_and_grad(gav)
        sb, dsb = _silu_and_grad(gbv)
        ysv = ys_ref[...]
        y_ref[:, 0:D] = (yl * sa).astype(BF16)
        y_ref[:, D:2 * D] = (ysv * sb).astype(BF16)
        o = _dot(y_ref[...], w_ref[...])
        gx = gx_ref[0:1, :]
        xnew = x_ref[...] + gx * o
        r2 = lax.rsqrt(jnp.mean(xnew * xnew, axis=-1, keepdims=True) + NORM_EPS)
        xh = xnew * r2
        fg = fg_ref[...]
        err = xh * fg - t_ref[...]
        loss_ref[...] += 0.5 * jnp.sum(jnp.mean(err * err, axis=-1, keepdims=True), axis=0, keepdims=True)
        dout = err * (1.0 / D)
        dfg_ref[...] += jnp.sum(dout * xh, axis=0, keepdims=True)
        dxh = dout * fg
        dxn = r2 * (dxh - xh * jnp.mean(dxh * xh, axis=-1, keepdims=True))
        dxn_ref[...] = dxn
        dgx_ref[...] += jnp.sum(dxn * o, axis=0, keepdims=True)
        do = (dxn * gx).astype(BF16)
        do_ref[...] = do
        dy = _dot_nt(do, w_ref[...])
        dy1 = dy[:, 0:D]
        dy2 = dy[:, D:2 * D]
        dga_ref[...] = (dy1 * yl * dsa).astype(BF16)
        dgb_ref[...] = (dy2 * ysv * dsb).astype(BF16)
        dys_ref[...] = dy2 * sb
        yl_s[...] = dy1 * sa
        for h in range(NH):
            dyl_ref[_zrows(h, T), :] = yl_s[:, HD * h:HD * h + HD]

    row = pl.BlockSpec((T, D), lambda i: (i, 0))
    vec = pl.BlockSpec((1, D), lambda i: (0, 0))
    in_specs = [ZT, ZT, row, row, row, row, row, pl.BlockSpec((8, D), lambda i: (0, 2)), vec,
                pl.BlockSpec((2 * D, D), lambda i: (0, 0))]
    out_shape = (jax.ShapeDtypeStruct((1, 1), F32), jax.ShapeDtypeStruct((1, D), F32), jax.ShapeDtypeStruct((1, D), F32),
                 jax.ShapeDtypeStruct((lx, D), F32), jax.ShapeDtypeStruct((lx, 2 * D), BF16),
                 jax.ShapeDtypeStruct((lx, D), BF16), jax.ShapeDtypeStruct((lx, D), BF16),
                 jax.ShapeDtypeStruct((lx, D), BF16), jax.ShapeDtypeStruct((lx * NH, HD), F32),
                 jax.ShapeDtypeStruct((lx, D), F32))
    out_specs = (pl.BlockSpec((1, 1), lambda i: (0, 0)), vec, vec, row, pl.BlockSpec((T, 2 * D), lambda i: (i, 0)),
                 row, row, row, ZT, row)
    return pl.pallas_call(
        body, name="out_fwd_bwd", grid=(n,), out_shape=out_shape, in_specs=in_specs, out_specs=out_specs,
        scratch_shapes=[pltpu.VMEM((T, D), F32)],
        compiler_params=_cp(1, vmem_mb=56),
    )(hf_z, hb_z, ga, gb, ys, x, tgt, mods, final_g, w_out_full)


def _lru_gates_bwd(xc_z, lf_z, lb_z, pf_z, pb_z, wcat, bcat, lamcat, dw0, db0, dl0, name):
    lx = xc_z.shape[0] // NH
    n = lx // T

    def body(xc_ref, lf_ref, lb_ref, pf_ref, pb_ref, w_ref, b_ref, lam_ref, dw0_ref, db0_ref, dl0_ref,
             dxc_ref, dw_ref, db_ref, dl_ref, dpre_s):
        i = pl.program_id(0)

        @pl.when(i == 0)
        def _():
            dw_ref[...] = dw0_ref[...]
            db_ref[...] = db0_ref[...]
            dl_ref[...] = dl0_ref[...]

        lam_refs = (lf_ref, lb_ref)
        prev_refs = (pf_ref, pb_ref)
        for h in range(NH):
            xch = xc_ref[_zrows(h, T), :]
            xcb = xch.astype(BF16)
            pre = _dot(xcb, w_ref[h]) + b_ref[h:h + 1, :]
            dxc = jnp.zeros((T, HD), F32)
            for d in range(2):
                r, gi, lam, sp, a, mult = _lru_gate(pre, lam_ref[h:h + 1, :], d)
                du = lam_refs[d][_zrows(h, T), :]
                da = du * prev_refs[d][_zrows(h, T), :]
                dgi = du * mult * xch
                dxc = dxc + du * mult * gi
                dmult = du * gi * xch
                dla = da * a - dmult * (a * a) / mult
                dr = dla * ((-LRU_C) * sp)
                dsp = jnp.sum(dla * ((-LRU_C) * r), axis=0, keepdims=True)
                dl_ref[h:h + 1, HD * d:HD * d + HD] += dsp * (-_sigmoid(-lam))
                dpre_s[:, 256 * d:256 * d + HD] = dr * r * (1.0 - r)
                dpre_s[:, 256 * d + HD:256 * d + 2 * HD] = dgi * gi * (1.0 - gi)
            dpre = dpre_s[...]
            dpb = dpre.astype(BF16)
            dw_ref[h] += _dot_tn(xcb, dpb)
            db_ref[h:h + 1, :] += jnp.sum(dpre, axis=0, keepdims=True)
            dxc_ref[_zrows(h, T), :] = dxc + _dot_nt(dpb, w_ref[h])

    full = lambda shape: pl.BlockSpec(shape, lambda i: (0,) * len(shape))
    wsp, bsp, lsp = full((NH, HD, 4 * HD)), full((NH, 4 * HD)), full((NH, 2 * HD))
    return pl.pallas_call(
        body, name=name, grid=(n,),
        out_shape=(jax.ShapeDtypeStruct((lx * NH, HD), F32), jax.ShapeDtypeStruct((NH, HD, 4 * HD), F32),
                   jax.ShapeDtypeStruct((NH, 4 * HD), F32), jax.ShapeDtypeStruct((NH, 2 * HD), F32)),
        in_specs=[ZT] * 5 + [wsp, bsp, lsp, wsp, bsp, lsp], out_specs=(ZT, wsp, bsp, lsp),
        scratch_shapes=[pltpu.VMEM((T, 4 * HD), F32)],
        compiler_params=_cp(1, vmem_mb=48),
    )(xc_z, lf_z, lb_z, pf_z, pb_z, wcat, bcat, lamcat, dw0, db0, dl0)


def _conv_bwd(dxc_z, xa_z, conv_wz, dcw0, dcb0, name):
    lx = dxc_z.shape[0] // NH
    n = lx // T

    def body(dm, dp, dn, xa_ref, cw, dcw0_ref, dcb0_ref, dxa_ref, dcw_ref, dcb_ref, pad, dxa_s):
        i = pl.program_id(0)

        @pl.when(i == 0)
        def _():
            dcw_ref[...] = dcw0_ref[...]
            dcb_ref[...] = dcb0_ref[...]

        pmask = jnp.where(i == 0, 0.0, 1.0)
        nmask = jnp.where(i == n - 1, 0.0, 1.0)
        pad[pl.ds(0, 8 * NH), :] = dp[...] * pmask
        pad[pl.ds(8 * NH, T * NH), :] = dm[...]
        pad[pl.ds((T + 8) * NH, 8 * NH), :] = dn[...] * nmask

        def chunk(ci, carry):
            base = pl.multiple_of(ci * (CONV_CHUNK * NH), CONV_CHUNK * NH)
            xav = xa_ref[pl.ds(base, CONV_CHUNK * NH), :].reshape(CONV_CHUNK, NH, HD)
            acc = None
            for k in range(4):
                sl = pad[pl.ds(base + (9 - k) * NH, CONV_CHUNK * NH), :].reshape(CONV_CHUNK, NH, HD)
                term = sl * cw[k][None]
                acc = term if acc is None else acc + term
                dcw_ref[k] += jnp.sum(sl * xav, axis=0)
                if k == 1:
                    dcb_ref[...] += jnp.sum(sl, axis=0)
            dxa_s[pl.ds(base, CONV_CHUNK * NH), :] = acc.reshape(CONV_CHUNK * NH, HD)
            return carry
        lax.fori_loop(0, T // CONV_CHUNK, chunk, 0)
        for h in range(NH):
            dxa_ref[:, HD * h:HD * h + HD] = dxa_s[_zrows(h, T), :].astype(BF16)

    full = lambda shape: pl.BlockSpec(shape, lambda i: (0,) * len(shape))
    return pl.pallas_call(
        body, name=name, grid=(n,),
        out_shape=(jax.ShapeDtypeStruct((lx, D), BF16), jax.ShapeDtypeStruct((4, NH, HD), F32),
                   jax.ShapeDtypeStruct((NH, HD), F32)),
        in_specs=_zhalo_specs(lx) + [ZT, full((4, NH, HD)), full((4, NH, HD)), full((NH, HD))],
        out_specs=(pl.BlockSpec((T, D), lambda i: (i, 0)), full((4, NH, HD)), full((NH, HD))),
        scratch_shapes=[pltpu.VMEM(((T + 16) * NH, HD), F32), pltpu.VMEM((T * NH, HD), F32)],
        compiler_params=_cp(1, vmem_mb=48),
    )(dxc_z, dxc_z, dxc_z, xa_z, conv_wz, dcw0, dcb0)


def _proj_bwd(dzs, x, dxn, mods, mrow, norm_g, w_full, dng0, name):
    lx = x.shape[0]
    n = lx // T
    nk = len(dzs)
    has_x = dxn is not None

    def body(*refs):
        dz_refs = refs[:nk]
        w_refs = refs[nk:2 * nk]
        x_ref, sc_ref, ng_ref, dng0_ref = refs[2 * nk:2 * nk + 4]
        rest = refs[2 * nk + 4:]
        if has_x:
            dxn_ref, gx_ref, dng_ref, dsc_ref, dsh_ref = rest
        else:
            dng_ref, dsc_ref, dsh_ref = rest
        i = pl.program_id(0)

        @pl.when(i == 0)
        def _():
            dng_ref[...] = dng0_ref[...]
            dsc_ref[...] = jnp.zeros_like(dsc_ref)
            dsh_ref[...] = jnp.zeros_like(dsh_ref)

        dhn = _dot_nt(dz_refs[0][...], w_refs[0][...])
        for k in range(1, nk):
            dhn = dhn + _dot_nt(dz_refs[k][...], w_refs[k][...])
        xv = x_ref[...]
        r = lax.rsqrt(jnp.mean(xv * xv, axis=-1, keepdims=True) + NORM_EPS)
        xn = xv * r
        ng = ng_ref[...]
        sc1 = 1.0 + sc_ref[mrow:mrow + 1, :]
        t = dhn * xn
        dng_ref[...] += jnp.sum(t * sc1, axis=0, keepdims=True)
        dsc_ref[...] += jnp.sum(t * ng, axis=0, keepdims=True)
        dsh_ref[...] += jnp.sum(dhn, axis=0, keepdims=True)
        if has_x:
            dxh = dhn * (ng * sc1)
            gx_ref[...] = dxn_ref[...] + r * (dxh - xn * jnp.mean(dxh * xn, axis=-1, keepdims=True))

    row = pl.BlockSpec((T, D), lambda i: (i, 0))
    vec = pl.BlockSpec((1, D), lambda i: (0, 0))
    in_specs = [row] * nk + [pl.BlockSpec((D, D), lambda i, k=k: (0, k)) for k in range(nk)]
    in_specs += [row, pl.BlockSpec((8, D), lambda i: (0, 1)), vec, vec]
    args = list(dzs) + [w_full] * nk + [x, mods, norm_g, dng0]
    vs = jax.ShapeDtypeStruct((1, D), F32)
    out_shape, out_specs = (vs, vs, vs), (vec, vec, vec)
    if has_x:
        in_specs.append(row)
        args.append(dxn)
        out_shape = (jax.ShapeDtypeStruct((lx, D), F32),) + out_shape
        out_specs = (row,) + out_specs
    return pl.pallas_call(
        body, name=name, grid=(n,), out_shape=out_shape, in_specs=in_specs, out_specs=out_specs,
        compiler_params=_cp(1, vmem_mb=56),
    )(*args)


def _tn_matmul(a, bs, extra, name):
    lx, m = a.shape
    tm = min(lx, 1024)
    n = lx // tm
    widths = [b.shape[1] for b in bs]
    nb = len(bs)

    def body(a_ref, *rest):
        b_refs = rest[:nb]
        rest = rest[nb:]
        if extra is not None:
            ea_ref, eb_ref = rest[:2]
            rest = rest[2:]
        out_ref, acc = rest
        i = pl.program_id(0)
        av = a_ref[...]
        off = 0
        for k in range(nb):
            cols = slice(off, off + widths[k])
            part = _dot_tn(av, b_refs[k][...])

            @pl.when(i == 0)
            def _():
                acc[:, cols] = part

            @pl.when(i > 0)
            def _():
                acc[:, cols] += part
            off += widths[k]

        @pl.when(i == n - 1)
        def _():
            if extra is not None:
                acc[:, 0:widths[0]] += _dot_tn(ea_ref[...], eb_ref[...])
            pltpu.sync_copy(acc, out_ref)

    in_specs = [pl.BlockSpec((tm, m), lambda i: (i, 0))] + [pl.BlockSpec((tm, w), lambda i: (i, 0)) for w in widths]
    args = [a] + list(bs)
    if extra is not None:
        in_specs += [VMEM, VMEM]
        args += list(extra)
    return pl.pallas_call(
        body, name=name, grid=(n,), out_shape=jax.ShapeDtypeStruct((m, sum(widths)), F32),
        in_specs=in_specs, out_specs=ANY, scratch_shapes=[pltpu.VMEM((m, sum(widths)), F32)],
        compiler_params=_cp(1, vmem_mb=56),
    )(*args)


def _adam_math(w, g, m, v):
    m = ADAM_B1 * m + (1.0 - ADAM_B1) * g
    v = ADAM_B2 * v + (1.0 - ADAM_B2) * (g * g)
    m_hat = m / (1.0 - ADAM_B1 ** ADAM_STEP)
    v_hat = v / (1.0 - ADAM_B2 ** ADAM_STEP)
    delta = -ADAM_LR * (m_hat / (jnp.sqrt(v_hat) + ADAM_EPS) + ADAM_WD * w)
    return delta, m, v


def _adam_big(w, g, m, v, name):
    rows, cols = w.shape
    tr = 256

    def body(w_ref, g_ref, m_ref, v_ref, d_o, m_o, v_o):
        d, mm, vv = _adam_math(w_ref[...], g_ref[...], m_ref[...], v_ref[...])
        d_o[...] = d
        m_o[...] = mm
        v_o[...] = vv

    blk = pl.BlockSpec((tr, cols), lambda i: (i, 0))
    s = jax.ShapeDtypeStruct((rows, cols), F32)
    return pl.pallas_call(
        body, name=name, grid=(rows // tr,), out_shape=(s, s, s), in_specs=[blk] * 4, out_specs=(blk,) * 3,
        compiler_params=_cp(1, vmem_mb=48),
    )(w, g, m, v)


def _adam_small(items):
    ni = len(items)

    def body(*refs):
        ins, outs = refs[:4 * ni], refs[4 * ni:]
        for k in range(ni):
            w_ref, g_ref, m_ref, v_ref = ins[4 * k:4 * k + 4]
            d, mm, vv = _adam_math(w_ref[...], g_ref[...], m_ref[...], v_ref[...])
            outs[3 * k][...] = d
            outs[3 * k + 1][...] = mm
            outs[3 * k + 2][...] = vv

    flat = [a for it in items for a in it]
    out_shape = tuple(jax.ShapeDtypeStruct(it[0].shape, F32) for it in items for _ in range(3))
    res = pl.pallas_call(
        body, name="adam_small", out_shape=out_shape, in_specs=[VMEM] * (4 * ni), out_specs=(VMEM,) * (3 * ni),
        compiler_params=_cp(vmem_mb=32),
    )(*flat)
    return [tuple(res[3 * k:3 * k + 3]) for k in range(ni)]


def kernel(x, c, ctx, c_ctx, ada_w, ada_b, norm_g, w_in, conv_w, conv_b, lru_wa, lru_ba, lru_wx, lru_bx, lru_lambda, sgu_ln_g, sgu_ln_b, sgu_w, sgu_b, w_out, final_g, loss_target, m_c_ctx, m_ada_w, m_ada_b, m_norm_g, m_w_in, m_conv_w, m_conv_b, m_lru_wa, m_lru_ba, m_lru_wx, m_lru_bx, m_lru_lambda, m_sgu_ln_g, m_sgu_ln_b, m_sgu_w, m_sgu_b, m_w_out, m_final_g, v_c_ctx, v_ada_w, v_ada_b, v_norm_g, v_w_in, v_conv_w, v_conv_b, v_lru_wa, v_lru_ba, v_lru_wx, v_lru_bx, v_lru_lambda, v_sgu_ln_g, v_sgu_ln_b, v_sgu_w, v_sgu_b, v_w_out, v_final_g):
    ix, iy, ic = lax.axis_index("x"), lax.axis_index("y"), lax.axis_index("c")
    chip = 2 * ix + iy
    dev = 2 * chip + ic
    lx = x.shape[1]
    lc = ctx.shape[1]

    smalls = jnp.concatenate([conv_w[0], lru_lambda[0], jnp.zeros((10, 256), F32)], axis=0)
    c_ctx2 = c_ctx.reshape(1, D)
    ada_b_j = lax.dynamic_slice(ada_b, (0, 768 * chip), (1, 768))
    mods, c_slots, sm_all, w_in_full, w_out_bf, ada_bf = _gather_in(c, c_ctx2, ada_w[0], ada_b_j, w_in[0], w_out[0],
                                                                     smalls)
    wo_land = lax.dynamic_update_slice(jnp.zeros((2048, D), BF16), w_out_bf, (512 * chip, 0))
    ada_land = lax.dynamic_update_slice(jnp.zeros((D, 3 * D), BF16), ada_bf, (0, 768 * chip))
    wo_ss, wo_rs, ada_ss, ada_rs, wo_land, ada_land, token = _late_gather_start(wo_land, ada_land)
    mods = mods + token[0:1, 0:1]
    sm3 = sm_all.reshape(NCHIP, 16, 256)
    conv_w_full = sm3[:, 0:4, :].transpose(1, 0, 2).reshape(4, D)
    lam_full = sm3[:, 4:6, :].transpose(1, 0, 2).reshape(2, D)
    conv_wz = conv_w_full.reshape(4, NH, HD)
    conv_bz = conv_b.reshape(NH, HD)
    lamcat = lam_full.reshape(2, NH, HD).transpose(1, 0, 2).reshape(NH, 2 * HD)
    wa, wx, ba, bx = lru_wa[0], lru_wx[0], lru_ba[0], lru_bx[0]
    wcat = jnp.concatenate([wa[0], wx[0], wa[1], wx[1]], axis=-1).astype(BF16)
    bcat = jnp.concatenate([ba[0], bx[0], ba[1], bx[1]], axis=-1)
    sgu_wb = sgu_w[0].astype(BF16)
    sgu_bt = sgu_b[0].T
    final_g2 = final_g.reshape(1, D)

    zero_s = jnp.zeros((NH, HD), F32)
    hn_c, xa_c = _proj(ctx[0], mods, 1, norm_g, w_in_full, 1, "proj_ctx")
    xaz_c, xcz_c, af_c, uf_c, ab_c, ub_c = _lru_gates_fwd(xa_c, conv_wz, conv_bz, wcat, bcat, lamcat, "lru_gates_ctx")
    _, _, pf_c, pb_c, hf0, hb0 = _scan(af_c, uf_c, ab_c, ub_c, zero_s, zero_s, False, "scan_ctx")

    hn, xa, ga, u, v, gb = _proj(x[0], mods, 0, norm_g, w_in_full, 5, "proj")
    xaz, xcz, af, uf, ab, ub = _lru_gates_fwd(xa, conv_wz, conv_bz, wcat, bcat, lamcat, "lru_gates")
    hf, hb, pf, pb, _, _ = _scan(af, uf, ab, ub, hf0, hb0, False, "scan")
    ys = _sgu_fwd(u, v, sgu_ln_g, sgu_ln_b, sgu_wb, sgu_bt)

    w_out_full = _late_gather_wait(wo_land, wo_ss, wo_rs, "w_out", ys, "late_gather_wait_w_out")
    (loss_part, dfg, dgx, dxn, y, do, dga, dgb, dyl_z, dys) = _out_fwd_bwd(
        hf, hb, ga, gb, ys, x[0], loss_target[0], mods, final_g2, w_out_full)
    g_w_out_part = _tn_matmul(y, [do], None, "grad_w_out")

    du, dv, d_sgu_w, d_sgu_b, d_ln_g, d_ln_b = _sgu_bwd(u, v, dys, sgu_ln_g, sgu_ln_b, sgu_wb, sgu_bt)
    lb, lf, dh0b, dh0f = _scan(ab, dyl_z, af, dyl_z, zero_s, zero_s, True, "scan_adj")
    zw = jnp.zeros((NH, HD, 4 * HD), F32)
    zb = jnp.zeros((NH, 4 * HD), F32)
    zl = jnp.zeros((NH, 2 * HD), F32)
    dxc_z, dwc, dbc, dlc = _lru_gates_bwd(xcz, lf, lb, pf, pb, wcat, bcat, lamcat, zw, zb, zl, "lru_gates_bwd")
    dxa, dcw, dcb = _conv_bwd(dxc_z, xaz, conv_wz, jnp.zeros((4, NH, HD), F32), zero_s, "conv_bwd")

    zc = jnp.zeros((lc * NH, HD), F32)
    dhf_c = lax.dynamic_update_slice(zc, dh0f, ((lc - 1) * NH, 0))
    dhb_c = lax.dynamic_update_slice(zc, dh0b, (0, 0))
    lb_c, lf_c, _, _ = _scan(ab_c, dhb_c, af_c, dhf_c, zero_s, zero_s, True, "scan_adj_ctx")
    dxc_zc, dwc, dbc, dlc = _lru_gates_bwd(xcz_c, lf_c, lb_c, pf_c, pb_c, wcat, bcat, lamcat, dwc, dbc, dlc,
                                           "lru_gates_bwd_ctx")
    dxa_c, dcw, dcb = _conv_bwd(dxc_zc, xaz_c, conv_wz, dcw, dcb, "conv_bwd_ctx")

    dzs = [dxa, dga, du, dv, dgb]
    grad_x, dng, dsc_x, dsh_x = _proj_bwd(dzs, x[0], dxn, mods, 0, norm_g, w_in_full, jnp.zeros((1, D), F32), "proj_bwd")
    dng, dsc_c, dsh_c = _proj_bwd([dxa_c], ctx[0], None, mods, 1, norm_g, w_in_full, dng, "proj_bwd_ctx")
    g_w_in_part = _tn_matmul(hn, dzs, (hn_c, dxa_c), "grad_w_in")

    dmx = jnp.concatenate([dsh_x, dsc_x, dgx], axis=0)
    dmc = jnp.concatenate([dsh_c, dsc_c, jnp.zeros((1, D), F32)], axis=0)
    slot = jnp.concatenate([dmx, jnp.zeros((1, D), F32)], axis=0)
    slots = lax.dynamic_update_slice(jnp.zeros((32, D), F32), slot, (4 * dev, 0))
    vecs = jnp.concatenate([dfg, dng, dcb.reshape(1, D), d_ln_g, d_ln_b, dcw.reshape(4, D), dmc,
                            jnp.zeros((4, D), F32), slots], axis=0)
    d_sgu_w4 = d_sgu_w.reshape(4, 256, HD).transpose(1, 0, 2).reshape(256, 4 * HD)
    pad8 = lambda a: jnp.pad(a, ((0, 8 - a.shape[0]), (0, 4 * HD - a.shape[1])))
    pack = jnp.concatenate([dwc.reshape(NH * HD, 4 * HD), pad8(dbc), pad8(dlc), d_sgu_w4, pad8(d_sgu_b),
                            vecs.reshape(96, 4 * HD), jnp.zeros((8, 4 * HD), F32)], axis=0)
    g_w_in, g_w_out, tot = _reduce_all(g_w_in_part, g_w_out_part, pack)

    g_wc = tot[0:1024].reshape(NH, HD, 4 * HD)
    g_bc = tot[1024:1032]
    g_lc = tot[1032:1040, 0:2 * HD]
    g_sgu_w = tot[1040:1296].reshape(256, 4, HD).transpose(1, 0, 2).reshape(NH, HD, HD)
    g_sgu_b = tot[1296:1304, 0:HD]
    tv = tot[1304:1400].reshape(48, D)
    g_final_g, g_norm_g, g_conv_b, g_ln_g, g_ln_b = tv[0:1], tv[1:2], tv[2:3], tv[3:4], tv[4:5]
    g_conv_w_full = tv[5:9]
    dmc_tot = tv[9:12].reshape(1, 3 * D)
    slots_all = tv[16:48].reshape(8, 4, D)
    dmx_all = slots_all[:, 0:3, :].reshape(8, 3 * D)
    c_all = c_slots.reshape(8, 8, D)[:, 0, :]
    g_lru_wa = jnp.stack([g_wc[:, :, 0:HD], g_wc[:, :, 2 * HD:3 * HD]])
    g_lru_wx = jnp.stack([g_wc[:, :, HD:2 * HD], g_wc[:, :, 3 * HD:4 * HD]])
    g_lru_ba = jnp.stack([g_bc[:, 0:HD], g_bc[:, 2 * HD:3 * HD]])
    g_lru_bx = jnp.stack([g_bc[:, HD:2 * HD], g_bc[:, 3 * HD:4 * HD]])
    g_lam_full = jnp.stack([g_lc[:, 0:HD], g_lc[:, HD:2 * HD]]).reshape(2, D)
    g_conv_w = lax.dynamic_slice(g_conv_w_full, (0, 256 * chip), (4, 256))
    g_lam = lax.dynamic_slice(g_lam_full, (0, 256 * chip), (2, 256))
    dmx_all_j = lax.dynamic_slice(dmx_all, (0, 768 * chip), (8, 768))
    dmc_j = lax.dynamic_slice(dmc_tot, (0, 768 * chip), (1, 768))
    ada_full = _late_gather_wait(ada_land, ada_ss, ada_rs, "ada_w", tot, "late_gather_wait_ada_w")
    g_ada_w, g_ada_b, g_c_ctx = _ada_bwd(c_all, dmx_all_j, dmc_j, dmx_all, dmc_tot, c_ctx2, ada_full)

    big = {
        "ada_w": _adam_big(ada_w[0], g_ada_w, m_ada_w[0], v_ada_w[0], "adam_ada_w"),
        "w_in": _adam_big(w_in[0], g_w_in, m_w_in[0], v_w_in[0], "adam_w_in"),
        "w_out": _adam_big(w_out[0], g_w_out, m_w_out[0], v_w_out[0], "adam_w_out"),
    }
    small_in = {
        "c_ctx": (c_ctx, g_c_ctx, m_c_ctx, v_c_ctx, (1, D)),
        "ada_b": (ada_b, g_ada_b, m_ada_b, v_ada_b, (1, 3 * D)),
        "norm_g": (norm_g, g_norm_g, m_norm_g, v_norm_g, (1, D)),
        "conv_w": (conv_w, g_conv_w, m_conv_w, v_conv_w, (4, 256)),
        "conv_b": (conv_b, g_conv_b, m_conv_b, v_conv_b, (1, D)),
        "lru_wa": (lru_wa, g_lru_wa, m_lru_wa, v_lru_wa, (2 * NH * HD, HD)),
        "lru_ba": (lru_ba, g_lru_ba, m_lru_ba, v_lru_ba, (2 * NH, HD)),
        "lru_wx": (lru_wx, g_lru_wx, m_lru_wx, v_lru_wx, (2 * NH * HD, HD)),
        "lru_bx": (lru_bx, g_lru_bx, m_lru_bx, v_lru_bx, (2 * NH, HD)),
        "lru_lambda": (lru_lambda, g_lam, m_lru_lambda, v_lru_lambda, (2, 256)),
        "sgu_ln_g": (sgu_ln_g, g_ln_g, m_sgu_ln_g, v_sgu_ln_g, (1, D)),
        "sgu_ln_b": (sgu_ln_b, g_ln_b, m_sgu_ln_b, v_sgu_ln_b, (1, D)),
        "sgu_w": (sgu_w, g_sgu_w, m_sgu_w, v_sgu_w, (NH * HD, HD)),
        "sgu_b": (sgu_b, g_sgu_b, m_sgu_b, v_sgu_b, (NH, HD)),
        "final_g": (final_g, g_final_g, m_final_g, v_final_g, (1, D)),
    }
    names_small = list(small_in)
    res_small = _adam_small([tuple(a.reshape(small_in[k][4]) for a in small_in[k][:4]) for k in names_small])
    full_shapes = {"ada_w": ada_w.shape, "w_in": w_in.shape, "w_out": w_out.shape}
    grads, deltas, new_m, new_v = {}, {}, {}, {}
    for k in ("ada_w", "w_in", "w_out"):
        g = {"ada_w": g_ada_w, "w_in": g_w_in, "w_out": g_w_out}[k]
        grads[k] = g.reshape(full_shapes[k])
        deltas[k], new_m[k], new_v[k] = (a.reshape(full_shapes[k]) for a in big[k])
    for k, res in zip(names_small, res_small):
        shape = small_in[k][0].shape
        grads[k] = small_in[k][1].reshape(shape)
        deltas[k], new_m[k], new_v[k] = (a.reshape(shape) for a in res)

    loss = lax.psum(loss_part[0, 0], ("x", "y", "c"))
    order = ["c_ctx", "ada_w", "ada_b", "norm_g", "w_in", "conv_w", "conv_b", "lru_wa", "lru_ba", "lru_wx", "lru_bx",
             "lru_lambda", "sgu_ln_g", "sgu_ln_b", "sgu_w", "sgu_b", "w_out", "final_g"]
    return (loss, grad_x.reshape(x.shape), *[grads[k] for k in order], *[deltas[k] for k in order],
            *[new_m[k] for k in order], *[new_v[k] for k in order])
```

```python
import functools

import jax
import jax.numpy as jnp
from jax import lax
from jax.experimental import pallas as pl
from jax.experimental.pallas import tpu as pltpu

F32 = jnp.float32
BF16 = jnp.bfloat16

D = 1024
NH = 8
HD = 128
NCHIP = 4
T = 256
NORM_EPS = 1e-6
LN_EPS = 1e-5
LRU_C = 8.0
ADAM_LR = 0.001
ADAM_B1 = 0.9
ADAM_B2 = 0.999
ADAM_EPS = 1e-08
ADAM_WD = 0.01
ADAM_STEP = 10

VMEM = pl.BlockSpec(memory_space=pltpu.VMEM)
ANY = pl.BlockSpec(memory_space=pl.ANY)
MESH = pl.DeviceIdType.MESH


def _cp(n_grid=0, vmem_mb=None):
    kw = {}
    if n_grid:
        kw["dimension_semantics"] = ("arbitrary",) * n_grid
    if vmem_mb:
        kw["vmem_limit_bytes"] = vmem_mb << 20
    return pltpu.CompilerParams(**kw)


def _sigmoid(x):
    return 1.0 / (1.0 + jnp.exp(-x))


def _silu_and_grad(x):
    s = _sigmoid(x)
    return x * s, s * (1.0 + x * (1.0 - s))


_GELU_K = 0.7978845608028654
_GELU_C = 0.044715


def _gelu_and_grad(x):
    x2 = x * x
    th = jnp.tanh(_GELU_K * (x + _GELU_C * x * x2))
    g = 0.5 * x * (1.0 + th)
    dg = 0.5 * (1.0 + th) + 0.5 * x * (1.0 - th * th) * (_GELU_K * (1.0 + 3.0 * _GELU_C * x2))
    return g, dg


def _softplus(x):
    return jnp.maximum(x, 0.0) + jnp.log1p(jnp.exp(-jnp.abs(x)))


def _lru_gate(pre, lam_row, d):
    r = _sigmoid(pre[:, 256 * d:256 * d + HD])
    gi = _sigmoid(pre[:, 256 * d + HD:256 * d + 2 * HD])
    lam = lam_row[:, HD * d:HD * d + HD]
    sp = _softplus(-lam)
    la = (-LRU_C) * r * sp
    a = jnp.exp(la)
    x2 = 2.0 * la
    m2 = jnp.where(x2 > -1e-3, -x2 * (1.0 + 0.5 * x2), 1.0 - a * a)
    mult = jnp.sqrt(m2)
    return r, gi, lam, sp, a, mult


def _dot(a, b):
    return jnp.dot(a, b, preferred_element_type=F32)


def _dot_tn(a, b):
    return lax.dot_general(a, b, (((0,), (0,)), ((), ())), preferred_element_type=F32)


def _dot_nt(a, b):
    return lax.dot_general(a, b, (((1,), (1,)), ((), ())), preferred_element_type=F32)


def _mo(v, m):
    return v if isinstance(v, int) else pl.multiple_of(v, m)


def _zrows(h, n):
    return pl.ds(h, n, stride=NH)


def _gather_in(c, c_ctx, ada_w, ada_b_j, w_in, w_out, smalls):
    specs = [
        ((64, 256), F32, lambda r, jj, cc: r.at[pl.ds(_mo(16 * jj + 8 * cc, 8), 8), :]),
        ((D, 5120), BF16, lambda r, jj, cc: r.at[pl.ds(_mo(512 * cc, 16), 512), pl.ds(_mo(1280 * jj, 128), 1280)]),
    ]
    halves = [lambda r, cc: r.at[pl.ds(_mo(8 * cc, 8), 8), :],
              lambda r, cc: r.at[pl.ds(_mo(512 * cc, 16), 512), :]]
    na = len(specs)
    n_sem = 6 * na + 10

    def body(c_ref, cc_ref, ada_ref, adab_ref, win_ref, wout_ref, sm_ref,
             mods_o, call_o, sm_o, win_o, wol_o, adal_o, s_win, s_ada, cslot, lhs, mbuf,
             send_sems, recv_sems, local_sems):
        x, y, c = lax.axis_index("x"), lax.axis_index("y"), lax.axis_index("c")
        j = 2 * x + y
        dev = 2 * j + c
        sib = (x, y, 1 - c)
        chips = [(1 - x, y), (x, 1 - y), (1 - x, 1 - y)]
        cj = [2 * cx + cy for cx, cy in chips]
        outs = [sm_o, win_o]
        srcs = [sm_ref, s_win]

        def copy(idx, src, dst, to):
            return pltpu.make_async_remote_copy(src_ref=src, dst_ref=dst, send_sem=send_sems.at[idx],
                                                recv_sem=recv_sems.at[idx], device_id=to, device_id_type=MESH)

        sends = []

        def start(cp):
            cp.start()
            sends.append(cp)

        cslot[...] = jnp.zeros_like(cslot)
        cslot[0:1, :] = c_ref[...]
        my_slot = pl.ds(_mo(8 * dev, 8), 8)
        others = [sib] + [(*chips[k], c) for k in range(3)] + [(*chips[k], 1 - c) for k in range(3)]
        other_dev = [dev + 1 - 2 * c] + [2 * cj[k] + c for k in range(3)] + [2 * cj[k] + 1 - c for k in range(3)]
        base = 6 * na
        for r in range(7):
            start(copy(base + r, cslot, call_o.at[my_slot, :], others[r]))
        call_o[my_slot, :] = cslot[...]

        s_win[...] = win_ref[...].astype(BF16)
        local = []
        for a in range(na):
            for cc in range(2):
                lc = pltpu.make_async_copy(halves[a](srcs[a], cc), specs[a][2](outs[a], j, cc), local_sems.at[2 * a + cc])
                lc.start()
                local.append(lc)
        for k in range(3):
            for a in range(na):
                start(copy(6 * a + k, halves[a](srcs[a], c), specs[a][2](outs[a], j, c), (*chips[k], c)))
        wol_o[pl.ds(_mo(512 * j, 16), 512), :] = wout_ref[...].astype(BF16)
        s_ada[...] = ada_ref[...].astype(BF16)
        lc = pltpu.make_async_copy(s_ada, adal_o.at[:, pl.ds(_mo(768 * j, 128), 768)], local_sems.at[2 * na])
        lc.start()
        local.append(lc)

        for r in range(7):
            slot = call_o.at[pl.ds(_mo(8 * other_dev[r], 8), 8), :]
            copy(base + r, slot, slot, sib).wait_recv()
        lhs[...] = jnp.zeros_like(lhs)
        for b in range(8):
            cv = call_o[8 * b:8 * b + 1, :]
            lhs[b:b + 1, :] = cv * _sigmoid(cv)
        cv = cc_ref[...]
        lhs[8:9, :] = cv * _sigmoid(cv)
        mbuf[j] = _dot(lhs[...].astype(BF16), s_ada[...]) + adab_ref[...]
        for k in range(3):
            start(copy(base + 7 + k, mbuf.at[j], mbuf.at[j], (*chips[k], c)))
        for k in range(3):
            copy(base + 7 + k, mbuf.at[cj[k]], mbuf.at[cj[k]], sib).wait_recv()
        mods_o[...] = jnp.zeros_like(mods_o)
        for jj in range(NCHIP):
            mods_o[0:1, 768 * jj:768 * jj + 768] = mbuf[jj, pl.ds(dev, 1), :]
            mods_o[1:2, 768 * jj:768 * jj + 768] = mbuf[jj, 8:9, :]

        for k in range(3):
            for a in range(na):
                reg = specs[a][2](outs[a], cj[k], c)
                copy(6 * a + k, reg, reg, sib).wait_recv()
                start(copy(6 * a + 3 + k, reg, reg, sib))
        for k in range(3):
            for a in range(na):
                reg = specs[a][2](outs[a], cj[k], 1 - c)
                copy(6 * a + 3 + k, reg, reg, sib).wait_recv()
        for cp in sends:
            cp.wait_send()
        for lc in local:
            lc.wait()

    out_shape = (jax.ShapeDtypeStruct((8, 3 * D), F32), jax.ShapeDtypeStruct((64, D), F32),
                 jax.ShapeDtypeStruct(specs[0][0], F32), jax.ShapeDtypeStruct(specs[1][0], BF16),
                 jax.ShapeDtypeStruct((2048, D), BF16), jax.ShapeDtypeStruct((D, 3 * D), BF16))
    return pl.pallas_call(
        body, name="gather_in", out_shape=out_shape,
        in_specs=[VMEM] * 7, out_specs=(VMEM,) * 6,
        scratch_shapes=[pltpu.VMEM((D, 1280), BF16), pltpu.VMEM((D, 768), BF16), pltpu.VMEM((8, D), F32),
                        pltpu.VMEM((16, D), F32), pltpu.VMEM((NCHIP, 16, 768), F32),
                        pltpu.SemaphoreType.DMA((n_sem,)), pltpu.SemaphoreType.DMA((n_sem,)),
                        pltpu.SemaphoreType.DMA((2 * na + 1,))],
        compiler_params=_cp(vmem_mb=56),
    )(c, c_ctx, ada_w, ada_b_j, w_in, w_out, smalls)


HBM = pl.BlockSpec(memory_space=pltpu.HBM)
SEM = pl.BlockSpec(memory_space=pltpu.SEMAPHORE)


def _late_gather_regions(x, y, c):
    chips = [(1 - x, y), (x, 1 - y), (1 - x, 1 - y)]
    wo_reg = lambda r, jj, cc: r.at[pl.ds(_mo(512 * jj + 256 * cc, 16), 256), :]
    ada_reg = lambda r, jj, cc: r.at[pl.ds(_mo(512 * cc, 16), 512), pl.ds(_mo(768 * jj, 128), 768)]
    return chips, wo_reg, ada_reg


def _late_gather_start(wo_land, ada_land):
    def body(wol_ref, adal_ref, wo_ss, wo_rs, ada_ss, ada_rs, wol_thru, adal_thru, token):
        x, y, c = lax.axis_index("x"), lax.axis_index("y"), lax.axis_index("c")
        j = 2 * x + y
        chips, wo_reg, ada_reg = _late_gather_regions(x, y, c)
        for k in range(3):
            for cc in range(2):
                pltpu.make_async_remote_copy(src_ref=wo_reg(wol_ref, j, c), dst_ref=wo_reg(wol_ref, j, c),
                                             send_sem=wo_ss.at[2 * k + cc], recv_sem=wo_rs.at[2 * k + c],
                                             device_id=(*chips[k], cc), device_id_type=MESH).start()
        for k in range(3):
            for cc in range(2):
                pltpu.make_async_remote_copy(src_ref=ada_reg(adal_ref, j, c), dst_ref=ada_reg(adal_ref, j, c),
                                             send_sem=ada_ss.at[2 * k + cc], recv_sem=ada_rs.at[2 * k + c],
                                             device_id=(*chips[k], cc), device_id_type=MESH).start()
        token[...] = jnp.zeros_like(token)

    sems = pltpu.SemaphoreType.DMA((6,))
    return pl.pallas_call(
        body, name="late_gather_start",
        out_shape=(sems, sems, sems, sems, pltpu.HBM(wo_land.shape, BF16), pltpu.HBM(ada_land.shape, BF16),
                   jax.ShapeDtypeStruct((8, 128), F32)),
        in_specs=(HBM, HBM), out_specs=(SEM, SEM, SEM, SEM, HBM, HBM, VMEM), input_output_aliases={0: 4, 1: 5},
        compiler_params=pltpu.CompilerParams(has_side_effects=pltpu.SideEffectType.DATAFLOW_SIDE_EFFECTING),
    )(pltpu.with_memory_space_constraint(wo_land, pltpu.HBM), pltpu.with_memory_space_constraint(ada_land, pltpu.HBM))


def _late_gather_wait(land, send_sems, recv_sems, which, after, name):
    def body(land_ref, ss, rs, after_ref, land_out):
        x, y, c = lax.axis_index("x"), lax.axis_index("y"), lax.axis_index("c")
        j = 2 * x + y
        chips, wo_reg, ada_reg = _late_gather_regions(x, y, c)
        reg = wo_reg if which == "w_out" else ada_reg
        for k in range(3):
            kj = 2 * chips[k][0] + chips[k][1]
            for cc in range(2):
                cp = pltpu.make_async_remote_copy(src_ref=reg(land_ref, j, c), dst_ref=reg(land_ref, kj, cc),
                                                  send_sem=ss.at[2 * k + cc], recv_sem=rs.at[2 * k + cc],
                                                  device_id=(*chips[k], cc), device_id_type=MESH)
                cp.wait_send()
                cp.wait_recv()

    return pl.pallas_call(
        body, name=name, out_shape=pltpu.HBM(land.shape, land.dtype),
        in_specs=(HBM, SEM, SEM, ANY), out_specs=HBM, input_output_aliases={0: 0},
        compiler_params=pltpu.CompilerParams(has_side_effects=pltpu.SideEffectType.DATAFLOW_SIDE_EFFECTING),
    )(land, send_sems, recv_sems, after)


RCHUNK = 16


def _reduce_all(g_w_in, g_w_out, pack):
    rp = pack.shape[0]
    hp = rp // 2
    assert hp % RCHUNK == 0
    wi_w = 1280

    def body(wi_hbm, wo_hbm, pk_hbm, wi_out, wo_out, pk_out,
             wi_mine, wi_recv, wi_send, wi_rb, wo_mine, wo_recv, wo_send, wo_rb, wo_own,
             pk_mine, pk_recv, pk_send, pk_rb, pk_own, send_sems, recv_sems, local_sems):
        x, y, c = lax.axis_index("x"), lax.axis_index("y"), lax.axis_index("c")
        j = 2 * x + y
        sib = (x, y, 1 - c)
        chips = [(1 - x, y), (x, 1 - y), (1 - x, 1 - y)]
        cj = [2 * cx + cy for cx, cy in chips]
        slabs = cj + [j]

        def copy(k, src, dst, to):
            return pltpu.make_async_remote_copy(src_ref=src, dst_ref=dst, send_sem=send_sems.at[k],
                                                recv_sem=recv_sems.at[k], device_id=to, device_id_type=MESH)

        def local(k, src, dst):
            cp = pltpu.make_async_copy(src, dst, local_sems.at[k])
            cp.start()
            return cp

        rows_half = lambda r, cc, n: r.at[pl.ds(_mo(cc * n, 16), n), :]
        cols_half = lambda r, cc, n: r.at[:, pl.ds(_mo(cc * n, 128), n)]
        wi_slab = lambda r, cc, jj: r.at[pl.ds(_mo(cc * 512, 16), 512), pl.ds(_mo(jj * wi_w, 128), wi_w)]
        pk_piece = lambda r, cc, jj: r.at[pl.ds(_mo(cc * hp, 16), hp), pl.ds(_mo(jj * 128, 128), 128)]

        sends = []

        def start(cp):
            cp.start()
            sends.append(cp)

        l_pk = local(0, rows_half(pk_hbm, c, hp), pk_mine)
        start(copy(0, rows_half(pk_hbm, 1 - c, hp), pk_recv, sib))
        l_wo = local(1, cols_half(wo_hbm, c, 512), wo_mine)
        start(copy(1, cols_half(wo_hbm, 1 - c, 512), wo_recv, sib))
        l_wi = []
        for s in range(4):
            l_wi.append(local(2 + s, wi_slab(wi_hbm, c, slabs[s]), wi_mine.at[s]))
            start(copy(2 + s, wi_slab(wi_hbm, 1 - c, slabs[s]), wi_recv.at[s], sib))

        def pair_sum(mine, recv, send, nrows, keep):
            def step(i, carry):
                rows = pl.ds(_mo(i * RCHUNK, RCHUNK), RCHUNK)
                s = mine[rows, :] + recv[rows, :]
                if keep:
                    mine[rows, :] = s
                if send is not None:
                    send[rows, :] = s.astype(BF16)
                return carry
            lax.fori_loop(0, nrows // RCHUNK, step, 0)

        def chip_sum(own, rb, nrows):
            def step(i, carry):
                rows = pl.ds(_mo(i * RCHUNK, RCHUNK), RCHUNK)
                own[rows, :] = (own[rows, :] + rb[0, rows, :].astype(F32)
                                + rb[1, rows, :].astype(F32) + rb[2, rows, :].astype(F32))
                return carry
            lax.fori_loop(0, nrows // RCHUNK, step, 0)

        l_pk.wait()
        copy(0, pk_recv, pk_recv, sib).wait_recv()
        pair_sum(pk_mine, pk_recv, pk_send, hp, True)
        for k in range(3):
            start(copy(6 + k, pk_send.at[:, pl.ds(_mo(cj[k] * 128, 128), 128)], pk_rb.at[k], (*chips[k], c)))
        l_pk_own = local(6, pk_mine.at[:, pl.ds(_mo(j * 128, 128), 128)], pk_own)

        l_wo.wait()
        copy(1, wo_recv, wo_recv, sib).wait_recv()
        pair_sum(wo_mine, wo_recv, wo_send, 2048, True)
        for k in range(3):
            start(copy(9 + k, wo_send.at[pl.ds(_mo(cj[k] * 512, 16), 512), :], wo_rb.at[k], (*chips[k], c)))
        l_wo_own = local(7, wo_mine.at[pl.ds(_mo(j * 512, 16), 512), :], wo_own)

        l_pk_own.wait()
        for k in range(3):
            copy(6 + k, pk_rb.at[k], pk_rb.at[k], sib).wait_recv()
        chip_sum(pk_own, pk_rb, hp)
        l_pk_out = local(8, pk_own, pk_piece(pk_out, c, j))
        start(copy(15, pk_own, pk_piece(pk_out, c, j), sib))
        for k in range(3):
            start(copy(16 + k, pk_own, pk_piece(pk_out, c, j), (*chips[k], c)))

        for s in range(3):
            l_wi[s].wait()
            copy(2 + s, wi_recv.at[s], wi_recv.at[s], sib).wait_recv()
            pair_sum(wi_mine.at[s], wi_recv.at[s], wi_send.at[s], 512, False)
            start(copy(12 + s, wi_send.at[s], wi_rb.at[s], (*chips[s], c)))
        l_wi[3].wait()
        copy(5, wi_recv.at[3], wi_recv.at[3], sib).wait_recv()
        pair_sum(wi_mine.at[3], wi_recv.at[3], None, 512, True)

        l_wo_own.wait()
        for k in range(3):
            copy(9 + k, wo_rb.at[k], wo_rb.at[k], sib).wait_recv()
        chip_sum(wo_own, wo_rb, 512)
        l_wo_out = local(9, wo_own, cols_half(wo_out, c, 512))
        start(copy(22, wo_own, cols_half(wo_out, c, 512), sib))

        for k in range(3):
            reg = pk_piece(pk_out, c, cj[k])
            copy(16 + k, reg, reg, sib).wait_recv()
            start(copy(19 + k, reg, reg, sib))

        for k in range(3):
            copy(12 + k, wi_rb.at[k], wi_rb.at[k], sib).wait_recv()
        chip_sum(wi_mine.at[3], wi_rb, 512)
        l_wi_out = local(10, wi_mine.at[3], rows_half(wi_out, c, 512))
        start(copy(23, wi_mine.at[3], rows_half(wi_out, c, 512), sib))

        reg = pk_piece(pk_out, 1 - c, j)
        copy(15, reg, reg, sib).wait_recv()
        for k in range(3):
            reg = pk_piece(pk_out, 1 - c, cj[k])
            copy(19 + k, reg, reg, sib).wait_recv()
        reg = cols_half(wo_out, 1 - c, 512)
        copy(22, reg, reg, sib).wait_recv()
        reg = rows_half(wi_out, 1 - c, 512)
        copy(23, reg, reg, sib).wait_recv()
        for cp in sends:
            cp.wait_send()
        for cp in (l_pk_out, l_wo_out, l_wi_out):
            cp.wait()

    return pl.pallas_call(
        body, name="reduce_all",
        out_shape=(jax.ShapeDtypeStruct((D, wi_w), F32), jax.ShapeDtypeStruct((512, D), F32),
                   jax.ShapeDtypeStruct(pack.shape, F32)),
        in_specs=[ANY] * 3, out_specs=(ANY,) * 3,
        scratch_shapes=[
            pltpu.VMEM((4, 512, wi_w), F32), pltpu.VMEM((4, 512, wi_w), F32), pltpu.VMEM((3, 512, wi_w), BF16),
            pltpu.VMEM((3, 512, wi_w), BF16),
            pltpu.VMEM((2048, 512), F32), pltpu.VMEM((2048, 512), F32), pltpu.VMEM((2048, 512), BF16),
            pltpu.VMEM((3, 512, 512), BF16), pltpu.VMEM((512, 512), F32),
            pltpu.VMEM((hp, 512), F32), pltpu.VMEM((hp, 512), F32), pltpu.VMEM((hp, 512), BF16),
            pltpu.VMEM((3, hp, 128), BF16), pltpu.VMEM((hp, 128), F32),
            pltpu.SemaphoreType.DMA((24,)), pltpu.SemaphoreType.DMA((24,)), pltpu.SemaphoreType.DMA((11,))],
        compiler_params=_cp(vmem_mb=56),
    )(g_w_in, g_w_out, pack)


def _ada_bwd(c_all, dmx_all_j, dmc_j, dmx_all, dmc, c_ctx, ada_w_full):
    def body(c_ref, dmxj_ref, dmcj_ref, dmx_ref, dmc_ref, cc_ref, w_ref, gw_ref, gb_ref, gc_ref, lhs, rhs, dm8):
        lhs[...] = jnp.zeros_like(lhs)
        rhs[...] = jnp.zeros_like(rhs)
        cv = c_ref[...]
        lhs[0:8, :] = cv * _sigmoid(cv)
        cc = cc_ref[...]
        a_c, da_c = _silu_and_grad(cc)
        lhs[8:9, :] = a_c
        rhs[0:8, :] = dmxj_ref[...]
        rhs[8:9, :] = dmcj_ref[...]
        gw_ref[...] = _dot_tn(lhs[...].astype(BF16), rhs[...].astype(BF16))
        gb_ref[...] = jnp.sum(dmx_ref[...], axis=0, keepdims=True) + dmc_ref[...]
        dm8[...] = jnp.zeros_like(dm8)
        dm8[0:1, :] = dmc_ref[...]
        da = _dot_nt(dm8[...].astype(BF16), w_ref[...])
        gc_ref[...] = da[0:1, :] * da_c

    return pl.pallas_call(
        body, name="ada_bwd",
        out_shape=(jax.ShapeDtypeStruct((D, 768), F32), jax.ShapeDtypeStruct((1, 3 * D), F32),
                   jax.ShapeDtypeStruct((1, D), F32)),
        in_specs=[VMEM] * 7, out_specs=(VMEM,) * 3,
        scratch_shapes=[pltpu.VMEM((16, D), F32), pltpu.VMEM((16, 768), F32), pltpu.VMEM((8, 3 * D), F32)],
        compiler_params=_cp(vmem_mb=32),
    )(c_all, dmx_all_j, dmc_j, dmx_all, dmc, c_ctx, ada_w_full)


def _proj(x, mods, mrow, norm_g, w_full, nk, name):
    lx = x.shape[0]
    n = lx // T

    def body(x_ref, sh_ref, sc_ref, ng_ref, *rest):
        w_refs, hn_ref, z_refs = rest[:nk], rest[nk], rest[nk + 1:]
        xv = x_ref[...]
        r = lax.rsqrt(jnp.mean(xv * xv, axis=-1, keepdims=True) + NORM_EPS)
        hn = (xv * r) * ng_ref[...] * (1.0 + sc_ref[mrow:mrow + 1, :]) + sh_ref[mrow:mrow + 1, :]
        hb = hn.astype(BF16)
        hn_ref[...] = hb
        for k in range(nk):
            z_refs[k][...] = _dot(hb, w_refs[k][...])

    row = pl.BlockSpec((T, D), lambda i: (i, 0))
    in_specs = [row, pl.BlockSpec((8, D), lambda i: (0, 0)), pl.BlockSpec((8, D), lambda i: (0, 1)),
                pl.BlockSpec((1, D), lambda i: (0, 0))]
    in_specs += [pl.BlockSpec((D, D), lambda i, k=k: (0, k)) for k in range(nk)]
    out_shape = (jax.ShapeDtypeStruct((lx, D), BF16),) + tuple(jax.ShapeDtypeStruct((lx, D), F32) for _ in range(nk))
    return pl.pallas_call(
        body, name=name, grid=(n,), out_shape=out_shape, in_specs=in_specs, out_specs=(row,) * (nk + 1),
        compiler_params=_cp(1, vmem_mb=56),
    )(x, mods, mods, norm_g, *([w_full] * nk))


def _halo_specs(lx):
    last = lx // 8 - 1
    return [pl.BlockSpec((T, D), lambda i: (i, 0)),
            pl.BlockSpec((8, D), lambda i: (jnp.maximum(i * (T // 8) - 1, 0), 0)),
            pl.BlockSpec((8, D), lambda i: (jnp.minimum((i + 1) * (T // 8), last), 0))]


def _zhalo_specs(lx):
    last = lx // 8 - 1
    return [pl.BlockSpec((T * NH, HD), lambda i: (i, 0)),
            pl.BlockSpec((8 * NH, HD), lambda i: (jnp.maximum(i * (T // 8) - 1, 0), 0)),
            pl.BlockSpec((8 * NH, HD), lambda i: (jnp.minimum((i + 1) * (T // 8), last), 0))]


ZT = pl.BlockSpec((T * NH, HD), lambda i: (i, 0))
CONV_CHUNK = 32


def _lru_gates_fwd(xa, conv_wz, conv_bz, wcat, bcat, lamcat, name):
    lx = xa.shape[0]
    n = lx // T

    def body(xm, xp, xn, cw, cb, w_ref, b_ref, lam_ref, xaz_o, xcz_o, af_o, uf_o, ab_o, ub_o, pad):
        i = pl.program_id(0)
        pmask = jnp.where(i == 0, 0.0, 1.0)
        nmask = jnp.where(i == n - 1, 0.0, 1.0)
        for h in range(NH):
            cols = slice(HD * h, HD * h + HD)
            pad[_zrows(h, 8), :] = xp[:, cols] * pmask
            pad[pl.ds(8 * NH + h, T, stride=NH), :] = xm[:, cols]
            pad[pl.ds((T + 8) * NH + h, 8, stride=NH), :] = xn[:, cols] * nmask
        xaz_o[...] = pad[pl.ds(8 * NH, T * NH), :]

        def conv_chunk(ci, carry):
            base = pl.multiple_of(ci * (CONV_CHUNK * NH), CONV_CHUNK * NH)
            acc = None
            for k in range(4):
                sl = pad[pl.ds(base + (7 + k) * NH, CONV_CHUNK * NH), :].reshape(CONV_CHUNK, NH, HD)
                term = sl * cw[k][None]
                acc = term if acc is None else acc + term
            acc = acc + cb[...][None]
            xcz_o[pl.ds(base, CONV_CHUNK * NH), :] = acc.reshape(CONV_CHUNK * NH, HD)
            return carry
        lax.fori_loop(0, T // CONV_CHUNK, conv_chunk, 0)

        outs = ((af_o, uf_o), (ab_o, ub_o))
        for h in range(NH):
            xch = xcz_o[_zrows(h, T), :]
            pre = _dot(xch.astype(BF16), w_ref[h]) + b_ref[h:h + 1, :]
            for d in range(2):
                _, gi, _, _, a, mult = _lru_gate(pre, lam_ref[h:h + 1, :], d)
                outs[d][0][_zrows(h, T), :] = a
                outs[d][1][_zrows(h, T), :] = mult * gi * xch

    full = lambda shape: pl.BlockSpec(shape, lambda i: (0,) * len(shape))
    in_specs = _halo_specs(lx) + [full((4, NH, HD)), full((NH, HD)), full((NH, HD, 4 * HD)), full((NH, 4 * HD)),
                                  full((NH, 2 * HD))]
    zs = jax.ShapeDtypeStruct((lx * NH, HD), F32)
    return pl.pallas_call(
        body, name=name, grid=(n,), out_shape=(zs,) * 6, in_specs=in_specs, out_specs=(ZT,) * 6,
        scratch_shapes=[pltpu.VMEM(((T + 16) * NH, HD), F32)],
        compiler_params=_cp(1, vmem_mb=48),
    )(xa, xa, xa, conv_wz, conv_bz, wcat, bcat, lamcat)


def _scan(a_up, x_up, a_dn, x_dn, s_up, s_dn, post, name):
    lx = a_up.shape[0] // NH
    n = lx // T

    def body(au, xu, ad, xd, su0, sd0, *rest):
        if post:
            ou, od, fu, fd, carry = rest
        else:
            ou, od, pu, pd, fu, fd, carry = rest
        i = pl.program_id(0)

        @pl.when(i == 0)
        def _():
            carry[0] = su0[...]
            carry[1] = sd0[...]

        def step(k, s):
            su, sd = s
            ru = pl.ds(_mo(k * NH, NH), NH)
            rd = pl.ds(_mo((T - 1 - k) * NH, NH), NH)
            if post:
                vu = xu[ru, :] + su
                vd = xd[rd, :] + sd
                ou[ru, :] = vu
                od[rd, :] = vd
                return au[ru, :] * vu, ad[rd, :] * vd
            pu[ru, :] = su
            pd[rd, :] = sd
            vu = au[ru, :] * su + xu[ru, :]
            vd = ad[rd, :] * sd + xd[rd, :]
            ou[ru, :] = vu
            od[rd, :] = vd
            return vu, vd

        su, sd = lax.fori_loop(0, T, step, (carry[0], carry[1]), unroll=8)
        carry[0] = su
        carry[1] = sd
        fu[...] = su
        fd[...] = sd

    up = pl.BlockSpec((T * NH, HD), lambda i: (i, 0))
    dn = pl.BlockSpec((T * NH, HD), lambda i: (n - 1 - i, 0))
    st = pl.BlockSpec((NH, HD), lambda i: (0, 0))
    zs = jax.ShapeDtypeStruct((lx * NH, HD), F32)
    ss = jax.ShapeDtypeStruct((NH, HD), F32)
    if post:
        out_shape, out_specs = (zs, zs, ss, ss), (up, dn, st, st)
    else:
        out_shape, out_specs = (zs, zs, zs, zs, ss, ss), (up, dn, up, dn, st, st)
    return pl.pallas_call(
        body, name=name, grid=(n,), out_shape=out_shape, in_specs=[up, up, dn, dn, st, st], out_specs=out_specs,
        scratch_shapes=[pltpu.VMEM((2, NH, HD), F32)],
        compiler_params=_cp(1, vmem_mb=48),
    )(a_up, x_up, a_dn, x_dn, s_up, s_dn)


def _sgu_parts(u, v, lng, lnb, w_ref, bt_ref, mixed_s):
    ug, dug = _gelu_and_grad(u)
    vg, dvg = _gelu_and_grad(v)
    mu = jnp.mean(vg, axis=-1, keepdims=True)
    vc = vg - mu
    rstd = lax.rsqrt(jnp.mean(vc * vc, axis=-1, keepdims=True) + LN_EPS)
    vh = vc * rstd
    vn = (vh * lng + lnb).astype(BF16)
    for g in range(NH):
        cols = slice(HD * g, HD * g + HD)
        mixed_s[:, cols] = _dot(w_ref[g], vn[:, cols]) + bt_ref[:, g:g + 1]
    return ug, dug, dvg, rstd, vh, vn


def _sgu_fwd(u, v, ln_g, ln_b, sgu_w, sgu_bt):
    lx = u.shape[0]
    n = lx // HD

    def body(u_ref, v_ref, g_ref, b_ref, w_ref, bt_ref, y_ref, mixed_s):
        ug, _, _, _, _, _ = _sgu_parts(u_ref[...], v_ref[...], g_ref[...], b_ref[...], w_ref, bt_ref, mixed_s)
        y_ref[...] = ug * mixed_s[...]

    row = pl.BlockSpec((HD, D), lambda i: (i, 0))
    vec = pl.BlockSpec((1, D), lambda i: (0, 0))
    return pl.pallas_call(
        body, name="sgu_fwd", grid=(n,), out_shape=jax.ShapeDtypeStruct((lx, D), F32),
        in_specs=[row, row, vec, vec, pl.BlockSpec((NH, HD, HD), lambda i: (0, 0, 0)),
                  pl.BlockSpec((HD, NH), lambda i: (0, 0))],
        out_specs=row, scratch_shapes=[pltpu.VMEM((HD, D), F32)],
        compiler_params=_cp(1),
    )(u, v, ln_g, ln_b, sgu_w, sgu_bt)


def _sgu_bwd(u, v, dys, ln_g, ln_b, sgu_w, sgu_bt):
    lx = u.shape[0]
    n = lx // HD

    def body(u_ref, v_ref, dy_ref, g_ref, b_ref, w_ref, bt_ref, du_ref, dv_ref, dw_ref, db_ref, dg_ref, dbl_ref,
             mixed_s, dvn_s):
        i = pl.program_id(0)

        @pl.when(i == 0)
        def _():
            dw_ref[...] = jnp.zeros_like(dw_ref)
            db_ref[...] = jnp.zeros_like(db_ref)
            dg_ref[...] = jnp.zeros_like(dg_ref)
            dbl_ref[...] = jnp.zeros_like(dbl_ref)

        lng = g_ref[...]
        ug, dug, dvg, rstd, vh, vn = _sgu_parts(u_ref[...], v_ref[...], lng, b_ref[...], w_ref, bt_ref, mixed_s)
        dys_v = dy_ref[...]
        du_ref[...] = (dys_v * mixed_s[...] * dug).astype(BF16)
        dmix = dys_v * ug
        ones = jnp.ones((8, HD), BF16)
        for g in range(NH):
            cols = slice(HD * g, HD * g + HD)
            dm = dmix[:, cols]
            hi = dm.astype(BF16)
            lo = (dm - hi.astype(F32)).astype(BF16)
            dw_ref[g] += _dot_nt(hi, vn[:, cols])
            db_ref[g:g + 1, :] += (_dot_nt(ones, hi) + _dot_nt(ones, lo))[0:1, :]
            dvn_s[:, cols] = _dot_tn(w_ref[g], hi)
        dvn = dvn_s[...]
        dg_ref[...] += jnp.sum(dvn * vh, axis=0, keepdims=True)
        dbl_ref[...] += jnp.sum(dvn, axis=0, keepdims=True)
        dvh = dvn * lng
        dvg_in = rstd * (dvh - jnp.mean(dvh, axis=-1, keepdims=True)
                         - vh * jnp.mean(dvh * vh, axis=-1, keepdims=True))
        dv_ref[...] = (dvg_in * dvg).astype(BF16)

    row = pl.BlockSpec((HD, D), lambda i: (i, 0))
    vec = pl.BlockSpec((1, D), lambda i: (0, 0))
    wsp = pl.BlockSpec((NH, HD, HD), lambda i: (0, 0, 0))
    bsp = pl.BlockSpec((NH, HD), lambda i: (0, 0))
    return pl.pallas_call(
        body, name="sgu_bwd", grid=(n,),
        out_shape=(jax.ShapeDtypeStruct((lx, D), BF16), jax.ShapeDtypeStruct((lx, D), BF16),
                   jax.ShapeDtypeStruct((NH, HD, HD), F32), jax.ShapeDtypeStruct((NH, HD), F32),
                   jax.ShapeDtypeStruct((1, D), F32), jax.ShapeDtypeStruct((1, D), F32)),
        in_specs=[row, row, row, vec, vec, wsp, pl.BlockSpec((HD, NH), lambda i: (0, 0))],
        out_specs=(row, row, wsp, bsp, vec, vec),
        scratch_shapes=[pltpu.VMEM((HD, D), F32), pltpu.VMEM((HD, D), F32)],
        compiler_params=_cp(1),
    )(u, v, dys, ln_g, ln_b, sgu_w, sgu_bt)


def _out_fwd_bwd(hf_z, hb_z, ga, gb, ys, x, tgt, mods, final_g, w_out_full):
    lx = x.shape[0]
    n = lx // T

    def body(hf_ref, hb_ref, ga_ref, gb_ref, ys_ref, x_ref, t_ref, gx_ref, fg_ref, w_ref,
             loss_ref, dfg_ref, dgx_ref, dxn_ref, y_ref, do_ref, dga_ref, dgb_ref, dyl_ref, dys_ref, yl_s):
        i = pl.program_id(0)

        @pl.when(i == 0)
        def _():
            loss_ref[...] = jnp.zeros_like(loss_ref)
            dfg_ref[...] = jnp.zeros_like(dfg_ref)
            dgx_ref[...] = jnp.zeros_like(dgx_ref)

        for h in range(NH):
            yl_s[:, HD * h:HD * h + HD] = hf_ref[_zrows(h, T), :] + hb_ref[_zrows(h, T), :]
        yl = yl_s[...]
        gav = ga_ref[...]
        gbv = gb_ref[...]
        sa, dsa = _silu_and_grad(gav)
        sb, dsb = _silu_and_grad(gbv)
        ysv = ys_ref[...]
        y_ref[:, 0:D] = (yl * sa).astype(BF16)
        y_ref[:, D:2 * D] = (ysv * sb).astype(BF16)
        o = _dot(y_ref[...], w_ref[...])
        gx = gx_ref[0:1, :]
        xnew = x_ref[...] + gx * o
        r2 = lax.rsqrt(jnp.mean(xnew * xnew, axis=-1, keepdims=True) + NORM_EPS)
        xh = xnew * r2
        fg = fg_ref[...]
        err = xh * fg - t_ref[...]
        loss_ref[...] += 0.5 * jnp.sum(jnp.mean(err * err, axis=-1, keepdims=True), axis=0, keepdims=True)
        dout = err * (1.0 / D)
        dfg_ref[...] += jnp.sum(dout * xh, axis=0, keepdims=True)
        dxh = dout * fg
        dxn = r2 * (dxh - xh * jnp.mean(dxh * xh, axis=-1, keepdims=True))
        dxn_ref[...] = dxn
        dgx_ref[...] += jnp.sum(dxn * o, axis=0, keepdims=True)
        do = (dxn * gx).astype(BF16)
        do_ref[...] = do
        dy = _dot_nt(do, w_ref[...])
        dy1 = dy[:, 0:D]
        dy2 = dy[:, D:2 * D]
        dga_ref[...] = (dy1 * yl * dsa).astype(BF16)
        dgb_ref[...] = (dy2 * ysv * dsb).astype(BF16)
        dys_ref[...] = dy2 * sb
        yl_s[...] = dy1 * sa
        for h in range(NH):
            dyl_ref[_zrows(h, T), :] = yl_s[:, HD * h:HD * h + HD]

    row = pl.BlockSpec((T, D), lambda i: (i, 0))
    vec = pl.BlockSpec((1, D), lambda i: (0, 0))
    in_specs = [ZT, ZT, row, row, row, row, row, pl.BlockSpec((8, D), lambda i: (0, 2)), vec,
                pl.BlockSpec((2 * D, D), lambda i: (0, 0))]
    out_shape = (jax.ShapeDtypeStruct((1, 1), F32), jax.ShapeDtypeStruct((1, D), F32), jax.ShapeDtypeStruct((1, D), F32),
                 jax.ShapeDtypeStruct((lx, D), F32), jax.ShapeDtypeStruct((lx, 2 * D), BF16),
                 jax.ShapeDtypeStruct((lx, D), BF16), jax.ShapeDtypeStruct((lx, D), BF16),
                 jax.ShapeDtypeStruct((lx, D), BF16), jax.ShapeDtypeStruct((lx * NH, HD), F32),
                 jax.ShapeDtypeStruct((lx, D), F32))
    out_specs = (pl.BlockSpec((1, 1), lambda i: (0, 0)), vec, vec, row, pl.BlockSpec((T, 2 * D), lambda i: (i, 0)),
                 row, row, row, ZT, row)
    return pl.pallas_call(
        body, name="out_fwd_bwd", grid=(n,), out_shape=out_shape, in_specs=in_specs, out_specs=out_specs,
        scratch_shapes=[pltpu.VMEM((T, D), F32)],
        compiler_params=_cp(1, vmem_mb=56),
    )(hf_z, hb_z, ga, gb, ys, x, tgt, mods, final_g, w_out_full)


def _lru_gates_bwd(xc_z, lf_z, lb_z, pf_z, pb_z, wcat, bcat, lamcat, dw0, db0, dl0, name):
    lx = xc_z.shape[0] // NH
    n = lx // T

    def body(xc_ref, lf_ref, lb_ref, pf_ref, pb_ref, w_ref, b_ref, lam_ref, dw0_ref, db0_ref, dl0_ref,
             dxc_ref, dw_ref, db_ref, dl_ref, dpre_s):
        i = pl.program_id(0)

        @pl.when(i == 0)
        def _():
            dw_ref[...] = dw0_ref[...]
            db_ref[...] = db0_ref[...]
            dl_ref[...] = dl0_ref[...]

        lam_refs = (lf_ref, lb_ref)
        prev_refs = (pf_ref, pb_ref)
        for h in range(NH):
            xch = xc_ref[_zrows(h, T), :]
            xcb = xch.astype(BF16)
            pre = _dot(xcb, w_ref[h]) + b_ref[h:h + 1, :]
            dxc = jnp.zeros((T, HD), F32)
            for d in range(2):
                r, gi, lam, sp, a, mult = _lru_gate(pre, lam_ref[h:h + 1, :], d)
                du = lam_refs[d][_zrows(h, T), :]
                da = du * prev_refs[d][_zrows(h, T), :]
                dgi = du * mult * xch
                dxc = dxc + du * mult * gi
                dmult = du * gi * xch
                dla = da * a - dmult * (a * a) / mult
                dr = dla * ((-LRU_C) * sp)
                dsp = jnp.sum(dla * ((-LRU_C) * r), axis=0, keepdims=True)
                dl_ref[h:h + 1, HD * d:HD * d + HD] += dsp * (-_sigmoid(-lam))
                dpre_s[:, 256 * d:256 * d + HD] = dr * r * (1.0 - r)
                dpre_s[:, 256 * d + HD:256 * d + 2 * HD] = dgi * gi * (1.0 - gi)
            dpre = dpre_s[...]
            dpb = dpre.astype(BF16)
            dw_ref[h] += _dot_tn(xcb, dpb)
            db_ref[h:h + 1, :] += jnp.sum(dpre, axis=0, keepdims=True)
            dxc_ref[_zrows(h, T), :] = dxc + _dot_nt(dpb, w_ref[h])

    full = lambda shape: pl.BlockSpec(shape, lambda i: (0,) * len(shape))
    wsp, bsp, lsp = full((NH, HD, 4 * HD)), full((NH, 4 * HD)), full((NH, 2 * HD))
    return pl.pallas_call(
        body, name=name, grid=(n,),
        out_shape=(jax.ShapeDtypeStruct((lx * NH, HD), F32), jax.ShapeDtypeStruct((NH, HD, 4 * HD), F32),
                   jax.ShapeDtypeStruct((NH, 4 * HD), F32), jax.ShapeDtypeStruct((NH, 2 * HD), F32)),
        in_specs=[ZT] * 5 + [wsp, bsp, lsp, wsp, bsp, lsp], out_specs=(ZT, wsp, bsp, lsp),
        scratch_shapes=[pltpu.VMEM((T, 4 * HD), F32)],
        compiler_params=_cp(1, vmem_mb=48),
    )(xc_z, lf_z, lb_z, pf_z, pb_z, wcat, bcat, lamcat, dw0, db0, dl0)


def _conv_bwd(dxc_z, xa_z, conv_wz, dcw0, dcb0, name):
    lx = dxc_z.shape[0] // NH
    n = lx // T

    def body(dm, dp, dn, xa_ref, cw, dcw0_ref, dcb0_ref, dxa_ref, dcw_ref, dcb_ref, pad, dxa_s):
        i = pl.program_id(0)

        @pl.when(i == 0)
        def _():
            dcw_ref[...] = dcw0_ref[...]
            dcb_ref[...] = dcb0_ref[...]

        pmask = jnp.where(i == 0, 0.0, 1.0)
        nmask = jnp.where(i == n - 1, 0.0, 1.0)
        pad[pl.ds(0, 8 * NH), :] = dp[...] * pmask
        pad[pl.ds(8 * NH, T * NH), :] = dm[...]
        pad[pl.ds((T + 8) * NH, 8 * NH), :] = dn[...] * nmask

        def chunk(ci, carry):
            base = pl.multiple_of(ci * (CONV_CHUNK * NH), CONV_CHUNK * NH)
            xav = xa_ref[pl.ds(base, CONV_CHUNK * NH), :].reshape(CONV_CHUNK, NH, HD)
            acc = None
            for k in range(4):
                sl = pad[pl.ds(base + (9 - k) * NH, CONV_CHUNK * NH), :].reshape(CONV_CHUNK, NH, HD)
                term = sl * cw[k][None]
                acc = term if acc is None else acc + term
                dcw_ref[k] += jnp.sum(sl * xav, axis=0)
                if k == 1:
                    dcb_ref[...] += jnp.sum(sl, axis=0)
            dxa_s[pl.ds(base, CONV_CHUNK * NH), :] = acc.reshape(CONV_CHUNK * NH, HD)
            return carry
        lax.fori_loop(0, T // CONV_CHUNK, chunk, 0)
        for h in range(NH):
            dxa_ref[:, HD * h:HD * h + HD] = dxa_s[_zrows(h, T), :].astype(BF16)

    full = lambda shape: pl.BlockSpec(shape, lambda i: (0,) * len(shape))
    return pl.pallas_call(
        body, name=name, grid=(n,),
        out_shape=(jax.ShapeDtypeStruct((lx, D), BF16), jax.ShapeDtypeStruct((4, NH, HD), F32),
                   jax.ShapeDtypeStruct((NH, HD), F32)),
        in_specs=_zhalo_specs(lx) + [ZT, full((4, NH, HD)), full((4, NH, HD)), full((NH, HD))],
        out_specs=(pl.BlockSpec((T, D), lambda i: (i, 0)), full((4, NH, HD)), full((NH, HD))),
        scratch_shapes=[pltpu.VMEM(((T + 16) * NH, HD), F32), pltpu.VMEM((T * NH, HD), F32)],
        compiler_params=_cp(1, vmem_mb=48),
    )(dxc_z, dxc_z, dxc_z, xa_z, conv_wz, dcw0, dcb0)


def _proj_bwd(dzs, x, dxn, mods, mrow, norm_g, w_full, dng0, name):
    lx = x.shape[0]
    n = lx // T
    nk = len(dzs)
    has_x = dxn is not None

    def body(*refs):
        dz_refs = refs[:nk]
        w_refs = refs[nk:2 * nk]
        x_ref, sc_ref, ng_ref, dng0_ref = refs[2 * nk:2 * nk + 4]
        rest = refs[2 * nk + 4:]
        if has_x:
            dxn_ref, gx_ref, dng_ref, dsc_ref, dsh_ref = rest
        else:
            dng_ref, dsc_ref, dsh_ref = rest
        i = pl.program_id(0)

        @pl.when(i == 0)
        def _():
            dng_ref[...] = dng0_ref[...]
            dsc_ref[...] = jnp.zeros_like(dsc_ref)
            dsh_ref[...] = jnp.zeros_like(dsh_ref)

        dhn = _dot_nt(dz_refs[0][...], w_refs[0][...])
        for k in range(1, nk):
            dhn = dhn + _dot_nt(dz_refs[k][...], w_refs[k][...])
        xv = x_ref[...]
        r = lax.rsqrt(jnp.mean(xv * xv, axis=-1, keepdims=True) + NORM_EPS)
        xn = xv * r
        ng = ng_ref[...]
        sc1 = 1.0 + sc_ref[mrow:mrow + 1, :]
        t = dhn * xn
        dng_ref[...] += jnp.sum(t * sc1, axis=0, keepdims=True)
        dsc_ref[...] += jnp.sum(t * ng, axis=0, keepdims=True)
        dsh_ref[...] += jnp.sum(dhn, axis=0, keepdims=True)
        if has_x:
            dxh = dhn * (ng * sc1)
            gx_ref[...] = dxn_ref[...] + r * (dxh - xn * jnp.mean(dxh * xn, axis=-1, keepdims=True))

    row = pl.BlockSpec((T, D), lambda i: (i, 0))
    vec = pl.BlockSpec((1, D), lambda i: (0, 0))
    in_specs = [row] * nk + [pl.BlockSpec((D, D), lambda i, k=k: (0, k)) for k in range(nk)]
    in_specs += [row, pl.BlockSpec((8, D), lambda i: (0, 1)), vec, vec]
    args = list(dzs) + [w_full] * nk + [x, mods, norm_g, dng0]
    vs = jax.ShapeDtypeStruct((1, D), F32)
    out_shape, out_specs = (vs, vs, vs), (vec, vec, vec)
    if has_x:
        in_specs.append(row)
        args.append(dxn)
        out_shape = (jax.ShapeDtypeStruct((lx, D), F32),) + out_shape
        out_specs = (row,) + out_specs
    return pl.pallas_call(
        body, name=name, grid=(n,), out_shape=out_shape, in_specs=in_specs, out_specs=out_specs,
        compiler_params=_cp(1, vmem_mb=56),
    )(*args)


def _tn_matmul(a, bs, extra, name):
    lx, m = a.shape
    tm = min(lx, 1024)
    n = lx // tm
    widths = [b.shape[1] for b in bs]
    nb = len(bs)

    def body(a_ref, *rest):
        b_refs = rest[:nb]
        rest = rest[nb:]
        if extra is not None:
            ea_ref, eb_ref = rest[:2]
            rest = rest[2:]
        out_ref, acc = rest
        i = pl.program_id(0)
        av = a_ref[...]
        off = 0
        for k in range(nb):
            cols = slice(off, off + widths[k])
            part = _dot_tn(av, b_refs[k][...])

            @pl.when(i == 0)
            def _():
                acc[:, cols] = part

            @pl.when(i > 0)
            def _():
                acc[:, cols] += part
            off += widths[k]

        @pl.when(i == n - 1)
        def _():
            if extra is not None:
                acc[:, 0:widths[0]] += _dot_tn(ea_ref[...], eb_ref[...])
            pltpu.sync_copy(acc, out_ref)

    in_specs = [pl.BlockSpec((tm, m), lambda i: (i, 0))] + [pl.BlockSpec((tm, w), lambda i: (i, 0)) for w in widths]
    args = [a] + list(bs)
    if extra is not None:
        in_specs += [VMEM, VMEM]
        args += list(extra)
    return pl.pallas_call(
        body, name=name, grid=(n,), out_shape=jax.ShapeDtypeStruct((m, sum(widths)), F32),
        in_specs=in_specs, out_specs=ANY, scratch_shapes=[pltpu.VMEM((m, sum(widths)), F32)],
        compiler_params=_cp(1, vmem_mb=56),
    )(*args)


def _adam_math(w, g, m, v):
    m = ADAM_B1 * m + (1.0 - ADAM_B1) * g
    v = ADAM_B2 * v + (1.0 - ADAM_B2) * (g * g)
    m_hat = m / (1.0 - ADAM_B1 ** ADAM_STEP)
    v_hat = v / (1.0 - ADAM_B2 ** ADAM_STEP)
    delta = -ADAM_LR * (m_hat / (jnp.sqrt(v_hat) + ADAM_EPS) + ADAM_WD * w)
    return delta, m, v


def _adam_big(w, g, m, v, name):
    rows, cols = w.shape
    tr = 256

    def body(w_ref, g_ref, m_ref, v_ref, d_o, m_o, v_o):
        d, mm, vv = _adam_math(w_ref[...], g_ref[...], m_ref[...], v_ref[...])
        d_o[...] = d
        m_o[...] = mm
        v_o[...] = vv

    blk = pl.BlockSpec((tr, cols), lambda i: (i, 0))
    s = jax.ShapeDtypeStruct((rows, cols), F32)
    return pl.pallas_call(
        body, name=name, grid=(rows // tr,), out_shape=(s, s, s), in_specs=[blk] * 4, out_specs=(blk,) * 3,
        compiler_params=_cp(1, vmem_mb=48),
    )(w, g, m, v)


def _adam_small(items):
    ni = len(items)

    def body(*refs):
        ins, outs = refs[:4 * ni], refs[4 * ni:]
        for k in range(ni):
            w_ref, g_ref, m_ref, v_ref = ins[4 * k:4 * k + 4]
            d, mm, vv = _adam_math(w_ref[...], g_ref[...], m_ref[...], v_ref[...])
            outs[3 * k][...] = d
            outs[3 * k + 1][...] = mm
            outs[3 * k + 2][...] = vv

    flat = [a for it in items for a in it]
    out_shape = tuple(jax.ShapeDtypeStruct(it[0].shape, F32) for it in items for _ in range(3))
    res = pl.pallas_call(
        body, name="adam_small", out_shape=out_shape, in_specs=[VMEM] * (4 * ni), out_specs=(VMEM,) * (3 * ni),
        compiler_params=_cp(vmem_mb=32),
    )(*flat)
    return [tuple(res[3 * k:3 * k + 3]) for k in range(ni)]


def kernel(x, c, ctx, c_ctx, ada_w, ada_b, norm_g, w_in, conv_w, conv_b, lru_wa, lru_ba, lru_wx, lru_bx, lru_lambda, sgu_ln_g, sgu_ln_b, sgu_w, sgu_b, w_out, final_g, loss_target, m_c_ctx, m_ada_w, m_ada_b, m_norm_g, m_w_in, m_conv_w, m_conv_b, m_lru_wa, m_lru_ba, m_lru_wx, m_lru_bx, m_lru_lambda, m_sgu_ln_g, m_sgu_ln_b, m_sgu_w, m_sgu_b, m_w_out, m_final_g, v_c_ctx, v_ada_w, v_ada_b, v_norm_g, v_w_in, v_conv_w, v_conv_b, v_lru_wa, v_lru_ba, v_lru_wx, v_lru_bx, v_lru_lambda, v_sgu_ln_g, v_sgu_ln_b, v_sgu_w, v_sgu_b, v_w_out, v_final_g):
    ix, iy, ic = lax.axis_index("x"), lax.axis_index("y"), lax.axis_index("c")
    chip = 2 * ix + iy
    dev = 2 * chip + ic
    lx = x.shape[1]
    lc = ctx.shape[1]

    smalls = jnp.concatenate([conv_w[0], lru_lambda[0], jnp.zeros((10, 256), F32)], axis=0)
    c_ctx2 = c_ctx.reshape(1, D)
    ada_b_j = lax.dynamic_slice(ada_b, (0, 768 * chip), (1, 768))
    mods, c_slots, sm_all, w_in_full, wo_land, ada_land = _gather_in(c, c_ctx2, ada_w[0], ada_b_j, w_in[0], w_out[0],
                                                                     smalls)
    wo_ss, wo_rs, ada_ss, ada_rs, wo_land, ada_land, token = _late_gather_start(wo_land, ada_land)
    mods = mods + token[0:1, 0:1]
    sm3 = sm_all.reshape(NCHIP, 16, 256)
    conv_w_full = sm3[:, 0:4, :].transpose(1, 0, 2).reshape(4, D)
    lam_full = sm3[:, 4:6, :].transpose(1, 0, 2).reshape(2, D)
    conv_wz = conv_w_full.reshape(4, NH, HD)
    conv_bz = conv_b.reshape(NH, HD)
    lamcat = lam_full.reshape(2, NH, HD).transpose(1, 0, 2).reshape(NH, 2 * HD)
    wa, wx, ba, bx = lru_wa[0], lru_wx[0], lru_ba[0], lru_bx[0]
    wcat = jnp.concatenate([wa[0], wx[0], wa[1], wx[1]], axis=-1).astype(BF16)
    bcat = jnp.concatenate([ba[0], bx[0], ba[1], bx[1]], axis=-1)
    sgu_wb = sgu_w[0].astype(BF16)
    sgu_bt = sgu_b[0].T
    final_g2 = final_g.reshape(1, D)

    zero_s = jnp.zeros((NH, HD), F32)
    hn_c, xa_c = _proj(ctx[0], mods, 1, norm_g, w_in_full, 1, "proj_ctx")
    xaz_c, xcz_c, af_c, uf_c, ab_c, ub_c = _lru_gates_fwd(xa_c, conv_wz, conv_bz, wcat, bcat, lamcat, "lru_gates_ctx")
    _, _, pf_c, pb_c, hf0, hb0 = _scan(af_c, uf_c, ab_c, ub_c, zero_s, zero_s, False, "scan_ctx")

    hn, xa, ga, u, v, gb = _proj(x[0], mods, 0, norm_g, w_in_full, 5, "proj")
    xaz, xcz, af, uf, ab, ub = _lru_gates_fwd(xa, conv_wz, conv_bz, wcat, bcat, lamcat, "lru_gates")
    hf, hb, pf, pb, _, _ = _scan(af, uf, ab, ub, hf0, hb0, False, "scan")
    ys = _sgu_fwd(u, v, sgu_ln_g, sgu_ln_b, sgu_wb, sgu_bt)

    w_out_full = _late_gather_wait(wo_land, wo_ss, wo_rs, "w_out", ys, "late_gather_wait_w_out")
    (loss_part, dfg, dgx, dxn, y, do, dga, dgb, dyl_z, dys) = _out_fwd_bwd(
        hf, hb, ga, gb, ys, x[0], loss_target[0], mods, final_g2, w_out_full)
    g_w_out_part = _tn_matmul(y, [do], None, "grad_w_out")

    du, dv, d_sgu_w, d_sgu_b, d_ln_g, d_ln_b = _sgu_bwd(u, v, dys, sgu_ln_g, sgu_ln_b, sgu_wb, sgu_bt)
    lb, lf, dh0b, dh0f = _scan(ab, dyl_z, af, dyl_z, zero_s, zero_s, True, "scan_adj")
    zw = jnp.zeros((NH, HD, 4 * HD), F32)
    zb = jnp.zeros((NH, 4 * HD), F32)
    zl = jnp.zeros((NH, 2 * HD), F32)
    dxc_z, dwc, dbc, dlc = _lru_gates_bwd(xcz, lf, lb, pf, pb, wcat, bcat, lamcat, zw, zb, zl, "lru_gates_bwd")
    dxa, dcw, dcb = _conv_bwd(dxc_z, xaz, conv_wz, jnp.zeros((4, NH, HD), F32), zero_s, "conv_bwd")

    zc = jnp.zeros((lc * NH, HD), F32)
    dhf_c = lax.dynamic_update_slice(zc, dh0f, ((lc - 1) * NH, 0))
    dhb_c = lax.dynamic_update_slice(zc, dh0b, (0, 0))
    lb_c, lf_c, _, _ = _scan(ab_c, dhb_c, af_c, dhf_c, zero_s, zero_s, True, "scan_adj_ctx")
    dxc_zc, dwc, dbc, dlc = _lru_gates_bwd(xcz_c, lf_c, lb_c, pf_c, pb_c, wcat, bcat, lamcat, dwc, dbc, dlc,
                                           "lru_gates_bwd_ctx")
    dxa_c, dcw, dcb = _conv_bwd(dxc_zc, xaz_c, conv_wz, dcw, dcb, "conv_bwd_ctx")

    dzs = [dxa, dga, du, dv, dgb]
    grad_x, dng, dsc_x, dsh_x = _proj_bwd(dzs, x[0], dxn, mods, 0, norm_g, w_in_full, jnp.zeros((1, D), F32), "proj_bwd")
    dng, dsc_c, dsh_c = _proj_bwd([dxa_c], ctx[0], None, mods, 1, norm_g, w_in_full, dng, "proj_bwd_ctx")
    g_w_in_part = _tn_matmul(hn, dzs, (hn_c, dxa_c), "grad_w_in")

    dmx = jnp.concatenate([dsh_x, dsc_x, dgx], axis=0)
    dmc = jnp.concatenate([dsh_c, dsc_c, jnp.zeros((1, D), F32)], axis=0)
    slot = jnp.concatenate([dmx, jnp.zeros((1, D), F32)], axis=0)
    slots = lax.dynamic_update_slice(jnp.zeros((32, D), F32), slot, (4 * dev, 0))
    vecs = jnp.concatenate([dfg, dng, dcb.reshape(1, D), d_ln_g, d_ln_b, dcw.reshape(4, D), dmc,
                            jnp.zeros((4, D), F32), slots], axis=0)
    d_sgu_w4 = d_sgu_w.reshape(4, 256, HD).transpose(1, 0, 2).reshape(256, 4 * HD)
    pad8 = lambda a: jnp.pad(a, ((0, 8 - a.shape[0]), (0, 4 * HD - a.shape[1])))
    pack = jnp.concatenate([dwc.reshape(NH * HD, 4 * HD), pad8(dbc), pad8(dlc), d_sgu_w4, pad8(d_sgu_b),
                            vecs.reshape(96, 4 * HD), jnp.zeros((8, 4 * HD), F32)], axis=0)
    g_w_in, g_w_out, tot = _reduce_all(g_w_in_part, g_w_out_part, pack)

    g_wc = tot[0:1024].reshape(NH, HD, 4 * HD)
    g_bc = tot[1024:1032]
    g_lc = tot[1032:1040, 0:2 * HD]
    g_sgu_w = tot[1040:1296].reshape(256, 4, HD).transpose(1, 0, 2).reshape(NH, HD, HD)
    g_sgu_b = tot[1296:1304, 0:HD]
    tv = tot[1304:1400].reshape(48, D)
    g_final_g, g_norm_g, g_conv_b, g_ln_g, g_ln_b = tv[0:1], tv[1:2], tv[2:3], tv[3:4], tv[4:5]
    g_conv_w_full = tv[5:9]
    dmc_tot = tv[9:12].reshape(1, 3 * D)
    slots_all = tv[16:48].reshape(8, 4, D)
    dmx_all = slots_all[:, 0:3, :].reshape(8, 3 * D)
    c_all = c_slots.reshape(8, 8, D)[:, 0, :]
    g_lru_wa = jnp.stack([g_wc[:, :, 0:HD], g_wc[:, :, 2 * HD:3 * HD]])
    g_lru_wx = jnp.stack([g_wc[:, :, HD:2 * HD], g_wc[:, :, 3 * HD:4 * HD]])
    g_lru_ba = jnp.stack([g_bc[:, 0:HD], g_bc[:, 2 * HD:3 * HD]])
    g_lru_bx = jnp.stack([g_bc[:, HD:2 * HD], g_bc[:, 3 * HD:4 * HD]])
    g_lam_full = jnp.stack([g_lc[:, 0:HD], g_lc[:, HD:2 * HD]]).reshape(2, D)
    g_conv_w = lax.dynamic_slice(g_conv_w_full, (0, 256 * chip), (4, 256))
    g_lam = lax.dynamic_slice(g_lam_full, (0, 256 * chip), (2, 256))
    dmx_all_j = lax.dynamic_slice(dmx_all, (0, 768 * chip), (8, 768))
    dmc_j = lax.dynamic_slice(dmc_tot, (0, 768 * chip), (1, 768))
    ada_full = _late_gather_wait(ada_land, ada_ss, ada_rs, "ada_w", tot, "late_gather_wait_ada_w")
    g_ada_w, g_ada_b, g_c_ctx = _ada_bwd(c_all, dmx_all_j, dmc_j, dmx_all, dmc_tot, c_ctx2, ada_full)

    big = {
        "ada_w": _adam_big(ada_w[0], g_ada_w, m_ada_w[0], v_ada_w[0], "adam_ada_w"),
        "w_in": _adam_big(w_in[0], g_w_in, m_w_in[0], v_w_in[0], "adam_w_in"),
        "w_out": _adam_big(w_out[0], g_w_out, m_w_out[0], v_w_out[0], "adam_w_out"),
    }
    small_in = {
        "c_ctx": (c_ctx, g_c_ctx, m_c_ctx, v_c_ctx, (1, D)),
        "ada_b": (ada_b, g_ada_b, m_ada_b, v_ada_b, (1, 3 * D)),
        "norm_g": (norm_g, g_norm_g, m_norm_g, v_norm_g, (1, D)),
        "conv_w": (conv_w, g_conv_w, m_conv_w, v_conv_w, (4, 256)),
        "conv_b": (conv_b, g_conv_b, m_conv_b, v_conv_b, (1, D)),
        "lru_wa": (lru_wa, g_lru_wa, m_lru_wa, v_lru_wa, (2 * NH * HD, HD)),
        "lru_ba": (lru_ba, g_lru_ba, m_lru_ba, v_lru_ba, (2 * NH, HD)),
        "lru_wx": (lru_wx, g_lru_wx, m_lru_wx, v_lru_wx, (2 * NH * HD, HD)),
        "lru_bx": (lru_bx, g_lru_bx, m_lru_bx, v_lru_bx, (2 * NH, HD)),
        "lru_lambda": (lru_lambda, g_lam, m_lru_lambda, v_lru_lambda, (2, 256)),
        "sgu_ln_g": (sgu_ln_g, g_ln_g, m_sgu_ln_g, v_sgu_ln_g, (1, D)),
        "sgu_ln_b": (sgu_ln_b, g_ln_b, m_sgu_ln_b, v_sgu_ln_b, (1, D)),
        "sgu_w": (sgu_w, g_sgu_w, m_sgu_w, v_sgu_w, (NH * HD, HD)),
        "sgu_b": (sgu_b, g_sgu_b, m_sgu_b, v_sgu_b, (NH, HD)),
        "final_g": (final_g, g_final_g, m_final_g, v_final_g, (1, D)),
    }
    names_small = list(small_in)
    res_small = _adam_small([tuple(a.reshape(small_in[k][4]) for a in small_in[k][:4]) for k in names_small])
    full_shapes = {"ada_w": ada_w.shape, "w_in": w_in.shape, "w_out": w_out.shape}
    grads, deltas, new_m, new_v = {}, {}, {}, {}
    for k in ("ada_w", "w_in", "w_out"):
        g = {"ada_w": g_ada_w, "w_in": g_w_in, "w_out": g_w_out}[k]
        grads[k] = g.reshape(full_shapes[k])
        deltas[k], new_m[k], new_v[k] = (a.reshape(full_shapes[k]) for a in big[k])
    for k, res in zip(names_small, res_small):
        shape = small_in[k][0].shape
        grads[k] = small_in[k][1].reshape(shape)
        deltas[k], new_m[k], new_v[k] = (a.reshape(shape) for a in res)

    loss = lax.psum(loss_part[0, 0], ("x", "y", "c"))
    order = ["c_ctx", "ada_w", "ada_b", "norm_g", "w_in", "conv_w", "conv_b", "lru_wa", "lru_ba", "lru_wx", "lru_bx",
             "lru_lambda", "sgu_ln_g", "sgu_ln_b", "sgu_w", "sgu_b", "w_out", "final_g"]
    return (loss, grad_x.reshape(x.shape), *[grads[k] for k in order], *[deltas[k] for k in order],
            *[new_m[k] for k in order], *[new_v[k] for k in order])
```

```python
import functools

import jax
import jax.numpy as jnp
from jax import lax
from jax.experimental import pallas as pl
from jax.experimental.pallas import tpu as pltpu

F32 = jnp.float32
BF16 = jnp.bfloat16

D = 1024
NH = 8
HD = 128
NCHIP = 4
T = 256
NORM_EPS = 1e-6
LN_EPS = 1e-5
LRU_C = 8.0
ADAM_LR = 0.001
ADAM_B1 = 0.9
ADAM_B2 = 0.999
ADAM_EPS = 1e-08
ADAM_WD = 0.01
ADAM_STEP = 10

VMEM = pl.BlockSpec(memory_space=pltpu.VMEM)
ANY = pl.BlockSpec(memory_space=pl.ANY)
MESH = pl.DeviceIdType.MESH


def _cp(n_grid=0, vmem_mb=None):
    kw = {}
    if n_grid:
        kw["dimension_semantics"] = ("arbitrary",) * n_grid
    if vmem_mb:
        kw["vmem_limit_bytes"] = vmem_mb << 20
    return pltpu.CompilerParams(**kw)


def _sigmoid(x):
    return 1.0 / (1.0 + jnp.exp(-x))


def _silu_and_grad(x):
    s = _sigmoid(x)
    return x * s, s * (1.0 + x * (1.0 - s))


_GELU_K = 0.7978845608028654
_GELU_C = 0.044715


def _gelu_and_grad(x):
    x2 = x * x
    th = jnp.tanh(_GELU_K * (x + _GELU_C * x * x2))
    g = 0.5 * x * (1.0 + th)
    dg = 0.5 * (1.0 + th) + 0.5 * x * (1.0 - th * th) * (_GELU_K * (1.0 + 3.0 * _GELU_C * x2))
    return g, dg


def _softplus(x):
    return jnp.maximum(x, 0.0) + jnp.log1p(jnp.exp(-jnp.abs(x)))


def _lru_gate(pre, lam_row, d):
    r = _sigmoid(pre[:, 256 * d:256 * d + HD])
    gi = _sigmoid(pre[:, 256 * d + HD:256 * d + 2 * HD])
    lam = lam_row[:, HD * d:HD * d + HD]
    sp = _softplus(-lam)
    la = (-LRU_C) * r * sp
    a = jnp.exp(la)
    x2 = 2.0 * la
    m2 = jnp.where(x2 > -1e-3, -x2 * (1.0 + 0.5 * x2), 1.0 - a * a)
    mult = jnp.sqrt(m2)
    return r, gi, lam, sp, a, mult


def _dot(a, b):
    return jnp.dot(a, b, preferred_element_type=F32)


def _dot_tn(a, b):
    return lax.dot_general(a, b, (((0,), (0,)), ((), ())), preferred_element_type=F32)


def _dot_nt(a, b):
    return lax.dot_general(a, b, (((1,), (1,)), ((), ())), preferred_element_type=F32)


def _mo(v, m):
    return v if isinstance(v, int) else pl.multiple_of(v, m)


def _zrows(h, n):
    return pl.ds(h, n, stride=NH)


def _gather_in(c, c_ctx, ada_w, ada_b_j, w_in, w_out, smalls):
    specs = [
        ((64, 256), F32, lambda r, jj, cc: r.at[pl.ds(_mo(16 * jj + 8 * cc, 8), 8), :]),
        ((D, 5120), BF16, lambda r, jj, cc: r.at[pl.ds(_mo(512 * cc, 16), 512), pl.ds(_mo(1280 * jj, 128), 1280)]),
    ]
    halves = [lambda r, cc: r.at[pl.ds(_mo(8 * cc, 8), 8), :],
              lambda r, cc: r.at[pl.ds(_mo(512 * cc, 16), 512), :]]
    na = len(specs)
    n_sem = 6 * na + 10

    def body(c_ref, cc_ref, ada_ref, adab_ref, win_ref, wout_ref, sm_ref,
             mods_o, call_o, sm_o, win_o, wol_o, adal_o, s_win, s_ada, cslot, lhs, mbuf,
             send_sems, recv_sems, local_sems):
        x, y, c = lax.axis_index("x"), lax.axis_index("y"), lax.axis_index("c")
        j = 2 * x + y
        dev = 2 * j + c
        sib = (x, y, 1 - c)
        chips = [(1 - x, y), (x, 1 - y), (1 - x, 1 - y)]
        cj = [2 * cx + cy for cx, cy in chips]
        outs = [sm_o, win_o]
        srcs = [sm_ref, s_win]

        def copy(idx, src, dst, to):
            return pltpu.make_async_remote_copy(src_ref=src, dst_ref=dst, send_sem=send_sems.at[idx],
                                                recv_sem=recv_sems.at[idx], device_id=to, device_id_type=MESH)

        sends = []

        def start(cp):
            cp.start()
            sends.append(cp)

        cslot[...] = jnp.zeros_like(cslot)
        cslot[0:1, :] = c_ref[...]
        my_slot = pl.ds(_mo(8 * dev, 8), 8)
        others = [sib] + [(*chips[k], c) for k in range(3)] + [(*chips[k], 1 - c) for k in range(3)]
        other_dev = [dev + 1 - 2 * c] + [2 * cj[k] + c for k in range(3)] + [2 * cj[k] + 1 - c for k in range(3)]
        base = 6 * na
        for r in range(7):
            start(copy(base + r, cslot, call_o.at[my_slot, :], others[r]))
        call_o[my_slot, :] = cslot[...]

        s_win[...] = win_ref[...].astype(BF16)
        local = []
        for a in range(na):
            for cc in range(2):
                lc = pltpu.make_async_copy(halves[a](srcs[a], cc), specs[a][2](outs[a], j, cc), local_sems.at[2 * a + cc])
                lc.start()
                local.append(lc)
        for k in range(3):
            for a in range(na):
                start(copy(6 * a + k, halves[a](srcs[a], c), specs[a][2](outs[a], j, c), (*chips[k], c)))
        wol_o[pl.ds(_mo(512 * j, 16), 512), :] = wout_ref[...].astype(BF16)
        s_ada[...] = ada_ref[...].astype(BF16)
        lc = pltpu.make_async_copy(s_ada, adal_o.at[:, pl.ds(_mo(768 * j, 128), 768)], local_sems.at[2 * na])
        lc.start()
        local.append(lc)

        for r in range(7):
            slot = call_o.at[pl.ds(_mo(8 * other_dev[r], 8), 8), :]
            copy(base + r, slot, slot, sib).wait_recv()
        lhs[...] = jnp.zeros_like(lhs)
        for b in range(8):
            cv = call_o[8 * b:8 * b + 1, :]
            lhs[b:b + 1, :] = cv * _sigmoid(cv)
        cv = cc_ref[...]
        lhs[8:9, :] = cv * _sigmoid(cv)
        mbuf[j] = _dot(lhs[...].astype(BF16), s_ada[...]) + adab_ref[...]
        for k in range(3):
            start(copy(base + 7 + k, mbuf.at[j], mbuf.at[j], (*chips[k], c)))
        for k in range(3):
            copy(base + 7 + k, mbuf.at[cj[k]], mbuf.at[cj[k]], sib).wait_recv()
        mods_o[...] = jnp.zeros_like(mods_o)
        for jj in range(NCHIP):
            mods_o[0:1, 768 * jj:768 * jj + 768] = mbuf[jj, pl.ds(dev, 1), :]
            mods_o[1:2, 768 * jj:768 * jj + 768] = mbuf[jj, 8:9, :]

        for k in range(3):
            for a in range(na):
                reg = specs[a][2](outs[a], cj[k], c)
                copy(6 * a + k, reg, reg, sib).wait_recv()
                start(copy(6 * a + 3 + k, reg, reg, sib))
        for k in range(3):
            for a in range(na):
                reg = specs[a][2](outs[a], cj[k], 1 - c)
                copy(6 * a + 3 + k, reg, reg, sib).wait_recv()
        for cp in sends:
            cp.wait_send()
        for lc in local:
            lc.wait()

    out_shape = (jax.ShapeDtypeStruct((8, 3 * D), F32), jax.ShapeDtypeStruct((64, D), F32),
                 jax.ShapeDtypeStruct(specs[0][0], F32), jax.ShapeDtypeStruct(specs[1][0], BF16),
                 jax.ShapeDtypeStruct((2048, D), BF16), jax.ShapeDtypeStruct((D, 3 * D), BF16))
    return pl.pallas_call(
        body, name="gather_in", out_shape=out_shape,
        in_specs=[VMEM] * 7, out_specs=(VMEM,) * 6,
        scratch_shapes=[pltpu.VMEM((D, 1280), BF16), pltpu.VMEM((D, 768), BF16), pltpu.VMEM((8, D), F32),
                        pltpu.VMEM((16, D), F32), pltpu.VMEM((NCHIP, 16, 768), F32),
                        pltpu.SemaphoreType.DMA((n_sem,)), pltpu.SemaphoreType.DMA((n_sem,)),
                        pltpu.SemaphoreType.DMA((2 * na + 1,))],
        compiler_params=_cp(vmem_mb=56),
    )(c, c_ctx, ada_w, ada_b_j, w_in, w_out, smalls)


HBM = pl.BlockSpec(memory_space=pltpu.HBM)
SEM = pl.BlockSpec(memory_space=pltpu.SEMAPHORE)


def _late_gather_regions(x, y, c):
    chips = [(1 - x, y), (x, 1 - y), (1 - x, 1 - y)]
    wo_reg = lambda r, jj, cc: r.at[pl.ds(_mo(512 * jj + 256 * cc, 16), 256), :]
    ada_reg = lambda r, jj, cc: r.at[pl.ds(_mo(512 * cc, 16), 512), pl.ds(_mo(768 * jj, 128), 768)]
    return chips, wo_reg, ada_reg


def _late_gather_start(wo_land, ada_land):
    def body(wol_ref, adal_ref, wo_ss, wo_rs, ada_ss, ada_rs, wol_thru, adal_thru, token):
        x, y, c = lax.axis_index("x"), lax.axis_index("y"), lax.axis_index("c")
        j = 2 * x + y
        chips, wo_reg, ada_reg = _late_gather_regions(x, y, c)
        for k in range(3):
            for cc in range(2):
                pltpu.make_async_remote_copy(src_ref=wo_reg(wol_ref, j, c), dst_ref=wo_reg(wol_ref, j, c),
                                             send_sem=wo_ss.at[2 * k + cc], recv_sem=wo_rs.at[2 * k + c],
                                             device_id=(*chips[k], cc), device_id_type=MESH).start()
        for k in range(3):
            for cc in range(2):
                pltpu.make_async_remote_copy(src_ref=ada_reg(adal_ref, j, c), dst_ref=ada_reg(adal_ref, j, c),
                                             send_sem=ada_ss.at[2 * k + cc], recv_sem=ada_rs.at[2 * k + c],
                                             device_id=(*chips[k], cc), device_id_type=MESH).start()
        token[...] = jnp.zeros_like(token)

    sems = pltpu.SemaphoreType.DMA((6,))
    return pl.pallas_call(
        body, name="late_gather_start",
        out_shape=(sems, sems, sems, sems, pltpu.HBM(wo_land.shape, BF16), pltpu.HBM(ada_land.shape, BF16),
                   jax.ShapeDtypeStruct((8, 128), F32)),
        in_specs=(HBM, HBM), out_specs=(SEM, SEM, SEM, SEM, HBM, HBM, VMEM), input_output_aliases={0: 4, 1: 5},
        compiler_params=pltpu.CompilerParams(has_side_effects=pltpu.SideEffectType.DATAFLOW_SIDE_EFFECTING),
    )(pltpu.with_memory_space_constraint(wo_land, pltpu.HBM), pltpu.with_memory_space_constraint(ada_land, pltpu.HBM))


def _late_gather_wait(land, send_sems, recv_sems, which, after, name):
    def body(land_ref, ss, rs, after_ref, land_out):
        x, y, c = lax.axis_index("x"), lax.axis_index("y"), lax.axis_index("c")
        j = 2 * x + y
        chips, wo_reg, ada_reg = _late_gather_regions(x, y, c)
        reg = wo_reg if which == "w_out" else ada_reg
        for k in range(3):
            kj = 2 * chips[k][0] + chips[k][1]
            for cc in range(2):
                cp = pltpu.make_async_remote_copy(src_ref=reg(land_ref, j, c), dst_ref=reg(land_ref, kj, cc),
                                                  send_sem=ss.at[2 * k + cc], recv_sem=rs.at[2 * k + cc],
                                                  device_id=(*chips[k], cc), device_id_type=MESH)
                cp.wait_send()
                cp.wait_recv()

    return pl.pallas_call(
        body, name=name, out_shape=pltpu.HBM(land.shape, land.dtype),
        in_specs=(HBM, SEM, SEM, ANY), out_specs=HBM, input_output_aliases={0: 0},
        compiler_params=pltpu.CompilerParams(has_side_effects=pltpu.SideEffectType.DATAFLOW_SIDE_EFFECTING),
    )(land, send_sems, recv_sems, after)


RCHUNK = 16


def _grads_reduce(hn, dzs, hn_c, dxa_c, g_w_out, pack):
    rp = pack.shape[0]
    hp = rp // 2
    assert hp % RCHUNK == 0
    wi_w = 1280
    lx, lc = hn.shape[0], hn_c.shape[0]
    lt = lx + lc
    n_dz = len(dzs)

    def body(*refs):
        hn_hbm, dz_hbm = refs[0], refs[1:1 + n_dz]
        hnc_hbm, dxac_hbm, wo_hbm, pk_hbm, wi_out, wo_out, pk_out = refs[1 + n_dz:8 + n_dz]
        (hn_mine, hn_other, dzbuf, wi_other, wi_mine, wi_recv, wi_send, wi_rb,
         wo_mine, wo_recv, wo_send, wo_rb, wo_own, pk_mine, pk_recv, pk_send, pk_rb, pk_own,
         send_sems, recv_sems, local_sems) = refs[8 + n_dz:]
        x, y, c = lax.axis_index("x"), lax.axis_index("y"), lax.axis_index("c")
        j = 2 * x + y
        sib = (x, y, 1 - c)
        chips = [(1 - x, y), (x, 1 - y), (1 - x, 1 - y)]
        cj = [2 * cx + cy for cx, cy in chips]
        slabs = cj + [j]

        def copy(k, src, dst, to):
            return pltpu.make_async_remote_copy(src_ref=src, dst_ref=dst, send_sem=send_sems.at[k],
                                                recv_sem=recv_sems.at[k], device_id=to, device_id_type=MESH)

        def local(k, src, dst):
            cp = pltpu.make_async_copy(src, dst, local_sems.at[k])
            cp.start()
            return cp

        rows_half = lambda r, cc, n: r.at[pl.ds(_mo(cc * n, 16), n), :]
        cols_half = lambda r, cc, n: r.at[:, pl.ds(_mo(cc * n, 128), n)]
        pk_piece = lambda r, cc, jj: r.at[pl.ds(_mo(cc * hp, 16), hp), pl.ds(_mo(jj * 128, 128), 128)]

        sends = []

        def start(cp):
            cp.start()
            sends.append(cp)

        def dz_pieces(s):
            g0 = wi_w * s
            k0, off0 = g0 // D, g0 % D
            w0 = min(D - off0, wi_w)
            pieces = [(k0, off0, w0, 0)]
            if w0 < wi_w:
                pieces.append((k0 + 1, 0, wi_w - w0, w0))
            return pieces

        def dz_copies(s):
            cps = []
            for q, (k, off, w, dst) in enumerate(dz_pieces(s)):
                cps.append(pltpu.make_async_copy(dz_hbm[k].at[:, pl.ds(off, w)], dzbuf.at[pl.ds(0, lx), pl.ds(dst, w)],
                                                 local_sems.at[11 + q]))
            if s == 0:
                cps.append(pltpu.make_async_copy(dxac_hbm, dzbuf.at[pl.ds(lx, lc), pl.ds(0, D)], local_sems.at[13]))
            return cps

        def dz_load(sl):
            for s in range(NCHIP):
                @pl.when(sl == s)
                def _():
                    if s == 0:
                        dzbuf[pl.ds(lx, lc), pl.ds(D, wi_w - D)] = jnp.zeros((lc, wi_w - D), BF16)
                    else:
                        dzbuf[pl.ds(lx, lc), :] = jnp.zeros((lc, wi_w), BF16)
                    for cp in dz_copies(s):
                        cp.start()

        def dz_wait(sl):
            for s in range(NCHIP):
                @pl.when(sl == s)
                def _():
                    for cp in dz_copies(s):
                        cp.wait()

        l_pk = local(0, rows_half(pk_hbm, c, hp), pk_mine)
        start(copy(0, rows_half(pk_hbm, 1 - c, hp), pk_recv, sib))
        l_wo = local(1, cols_half(wo_hbm, c, 512), wo_mine)
        start(copy(1, cols_half(wo_hbm, 1 - c, 512), wo_recv, sib))
        hn_loads = [local(2, hn_hbm.at[:, pl.ds(_mo(c * 512, 128), 512)], hn_mine.at[pl.ds(0, lx), :]),
                    local(3, hnc_hbm.at[:, pl.ds(_mo(c * 512, 128), 512)], hn_mine.at[pl.ds(lx, lc), :]),
                    local(4, hn_hbm.at[:, pl.ds(_mo((1 - c) * 512, 128), 512)], hn_other.at[pl.ds(0, lx), :]),
                    local(5, hnc_hbm.at[:, pl.ds(_mo((1 - c) * 512, 128), 512)], hn_other.at[pl.ds(lx, lc), :])]
        dz_load(slabs[0])

        def pair_sum(mine, recv, send, nrows, keep):
            def step(i, carry):
                rows = pl.ds(_mo(i * RCHUNK, RCHUNK), RCHUNK)
                s = mine[rows, :] + recv[rows, :].astype(F32)
                if keep:
                    mine[rows, :] = s
                if send is not None:
                    send[rows, :] = s.astype(BF16)
                return carry
            lax.fori_loop(0, nrows // RCHUNK, step, 0)

        def chip_sum(own, rb, nrows):
            def step(i, carry):
                rows = pl.ds(_mo(i * RCHUNK, RCHUNK), RCHUNK)
                own[rows, :] = (own[rows, :] + rb[0, rows, :].astype(F32)
                                + rb[1, rows, :].astype(F32) + rb[2, rows, :].astype(F32))
                return carry
            lax.fori_loop(0, nrows // RCHUNK, step, 0)

        p1 = [None] * NCHIP

        def slab_matmuls(s):
            if s >= 2:
                p1[s - 2].wait_send()
            dz_wait(slabs[s])
            wi_other[s % 2] = _dot_tn(hn_other[...], dzbuf[...]).astype(BF16)
            p1[s] = copy(2 + s, wi_other.at[s % 2], wi_recv.at[s], sib)
            p1[s].start()
            wi_mine[s % 2] = _dot_tn(hn_mine[...], dzbuf[...])
            if s + 1 < NCHIP:
                dz_load(slabs[s + 1])

        def slab_finish(s):
            copy(2 + s, wi_recv.at[s], wi_recv.at[s], sib).wait_recv()
            if s < 3:
                pair_sum(wi_mine.at[s % 2], wi_recv.at[s], wi_send.at[s], 512, False)
                start(copy(12 + s, wi_send.at[s], wi_rb.at[s], (*chips[s], c)))
            else:
                pair_sum(wi_mine.at[s % 2], wi_recv.at[s], None, 512, True)

        for cp in hn_loads:
            cp.wait()
        slab_matmuls(0)

        l_pk.wait()
        copy(0, pk_recv, pk_recv, sib).wait_recv()
        pair_sum(pk_mine, pk_recv, pk_send, hp, True)
        for k in range(3):
            start(copy(6 + k, pk_send.at[:, pl.ds(_mo(cj[k] * 128, 128), 128)], pk_rb.at[k], (*chips[k], c)))
        l_pk_own = local(6, pk_mine.at[:, pl.ds(_mo(j * 128, 128), 128)], pk_own)

        l_wo.wait()
        copy(1, wo_recv, wo_recv, sib).wait_recv()
        pair_sum(wo_mine, wo_recv, wo_send, 2048, True)
        for k in range(3):
            start(copy(9 + k, wo_send.at[pl.ds(_mo(cj[k] * 512, 16), 512), :], wo_rb.at[k], (*chips[k], c)))
        l_wo_own = local(7, wo_mine.at[pl.ds(_mo(j * 512, 16), 512), :], wo_own)

        slab_matmuls(1)
        slab_finish(0)

        l_pk_own.wait()
        for k in range(3):
            copy(6 + k, pk_rb.at[k], pk_rb.at[k], sib).wait_recv()
        chip_sum(pk_own, pk_rb, hp)
        l_pk_out = local(8, pk_own, pk_piece(pk_out, c, j))
        start(copy(15, pk_own, pk_piece(pk_out, c, j), sib))
        for k in range(3):
            start(copy(16 + k, pk_own, pk_piece(pk_out, c, j), (*chips[k], c)))

        slab_matmuls(2)
        slab_finish(1)
        slab_matmuls(3)
        slab_finish(2)

        l_wo_own.wait()
        for k in range(3):
            copy(9 + k, wo_rb.at[k], wo_rb.at[k], sib).wait_recv()
        chip_sum(wo_own, wo_rb, 512)
        l_wo_out = local(9, wo_own, cols_half(wo_out, c, 512))
        start(copy(22, wo_own, cols_half(wo_out, c, 512), sib))

        for k in range(3):
            reg = pk_piece(pk_out, c, cj[k])
            copy(16 + k, reg, reg, sib).wait_recv()
            start(copy(19 + k, reg, reg, sib))

        slab_finish(3)
        for k in range(3):
            copy(12 + k, wi_rb.at[k], wi_rb.at[k], sib).wait_recv()
        chip_sum(wi_mine.at[1], wi_rb, 512)
        l_wi_out = local(10, wi_mine.at[1], rows_half(wi_out, c, 512))
        start(copy(23, wi_mine.at[1], rows_half(wi_out, c, 512), sib))

        reg = pk_piece(pk_out, 1 - c, j)
        copy(15, reg, reg, sib).wait_recv()
        for k in range(3):
            reg = pk_piece(pk_out, 1 - c, cj[k])
            copy(19 + k, reg, reg, sib).wait_recv()
        reg = cols_half(wo_out, 1 - c, 512)
        copy(22, reg, reg, sib).wait_recv()
        reg = rows_half(wi_out, 1 - c, 512)
        copy(23, reg, reg, sib).wait_recv()
        for cp in sends + p1[2:]:
            cp.wait_send()
        for cp in (l_pk_out, l_wo_out, l_wi_out):
            cp.wait()

    return pl.pallas_call(
        body, name="grads_reduce",
        out_shape=(jax.ShapeDtypeStruct((D, wi_w), F32), jax.ShapeDtypeStruct((512, D), F32),
                   jax.ShapeDtypeStruct(pack.shape, F32)),
        in_specs=[ANY] * (5 + n_dz), out_specs=(ANY,) * 3,
        scratch_shapes=[
            pltpu.VMEM((lt, 512), BF16), pltpu.VMEM((lt, 512), BF16), pltpu.VMEM((lt, wi_w), BF16),
            pltpu.VMEM((2, 512, wi_w), BF16), pltpu.VMEM((2, 512, wi_w), F32), pltpu.VMEM((4, 512, wi_w), BF16),
            pltpu.VMEM((3, 512, wi_w), BF16), pltpu.VMEM((3, 512, wi_w), BF16),
            pltpu.VMEM((2048, 512), F32), pltpu.VMEM((2048, 512), F32), pltpu.VMEM((2048, 512), BF16),
            pltpu.VMEM((3, 512, 512), BF16), pltpu.VMEM((512, 512), F32),
            pltpu.VMEM((hp, 512), F32), pltpu.VMEM((hp, 512), F32), pltpu.VMEM((hp, 512), BF16),
            pltpu.VMEM((3, hp, 128), BF16), pltpu.VMEM((hp, 128), F32),
            pltpu.SemaphoreType.DMA((24,)), pltpu.SemaphoreType.DMA((24,)), pltpu.SemaphoreType.DMA((14,))],
        compiler_params=_cp(vmem_mb=56),
    )(hn, *dzs, hn_c, dxa_c, g_w_out, pack)


def _ada_bwd(c_all, dmx_all_j, dmc_j, dmx_all, dmc, c_ctx, ada_w_full):
    def body(c_ref, dmxj_ref, dmcj_ref, dmx_ref, dmc_ref, cc_ref, w_ref, gw_ref, gb_ref, gc_ref, lhs, rhs, dm8):
        lhs[...] = jnp.zeros_like(lhs)
        rhs[...] = jnp.zeros_like(rhs)
        cv = c_ref[...]
        lhs[0:8, :] = cv * _sigmoid(cv)
        cc = cc_ref[...]
        a_c, da_c = _silu_and_grad(cc)
        lhs[8:9, :] = a_c
        rhs[0:8, :] = dmxj_ref[...]
        rhs[8:9, :] = dmcj_ref[...]
        gw_ref[...] = _dot_tn(lhs[...].astype(BF16), rhs[...].astype(BF16))
        gb_ref[...] = jnp.sum(dmx_ref[...], axis=0, keepdims=True) + dmc_ref[...]
        dm8[...] = jnp.zeros_like(dm8)
        dm8[0:1, :] = dmc_ref[...]
        da = _dot_nt(dm8[...].astype(BF16), w_ref[...])
        gc_ref[...] = da[0:1, :] * da_c

    return pl.pallas_call(
        body, name="ada_bwd",
        out_shape=(jax.ShapeDtypeStruct((D, 768), F32), jax.ShapeDtypeStruct((1, 3 * D), F32),
                   jax.ShapeDtypeStruct((1, D), F32)),
        in_specs=[VMEM] * 7, out_specs=(VMEM,) * 3,
        scratch_shapes=[pltpu.VMEM((16, D), F32), pltpu.VMEM((16, 768), F32), pltpu.VMEM((8, 3 * D), F32)],
        compiler_params=_cp(vmem_mb=32),
    )(c_all, dmx_all_j, dmc_j, dmx_all, dmc, c_ctx, ada_w_full)


def _proj(x, mods, mrow, norm_g, w_full, nk, name):
    lx = x.shape[0]
    n = lx // T

    def body(x_ref, sh_ref, sc_ref, ng_ref, *rest):
        w_refs, hn_ref, z_refs = rest[:nk], rest[nk], rest[nk + 1:]
        xv = x_ref[...]
        r = lax.rsqrt(jnp.mean(xv * xv, axis=-1, keepdims=True) + NORM_EPS)
        hn = (xv * r) * ng_ref[...] * (1.0 + sc_ref[mrow:mrow + 1, :]) + sh_ref[mrow:mrow + 1, :]
        hb = hn.astype(BF16)
        hn_ref[...] = hb
        for k in range(nk):
            z_refs[k][...] = _dot(hb, w_refs[k][...])

    row = pl.BlockSpec((T, D), lambda i: (i, 0))
    in_specs = [row, pl.BlockSpec((8, D), lambda i: (0, 0)), pl.BlockSpec((8, D), lambda i: (0, 1)),
                pl.BlockSpec((1, D), lambda i: (0, 0))]
    in_specs += [pl.BlockSpec((D, D), lambda i, k=k: (0, k)) for k in range(nk)]
    out_shape = (jax.ShapeDtypeStruct((lx, D), BF16),) + tuple(jax.ShapeDtypeStruct((lx, D), F32) for _ in range(nk))
    return pl.pallas_call(
        body, name=name, grid=(n,), out_shape=out_shape, in_specs=in_specs, out_specs=(row,) * (nk + 1),
        compiler_params=_cp(1, vmem_mb=56),
    )(x, mods, mods, norm_g, *([w_full] * nk))


def _halo_specs(lx):
    last = lx // 8 - 1
    return [pl.BlockSpec((T, D), lambda i: (i, 0)),
            pl.BlockSpec((8, D), lambda i: (jnp.maximum(i * (T // 8) - 1, 0), 0)),
            pl.BlockSpec((8, D), lambda i: (jnp.minimum((i + 1) * (T // 8), last), 0))]


def _zhalo_specs(lx):
    last = lx // 8 - 1
    return [pl.BlockSpec((T * NH, HD), lambda i: (i, 0)),
            pl.BlockSpec((8 * NH, HD), lambda i: (jnp.maximum(i * (T // 8) - 1, 0), 0)),
            pl.BlockSpec((8 * NH, HD), lambda i: (jnp.minimum((i + 1) * (T // 8), last), 0))]


ZT = pl.BlockSpec((T * NH, HD), lambda i: (i, 0))
CONV_CHUNK = 32


def _lru_gates_fwd(xa, conv_wz, conv_bz, wcat, bcat, lamcat, name):
    lx = xa.shape[0]
    n = lx // T

    def body(xm, xp, xn, cw, cb, w_ref, b_ref, lam_ref, xaz_o, xcz_o, af_o, uf_o, ab_o, ub_o, pad):
        i = pl.program_id(0)
        pmask = jnp.where(i == 0, 0.0, 1.0)
        nmask = jnp.where(i == n - 1, 0.0, 1.0)
        for h in range(NH):
            cols = slice(HD * h, HD * h + HD)
            pad[_zrows(h, 8), :] = xp[:, cols] * pmask
            pad[pl.ds(8 * NH + h, T, stride=NH), :] = xm[:, cols]
            pad[pl.ds((T + 8) * NH + h, 8, stride=NH), :] = xn[:, cols] * nmask
        xaz_o[...] = pad[pl.ds(8 * NH, T * NH), :]

        def conv_chunk(ci, carry):
            base = pl.multiple_of(ci * (CONV_CHUNK * NH), CONV_CHUNK * NH)
            acc = None
            for k in range(4):
                sl = pad[pl.ds(base + (7 + k) * NH, CONV_CHUNK * NH), :].reshape(CONV_CHUNK, NH, HD)
                term = sl * cw[k][None]
                acc = term if acc is None else acc + term
            acc = acc + cb[...][None]
            xcz_o[pl.ds(base, CONV_CHUNK * NH), :] = acc.reshape(CONV_CHUNK * NH, HD)
            return carry
        lax.fori_loop(0, T // CONV_CHUNK, conv_chunk, 0)

        outs = ((af_o, uf_o), (ab_o, ub_o))
        for h in range(NH):
            xch = xcz_o[_zrows(h, T), :]
            pre = _dot(xch.astype(BF16), w_ref[h]) + b_ref[h:h + 1, :]
            for d in range(2):
                _, gi, _, _, a, mult = _lru_gate(pre, lam_ref[h:h + 1, :], d)
                outs[d][0][_zrows(h, T), :] = a
                outs[d][1][_zrows(h, T), :] = mult * gi * xch

    full = lambda shape: pl.BlockSpec(shape, lambda i: (0,) * len(shape))
    in_specs = _halo_specs(lx) + [full((4, NH, HD)), full((NH, HD)), full((NH, HD, 4 * HD)), full((NH, 4 * HD)),
                                  full((NH, 2 * HD))]
    zs = jax.ShapeDtypeStruct((lx * NH, HD), F32)
    return pl.pallas_call(
        body, name=name, grid=(n,), out_shape=(zs,) * 6, in_specs=in_specs, out_specs=(ZT,) * 6,
        scratch_shapes=[pltpu.VMEM(((T + 16) * NH, HD), F32)],
        compiler_params=_cp(1, vmem_mb=48),
    )(xa, xa, xa, conv_wz, conv_bz, wcat, bcat, lamcat)


def _scan(a_up, x_up, a_dn, x_dn, s_up, s_dn, post, name):
    lx = a_up.shape[0] // NH
    n = lx // T

    def body(au, xu, ad, xd, su0, sd0, *rest):
        if post:
            ou, od, fu, fd, carry = rest
        else:
            ou, od, pu, pd, fu, fd, carry = rest
        i = pl.program_id(0)

        @pl.when(i == 0)
        def _():
            carry[0] = su0[...]
            carry[1] = sd0[...]

        def step(k, s):
            su, sd = s
            ru = pl.ds(_mo(k * NH, NH), NH)
            rd = pl.ds(_mo((T - 1 - k) * NH, NH), NH)
            if post:
                vu = xu[ru, :] + su
                vd = xd[rd, :] + sd
                ou[ru, :] = vu
                od[rd, :] = vd
                return au[ru, :] * vu, ad[rd, :] * vd
            pu[ru, :] = su
            pd[rd, :] = sd
            vu = au[ru, :] * su + xu[ru, :]
            vd = ad[rd, :] * sd + xd[rd, :]
            ou[ru, :] = vu
            od[rd, :] = vd
            return vu, vd

        su, sd = lax.fori_loop(0, T, step, (carry[0], carry[1]), unroll=8)
        carry[0] = su
        carry[1] = sd
        fu[...] = su
        fd[...] = sd

    up = pl.BlockSpec((T * NH, HD), lambda i: (i, 0))
    dn = pl.BlockSpec((T * NH, HD), lambda i: (n - 1 - i, 0))
    st = pl.BlockSpec((NH, HD), lambda i: (0, 0))
    zs = jax.ShapeDtypeStruct((lx * NH, HD), F32)
    ss = jax.ShapeDtypeStruct((NH, HD), F32)
    if post:
        out_shape, out_specs = (zs, zs, ss, ss), (up, dn, st, st)
    else:
        out_shape, out_specs = (zs, zs, zs, zs, ss, ss), (up, dn, up, dn, st, st)
    return pl.pallas_call(
        body, name=name, grid=(n,), out_shape=out_shape, in_specs=[up, up, dn, dn, st, st], out_specs=out_specs,
        scratch_shapes=[pltpu.VMEM((2, NH, HD), F32)],
        compiler_params=_cp(1, vmem_mb=48),
    )(a_up, x_up, a_dn, x_dn, s_up, s_dn)


def _sgu_parts(u, v, lng, lnb, w_ref, bt_ref, mixed_s):
    ug, dug = _gelu_and_grad(u)
    vg, dvg = _gelu_and_grad(v)
    mu = jnp.mean(vg, axis=-1, keepdims=True)
    vc = vg - mu
    rstd = lax.rsqrt(jnp.mean(vc * vc, axis=-1, keepdims=True) + LN_EPS)
    vh = vc * rstd
    vn = (vh * lng + lnb).astype(BF16)
    for g in range(NH):
        cols = slice(HD * g, HD * g + HD)
        mixed_s[:, cols] = _dot(w_ref[g], vn[:, cols]) + bt_ref[:, g:g + 1]
    return ug, dug, dvg, rstd, vh, vn


def _sgu_fwd(u, v, ln_g, ln_b, sgu_w, sgu_bt):
    lx = u.shape[0]
    n = lx // HD

    def body(u_ref, v_ref, g_ref, b_ref, w_ref, bt_ref, y_ref, mixed_s):
        ug, _, _, _, _, _ = _sgu_parts(u_ref[...], v_ref[...], g_ref[...], b_ref[...], w_ref, bt_ref, mixed_s)
        y_ref[...] = ug * mixed_s[...]

    row = pl.BlockSpec((HD, D), lambda i: (i, 0))
    vec = pl.BlockSpec((1, D), lambda i: (0, 0))
    return pl.pallas_call(
        body, name="sgu_fwd", grid=(n,), out_shape=jax.ShapeDtypeStruct((lx, D), F32),
        in_specs=[row, row, vec, vec, pl.BlockSpec((NH, HD, HD), lambda i: (0, 0, 0)),
                  pl.BlockSpec((HD, NH), lambda i: (0, 0))],
        out_specs=row, scratch_shapes=[pltpu.VMEM((HD, D), F32)],
        compiler_params=_cp(1),
    )(u, v, ln_g, ln_b, sgu_w, sgu_bt)


def _sgu_bwd(u, v, dys, ln_g, ln_b, sgu_w, sgu_bt):
    lx = u.shape[0]
    n = lx // HD

    def body(u_ref, v_ref, dy_ref, g_ref, b_ref, w_ref, bt_ref, du_ref, dv_ref, dw_ref, db_ref, dg_ref, dbl_ref,
             mixed_s, dvn_s):
        i = pl.program_id(0)

        @pl.when(i == 0)
        def _():
            dw_ref[...] = jnp.zeros_like(dw_ref)
            db_ref[...] = jnp.zeros_like(db_ref)
            dg_ref[...] = jnp.zeros_like(dg_ref)
            dbl_ref[...] = jnp.zeros_like(dbl_ref)

        lng = g_ref[...]
        ug, dug, dvg, rstd, vh, vn = _sgu_parts(u_ref[...], v_ref[...], lng, b_ref[...], w_ref, bt_ref, mixed_s)
        dys_v = dy_ref[...]
        du_ref[...] = (dys_v * mixed_s[...] * dug).astype(BF16)
        dmix = dys_v * ug
        ones = jnp.ones((8, HD), BF16)
        for g in range(NH):
            cols = slice(HD * g, HD * g + HD)
            dm = dmix[:, cols]
            hi = dm.astype(BF16)
            lo = (dm - hi.astype(F32)).astype(BF16)
            dw_ref[g] += _dot_nt(hi, vn[:, cols])
            db_ref[g:g + 1, :] += (_dot_nt(ones, hi) + _dot_nt(ones, lo))[0:1, :]
            dvn_s[:, cols] = _dot_tn(w_ref[g], hi)
        dvn = dvn_s[...]
        dg_ref[...] += jnp.sum(dvn * vh, axis=0, keepdims=True)
        dbl_ref[...] += jnp.sum(dvn, axis=0, keepdims=True)
        dvh = dvn * lng
        dvg_in = rstd * (dvh - jnp.mean(dvh, axis=-1, keepdims=True)
                         - vh * jnp.mean(dvh * vh, axis=-1, keepdims=True))
        dv_ref[...] = (dvg_in * dvg).astype(BF16)

    row = pl.BlockSpec((HD, D), lambda i: (i, 0))
    vec = pl.BlockSpec((1, D), lambda i: (0, 0))
    wsp = pl.BlockSpec((NH, HD, HD), lambda i: (0, 0, 0))
    bsp = pl.BlockSpec((NH, HD), lambda i: (0, 0))
    return pl.pallas_call(
        body, name="sgu_bwd", grid=(n,),
        out_shape=(jax.ShapeDtypeStruct((lx, D), BF16), jax.ShapeDtypeStruct((lx, D), BF16),
                   jax.ShapeDtypeStruct((NH, HD, HD), F32), jax.ShapeDtypeStruct((NH, HD), F32),
                   jax.ShapeDtypeStruct((1, D), F32), jax.ShapeDtypeStruct((1, D), F32)),
        in_specs=[row, row, row, vec, vec, wsp, pl.BlockSpec((HD, NH), lambda i: (0, 0))],
        out_specs=(row, row, wsp, bsp, vec, vec),
        scratch_shapes=[pltpu.VMEM((HD, D), F32), pltpu.VMEM((HD, D), F32)],
        compiler_params=_cp(1),
    )(u, v, dys, ln_g, ln_b, sgu_w, sgu_bt)


def _out_fwd_bwd(hf_z, hb_z, ga, gb, ys, x, tgt, mods, final_g, w_out_full):
    lx = x.shape[0]
    n = lx // T

    def body(hf_ref, hb_ref, ga_ref, gb_ref, ys_ref, x_ref, t_ref, gx_ref, fg_ref, w_ref,
             loss_ref, dfg_ref, dgx_ref, dxn_ref, y_ref, do_ref, dga_ref, dgb_ref, dyl_ref, dys_ref, yl_s):
        i = pl.program_id(0)

        @pl.when(i == 0)
        def _():
            loss_ref[...] = jnp.zeros_like(loss_ref)
            dfg_ref[...] = jnp.zeros_like(dfg_ref)
            dgx_ref[...] = jnp.zeros_like(dgx_ref)

        for h in range(NH):
            yl_s[:, HD * h:HD * h + HD] = hf_ref[_zrows(h, T), :] + hb_ref[_zrows(h, T), :]
        yl = yl_s[...]
        gav = ga_ref[...]
        gbv = gb_ref[...]
        sa, dsa = _silu_and_grad(gav)
        sb, dsb = _silu_and_grad(gbv)
        ysv = ys_ref[...]
        y_ref[:, 0:D] = (yl * sa).astype(BF16)
        y_ref[:, D:2 * D] = (ysv * sb).astype(BF16)
        o = _dot(y_ref[...], w_ref[...])
        gx = gx_ref[0:1, :]
        xnew = x_ref[...] + gx * o
        r2 = lax.rsqrt(jnp.mean(xnew * xnew, axis=-1, keepdims=True) + NORM_EPS)
        xh = xnew * r2
        fg = fg_ref[...]
        err = xh * fg - t_ref[...]
        loss_ref[...] += 0.5 * jnp.sum(jnp.mean(err * err, axis=-1, keepdims=True), axis=0, keepdims=True)
        dout = err * (1.0 / D)
        dfg_ref[...] += jnp.sum(dout * xh, axis=0, keepdims=True)
        dxh = dout * fg
        dxn = r2 * (dxh - xh * jnp.mean(dxh * xh, axis=-1, keepdims=True))
        dxn_ref[...] = dxn
        dgx_ref[...] += jnp.sum(dxn * o, axis=0, keepdims=True)
        do = (dxn * gx).astype(BF16)
        do_ref[...] = do
        dy = _dot_nt(do, w_ref[...])
        dy1 = dy[:, 0:D]
        dy2 = dy[:, D:2 * D]
        dga_ref[...] = (dy1 * yl * dsa).astype(BF16)
        dgb_ref[...] = (dy2 * ysv * dsb).astype(BF16)
        dys_ref[...] = dy2 * sb
        yl_s[...] = dy1 * sa
        for h in range(NH):
            dyl_ref[_zrows(h, T), :] = yl_s[:, HD * h:HD * h + HD]

    row = pl.BlockSpec((T, D), lambda i: (i, 0))
    vec = pl.BlockSpec((1, D), lambda i: (0, 0))
    in_specs = [ZT, ZT, row, row, row, row, row, pl.BlockSpec((8, D), lambda i: (0, 2)), vec,
                pl.BlockSpec((2 * D, D), lambda i: (0, 0))]
    out_shape = (jax.ShapeDtypeStruct((1, 1), F32), jax.ShapeDtypeStruct((1, D), F32), jax.ShapeDtypeStruct((1, D), F32),
                 jax.ShapeDtypeStruct((lx, D), F32), jax.ShapeDtypeStruct((lx, 2 * D), BF16),
                 jax.ShapeDtypeStruct((lx, D), BF16), jax.ShapeDtypeStruct((lx, D), BF16),
                 jax.ShapeDtypeStruct((lx, D), BF16), jax.ShapeDtypeStruct((lx * NH, HD), F32),
                 jax.ShapeDtypeStruct((lx, D), F32))
    out_specs = (pl.BlockSpec((1, 1), lambda i: (0, 0)), vec, vec, row, pl.BlockSpec((T, 2 * D), lambda i: (i, 0)),
                 row, row, row, ZT, row)
    return pl.pallas_call(
        body, name="out_fwd_bwd", grid=(n,), out_shape=out_shape, in_specs=in_specs, out_specs=out_specs,
        scratch_shapes=[pltpu.VMEM((T, D), F32)],
        compiler_params=_cp(1, vmem_mb=56),
    )(hf_z, hb_z, ga, gb, ys, x, tgt, mods, final_g, w_out_full)


def _lru_gates_bwd(xc_z, lf_z, lb_z, pf_z, pb_z, wcat, bcat, lamcat, dw0, db0, dl0, name):
    lx = xc_z.shape[0] // NH
    n = lx // T

    def body(xc_ref, lf_ref, lb_ref, pf_ref, pb_ref, w_ref, b_ref, lam_ref, dw0_ref, db0_ref, dl0_ref,
             dxc_ref, dw_ref, db_ref, dl_ref, dpre_s):
        i = pl.program_id(0)

        @pl.when(i == 0)
        def _():
            dw_ref[...] = dw0_ref[...]
            db_ref[...] = db0_ref[...]
            dl_ref[...] = dl0_ref[...]

        lam_refs = (lf_ref, lb_ref)
        prev_refs = (pf_ref, pb_ref)
        for h in range(NH):
            xch = xc_ref[_zrows(h, T), :]
            xcb = xch.astype(BF16)
            pre = _dot(xcb, w_ref[h]) + b_ref[h:h + 1, :]
            dxc = jnp.zeros((T, HD), F32)
            for d in range(2):
                r, gi, lam, sp, a, mult = _lru_gate(pre, lam_ref[h:h + 1, :], d)
                du = lam_refs[d][_zrows(h, T), :]
                da = du * prev_refs[d][_zrows(h, T), :]
                dgi = du * mult * xch
                dxc = dxc + du * mult * gi
                dmult = du * gi * xch
                dla = da * a - dmult * (a * a) / mult
                dr = dla * ((-LRU_C) * sp)
                dsp = jnp.sum(dla * ((-LRU_C) * r), axis=0, keepdims=True)
                dl_ref[h:h + 1, HD * d:HD * d + HD] += dsp * (-_sigmoid(-lam))
                dpre_s[:, 256 * d:256 * d + HD] = dr * r * (1.0 - r)
                dpre_s[:, 256 * d + HD:256 * d + 2 * HD] = dgi * gi * (1.0 - gi)
            dpre = dpre_s[...]
            dpb = dpre.astype(BF16)
            dw_ref[h] += _dot_tn(xcb, dpb)
            db_ref[h:h + 1, :] += jnp.sum(dpre, axis=0, keepdims=True)
            dxc_ref[_zrows(h, T), :] = dxc + _dot_nt(dpb, w_ref[h])

    full = lambda shape: pl.BlockSpec(shape, lambda i: (0,) * len(shape))
    wsp, bsp, lsp = full((NH, HD, 4 * HD)), full((NH, 4 * HD)), full((NH, 2 * HD))
    return pl.pallas_call(
        body, name=name, grid=(n,),
        out_shape=(jax.ShapeDtypeStruct((lx * NH, HD), F32), jax.ShapeDtypeStruct((NH, HD, 4 * HD), F32),
                   jax.ShapeDtypeStruct((NH, 4 * HD), F32), jax.ShapeDtypeStruct((NH, 2 * HD), F32)),
        in_specs=[ZT] * 5 + [wsp, bsp, lsp, wsp, bsp, lsp], out_specs=(ZT, wsp, bsp, lsp),
        scratch_shapes=[pltpu.VMEM((T, 4 * HD), F32)],
        compiler_params=_cp(1, vmem_mb=48),
    )(xc_z, lf_z, lb_z, pf_z, pb_z, wcat, bcat, lamcat, dw0, db0, dl0)


def _conv_bwd(dxc_z, xa_z, conv_wz, dcw0, dcb0, name):
    lx = dxc_z.shape[0] // NH
    n = lx // T

    def body(dm, dp, dn, xa_ref, cw, dcw0_ref, dcb0_ref, dxa_ref, dcw_ref, dcb_ref, pad, dxa_s):
        i = pl.program_id(0)

        @pl.when(i == 0)
        def _():
            dcw_ref[...] = dcw0_ref[...]
            dcb_ref[...] = dcb0_ref[...]

        pmask = jnp.where(i == 0, 0.0, 1.0)
        nmask = jnp.where(i == n - 1, 0.0, 1.0)
        pad[pl.ds(0, 8 * NH), :] = dp[...] * pmask
        pad[pl.ds(8 * NH, T * NH), :] = dm[...]
        pad[pl.ds((T + 8) * NH, 8 * NH), :] = dn[...] * nmask

        def chunk(ci, carry):
            base = pl.multiple_of(ci * (CONV_CHUNK * NH), CONV_CHUNK * NH)
            xav = xa_ref[pl.ds(base, CONV_CHUNK * NH), :].reshape(CONV_CHUNK, NH, HD)
            acc = None
            for k in range(4):
                sl = pad[pl.ds(base + (9 - k) * NH, CONV_CHUNK * NH), :].reshape(CONV_CHUNK, NH, HD)
                term = sl * cw[k][None]
                acc = term if acc is None else acc + term
                dcw_ref[k] += jnp.sum(sl * xav, axis=0)
                if k == 1:
                    dcb_ref[...] += jnp.sum(sl, axis=0)
            dxa_s[pl.ds(base, CONV_CHUNK * NH), :] = acc.reshape(CONV_CHUNK * NH, HD)
            return carry
        lax.fori_loop(0, T // CONV_CHUNK, chunk, 0)
        for h in range(NH):
            dxa_ref[:, HD * h:HD * h + HD] = dxa_s[_zrows(h, T), :].astype(BF16)

    full = lambda shape: pl.BlockSpec(shape, lambda i: (0,) * len(shape))
    return pl.pallas_call(
        body, name=name, grid=(n,),
        out_shape=(jax.ShapeDtypeStruct((lx, D), BF16), jax.ShapeDtypeStruct((4, NH, HD), F32),
                   jax.ShapeDtypeStruct((NH, HD), F32)),
        in_specs=_zhalo_specs(lx) + [ZT, full((4, NH, HD)), full((4, NH, HD)), full((NH, HD))],
        out_specs=(pl.BlockSpec((T, D), lambda i: (i, 0)), full((4, NH, HD)), full((NH, HD))),
        scratch_shapes=[pltpu.VMEM(((T + 16) * NH, HD), F32), pltpu.VMEM((T * NH, HD), F32)],
        compiler_params=_cp(1, vmem_mb=48),
    )(dxc_z, dxc_z, dxc_z, xa_z, conv_wz, dcw0, dcb0)


def _proj_bwd(dzs, x, dxn, mods, mrow, norm_g, w_full, dng0, name):
    lx = x.shape[0]
    n = lx // T
    nk = len(dzs)
    has_x = dxn is not None

    def body(*refs):
        dz_refs = refs[:nk]
        w_refs = refs[nk:2 * nk]
        x_ref, sc_ref, ng_ref, dng0_ref = refs[2 * nk:2 * nk + 4]
        rest = refs[2 * nk + 4:]
        if has_x:
            dxn_ref, gx_ref, dng_ref, dsc_ref, dsh_ref = rest
        else:
            dng_ref, dsc_ref, dsh_ref = rest
        i = pl.program_id(0)

        @pl.when(i == 0)
        def _():
            dng_ref[...] = dng0_ref[...]
            dsc_ref[...] = jnp.zeros_like(dsc_ref)
            dsh_ref[...] = jnp.zeros_like(dsh_ref)

        dhn = _dot_nt(dz_refs[0][...], w_refs[0][...])
        for k in range(1, nk):
            dhn = dhn + _dot_nt(dz_refs[k][...], w_refs[k][...])
        xv = x_ref[...]
        r = lax.rsqrt(jnp.mean(xv * xv, axis=-1, keepdims=True) + NORM_EPS)
        xn = xv * r
        ng = ng_ref[...]
        sc1 = 1.0 + sc_ref[mrow:mrow + 1, :]
        t = dhn * xn
        dng_ref[...] += jnp.sum(t * sc1, axis=0, keepdims=True)
        dsc_ref[...] += jnp.sum(t * ng, axis=0, keepdims=True)
        dsh_ref[...] += jnp.sum(dhn, axis=0, keepdims=True)
        if has_x:
            dxh = dhn * (ng * sc1)
            gx_ref[...] = dxn_ref[...] + r * (dxh - xn * jnp.mean(dxh * xn, axis=-1, keepdims=True))

    row = pl.BlockSpec((T, D), lambda i: (i, 0))
    vec = pl.BlockSpec((1, D), lambda i: (0, 0))
    in_specs = [row] * nk + [pl.BlockSpec((D, D), lambda i, k=k: (0, k)) for k in range(nk)]
    in_specs += [row, pl.BlockSpec((8, D), lambda i: (0, 1)), vec, vec]
    args = list(dzs) + [w_full] * nk + [x, mods, norm_g, dng0]
    vs = jax.ShapeDtypeStruct((1, D), F32)
    out_shape, out_specs = (vs, vs, vs), (vec, vec, vec)
    if has_x:
        in_specs.append(row)
        args.append(dxn)
        out_shape = (jax.ShapeDtypeStruct((lx, D), F32),) + out_shape
        out_specs = (row,) + out_specs
    return pl.pallas_call(
        body, name=name, grid=(n,), out_shape=out_shape, in_specs=in_specs, out_specs=out_specs,
        compiler_params=_cp(1, vmem_mb=56),
    )(*args)


def _tn_matmul(a, bs, extra, name):
    lx, m = a.shape
    tm = min(lx, 1024)
    n = lx // tm
    widths = [b.shape[1] for b in bs]
    nb = len(bs)

    def body(a_ref, *rest):
        b_refs = rest[:nb]
        rest = rest[nb:]
        if extra is not None:
            ea_ref, eb_ref = rest[:2]
            rest = rest[2:]
        out_ref, acc = rest
        i = pl.program_id(0)
        av = a_ref[...]
        off = 0
        for k in range(nb):
            cols = slice(off, off + widths[k])
            part = _dot_tn(av, b_refs[k][...])

            @pl.when(i == 0)
            def _():
                acc[:, cols] = part

            @pl.when(i > 0)
            def _():
                acc[:, cols] += part
            off += widths[k]

        @pl.when(i == n - 1)
        def _():
            if extra is not None:
                acc[:, 0:widths[0]] += _dot_tn(ea_ref[...], eb_ref[...])
            pltpu.sync_copy(acc, out_ref)

    in_specs = [pl.BlockSpec((tm, m), lambda i: (i, 0))] + [pl.BlockSpec((tm, w), lambda i: (i, 0)) for w in widths]
    args = [a] + list(bs)
    if extra is not None:
        in_specs += [VMEM, VMEM]
        args += list(extra)
    return pl.pallas_call(
        body, name=name, grid=(n,), out_shape=jax.ShapeDtypeStruct((m, sum(widths)), F32),
        in_specs=in_specs, out_specs=ANY, scratch_shapes=[pltpu.VMEM((m, sum(widths)), F32)],
        compiler_params=_cp(1, vmem_mb=56),
    )(*args)


def _adam_math(w, g, m, v):
    m = ADAM_B1 * m + (1.0 - ADAM_B1) * g
    v = ADAM_B2 * v + (1.0 - ADAM_B2) * (g * g)
    m_hat = m / (1.0 - ADAM_B1 ** ADAM_STEP)
    v_hat = v / (1.0 - ADAM_B2 ** ADAM_STEP)
    delta = -ADAM_LR * (m_hat / (jnp.sqrt(v_hat) + ADAM_EPS) + ADAM_WD * w)
    return delta, m, v


def _adam_big(w, g, m, v, name):
    rows, cols = w.shape
    tr = 256

    def body(w_ref, g_ref, m_ref, v_ref, d_o, m_o, v_o):
        d, mm, vv = _adam_math(w_ref[...], g_ref[...], m_ref[...], v_ref[...])
        d_o[...] = d
        m_o[...] = mm
        v_o[...] = vv

    blk = pl.BlockSpec((tr, cols), lambda i: (i, 0))
    s = jax.ShapeDtypeStruct((rows, cols), F32)
    return pl.pallas_call(
        body, name=name, grid=(rows // tr,), out_shape=(s, s, s), in_specs=[blk] * 4, out_specs=(blk,) * 3,
        compiler_params=_cp(1, vmem_mb=48),
    )(w, g, m, v)


def _adam_small(items):
    ni = len(items)

    def body(*refs):
        ins, outs = refs[:4 * ni], refs[4 * ni:7 * ni]
        bufs_in, bufs_out = refs[7 * ni:11 * ni], refs[11 * ni:14 * ni]
        sem_in, sem_out = refs[14 * ni], refs[14 * ni + 1]
        loads = [pltpu.make_async_copy(ins[q], bufs_in[q], sem_in.at[q]) for q in range(4 * ni)]
        for cp in loads:
            cp.start()
        stores = []
        for k in range(ni):
            for q in range(4):
                loads[4 * k + q].wait()
            w_b, g_b, m_b, v_b = bufs_in[4 * k:4 * k + 4]
            res = _adam_math(w_b[...], g_b[...], m_b[...], v_b[...])
            for q in range(3):
                bufs_out[3 * k + q][...] = res[q]
                cp = pltpu.make_async_copy(bufs_out[3 * k + q], outs[3 * k + q], sem_out.at[3 * k + q])
                cp.start()
                stores.append(cp)
        for cp in stores:
            cp.wait()

    flat = [a for it in items for a in it]
    out_shape = tuple(jax.ShapeDtypeStruct(it[0].shape, F32) for it in items for _ in range(3))
    scratch = [pltpu.VMEM(a.shape, F32) for a in flat] + [pltpu.VMEM(s.shape, F32) for s in out_shape]
    scratch += [pltpu.SemaphoreType.DMA((4 * ni,)), pltpu.SemaphoreType.DMA((3 * ni,))]
    res = pl.pallas_call(
        body, name="adam_small", out_shape=out_shape, in_specs=[ANY] * (4 * ni), out_specs=(ANY,) * (3 * ni),
        scratch_shapes=scratch, compiler_params=_cp(vmem_mb=40),
    )(*flat)
    return [tuple(res[3 * k:3 * k + 3]) for k in range(ni)]


def kernel(x, c, ctx, c_ctx, ada_w, ada_b, norm_g, w_in, conv_w, conv_b, lru_wa, lru_ba, lru_wx, lru_bx, lru_lambda, sgu_ln_g, sgu_ln_b, sgu_w, sgu_b, w_out, final_g, loss_target, m_c_ctx, m_ada_w, m_ada_b, m_norm_g, m_w_in, m_conv_w, m_conv_b, m_lru_wa, m_lru_ba, m_lru_wx, m_lru_bx, m_lru_lambda, m_sgu_ln_g, m_sgu_ln_b, m_sgu_w, m_sgu_b, m_w_out, m_final_g, v_c_ctx, v_ada_w, v_ada_b, v_norm_g, v_w_in, v_conv_w, v_conv_b, v_lru_wa, v_lru_ba, v_lru_wx, v_lru_bx, v_lru_lambda, v_sgu_ln_g, v_sgu_ln_b, v_sgu_w, v_sgu_b, v_w_out, v_final_g):
    ix, iy, ic = lax.axis_index("x"), lax.axis_index("y"), lax.axis_index("c")
    chip = 2 * ix + iy
    dev = 2 * chip + ic
    lx = x.shape[1]
    lc = ctx.shape[1]

    smalls = jnp.concatenate([conv_w[0], lru_lambda[0], jnp.zeros((10, 256), F32)], axis=0)
    c_ctx2 = c_ctx.reshape(1, D)
    ada_b_j = lax.dynamic_slice(ada_b, (0, 768 * chip), (1, 768))
    mods, c_slots, sm_all, w_in_full, wo_land, ada_land = _gather_in(c, c_ctx2, ada_w[0], ada_b_j, w_in[0], w_out[0],
                                                                     smalls)
    wo_ss, wo_rs, ada_ss, ada_rs, wo_land, ada_land, token = _late_gather_start(wo_land, ada_land)
    mods = mods + token[0:1, 0:1]
    sm3 = sm_all.reshape(NCHIP, 16, 256)
    conv_w_full = sm3[:, 0:4, :].transpose(1, 0, 2).reshape(4, D)
    lam_full = sm3[:, 4:6, :].transpose(1, 0, 2).reshape(2, D)
    conv_wz = conv_w_full.reshape(4, NH, HD)
    conv_bz = conv_b.reshape(NH, HD)
    lamcat = lam_full.reshape(2, NH, HD).transpose(1, 0, 2).reshape(NH, 2 * HD)
    wa, wx, ba, bx = lru_wa[0], lru_wx[0], lru_ba[0], lru_bx[0]
    wcat = jnp.concatenate([wa[0], wx[0], wa[1], wx[1]], axis=-1).astype(BF16)
    bcat = jnp.concatenate([ba[0], bx[0], ba[1], bx[1]], axis=-1)
    sgu_wb = sgu_w[0].astype(BF16)
    sgu_bt = sgu_b[0].T
    final_g2 = final_g.reshape(1, D)

    zero_s = jnp.zeros((NH, HD), F32)
    hn_c, xa_c = _proj(ctx[0], mods, 1, norm_g, w_in_full, 1, "proj_ctx")
    xaz_c, xcz_c, af_c, uf_c, ab_c, ub_c = _lru_gates_fwd(xa_c, conv_wz, conv_bz, wcat, bcat, lamcat, "lru_gates_ctx")
    _, _, pf_c, pb_c, hf0, hb0 = _scan(af_c, uf_c, ab_c, ub_c, zero_s, zero_s, False, "scan_ctx")

    hn, xa, ga, u, v, gb = _proj(x[0], mods, 0, norm_g, w_in_full, 5, "proj")
    xaz, xcz, af, uf, ab, ub = _lru_gates_fwd(xa, conv_wz, conv_bz, wcat, bcat, lamcat, "lru_gates")
    hf, hb, pf, pb, _, _ = _scan(af, uf, ab, ub, hf0, hb0, False, "scan")
    ys = _sgu_fwd(u, v, sgu_ln_g, sgu_ln_b, sgu_wb, sgu_bt)

    w_out_full = _late_gather_wait(wo_land, wo_ss, wo_rs, "w_out", ys, "late_gather_wait_w_out")
    (loss_part, dfg, dgx, dxn, y, do, dga, dgb, dyl_z, dys) = _out_fwd_bwd(
        hf, hb, ga, gb, ys, x[0], loss_target[0], mods, final_g2, w_out_full)
    g_w_out_part = _tn_matmul(y, [do], None, "grad_w_out")

    du, dv, d_sgu_w, d_sgu_b, d_ln_g, d_ln_b = _sgu_bwd(u, v, dys, sgu_ln_g, sgu_ln_b, sgu_wb, sgu_bt)
    lb, lf, dh0b, dh0f = _scan(ab, dyl_z, af, dyl_z, zero_s, zero_s, True, "scan_adj")
    zw = jnp.zeros((NH, HD, 4 * HD), F32)
    zb = jnp.zeros((NH, 4 * HD), F32)
    zl = jnp.zeros((NH, 2 * HD), F32)
    dxc_z, dwc, dbc, dlc = _lru_gates_bwd(xcz, lf, lb, pf, pb, wcat, bcat, lamcat, zw, zb, zl, "lru_gates_bwd")
    dxa, dcw, dcb = _conv_bwd(dxc_z, xaz, conv_wz, jnp.zeros((4, NH, HD), F32), zero_s, "conv_bwd")

    zc = jnp.zeros((lc * NH, HD), F32)
    dhf_c = lax.dynamic_update_slice(zc, dh0f, ((lc - 1) * NH, 0))
    dhb_c = lax.dynamic_update_slice(zc, dh0b, (0, 0))
    lb_c, lf_c, _, _ = _scan(ab_c, dhb_c, af_c, dhf_c, zero_s, zero_s, True, "scan_adj_ctx")
    dxc_zc, dwc, dbc, dlc = _lru_gates_bwd(xcz_c, lf_c, lb_c, pf_c, pb_c, wcat, bcat, lamcat, dwc, dbc, dlc,
                                           "lru_gates_bwd_ctx")
    dxa_c, dcw, dcb = _conv_bwd(dxc_zc, xaz_c, conv_wz, dcw, dcb, "conv_bwd_ctx")

    dzs = [dxa, dga, du, dv, dgb]
    grad_x, dng, dsc_x, dsh_x = _proj_bwd(dzs, x[0], dxn, mods, 0, norm_g, w_in_full, jnp.zeros((1, D), F32), "proj_bwd")
    dng, dsc_c, dsh_c = _proj_bwd([dxa_c], ctx[0], None, mods, 1, norm_g, w_in_full, dng, "proj_bwd_ctx")

    dmx = jnp.concatenate([dsh_x, dsc_x, dgx], axis=0)
    dmc = jnp.concatenate([dsh_c, dsc_c, jnp.zeros((1, D), F32)], axis=0)
    slot = jnp.concatenate([dmx, jnp.zeros((1, D), F32)], axis=0)
    slots = lax.dynamic_update_slice(jnp.zeros((32, D), F32), slot, (4 * dev, 0))
    vecs = jnp.concatenate([dfg, dng, dcb.reshape(1, D), d_ln_g, d_ln_b, dcw.reshape(4, D), dmc,
                            jnp.zeros((4, D), F32), slots], axis=0)
    d_sgu_w4 = d_sgu_w.reshape(4, 256, HD).transpose(1, 0, 2).reshape(256, 4 * HD)
    pad8 = lambda a: jnp.pad(a, ((0, 8 - a.shape[0]), (0, 4 * HD - a.shape[1])))
    pack = jnp.concatenate([dwc.reshape(NH * HD, 4 * HD), pad8(dbc), pad8(dlc), d_sgu_w4, pad8(d_sgu_b),
                            vecs.reshape(96, 4 * HD), jnp.zeros((8, 4 * HD), F32)], axis=0)
    g_w_in, g_w_out, tot = _grads_reduce(hn, dzs, hn_c, dxa_c, g_w_out_part, pack)

    g_wc = tot[0:1024].reshape(NH, HD, 4 * HD)
    g_bc = tot[1024:1032]
    g_lc = tot[1032:1040, 0:2 * HD]
    g_sgu_w = tot[1040:1296].reshape(256, 4, HD).transpose(1, 0, 2).reshape(NH, HD, HD)
    g_sgu_b = tot[1296:1304, 0:HD]
    tv = tot[1304:1400].reshape(48, D)
    g_final_g, g_norm_g, g_conv_b, g_ln_g, g_ln_b = tv[0:1], tv[1:2], tv[2:3], tv[3:4], tv[4:5]
    g_conv_w_full = tv[5:9]
    dmc_tot = tv[9:12].reshape(1, 3 * D)
    slots_all = tv[16:48].reshape(8, 4, D)
    dmx_all = slots_all[:, 0:3, :].reshape(8, 3 * D)
    c_all = c_slots.reshape(8, 8, D)[:, 0, :]
    g_lru_wa = jnp.stack([g_wc[:, :, 0:HD], g_wc[:, :, 2 * HD:3 * HD]])
    g_lru_wx = jnp.stack([g_wc[:, :, HD:2 * HD], g_wc[:, :, 3 * HD:4 * HD]])
    g_lru_ba = jnp.stack([g_bc[:, 0:HD], g_bc[:, 2 * HD:3 * HD]])
    g_lru_bx = jnp.stack([g_bc[:, HD:2 * HD], g_bc[:, 3 * HD:4 * HD]])
    g_lam_full = jnp.stack([g_lc[:, 0:HD], g_lc[:, HD:2 * HD]]).reshape(2, D)
    g_conv_w = lax.dynamic_slice(g_conv_w_full, (0, 256 * chip), (4, 256))
    g_lam = lax.dynamic_slice(g_lam_full, (0, 256 * chip), (2, 256))
    dmx_all_j = lax.dynamic_slice(dmx_all, (0, 768 * chip), (8, 768))
    dmc_j = lax.dynamic_slice(dmc_tot, (0, 768 * chip), (1, 768))
    ada_full = _late_gather_wait(ada_land, ada_ss, ada_rs, "ada_w", tot, "late_gather_wait_ada_w")
    g_ada_w, g_ada_b, g_c_ctx = _ada_bwd(c_all, dmx_all_j, dmc_j, dmx_all, dmc_tot, c_ctx2, ada_full)

    big = {
        "ada_w": _adam_big(ada_w[0], g_ada_w, m_ada_w[0], v_ada_w[0], "adam_ada_w"),
        "w_in": _adam_big(w_in[0], g_w_in, m_w_in[0], v_w_in[0], "adam_w_in"),
        "w_out": _adam_big(w_out[0], g_w_out, m_w_out[0], v_w_out[0], "adam_w_out"),
    }
    small_in = {
        "c_ctx": (c_ctx, g_c_ctx, m_c_ctx, v_c_ctx, (1, D)),
        "ada_b": (ada_b, g_ada_b, m_ada_b, v_ada_b, (1, 3 * D)),
        "norm_g": (norm_g, g_norm_g, m_norm_g, v_norm_g, (1, D)),
        "conv_w": (conv_w, g_conv_w, m_conv_w, v_conv_w, (4, 256)),
        "conv_b": (conv_b, g_conv_b, m_conv_b, v_conv_b, (1, D)),
        "lru_wa": (lru_wa, g_lru_wa, m_lru_wa, v_lru_wa, (2 * NH * HD, HD)),
        "lru_ba": (lru_ba, g_lru_ba, m_lru_ba, v_lru_ba, (2 * NH, HD)),
        "lru_wx": (lru_wx, g_lru_wx, m_lru_wx, v_lru_wx, (2 * NH * HD, HD)),
        "lru_bx": (lru_bx, g_lru_bx, m_lru_bx, v_lru_bx, (2 * NH, HD)),
        "lru_lambda": (lru_lambda, g_lam, m_lru_lambda, v_lru_lambda, (2, 256)),
        "sgu_ln_g": (sgu_ln_g, g_ln_g, m_sgu_ln_g, v_sgu_ln_g, (1, D)),
        "sgu_ln_b": (sgu_ln_b, g_ln_b, m_sgu_ln_b, v_sgu_ln_b, (1, D)),
        "sgu_w": (sgu_w, g_sgu_w, m_sgu_w, v_sgu_w, (NH * HD, HD)),
        "sgu_b": (sgu_b, g_sgu_b, m_sgu_b, v_sgu_b, (NH, HD)),
        "final_g": (final_g, g_final_g, m_final_g, v_final_g, (1, D)),
    }
    names_small = list(small_in)
    res_small = _adam_small([tuple(a.reshape(small_in[k][4]) for a in small_in[k][:4]) for k in names_small])
    full_shapes = {"ada_w": ada_w.shape, "w_in": w_in.shape, "w_out": w_out.shape}
    grads, deltas, new_m, new_v = {}, {}, {}, {}
    for k in ("ada_w", "w_in", "w_out"):
        g = {"ada_w": g_ada_w, "w_in": g_w_in, "w_out": g_w_out}[k]
        grads[k] = g.reshape(full_shapes[k])
        deltas[k], new_m[k], new_v[k] = (a.reshape(full_shapes[k]) for a in big[k])
    for k, res in zip(names_small, res_small):
        shape = small_in[k][0].shape
        grads[k] = small_in[k][1].reshape(shape)
        deltas[k], new_m[k], new_v[k] = (a.reshape(shape) for a in res)

    loss = lax.psum(loss_part[0, 0], ("x", "y", "c"))
    order = ["c_ctx", "ada_w", "ada_b", "norm_g", "w_in", "conv_w", "conv_b", "lru_wa", "lru_ba", "lru_wx", "lru_bx",
             "lru_lambda", "sgu_ln_g", "sgu_ln_b", "sgu_w", "sgu_b", "w_out", "final_g"]
    return (loss, grad_x.reshape(x.shape), *[grads[k] for k in order], *[deltas[k] for k in order],
            *[new_m[k] for k in order], *[new_v[k] for k in order])
```

```python
import functools

import jax
import jax.numpy as jnp
from jax import lax
from jax.experimental import pallas as pl
from jax.experimental.pallas import tpu as pltpu

F32 = jnp.float32
BF16 = jnp.bfloat16

D = 1024
NH = 8
HD = 128
NCHIP = 4
T = 256
NORM_EPS = 1e-6
LN_EPS = 1e-5
LRU_C = 8.0
ADAM_LR = 0.001
ADAM_B1 = 0.9
ADAM_B2 = 0.999
ADAM_EPS = 1e-08
ADAM_WD = 0.01
ADAM_STEP = 10

VMEM = pl.BlockSpec(memory_space=pltpu.VMEM)
ANY = pl.BlockSpec(memory_space=pl.ANY)
MESH = pl.DeviceIdType.MESH


def _cp(n_grid=0, vmem_mb=None):
    kw = {}
    if n_grid:
        kw["dimension_semantics"] = ("arbitrary",) * n_grid
    if vmem_mb:
        kw["vmem_limit_bytes"] = vmem_mb << 20
    return pltpu.CompilerParams(**kw)


def _sigmoid(x):
    return 1.0 / (1.0 + jnp.exp(-x))


def _silu_and_grad(x):
    s = _sigmoid(x)
    return x * s, s * (1.0 + x * (1.0 - s))


_GELU_K = 0.7978845608028654
_GELU_C = 0.044715


def _gelu_and_grad(x):
    x2 = x * x
    th = jnp.tanh(_GELU_K * (x + _GELU_C * x * x2))
    g = 0.5 * x * (1.0 + th)
    dg = 0.5 * (1.0 + th) + 0.5 * x * (1.0 - th * th) * (_GELU_K * (1.0 + 3.0 * _GELU_C * x2))
    return g, dg


def _softplus(x):
    return jnp.maximum(x, 0.0) + jnp.log1p(jnp.exp(-jnp.abs(x)))


def _lru_gate(pre, lam_row, d):
    r = _sigmoid(pre[:, 256 * d:256 * d + HD])
    gi = _sigmoid(pre[:, 256 * d + HD:256 * d + 2 * HD])
    lam = lam_row[:, HD * d:HD * d + HD]
    sp = _softplus(-lam)
    la = (-LRU_C) * r * sp
    a = jnp.exp(la)
    x2 = 2.0 * la
    m2 = jnp.where(x2 > -1e-3, -x2 * (1.0 + 0.5 * x2), 1.0 - a * a)
    mult = jnp.sqrt(m2)
    return r, gi, lam, sp, a, mult


def _dot(a, b):
    return jnp.dot(a, b, preferred_element_type=F32)


def _dot_tn(a, b):
    return lax.dot_general(a, b, (((0,), (0,)), ((), ())), preferred_element_type=F32)


def _dot_nt(a, b):
    return lax.dot_general(a, b, (((1,), (1,)), ((), ())), preferred_element_type=F32)


def _mo(v, m):
    return v if isinstance(v, int) else pl.multiple_of(v, m)


def _zrows(h, n):
    return pl.ds(h, n, stride=NH)


def _gather_in(c, c_ctx, ada_w, ada_b_j, w_in, w_out, smalls):
    specs = [
        ((64, 256), F32, lambda r, jj, cc: r.at[pl.ds(_mo(16 * jj + 8 * cc, 8), 8), :]),
        ((D, 5120), BF16, lambda r, jj, cc: r.at[pl.ds(_mo(512 * cc, 16), 512), pl.ds(_mo(1280 * jj, 128), 1280)]),
    ]
    halves = [lambda r, cc: r.at[pl.ds(_mo(8 * cc, 8), 8), :],
              lambda r, cc: r.at[pl.ds(_mo(512 * cc, 16), 512), :]]
    na = len(specs)
    n_sem = 6 * na + 10

    def body(c_ref, cc_ref, ada_ref, adab_ref, win_ref, wout_ref, sm_ref,
             mods_o, call_o, sm_o, win_o, wol_o, adal_o, s_win, s_ada, cslot, lhs, mbuf,
             send_sems, recv_sems, local_sems):
        x, y, c = lax.axis_index("x"), lax.axis_index("y"), lax.axis_index("c")
        j = 2 * x + y
        dev = 2 * j + c
        sib = (x, y, 1 - c)
        chips = [(1 - x, y), (x, 1 - y), (1 - x, 1 - y)]
        cj = [2 * cx + cy for cx, cy in chips]
        outs = [sm_o, win_o]
        srcs = [sm_ref, s_win]

        def copy(idx, src, dst, to):
            return pltpu.make_async_remote_copy(src_ref=src, dst_ref=dst, send_sem=send_sems.at[idx],
                                                recv_sem=recv_sems.at[idx], device_id=to, device_id_type=MESH)

        sends = []

        def start(cp):
            cp.start()
            sends.append(cp)

        cslot[...] = jnp.zeros_like(cslot)
        cslot[0:1, :] = c_ref[...]
        my_slot = pl.ds(_mo(8 * dev, 8), 8)
        others = [sib] + [(*chips[k], c) for k in range(3)] + [(*chips[k], 1 - c) for k in range(3)]
        other_dev = [dev + 1 - 2 * c] + [2 * cj[k] + c for k in range(3)] + [2 * cj[k] + 1 - c for k in range(3)]
        base = 6 * na
        for r in range(7):
            start(copy(base + r, cslot, call_o.at[my_slot, :], others[r]))
        call_o[my_slot, :] = cslot[...]

        s_win[...] = win_ref[...].astype(BF16)
        local = []
        for a in range(na):
            for cc in range(2):
                lc = pltpu.make_async_copy(halves[a](srcs[a], cc), specs[a][2](outs[a], j, cc), local_sems.at[2 * a + cc])
                lc.start()
                local.append(lc)
        for k in range(3):
            for a in range(na):
                start(copy(6 * a + k, halves[a](srcs[a], c), specs[a][2](outs[a], j, c), (*chips[k], c)))
        wol_o[pl.ds(_mo(512 * j, 16), 512), :] = wout_ref[...].astype(BF16)
        s_ada[...] = ada_ref[...].astype(BF16)
        lc = pltpu.make_async_copy(s_ada, adal_o.at[:, pl.ds(_mo(768 * j, 128), 768)], local_sems.at[2 * na])
        lc.start()
        local.append(lc)

        for r in range(7):
            slot = call_o.at[pl.ds(_mo(8 * other_dev[r], 8), 8), :]
            copy(base + r, slot, slot, sib).wait_recv()
        lhs[...] = jnp.zeros_like(lhs)
        for b in range(8):
            cv = call_o[8 * b:8 * b + 1, :]
            lhs[b:b + 1, :] = cv * _sigmoid(cv)
        cv = cc_ref[...]
        lhs[8:9, :] = cv * _sigmoid(cv)
        mbuf[j] = _dot(lhs[...].astype(BF16), s_ada[...]) + adab_ref[...]
        for k in range(3):
            start(copy(base + 7 + k, mbuf.at[j], mbuf.at[j], (*chips[k], c)))
        for k in range(3):
            copy(base + 7 + k, mbuf.at[cj[k]], mbuf.at[cj[k]], sib).wait_recv()
        mods_o[...] = jnp.zeros_like(mods_o)
        for jj in range(NCHIP):
            mods_o[0:1, 768 * jj:768 * jj + 768] = mbuf[jj, pl.ds(dev, 1), :]
            mods_o[1:2, 768 * jj:768 * jj + 768] = mbuf[jj, 8:9, :]

        for k in range(3):
            for a in range(na):
                reg = specs[a][2](outs[a], cj[k], c)
                copy(6 * a + k, reg, reg, sib).wait_recv()
                start(copy(6 * a + 3 + k, reg, reg, sib))
        for k in range(3):
            for a in range(na):
                reg = specs[a][2](outs[a], cj[k], 1 - c)
                copy(6 * a + 3 + k, reg, reg, sib).wait_recv()
        for cp in sends:
            cp.wait_send()
        for lc in local:
            lc.wait()

    out_shape = (jax.ShapeDtypeStruct((8, 3 * D), F32), jax.ShapeDtypeStruct((64, D), F32),
                 jax.ShapeDtypeStruct(specs[0][0], F32), jax.ShapeDtypeStruct(specs[1][0], BF16),
                 jax.ShapeDtypeStruct((2048, D), BF16), jax.ShapeDtypeStruct((D, 3 * D), BF16))
    return pl.pallas_call(
        body, name="gather_in", out_shape=out_shape,
        in_specs=[VMEM] * 7, out_specs=(VMEM,) * 6,
        scratch_shapes=[pltpu.VMEM((D, 1280), BF16), pltpu.VMEM((D, 768), BF16), pltpu.VMEM((8, D), F32),
                        pltpu.VMEM((16, D), F32), pltpu.VMEM((NCHIP, 16, 768), F32),
                        pltpu.SemaphoreType.DMA((n_sem,)), pltpu.SemaphoreType.DMA((n_sem,)),
                        pltpu.SemaphoreType.DMA((2 * na + 1,))],
        compiler_params=_cp(vmem_mb=56),
    )(c, c_ctx, ada_w, ada_b_j, w_in, w_out, smalls)


HBM = pl.BlockSpec(memory_space=pltpu.HBM)
SEM = pl.BlockSpec(memory_space=pltpu.SEMAPHORE)


def _late_gather_regions(x, y, c):
    chips = [(1 - x, y), (x, 1 - y), (1 - x, 1 - y)]
    wo_reg = lambda r, jj, cc: r.at[pl.ds(_mo(512 * jj + 256 * cc, 16), 256), :]
    ada_reg = lambda r, jj, cc: r.at[pl.ds(_mo(512 * cc, 16), 512), pl.ds(_mo(768 * jj, 128), 768)]
    return chips, wo_reg, ada_reg


def _late_gather_start(wo_land, ada_land):
    def body(wol_ref, adal_ref, wo_ss, wo_rs, ada_ss, ada_rs, wol_thru, adal_thru, token):
        x, y, c = lax.axis_index("x"), lax.axis_index("y"), lax.axis_index("c")
        j = 2 * x + y
        chips, wo_reg, ada_reg = _late_gather_regions(x, y, c)
        for k in range(3):
            for cc in range(2):
                pltpu.make_async_remote_copy(src_ref=wo_reg(wol_ref, j, c), dst_ref=wo_reg(wol_ref, j, c),
                                             send_sem=wo_ss.at[2 * k + cc], recv_sem=wo_rs.at[2 * k + c],
                                             device_id=(*chips[k], cc), device_id_type=MESH).start()
        for k in range(3):
            for cc in range(2):
                pltpu.make_async_remote_copy(src_ref=ada_reg(adal_ref, j, c), dst_ref=ada_reg(adal_ref, j, c),
                                             send_sem=ada_ss.at[2 * k + cc], recv_sem=ada_rs.at[2 * k + c],
                                             device_id=(*chips[k], cc), device_id_type=MESH).start()
        token[...] = jnp.zeros_like(token)

    sems = pltpu.SemaphoreType.DMA((6,))
    return pl.pallas_call(
        body, name="late_gather_start",
        out_shape=(sems, sems, sems, sems, pltpu.HBM(wo_land.shape, BF16), pltpu.HBM(ada_land.shape, BF16),
                   jax.ShapeDtypeStruct((8, 128), F32)),
        in_specs=(HBM, HBM), out_specs=(SEM, SEM, SEM, SEM, HBM, HBM, VMEM), input_output_aliases={0: 4, 1: 5},
        compiler_params=pltpu.CompilerParams(has_side_effects=pltpu.SideEffectType.DATAFLOW_SIDE_EFFECTING),
    )(pltpu.with_memory_space_constraint(wo_land, pltpu.HBM), pltpu.with_memory_space_constraint(ada_land, pltpu.HBM))


def _late_gather_wait(land, send_sems, recv_sems, which, after, name):
    def body(land_ref, ss, rs, after_ref, land_out):
        x, y, c = lax.axis_index("x"), lax.axis_index("y"), lax.axis_index("c")
        j = 2 * x + y
        chips, wo_reg, ada_reg = _late_gather_regions(x, y, c)
        reg = wo_reg if which == "w_out" else ada_reg
        for k in range(3):
            kj = 2 * chips[k][0] + chips[k][1]
            for cc in range(2):
                cp = pltpu.make_async_remote_copy(src_ref=reg(land_ref, j, c), dst_ref=reg(land_ref, kj, cc),
                                                  send_sem=ss.at[2 * k + cc], recv_sem=rs.at[2 * k + cc],
                                                  device_id=(*chips[k], cc), device_id_type=MESH)
                cp.wait_send()
                cp.wait_recv()

    return pl.pallas_call(
        body, name=name, out_shape=pltpu.HBM(land.shape, land.dtype),
        in_specs=(HBM, SEM, SEM, ANY), out_specs=HBM, input_output_aliases={0: 0},
        compiler_params=pltpu.CompilerParams(has_side_effects=pltpu.SideEffectType.DATAFLOW_SIDE_EFFECTING),
    )(land, send_sems, recv_sems, after)


RCHUNK = 16


def _grads_reduce(hn, dzs, hn_c, dxa_c, g_w_out, pack):
    rp = pack.shape[0]
    hp = rp // 2
    assert hp % RCHUNK == 0
    wi_w = 1280
    lx, lc = hn.shape[0], hn_c.shape[0]
    lt = lx + lc
    n_dz = len(dzs)

    def body(*refs):
        hn_hbm, dz_hbm = refs[0], refs[1:1 + n_dz]
        hnc_hbm, dxac_hbm, wo_hbm, pk_hbm, wi_out, wo_out, pk_out = refs[1 + n_dz:8 + n_dz]
        (hn_mine, hn_other, dzbuf, wi_other, wi_mine, wi_recv, wi_send, wi_rb,
         wo_mine, wo_recv, wo_send, wo_rb, wo_own, pk_mine, pk_recv, pk_send, pk_rb, pk_own,
         send_sems, recv_sems, local_sems) = refs[8 + n_dz:]
        x, y, c = lax.axis_index("x"), lax.axis_index("y"), lax.axis_index("c")
        j = 2 * x + y
        sib = (x, y, 1 - c)
        chips = [(1 - x, y), (x, 1 - y), (1 - x, 1 - y)]
        cj = [2 * cx + cy for cx, cy in chips]
        slabs = cj + [j]

        def copy(k, src, dst, to):
            return pltpu.make_async_remote_copy(src_ref=src, dst_ref=dst, send_sem=send_sems.at[k],
                                                recv_sem=recv_sems.at[k], device_id=to, device_id_type=MESH)

        def local(k, src, dst):
            cp = pltpu.make_async_copy(src, dst, local_sems.at[k])
            cp.start()
            return cp

        rows_half = lambda r, cc, n: r.at[pl.ds(_mo(cc * n, 16), n), :]
        cols_half = lambda r, cc, n: r.at[:, pl.ds(_mo(cc * n, 128), n)]
        pk_piece = lambda r, cc, jj: r.at[pl.ds(_mo(cc * hp, 16), hp), pl.ds(_mo(jj * 128, 128), 128)]

        sends = []

        def start(cp):
            cp.start()
            sends.append(cp)

        def dz_pieces(s):
            g0 = wi_w * s
            k0, off0 = g0 // D, g0 % D
            w0 = min(D - off0, wi_w)
            pieces = [(k0, off0, w0, 0)]
            if w0 < wi_w:
                pieces.append((k0 + 1, 0, wi_w - w0, w0))
            return pieces

        def dz_copies(s):
            cps = []
            for q, (k, off, w, dst) in enumerate(dz_pieces(s)):
                cps.append(pltpu.make_async_copy(dz_hbm[k].at[:, pl.ds(off, w)], dzbuf.at[pl.ds(0, lx), pl.ds(dst, w)],
                                                 local_sems.at[11 + q]))
            if s == 0:
                cps.append(pltpu.make_async_copy(dxac_hbm, dzbuf.at[pl.ds(lx, lc), pl.ds(0, D)], local_sems.at[13]))
            return cps

        def dz_load(sl):
            for s in range(NCHIP):
                @pl.when(sl == s)
                def _():
                    if s == 0:
                        dzbuf[pl.ds(lx, lc), pl.ds(D, wi_w - D)] = jnp.zeros((lc, wi_w - D), BF16)
                    else:
                        dzbuf[pl.ds(lx, lc), :] = jnp.zeros((lc, wi_w), BF16)
                    for cp in dz_copies(s):
                        cp.start()

        def dz_wait(sl):
            for s in range(NCHIP):
                @pl.when(sl == s)
                def _():
                    for cp in dz_copies(s):
                        cp.wait()

        l_pk = local(0, rows_half(pk_hbm, c, hp), pk_mine)
        start(copy(0, rows_half(pk_hbm, 1 - c, hp), pk_recv, sib))
        l_wo = local(1, cols_half(wo_hbm, c, 512), wo_mine)
        start(copy(1, cols_half(wo_hbm, 1 - c, 512), wo_recv, sib))
        hn_loads = [local(2, hn_hbm.at[:, pl.ds(_mo(c * 512, 128), 512)], hn_mine.at[pl.ds(0, lx), :]),
                    local(3, hnc_hbm.at[:, pl.ds(_mo(c * 512, 128), 512)], hn_mine.at[pl.ds(lx, lc), :]),
                    local(4, hn_hbm.at[:, pl.ds(_mo((1 - c) * 512, 128), 512)], hn_other.at[pl.ds(0, lx), :]),
                    local(5, hnc_hbm.at[:, pl.ds(_mo((1 - c) * 512, 128), 512)], hn_other.at[pl.ds(lx, lc), :])]
        dz_load(slabs[0])

        def pair_sum(mine, recv, send, nrows, keep):
            def step(i, carry):
                rows = pl.ds(_mo(i * RCHUNK, RCHUNK), RCHUNK)
                s = mine[rows, :] + recv[rows, :].astype(F32)
                if keep:
                    mine[rows, :] = s
                if send is not None:
                    send[rows, :] = s.astype(BF16)
                return carry
            lax.fori_loop(0, nrows // RCHUNK, step, 0)

        def chip_sum(own, rb, nrows):
            def step(i, carry):
                rows = pl.ds(_mo(i * RCHUNK, RCHUNK), RCHUNK)
                own[rows, :] = (own[rows, :] + rb[0, rows, :].astype(F32)
                                + rb[1, rows, :].astype(F32) + rb[2, rows, :].astype(F32))
                return carry
            lax.fori_loop(0, nrows // RCHUNK, step, 0)

        p1 = [None] * NCHIP

        def slab_matmuls(s):
            if s >= 2:
                p1[s - 2].wait_send()
            dz_wait(slabs[s])
            wi_other[s % 2] = _dot_tn(hn_other[...], dzbuf[...]).astype(BF16)
            p1[s] = copy(2 + s, wi_other.at[s % 2], wi_recv.at[s], sib)
            p1[s].start()
            wi_mine[s % 2] = _dot_tn(hn_mine[...], dzbuf[...])
            if s + 1 < NCHIP:
                dz_load(slabs[s + 1])

        def slab_finish(s):
            copy(2 + s, wi_recv.at[s], wi_recv.at[s], sib).wait_recv()
            if s < 3:
                pair_sum(wi_mine.at[s % 2], wi_recv.at[s], wi_send.at[s], 512, False)
                start(copy(12 + s, wi_send.at[s], wi_rb.at[s], (*chips[s], c)))
            else:
                pair_sum(wi_mine.at[s % 2], wi_recv.at[s], None, 512, True)

        l_wo.wait()
        copy(1, wo_recv, wo_recv, sib).wait_recv()
        pair_sum(wo_mine, wo_recv, wo_send, 2048, True)
        for k in range(3):
            start(copy(9 + k, wo_send.at[pl.ds(_mo(cj[k] * 512, 16), 512), :], wo_rb.at[k], (*chips[k], c)))
        l_wo_own = local(7, wo_mine.at[pl.ds(_mo(j * 512, 16), 512), :], wo_own)

        for cp in hn_loads:
            cp.wait()
        slab_matmuls(0)

        l_pk.wait()
        copy(0, pk_recv, pk_recv, sib).wait_recv()
        pair_sum(pk_mine, pk_recv, pk_send, hp, True)
        for k in range(3):
            start(copy(6 + k, pk_send.at[:, pl.ds(_mo(cj[k] * 128, 128), 128)], pk_rb.at[k], (*chips[k], c)))
        l_pk_own = local(6, pk_mine.at[:, pl.ds(_mo(j * 128, 128), 128)], pk_own)

        slab_matmuls(1)
        slab_finish(0)

        l_pk_own.wait()
        for k in range(3):
            copy(6 + k, pk_rb.at[k], pk_rb.at[k], sib).wait_recv()
        chip_sum(pk_own, pk_rb, hp)
        l_pk_out = local(8, pk_own, pk_piece(pk_out, c, j))
        start(copy(15, pk_own, pk_piece(pk_out, c, j), sib))
        for k in range(3):
            start(copy(16 + k, pk_own, pk_piece(pk_out, c, j), (*chips[k], c)))

        slab_matmuls(2)
        slab_finish(1)
        slab_matmuls(3)
        slab_finish(2)

        l_wo_own.wait()
        for k in range(3):
            copy(9 + k, wo_rb.at[k], wo_rb.at[k], sib).wait_recv()
        chip_sum(wo_own, wo_rb, 512)
        l_wo_out = local(9, wo_own, cols_half(wo_out, c, 512))
        start(copy(22, wo_own, cols_half(wo_out, c, 512), sib))

        for k in range(3):
            reg = pk_piece(pk_out, c, cj[k])
            copy(16 + k, reg, reg, sib).wait_recv()
            start(copy(19 + k, reg, reg, sib))

        slab_finish(3)
        for k in range(3):
            copy(12 + k, wi_rb.at[k], wi_rb.at[k], sib).wait_recv()
        chip_sum(wi_mine.at[1], wi_rb, 512)
        l_wi_out = local(10, wi_mine.at[1], rows_half(wi_out, c, 512))
        start(copy(23, wi_mine.at[1], rows_half(wi_out, c, 512), sib))

        reg = pk_piece(pk_out, 1 - c, j)
        copy(15, reg, reg, sib).wait_recv()
        for k in range(3):
            reg = pk_piece(pk_out, 1 - c, cj[k])
            copy(19 + k, reg, reg, sib).wait_recv()
        reg = cols_half(wo_out, 1 - c, 512)
        copy(22, reg, reg, sib).wait_recv()
        reg = rows_half(wi_out, 1 - c, 512)
        copy(23, reg, reg, sib).wait_recv()
        for cp in sends + p1[2:]:
            cp.wait_send()
        for cp in (l_pk_out, l_wo_out, l_wi_out):
            cp.wait()

    return pl.pallas_call(
        body, name="grads_reduce",
        out_shape=(jax.ShapeDtypeStruct((D, wi_w), F32), jax.ShapeDtypeStruct((512, D), F32),
                   jax.ShapeDtypeStruct(pack.shape, F32)),
        in_specs=[ANY] * (5 + n_dz), out_specs=(ANY,) * 3,
        scratch_shapes=[
            pltpu.VMEM((lt, 512), BF16), pltpu.VMEM((lt, 512), BF16), pltpu.VMEM((lt, wi_w), BF16),
            pltpu.VMEM((2, 512, wi_w), BF16), pltpu.VMEM((2, 512, wi_w), F32), pltpu.VMEM((4, 512, wi_w), BF16),
            pltpu.VMEM((3, 512, wi_w), BF16), pltpu.VMEM((3, 512, wi_w), BF16),
            pltpu.VMEM((2048, 512), F32), pltpu.VMEM((2048, 512), F32), pltpu.VMEM((2048, 512), BF16),
            pltpu.VMEM((3, 512, 512), BF16), pltpu.VMEM((512, 512), F32),
            pltpu.VMEM((hp, 512), F32), pltpu.VMEM((hp, 512), F32), pltpu.VMEM((hp, 512), BF16),
            pltpu.VMEM((3, hp, 128), BF16), pltpu.VMEM((hp, 128), F32),
            pltpu.SemaphoreType.DMA((24,)), pltpu.SemaphoreType.DMA((24,)), pltpu.SemaphoreType.DMA((14,))],
        compiler_params=_cp(vmem_mb=56),
    )(hn, *dzs, hn_c, dxa_c, g_w_out, pack)


def _ada_bwd(c_all, dmx_all_j, dmc_j, dmx_all, dmc, c_ctx, ada_w_full):
    def body(c_ref, dmxj_ref, dmcj_ref, dmx_ref, dmc_ref, cc_ref, w_ref, gw_ref, gb_ref, gc_ref, lhs, rhs, dm8):
        lhs[...] = jnp.zeros_like(lhs)
        rhs[...] = jnp.zeros_like(rhs)
        cv = c_ref[...]
        lhs[0:8, :] = cv * _sigmoid(cv)
        cc = cc_ref[...]
        a_c, da_c = _silu_and_grad(cc)
        lhs[8:9, :] = a_c
        rhs[0:8, :] = dmxj_ref[...]
        rhs[8:9, :] = dmcj_ref[...]
        gw_ref[...] = _dot_tn(lhs[...].astype(BF16), rhs[...].astype(BF16))
        gb_ref[...] = jnp.sum(dmx_ref[...], axis=0, keepdims=True) + dmc_ref[...]
        dm8[...] = jnp.zeros_like(dm8)
        dm8[0:1, :] = dmc_ref[...]
        da = _dot_nt(dm8[...].astype(BF16), w_ref[...])
        gc_ref[...] = da[0:1, :] * da_c

    return pl.pallas_call(
        body, name="ada_bwd",
        out_shape=(jax.ShapeDtypeStruct((D, 768), F32), jax.ShapeDtypeStruct((1, 3 * D), F32),
                   jax.ShapeDtypeStruct((1, D), F32)),
        in_specs=[VMEM] * 7, out_specs=(VMEM,) * 3,
        scratch_shapes=[pltpu.VMEM((16, D), F32), pltpu.VMEM((16, 768), F32), pltpu.VMEM((8, 3 * D), F32)],
        compiler_params=_cp(vmem_mb=32),
    )(c_all, dmx_all_j, dmc_j, dmx_all, dmc, c_ctx, ada_w_full)


def _proj(x, mods, mrow, norm_g, w_full, nk, name):
    lx = x.shape[0]
    n = lx // T

    def body(x_ref, sh_ref, sc_ref, ng_ref, *rest):
        w_refs, hn_ref, z_refs = rest[:nk], rest[nk], rest[nk + 1:]
        xv = x_ref[...]
        r = lax.rsqrt(jnp.mean(xv * xv, axis=-1, keepdims=True) + NORM_EPS)
        hn = (xv * r) * ng_ref[...] * (1.0 + sc_ref[mrow:mrow + 1, :]) + sh_ref[mrow:mrow + 1, :]
        hb = hn.astype(BF16)
        hn_ref[...] = hb
        for k in range(nk):
            z_refs[k][...] = _dot(hb, w_refs[k][...])

    row = pl.BlockSpec((T, D), lambda i: (i, 0))
    in_specs = [row, pl.BlockSpec((8, D), lambda i: (0, 0)), pl.BlockSpec((8, D), lambda i: (0, 1)),
                pl.BlockSpec((1, D), lambda i: (0, 0))]
    in_specs += [pl.BlockSpec((D, D), lambda i, k=k: (0, k)) for k in range(nk)]
    out_shape = (jax.ShapeDtypeStruct((lx, D), BF16),) + tuple(jax.ShapeDtypeStruct((lx, D), F32) for _ in range(nk))
    return pl.pallas_call(
        body, name=name, grid=(n,), out_shape=out_shape, in_specs=in_specs, out_specs=(row,) * (nk + 1),
        compiler_params=_cp(1, vmem_mb=56),
    )(x, mods, mods, norm_g, *([w_full] * nk))


def _halo_specs(lx):
    last = lx // 8 - 1
    return [pl.BlockSpec((T, D), lambda i: (i, 0)),
            pl.BlockSpec((8, D), lambda i: (jnp.maximum(i * (T // 8) - 1, 0), 0)),
            pl.BlockSpec((8, D), lambda i: (jnp.minimum((i + 1) * (T // 8), last), 0))]


def _zhalo_specs(lx):
    last = lx // 8 - 1
    return [pl.BlockSpec((T * NH, HD), lambda i: (i, 0)),
            pl.BlockSpec((8 * NH, HD), lambda i: (jnp.maximum(i * (T // 8) - 1, 0), 0)),
            pl.BlockSpec((8 * NH, HD), lambda i: (jnp.minimum((i + 1) * (T // 8), last), 0))]


ZT = pl.BlockSpec((T * NH, HD), lambda i: (i, 0))
CONV_CHUNK = 32


def _lru_gates_fwd(xa, conv_wz, conv_bz, wcat, bcat, lamcat, name):
    lx = xa.shape[0]
    n = lx // T

    def body(xm, xp, xn, cw, cb, w_ref, b_ref, lam_ref, xaz_o, xcz_o, af_o, uf_o, ab_o, ub_o, pad):
        i = pl.program_id(0)
        pmask = jnp.where(i == 0, 0.0, 1.0)
        nmask = jnp.where(i == n - 1, 0.0, 1.0)
        for h in range(NH):
            cols = slice(HD * h, HD * h + HD)
            pad[_zrows(h, 8), :] = xp[:, cols] * pmask
            pad[pl.ds(8 * NH + h, T, stride=NH), :] = xm[:, cols]
            pad[pl.ds((T + 8) * NH + h, 8, stride=NH), :] = xn[:, cols] * nmask
        xaz_o[...] = pad[pl.ds(8 * NH, T * NH), :]

        def conv_chunk(ci, carry):
            base = pl.multiple_of(ci * (CONV_CHUNK * NH), CONV_CHUNK * NH)
            acc = None
            for k in range(4):
                sl = pad[pl.ds(base + (7 + k) * NH, CONV_CHUNK * NH), :].reshape(CONV_CHUNK, NH, HD)
                term = sl * cw[k][None]
                acc = term if acc is None else acc + term
            acc = acc + cb[...][None]
            xcz_o[pl.ds(base, CONV_CHUNK * NH), :] = acc.reshape(CONV_CHUNK * NH, HD)
            return carry
        lax.fori_loop(0, T // CONV_CHUNK, conv_chunk, 0)

        outs = ((af_o, uf_o), (ab_o, ub_o))
        for h in range(NH):
            xch = xcz_o[_zrows(h, T), :]
            pre = _dot(xch.astype(BF16), w_ref[h]) + b_ref[h:h + 1, :]
            for d in range(2):
                _, gi, _, _, a, mult = _lru_gate(pre, lam_ref[h:h + 1, :], d)
                outs[d][0][_zrows(h, T), :] = a
                outs[d][1][_zrows(h, T), :] = mult * gi * xch

    full = lambda shape: pl.BlockSpec(shape, lambda i: (0,) * len(shape))
    in_specs = _halo_specs(lx) + [full((4, NH, HD)), full((NH, HD)), full((NH, HD, 4 * HD)), full((NH, 4 * HD)),
                                  full((NH, 2 * HD))]
    zs = jax.ShapeDtypeStruct((lx * NH, HD), F32)
    return pl.pallas_call(
        body, name=name, grid=(n,), out_shape=(zs,) * 6, in_specs=in_specs, out_specs=(ZT,) * 6,
        scratch_shapes=[pltpu.VMEM(((T + 16) * NH, HD), F32)],
        compiler_params=_cp(1, vmem_mb=48),
    )(xa, xa, xa, conv_wz, conv_bz, wcat, bcat, lamcat)


def _scan(a_up, x_up, a_dn, x_dn, s_up, s_dn, post, name):
    lx = a_up.shape[0] // NH
    n = lx // T

    def body(au, xu, ad, xd, su0, sd0, *rest):
        if post:
            ou, od, fu, fd, carry = rest
        else:
            ou, od, pu, pd, fu, fd, carry = rest
        i = pl.program_id(0)

        @pl.when(i == 0)
        def _():
            carry[0] = su0[...]
            carry[1] = sd0[...]

        def step(k, s):
            su, sd = s
            ru = pl.ds(_mo(k * NH, NH), NH)
            rd = pl.ds(_mo((T - 1 - k) * NH, NH), NH)
            if post:
                vu = xu[ru, :] + su
                vd = xd[rd, :] + sd
                ou[ru, :] = vu
                od[rd, :] = vd
                return au[ru, :] * vu, ad[rd, :] * vd
            pu[ru, :] = su
            pd[rd, :] = sd
            vu = au[ru, :] * su + xu[ru, :]
            vd = ad[rd, :] * sd + xd[rd, :]
            ou[ru, :] = vu
            od[rd, :] = vd
            return vu, vd

        su, sd = lax.fori_loop(0, T, step, (carry[0], carry[1]), unroll=8)
        carry[0] = su
        carry[1] = sd
        fu[...] = su
        fd[...] = sd

    up = pl.BlockSpec((T * NH, HD), lambda i: (i, 0))
    dn = pl.BlockSpec((T * NH, HD), lambda i: (n - 1 - i, 0))
    st = pl.BlockSpec((NH, HD), lambda i: (0, 0))
    zs = jax.ShapeDtypeStruct((lx * NH, HD), F32)
    ss = jax.ShapeDtypeStruct((NH, HD), F32)
    if post:
        out_shape, out_specs = (zs, zs, ss, ss), (up, dn, st, st)
    else:
        out_shape, out_specs = (zs, zs, zs, zs, ss, ss), (up, dn, up, dn, st, st)
    return pl.pallas_call(
        body, name=name, grid=(n,), out_shape=out_shape, in_specs=[up, up, dn, dn, st, st], out_specs=out_specs,
        scratch_shapes=[pltpu.VMEM((2, NH, HD), F32)],
        compiler_params=_cp(1, vmem_mb=48),
    )(a_up, x_up, a_dn, x_dn, s_up, s_dn)


def _sgu_parts(u, v, lng, lnb, w_ref, bt_ref, mixed_s):
    ug, dug = _gelu_and_grad(u)
    vg, dvg = _gelu_and_grad(v)
    mu = jnp.mean(vg, axis=-1, keepdims=True)
    vc = vg - mu
    rstd = lax.rsqrt(jnp.mean(vc * vc, axis=-1, keepdims=True) + LN_EPS)
    vh = vc * rstd
    vn = (vh * lng + lnb).astype(BF16)
    for g in range(NH):
        cols = slice(HD * g, HD * g + HD)
        mixed_s[:, cols] = _dot(w_ref[g], vn[:, cols]) + bt_ref[:, g:g + 1]
    return ug, dug, dvg, rstd, vh, vn


def _sgu_fwd(u, v, ln_g, ln_b, sgu_w, sgu_bt):
    lx = u.shape[0]
    n = lx // HD

    def body(u_ref, v_ref, g_ref, b_ref, w_ref, bt_ref, y_ref, mixed_s):
        ug, _, _, _, _, _ = _sgu_parts(u_ref[...], v_ref[...], g_ref[...], b_ref[...], w_ref, bt_ref, mixed_s)
        y_ref[...] = ug * mixed_s[...]

    row = pl.BlockSpec((HD, D), lambda i: (i, 0))
    vec = pl.BlockSpec((1, D), lambda i: (0, 0))
    return pl.pallas_call(
        body, name="sgu_fwd", grid=(n,), out_shape=jax.ShapeDtypeStruct((lx, D), F32),
        in_specs=[row, row, vec, vec, pl.BlockSpec((NH, HD, HD), lambda i: (0, 0, 0)),
                  pl.BlockSpec((HD, NH), lambda i: (0, 0))],
        out_specs=row, scratch_shapes=[pltpu.VMEM((HD, D), F32)],
        compiler_params=_cp(1),
    )(u, v, ln_g, ln_b, sgu_w, sgu_bt)


def _sgu_bwd(u, v, dys, ln_g, ln_b, sgu_w, sgu_bt):
    lx = u.shape[0]
    n = lx // HD

    def body(u_ref, v_ref, dy_ref, g_ref, b_ref, w_ref, bt_ref, du_ref, dv_ref, dw_ref, db_ref, dg_ref, dbl_ref,
             mixed_s, dvn_s):
        i = pl.program_id(0)

        @pl.when(i == 0)
        def _():
            dw_ref[...] = jnp.zeros_like(dw_ref)
            db_ref[...] = jnp.zeros_like(db_ref)
            dg_ref[...] = jnp.zeros_like(dg_ref)
            dbl_ref[...] = jnp.zeros_like(dbl_ref)

        lng = g_ref[...]
        ug, dug, dvg, rstd, vh, vn = _sgu_parts(u_ref[...], v_ref[...], lng, b_ref[...], w_ref, bt_ref, mixed_s)
        dys_v = dy_ref[...]
        du_ref[...] = (dys_v * mixed_s[...] * dug).astype(BF16)
        dmix = dys_v * ug
        ones = jnp.ones((8, HD), BF16)
        for g in range(NH):
            cols = slice(HD * g, HD * g + HD)
            dm = dmix[:, cols]
            hi = dm.astype(BF16)
            lo = (dm - hi.astype(F32)).astype(BF16)
            dw_ref[g] += _dot_nt(hi, vn[:, cols])
            db_ref[g:g + 1, :] += (_dot_nt(ones, hi) + _dot_nt(ones, lo))[0:1, :]
            dvn_s[:, cols] = _dot_tn(w_ref[g], hi)
        dvn = dvn_s[...]
        dg_ref[...] += jnp.sum(dvn * vh, axis=0, keepdims=True)
        dbl_ref[...] += jnp.sum(dvn, axis=0, keepdims=True)
        dvh = dvn * lng
        dvg_in = rstd * (dvh - jnp.mean(dvh, axis=-1, keepdims=True)
                         - vh * jnp.mean(dvh * vh, axis=-1, keepdims=True))
        dv_ref[...] = (dvg_in * dvg).astype(BF16)

    row = pl.BlockSpec((HD, D), lambda i: (i, 0))
    vec = pl.BlockSpec((1, D), lambda i: (0, 0))
    wsp = pl.BlockSpec((NH, HD, HD), lambda i: (0, 0, 0))
    bsp = pl.BlockSpec((NH, HD), lambda i: (0, 0))
    return pl.pallas_call(
        body, name="sgu_bwd", grid=(n,),
        out_shape=(jax.ShapeDtypeStruct((lx, D), BF16), jax.ShapeDtypeStruct((lx, D), BF16),
                   jax.ShapeDtypeStruct((NH, HD, HD), F32), jax.ShapeDtypeStruct((NH, HD), F32),
                   jax.ShapeDtypeStruct((1, D), F32), jax.ShapeDtypeStruct((1, D), F32)),
        in_specs=[row, row, row, vec, vec, wsp, pl.BlockSpec((HD, NH), lambda i: (0, 0))],
        out_specs=(row, row, wsp, bsp, vec, vec),
        scratch_shapes=[pltpu.VMEM((HD, D), F32), pltpu.VMEM((HD, D), F32)],
        compiler_params=_cp(1),
    )(u, v, dys, ln_g, ln_b, sgu_w, sgu_bt)


def _out_fwd_bwd(hf_z, hb_z, ga, gb, ys, x, tgt, mods, final_g, w_out_full):
    lx = x.shape[0]
    n = lx // T

    def body(hf_ref, hb_ref, ga_ref, gb_ref, ys_ref, x_ref, t_ref, gx_ref, fg_ref, w_ref,
             loss_ref, dfg_ref, dgx_ref, dxn_ref, y_ref, do_ref, dga_ref, dgb_ref, dyl_ref, dys_ref, yl_s):
        i = pl.program_id(0)

        @pl.when(i == 0)
        def _():
            loss_ref[...] = jnp.zeros_like(loss_ref)
            dfg_ref[...] = jnp.zeros_like(dfg_ref)
            dgx_ref[...] = jnp.zeros_like(dgx_ref)

        for h in range(NH):
            yl_s[:, HD * h:HD * h + HD] = hf_ref[_zrows(h, T), :] + hb_ref[_zrows(h, T), :]
        yl = yl_s[...]
        gav = ga_ref[...]
        gbv = gb_ref[...]
        sa, dsa = _silu_and_grad(gav)
        sb, dsb = _silu_and_grad(gbv)
        ysv = ys_ref[...]
        y_ref[:, 0:D] = (yl * sa).astype(BF16)
        y_ref[:, D:2 * D] = (ysv * sb).astype(BF16)
        o = _dot(y_ref[...], w_ref[...])
        gx = gx_ref[0:1, :]
        xnew = x_ref[...] + gx * o
        r2 = lax.rsqrt(jnp.mean(xnew * xnew, axis=-1, keepdims=True) + NORM_EPS)
        xh = xnew * r2
        fg = fg_ref[...]
        err = xh * fg - t_ref[...]
        loss_ref[...] += 0.5 * jnp.sum(jnp.mean(err * err, axis=-1, keepdims=True), axis=0, keepdims=True)
        dout = err * (1.0 / D)
        dfg_ref[...] += jnp.sum(dout * xh, axis=0, keepdims=True)
        dxh = dout * fg
        dxn = r2 * (dxh - xh * jnp.mean(dxh * xh, axis=-1, keepdims=True))
        dxn_ref[...] = dxn
        dgx_ref[...] += jnp.sum(dxn * o, axis=0, keepdims=True)
        do = (dxn * gx).astype(BF16)
        do_ref[...] = do
        dy = _dot_nt(do, w_ref[...])
        dy1 = dy[:, 0:D]
        dy2 = dy[:, D:2 * D]
        dga_ref[...] = (dy1 * yl * dsa).astype(BF16)
        dgb_ref[...] = (dy2 * ysv * dsb).astype(BF16)
        dys_ref[...] = dy2 * sb
        yl_s[...] = dy1 * sa
        for h in range(NH):
            dyl_ref[_zrows(h, T), :] = yl_s[:, HD * h:HD * h + HD]

    row = pl.BlockSpec((T, D), lambda i: (i, 0))
    vec = pl.BlockSpec((1, D), lambda i: (0, 0))
    in_specs = [ZT, ZT, row, row, row, row, row, pl.BlockSpec((8, D), lambda i: (0, 2)), vec,
                pl.BlockSpec((2 * D, D), lambda i: (0, 0))]
    out_shape = (jax.ShapeDtypeStruct((1, 1), F32), jax.ShapeDtypeStruct((1, D), F32), jax.ShapeDtypeStruct((1, D), F32),
                 jax.ShapeDtypeStruct((lx, D), F32), jax.ShapeDtypeStruct((lx, 2 * D), BF16),
                 jax.ShapeDtypeStruct((lx, D), BF16), jax.ShapeDtypeStruct((lx, D), BF16),
                 jax.ShapeDtypeStruct((lx, D), BF16), jax.ShapeDtypeStruct((lx * NH, HD), F32),
                 jax.ShapeDtypeStruct((lx, D), F32))
    out_specs = (pl.BlockSpec((1, 1), lambda i: (0, 0)), vec, vec, row, pl.BlockSpec((T, 2 * D), lambda i: (i, 0)),
                 row, row, row, ZT, row)
    return pl.pallas_call(
        body, name="out_fwd_bwd", grid=(n,), out_shape=out_shape, in_specs=in_specs, out_specs=out_specs,
        scratch_shapes=[pltpu.VMEM((T, D), F32)],
        compiler_params=_cp(1, vmem_mb=56),
    )(hf_z, hb_z, ga, gb, ys, x, tgt, mods, final_g, w_out_full)


def _lru_gates_bwd(xc_z, lf_z, lb_z, pf_z, pb_z, wcat, bcat, lamcat, dw0, db0, dl0, name):
    lx = xc_z.shape[0] // NH
    n = lx // T

    def body(xc_ref, lf_ref, lb_ref, pf_ref, pb_ref, w_ref, b_ref, lam_ref, dw0_ref, db0_ref, dl0_ref,
             dxc_ref, dw_ref, db_ref, dl_ref, dpre_s):
        i = pl.program_id(0)

        @pl.when(i == 0)
        def _():
            dw_ref[...] = dw0_ref[...]
            db_ref[...] = db0_ref[...]
            dl_ref[...] = dl0_ref[...]

        lam_refs = (lf_ref, lb_ref)
        prev_refs = (pf_ref, pb_ref)
        for h in range(NH):
            xch = xc_ref[_zrows(h, T), :]
            xcb = xch.astype(BF16)
            pre = _dot(xcb, w_ref[h]) + b_ref[h:h + 1, :]
            dxc = jnp.zeros((T, HD), F32)
            for d in range(2):
                r, gi, lam, sp, a, mult = _lru_gate(pre, lam_ref[h:h + 1, :], d)
                du = lam_refs[d][_zrows(h, T), :]
                da = du * prev_refs[d][_zrows(h, T), :]
                dgi = du * mult * xch
                dxc = dxc + du * mult * gi
                dmult = du * gi * xch
                dla = da * a - dmult * (a * a) / mult
                dr = dla * ((-LRU_C) * sp)
                dsp = jnp.sum(dla * ((-LRU_C) * r), axis=0, keepdims=True)
                dl_ref[h:h + 1, HD * d:HD * d + HD] += dsp * (-_sigmoid(-lam))
                dpre_s[:, 256 * d:256 * d + HD] = dr * r * (1.0 - r)
                dpre_s[:, 256 * d + HD:256 * d + 2 * HD] = dgi * gi * (1.0 - gi)
            dpre = dpre_s[...]
            dpb = dpre.astype(BF16)
            dw_ref[h] += _dot_tn(xcb, dpb)
            db_ref[h:h + 1, :] += jnp.sum(dpre, axis=0, keepdims=True)
            dxc_ref[_zrows(h, T), :] = dxc + _dot_nt(dpb, w_ref[h])

    full = lambda shape: pl.BlockSpec(shape, lambda i: (0,) * len(shape))
    wsp, bsp, lsp = full((NH, HD, 4 * HD)), full((NH, 4 * HD)), full((NH, 2 * HD))
    return pl.pallas_call(
        body, name=name, grid=(n,),
        out_shape=(jax.ShapeDtypeStruct((lx * NH, HD), F32), jax.ShapeDtypeStruct((NH, HD, 4 * HD), F32),
                   jax.ShapeDtypeStruct((NH, 4 * HD), F32), jax.ShapeDtypeStruct((NH, 2 * HD), F32)),
        in_specs=[ZT] * 5 + [wsp, bsp, lsp, wsp, bsp, lsp], out_specs=(ZT, wsp, bsp, lsp),
        scratch_shapes=[pltpu.VMEM((T, 4 * HD), F32)],
        compiler_params=_cp(1, vmem_mb=48),
    )(xc_z, lf_z, lb_z, pf_z, pb_z, wcat, bcat, lamcat, dw0, db0, dl0)


def _conv_bwd(dxc_z, xa_z, conv_wz, dcw0, dcb0, name):
    lx = dxc_z.shape[0] // NH
    n = lx // T

    def body(dm, dp, dn, xa_ref, cw, dcw0_ref, dcb0_ref, dxa_ref, dcw_ref, dcb_ref, pad, dxa_s):
        i = pl.program_id(0)

        @pl.when(i == 0)
        def _():
            dcw_ref[...] = dcw0_ref[...]
            dcb_ref[...] = dcb0_ref[...]

        pmask = jnp.where(i == 0, 0.0, 1.0)
        nmask = jnp.where(i == n - 1, 0.0, 1.0)
        pad[pl.ds(0, 8 * NH), :] = dp[...] * pmask
        pad[pl.ds(8 * NH, T * NH), :] = dm[...]
        pad[pl.ds((T + 8) * NH, 8 * NH), :] = dn[...] * nmask

        def chunk(ci, carry):
            base = pl.multiple_of(ci * (CONV_CHUNK * NH), CONV_CHUNK * NH)
            xav = xa_ref[pl.ds(base, CONV_CHUNK * NH), :].reshape(CONV_CHUNK, NH, HD)
            acc = None
            for k in range(4):
                sl = pad[pl.ds(base + (9 - k) * NH, CONV_CHUNK * NH), :].reshape(CONV_CHUNK, NH, HD)
                term = sl * cw[k][None]
                acc = term if acc is None else acc + term
                dcw_ref[k] += jnp.sum(sl * xav, axis=0)
                if k == 1:
                    dcb_ref[...] += jnp.sum(sl, axis=0)
            dxa_s[pl.ds(base, CONV_CHUNK * NH), :] = acc.reshape(CONV_CHUNK * NH, HD)
            return carry
        lax.fori_loop(0, T // CONV_CHUNK, chunk, 0)
        for h in range(NH):
            dxa_ref[:, HD * h:HD * h + HD] = dxa_s[_zrows(h, T), :].astype(BF16)

    full = lambda shape: pl.BlockSpec(shape, lambda i: (0,) * len(shape))
    return pl.pallas_call(
        body, name=name, grid=(n,),
        out_shape=(jax.ShapeDtypeStruct((lx, D), BF16), jax.ShapeDtypeStruct((4, NH, HD), F32),
                   jax.ShapeDtypeStruct((NH, HD), F32)),
        in_specs=_zhalo_specs(lx) + [ZT, full((4, NH, HD)), full((4, NH, HD)), full((NH, HD))],
        out_specs=(pl.BlockSpec((T, D), lambda i: (i, 0)), full((4, NH, HD)), full((NH, HD))),
        scratch_shapes=[pltpu.VMEM(((T + 16) * NH, HD), F32), pltpu.VMEM((T * NH, HD), F32)],
        compiler_params=_cp(1, vmem_mb=48),
    )(dxc_z, dxc_z, dxc_z, xa_z, conv_wz, dcw0, dcb0)


def _proj_bwd(dzs, x, dxn, mods, mrow, norm_g, w_full, dng0, name):
    lx = x.shape[0]
    n = lx // T
    nk = len(dzs)
    has_x = dxn is not None

    def body(*refs):
        dz_refs = refs[:nk]
        w_refs = refs[nk:2 * nk]
        x_ref, sc_ref, ng_ref, dng0_ref = refs[2 * nk:2 * nk + 4]
        rest = refs[2 * nk + 4:]
        if has_x:
            dxn_ref, gx_ref, dng_ref, dsc_ref, dsh_ref = rest
        else:
            dng_ref, dsc_ref, dsh_ref = rest
        i = pl.program_id(0)

        @pl.when(i == 0)
        def _():
            dng_ref[...] = dng0_ref[...]
            dsc_ref[...] = jnp.zeros_like(dsc_ref)
            dsh_ref[...] = jnp.zeros_like(dsh_ref)

        dhn = _dot_nt(dz_refs[0][...], w_refs[0][...])
        for k in range(1, nk):
            dhn = dhn + _dot_nt(dz_refs[k][...], w_refs[k][...])
        xv = x_ref[...]
        r = lax.rsqrt(jnp.mean(xv * xv, axis=-1, keepdims=True) + NORM_EPS)
        xn = xv * r
        ng = ng_ref[...]
        sc1 = 1.0 + sc_ref[mrow:mrow + 1, :]
        t = dhn * xn
        dng_ref[...] += jnp.sum(t * sc1, axis=0, keepdims=True)
        dsc_ref[...] += jnp.sum(t * ng, axis=0, keepdims=True)
        dsh_ref[...] += jnp.sum(dhn, axis=0, keepdims=True)
        if has_x:
            dxh = dhn * (ng * sc1)
            gx_ref[...] = dxn_ref[...] + r * (dxh - xn * jnp.mean(dxh * xn, axis=-1, keepdims=True))

    row = pl.BlockSpec((T, D), lambda i: (i, 0))
    vec = pl.BlockSpec((1, D), lambda i: (0, 0))
    in_specs = [row] * nk + [pl.BlockSpec((D, D), lambda i, k=k: (0, k)) for k in range(nk)]
    in_specs += [row, pl.BlockSpec((8, D), lambda i: (0, 1)), vec, vec]
    args = list(dzs) + [w_full] * nk + [x, mods, norm_g, dng0]
    vs = jax.ShapeDtypeStruct((1, D), F32)
    out_shape, out_specs = (vs, vs, vs), (vec, vec, vec)
    if has_x:
        in_specs.append(row)
        args.append(dxn)
        out_shape = (jax.ShapeDtypeStruct((lx, D), F32),) + out_shape
        out_specs = (row,) + out_specs
    return pl.pallas_call(
        body, name=name, grid=(n,), out_shape=out_shape, in_specs=in_specs, out_specs=out_specs,
        compiler_params=_cp(1, vmem_mb=56),
    )(*args)


def _tn_matmul(a, bs, extra, name):
    lx, m = a.shape
    tm = min(lx, 1024)
    n = lx // tm
    widths = [b.shape[1] for b in bs]
    nb = len(bs)

    def body(a_ref, *rest):
        b_refs = rest[:nb]
        rest = rest[nb:]
        if extra is not None:
            ea_ref, eb_ref = rest[:2]
            rest = rest[2:]
        out_ref, acc = rest
        i = pl.program_id(0)
        av = a_ref[...]
        off = 0
        for k in range(nb):
            cols = slice(off, off + widths[k])
            part = _dot_tn(av, b_refs[k][...])

            @pl.when(i == 0)
            def _():
                acc[:, cols] = part

            @pl.when(i > 0)
            def _():
                acc[:, cols] += part
            off += widths[k]

        @pl.when(i == n - 1)
        def _():
            if extra is not None:
                acc[:, 0:widths[0]] += _dot_tn(ea_ref[...], eb_ref[...])
            pltpu.sync_copy(acc, out_ref)

    in_specs = [pl.BlockSpec((tm, m), lambda i: (i, 0))] + [pl.BlockSpec((tm, w), lambda i: (i, 0)) for w in widths]
    args = [a] + list(bs)
    if extra is not None:
        in_specs += [VMEM, VMEM]
        args += list(extra)
    return pl.pallas_call(
        body, name=name, grid=(n,), out_shape=jax.ShapeDtypeStruct((m, sum(widths)), F32),
        in_specs=in_specs, out_specs=ANY, scratch_shapes=[pltpu.VMEM((m, sum(widths)), F32)],
        compiler_params=_cp(1, vmem_mb=56),
    )(*args)


def _adam_math(w, g, m, v):
    m = ADAM_B1 * m + (1.0 - ADAM_B1) * g
    v = ADAM_B2 * v + (1.0 - ADAM_B2) * (g * g)
    m_hat = m / (1.0 - ADAM_B1 ** ADAM_STEP)
    v_hat = v / (1.0 - ADAM_B2 ** ADAM_STEP)
    delta = -ADAM_LR * (m_hat / (jnp.sqrt(v_hat) + ADAM_EPS) + ADAM_WD * w)
    return delta, m, v


def _adam_big(w, g, m, v, name):
    rows, cols = w.shape
    tr = 256

    def body(w_ref, g_ref, m_ref, v_ref, d_o, m_o, v_o):
        d, mm, vv = _adam_math(w_ref[...], g_ref[...], m_ref[...], v_ref[...])
        d_o[...] = d
        m_o[...] = mm
        v_o[...] = vv

    blk = pl.BlockSpec((tr, cols), lambda i: (i, 0))
    s = jax.ShapeDtypeStruct((rows, cols), F32)
    return pl.pallas_call(
        body, name=name, grid=(rows // tr,), out_shape=(s, s, s), in_specs=[blk] * 4, out_specs=(blk,) * 3,
        compiler_params=_cp(1, vmem_mb=48),
    )(w, g, m, v)


def _adam_small(items):
    ni = len(items)

    def body(*refs):
        ins, outs = refs[:4 * ni], refs[4 * ni:7 * ni]
        bufs_in, bufs_out = refs[7 * ni:11 * ni], refs[11 * ni:14 * ni]
        sem_in, sem_out = refs[14 * ni], refs[14 * ni + 1]
        loads = [pltpu.make_async_copy(ins[q], bufs_in[q], sem_in.at[q]) for q in range(4 * ni)]
        for cp in loads:
            cp.start()
        stores = []
        for k in range(ni):
            for q in range(4):
                loads[4 * k + q].wait()
            w_b, g_b, m_b, v_b = bufs_in[4 * k:4 * k + 4]
            res = _adam_math(w_b[...], g_b[...], m_b[...], v_b[...])
            for q in range(3):
                bufs_out[3 * k + q][...] = res[q]
                cp = pltpu.make_async_copy(bufs_out[3 * k + q], outs[3 * k + q], sem_out.at[3 * k + q])
                cp.start()
                stores.append(cp)
        for cp in stores:
            cp.wait()

    flat = [a for it in items for a in it]
    out_shape = tuple(jax.ShapeDtypeStruct(it[0].shape, F32) for it in items for _ in range(3))
    scratch = [pltpu.VMEM(a.shape, F32) for a in flat] + [pltpu.VMEM(s.shape, F32) for s in out_shape]
    scratch += [pltpu.SemaphoreType.DMA((4 * ni,)), pltpu.SemaphoreType.DMA((3 * ni,))]
    res = pl.pallas_call(
        body, name="adam_small", out_shape=out_shape, in_specs=[HBM] * (4 * ni), out_specs=(HBM,) * (3 * ni),
        scratch_shapes=scratch, compiler_params=_cp(vmem_mb=40),
    )(*flat)
    return [tuple(res[3 * k:3 * k + 3]) for k in range(ni)]


def kernel(x, c, ctx, c_ctx, ada_w, ada_b, norm_g, w_in, conv_w, conv_b, lru_wa, lru_ba, lru_wx, lru_bx, lru_lambda, sgu_ln_g, sgu_ln_b, sgu_w, sgu_b, w_out, final_g, loss_target, m_c_ctx, m_ada_w, m_ada_b, m_norm_g, m_w_in, m_conv_w, m_conv_b, m_lru_wa, m_lru_ba, m_lru_wx, m_lru_bx, m_lru_lambda, m_sgu_ln_g, m_sgu_ln_b, m_sgu_w, m_sgu_b, m_w_out, m_final_g, v_c_ctx, v_ada_w, v_ada_b, v_norm_g, v_w_in, v_conv_w, v_conv_b, v_lru_wa, v_lru_ba, v_lru_wx, v_lru_bx, v_lru_lambda, v_sgu_ln_g, v_sgu_ln_b, v_sgu_w, v_sgu_b, v_w_out, v_final_g):
    ix, iy, ic = lax.axis_index("x"), lax.axis_index("y"), lax.axis_index("c")
    chip = 2 * ix + iy
    dev = 2 * chip + ic
    lx = x.shape[1]
    lc = ctx.shape[1]

    smalls = jnp.concatenate([conv_w[0], lru_lambda[0], jnp.zeros((10, 256), F32)], axis=0)
    c_ctx2 = c_ctx.reshape(1, D)
    ada_b_j = lax.dynamic_slice(ada_b, (0, 768 * chip), (1, 768))
    mods, c_slots, sm_all, w_in_full, wo_land, ada_land = _gather_in(c, c_ctx2, ada_w[0], ada_b_j, w_in[0], w_out[0],
                                                                     smalls)
    wo_ss, wo_rs, ada_ss, ada_rs, wo_land, ada_land, token = _late_gather_start(wo_land, ada_land)
    mods = mods + token[0:1, 0:1]
    sm3 = sm_all.reshape(NCHIP, 16, 256)
    conv_w_full = sm3[:, 0:4, :].transpose(1, 0, 2).reshape(4, D)
    lam_full = sm3[:, 4:6, :].transpose(1, 0, 2).reshape(2, D)
    conv_wz = conv_w_full.reshape(4, NH, HD)
    conv_bz = conv_b.reshape(NH, HD)
    lamcat = lam_full.reshape(2, NH, HD).transpose(1, 0, 2).reshape(NH, 2 * HD)
    wa, wx, ba, bx = lru_wa[0], lru_wx[0], lru_ba[0], lru_bx[0]
    wcat = jnp.concatenate([wa[0], wx[0], wa[1], wx[1]], axis=-1).astype(BF16)
    bcat = jnp.concatenate([ba[0], bx[0], ba[1], bx[1]], axis=-1)
    sgu_wb = sgu_w[0].astype(BF16)
    sgu_bt = sgu_b[0].T
    final_g2 = final_g.reshape(1, D)

    zero_s = jnp.zeros((NH, HD), F32)
    hn_c, xa_c = _proj(ctx[0], mods, 1, norm_g, w_in_full, 1, "proj_ctx")
    xaz_c, xcz_c, af_c, uf_c, ab_c, ub_c = _lru_gates_fwd(xa_c, conv_wz, conv_bz, wcat, bcat, lamcat, "lru_gates_ctx")
    _, _, pf_c, pb_c, hf0, hb0 = _scan(af_c, uf_c, ab_c, ub_c, zero_s, zero_s, False, "scan_ctx")

    hn, xa, ga, u, v, gb = _proj(x[0], mods, 0, norm_g, w_in_full, 5, "proj")
    xaz, xcz, af, uf, ab, ub = _lru_gates_fwd(xa, conv_wz, conv_bz, wcat, bcat, lamcat, "lru_gates")
    hf, hb, pf, pb, _, _ = _scan(af, uf, ab, ub, hf0, hb0, False, "scan")
    ys = _sgu_fwd(u, v, sgu_ln_g, sgu_ln_b, sgu_wb, sgu_bt)

    w_out_full = _late_gather_wait(wo_land, wo_ss, wo_rs, "w_out", ys, "late_gather_wait_w_out")
    (loss_part, dfg, dgx, dxn, y, do, dga, dgb, dyl_z, dys) = _out_fwd_bwd(
        hf, hb, ga, gb, ys, x[0], loss_target[0], mods, final_g2, w_out_full)
    g_w_out_part = _tn_matmul(y, [do], None, "grad_w_out")

    du, dv, d_sgu_w, d_sgu_b, d_ln_g, d_ln_b = _sgu_bwd(u, v, dys, sgu_ln_g, sgu_ln_b, sgu_wb, sgu_bt)
    lb, lf, dh0b, dh0f = _scan(ab, dyl_z, af, dyl_z, zero_s, zero_s, True, "scan_adj")
    zw = jnp.zeros((NH, HD, 4 * HD), F32)
    zb = jnp.zeros((NH, 4 * HD), F32)
    zl = jnp.zeros((NH, 2 * HD), F32)
    dxc_z, dwc, dbc, dlc = _lru_gates_bwd(xcz, lf, lb, pf, pb, wcat, bcat, lamcat, zw, zb, zl, "lru_gates_bwd")
    dxa, dcw, dcb = _conv_bwd(dxc_z, xaz, conv_wz, jnp.zeros((4, NH, HD), F32), zero_s, "conv_bwd")

    zc = jnp.zeros((lc * NH, HD), F32)
    dhf_c = lax.dynamic_update_slice(zc, dh0f, ((lc - 1) * NH, 0))
    dhb_c = lax.dynamic_update_slice(zc, dh0b, (0, 0))
    lb_c, lf_c, _, _ = _scan(ab_c, dhb_c, af_c, dhf_c, zero_s, zero_s, True, "scan_adj_ctx")
    dxc_zc, dwc, dbc, dlc = _lru_gates_bwd(xcz_c, lf_c, lb_c, pf_c, pb_c, wcat, bcat, lamcat, dwc, dbc, dlc,
                                           "lru_gates_bwd_ctx")
    dxa_c, dcw, dcb = _conv_bwd(dxc_zc, xaz_c, conv_wz, dcw, dcb, "conv_bwd_ctx")

    dzs = [dxa, dga, du, dv, dgb]
    grad_x, dng, dsc_x, dsh_x = _proj_bwd(dzs, x[0], dxn, mods, 0, norm_g, w_in_full, jnp.zeros((1, D), F32), "proj_bwd")
    dng, dsc_c, dsh_c = _proj_bwd([dxa_c], ctx[0], None, mods, 1, norm_g, w_in_full, dng, "proj_bwd_ctx")

    dmx = jnp.concatenate([dsh_x, dsc_x, dgx], axis=0)
    dmc = jnp.concatenate([dsh_c, dsc_c, jnp.zeros((1, D), F32)], axis=0)
    slot = jnp.concatenate([dmx, jnp.zeros((1, D), F32)], axis=0)
    slots = lax.dynamic_update_slice(jnp.zeros((32, D), F32), slot, (4 * dev, 0))
    vecs = jnp.concatenate([dfg, dng, dcb.reshape(1, D), d_ln_g, d_ln_b, dcw.reshape(4, D), dmc,
                            jnp.zeros((4, D), F32), slots], axis=0)
    d_sgu_w4 = d_sgu_w.reshape(4, 256, HD).transpose(1, 0, 2).reshape(256, 4 * HD)
    pad8 = lambda a: jnp.pad(a, ((0, 8 - a.shape[0]), (0, 4 * HD - a.shape[1])))
    pack = jnp.concatenate([dwc.reshape(NH * HD, 4 * HD), pad8(dbc), pad8(dlc), d_sgu_w4, pad8(d_sgu_b),
                            vecs.reshape(96, 4 * HD), jnp.zeros((8, 4 * HD), F32)], axis=0)
    g_w_in, g_w_out, tot = _grads_reduce(hn, dzs, hn_c, dxa_c, g_w_out_part, pack)

    g_wc = tot[0:1024].reshape(NH, HD, 4 * HD)
    g_bc = tot[1024:1032]
    g_lc = tot[1032:1040, 0:2 * HD]
    g_sgu_w = tot[1040:1296].reshape(256, 4, HD).transpose(1, 0, 2).reshape(NH, HD, HD)
    g_sgu_b = tot[1296:1304, 0:HD]
    tv = tot[1304:1400].reshape(48, D)
    g_final_g, g_norm_g, g_conv_b, g_ln_g, g_ln_b = tv[0:1], tv[1:2], tv[2:3], tv[3:4], tv[4:5]
    g_conv_w_full = tv[5:9]
    dmc_tot = tv[9:12].reshape(1, 3 * D)
    slots_all = tv[16:48].reshape(8, 4, D)
    dmx_all = slots_all[:, 0:3, :].reshape(8, 3 * D)
    c_all = c_slots.reshape(8, 8, D)[:, 0, :]
    g_lru_wa = jnp.stack([g_wc[:, :, 0:HD], g_wc[:, :, 2 * HD:3 * HD]])
    g_lru_wx = jnp.stack([g_wc[:, :, HD:2 * HD], g_wc[:, :, 3 * HD:4 * HD]])
    g_lru_ba = jnp.stack([g_bc[:, 0:HD], g_bc[:, 2 * HD:3 * HD]])
    g_lru_bx = jnp.stack([g_bc[:, HD:2 * HD], g_bc[:, 3 * HD:4 * HD]])
    g_lam_full = jnp.stack([g_lc[:, 0:HD], g_lc[:, HD:2 * HD]]).reshape(2, D)
    g_conv_w = lax.dynamic_slice(g_conv_w_full, (0, 256 * chip), (4, 256))
    g_lam = lax.dynamic_slice(g_lam_full, (0, 256 * chip), (2, 256))
    dmx_all_j = lax.dynamic_slice(dmx_all, (0, 768 * chip), (8, 768))
    dmc_j = lax.dynamic_slice(dmc_tot, (0, 768 * chip), (1, 768))
    ada_full = _late_gather_wait(ada_land, ada_ss, ada_rs, "ada_w", tot, "late_gather_wait_ada_w")
    g_ada_w, g_ada_b, g_c_ctx = _ada_bwd(c_all, dmx_all_j, dmc_j, dmx_all, dmc_tot, c_ctx2, ada_full)

    big = {
        "ada_w": _adam_big(ada_w[0], g_ada_w, m_ada_w[0], v_ada_w[0], "adam_ada_w"),
        "w_in": _adam_big(w_in[0], g_w_in, m_w_in[0], v_w_in[0], "adam_w_in"),
        "w_out": _adam_big(w_out[0], g_w_out, m_w_out[0], v_w_out[0], "adam_w_out"),
    }
    small_in = {
        "c_ctx": (c_ctx, g_c_ctx, m_c_ctx, v_c_ctx, (1, D)),
        "ada_b": (ada_b, g_ada_b, m_ada_b, v_ada_b, (1, 3 * D)),
        "norm_g": (norm_g, g_norm_g, m_norm_g, v_norm_g, (1, D)),
        "conv_w": (conv_w, g_conv_w, m_conv_w, v_conv_w, (4, 256)),
        "conv_b": (conv_b, g_conv_b, m_conv_b, v_conv_b, (1, D)),
        "lru_wa": (lru_wa, g_lru_wa, m_lru_wa, v_lru_wa, (2 * NH * HD, HD)),
        "lru_ba": (lru_ba, g_lru_ba, m_lru_ba, v_lru_ba, (2 * NH, HD)),
        "lru_wx": (lru_wx, g_lru_wx, m_lru_wx, v_lru_wx, (2 * NH * HD, HD)),
        "lru_bx": (lru_bx, g_lru_bx, m_lru_bx, v_lru_bx, (2 * NH, HD)),
        "lru_lambda": (lru_lambda, g_lam, m_lru_lambda, v_lru_lambda, (2, 256)),
        "sgu_ln_g": (sgu_ln_g, g_ln_g, m_sgu_ln_g, v_sgu_ln_g, (1, D)),
        "sgu_ln_b": (sgu_ln_b, g_ln_b, m_sgu_ln_b, v_sgu_ln_b, (1, D)),
        "sgu_w": (sgu_w, g_sgu_w, m_sgu_w, v_sgu_w, (NH * HD, HD)),
        "sgu_b": (sgu_b, g_sgu_b, m_sgu_b, v_sgu_b, (NH, HD)),
        "final_g": (final_g, g_final_g, m_final_g, v_final_g, (1, D)),
    }
    names_small = list(small_in)
    res_small = _adam_small([tuple(a.reshape(small_in[k][4]) for a in small_in[k][:4]) for k in names_small])
    full_shapes = {"ada_w": ada_w.shape, "w_in": w_in.shape, "w_out": w_out.shape}
    grads, deltas, new_m, new_v = {}, {}, {}, {}
    for k in ("ada_w", "w_in", "w_out"):
        g = {"ada_w": g_ada_w, "w_in": g_w_in, "w_out": g_w_out}[k]
        grads[k] = g.reshape(full_shapes[k])
        deltas[k], new_m[k], new_v[k] = (a.reshape(full_shapes[k]) for a in big[k])
    for k, res in zip(names_small, res_small):
        shape = small_in[k][0].shape
        grads[k] = small_in[k][1].reshape(shape)
        deltas[k], new_m[k], new_v[k] = (a.reshape(shape) for a in res)

    loss = lax.psum(loss_part[0, 0], ("x", "y", "c"))
    order = ["c_ctx", "ada_w", "ada_b", "norm_g", "w_in", "conv_w", "conv_b", "lru_wa", "lru_ba", "lru_wx", "lru_bx",
             "lru_lambda", "sgu_ln_g", "sgu_ln_b", "sgu_w", "sgu_b", "w_out", "final_g"]
    return (loss, grad_x.reshape(x.shape), *[grads[k] for k in order], *[deltas[k] for k in order],
            *[new_m[k] for k in order], *[new_v[k] for k in order])
```

```python
import functools

import jax
import jax.numpy as jnp
from jax import lax
from jax.experimental import pallas as pl
from jax.experimental.pallas import tpu as pltpu

F32 = jnp.float32
BF16 = jnp.bfloat16

D = 1024
NH = 8
HD = 128
NCHIP = 4
T = 256
NORM_EPS = 1e-6
LN_EPS = 1e-5
LRU_C = 8.0
ADAM_LR = 0.001
ADAM_B1 = 0.9
ADAM_B2 = 0.999
ADAM_EPS = 1e-08
ADAM_WD = 0.01
ADAM_STEP = 10

VMEM = pl.BlockSpec(memory_space=pltpu.VMEM)
ANY = pl.BlockSpec(memory_space=pl.ANY)
MESH = pl.DeviceIdType.MESH


def _cp(n_grid=0, vmem_mb=None):
    kw = {}
    if n_grid:
        kw["dimension_semantics"] = ("arbitrary",) * n_grid
    if vmem_mb:
        kw["vmem_limit_bytes"] = vmem_mb << 20
    return pltpu.CompilerParams(**kw)


def _sigmoid(x):
    return 1.0 / (1.0 + jnp.exp(-x))


def _silu_and_grad(x):
    s = _sigmoid(x)
    return x * s, s * (1.0 + x * (1.0 - s))


_GELU_K = 0.7978845608028654
_GELU_C = 0.044715


def _gelu_and_grad(x):
    x2 = x * x
    th = jnp.tanh(_GELU_K * (x + _GELU_C * x * x2))
    g = 0.5 * x * (1.0 + th)
    dg = 0.5 * (1.0 + th) + 0.5 * x * (1.0 - th * th) * (_GELU_K * (1.0 + 3.0 * _GELU_C * x2))
    return g, dg


def _softplus(x):
    return jnp.maximum(x, 0.0) + jnp.log1p(jnp.exp(-jnp.abs(x)))


def _lru_gate(pre, lam_row, d):
    r = _sigmoid(pre[:, 256 * d:256 * d + HD])
    gi = _sigmoid(pre[:, 256 * d + HD:256 * d + 2 * HD])
    lam = lam_row[:, HD * d:HD * d + HD]
    sp = _softplus(-lam)
    la = (-LRU_C) * r * sp
    a = jnp.exp(la)
    x2 = 2.0 * la
    m2 = jnp.where(x2 > -1e-3, -x2 * (1.0 + 0.5 * x2), 1.0 - a * a)
    mult = jnp.sqrt(m2)
    return r, gi, lam, sp, a, mult


def _dot(a, b):
    return jnp.dot(a, b, preferred_element_type=F32)


def _dot_tn(a, b):
    return lax.dot_general(a, b, (((0,), (0,)), ((), ())), preferred_element_type=F32)


def _dot_nt(a, b):
    return lax.dot_general(a, b, (((1,), (1,)), ((), ())), preferred_element_type=F32)


def _mo(v, m):
    return v if isinstance(v, int) else pl.multiple_of(v, m)


def _zrows(h, n):
    return pl.ds(h, n, stride=NH)


def _gather_in(c, c_ctx, ada_w, ada_b_j, w_in, w_out, smalls):
    specs = [
        ((64, 256), F32, lambda r, jj, cc: r.at[pl.ds(_mo(16 * jj + 8 * cc, 8), 8), :]),
        ((D, 5120), BF16, lambda r, jj, cc: r.at[pl.ds(_mo(512 * cc, 16), 512), pl.ds(_mo(1280 * jj, 128), 1280)]),
    ]
    halves = [lambda r, cc: r.at[pl.ds(_mo(8 * cc, 8), 8), :],
              lambda r, cc: r.at[pl.ds(_mo(512 * cc, 16), 512), :]]
    na = len(specs)
    n_sem = 6 * na + 10

    def body(c_ref, cc_ref, ada_ref, adab_ref, win_ref, wout_ref, sm_ref,
             mods_o, call_o, sm_o, win_o, wol_o, adal_o, s_win, s_ada, s_wout, cslot, lhs, mbuf,
             send_sems, recv_sems, local_sems):
        x, y, c = lax.axis_index("x"), lax.axis_index("y"), lax.axis_index("c")
        j = 2 * x + y
        dev = 2 * j + c
        sib = (x, y, 1 - c)
        chips = [(1 - x, y), (x, 1 - y), (1 - x, 1 - y)]
        cj = [2 * cx + cy for cx, cy in chips]
        outs = [sm_o, win_o]
        srcs = [sm_ref, s_win]

        def copy(idx, src, dst, to):
            return pltpu.make_async_remote_copy(src_ref=src, dst_ref=dst, send_sem=send_sems.at[idx],
                                                recv_sem=recv_sems.at[idx], device_id=to, device_id_type=MESH)

        sends = []

        def start(cp):
            cp.start()
            sends.append(cp)

        cslot[...] = jnp.zeros_like(cslot)
        cslot[0:1, :] = c_ref[...]
        my_slot = pl.ds(_mo(8 * dev, 8), 8)
        others = [sib] + [(*chips[k], c) for k in range(3)] + [(*chips[k], 1 - c) for k in range(3)]
        other_dev = [dev + 1 - 2 * c] + [2 * cj[k] + c for k in range(3)] + [2 * cj[k] + 1 - c for k in range(3)]
        base = 6 * na
        for r in range(7):
            start(copy(base + r, cslot, call_o.at[my_slot, :], others[r]))
        call_o[my_slot, :] = cslot[...]

        s_win[...] = win_ref[...].astype(BF16)
        local = []
        for a in range(na):
            for cc in range(2):
                lc = pltpu.make_async_copy(halves[a](srcs[a], cc), specs[a][2](outs[a], j, cc), local_sems.at[2 * a + cc])
                lc.start()
                local.append(lc)
        for k in range(2):
            for a in range(na):
                start(copy(6 * a + k, halves[a](srcs[a], c), specs[a][2](outs[a], j, c), (*chips[k], c)))
        s_wout[...] = wout_ref[...].astype(BF16)
        s_ada[...] = ada_ref[...].astype(BF16)
        for q, (src, dst) in enumerate([(s_wout, wol_o.at[pl.ds(_mo(512 * j, 16), 512), :]),
                                        (s_ada, adal_o.at[:, pl.ds(_mo(768 * j, 128), 768)])]):
            lc = pltpu.make_async_copy(src, dst, local_sems.at[2 * na + q])
            lc.start()
            local.append(lc)

        for r in range(7):
            slot = call_o.at[pl.ds(_mo(8 * other_dev[r], 8), 8), :]
            copy(base + r, slot, slot, sib).wait_recv()
        lhs[...] = jnp.zeros_like(lhs)
        for b in range(8):
            cv = call_o[8 * b:8 * b + 1, :]
            lhs[b:b + 1, :] = cv * _sigmoid(cv)
        cv = cc_ref[...]
        lhs[8:9, :] = cv * _sigmoid(cv)
        mbuf[j] = _dot(lhs[...].astype(BF16), s_ada[...]) + adab_ref[...]
        for k in range(3):
            start(copy(base + 7 + k, mbuf.at[j], mbuf.at[j], (*chips[k], c)))
        for k in range(3):
            copy(base + 7 + k, mbuf.at[cj[k]], mbuf.at[cj[k]], sib).wait_recv()
        mods_o[...] = jnp.zeros_like(mods_o)
        for jj in range(NCHIP):
            mods_o[0:1, 768 * jj:768 * jj + 768] = mbuf[jj, pl.ds(dev, 1), :]
            mods_o[1:2, 768 * jj:768 * jj + 768] = mbuf[jj, 8:9, :]

        kx = [1 - x, x, 1 - x]
        ky = [y, 1 - y, 1 - y]
        pick = lambda k, lst: jnp.where(k == 0, lst[0], jnp.where(k == 1, lst[1], lst[2]))
        for step, k in enumerate([c, 1 - c, 2]):
            for a in range(na):
                reg = specs[a][2](outs[a], pick(k, cj), c)
                copy(6 * a + k, reg, reg, sib).wait_recv()
                if step == 0:
                    start(copy(6 * a + 2, reg, reg, (pick(1 - c, kx), pick(1 - c, ky), c)))
                start(copy(6 * a + 3 + k, reg, reg, sib))
        for k in range(3):
            for a in range(na):
                reg = specs[a][2](outs[a], cj[k], 1 - c)
                copy(6 * a + 3 + k, reg, reg, sib).wait_recv()
        for cp in sends:
            cp.wait_send()
        for lc in local:
            lc.wait()

    out_shape = (jax.ShapeDtypeStruct((8, 3 * D), F32), jax.ShapeDtypeStruct((64, D), F32),
                 jax.ShapeDtypeStruct(specs[0][0], F32), jax.ShapeDtypeStruct(specs[1][0], BF16),
                 jax.ShapeDtypeStruct((2048, D), BF16), jax.ShapeDtypeStruct((D, 3 * D), BF16))
    return pl.pallas_call(
        body, name="gather_in", out_shape=out_shape,
        in_specs=[VMEM] * 7, out_specs=(VMEM, VMEM, VMEM, ANY, ANY, ANY),
        scratch_shapes=[pltpu.VMEM((D, 1280), BF16), pltpu.VMEM((D, 768), BF16), pltpu.VMEM((512, D), BF16),
                        pltpu.VMEM((8, D), F32), pltpu.VMEM((16, D), F32), pltpu.VMEM((NCHIP, 16, 768), F32),
                        pltpu.SemaphoreType.DMA((n_sem,)), pltpu.SemaphoreType.DMA((n_sem,)),
                        pltpu.SemaphoreType.DMA((2 * na + 2,))],
        compiler_params=_cp(vmem_mb=56),
    )(c, c_ctx, ada_w, ada_b_j, w_in, w_out, smalls)


HBM = pl.BlockSpec(memory_space=pltpu.HBM)
SEM = pl.BlockSpec(memory_space=pltpu.SEMAPHORE)


def _late_gather_regions(x, y, c):
    chips = [(1 - x, y), (x, 1 - y), (1 - x, 1 - y)]
    wo_reg = lambda r, jj, cc: r.at[pl.ds(_mo(512 * jj + 256 * cc, 16), 256), :]
    ada_reg = lambda r, jj, cc: r.at[pl.ds(_mo(512 * cc, 16), 512), pl.ds(_mo(768 * jj, 128), 768)]
    return chips, wo_reg, ada_reg


def _late_gather_start(wo_land, ada_land):
    def body(wol_ref, adal_ref, wo_ss, wo_rs, ada_ss, ada_rs, wol_thru, adal_thru, token):
        x, y, c = lax.axis_index("x"), lax.axis_index("y"), lax.axis_index("c")
        j = 2 * x + y
        chips, wo_reg, ada_reg = _late_gather_regions(x, y, c)
        for k in range(3):
            for cc in range(2):
                pltpu.make_async_remote_copy(src_ref=wo_reg(wol_ref, j, c), dst_ref=wo_reg(wol_ref, j, c),
                                             send_sem=wo_ss.at[2 * k + cc], recv_sem=wo_rs.at[2 * k + c],
                                             device_id=(*chips[k], cc), device_id_type=MESH).start()
        for k in range(3):
            for cc in range(2):
                pltpu.make_async_remote_copy(src_ref=ada_reg(adal_ref, j, c), dst_ref=ada_reg(adal_ref, j, c),
                                             send_sem=ada_ss.at[2 * k + cc], recv_sem=ada_rs.at[2 * k + c],
                                             device_id=(*chips[k], cc), device_id_type=MESH).start()
        token[...] = jnp.zeros_like(token)

    sems = pltpu.SemaphoreType.DMA((6,))
    return pl.pallas_call(
        body, name="late_gather_start",
        out_shape=(sems, sems, sems, sems, pltpu.HBM(wo_land.shape, BF16), pltpu.HBM(ada_land.shape, BF16),
                   jax.ShapeDtypeStruct((8, 128), F32)),
        in_specs=(HBM, HBM), out_specs=(SEM, SEM, SEM, SEM, HBM, HBM, VMEM), input_output_aliases={0: 4, 1: 5},
        compiler_params=pltpu.CompilerParams(has_side_effects=pltpu.SideEffectType.DATAFLOW_SIDE_EFFECTING),
    )(pltpu.with_memory_space_constraint(wo_land, pltpu.HBM), pltpu.with_memory_space_constraint(ada_land, pltpu.HBM))


def _late_gather_wait(land, send_sems, recv_sems, which, after, name):
    def body(land_ref, ss, rs, after_ref, land_out):
        x, y, c = lax.axis_index("x"), lax.axis_index("y"), lax.axis_index("c")
        j = 2 * x + y
        chips, wo_reg, ada_reg = _late_gather_regions(x, y, c)
        reg = wo_reg if which == "w_out" else ada_reg
        for k in range(3):
            kj = 2 * chips[k][0] + chips[k][1]
            for cc in range(2):
                cp = pltpu.make_async_remote_copy(src_ref=reg(land_ref, j, c), dst_ref=reg(land_ref, kj, cc),
                                                  send_sem=ss.at[2 * k + cc], recv_sem=rs.at[2 * k + cc],
                                                  device_id=(*chips[k], cc), device_id_type=MESH)
                cp.wait_send()
                cp.wait_recv()

    return pl.pallas_call(
        body, name=name, out_shape=pltpu.HBM(land.shape, land.dtype),
        in_specs=(HBM, SEM, SEM, ANY), out_specs=HBM, input_output_aliases={0: 0},
        compiler_params=pltpu.CompilerParams(has_side_effects=pltpu.SideEffectType.DATAFLOW_SIDE_EFFECTING),
    )(land, send_sems, recv_sems, after)


RCHUNK = 16


def _grads_reduce(hn, dzs, hn_c, dxa_c, g_w_out, pack):
    rp = pack.shape[0]
    hp = rp // 2
    assert hp % RCHUNK == 0
    wi_w = 1280
    lx, lc = hn.shape[0], hn_c.shape[0]
    lt = lx + lc
    n_dz = len(dzs)

    def body(*refs):
        hn_hbm, dz_hbm = refs[0], refs[1:1 + n_dz]
        hnc_hbm, dxac_hbm, wo_hbm, pk_hbm, wi_out, wo_out, pk_out = refs[1 + n_dz:8 + n_dz]
        (hn_mine, hn_other, dzbuf, wi_other, wi_mine, wi_recv, wi_send, wi_rb,
         wo_mine, wo_recv, wo_send, wo_rb, wo_own, pk_mine, pk_recv, pk_send, pk_rb, pk_own,
         send_sems, recv_sems, local_sems) = refs[8 + n_dz:]
        x, y, c = lax.axis_index("x"), lax.axis_index("y"), lax.axis_index("c")
        j = 2 * x + y
        sib = (x, y, 1 - c)
        chips = [(1 - x, y), (x, 1 - y), (1 - x, 1 - y)]
        cj = [2 * cx + cy for cx, cy in chips]
        slabs = cj + [j]

        def copy(k, src, dst, to):
            return pltpu.make_async_remote_copy(src_ref=src, dst_ref=dst, send_sem=send_sems.at[k],
                                                recv_sem=recv_sems.at[k], device_id=to, device_id_type=MESH)

        def local(k, src, dst):
            cp = pltpu.make_async_copy(src, dst, local_sems.at[k])
            cp.start()
            return cp

        rows_half = lambda r, cc, n: r.at[pl.ds(_mo(cc * n, 16), n), :]
        cols_half = lambda r, cc, n: r.at[:, pl.ds(_mo(cc * n, 128), n)]
        pk_piece = lambda r, cc, jj: r.at[pl.ds(_mo(cc * hp, 16), hp), pl.ds(_mo(jj * 128, 128), 128)]

        sends = []

        def start(cp):
            cp.start()
            sends.append(cp)

        def dz_pieces(s):
            g0 = wi_w * s
            k0, off0 = g0 // D, g0 % D
            w0 = min(D - off0, wi_w)
            pieces = [(k0, off0, w0, 0)]
            if w0 < wi_w:
                pieces.append((k0 + 1, 0, wi_w - w0, w0))
            return pieces

        def dz_copies(s):
            cps = []
            for q, (k, off, w, dst) in enumerate(dz_pieces(s)):
                cps.append(pltpu.make_async_copy(dz_hbm[k].at[:, pl.ds(off, w)], dzbuf.at[pl.ds(0, lx), pl.ds(dst, w)],
                                                 local_sems.at[11 + q]))
            if s == 0:
                cps.append(pltpu.make_async_copy(dxac_hbm, dzbuf.at[pl.ds(lx, lc), pl.ds(0, D)], local_sems.at[13]))
            return cps

        def dz_load(sl):
            for s in range(NCHIP):
                @pl.when(sl == s)
                def _():
                    if s == 0:
                        dzbuf[pl.ds(lx, lc), pl.ds(D, wi_w - D)] = jnp.zeros((lc, wi_w - D), BF16)
                    else:
                        dzbuf[pl.ds(lx, lc), :] = jnp.zeros((lc, wi_w), BF16)
                    for cp in dz_copies(s):
                        cp.start()

        def dz_wait(sl):
            for s in range(NCHIP):
                @pl.when(sl == s)
                def _():
                    for cp in dz_copies(s):
                        cp.wait()

        l_pk = local(0, rows_half(pk_hbm, c, hp), pk_mine)
        start(copy(0, rows_half(pk_hbm, 1 - c, hp), pk_recv, sib))
        l_wo = local(1, cols_half(wo_hbm, c, 512), wo_mine)
        start(copy(1, cols_half(wo_hbm, 1 - c, 512), wo_recv, sib))
        hn_loads = [local(2, hn_hbm.at[:, pl.ds(_mo(c * 512, 128), 512)], hn_mine.at[pl.ds(0, lx), :]),
                    local(3, hnc_hbm.at[:, pl.ds(_mo(c * 512, 128), 512)], hn_mine.at[pl.ds(lx, lc), :]),
                    local(4, hn_hbm.at[:, pl.ds(_mo((1 - c) * 512, 128), 512)], hn_other.at[pl.ds(0, lx), :]),
                    local(5, hnc_hbm.at[:, pl.ds(_mo((1 - c) * 512, 128), 512)], hn_other.at[pl.ds(lx, lc), :])]
        dz_load(slabs[0])

        def pair_sum(mine, recv, send, nrows, keep):
            def step(i, carry):
                rows = pl.ds(_mo(i * RCHUNK, RCHUNK), RCHUNK)
                s = mine[rows, :] + recv[rows, :].astype(F32)
                if keep:
                    mine[rows, :] = s
                if send is not None:
                    send[rows, :] = s.astype(BF16)
                return carry
            lax.fori_loop(0, nrows // RCHUNK, step, 0)

        def chip_sum(own, rb, nrows):
            def step(i, carry):
                rows = pl.ds(_mo(i * RCHUNK, RCHUNK), RCHUNK)
                own[rows, :] = (own[rows, :] + rb[0, rows, :].astype(F32)
                                + rb[1, rows, :].astype(F32) + rb[2, rows, :].astype(F32))
                return carry
            lax.fori_loop(0, nrows // RCHUNK, step, 0)

        p1 = [None] * NCHIP

        def slab_matmuls(s):
            if s >= 2:
                p1[s - 2].wait_send()
            dz_wait(slabs[s])
            wi_other[s % 2] = _dot_tn(hn_other[...], dzbuf[...]).astype(BF16)
            p1[s] = copy(2 + s, wi_other.at[s % 2], wi_recv.at[s], sib)
            p1[s].start()
            wi_mine[s % 2] = _dot_tn(hn_mine[...], dzbuf[...])
            if s + 1 < NCHIP:
                dz_load(slabs[s + 1])

        def slab_finish(s):
            copy(2 + s, wi_recv.at[s], wi_recv.at[s], sib).wait_recv()
            if s < 3:
                pair_sum(wi_mine.at[s % 2], wi_recv.at[s], wi_send.at[s], 512, False)
                start(copy(12 + s, wi_send.at[s], wi_rb.at[s], (*chips[s], c)))
            else:
                pair_sum(wi_mine.at[s % 2], wi_recv.at[s], None, 512, True)

        l_wo.wait()
        copy(1, wo_recv, wo_recv, sib).wait_recv()
        pair_sum(wo_mine, wo_recv, wo_send, 2048, True)
        for k in range(3):
            start(copy(9 + k, wo_send.at[pl.ds(_mo(cj[k] * 512, 16), 512), :], wo_rb.at[k], (*chips[k], c)))
        l_wo_own = local(7, wo_mine.at[pl.ds(_mo(j * 512, 16), 512), :], wo_own)

        for cp in hn_loads:
            cp.wait()
        slab_matmuls(0)

        l_pk.wait()
        copy(0, pk_recv, pk_recv, sib).wait_recv()
        pair_sum(pk_mine, pk_recv, pk_send, hp, True)
        for k in range(3):
            start(copy(6 + k, pk_send.at[:, pl.ds(_mo(cj[k] * 128, 128), 128)], pk_rb.at[k], (*chips[k], c)))
        l_pk_own = local(6, pk_mine.at[:, pl.ds(_mo(j * 128, 128), 128)], pk_own)

        slab_matmuls(1)
        slab_finish(0)

        l_pk_own.wait()
        for k in range(3):
            copy(6 + k, pk_rb.at[k], pk_rb.at[k], sib).wait_recv()
        chip_sum(pk_own, pk_rb, hp)
        l_pk_out = local(8, pk_own, pk_piece(pk_out, c, j))
        start(copy(15, pk_own, pk_piece(pk_out, c, j), sib))
        for k in range(3):
            start(copy(16 + k, pk_own, pk_piece(pk_out, c, j), (*chips[k], c)))

        slab_matmuls(2)
        slab_finish(1)
        slab_matmuls(3)
        slab_finish(2)

        l_wo_own.wait()
        for k in range(3):
            copy(9 + k, wo_rb.at[k], wo_rb.at[k], sib).wait_recv()
        chip_sum(wo_own, wo_rb, 512)
        l_wo_out = local(9, wo_own, cols_half(wo_out, c, 512))
        start(copy(22, wo_own, cols_half(wo_out, c, 512), sib))

        for k in range(3):
            reg = pk_piece(pk_out, c, cj[k])
            copy(16 + k, reg, reg, sib).wait_recv()
            start(copy(19 + k, reg, reg, sib))

        slab_finish(3)
        for k in range(3):
            copy(12 + k, wi_rb.at[k], wi_rb.at[k], sib).wait_recv()
        chip_sum(wi_mine.at[1], wi_rb, 512)
        l_wi_out = local(10, wi_mine.at[1], rows_half(wi_out, c, 512))
        start(copy(23, wi_mine.at[1], rows_half(wi_out, c, 512), sib))

        reg = pk_piece(pk_out, 1 - c, j)
        copy(15, reg, reg, sib).wait_recv()
        for k in range(3):
            reg = pk_piece(pk_out, 1 - c, cj[k])
            copy(19 + k, reg, reg, sib).wait_recv()
        reg = cols_half(wo_out, 1 - c, 512)
        copy(22, reg, reg, sib).wait_recv()
        reg = rows_half(wi_out, 1 - c, 512)
        copy(23, reg, reg, sib).wait_recv()
        for cp in sends + p1[2:]:
            cp.wait_send()
        for cp in (l_pk_out, l_wo_out, l_wi_out):
            cp.wait()

    return pl.pallas_call(
        body, name="grads_reduce",
        out_shape=(jax.ShapeDtypeStruct((D, wi_w), F32), jax.ShapeDtypeStruct((512, D), F32),
                   jax.ShapeDtypeStruct(pack.shape, F32)),
        in_specs=[ANY] * (5 + n_dz), out_specs=(ANY,) * 3,
        scratch_shapes=[
            pltpu.VMEM((lt, 512), BF16), pltpu.VMEM((lt, 512), BF16), pltpu.VMEM((lt, wi_w), BF16),
            pltpu.VMEM((2, 512, wi_w), BF16), pltpu.VMEM((2, 512, wi_w), F32), pltpu.VMEM((4, 512, wi_w), BF16),
            pltpu.VMEM((3, 512, wi_w), BF16), pltpu.VMEM((3, 512, wi_w), BF16),
            pltpu.VMEM((2048, 512), F32), pltpu.VMEM((2048, 512), F32), pltpu.VMEM((2048, 512), BF16),
            pltpu.VMEM((3, 512, 512), BF16), pltpu.VMEM((512, 512), F32),
            pltpu.VMEM((hp, 512), F32), pltpu.VMEM((hp, 512), F32), pltpu.VMEM((hp, 512), BF16),
            pltpu.VMEM((3, hp, 128), BF16), pltpu.VMEM((hp, 128), F32),
            pltpu.SemaphoreType.DMA((24,)), pltpu.SemaphoreType.DMA((24,)), pltpu.SemaphoreType.DMA((14,))],
        compiler_params=_cp(vmem_mb=56),
    )(hn, *dzs, hn_c, dxa_c, g_w_out, pack)


def _ada_bwd(c_all, dmx_all_j, dmc_j, dmx_all, dmc, c_ctx, ada_w_full):
    def body(c_ref, dmxj_ref, dmcj_ref, dmx_ref, dmc_ref, cc_ref, w_ref, gw_ref, gb_ref, gc_ref, lhs, rhs, dm8):
        lhs[...] = jnp.zeros_like(lhs)
        rhs[...] = jnp.zeros_like(rhs)
        cv = c_ref[...]
        lhs[0:8, :] = cv * _sigmoid(cv)
        cc = cc_ref[...]
        a_c, da_c = _silu_and_grad(cc)
        lhs[8:9, :] = a_c
        rhs[0:8, :] = dmxj_ref[...]
        rhs[8:9, :] = dmcj_ref[...]
        gw_ref[...] = _dot_tn(lhs[...].astype(BF16), rhs[...].astype(BF16))
        gb_ref[...] = jnp.sum(dmx_ref[...], axis=0, keepdims=True) + dmc_ref[...]
        dm8[...] = jnp.zeros_like(dm8)
        dm8[0:1, :] = dmc_ref[...]
        da = _dot_nt(dm8[...].astype(BF16), w_ref[...])
        gc_ref[...] = da[0:1, :] * da_c

    return pl.pallas_call(
        body, name="ada_bwd",
        out_shape=(jax.ShapeDtypeStruct((D, 768), F32), jax.ShapeDtypeStruct((1, 3 * D), F32),
                   jax.ShapeDtypeStruct((1, D), F32)),
        in_specs=[VMEM] * 7, out_specs=(VMEM,) * 3,
        scratch_shapes=[pltpu.VMEM((16, D), F32), pltpu.VMEM((16, 768), F32), pltpu.VMEM((8, 3 * D), F32)],
        compiler_params=_cp(vmem_mb=32),
    )(c_all, dmx_all_j, dmc_j, dmx_all, dmc, c_ctx, ada_w_full)


def _proj(x, mods, mrow, norm_g, w_full, nk, name):
    lx = x.shape[0]
    n = lx // T

    def body(x_ref, sh_ref, sc_ref, ng_ref, *rest):
        w_refs, hn_ref, z_refs = rest[:nk], rest[nk], rest[nk + 1:]
        xv = x_ref[...]
        r = lax.rsqrt(jnp.mean(xv * xv, axis=-1, keepdims=True) + NORM_EPS)
        hn = (xv * r) * ng_ref[...] * (1.0 + sc_ref[mrow:mrow + 1, :]) + sh_ref[mrow:mrow + 1, :]
        hb = hn.astype(BF16)
        hn_ref[...] = hb
        for k in range(nk):
            z_refs[k][...] = _dot(hb, w_refs[k][...])

    row = pl.BlockSpec((T, D), lambda i: (i, 0))
    in_specs = [row, pl.BlockSpec((8, D), lambda i: (0, 0)), pl.BlockSpec((8, D), lambda i: (0, 1)),
                pl.BlockSpec((1, D), lambda i: (0, 0))]
    in_specs += [pl.BlockSpec((D, D), lambda i, k=k: (0, k)) for k in range(nk)]
    out_shape = (jax.ShapeDtypeStruct((lx, D), BF16),) + tuple(jax.ShapeDtypeStruct((lx, D), F32) for _ in range(nk))
    return pl.pallas_call(
        body, name=name, grid=(n,), out_shape=out_shape, in_specs=in_specs, out_specs=(row,) * (nk + 1),
        compiler_params=_cp(1, vmem_mb=56),
    )(x, mods, mods, norm_g, *([w_full] * nk))


def _halo_specs(lx):
    last = lx // 8 - 1
    return [pl.BlockSpec((T, D), lambda i: (i, 0)),
            pl.BlockSpec((8, D), lambda i: (jnp.maximum(i * (T // 8) - 1, 0), 0)),
            pl.BlockSpec((8, D), lambda i: (jnp.minimum((i + 1) * (T // 8), last), 0))]


def _zhalo_specs(lx):
    last = lx // 8 - 1
    return [pl.BlockSpec((T * NH, HD), lambda i: (i, 0)),
            pl.BlockSpec((8 * NH, HD), lambda i: (jnp.maximum(i * (T // 8) - 1, 0), 0)),
            pl.BlockSpec((8 * NH, HD), lambda i: (jnp.minimum((i + 1) * (T // 8), last), 0))]


ZT = pl.BlockSpec((T * NH, HD), lambda i: (i, 0))
CONV_CHUNK = 32


def _lru_gates_fwd(xa, conv_wz, conv_bz, wcat, bcat, lamcat, name):
    lx = xa.shape[0]
    n = lx // T

    def body(xm, xp, xn, cw, cb, w_ref, b_ref, lam_ref, xaz_o, xcz_o, af_o, uf_o, ab_o, ub_o, pad):
        i = pl.program_id(0)
        pmask = jnp.where(i == 0, 0.0, 1.0)
        nmask = jnp.where(i == n - 1, 0.0, 1.0)
        for h in range(NH):
            cols = slice(HD * h, HD * h + HD)
            pad[_zrows(h, 8), :] = xp[:, cols] * pmask
            pad[pl.ds(8 * NH + h, T, stride=NH), :] = xm[:, cols]
            pad[pl.ds((T + 8) * NH + h, 8, stride=NH), :] = xn[:, cols] * nmask
        xaz_o[...] = pad[pl.ds(8 * NH, T * NH), :]

        def conv_chunk(ci, carry):
            base = pl.multiple_of(ci * (CONV_CHUNK * NH), CONV_CHUNK * NH)
            acc = None
            for k in range(4):
                sl = pad[pl.ds(base + (7 + k) * NH, CONV_CHUNK * NH), :].reshape(CONV_CHUNK, NH, HD)
                term = sl * cw[k][None]
                acc = term if acc is None else acc + term
            acc = acc + cb[...][None]
            xcz_o[pl.ds(base, CONV_CHUNK * NH), :] = acc.reshape(CONV_CHUNK * NH, HD)
            return carry
        lax.fori_loop(0, T // CONV_CHUNK, conv_chunk, 0)

        outs = ((af_o, uf_o), (ab_o, ub_o))
        for h in range(NH):
            xch = xcz_o[_zrows(h, T), :]
            pre = _dot(xch.astype(BF16), w_ref[h]) + b_ref[h:h + 1, :]
            for d in range(2):
                _, gi, _, _, a, mult = _lru_gate(pre, lam_ref[h:h + 1, :], d)
                outs[d][0][_zrows(h, T), :] = a
                outs[d][1][_zrows(h, T), :] = mult * gi * xch

    full = lambda shape: pl.BlockSpec(shape, lambda i: (0,) * len(shape))
    in_specs = _halo_specs(lx) + [full((4, NH, HD)), full((NH, HD)), full((NH, HD, 4 * HD)), full((NH, 4 * HD)),
                                  full((NH, 2 * HD))]
    zs = jax.ShapeDtypeStruct((lx * NH, HD), F32)
    return pl.pallas_call(
        body, name=name, grid=(n,), out_shape=(zs,) * 6, in_specs=in_specs, out_specs=(ZT,) * 6,
        scratch_shapes=[pltpu.VMEM(((T + 16) * NH, HD), F32)],
        compiler_params=_cp(1, vmem_mb=48),
    )(xa, xa, xa, conv_wz, conv_bz, wcat, bcat, lamcat)


def _scan(a_up, x_up, a_dn, x_dn, s_up, s_dn, post, name):
    lx = a_up.shape[0] // NH
    n = lx // T

    def body(au, xu, ad, xd, su0, sd0, *rest):
        if post:
            ou, od, fu, fd, carry = rest
        else:
            ou, od, pu, pd, fu, fd, carry = rest
        i = pl.program_id(0)

        @pl.when(i == 0)
        def _():
            carry[0] = su0[...]
            carry[1] = sd0[...]

        def step(k, s):
            su, sd = s
            ru = pl.ds(_mo(k * NH, NH), NH)
            rd = pl.ds(_mo((T - 1 - k) * NH, NH), NH)
            if post:
                vu = xu[ru, :] + su
                vd = xd[rd, :] + sd
                ou[ru, :] = vu
                od[rd, :] = vd
                return au[ru, :] * vu, ad[rd, :] * vd
            pu[ru, :] = su
            pd[rd, :] = sd
            vu = au[ru, :] * su + xu[ru, :]
            vd = ad[rd, :] * sd + xd[rd, :]
            ou[ru, :] = vu
            od[rd, :] = vd
            return vu, vd

        su, sd = lax.fori_loop(0, T, step, (carry[0], carry[1]), unroll=8)
        carry[0] = su
        carry[1] = sd
        fu[...] = su
        fd[...] = sd

    up = pl.BlockSpec((T * NH, HD), lambda i: (i, 0))
    dn = pl.BlockSpec((T * NH, HD), lambda i: (n - 1 - i, 0))
    st = pl.BlockSpec((NH, HD), lambda i: (0, 0))
    zs = jax.ShapeDtypeStruct((lx * NH, HD), F32)
    ss = jax.ShapeDtypeStruct((NH, HD), F32)
    if post:
        out_shape, out_specs = (zs, zs, ss, ss), (up, dn, st, st)
    else:
        out_shape, out_specs = (zs, zs, zs, zs, ss, ss), (up, dn, up, dn, st, st)
    return pl.pallas_call(
        body, name=name, grid=(n,), out_shape=out_shape, in_specs=[up, up, dn, dn, st, st], out_specs=out_specs,
        scratch_shapes=[pltpu.VMEM((2, NH, HD), F32)],
        compiler_params=_cp(1, vmem_mb=48),
    )(a_up, x_up, a_dn, x_dn, s_up, s_dn)


def _sgu_parts(u, v, lng, lnb, w_ref, bt_ref, mixed_s):
    ug, dug = _gelu_and_grad(u)
    vg, dvg = _gelu_and_grad(v)
    mu = jnp.mean(vg, axis=-1, keepdims=True)
    vc = vg - mu
    rstd = lax.rsqrt(jnp.mean(vc * vc, axis=-1, keepdims=True) + LN_EPS)
    vh = vc * rstd
    vn = (vh * lng + lnb).astype(BF16)
    for g in range(NH):
        cols = slice(HD * g, HD * g + HD)
        mixed_s[:, cols] = _dot(w_ref[g], vn[:, cols]) + bt_ref[:, g:g + 1]
    return ug, dug, dvg, rstd, vh, vn


def _sgu_fwd(u, v, ln_g, ln_b, sgu_w, sgu_bt):
    lx = u.shape[0]
    n = lx // HD

    def body(u_ref, v_ref, g_ref, b_ref, w_ref, bt_ref, y_ref, mixed_s):
        ug, _, _, _, _, _ = _sgu_parts(u_ref[...], v_ref[...], g_ref[...], b_ref[...], w_ref, bt_ref, mixed_s)
        y_ref[...] = ug * mixed_s[...]

    row = pl.BlockSpec((HD, D), lambda i: (i, 0))
    vec = pl.BlockSpec((1, D), lambda i: (0, 0))
    return pl.pallas_call(
        body, name="sgu_fwd", grid=(n,), out_shape=jax.ShapeDtypeStruct((lx, D), F32),
        in_specs=[row, row, vec, vec, pl.BlockSpec((NH, HD, HD), lambda i: (0, 0, 0)),
                  pl.BlockSpec((HD, NH), lambda i: (0, 0))],
        out_specs=row, scratch_shapes=[pltpu.VMEM((HD, D), F32)],
        compiler_params=_cp(1),
    )(u, v, ln_g, ln_b, sgu_w, sgu_bt)


def _sgu_bwd(u, v, dys, ln_g, ln_b, sgu_w, sgu_bt):
    lx = u.shape[0]
    n = lx // HD

    def body(u_ref, v_ref, dy_ref, g_ref, b_ref, w_ref, bt_ref, du_ref, dv_ref, dw_ref, db_ref, dg_ref, dbl_ref,
             mixed_s, dvn_s):
        i = pl.program_id(0)

        @pl.when(i == 0)
        def _():
            dw_ref[...] = jnp.zeros_like(dw_ref)
            db_ref[...] = jnp.zeros_like(db_ref)
            dg_ref[...] = jnp.zeros_like(dg_ref)
            dbl_ref[...] = jnp.zeros_like(dbl_ref)

        lng = g_ref[...]
        ug, dug, dvg, rstd, vh, vn = _sgu_parts(u_ref[...], v_ref[...], lng, b_ref[...], w_ref, bt_ref, mixed_s)
        dys_v = dy_ref[...]
        du_ref[...] = (dys_v * mixed_s[...] * dug).astype(BF16)
        dmix = dys_v * ug
        ones = jnp.ones((8, HD), BF16)
        for g in range(NH):
            cols = slice(HD * g, HD * g + HD)
            dm = dmix[:, cols]
            hi = dm.astype(BF16)
            lo = (dm - hi.astype(F32)).astype(BF16)
            dw_ref[g] += _dot_nt(hi, vn[:, cols])
            db_ref[g:g + 1, :] += (_dot_nt(ones, hi) + _dot_nt(ones, lo))[0:1, :]
            dvn_s[:, cols] = _dot_tn(w_ref[g], hi)
        dvn = dvn_s[...]
        dg_ref[...] += jnp.sum(dvn * vh, axis=0, keepdims=True)
        dbl_ref[...] += jnp.sum(dvn, axis=0, keepdims=True)
        dvh = dvn * lng
        dvg_in = rstd * (dvh - jnp.mean(dvh, axis=-1, keepdims=True)
                         - vh * jnp.mean(dvh * vh, axis=-1, keepdims=True))
        dv_ref[...] = (dvg_in * dvg).astype(BF16)

    row = pl.BlockSpec((HD, D), lambda i: (i, 0))
    vec = pl.BlockSpec((1, D), lambda i: (0, 0))
    wsp = pl.BlockSpec((NH, HD, HD), lambda i: (0, 0, 0))
    bsp = pl.BlockSpec((NH, HD), lambda i: (0, 0))
    return pl.pallas_call(
        body, name="sgu_bwd", grid=(n,),
        out_shape=(jax.ShapeDtypeStruct((lx, D), BF16), jax.ShapeDtypeStruct((lx, D), BF16),
                   jax.ShapeDtypeStruct((NH, HD, HD), F32), jax.ShapeDtypeStruct((NH, HD), F32),
                   jax.ShapeDtypeStruct((1, D), F32), jax.ShapeDtypeStruct((1, D), F32)),
        in_specs=[row, row, row, vec, vec, wsp, pl.BlockSpec((HD, NH), lambda i: (0, 0))],
        out_specs=(row, row, wsp, bsp, vec, vec),
        scratch_shapes=[pltpu.VMEM((HD, D), F32), pltpu.VMEM((HD, D), F32)],
        compiler_params=_cp(1),
    )(u, v, dys, ln_g, ln_b, sgu_w, sgu_bt)


def _out_fwd_bwd(hf_z, hb_z, ga, gb, ys, x, tgt, mods, final_g, w_out_full):
    lx = x.shape[0]
    n = lx // T

    def body(hf_ref, hb_ref, ga_ref, gb_ref, ys_ref, x_ref, t_ref, gx_ref, fg_ref, w_ref,
             loss_ref, dfg_ref, dgx_ref, dxn_ref, y_ref, do_ref, dga_ref, dgb_ref, dyl_ref, dys_ref, yl_s):
        i = pl.program_id(0)

        @pl.when(i == 0)
        def _():
            loss_ref[...] = jnp.zeros_like(loss_ref)
            dfg_ref[...] = jnp.zeros_like(dfg_ref)
            dgx_ref[...] = jnp.zeros_like(dgx_ref)

        for h in range(NH):
            yl_s[:, HD * h:HD * h + HD] = hf_ref[_zrows(h, T), :] + hb_ref[_zrows(h, T), :]
        yl = yl_s[...]
        gav = ga_ref[...]
        gbv = gb_ref[...]
        sa, dsa = _silu_and_grad(gav)
        sb, dsb = _silu_and_grad(gbv)
        ysv = ys_ref[...]
        y_ref[:, 0:D] = (yl * sa).astype(BF16)
        y_ref[:, D:2 * D] = (ysv * sb).astype(BF16)
        o = _dot(y_ref[...], w_ref[...])
        gx = gx_ref[0:1, :]
        xnew = x_ref[...] + gx * o
        r2 = lax.rsqrt(jnp.mean(xnew * xnew, axis=-1, keepdims=True) + NORM_EPS)
        xh = xnew * r2
        fg = fg_ref[...]
        err = xh * fg - t_ref[...]
        loss_ref[...] += 0.5 * jnp.sum(jnp.mean(err * err, axis=-1, keepdims=True), axis=0, keepdims=True)
        dout = err * (1.0 / D)
        dfg_ref[...] += jnp.sum(dout * xh, axis=0, keepdims=True)
        dxh = dout * fg
        dxn = r2 * (dxh - xh * jnp.mean(dxh * xh, axis=-1, keepdims=True))
        dxn_ref[...] = dxn
        dgx_ref[...] += jnp.sum(dxn * o, axis=0, keepdims=True)
        do = (dxn * gx).astype(BF16)
        do_ref[...] = do
        dy = _dot_nt(do, w_ref[...])
        dy1 = dy[:, 0:D]
        dy2 = dy[:, D:2 * D]
        dga_ref[...] = (dy1 * yl * dsa).astype(BF16)
        dgb_ref[...] = (dy2 * ysv * dsb).astype(BF16)
        dys_ref[...] = dy2 * sb
        yl_s[...] = dy1 * sa
        for h in range(NH):
            dyl_ref[_zrows(h, T), :] = yl_s[:, HD * h:HD * h + HD]

    row = pl.BlockSpec((T, D), lambda i: (i, 0))
    vec = pl.BlockSpec((1, D), lambda i: (0, 0))
    in_specs = [ZT, ZT, row, row, row, row, row, pl.BlockSpec((8, D), lambda i: (0, 2)), vec,
                pl.BlockSpec((2 * D, D), lambda i: (0, 0))]
    out_shape = (jax.ShapeDtypeStruct((1, 1), F32), jax.ShapeDtypeStruct((1, D), F32), jax.ShapeDtypeStruct((1, D), F32),
                 jax.ShapeDtypeStruct((lx, D), F32), jax.ShapeDtypeStruct((lx, 2 * D), BF16),
                 jax.ShapeDtypeStruct((lx, D), BF16), jax.ShapeDtypeStruct((lx, D), BF16),
                 jax.ShapeDtypeStruct((lx, D), BF16), jax.ShapeDtypeStruct((lx * NH, HD), F32),
                 jax.ShapeDtypeStruct((lx, D), F32))
    out_specs = (pl.BlockSpec((1, 1), lambda i: (0, 0)), vec, vec, row, pl.BlockSpec((T, 2 * D), lambda i: (i, 0)),
                 row, row, row, ZT, row)
    return pl.pallas_call(
        body, name="out_fwd_bwd", grid=(n,), out_shape=out_shape, in_specs=in_specs, out_specs=out_specs,
        scratch_shapes=[pltpu.VMEM((T, D), F32)],
        compiler_params=_cp(1, vmem_mb=56),
    )(hf_z, hb_z, ga, gb, ys, x, tgt, mods, final_g, w_out_full)


def _lru_gates_bwd(xc_z, lf_z, lb_z, pf_z, pb_z, wcat, bcat, lamcat, dw0, db0, dl0, name):
    lx = xc_z.shape[0] // NH
    n = lx // T

    def body(xc_ref, lf_ref, lb_ref, pf_ref, pb_ref, w_ref, b_ref, lam_ref, dw0_ref, db0_ref, dl0_ref,
             dxc_ref, dw_ref, db_ref, dl_ref, dpre_s):
        i = pl.program_id(0)

        @pl.when(i == 0)
        def _():
            dw_ref[...] = dw0_ref[...]
            db_ref[...] = db0_ref[...]
            dl_ref[...] = dl0_ref[...]

        lam_refs = (lf_ref, lb_ref)
        prev_refs = (pf_ref, pb_ref)
        for h in range(NH):
            xch = xc_ref[_zrows(h, T), :]
            xcb = xch.astype(BF16)
            pre = _dot(xcb, w_ref[h]) + b_ref[h:h + 1, :]
            dxc = jnp.zeros((T, HD), F32)
            for d in range(2):
                r, gi, lam, sp, a, mult = _lru_gate(pre, lam_ref[h:h + 1, :], d)
                du = lam_refs[d][_zrows(h, T), :]
                da = du * prev_refs[d][_zrows(h, T), :]
                dgi = du * mult * xch
                dxc = dxc + du * mult * gi
                dmult = du * gi * xch
                dla = da * a - dmult * (a * a) / mult
                dr = dla * ((-LRU_C) * sp)
                dsp = jnp.sum(dla * ((-LRU_C) * r), axis=0, keepdims=True)
                dl_ref[h:h + 1, HD * d:HD * d + HD] += dsp * (-_sigmoid(-lam))
                dpre_s[:, 256 * d:256 * d + HD] = dr * r * (1.0 - r)
                dpre_s[:, 256 * d + HD:256 * d + 2 * HD] = dgi * gi * (1.0 - gi)
            dpre = dpre_s[...]
            dpb = dpre.astype(BF16)
            dw_ref[h] += _dot_tn(xcb, dpb)
            db_ref[h:h + 1, :] += jnp.sum(dpre, axis=0, keepdims=True)
            dxc_ref[_zrows(h, T), :] = dxc + _dot_nt(dpb, w_ref[h])

    full = lambda shape: pl.BlockSpec(shape, lambda i: (0,) * len(shape))
    wsp, bsp, lsp = full((NH, HD, 4 * HD)), full((NH, 4 * HD)), full((NH, 2 * HD))
    return pl.pallas_call(
        body, name=name, grid=(n,),
        out_shape=(jax.ShapeDtypeStruct((lx * NH, HD), F32), jax.ShapeDtypeStruct((NH, HD, 4 * HD), F32),
                   jax.ShapeDtypeStruct((NH, 4 * HD), F32), jax.ShapeDtypeStruct((NH, 2 * HD), F32)),
        in_specs=[ZT] * 5 + [wsp, bsp, lsp, wsp, bsp, lsp], out_specs=(ZT, wsp, bsp, lsp),
        scratch_shapes=[pltpu.VMEM((T, 4 * HD), F32)],
        compiler_params=_cp(1, vmem_mb=48),
    )(xc_z, lf_z, lb_z, pf_z, pb_z, wcat, bcat, lamcat, dw0, db0, dl0)


def _conv_bwd(dxc_z, xa_z, conv_wz, dcw0, dcb0, name):
    lx = dxc_z.shape[0] // NH
    n = lx // T

    def body(dm, dp, dn, xa_ref, cw, dcw0_ref, dcb0_ref, dxa_ref, dcw_ref, dcb_ref, pad, dxa_s):
        i = pl.program_id(0)

        @pl.when(i == 0)
        def _():
            dcw_ref[...] = dcw0_ref[...]
            dcb_ref[...] = dcb0_ref[...]

        pmask = jnp.where(i == 0, 0.0, 1.0)
        nmask = jnp.where(i == n - 1, 0.0, 1.0)
        pad[pl.ds(0, 8 * NH), :] = dp[...] * pmask
        pad[pl.ds(8 * NH, T * NH), :] = dm[...]
        pad[pl.ds((T + 8) * NH, 8 * NH), :] = dn[...] * nmask

        def chunk(ci, carry):
            base = pl.multiple_of(ci * (CONV_CHUNK * NH), CONV_CHUNK * NH)
            xav = xa_ref[pl.ds(base, CONV_CHUNK * NH), :].reshape(CONV_CHUNK, NH, HD)
            acc = None
            for k in range(4):
                sl = pad[pl.ds(base + (9 - k) * NH, CONV_CHUNK * NH), :].reshape(CONV_CHUNK, NH, HD)
                term = sl * cw[k][None]
                acc = term if acc is None else acc + term
                dcw_ref[k] += jnp.sum(sl * xav, axis=0)
                if k == 1:
                    dcb_ref[...] += jnp.sum(sl, axis=0)
            dxa_s[pl.ds(base, CONV_CHUNK * NH), :] = acc.reshape(CONV_CHUNK * NH, HD)
            return carry
        lax.fori_loop(0, T // CONV_CHUNK, chunk, 0)
        for h in range(NH):
            dxa_ref[:, HD * h:HD * h + HD] = dxa_s[_zrows(h, T), :].astype(BF16)

    full = lambda shape: pl.BlockSpec(shape, lambda i: (0,) * len(shape))
    return pl.pallas_call(
        body, name=name, grid=(n,),
        out_shape=(jax.ShapeDtypeStruct((lx, D), BF16), jax.ShapeDtypeStruct((4, NH, HD), F32),
                   jax.ShapeDtypeStruct((NH, HD), F32)),
        in_specs=_zhalo_specs(lx) + [ZT, full((4, NH, HD)), full((4, NH, HD)), full((NH, HD))],
        out_specs=(pl.BlockSpec((T, D), lambda i: (i, 0)), full((4, NH, HD)), full((NH, HD))),
        scratch_shapes=[pltpu.VMEM(((T + 16) * NH, HD), F32), pltpu.VMEM((T * NH, HD), F32)],
        compiler_params=_cp(1, vmem_mb=48),
    )(dxc_z, dxc_z, dxc_z, xa_z, conv_wz, dcw0, dcb0)


def _proj_bwd(dzs, x, dxn, mods, mrow, norm_g, w_full, dng0, name):
    lx = x.shape[0]
    n = lx // T
    nk = len(dzs)
    has_x = dxn is not None

    def body(*refs):
        dz_refs = refs[:nk]
        w_refs = refs[nk:2 * nk]
        x_ref, sc_ref, ng_ref, dng0_ref = refs[2 * nk:2 * nk + 4]
        rest = refs[2 * nk + 4:]
        if has_x:
            dxn_ref, gx_ref, dng_ref, dsc_ref, dsh_ref = rest
        else:
            dng_ref, dsc_ref, dsh_ref = rest
        i = pl.program_id(0)

        @pl.when(i == 0)
        def _():
            dng_ref[...] = dng0_ref[...]
            dsc_ref[...] = jnp.zeros_like(dsc_ref)
            dsh_ref[...] = jnp.zeros_like(dsh_ref)

        dhn = _dot_nt(dz_refs[0][...], w_refs[0][...])
        for k in range(1, nk):
            dhn = dhn + _dot_nt(dz_refs[k][...], w_refs[k][...])
        xv = x_ref[...]
        r = lax.rsqrt(jnp.mean(xv * xv, axis=-1, keepdims=True) + NORM_EPS)
        xn = xv * r
        ng = ng_ref[...]
        sc1 = 1.0 + sc_ref[mrow:mrow + 1, :]
        t = dhn * xn
        dng_ref[...] += jnp.sum(t * sc1, axis=0, keepdims=True)
        dsc_ref[...] += jnp.sum(t * ng, axis=0, keepdims=True)
        dsh_ref[...] += jnp.sum(dhn, axis=0, keepdims=True)
        if has_x:
            dxh = dhn * (ng * sc1)
            gx_ref[...] = dxn_ref[...] + r * (dxh - xn * jnp.mean(dxh * xn, axis=-1, keepdims=True))

    row = pl.BlockSpec((T, D), lambda i: (i, 0))
    vec = pl.BlockSpec((1, D), lambda i: (0, 0))
    in_specs = [row] * nk + [pl.BlockSpec((D, D), lambda i, k=k: (0, k)) for k in range(nk)]
    in_specs += [row, pl.BlockSpec((8, D), lambda i: (0, 1)), vec, vec]
    args = list(dzs) + [w_full] * nk + [x, mods, norm_g, dng0]
    vs = jax.ShapeDtypeStruct((1, D), F32)
    out_shape, out_specs = (vs, vs, vs), (vec, vec, vec)
    if has_x:
        in_specs.append(row)
        args.append(dxn)
        out_shape = (jax.ShapeDtypeStruct((lx, D), F32),) + out_shape
        out_specs = (row,) + out_specs
    return pl.pallas_call(
        body, name=name, grid=(n,), out_shape=out_shape, in_specs=in_specs, out_specs=out_specs,
        compiler_params=_cp(1, vmem_mb=56),
    )(*args)


def _tn_matmul(a, bs, extra, name):
    lx, m = a.shape
    tm = min(lx, 1024)
    n = lx // tm
    widths = [b.shape[1] for b in bs]
    nb = len(bs)

    def body(a_ref, *rest):
        b_refs = rest[:nb]
        rest = rest[nb:]
        if extra is not None:
            ea_ref, eb_ref = rest[:2]
            rest = rest[2:]
        out_ref, acc = rest
        i = pl.program_id(0)
        av = a_ref[...]
        off = 0
        for k in range(nb):
            cols = slice(off, off + widths[k])
            part = _dot_tn(av, b_refs[k][...])

            @pl.when(i == 0)
            def _():
                acc[:, cols] = part

            @pl.when(i > 0)
            def _():
                acc[:, cols] += part
            off += widths[k]

        @pl.when(i == n - 1)
        def _():
            if extra is not None:
                acc[:, 0:widths[0]] += _dot_tn(ea_ref[...], eb_ref[...])
            pltpu.sync_copy(acc, out_ref)

    in_specs = [pl.BlockSpec((tm, m), lambda i: (i, 0))] + [pl.BlockSpec((tm, w), lambda i: (i, 0)) for w in widths]
    args = [a] + list(bs)
    if extra is not None:
        in_specs += [VMEM, VMEM]
        args += list(extra)
    return pl.pallas_call(
        body, name=name, grid=(n,), out_shape=jax.ShapeDtypeStruct((m, sum(widths)), F32),
        in_specs=in_specs, out_specs=ANY, scratch_shapes=[pltpu.VMEM((m, sum(widths)), F32)],
        compiler_params=_cp(1, vmem_mb=56),
    )(*args)


def _adam_math(w, g, m, v):
    m = ADAM_B1 * m + (1.0 - ADAM_B1) * g
    v = ADAM_B2 * v + (1.0 - ADAM_B2) * (g * g)
    m_hat = m / (1.0 - ADAM_B1 ** ADAM_STEP)
    v_hat = v / (1.0 - ADAM_B2 ** ADAM_STEP)
    delta = -ADAM_LR * (m_hat / (jnp.sqrt(v_hat) + ADAM_EPS) + ADAM_WD * w)
    return delta, m, v


def _adam_big(w, g, m, v, name):
    rows, cols = w.shape
    tr = 256

    def body(w_ref, g_ref, m_ref, v_ref, d_o, m_o, v_o):
        d, mm, vv = _adam_math(w_ref[...], g_ref[...], m_ref[...], v_ref[...])
        d_o[...] = d
        m_o[...] = mm
        v_o[...] = vv

    blk = pl.BlockSpec((tr, cols), lambda i: (i, 0))
    s = jax.ShapeDtypeStruct((rows, cols), F32)
    return pl.pallas_call(
        body, name=name, grid=(rows // tr,), out_shape=(s, s, s), in_specs=[blk] * 4, out_specs=(blk,) * 3,
        compiler_params=_cp(1, vmem_mb=48),
    )(w, g, m, v)


def _adam_small(items):
    ni = len(items)

    def body(*refs):
        ins, outs = refs[:4 * ni], refs[4 * ni:7 * ni]
        bufs_in, bufs_out = refs[7 * ni:11 * ni], refs[11 * ni:14 * ni]
        sem_in, sem_out = refs[14 * ni], refs[14 * ni + 1]
        loads = [pltpu.make_async_copy(ins[q], bufs_in[q], sem_in.at[q]) for q in range(4 * ni)]
        for cp in loads:
            cp.start()
        stores = []
        for k in range(ni):
            for q in range(4):
                loads[4 * k + q].wait()
            w_b, g_b, m_b, v_b = bufs_in[4 * k:4 * k + 4]
            res = _adam_math(w_b[...], g_b[...], m_b[...], v_b[...])
            for q in range(3):
                bufs_out[3 * k + q][...] = res[q]
                cp = pltpu.make_async_copy(bufs_out[3 * k + q], outs[3 * k + q], sem_out.at[3 * k + q])
                cp.start()
                stores.append(cp)
        for cp in stores:
            cp.wait()

    flat = [a for it in items for a in it]
    out_shape = tuple(jax.ShapeDtypeStruct(it[0].shape, F32) for it in items for _ in range(3))
    scratch = [pltpu.VMEM(a.shape, F32) for a in flat] + [pltpu.VMEM(s.shape, F32) for s in out_shape]
    scratch += [pltpu.SemaphoreType.DMA((4 * ni,)), pltpu.SemaphoreType.DMA((3 * ni,))]
    res = pl.pallas_call(
        body, name="adam_small", out_shape=out_shape, in_specs=[HBM] * (4 * ni), out_specs=(HBM,) * (3 * ni),
        scratch_shapes=scratch, compiler_params=_cp(vmem_mb=40),
    )(*flat)
    return [tuple(res[3 * k:3 * k + 3]) for k in range(ni)]


def kernel(x, c, ctx, c_ctx, ada_w, ada_b, norm_g, w_in, conv_w, conv_b, lru_wa, lru_ba, lru_wx, lru_bx, lru_lambda, sgu_ln_g, sgu_ln_b, sgu_w, sgu_b, w_out, final_g, loss_target, m_c_ctx, m_ada_w, m_ada_b, m_norm_g, m_w_in, m_conv_w, m_conv_b, m_lru_wa, m_lru_ba, m_lru_wx, m_lru_bx, m_lru_lambda, m_sgu_ln_g, m_sgu_ln_b, m_sgu_w, m_sgu_b, m_w_out, m_final_g, v_c_ctx, v_ada_w, v_ada_b, v_norm_g, v_w_in, v_conv_w, v_conv_b, v_lru_wa, v_lru_ba, v_lru_wx, v_lru_bx, v_lru_lambda, v_sgu_ln_g, v_sgu_ln_b, v_sgu_w, v_sgu_b, v_w_out, v_final_g):
    ix, iy, ic = lax.axis_index("x"), lax.axis_index("y"), lax.axis_index("c")
    chip = 2 * ix + iy
    dev = 2 * chip + ic
    lx = x.shape[1]
    lc = ctx.shape[1]

    smalls = jnp.concatenate([conv_w[0], lru_lambda[0], jnp.zeros((10, 256), F32)], axis=0)
    c_ctx2 = c_ctx.reshape(1, D)
    ada_b_j = lax.dynamic_slice(ada_b, (0, 768 * chip), (1, 768))
    mods, c_slots, sm_all, w_in_full, wo_land, ada_land = _gather_in(c, c_ctx2, ada_w[0], ada_b_j, w_in[0], w_out[0],
                                                                     smalls)
    wo_ss, wo_rs, ada_ss, ada_rs, wo_land, ada_land, token = _late_gather_start(wo_land, ada_land)
    mods = mods + token[0:1, 0:1]
    sm3 = sm_all.reshape(NCHIP, 16, 256)
    conv_w_full = sm3[:, 0:4, :].transpose(1, 0, 2).reshape(4, D)
    lam_full = sm3[:, 4:6, :].transpose(1, 0, 2).reshape(2, D)
    conv_wz = conv_w_full.reshape(4, NH, HD)
    conv_bz = conv_b.reshape(NH, HD)
    lamcat = lam_full.reshape(2, NH, HD).transpose(1, 0, 2).reshape(NH, 2 * HD)
    wa, wx, ba, bx = lru_wa[0], lru_wx[0], lru_ba[0], lru_bx[0]
    wcat = jnp.concatenate([wa[0], wx[0], wa[1], wx[1]], axis=-1).astype(BF16)
    bcat = jnp.concatenate([ba[0], bx[0], ba[1], bx[1]], axis=-1)
    sgu_wb = sgu_w[0].astype(BF16)
    sgu_bt = sgu_b[0].T
    final_g2 = final_g.reshape(1, D)

    zero_s = jnp.zeros((NH, HD), F32)
    hn_c, xa_c = _proj(ctx[0], mods, 1, norm_g, w_in_full, 1, "proj_ctx")
    xaz_c, xcz_c, af_c, uf_c, ab_c, ub_c = _lru_gates_fwd(xa_c, conv_wz, conv_bz, wcat, bcat, lamcat, "lru_gates_ctx")
    _, _, pf_c, pb_c, hf0, hb0 = _scan(af_c, uf_c, ab_c, ub_c, zero_s, zero_s, False, "scan_ctx")

    hn, xa, ga, u, v, gb = _proj(x[0], mods, 0, norm_g, w_in_full, 5, "proj")
    xaz, xcz, af, uf, ab, ub = _lru_gates_fwd(xa, conv_wz, conv_bz, wcat, bcat, lamcat, "lru_gates")
    hf, hb, pf, pb, _, _ = _scan(af, uf, ab, ub, hf0, hb0, False, "scan")
    ys = _sgu_fwd(u, v, sgu_ln_g, sgu_ln_b, sgu_wb, sgu_bt)

    w_out_full = _late_gather_wait(wo_land, wo_ss, wo_rs, "w_out", ys, "late_gather_wait_w_out")
    (loss_part, dfg, dgx, dxn, y, do, dga, dgb, dyl_z, dys) = _out_fwd_bwd(
        hf, hb, ga, gb, ys, x[0], loss_target[0], mods, final_g2, w_out_full)
    g_w_out_part = _tn_matmul(y, [do], None, "grad_w_out")

    du, dv, d_sgu_w, d_sgu_b, d_ln_g, d_ln_b = _sgu_bwd(u, v, dys, sgu_ln_g, sgu_ln_b, sgu_wb, sgu_bt)
    lb, lf, dh0b, dh0f = _scan(ab, dyl_z, af, dyl_z, zero_s, zero_s, True, "scan_adj")
    zw = jnp.zeros((NH, HD, 4 * HD), F32)
    zb = jnp.zeros((NH, 4 * HD), F32)
    zl = jnp.zeros((NH, 2 * HD), F32)
    dxc_z, dwc, dbc, dlc = _lru_gates_bwd(xcz, lf, lb, pf, pb, wcat, bcat, lamcat, zw, zb, zl, "lru_gates_bwd")
    dxa, dcw, dcb = _conv_bwd(dxc_z, xaz, conv_wz, jnp.zeros((4, NH, HD), F32), zero_s, "conv_bwd")

    zc = jnp.zeros((lc * NH, HD), F32)
    dhf_c = lax.dynamic_update_slice(zc, dh0f, ((lc - 1) * NH, 0))
    dhb_c = lax.dynamic_update_slice(zc, dh0b, (0, 0))
    lb_c, lf_c, _, _ = _scan(ab_c, dhb_c, af_c, dhf_c, zero_s, zero_s, True, "scan_adj_ctx")
    dxc_zc, dwc, dbc, dlc = _lru_gates_bwd(xcz_c, lf_c, lb_c, pf_c, pb_c, wcat, bcat, lamcat, dwc, dbc, dlc,
                                           "lru_gates_bwd_ctx")
    dxa_c, dcw, dcb = _conv_bwd(dxc_zc, xaz_c, conv_wz, dcw, dcb, "conv_bwd_ctx")

    dzs = [dxa, dga, du, dv, dgb]
    grad_x, dng, dsc_x, dsh_x = _proj_bwd(dzs, x[0], dxn, mods, 0, norm_g, w_in_full, jnp.zeros((1, D), F32), "proj_bwd")
    dng, dsc_c, dsh_c = _proj_bwd([dxa_c], ctx[0], None, mods, 1, norm_g, w_in_full, dng, "proj_bwd_ctx")

    dmx = jnp.concatenate([dsh_x, dsc_x, dgx], axis=0)
    dmc = jnp.concatenate([dsh_c, dsc_c, jnp.zeros((1, D), F32)], axis=0)
    slot = jnp.concatenate([dmx, jnp.zeros((1, D), F32)], axis=0)
    slots = lax.dynamic_update_slice(jnp.zeros((32, D), F32), slot, (4 * dev, 0))
    vecs = jnp.concatenate([dfg, dng, dcb.reshape(1, D), d_ln_g, d_ln_b, dcw.reshape(4, D), dmc,
                            jnp.zeros((4, D), F32), slots], axis=0)
    d_sgu_w4 = d_sgu_w.reshape(4, 256, HD).transpose(1, 0, 2).reshape(256, 4 * HD)
    pad8 = lambda a: jnp.pad(a, ((0, 8 - a.shape[0]), (0, 4 * HD - a.shape[1])))
    pack = jnp.concatenate([dwc.reshape(NH * HD, 4 * HD), pad8(dbc), pad8(dlc), d_sgu_w4, pad8(d_sgu_b),
                            vecs.reshape(96, 4 * HD), jnp.zeros((8, 4 * HD), F32)], axis=0)
    g_w_in, g_w_out, tot = _grads_reduce(hn, dzs, hn_c, dxa_c, g_w_out_part, pack)

    g_wc = tot[0:1024].reshape(NH, HD, 4 * HD)
    g_bc = tot[1024:1032]
    g_lc = tot[1032:1040, 0:2 * HD]
    g_sgu_w = tot[1040:1296].reshape(256, 4, HD).transpose(1, 0, 2).reshape(NH, HD, HD)
    g_sgu_b = tot[1296:1304, 0:HD]
    tv = tot[1304:1400].reshape(48, D)
    g_final_g, g_norm_g, g_conv_b, g_ln_g, g_ln_b = tv[0:1], tv[1:2], tv[2:3], tv[3:4], tv[4:5]
    g_conv_w_full = tv[5:9]
    dmc_tot = tv[9:12].reshape(1, 3 * D)
    slots_all = tv[16:48].reshape(8, 4, D)
    dmx_all = slots_all[:, 0:3, :].reshape(8, 3 * D)
    c_all = c_slots.reshape(8, 8, D)[:, 0, :]
    g_lru_wa = jnp.stack([g_wc[:, :, 0:HD], g_wc[:, :, 2 * HD:3 * HD]])
    g_lru_wx = jnp.stack([g_wc[:, :, HD:2 * HD], g_wc[:, :, 3 * HD:4 * HD]])
    g_lru_ba = jnp.stack([g_bc[:, 0:HD], g_bc[:, 2 * HD:3 * HD]])
    g_lru_bx = jnp.stack([g_bc[:, HD:2 * HD], g_bc[:, 3 * HD:4 * HD]])
    g_lam_full = jnp.stack([g_lc[:, 0:HD], g_lc[:, HD:2 * HD]]).reshape(2, D)
    g_conv_w = lax.dynamic_slice(g_conv_w_full, (0, 256 * chip), (4, 256))
    g_lam = lax.dynamic_slice(g_lam_full, (0, 256 * chip), (2, 256))
    dmx_all_j = lax.dynamic_slice(dmx_all, (0, 768 * chip), (8, 768))
    dmc_j = lax.dynamic_slice(dmc_tot, (0, 768 * chip), (1, 768))
    ada_full = _late_gather_wait(ada_land, ada_ss, ada_rs, "ada_w", tot, "late_gather_wait_ada_w")
    g_ada_w, g_ada_b, g_c_ctx = _ada_bwd(c_all, dmx_all_j, dmc_j, dmx_all, dmc_tot, c_ctx2, ada_full)

    big = {
        "ada_w": _adam_big(ada_w[0], g_ada_w, m_ada_w[0], v_ada_w[0], "adam_ada_w"),
        "w_in": _adam_big(w_in[0], g_w_in, m_w_in[0], v_w_in[0], "adam_w_in"),
        "w_out": _adam_big(w_out[0], g_w_out, m_w_out[0], v_w_out[0], "adam_w_out"),
    }
    small_in = {
        "c_ctx": (c_ctx, g_c_ctx, m_c_ctx, v_c_ctx, (1, D)),
        "ada_b": (ada_b, g_ada_b, m_ada_b, v_ada_b, (1, 3 * D)),
        "norm_g": (norm_g, g_norm_g, m_norm_g, v_norm_g, (1, D)),
        "conv_w": (conv_w, g_conv_w, m_conv_w, v_conv_w, (4, 256)),
        "conv_b": (conv_b, g_conv_b, m_conv_b, v_conv_b, (1, D)),
        "lru_wa": (lru_wa, g_lru_wa, m_lru_wa, v_lru_wa, (2 * NH * HD, HD)),
        "lru_ba": (lru_ba, g_lru_ba, m_lru_ba, v_lru_ba, (2 * NH, HD)),
        "lru_wx": (lru_wx, g_lru_wx, m_lru_wx, v_lru_wx, (2 * NH * HD, HD)),
        "lru_bx": (lru_bx, g_lru_bx, m_lru_bx, v_lru_bx, (2 * NH, HD)),
        "lru_lambda": (lru_lambda, g_lam, m_lru_lambda, v_lru_lambda, (2, 256)),
        "sgu_ln_g": (sgu_ln_g, g_ln_g, m_sgu_ln_g, v_sgu_ln_g, (1, D)),
        "sgu_ln_b": (sgu_ln_b, g_ln_b, m_sgu_ln_b, v_sgu_ln_b, (1, D)),
        "sgu_w": (sgu_w, g_sgu_w, m_sgu_w, v_sgu_w, (NH * HD, HD)),
        "sgu_b": (sgu_b, g_sgu_b, m_sgu_b, v_sgu_b, (NH, HD)),
        "final_g": (final_g, g_final_g, m_final_g, v_final_g, (1, D)),
    }
    names_small = list(small_in)
    res_small = _adam_small([tuple(a.reshape(small_in[k][4]) for a in small_in[k][:4]) for k in names_small])
    full_shapes = {"ada_w": ada_w.shape, "w_in": w_in.shape, "w_out": w_out.shape}
    grads, deltas, new_m, new_v = {}, {}, {}, {}
    for k in ("ada_w", "w_in", "w_out"):
        g = {"ada_w": g_ada_w, "w_in": g_w_in, "w_out": g_w_out}[k]
        grads[k] = g.reshape(full_shapes[k])
        deltas[k], new_m[k], new_v[k] = (a.reshape(full_shapes[k]) for a in big[k])
    for k, res in zip(names_small, res_small):
        shape = small_in[k][0].shape
        grads[k] = small_in[k][1].reshape(shape)
        deltas[k], new_m[k], new_v[k] = (a.reshape(shape) for a in res)

    loss = lax.psum(loss_part[0, 0], ("x", "y", "c"))
    order = ["c_ctx", "ada_w", "ada_b", "norm_g", "w_in", "conv_w", "conv_b", "lru_wa", "lru_ba", "lru_wx", "lru_bx",
             "lru_lambda", "sgu_ln_g", "sgu_ln_b", "sgu_w", "sgu_b", "w_out", "final_g"]
    return (loss, grad_x.reshape(x.shape), *[grads[k] for k in order], *[deltas[k] for k in order],
            *[new_m[k] for k in order], *[new_v[k] for k in order])
```

```python
import functools

import jax
import jax.numpy as jnp
from jax import lax
from jax.experimental import pallas as pl
from jax.experimental.pallas import tpu as pltpu

F32 = jnp.float32
BF16 = jnp.bfloat16

D = 1024
NH = 8
HD = 128
NCHIP = 4
T = 256
NORM_EPS = 1e-6
LN_EPS = 1e-5
LRU_C = 8.0
ADAM_LR = 0.001
ADAM_B1 = 0.9
ADAM_B2 = 0.999
ADAM_EPS = 1e-08
ADAM_WD = 0.01
ADAM_STEP = 10

VMEM = pl.BlockSpec(memory_space=pltpu.VMEM)
ANY = pl.BlockSpec(memory_space=pl.ANY)
MESH = pl.DeviceIdType.MESH


def _cp(n_grid=0, vmem_mb=None):
    kw = {}
    if n_grid:
        kw["dimension_semantics"] = ("arbitrary",) * n_grid
    if vmem_mb:
        kw["vmem_limit_bytes"] = vmem_mb << 20
    return pltpu.CompilerParams(**kw)


def _sigmoid(x):
    return 1.0 / (1.0 + jnp.exp(-x))


def _silu_and_grad(x):
    s = _sigmoid(x)
    return x * s, s * (1.0 + x * (1.0 - s))


_GELU_K = 0.7978845608028654
_GELU_C = 0.044715


def _gelu_and_grad(x):
    x2 = x * x
    th = jnp.tanh(_GELU_K * (x + _GELU_C * x * x2))
    g = 0.5 * x * (1.0 + th)
    dg = 0.5 * (1.0 + th) + 0.5 * x * (1.0 - th * th) * (_GELU_K * (1.0 + 3.0 * _GELU_C * x2))
    return g, dg


def _softplus(x):
    return jnp.maximum(x, 0.0) + jnp.log1p(jnp.exp(-jnp.abs(x)))


def _lru_gate(pre, lam_row, d):
    r = _sigmoid(pre[:, 256 * d:256 * d + HD])
    gi = _sigmoid(pre[:, 256 * d + HD:256 * d + 2 * HD])
    lam = lam_row[:, HD * d:HD * d + HD]
    sp = _softplus(-lam)
    la = (-LRU_C) * r * sp
    a = jnp.exp(la)
    x2 = 2.0 * la
    m2 = jnp.where(x2 > -1e-3, -x2 * (1.0 + 0.5 * x2), 1.0 - a * a)
    mult = jnp.sqrt(m2)
    return r, gi, lam, sp, a, mult


def _dot(a, b):
    return jnp.dot(a, b, preferred_element_type=F32)


def _dot_tn(a, b):
    return lax.dot_general(a, b, (((0,), (0,)), ((), ())), preferred_element_type=F32)


def _dot_nt(a, b):
    return lax.dot_general(a, b, (((1,), (1,)), ((), ())), preferred_element_type=F32)


def _mo(v, m):
    return v if isinstance(v, int) else pl.multiple_of(v, m)


def _zrows(h, n):
    return pl.ds(h, n, stride=NH)


def _gather_in(c, c_ctx, ada_w, ada_b_j, w_in, w_out, smalls):
    specs = [
        ((64, 256), F32, lambda r, jj, cc: r.at[pl.ds(_mo(16 * jj + 8 * cc, 8), 8), :]),
        ((D, 5120), BF16, lambda r, jj, cc: r.at[pl.ds(_mo(512 * cc, 16), 512), pl.ds(_mo(1280 * jj, 128), 1280)]),
    ]
    halves = [lambda r, cc: r.at[pl.ds(_mo(8 * cc, 8), 8), :],
              lambda r, cc: r.at[pl.ds(_mo(512 * cc, 16), 512), :]]
    na = len(specs)
    n_sem = 6 * na + 10

    def body(c_ref, cc_ref, ada_ref, adab_ref, win_ref, wout_ref, sm_ref,
             mods_o, call_o, sm_o, win_o, wol_o, adal_o, s_win, s_ada, s_wout, cslot, lhs, mbuf,
             send_sems, recv_sems, local_sems):
        x, y, c = lax.axis_index("x"), lax.axis_index("y"), lax.axis_index("c")
        j = 2 * x + y
        dev = 2 * j + c
        sib = (x, y, 1 - c)
        chips = [(1 - x, y), (x, 1 - y), (1 - x, 1 - y)]
        cj = [2 * cx + cy for cx, cy in chips]
        outs = [sm_o, win_o]
        srcs = [sm_ref, s_win]

        def copy(idx, src, dst, to):
            return pltpu.make_async_remote_copy(src_ref=src, dst_ref=dst, send_sem=send_sems.at[idx],
                                                recv_sem=recv_sems.at[idx], device_id=to, device_id_type=MESH)

        sends = []

        def start(cp):
            cp.start()
            sends.append(cp)

        cslot[...] = jnp.zeros_like(cslot)
        cslot[0:1, :] = c_ref[...]
        my_slot = pl.ds(_mo(8 * dev, 8), 8)
        others = [sib] + [(*chips[k], c) for k in range(3)] + [(*chips[k], 1 - c) for k in range(3)]
        other_dev = [dev + 1 - 2 * c] + [2 * cj[k] + c for k in range(3)] + [2 * cj[k] + 1 - c for k in range(3)]
        base = 6 * na
        for r in range(7):
            start(copy(base + r, cslot, call_o.at[my_slot, :], others[r]))
        call_o[my_slot, :] = cslot[...]

        s_win[...] = win_ref[...].astype(BF16)
        local = []
        for a in range(na):
            for cc in range(2):
                lc = pltpu.make_async_copy(halves[a](srcs[a], cc), specs[a][2](outs[a], j, cc), local_sems.at[2 * a + cc])
                lc.start()
                local.append(lc)
        for k in range(2):
            for a in range(na):
                start(copy(6 * a + k, halves[a](srcs[a], c), specs[a][2](outs[a], j, c), (*chips[k], c)))
        s_wout[...] = wout_ref[...].astype(BF16)
        s_ada[...] = ada_ref[...].astype(BF16)
        for q, (src, dst) in enumerate([(s_wout, wol_o.at[pl.ds(_mo(512 * j, 16), 512), :]),
                                        (s_ada, adal_o.at[:, pl.ds(_mo(768 * j, 128), 768)])]):
            lc = pltpu.make_async_copy(src, dst, local_sems.at[2 * na + q])
            lc.start()
            local.append(lc)

        for r in range(7):
            slot = call_o.at[pl.ds(_mo(8 * other_dev[r], 8), 8), :]
            copy(base + r, slot, slot, sib).wait_recv()
        lhs[...] = jnp.zeros_like(lhs)
        for b in range(8):
            cv = call_o[8 * b:8 * b + 1, :]
            lhs[b:b + 1, :] = cv * _sigmoid(cv)
        cv = cc_ref[...]
        lhs[8:9, :] = cv * _sigmoid(cv)
        mbuf[j] = _dot(lhs[...].astype(BF16), s_ada[...]) + adab_ref[...]
        for k in range(3):
            start(copy(base + 7 + k, mbuf.at[j], mbuf.at[j], (*chips[k], c)))
        for k in range(3):
            copy(base + 7 + k, mbuf.at[cj[k]], mbuf.at[cj[k]], sib).wait_recv()
        mods_o[...] = jnp.zeros_like(mods_o)
        for jj in range(NCHIP):
            mods_o[0:1, 768 * jj:768 * jj + 768] = mbuf[jj, pl.ds(dev, 1), :]
            mods_o[1:2, 768 * jj:768 * jj + 768] = mbuf[jj, 8:9, :]

        kx = [1 - x, x, 1 - x]
        ky = [y, 1 - y, 1 - y]
        pick = lambda k, lst: jnp.where(k == 0, lst[0], jnp.where(k == 1, lst[1], lst[2]))
        for step, k in enumerate([c, 1 - c, 2]):
            for a in range(na):
                reg = specs[a][2](outs[a], pick(k, cj), c)
                copy(6 * a + k, reg, reg, sib).wait_recv()
                if step == 0:
                    start(copy(6 * a + 2, reg, reg, (pick(1 - c, kx), pick(1 - c, ky), c)))
                start(copy(6 * a + 3 + k, reg, reg, sib))
        for k in range(3):
            for a in range(na):
                reg = specs[a][2](outs[a], cj[k], 1 - c)
                copy(6 * a + 3 + k, reg, reg, sib).wait_recv()
        for cp in sends:
            cp.wait_send()
        for lc in local:
            lc.wait()

    out_shape = (jax.ShapeDtypeStruct((8, 3 * D), F32), jax.ShapeDtypeStruct((64, D), F32),
                 jax.ShapeDtypeStruct(specs[0][0], F32), jax.ShapeDtypeStruct(specs[1][0], BF16),
                 jax.ShapeDtypeStruct((2048, D), BF16), jax.ShapeDtypeStruct((D, 3 * D), BF16))
    return pl.pallas_call(
        body, name="gather_in", out_shape=out_shape,
        in_specs=[VMEM] * 7, out_specs=(VMEM, VMEM, VMEM, ANY, ANY, ANY),
        scratch_shapes=[pltpu.VMEM((D, 1280), BF16), pltpu.VMEM((D, 768), BF16), pltpu.VMEM((512, D), BF16),
                        pltpu.VMEM((8, D), F32), pltpu.VMEM((16, D), F32), pltpu.VMEM((NCHIP, 16, 768), F32),
                        pltpu.SemaphoreType.DMA((n_sem,)), pltpu.SemaphoreType.DMA((n_sem,)),
                        pltpu.SemaphoreType.DMA((2 * na + 2,))],
        compiler_params=_cp(vmem_mb=56),
    )(c, c_ctx, ada_w, ada_b_j, w_in, w_out, smalls)


HBM = pl.BlockSpec(memory_space=pltpu.HBM)
SEM = pl.BlockSpec(memory_space=pltpu.SEMAPHORE)


def _late_gather_regions(x, y, c):
    chips = [(1 - x, y), (x, 1 - y), (1 - x, 1 - y)]
    wo_reg = lambda r, jj, cc: r.at[pl.ds(_mo(512 * jj + 256 * cc, 16), 256), :]
    ada_reg = lambda r, jj, cc: r.at[pl.ds(_mo(512 * cc, 16), 512), pl.ds(_mo(768 * jj, 128), 768)]
    return chips, wo_reg, ada_reg


def _late_gather_start(wo_land, ada_land):
    def body(wol_ref, adal_ref, wo_ss, wo_rs, ada_ss, ada_rs, wol_thru, adal_thru, token):
        x, y, c = lax.axis_index("x"), lax.axis_index("y"), lax.axis_index("c")
        j = 2 * x + y
        chips, wo_reg, ada_reg = _late_gather_regions(x, y, c)
        for k in range(3):
            for cc in range(2):
                pltpu.make_async_remote_copy(src_ref=wo_reg(wol_ref, j, c), dst_ref=wo_reg(wol_ref, j, c),
                                             send_sem=wo_ss.at[2 * k + cc], recv_sem=wo_rs.at[2 * k + c],
                                             device_id=(*chips[k], cc), device_id_type=MESH).start()
        for k in range(3):
            for cc in range(2):
                pltpu.make_async_remote_copy(src_ref=ada_reg(adal_ref, j, c), dst_ref=ada_reg(adal_ref, j, c),
                                             send_sem=ada_ss.at[2 * k + cc], recv_sem=ada_rs.at[2 * k + c],
                                             device_id=(*chips[k], cc), device_id_type=MESH).start()
        token[...] = jnp.zeros_like(token)

    sems = pltpu.SemaphoreType.DMA((6,))
    return pl.pallas_call(
        body, name="late_gather_start",
        out_shape=(sems, sems, sems, sems, pltpu.HBM(wo_land.shape, BF16), pltpu.HBM(ada_land.shape, BF16),
                   jax.ShapeDtypeStruct((8, 128), F32)),
        in_specs=(HBM, HBM), out_specs=(SEM, SEM, SEM, SEM, HBM, HBM, VMEM), input_output_aliases={0: 4, 1: 5},
        compiler_params=pltpu.CompilerParams(has_side_effects=pltpu.SideEffectType.DATAFLOW_SIDE_EFFECTING),
    )(pltpu.with_memory_space_constraint(wo_land, pltpu.HBM), pltpu.with_memory_space_constraint(ada_land, pltpu.HBM))


def _late_gather_wait(land, send_sems, recv_sems, which, after, name):
    def body(land_ref, ss, rs, after_ref, land_out):
        x, y, c = lax.axis_index("x"), lax.axis_index("y"), lax.axis_index("c")
        j = 2 * x + y
        chips, wo_reg, ada_reg = _late_gather_regions(x, y, c)
        reg = wo_reg if which == "w_out" else ada_reg
        for k in range(3):
            kj = 2 * chips[k][0] + chips[k][1]
            for cc in range(2):
                cp = pltpu.make_async_remote_copy(src_ref=reg(land_ref, j, c), dst_ref=reg(land_ref, kj, cc),
                                                  send_sem=ss.at[2 * k + cc], recv_sem=rs.at[2 * k + cc],
                                                  device_id=(*chips[k], cc), device_id_type=MESH)
                cp.wait_send()
                cp.wait_recv()

    return pl.pallas_call(
        body, name=name, out_shape=pltpu.HBM(land.shape, land.dtype),
        in_specs=(HBM, SEM, SEM, ANY), out_specs=HBM, input_output_aliases={0: 0},
        compiler_params=pltpu.CompilerParams(has_side_effects=pltpu.SideEffectType.DATAFLOW_SIDE_EFFECTING),
    )(land, send_sems, recv_sems, after)


RCHUNK = 16


def _grads_reduce(hn, dzs, hn_c, dxa_c, g_w_out, pack):
    rp = pack.shape[0]
    hp = rp // 2
    assert hp % RCHUNK == 0
    wi_w = 1280
    lx, lc = hn.shape[0], hn_c.shape[0]
    lt = lx + lc
    n_dz = len(dzs)

    def body(*refs):
        hn_hbm, dz_hbm = refs[0], refs[1:1 + n_dz]
        hnc_hbm, dxac_hbm, wo_hbm, pk_hbm, wi_out, wo_out, pk_out = refs[1 + n_dz:8 + n_dz]
        (hn_mine, hn_other, dzbuf, wi_other, wi_mine, wi_recv, wi_send, wi_rb,
         wo_mine, wo_recv, wo_send, wo_rb, wo_own, pk_mine, pk_recv, pk_send, pk_rb, pk_own,
         send_sems, recv_sems, local_sems) = refs[8 + n_dz:]
        x, y, c = lax.axis_index("x"), lax.axis_index("y"), lax.axis_index("c")
        j = 2 * x + y
        sib = (x, y, 1 - c)
        chips = [(1 - x, y), (x, 1 - y), (1 - x, 1 - y)]
        cj = [2 * cx + cy for cx, cy in chips]
        near = (jnp.where(c == 0, 1 - x, x), jnp.where(c == 0, y, 1 - y), c)
        slabs = [cj[2], cj[0], cj[1], j]

        def copy(k, src, dst, to):
            return pltpu.make_async_remote_copy(src_ref=src, dst_ref=dst, send_sem=send_sems.at[k],
                                                recv_sem=recv_sems.at[k], device_id=to, device_id_type=MESH)

        def local(k, src, dst):
            cp = pltpu.make_async_copy(src, dst, local_sems.at[k])
            cp.start()
            return cp

        rows_half = lambda r, cc, n: r.at[pl.ds(_mo(cc * n, 16), n), :]
        cols_half = lambda r, cc, n: r.at[:, pl.ds(_mo(cc * n, 128), n)]
        pk_piece = lambda r, cc, jj: r.at[pl.ds(_mo(cc * hp, 16), hp), pl.ds(_mo(jj * 128, 128), 128)]

        sends = []

        def start(cp):
            cp.start()
            sends.append(cp)

        def dz_pieces(s):
            g0 = wi_w * s
            k0, off0 = g0 // D, g0 % D
            w0 = min(D - off0, wi_w)
            pieces = [(k0, off0, w0, 0)]
            if w0 < wi_w:
                pieces.append((k0 + 1, 0, wi_w - w0, w0))
            return pieces

        def dz_copies(s):
            cps = []
            for q, (k, off, w, dst) in enumerate(dz_pieces(s)):
                cps.append(pltpu.make_async_copy(dz_hbm[k].at[:, pl.ds(off, w)], dzbuf.at[pl.ds(0, lx), pl.ds(dst, w)],
                                                 local_sems.at[11 + q]))
            if s == 0:
                cps.append(pltpu.make_async_copy(dxac_hbm, dzbuf.at[pl.ds(lx, lc), pl.ds(0, D)], local_sems.at[13]))
            return cps

        def dz_load(sl):
            for s in range(NCHIP):
                @pl.when(sl == s)
                def _():
                    if s == 0:
                        dzbuf[pl.ds(lx, lc), pl.ds(D, wi_w - D)] = jnp.zeros((lc, wi_w - D), BF16)
                    else:
                        dzbuf[pl.ds(lx, lc), :] = jnp.zeros((lc, wi_w), BF16)
                    for cp in dz_copies(s):
                        cp.start()

        def dz_wait(sl):
            for s in range(NCHIP):
                @pl.when(sl == s)
                def _():
                    for cp in dz_copies(s):
                        cp.wait()

        l_pk = local(0, rows_half(pk_hbm, c, hp), pk_mine)
        start(copy(0, rows_half(pk_hbm, 1 - c, hp), pk_recv, sib))
        l_wo = local(1, cols_half(wo_hbm, c, 512), wo_mine)
        start(copy(1, cols_half(wo_hbm, 1 - c, 512), wo_recv, sib))
        hn_loads = [local(2, hn_hbm.at[:, pl.ds(_mo(c * 512, 128), 512)], hn_mine.at[pl.ds(0, lx), :]),
                    local(3, hnc_hbm.at[:, pl.ds(_mo(c * 512, 128), 512)], hn_mine.at[pl.ds(lx, lc), :]),
                    local(4, hn_hbm.at[:, pl.ds(_mo((1 - c) * 512, 128), 512)], hn_other.at[pl.ds(0, lx), :]),
                    local(5, hnc_hbm.at[:, pl.ds(_mo((1 - c) * 512, 128), 512)], hn_other.at[pl.ds(lx, lc), :])]
        dz_load(slabs[0])

        def pair_sum(mine, recv, send, nrows, keep, relayed=None):
            def step(i, carry):
                rows = pl.ds(_mo(i * RCHUNK, RCHUNK), RCHUNK)
                s = mine[rows, :] + recv[rows, :].astype(F32)
                if relayed is not None:
                    s = s + relayed[rows, :].astype(F32)
                if keep:
                    mine[rows, :] = s
                if send is not None:
                    send[rows, :] = s.astype(BF16)
                return carry
            lax.fori_loop(0, nrows // RCHUNK, step, 0)

        def chip_sum(own, rb, nrows, terms=(0, 1, 2)):
            def step(i, carry):
                rows = pl.ds(_mo(i * RCHUNK, RCHUNK), RCHUNK)
                acc = own[rows, :]
                for q in terms:
                    acc = acc + rb[q, rows, :].astype(F32)
                own[rows, :] = acc
                return carry
            lax.fori_loop(0, nrows // RCHUNK, step, 0)

        p1 = [None] * NCHIP

        def slab_matmuls(s):
            if s >= 2:
                p1[s - 2].wait_send()
            dz_wait(slabs[s])
            wi_other[s % 2] = _dot_tn(hn_other[...], dzbuf[...]).astype(BF16)
            p1[s] = copy(2 + s, wi_other.at[s % 2], wi_recv.at[s], sib)
            p1[s].start()
            wi_mine[s % 2] = _dot_tn(hn_mine[...], dzbuf[...])
            if s + 1 < NCHIP:
                dz_load(slabs[s + 1])

        def slab_finish(s):
            copy(2 + s, wi_recv.at[s], wi_recv.at[s], sib).wait_recv()
            if s == 3:
                pair_sum(wi_mine.at[s % 2], wi_recv.at[s], None, 512, True)
                return
            if s == 0:
                pair_sum(wi_mine.at[0], wi_recv.at[0], wi_send.at[0], 512, False)
                start(copy(12, wi_send.at[0], wi_rb.at[0], near))
                return
            adds_relayed = c == (1 if s == 1 else 0)

            @pl.when(adds_relayed)
            def _():
                copy(12, wi_rb.at[0], wi_rb.at[0], sib).wait_recv()
                pair_sum(wi_mine.at[s % 2], wi_recv.at[s], wi_send.at[s], 512, False, wi_rb.at[0])

            @pl.when(jnp.logical_not(adds_relayed))
            def _():
                pair_sum(wi_mine.at[s % 2], wi_recv.at[s], wi_send.at[s], 512, False)
            start(copy(12 + s, wi_send.at[s], wi_rb.at[s], (*chips[s - 1], c)))

        l_wo.wait()
        copy(1, wo_recv, wo_recv, sib).wait_recv()
        pair_sum(wo_mine, wo_recv, wo_send, 2048, True)
        for k in range(3):
            start(copy(9 + k, wo_send.at[pl.ds(_mo(cj[k] * 512, 16), 512), :], wo_rb.at[k], (*chips[k], c)))
        l_wo_own = local(7, wo_mine.at[pl.ds(_mo(j * 512, 16), 512), :], wo_own)

        for cp in hn_loads:
            cp.wait()
        slab_matmuls(0)

        l_pk.wait()
        copy(0, pk_recv, pk_recv, sib).wait_recv()
        pair_sum(pk_mine, pk_recv, pk_send, hp, True)
        for k in range(3):
            start(copy(6 + k, pk_send.at[:, pl.ds(_mo(cj[k] * 128, 128), 128)], pk_rb.at[k], (*chips[k], c)))
        l_pk_own = local(6, pk_mine.at[:, pl.ds(_mo(j * 128, 128), 128)], pk_own)

        slab_matmuls(1)
        slab_finish(0)

        l_pk_own.wait()
        for k in range(3):
            copy(6 + k, pk_rb.at[k], pk_rb.at[k], sib).wait_recv()
        chip_sum(pk_own, pk_rb, hp)
        l_pk_out = local(8, pk_own, pk_piece(pk_out, c, j))
        start(copy(15, pk_own, pk_piece(pk_out, c, j), sib))
        for k in range(3):
            start(copy(16 + k, pk_own, pk_piece(pk_out, c, j), (*chips[k], c)))

        slab_matmuls(2)
        slab_finish(1)
        slab_matmuls(3)
        slab_finish(2)

        l_wo_own.wait()
        for k in range(3):
            copy(9 + k, wo_rb.at[k], wo_rb.at[k], sib).wait_recv()
        chip_sum(wo_own, wo_rb, 512)
        l_wo_out = local(9, wo_own, cols_half(wo_out, c, 512))
        start(copy(22, wo_own, cols_half(wo_out, c, 512), sib))

        for k in range(3):
            reg = pk_piece(pk_out, c, cj[k])
            copy(16 + k, reg, reg, sib).wait_recv()
            start(copy(19 + k, reg, reg, sib))

        slab_finish(3)
        for k in (1, 2):
            copy(12 + k, wi_rb.at[k], wi_rb.at[k], sib).wait_recv()
        chip_sum(wi_mine.at[1], wi_rb, 512, (1, 2))
        l_wi_out = local(10, wi_mine.at[1], rows_half(wi_out, c, 512))
        start(copy(23, wi_mine.at[1], rows_half(wi_out, c, 512), sib))

        reg = pk_piece(pk_out, 1 - c, j)
        copy(15, reg, reg, sib).wait_recv()
        for k in range(3):
            reg = pk_piece(pk_out, 1 - c, cj[k])
            copy(19 + k, reg, reg, sib).wait_recv()
        reg = cols_half(wo_out, 1 - c, 512)
        copy(22, reg, reg, sib).wait_recv()
        reg = rows_half(wi_out, 1 - c, 512)
        copy(23, reg, reg, sib).wait_recv()
        for cp in sends + p1[2:]:
            cp.wait_send()
        for cp in (l_pk_out, l_wo_out, l_wi_out):
            cp.wait()

    return pl.pallas_call(
        body, name="grads_reduce",
        out_shape=(jax.ShapeDtypeStruct((D, wi_w), F32), jax.ShapeDtypeStruct((512, D), F32),
                   jax.ShapeDtypeStruct(pack.shape, F32)),
        in_specs=[ANY] * (5 + n_dz), out_specs=(ANY,) * 3,
        scratch_shapes=[
            pltpu.VMEM((lt, 512), BF16), pltpu.VMEM((lt, 512), BF16), pltpu.VMEM((lt, wi_w), BF16),
            pltpu.VMEM((2, 512, wi_w), BF16), pltpu.VMEM((2, 512, wi_w), F32), pltpu.VMEM((4, 512, wi_w), BF16),
            pltpu.VMEM((3, 512, wi_w), BF16), pltpu.VMEM((3, 512, wi_w), BF16),
            pltpu.VMEM((2048, 512), F32), pltpu.VMEM((2048, 512), F32), pltpu.VMEM((2048, 512), BF16),
            pltpu.VMEM((3, 512, 512), BF16), pltpu.VMEM((512, 512), F32),
            pltpu.VMEM((hp, 512), F32), pltpu.VMEM((hp, 512), F32), pltpu.VMEM((hp, 512), BF16),
            pltpu.VMEM((3, hp, 128), BF16), pltpu.VMEM((hp, 128), F32),
            pltpu.SemaphoreType.DMA((24,)), pltpu.SemaphoreType.DMA((24,)), pltpu.SemaphoreType.DMA((14,))],
        compiler_params=_cp(vmem_mb=56),
    )(hn, *dzs, hn_c, dxa_c, g_w_out, pack)


def _ada_bwd(c_all, dmx_all_j, dmc_j, dmx_all, dmc, c_ctx, ada_w_full):
    def body(c_ref, dmxj_ref, dmcj_ref, dmx_ref, dmc_ref, cc_ref, w_ref, gw_ref, gb_ref, gc_ref, lhs, rhs, dm8):
        lhs[...] = jnp.zeros_like(lhs)
        rhs[...] = jnp.zeros_like(rhs)
        cv = c_ref[...]
        lhs[0:8, :] = cv * _sigmoid(cv)
        cc = cc_ref[...]
        a_c, da_c = _silu_and_grad(cc)
        lhs[8:9, :] = a_c
        rhs[0:8, :] = dmxj_ref[...]
        rhs[8:9, :] = dmcj_ref[...]
        gw_ref[...] = _dot_tn(lhs[...].astype(BF16), rhs[...].astype(BF16))
        gb_ref[...] = jnp.sum(dmx_ref[...], axis=0, keepdims=True) + dmc_ref[...]
        dm8[...] = jnp.zeros_like(dm8)
        dm8[0:1, :] = dmc_ref[...]
        da = _dot_nt(dm8[...].astype(BF16), w_ref[...])
        gc_ref[...] = da[0:1, :] * da_c

    return pl.pallas_call(
        body, name="ada_bwd",
        out_shape=(jax.ShapeDtypeStruct((D, 768), F32), jax.ShapeDtypeStruct((1, 3 * D), F32),
                   jax.ShapeDtypeStruct((1, D), F32)),
        in_specs=[VMEM] * 7, out_specs=(VMEM,) * 3,
        scratch_shapes=[pltpu.VMEM((16, D), F32), pltpu.VMEM((16, 768), F32), pltpu.VMEM((8, 3 * D), F32)],
        compiler_params=_cp(vmem_mb=32),
    )(c_all, dmx_all_j, dmc_j, dmx_all, dmc, c_ctx, ada_w_full)


def _proj(x, mods, mrow, norm_g, w_full, nk, name):
    lx = x.shape[0]
    n = lx // T

    def body(x_ref, sh_ref, sc_ref, ng_ref, *rest):
        w_refs, hn_ref, z_refs = rest[:nk], rest[nk], rest[nk + 1:]
        xv = x_ref[...]
        r = lax.rsqrt(jnp.mean(xv * xv, axis=-1, keepdims=True) + NORM_EPS)
        hn = (xv * r) * ng_ref[...] * (1.0 + sc_ref[mrow:mrow + 1, :]) + sh_ref[mrow:mrow + 1, :]
        hb = hn.astype(BF16)
        hn_ref[...] = hb
        for k in range(nk):
            z_refs[k][...] = _dot(hb, w_refs[k][...])

    row = pl.BlockSpec((T, D), lambda i: (i, 0))
    in_specs = [row, pl.BlockSpec((8, D), lambda i: (0, 0)), pl.BlockSpec((8, D), lambda i: (0, 1)),
                pl.BlockSpec((1, D), lambda i: (0, 0))]
    in_specs += [pl.BlockSpec((D, D), lambda i, k=k: (0, k)) for k in range(nk)]
    out_shape = (jax.ShapeDtypeStruct((lx, D), BF16),) + tuple(jax.ShapeDtypeStruct((lx, D), F32) for _ in range(nk))
    return pl.pallas_call(
        body, name=name, grid=(n,), out_shape=out_shape, in_specs=in_specs, out_specs=(row,) * (nk + 1),
        compiler_params=_cp(1, vmem_mb=56),
    )(x, mods, mods, norm_g, *([w_full] * nk))


def _halo_specs(lx):
    last = lx // 8 - 1
    return [pl.BlockSpec((T, D), lambda i: (i, 0)),
            pl.BlockSpec((8, D), lambda i: (jnp.maximum(i * (T // 8) - 1, 0), 0)),
            pl.BlockSpec((8, D), lambda i: (jnp.minimum((i + 1) * (T // 8), last), 0))]


def _zhalo_specs(lx):
    last = lx // 8 - 1
    return [pl.BlockSpec((T * NH, HD), lambda i: (i, 0)),
            pl.BlockSpec((8 * NH, HD), lambda i: (jnp.maximum(i * (T // 8) - 1, 0), 0)),
            pl.BlockSpec((8 * NH, HD), lambda i: (jnp.minimum((i + 1) * (T // 8), last), 0))]


ZT = pl.BlockSpec((T * NH, HD), lambda i: (i, 0))
CONV_CHUNK = 32


def _lru_gates_fwd(xa, conv_wz, conv_bz, wcat, bcat, lamcat, name):
    lx = xa.shape[0]
    n = lx // T

    def body(xm, xp, xn, cw, cb, w_ref, b_ref, lam_ref, xaz_o, xcz_o, af_o, uf_o, ab_o, ub_o, pad):
        i = pl.program_id(0)
        pmask = jnp.where(i == 0, 0.0, 1.0)
        nmask = jnp.where(i == n - 1, 0.0, 1.0)
        for h in range(NH):
            cols = slice(HD * h, HD * h + HD)
            pad[_zrows(h, 8), :] = xp[:, cols] * pmask
            pad[pl.ds(8 * NH + h, T, stride=NH), :] = xm[:, cols]
            pad[pl.ds((T + 8) * NH + h, 8, stride=NH), :] = xn[:, cols] * nmask
        xaz_o[...] = pad[pl.ds(8 * NH, T * NH), :]

        def conv_chunk(ci, carry):
            base = pl.multiple_of(ci * (CONV_CHUNK * NH), CONV_CHUNK * NH)
            acc = None
            for k in range(4):
                sl = pad[pl.ds(base + (7 + k) * NH, CONV_CHUNK * NH), :].reshape(CONV_CHUNK, NH, HD)
                term = sl * cw[k][None]
                acc = term if acc is None else acc + term
            acc = acc + cb[...][None]
            xcz_o[pl.ds(base, CONV_CHUNK * NH), :] = acc.reshape(CONV_CHUNK * NH, HD)
            return carry
        lax.fori_loop(0, T // CONV_CHUNK, conv_chunk, 0)

        outs = ((af_o, uf_o), (ab_o, ub_o))
        for h in range(NH):
            xch = xcz_o[_zrows(h, T), :]
            pre = _dot(xch.astype(BF16), w_ref[h]) + b_ref[h:h + 1, :]
            for d in range(2):
                _, gi, _, _, a, mult = _lru_gate(pre, lam_ref[h:h + 1, :], d)
                outs[d][0][_zrows(h, T), :] = a
                outs[d][1][_zrows(h, T), :] = mult * gi * xch

    full = lambda shape: pl.BlockSpec(shape, lambda i: (0,) * len(shape))
    in_specs = _halo_specs(lx) + [full((4, NH, HD)), full((NH, HD)), full((NH, HD, 4 * HD)), full((NH, 4 * HD)),
                                  full((NH, 2 * HD))]
    zs = jax.ShapeDtypeStruct((lx * NH, HD), F32)
    return pl.pallas_call(
        body, name=name, grid=(n,), out_shape=(zs,) * 6, in_specs=in_specs, out_specs=(ZT,) * 6,
        scratch_shapes=[pltpu.VMEM(((T + 16) * NH, HD), F32)],
        compiler_params=_cp(1, vmem_mb=48),
    )(xa, xa, xa, conv_wz, conv_bz, wcat, bcat, lamcat)


def _scan(a_up, x_up, a_dn, x_dn, s_up, s_dn, post, name):
    lx = a_up.shape[0] // NH
    n = lx // T

    def body(au, xu, ad, xd, su0, sd0, *rest):
        if post:
            ou, od, fu, fd, carry = rest
        else:
            ou, od, pu, pd, fu, fd, carry = rest
        i = pl.program_id(0)

        @pl.when(i == 0)
        def _():
            carry[0] = su0[...]
            carry[1] = sd0[...]

        def step(k, s):
            su, sd = s
            ru = pl.ds(_mo(k * NH, NH), NH)
            rd = pl.ds(_mo((T - 1 - k) * NH, NH), NH)
            if post:
                vu = xu[ru, :] + su
                vd = xd[rd, :] + sd
                ou[ru, :] = vu
                od[rd, :] = vd
                return au[ru, :] * vu, ad[rd, :] * vd
            pu[ru, :] = su
            pd[rd, :] = sd
            vu = au[ru, :] * su + xu[ru, :]
            vd = ad[rd, :] * sd + xd[rd, :]
            ou[ru, :] = vu
            od[rd, :] = vd
            return vu, vd

        su, sd = lax.fori_loop(0, T, step, (carry[0], carry[1]), unroll=8)
        carry[0] = su
        carry[1] = sd
        fu[...] = su
        fd[...] = sd

    up = pl.BlockSpec((T * NH, HD), lambda i: (i, 0))
    dn = pl.BlockSpec((T * NH, HD), lambda i: (n - 1 - i, 0))
    st = pl.BlockSpec((NH, HD), lambda i: (0, 0))
    zs = jax.ShapeDtypeStruct((lx * NH, HD), F32)
    ss = jax.ShapeDtypeStruct((NH, HD), F32)
    if post:
        out_shape, out_specs = (zs, zs, ss, ss), (up, dn, st, st)
    else:
        out_shape, out_specs = (zs, zs, zs, zs, ss, ss), (up, dn, up, dn, st, st)
    return pl.pallas_call(
        body, name=name, grid=(n,), out_shape=out_shape, in_specs=[up, up, dn, dn, st, st], out_specs=out_specs,
        scratch_shapes=[pltpu.VMEM((2, NH, HD), F32)],
        compiler_params=_cp(1, vmem_mb=48),
    )(a_up, x_up, a_dn, x_dn, s_up, s_dn)


def _sgu_parts(u, v, lng, lnb, w_ref, bt_ref, mixed_s):
    ug, dug = _gelu_and_grad(u)
    vg, dvg = _gelu_and_grad(v)
    mu = jnp.mean(vg, axis=-1, keepdims=True)
    vc = vg - mu
    rstd = lax.rsqrt(jnp.mean(vc * vc, axis=-1, keepdims=True) + LN_EPS)
    vh = vc * rstd
    vn = (vh * lng + lnb).astype(BF16)
    for g in range(NH):
        cols = slice(HD * g, HD * g + HD)
        mixed_s[:, cols] = _dot(w_ref[g], vn[:, cols]) + bt_ref[:, g:g + 1]
    return ug, dug, dvg, rstd, vh, vn


def _sgu_fwd(u, v, ln_g, ln_b, sgu_w, sgu_bt):
    lx = u.shape[0]
    n = lx // HD

    def body(u_ref, v_ref, g_ref, b_ref, w_ref, bt_ref, y_ref, mixed_s):
        ug, _, _, _, _, _ = _sgu_parts(u_ref[...], v_ref[...], g_ref[...], b_ref[...], w_ref, bt_ref, mixed_s)
        y_ref[...] = ug * mixed_s[...]

    row = pl.BlockSpec((HD, D), lambda i: (i, 0))
    vec = pl.BlockSpec((1, D), lambda i: (0, 0))
    return pl.pallas_call(
        body, name="sgu_fwd", grid=(n,), out_shape=jax.ShapeDtypeStruct((lx, D), F32),
        in_specs=[row, row, vec, vec, pl.BlockSpec((NH, HD, HD), lambda i: (0, 0, 0)),
                  pl.BlockSpec((HD, NH), lambda i: (0, 0))],
        out_specs=row, scratch_shapes=[pltpu.VMEM((HD, D), F32)],
        compiler_params=_cp(1),
    )(u, v, ln_g, ln_b, sgu_w, sgu_bt)


def _sgu_bwd(u, v, dys, ln_g, ln_b, sgu_w, sgu_bt):
    lx = u.shape[0]
    n = lx // HD

    def body(u_ref, v_ref, dy_ref, g_ref, b_ref, w_ref, bt_ref, du_ref, dv_ref, dw_ref, db_ref, dg_ref, dbl_ref,
             mixed_s, dvn_s):
        i = pl.program_id(0)

        @pl.when(i == 0)
        def _():
            dw_ref[...] = jnp.zeros_like(dw_ref)
            db_ref[...] = jnp.zeros_like(db_ref)
            dg_ref[...] = jnp.zeros_like(dg_ref)
            dbl_ref[...] = jnp.zeros_like(dbl_ref)

        lng = g_ref[...]
        ug, dug, dvg, rstd, vh, vn = _sgu_parts(u_ref[...], v_ref[...], lng, b_ref[...], w_ref, bt_ref, mixed_s)
        dys_v = dy_ref[...]
        du_ref[...] = (dys_v * mixed_s[...] * dug).astype(BF16)
        dmix = dys_v * ug
        ones = jnp.ones((8, HD), BF16)
        for g in range(NH):
            cols = slice(HD * g, HD * g + HD)
            dm = dmix[:, cols]
            hi = dm.astype(BF16)
            lo = (dm - hi.astype(F32)).astype(BF16)
            dw_ref[g] += _dot_nt(hi, vn[:, cols])
            db_ref[g:g + 1, :] += (_dot_nt(ones, hi) + _dot_nt(ones, lo))[0:1, :]
            dvn_s[:, cols] = _dot_tn(w_ref[g], hi)
        dvn = dvn_s[...]
        dg_ref[...] += jnp.sum(dvn * vh, axis=0, keepdims=True)
        dbl_ref[...] += jnp.sum(dvn, axis=0, keepdims=True)
        dvh = dvn * lng
        dvg_in = rstd * (dvh - jnp.mean(dvh, axis=-1, keepdims=True)
                         - vh * jnp.mean(dvh * vh, axis=-1, keepdims=True))
        dv_ref[...] = (dvg_in * dvg).astype(BF16)

    row = pl.BlockSpec((HD, D), lambda i: (i, 0))
    vec = pl.BlockSpec((1, D), lambda i: (0, 0))
    wsp = pl.BlockSpec((NH, HD, HD), lambda i: (0, 0, 0))
    bsp = pl.BlockSpec((NH, HD), lambda i: (0, 0))
    return pl.pallas_call(
        body, name="sgu_bwd", grid=(n,),
        out_shape=(jax.ShapeDtypeStruct((lx, D), BF16), jax.ShapeDtypeStruct((lx, D), BF16),
                   jax.ShapeDtypeStruct((NH, HD, HD), F32), jax.ShapeDtypeStruct((NH, HD), F32),
                   jax.ShapeDtypeStruct((1, D), F32), jax.ShapeDtypeStruct((1, D), F32)),
        in_specs=[row, row, row, vec, vec, wsp, pl.BlockSpec((HD, NH), lambda i: (0, 0))],
        out_specs=(row, row, wsp, bsp, vec, vec),
        scratch_shapes=[pltpu.VMEM((HD, D), F32), pltpu.VMEM((HD, D), F32)],
        compiler_params=_cp(1),
    )(u, v, dys, ln_g, ln_b, sgu_w, sgu_bt)


def _out_fwd_bwd(hf_z, hb_z, ga, gb, ys, x, tgt, mods, final_g, w_out_full):
    lx = x.shape[0]
    n = lx // T

    def body(hf_ref, hb_ref, ga_ref, gb_ref, ys_ref, x_ref, t_ref, gx_ref, fg_ref, w_ref,
             loss_ref, dfg_ref, dgx_ref, dxn_ref, y_ref, do_ref, dga_ref, dgb_ref, dyl_ref, dys_ref, yl_s):
        i = pl.program_id(0)

        @pl.when(i == 0)
        def _():
            loss_ref[...] = jnp.zeros_like(loss_ref)
            dfg_ref[...] = jnp.zeros_like(dfg_ref)
            dgx_ref[...] = jnp.zeros_like(dgx_ref)

        for h in range(NH):
            yl_s[:, HD * h:HD * h + HD] = hf_ref[_zrows(h, T), :] + hb_ref[_zrows(h, T), :]
        yl = yl_s[...]
        gav = ga_ref[...]
        gbv = gb_ref[...]
        sa, dsa = _silu_and_grad(gav)
        sb, dsb = _silu_and_grad(gbv)
        ysv = ys_ref[...]
        y_ref[:, 0:D] = (yl * sa).astype(BF16)
        y_ref[:, D:2 * D] = (ysv * sb).astype(BF16)
        o = _dot(y_ref[...], w_ref[...])
        gx = gx_ref[0:1, :]
        xnew = x_ref[...] + gx * o
        r2 = lax.rsqrt(jnp.mean(xnew * xnew, axis=-1, keepdims=True) + NORM_EPS)
        xh = xnew * r2
        fg = fg_ref[...]
        err = xh * fg - t_ref[...]
        loss_ref[...] += 0.5 * jnp.sum(jnp.mean(err * err, axis=-1, keepdims=True), axis=0, keepdims=True)
        dout = err * (1.0 / D)
        dfg_ref[...] += jnp.sum(dout * xh, axis=0, keepdims=True)
        dxh = dout * fg
        dxn = r2 * (dxh - xh * jnp.mean(dxh * xh, axis=-1, keepdims=True))
        dxn_ref[...] = dxn
        dgx_ref[...] += jnp.sum(dxn * o, axis=0, keepdims=True)
        do = (dxn * gx).astype(BF16)
        do_ref[...] = do
        dy = _dot_nt(do, w_ref[...])
        dy1 = dy[:, 0:D]
        dy2 = dy[:, D:2 * D]
        dga_ref[...] = (dy1 * yl * dsa).astype(BF16)
        dgb_ref[...] = (dy2 * ysv * dsb).astype(BF16)
        dys_ref[...] = dy2 * sb
        yl_s[...] = dy1 * sa
        for h in range(NH):
            dyl_ref[_zrows(h, T), :] = yl_s[:, HD * h:HD * h + HD]

    row = pl.BlockSpec((T, D), lambda i: (i, 0))
    vec = pl.BlockSpec((1, D), lambda i: (0, 0))
    in_specs = [ZT, ZT, row, row, row, row, row, pl.BlockSpec((8, D), lambda i: (0, 2)), vec,
                pl.BlockSpec((2 * D, D), lambda i: (0, 0))]
    out_shape = (jax.ShapeDtypeStruct((1, 1), F32), jax.ShapeDtypeStruct((1, D), F32), jax.ShapeDtypeStruct((1, D), F32),
                 jax.ShapeDtypeStruct((lx, D), F32), jax.ShapeDtypeStruct((lx, 2 * D), BF16),
                 jax.ShapeDtypeStruct((lx, D), BF16), jax.ShapeDtypeStruct((lx, D), BF16),
                 jax.ShapeDtypeStruct((lx, D), BF16), jax.ShapeDtypeStruct((lx * NH, HD), F32),
                 jax.ShapeDtypeStruct((lx, D), F32))
    out_specs = (pl.BlockSpec((1, 1), lambda i: (0, 0)), vec, vec, row, pl.BlockSpec((T, 2 * D), lambda i: (i, 0)),
                 row, row, row, ZT, row)
    return pl.pallas_call(
        body, name="out_fwd_bwd", grid=(n,), out_shape=out_shape, in_specs=in_specs, out_specs=out_specs,
        scratch_shapes=[pltpu.VMEM((T, D), F32)],
        compiler_params=_cp(1, vmem_mb=56),
    )(hf_z, hb_z, ga, gb, ys, x, tgt, mods, final_g, w_out_full)


def _lru_gates_bwd(xc_z, lf_z, lb_z, pf_z, pb_z, wcat, bcat, lamcat, dw0, db0, dl0, name):
    lx = xc_z.shape[0] // NH
    n = lx // T

    def body(xc_ref, lf_ref, lb_ref, pf_ref, pb_ref, w_ref, b_ref, lam_ref, dw0_ref, db0_ref, dl0_ref,
             dxc_ref, dw_ref, db_ref, dl_ref, dpre_s):
        i = pl.program_id(0)

        @pl.when(i == 0)
        def _():
            dw_ref[...] = dw0_ref[...]
            db_ref[...] = db0_ref[...]
            dl_ref[...] = dl0_ref[...]

        lam_refs = (lf_ref, lb_ref)
        prev_refs = (pf_ref, pb_ref)
        for h in range(NH):
            xch = xc_ref[_zrows(h, T), :]
            xcb = xch.astype(BF16)
            pre = _dot(xcb, w_ref[h]) + b_ref[h:h + 1, :]
            dxc = jnp.zeros((T, HD), F32)
            for d in range(2):
                r, gi, lam, sp, a, mult = _lru_gate(pre, lam_ref[h:h + 1, :], d)
                du = lam_refs[d][_zrows(h, T), :]
                da = du * prev_refs[d][_zrows(h, T), :]
                dgi = du * mult * xch
                dxc = dxc + du * mult * gi
                dmult = du * gi * xch
                dla = da * a - dmult * (a * a) / mult
                dr = dla * ((-LRU_C) * sp)
                dsp = jnp.sum(dla * ((-LRU_C) * r), axis=0, keepdims=True)
                dl_ref[h:h + 1, HD * d:HD * d + HD] += dsp * (-_sigmoid(-lam))
                dpre_s[:, 256 * d:256 * d + HD] = dr * r * (1.0 - r)
                dpre_s[:, 256 * d + HD:256 * d + 2 * HD] = dgi * gi * (1.0 - gi)
            dpre = dpre_s[...]
            dpb = dpre.astype(BF16)
            dw_ref[h] += _dot_tn(xcb, dpb)
            db_ref[h:h + 1, :] += jnp.sum(dpre, axis=0, keepdims=True)
            dxc_ref[_zrows(h, T), :] = dxc + _dot_nt(dpb, w_ref[h])

    full = lambda shape: pl.BlockSpec(shape, lambda i: (0,) * len(shape))
    wsp, bsp, lsp = full((NH, HD, 4 * HD)), full((NH, 4 * HD)), full((NH, 2 * HD))
    return pl.pallas_call(
        body, name=name, grid=(n,),
        out_shape=(jax.ShapeDtypeStruct((lx * NH, HD), F32), jax.ShapeDtypeStruct((NH, HD, 4 * HD), F32),
                   jax.ShapeDtypeStruct((NH, 4 * HD), F32), jax.ShapeDtypeStruct((NH, 2 * HD), F32)),
        in_specs=[ZT] * 5 + [wsp, bsp, lsp, wsp, bsp, lsp], out_specs=(ZT, wsp, bsp, lsp),
        scratch_shapes=[pltpu.VMEM((T, 4 * HD), F32)],
        compiler_params=_cp(1, vmem_mb=48),
    )(xc_z, lf_z, lb_z, pf_z, pb_z, wcat, bcat, lamcat, dw0, db0, dl0)


def _conv_bwd(dxc_z, xa_z, conv_wz, dcw0, dcb0, name):
    lx = dxc_z.shape[0] // NH
    n = lx // T

    def body(dm, dp, dn, xa_ref, cw, dcw0_ref, dcb0_ref, dxa_ref, dcw_ref, dcb_ref, pad, dxa_s):
        i = pl.program_id(0)

        @pl.when(i == 0)
        def _():
            dcw_ref[...] = dcw0_ref[...]
            dcb_ref[...] = dcb0_ref[...]

        pmask = jnp.where(i == 0, 0.0, 1.0)
        nmask = jnp.where(i == n - 1, 0.0, 1.0)
        pad[pl.ds(0, 8 * NH), :] = dp[...] * pmask
        pad[pl.ds(8 * NH, T * NH), :] = dm[...]
        pad[pl.ds((T + 8) * NH, 8 * NH), :] = dn[...] * nmask

        def chunk(ci, carry):
            base = pl.multiple_of(ci * (CONV_CHUNK * NH), CONV_CHUNK * NH)
            xav = xa_ref[pl.ds(base, CONV_CHUNK * NH), :].reshape(CONV_CHUNK, NH, HD)
            acc = None
            for k in range(4):
                sl = pad[pl.ds(base + (9 - k) * NH, CONV_CHUNK * NH), :].reshape(CONV_CHUNK, NH, HD)
                term = sl * cw[k][None]
                acc = term if acc is None else acc + term
                dcw_ref[k] += jnp.sum(sl * xav, axis=0)
                if k == 1:
                    dcb_ref[...] += jnp.sum(sl, axis=0)
            dxa_s[pl.ds(base, CONV_CHUNK * NH), :] = acc.reshape(CONV_CHUNK * NH, HD)
            return carry
        lax.fori_loop(0, T // CONV_CHUNK, chunk, 0)
        for h in range(NH):
            dxa_ref[:, HD * h:HD * h + HD] = dxa_s[_zrows(h, T), :].astype(BF16)

    full = lambda shape: pl.BlockSpec(shape, lambda i: (0,) * len(shape))
    return pl.pallas_call(
        body, name=name, grid=(n,),
        out_shape=(jax.ShapeDtypeStruct((lx, D), BF16), jax.ShapeDtypeStruct((4, NH, HD), F32),
                   jax.ShapeDtypeStruct((NH, HD), F32)),
        in_specs=_zhalo_specs(lx) + [ZT, full((4, NH, HD)), full((4, NH, HD)), full((NH, HD))],
        out_specs=(pl.BlockSpec((T, D), lambda i: (i, 0)), full((4, NH, HD)), full((NH, HD))),
        scratch_shapes=[pltpu.VMEM(((T + 16) * NH, HD), F32), pltpu.VMEM((T * NH, HD), F32)],
        compiler_params=_cp(1, vmem_mb=48),
    )(dxc_z, dxc_z, dxc_z, xa_z, conv_wz, dcw0, dcb0)


def _proj_bwd(dzs, x, dxn, mods, mrow, norm_g, w_full, dng0, name):
    lx = x.shape[0]
    n = lx // T
    nk = len(dzs)
    has_x = dxn is not None

    def body(*refs):
        dz_refs = refs[:nk]
        w_refs = refs[nk:2 * nk]
        x_ref, sc_ref, ng_ref, dng0_ref = refs[2 * nk:2 * nk + 4]
        rest = refs[2 * nk + 4:]
        if has_x:
            dxn_ref, gx_ref, dng_ref, dsc_ref, dsh_ref = rest
        else:
            dng_ref, dsc_ref, dsh_ref = rest
        i = pl.program_id(0)

        @pl.when(i == 0)
        def _():
            dng_ref[...] = dng0_ref[...]
            dsc_ref[...] = jnp.zeros_like(dsc_ref)
            dsh_ref[...] = jnp.zeros_like(dsh_ref)

        dhn = _dot_nt(dz_refs[0][...], w_refs[0][...])
        for k in range(1, nk):
            dhn = dhn + _dot_nt(dz_refs[k][...], w_refs[k][...])
        xv = x_ref[...]
        r = lax.rsqrt(jnp.mean(xv * xv, axis=-1, keepdims=True) + NORM_EPS)
        xn = xv * r
        ng = ng_ref[...]
        sc1 = 1.0 + sc_ref[mrow:mrow + 1, :]
        t = dhn * xn
        dng_ref[...] += jnp.sum(t * sc1, axis=0, keepdims=True)
        dsc_ref[...] += jnp.sum(t * ng, axis=0, keepdims=True)
        dsh_ref[...] += jnp.sum(dhn, axis=0, keepdims=True)
        if has_x:
            dxh = dhn * (ng * sc1)
            gx_ref[...] = dxn_ref[...] + r * (dxh - xn * jnp.mean(dxh * xn, axis=-1, keepdims=True))

    row = pl.BlockSpec((T, D), lambda i: (i, 0))
    vec = pl.BlockSpec((1, D), lambda i: (0, 0))
    in_specs = [row] * nk + [pl.BlockSpec((D, D), lambda i, k=k: (0, k)) for k in range(nk)]
    in_specs += [row, pl.BlockSpec((8, D), lambda i: (0, 1)), vec, vec]
    args = list(dzs) + [w_full] * nk + [x, mods, norm_g, dng0]
    vs = jax.ShapeDtypeStruct((1, D), F32)
    out_shape, out_specs = (vs, vs, vs), (vec, vec, vec)
    if has_x:
        in_specs.append(row)
        args.append(dxn)
        out_shape = (jax.ShapeDtypeStruct((lx, D), F32),) + out_shape
        out_specs = (row,) + out_specs
    return pl.pallas_call(
        body, name=name, grid=(n,), out_shape=out_shape, in_specs=in_specs, out_specs=out_specs,
        compiler_params=_cp(1, vmem_mb=56),
    )(*args)


def _tn_matmul(a, bs, extra, name):
    lx, m = a.shape
    tm = min(lx, 1024)
    n = lx // tm
    widths = [b.shape[1] for b in bs]
    nb = len(bs)

    def body(a_ref, *rest):
        b_refs = rest[:nb]
        rest = rest[nb:]
        if extra is not None:
            ea_ref, eb_ref = rest[:2]
            rest = rest[2:]
        out_ref, acc = rest
        i = pl.program_id(0)
        av = a_ref[...]
        off = 0
        for k in range(nb):
            cols = slice(off, off + widths[k])
            part = _dot_tn(av, b_refs[k][...])

            @pl.when(i == 0)
            def _():
                acc[:, cols] = part

            @pl.when(i > 0)
            def _():
                acc[:, cols] += part
            off += widths[k]

        @pl.when(i == n - 1)
        def _():
            if extra is not None:
                acc[:, 0:widths[0]] += _dot_tn(ea_ref[...], eb_ref[...])
            pltpu.sync_copy(acc, out_ref)

    in_specs = [pl.BlockSpec((tm, m), lambda i: (i, 0))] + [pl.BlockSpec((tm, w), lambda i: (i, 0)) for w in widths]
    args = [a] + list(bs)
    if extra is not None:
        in_specs += [VMEM, VMEM]
        args += list(extra)
    return pl.pallas_call(
        body, name=name, grid=(n,), out_shape=jax.ShapeDtypeStruct((m, sum(widths)), F32),
        in_specs=in_specs, out_specs=ANY, scratch_shapes=[pltpu.VMEM((m, sum(widths)), F32)],
        compiler_params=_cp(1, vmem_mb=56),
    )(*args)


def _adam_math(w, g, m, v):
    m = ADAM_B1 * m + (1.0 - ADAM_B1) * g
    v = ADAM_B2 * v + (1.0 - ADAM_B2) * (g * g)
    m_hat = m / (1.0 - ADAM_B1 ** ADAM_STEP)
    v_hat = v / (1.0 - ADAM_B2 ** ADAM_STEP)
    delta = -ADAM_LR * (m_hat / (jnp.sqrt(v_hat) + ADAM_EPS) + ADAM_WD * w)
    return delta, m, v


def _adam_big(w, g, m, v, name):
    rows, cols = w.shape
    tr = 256

    def body(w_ref, g_ref, m_ref, v_ref, d_o, m_o, v_o):
        d, mm, vv = _adam_math(w_ref[...], g_ref[...], m_ref[...], v_ref[...])
        d_o[...] = d
        m_o[...] = mm
        v_o[...] = vv

    blk = pl.BlockSpec((tr, cols), lambda i: (i, 0))
    s = jax.ShapeDtypeStruct((rows, cols), F32)
    return pl.pallas_call(
        body, name=name, grid=(rows // tr,), out_shape=(s, s, s), in_specs=[blk] * 4, out_specs=(blk,) * 3,
        compiler_params=_cp(1, vmem_mb=48),
    )(w, g, m, v)


def _adam_small(items):
    ni = len(items)

    def body(*refs):
        ins, outs = refs[:4 * ni], refs[4 * ni:7 * ni]
        bufs_in, bufs_out = refs[7 * ni:11 * ni], refs[11 * ni:14 * ni]
        sem_in, sem_out = refs[14 * ni], refs[14 * ni + 1]
        loads = [pltpu.make_async_copy(ins[q], bufs_in[q], sem_in.at[q]) for q in range(4 * ni)]
        for cp in loads:
            cp.start()
        stores = []
        for k in range(ni):
            for q in range(4):
                loads[4 * k + q].wait()
            w_b, g_b, m_b, v_b = bufs_in[4 * k:4 * k + 4]
            res = _adam_math(w_b[...], g_b[...], m_b[...], v_b[...])
            for q in range(3):
                bufs_out[3 * k + q][...] = res[q]
                cp = pltpu.make_async_copy(bufs_out[3 * k + q], outs[3 * k + q], sem_out.at[3 * k + q])
                cp.start()
                stores.append(cp)
        for cp in stores:
            cp.wait()

    flat = [a for it in items for a in it]
    out_shape = tuple(jax.ShapeDtypeStruct(it[0].shape, F32) for it in items for _ in range(3))
    scratch = [pltpu.VMEM(a.shape, F32) for a in flat] + [pltpu.VMEM(s.shape, F32) for s in out_shape]
    scratch += [pltpu.SemaphoreType.DMA((4 * ni,)), pltpu.SemaphoreType.DMA((3 * ni,))]
    res = pl.pallas_call(
        body, name="adam_small", out_shape=out_shape, in_specs=[HBM] * (4 * ni), out_specs=(HBM,) * (3 * ni),
        scratch_shapes=scratch, compiler_params=_cp(vmem_mb=40),
    )(*flat)
    return [tuple(res[3 * k:3 * k + 3]) for k in range(ni)]


def kernel(x, c, ctx, c_ctx, ada_w, ada_b, norm_g, w_in, conv_w, conv_b, lru_wa, lru_ba, lru_wx, lru_bx, lru_lambda, sgu_ln_g, sgu_ln_b, sgu_w, sgu_b, w_out, final_g, loss_target, m_c_ctx, m_ada_w, m_ada_b, m_norm_g, m_w_in, m_conv_w, m_conv_b, m_lru_wa, m_lru_ba, m_lru_wx, m_lru_bx, m_lru_lambda, m_sgu_ln_g, m_sgu_ln_b, m_sgu_w, m_sgu_b, m_w_out, m_final_g, v_c_ctx, v_ada_w, v_ada_b, v_norm_g, v_w_in, v_conv_w, v_conv_b, v_lru_wa, v_lru_ba, v_lru_wx, v_lru_bx, v_lru_lambda, v_sgu_ln_g, v_sgu_ln_b, v_sgu_w, v_sgu_b, v_w_out, v_final_g):
    ix, iy, ic = lax.axis_index("x"), lax.axis_index("y"), lax.axis_index("c")
    chip = 2 * ix + iy
    dev = 2 * chip + ic
    lx = x.shape[1]
    lc = ctx.shape[1]

    smalls = jnp.concatenate([conv_w[0], lru_lambda[0], jnp.zeros((10, 256), F32)], axis=0)
    c_ctx2 = c_ctx.reshape(1, D)
    ada_b_j = lax.dynamic_slice(ada_b, (0, 768 * chip), (1, 768))
    mods, c_slots, sm_all, w_in_full, wo_land, ada_land = _gather_in(c, c_ctx2, ada_w[0], ada_b_j, w_in[0], w_out[0],
                                                                     smalls)
    wo_ss, wo_rs, ada_ss, ada_rs, wo_land, ada_land, token = _late_gather_start(wo_land, ada_land)
    mods = mods + token[0:1, 0:1]
    sm3 = sm_all.reshape(NCHIP, 16, 256)
    conv_w_full = sm3[:, 0:4, :].transpose(1, 0, 2).reshape(4, D)
    lam_full = sm3[:, 4:6, :].transpose(1, 0, 2).reshape(2, D)
    conv_wz = conv_w_full.reshape(4, NH, HD)
    conv_bz = conv_b.reshape(NH, HD)
    lamcat = lam_full.reshape(2, NH, HD).transpose(1, 0, 2).reshape(NH, 2 * HD)
    wa, wx, ba, bx = lru_wa[0], lru_wx[0], lru_ba[0], lru_bx[0]
    wcat = jnp.concatenate([wa[0], wx[0], wa[1], wx[1]], axis=-1).astype(BF16)
    bcat = jnp.concatenate([ba[0], bx[0], ba[1], bx[1]], axis=-1)
    sgu_wb = sgu_w[0].astype(BF16)
    sgu_bt = sgu_b[0].T
    final_g2 = final_g.reshape(1, D)

    zero_s = jnp.zeros((NH, HD), F32)
    hn_c, xa_c = _proj(ctx[0], mods, 1, norm_g, w_in_full, 1, "proj_ctx")
    xaz_c, xcz_c, af_c, uf_c, ab_c, ub_c = _lru_gates_fwd(xa_c, conv_wz, conv_bz, wcat, bcat, lamcat, "lru_gates_ctx")
    _, _, pf_c, pb_c, hf0, hb0 = _scan(af_c, uf_c, ab_c, ub_c, zero_s, zero_s, False, "scan_ctx")

    hn, xa, ga, u, v, gb = _proj(x[0], mods, 0, norm_g, w_in_full, 5, "proj")
    xaz, xcz, af, uf, ab, ub = _lru_gates_fwd(xa, conv_wz, conv_bz, wcat, bcat, lamcat, "lru_gates")
    hf, hb, pf, pb, _, _ = _scan(af, uf, ab, ub, hf0, hb0, False, "scan")
    ys = _sgu_fwd(u, v, sgu_ln_g, sgu_ln_b, sgu_wb, sgu_bt)

    w_out_full = _late_gather_wait(wo_land, wo_ss, wo_rs, "w_out", ys, "late_gather_wait_w_out")
    (loss_part, dfg, dgx, dxn, y, do, dga, dgb, dyl_z, dys) = _out_fwd_bwd(
        hf, hb, ga, gb, ys, x[0], loss_target[0], mods, final_g2, w_out_full)
    g_w_out_part = _tn_matmul(y, [do], None, "grad_w_out")

    du, dv, d_sgu_w, d_sgu_b, d_ln_g, d_ln_b = _sgu_bwd(u, v, dys, sgu_ln_g, sgu_ln_b, sgu_wb, sgu_bt)
    lb, lf, dh0b, dh0f = _scan(ab, dyl_z, af, dyl_z, zero_s, zero_s, True, "scan_adj")
    zw = jnp.zeros((NH, HD, 4 * HD), F32)
    zb = jnp.zeros((NH, 4 * HD), F32)
    zl = jnp.zeros((NH, 2 * HD), F32)
    dxc_z, dwc, dbc, dlc = _lru_gates_bwd(xcz, lf, lb, pf, pb, wcat, bcat, lamcat, zw, zb, zl, "lru_gates_bwd")
    dxa, dcw, dcb = _conv_bwd(dxc_z, xaz, conv_wz, jnp.zeros((4, NH, HD), F32), zero_s, "conv_bwd")

    zc = jnp.zeros((lc * NH, HD), F32)
    dhf_c = lax.dynamic_update_slice(zc, dh0f, ((lc - 1) * NH, 0))
    dhb_c = lax.dynamic_update_slice(zc, dh0b, (0, 0))
    lb_c, lf_c, _, _ = _scan(ab_c, dhb_c, af_c, dhf_c, zero_s, zero_s, True, "scan_adj_ctx")
    dxc_zc, dwc, dbc, dlc = _lru_gates_bwd(xcz_c, lf_c, lb_c, pf_c, pb_c, wcat, bcat, lamcat, dwc, dbc, dlc,
                                           "lru_gates_bwd_ctx")
    dxa_c, dcw, dcb = _conv_bwd(dxc_zc, xaz_c, conv_wz, dcw, dcb, "conv_bwd_ctx")

    dzs = [dxa, dga, du, dv, dgb]
    grad_x, dng, dsc_x, dsh_x = _proj_bwd(dzs, x[0], dxn, mods, 0, norm_g, w_in_full, jnp.zeros((1, D), F32), "proj_bwd")
    dng, dsc_c, dsh_c = _proj_bwd([dxa_c], ctx[0], None, mods, 1, norm_g, w_in_full, dng, "proj_bwd_ctx")

    dmx = jnp.concatenate([dsh_x, dsc_x, dgx], axis=0)
    dmc = jnp.concatenate([dsh_c, dsc_c, jnp.zeros((1, D), F32)], axis=0)
    slot = jnp.concatenate([dmx, jnp.zeros((1, D), F32)], axis=0)
    slots = lax.dynamic_update_slice(jnp.zeros((32, D), F32), slot, (4 * dev, 0))
    vecs = jnp.concatenate([dfg, dng, dcb.reshape(1, D), d_ln_g, d_ln_b, dcw.reshape(4, D), dmc,
                            jnp.zeros((4, D), F32), slots], axis=0)
    d_sgu_w4 = d_sgu_w.reshape(4, 256, HD).transpose(1, 0, 2).reshape(256, 4 * HD)
    pad8 = lambda a: jnp.pad(a, ((0, 8 - a.shape[0]), (0, 4 * HD - a.shape[1])))
    pack = jnp.concatenate([dwc.reshape(NH * HD, 4 * HD), pad8(dbc), pad8(dlc), d_sgu_w4, pad8(d_sgu_b),
                            vecs.reshape(96, 4 * HD), jnp.zeros((8, 4 * HD), F32)], axis=0)
    g_w_in, g_w_out, tot = _grads_reduce(hn, dzs, hn_c, dxa_c, g_w_out_part, pack)

    g_wc = tot[0:1024].reshape(NH, HD, 4 * HD)
    g_bc = tot[1024:1032]
    g_lc = tot[1032:1040, 0:2 * HD]
    g_sgu_w = tot[1040:1296].reshape(256, 4, HD).transpose(1, 0, 2).reshape(NH, HD, HD)
    g_sgu_b = tot[1296:1304, 0:HD]
    tv = tot[1304:1400].reshape(48, D)
    g_final_g, g_norm_g, g_conv_b, g_ln_g, g_ln_b = tv[0:1], tv[1:2], tv[2:3], tv[3:4], tv[4:5]
    g_conv_w_full = tv[5:9]
    dmc_tot = tv[9:12].reshape(1, 3 * D)
    slots_all = tv[16:48].reshape(8, 4, D)
    dmx_all = slots_all[:, 0:3, :].reshape(8, 3 * D)
    c_all = c_slots.reshape(8, 8, D)[:, 0, :]
    g_lru_wa = jnp.stack([g_wc[:, :, 0:HD], g_wc[:, :, 2 * HD:3 * HD]])
    g_lru_wx = jnp.stack([g_wc[:, :, HD:2 * HD], g_wc[:, :, 3 * HD:4 * HD]])
    g_lru_ba = jnp.stack([g_bc[:, 0:HD], g_bc[:, 2 * HD:3 * HD]])
    g_lru_bx = jnp.stack([g_bc[:, HD:2 * HD], g_bc[:, 3 * HD:4 * HD]])
    g_lam_full = jnp.stack([g_lc[:, 0:HD], g_lc[:, HD:2 * HD]]).reshape(2, D)
    g_conv_w = lax.dynamic_slice(g_conv_w_full, (0, 256 * chip), (4, 256))
    g_lam = lax.dynamic_slice(g_lam_full, (0, 256 * chip), (2, 256))
    dmx_all_j = lax.dynamic_slice(dmx_all, (0, 768 * chip), (8, 768))
    dmc_j = lax.dynamic_slice(dmc_tot, (0, 768 * chip), (1, 768))
    ada_full = _late_gather_wait(ada_land, ada_ss, ada_rs, "ada_w", tot, "late_gather_wait_ada_w")
    g_ada_w, g_ada_b, g_c_ctx = _ada_bwd(c_all, dmx_all_j, dmc_j, dmx_all, dmc_tot, c_ctx2, ada_full)

    big = {
        "ada_w": _adam_big(ada_w[0], g_ada_w, m_ada_w[0], v_ada_w[0], "adam_ada_w"),
        "w_in": _adam_big(w_in[0], g_w_in, m_w_in[0], v_w_in[0], "adam_w_in"),
        "w_out": _adam_big(w_out[0], g_w_out, m_w_out[0], v_w_out[0], "adam_w_out"),
    }
    small_in = {
        "c_ctx": (c_ctx, g_c_ctx, m_c_ctx, v_c_ctx, (1, D)),
        "ada_b": (ada_b, g_ada_b, m_ada_b, v_ada_b, (1, 3 * D)),
        "norm_g": (norm_g, g_norm_g, m_norm_g, v_norm_g, (1, D)),
        "conv_w": (conv_w, g_conv_w, m_conv_w, v_conv_w, (4, 256)),
        "conv_b": (conv_b, g_conv_b, m_conv_b, v_conv_b, (1, D)),
        "lru_wa": (lru_wa, g_lru_wa, m_lru_wa, v_lru_wa, (2 * NH * HD, HD)),
        "lru_ba": (lru_ba, g_lru_ba, m_lru_ba, v_lru_ba, (2 * NH, HD)),
        "lru_wx": (lru_wx, g_lru_wx, m_lru_wx, v_lru_wx, (2 * NH * HD, HD)),
        "lru_bx": (lru_bx, g_lru_bx, m_lru_bx, v_lru_bx, (2 * NH, HD)),
        "lru_lambda": (lru_lambda, g_lam, m_lru_lambda, v_lru_lambda, (2, 256)),
        "sgu_ln_g": (sgu_ln_g, g_ln_g, m_sgu_ln_g, v_sgu_ln_g, (1, D)),
        "sgu_ln_b": (sgu_ln_b, g_ln_b, m_sgu_ln_b, v_sgu_ln_b, (1, D)),
        "sgu_w": (sgu_w, g_sgu_w, m_sgu_w, v_sgu_w, (NH * HD, HD)),
        "sgu_b": (sgu_b, g_sgu_b, m_sgu_b, v_sgu_b, (NH, HD)),
        "final_g": (final_g, g_final_g, m_final_g, v_final_g, (1, D)),
    }
    names_small = list(small_in)
    res_small = _adam_small([tuple(a.reshape(small_in[k][4]) for a in small_in[k][:4]) for k in names_small])
    full_shapes = {"ada_w": ada_w.shape, "w_in": w_in.shape, "w_out": w_out.shape}
    grads, deltas, new_m, new_v = {}, {}, {}, {}
    for k in ("ada_w", "w_in", "w_out"):
        g = {"ada_w": g_ada_w, "w_in": g_w_in, "w_out": g_w_out}[k]
        grads[k] = g.reshape(full_shapes[k])
        deltas[k], new_m[k], new_v[k] = (a.reshape(full_shapes[k]) for a in big[k])
    for k, res in zip(names_small, res_small):
        shape = small_in[k][0].shape
        grads[k] = small_in[k][1].reshape(shape)
        deltas[k], new_m[k], new_v[k] = (a.reshape(shape) for a in res)

    loss = lax.psum(loss_part[0, 0], ("x", "y", "c"))
    order = ["c_ctx", "ada_w", "ada_b", "norm_g", "w_in", "conv_w", "conv_b", "lru_wa", "lru_ba", "lru_wx", "lru_bx",
             "lru_lambda", "sgu_ln_g", "sgu_ln_b", "sgu_w", "sgu_b", "w_out", "final_g"]
    return (loss, grad_x.reshape(x.shape), *[grads[k] for k in order], *[deltas[k] for k in order],
            *[new_m[k] for k in order], *[new_v[k] for k in order])
```

```python
import functools

import jax
import jax.numpy as jnp
from jax import lax
from jax.experimental import pallas as pl
from jax.experimental.pallas import tpu as pltpu

F32 = jnp.float32
BF16 = jnp.bfloat16

D = 1024
NH = 8
HD = 128
NCHIP = 4
T = 256
NORM_EPS = 1e-6
LN_EPS = 1e-5
LRU_C = 8.0
ADAM_LR = 0.001
ADAM_B1 = 0.9
ADAM_B2 = 0.999
ADAM_EPS = 1e-08
ADAM_WD = 0.01
ADAM_STEP = 10

VMEM = pl.BlockSpec(memory_space=pltpu.VMEM)
ANY = pl.BlockSpec(memory_space=pl.ANY)
MESH = pl.DeviceIdType.MESH


def _cp(n_grid=0, vmem_mb=None):
    kw = {}
    if n_grid:
        kw["dimension_semantics"] = ("arbitrary",) * n_grid
    if vmem_mb:
        kw["vmem_limit_bytes"] = vmem_mb << 20
    return pltpu.CompilerParams(**kw)


def _sigmoid(x):
    return 1.0 / (1.0 + jnp.exp(-x))


def _silu_and_grad(x):
    s = _sigmoid(x)
    return x * s, s * (1.0 + x * (1.0 - s))


_GELU_K = 0.7978845608028654
_GELU_C = 0.044715


def _gelu_and_grad(x):
    x2 = x * x
    th = jnp.tanh(_GELU_K * (x + _GELU_C * x * x2))
    g = 0.5 * x * (1.0 + th)
    dg = 0.5 * (1.0 + th) + 0.5 * x * (1.0 - th * th) * (_GELU_K * (1.0 + 3.0 * _GELU_C * x2))
    return g, dg


def _softplus(x):
    return jnp.maximum(x, 0.0) + jnp.log1p(jnp.exp(-jnp.abs(x)))


def _lru_gate(pre, lam_row, d, off=None):
    off = 256 * d if off is None else off
    r = _sigmoid(pre[:, off:off + HD])
    gi = _sigmoid(pre[:, off + HD:off + 2 * HD])
    lam = lam_row[:, HD * d:HD * d + HD]
    sp = _softplus(-lam)
    la = (-LRU_C) * r * sp
    a = jnp.exp(la)
    x2 = 2.0 * la
    m2 = jnp.where(x2 > -1e-3, -x2 * (1.0 + 0.5 * x2), 1.0 - a * a)
    mult = jnp.sqrt(m2)
    return r, gi, lam, sp, a, mult


def _dot(a, b):
    return jnp.dot(a, b, preferred_element_type=F32)


def _dot_tn(a, b):
    return lax.dot_general(a, b, (((0,), (0,)), ((), ())), preferred_element_type=F32)


def _dot_nt(a, b):
    return lax.dot_general(a, b, (((1,), (1,)), ((), ())), preferred_element_type=F32)


def _mo(v, m):
    return v if isinstance(v, int) else pl.multiple_of(v, m)


def _zrows(h, n):
    return pl.ds(h, n, stride=NH)


def _gather_in(c, c_ctx, ada_w, ada_b_j, w_in, w_out, smalls):
    specs = [
        ((64, 256), F32, lambda r, jj, cc: r.at[pl.ds(_mo(16 * jj + 8 * cc, 8), 8), :]),
        ((D, 5120), BF16, lambda r, jj, cc: r.at[pl.ds(_mo(512 * cc, 16), 512), pl.ds(_mo(1280 * jj, 128), 1280)]),
    ]
    halves = [lambda r, cc: r.at[pl.ds(_mo(8 * cc, 8), 8), :],
              lambda r, cc: r.at[pl.ds(_mo(512 * cc, 16), 512), :]]
    na = len(specs)
    n_sem = 6 * na + 10

    def body(c_ref, cc_ref, ada_ref, adab_ref, win_ref, wout_ref, sm_ref,
             mods_o, call_o, sm_o, win_o, wol_o, adal_o, s_win, s_ada, s_wout, cslot, lhs, mbuf,
             send_sems, recv_sems, local_sems):
        x, y, c = lax.axis_index("x"), lax.axis_index("y"), lax.axis_index("c")
        j = 2 * x + y
        dev = 2 * j + c
        sib = (x, y, 1 - c)
        chips = [(1 - x, y), (x, 1 - y), (1 - x, 1 - y)]
        cj = [2 * cx + cy for cx, cy in chips]
        outs = [sm_o, win_o]
        srcs = [sm_ref, s_win]

        def copy(idx, src, dst, to):
            return pltpu.make_async_remote_copy(src_ref=src, dst_ref=dst, send_sem=send_sems.at[idx],
                                                recv_sem=recv_sems.at[idx], device_id=to, device_id_type=MESH)

        sends = []

        def start(cp):
            cp.start()
            sends.append(cp)

        cslot[...] = jnp.zeros_like(cslot)
        cslot[0:1, :] = c_ref[...]
        my_slot = pl.ds(_mo(8 * dev, 8), 8)
        others = [sib] + [(*chips[k], c) for k in range(3)] + [(*chips[k], 1 - c) for k in range(3)]
        other_dev = [dev + 1 - 2 * c] + [2 * cj[k] + c for k in range(3)] + [2 * cj[k] + 1 - c for k in range(3)]
        base = 6 * na
        for r in range(7):
            start(copy(base + r, cslot, call_o.at[my_slot, :], others[r]))
        call_o[my_slot, :] = cslot[...]

        s_win[...] = win_ref[...].astype(BF16)
        local = []
        for a in range(na):
            for cc in range(2):
                lc = pltpu.make_async_copy(halves[a](srcs[a], cc), specs[a][2](outs[a], j, cc), local_sems.at[2 * a + cc])
                lc.start()
                local.append(lc)
        for k in range(2):
            for a in range(na):
                start(copy(6 * a + k, halves[a](srcs[a], c), specs[a][2](outs[a], j, c), (*chips[k], c)))
        s_wout[...] = wout_ref[...].astype(BF16)
        s_ada[...] = ada_ref[...].astype(BF16)
        for q, (src, dst) in enumerate([(s_wout, wol_o.at[pl.ds(_mo(512 * j, 16), 512), :]),
                                        (s_ada, adal_o.at[:, pl.ds(_mo(768 * j, 128), 768)])]):
            lc = pltpu.make_async_copy(src, dst, local_sems.at[2 * na + q])
            lc.start()
            local.append(lc)

        for r in range(7):
            slot = call_o.at[pl.ds(_mo(8 * other_dev[r], 8), 8), :]
            copy(base + r, slot, slot, sib).wait_recv()
        lhs[...] = jnp.zeros_like(lhs)
        for b in range(8):
            cv = call_o[8 * b:8 * b + 1, :]
            lhs[b:b + 1, :] = cv * _sigmoid(cv)
        cv = cc_ref[...]
        lhs[8:9, :] = cv * _sigmoid(cv)
        mbuf[j] = _dot(lhs[...].astype(BF16), s_ada[...]) + adab_ref[...]
        for k in range(3):
            start(copy(base + 7 + k, mbuf.at[j], mbuf.at[j], (*chips[k], c)))
        for k in range(3):
            copy(base + 7 + k, mbuf.at[cj[k]], mbuf.at[cj[k]], sib).wait_recv()
        mods_o[...] = jnp.zeros_like(mods_o)
        for jj in range(NCHIP):
            mods_o[0:1, 768 * jj:768 * jj + 768] = mbuf[jj, pl.ds(dev, 1), :]
            mods_o[1:2, 768 * jj:768 * jj + 768] = mbuf[jj, 8:9, :]

        kx = [1 - x, x, 1 - x]
        ky = [y, 1 - y, 1 - y]
        pick = lambda k, lst: jnp.where(k == 0, lst[0], jnp.where(k == 1, lst[1], lst[2]))
        for step, k in enumerate([c, 1 - c, 2]):
            for a in range(na):
                reg = specs[a][2](outs[a], pick(k, cj), c)
                copy(6 * a + k, reg, reg, sib).wait_recv()
                if step == 0:
                    start(copy(6 * a + 2, reg, reg, (pick(1 - c, kx), pick(1 - c, ky), c)))
                start(copy(6 * a + 3 + k, reg, reg, sib))
        for k in range(3):
            for a in range(na):
                reg = specs[a][2](outs[a], cj[k], 1 - c)
                copy(6 * a + 3 + k, reg, reg, sib).wait_recv()
        for cp in sends:
            cp.wait_send()
        for lc in local:
            lc.wait()

    out_shape = (jax.ShapeDtypeStruct((8, 3 * D), F32), jax.ShapeDtypeStruct((64, D), F32),
                 jax.ShapeDtypeStruct(specs[0][0], F32), jax.ShapeDtypeStruct(specs[1][0], BF16),
                 jax.ShapeDtypeStruct((2048, D), BF16), jax.ShapeDtypeStruct((D, 3 * D), BF16))
    return pl.pallas_call(
        body, name="gather_in", out_shape=out_shape,
        in_specs=[VMEM] * 7, out_specs=(VMEM, VMEM, VMEM, ANY, ANY, ANY),
        scratch_shapes=[pltpu.VMEM((D, 1280), BF16), pltpu.VMEM((D, 768), BF16), pltpu.VMEM((512, D), BF16),
                        pltpu.VMEM((8, D), F32), pltpu.VMEM((16, D), F32), pltpu.VMEM((NCHIP, 16, 768), F32),
                        pltpu.SemaphoreType.DMA((n_sem,)), pltpu.SemaphoreType.DMA((n_sem,)),
                        pltpu.SemaphoreType.DMA((2 * na + 2,))],
        compiler_params=_cp(vmem_mb=56),
    )(c, c_ctx, ada_w, ada_b_j, w_in, w_out, smalls)


HBM = pl.BlockSpec(memory_space=pltpu.HBM)
SEM = pl.BlockSpec(memory_space=pltpu.SEMAPHORE)


def _late_gather_regions(x, y, c):
    chips = [(1 - x, y), (x, 1 - y), (1 - x, 1 - y)]
    wo_reg = lambda r, jj, cc: r.at[pl.ds(_mo(512 * jj + 256 * cc, 16), 256), :]
    ada_reg = lambda r, jj, cc: r.at[pl.ds(_mo(512 * cc, 16), 512), pl.ds(_mo(768 * jj, 128), 768)]
    return chips, wo_reg, ada_reg


def _late_gather_start(wo_land, ada_land):
    def body(wol_ref, adal_ref, wo_ss, wo_rs, ada_ss, ada_rs, wol_thru, adal_thru, token):
        x, y, c = lax.axis_index("x"), lax.axis_index("y"), lax.axis_index("c")
        j = 2 * x + y
        chips, wo_reg, ada_reg = _late_gather_regions(x, y, c)
        for k in range(3):
            for cc in range(2):
                pltpu.make_async_remote_copy(src_ref=wo_reg(wol_ref, j, c), dst_ref=wo_reg(wol_ref, j, c),
                                             send_sem=wo_ss.at[2 * k + cc], recv_sem=wo_rs.at[2 * k + c],
                                             device_id=(*chips[k], cc), device_id_type=MESH).start()
        for k in range(3):
            for cc in range(2):
                pltpu.make_async_remote_copy(src_ref=ada_reg(adal_ref, j, c), dst_ref=ada_reg(adal_ref, j, c),
                                             send_sem=ada_ss.at[2 * k + cc], recv_sem=ada_rs.at[2 * k + c],
                                             device_id=(*chips[k], cc), device_id_type=MESH).start()
        token[...] = jnp.zeros_like(token)

    sems = pltpu.SemaphoreType.DMA((6,))
    return pl.pallas_call(
        body, name="late_gather_start",
        out_shape=(sems, sems, sems, sems, pltpu.HBM(wo_land.shape, BF16), pltpu.HBM(ada_land.shape, BF16),
                   jax.ShapeDtypeStruct((8, 128), F32)),
        in_specs=(HBM, HBM), out_specs=(SEM, SEM, SEM, SEM, HBM, HBM, VMEM), input_output_aliases={0: 4, 1: 5},
        compiler_params=pltpu.CompilerParams(has_side_effects=pltpu.SideEffectType.DATAFLOW_SIDE_EFFECTING),
    )(pltpu.with_memory_space_constraint(wo_land, pltpu.HBM), pltpu.with_memory_space_constraint(ada_land, pltpu.HBM))


def _late_gather_wait(land, send_sems, recv_sems, which, after, name):
    def body(land_ref, ss, rs, after_ref, land_out):
        x, y, c = lax.axis_index("x"), lax.axis_index("y"), lax.axis_index("c")
        j = 2 * x + y
        chips, wo_reg, ada_reg = _late_gather_regions(x, y, c)
        reg = wo_reg if which == "w_out" else ada_reg
        for k in range(3):
            kj = 2 * chips[k][0] + chips[k][1]
            for cc in range(2):
                cp = pltpu.make_async_remote_copy(src_ref=reg(land_ref, j, c), dst_ref=reg(land_ref, kj, cc),
                                                  send_sem=ss.at[2 * k + cc], recv_sem=rs.at[2 * k + cc],
                                                  device_id=(*chips[k], cc), device_id_type=MESH)
                cp.wait_send()
                cp.wait_recv()

    return pl.pallas_call(
        body, name=name, out_shape=pltpu.HBM(land.shape, land.dtype),
        in_specs=(HBM, SEM, SEM, ANY), out_specs=HBM, input_output_aliases={0: 0},
        compiler_params=pltpu.CompilerParams(has_side_effects=pltpu.SideEffectType.DATAFLOW_SIDE_EFFECTING),
    )(land, send_sems, recv_sems, after)


RCHUNK = 16


def _grads_reduce(hn, dzs, hn_c, dxa_c, g_w_out, pack):
    rp = pack.shape[0]
    hp = rp // 2
    assert hp % RCHUNK == 0
    wi_w = 1280
    lx, lc = hn.shape[0], hn_c.shape[0]
    lt = lx + lc
    n_dz = len(dzs)

    def body(*refs):
        hn_hbm, dz_hbm = refs[0], refs[1:1 + n_dz]
        hnc_hbm, dxac_hbm, wo_hbm, pk_hbm, wi_out, wo_out, pk_out = refs[1 + n_dz:8 + n_dz]
        (hn_mine, hn_other, dzbuf, wi_other, wi_mine, wi_recv, wi_send, wi_rb,
         wo_mine, wo_recv, wo_send, wo_rb, wo_own, pk_mine, pk_recv, pk_send, pk_rb, pk_own,
         send_sems, recv_sems, local_sems) = refs[8 + n_dz:]
        x, y, c = lax.axis_index("x"), lax.axis_index("y"), lax.axis_index("c")
        j = 2 * x + y
        sib = (x, y, 1 - c)
        chips = [(1 - x, y), (x, 1 - y), (1 - x, 1 - y)]
        cj = [2 * cx + cy for cx, cy in chips]
        near = (jnp.where(c == 0, 1 - x, x), jnp.where(c == 0, y, 1 - y), c)
        slabs = [cj[2], cj[0], cj[1], j]

        def copy(k, src, dst, to):
            return pltpu.make_async_remote_copy(src_ref=src, dst_ref=dst, send_sem=send_sems.at[k],
                                                recv_sem=recv_sems.at[k], device_id=to, device_id_type=MESH)

        def local(k, src, dst):
            cp = pltpu.make_async_copy(src, dst, local_sems.at[k])
            cp.start()
            return cp

        rows_half = lambda r, cc, n: r.at[pl.ds(_mo(cc * n, 16), n), :]
        cols_half = lambda r, cc, n: r.at[:, pl.ds(_mo(cc * n, 128), n)]
        pk_piece = lambda r, cc, jj: r.at[pl.ds(_mo(cc * hp, 16), hp), pl.ds(_mo(jj * 128, 128), 128)]

        sends = []

        def start(cp):
            cp.start()
            sends.append(cp)

        def dz_pieces(s):
            g0 = wi_w * s
            k0, off0 = g0 // D, g0 % D
            w0 = min(D - off0, wi_w)
            pieces = [(k0, off0, w0, 0)]
            if w0 < wi_w:
                pieces.append((k0 + 1, 0, wi_w - w0, w0))
            return pieces

        def dz_copies(s):
            cps = []
            for q, (k, off, w, dst) in enumerate(dz_pieces(s)):
                cps.append(pltpu.make_async_copy(dz_hbm[k].at[:, pl.ds(off, w)], dzbuf.at[pl.ds(0, lx), pl.ds(dst, w)],
                                                 local_sems.at[11 + q]))
            if s == 0:
                cps.append(pltpu.make_async_copy(dxac_hbm, dzbuf.at[pl.ds(lx, lc), pl.ds(0, D)], local_sems.at[13]))
            return cps

        def dz_load(sl):
            for s in range(NCHIP):
                @pl.when(sl == s)
                def _():
                    if s == 0:
                        dzbuf[pl.ds(lx, lc), pl.ds(D, wi_w - D)] = jnp.zeros((lc, wi_w - D), BF16)
                    else:
                        dzbuf[pl.ds(lx, lc), :] = jnp.zeros((lc, wi_w), BF16)
                    for cp in dz_copies(s):
                        cp.start()

        def dz_wait(sl):
            for s in range(NCHIP):
                @pl.when(sl == s)
                def _():
                    for cp in dz_copies(s):
                        cp.wait()

        l_pk = local(0, rows_half(pk_hbm, c, hp), pk_mine)
        start(copy(0, rows_half(pk_hbm, 1 - c, hp), pk_recv, sib))
        l_wo = local(1, cols_half(wo_hbm, c, 512), wo_mine)
        start(copy(1, cols_half(wo_hbm, 1 - c, 512), wo_recv, sib))
        hn_loads = [local(2, hn_hbm.at[:, pl.ds(_mo(c * 512, 128), 512)], hn_mine.at[pl.ds(0, lx), :]),
                    local(3, hnc_hbm.at[:, pl.ds(_mo(c * 512, 128), 512)], hn_mine.at[pl.ds(lx, lc), :]),
                    local(4, hn_hbm.at[:, pl.ds(_mo((1 - c) * 512, 128), 512)], hn_other.at[pl.ds(0, lx), :]),
                    local(5, hnc_hbm.at[:, pl.ds(_mo((1 - c) * 512, 128), 512)], hn_other.at[pl.ds(lx, lc), :])]
        dz_load(slabs[0])

        def pair_sum(mine, recv, send, nrows, keep, relayed=None):
            def step(i, carry):
                rows = pl.ds(_mo(i * RCHUNK, RCHUNK), RCHUNK)
                s = mine[rows, :] + recv[rows, :].astype(F32)
                if relayed is not None:
                    s = s + relayed[rows, :].astype(F32)
                if keep:
                    mine[rows, :] = s
                if send is not None:
                    send[rows, :] = s.astype(BF16)
                return carry
            lax.fori_loop(0, nrows // RCHUNK, step, 0)

        def chip_sum(own, rb, nrows, terms=(0, 1, 2)):
            def step(i, carry):
                rows = pl.ds(_mo(i * RCHUNK, RCHUNK), RCHUNK)
                acc = own[rows, :]
                for q in terms:
                    acc = acc + rb[q, rows, :].astype(F32)
                own[rows, :] = acc
                return carry
            lax.fori_loop(0, nrows // RCHUNK, step, 0)

        p1 = [None] * NCHIP

        def slab_matmuls(s):
            if s >= 2:
                p1[s - 2].wait_send()
            dz_wait(slabs[s])
            wi_other[s % 2] = _dot_tn(hn_other[...], dzbuf[...]).astype(BF16)
            p1[s] = copy(2 + s, wi_other.at[s % 2], wi_recv.at[s], sib)
            p1[s].start()
            wi_mine[s % 2] = _dot_tn(hn_mine[...], dzbuf[...])
            if s + 1 < NCHIP:
                dz_load(slabs[s + 1])

        def slab_finish(s):
            copy(2 + s, wi_recv.at[s], wi_recv.at[s], sib).wait_recv()
            if s == 3:
                pair_sum(wi_mine.at[s % 2], wi_recv.at[s], None, 512, True)
                return
            if s == 0:
                pair_sum(wi_mine.at[0], wi_recv.at[0], wi_send.at[0], 512, False)
                start(copy(12, wi_send.at[0], wi_rb.at[0], near))
                return
            adds_relayed = c == (1 if s == 1 else 0)

            @pl.when(adds_relayed)
            def _():
                copy(12, wi_rb.at[0], wi_rb.at[0], sib).wait_recv()
                pair_sum(wi_mine.at[s % 2], wi_recv.at[s], wi_send.at[s], 512, False, wi_rb.at[0])

            @pl.when(jnp.logical_not(adds_relayed))
            def _():
                pair_sum(wi_mine.at[s % 2], wi_recv.at[s], wi_send.at[s], 512, False)
            start(copy(12 + s, wi_send.at[s], wi_rb.at[s], (*chips[s - 1], c)))

        l_wo.wait()
        copy(1, wo_recv, wo_recv, sib).wait_recv()
        pair_sum(wo_mine, wo_recv, wo_send, 2048, True)
        for k in range(3):
            start(copy(9 + k, wo_send.at[pl.ds(_mo(cj[k] * 512, 16), 512), :], wo_rb.at[k], (*chips[k], c)))
        l_wo_own = local(7, wo_mine.at[pl.ds(_mo(j * 512, 16), 512), :], wo_own)

        for cp in hn_loads:
            cp.wait()
        slab_matmuls(0)

        l_pk.wait()
        copy(0, pk_recv, pk_recv, sib).wait_recv()
        pair_sum(pk_mine, pk_recv, pk_send, hp, True)
        for k in range(3):
            start(copy(6 + k, pk_send.at[:, pl.ds(_mo(cj[k] * 128, 128), 128)], pk_rb.at[k], (*chips[k], c)))
        l_pk_own = local(6, pk_mine.at[:, pl.ds(_mo(j * 128, 128), 128)], pk_own)

        slab_matmuls(1)
        slab_finish(0)

        l_pk_own.wait()
        for k in range(3):
            copy(6 + k, pk_rb.at[k], pk_rb.at[k], sib).wait_recv()
        chip_sum(pk_own, pk_rb, hp)
        l_pk_out = local(8, pk_own, pk_piece(pk_out, c, j))
        start(copy(15, pk_own, pk_piece(pk_out, c, j), sib))
        for k in range(3):
            start(copy(16 + k, pk_own, pk_piece(pk_out, c, j), (*chips[k], c)))

        slab_matmuls(2)
        slab_finish(1)
        slab_matmuls(3)
        slab_finish(2)

        l_wo_own.wait()
        for k in range(3):
            copy(9 + k, wo_rb.at[k], wo_rb.at[k], sib).wait_recv()
        chip_sum(wo_own, wo_rb, 512)
        l_wo_out = local(9, wo_own, cols_half(wo_out, c, 512))
        start(copy(22, wo_own, cols_half(wo_out, c, 512), sib))

        for k in range(3):
            reg = pk_piece(pk_out, c, cj[k])
            copy(16 + k, reg, reg, sib).wait_recv()
            start(copy(19 + k, reg, reg, sib))

        slab_finish(3)
        for k in (1, 2):
            copy(12 + k, wi_rb.at[k], wi_rb.at[k], sib).wait_recv()
        chip_sum(wi_mine.at[1], wi_rb, 512, (1, 2))
        l_wi_out = local(10, wi_mine.at[1], rows_half(wi_out, c, 512))
        start(copy(23, wi_mine.at[1], rows_half(wi_out, c, 512), sib))

        reg = pk_piece(pk_out, 1 - c, j)
        copy(15, reg, reg, sib).wait_recv()
        for k in range(3):
            reg = pk_piece(pk_out, 1 - c, cj[k])
            copy(19 + k, reg, reg, sib).wait_recv()
        reg = cols_half(wo_out, 1 - c, 512)
        copy(22, reg, reg, sib).wait_recv()
        reg = rows_half(wi_out, 1 - c, 512)
        copy(23, reg, reg, sib).wait_recv()
        for cp in sends + p1[2:]:
            cp.wait_send()
        for cp in (l_pk_out, l_wo_out, l_wi_out):
            cp.wait()

    return pl.pallas_call(
        body, name="grads_reduce",
        out_shape=(jax.ShapeDtypeStruct((D, wi_w), F32), jax.ShapeDtypeStruct((512, D), F32),
                   jax.ShapeDtypeStruct(pack.shape, F32)),
        in_specs=[ANY] * (5 + n_dz), out_specs=(ANY,) * 3,
        scratch_shapes=[
            pltpu.VMEM((lt, 512), BF16), pltpu.VMEM((lt, 512), BF16), pltpu.VMEM((lt, wi_w), BF16),
            pltpu.VMEM((2, 512, wi_w), BF16), pltpu.VMEM((2, 512, wi_w), F32), pltpu.VMEM((4, 512, wi_w), BF16),
            pltpu.VMEM((3, 512, wi_w), BF16), pltpu.VMEM((3, 512, wi_w), BF16),
            pltpu.VMEM((2048, 512), F32), pltpu.VMEM((2048, 512), F32), pltpu.VMEM((2048, 512), BF16),
            pltpu.VMEM((3, 512, 512), BF16), pltpu.VMEM((512, 512), F32),
            pltpu.VMEM((hp, 512), F32), pltpu.VMEM((hp, 512), F32), pltpu.VMEM((hp, 512), BF16),
            pltpu.VMEM((3, hp, 128), BF16), pltpu.VMEM((hp, 128), F32),
            pltpu.SemaphoreType.DMA((24,)), pltpu.SemaphoreType.DMA((24,)), pltpu.SemaphoreType.DMA((14,))],
        compiler_params=_cp(vmem_mb=56),
    )(hn, *dzs, hn_c, dxa_c, g_w_out, pack)


def _ada_bwd(c_all, dmx_all_j, dmc_j, dmx_all, dmc, c_ctx, ada_w_full):
    def body(c_ref, dmxj_ref, dmcj_ref, dmx_ref, dmc_ref, cc_ref, w_ref, gw_ref, gb_ref, gc_ref, lhs, rhs, dm8):
        lhs[...] = jnp.zeros_like(lhs)
        rhs[...] = jnp.zeros_like(rhs)
        cv = c_ref[...]
        lhs[0:8, :] = cv * _sigmoid(cv)
        cc = cc_ref[...]
        a_c, da_c = _silu_and_grad(cc)
        lhs[8:9, :] = a_c
        rhs[0:8, :] = dmxj_ref[...]
        rhs[8:9, :] = dmcj_ref[...]
        gw_ref[...] = _dot_tn(lhs[...].astype(BF16), rhs[...].astype(BF16))
        gb_ref[...] = jnp.sum(dmx_ref[...], axis=0, keepdims=True) + dmc_ref[...]
        dm8[...] = jnp.zeros_like(dm8)
        dm8[0:1, :] = dmc_ref[...]
        da = _dot_nt(dm8[...].astype(BF16), w_ref[...])
        gc_ref[...] = da[0:1, :] * da_c

    return pl.pallas_call(
        body, name="ada_bwd",
        out_shape=(jax.ShapeDtypeStruct((D, 768), F32), jax.ShapeDtypeStruct((1, 3 * D), F32),
                   jax.ShapeDtypeStruct((1, D), F32)),
        in_specs=[VMEM] * 7, out_specs=(VMEM,) * 3,
        scratch_shapes=[pltpu.VMEM((16, D), F32), pltpu.VMEM((16, 768), F32), pltpu.VMEM((8, 3 * D), F32)],
        compiler_params=_cp(vmem_mb=32),
    )(c_all, dmx_all_j, dmc_j, dmx_all, dmc, c_ctx, ada_w_full)


def _proj(x, mods, mrow, norm_g, w_full, nk, name):
    lx = x.shape[0]
    n = lx // T

    def body(x_ref, sh_ref, sc_ref, ng_ref, *rest):
        w_refs, hn_ref, z_refs = rest[:nk], rest[nk], rest[nk + 1:]
        xv = x_ref[...]
        r = lax.rsqrt(jnp.mean(xv * xv, axis=-1, keepdims=True) + NORM_EPS)
        hn = (xv * r) * ng_ref[...] * (1.0 + sc_ref[mrow:mrow + 1, :]) + sh_ref[mrow:mrow + 1, :]
        hb = hn.astype(BF16)
        hn_ref[...] = hb
        for k in range(nk):
            z_refs[k][...] = _dot(hb, w_refs[k][...])

    row = pl.BlockSpec((T, D), lambda i: (i, 0))
    in_specs = [row, pl.BlockSpec((8, D), lambda i: (0, 0)), pl.BlockSpec((8, D), lambda i: (0, 1)),
                pl.BlockSpec((1, D), lambda i: (0, 0))]
    in_specs += [pl.BlockSpec((D, D), lambda i, k=k: (0, k)) for k in range(nk)]
    out_shape = (jax.ShapeDtypeStruct((lx, D), BF16),) + tuple(jax.ShapeDtypeStruct((lx, D), F32) for _ in range(nk))
    return pl.pallas_call(
        body, name=name, grid=(n,), out_shape=out_shape, in_specs=in_specs, out_specs=(row,) * (nk + 1),
        compiler_params=_cp(1, vmem_mb=56),
    )(x, mods, mods, norm_g, *([w_full] * nk))


def _halo_specs(lx):
    last = lx // 8 - 1
    return [pl.BlockSpec((T, D), lambda i: (i, 0)),
            pl.BlockSpec((8, D), lambda i: (jnp.maximum(i * (T // 8) - 1, 0), 0)),
            pl.BlockSpec((8, D), lambda i: (jnp.minimum((i + 1) * (T // 8), last), 0))]


def _zhalo_specs(lx):
    last = lx // 8 - 1
    return [pl.BlockSpec((T * NH, HD), lambda i: (i, 0)),
            pl.BlockSpec((8 * NH, HD), lambda i: (jnp.maximum(i * (T // 8) - 1, 0), 0)),
            pl.BlockSpec((8 * NH, HD), lambda i: (jnp.minimum((i + 1) * (T // 8), last), 0))]


ZT = pl.BlockSpec((T * NH, HD), lambda i: (i, 0))
CONV_CHUNK = 32


def _lru_fwd(xa, conv_wz, conv_bz, wcat, bcat, lamcat, s_f, s_b, name):
    lx = xa.shape[0]
    n = lx // T

    def body(xm_u, xp_u, xn_u, xm_d, xp_d, xn_d, cw, cb, w_ref, b_ref, lam_ref, su0, sd0,
             xaz_o, xcz_o, af_o, ab_o, hf_o, hb_o, pf_o, pb_o, fu, fd, pad, xc_d, x_u, x_d, carry):
        i = pl.program_id(0)

        @pl.when(i == 0)
        def _():
            carry[0] = su0[...]
            carry[1] = sd0[...]

        def conv_gates(xm, xp, xn, tile, d, xc_ref, a_ref, x_ref, xaz_ref):
            pmask = jnp.where(tile == 0, 0.0, 1.0)
            nmask = jnp.where(tile == n - 1, 0.0, 1.0)
            for h in range(NH):
                cols = slice(HD * h, HD * h + HD)
                pad[_zrows(h, 8), :] = xp[:, cols] * pmask
                pad[pl.ds(8 * NH + h, T, stride=NH), :] = xm[:, cols]
                pad[pl.ds((T + 8) * NH + h, 8, stride=NH), :] = xn[:, cols] * nmask
            if xaz_ref is not None:
                xaz_ref[...] = pad[pl.ds(8 * NH, T * NH), :]

            def conv_chunk(ci, c_):
                base = pl.multiple_of(ci * (CONV_CHUNK * NH), CONV_CHUNK * NH)
                acc = None
                for k in range(4):
                    sl = pad[pl.ds(base + (7 + k) * NH, CONV_CHUNK * NH), :].reshape(CONV_CHUNK, NH, HD)
                    term = sl * cw[k][None]
                    acc = term if acc is None else acc + term
                acc = acc + cb[...][None]
                xc_ref[pl.ds(base, CONV_CHUNK * NH), :] = acc.reshape(CONV_CHUNK * NH, HD)
                return c_
            lax.fori_loop(0, T // CONV_CHUNK, conv_chunk, 0)

            for h in range(NH):
                xch = xc_ref[_zrows(h, T), :]
                pre = _dot(xch.astype(BF16), w_ref[h, :, 256 * d:256 * d + 256]) + b_ref[h:h + 1, 256 * d:256 * d + 256]
                _, gi, _, _, a, mult = _lru_gate(pre, lam_ref[h:h + 1, :], d, 0)
                a_ref[_zrows(h, T), :] = a
                x_ref[_zrows(h, T), :] = mult * gi * xch

        conv_gates(xm_u, xp_u, xn_u, i, 0, xcz_o, af_o, x_u, xaz_o)
        conv_gates(xm_d, xp_d, xn_d, n - 1 - i, 1, xc_d, ab_o, x_d, None)

        def step(k, s):
            su, sd = s
            ru = pl.ds(_mo(k * NH, NH), NH)
            rd = pl.ds(_mo((T - 1 - k) * NH, NH), NH)
            pf_o[ru, :] = su
            pb_o[rd, :] = sd
            vu = af_o[ru, :] * su + x_u[ru, :]
            vd = ab_o[rd, :] * sd + x_d[rd, :]
            hf_o[ru, :] = vu
            hb_o[rd, :] = vd
            return vu, vd

        su, sd = lax.fori_loop(0, T, step, (carry[0], carry[1]), unroll=8)
        carry[0] = su
        carry[1] = sd
        fu[...] = su
        fd[...] = sd

    full = lambda shape: pl.BlockSpec(shape, lambda i: (0,) * len(shape))
    last = lx // 8 - 1
    rev = lambda i: n - 1 - i
    halo_dn = [pl.BlockSpec((T, D), lambda i: (rev(i), 0)),
               pl.BlockSpec((8, D), lambda i: (jnp.maximum(rev(i) * (T // 8) - 1, 0), 0)),
               pl.BlockSpec((8, D), lambda i: (jnp.minimum((rev(i) + 1) * (T // 8), last), 0))]
    st = full((NH, HD))
    in_specs = _halo_specs(lx) + halo_dn + [full((4, NH, HD)), st, full((NH, HD, 4 * HD)), full((NH, 4 * HD)),
                                            full((NH, 2 * HD)), st, st]
    dn = pl.BlockSpec((T * NH, HD), lambda i: (rev(i), 0))
    zs = jax.ShapeDtypeStruct((lx * NH, HD), F32)
    ss = jax.ShapeDtypeStruct((NH, HD), F32)
    zbuf = pltpu.VMEM((T * NH, HD), F32)
    return pl.pallas_call(
        body, name=name, grid=(n,), out_shape=(zs,) * 8 + (ss, ss), in_specs=in_specs,
        out_specs=(ZT, ZT, ZT, dn, ZT, dn, ZT, dn, st, st),
        scratch_shapes=[pltpu.VMEM(((T + 16) * NH, HD), F32), zbuf, zbuf, zbuf, pltpu.VMEM((2, NH, HD), F32)],
        compiler_params=_cp(1, vmem_mb=48),
    )(xa, xa, xa, xa, xa, xa, conv_wz, conv_bz, wcat, bcat, lamcat, s_f, s_b)


def _scan(a_up, x_up, a_dn, x_dn, s_up, s_dn, post, name):
    lx = a_up.shape[0] // NH
    n = lx // T

    def body(au, xu, ad, xd, su0, sd0, *rest):
        if post:
            ou, od, fu, fd, carry = rest
        else:
            ou, od, pu, pd, fu, fd, carry = rest
        i = pl.program_id(0)

        @pl.when(i == 0)
        def _():
            carry[0] = su0[...]
            carry[1] = sd0[...]

        def step(k, s):
            su, sd = s
            ru = pl.ds(_mo(k * NH, NH), NH)
            rd = pl.ds(_mo((T - 1 - k) * NH, NH), NH)
            if post:
                vu = xu[ru, :] + su
                vd = xd[rd, :] + sd
                ou[ru, :] = vu
                od[rd, :] = vd
                return au[ru, :] * vu, ad[rd, :] * vd
            pu[ru, :] = su
            pd[rd, :] = sd
            vu = au[ru, :] * su + xu[ru, :]
            vd = ad[rd, :] * sd + xd[rd, :]
            ou[ru, :] = vu
            od[rd, :] = vd
            return vu, vd

        su, sd = lax.fori_loop(0, T, step, (carry[0], carry[1]), unroll=8)
        carry[0] = su
        carry[1] = sd
        fu[...] = su
        fd[...] = sd

    up = pl.BlockSpec((T * NH, HD), lambda i: (i, 0))
    dn = pl.BlockSpec((T * NH, HD), lambda i: (n - 1 - i, 0))
    st = pl.BlockSpec((NH, HD), lambda i: (0, 0))
    zs = jax.ShapeDtypeStruct((lx * NH, HD), F32)
    ss = jax.ShapeDtypeStruct((NH, HD), F32)
    if post:
        out_shape, out_specs = (zs, zs, ss, ss), (up, dn, st, st)
    else:
        out_shape, out_specs = (zs, zs, zs, zs, ss, ss), (up, dn, up, dn, st, st)
    return pl.pallas_call(
        body, name=name, grid=(n,), out_shape=out_shape, in_specs=[up, up, dn, dn, st, st], out_specs=out_specs,
        scratch_shapes=[pltpu.VMEM((2, NH, HD), F32)],
        compiler_params=_cp(1, vmem_mb=48),
    )(a_up, x_up, a_dn, x_dn, s_up, s_dn)


def _sgu_parts(u, v, lng, lnb, w_ref, bt_ref, mixed_s):
    ug, dug = _gelu_and_grad(u)
    vg, dvg = _gelu_and_grad(v)
    mu = jnp.mean(vg, axis=-1, keepdims=True)
    vc = vg - mu
    rstd = lax.rsqrt(jnp.mean(vc * vc, axis=-1, keepdims=True) + LN_EPS)
    vh = vc * rstd
    vn = (vh * lng + lnb).astype(BF16)
    for g in range(NH):
        cols = slice(HD * g, HD * g + HD)
        mixed_s[:, cols] = _dot(w_ref[g], vn[:, cols]) + bt_ref[:, g:g + 1]
    return ug, dug, dvg, rstd, vh, vn


def _sgu_fwd(u, v, ln_g, ln_b, sgu_w, sgu_bt):
    lx = u.shape[0]
    n = lx // HD

    def body(u_ref, v_ref, g_ref, b_ref, w_ref, bt_ref, y_ref, mixed_s):
        ug, _, _, _, _, _ = _sgu_parts(u_ref[...], v_ref[...], g_ref[...], b_ref[...], w_ref, bt_ref, mixed_s)
        y_ref[...] = ug * mixed_s[...]

    row = pl.BlockSpec((HD, D), lambda i: (i, 0))
    vec = pl.BlockSpec((1, D), lambda i: (0, 0))
    return pl.pallas_call(
        body, name="sgu_fwd", grid=(n,), out_shape=jax.ShapeDtypeStruct((lx, D), F32),
        in_specs=[row, row, vec, vec, pl.BlockSpec((NH, HD, HD), lambda i: (0, 0, 0)),
                  pl.BlockSpec((HD, NH), lambda i: (0, 0))],
        out_specs=row, scratch_shapes=[pltpu.VMEM((HD, D), F32)],
        compiler_params=_cp(1),
    )(u, v, ln_g, ln_b, sgu_w, sgu_bt)


def _sgu_bwd(u, v, dys, ln_g, ln_b, sgu_w, sgu_bt):
    lx = u.shape[0]
    n = lx // HD

    def body(u_ref, v_ref, dy_ref, g_ref, b_ref, w_ref, bt_ref, du_ref, dv_ref, dw_ref, db_ref, dg_ref, dbl_ref,
             mixed_s, dvn_s):
        i = pl.program_id(0)

        @pl.when(i == 0)
        def _():
            dw_ref[...] = jnp.zeros_like(dw_ref)
            db_ref[...] = jnp.zeros_like(db_ref)
            dg_ref[...] = jnp.zeros_like(dg_ref)
            dbl_ref[...] = jnp.zeros_like(dbl_ref)

        lng = g_ref[...]
        ug, dug, dvg, rstd, vh, vn = _sgu_parts(u_ref[...], v_ref[...], lng, b_ref[...], w_ref, bt_ref, mixed_s)
        dys_v = dy_ref[...]
        du_ref[...] = (dys_v * mixed_s[...] * dug).astype(BF16)
        dmix = dys_v * ug
        ones = jnp.ones((8, HD), BF16)
        for g in range(NH):
            cols = slice(HD * g, HD * g + HD)
            dm = dmix[:, cols]
            hi = dm.astype(BF16)
            lo = (dm - hi.astype(F32)).astype(BF16)
            dw_ref[g] += _dot_nt(hi, vn[:, cols])
            db_ref[g:g + 1, :] += (_dot_nt(ones, hi) + _dot_nt(ones, lo))[0:1, :]
            dvn_s[:, cols] = _dot_tn(w_ref[g], hi)
        dvn = dvn_s[...]
        dg_ref[...] += jnp.sum(dvn * vh, axis=0, keepdims=True)
        dbl_ref[...] += jnp.sum(dvn, axis=0, keepdims=True)
        dvh = dvn * lng
        dvg_in = rstd * (dvh - jnp.mean(dvh, axis=-1, keepdims=True)
                         - vh * jnp.mean(dvh * vh, axis=-1, keepdims=True))
        dv_ref[...] = (dvg_in * dvg).astype(BF16)

    row = pl.BlockSpec((HD, D), lambda i: (i, 0))
    vec = pl.BlockSpec((1, D), lambda i: (0, 0))
    wsp = pl.BlockSpec((NH, HD, HD), lambda i: (0, 0, 0))
    bsp = pl.BlockSpec((NH, HD), lambda i: (0, 0))
    return pl.pallas_call(
        body, name="sgu_bwd", grid=(n,),
        out_shape=(jax.ShapeDtypeStruct((lx, D), BF16), jax.ShapeDtypeStruct((lx, D), BF16),
                   jax.ShapeDtypeStruct((NH, HD, HD), F32), jax.ShapeDtypeStruct((NH, HD), F32),
                   jax.ShapeDtypeStruct((1, D), F32), jax.ShapeDtypeStruct((1, D), F32)),
        in_specs=[row, row, row, vec, vec, wsp, pl.BlockSpec((HD, NH), lambda i: (0, 0))],
        out_specs=(row, row, wsp, bsp, vec, vec),
        scratch_shapes=[pltpu.VMEM((HD, D), F32), pltpu.VMEM((HD, D), F32)],
        compiler_params=_cp(1),
    )(u, v, dys, ln_g, ln_b, sgu_w, sgu_bt)


def _out_fwd_bwd(hf_z, hb_z, ga, gb, ys, x, tgt, mods, final_g, w_out_full):
    lx = x.shape[0]
    n = lx // T

    def body(hf_ref, hb_ref, ga_ref, gb_ref, ys_ref, x_ref, t_ref, gx_ref, fg_ref, w_ref,
             loss_ref, dfg_ref, dgx_ref, dxn_ref, y_ref, do_ref, dga_ref, dgb_ref, dyl_ref, dys_ref, yl_s):
        i = pl.program_id(0)

        @pl.when(i == 0)
        def _():
            loss_ref[...] = jnp.zeros_like(loss_ref)
            dfg_ref[...] = jnp.zeros_like(dfg_ref)
            dgx_ref[...] = jnp.zeros_like(dgx_ref)

        for h in range(NH):
            yl_s[:, HD * h:HD * h + HD] = hf_ref[_zrows(h, T), :] + hb_ref[_zrows(h, T), :]
        yl = yl_s[...]
        gav = ga_ref[...]
        gbv = gb_ref[...]
        sa, dsa = _silu_and_grad(gav)
        sb, dsb = _silu_and_grad(gbv)
        ysv = ys_ref[...]
        y_ref[:, 0:D] = (yl * sa).astype(BF16)
        y_ref[:, D:2 * D] = (ysv * sb).astype(BF16)
        o = _dot(y_ref[...], w_ref[...])
        gx = gx_ref[0:1, :]
        xnew = x_ref[...] + gx * o
        r2 = lax.rsqrt(jnp.mean(xnew * xnew, axis=-1, keepdims=True) + NORM_EPS)
        xh = xnew * r2
        fg = fg_ref[...]
        err = xh * fg - t_ref[...]
        loss_ref[...] += 0.5 * jnp.sum(jnp.mean(err * err, axis=-1, keepdims=True), axis=0, keepdims=True)
        dout = err * (1.0 / D)
        dfg_ref[...] += jnp.sum(dout * xh, axis=0, keepdims=True)
        dxh = dout * fg
        dxn = r2 * (dxh - xh * jnp.mean(dxh * xh, axis=-1, keepdims=True))
        dxn_ref[...] = dxn
        dgx_ref[...] += jnp.sum(dxn * o, axis=0, keepdims=True)
        do = (dxn * gx).astype(BF16)
        do_ref[...] = do
        dy = _dot_nt(do, w_ref[...])
        dy1 = dy[:, 0:D]
        dy2 = dy[:, D:2 * D]
        dga_ref[...] = (dy1 * yl * dsa).astype(BF16)
        dgb_ref[...] = (dy2 * ysv * dsb).astype(BF16)
        dys_ref[...] = dy2 * sb
        yl_s[...] = dy1 * sa
        for h in range(NH):
            dyl_ref[_zrows(h, T), :] = yl_s[:, HD * h:HD * h + HD]

    row = pl.BlockSpec((T, D), lambda i: (i, 0))
    vec = pl.BlockSpec((1, D), lambda i: (0, 0))
    in_specs = [ZT, ZT, row, row, row, row, row, pl.BlockSpec((8, D), lambda i: (0, 2)), vec,
                pl.BlockSpec((2 * D, D), lambda i: (0, 0))]
    out_shape = (jax.ShapeDtypeStruct((1, 1), F32), jax.ShapeDtypeStruct((1, D), F32), jax.ShapeDtypeStruct((1, D), F32),
                 jax.ShapeDtypeStruct((lx, D), F32), jax.ShapeDtypeStruct((lx, 2 * D), BF16),
                 jax.ShapeDtypeStruct((lx, D), BF16), jax.ShapeDtypeStruct((lx, D), BF16),
                 jax.ShapeDtypeStruct((lx, D), BF16), jax.ShapeDtypeStruct((lx * NH, HD), F32),
                 jax.ShapeDtypeStruct((lx, D), F32))
    out_specs = (pl.BlockSpec((1, 1), lambda i: (0, 0)), vec, vec, row, pl.BlockSpec((T, 2 * D), lambda i: (i, 0)),
                 row, row, row, ZT, row)
    return pl.pallas_call(
        body, name="out_fwd_bwd", grid=(n,), out_shape=out_shape, in_specs=in_specs, out_specs=out_specs,
        scratch_shapes=[pltpu.VMEM((T, D), F32)],
        compiler_params=_cp(1, vmem_mb=56),
    )(hf_z, hb_z, ga, gb, ys, x, tgt, mods, final_g, w_out_full)


def _lru_gates_bwd(xc_z, lf_z, lb_z, pf_z, pb_z, wcat, bcat, lamcat, dw0, db0, dl0, name):
    lx = xc_z.shape[0] // NH
    n = lx // T

    def body(xc_ref, lf_ref, lb_ref, pf_ref, pb_ref, w_ref, b_ref, lam_ref, dw0_ref, db0_ref, dl0_ref,
             dxc_ref, dw_ref, db_ref, dl_ref, dpre_s):
        i = pl.program_id(0)

        @pl.when(i == 0)
        def _():
            dw_ref[...] = dw0_ref[...]
            db_ref[...] = db0_ref[...]
            dl_ref[...] = dl0_ref[...]

        lam_refs = (lf_ref, lb_ref)
        prev_refs = (pf_ref, pb_ref)
        for h in range(NH):
            xch = xc_ref[_zrows(h, T), :]
            xcb = xch.astype(BF16)
            pre = _dot(xcb, w_ref[h]) + b_ref[h:h + 1, :]
            dxc = jnp.zeros((T, HD), F32)
            for d in range(2):
                r, gi, lam, sp, a, mult = _lru_gate(pre, lam_ref[h:h + 1, :], d)
                du = lam_refs[d][_zrows(h, T), :]
                da = du * prev_refs[d][_zrows(h, T), :]
                dgi = du * mult * xch
                dxc = dxc + du * mult * gi
                dmult = du * gi * xch
                dla = da * a - dmult * (a * a) / mult
                dr = dla * ((-LRU_C) * sp)
                dsp = jnp.sum(dla * ((-LRU_C) * r), axis=0, keepdims=True)
                dl_ref[h:h + 1, HD * d:HD * d + HD] += dsp * (-_sigmoid(-lam))
                dpre_s[:, 256 * d:256 * d + HD] = dr * r * (1.0 - r)
                dpre_s[:, 256 * d + HD:256 * d + 2 * HD] = dgi * gi * (1.0 - gi)
            dpre = dpre_s[...]
            dpb = dpre.astype(BF16)
            dw_ref[h] += _dot_tn(xcb, dpb)
            db_ref[h:h + 1, :] += jnp.sum(dpre, axis=0, keepdims=True)
            dxc_ref[_zrows(h, T), :] = dxc + _dot_nt(dpb, w_ref[h])

    full = lambda shape: pl.BlockSpec(shape, lambda i: (0,) * len(shape))
    wsp, bsp, lsp = full((NH, HD, 4 * HD)), full((NH, 4 * HD)), full((NH, 2 * HD))
    return pl.pallas_call(
        body, name=name, grid=(n,),
        out_shape=(jax.ShapeDtypeStruct((lx * NH, HD), F32), jax.ShapeDtypeStruct((NH, HD, 4 * HD), F32),
                   jax.ShapeDtypeStruct((NH, 4 * HD), F32), jax.ShapeDtypeStruct((NH, 2 * HD), F32)),
        in_specs=[ZT] * 5 + [wsp, bsp, lsp, wsp, bsp, lsp], out_specs=(ZT, wsp, bsp, lsp),
        scratch_shapes=[pltpu.VMEM((T, 4 * HD), F32)],
        compiler_params=_cp(1, vmem_mb=48),
    )(xc_z, lf_z, lb_z, pf_z, pb_z, wcat, bcat, lamcat, dw0, db0, dl0)


def _conv_bwd(dxc_z, xa_z, conv_wz, dcw0, dcb0, name):
    lx = dxc_z.shape[0] // NH
    n = lx // T

    def body(dm, dp, dn, xa_ref, cw, dcw0_ref, dcb0_ref, dxa_ref, dcw_ref, dcb_ref, pad, dxa_s):
        i = pl.program_id(0)

        @pl.when(i == 0)
        def _():
            dcw_ref[...] = dcw0_ref[...]
            dcb_ref[...] = dcb0_ref[...]

        pmask = jnp.where(i == 0, 0.0, 1.0)
        nmask = jnp.where(i == n - 1, 0.0, 1.0)
        pad[pl.ds(0, 8 * NH), :] = dp[...] * pmask
        pad[pl.ds(8 * NH, T * NH), :] = dm[...]
        pad[pl.ds((T + 8) * NH, 8 * NH), :] = dn[...] * nmask

        def chunk(ci, carry):
            base = pl.multiple_of(ci * (CONV_CHUNK * NH), CONV_CHUNK * NH)
            xav = xa_ref[pl.ds(base, CONV_CHUNK * NH), :].reshape(CONV_CHUNK, NH, HD)
            acc = None
            for k in range(4):
                sl = pad[pl.ds(base + (9 - k) * NH, CONV_CHUNK * NH), :].reshape(CONV_CHUNK, NH, HD)
                term = sl * cw[k][None]
                acc = term if acc is None else acc + term
                dcw_ref[k] += jnp.sum(sl * xav, axis=0)
                if k == 1:
                    dcb_ref[...] += jnp.sum(sl, axis=0)
            dxa_s[pl.ds(base, CONV_CHUNK * NH), :] = acc.reshape(CONV_CHUNK * NH, HD)
            return carry
        lax.fori_loop(0, T // CONV_CHUNK, chunk, 0)
        for h in range(NH):
            dxa_ref[:, HD * h:HD * h + HD] = dxa_s[_zrows(h, T), :].astype(BF16)

    full = lambda shape: pl.BlockSpec(shape, lambda i: (0,) * len(shape))
    return pl.pallas_call(
        body, name=name, grid=(n,),
        out_shape=(jax.ShapeDtypeStruct((lx, D), BF16), jax.ShapeDtypeStruct((4, NH, HD), F32),
                   jax.ShapeDtypeStruct((NH, HD), F32)),
        in_specs=_zhalo_specs(lx) + [ZT, full((4, NH, HD)), full((4, NH, HD)), full((NH, HD))],
        out_specs=(pl.BlockSpec((T, D), lambda i: (i, 0)), full((4, NH, HD)), full((NH, HD))),
        scratch_shapes=[pltpu.VMEM(((T + 16) * NH, HD), F32), pltpu.VMEM((T * NH, HD), F32)],
        compiler_params=_cp(1, vmem_mb=48),
    )(dxc_z, dxc_z, dxc_z, xa_z, conv_wz, dcw0, dcb0)


def _proj_bwd(dzs, x, dxn, mods, mrow, norm_g, w_full, dng0, name):
    lx = x.shape[0]
    n = lx // T
    nk = len(dzs)
    has_x = dxn is not None

    def body(*refs):
        dz_refs = refs[:nk]
        w_refs = refs[nk:2 * nk]
        x_ref, sc_ref, ng_ref, dng0_ref = refs[2 * nk:2 * nk + 4]
        rest = refs[2 * nk + 4:]
        if has_x:
            dxn_ref, gx_ref, dng_ref, dsc_ref, dsh_ref = rest
        else:
            dng_ref, dsc_ref, dsh_ref = rest
        i = pl.program_id(0)

        @pl.when(i == 0)
        def _():
            dng_ref[...] = dng0_ref[...]
            dsc_ref[...] = jnp.zeros_like(dsc_ref)
            dsh_ref[...] = jnp.zeros_like(dsh_ref)

        dhn = _dot_nt(dz_refs[0][...], w_refs[0][...])
        for k in range(1, nk):
            dhn = dhn + _dot_nt(dz_refs[k][...], w_refs[k][...])
        xv = x_ref[...]
        r = lax.rsqrt(jnp.mean(xv * xv, axis=-1, keepdims=True) + NORM_EPS)
        xn = xv * r
        ng = ng_ref[...]
        sc1 = 1.0 + sc_ref[mrow:mrow + 1, :]
        t = dhn * xn
        dng_ref[...] += jnp.sum(t * sc1, axis=0, keepdims=True)
        dsc_ref[...] += jnp.sum(t * ng, axis=0, keepdims=True)
        dsh_ref[...] += jnp.sum(dhn, axis=0, keepdims=True)
        if has_x:
            dxh = dhn * (ng * sc1)
            gx_ref[...] = dxn_ref[...] + r * (dxh - xn * jnp.mean(dxh * xn, axis=-1, keepdims=True))

    row = pl.BlockSpec((T, D), lambda i: (i, 0))
    vec = pl.BlockSpec((1, D), lambda i: (0, 0))
    in_specs = [row] * nk + [pl.BlockSpec((D, D), lambda i, k=k: (0, k)) for k in range(nk)]
    in_specs += [row, pl.BlockSpec((8, D), lambda i: (0, 1)), vec, vec]
    args = list(dzs) + [w_full] * nk + [x, mods, norm_g, dng0]
    vs = jax.ShapeDtypeStruct((1, D), F32)
    out_shape, out_specs = (vs, vs, vs), (vec, vec, vec)
    if has_x:
        in_specs.append(row)
        args.append(dxn)
        out_shape = (jax.ShapeDtypeStruct((lx, D), F32),) + out_shape
        out_specs = (row,) + out_specs
    return pl.pallas_call(
        body, name=name, grid=(n,), out_shape=out_shape, in_specs=in_specs, out_specs=out_specs,
        compiler_params=_cp(1, vmem_mb=56),
    )(*args)


def _tn_matmul(a, bs, extra, name):
    lx, m = a.shape
    tm = min(lx, 1024)
    n = lx // tm
    widths = [b.shape[1] for b in bs]
    nb = len(bs)

    def body(a_ref, *rest):
        b_refs = rest[:nb]
        rest = rest[nb:]
        if extra is not None:
            ea_ref, eb_ref = rest[:2]
            rest = rest[2:]
        out_ref, acc = rest
        i = pl.program_id(0)
        av = a_ref[...]
        off = 0
        for k in range(nb):
            cols = slice(off, off + widths[k])
            part = _dot_tn(av, b_refs[k][...])

            @pl.when(i == 0)
            def _():
                acc[:, cols] = part

            @pl.when(i > 0)
            def _():
                acc[:, cols] += part
            off += widths[k]

        @pl.when(i == n - 1)
        def _():
            if extra is not None:
                acc[:, 0:widths[0]] += _dot_tn(ea_ref[...], eb_ref[...])
            pltpu.sync_copy(acc, out_ref)

    in_specs = [pl.BlockSpec((tm, m), lambda i: (i, 0))] + [pl.BlockSpec((tm, w), lambda i: (i, 0)) for w in widths]
    args = [a] + list(bs)
    if extra is not None:
        in_specs += [VMEM, VMEM]
        args += list(extra)
    return pl.pallas_call(
        body, name=name, grid=(n,), out_shape=jax.ShapeDtypeStruct((m, sum(widths)), F32),
        in_specs=in_specs, out_specs=ANY, scratch_shapes=[pltpu.VMEM((m, sum(widths)), F32)],
        compiler_params=_cp(1, vmem_mb=56),
    )(*args)


def _adam_math(w, g, m, v):
    m = ADAM_B1 * m + (1.0 - ADAM_B1) * g
    v = ADAM_B2 * v + (1.0 - ADAM_B2) * (g * g)
    m_hat = m / (1.0 - ADAM_B1 ** ADAM_STEP)
    v_hat = v / (1.0 - ADAM_B2 ** ADAM_STEP)
    delta = -ADAM_LR * (m_hat / (jnp.sqrt(v_hat) + ADAM_EPS) + ADAM_WD * w)
    return delta, m, v


def _adam_big(w, g, m, v, name):
    rows, cols = w.shape
    tr = 256

    def body(w_ref, g_ref, m_ref, v_ref, d_o, m_o, v_o):
        d, mm, vv = _adam_math(w_ref[...], g_ref[...], m_ref[...], v_ref[...])
        d_o[...] = d
        m_o[...] = mm
        v_o[...] = vv

    blk = pl.BlockSpec((tr, cols), lambda i: (i, 0))
    s = jax.ShapeDtypeStruct((rows, cols), F32)
    return pl.pallas_call(
        body, name=name, grid=(rows // tr,), out_shape=(s, s, s), in_specs=[blk] * 4, out_specs=(blk,) * 3,
        compiler_params=_cp(1, vmem_mb=48),
    )(w, g, m, v)


def _adam_small(items):
    ni = len(items)

    def body(*refs):
        ins, outs = refs[:4 * ni], refs[4 * ni:7 * ni]
        bufs_in, bufs_out = refs[7 * ni:11 * ni], refs[11 * ni:14 * ni]
        sem_in, sem_out = refs[14 * ni], refs[14 * ni + 1]
        loads = [pltpu.make_async_copy(ins[q], bufs_in[q], sem_in.at[q]) for q in range(4 * ni)]
        for cp in loads:
            cp.start()
        stores = []
        for k in range(ni):
            for q in range(4):
                loads[4 * k + q].wait()
            w_b, g_b, m_b, v_b = bufs_in[4 * k:4 * k + 4]
            res = _adam_math(w_b[...], g_b[...], m_b[...], v_b[...])
            for q in range(3):
                bufs_out[3 * k + q][...] = res[q]
                cp = pltpu.make_async_copy(bufs_out[3 * k + q], outs[3 * k + q], sem_out.at[3 * k + q])
                cp.start()
                stores.append(cp)
        for cp in stores:
            cp.wait()

    flat = [a for it in items for a in it]
    out_shape = tuple(jax.ShapeDtypeStruct(it[0].shape, F32) for it in items for _ in range(3))
    scratch = [pltpu.VMEM(a.shape, F32) for a in flat] + [pltpu.VMEM(s.shape, F32) for s in out_shape]
    scratch += [pltpu.SemaphoreType.DMA((4 * ni,)), pltpu.SemaphoreType.DMA((3 * ni,))]
    res = pl.pallas_call(
        body, name="adam_small", out_shape=out_shape, in_specs=[HBM] * (4 * ni), out_specs=(HBM,) * (3 * ni),
        scratch_shapes=scratch, compiler_params=_cp(vmem_mb=40),
    )(*flat)
    return [tuple(res[3 * k:3 * k + 3]) for k in range(ni)]


def kernel(x, c, ctx, c_ctx, ada_w, ada_b, norm_g, w_in, conv_w, conv_b, lru_wa, lru_ba, lru_wx, lru_bx, lru_lambda, sgu_ln_g, sgu_ln_b, sgu_w, sgu_b, w_out, final_g, loss_target, m_c_ctx, m_ada_w, m_ada_b, m_norm_g, m_w_in, m_conv_w, m_conv_b, m_lru_wa, m_lru_ba, m_lru_wx, m_lru_bx, m_lru_lambda, m_sgu_ln_g, m_sgu_ln_b, m_sgu_w, m_sgu_b, m_w_out, m_final_g, v_c_ctx, v_ada_w, v_ada_b, v_norm_g, v_w_in, v_conv_w, v_conv_b, v_lru_wa, v_lru_ba, v_lru_wx, v_lru_bx, v_lru_lambda, v_sgu_ln_g, v_sgu_ln_b, v_sgu_w, v_sgu_b, v_w_out, v_final_g):
    ix, iy, ic = lax.axis_index("x"), lax.axis_index("y"), lax.axis_index("c")
    chip = 2 * ix + iy
    dev = 2 * chip + ic
    lx = x.shape[1]
    lc = ctx.shape[1]

    smalls = jnp.concatenate([conv_w[0], lru_lambda[0], jnp.zeros((10, 256), F32)], axis=0)
    c_ctx2 = c_ctx.reshape(1, D)
    ada_b_j = lax.dynamic_slice(ada_b, (0, 768 * chip), (1, 768))
    mods, c_slots, sm_all, w_in_full, wo_land, ada_land = _gather_in(c, c_ctx2, ada_w[0], ada_b_j, w_in[0], w_out[0],
                                                                     smalls)
    wo_ss, wo_rs, ada_ss, ada_rs, wo_land, ada_land, token = _late_gather_start(wo_land, ada_land)
    mods = mods + token[0:1, 0:1]
    sm3 = sm_all.reshape(NCHIP, 16, 256)
    conv_w_full = sm3[:, 0:4, :].transpose(1, 0, 2).reshape(4, D)
    lam_full = sm3[:, 4:6, :].transpose(1, 0, 2).reshape(2, D)
    conv_wz = conv_w_full.reshape(4, NH, HD)
    conv_bz = conv_b.reshape(NH, HD)
    lamcat = lam_full.reshape(2, NH, HD).transpose(1, 0, 2).reshape(NH, 2 * HD)
    wa, wx, ba, bx = lru_wa[0], lru_wx[0], lru_ba[0], lru_bx[0]
    wcat = jnp.concatenate([wa[0], wx[0], wa[1], wx[1]], axis=-1).astype(BF16)
    bcat = jnp.concatenate([ba[0], bx[0], ba[1], bx[1]], axis=-1)
    sgu_wb = sgu_w[0].astype(BF16)
    sgu_bt = sgu_b[0].T
    final_g2 = final_g.reshape(1, D)

    zero_s = jnp.zeros((NH, HD), F32)
    hn_c, xa_c = _proj(ctx[0], mods, 1, norm_g, w_in_full, 1, "proj_ctx")
    xaz_c, xcz_c, af_c, ab_c, _, _, pf_c, pb_c, hf0, hb0 = _lru_fwd(xa_c, conv_wz, conv_bz, wcat, bcat, lamcat,
                                                                     zero_s, zero_s, "lru_fwd_ctx")

    hn, xa, ga, u, v, gb = _proj(x[0], mods, 0, norm_g, w_in_full, 5, "proj")
    xaz, xcz, af, ab, hf, hb, pf, pb, _, _ = _lru_fwd(xa, conv_wz, conv_bz, wcat, bcat, lamcat, hf0, hb0, "lru_fwd")
    ys = _sgu_fwd(u, v, sgu_ln_g, sgu_ln_b, sgu_wb, sgu_bt)

    w_out_full = _late_gather_wait(wo_land, wo_ss, wo_rs, "w_out", ys, "late_gather_wait_w_out")
    (loss_part, dfg, dgx, dxn, y, do, dga, dgb, dyl_z, dys) = _out_fwd_bwd(
        hf, hb, ga, gb, ys, x[0], loss_target[0], mods, final_g2, w_out_full)
    g_w_out_part = _tn_matmul(y, [do], None, "grad_w_out")

    du, dv, d_sgu_w, d_sgu_b, d_ln_g, d_ln_b = _sgu_bwd(u, v, dys, sgu_ln_g, sgu_ln_b, sgu_wb, sgu_bt)
    lb, lf, dh0b, dh0f = _scan(ab, dyl_z, af, dyl_z, zero_s, zero_s, True, "scan_adj")
    zw = jnp.zeros((NH, HD, 4 * HD), F32)
    zb = jnp.zeros((NH, 4 * HD), F32)
    zl = jnp.zeros((NH, 2 * HD), F32)
    dxc_z, dwc, dbc, dlc = _lru_gates_bwd(xcz, lf, lb, pf, pb, wcat, bcat, lamcat, zw, zb, zl, "lru_gates_bwd")
    dxa, dcw, dcb = _conv_bwd(dxc_z, xaz, conv_wz, jnp.zeros((4, NH, HD), F32), zero_s, "conv_bwd")

    zc = jnp.zeros((lc * NH, HD), F32)
    dhf_c = lax.dynamic_update_slice(zc, dh0f, ((lc - 1) * NH, 0))
    dhb_c = lax.dynamic_update_slice(zc, dh0b, (0, 0))
    lb_c, lf_c, _, _ = _scan(ab_c, dhb_c, af_c, dhf_c, zero_s, zero_s, True, "scan_adj_ctx")
    dxc_zc, dwc, dbc, dlc = _lru_gates_bwd(xcz_c, lf_c, lb_c, pf_c, pb_c, wcat, bcat, lamcat, dwc, dbc, dlc,
                                           "lru_gates_bwd_ctx")
    dxa_c, dcw, dcb = _conv_bwd(dxc_zc, xaz_c, conv_wz, dcw, dcb, "conv_bwd_ctx")

    dzs = [dxa, dga, du, dv, dgb]
    grad_x, dng, dsc_x, dsh_x = _proj_bwd(dzs, x[0], dxn, mods, 0, norm_g, w_in_full, jnp.zeros((1, D), F32), "proj_bwd")
    dng, dsc_c, dsh_c = _proj_bwd([dxa_c], ctx[0], None, mods, 1, norm_g, w_in_full, dng, "proj_bwd_ctx")

    dmx = jnp.concatenate([dsh_x, dsc_x, dgx], axis=0)
    dmc = jnp.concatenate([dsh_c, dsc_c, jnp.zeros((1, D), F32)], axis=0)
    lp = loss_part[0, 0]
    lp1 = lax.reduce_precision(lp, 8, 7)
    lp2 = lax.reduce_precision(lp - lp1, 8, 7)
    lp3 = lax.reduce_precision(lp - lp1 - lp2, 8, 7)
    loss_row = jnp.pad(jnp.stack([lp1, lp2, lp3]).reshape(1, 3), ((0, 0), (0, D - 3)))
    slot = jnp.concatenate([dmx, loss_row], axis=0)
    slots = lax.dynamic_update_slice(jnp.zeros((32, D), F32), slot, (4 * dev, 0))
    vecs = jnp.concatenate([dfg, dng, dcb.reshape(1, D), d_ln_g, d_ln_b, dcw.reshape(4, D), dmc,
                            jnp.zeros((4, D), F32), slots], axis=0)
    d_sgu_w4 = d_sgu_w.reshape(4, 256, HD).transpose(1, 0, 2).reshape(256, 4 * HD)
    pad8 = lambda a: jnp.pad(a, ((0, 8 - a.shape[0]), (0, 4 * HD - a.shape[1])))
    pack = jnp.concatenate([dwc.reshape(NH * HD, 4 * HD), pad8(dbc), pad8(dlc), d_sgu_w4, pad8(d_sgu_b),
                            vecs.reshape(96, 4 * HD), jnp.zeros((8, 4 * HD), F32)], axis=0)
    g_w_in, g_w_out, tot = _grads_reduce(hn, dzs, hn_c, dxa_c, g_w_out_part, pack)

    g_wc = tot[0:1024].reshape(NH, HD, 4 * HD)
    g_bc = tot[1024:1032]
    g_lc = tot[1032:1040, 0:2 * HD]
    g_sgu_w = tot[1040:1296].reshape(256, 4, HD).transpose(1, 0, 2).reshape(NH, HD, HD)
    g_sgu_b = tot[1296:1304, 0:HD]
    tv = tot[1304:1400].reshape(48, D)
    g_final_g, g_norm_g, g_conv_b, g_ln_g, g_ln_b = tv[0:1], tv[1:2], tv[2:3], tv[3:4], tv[4:5]
    g_conv_w_full = tv[5:9]
    dmc_tot = tv[9:12].reshape(1, 3 * D)
    slots_all = tv[16:48].reshape(8, 4, D)
    dmx_all = slots_all[:, 0:3, :].reshape(8, 3 * D)
    c_all = c_slots.reshape(8, 8, D)[:, 0, :]
    g_lru_wa = jnp.stack([g_wc[:, :, 0:HD], g_wc[:, :, 2 * HD:3 * HD]])
    g_lru_wx = jnp.stack([g_wc[:, :, HD:2 * HD], g_wc[:, :, 3 * HD:4 * HD]])
    g_lru_ba = jnp.stack([g_bc[:, 0:HD], g_bc[:, 2 * HD:3 * HD]])
    g_lru_bx = jnp.stack([g_bc[:, HD:2 * HD], g_bc[:, 3 * HD:4 * HD]])
    g_lam_full = jnp.stack([g_lc[:, 0:HD], g_lc[:, HD:2 * HD]]).reshape(2, D)
    g_conv_w = lax.dynamic_slice(g_conv_w_full, (0, 256 * chip), (4, 256))
    g_lam = lax.dynamic_slice(g_lam_full, (0, 256 * chip), (2, 256))
    dmx_all_j = lax.dynamic_slice(dmx_all, (0, 768 * chip), (8, 768))
    dmc_j = lax.dynamic_slice(dmc_tot, (0, 768 * chip), (1, 768))
    ada_full = _late_gather_wait(ada_land, ada_ss, ada_rs, "ada_w", tot, "late_gather_wait_ada_w")
    g_ada_w, g_ada_b, g_c_ctx = _ada_bwd(c_all, dmx_all_j, dmc_j, dmx_all, dmc_tot, c_ctx2, ada_full)

    big = {
        "ada_w": _adam_big(ada_w[0], g_ada_w, m_ada_w[0], v_ada_w[0], "adam_ada_w"),
        "w_in": _adam_big(w_in[0], g_w_in, m_w_in[0], v_w_in[0], "adam_w_in"),
        "w_out": _adam_big(w_out[0], g_w_out, m_w_out[0], v_w_out[0], "adam_w_out"),
    }
    small_in = {
        "c_ctx": (c_ctx, g_c_ctx, m_c_ctx, v_c_ctx, (1, D)),
        "ada_b": (ada_b, g_ada_b, m_ada_b, v_ada_b, (1, 3 * D)),
        "norm_g": (norm_g, g_norm_g, m_norm_g, v_norm_g, (1, D)),
        "conv_w": (conv_w, g_conv_w, m_conv_w, v_conv_w, (4, 256)),
        "conv_b": (conv_b, g_conv_b, m_conv_b, v_conv_b, (1, D)),
        "lru_wa": (lru_wa, g_lru_wa, m_lru_wa, v_lru_wa, (2 * NH * HD, HD)),
        "lru_ba": (lru_ba, g_lru_ba, m_lru_ba, v_lru_ba, (2 * NH, HD)),
        "lru_wx": (lru_wx, g_lru_wx, m_lru_wx, v_lru_wx, (2 * NH * HD, HD)),
        "lru_bx": (lru_bx, g_lru_bx, m_lru_bx, v_lru_bx, (2 * NH, HD)),
        "lru_lambda": (lru_lambda, g_lam, m_lru_lambda, v_lru_lambda, (2, 256)),
        "sgu_ln_g": (sgu_ln_g, g_ln_g, m_sgu_ln_g, v_sgu_ln_g, (1, D)),
        "sgu_ln_b": (sgu_ln_b, g_ln_b, m_sgu_ln_b, v_sgu_ln_b, (1, D)),
        "sgu_w": (sgu_w, g_sgu_w, m_sgu_w, v_sgu_w, (NH * HD, HD)),
        "sgu_b": (sgu_b, g_sgu_b, m_sgu_b, v_sgu_b, (NH, HD)),
        "final_g": (final_g, g_final_g, m_final_g, v_final_g, (1, D)),
    }
    names_small = list(small_in)
    res_small = _adam_small([tuple(a.reshape(small_in[k][4]) for a in small_in[k][:4]) for k in names_small])
    full_shapes = {"ada_w": ada_w.shape, "w_in": w_in.shape, "w_out": w_out.shape}
    grads, deltas, new_m, new_v = {}, {}, {}, {}
    for k in ("ada_w", "w_in", "w_out"):
        g = {"ada_w": g_ada_w, "w_in": g_w_in, "w_out": g_w_out}[k]
        grads[k] = g.reshape(full_shapes[k])
        deltas[k], new_m[k], new_v[k] = (a.reshape(full_shapes[k]) for a in big[k])
    for k, res in zip(names_small, res_small):
        shape = small_in[k][0].shape
        grads[k] = small_in[k][1].reshape(shape)
        deltas[k], new_m[k], new_v[k] = (a.reshape(shape) for a in res)

    loss = jnp.sum(slots_all[:, 3, 0:3])
    order = ["c_ctx", "ada_w", "ada_b", "norm_g", "w_in", "conv_w", "conv_b", "lru_wa", "lru_ba", "lru_wx", "lru_bx",
             "lru_lambda", "sgu_ln_g", "sgu_ln_b", "sgu_w", "sgu_b", "w_out", "final_g"]
    return (loss, grad_x.reshape(x.shape), *[grads[k] for k in order], *[deltas[k] for k in order],
            *[new_m[k] for k in order], *[new_v[k] for k in order])
```

```python
import functools

import jax
import jax.numpy as jnp
from jax import lax
from jax.experimental import pallas as pl
from jax.experimental.pallas import tpu as pltpu

F32 = jnp.float32
BF16 = jnp.bfloat16

D = 1024
NH = 8
HD = 128
NCHIP = 4
T = 256
NORM_EPS = 1e-6
LN_EPS = 1e-5
LRU_C = 8.0
ADAM_LR = 0.001
ADAM_B1 = 0.9
ADAM_B2 = 0.999
ADAM_EPS = 1e-08
ADAM_WD = 0.01
ADAM_STEP = 10

VMEM = pl.BlockSpec(memory_space=pltpu.VMEM)
ANY = pl.BlockSpec(memory_space=pl.ANY)
MESH = pl.DeviceIdType.MESH


def _cp(n_grid=0, vmem_mb=None):
    kw = {}
    if n_grid:
        kw["dimension_semantics"] = ("arbitrary",) * n_grid
    if vmem_mb:
        kw["vmem_limit_bytes"] = vmem_mb << 20
    return pltpu.CompilerParams(**kw)


def _sigmoid(x):
    return 1.0 / (1.0 + jnp.exp(-x))


def _silu_and_grad(x):
    s = _sigmoid(x)
    return x * s, s * (1.0 + x * (1.0 - s))


_GELU_K = 0.7978845608028654
_GELU_C = 0.044715


def _gelu_and_grad(x):
    x2 = x * x
    th = jnp.tanh(_GELU_K * (x + _GELU_C * x * x2))
    g = 0.5 * x * (1.0 + th)
    dg = 0.5 * (1.0 + th) + 0.5 * x * (1.0 - th * th) * (_GELU_K * (1.0 + 3.0 * _GELU_C * x2))
    return g, dg


def _softplus(x):
    return jnp.maximum(x, 0.0) + jnp.log1p(jnp.exp(-jnp.abs(x)))


def _lru_gate(pre, lam_row, d, off=None):
    off = 256 * d if off is None else off
    r = _sigmoid(pre[:, off:off + HD])
    gi = _sigmoid(pre[:, off + HD:off + 2 * HD])
    lam = lam_row[:, HD * d:HD * d + HD]
    sp = _softplus(-lam)
    la = (-LRU_C) * r * sp
    a = jnp.exp(la)
    x2 = 2.0 * la
    m2 = jnp.where(x2 > -1e-3, -x2 * (1.0 + 0.5 * x2), 1.0 - a * a)
    mult = jnp.sqrt(m2)
    return r, gi, lam, sp, a, mult


def _dot(a, b):
    return jnp.dot(a, b, preferred_element_type=F32)


def _dot_tn(a, b):
    return lax.dot_general(a, b, (((0,), (0,)), ((), ())), preferred_element_type=F32)


def _dot_nt(a, b):
    return lax.dot_general(a, b, (((1,), (1,)), ((), ())), preferred_element_type=F32)


def _mo(v, m):
    return v if isinstance(v, int) else pl.multiple_of(v, m)


def _zrows(h, n):
    return pl.ds(h, n, stride=NH)


def _gather_in(c, c_ctx, ada_w, ada_b_j, w_in, w_out, smalls):
    specs = [
        ((64, 256), F32, lambda r, jj, cc: r.at[pl.ds(_mo(16 * jj + 8 * cc, 8), 8), :]),
        ((D, 5120), BF16, lambda r, jj, cc: r.at[pl.ds(_mo(512 * cc, 16), 512), pl.ds(_mo(1280 * jj, 128), 1280)]),
    ]
    halves = [lambda r, cc: r.at[pl.ds(_mo(8 * cc, 8), 8), :],
              lambda r, cc: r.at[pl.ds(_mo(512 * cc, 16), 512), :]]
    na = len(specs)
    n_sem = 6 * na + 10

    def body(c_ref, cc_ref, ada_ref, adab_ref, win_ref, wout_ref, sm_ref,
             mods_o, call_o, sm_o, win_o, wol_o, adal_o, s_win, s_ada, s_wout, cslot, lhs, mbuf,
             send_sems, recv_sems, local_sems):
        x, y, c = lax.axis_index("x"), lax.axis_index("y"), lax.axis_index("c")
        j = 2 * x + y
        dev = 2 * j + c
        sib = (x, y, 1 - c)
        chips = [(1 - x, y), (x, 1 - y), (1 - x, 1 - y)]
        cj = [2 * cx + cy for cx, cy in chips]
        outs = [sm_o, win_o]
        srcs = [sm_ref, s_win]

        def copy(idx, src, dst, to):
            return pltpu.make_async_remote_copy(src_ref=src, dst_ref=dst, send_sem=send_sems.at[idx],
                                                recv_sem=recv_sems.at[idx], device_id=to, device_id_type=MESH)

        sends = []

        def start(cp):
            cp.start()
            sends.append(cp)

        cslot[...] = jnp.zeros_like(cslot)
        cslot[0:1, :] = c_ref[...]
        my_slot = pl.ds(_mo(8 * dev, 8), 8)
        others = [sib] + [(*chips[k], c) for k in range(3)] + [(*chips[k], 1 - c) for k in range(3)]
        other_dev = [dev + 1 - 2 * c] + [2 * cj[k] + c for k in range(3)] + [2 * cj[k] + 1 - c for k in range(3)]
        base = 6 * na
        for r in range(7):
            start(copy(base + r, cslot, call_o.at[my_slot, :], others[r]))
        call_o[my_slot, :] = cslot[...]

        s_win[...] = win_ref[...].astype(BF16)
        local = []
        for a in range(na):
            for cc in range(2):
                lc = pltpu.make_async_copy(halves[a](srcs[a], cc), specs[a][2](outs[a], j, cc), local_sems.at[2 * a + cc])
                lc.start()
                local.append(lc)
        for k in range(2):
            for a in range(na):
                start(copy(6 * a + k, halves[a](srcs[a], c), specs[a][2](outs[a], j, c), (*chips[k], c)))
        s_wout[...] = wout_ref[...].astype(BF16)
        s_ada[...] = ada_ref[...].astype(BF16)
        for q, (src, dst) in enumerate([(s_wout, wol_o.at[pl.ds(_mo(512 * j, 16), 512), :]),
                                        (s_ada, adal_o.at[:, pl.ds(_mo(768 * j, 128), 768)])]):
            lc = pltpu.make_async_copy(src, dst, local_sems.at[2 * na + q])
            lc.start()
            local.append(lc)

        for r in range(7):
            slot = call_o.at[pl.ds(_mo(8 * other_dev[r], 8), 8), :]
            copy(base + r, slot, slot, sib).wait_recv()
        lhs[...] = jnp.zeros_like(lhs)
        for b in range(8):
            cv = call_o[8 * b:8 * b + 1, :]
            lhs[b:b + 1, :] = cv * _sigmoid(cv)
        cv = cc_ref[...]
        lhs[8:9, :] = cv * _sigmoid(cv)
        mbuf[j] = _dot(lhs[...].astype(BF16), s_ada[...]) + adab_ref[...]
        for k in range(3):
            start(copy(base + 7 + k, mbuf.at[j], mbuf.at[j], (*chips[k], c)))
        for k in range(3):
            copy(base + 7 + k, mbuf.at[cj[k]], mbuf.at[cj[k]], sib).wait_recv()
        mods_o[...] = jnp.zeros_like(mods_o)
        for jj in range(NCHIP):
            mods_o[0:1, 768 * jj:768 * jj + 768] = mbuf[jj, pl.ds(dev, 1), :]
            mods_o[1:2, 768 * jj:768 * jj + 768] = mbuf[jj, 8:9, :]

        kx = [1 - x, x, 1 - x]
        ky = [y, 1 - y, 1 - y]
        pick = lambda k, lst: jnp.where(k == 0, lst[0], jnp.where(k == 1, lst[1], lst[2]))
        for step, k in enumerate([c, 1 - c, 2]):
            for a in range(na):
                reg = specs[a][2](outs[a], pick(k, cj), c)
                copy(6 * a + k, reg, reg, sib).wait_recv()
                if step == 0:
                    start(copy(6 * a + 2, reg, reg, (pick(1 - c, kx), pick(1 - c, ky), c)))
                start(copy(6 * a + 3 + k, reg, reg, sib))
        for k in range(3):
            for a in range(na):
                reg = specs[a][2](outs[a], cj[k], 1 - c)
                copy(6 * a + 3 + k, reg, reg, sib).wait_recv()
        for cp in sends:
            cp.wait_send()
        for lc in local:
            lc.wait()

    out_shape = (jax.ShapeDtypeStruct((8, 3 * D), F32), jax.ShapeDtypeStruct((64, D), F32),
                 jax.ShapeDtypeStruct(specs[0][0], F32), jax.ShapeDtypeStruct(specs[1][0], BF16),
                 jax.ShapeDtypeStruct((2048, D), BF16), jax.ShapeDtypeStruct((D, 3 * D), BF16))
    return pl.pallas_call(
        body, name="gather_in", out_shape=out_shape,
        in_specs=[VMEM] * 7, out_specs=(VMEM, VMEM, VMEM, ANY, ANY, ANY),
        scratch_shapes=[pltpu.VMEM((D, 1280), BF16), pltpu.VMEM((D, 768), BF16), pltpu.VMEM((512, D), BF16),
                        pltpu.VMEM((8, D), F32), pltpu.VMEM((16, D), F32), pltpu.VMEM((NCHIP, 16, 768), F32),
                        pltpu.SemaphoreType.DMA((n_sem,)), pltpu.SemaphoreType.DMA((n_sem,)),
                        pltpu.SemaphoreType.DMA((2 * na + 2,))],
        compiler_params=_cp(vmem_mb=56),
    )(c, c_ctx, ada_w, ada_b_j, w_in, w_out, smalls)


HBM = pl.BlockSpec(memory_space=pltpu.HBM)
SEM = pl.BlockSpec(memory_space=pltpu.SEMAPHORE)


def _late_gather_regions(x, y, c):
    chips = [(1 - x, y), (x, 1 - y), (1 - x, 1 - y)]
    wo_reg = lambda r, jj, cc: r.at[pl.ds(_mo(512 * jj + 256 * cc, 16), 256), :]
    ada_reg = lambda r, jj, cc: r.at[pl.ds(_mo(512 * cc, 16), 512), pl.ds(_mo(768 * jj, 128), 768)]
    return chips, wo_reg, ada_reg


def _late_gather_start(wo_land, ada_land):
    def body(wol_ref, adal_ref, wo_ss, wo_rs, ada_ss, ada_rs, wol_thru, adal_thru, token):
        x, y, c = lax.axis_index("x"), lax.axis_index("y"), lax.axis_index("c")
        j = 2 * x + y
        chips, wo_reg, ada_reg = _late_gather_regions(x, y, c)
        for k in range(3):
            for cc in range(2):
                pltpu.make_async_remote_copy(src_ref=wo_reg(wol_ref, j, c), dst_ref=wo_reg(wol_ref, j, c),
                                             send_sem=wo_ss.at[2 * k + cc], recv_sem=wo_rs.at[2 * k + c],
                                             device_id=(*chips[k], cc), device_id_type=MESH).start()
        for k in range(3):
            for cc in range(2):
                pltpu.make_async_remote_copy(src_ref=ada_reg(adal_ref, j, c), dst_ref=ada_reg(adal_ref, j, c),
                                             send_sem=ada_ss.at[2 * k + cc], recv_sem=ada_rs.at[2 * k + c],
                                             device_id=(*chips[k], cc), device_id_type=MESH).start()
        token[...] = jnp.zeros_like(token)

    sems = pltpu.SemaphoreType.DMA((6,))
    return pl.pallas_call(
        body, name="late_gather_start",
        out_shape=(sems, sems, sems, sems, pltpu.HBM(wo_land.shape, BF16), pltpu.HBM(ada_land.shape, BF16),
                   jax.ShapeDtypeStruct((8, 128), F32)),
        in_specs=(HBM, HBM), out_specs=(SEM, SEM, SEM, SEM, HBM, HBM, VMEM), input_output_aliases={0: 4, 1: 5},
        compiler_params=pltpu.CompilerParams(has_side_effects=pltpu.SideEffectType.DATAFLOW_SIDE_EFFECTING),
    )(pltpu.with_memory_space_constraint(wo_land, pltpu.HBM), pltpu.with_memory_space_constraint(ada_land, pltpu.HBM))


def _late_gather_wait(land, send_sems, recv_sems, which, after, name):
    def body(land_ref, ss, rs, after_ref, land_out):
        x, y, c = lax.axis_index("x"), lax.axis_index("y"), lax.axis_index("c")
        j = 2 * x + y
        chips, wo_reg, ada_reg = _late_gather_regions(x, y, c)
        reg = wo_reg if which == "w_out" else ada_reg
        for k in range(3):
            kj = 2 * chips[k][0] + chips[k][1]
            for cc in range(2):
                cp = pltpu.make_async_remote_copy(src_ref=reg(land_ref, j, c), dst_ref=reg(land_ref, kj, cc),
                                                  send_sem=ss.at[2 * k + cc], recv_sem=rs.at[2 * k + cc],
                                                  device_id=(*chips[k], cc), device_id_type=MESH)
                cp.wait_send()
                cp.wait_recv()

    return pl.pallas_call(
        body, name=name, out_shape=pltpu.HBM(land.shape, land.dtype),
        in_specs=(HBM, SEM, SEM, ANY), out_specs=HBM, input_output_aliases={0: 0},
        compiler_params=pltpu.CompilerParams(has_side_effects=pltpu.SideEffectType.DATAFLOW_SIDE_EFFECTING),
    )(land, send_sems, recv_sems, after)


RCHUNK = 16


def _grads_reduce(hn, dzs, hn_c, dxa_c, g_w_out, pack):
    rp = pack.shape[0]
    hp = rp // 2
    assert hp % RCHUNK == 0
    wi_w = 1280
    lx, lc = hn.shape[0], hn_c.shape[0]
    lt = lx + lc
    n_dz = len(dzs)

    def body(*refs):
        hn_hbm, dz_hbm = refs[0], refs[1:1 + n_dz]
        hnc_hbm, dxac_hbm, wo_hbm, pk_hbm, wi_out, wo_out, pk_out = refs[1 + n_dz:8 + n_dz]
        (hn_mine, hn_other, dzbuf, wi_other, wi_mine, wi_recv, wi_send, wi_rb,
         wo_mine, wo_recv, wo_send, wo_rb, wo_own, pk_mine, pk_recv, pk_send, pk_rb, pk_own,
         send_sems, recv_sems, local_sems) = refs[8 + n_dz:]
        x, y, c = lax.axis_index("x"), lax.axis_index("y"), lax.axis_index("c")
        j = 2 * x + y
        sib = (x, y, 1 - c)
        chips = [(1 - x, y), (x, 1 - y), (1 - x, 1 - y)]
        cj = [2 * cx + cy for cx, cy in chips]
        near = (jnp.where(c == 0, 1 - x, x), jnp.where(c == 0, y, 1 - y), c)
        slabs = [cj[2], cj[0], cj[1], j]

        def copy(k, src, dst, to):
            return pltpu.make_async_remote_copy(src_ref=src, dst_ref=dst, send_sem=send_sems.at[k],
                                                recv_sem=recv_sems.at[k], device_id=to, device_id_type=MESH)

        def local(k, src, dst):
            cp = pltpu.make_async_copy(src, dst, local_sems.at[k])
            cp.start()
            return cp

        rows_half = lambda r, cc, n: r.at[pl.ds(_mo(cc * n, 16), n), :]
        cols_half = lambda r, cc, n: r.at[:, pl.ds(_mo(cc * n, 128), n)]
        pk_piece = lambda r, cc, jj: r.at[pl.ds(_mo(cc * hp, 16), hp), pl.ds(_mo(jj * 128, 128), 128)]

        sends = []

        def start(cp):
            cp.start()
            sends.append(cp)

        def dz_pieces(s):
            g0 = wi_w * s
            k0, off0 = g0 // D, g0 % D
            w0 = min(D - off0, wi_w)
            pieces = [(k0, off0, w0, 0)]
            if w0 < wi_w:
                pieces.append((k0 + 1, 0, wi_w - w0, w0))
            return pieces

        def dz_copies(s):
            cps = []
            for q, (k, off, w, dst) in enumerate(dz_pieces(s)):
                cps.append(pltpu.make_async_copy(dz_hbm[k].at[:, pl.ds(off, w)], dzbuf.at[pl.ds(0, lx), pl.ds(dst, w)],
                                                 local_sems.at[11 + q]))
            if s == 0:
                cps.append(pltpu.make_async_copy(dxac_hbm, dzbuf.at[pl.ds(lx, lc), pl.ds(0, D)], local_sems.at[13]))
            return cps

        def dz_load(sl):
            for s in range(NCHIP):
                @pl.when(sl == s)
                def _():
                    if s == 0:
                        dzbuf[pl.ds(lx, lc), pl.ds(D, wi_w - D)] = jnp.zeros((lc, wi_w - D), BF16)
                    else:
                        dzbuf[pl.ds(lx, lc), :] = jnp.zeros((lc, wi_w), BF16)
                    for cp in dz_copies(s):
                        cp.start()

        def dz_wait(sl):
            for s in range(NCHIP):
                @pl.when(sl == s)
                def _():
                    for cp in dz_copies(s):
                        cp.wait()

        l_pk = local(0, rows_half(pk_hbm, c, hp), pk_mine)
        start(copy(0, rows_half(pk_hbm, 1 - c, hp), pk_recv, sib))
        l_wo = local(1, cols_half(wo_hbm, c, 512), wo_mine)
        start(copy(1, cols_half(wo_hbm, 1 - c, 512), wo_recv, sib))
        hn_loads = [local(2, hn_hbm.at[:, pl.ds(_mo(c * 512, 128), 512)], hn_mine.at[pl.ds(0, lx), :]),
                    local(3, hnc_hbm.at[:, pl.ds(_mo(c * 512, 128), 512)], hn_mine.at[pl.ds(lx, lc), :]),
                    local(4, hn_hbm.at[:, pl.ds(_mo((1 - c) * 512, 128), 512)], hn_other.at[pl.ds(0, lx), :]),
                    local(5, hnc_hbm.at[:, pl.ds(_mo((1 - c) * 512, 128), 512)], hn_other.at[pl.ds(lx, lc), :])]
        dz_load(slabs[0])

        def pair_sum(mine, recv, send, nrows, keep, relayed=None):
            def step(i, carry):
                rows = pl.ds(_mo(i * RCHUNK, RCHUNK), RCHUNK)
                s = mine[rows, :] + recv[rows, :].astype(F32)
                if relayed is not None:
                    s = s + relayed[rows, :].astype(F32)
                if keep:
                    mine[rows, :] = s
                if send is not None:
                    send[rows, :] = s.astype(BF16)
                return carry
            lax.fori_loop(0, nrows // RCHUNK, step, 0)

        def chip_sum(own, rb, nrows, terms=(0, 1, 2)):
            def step(i, carry):
                rows = pl.ds(_mo(i * RCHUNK, RCHUNK), RCHUNK)
                acc = own[rows, :]
                for q in terms:
                    acc = acc + rb[q, rows, :].astype(F32)
                own[rows, :] = acc
                return carry
            lax.fori_loop(0, nrows // RCHUNK, step, 0)

        p1 = [None] * NCHIP

        def slab_matmuls(s):
            if s >= 2:
                p1[s - 2].wait_send()
            dz_wait(slabs[s])
            wi_other[s % 2] = _dot_tn(hn_other[...], dzbuf[...]).astype(BF16)
            p1[s] = copy(2 + s, wi_other.at[s % 2], wi_recv.at[s], sib)
            p1[s].start()
            wi_mine[s % 2] = _dot_tn(hn_mine[...], dzbuf[...])
            if s + 1 < NCHIP:
                dz_load(slabs[s + 1])

        def slab_finish(s):
            copy(2 + s, wi_recv.at[s], wi_recv.at[s], sib).wait_recv()
            if s == 3:
                pair_sum(wi_mine.at[s % 2], wi_recv.at[s], None, 512, True)
                return
            if s == 0:
                pair_sum(wi_mine.at[0], wi_recv.at[0], wi_send.at[0], 512, False)
                start(copy(12, wi_send.at[0], wi_rb.at[0], near))
                return
            adds_relayed = c == (1 if s == 1 else 0)

            @pl.when(adds_relayed)
            def _():
                copy(12, wi_rb.at[0], wi_rb.at[0], sib).wait_recv()
                pair_sum(wi_mine.at[s % 2], wi_recv.at[s], wi_send.at[s], 512, False, wi_rb.at[0])

            @pl.when(jnp.logical_not(adds_relayed))
            def _():
                pair_sum(wi_mine.at[s % 2], wi_recv.at[s], wi_send.at[s], 512, False)
            start(copy(12 + s, wi_send.at[s], wi_rb.at[s], (*chips[s - 1], c)))

        l_wo.wait()
        copy(1, wo_recv, wo_recv, sib).wait_recv()
        pair_sum(wo_mine, wo_recv, wo_send, 2048, True)
        for k in range(3):
            start(copy(9 + k, wo_send.at[pl.ds(_mo(cj[k] * 512, 16), 512), :], wo_rb.at[k], (*chips[k], c)))
        l_wo_own = local(7, wo_mine.at[pl.ds(_mo(j * 512, 16), 512), :], wo_own)

        for cp in hn_loads:
            cp.wait()
        slab_matmuls(0)

        l_pk.wait()
        copy(0, pk_recv, pk_recv, sib).wait_recv()
        pair_sum(pk_mine, pk_recv, pk_send, hp, True)
        for k in range(3):
            start(copy(6 + k, pk_send.at[:, pl.ds(_mo(cj[k] * 128, 128), 128)], pk_rb.at[k], (*chips[k], c)))
        l_pk_own = local(6, pk_mine.at[:, pl.ds(_mo(j * 128, 128), 128)], pk_own)

        slab_matmuls(1)
        slab_finish(0)

        l_pk_own.wait()
        for k in range(3):
            copy(6 + k, pk_rb.at[k], pk_rb.at[k], sib).wait_recv()
        chip_sum(pk_own, pk_rb, hp)
        l_pk_out = local(8, pk_own, pk_piece(pk_out, c, j))
        start(copy(15, pk_own, pk_piece(pk_out, c, j), sib))
        for k in range(3):
            start(copy(16 + k, pk_own, pk_piece(pk_out, c, j), (*chips[k], c)))

        slab_matmuls(2)
        slab_finish(1)
        slab_matmuls(3)
        slab_finish(2)

        l_wo_own.wait()
        for k in range(3):
            copy(9 + k, wo_rb.at[k], wo_rb.at[k], sib).wait_recv()
        chip_sum(wo_own, wo_rb, 512)
        l_wo_out = local(9, wo_own, cols_half(wo_out, c, 512))
        start(copy(22, wo_own, cols_half(wo_out, c, 512), sib))

        for k in range(3):
            reg = pk_piece(pk_out, c, cj[k])
            copy(16 + k, reg, reg, sib).wait_recv()
            start(copy(19 + k, reg, reg, sib))

        slab_finish(3)
        for k in (1, 2):
            copy(12 + k, wi_rb.at[k], wi_rb.at[k], sib).wait_recv()
        chip_sum(wi_mine.at[1], wi_rb, 512, (1, 2))
        l_wi_out = local(10, wi_mine.at[1], rows_half(wi_out, c, 512))
        start(copy(23, wi_mine.at[1], rows_half(wi_out, c, 512), sib))

        reg = pk_piece(pk_out, 1 - c, j)
        copy(15, reg, reg, sib).wait_recv()
        for k in range(3):
            reg = pk_piece(pk_out, 1 - c, cj[k])
            copy(19 + k, reg, reg, sib).wait_recv()
        reg = cols_half(wo_out, 1 - c, 512)
        copy(22, reg, reg, sib).wait_recv()
        reg = rows_half(wi_out, 1 - c, 512)
        copy(23, reg, reg, sib).wait_recv()
        for cp in sends + p1[2:]:
            cp.wait_send()
        for cp in (l_pk_out, l_wo_out, l_wi_out):
            cp.wait()

    return pl.pallas_call(
        body, name="grads_reduce",
        out_shape=(jax.ShapeDtypeStruct((D, wi_w), F32), jax.ShapeDtypeStruct((512, D), F32),
                   jax.ShapeDtypeStruct(pack.shape, F32)),
        in_specs=[ANY] * (5 + n_dz), out_specs=(ANY,) * 3,
        scratch_shapes=[
            pltpu.VMEM((lt, 512), BF16), pltpu.VMEM((lt, 512), BF16), pltpu.VMEM((lt, wi_w), BF16),
            pltpu.VMEM((2, 512, wi_w), BF16), pltpu.VMEM((2, 512, wi_w), F32), pltpu.VMEM((4, 512, wi_w), BF16),
            pltpu.VMEM((3, 512, wi_w), BF16), pltpu.VMEM((3, 512, wi_w), BF16),
            pltpu.VMEM((2048, 512), F32), pltpu.VMEM((2048, 512), F32), pltpu.VMEM((2048, 512), BF16),
            pltpu.VMEM((3, 512, 512), BF16), pltpu.VMEM((512, 512), F32),
            pltpu.VMEM((hp, 512), F32), pltpu.VMEM((hp, 512), F32), pltpu.VMEM((hp, 512), BF16),
            pltpu.VMEM((3, hp, 128), BF16), pltpu.VMEM((hp, 128), F32),
            pltpu.SemaphoreType.DMA((24,)), pltpu.SemaphoreType.DMA((24,)), pltpu.SemaphoreType.DMA((14,))],
        compiler_params=_cp(vmem_mb=56),
    )(hn, *dzs, hn_c, dxa_c, g_w_out, pack)


def _ada_bwd(c_all, dmx_all_j, dmc_j, dmx_all, dmc, c_ctx, ada_w_full):
    def body(c_ref, dmxj_ref, dmcj_ref, dmx_ref, dmc_ref, cc_ref, w_ref, gw_ref, gb_ref, gc_ref, lhs, rhs, dm8):
        lhs[...] = jnp.zeros_like(lhs)
        rhs[...] = jnp.zeros_like(rhs)
        cv = c_ref[...]
        lhs[0:8, :] = cv * _sigmoid(cv)
        cc = cc_ref[...]
        a_c, da_c = _silu_and_grad(cc)
        lhs[8:9, :] = a_c
        rhs[0:8, :] = dmxj_ref[...]
        rhs[8:9, :] = dmcj_ref[...]
        gw_ref[...] = _dot_tn(lhs[...].astype(BF16), rhs[...].astype(BF16))
        gb_ref[...] = jnp.sum(dmx_ref[...], axis=0, keepdims=True) + dmc_ref[...]
        dm8[...] = jnp.zeros_like(dm8)
        dm8[0:1, :] = dmc_ref[...]
        da = _dot_nt(dm8[...].astype(BF16), w_ref[...])
        gc_ref[...] = da[0:1, :] * da_c

    return pl.pallas_call(
        body, name="ada_bwd",
        out_shape=(jax.ShapeDtypeStruct((D, 768), F32), jax.ShapeDtypeStruct((1, 3 * D), F32),
                   jax.ShapeDtypeStruct((1, D), F32)),
        in_specs=[VMEM] * 7, out_specs=(VMEM,) * 3,
        scratch_shapes=[pltpu.VMEM((16, D), F32), pltpu.VMEM((16, 768), F32), pltpu.VMEM((8, 3 * D), F32)],
        compiler_params=_cp(vmem_mb=32),
    )(c_all, dmx_all_j, dmc_j, dmx_all, dmc, c_ctx, ada_w_full)


def _proj(x, mods, mrow, norm_g, w_full, nk, name):
    lx = x.shape[0]
    n = lx // T

    def body(x_ref, sh_ref, sc_ref, ng_ref, *rest):
        w_refs, hn_ref, z_refs = rest[:nk], rest[nk], rest[nk + 1:]
        xv = x_ref[...]
        r = lax.rsqrt(jnp.mean(xv * xv, axis=-1, keepdims=True) + NORM_EPS)
        hn = (xv * r) * ng_ref[...] * (1.0 + sc_ref[mrow:mrow + 1, :]) + sh_ref[mrow:mrow + 1, :]
        hb = hn.astype(BF16)
        hn_ref[...] = hb
        for k in range(nk):
            z_refs[k][...] = _dot(hb, w_refs[k][...])

    row = pl.BlockSpec((T, D), lambda i: (i, 0))
    in_specs = [row, pl.BlockSpec((8, D), lambda i: (0, 0)), pl.BlockSpec((8, D), lambda i: (0, 1)),
                pl.BlockSpec((1, D), lambda i: (0, 0))]
    in_specs += [pl.BlockSpec((D, D), lambda i, k=k: (0, k)) for k in range(nk)]
    out_shape = (jax.ShapeDtypeStruct((lx, D), BF16),) + tuple(jax.ShapeDtypeStruct((lx, D), F32) for _ in range(nk))
    return pl.pallas_call(
        body, name=name, grid=(n,), out_shape=out_shape, in_specs=in_specs, out_specs=(row,) * (nk + 1),
        compiler_params=_cp(1, vmem_mb=56),
    )(x, mods, mods, norm_g, *([w_full] * nk))


def _halo_specs(lx):
    last = lx // 8 - 1
    return [pl.BlockSpec((T, D), lambda i: (i, 0)),
            pl.BlockSpec((8, D), lambda i: (jnp.maximum(i * (T // 8) - 1, 0), 0)),
            pl.BlockSpec((8, D), lambda i: (jnp.minimum((i + 1) * (T // 8), last), 0))]


def _zhalo_specs(lx):
    last = lx // 8 - 1
    return [pl.BlockSpec((T * NH, HD), lambda i: (i, 0)),
            pl.BlockSpec((8 * NH, HD), lambda i: (jnp.maximum(i * (T // 8) - 1, 0), 0)),
            pl.BlockSpec((8 * NH, HD), lambda i: (jnp.minimum((i + 1) * (T // 8), last), 0))]


ZT = pl.BlockSpec((T * NH, HD), lambda i: (i, 0))
CONV_CHUNK = 32


SCAN_SUB = 4


def _scan_tile(chains, post, carry_ref):
    blk = T // SCAN_SUB

    def step(k, state):
        new = []
        for ci, (a_ref, x_ref, o_ref, q_ref, reverse) in enumerate(chains):
            for q in range(SCAN_SUB):
                s, p = state[ci * SCAN_SUB + q]
                t = (q + 1) * blk - 1 - k if reverse else q * blk + k
                r = pl.ds(_mo(t * NH, NH), NH)
                a = a_ref[r, :]
                if post:
                    o = x_ref[r, :] + s
                    o_ref[r, :] = o
                    q_ref[r, :] = p
                    new.append((a * o, a * p))
                else:
                    o = a * s + x_ref[r, :]
                    p = a * p
                    o_ref[r, :] = o
                    q_ref[r, :] = p
                    new.append((o, p))
        return tuple(new)

    zero = jnp.zeros((NH, HD), F32)
    one = jnp.ones((NH, HD), F32)
    final = lax.fori_loop(0, blk, step, tuple((zero, one) for _ in range(len(chains) * SCAN_SUB)), unroll=2)
    for ci, (a_ref, x_ref, o_ref, q_ref, reverse) in enumerate(chains):
        carry = carry_ref[ci]
        for q in (range(SCAN_SUB - 1, -1, -1) if reverse else range(SCAN_SUB)):
            rows = pl.ds(q * blk * NH, blk * NH)
            fixed = o_ref[rows, :].reshape(blk, NH, HD) + q_ref[rows, :].reshape(blk, NH, HD) * carry[None]
            o_ref[rows, :] = fixed.reshape(blk * NH, HD)
            s_loc, p_loc = final[ci * SCAN_SUB + q]
            carry = s_loc + p_loc * carry
        carry_ref[ci] = carry


def _lru_fwd(xa, conv_wz, conv_bz, wcat, bcat, lamcat, s_f, s_b, name):
    lx = xa.shape[0]
    n = lx // T

    def body(xm_u, xp_u, xn_u, xm_d, xp_d, xn_d, cw, cb, w_ref, b_ref, lam_ref, su0, sd0,
             xaz_o, xcz_o, af_o, ab_o, hf_o, hb_o, fu, fd, pad, xc_d, x_u, x_d, q_u, q_d, carry):
        i = pl.program_id(0)

        @pl.when(i == 0)
        def _():
            carry[0] = su0[...]
            carry[1] = sd0[...]

        def conv_gates(xm, xp, xn, tile, d, xc_ref, a_ref, x_ref, xaz_ref):
            pmask = jnp.where(tile == 0, 0.0, 1.0)
            nmask = jnp.where(tile == n - 1, 0.0, 1.0)
            for h in range(NH):
                cols = slice(HD * h, HD * h + HD)
                pad[_zrows(h, 8), :] = xp[:, cols] * pmask
                pad[pl.ds(8 * NH + h, T, stride=NH), :] = xm[:, cols]
                pad[pl.ds((T + 8) * NH + h, 8, stride=NH), :] = xn[:, cols] * nmask
            if xaz_ref is not None:
                xaz_ref[...] = pad[pl.ds(8 * NH, T * NH), :]

            def conv_chunk(ci, c_):
                base = pl.multiple_of(ci * (CONV_CHUNK * NH), CONV_CHUNK * NH)
                acc = None
                for k in range(4):
                    sl = pad[pl.ds(base + (7 + k) * NH, CONV_CHUNK * NH), :].reshape(CONV_CHUNK, NH, HD)
                    term = sl * cw[k][None]
                    acc = term if acc is None else acc + term
                acc = acc + cb[...][None]
                xc_ref[pl.ds(base, CONV_CHUNK * NH), :] = acc.reshape(CONV_CHUNK * NH, HD)
                return c_
            lax.fori_loop(0, T // CONV_CHUNK, conv_chunk, 0)

            for h in range(NH):
                xch = xc_ref[_zrows(h, T), :]
                pre = _dot(xch.astype(BF16), w_ref[h, :, 256 * d:256 * d + 256]) + b_ref[h:h + 1, 256 * d:256 * d + 256]
                _, gi, _, _, a, mult = _lru_gate(pre, lam_ref[h:h + 1, :], d, 0)
                a_ref[_zrows(h, T), :] = a
                x_ref[_zrows(h, T), :] = mult * gi * xch

        conv_gates(xm_u, xp_u, xn_u, i, 0, xcz_o, af_o, x_u, xaz_o)
        conv_gates(xm_d, xp_d, xn_d, n - 1 - i, 1, xc_d, ab_o, x_d, None)

        _scan_tile([(af_o, x_u, hf_o, q_u, False), (ab_o, x_d, hb_o, q_d, True)], False, carry)
        fu[...] = carry[0]
        fd[...] = carry[1]

    full = lambda shape: pl.BlockSpec(shape, lambda i: (0,) * len(shape))
    last = lx // 8 - 1
    rev = lambda i: n - 1 - i
    halo_dn = [pl.BlockSpec((T, D), lambda i: (rev(i), 0)),
               pl.BlockSpec((8, D), lambda i: (jnp.maximum(rev(i) * (T // 8) - 1, 0), 0)),
               pl.BlockSpec((8, D), lambda i: (jnp.minimum((rev(i) + 1) * (T // 8), last), 0))]
    st = full((NH, HD))
    in_specs = _halo_specs(lx) + halo_dn + [full((4, NH, HD)), st, full((NH, HD, 4 * HD)), full((NH, 4 * HD)),
                                            full((NH, 2 * HD)), st, st]
    dn = pl.BlockSpec((T * NH, HD), lambda i: (rev(i), 0))
    zs = jax.ShapeDtypeStruct((lx * NH, HD), F32)
    ss = jax.ShapeDtypeStruct((NH, HD), F32)
    zbuf = pltpu.VMEM((T * NH, HD), F32)
    return pl.pallas_call(
        body, name=name, grid=(n,), out_shape=(zs,) * 6 + (ss, ss), in_specs=in_specs,
        out_specs=(ZT, ZT, ZT, dn, ZT, dn, st, st),
        scratch_shapes=[pltpu.VMEM(((T + 16) * NH, HD), F32), zbuf, zbuf, zbuf, zbuf, zbuf,
                        pltpu.VMEM((2, NH, HD), F32)],
        compiler_params=_cp(1, vmem_mb=48),
    )(xa, xa, xa, xa, xa, xa, conv_wz, conv_bz, wcat, bcat, lamcat, s_f, s_b)


def _scan_adjoint(a_up, x_up, a_dn, x_dn, name):
    lx = a_up.shape[0] // NH
    n = lx // T

    def body(au, xu, ad, xd, ou, od, fu, fd, q_u, q_d, carry):
        @pl.when(pl.program_id(0) == 0)
        def _():
            carry[...] = jnp.zeros_like(carry)

        _scan_tile([(au, xu, ou, q_u, False), (ad, xd, od, q_d, True)], True, carry)
        fu[...] = carry[0]
        fd[...] = carry[1]

    up = pl.BlockSpec((T * NH, HD), lambda i: (i, 0))
    dn = pl.BlockSpec((T * NH, HD), lambda i: (n - 1 - i, 0))
    st = pl.BlockSpec((NH, HD), lambda i: (0, 0))
    zs = jax.ShapeDtypeStruct((lx * NH, HD), F32)
    ss = jax.ShapeDtypeStruct((NH, HD), F32)
    zbuf = pltpu.VMEM((T * NH, HD), F32)
    return pl.pallas_call(
        body, name=name, grid=(n,), out_shape=(zs, zs, ss, ss), in_specs=[up, up, dn, dn],
        out_specs=(up, dn, st, st), scratch_shapes=[zbuf, zbuf, pltpu.VMEM((2, NH, HD), F32)],
        compiler_params=_cp(1, vmem_mb=48),
    )(a_up, x_up, a_dn, x_dn)


def _sgu_parts(u, v, lng, lnb, w_ref, bt_ref, mixed_s):
    ug, dug = _gelu_and_grad(u)
    vg, dvg = _gelu_and_grad(v)
    mu = jnp.mean(vg, axis=-1, keepdims=True)
    vc = vg - mu
    rstd = lax.rsqrt(jnp.mean(vc * vc, axis=-1, keepdims=True) + LN_EPS)
    vh = vc * rstd
    vn = (vh * lng + lnb).astype(BF16)
    for g in range(NH):
        cols = slice(HD * g, HD * g + HD)
        mixed_s[:, cols] = _dot(w_ref[g], vn[:, cols]) + bt_ref[:, g:g + 1]
    return ug, dug, dvg, rstd, vh, vn


def _sgu_fwd(u, v, ln_g, ln_b, sgu_w, sgu_bt):
    lx = u.shape[0]
    n = lx // HD

    def body(u_ref, v_ref, g_ref, b_ref, w_ref, bt_ref, y_ref, mixed_s):
        ug, _, _, _, _, _ = _sgu_parts(u_ref[...], v_ref[...], g_ref[...], b_ref[...], w_ref, bt_ref, mixed_s)
        y_ref[...] = ug * mixed_s[...]

    row = pl.BlockSpec((HD, D), lambda i: (i, 0))
    vec = pl.BlockSpec((1, D), lambda i: (0, 0))
    return pl.pallas_call(
        body, name="sgu_fwd", grid=(n,), out_shape=jax.ShapeDtypeStruct((lx, D), F32),
        in_specs=[row, row, vec, vec, pl.BlockSpec((NH, HD, HD), lambda i: (0, 0, 0)),
                  pl.BlockSpec((HD, NH), lambda i: (0, 0))],
        out_specs=row, scratch_shapes=[pltpu.VMEM((HD, D), F32)],
        compiler_params=_cp(1),
    )(u, v, ln_g, ln_b, sgu_w, sgu_bt)


def _sgu_bwd(u, v, dys, ln_g, ln_b, sgu_w, sgu_bt):
    lx = u.shape[0]
    n = lx // HD

    def body(u_ref, v_ref, dy_ref, g_ref, b_ref, w_ref, bt_ref, du_ref, dv_ref, dw_ref, db_ref, dg_ref, dbl_ref,
             mixed_s, dvn_s):
        i = pl.program_id(0)

        @pl.when(i == 0)
        def _():
            dw_ref[...] = jnp.zeros_like(dw_ref)
            db_ref[...] = jnp.zeros_like(db_ref)
            dg_ref[...] = jnp.zeros_like(dg_ref)
            dbl_ref[...] = jnp.zeros_like(dbl_ref)

        lng = g_ref[...]
        ug, dug, dvg, rstd, vh, vn = _sgu_parts(u_ref[...], v_ref[...], lng, b_ref[...], w_ref, bt_ref, mixed_s)
        dys_v = dy_ref[...]
        du_ref[...] = (dys_v * mixed_s[...] * dug).astype(BF16)
        dmix = dys_v * ug
        ones = jnp.ones((8, HD), BF16)
        for g in range(NH):
            cols = slice(HD * g, HD * g + HD)
            dm = dmix[:, cols]
            hi = dm.astype(BF16)
            lo = (dm - hi.astype(F32)).astype(BF16)
            dw_ref[g] += _dot_nt(hi, vn[:, cols])
            db_ref[g:g + 1, :] += (_dot_nt(ones, hi) + _dot_nt(ones, lo))[0:1, :]
            dvn_s[:, cols] = _dot_tn(w_ref[g], hi)
        dvn = dvn_s[...]
        dg_ref[...] += jnp.sum(dvn * vh, axis=0, keepdims=True)
        dbl_ref[...] += jnp.sum(dvn, axis=0, keepdims=True)
        dvh = dvn * lng
        dvg_in = rstd * (dvh - jnp.mean(dvh, axis=-1, keepdims=True)
                         - vh * jnp.mean(dvh * vh, axis=-1, keepdims=True))
        dv_ref[...] = (dvg_in * dvg).astype(BF16)

    row = pl.BlockSpec((HD, D), lambda i: (i, 0))
    vec = pl.BlockSpec((1, D), lambda i: (0, 0))
    wsp = pl.BlockSpec((NH, HD, HD), lambda i: (0, 0, 0))
    bsp = pl.BlockSpec((NH, HD), lambda i: (0, 0))
    return pl.pallas_call(
        body, name="sgu_bwd", grid=(n,),
        out_shape=(jax.ShapeDtypeStruct((lx, D), BF16), jax.ShapeDtypeStruct((lx, D), BF16),
                   jax.ShapeDtypeStruct((NH, HD, HD), F32), jax.ShapeDtypeStruct((NH, HD), F32),
                   jax.ShapeDtypeStruct((1, D), F32), jax.ShapeDtypeStruct((1, D), F32)),
        in_specs=[row, row, row, vec, vec, wsp, pl.BlockSpec((HD, NH), lambda i: (0, 0))],
        out_specs=(row, row, wsp, bsp, vec, vec),
        scratch_shapes=[pltpu.VMEM((HD, D), F32), pltpu.VMEM((HD, D), F32)],
        compiler_params=_cp(1),
    )(u, v, dys, ln_g, ln_b, sgu_w, sgu_bt)


def _out_fwd_bwd(hf_z, hb_z, ga, gb, ys, x, tgt, mods, final_g, w_out_full):
    lx = x.shape[0]
    n = lx // T

    def body(hf_ref, hb_ref, ga_ref, gb_ref, ys_ref, x_ref, t_ref, gx_ref, fg_ref, w_ref,
             loss_ref, dfg_ref, dgx_ref, dxn_ref, y_ref, do_ref, dga_ref, dgb_ref, dyl_ref, dys_ref, yl_s):
        i = pl.program_id(0)

        @pl.when(i == 0)
        def _():
            loss_ref[...] = jnp.zeros_like(loss_ref)
            dfg_ref[...] = jnp.zeros_like(dfg_ref)
            dgx_ref[...] = jnp.zeros_like(dgx_ref)

        for h in range(NH):
            yl_s[:, HD * h:HD * h + HD] = hf_ref[_zrows(h, T), :] + hb_ref[_zrows(h, T), :]
        yl = yl_s[...]
        gav = ga_ref[...]
        gbv = gb_ref[...]
        sa, dsa = _silu_and_grad(gav)
        sb, dsb = _silu_and_grad(gbv)
        ysv = ys_ref[...]
        y_ref[:, 0:D] = (yl * sa).astype(BF16)
        y_ref[:, D:2 * D] = (ysv * sb).astype(BF16)
        o = _dot(y_ref[...], w_ref[...])
        gx = gx_ref[0:1, :]
        xnew = x_ref[...] + gx * o
        r2 = lax.rsqrt(jnp.mean(xnew * xnew, axis=-1, keepdims=True) + NORM_EPS)
        xh = xnew * r2
        fg = fg_ref[...]
        err = xh * fg - t_ref[...]
        loss_ref[...] += 0.5 * jnp.sum(jnp.mean(err * err, axis=-1, keepdims=True), axis=0, keepdims=True)
        dout = err * (1.0 / D)
        dfg_ref[...] += jnp.sum(dout * xh, axis=0, keepdims=True)
        dxh = dout * fg
        dxn = r2 * (dxh - xh * jnp.mean(dxh * xh, axis=-1, keepdims=True))
        dxn_ref[...] = dxn
        dgx_ref[...] += jnp.sum(dxn * o, axis=0, keepdims=True)
        do = (dxn * gx).astype(BF16)
        do_ref[...] = do
        dy = _dot_nt(do, w_ref[...])
        dy1 = dy[:, 0:D]
        dy2 = dy[:, D:2 * D]
        dga_ref[...] = (dy1 * yl * dsa).astype(BF16)
        dgb_ref[...] = (dy2 * ysv * dsb).astype(BF16)
        dys_ref[...] = dy2 * sb
        yl_s[...] = dy1 * sa
        for h in range(NH):
            dyl_ref[_zrows(h, T), :] = yl_s[:, HD * h:HD * h + HD]

    row = pl.BlockSpec((T, D), lambda i: (i, 0))
    vec = pl.BlockSpec((1, D), lambda i: (0, 0))
    in_specs = [ZT, ZT, row, row, row, row, row, pl.BlockSpec((8, D), lambda i: (0, 2)), vec,
                pl.BlockSpec((2 * D, D), lambda i: (0, 0))]
    out_shape = (jax.ShapeDtypeStruct((1, 1), F32), jax.ShapeDtypeStruct((1, D), F32), jax.ShapeDtypeStruct((1, D), F32),
                 jax.ShapeDtypeStruct((lx, D), F32), jax.ShapeDtypeStruct((lx, 2 * D), BF16),
                 jax.ShapeDtypeStruct((lx, D), BF16), jax.ShapeDtypeStruct((lx, D), BF16),
                 jax.ShapeDtypeStruct((lx, D), BF16), jax.ShapeDtypeStruct((lx * NH, HD), F32),
                 jax.ShapeDtypeStruct((lx, D), F32))
    out_specs = (pl.BlockSpec((1, 1), lambda i: (0, 0)), vec, vec, row, pl.BlockSpec((T, 2 * D), lambda i: (i, 0)),
                 row, row, row, ZT, row)
    return pl.pallas_call(
        body, name="out_fwd_bwd", grid=(n,), out_shape=out_shape, in_specs=in_specs, out_specs=out_specs,
        scratch_shapes=[pltpu.VMEM((T, D), F32)],
        compiler_params=_cp(1, vmem_mb=56),
    )(hf_z, hb_z, ga, gb, ys, x, tgt, mods, final_g, w_out_full)


def _lru_gates_bwd(xc_z, lf_z, lb_z, hf_z, hb_z, s_f, s_b, wcat, bcat, lamcat, dw0, db0, dl0, name):
    lx = xc_z.shape[0] // NH
    n = lx // T

    def body(xc_ref, lf_ref, lb_ref, hf_ref, hfp_ref, hb_ref, hbn_ref, sf_ref, sb_ref, w_ref, b_ref, lam_ref,
             dw0_ref, db0_ref, dl0_ref, dxc_ref, dw_ref, db_ref, dl_ref, dpre_s, pf_s, pb_s):
        i = pl.program_id(0)

        @pl.when(i == 0)
        def _():
            dw_ref[...] = dw0_ref[...]
            db_ref[...] = db0_ref[...]
            dl_ref[...] = dl0_ref[...]

        pf_s[pl.ds(0, NH), :] = jnp.where(i == 0, sf_ref[...], hfp_ref[pl.ds(7 * NH, NH), :])
        pf_s[pl.ds(NH, T * NH), :] = hf_ref[...]
        pb_s[pl.ds(0, T * NH), :] = hb_ref[...]
        pb_s[pl.ds(T * NH, NH), :] = jnp.where(i == n - 1, sb_ref[...], hbn_ref[pl.ds(0, NH), :])
        lam_refs = (lf_ref, lb_ref)
        prev = ((pf_s, 0), (pb_s, NH))
        for h in range(NH):
            xch = xc_ref[_zrows(h, T), :]
            xcb = xch.astype(BF16)
            pre = _dot(xcb, w_ref[h]) + b_ref[h:h + 1, :]
            dxc = jnp.zeros((T, HD), F32)
            for d in range(2):
                r, gi, lam, sp, a, mult = _lru_gate(pre, lam_ref[h:h + 1, :], d)
                du = lam_refs[d][_zrows(h, T), :]
                da = du * prev[d][0][pl.ds(prev[d][1] + h, T, stride=NH), :]
                dgi = du * mult * xch
                dxc = dxc + du * mult * gi
                dmult = du * gi * xch
                dla = da * a - dmult * (a * a) / mult
                dr = dla * ((-LRU_C) * sp)
                dsp = jnp.sum(dla * ((-LRU_C) * r), axis=0, keepdims=True)
                dl_ref[h:h + 1, HD * d:HD * d + HD] += dsp * (-_sigmoid(-lam))
                dpre_s[:, 256 * d:256 * d + HD] = dr * r * (1.0 - r)
                dpre_s[:, 256 * d + HD:256 * d + 2 * HD] = dgi * gi * (1.0 - gi)
            dpre = dpre_s[...]
            dpb = dpre.astype(BF16)
            dw_ref[h] += _dot_tn(xcb, dpb)
            db_ref[h:h + 1, :] += jnp.sum(dpre, axis=0, keepdims=True)
            dxc_ref[_zrows(h, T), :] = dxc + _dot_nt(dpb, w_ref[h])

    full = lambda shape: pl.BlockSpec(shape, lambda i: (0,) * len(shape))
    wsp, bsp, lsp = full((NH, HD, 4 * HD)), full((NH, 4 * HD)), full((NH, 2 * HD))
    st = full((NH, HD))
    zh = _zhalo_specs(lx)
    return pl.pallas_call(
        body, name=name, grid=(n,),
        out_shape=(jax.ShapeDtypeStruct((lx * NH, HD), F32), jax.ShapeDtypeStruct((NH, HD, 4 * HD), F32),
                   jax.ShapeDtypeStruct((NH, 4 * HD), F32), jax.ShapeDtypeStruct((NH, 2 * HD), F32)),
        in_specs=[ZT] * 3 + zh[0:2] + [zh[0], zh[2], st, st, wsp, bsp, lsp, wsp, bsp, lsp], out_specs=(ZT, wsp, bsp, lsp),
        scratch_shapes=[pltpu.VMEM((T, 4 * HD), F32), pltpu.VMEM(((T + 1) * NH, HD), F32),
                        pltpu.VMEM(((T + 1) * NH, HD), F32)],
        compiler_params=_cp(1, vmem_mb=48),
    )(xc_z, lf_z, lb_z, hf_z, hf_z, hb_z, hb_z, s_f, s_b, wcat, bcat, lamcat, dw0, db0, dl0)


def _conv_bwd(dxc_z, xa_z, conv_wz, dcw0, dcb0, name):
    lx = dxc_z.shape[0] // NH
    n = lx // T

    def body(dm, dp, dn, xa_ref, cw, dcw0_ref, dcb0_ref, dxa_ref, dcw_ref, dcb_ref, pad, dxa_s):
        i = pl.program_id(0)

        @pl.when(i == 0)
        def _():
            dcw_ref[...] = dcw0_ref[...]
            dcb_ref[...] = dcb0_ref[...]

        pmask = jnp.where(i == 0, 0.0, 1.0)
        nmask = jnp.where(i == n - 1, 0.0, 1.0)
        pad[pl.ds(0, 8 * NH), :] = dp[...] * pmask
        pad[pl.ds(8 * NH, T * NH), :] = dm[...]
        pad[pl.ds((T + 8) * NH, 8 * NH), :] = dn[...] * nmask

        def chunk(ci, carry):
            base = pl.multiple_of(ci * (CONV_CHUNK * NH), CONV_CHUNK * NH)
            xav = xa_ref[pl.ds(base, CONV_CHUNK * NH), :].reshape(CONV_CHUNK, NH, HD)
            acc = None
            for k in range(4):
                sl = pad[pl.ds(base + (9 - k) * NH, CONV_CHUNK * NH), :].reshape(CONV_CHUNK, NH, HD)
                term = sl * cw[k][None]
                acc = term if acc is None else acc + term
                dcw_ref[k] += jnp.sum(sl * xav, axis=0)
                if k == 1:
                    dcb_ref[...] += jnp.sum(sl, axis=0)
            dxa_s[pl.ds(base, CONV_CHUNK * NH), :] = acc.reshape(CONV_CHUNK * NH, HD)
            return carry
        lax.fori_loop(0, T // CONV_CHUNK, chunk, 0)
        for h in range(NH):
            dxa_ref[:, HD * h:HD * h + HD] = dxa_s[_zrows(h, T), :].astype(BF16)

    full = lambda shape: pl.BlockSpec(shape, lambda i: (0,) * len(shape))
    return pl.pallas_call(
        body, name=name, grid=(n,),
        out_shape=(jax.ShapeDtypeStruct((lx, D), BF16), jax.ShapeDtypeStruct((4, NH, HD), F32),
                   jax.ShapeDtypeStruct((NH, HD), F32)),
        in_specs=_zhalo_specs(lx) + [ZT, full((4, NH, HD)), full((4, NH, HD)), full((NH, HD))],
        out_specs=(pl.BlockSpec((T, D), lambda i: (i, 0)), full((4, NH, HD)), full((NH, HD))),
        scratch_shapes=[pltpu.VMEM(((T + 16) * NH, HD), F32), pltpu.VMEM((T * NH, HD), F32)],
        compiler_params=_cp(1, vmem_mb=48),
    )(dxc_z, dxc_z, dxc_z, xa_z, conv_wz, dcw0, dcb0)


def _proj_bwd(dzs, x, dxn, mods, mrow, norm_g, w_full, dng0, name):
    lx = x.shape[0]
    n = lx // T
    nk = len(dzs)
    has_x = dxn is not None

    def body(*refs):
        dz_refs = refs[:nk]
        w_refs = refs[nk:2 * nk]
        x_ref, sc_ref, ng_ref, dng0_ref = refs[2 * nk:2 * nk + 4]
        rest = refs[2 * nk + 4:]
        if has_x:
            dxn_ref, gx_ref, dng_ref, dsc_ref, dsh_ref = rest
        else:
            dng_ref, dsc_ref, dsh_ref = rest
        i = pl.program_id(0)

        @pl.when(i == 0)
        def _():
            dng_ref[...] = dng0_ref[...]
            dsc_ref[...] = jnp.zeros_like(dsc_ref)
            dsh_ref[...] = jnp.zeros_like(dsh_ref)

        dhn = _dot_nt(dz_refs[0][...], w_refs[0][...])
        for k in range(1, nk):
            dhn = dhn + _dot_nt(dz_refs[k][...], w_refs[k][...])
        xv = x_ref[...]
        r = lax.rsqrt(jnp.mean(xv * xv, axis=-1, keepdims=True) + NORM_EPS)
        xn = xv * r
        ng = ng_ref[...]
        sc1 = 1.0 + sc_ref[mrow:mrow + 1, :]
        t = dhn * xn
        dng_ref[...] += jnp.sum(t * sc1, axis=0, keepdims=True)
        dsc_ref[...] += jnp.sum(t * ng, axis=0, keepdims=True)
        dsh_ref[...] += jnp.sum(dhn, axis=0, keepdims=True)
        if has_x:
            dxh = dhn * (ng * sc1)
            gx_ref[...] = dxn_ref[...] + r * (dxh - xn * jnp.mean(dxh * xn, axis=-1, keepdims=True))

    row = pl.BlockSpec((T, D), lambda i: (i, 0))
    vec = pl.BlockSpec((1, D), lambda i: (0, 0))
    in_specs = [row] * nk + [pl.BlockSpec((D, D), lambda i, k=k: (0, k)) for k in range(nk)]
    in_specs += [row, pl.BlockSpec((8, D), lambda i: (0, 1)), vec, vec]
    args = list(dzs) + [w_full] * nk + [x, mods, norm_g, dng0]
    vs = jax.ShapeDtypeStruct((1, D), F32)
    out_shape, out_specs = (vs, vs, vs), (vec, vec, vec)
    if has_x:
        in_specs.append(row)
        args.append(dxn)
        out_shape = (jax.ShapeDtypeStruct((lx, D), F32),) + out_shape
        out_specs = (row,) + out_specs
    return pl.pallas_call(
        body, name=name, grid=(n,), out_shape=out_shape, in_specs=in_specs, out_specs=out_specs,
        compiler_params=_cp(1, vmem_mb=56),
    )(*args)


def _tn_matmul(a, bs, extra, name):
    lx, m = a.shape
    tm = min(lx, 1024)
    n = lx // tm
    widths = [b.shape[1] for b in bs]
    nb = len(bs)

    def body(a_ref, *rest):
        b_refs = rest[:nb]
        rest = rest[nb:]
        if extra is not None:
            ea_ref, eb_ref = rest[:2]
            rest = rest[2:]
        out_ref, acc = rest
        i = pl.program_id(0)
        av = a_ref[...]
        off = 0
        for k in range(nb):
            cols = slice(off, off + widths[k])
            part = _dot_tn(av, b_refs[k][...])

            @pl.when(i == 0)
            def _():
                acc[:, cols] = part

            @pl.when(i > 0)
            def _():
                acc[:, cols] += part
            off += widths[k]

        @pl.when(i == n - 1)
        def _():
            if extra is not None:
                acc[:, 0:widths[0]] += _dot_tn(ea_ref[...], eb_ref[...])
            pltpu.sync_copy(acc, out_ref)

    in_specs = [pl.BlockSpec((tm, m), lambda i: (i, 0))] + [pl.BlockSpec((tm, w), lambda i: (i, 0)) for w in widths]
    args = [a] + list(bs)
    if extra is not None:
        in_specs += [VMEM, VMEM]
        args += list(extra)
    return pl.pallas_call(
        body, name=name, grid=(n,), out_shape=jax.ShapeDtypeStruct((m, sum(widths)), F32),
        in_specs=in_specs, out_specs=ANY, scratch_shapes=[pltpu.VMEM((m, sum(widths)), F32)],
        compiler_params=_cp(1, vmem_mb=56),
    )(*args)


def _adam_math(w, g, m, v):
    m = ADAM_B1 * m + (1.0 - ADAM_B1) * g
    v = ADAM_B2 * v + (1.0 - ADAM_B2) * (g * g)
    m_hat = m / (1.0 - ADAM_B1 ** ADAM_STEP)
    v_hat = v / (1.0 - ADAM_B2 ** ADAM_STEP)
    delta = -ADAM_LR * (m_hat / (jnp.sqrt(v_hat) + ADAM_EPS) + ADAM_WD * w)
    return delta, m, v


def _adam_big(w, g, m, v, name):
    rows, cols = w.shape
    tr = 256

    def body(w_ref, g_ref, m_ref, v_ref, d_o, m_o, v_o):
        d, mm, vv = _adam_math(w_ref[...], g_ref[...], m_ref[...], v_ref[...])
        d_o[...] = d
        m_o[...] = mm
        v_o[...] = vv

    blk = pl.BlockSpec((tr, cols), lambda i: (i, 0))
    s = jax.ShapeDtypeStruct((rows, cols), F32)
    return pl.pallas_call(
        body, name=name, grid=(rows // tr,), out_shape=(s, s, s), in_specs=[blk] * 4, out_specs=(blk,) * 3,
        compiler_params=_cp(1, vmem_mb=48),
    )(w, g, m, v)


def _adam_small(items):
    ni = len(items)

    def body(*refs):
        ins, outs = refs[:4 * ni], refs[4 * ni:7 * ni]
        bufs_in, bufs_out = refs[7 * ni:11 * ni], refs[11 * ni:14 * ni]
        sem_in, sem_out = refs[14 * ni], refs[14 * ni + 1]
        loads = [pltpu.make_async_copy(ins[q], bufs_in[q], sem_in.at[q]) for q in range(4 * ni)]
        for cp in loads:
            cp.start()
        stores = []
        for k in range(ni):
            for q in range(4):
                loads[4 * k + q].wait()
            w_b, g_b, m_b, v_b = bufs_in[4 * k:4 * k + 4]
            res = _adam_math(w_b[...], g_b[...], m_b[...], v_b[...])
            for q in range(3):
                bufs_out[3 * k + q][...] = res[q]
                cp = pltpu.make_async_copy(bufs_out[3 * k + q], outs[3 * k + q], sem_out.at[3 * k + q])
                cp.start()
                stores.append(cp)
        for cp in stores:
            cp.wait()

    flat = [a for it in items for a in it]
    out_shape = tuple(jax.ShapeDtypeStruct(it[0].shape, F32) for it in items for _ in range(3))
    scratch = [pltpu.VMEM(a.shape, F32) for a in flat] + [pltpu.VMEM(s.shape, F32) for s in out_shape]
    scratch += [pltpu.SemaphoreType.DMA((4 * ni,)), pltpu.SemaphoreType.DMA((3 * ni,))]
    res = pl.pallas_call(
        body, name="adam_small", out_shape=out_shape, in_specs=[HBM] * (4 * ni), out_specs=(HBM,) * (3 * ni),
        scratch_shapes=scratch, compiler_params=_cp(vmem_mb=40),
    )(*flat)
    return [tuple(res[3 * k:3 * k + 3]) for k in range(ni)]


def kernel(x, c, ctx, c_ctx, ada_w, ada_b, norm_g, w_in, conv_w, conv_b, lru_wa, lru_ba, lru_wx, lru_bx, lru_lambda, sgu_ln_g, sgu_ln_b, sgu_w, sgu_b, w_out, final_g, loss_target, m_c_ctx, m_ada_w, m_ada_b, m_norm_g, m_w_in, m_conv_w, m_conv_b, m_lru_wa, m_lru_ba, m_lru_wx, m_lru_bx, m_lru_lambda, m_sgu_ln_g, m_sgu_ln_b, m_sgu_w, m_sgu_b, m_w_out, m_final_g, v_c_ctx, v_ada_w, v_ada_b, v_norm_g, v_w_in, v_conv_w, v_conv_b, v_lru_wa, v_lru_ba, v_lru_wx, v_lru_bx, v_lru_lambda, v_sgu_ln_g, v_sgu_ln_b, v_sgu_w, v_sgu_b, v_w_out, v_final_g):
    ix, iy, ic = lax.axis_index("x"), lax.axis_index("y"), lax.axis_index("c")
    chip = 2 * ix + iy
    dev = 2 * chip + ic
    lx = x.shape[1]
    lc = ctx.shape[1]

    smalls = jnp.concatenate([conv_w[0], lru_lambda[0], jnp.zeros((10, 256), F32)], axis=0)
    c_ctx2 = c_ctx.reshape(1, D)
    ada_b_j = lax.dynamic_slice(ada_b, (0, 768 * chip), (1, 768))
    mods, c_slots, sm_all, w_in_full, wo_land, ada_land = _gather_in(c, c_ctx2, ada_w[0], ada_b_j, w_in[0], w_out[0],
                                                                     smalls)
    wo_ss, wo_rs, ada_ss, ada_rs, wo_land, ada_land, token = _late_gather_start(wo_land, ada_land)
    mods = mods + token[0:1, 0:1]
    sm3 = sm_all.reshape(NCHIP, 16, 256)
    conv_w_full = sm3[:, 0:4, :].transpose(1, 0, 2).reshape(4, D)
    lam_full = sm3[:, 4:6, :].transpose(1, 0, 2).reshape(2, D)
    conv_wz = conv_w_full.reshape(4, NH, HD)
    conv_bz = conv_b.reshape(NH, HD)
    lamcat = lam_full.reshape(2, NH, HD).transpose(1, 0, 2).reshape(NH, 2 * HD)
    wa, wx, ba, bx = lru_wa[0], lru_wx[0], lru_ba[0], lru_bx[0]
    wcat = jnp.concatenate([wa[0], wx[0], wa[1], wx[1]], axis=-1).astype(BF16)
    bcat = jnp.concatenate([ba[0], bx[0], ba[1], bx[1]], axis=-1)
    sgu_wb = sgu_w[0].astype(BF16)
    sgu_bt = sgu_b[0].T
    final_g2 = final_g.reshape(1, D)

    zero_s = jnp.zeros((NH, HD), F32)
    hn_c, xa_c = _proj(ctx[0], mods, 1, norm_g, w_in_full, 1, "proj_ctx")
    xaz_c, xcz_c, af_c, ab_c, hf_c, hb_c, hf0, hb0 = _lru_fwd(xa_c, conv_wz, conv_bz, wcat, bcat, lamcat,
                                                               zero_s, zero_s, "lru_fwd_ctx")

    hn, xa, ga, u, v, gb = _proj(x[0], mods, 0, norm_g, w_in_full, 5, "proj")
    xaz, xcz, af, ab, hf, hb, _, _ = _lru_fwd(xa, conv_wz, conv_bz, wcat, bcat, lamcat, hf0, hb0, "lru_fwd")
    ys = _sgu_fwd(u, v, sgu_ln_g, sgu_ln_b, sgu_wb, sgu_bt)

    w_out_full = _late_gather_wait(wo_land, wo_ss, wo_rs, "w_out", ys, "late_gather_wait_w_out")
    (loss_part, dfg, dgx, dxn, y, do, dga, dgb, dyl_z, dys) = _out_fwd_bwd(
        hf, hb, ga, gb, ys, x[0], loss_target[0], mods, final_g2, w_out_full)
    g_w_out_part = _tn_matmul(y, [do], None, "grad_w_out")

    du, dv, d_sgu_w, d_sgu_b, d_ln_g, d_ln_b = _sgu_bwd(u, v, dys, sgu_ln_g, sgu_ln_b, sgu_wb, sgu_bt)
    lb, lf, dh0b, dh0f = _scan_adjoint(ab, dyl_z, af, dyl_z, "scan_adj")
    zw = jnp.zeros((NH, HD, 4 * HD), F32)
    zb = jnp.zeros((NH, 4 * HD), F32)
    zl = jnp.zeros((NH, 2 * HD), F32)
    dxc_z, dwc, dbc, dlc = _lru_gates_bwd(xcz, lf, lb, hf, hb, hf0, hb0, wcat, bcat, lamcat, zw, zb, zl,
                                          "lru_gates_bwd")
    dxa, dcw, dcb = _conv_bwd(dxc_z, xaz, conv_wz, jnp.zeros((4, NH, HD), F32), zero_s, "conv_bwd")

    zc = jnp.zeros((lc * NH, HD), F32)
    dhf_c = lax.dynamic_update_slice(zc, dh0f, ((lc - 1) * NH, 0))
    dhb_c = lax.dynamic_update_slice(zc, dh0b, (0, 0))
    lb_c, lf_c, _, _ = _scan_adjoint(ab_c, dhb_c, af_c, dhf_c, "scan_adj_ctx")
    dxc_zc, dwc, dbc, dlc = _lru_gates_bwd(xcz_c, lf_c, lb_c, hf_c, hb_c, zero_s, zero_s, wcat, bcat, lamcat, dwc, dbc, dlc,
                                           "lru_gates_bwd_ctx")
    dxa_c, dcw, dcb = _conv_bwd(dxc_zc, xaz_c, conv_wz, dcw, dcb, "conv_bwd_ctx")

    dzs = [dxa, dga, du, dv, dgb]
    grad_x, dng, dsc_x, dsh_x = _proj_bwd(dzs, x[0], dxn, mods, 0, norm_g, w_in_full, jnp.zeros((1, D), F32), "proj_bwd")
    dng, dsc_c, dsh_c = _proj_bwd([dxa_c], ctx[0], None, mods, 1, norm_g, w_in_full, dng, "proj_bwd_ctx")

    dmx = jnp.concatenate([dsh_x, dsc_x, dgx], axis=0)
    dmc = jnp.concatenate([dsh_c, dsc_c, jnp.zeros((1, D), F32)], axis=0)
    lp = loss_part[0, 0]
    lp1 = lax.reduce_precision(lp, 8, 7)
    lp2 = lax.reduce_precision(lp - lp1, 8, 7)
    lp3 = lax.reduce_precision(lp - lp1 - lp2, 8, 7)
    loss_row = jnp.pad(jnp.stack([lp1, lp2, lp3]).reshape(1, 3), ((0, 0), (0, D - 3)))
    slot = jnp.concatenate([dmx, loss_row], axis=0)
    slots = lax.dynamic_update_slice(jnp.zeros((32, D), F32), slot, (4 * dev, 0))
    vecs = jnp.concatenate([dfg, dng, dcb.reshape(1, D), d_ln_g, d_ln_b, dcw.reshape(4, D), dmc,
                            jnp.zeros((4, D), F32), slots], axis=0)
    d_sgu_w4 = d_sgu_w.reshape(4, 256, HD).transpose(1, 0, 2).reshape(256, 4 * HD)
    pad8 = lambda a: jnp.pad(a, ((0, 8 - a.shape[0]), (0, 4 * HD - a.shape[1])))
    pack = jnp.concatenate([dwc.reshape(NH * HD, 4 * HD), pad8(dbc), pad8(dlc), d_sgu_w4, pad8(d_sgu_b),
                            vecs.reshape(96, 4 * HD), jnp.zeros((8, 4 * HD), F32)], axis=0)
    g_w_in, g_w_out, tot = _grads_reduce(hn, dzs, hn_c, dxa_c, g_w_out_part, pack)

    g_wc = tot[0:1024].reshape(NH, HD, 4 * HD)
    g_bc = tot[1024:1032]
    g_lc = tot[1032:1040, 0:2 * HD]
    g_sgu_w = tot[1040:1296].reshape(256, 4, HD).transpose(1, 0, 2).reshape(NH, HD, HD)
    g_sgu_b = tot[1296:1304, 0:HD]
    tv = tot[1304:1400].reshape(48, D)
    g_final_g, g_norm_g, g_conv_b, g_ln_g, g_ln_b = tv[0:1], tv[1:2], tv[2:3], tv[3:4], tv[4:5]
    g_conv_w_full = tv[5:9]
    dmc_tot = tv[9:12].reshape(1, 3 * D)
    slots_all = tv[16:48].reshape(8, 4, D)
    dmx_all = slots_all[:, 0:3, :].reshape(8, 3 * D)
    c_all = c_slots.reshape(8, 8, D)[:, 0, :]
    g_lru_wa = jnp.stack([g_wc[:, :, 0:HD], g_wc[:, :, 2 * HD:3 * HD]])
    g_lru_wx = jnp.stack([g_wc[:, :, HD:2 * HD], g_wc[:, :, 3 * HD:4 * HD]])
    g_lru_ba = jnp.stack([g_bc[:, 0:HD], g_bc[:, 2 * HD:3 * HD]])
    g_lru_bx = jnp.stack([g_bc[:, HD:2 * HD], g_bc[:, 3 * HD:4 * HD]])
    g_lam_full = jnp.stack([g_lc[:, 0:HD], g_lc[:, HD:2 * HD]]).reshape(2, D)
    g_conv_w = lax.dynamic_slice(g_conv_w_full, (0, 256 * chip), (4, 256))
    g_lam = lax.dynamic_slice(g_lam_full, (0, 256 * chip), (2, 256))
    dmx_all_j = lax.dynamic_slice(dmx_all, (0, 768 * chip), (8, 768))
    dmc_j = lax.dynamic_slice(dmc_tot, (0, 768 * chip), (1, 768))
    ada_full = _late_gather_wait(ada_land, ada_ss, ada_rs, "ada_w", tot, "late_gather_wait_ada_w")
    g_ada_w, g_ada_b, g_c_ctx = _ada_bwd(c_all, dmx_all_j, dmc_j, dmx_all, dmc_tot, c_ctx2, ada_full)

    big = {
        "ada_w": _adam_big(ada_w[0], g_ada_w, m_ada_w[0], v_ada_w[0], "adam_ada_w"),
        "w_in": _adam_big(w_in[0], g_w_in, m_w_in[0], v_w_in[0], "adam_w_in"),
        "w_out": _adam_big(w_out[0], g_w_out, m_w_out[0], v_w_out[0], "adam_w_out"),
    }
    small_in = {
        "c_ctx": (c_ctx, g_c_ctx, m_c_ctx, v_c_ctx, (1, D)),
        "ada_b": (ada_b, g_ada_b, m_ada_b, v_ada_b, (1, 3 * D)),
        "norm_g": (norm_g, g_norm_g, m_norm_g, v_norm_g, (1, D)),
        "conv_w": (conv_w, g_conv_w, m_conv_w, v_conv_w, (4, 256)),
        "conv_b": (conv_b, g_conv_b, m_conv_b, v_conv_b, (1, D)),
        "lru_wa": (lru_wa, g_lru_wa, m_lru_wa, v_lru_wa, (2 * NH * HD, HD)),
        "lru_ba": (lru_ba, g_lru_ba, m_lru_ba, v_lru_ba, (2 * NH, HD)),
        "lru_wx": (lru_wx, g_lru_wx, m_lru_wx, v_lru_wx, (2 * NH * HD, HD)),
        "lru_bx": (lru_bx, g_lru_bx, m_lru_bx, v_lru_bx, (2 * NH, HD)),
        "lru_lambda": (lru_lambda, g_lam, m_lru_lambda, v_lru_lambda, (2, 256)),
        "sgu_ln_g": (sgu_ln_g, g_ln_g, m_sgu_ln_g, v_sgu_ln_g, (1, D)),
        "sgu_ln_b": (sgu_ln_b, g_ln_b, m_sgu_ln_b, v_sgu_ln_b, (1, D)),
        "sgu_w": (sgu_w, g_sgu_w, m_sgu_w, v_sgu_w, (NH * HD, HD)),
        "sgu_b": (sgu_b, g_sgu_b, m_sgu_b, v_sgu_b, (NH, HD)),
        "final_g": (final_g, g_final_g, m_final_g, v_final_g, (1, D)),
    }
    names_small = list(small_in)
    res_small = _adam_small([tuple(a.reshape(small_in[k][4]) for a in small_in[k][:4]) for k in names_small])
    full_shapes = {"ada_w": ada_w.shape, "w_in": w_in.shape, "w_out": w_out.shape}
    grads, deltas, new_m, new_v = {}, {}, {}, {}
    for k in ("ada_w", "w_in", "w_out"):
        g = {"ada_w": g_ada_w, "w_in": g_w_in, "w_out": g_w_out}[k]
        grads[k] = g.reshape(full_shapes[k])
        deltas[k], new_m[k], new_v[k] = (a.reshape(full_shapes[k]) for a in big[k])
    for k, res in zip(names_small, res_small):
        shape = small_in[k][0].shape
        grads[k] = small_in[k][1].reshape(shape)
        deltas[k], new_m[k], new_v[k] = (a.reshape(shape) for a in res)

    loss = jnp.sum(slots_all[:, 3, 0:3])
    order = ["c_ctx", "ada_w", "ada_b", "norm_g", "w_in", "conv_w", "conv_b", "lru_wa", "lru_ba", "lru_wx", "lru_bx",
             "lru_lambda", "sgu_ln_g", "sgu_ln_b", "sgu_w", "sgu_b", "w_out", "final_g"]
    return (loss, grad_x.reshape(x.shape), *[grads[k] for k in order], *[deltas[k] for k in order],
            *[new_m[k] for k in order], *[new_v[k] for k in order])
```

```python
import functools

import jax
import jax.numpy as jnp
from jax import lax
from jax.experimental import pallas as pl
from jax.experimental.pallas import tpu as pltpu

F32 = jnp.float32
BF16 = jnp.bfloat16

D = 1024
NH = 8
HD = 128
NCHIP = 4
T = 256
TP = 512
NORM_EPS = 1e-6
LN_EPS = 1e-5
LRU_C = 8.0
ADAM_LR = 0.001
ADAM_B1 = 0.9
ADAM_B2 = 0.999
ADAM_EPS = 1e-08
ADAM_WD = 0.01
ADAM_STEP = 10

VMEM = pl.BlockSpec(memory_space=pltpu.VMEM)
ANY = pl.BlockSpec(memory_space=pl.ANY)
MESH = pl.DeviceIdType.MESH


def _cp(n_grid=0, vmem_mb=None):
    kw = {}
    if n_grid:
        kw["dimension_semantics"] = ("arbitrary",) * n_grid
    if vmem_mb:
        kw["vmem_limit_bytes"] = vmem_mb << 20
    return pltpu.CompilerParams(**kw)


def _sigmoid(x):
    return 1.0 / (1.0 + jnp.exp(-x))


def _silu_and_grad(x):
    s = _sigmoid(x)
    return x * s, s * (1.0 + x * (1.0 - s))


_GELU_K = 0.7978845608028654
_GELU_C = 0.044715


def _gelu_and_grad(x):
    x2 = x * x
    th = jnp.tanh(_GELU_K * (x + _GELU_C * x * x2))
    g = 0.5 * x * (1.0 + th)
    dg = 0.5 * (1.0 + th) + 0.5 * x * (1.0 - th * th) * (_GELU_K * (1.0 + 3.0 * _GELU_C * x2))
    return g, dg


def _softplus(x):
    return jnp.maximum(x, 0.0) + jnp.log1p(jnp.exp(-jnp.abs(x)))


def _lru_gate(pre, lam_row, d, off=None):
    off = 256 * d if off is None else off
    r = _sigmoid(pre[:, off:off + HD])
    gi = _sigmoid(pre[:, off + HD:off + 2 * HD])
    lam = lam_row[:, HD * d:HD * d + HD]
    sp = _softplus(-lam)
    la = (-LRU_C) * r * sp
    a = jnp.exp(la)
    x2 = 2.0 * la
    m2 = jnp.where(x2 > -1e-3, -x2 * (1.0 + 0.5 * x2), 1.0 - a * a)
    mult = jnp.sqrt(m2)
    return r, gi, lam, sp, a, mult


def _dot(a, b):
    return jnp.dot(a, b, preferred_element_type=F32)


def _dot_tn(a, b):
    return lax.dot_general(a, b, (((0,), (0,)), ((), ())), preferred_element_type=F32)


def _dot_nt(a, b):
    return lax.dot_general(a, b, (((1,), (1,)), ((), ())), preferred_element_type=F32)


def _mo(v, m):
    return v if isinstance(v, int) else pl.multiple_of(v, m)


def _zrows(h, n):
    return pl.ds(h, n, stride=NH)


def _gather_in(c, c_ctx, ada_w, ada_b_j, w_in, w_out, smalls):
    nch = [1, 4]
    wrows = lambda cc, q: (pl.ds(_mo(512 * cc, 16), 512) if q is None
                           else pl.ds(_mo(512 * cc + (512 // nch[1]) * q, 16), 512 // nch[1]))
    specs = [
        ((64, 256), F32, lambda r, jj, cc, q=None: r.at[pl.ds(_mo(16 * jj + 8 * cc, 8), 8), :]),
        ((D, 5120), BF16, lambda r, jj, cc, q=None: r.at[wrows(cc, q), pl.ds(_mo(1280 * jj, 128), 1280)]),
    ]
    halves = [lambda r, cc, q=None: r.at[pl.ds(_mo(8 * cc, 8), 8), :],
              lambda r, cc, q=None: r.at[wrows(cc, q), :]]
    na = len(specs)
    sem_base = [0, 6 * nch[0]]
    sidx = lambda a, q, k: sem_base[a] + 6 * q + k
    n_tiny = 6 * sum(nch)
    n_sem = n_tiny + 10

    def body(c_ref, cc_ref, ada_ref, adab_ref, win_ref, wout_ref, sm_ref,
             mods_o, call_o, sm_o, win_o, wol_o, adal_o, s_win, s_ada, s_wout, cslot, lhs, mbuf,
             send_sems, recv_sems, local_sems):
        x, y, c = lax.axis_index("x"), lax.axis_index("y"), lax.axis_index("c")
        j = 2 * x + y
        dev = 2 * j + c
        sib = (x, y, 1 - c)
        chips = [(1 - x, y), (x, 1 - y), (1 - x, 1 - y)]
        cj = [2 * cx + cy for cx, cy in chips]
        outs = [sm_o, win_o]
        srcs = [sm_ref, s_win]

        def copy(idx, src, dst, to):
            return pltpu.make_async_remote_copy(src_ref=src, dst_ref=dst, send_sem=send_sems.at[idx],
                                                recv_sem=recv_sems.at[idx], device_id=to, device_id_type=MESH)

        sends = []

        def start(cp):
            cp.start()
            sends.append(cp)

        cslot[...] = jnp.zeros_like(cslot)
        cslot[0:1, :] = c_ref[...]
        my_slot = pl.ds(_mo(8 * dev, 8), 8)
        others = [sib] + [(*chips[k], c) for k in range(3)] + [(*chips[k], 1 - c) for k in range(3)]
        other_dev = [dev + 1 - 2 * c] + [2 * cj[k] + c for k in range(3)] + [2 * cj[k] + 1 - c for k in range(3)]
        base = n_tiny
        for r in range(7):
            start(copy(base + r, cslot, call_o.at[my_slot, :], others[r]))
        call_o[my_slot, :] = cslot[...]

        s_win[...] = win_ref[...].astype(BF16)
        local = []
        for a in range(na):
            for cc in range(2):
                lc = pltpu.make_async_copy(halves[a](srcs[a], cc), specs[a][2](outs[a], j, cc), local_sems.at[2 * a + cc])
                lc.start()
                local.append(lc)
        for a in range(na):
            for q in range(nch[a]):
                for k in range(2):
                    start(copy(sidx(a, q, k), halves[a](srcs[a], c, q), specs[a][2](outs[a], j, c, q), (*chips[k], c)))
        s_wout[...] = wout_ref[...].astype(BF16)
        s_ada[...] = ada_ref[...].astype(BF16)
        for q, (src, dst) in enumerate([(s_wout, wol_o.at[pl.ds(_mo(512 * j, 16), 512), :]),
                                        (s_ada, adal_o.at[:, pl.ds(_mo(768 * j, 128), 768)])]):
            lc = pltpu.make_async_copy(src, dst, local_sems.at[2 * na + q])
            lc.start()
            local.append(lc)

        for r in range(7):
            slot = call_o.at[pl.ds(_mo(8 * other_dev[r], 8), 8), :]
            copy(base + r, slot, slot, sib).wait_recv()
        lhs[...] = jnp.zeros_like(lhs)
        for b in range(8):
            cv = call_o[8 * b:8 * b + 1, :]
            lhs[b:b + 1, :] = cv * _sigmoid(cv)
        cv = cc_ref[...]
        lhs[8:9, :] = cv * _sigmoid(cv)
        mbuf[j] = _dot(lhs[...].astype(BF16), s_ada[...]) + adab_ref[...]
        for k in range(3):
            start(copy(base + 7 + k, mbuf.at[j], mbuf.at[j], (*chips[k], c)))
        for k in range(3):
            copy(base + 7 + k, mbuf.at[cj[k]], mbuf.at[cj[k]], sib).wait_recv()
        mods_o[...] = jnp.zeros_like(mods_o)
        for jj in range(NCHIP):
            mods_o[0:1, 768 * jj:768 * jj + 768] = mbuf[jj, pl.ds(dev, 1), :]
            mods_o[1:2, 768 * jj:768 * jj + 768] = mbuf[jj, 8:9, :]

        kx = [1 - x, x, 1 - x]
        ky = [y, 1 - y, 1 - y]
        pick = lambda k, lst: jnp.where(k == 0, lst[0], jnp.where(k == 1, lst[1], lst[2]))
        for a in range(na):
            for q in range(nch[a]):
                for step, k in enumerate([c, 1 - c]):
                    reg = specs[a][2](outs[a], pick(k, cj), c, q)
                    copy(sidx(a, q, k), reg, reg, sib).wait_recv()
                    if step == 0:
                        start(copy(sidx(a, q, 2), reg, reg, (pick(1 - c, kx), pick(1 - c, ky), c)))
                    start(copy(sidx(a, q, 3 + k), reg, reg, sib))
        for a in range(na):
            for q in range(nch[a]):
                reg = specs[a][2](outs[a], cj[2], c, q)
                copy(sidx(a, q, 2), reg, reg, sib).wait_recv()
                start(copy(sidx(a, q, 5), reg, reg, sib))
        for a in range(na):
            for q in range(nch[a]):
                for k in range(3):
                    reg = specs[a][2](outs[a], cj[k], 1 - c, q)
                    copy(sidx(a, q, 3 + k), reg, reg, sib).wait_recv()
        for cp in sends:
            cp.wait_send()
        for lc in local:
            lc.wait()

    out_shape = (jax.ShapeDtypeStruct((8, 3 * D), F32), jax.ShapeDtypeStruct((64, D), F32),
                 jax.ShapeDtypeStruct(specs[0][0], F32), jax.ShapeDtypeStruct(specs[1][0], BF16),
                 jax.ShapeDtypeStruct((2048, D), BF16), jax.ShapeDtypeStruct((D, 3 * D), BF16))
    return pl.pallas_call(
        body, name="gather_in", out_shape=out_shape,
        in_specs=[VMEM] * 7, out_specs=(VMEM, VMEM, VMEM, ANY, ANY, ANY),
        scratch_shapes=[pltpu.VMEM((D, 1280), BF16), pltpu.VMEM((D, 768), BF16), pltpu.VMEM((512, D), BF16),
                        pltpu.VMEM((8, D), F32), pltpu.VMEM((16, D), F32), pltpu.VMEM((NCHIP, 16, 768), F32),
                        pltpu.SemaphoreType.DMA((n_sem,)), pltpu.SemaphoreType.DMA((n_sem,)),
                        pltpu.SemaphoreType.DMA((2 * na + 2,))],
        compiler_params=_cp(vmem_mb=56),
    )(c, c_ctx, ada_w, ada_b_j, w_in, w_out, smalls)


HBM = pl.BlockSpec(memory_space=pltpu.HBM)
SEM = pl.BlockSpec(memory_space=pltpu.SEMAPHORE)


def _late_gather_regions(x, y, c):
    chips = [(1 - x, y), (x, 1 - y), (1 - x, 1 - y)]
    wo_reg = lambda r, jj, cc: r.at[pl.ds(_mo(512 * jj + 256 * cc, 16), 256), :]
    ada_reg = lambda r, jj, cc: r.at[pl.ds(_mo(512 * cc, 16), 512), pl.ds(_mo(768 * jj, 128), 768)]
    return chips, wo_reg, ada_reg


def _late_gather_start(wo_land, ada_land):
    def body(wol_ref, adal_ref, wo_ss, wo_rs, ada_ss, ada_rs, wol_thru, adal_thru, token):
        x, y, c = lax.axis_index("x"), lax.axis_index("y"), lax.axis_index("c")
        j = 2 * x + y
        chips, wo_reg, ada_reg = _late_gather_regions(x, y, c)
        for k in range(3):
            for cc in range(2):
                pltpu.make_async_remote_copy(src_ref=wo_reg(wol_ref, j, c), dst_ref=wo_reg(wol_ref, j, c),
                                             send_sem=wo_ss.at[2 * k + cc], recv_sem=wo_rs.at[2 * k + c],
                                             device_id=(*chips[k], cc), device_id_type=MESH).start()
        for k in range(3):
            for cc in range(2):
                pltpu.make_async_remote_copy(src_ref=ada_reg(adal_ref, j, c), dst_ref=ada_reg(adal_ref, j, c),
                                             send_sem=ada_ss.at[2 * k + cc], recv_sem=ada_rs.at[2 * k + c],
                                             device_id=(*chips[k], cc), device_id_type=MESH).start()
        token[...] = jnp.zeros_like(token)

    sems = pltpu.SemaphoreType.DMA((6,))
    return pl.pallas_call(
        body, name="late_gather_start",
        out_shape=(sems, sems, sems, sems, pltpu.HBM(wo_land.shape, BF16), pltpu.HBM(ada_land.shape, BF16),
                   jax.ShapeDtypeStruct((8, 128), F32)),
        in_specs=(HBM, HBM), out_specs=(SEM, SEM, SEM, SEM, HBM, HBM, VMEM), input_output_aliases={0: 4, 1: 5},
        compiler_params=pltpu.CompilerParams(has_side_effects=pltpu.SideEffectType.DATAFLOW_SIDE_EFFECTING),
    )(pltpu.with_memory_space_constraint(wo_land, pltpu.HBM), pltpu.with_memory_space_constraint(ada_land, pltpu.HBM))


def _late_gather_wait(land, send_sems, recv_sems, which, after, name):
    def body(land_ref, ss, rs, after_ref, land_out):
        x, y, c = lax.axis_index("x"), lax.axis_index("y"), lax.axis_index("c")
        j = 2 * x + y
        chips, wo_reg, ada_reg = _late_gather_regions(x, y, c)
        reg = wo_reg if which == "w_out" else ada_reg
        for k in range(3):
            kj = 2 * chips[k][0] + chips[k][1]
            for cc in range(2):
                cp = pltpu.make_async_remote_copy(src_ref=reg(land_ref, j, c), dst_ref=reg(land_ref, kj, cc),
                                                  send_sem=ss.at[2 * k + cc], recv_sem=rs.at[2 * k + cc],
                                                  device_id=(*chips[k], cc), device_id_type=MESH)
                cp.wait_send()
                cp.wait_recv()

    return pl.pallas_call(
        body, name=name, out_shape=pltpu.HBM(land.shape, land.dtype),
        in_specs=(HBM, SEM, SEM, ANY), out_specs=HBM, input_output_aliases={0: 0},
        compiler_params=pltpu.CompilerParams(has_side_effects=pltpu.SideEffectType.DATAFLOW_SIDE_EFFECTING),
    )(land, send_sems, recv_sems, after)


RCHUNK = 16


def _grads_reduce(hn, dzs, hn_c, dxa_c, g_w_out, pack):
    rp = pack.shape[0]
    hp = rp // 2
    assert hp % RCHUNK == 0
    wi_w = 1280
    lx, lc = hn.shape[0], hn_c.shape[0]
    lt = lx + lc
    n_dz = len(dzs)

    def body(*refs):
        hn_hbm, dz_hbm = refs[0], refs[1:1 + n_dz]
        hnc_hbm, dxac_hbm, wo_hbm, pk_hbm, wi_out, wo_out, pk_out = refs[1 + n_dz:8 + n_dz]
        (hn_mine, hn_other, dzbuf, wi_other, wi_mine, wi_recv, wi_send, wi_rb,
         wo_mine, wo_recv, wo_send, wo_rb, wo_own, pk_mine, pk_recv, pk_send, pk_rb, pk_own,
         send_sems, recv_sems, local_sems) = refs[8 + n_dz:]
        x, y, c = lax.axis_index("x"), lax.axis_index("y"), lax.axis_index("c")
        j = 2 * x + y
        sib = (x, y, 1 - c)
        chips = [(1 - x, y), (x, 1 - y), (1 - x, 1 - y)]
        cj = [2 * cx + cy for cx, cy in chips]
        near = (jnp.where(c == 0, 1 - x, x), jnp.where(c == 0, y, 1 - y), c)
        slabs = [cj[2], cj[0], cj[1], j]

        def copy(k, src, dst, to):
            return pltpu.make_async_remote_copy(src_ref=src, dst_ref=dst, send_sem=send_sems.at[k],
                                                recv_sem=recv_sems.at[k], device_id=to, device_id_type=MESH)

        def local(k, src, dst):
            cp = pltpu.make_async_copy(src, dst, local_sems.at[k])
            cp.start()
            return cp

        rows_half = lambda r, cc, n: r.at[pl.ds(_mo(cc * n, 16), n), :]
        cols_half = lambda r, cc, n: r.at[:, pl.ds(_mo(cc * n, 128), n)]
        pk_piece = lambda r, cc, jj: r.at[pl.ds(_mo(cc * hp, 16), hp), pl.ds(_mo(jj * 128, 128), 128)]

        sends = []

        def start(cp):
            cp.start()
            sends.append(cp)

        def dz_pieces(s):
            g0 = wi_w * s
            k0, off0 = g0 // D, g0 % D
            w0 = min(D - off0, wi_w)
            pieces = [(k0, off0, w0, 0)]
            if w0 < wi_w:
                pieces.append((k0 + 1, 0, wi_w - w0, w0))
            return pieces

        def dz_copies(s):
            cps = []
            for q, (k, off, w, dst) in enumerate(dz_pieces(s)):
                cps.append(pltpu.make_async_copy(dz_hbm[k].at[:, pl.ds(off, w)], dzbuf.at[pl.ds(0, lx), pl.ds(dst, w)],
                                                 local_sems.at[11 + q]))
            if s == 0:
                cps.append(pltpu.make_async_copy(dxac_hbm, dzbuf.at[pl.ds(lx, lc), pl.ds(0, D)], local_sems.at[13]))
            return cps

        def dz_load(sl):
            for s in range(NCHIP):
                @pl.when(sl == s)
                def _():
                    if s == 0:
                        dzbuf[pl.ds(lx, lc), pl.ds(D, wi_w - D)] = jnp.zeros((lc, wi_w - D), BF16)
                    else:
                        dzbuf[pl.ds(lx, lc), :] = jnp.zeros((lc, wi_w), BF16)
                    for cp in dz_copies(s):
                        cp.start()

        def dz_wait(sl):
            for s in range(NCHIP):
                @pl.when(sl == s)
                def _():
                    for cp in dz_copies(s):
                        cp.wait()

        l_pk = local(0, rows_half(pk_hbm, c, hp), pk_mine)
        start(copy(0, rows_half(pk_hbm, 1 - c, hp), pk_recv, sib))
        l_wo = local(1, cols_half(wo_hbm, c, 512), wo_mine)
        start(copy(1, cols_half(wo_hbm, 1 - c, 512), wo_recv, sib))
        hn_loads = [local(2, hn_hbm.at[:, pl.ds(_mo(c * 512, 128), 512)], hn_mine.at[pl.ds(0, lx), :]),
                    local(3, hnc_hbm.at[:, pl.ds(_mo(c * 512, 128), 512)], hn_mine.at[pl.ds(lx, lc), :]),
                    local(4, hn_hbm.at[:, pl.ds(_mo((1 - c) * 512, 128), 512)], hn_other.at[pl.ds(0, lx), :]),
                    local(5, hnc_hbm.at[:, pl.ds(_mo((1 - c) * 512, 128), 512)], hn_other.at[pl.ds(lx, lc), :])]
        dz_load(slabs[0])

        def pair_sum(mine, recv, send, nrows, keep, relayed=None):
            def step(i, carry):
                rows = pl.ds(_mo(i * RCHUNK, RCHUNK), RCHUNK)
                s = mine[rows, :] + recv[rows, :].astype(F32)
                if relayed is not None:
                    s = s + relayed[rows, :].astype(F32)
                if keep:
                    mine[rows, :] = s
                if send is not None:
                    send[rows, :] = s.astype(BF16)
                return carry
            lax.fori_loop(0, nrows // RCHUNK, step, 0)

        def chip_sum(own, rb, nrows, terms=(0, 1, 2)):
            def step(i, carry):
                rows = pl.ds(_mo(i * RCHUNK, RCHUNK), RCHUNK)
                acc = own[rows, :]
                for q in terms:
                    acc = acc + rb[q, rows, :].astype(F32)
                own[rows, :] = acc
                return carry
            lax.fori_loop(0, nrows // RCHUNK, step, 0)

        p1 = [None] * NCHIP

        def slab_matmuls(s):
            if s >= 2:
                p1[s - 2].wait_send()
            dz_wait(slabs[s])
            wi_other[s % 2] = _dot_tn(hn_other[...], dzbuf[...]).astype(BF16)
            p1[s] = copy(2 + s, wi_other.at[s % 2], wi_recv.at[s], sib)
            p1[s].start()
            wi_mine[s % 2] = _dot_tn(hn_mine[...], dzbuf[...])
            if s + 1 < NCHIP:
                dz_load(slabs[s + 1])

        def slab_finish(s):
            copy(2 + s, wi_recv.at[s], wi_recv.at[s], sib).wait_recv()
            if s == 3:
                pair_sum(wi_mine.at[s % 2], wi_recv.at[s], None, 512, True)
                return
            if s == 0:
                pair_sum(wi_mine.at[0], wi_recv.at[0], wi_send.at[0], 512, False)
                start(copy(12, wi_send.at[0], wi_rb.at[0], near))
                return
            adds_relayed = c == (1 if s == 1 else 0)

            @pl.when(adds_relayed)
            def _():
                copy(12, wi_rb.at[0], wi_rb.at[0], sib).wait_recv()
                pair_sum(wi_mine.at[s % 2], wi_recv.at[s], wi_send.at[s], 512, False, wi_rb.at[0])

            @pl.when(jnp.logical_not(adds_relayed))
            def _():
                pair_sum(wi_mine.at[s % 2], wi_recv.at[s], wi_send.at[s], 512, False)
            start(copy(12 + s, wi_send.at[s], wi_rb.at[s], (*chips[s - 1], c)))

        l_wo.wait()
        copy(1, wo_recv, wo_recv, sib).wait_recv()
        pair_sum(wo_mine, wo_recv, wo_send, 2048, True)
        for k in range(3):
            start(copy(9 + k, wo_send.at[pl.ds(_mo(cj[k] * 512, 16), 512), :], wo_rb.at[k], (*chips[k], c)))
        l_wo_own = local(7, wo_mine.at[pl.ds(_mo(j * 512, 16), 512), :], wo_own)

        for cp in hn_loads:
            cp.wait()
        slab_matmuls(0)

        l_pk.wait()
        copy(0, pk_recv, pk_recv, sib).wait_recv()
        pair_sum(pk_mine, pk_recv, pk_send, hp, True)
        for k in range(3):
            start(copy(6 + k, pk_send.at[:, pl.ds(_mo(cj[k] * 128, 128), 128)], pk_rb.at[k], (*chips[k], c)))
        l_pk_own = local(6, pk_mine.at[:, pl.ds(_mo(j * 128, 128), 128)], pk_own)

        slab_matmuls(1)
        slab_finish(0)

        l_pk_own.wait()
        for k in range(3):
            copy(6 + k, pk_rb.at[k], pk_rb.at[k], sib).wait_recv()
        chip_sum(pk_own, pk_rb, hp)
        l_pk_out = local(8, pk_own, pk_piece(pk_out, c, j))
        start(copy(15, pk_own, pk_piece(pk_out, c, j), sib))
        for k in range(3):
            start(copy(16 + k, pk_own, pk_piece(pk_out, c, j), (*chips[k], c)))

        slab_matmuls(2)
        slab_finish(1)
        slab_matmuls(3)
        slab_finish(2)

        l_wo_own.wait()
        for k in range(3):
            copy(9 + k, wo_rb.at[k], wo_rb.at[k], sib).wait_recv()
        chip_sum(wo_own, wo_rb, 512)
        l_wo_out = local(9, wo_own, cols_half(wo_out, c, 512))
        start(copy(22, wo_own, cols_half(wo_out, c, 512), sib))

        for k in range(3):
            reg = pk_piece(pk_out, c, cj[k])
            copy(16 + k, reg, reg, sib).wait_recv()
            start(copy(19 + k, reg, reg, sib))

        slab_finish(3)
        for k in (1, 2):
            copy(12 + k, wi_rb.at[k], wi_rb.at[k], sib).wait_recv()
        chip_sum(wi_mine.at[1], wi_rb, 512, (1, 2))
        l_wi_out = local(10, wi_mine.at[1], rows_half(wi_out, c, 512))
        start(copy(23, wi_mine.at[1], rows_half(wi_out, c, 512), sib))

        reg = pk_piece(pk_out, 1 - c, j)
        copy(15, reg, reg, sib).wait_recv()
        for k in range(3):
            reg = pk_piece(pk_out, 1 - c, cj[k])
            copy(19 + k, reg, reg, sib).wait_recv()
        reg = cols_half(wo_out, 1 - c, 512)
        copy(22, reg, reg, sib).wait_recv()
        reg = rows_half(wi_out, 1 - c, 512)
        copy(23, reg, reg, sib).wait_recv()
        for cp in sends + p1[2:]:
            cp.wait_send()
        for cp in (l_pk_out, l_wo_out, l_wi_out):
            cp.wait()

    return pl.pallas_call(
        body, name="grads_reduce",
        out_shape=(jax.ShapeDtypeStruct((D, wi_w), F32), jax.ShapeDtypeStruct((512, D), F32),
                   jax.ShapeDtypeStruct(pack.shape, F32)),
        in_specs=[ANY] * (5 + n_dz), out_specs=(ANY,) * 3,
        scratch_shapes=[
            pltpu.VMEM((lt, 512), BF16), pltpu.VMEM((lt, 512), BF16), pltpu.VMEM((lt, wi_w), BF16),
            pltpu.VMEM((2, 512, wi_w), BF16), pltpu.VMEM((2, 512, wi_w), F32), pltpu.VMEM((4, 512, wi_w), BF16),
            pltpu.VMEM((3, 512, wi_w), BF16), pltpu.VMEM((3, 512, wi_w), BF16),
            pltpu.VMEM((2048, 512), F32), pltpu.VMEM((2048, 512), F32), pltpu.VMEM((2048, 512), BF16),
            pltpu.VMEM((3, 512, 512), BF16), pltpu.VMEM((512, 512), F32),
            pltpu.VMEM((hp, 512), F32), pltpu.VMEM((hp, 512), F32), pltpu.VMEM((hp, 512), BF16),
            pltpu.VMEM((3, hp, 128), BF16), pltpu.VMEM((hp, 128), F32),
            pltpu.SemaphoreType.DMA((24,)), pltpu.SemaphoreType.DMA((24,)), pltpu.SemaphoreType.DMA((14,))],
        compiler_params=_cp(vmem_mb=56),
    )(hn, *dzs, hn_c, dxa_c, g_w_out, pack)


def _ada_bwd(c_all, dmx_all_j, dmc_j, dmx_all, dmc, c_ctx, ada_w_full):
    def body(c_ref, dmxj_ref, dmcj_ref, dmx_ref, dmc_ref, cc_ref, w_ref, gw_ref, gb_ref, gc_ref, lhs, rhs, dm8):
        lhs[...] = jnp.zeros_like(lhs)
        rhs[...] = jnp.zeros_like(rhs)
        cv = c_ref[...]
        lhs[0:8, :] = cv * _sigmoid(cv)
        cc = cc_ref[...]
        a_c, da_c = _silu_and_grad(cc)
        lhs[8:9, :] = a_c
        rhs[0:8, :] = dmxj_ref[...]
        rhs[8:9, :] = dmcj_ref[...]
        gw_ref[...] = _dot_tn(lhs[...].astype(BF16), rhs[...].astype(BF16))
        gb_ref[...] = jnp.sum(dmx_ref[...], axis=0, keepdims=True) + dmc_ref[...]
        dm8[...] = jnp.zeros_like(dm8)
        dm8[0:1, :] = dmc_ref[...]
        da = _dot_nt(dm8[...].astype(BF16), w_ref[...])
        gc_ref[...] = da[0:1, :] * da_c

    return pl.pallas_call(
        body, name="ada_bwd",
        out_shape=(jax.ShapeDtypeStruct((D, 768), F32), jax.ShapeDtypeStruct((1, 3 * D), F32),
                   jax.ShapeDtypeStruct((1, D), F32)),
        in_specs=[VMEM] * 7, out_specs=(VMEM,) * 3,
        scratch_shapes=[pltpu.VMEM((16, D), F32), pltpu.VMEM((16, 768), F32), pltpu.VMEM((8, 3 * D), F32)],
        compiler_params=_cp(vmem_mb=32),
    )(c_all, dmx_all_j, dmc_j, dmx_all, dmc, c_ctx, ada_w_full)


def _proj(x, mods, mrow, norm_g, w_full, nk, name):
    lx = x.shape[0]
    tp = min(lx, TP)
    n = lx // tp

    def body(x_ref, sh_ref, sc_ref, ng_ref, *rest):
        w_refs, hn_ref, z_refs = rest[:nk], rest[nk], rest[nk + 1:]
        xv = x_ref[...]
        r = lax.rsqrt(jnp.mean(xv * xv, axis=-1, keepdims=True) + NORM_EPS)
        hn = (xv * r) * ng_ref[...] * (1.0 + sc_ref[mrow:mrow + 1, :]) + sh_ref[mrow:mrow + 1, :]
        hb = hn.astype(BF16)
        hn_ref[...] = hb
        for k in range(nk):
            z_refs[k][...] = _dot(hb, w_refs[k][...])

    row = pl.BlockSpec((tp, D), lambda i: (i, 0))
    in_specs = [row, pl.BlockSpec((8, D), lambda i: (0, 0)), pl.BlockSpec((8, D), lambda i: (0, 1)),
                pl.BlockSpec((1, D), lambda i: (0, 0))]
    in_specs += [pl.BlockSpec((D, D), lambda i, k=k: (0, k)) for k in range(nk)]
    out_shape = (jax.ShapeDtypeStruct((lx, D), BF16),) + tuple(jax.ShapeDtypeStruct((lx, D), F32) for _ in range(nk))
    return pl.pallas_call(
        body, name=name, grid=(n,), out_shape=out_shape, in_specs=in_specs, out_specs=(row,) * (nk + 1),
        compiler_params=_cp(1, vmem_mb=56),
    )(x, mods, mods, norm_g, *([w_full] * nk))


def _halo_specs(lx):
    last = lx // 8 - 1
    return [pl.BlockSpec((T, D), lambda i: (i, 0)),
            pl.BlockSpec((8, D), lambda i: (jnp.maximum(i * (T // 8) - 1, 0), 0)),
            pl.BlockSpec((8, D), lambda i: (jnp.minimum((i + 1) * (T // 8), last), 0))]


def _zhalo_specs(lx):
    last = lx // 8 - 1
    return [pl.BlockSpec((T * NH, HD), lambda i: (i, 0)),
            pl.BlockSpec((8 * NH, HD), lambda i: (jnp.maximum(i * (T // 8) - 1, 0), 0)),
            pl.BlockSpec((8 * NH, HD), lambda i: (jnp.minimum((i + 1) * (T // 8), last), 0))]


ZT = pl.BlockSpec((T * NH, HD), lambda i: (i, 0))
CONV_CHUNK = 32


SCAN_SUB = 4


def _scan_tile(chains, post, carry_ref):
    blk = T // SCAN_SUB

    def step(k, state):
        new = []
        for ci, (a_ref, x_ref, o_ref, q_ref, reverse) in enumerate(chains):
            for q in range(SCAN_SUB):
                s, p = state[ci * SCAN_SUB + q]
                t = (q + 1) * blk - 1 - k if reverse else q * blk + k
                r = pl.ds(_mo(t * NH, NH), NH)
                a = a_ref[r, :]
                if post:
                    o = x_ref[r, :] + s
                    o_ref[r, :] = o
                    q_ref[r, :] = p
                    new.append((a * o, a * p))
                else:
                    o = a * s + x_ref[r, :]
                    p = a * p
                    o_ref[r, :] = o
                    q_ref[r, :] = p
                    new.append((o, p))
        return tuple(new)

    zero = jnp.zeros((NH, HD), F32)
    one = jnp.ones((NH, HD), F32)
    final = lax.fori_loop(0, blk, step, tuple((zero, one) for _ in range(len(chains) * SCAN_SUB)), unroll=2)
    for ci, (a_ref, x_ref, o_ref, q_ref, reverse) in enumerate(chains):
        carry = carry_ref[ci]
        for q in (range(SCAN_SUB - 1, -1, -1) if reverse else range(SCAN_SUB)):
            rows = pl.ds(q * blk * NH, blk * NH)
            fixed = o_ref[rows, :].reshape(blk, NH, HD) + q_ref[rows, :].reshape(blk, NH, HD) * carry[None]
            o_ref[rows, :] = fixed.reshape(blk * NH, HD)
            s_loc, p_loc = final[ci * SCAN_SUB + q]
            carry = s_loc + p_loc * carry
        carry_ref[ci] = carry


def _lru_fwd(xa, conv_wz, conv_bz, wcat, bcat, lamcat, s_f, s_b, name):
    lx = xa.shape[0]
    n = lx // T

    def body(xm_u, xp_u, xn_u, xm_d, xp_d, xn_d, cw, cb, w_ref, b_ref, lam_ref, su0, sd0,
             xaz_o, xcz_o, af_o, ab_o, hf_o, hb_o, fu, fd, pad, xc_d, x_u, x_d, q_u, q_d, carry):
        i = pl.program_id(0)

        @pl.when(i == 0)
        def _():
            carry[0] = su0[...]
            carry[1] = sd0[...]

        def conv_gates(xm, xp, xn, tile, d, xc_ref, a_ref, x_ref, xaz_ref):
            pmask = jnp.where(tile == 0, 0.0, 1.0)
            nmask = jnp.where(tile == n - 1, 0.0, 1.0)
            for h in range(NH):
                cols = slice(HD * h, HD * h + HD)
                pad[_zrows(h, 8), :] = xp[:, cols] * pmask
                pad[pl.ds(8 * NH + h, T, stride=NH), :] = xm[:, cols]
                pad[pl.ds((T + 8) * NH + h, 8, stride=NH), :] = xn[:, cols] * nmask
            if xaz_ref is not None:
                xaz_ref[...] = pad[pl.ds(8 * NH, T * NH), :]

            def conv_chunk(ci, c_):
                base = pl.multiple_of(ci * (CONV_CHUNK * NH), CONV_CHUNK * NH)
                acc = None
                for k in range(4):
                    sl = pad[pl.ds(base + (7 + k) * NH, CONV_CHUNK * NH), :].reshape(CONV_CHUNK, NH, HD)
                    term = sl * cw[k][None]
                    acc = term if acc is None else acc + term
                acc = acc + cb[...][None]
                xc_ref[pl.ds(base, CONV_CHUNK * NH), :] = acc.reshape(CONV_CHUNK * NH, HD)
                return c_
            lax.fori_loop(0, T // CONV_CHUNK, conv_chunk, 0)

            for h in range(NH):
                xch = xc_ref[_zrows(h, T), :]
                pre = _dot(xch.astype(BF16), w_ref[h, :, 256 * d:256 * d + 256]) + b_ref[h:h + 1, 256 * d:256 * d + 256]
                _, gi, _, _, a, mult = _lru_gate(pre, lam_ref[h:h + 1, :], d, 0)
                a_ref[_zrows(h, T), :] = a
                x_ref[_zrows(h, T), :] = mult * gi * xch

        conv_gates(xm_u, xp_u, xn_u, i, 0, xcz_o, af_o, x_u, xaz_o)
        conv_gates(xm_d, xp_d, xn_d, n - 1 - i, 1, xc_d, ab_o, x_d, None)

        _scan_tile([(af_o, x_u, hf_o, q_u, False), (ab_o, x_d, hb_o, q_d, True)], False, carry)
        fu[...] = carry[0]
        fd[...] = carry[1]

    full = lambda shape: pl.BlockSpec(shape, lambda i: (0,) * len(shape))
    last = lx // 8 - 1
    rev = lambda i: n - 1 - i
    halo_dn = [pl.BlockSpec((T, D), lambda i: (rev(i), 0)),
               pl.BlockSpec((8, D), lambda i: (jnp.maximum(rev(i) * (T // 8) - 1, 0), 0)),
               pl.BlockSpec((8, D), lambda i: (jnp.minimum((rev(i) + 1) * (T // 8), last), 0))]
    st = full((NH, HD))
    in_specs = _halo_specs(lx) + halo_dn + [full((4, NH, HD)), st, full((NH, HD, 4 * HD)), full((NH, 4 * HD)),
                                            full((NH, 2 * HD)), st, st]
    dn = pl.BlockSpec((T * NH, HD), lambda i: (rev(i), 0))
    zs = jax.ShapeDtypeStruct((lx * NH, HD), F32)
    ss = jax.ShapeDtypeStruct((NH, HD), F32)
    zbuf = pltpu.VMEM((T * NH, HD), F32)
    return pl.pallas_call(
        body, name=name, grid=(n,), out_shape=(zs,) * 6 + (ss, ss), in_specs=in_specs,
        out_specs=(ZT, ZT, ZT, dn, ZT, dn, st, st),
        scratch_shapes=[pltpu.VMEM(((T + 16) * NH, HD), F32), zbuf, zbuf, zbuf, zbuf, zbuf,
                        pltpu.VMEM((2, NH, HD), F32)],
        compiler_params=_cp(1, vmem_mb=48),
    )(xa, xa, xa, xa, xa, xa, conv_wz, conv_bz, wcat, bcat, lamcat, s_f, s_b)


def _scan_adjoint(a_up, x_up, a_dn, x_dn, name):
    lx = a_up.shape[0] // NH
    n = lx // T

    def body(au, xu, ad, xd, ou, od, fu, fd, q_u, q_d, carry):
        @pl.when(pl.program_id(0) == 0)
        def _():
            carry[...] = jnp.zeros_like(carry)

        _scan_tile([(au, xu, ou, q_u, False), (ad, xd, od, q_d, True)], True, carry)
        fu[...] = carry[0]
        fd[...] = carry[1]

    up = pl.BlockSpec((T * NH, HD), lambda i: (i, 0))
    dn = pl.BlockSpec((T * NH, HD), lambda i: (n - 1 - i, 0))
    st = pl.BlockSpec((NH, HD), lambda i: (0, 0))
    zs = jax.ShapeDtypeStruct((lx * NH, HD), F32)
    ss = jax.ShapeDtypeStruct((NH, HD), F32)
    zbuf = pltpu.VMEM((T * NH, HD), F32)
    return pl.pallas_call(
        body, name=name, grid=(n,), out_shape=(zs, zs, ss, ss), in_specs=[up, up, dn, dn],
        out_specs=(up, dn, st, st), scratch_shapes=[zbuf, zbuf, pltpu.VMEM((2, NH, HD), F32)],
        compiler_params=_cp(1, vmem_mb=48),
    )(a_up, x_up, a_dn, x_dn)


def _sgu_parts(u, v, lng, lnb, w_ref, bt_ref, mixed_s):
    ug, dug = _gelu_and_grad(u)
    vg, dvg = _gelu_and_grad(v)
    mu = jnp.mean(vg, axis=-1, keepdims=True)
    vc = vg - mu
    rstd = lax.rsqrt(jnp.mean(vc * vc, axis=-1, keepdims=True) + LN_EPS)
    vh = vc * rstd
    vn = (vh * lng + lnb).astype(BF16)
    for g in range(NH):
        cols = slice(HD * g, HD * g + HD)
        mixed_s[:, cols] = _dot(w_ref[g], vn[:, cols]) + bt_ref[:, g:g + 1]
    return ug, dug, dvg, rstd, vh, vn


def _sgu_fwd(u, v, ln_g, ln_b, sgu_w, sgu_bt):
    lx = u.shape[0]
    n = lx // HD

    def body(u_ref, v_ref, g_ref, b_ref, w_ref, bt_ref, y_ref, mixed_s):
        ug, _, _, _, _, _ = _sgu_parts(u_ref[...], v_ref[...], g_ref[...], b_ref[...], w_ref, bt_ref, mixed_s)
        y_ref[...] = ug * mixed_s[...]

    row = pl.BlockSpec((HD, D), lambda i: (i, 0))
    vec = pl.BlockSpec((1, D), lambda i: (0, 0))
    return pl.pallas_call(
        body, name="sgu_fwd", grid=(n,), out_shape=jax.ShapeDtypeStruct((lx, D), F32),
        in_specs=[row, row, vec, vec, pl.BlockSpec((NH, HD, HD), lambda i: (0, 0, 0)),
                  pl.BlockSpec((HD, NH), lambda i: (0, 0))],
        out_specs=row, scratch_shapes=[pltpu.VMEM((HD, D), F32)],
        compiler_params=_cp(1),
    )(u, v, ln_g, ln_b, sgu_w, sgu_bt)


def _sgu_bwd(u, v, dys, ln_g, ln_b, sgu_w, sgu_bt):
    lx = u.shape[0]
    n = lx // HD

    def body(u_ref, v_ref, dy_ref, g_ref, b_ref, w_ref, bt_ref, du_ref, dv_ref, dw_ref, db_ref, dg_ref, dbl_ref,
             mixed_s, dvn_s):
        i = pl.program_id(0)

        @pl.when(i == 0)
        def _():
            dw_ref[...] = jnp.zeros_like(dw_ref)
            db_ref[...] = jnp.zeros_like(db_ref)
            dg_ref[...] = jnp.zeros_like(dg_ref)
            dbl_ref[...] = jnp.zeros_like(dbl_ref)

        lng = g_ref[...]
        ug, dug, dvg, rstd, vh, vn = _sgu_parts(u_ref[...], v_ref[...], lng, b_ref[...], w_ref, bt_ref, mixed_s)
        dys_v = dy_ref[...]
        du_ref[...] = (dys_v * mixed_s[...] * dug).astype(BF16)
        dmix = dys_v * ug
        ones = jnp.ones((8, HD), BF16)
        for g in range(NH):
            cols = slice(HD * g, HD * g + HD)
            dm = dmix[:, cols]
            hi = dm.astype(BF16)
            lo = (dm - hi.astype(F32)).astype(BF16)
            dw_ref[g] += _dot_nt(hi, vn[:, cols])
            db_ref[g:g + 1, :] += (_dot_nt(ones, hi) + _dot_nt(ones, lo))[0:1, :]
            dvn_s[:, cols] = _dot_tn(w_ref[g], hi)
        dvn = dvn_s[...]
        dg_ref[...] += jnp.sum(dvn * vh, axis=0, keepdims=True)
        dbl_ref[...] += jnp.sum(dvn, axis=0, keepdims=True)
        dvh = dvn * lng
        dvg_in = rstd * (dvh - jnp.mean(dvh, axis=-1, keepdims=True)
                         - vh * jnp.mean(dvh * vh, axis=-1, keepdims=True))
        dv_ref[...] = (dvg_in * dvg).astype(BF16)

    row = pl.BlockSpec((HD, D), lambda i: (i, 0))
    vec = pl.BlockSpec((1, D), lambda i: (0, 0))
    wsp = pl.BlockSpec((NH, HD, HD), lambda i: (0, 0, 0))
    bsp = pl.BlockSpec((NH, HD), lambda i: (0, 0))
    return pl.pallas_call(
        body, name="sgu_bwd", grid=(n,),
        out_shape=(jax.ShapeDtypeStruct((lx, D), BF16), jax.ShapeDtypeStruct((lx, D), BF16),
                   jax.ShapeDtypeStruct((NH, HD, HD), F32), jax.ShapeDtypeStruct((NH, HD), F32),
                   jax.ShapeDtypeStruct((1, D), F32), jax.ShapeDtypeStruct((1, D), F32)),
        in_specs=[row, row, row, vec, vec, wsp, pl.BlockSpec((HD, NH), lambda i: (0, 0))],
        out_specs=(row, row, wsp, bsp, vec, vec),
        scratch_shapes=[pltpu.VMEM((HD, D), F32), pltpu.VMEM((HD, D), F32)],
        compiler_params=_cp(1),
    )(u, v, dys, ln_g, ln_b, sgu_w, sgu_bt)


def _out_fwd_bwd(hf_z, hb_z, ga, gb, ys, x, tgt, mods, final_g, w_out_full):
    lx = x.shape[0]
    n = lx // T

    def body(hf_ref, hb_ref, ga_ref, gb_ref, ys_ref, x_ref, t_ref, gx_ref, fg_ref, w_ref,
             loss_ref, dfg_ref, dgx_ref, dxn_ref, y_ref, do_ref, dga_ref, dgb_ref, dyl_ref, dys_ref, yl_s):
        i = pl.program_id(0)

        @pl.when(i == 0)
        def _():
            loss_ref[...] = jnp.zeros_like(loss_ref)
            dfg_ref[...] = jnp.zeros_like(dfg_ref)
            dgx_ref[...] = jnp.zeros_like(dgx_ref)

        for h in range(NH):
            yl_s[:, HD * h:HD * h + HD] = hf_ref[_zrows(h, T), :] + hb_ref[_zrows(h, T), :]
        yl = yl_s[...]
        gav = ga_ref[...]
        gbv = gb_ref[...]
        sa, dsa = _silu_and_grad(gav)
        sb, dsb = _silu_and_grad(gbv)
        ysv = ys_ref[...]
        y_ref[:, 0:D] = (yl * sa).astype(BF16)
        y_ref[:, D:2 * D] = (ysv * sb).astype(BF16)
        o = _dot(y_ref[...], w_ref[...])
        gx = gx_ref[0:1, :]
        xnew = x_ref[...] + gx * o
        r2 = lax.rsqrt(jnp.mean(xnew * xnew, axis=-1, keepdims=True) + NORM_EPS)
        xh = xnew * r2
        fg = fg_ref[...]
        err = xh * fg - t_ref[...]
        loss_ref[...] += 0.5 * jnp.sum(jnp.mean(err * err, axis=-1, keepdims=True), axis=0, keepdims=True)
        dout = err * (1.0 / D)
        dfg_ref[...] += jnp.sum(dout * xh, axis=0, keepdims=True)
        dxh = dout * fg
        dxn = r2 * (dxh - xh * jnp.mean(dxh * xh, axis=-1, keepdims=True))
        dxn_ref[...] = dxn
        dgx_ref[...] += jnp.sum(dxn * o, axis=0, keepdims=True)
        do = (dxn * gx).astype(BF16)
        do_ref[...] = do
        dy = _dot_nt(do, w_ref[...])
        dy1 = dy[:, 0:D]
        dy2 = dy[:, D:2 * D]
        dga_ref[...] = (dy1 * yl * dsa).astype(BF16)
        dgb_ref[...] = (dy2 * ysv * dsb).astype(BF16)
        dys_ref[...] = dy2 * sb
        yl_s[...] = dy1 * sa
        for h in range(NH):
            dyl_ref[_zrows(h, T), :] = yl_s[:, HD * h:HD * h + HD]

    row = pl.BlockSpec((T, D), lambda i: (i, 0))
    vec = pl.BlockSpec((1, D), lambda i: (0, 0))
    in_specs = [ZT, ZT, row, row, row, row, row, pl.BlockSpec((8, D), lambda i: (0, 2)), vec,
                pl.BlockSpec((2 * D, D), lambda i: (0, 0))]
    out_shape = (jax.ShapeDtypeStruct((1, 1), F32), jax.ShapeDtypeStruct((1, D), F32), jax.ShapeDtypeStruct((1, D), F32),
                 jax.ShapeDtypeStruct((lx, D), F32), jax.ShapeDtypeStruct((lx, 2 * D), BF16),
                 jax.ShapeDtypeStruct((lx, D), BF16), jax.ShapeDtypeStruct((lx, D), BF16),
                 jax.ShapeDtypeStruct((lx, D), BF16), jax.ShapeDtypeStruct((lx * NH, HD), F32),
                 jax.ShapeDtypeStruct((lx, D), F32))
    out_specs = (pl.BlockSpec((1, 1), lambda i: (0, 0)), vec, vec, row, pl.BlockSpec((T, 2 * D), lambda i: (i, 0)),
                 row, row, row, ZT, row)
    return pl.pallas_call(
        body, name="out_fwd_bwd", grid=(n,), out_shape=out_shape, in_specs=in_specs, out_specs=out_specs,
        scratch_shapes=[pltpu.VMEM((T, D), F32)],
        compiler_params=_cp(1, vmem_mb=56),
    )(hf_z, hb_z, ga, gb, ys, x, tgt, mods, final_g, w_out_full)


def _lru_gates_bwd(xc_z, lf_z, lb_z, hf_z, hb_z, s_f, s_b, wcat, bcat, lamcat, dw0, db0, dl0, name):
    lx = xc_z.shape[0] // NH
    n = lx // T

    def body(xc_ref, lf_ref, lb_ref, hf_ref, hfp_ref, hb_ref, hbn_ref, sf_ref, sb_ref, w_ref, b_ref, lam_ref,
             dw0_ref, db0_ref, dl0_ref, dxc_ref, dw_ref, db_ref, dl_ref, dpre_s, pf_s, pb_s):
        i = pl.program_id(0)

        @pl.when(i == 0)
        def _():
            dw_ref[...] = dw0_ref[...]
            db_ref[...] = db0_ref[...]
            dl_ref[...] = dl0_ref[...]

        pf_s[pl.ds(0, NH), :] = jnp.where(i == 0, sf_ref[...], hfp_ref[pl.ds(7 * NH, NH), :])
        pf_s[pl.ds(NH, T * NH), :] = hf_ref[...]
        pb_s[pl.ds(0, T * NH), :] = hb_ref[...]
        pb_s[pl.ds(T * NH, NH), :] = jnp.where(i == n - 1, sb_ref[...], hbn_ref[pl.ds(0, NH), :])
        lam_refs = (lf_ref, lb_ref)
        prev = ((pf_s, 0), (pb_s, NH))
        for h in range(NH):
            xch = xc_ref[_zrows(h, T), :]
            xcb = xch.astype(BF16)
            pre = _dot(xcb, w_ref[h]) + b_ref[h:h + 1, :]
            dxc = jnp.zeros((T, HD), F32)
            for d in range(2):
                r, gi, lam, sp, a, mult = _lru_gate(pre, lam_ref[h:h + 1, :], d)
                du = lam_refs[d][_zrows(h, T), :]
                da = du * prev[d][0][pl.ds(prev[d][1] + h, T, stride=NH), :]
                dgi = du * mult * xch
                dxc = dxc + du * mult * gi
                dmult = du * gi * xch
                dla = da * a - dmult * (a * a) / mult
                dr = dla * ((-LRU_C) * sp)
                dsp = jnp.sum(dla * ((-LRU_C) * r), axis=0, keepdims=True)
                dl_ref[h:h + 1, HD * d:HD * d + HD] += dsp * (-_sigmoid(-lam))
                dpre_s[:, 256 * d:256 * d + HD] = dr * r * (1.0 - r)
                dpre_s[:, 256 * d + HD:256 * d + 2 * HD] = dgi * gi * (1.0 - gi)
            dpre = dpre_s[...]
            dpb = dpre.astype(BF16)
            dw_ref[h] += _dot_tn(xcb, dpb)
            db_ref[h:h + 1, :] += jnp.sum(dpre, axis=0, keepdims=True)
            dxc_ref[_zrows(h, T), :] = dxc + _dot_nt(dpb, w_ref[h])

    full = lambda shape: pl.BlockSpec(shape, lambda i: (0,) * len(shape))
    wsp, bsp, lsp = full((NH, HD, 4 * HD)), full((NH, 4 * HD)), full((NH, 2 * HD))
    st = full((NH, HD))
    zh = _zhalo_specs(lx)
    return pl.pallas_call(
        body, name=name, grid=(n,),
        out_shape=(jax.ShapeDtypeStruct((lx * NH, HD), F32), jax.ShapeDtypeStruct((NH, HD, 4 * HD), F32),
                   jax.ShapeDtypeStruct((NH, 4 * HD), F32), jax.ShapeDtypeStruct((NH, 2 * HD), F32)),
        in_specs=[ZT] * 3 + zh[0:2] + [zh[0], zh[2], st, st, wsp, bsp, lsp, wsp, bsp, lsp], out_specs=(ZT, wsp, bsp, lsp),
        scratch_shapes=[pltpu.VMEM((T, 4 * HD), F32), pltpu.VMEM(((T + 1) * NH, HD), F32),
                        pltpu.VMEM(((T + 1) * NH, HD), F32)],
        compiler_params=_cp(1, vmem_mb=48),
    )(xc_z, lf_z, lb_z, hf_z, hf_z, hb_z, hb_z, s_f, s_b, wcat, bcat, lamcat, dw0, db0, dl0)


def _conv_bwd(dxc_z, xa_z, conv_wz, dcw0, dcb0, name):
    lx = dxc_z.shape[0] // NH
    n = lx // T

    def body(dm, dp, dn, xa_ref, cw, dcw0_ref, dcb0_ref, dxa_ref, dcw_ref, dcb_ref, pad, dxa_s):
        i = pl.program_id(0)

        @pl.when(i == 0)
        def _():
            dcw_ref[...] = dcw0_ref[...]
            dcb_ref[...] = dcb0_ref[...]

        pmask = jnp.where(i == 0, 0.0, 1.0)
        nmask = jnp.where(i == n - 1, 0.0, 1.0)
        pad[pl.ds(0, 8 * NH), :] = dp[...] * pmask
        pad[pl.ds(8 * NH, T * NH), :] = dm[...]
        pad[pl.ds((T + 8) * NH, 8 * NH), :] = dn[...] * nmask

        def chunk(ci, carry):
            base = pl.multiple_of(ci * (CONV_CHUNK * NH), CONV_CHUNK * NH)
            xav = xa_ref[pl.ds(base, CONV_CHUNK * NH), :].reshape(CONV_CHUNK, NH, HD)
            acc = None
            for k in range(4):
                sl = pad[pl.ds(base + (9 - k) * NH, CONV_CHUNK * NH), :].reshape(CONV_CHUNK, NH, HD)
                term = sl * cw[k][None]
                acc = term if acc is None else acc + term
                dcw_ref[k] += jnp.sum(sl * xav, axis=0)
                if k == 1:
                    dcb_ref[...] += jnp.sum(sl, axis=0)
            dxa_s[pl.ds(base, CONV_CHUNK * NH), :] = acc.reshape(CONV_CHUNK * NH, HD)
            return carry
        lax.fori_loop(0, T // CONV_CHUNK, chunk, 0)
        for h in range(NH):
            dxa_ref[:, HD * h:HD * h + HD] = dxa_s[_zrows(h, T), :].astype(BF16)

    full = lambda shape: pl.BlockSpec(shape, lambda i: (0,) * len(shape))
    return pl.pallas_call(
        body, name=name, grid=(n,),
        out_shape=(jax.ShapeDtypeStruct((lx, D), BF16), jax.ShapeDtypeStruct((4, NH, HD), F32),
                   jax.ShapeDtypeStruct((NH, HD), F32)),
        in_specs=_zhalo_specs(lx) + [ZT, full((4, NH, HD)), full((4, NH, HD)), full((NH, HD))],
        out_specs=(pl.BlockSpec((T, D), lambda i: (i, 0)), full((4, NH, HD)), full((NH, HD))),
        scratch_shapes=[pltpu.VMEM(((T + 16) * NH, HD), F32), pltpu.VMEM((T * NH, HD), F32)],
        compiler_params=_cp(1, vmem_mb=48),
    )(dxc_z, dxc_z, dxc_z, xa_z, conv_wz, dcw0, dcb0)


def _proj_bwd(dzs, x, dxn, mods, mrow, norm_g, w_full, dng0, name):
    lx = x.shape[0]
    tp = min(lx, TP)
    n = lx // tp
    nk = len(dzs)
    has_x = dxn is not None

    def body(*refs):
        dz_refs = refs[:nk]
        w_refs = refs[nk:2 * nk]
        x_ref, sc_ref, ng_ref, dng0_ref = refs[2 * nk:2 * nk + 4]
        rest = refs[2 * nk + 4:]
        if has_x:
            dxn_ref, gx_ref, dng_ref, dsc_ref, dsh_ref = rest
        else:
            dng_ref, dsc_ref, dsh_ref = rest
        i = pl.program_id(0)

        @pl.when(i == 0)
        def _():
            dng_ref[...] = dng0_ref[...]
            dsc_ref[...] = jnp.zeros_like(dsc_ref)
            dsh_ref[...] = jnp.zeros_like(dsh_ref)

        dhn = _dot_nt(dz_refs[0][...], w_refs[0][...])
        for k in range(1, nk):
            dhn = dhn + _dot_nt(dz_refs[k][...], w_refs[k][...])
        xv = x_ref[...]
        r = lax.rsqrt(jnp.mean(xv * xv, axis=-1, keepdims=True) + NORM_EPS)
        xn = xv * r
        ng = ng_ref[...]
        sc1 = 1.0 + sc_ref[mrow:mrow + 1, :]
        t = dhn * xn
        dng_ref[...] += jnp.sum(t * sc1, axis=0, keepdims=True)
        dsc_ref[...] += jnp.sum(t * ng, axis=0, keepdims=True)
        dsh_ref[...] += jnp.sum(dhn, axis=0, keepdims=True)
        if has_x:
            dxh = dhn * (ng * sc1)
            gx_ref[...] = dxn_ref[...] + r * (dxh - xn * jnp.mean(dxh * xn, axis=-1, keepdims=True))

    row = pl.BlockSpec((tp, D), lambda i: (i, 0))
    vec = pl.BlockSpec((1, D), lambda i: (0, 0))
    in_specs = [row] * nk + [pl.BlockSpec((D, D), lambda i, k=k: (0, k)) for k in range(nk)]
    in_specs += [row, pl.BlockSpec((8, D), lambda i: (0, 1)), vec, vec]
    args = list(dzs) + [w_full] * nk + [x, mods, norm_g, dng0]
    vs = jax.ShapeDtypeStruct((1, D), F32)
    out_shape, out_specs = (vs, vs, vs), (vec, vec, vec)
    if has_x:
        in_specs.append(row)
        args.append(dxn)
        out_shape = (jax.ShapeDtypeStruct((lx, D), F32),) + out_shape
        out_specs = (row,) + out_specs
    return pl.pallas_call(
        body, name=name, grid=(n,), out_shape=out_shape, in_specs=in_specs, out_specs=out_specs,
        compiler_params=_cp(1, vmem_mb=56),
    )(*args)


def _tn_matmul(a, bs, extra, name):
    lx, m = a.shape
    tm = min(lx, 1024)
    n = lx // tm
    widths = [b.shape[1] for b in bs]
    nb = len(bs)

    def body(a_ref, *rest):
        b_refs = rest[:nb]
        rest = rest[nb:]
        if extra is not None:
            ea_ref, eb_ref = rest[:2]
            rest = rest[2:]
        out_ref, acc = rest
        i = pl.program_id(0)
        av = a_ref[...]
        off = 0
        for k in range(nb):
            cols = slice(off, off + widths[k])
            part = _dot_tn(av, b_refs[k][...])

            @pl.when(i == 0)
            def _():
                acc[:, cols] = part

            @pl.when(i > 0)
            def _():
                acc[:, cols] += part
            off += widths[k]

        @pl.when(i == n - 1)
        def _():
            if extra is not None:
                acc[:, 0:widths[0]] += _dot_tn(ea_ref[...], eb_ref[...])
            pltpu.sync_copy(acc, out_ref)

    in_specs = [pl.BlockSpec((tm, m), lambda i: (i, 0))] + [pl.BlockSpec((tm, w), lambda i: (i, 0)) for w in widths]
    args = [a] + list(bs)
    if extra is not None:
        in_specs += [VMEM, VMEM]
        args += list(extra)
    return pl.pallas_call(
        body, name=name, grid=(n,), out_shape=jax.ShapeDtypeStruct((m, sum(widths)), F32),
        in_specs=in_specs, out_specs=ANY, scratch_shapes=[pltpu.VMEM((m, sum(widths)), F32)],
        compiler_params=_cp(1, vmem_mb=56),
    )(*args)


def _adam_math(w, g, m, v):
    m = ADAM_B1 * m + (1.0 - ADAM_B1) * g
    v = ADAM_B2 * v + (1.0 - ADAM_B2) * (g * g)
    m_hat = m / (1.0 - ADAM_B1 ** ADAM_STEP)
    v_hat = v / (1.0 - ADAM_B2 ** ADAM_STEP)
    delta = -ADAM_LR * (m_hat / (jnp.sqrt(v_hat) + ADAM_EPS) + ADAM_WD * w)
    return delta, m, v


def _adam_big(w, g, m, v, name):
    rows, cols = w.shape
    tr = 256

    def body(w_ref, g_ref, m_ref, v_ref, d_o, m_o, v_o):
        d, mm, vv = _adam_math(w_ref[...], g_ref[...], m_ref[...], v_ref[...])
        d_o[...] = d
        m_o[...] = mm
        v_o[...] = vv

    blk = pl.BlockSpec((tr, cols), lambda i: (i, 0))
    s = jax.ShapeDtypeStruct((rows, cols), F32)
    return pl.pallas_call(
        body, name=name, grid=(rows // tr,), out_shape=(s, s, s), in_specs=[blk] * 4, out_specs=(blk,) * 3,
        compiler_params=_cp(1, vmem_mb=48),
    )(w, g, m, v)


def _adam_small(items):
    ni = len(items)

    def body(*refs):
        ins, outs = refs[:4 * ni], refs[4 * ni:7 * ni]
        bufs_in, bufs_out = refs[7 * ni:11 * ni], refs[11 * ni:14 * ni]
        sem_in, sem_out = refs[14 * ni], refs[14 * ni + 1]
        loads = [pltpu.make_async_copy(ins[q], bufs_in[q], sem_in.at[q]) for q in range(4 * ni)]
        for cp in loads:
            cp.start()
        stores = []
        for k in range(ni):
            for q in range(4):
                loads[4 * k + q].wait()
            w_b, g_b, m_b, v_b = bufs_in[4 * k:4 * k + 4]
            res = _adam_math(w_b[...], g_b[...], m_b[...], v_b[...])
            for q in range(3):
                bufs_out[3 * k + q][...] = res[q]
                cp = pltpu.make_async_copy(bufs_out[3 * k + q], outs[3 * k + q], sem_out.at[3 * k + q])
                cp.start()
                stores.append(cp)
        for cp in stores:
            cp.wait()

    flat = [a for it in items for a in it]
    out_shape = tuple(jax.ShapeDtypeStruct(it[0].shape, F32) for it in items for _ in range(3))
    scratch = [pltpu.VMEM(a.shape, F32) for a in flat] + [pltpu.VMEM(s.shape, F32) for s in out_shape]
    scratch += [pltpu.SemaphoreType.DMA((4 * ni,)), pltpu.SemaphoreType.DMA((3 * ni,))]
    res = pl.pallas_call(
        body, name="adam_small", out_shape=out_shape, in_specs=[HBM] * (4 * ni), out_specs=(HBM,) * (3 * ni),
        scratch_shapes=scratch, compiler_params=_cp(vmem_mb=40),
    )(*flat)
    return [tuple(res[3 * k:3 * k + 3]) for k in range(ni)]


def kernel(x, c, ctx, c_ctx, ada_w, ada_b, norm_g, w_in, conv_w, conv_b, lru_wa, lru_ba, lru_wx, lru_bx, lru_lambda, sgu_ln_g, sgu_ln_b, sgu_w, sgu_b, w_out, final_g, loss_target, m_c_ctx, m_ada_w, m_ada_b, m_norm_g, m_w_in, m_conv_w, m_conv_b, m_lru_wa, m_lru_ba, m_lru_wx, m_lru_bx, m_lru_lambda, m_sgu_ln_g, m_sgu_ln_b, m_sgu_w, m_sgu_b, m_w_out, m_final_g, v_c_ctx, v_ada_w, v_ada_b, v_norm_g, v_w_in, v_conv_w, v_conv_b, v_lru_wa, v_lru_ba, v_lru_wx, v_lru_bx, v_lru_lambda, v_sgu_ln_g, v_sgu_ln_b, v_sgu_w, v_sgu_b, v_w_out, v_final_g):
    ix, iy, ic = lax.axis_index("x"), lax.axis_index("y"), lax.axis_index("c")
    chip = 2 * ix + iy
    dev = 2 * chip + ic
    lx = x.shape[1]
    lc = ctx.shape[1]

    smalls = jnp.concatenate([conv_w[0], lru_lambda[0], jnp.zeros((10, 256), F32)], axis=0)
    c_ctx2 = c_ctx.reshape(1, D)
    ada_b_j = lax.dynamic_slice(ada_b, (0, 768 * chip), (1, 768))
    mods, c_slots, sm_all, w_in_full, wo_land, ada_land = _gather_in(c, c_ctx2, ada_w[0], ada_b_j, w_in[0], w_out[0],
                                                                     smalls)
    wo_ss, wo_rs, ada_ss, ada_rs, wo_land, ada_land, token = _late_gather_start(wo_land, ada_land)
    mods = mods + token[0:1, 0:1]
    sm3 = sm_all.reshape(NCHIP, 16, 256)
    conv_w_full = sm3[:, 0:4, :].transpose(1, 0, 2).reshape(4, D)
    lam_full = sm3[:, 4:6, :].transpose(1, 0, 2).reshape(2, D)
    conv_wz = conv_w_full.reshape(4, NH, HD)
    conv_bz = conv_b.reshape(NH, HD)
    lamcat = lam_full.reshape(2, NH, HD).transpose(1, 0, 2).reshape(NH, 2 * HD)
    wa, wx, ba, bx = lru_wa[0], lru_wx[0], lru_ba[0], lru_bx[0]
    wcat = jnp.concatenate([wa[0], wx[0], wa[1], wx[1]], axis=-1).astype(BF16)
    bcat = jnp.concatenate([ba[0], bx[0], ba[1], bx[1]], axis=-1)
    sgu_wb = sgu_w[0].astype(BF16)
    sgu_bt = sgu_b[0].T
    final_g2 = final_g.reshape(1, D)

    zero_s = jnp.zeros((NH, HD), F32)
    hn_c, xa_c = _proj(ctx[0], mods, 1, norm_g, w_in_full, 1, "proj_ctx")
    xaz_c, xcz_c, af_c, ab_c, hf_c, hb_c, hf0, hb0 = _lru_fwd(xa_c, conv_wz, conv_bz, wcat, bcat, lamcat,
                                                               zero_s, zero_s, "lru_fwd_ctx")

    hn, xa, ga, u, v, gb = _proj(x[0], mods, 0, norm_g, w_in_full, 5, "proj")
    xaz, xcz, af, ab, hf, hb, _, _ = _lru_fwd(xa, conv_wz, conv_bz, wcat, bcat, lamcat, hf0, hb0, "lru_fwd")
    ys = _sgu_fwd(u, v, sgu_ln_g, sgu_ln_b, sgu_wb, sgu_bt)

    w_out_full = _late_gather_wait(wo_land, wo_ss, wo_rs, "w_out", ys, "late_gather_wait_w_out")
    (loss_part, dfg, dgx, dxn, y, do, dga, dgb, dyl_z, dys) = _out_fwd_bwd(
        hf, hb, ga, gb, ys, x[0], loss_target[0], mods, final_g2, w_out_full)
    g_w_out_part = _tn_matmul(y, [do], None, "grad_w_out")

    du, dv, d_sgu_w, d_sgu_b, d_ln_g, d_ln_b = _sgu_bwd(u, v, dys, sgu_ln_g, sgu_ln_b, sgu_wb, sgu_bt)
    lb, lf, dh0b, dh0f = _scan_adjoint(ab, dyl_z, af, dyl_z, "scan_adj")
    zw = jnp.zeros((NH, HD, 4 * HD), F32)
    zb = jnp.zeros((NH, 4 * HD), F32)
    zl = jnp.zeros((NH, 2 * HD), F32)
    dxc_z, dwc, dbc, dlc = _lru_gates_bwd(xcz, lf, lb, hf, hb, hf0, hb0, wcat, bcat, lamcat, zw, zb, zl,
                                          "lru_gates_bwd")
    dxa, dcw, dcb = _conv_bwd(dxc_z, xaz, conv_wz, jnp.zeros((4, NH, HD), F32), zero_s, "conv_bwd")

    zc = jnp.zeros((lc * NH, HD), F32)
    dhf_c = lax.dynamic_update_slice(zc, dh0f, ((lc - 1) * NH, 0))
    dhb_c = lax.dynamic_update_slice(zc, dh0b, (0, 0))
    lb_c, lf_c, _, _ = _scan_adjoint(ab_c, dhb_c, af_c, dhf_c, "scan_adj_ctx")
    dxc_zc, dwc, dbc, dlc = _lru_gates_bwd(xcz_c, lf_c, lb_c, hf_c, hb_c, zero_s, zero_s, wcat, bcat, lamcat, dwc, dbc, dlc,
                                           "lru_gates_bwd_ctx")
    dxa_c, dcw, dcb = _conv_bwd(dxc_zc, xaz_c, conv_wz, dcw, dcb, "conv_bwd_ctx")

    dzs = [dxa, dga, du, dv, dgb]
    grad_x, dng, dsc_x, dsh_x = _proj_bwd(dzs, x[0], dxn, mods, 0, norm_g, w_in_full, jnp.zeros((1, D), F32), "proj_bwd")
    dng, dsc_c, dsh_c = _proj_bwd([dxa_c], ctx[0], None, mods, 1, norm_g, w_in_full, dng, "proj_bwd_ctx")

    dmx = jnp.concatenate([dsh_x, dsc_x, dgx], axis=0)
    dmc = jnp.concatenate([dsh_c, dsc_c, jnp.zeros((1, D), F32)], axis=0)
    lp = loss_part[0, 0]
    lp1 = lax.reduce_precision(lp, 8, 7)
    lp2 = lax.reduce_precision(lp - lp1, 8, 7)
    lp3 = lax.reduce_precision(lp - lp1 - lp2, 8, 7)
    loss_row = jnp.pad(jnp.stack([lp1, lp2, lp3]).reshape(1, 3), ((0, 0), (0, D - 3)))
    slot = jnp.concatenate([dmx, loss_row], axis=0)
    slots = lax.dynamic_update_slice(jnp.zeros((32, D), F32), slot, (4 * dev, 0))
    vecs = jnp.concatenate([dfg, dng, dcb.reshape(1, D), d_ln_g, d_ln_b, dcw.reshape(4, D), dmc,
                            jnp.zeros((4, D), F32), slots], axis=0)
    d_sgu_w4 = d_sgu_w.reshape(4, 256, HD).transpose(1, 0, 2).reshape(256, 4 * HD)
    pad8 = lambda a: jnp.pad(a, ((0, 8 - a.shape[0]), (0, 4 * HD - a.shape[1])))
    pack = jnp.concatenate([dwc.reshape(NH * HD, 4 * HD), pad8(dbc), pad8(dlc), d_sgu_w4, pad8(d_sgu_b),
                            vecs.reshape(96, 4 * HD), jnp.zeros((8, 4 * HD), F32)], axis=0)
    g_w_in, g_w_out, tot = _grads_reduce(hn, dzs, hn_c, dxa_c, g_w_out_part, pack)

    g_wc = tot[0:1024].reshape(NH, HD, 4 * HD)
    g_bc = tot[1024:1032]
    g_lc = tot[1032:1040, 0:2 * HD]
    g_sgu_w = tot[1040:1296].reshape(256, 4, HD).transpose(1, 0, 2).reshape(NH, HD, HD)
    g_sgu_b = tot[1296:1304, 0:HD]
    tv = tot[1304:1400].reshape(48, D)
    g_final_g, g_norm_g, g_conv_b, g_ln_g, g_ln_b = tv[0:1], tv[1:2], tv[2:3], tv[3:4], tv[4:5]
    g_conv_w_full = tv[5:9]
    dmc_tot = tv[9:12].reshape(1, 3 * D)
    slots_all = tv[16:48].reshape(8, 4, D)
    dmx_all = slots_all[:, 0:3, :].reshape(8, 3 * D)
    c_all = c_slots.reshape(8, 8, D)[:, 0, :]
    g_lru_wa = jnp.stack([g_wc[:, :, 0:HD], g_wc[:, :, 2 * HD:3 * HD]])
    g_lru_wx = jnp.stack([g_wc[:, :, HD:2 * HD], g_wc[:, :, 3 * HD:4 * HD]])
    g_lru_ba = jnp.stack([g_bc[:, 0:HD], g_bc[:, 2 * HD:3 * HD]])
    g_lru_bx = jnp.stack([g_bc[:, HD:2 * HD], g_bc[:, 3 * HD:4 * HD]])
    g_lam_full = jnp.stack([g_lc[:, 0:HD], g_lc[:, HD:2 * HD]]).reshape(2, D)
    g_conv_w = lax.dynamic_slice(g_conv_w_full, (0, 256 * chip), (4, 256))
    g_lam = lax.dynamic_slice(g_lam_full, (0, 256 * chip), (2, 256))
    dmx_all_j = lax.dynamic_slice(dmx_all, (0, 768 * chip), (8, 768))
    dmc_j = lax.dynamic_slice(dmc_tot, (0, 768 * chip), (1, 768))
    ada_full = _late_gather_wait(ada_land, ada_ss, ada_rs, "ada_w", tot, "late_gather_wait_ada_w")
    g_ada_w, g_ada_b, g_c_ctx = _ada_bwd(c_all, dmx_all_j, dmc_j, dmx_all, dmc_tot, c_ctx2, ada_full)

    big = {
        "ada_w": _adam_big(ada_w[0], g_ada_w, m_ada_w[0], v_ada_w[0], "adam_ada_w"),
        "w_in": _adam_big(w_in[0], g_w_in, m_w_in[0], v_w_in[0], "adam_w_in"),
        "w_out": _adam_big(w_out[0], g_w_out, m_w_out[0], v_w_out[0], "adam_w_out"),
    }
    small_in = {
        "c_ctx": (c_ctx, g_c_ctx, m_c_ctx, v_c_ctx, (1, D)),
        "ada_b": (ada_b, g_ada_b, m_ada_b, v_ada_b, (1, 3 * D)),
        "norm_g": (norm_g, g_norm_g, m_norm_g, v_norm_g, (1, D)),
        "conv_w": (conv_w, g_conv_w, m_conv_w, v_conv_w, (4, 256)),
        "conv_b": (conv_b, g_conv_b, m_conv_b, v_conv_b, (1, D)),
        "lru_wa": (lru_wa, g_lru_wa, m_lru_wa, v_lru_wa, (2 * NH * HD, HD)),
        "lru_ba": (lru_ba, g_lru_ba, m_lru_ba, v_lru_ba, (2 * NH, HD)),
        "lru_wx": (lru_wx, g_lru_wx, m_lru_wx, v_lru_wx, (2 * NH * HD, HD)),
        "lru_bx": (lru_bx, g_lru_bx, m_lru_bx, v_lru_bx, (2 * NH, HD)),
        "lru_lambda": (lru_lambda, g_lam, m_lru_lambda, v_lru_lambda, (2, 256)),
        "sgu_ln_g": (sgu_ln_g, g_ln_g, m_sgu_ln_g, v_sgu_ln_g, (1, D)),
        "sgu_ln_b": (sgu_ln_b, g_ln_b, m_sgu_ln_b, v_sgu_ln_b, (1, D)),
        "sgu_w": (sgu_w, g_sgu_w, m_sgu_w, v_sgu_w, (NH * HD, HD)),
        "sgu_b": (sgu_b, g_sgu_b, m_sgu_b, v_sgu_b, (NH, HD)),
        "final_g": (final_g, g_final_g, m_final_g, v_final_g, (1, D)),
    }
    names_small = list(small_in)
    res_small = _adam_small([tuple(a.reshape(small_in[k][4]) for a in small_in[k][:4]) for k in names_small])
    full_shapes = {"ada_w": ada_w.shape, "w_in": w_in.shape, "w_out": w_out.shape}
    grads, deltas, new_m, new_v = {}, {}, {}, {}
    for k in ("ada_w", "w_in", "w_out"):
        g = {"ada_w": g_ada_w, "w_in": g_w_in, "w_out": g_w_out}[k]
        grads[k] = g.reshape(full_shapes[k])
        deltas[k], new_m[k], new_v[k] = (a.reshape(full_shapes[k]) for a in big[k])
    for k, res in zip(names_small, res_small):
        shape = small_in[k][0].shape
        grads[k] = small_in[k][1].reshape(shape)
        deltas[k], new_m[k], new_v[k] = (a.reshape(shape) for a in res)

    loss = jnp.sum(slots_all[:, 3, 0:3])
    order = ["c_ctx", "ada_w", "ada_b", "norm_g", "w_in", "conv_w", "conv_b", "lru_wa", "lru_ba", "lru_wx", "lru_bx",
             "lru_lambda", "sgu_ln_g", "sgu_ln_b", "sgu_w", "sgu_b", "w_out", "final_g"]
    return (loss, grad_x.reshape(x.shape), *[grads[k] for k in order], *[deltas[k] for k in order],
            *[new_m[k] for k in order], *[new_v[k] for k in order])
```

```python
import functools

import jax
import jax.numpy as jnp
from jax import lax
from jax.experimental import pallas as pl
from jax.experimental.pallas import tpu as pltpu

F32 = jnp.float32
BF16 = jnp.bfloat16

D = 1024
NH = 8
HD = 128
NCHIP = 4
T = 256
TP = 512
NORM_EPS = 1e-6
LN_EPS = 1e-5
LRU_C = 8.0
ADAM_LR = 0.001
ADAM_B1 = 0.9
ADAM_B2 = 0.999
ADAM_EPS = 1e-08
ADAM_WD = 0.01
ADAM_STEP = 10

VMEM = pl.BlockSpec(memory_space=pltpu.VMEM)
ANY = pl.BlockSpec(memory_space=pl.ANY)
MESH = pl.DeviceIdType.MESH


def _cp(n_grid=0, vmem_mb=None):
    kw = {}
    if n_grid:
        kw["dimension_semantics"] = ("arbitrary",) * n_grid
    if vmem_mb:
        kw["vmem_limit_bytes"] = vmem_mb << 20
    return pltpu.CompilerParams(**kw)


def _sigmoid(x):
    return 1.0 / (1.0 + jnp.exp(-x))


def _silu_and_grad(x):
    s = _sigmoid(x)
    return x * s, s * (1.0 + x * (1.0 - s))


_GELU_K = 0.7978845608028654
_GELU_C = 0.044715


def _gelu_and_grad(x):
    x2 = x * x
    th = jnp.tanh(_GELU_K * (x + _GELU_C * x * x2))
    g = 0.5 * x * (1.0 + th)
    dg = 0.5 * (1.0 + th) + 0.5 * x * (1.0 - th * th) * (_GELU_K * (1.0 + 3.0 * _GELU_C * x2))
    return g, dg


def _softplus(x):
    return jnp.maximum(x, 0.0) + jnp.log1p(jnp.exp(-jnp.abs(x)))


def _lru_gate(pre, lam_row, d, off=None):
    off = 256 * d if off is None else off
    r = _sigmoid(pre[:, off:off + HD])
    gi = _sigmoid(pre[:, off + HD:off + 2 * HD])
    lam = lam_row[:, HD * d:HD * d + HD]
    sp = _softplus(-lam)
    la = (-LRU_C) * r * sp
    a = jnp.exp(la)
    x2 = 2.0 * la
    m2 = jnp.where(x2 > -1e-3, -x2 * (1.0 + 0.5 * x2), 1.0 - a * a)
    mult = jnp.sqrt(m2)
    return r, gi, lam, sp, a, mult


def _dot(a, b):
    return jnp.dot(a, b, preferred_element_type=F32)


def _dot_tn(a, b):
    return lax.dot_general(a, b, (((0,), (0,)), ((), ())), preferred_element_type=F32)


def _dot_nt(a, b):
    return lax.dot_general(a, b, (((1,), (1,)), ((), ())), preferred_element_type=F32)


def _mo(v, m):
    return v if isinstance(v, int) else pl.multiple_of(v, m)


def _zrows(h, n):
    return pl.ds(h, n, stride=NH)


def _gather_in(c, c_ctx, ada_w, ada_b_j, w_in, w_out, smalls):
    nch = [1, 4]
    wrows = lambda cc, q: (pl.ds(_mo(512 * cc, 16), 512) if q is None
                           else pl.ds(_mo(512 * cc + (512 // nch[1]) * q, 16), 512 // nch[1]))
    specs = [
        ((64, 256), F32, lambda r, jj, cc, q=None: r.at[pl.ds(_mo(16 * jj + 8 * cc, 8), 8), :]),
        ((D, 5120), BF16, lambda r, jj, cc, q=None: r.at[wrows(cc, q), pl.ds(_mo(1280 * jj, 128), 1280)]),
    ]
    halves = [lambda r, cc, q=None: r.at[pl.ds(_mo(8 * cc, 8), 8), :],
              lambda r, cc, q=None: r.at[wrows(cc, q), :]]
    na = len(specs)
    sem_base = [0, 6 * nch[0]]
    sidx = lambda a, q, k: sem_base[a] + 6 * q + k
    n_tiny = 6 * sum(nch)
    n_sem = n_tiny + 10

    def body(c_ref, cc_ref, ada_ref, adab_ref, win_ref, wout_ref, sm_ref,
             mods_o, call_o, sm_o, win_o, wol_o, adal_o, s_win, s_ada, s_wout, cslot, lhs, mbuf,
             send_sems, recv_sems, local_sems):
        x, y, c = lax.axis_index("x"), lax.axis_index("y"), lax.axis_index("c")
        j = 2 * x + y
        dev = 2 * j + c
        sib = (x, y, 1 - c)
        chips = [(1 - x, y), (x, 1 - y), (1 - x, 1 - y)]
        cj = [2 * cx + cy for cx, cy in chips]
        outs = [sm_o, win_o]
        srcs = [sm_ref, s_win]

        def copy(idx, src, dst, to):
            return pltpu.make_async_remote_copy(src_ref=src, dst_ref=dst, send_sem=send_sems.at[idx],
                                                recv_sem=recv_sems.at[idx], device_id=to, device_id_type=MESH)

        sends = []

        def start(cp):
            cp.start()
            sends.append(cp)

        cslot[...] = jnp.zeros_like(cslot)
        cslot[0:1, :] = c_ref[...]
        my_slot = pl.ds(_mo(8 * dev, 8), 8)
        others = [sib] + [(*chips[k], c) for k in range(3)] + [(*chips[k], 1 - c) for k in range(3)]
        other_dev = [dev + 1 - 2 * c] + [2 * cj[k] + c for k in range(3)] + [2 * cj[k] + 1 - c for k in range(3)]
        base = n_tiny
        for r in range(7):
            start(copy(base + r, cslot, call_o.at[my_slot, :], others[r]))
        call_o[my_slot, :] = cslot[...]

        s_win[...] = win_ref[...].astype(BF16)
        local = []
        for a in range(na):
            for cc in range(2):
                lc = pltpu.make_async_copy(halves[a](srcs[a], cc), specs[a][2](outs[a], j, cc), local_sems.at[2 * a + cc])
                lc.start()
                local.append(lc)
        for a in range(na):
            for q in range(nch[a]):
                for k in range(2):
                    start(copy(sidx(a, q, k), halves[a](srcs[a], c, q), specs[a][2](outs[a], j, c, q), (*chips[k], c)))
        s_wout[...] = wout_ref[...].astype(BF16)
        s_ada[...] = ada_ref[...].astype(BF16)
        for q, (src, dst) in enumerate([(s_wout, wol_o.at[pl.ds(_mo(512 * j, 16), 512), :]),
                                        (s_ada, adal_o.at[:, pl.ds(_mo(768 * j, 128), 768)])]):
            lc = pltpu.make_async_copy(src, dst, local_sems.at[2 * na + q])
            lc.start()
            local.append(lc)

        for r in range(7):
            slot = call_o.at[pl.ds(_mo(8 * other_dev[r], 8), 8), :]
            copy(base + r, slot, slot, sib).wait_recv()
        lhs[...] = jnp.zeros_like(lhs)
        for b in range(8):
            cv = call_o[8 * b:8 * b + 1, :]
            lhs[b:b + 1, :] = cv * _sigmoid(cv)
        cv = cc_ref[...]
        lhs[8:9, :] = cv * _sigmoid(cv)
        mbuf[j] = _dot(lhs[...].astype(BF16), s_ada[...]) + adab_ref[...]
        for k in range(3):
            start(copy(base + 7 + k, mbuf.at[j], mbuf.at[j], (*chips[k], c)))
        for k in range(3):
            copy(base + 7 + k, mbuf.at[cj[k]], mbuf.at[cj[k]], sib).wait_recv()
        mods_o[...] = jnp.zeros_like(mods_o)
        for jj in range(NCHIP):
            mods_o[0:1, 768 * jj:768 * jj + 768] = mbuf[jj, pl.ds(dev, 1), :]
            mods_o[1:2, 768 * jj:768 * jj + 768] = mbuf[jj, 8:9, :]

        kx = [1 - x, x, 1 - x]
        ky = [y, 1 - y, 1 - y]
        pick = lambda k, lst: jnp.where(k == 0, lst[0], jnp.where(k == 1, lst[1], lst[2]))
        for a in range(na):
            for q in range(nch[a]):
                for step, k in enumerate([c, 1 - c]):
                    reg = specs[a][2](outs[a], pick(k, cj), c, q)
                    copy(sidx(a, q, k), reg, reg, sib).wait_recv()
                    if step == 0:
                        start(copy(sidx(a, q, 2), reg, reg, (pick(1 - c, kx), pick(1 - c, ky), c)))
                    start(copy(sidx(a, q, 3 + k), reg, reg, sib))
        for a in range(na):
            for q in range(nch[a]):
                reg = specs[a][2](outs[a], cj[2], c, q)
                copy(sidx(a, q, 2), reg, reg, sib).wait_recv()
                start(copy(sidx(a, q, 5), reg, reg, sib))
        for a in range(na):
            for q in range(nch[a]):
                for k in range(3):
                    reg = specs[a][2](outs[a], cj[k], 1 - c, q)
                    copy(sidx(a, q, 3 + k), reg, reg, sib).wait_recv()
        for cp in sends:
            cp.wait_send()
        for lc in local:
            lc.wait()

    out_shape = (jax.ShapeDtypeStruct((8, 3 * D), F32), jax.ShapeDtypeStruct((64, D), F32),
                 jax.ShapeDtypeStruct(specs[0][0], F32), jax.ShapeDtypeStruct(specs[1][0], BF16),
                 jax.ShapeDtypeStruct((2048, D), BF16), jax.ShapeDtypeStruct((D, 3 * D), BF16))
    return pl.pallas_call(
        body, name="gather_in", out_shape=out_shape,
        in_specs=[VMEM] * 7, out_specs=(VMEM, VMEM, VMEM, ANY, ANY, ANY),
        scratch_shapes=[pltpu.VMEM((D, 1280), BF16), pltpu.VMEM((D, 768), BF16), pltpu.VMEM((512, D), BF16),
                        pltpu.VMEM((8, D), F32), pltpu.VMEM((16, D), F32), pltpu.VMEM((NCHIP, 16, 768), F32),
                        pltpu.SemaphoreType.DMA((n_sem,)), pltpu.SemaphoreType.DMA((n_sem,)),
                        pltpu.SemaphoreType.DMA((2 * na + 2,))],
        compiler_params=_cp(vmem_mb=56),
    )(c, c_ctx, ada_w, ada_b_j, w_in, w_out, smalls)


HBM = pl.BlockSpec(memory_space=pltpu.HBM)
SEM = pl.BlockSpec(memory_space=pltpu.SEMAPHORE)


def _late_gather_regions(x, y, c):
    chips = [(1 - x, y), (x, 1 - y), (1 - x, 1 - y)]
    wo_reg = lambda r, jj, cc: r.at[pl.ds(_mo(512 * jj + 256 * cc, 16), 256), :]
    ada_reg = lambda r, jj, cc: r.at[pl.ds(_mo(512 * cc, 16), 512), pl.ds(_mo(768 * jj, 128), 768)]
    return chips, wo_reg, ada_reg


def _late_gather_start(wo_land, ada_land):
    def body(wol_ref, adal_ref, wo_ss, wo_rs, ada_ss, ada_rs, wol_thru, adal_thru, token):
        x, y, c = lax.axis_index("x"), lax.axis_index("y"), lax.axis_index("c")
        j = 2 * x + y
        chips, wo_reg, ada_reg = _late_gather_regions(x, y, c)
        for k in range(3):
            for cc in range(2):
                pltpu.make_async_remote_copy(src_ref=wo_reg(wol_ref, j, c), dst_ref=wo_reg(wol_ref, j, c),
                                             send_sem=wo_ss.at[2 * k + cc], recv_sem=wo_rs.at[2 * k + c],
                                             device_id=(*chips[k], cc), device_id_type=MESH).start()
        for k in range(3):
            for cc in range(2):
                pltpu.make_async_remote_copy(src_ref=ada_reg(adal_ref, j, c), dst_ref=ada_reg(adal_ref, j, c),
                                             send_sem=ada_ss.at[2 * k + cc], recv_sem=ada_rs.at[2 * k + c],
                                             device_id=(*chips[k], cc), device_id_type=MESH).start()
        token[...] = jnp.zeros_like(token)

    sems = pltpu.SemaphoreType.DMA((6,))
    return pl.pallas_call(
        body, name="late_gather_start",
        out_shape=(sems, sems, sems, sems, pltpu.HBM(wo_land.shape, BF16), pltpu.HBM(ada_land.shape, BF16),
                   jax.ShapeDtypeStruct((8, 128), F32)),
        in_specs=(HBM, HBM), out_specs=(SEM, SEM, SEM, SEM, HBM, HBM, VMEM), input_output_aliases={0: 4, 1: 5},
        compiler_params=pltpu.CompilerParams(has_side_effects=pltpu.SideEffectType.DATAFLOW_SIDE_EFFECTING),
    )(pltpu.with_memory_space_constraint(wo_land, pltpu.HBM), pltpu.with_memory_space_constraint(ada_land, pltpu.HBM))


def _late_gather_wait(land, send_sems, recv_sems, which, after, name):
    def body(land_ref, ss, rs, after_ref, land_out):
        x, y, c = lax.axis_index("x"), lax.axis_index("y"), lax.axis_index("c")
        j = 2 * x + y
        chips, wo_reg, ada_reg = _late_gather_regions(x, y, c)
        reg = wo_reg if which == "w_out" else ada_reg
        for k in range(3):
            kj = 2 * chips[k][0] + chips[k][1]
            for cc in range(2):
                cp = pltpu.make_async_remote_copy(src_ref=reg(land_ref, j, c), dst_ref=reg(land_ref, kj, cc),
                                                  send_sem=ss.at[2 * k + cc], recv_sem=rs.at[2 * k + cc],
                                                  device_id=(*chips[k], cc), device_id_type=MESH)
                cp.wait_send()
                cp.wait_recv()

    return pl.pallas_call(
        body, name=name, out_shape=pltpu.HBM(land.shape, land.dtype),
        in_specs=(HBM, SEM, SEM, ANY), out_specs=HBM, input_output_aliases={0: 0},
        compiler_params=pltpu.CompilerParams(has_side_effects=pltpu.SideEffectType.DATAFLOW_SIDE_EFFECTING),
    )(land, send_sems, recv_sems, after)


RCHUNK = 16


def _grads_reduce(hn, dzs, hn_c, dxa_c, g_w_out, pack):
    rp = pack.shape[0]
    hp = rp // 2
    assert hp % RCHUNK == 0
    wi_w = 1280
    lx, lc = hn.shape[0], hn_c.shape[0]
    lt = lx + lc
    n_dz = len(dzs)

    def body(*refs):
        hn_hbm, dz_hbm = refs[0], refs[1:1 + n_dz]
        hnc_hbm, dxac_hbm, wo_hbm, pk_hbm, wi_out, wo_out, pk_out = refs[1 + n_dz:8 + n_dz]
        (hn_mine, hn_other, dzbuf, wi_other, wi_mine, wi_recv, wi_send, wi_rb,
         wo_mine, wo_recv, wo_send, wo_rb, wo_own, pk_mine, pk_recv, pk_send, pk_rb, pk_own,
         send_sems, recv_sems, local_sems) = refs[8 + n_dz:]
        x, y, c = lax.axis_index("x"), lax.axis_index("y"), lax.axis_index("c")
        j = 2 * x + y
        sib = (x, y, 1 - c)
        chips = [(1 - x, y), (x, 1 - y), (1 - x, 1 - y)]
        cj = [2 * cx + cy for cx, cy in chips]
        near = (jnp.where(c == 0, 1 - x, x), jnp.where(c == 0, y, 1 - y), c)
        slabs = [cj[2], cj[0], cj[1], j]

        def copy(k, src, dst, to):
            return pltpu.make_async_remote_copy(src_ref=src, dst_ref=dst, send_sem=send_sems.at[k],
                                                recv_sem=recv_sems.at[k], device_id=to, device_id_type=MESH)

        def local(k, src, dst):
            cp = pltpu.make_async_copy(src, dst, local_sems.at[k])
            cp.start()
            return cp

        rows_half = lambda r, cc, n: r.at[pl.ds(_mo(cc * n, 16), n), :]
        cols_half = lambda r, cc, n: r.at[:, pl.ds(_mo(cc * n, 128), n)]
        pk_piece = lambda r, cc, jj: r.at[pl.ds(_mo(cc * hp, 16), hp), pl.ds(_mo(jj * 128, 128), 128)]

        sends = []

        def start(cp):
            cp.start()
            sends.append(cp)

        def dz_pieces(s):
            g0 = wi_w * s
            k0, off0 = g0 // D, g0 % D
            w0 = min(D - off0, wi_w)
            pieces = [(k0, off0, w0, 0)]
            if w0 < wi_w:
                pieces.append((k0 + 1, 0, wi_w - w0, w0))
            return pieces

        def dz_copies(s):
            cps = []
            for q, (k, off, w, dst) in enumerate(dz_pieces(s)):
                cps.append(pltpu.make_async_copy(dz_hbm[k].at[:, pl.ds(off, w)], dzbuf.at[pl.ds(0, lx), pl.ds(dst, w)],
                                                 local_sems.at[11 + q]))
            if s == 0:
                cps.append(pltpu.make_async_copy(dxac_hbm, dzbuf.at[pl.ds(lx, lc), pl.ds(0, D)], local_sems.at[13]))
            return cps

        def dz_load(sl):
            for s in range(NCHIP):
                @pl.when(sl == s)
                def _():
                    if s == 0:
                        dzbuf[pl.ds(lx, lc), pl.ds(D, wi_w - D)] = jnp.zeros((lc, wi_w - D), BF16)
                    else:
                        dzbuf[pl.ds(lx, lc), :] = jnp.zeros((lc, wi_w), BF16)
                    for cp in dz_copies(s):
                        cp.start()

        def dz_wait(sl):
            for s in range(NCHIP):
                @pl.when(sl == s)
                def _():
                    for cp in dz_copies(s):
                        cp.wait()

        l_pk = local(0, rows_half(pk_hbm, c, hp), pk_mine)
        start(copy(0, rows_half(pk_hbm, 1 - c, hp), pk_recv, sib))
        l_wo = local(1, cols_half(wo_hbm, c, 512), wo_mine)
        start(copy(1, cols_half(wo_hbm, 1 - c, 512), wo_recv, sib))
        hn_loads = [local(2, hn_hbm.at[:, pl.ds(_mo(c * 512, 128), 512)], hn_mine.at[pl.ds(0, lx), :]),
                    local(3, hnc_hbm.at[:, pl.ds(_mo(c * 512, 128), 512)], hn_mine.at[pl.ds(lx, lc), :]),
                    local(4, hn_hbm.at[:, pl.ds(_mo((1 - c) * 512, 128), 512)], hn_other.at[pl.ds(0, lx), :]),
                    local(5, hnc_hbm.at[:, pl.ds(_mo((1 - c) * 512, 128), 512)], hn_other.at[pl.ds(lx, lc), :])]
        dz_load(slabs[0])

        def pair_sum(mine, recv, send, nrows, keep, relayed=None):
            def step(i, carry):
                rows = pl.ds(_mo(i * RCHUNK, RCHUNK), RCHUNK)
                s = mine[rows, :] + recv[rows, :].astype(F32)
                if relayed is not None:
                    s = s + relayed[rows, :].astype(F32)
                if keep:
                    mine[rows, :] = s
                if send is not None:
                    send[rows, :] = s.astype(BF16)
                return carry
            lax.fori_loop(0, nrows // RCHUNK, step, 0)

        def chip_sum(own, rb, nrows, terms=(0, 1, 2)):
            def step(i, carry):
                rows = pl.ds(_mo(i * RCHUNK, RCHUNK), RCHUNK)
                acc = own[rows, :]
                for q in terms:
                    acc = acc + rb[q, rows, :].astype(F32)
                own[rows, :] = acc
                return carry
            lax.fori_loop(0, nrows // RCHUNK, step, 0)

        p1 = [None] * NCHIP

        def slab_matmuls(s):
            if s >= 2:
                p1[s - 2].wait_send()
            dz_wait(slabs[s])
            wi_other[s % 2] = _dot_tn(hn_other[...], dzbuf[...]).astype(BF16)
            p1[s] = copy(2 + s, wi_other.at[s % 2], wi_recv.at[s], sib)
            p1[s].start()
            wi_mine[s % 2] = _dot_tn(hn_mine[...], dzbuf[...])
            if s + 1 < NCHIP:
                dz_load(slabs[s + 1])

        def slab_finish(s):
            copy(2 + s, wi_recv.at[s], wi_recv.at[s], sib).wait_recv()
            if s == 3:
                pair_sum(wi_mine.at[s % 2], wi_recv.at[s], None, 512, True)
                return
            if s == 0:
                pair_sum(wi_mine.at[0], wi_recv.at[0], wi_send.at[0], 512, False)
                start(copy(12, wi_send.at[0], wi_rb.at[0], near))
                return
            adds_relayed = c == (1 if s == 1 else 0)

            @pl.when(adds_relayed)
            def _():
                copy(12, wi_rb.at[0], wi_rb.at[0], sib).wait_recv()
                pair_sum(wi_mine.at[s % 2], wi_recv.at[s], wi_send.at[s], 512, False, wi_rb.at[0])

            @pl.when(jnp.logical_not(adds_relayed))
            def _():
                pair_sum(wi_mine.at[s % 2], wi_recv.at[s], wi_send.at[s], 512, False)
            start(copy(12 + s, wi_send.at[s], wi_rb.at[s], (*chips[s - 1], c)))

        l_wo.wait()
        copy(1, wo_recv, wo_recv, sib).wait_recv()
        pair_sum(wo_mine, wo_recv, wo_send, 2048, True)
        for k in range(3):
            start(copy(9 + k, wo_send.at[pl.ds(_mo(cj[k] * 512, 16), 512), :], wo_rb.at[k], (*chips[k], c)))
        l_wo_own = local(7, wo_mine.at[pl.ds(_mo(j * 512, 16), 512), :], wo_own)

        for cp in hn_loads:
            cp.wait()
        slab_matmuls(0)

        l_pk.wait()
        copy(0, pk_recv, pk_recv, sib).wait_recv()
        pair_sum(pk_mine, pk_recv, pk_send, hp, True)
        for k in range(3):
            start(copy(6 + k, pk_send.at[:, pl.ds(_mo(cj[k] * 128, 128), 128)], pk_rb.at[k], (*chips[k], c)))
        l_pk_own = local(6, pk_mine.at[:, pl.ds(_mo(j * 128, 128), 128)], pk_own)

        slab_matmuls(1)
        slab_finish(0)

        l_pk_own.wait()
        for k in range(3):
            copy(6 + k, pk_rb.at[k], pk_rb.at[k], sib).wait_recv()
        chip_sum(pk_own, pk_rb, hp)
        l_pk_out = local(8, pk_own, pk_piece(pk_out, c, j))
        start(copy(15, pk_own, pk_piece(pk_out, c, j), sib))
        for k in range(3):
            start(copy(16 + k, pk_own, pk_piece(pk_out, c, j), (*chips[k], c)))

        slab_matmuls(2)
        slab_finish(1)
        slab_matmuls(3)
        slab_finish(2)

        l_wo_own.wait()
        for k in range(3):
            copy(9 + k, wo_rb.at[k], wo_rb.at[k], sib).wait_recv()
        chip_sum(wo_own, wo_rb, 512)
        l_wo_out = local(9, wo_own, cols_half(wo_out, c, 512))
        start(copy(22, wo_own, cols_half(wo_out, c, 512), sib))

        for k in range(3):
            reg = pk_piece(pk_out, c, cj[k])
            copy(16 + k, reg, reg, sib).wait_recv()
            start(copy(19 + k, reg, reg, sib))

        slab_finish(3)
        for k in (1, 2):
            copy(12 + k, wi_rb.at[k], wi_rb.at[k], sib).wait_recv()
        chip_sum(wi_mine.at[1], wi_rb, 512, (1, 2))
        l_wi_out = local(10, wi_mine.at[1], rows_half(wi_out, c, 512))
        start(copy(23, wi_mine.at[1], rows_half(wi_out, c, 512), sib))

        reg = pk_piece(pk_out, 1 - c, j)
        copy(15, reg, reg, sib).wait_recv()
        for k in range(3):
            reg = pk_piece(pk_out, 1 - c, cj[k])
            copy(19 + k, reg, reg, sib).wait_recv()
        reg = cols_half(wo_out, 1 - c, 512)
        copy(22, reg, reg, sib).wait_recv()
        reg = rows_half(wi_out, 1 - c, 512)
        copy(23, reg, reg, sib).wait_recv()
        for cp in sends + p1[2:]:
            cp.wait_send()
        for cp in (l_pk_out, l_wo_out, l_wi_out):
            cp.wait()

    return pl.pallas_call(
        body, name="grads_reduce",
        out_shape=(jax.ShapeDtypeStruct((D, wi_w), F32), jax.ShapeDtypeStruct((512, D), F32),
                   jax.ShapeDtypeStruct(pack.shape, F32)),
        in_specs=[ANY] * (5 + n_dz), out_specs=(ANY,) * 3,
        scratch_shapes=[
            pltpu.VMEM((lt, 512), BF16), pltpu.VMEM((lt, 512), BF16), pltpu.VMEM((lt, wi_w), BF16),
            pltpu.VMEM((2, 512, wi_w), BF16), pltpu.VMEM((2, 512, wi_w), F32), pltpu.VMEM((4, 512, wi_w), BF16),
            pltpu.VMEM((3, 512, wi_w), BF16), pltpu.VMEM((3, 512, wi_w), BF16),
            pltpu.VMEM((2048, 512), F32), pltpu.VMEM((2048, 512), F32), pltpu.VMEM((2048, 512), BF16),
            pltpu.VMEM((3, 512, 512), BF16), pltpu.VMEM((512, 512), F32),
            pltpu.VMEM((hp, 512), F32), pltpu.VMEM((hp, 512), F32), pltpu.VMEM((hp, 512), BF16),
            pltpu.VMEM((3, hp, 128), BF16), pltpu.VMEM((hp, 128), F32),
            pltpu.SemaphoreType.DMA((24,)), pltpu.SemaphoreType.DMA((24,)), pltpu.SemaphoreType.DMA((14,))],
        compiler_params=_cp(vmem_mb=56),
    )(hn, *dzs, hn_c, dxa_c, g_w_out, pack)


def _ada_bwd(c_all, dmx_all_j, dmc_j, dmx_all, dmc, c_ctx, ada_w_full):
    def body(c_ref, dmxj_ref, dmcj_ref, dmx_ref, dmc_ref, cc_ref, w_ref, gw_ref, gb_ref, gc_ref, lhs, rhs, dm8):
        lhs[...] = jnp.zeros_like(lhs)
        rhs[...] = jnp.zeros_like(rhs)
        cv = c_ref[...]
        lhs[0:8, :] = cv * _sigmoid(cv)
        cc = cc_ref[...]
        a_c, da_c = _silu_and_grad(cc)
        lhs[8:9, :] = a_c
        rhs[0:8, :] = dmxj_ref[...]
        rhs[8:9, :] = dmcj_ref[...]
        gw_ref[...] = _dot_tn(lhs[...].astype(BF16), rhs[...].astype(BF16))
        gb_ref[...] = jnp.sum(dmx_ref[...], axis=0, keepdims=True) + dmc_ref[...]
        dm8[...] = jnp.zeros_like(dm8)
        dm8[0:1, :] = dmc_ref[...]
        da = _dot_nt(dm8[...].astype(BF16), w_ref[...])
        gc_ref[...] = da[0:1, :] * da_c

    return pl.pallas_call(
        body, name="ada_bwd",
        out_shape=(jax.ShapeDtypeStruct((D, 768), F32), jax.ShapeDtypeStruct((1, 3 * D), F32),
                   jax.ShapeDtypeStruct((1, D), F32)),
        in_specs=[VMEM] * 7, out_specs=(VMEM,) * 3,
        scratch_shapes=[pltpu.VMEM((16, D), F32), pltpu.VMEM((16, 768), F32), pltpu.VMEM((8, 3 * D), F32)],
        compiler_params=_cp(vmem_mb=32),
    )(c_all, dmx_all_j, dmc_j, dmx_all, dmc, c_ctx, ada_w_full)


def _proj(x, mods, mrow, norm_g, w_full, nk, name):
    lx = x.shape[0]
    tp = min(lx, TP)
    n = lx // tp

    def body(x_ref, sh_ref, sc_ref, ng_ref, *rest):
        w_refs, hn_ref, z_refs = rest[:nk], rest[nk], rest[nk + 1:]
        xv = x_ref[...]
        r = lax.rsqrt(jnp.mean(xv * xv, axis=-1, keepdims=True) + NORM_EPS)
        hn = (xv * r) * ng_ref[...] * (1.0 + sc_ref[mrow:mrow + 1, :]) + sh_ref[mrow:mrow + 1, :]
        hb = hn.astype(BF16)
        hn_ref[...] = hb
        for k in range(nk):
            z_refs[k][...] = _dot(hb, w_refs[k][...])

    row = pl.BlockSpec((tp, D), lambda i: (i, 0))
    in_specs = [row, pl.BlockSpec((8, D), lambda i: (0, 0)), pl.BlockSpec((8, D), lambda i: (0, 1)),
                pl.BlockSpec((1, D), lambda i: (0, 0))]
    in_specs += [pl.BlockSpec((D, D), lambda i, k=k: (0, k)) for k in range(nk)]
    out_shape = (jax.ShapeDtypeStruct((lx, D), BF16),) + tuple(jax.ShapeDtypeStruct((lx, D), F32) for _ in range(nk))
    return pl.pallas_call(
        body, name=name, grid=(n,), out_shape=out_shape, in_specs=in_specs, out_specs=(row,) * (nk + 1),
        compiler_params=_cp(1, vmem_mb=56),
    )(x, mods, mods, norm_g, *([w_full] * nk))


def _halo_specs(lx):
    last = lx // 8 - 1
    return [pl.BlockSpec((T, D), lambda i: (i, 0)),
            pl.BlockSpec((8, D), lambda i: (jnp.maximum(i * (T // 8) - 1, 0), 0)),
            pl.BlockSpec((8, D), lambda i: (jnp.minimum((i + 1) * (T // 8), last), 0))]


def _zhalo_specs(lx):
    last = lx // 8 - 1
    return [pl.BlockSpec((T * NH, HD), lambda i: (i, 0)),
            pl.BlockSpec((8 * NH, HD), lambda i: (jnp.maximum(i * (T // 8) - 1, 0), 0)),
            pl.BlockSpec((8 * NH, HD), lambda i: (jnp.minimum((i + 1) * (T // 8), last), 0))]


ZT = pl.BlockSpec((T * NH, HD), lambda i: (i, 0))
CONV_CHUNK = 32


SCAN_SUB = 4


def _scan_tile(chains, post, carry_ref):
    blk = T // SCAN_SUB

    def step(k, state):
        new = []
        for ci, (a_ref, x_ref, o_ref, q_ref, reverse) in enumerate(chains):
            for q in range(SCAN_SUB):
                s, p = state[ci * SCAN_SUB + q]
                t = (q + 1) * blk - 1 - k if reverse else q * blk + k
                r = pl.ds(_mo(t * NH, NH), NH)
                a = a_ref[r, :]
                if post:
                    o = x_ref[r, :] + s
                    o_ref[r, :] = o
                    q_ref[r, :] = p
                    new.append((a * o, a * p))
                else:
                    o = a * s + x_ref[r, :]
                    p = a * p
                    o_ref[r, :] = o
                    q_ref[r, :] = p
                    new.append((o, p))
        return tuple(new)

    zero = jnp.zeros((NH, HD), F32)
    one = jnp.ones((NH, HD), F32)
    final = lax.fori_loop(0, blk, step, tuple((zero, one) for _ in range(len(chains) * SCAN_SUB)), unroll=2)
    for ci, (a_ref, x_ref, o_ref, q_ref, reverse) in enumerate(chains):
        carry = carry_ref[ci]
        for q in (range(SCAN_SUB - 1, -1, -1) if reverse else range(SCAN_SUB)):
            rows = pl.ds(q * blk * NH, blk * NH)
            fixed = o_ref[rows, :].reshape(blk, NH, HD) + q_ref[rows, :].reshape(blk, NH, HD) * carry[None]
            o_ref[rows, :] = fixed.reshape(blk * NH, HD)
            s_loc, p_loc = final[ci * SCAN_SUB + q]
            carry = s_loc + p_loc * carry
        carry_ref[ci] = carry


def _lru_fwd(xa, conv_wz, conv_bz, wcat, bcat, lamcat, s_f, s_b, name):
    lx = xa.shape[0]
    n = lx // T

    def body(xm_u, xp_u, xn_u, xm_d, xp_d, xn_d, cw, cb, w_ref, b_ref, lam_ref, su0, sd0,
             xaz_o, xcz_o, af_o, ab_o, hf_o, hb_o, gf_o, gb_o, fu, fd, pad, xc_d, x_u, x_d, q_u, q_d, carry):
        i = pl.program_id(0)

        @pl.when(i == 0)
        def _():
            carry[0] = su0[...]
            carry[1] = sd0[...]

        def conv_gates(xm, xp, xn, tile, d, xc_ref, a_ref, x_ref, xaz_ref, g_ref):
            pmask = jnp.where(tile == 0, 0.0, 1.0)
            nmask = jnp.where(tile == n - 1, 0.0, 1.0)
            for h in range(NH):
                cols = slice(HD * h, HD * h + HD)
                pad[_zrows(h, 8), :] = xp[:, cols] * pmask
                pad[pl.ds(8 * NH + h, T, stride=NH), :] = xm[:, cols]
                pad[pl.ds((T + 8) * NH + h, 8, stride=NH), :] = xn[:, cols] * nmask
            if xaz_ref is not None:
                xaz_ref[...] = pad[pl.ds(8 * NH, T * NH), :]

            def conv_chunk(ci, c_):
                base = pl.multiple_of(ci * (CONV_CHUNK * NH), CONV_CHUNK * NH)
                acc = None
                for k in range(4):
                    sl = pad[pl.ds(base + (7 + k) * NH, CONV_CHUNK * NH), :].reshape(CONV_CHUNK, NH, HD)
                    term = sl * cw[k][None]
                    acc = term if acc is None else acc + term
                acc = acc + cb[...][None]
                xc_ref[pl.ds(base, CONV_CHUNK * NH), :] = acc.reshape(CONV_CHUNK * NH, HD)
                return c_
            lax.fori_loop(0, T // CONV_CHUNK, conv_chunk, 0)

            for h in range(NH):
                xch = xc_ref[_zrows(h, T), :]
                pre = _dot(xch.astype(BF16), w_ref[h, :, 256 * d:256 * d + 256]) + b_ref[h:h + 1, 256 * d:256 * d + 256]
                r, gi, _, _, a, mult = _lru_gate(pre, lam_ref[h:h + 1, :], d, 0)
                a_ref[_zrows(h, T), :] = a
                x_ref[_zrows(h, T), :] = mult * gi * xch
                for q, val in enumerate((r, gi, mult)):
                    g_ref[:, q * D + HD * h:q * D + HD * h + HD] = val

        conv_gates(xm_u, xp_u, xn_u, i, 0, xcz_o, af_o, x_u, xaz_o, gf_o)
        conv_gates(xm_d, xp_d, xn_d, n - 1 - i, 1, xc_d, ab_o, x_d, None, gb_o)

        _scan_tile([(af_o, x_u, hf_o, q_u, False), (ab_o, x_d, hb_o, q_d, True)], False, carry)
        fu[...] = carry[0]
        fd[...] = carry[1]

    full = lambda shape: pl.BlockSpec(shape, lambda i: (0,) * len(shape))
    last = lx // 8 - 1
    rev = lambda i: n - 1 - i
    halo_dn = [pl.BlockSpec((T, D), lambda i: (rev(i), 0)),
               pl.BlockSpec((8, D), lambda i: (jnp.maximum(rev(i) * (T // 8) - 1, 0), 0)),
               pl.BlockSpec((8, D), lambda i: (jnp.minimum((rev(i) + 1) * (T // 8), last), 0))]
    st = full((NH, HD))
    in_specs = _halo_specs(lx) + halo_dn + [full((4, NH, HD)), st, full((NH, HD, 4 * HD)), full((NH, 4 * HD)),
                                            full((NH, 2 * HD)), st, st]
    dn = pl.BlockSpec((T * NH, HD), lambda i: (rev(i), 0))
    zs = jax.ShapeDtypeStruct((lx * NH, HD), F32)
    ss = jax.ShapeDtypeStruct((NH, HD), F32)
    zbuf = pltpu.VMEM((T * NH, HD), F32)
    gs = jax.ShapeDtypeStruct((lx, 3 * D), F32)
    g_up = pl.BlockSpec((T, 3 * D), lambda i: (i, 0))
    g_dn = pl.BlockSpec((T, 3 * D), lambda i: (rev(i), 0))
    return pl.pallas_call(
        body, name=name, grid=(n,), out_shape=(zs,) * 6 + (gs, gs, ss, ss), in_specs=in_specs,
        out_specs=(ZT, ZT, ZT, dn, ZT, dn, g_up, g_dn, st, st),
        scratch_shapes=[pltpu.VMEM(((T + 16) * NH, HD), F32), zbuf, zbuf, zbuf, zbuf, zbuf,
                        pltpu.VMEM((2, NH, HD), F32)],
        compiler_params=_cp(1, vmem_mb=48),
    )(xa, xa, xa, xa, xa, xa, conv_wz, conv_bz, wcat, bcat, lamcat, s_f, s_b)


def _scan_adjoint(a_up, x_up, a_dn, x_dn, name):
    lx = a_up.shape[0] // NH
    n = lx // T

    def body(au, xu, ad, xd, ou, od, fu, fd, q_u, q_d, carry):
        @pl.when(pl.program_id(0) == 0)
        def _():
            carry[...] = jnp.zeros_like(carry)

        _scan_tile([(au, xu, ou, q_u, False), (ad, xd, od, q_d, True)], True, carry)
        fu[...] = carry[0]
        fd[...] = carry[1]

    up = pl.BlockSpec((T * NH, HD), lambda i: (i, 0))
    dn = pl.BlockSpec((T * NH, HD), lambda i: (n - 1 - i, 0))
    st = pl.BlockSpec((NH, HD), lambda i: (0, 0))
    zs = jax.ShapeDtypeStruct((lx * NH, HD), F32)
    ss = jax.ShapeDtypeStruct((NH, HD), F32)
    zbuf = pltpu.VMEM((T * NH, HD), F32)
    return pl.pallas_call(
        body, name=name, grid=(n,), out_shape=(zs, zs, ss, ss), in_specs=[up, up, dn, dn],
        out_specs=(up, dn, st, st), scratch_shapes=[zbuf, zbuf, pltpu.VMEM((2, NH, HD), F32)],
        compiler_params=_cp(1, vmem_mb=48),
    )(a_up, x_up, a_dn, x_dn)


def _sgu_parts(u, v, lng, lnb, w_ref, bt_ref, mixed_s):
    ug, dug = _gelu_and_grad(u)
    vg, dvg = _gelu_and_grad(v)
    mu = jnp.mean(vg, axis=-1, keepdims=True)
    vc = vg - mu
    rstd = lax.rsqrt(jnp.mean(vc * vc, axis=-1, keepdims=True) + LN_EPS)
    vh = vc * rstd
    vn = (vh * lng + lnb).astype(BF16)
    for g in range(NH):
        cols = slice(HD * g, HD * g + HD)
        mixed_s[:, cols] = _dot(w_ref[g], vn[:, cols]) + bt_ref[:, g:g + 1]
    return ug, dug, dvg, rstd, vh, vn


def _sgu_fwd(u, v, ln_g, ln_b, sgu_w, sgu_bt):
    lx = u.shape[0]
    n = lx // HD

    def body(u_ref, v_ref, g_ref, b_ref, w_ref, bt_ref, y_ref, mixed_s):
        ug, _, _, _, _, _ = _sgu_parts(u_ref[...], v_ref[...], g_ref[...], b_ref[...], w_ref, bt_ref, mixed_s)
        y_ref[...] = ug * mixed_s[...]

    row = pl.BlockSpec((HD, D), lambda i: (i, 0))
    vec = pl.BlockSpec((1, D), lambda i: (0, 0))
    return pl.pallas_call(
        body, name="sgu_fwd", grid=(n,), out_shape=jax.ShapeDtypeStruct((lx, D), F32),
        in_specs=[row, row, vec, vec, pl.BlockSpec((NH, HD, HD), lambda i: (0, 0, 0)),
                  pl.BlockSpec((HD, NH), lambda i: (0, 0))],
        out_specs=row, scratch_shapes=[pltpu.VMEM((HD, D), F32)],
        compiler_params=_cp(1),
    )(u, v, ln_g, ln_b, sgu_w, sgu_bt)


def _sgu_bwd(u, v, dys, ln_g, ln_b, sgu_w, sgu_bt):
    lx = u.shape[0]
    n = lx // HD

    def body(u_ref, v_ref, dy_ref, g_ref, b_ref, w_ref, bt_ref, du_ref, dv_ref, dw_ref, db_ref, dg_ref, dbl_ref,
             mixed_s, dvn_s):
        i = pl.program_id(0)

        @pl.when(i == 0)
        def _():
            dw_ref[...] = jnp.zeros_like(dw_ref)
            db_ref[...] = jnp.zeros_like(db_ref)
            dg_ref[...] = jnp.zeros_like(dg_ref)
            dbl_ref[...] = jnp.zeros_like(dbl_ref)

        lng = g_ref[...]
        ug, dug, dvg, rstd, vh, vn = _sgu_parts(u_ref[...], v_ref[...], lng, b_ref[...], w_ref, bt_ref, mixed_s)
        dys_v = dy_ref[...]
        du_ref[...] = (dys_v * mixed_s[...] * dug).astype(BF16)
        dmix = dys_v * ug
        ones = jnp.ones((8, HD), BF16)
        for g in range(NH):
            cols = slice(HD * g, HD * g + HD)
            dm = dmix[:, cols]
            hi = dm.astype(BF16)
            lo = (dm - hi.astype(F32)).astype(BF16)
            dw_ref[g] += _dot_nt(hi, vn[:, cols])
            db_ref[g:g + 1, :] += (_dot_nt(ones, hi) + _dot_nt(ones, lo))[0:1, :]
            dvn_s[:, cols] = _dot_tn(w_ref[g], hi)
        dvn = dvn_s[...]
        dg_ref[...] += jnp.sum(dvn * vh, axis=0, keepdims=True)
        dbl_ref[...] += jnp.sum(dvn, axis=0, keepdims=True)
        dvh = dvn * lng
        dvg_in = rstd * (dvh - jnp.mean(dvh, axis=-1, keepdims=True)
                         - vh * jnp.mean(dvh * vh, axis=-1, keepdims=True))
        dv_ref[...] = (dvg_in * dvg).astype(BF16)

    row = pl.BlockSpec((HD, D), lambda i: (i, 0))
    vec = pl.BlockSpec((1, D), lambda i: (0, 0))
    wsp = pl.BlockSpec((NH, HD, HD), lambda i: (0, 0, 0))
    bsp = pl.BlockSpec((NH, HD), lambda i: (0, 0))
    return pl.pallas_call(
        body, name="sgu_bwd", grid=(n,),
        out_shape=(jax.ShapeDtypeStruct((lx, D), BF16), jax.ShapeDtypeStruct((lx, D), BF16),
                   jax.ShapeDtypeStruct((NH, HD, HD), F32), jax.ShapeDtypeStruct((NH, HD), F32),
                   jax.ShapeDtypeStruct((1, D), F32), jax.ShapeDtypeStruct((1, D), F32)),
        in_specs=[row, row, row, vec, vec, wsp, pl.BlockSpec((HD, NH), lambda i: (0, 0))],
        out_specs=(row, row, wsp, bsp, vec, vec),
        scratch_shapes=[pltpu.VMEM((HD, D), F32), pltpu.VMEM((HD, D), F32)],
        compiler_params=_cp(1),
    )(u, v, dys, ln_g, ln_b, sgu_w, sgu_bt)


def _out_fwd_bwd(hf_z, hb_z, ga, gb, ys, x, tgt, mods, final_g, w_out_full):
    lx = x.shape[0]
    n = lx // T

    def body(hf_ref, hb_ref, ga_ref, gb_ref, ys_ref, x_ref, t_ref, gx_ref, fg_ref, w_ref,
             loss_ref, dfg_ref, dgx_ref, dxn_ref, y_ref, do_ref, dga_ref, dgb_ref, dyl_ref, dys_ref, yl_s):
        i = pl.program_id(0)

        @pl.when(i == 0)
        def _():
            loss_ref[...] = jnp.zeros_like(loss_ref)
            dfg_ref[...] = jnp.zeros_like(dfg_ref)
            dgx_ref[...] = jnp.zeros_like(dgx_ref)

        for h in range(NH):
            yl_s[:, HD * h:HD * h + HD] = hf_ref[_zrows(h, T), :] + hb_ref[_zrows(h, T), :]
        yl = yl_s[...]
        gav = ga_ref[...]
        gbv = gb_ref[...]
        sa, dsa = _silu_and_grad(gav)
        sb, dsb = _silu_and_grad(gbv)
        ysv = ys_ref[...]
        y_ref[:, 0:D] = (yl * sa).astype(BF16)
        y_ref[:, D:2 * D] = (ysv * sb).astype(BF16)
        o = _dot(y_ref[...], w_ref[...])
        gx = gx_ref[0:1, :]
        xnew = x_ref[...] + gx * o
        r2 = lax.rsqrt(jnp.mean(xnew * xnew, axis=-1, keepdims=True) + NORM_EPS)
        xh = xnew * r2
        fg = fg_ref[...]
        err = xh * fg - t_ref[...]
        loss_ref[...] += 0.5 * jnp.sum(jnp.mean(err * err, axis=-1, keepdims=True), axis=0, keepdims=True)
        dout = err * (1.0 / D)
        dfg_ref[...] += jnp.sum(dout * xh, axis=0, keepdims=True)
        dxh = dout * fg
        dxn = r2 * (dxh - xh * jnp.mean(dxh * xh, axis=-1, keepdims=True))
        dxn_ref[...] = dxn
        dgx_ref[...] += jnp.sum(dxn * o, axis=0, keepdims=True)
        do = (dxn * gx).astype(BF16)
        do_ref[...] = do
        dy = _dot_nt(do, w_ref[...])
        dy1 = dy[:, 0:D]
        dy2 = dy[:, D:2 * D]
        dga_ref[...] = (dy1 * yl * dsa).astype(BF16)
        dgb_ref[...] = (dy2 * ysv * dsb).astype(BF16)
        dys_ref[...] = dy2 * sb
        yl_s[...] = dy1 * sa
        for h in range(NH):
            dyl_ref[_zrows(h, T), :] = yl_s[:, HD * h:HD * h + HD]

    row = pl.BlockSpec((T, D), lambda i: (i, 0))
    vec = pl.BlockSpec((1, D), lambda i: (0, 0))
    in_specs = [ZT, ZT, row, row, row, row, row, pl.BlockSpec((8, D), lambda i: (0, 2)), vec,
                pl.BlockSpec((2 * D, D), lambda i: (0, 0))]
    out_shape = (jax.ShapeDtypeStruct((1, 1), F32), jax.ShapeDtypeStruct((1, D), F32), jax.ShapeDtypeStruct((1, D), F32),
                 jax.ShapeDtypeStruct((lx, D), F32), jax.ShapeDtypeStruct((lx, 2 * D), BF16),
                 jax.ShapeDtypeStruct((lx, D), BF16), jax.ShapeDtypeStruct((lx, D), BF16),
                 jax.ShapeDtypeStruct((lx, D), BF16), jax.ShapeDtypeStruct((lx * NH, HD), F32),
                 jax.ShapeDtypeStruct((lx, D), F32))
    out_specs = (pl.BlockSpec((1, 1), lambda i: (0, 0)), vec, vec, row, pl.BlockSpec((T, 2 * D), lambda i: (i, 0)),
                 row, row, row, ZT, row)
    return pl.pallas_call(
        body, name="out_fwd_bwd", grid=(n,), out_shape=out_shape, in_specs=in_specs, out_specs=out_specs,
        scratch_shapes=[pltpu.VMEM((T, D), F32)],
        compiler_params=_cp(1, vmem_mb=56),
    )(hf_z, hb_z, ga, gb, ys, x, tgt, mods, final_g, w_out_full)


def _lru_gates_bwd(xc_z, lf_z, lb_z, hf_z, hb_z, af_z, ab_z, gf, gb, s_f, s_b, wcat, lamcat, dw0, db0, dl0, name):
    lx = xc_z.shape[0] // NH
    n = lx // T

    def body(xc_ref, lf_ref, lb_ref, hf_ref, hfp_ref, hb_ref, hbn_ref, af_ref, ab_ref, gf_ref, gb_ref, sf_ref, sb_ref,
             w_ref, lam_ref, dw0_ref, db0_ref, dl0_ref, dxc_ref, dw_ref, db_ref, dl_ref, dpre_s, pf_s, pb_s):
        i = pl.program_id(0)

        @pl.when(i == 0)
        def _():
            dw_ref[...] = dw0_ref[...]
            db_ref[...] = db0_ref[...]
            dl_ref[...] = dl0_ref[...]

        pf_s[pl.ds(0, NH), :] = jnp.where(i == 0, sf_ref[...], hfp_ref[pl.ds(7 * NH, NH), :])
        pf_s[pl.ds(NH, T * NH), :] = hf_ref[...]
        pb_s[pl.ds(0, T * NH), :] = hb_ref[...]
        pb_s[pl.ds(T * NH, NH), :] = jnp.where(i == n - 1, sb_ref[...], hbn_ref[pl.ds(0, NH), :])
        lam_refs = (lf_ref, lb_ref)
        prev = ((pf_s, 0), (pb_s, NH))
        a_refs = (af_ref, ab_ref)
        g_refs = (gf_ref, gb_ref)
        for h in range(NH):
            xch = xc_ref[_zrows(h, T), :]
            xcb = xch.astype(BF16)
            dxc = jnp.zeros((T, HD), F32)
            for d in range(2):
                r, gi, mult = (g_refs[d][:, q * D + HD * h:q * D + HD * h + HD] for q in range(3))
                a = a_refs[d][_zrows(h, T), :]
                lam = lam_ref[h:h + 1, HD * d:HD * d + HD]
                sp = _softplus(-lam)
                du = lam_refs[d][_zrows(h, T), :]
                da = du * prev[d][0][pl.ds(prev[d][1] + h, T, stride=NH), :]
                dgi = du * mult * xch
                dxc = dxc + du * mult * gi
                dmult = du * gi * xch
                dla = da * a - dmult * (a * a) / mult
                dr = dla * ((-LRU_C) * sp)
                dsp = jnp.sum(dla * ((-LRU_C) * r), axis=0, keepdims=True)
                dl_ref[h:h + 1, HD * d:HD * d + HD] += dsp * (-_sigmoid(-lam))
                dpre_s[:, 256 * d:256 * d + HD] = dr * r * (1.0 - r)
                dpre_s[:, 256 * d + HD:256 * d + 2 * HD] = dgi * gi * (1.0 - gi)
            dpre = dpre_s[...]
            dpb = dpre.astype(BF16)
            dw_ref[h] += _dot_tn(xcb, dpb)
            db_ref[h:h + 1, :] += jnp.sum(dpre, axis=0, keepdims=True)
            dxc_ref[_zrows(h, T), :] = dxc + _dot_nt(dpb, w_ref[h])

    full = lambda shape: pl.BlockSpec(shape, lambda i: (0,) * len(shape))
    wsp, bsp, lsp = full((NH, HD, 4 * HD)), full((NH, 4 * HD)), full((NH, 2 * HD))
    st = full((NH, HD))
    zh = _zhalo_specs(lx)
    gsp = pl.BlockSpec((T, 3 * D), lambda i: (i, 0))
    return pl.pallas_call(
        body, name=name, grid=(n,),
        out_shape=(jax.ShapeDtypeStruct((lx * NH, HD), F32), jax.ShapeDtypeStruct((NH, HD, 4 * HD), F32),
                   jax.ShapeDtypeStruct((NH, 4 * HD), F32), jax.ShapeDtypeStruct((NH, 2 * HD), F32)),
        in_specs=[ZT] * 3 + zh[0:2] + [zh[0], zh[2], ZT, ZT, gsp, gsp, st, st, wsp, lsp, wsp, bsp, lsp],
        out_specs=(ZT, wsp, bsp, lsp),
        scratch_shapes=[pltpu.VMEM((T, 4 * HD), F32), pltpu.VMEM(((T + 1) * NH, HD), F32),
                        pltpu.VMEM(((T + 1) * NH, HD), F32)],
        compiler_params=_cp(1, vmem_mb=56),
    )(xc_z, lf_z, lb_z, hf_z, hf_z, hb_z, hb_z, af_z, ab_z, gf, gb, s_f, s_b, wcat, lamcat, dw0, db0, dl0)


def _conv_bwd(dxc_z, xa_z, conv_wz, dcw0, dcb0, name):
    lx = dxc_z.shape[0] // NH
    n = lx // T

    def body(dm, dp, dn, xa_ref, cw, dcw0_ref, dcb0_ref, dxa_ref, dcw_ref, dcb_ref, pad, dxa_s):
        i = pl.program_id(0)

        @pl.when(i == 0)
        def _():
            dcw_ref[...] = dcw0_ref[...]
            dcb_ref[...] = dcb0_ref[...]

        pmask = jnp.where(i == 0, 0.0, 1.0)
        nmask = jnp.where(i == n - 1, 0.0, 1.0)
        pad[pl.ds(0, 8 * NH), :] = dp[...] * pmask
        pad[pl.ds(8 * NH, T * NH), :] = dm[...]
        pad[pl.ds((T + 8) * NH, 8 * NH), :] = dn[...] * nmask

        def chunk(ci, carry):
            base = pl.multiple_of(ci * (CONV_CHUNK * NH), CONV_CHUNK * NH)
            xav = xa_ref[pl.ds(base, CONV_CHUNK * NH), :].reshape(CONV_CHUNK, NH, HD)
            acc = None
            for k in range(4):
                sl = pad[pl.ds(base + (9 - k) * NH, CONV_CHUNK * NH), :].reshape(CONV_CHUNK, NH, HD)
                term = sl * cw[k][None]
                acc = term if acc is None else acc + term
                dcw_ref[k] += jnp.sum(sl * xav, axis=0)
                if k == 1:
                    dcb_ref[...] += jnp.sum(sl, axis=0)
            dxa_s[pl.ds(base, CONV_CHUNK * NH), :] = acc.reshape(CONV_CHUNK * NH, HD)
            return carry
        lax.fori_loop(0, T // CONV_CHUNK, chunk, 0)
        for h in range(NH):
            dxa_ref[:, HD * h:HD * h + HD] = dxa_s[_zrows(h, T), :].astype(BF16)

    full = lambda shape: pl.BlockSpec(shape, lambda i: (0,) * len(shape))
    return pl.pallas_call(
        body, name=name, grid=(n,),
        out_shape=(jax.ShapeDtypeStruct((lx, D), BF16), jax.ShapeDtypeStruct((4, NH, HD), F32),
                   jax.ShapeDtypeStruct((NH, HD), F32)),
        in_specs=_zhalo_specs(lx) + [ZT, full((4, NH, HD)), full((4, NH, HD)), full((NH, HD))],
        out_specs=(pl.BlockSpec((T, D), lambda i: (i, 0)), full((4, NH, HD)), full((NH, HD))),
        scratch_shapes=[pltpu.VMEM(((T + 16) * NH, HD), F32), pltpu.VMEM((T * NH, HD), F32)],
        compiler_params=_cp(1, vmem_mb=48),
    )(dxc_z, dxc_z, dxc_z, xa_z, conv_wz, dcw0, dcb0)


def _proj_bwd(dzs, x, dxn, mods, mrow, norm_g, w_full, dng0, name):
    lx = x.shape[0]
    tp = min(lx, TP)
    n = lx // tp
    nk = len(dzs)
    has_x = dxn is not None

    def body(*refs):
        dz_refs = refs[:nk]
        w_refs = refs[nk:2 * nk]
        x_ref, sc_ref, ng_ref, dng0_ref = refs[2 * nk:2 * nk + 4]
        rest = refs[2 * nk + 4:]
        if has_x:
            dxn_ref, gx_ref, dng_ref, dsc_ref, dsh_ref = rest
        else:
            dng_ref, dsc_ref, dsh_ref = rest
        i = pl.program_id(0)

        @pl.when(i == 0)
        def _():
            dng_ref[...] = dng0_ref[...]
            dsc_ref[...] = jnp.zeros_like(dsc_ref)
            dsh_ref[...] = jnp.zeros_like(dsh_ref)

        dhn = _dot_nt(dz_refs[0][...], w_refs[0][...])
        for k in range(1, nk):
            dhn = dhn + _dot_nt(dz_refs[k][...], w_refs[k][...])
        xv = x_ref[...]
        r = lax.rsqrt(jnp.mean(xv * xv, axis=-1, keepdims=True) + NORM_EPS)
        xn = xv * r
        ng = ng_ref[...]
        sc1 = 1.0 + sc_ref[mrow:mrow + 1, :]
        t = dhn * xn
        dng_ref[...] += jnp.sum(t * sc1, axis=0, keepdims=True)
        dsc_ref[...] += jnp.sum(t * ng, axis=0, keepdims=True)
        dsh_ref[...] += jnp.sum(dhn, axis=0, keepdims=True)
        if has_x:
            dxh = dhn * (ng * sc1)
            gx_ref[...] = dxn_ref[...] + r * (dxh - xn * jnp.mean(dxh * xn, axis=-1, keepdims=True))

    row = pl.BlockSpec((tp, D), lambda i: (i, 0))
    vec = pl.BlockSpec((1, D), lambda i: (0, 0))
    in_specs = [row] * nk + [pl.BlockSpec((D, D), lambda i, k=k: (0, k)) for k in range(nk)]
    in_specs += [row, pl.BlockSpec((8, D), lambda i: (0, 1)), vec, vec]
    args = list(dzs) + [w_full] * nk + [x, mods, norm_g, dng0]
    vs = jax.ShapeDtypeStruct((1, D), F32)
    out_shape, out_specs = (vs, vs, vs), (vec, vec, vec)
    if has_x:
        in_specs.append(row)
        args.append(dxn)
        out_shape = (jax.ShapeDtypeStruct((lx, D), F32),) + out_shape
        out_specs = (row,) + out_specs
    return pl.pallas_call(
        body, name=name, grid=(n,), out_shape=out_shape, in_specs=in_specs, out_specs=out_specs,
        compiler_params=_cp(1, vmem_mb=56),
    )(*args)


def _tn_matmul(a, bs, extra, name):
    lx, m = a.shape
    tm = min(lx, 1024)
    n = lx // tm
    widths = [b.shape[1] for b in bs]
    nb = len(bs)

    def body(a_ref, *rest):
        b_refs = rest[:nb]
        rest = rest[nb:]
        if extra is not None:
            ea_ref, eb_ref = rest[:2]
            rest = rest[2:]
        out_ref, acc = rest
        i = pl.program_id(0)
        av = a_ref[...]
        off = 0
        for k in range(nb):
            cols = slice(off, off + widths[k])
            part = _dot_tn(av, b_refs[k][...])

            @pl.when(i == 0)
            def _():
                acc[:, cols] = part

            @pl.when(i > 0)
            def _():
                acc[:, cols] += part
            off += widths[k]

        @pl.when(i == n - 1)
        def _():
            if extra is not None:
                acc[:, 0:widths[0]] += _dot_tn(ea_ref[...], eb_ref[...])
            pltpu.sync_copy(acc, out_ref)

    in_specs = [pl.BlockSpec((tm, m), lambda i: (i, 0))] + [pl.BlockSpec((tm, w), lambda i: (i, 0)) for w in widths]
    args = [a] + list(bs)
    if extra is not None:
        in_specs += [VMEM, VMEM]
        args += list(extra)
    return pl.pallas_call(
        body, name=name, grid=(n,), out_shape=jax.ShapeDtypeStruct((m, sum(widths)), F32),
        in_specs=in_specs, out_specs=ANY, scratch_shapes=[pltpu.VMEM((m, sum(widths)), F32)],
        compiler_params=_cp(1, vmem_mb=56),
    )(*args)


def _adam_math(w, g, m, v):
    m = ADAM_B1 * m + (1.0 - ADAM_B1) * g
    v = ADAM_B2 * v + (1.0 - ADAM_B2) * (g * g)
    m_hat = m / (1.0 - ADAM_B1 ** ADAM_STEP)
    v_hat = v / (1.0 - ADAM_B2 ** ADAM_STEP)
    delta = -ADAM_LR * (m_hat / (jnp.sqrt(v_hat) + ADAM_EPS) + ADAM_WD * w)
    return delta, m, v


def _adam_big(w, g, m, v, name):
    rows, cols = w.shape
    tr = 256

    def body(w_ref, g_ref, m_ref, v_ref, d_o, m_o, v_o):
        d, mm, vv = _adam_math(w_ref[...], g_ref[...], m_ref[...], v_ref[...])
        d_o[...] = d
        m_o[...] = mm
        v_o[...] = vv

    blk = pl.BlockSpec((tr, cols), lambda i: (i, 0))
    s = jax.ShapeDtypeStruct((rows, cols), F32)
    return pl.pallas_call(
        body, name=name, grid=(rows // tr,), out_shape=(s, s, s), in_specs=[blk] * 4, out_specs=(blk,) * 3,
        compiler_params=_cp(1, vmem_mb=48),
    )(w, g, m, v)


def _adam_small(items):
    ni = len(items)

    def body(*refs):
        ins, outs = refs[:4 * ni], refs[4 * ni:7 * ni]
        bufs_in, bufs_out = refs[7 * ni:11 * ni], refs[11 * ni:14 * ni]
        sem_in, sem_out = refs[14 * ni], refs[14 * ni + 1]
        loads = [pltpu.make_async_copy(ins[q], bufs_in[q], sem_in.at[q]) for q in range(4 * ni)]
        for cp in loads:
            cp.start()
        stores = []
        for k in range(ni):
            for q in range(4):
                loads[4 * k + q].wait()
            w_b, g_b, m_b, v_b = bufs_in[4 * k:4 * k + 4]
            res = _adam_math(w_b[...], g_b[...], m_b[...], v_b[...])
            for q in range(3):
                bufs_out[3 * k + q][...] = res[q]
                cp = pltpu.make_async_copy(bufs_out[3 * k + q], outs[3 * k + q], sem_out.at[3 * k + q])
                cp.start()
                stores.append(cp)
        for cp in stores:
            cp.wait()

    flat = [a for it in items for a in it]
    out_shape = tuple(jax.ShapeDtypeStruct(it[0].shape, F32) for it in items for _ in range(3))
    scratch = [pltpu.VMEM(a.shape, F32) for a in flat] + [pltpu.VMEM(s.shape, F32) for s in out_shape]
    scratch += [pltpu.SemaphoreType.DMA((4 * ni,)), pltpu.SemaphoreType.DMA((3 * ni,))]
    res = pl.pallas_call(
        body, name="adam_small", out_shape=out_shape, in_specs=[HBM] * (4 * ni), out_specs=(HBM,) * (3 * ni),
        scratch_shapes=scratch, compiler_params=_cp(vmem_mb=40),
    )(*flat)
    return [tuple(res[3 * k:3 * k + 3]) for k in range(ni)]


def kernel(x, c, ctx, c_ctx, ada_w, ada_b, norm_g, w_in, conv_w, conv_b, lru_wa, lru_ba, lru_wx, lru_bx, lru_lambda, sgu_ln_g, sgu_ln_b, sgu_w, sgu_b, w_out, final_g, loss_target, m_c_ctx, m_ada_w, m_ada_b, m_norm_g, m_w_in, m_conv_w, m_conv_b, m_lru_wa, m_lru_ba, m_lru_wx, m_lru_bx, m_lru_lambda, m_sgu_ln_g, m_sgu_ln_b, m_sgu_w, m_sgu_b, m_w_out, m_final_g, v_c_ctx, v_ada_w, v_ada_b, v_norm_g, v_w_in, v_conv_w, v_conv_b, v_lru_wa, v_lru_ba, v_lru_wx, v_lru_bx, v_lru_lambda, v_sgu_ln_g, v_sgu_ln_b, v_sgu_w, v_sgu_b, v_w_out, v_final_g):
    ix, iy, ic = lax.axis_index("x"), lax.axis_index("y"), lax.axis_index("c")
    chip = 2 * ix + iy
    dev = 2 * chip + ic
    lx = x.shape[1]
    lc = ctx.shape[1]

    smalls = jnp.concatenate([conv_w[0], lru_lambda[0], jnp.zeros((10, 256), F32)], axis=0)
    c_ctx2 = c_ctx.reshape(1, D)
    ada_b_j = lax.dynamic_slice(ada_b, (0, 768 * chip), (1, 768))
    mods, c_slots, sm_all, w_in_full, wo_land, ada_land = _gather_in(c, c_ctx2, ada_w[0], ada_b_j, w_in[0], w_out[0],
                                                                     smalls)
    wo_ss, wo_rs, ada_ss, ada_rs, wo_land, ada_land, token = _late_gather_start(wo_land, ada_land)
    mods = mods + token[0:1, 0:1]
    sm3 = sm_all.reshape(NCHIP, 16, 256)
    conv_w_full = sm3[:, 0:4, :].transpose(1, 0, 2).reshape(4, D)
    lam_full = sm3[:, 4:6, :].transpose(1, 0, 2).reshape(2, D)
    conv_wz = conv_w_full.reshape(4, NH, HD)
    conv_bz = conv_b.reshape(NH, HD)
    lamcat = lam_full.reshape(2, NH, HD).transpose(1, 0, 2).reshape(NH, 2 * HD)
    wa, wx, ba, bx = lru_wa[0], lru_wx[0], lru_ba[0], lru_bx[0]
    wcat = jnp.concatenate([wa[0], wx[0], wa[1], wx[1]], axis=-1).astype(BF16)
    bcat = jnp.concatenate([ba[0], bx[0], ba[1], bx[1]], axis=-1)
    sgu_wb = sgu_w[0].astype(BF16)
    sgu_bt = sgu_b[0].T
    final_g2 = final_g.reshape(1, D)

    zero_s = jnp.zeros((NH, HD), F32)
    hn_c, xa_c = _proj(ctx[0], mods, 1, norm_g, w_in_full, 1, "proj_ctx")
    xaz_c, xcz_c, af_c, ab_c, hf_c, hb_c, gf_c, gb_c, hf0, hb0 = _lru_fwd(xa_c, conv_wz, conv_bz, wcat, bcat, lamcat,
                                                                           zero_s, zero_s, "lru_fwd_ctx")

    hn, xa, ga, u, v, gb = _proj(x[0], mods, 0, norm_g, w_in_full, 5, "proj")
    xaz, xcz, af, ab, hf, hb, gf, gb_l, _, _ = _lru_fwd(xa, conv_wz, conv_bz, wcat, bcat, lamcat, hf0, hb0, "lru_fwd")
    ys = _sgu_fwd(u, v, sgu_ln_g, sgu_ln_b, sgu_wb, sgu_bt)

    w_out_full = _late_gather_wait(wo_land, wo_ss, wo_rs, "w_out", ys, "late_gather_wait_w_out")
    (loss_part, dfg, dgx, dxn, y, do, dga, dgb, dyl_z, dys) = _out_fwd_bwd(
        hf, hb, ga, gb, ys, x[0], loss_target[0], mods, final_g2, w_out_full)
    g_w_out_part = _tn_matmul(y, [do], None, "grad_w_out")

    du, dv, d_sgu_w, d_sgu_b, d_ln_g, d_ln_b = _sgu_bwd(u, v, dys, sgu_ln_g, sgu_ln_b, sgu_wb, sgu_bt)
    lb, lf, dh0b, dh0f = _scan_adjoint(ab, dyl_z, af, dyl_z, "scan_adj")
    zw = jnp.zeros((NH, HD, 4 * HD), F32)
    zb = jnp.zeros((NH, 4 * HD), F32)
    zl = jnp.zeros((NH, 2 * HD), F32)
    dxc_z, dwc, dbc, dlc = _lru_gates_bwd(xcz, lf, lb, hf, hb, af, ab, gf, gb_l, hf0, hb0, wcat, lamcat, zw, zb, zl,
                                          "lru_gates_bwd")
    dxa, dcw, dcb = _conv_bwd(dxc_z, xaz, conv_wz, jnp.zeros((4, NH, HD), F32), zero_s, "conv_bwd")

    zc = jnp.zeros((lc * NH, HD), F32)
    dhf_c = lax.dynamic_update_slice(zc, dh0f, ((lc - 1) * NH, 0))
    dhb_c = lax.dynamic_update_slice(zc, dh0b, (0, 0))
    lb_c, lf_c, _, _ = _scan_adjoint(ab_c, dhb_c, af_c, dhf_c, "scan_adj_ctx")
    dxc_zc, dwc, dbc, dlc = _lru_gates_bwd(xcz_c, lf_c, lb_c, hf_c, hb_c, af_c, ab_c, gf_c, gb_c, zero_s, zero_s,
                                           wcat, lamcat, dwc, dbc, dlc, "lru_gates_bwd_ctx")
    dxa_c, dcw, dcb = _conv_bwd(dxc_zc, xaz_c, conv_wz, dcw, dcb, "conv_bwd_ctx")

    dzs = [dxa, dga, du, dv, dgb]
    grad_x, dng, dsc_x, dsh_x = _proj_bwd(dzs, x[0], dxn, mods, 0, norm_g, w_in_full, jnp.zeros((1, D), F32), "proj_bwd")
    dng, dsc_c, dsh_c = _proj_bwd([dxa_c], ctx[0], None, mods, 1, norm_g, w_in_full, dng, "proj_bwd_ctx")

    dmx = jnp.concatenate([dsh_x, dsc_x, dgx], axis=0)
    dmc = jnp.concatenate([dsh_c, dsc_c, jnp.zeros((1, D), F32)], axis=0)
    lp = loss_part[0, 0]
    lp1 = lax.reduce_precision(lp, 8, 7)
    lp2 = lax.reduce_precision(lp - lp1, 8, 7)
    lp3 = lax.reduce_precision(lp - lp1 - lp2, 8, 7)
    loss_row = jnp.pad(jnp.stack([lp1, lp2, lp3]).reshape(1, 3), ((0, 0), (0, D - 3)))
    slot = jnp.concatenate([dmx, loss_row], axis=0)
    slots = lax.dynamic_update_slice(jnp.zeros((32, D), F32), slot, (4 * dev, 0))
    vecs = jnp.concatenate([dfg, dng, dcb.reshape(1, D), d_ln_g, d_ln_b, dcw.reshape(4, D), dmc,
                            jnp.zeros((4, D), F32), slots], axis=0)
    d_sgu_w4 = d_sgu_w.reshape(4, 256, HD).transpose(1, 0, 2).reshape(256, 4 * HD)
    pad8 = lambda a: jnp.pad(a, ((0, 8 - a.shape[0]), (0, 4 * HD - a.shape[1])))
    pack = jnp.concatenate([dwc.reshape(NH * HD, 4 * HD), pad8(dbc), pad8(dlc), d_sgu_w4, pad8(d_sgu_b),
                            vecs.reshape(96, 4 * HD), jnp.zeros((8, 4 * HD), F32)], axis=0)
    g_w_in, g_w_out, tot = _grads_reduce(hn, dzs, hn_c, dxa_c, g_w_out_part, pack)

    g_wc = tot[0:1024].reshape(NH, HD, 4 * HD)
    g_bc = tot[1024:1032]
    g_lc = tot[1032:1040, 0:2 * HD]
    g_sgu_w = tot[1040:1296].reshape(256, 4, HD).transpose(1, 0, 2).reshape(NH, HD, HD)
    g_sgu_b = tot[1296:1304, 0:HD]
    tv = tot[1304:1400].reshape(48, D)
    g_final_g, g_norm_g, g_conv_b, g_ln_g, g_ln_b = tv[0:1], tv[1:2], tv[2:3], tv[3:4], tv[4:5]
    g_conv_w_full = tv[5:9]
    dmc_tot = tv[9:12].reshape(1, 3 * D)
    slots_all = tv[16:48].reshape(8, 4, D)
    dmx_all = slots_all[:, 0:3, :].reshape(8, 3 * D)
    c_all = c_slots.reshape(8, 8, D)[:, 0, :]
    g_lru_wa = jnp.stack([g_wc[:, :, 0:HD], g_wc[:, :, 2 * HD:3 * HD]])
    g_lru_wx = jnp.stack([g_wc[:, :, HD:2 * HD], g_wc[:, :, 3 * HD:4 * HD]])
    g_lru_ba = jnp.stack([g_bc[:, 0:HD], g_bc[:, 2 * HD:3 * HD]])
    g_lru_bx = jnp.stack([g_bc[:, HD:2 * HD], g_bc[:, 3 * HD:4 * HD]])
    g_lam_full = jnp.stack([g_lc[:, 0:HD], g_lc[:, HD:2 * HD]]).reshape(2, D)
    g_conv_w = lax.dynamic_slice(g_conv_w_full, (0, 256 * chip), (4, 256))
    g_lam = lax.dynamic_slice(g_lam_full, (0, 256 * chip), (2, 256))
    dmx_all_j = lax.dynamic_slice(dmx_all, (0, 768 * chip), (8, 768))
    dmc_j = lax.dynamic_slice(dmc_tot, (0, 768 * chip), (1, 768))
    ada_full = _late_gather_wait(ada_land, ada_ss, ada_rs, "ada_w", tot, "late_gather_wait_ada_w")
    g_ada_w, g_ada_b, g_c_ctx = _ada_bwd(c_all, dmx_all_j, dmc_j, dmx_all, dmc_tot, c_ctx2, ada_full)

    big = {
        "ada_w": _adam_big(ada_w[0], g_ada_w, m_ada_w[0], v_ada_w[0], "adam_ada_w"),
        "w_in": _adam_big(w_in[0], g_w_in, m_w_in[0], v_w_in[0], "adam_w_in"),
        "w_out": _adam_big(w_out[0], g_w_out, m_w_out[0], v_w_out[0], "adam_w_out"),
    }
    small_in = {
        "c_ctx": (c_ctx, g_c_ctx, m_c_ctx, v_c_ctx, (1, D)),
        "ada_b": (ada_b, g_ada_b, m_ada_b, v_ada_b, (1, 3 * D)),
        "norm_g": (norm_g, g_norm_g, m_norm_g, v_norm_g, (1, D)),
        "conv_w": (conv_w, g_conv_w, m_conv_w, v_conv_w, (4, 256)),
        "conv_b": (conv_b, g_conv_b, m_conv_b, v_conv_b, (1, D)),
        "lru_wa": (lru_wa, g_lru_wa, m_lru_wa, v_lru_wa, (2 * NH * HD, HD)),
        "lru_ba": (lru_ba, g_lru_ba, m_lru_ba, v_lru_ba, (2 * NH, HD)),
        "lru_wx": (lru_wx, g_lru_wx, m_lru_wx, v_lru_wx, (2 * NH * HD, HD)),
        "lru_bx": (lru_bx, g_lru_bx, m_lru_bx, v_lru_bx, (2 * NH, HD)),
        "lru_lambda": (lru_lambda, g_lam, m_lru_lambda, v_lru_lambda, (2, 256)),
        "sgu_ln_g": (sgu_ln_g, g_ln_g, m_sgu_ln_g, v_sgu_ln_g, (1, D)),
        "sgu_ln_b": (sgu_ln_b, g_ln_b, m_sgu_ln_b, v_sgu_ln_b, (1, D)),
        "sgu_w": (sgu_w, g_sgu_w, m_sgu_w, v_sgu_w, (NH * HD, HD)),
        "sgu_b": (sgu_b, g_sgu_b, m_sgu_b, v_sgu_b, (NH, HD)),
        "final_g": (final_g, g_final_g, m_final_g, v_final_g, (1, D)),
    }
    names_small = list(small_in)
    res_small = _adam_small([tuple(a.reshape(small_in[k][4]) for a in small_in[k][:4]) for k in names_small])
    full_shapes = {"ada_w": ada_w.shape, "w_in": w_in.shape, "w_out": w_out.shape}
    grads, deltas, new_m, new_v = {}, {}, {}, {}
    for k in ("ada_w", "w_in", "w_out"):
        g = {"ada_w": g_ada_w, "w_in": g_w_in, "w_out": g_w_out}[k]
        grads[k] = g.reshape(full_shapes[k])
        deltas[k], new_m[k], new_v[k] = (a.reshape(full_shapes[k]) for a in big[k])
    for k, res in zip(names_small, res_small):
        shape = small_in[k][0].shape
        grads[k] = small_in[k][1].reshape(shape)
        deltas[k], new_m[k], new_v[k] = (a.reshape(shape) for a in res)

    loss = jnp.sum(slots_all[:, 3, 0:3])
    order = ["c_ctx", "ada_w", "ada_b", "norm_g", "w_in", "conv_w", "conv_b", "lru_wa", "lru_ba", "lru_wx", "lru_bx",
             "lru_lambda", "sgu_ln_g", "sgu_ln_b", "sgu_w", "sgu_b", "w_out", "final_g"]
    return (loss, grad_x.reshape(x.shape), *[grads[k] for k in order], *[deltas[k] for k in order],
            *[new_m[k] for k in order], *[new_v[k] for k in order])
```

```python
import functools

import jax
import jax.numpy as jnp
from jax import lax
from jax.experimental import pallas as pl
from jax.experimental.pallas import tpu as pltpu

F32 = jnp.float32
BF16 = jnp.bfloat16

D = 1024
NH = 8
HD = 128
NCHIP = 4
T = 256
TP = 512
NORM_EPS = 1e-6
LN_EPS = 1e-5
LRU_C = 8.0
ADAM_LR = 0.001
ADAM_B1 = 0.9
ADAM_B2 = 0.999
ADAM_EPS = 1e-08
ADAM_WD = 0.01
ADAM_STEP = 10

VMEM = pl.BlockSpec(memory_space=pltpu.VMEM)
ANY = pl.BlockSpec(memory_space=pl.ANY)
MESH = pl.DeviceIdType.MESH


def _cp(n_grid=0, vmem_mb=None):
    kw = {}
    if n_grid:
        kw["dimension_semantics"] = ("arbitrary",) * n_grid
    if vmem_mb:
        kw["vmem_limit_bytes"] = vmem_mb << 20
    return pltpu.CompilerParams(**kw)


def _sigmoid(x):
    return 1.0 / (1.0 + jnp.exp(-x))


def _silu_and_grad(x):
    s = _sigmoid(x)
    return x * s, s * (1.0 + x * (1.0 - s))


_GELU_K = 0.7978845608028654
_GELU_C = 0.044715


def _gelu_and_grad(x):
    x2 = x * x
    th = jnp.tanh(_GELU_K * (x + _GELU_C * x * x2))
    g = 0.5 * x * (1.0 + th)
    dg = 0.5 * (1.0 + th) + 0.5 * x * (1.0 - th * th) * (_GELU_K * (1.0 + 3.0 * _GELU_C * x2))
    return g, dg


def _softplus(x):
    return jnp.maximum(x, 0.0) + jnp.log1p(jnp.exp(-jnp.abs(x)))


def _lru_gate(pre, lam_row, d, off=None):
    off = 256 * d if off is None else off
    r = _sigmoid(pre[:, off:off + HD])
    gi = _sigmoid(pre[:, off + HD:off + 2 * HD])
    lam = lam_row[:, HD * d:HD * d + HD]
    sp = _softplus(-lam)
    la = (-LRU_C) * r * sp
    a = jnp.exp(la)
    x2 = 2.0 * la
    m2 = jnp.where(x2 > -1e-3, -x2 * (1.0 + 0.5 * x2), 1.0 - a * a)
    mult = jnp.sqrt(m2)
    return r, gi, lam, sp, a, mult


def _dot(a, b):
    return jnp.dot(a, b, preferred_element_type=F32)


def _dot_tn(a, b):
    return lax.dot_general(a, b, (((0,), (0,)), ((), ())), preferred_element_type=F32)


def _dot_nt(a, b):
    return lax.dot_general(a, b, (((1,), (1,)), ((), ())), preferred_element_type=F32)


def _mo(v, m):
    return v if isinstance(v, int) else pl.multiple_of(v, m)


def _zrows(h, n):
    return pl.ds(h, n, stride=NH)


def _gather_in(c, c_ctx, ada_w, ada_b_j, w_in, w_out, smalls):
    nch = [1, 4]
    wrows = lambda cc, q: (pl.ds(_mo(512 * cc, 16), 512) if q is None
                           else pl.ds(_mo(512 * cc + (512 // nch[1]) * q, 16), 512 // nch[1]))
    specs = [
        ((64, 256), F32, lambda r, jj, cc, q=None: r.at[pl.ds(_mo(16 * jj + 8 * cc, 8), 8), :]),
        ((D, 5120), BF16, lambda r, jj, cc, q=None: r.at[wrows(cc, q), pl.ds(_mo(1280 * jj, 128), 1280)]),
    ]
    halves = [lambda r, cc, q=None: r.at[pl.ds(_mo(8 * cc, 8), 8), :],
              lambda r, cc, q=None: r.at[wrows(cc, q), :]]
    na = len(specs)
    sem_base = [0, 6 * nch[0]]
    sidx = lambda a, q, k: sem_base[a] + 6 * q + k
    n_tiny = 6 * sum(nch)
    n_sem = n_tiny + 10

    def body(c_ref, cc_ref, ada_ref, adab_ref, win_ref, wout_ref, sm_ref,
             mods_o, call_o, sm_o, win_o, wol_o, adal_o, s_win, s_ada, s_wout, f_win, f_ada, f_wout, cslot, lhs, mbuf,
             send_sems, recv_sems, local_sems, load_sems):
        x, y, c = lax.axis_index("x"), lax.axis_index("y"), lax.axis_index("c")
        j = 2 * x + y
        dev = 2 * j + c
        sib = (x, y, 1 - c)
        chips = [(1 - x, y), (x, 1 - y), (1 - x, 1 - y)]
        cj = [2 * cx + cy for cx, cy in chips]
        outs = [sm_o, win_o]
        srcs = [sm_ref, s_win]

        def copy(idx, src, dst, to):
            return pltpu.make_async_remote_copy(src_ref=src, dst_ref=dst, send_sem=send_sems.at[idx],
                                                recv_sem=recv_sems.at[idx], device_id=to, device_id_type=MESH)

        sends = []

        def start(cp):
            cp.start()
            sends.append(cp)

        cslot[...] = jnp.zeros_like(cslot)
        cslot[0:1, :] = c_ref[...]
        my_slot = pl.ds(_mo(8 * dev, 8), 8)
        others = [sib] + [(*chips[k], c) for k in range(3)] + [(*chips[k], 1 - c) for k in range(3)]
        other_dev = [dev + 1 - 2 * c] + [2 * cj[k] + c for k in range(3)] + [2 * cj[k] + 1 - c for k in range(3)]
        base = n_tiny
        for r in range(7):
            start(copy(base + r, cslot, call_o.at[my_slot, :], others[r]))
        call_o[my_slot, :] = cslot[...]

        crow = 512 // nch[1]
        loads = []
        for cc in (c, 1 - c):
            for q in range(nch[1]):
                rows = pl.ds(_mo(512 * cc + crow * q, 16), crow)
                loads.append(pltpu.make_async_copy(win_ref.at[rows, :], f_win.at[rows, :], load_sems.at[len(loads)]))
        loads.append(pltpu.make_async_copy(ada_ref, f_ada, load_sems.at[len(loads)]))
        loads.append(pltpu.make_async_copy(wout_ref, f_wout, load_sems.at[len(loads)]))
        for ld in loads:
            ld.start()
        for k in range(2):
            start(copy(sidx(0, 0, k), halves[0](srcs[0], c), specs[0][2](outs[0], j, c), (*chips[k], c)))
        for q in range(nch[1]):
            loads[q].wait()
            rows = pl.ds(_mo(512 * c + crow * q, 16), crow)
            s_win[rows, :] = f_win[rows, :].astype(BF16)
            for k in range(2):
                start(copy(sidx(1, q, k), halves[1](s_win, c, q), specs[1][2](win_o, j, c, q), (*chips[k], c)))
        for q in range(nch[1]):
            loads[nch[1] + q].wait()
            rows = pl.ds(_mo(512 * (1 - c) + crow * q, 16), crow)
            s_win[rows, :] = f_win[rows, :].astype(BF16)
        local = []
        for a in range(na):
            for cc in range(2):
                lc = pltpu.make_async_copy(halves[a](srcs[a], cc), specs[a][2](outs[a], j, cc), local_sems.at[2 * a + cc])
                lc.start()
                local.append(lc)
        loads[2 * nch[1]].wait()
        s_ada[...] = f_ada[...].astype(BF16)
        loads[2 * nch[1] + 1].wait()
        s_wout[...] = f_wout[...].astype(BF16)
        for q, (src, dst) in enumerate([(s_wout, wol_o.at[pl.ds(_mo(512 * j, 16), 512), :]),
                                        (s_ada, adal_o.at[:, pl.ds(_mo(768 * j, 128), 768)])]):
            lc = pltpu.make_async_copy(src, dst, local_sems.at[2 * na + q])
            lc.start()
            local.append(lc)

        for r in range(7):
            slot = call_o.at[pl.ds(_mo(8 * other_dev[r], 8), 8), :]
            copy(base + r, slot, slot, sib).wait_recv()
        lhs[...] = jnp.zeros_like(lhs)
        for b in range(8):
            cv = call_o[8 * b:8 * b + 1, :]
            lhs[b:b + 1, :] = cv * _sigmoid(cv)
        cv = cc_ref[...]
        lhs[8:9, :] = cv * _sigmoid(cv)
        mbuf[j] = _dot(lhs[...].astype(BF16), s_ada[...]) + adab_ref[...]
        for k in range(3):
            start(copy(base + 7 + k, mbuf.at[j], mbuf.at[j], (*chips[k], c)))
        for k in range(3):
            copy(base + 7 + k, mbuf.at[cj[k]], mbuf.at[cj[k]], sib).wait_recv()
        mods_o[...] = jnp.zeros_like(mods_o)
        for jj in range(NCHIP):
            mods_o[0:1, 768 * jj:768 * jj + 768] = mbuf[jj, pl.ds(dev, 1), :]
            mods_o[1:2, 768 * jj:768 * jj + 768] = mbuf[jj, 8:9, :]

        kx = [1 - x, x, 1 - x]
        ky = [y, 1 - y, 1 - y]
        pick = lambda k, lst: jnp.where(k == 0, lst[0], jnp.where(k == 1, lst[1], lst[2]))
        for a in range(na):
            for q in range(nch[a]):
                for step, k in enumerate([c, 1 - c]):
                    reg = specs[a][2](outs[a], pick(k, cj), c, q)
                    copy(sidx(a, q, k), reg, reg, sib).wait_recv()
                    if step == 0:
                        start(copy(sidx(a, q, 2), reg, reg, (pick(1 - c, kx), pick(1 - c, ky), c)))
                    start(copy(sidx(a, q, 3 + k), reg, reg, sib))
        for a in range(na):
            for q in range(nch[a]):
                reg = specs[a][2](outs[a], cj[2], c, q)
                copy(sidx(a, q, 2), reg, reg, sib).wait_recv()
                start(copy(sidx(a, q, 5), reg, reg, sib))
        for a in range(na):
            for q in range(nch[a]):
                for k in range(3):
                    reg = specs[a][2](outs[a], cj[k], 1 - c, q)
                    copy(sidx(a, q, 3 + k), reg, reg, sib).wait_recv()
        for cp in sends:
            cp.wait_send()
        for lc in local:
            lc.wait()

    out_shape = (jax.ShapeDtypeStruct((8, 3 * D), F32), jax.ShapeDtypeStruct((64, D), F32),
                 jax.ShapeDtypeStruct(specs[0][0], F32), jax.ShapeDtypeStruct(specs[1][0], BF16),
                 jax.ShapeDtypeStruct((2048, D), BF16), jax.ShapeDtypeStruct((D, 3 * D), BF16))
    return pl.pallas_call(
        body, name="gather_in", out_shape=out_shape,
        in_specs=[VMEM, VMEM, ANY, VMEM, ANY, ANY, VMEM], out_specs=(VMEM, VMEM, VMEM, ANY, ANY, ANY),
        scratch_shapes=[pltpu.VMEM((D, 1280), BF16), pltpu.VMEM((D, 768), BF16), pltpu.VMEM((512, D), BF16),
                        pltpu.VMEM((D, 1280), F32), pltpu.VMEM((D, 768), F32), pltpu.VMEM((512, D), F32),
                        pltpu.VMEM((8, D), F32), pltpu.VMEM((16, D), F32), pltpu.VMEM((NCHIP, 16, 768), F32),
                        pltpu.SemaphoreType.DMA((n_sem,)), pltpu.SemaphoreType.DMA((n_sem,)),
                        pltpu.SemaphoreType.DMA((2 * na + 2,)), pltpu.SemaphoreType.DMA((2 * nch[1] + 2,))],
        compiler_params=_cp(vmem_mb=56),
    )(c, c_ctx, ada_w, ada_b_j, w_in, w_out, smalls)


HBM = pl.BlockSpec(memory_space=pltpu.HBM)
SEM = pl.BlockSpec(memory_space=pltpu.SEMAPHORE)


def _late_gather_regions(x, y, c):
    chips = [(1 - x, y), (x, 1 - y), (1 - x, 1 - y)]
    wo_reg = lambda r, jj, cc: r.at[pl.ds(_mo(512 * jj + 256 * cc, 16), 256), :]
    ada_reg = lambda r, jj, cc: r.at[pl.ds(_mo(512 * cc, 16), 512), pl.ds(_mo(768 * jj, 128), 768)]
    return chips, wo_reg, ada_reg


def _late_gather_start(wo_land, ada_land):
    def body(wol_ref, adal_ref, wo_ss, wo_rs, ada_ss, ada_rs, wol_thru, adal_thru, token):
        x, y, c = lax.axis_index("x"), lax.axis_index("y"), lax.axis_index("c")
        j = 2 * x + y
        chips, wo_reg, ada_reg = _late_gather_regions(x, y, c)
        for k in range(3):
            for cc in range(2):
                pltpu.make_async_remote_copy(src_ref=wo_reg(wol_ref, j, c), dst_ref=wo_reg(wol_ref, j, c),
                                             send_sem=wo_ss.at[2 * k + cc], recv_sem=wo_rs.at[2 * k + c],
                                             device_id=(*chips[k], cc), device_id_type=MESH).start()
        for k in range(3):
            for cc in range(2):
                pltpu.make_async_remote_copy(src_ref=ada_reg(adal_ref, j, c), dst_ref=ada_reg(adal_ref, j, c),
                                             send_sem=ada_ss.at[2 * k + cc], recv_sem=ada_rs.at[2 * k + c],
                                             device_id=(*chips[k], cc), device_id_type=MESH).start()
        token[...] = jnp.zeros_like(token)

    sems = pltpu.SemaphoreType.DMA((6,))
    return pl.pallas_call(
        body, name="late_gather_start",
        out_shape=(sems, sems, sems, sems, pltpu.HBM(wo_land.shape, BF16), pltpu.HBM(ada_land.shape, BF16),
                   jax.ShapeDtypeStruct((8, 128), F32)),
        in_specs=(HBM, HBM), out_specs=(SEM, SEM, SEM, SEM, HBM, HBM, VMEM), input_output_aliases={0: 4, 1: 5},
        compiler_params=pltpu.CompilerParams(has_side_effects=pltpu.SideEffectType.DATAFLOW_SIDE_EFFECTING),
    )(pltpu.with_memory_space_constraint(wo_land, pltpu.HBM), pltpu.with_memory_space_constraint(ada_land, pltpu.HBM))


def _late_gather_wait(land, send_sems, recv_sems, which, after, name):
    def body(land_ref, ss, rs, after_ref, land_out):
        x, y, c = lax.axis_index("x"), lax.axis_index("y"), lax.axis_index("c")
        j = 2 * x + y
        chips, wo_reg, ada_reg = _late_gather_regions(x, y, c)
        reg = wo_reg if which == "w_out" else ada_reg
        for k in range(3):
            kj = 2 * chips[k][0] + chips[k][1]
            for cc in range(2):
                cp = pltpu.make_async_remote_copy(src_ref=reg(land_ref, j, c), dst_ref=reg(land_ref, kj, cc),
                                                  send_sem=ss.at[2 * k + cc], recv_sem=rs.at[2 * k + cc],
                                                  device_id=(*chips[k], cc), device_id_type=MESH)
                cp.wait_send()
                cp.wait_recv()

    return pl.pallas_call(
        body, name=name, out_shape=pltpu.HBM(land.shape, land.dtype),
        in_specs=(HBM, SEM, SEM, ANY), out_specs=HBM, input_output_aliases={0: 0},
        compiler_params=pltpu.CompilerParams(has_side_effects=pltpu.SideEffectType.DATAFLOW_SIDE_EFFECTING),
    )(land, send_sems, recv_sems, after)


RCHUNK = 16


def _grads_reduce(hn, dzs, hn_c, dxa_c, g_w_out, pack):
    rp = pack.shape[0]
    hp = rp // 2
    assert hp % RCHUNK == 0
    wi_w = 1280
    lx, lc = hn.shape[0], hn_c.shape[0]
    lt = lx + lc
    n_dz = len(dzs)

    def body(*refs):
        hn_hbm, dz_hbm = refs[0], refs[1:1 + n_dz]
        hnc_hbm, dxac_hbm, wo_hbm, pk_hbm, wi_out, wo_out, pk_out = refs[1 + n_dz:8 + n_dz]
        (hn_mine, hn_other, dzbuf, wi_other, wi_mine, wi_recv, wi_send, wi_rb,
         wo_mine, wo_recv, wo_send, wo_rb, wo_own, pk_mine, pk_recv, pk_send, pk_rb, pk_own,
         send_sems, recv_sems, local_sems) = refs[8 + n_dz:]
        x, y, c = lax.axis_index("x"), lax.axis_index("y"), lax.axis_index("c")
        j = 2 * x + y
        sib = (x, y, 1 - c)
        chips = [(1 - x, y), (x, 1 - y), (1 - x, 1 - y)]
        cj = [2 * cx + cy for cx, cy in chips]
        near = (jnp.where(c == 0, 1 - x, x), jnp.where(c == 0, y, 1 - y), c)
        slabs = [cj[2], cj[0], cj[1], j]

        def copy(k, src, dst, to):
            return pltpu.make_async_remote_copy(src_ref=src, dst_ref=dst, send_sem=send_sems.at[k],
                                                recv_sem=recv_sems.at[k], device_id=to, device_id_type=MESH)

        def local(k, src, dst):
            cp = pltpu.make_async_copy(src, dst, local_sems.at[k])
            cp.start()
            return cp

        rows_half = lambda r, cc, n: r.at[pl.ds(_mo(cc * n, 16), n), :]
        cols_half = lambda r, cc, n: r.at[:, pl.ds(_mo(cc * n, 128), n)]
        pk_piece = lambda r, cc, jj: r.at[pl.ds(_mo(cc * hp, 16), hp), pl.ds(_mo(jj * 128, 128), 128)]

        sends = []

        def start(cp):
            cp.start()
            sends.append(cp)

        def dz_pieces(s):
            g0 = wi_w * s
            k0, off0 = g0 // D, g0 % D
            w0 = min(D - off0, wi_w)
            pieces = [(k0, off0, w0, 0)]
            if w0 < wi_w:
                pieces.append((k0 + 1, 0, wi_w - w0, w0))
            return pieces

        def dz_copies(s):
            cps = []
            for q, (k, off, w, dst) in enumerate(dz_pieces(s)):
                cps.append(pltpu.make_async_copy(dz_hbm[k].at[:, pl.ds(off, w)], dzbuf.at[pl.ds(0, lx), pl.ds(dst, w)],
                                                 local_sems.at[11 + q]))
            if s == 0:
                cps.append(pltpu.make_async_copy(dxac_hbm, dzbuf.at[pl.ds(lx, lc), pl.ds(0, D)], local_sems.at[13]))
            return cps

        def dz_load(sl):
            for s in range(NCHIP):
                @pl.when(sl == s)
                def _():
                    if s == 0:
                        dzbuf[pl.ds(lx, lc), pl.ds(D, wi_w - D)] = jnp.zeros((lc, wi_w - D), BF16)
                    else:
                        dzbuf[pl.ds(lx, lc), :] = jnp.zeros((lc, wi_w), BF16)
                    for cp in dz_copies(s):
                        cp.start()

        def dz_wait(sl):
            for s in range(NCHIP):
                @pl.when(sl == s)
                def _():
                    for cp in dz_copies(s):
                        cp.wait()

        l_pk = local(0, rows_half(pk_hbm, c, hp), pk_mine)
        start(copy(0, rows_half(pk_hbm, 1 - c, hp), pk_recv, sib))
        l_wo = local(1, cols_half(wo_hbm, c, 512), wo_mine)
        start(copy(1, cols_half(wo_hbm, 1 - c, 512), wo_recv, sib))
        hn_loads = [local(2, hn_hbm.at[:, pl.ds(_mo(c * 512, 128), 512)], hn_mine.at[pl.ds(0, lx), :]),
                    local(3, hnc_hbm.at[:, pl.ds(_mo(c * 512, 128), 512)], hn_mine.at[pl.ds(lx, lc), :]),
                    local(4, hn_hbm.at[:, pl.ds(_mo((1 - c) * 512, 128), 512)], hn_other.at[pl.ds(0, lx), :]),
                    local(5, hnc_hbm.at[:, pl.ds(_mo((1 - c) * 512, 128), 512)], hn_other.at[pl.ds(lx, lc), :])]
        dz_load(slabs[0])

        def pair_sum(mine, recv, send, nrows, keep, relayed=None):
            def step(i, carry):
                rows = pl.ds(_mo(i * RCHUNK, RCHUNK), RCHUNK)
                s = mine[rows, :] + recv[rows, :].astype(F32)
                if relayed is not None:
                    s = s + relayed[rows, :].astype(F32)
                if keep:
                    mine[rows, :] = s
                if send is not None:
                    send[rows, :] = s.astype(BF16)
                return carry
            lax.fori_loop(0, nrows // RCHUNK, step, 0)

        def chip_sum(own, rb, nrows, terms=(0, 1, 2)):
            def step(i, carry):
                rows = pl.ds(_mo(i * RCHUNK, RCHUNK), RCHUNK)
                acc = own[rows, :]
                for q in terms:
                    acc = acc + rb[q, rows, :].astype(F32)
                own[rows, :] = acc
                return carry
            lax.fori_loop(0, nrows // RCHUNK, step, 0)

        p1 = [None] * NCHIP

        def slab_matmuls(s):
            if s >= 2:
                p1[s - 2].wait_send()
            dz_wait(slabs[s])
            wi_other[s % 2] = _dot_tn(hn_other[...], dzbuf[...]).astype(BF16)
            p1[s] = copy(2 + s, wi_other.at[s % 2], wi_recv.at[s], sib)
            p1[s].start()
            wi_mine[s % 2] = _dot_tn(hn_mine[...], dzbuf[...])
            if s + 1 < NCHIP:
                dz_load(slabs[s + 1])

        def slab_finish(s):
            copy(2 + s, wi_recv.at[s], wi_recv.at[s], sib).wait_recv()
            if s == 3:
                pair_sum(wi_mine.at[s % 2], wi_recv.at[s], None, 512, True)
                return
            if s == 0:
                pair_sum(wi_mine.at[0], wi_recv.at[0], wi_send.at[0], 512, False)
                start(copy(12, wi_send.at[0], wi_rb.at[0], near))
                return
            adds_relayed = c == (1 if s == 1 else 0)

            @pl.when(adds_relayed)
            def _():
                copy(12, wi_rb.at[0], wi_rb.at[0], sib).wait_recv()
                pair_sum(wi_mine.at[s % 2], wi_recv.at[s], wi_send.at[s], 512, False, wi_rb.at[0])

            @pl.when(jnp.logical_not(adds_relayed))
            def _():
                pair_sum(wi_mine.at[s % 2], wi_recv.at[s], wi_send.at[s], 512, False)
            start(copy(12 + s, wi_send.at[s], wi_rb.at[s], (*chips[s - 1], c)))

        l_wo.wait()
        copy(1, wo_recv, wo_recv, sib).wait_recv()
        pair_sum(wo_mine, wo_recv, wo_send, 2048, True)
        for k in range(3):
            start(copy(9 + k, wo_send.at[pl.ds(_mo(cj[k] * 512, 16), 512), :], wo_rb.at[k], (*chips[k], c)))
        l_wo_own = local(7, wo_mine.at[pl.ds(_mo(j * 512, 16), 512), :], wo_own)

        for cp in hn_loads:
            cp.wait()
        slab_matmuls(0)

        l_pk.wait()
        copy(0, pk_recv, pk_recv, sib).wait_recv()
        pair_sum(pk_mine, pk_recv, pk_send, hp, True)
        for k in range(3):
            start(copy(6 + k, pk_send.at[:, pl.ds(_mo(cj[k] * 128, 128), 128)], pk_rb.at[k], (*chips[k], c)))
        l_pk_own = local(6, pk_mine.at[:, pl.ds(_mo(j * 128, 128), 128)], pk_own)

        slab_matmuls(1)
        slab_finish(0)

        l_pk_own.wait()
        for k in range(3):
            copy(6 + k, pk_rb.at[k], pk_rb.at[k], sib).wait_recv()
        chip_sum(pk_own, pk_rb, hp)
        l_pk_out = local(8, pk_own, pk_piece(pk_out, c, j))
        start(copy(15, pk_own, pk_piece(pk_out, c, j), sib))
        for k in range(3):
            start(copy(16 + k, pk_own, pk_piece(pk_out, c, j), (*chips[k], c)))

        slab_matmuls(2)
        slab_finish(1)
        slab_matmuls(3)
        slab_finish(2)

        l_wo_own.wait()
        for k in range(3):
            copy(9 + k, wo_rb.at[k], wo_rb.at[k], sib).wait_recv()
        chip_sum(wo_own, wo_rb, 512)
        l_wo_out = local(9, wo_own, cols_half(wo_out, c, 512))
        start(copy(22, wo_own, cols_half(wo_out, c, 512), sib))

        for k in range(3):
            reg = pk_piece(pk_out, c, cj[k])
            copy(16 + k, reg, reg, sib).wait_recv()
            start(copy(19 + k, reg, reg, sib))

        slab_finish(3)
        for k in (1, 2):
            copy(12 + k, wi_rb.at[k], wi_rb.at[k], sib).wait_recv()
        chip_sum(wi_mine.at[1], wi_rb, 512, (1, 2))
        l_wi_out = local(10, wi_mine.at[1], rows_half(wi_out, c, 512))
        start(copy(23, wi_mine.at[1], rows_half(wi_out, c, 512), sib))

        reg = pk_piece(pk_out, 1 - c, j)
        copy(15, reg, reg, sib).wait_recv()
        for k in range(3):
            reg = pk_piece(pk_out, 1 - c, cj[k])
            copy(19 + k, reg, reg, sib).wait_recv()
        reg = cols_half(wo_out, 1 - c, 512)
        copy(22, reg, reg, sib).wait_recv()
        reg = rows_half(wi_out, 1 - c, 512)
        copy(23, reg, reg, sib).wait_recv()
        for cp in sends + p1[2:]:
            cp.wait_send()
        for cp in (l_pk_out, l_wo_out, l_wi_out):
            cp.wait()

    return pl.pallas_call(
        body, name="grads_reduce",
        out_shape=(jax.ShapeDtypeStruct((D, wi_w), F32), jax.ShapeDtypeStruct((512, D), F32),
                   jax.ShapeDtypeStruct(pack.shape, F32)),
        in_specs=[ANY] * (5 + n_dz), out_specs=(ANY,) * 3,
        scratch_shapes=[
            pltpu.VMEM((lt, 512), BF16), pltpu.VMEM((lt, 512), BF16), pltpu.VMEM((lt, wi_w), BF16),
            pltpu.VMEM((2, 512, wi_w), BF16), pltpu.VMEM((2, 512, wi_w), F32), pltpu.VMEM((4, 512, wi_w), BF16),
            pltpu.VMEM((3, 512, wi_w), BF16), pltpu.VMEM((3, 512, wi_w), BF16),
            pltpu.VMEM((2048, 512), F32), pltpu.VMEM((2048, 512), F32), pltpu.VMEM((2048, 512), BF16),
            pltpu.VMEM((3, 512, 512), BF16), pltpu.VMEM((512, 512), F32),
            pltpu.VMEM((hp, 512), F32), pltpu.VMEM((hp, 512), F32), pltpu.VMEM((hp, 512), BF16),
            pltpu.VMEM((3, hp, 128), BF16), pltpu.VMEM((hp, 128), F32),
            pltpu.SemaphoreType.DMA((24,)), pltpu.SemaphoreType.DMA((24,)), pltpu.SemaphoreType.DMA((14,))],
        compiler_params=_cp(vmem_mb=56),
    )(hn, *dzs, hn_c, dxa_c, g_w_out, pack)


def _ada_bwd(c_all, dmx_all_j, dmc_j, dmx_all, dmc, c_ctx, ada_w_full):
    def body(c_ref, dmxj_ref, dmcj_ref, dmx_ref, dmc_ref, cc_ref, w_ref, gw_ref, gb_ref, gc_ref, lhs, rhs, dm8):
        lhs[...] = jnp.zeros_like(lhs)
        rhs[...] = jnp.zeros_like(rhs)
        cv = c_ref[...]
        lhs[0:8, :] = cv * _sigmoid(cv)
        cc = cc_ref[...]
        a_c, da_c = _silu_and_grad(cc)
        lhs[8:9, :] = a_c
        rhs[0:8, :] = dmxj_ref[...]
        rhs[8:9, :] = dmcj_ref[...]
        gw_ref[...] = _dot_tn(lhs[...].astype(BF16), rhs[...].astype(BF16))
        gb_ref[...] = jnp.sum(dmx_ref[...], axis=0, keepdims=True) + dmc_ref[...]
        dm8[...] = jnp.zeros_like(dm8)
        dm8[0:1, :] = dmc_ref[...]
        da = _dot_nt(dm8[...].astype(BF16), w_ref[...])
        gc_ref[...] = da[0:1, :] * da_c

    return pl.pallas_call(
        body, name="ada_bwd",
        out_shape=(jax.ShapeDtypeStruct((D, 768), F32), jax.ShapeDtypeStruct((1, 3 * D), F32),
                   jax.ShapeDtypeStruct((1, D), F32)),
        in_specs=[VMEM] * 7, out_specs=(VMEM,) * 3,
        scratch_shapes=[pltpu.VMEM((16, D), F32), pltpu.VMEM((16, 768), F32), pltpu.VMEM((8, 3 * D), F32)],
        compiler_params=_cp(vmem_mb=32),
    )(c_all, dmx_all_j, dmc_j, dmx_all, dmc, c_ctx, ada_w_full)


def _proj(x, mods, mrow, norm_g, w_full, nk, name):
    lx = x.shape[0]
    tp = min(lx, TP)
    n = lx // tp

    def body(x_ref, sh_ref, sc_ref, ng_ref, *rest):
        w_refs, hn_ref, z_refs = rest[:nk], rest[nk], rest[nk + 1:]
        xv = x_ref[...]
        r = lax.rsqrt(jnp.mean(xv * xv, axis=-1, keepdims=True) + NORM_EPS)
        hn = (xv * r) * ng_ref[...] * (1.0 + sc_ref[mrow:mrow + 1, :]) + sh_ref[mrow:mrow + 1, :]
        hb = hn.astype(BF16)
        hn_ref[...] = hb
        for k in range(nk):
            z_refs[k][...] = _dot(hb, w_refs[k][...])

    row = pl.BlockSpec((tp, D), lambda i: (i, 0))
    in_specs = [row, pl.BlockSpec((8, D), lambda i: (0, 0)), pl.BlockSpec((8, D), lambda i: (0, 1)),
                pl.BlockSpec((1, D), lambda i: (0, 0))]
    in_specs += [pl.BlockSpec((D, D), lambda i, k=k: (0, k)) for k in range(nk)]
    out_shape = (jax.ShapeDtypeStruct((lx, D), BF16),) + tuple(jax.ShapeDtypeStruct((lx, D), F32) for _ in range(nk))
    return pl.pallas_call(
        body, name=name, grid=(n,), out_shape=out_shape, in_specs=in_specs, out_specs=(row,) * (nk + 1),
        compiler_params=_cp(1, vmem_mb=56),
    )(x, mods, mods, norm_g, *([w_full] * nk))


def _halo_specs(lx):
    last = lx // 8 - 1
    return [pl.BlockSpec((T, D), lambda i: (i, 0)),
            pl.BlockSpec((8, D), lambda i: (jnp.maximum(i * (T // 8) - 1, 0), 0)),
            pl.BlockSpec((8, D), lambda i: (jnp.minimum((i + 1) * (T // 8), last), 0))]


def _zhalo_specs(lx):
    last = lx // 8 - 1
    return [pl.BlockSpec((T * NH, HD), lambda i: (i, 0)),
            pl.BlockSpec((8 * NH, HD), lambda i: (jnp.maximum(i * (T // 8) - 1, 0), 0)),
            pl.BlockSpec((8 * NH, HD), lambda i: (jnp.minimum((i + 1) * (T // 8), last), 0))]


ZT = pl.BlockSpec((T * NH, HD), lambda i: (i, 0))
CONV_CHUNK = 32


SCAN_SUB = 4


def _scan_tile(chains, post, carry_ref):
    blk = T // SCAN_SUB

    def step(k, state):
        new = []
        for ci, (a_ref, x_ref, o_ref, q_ref, reverse) in enumerate(chains):
            for q in range(SCAN_SUB):
                s, p = state[ci * SCAN_SUB + q]
                t = (q + 1) * blk - 1 - k if reverse else q * blk + k
                r = pl.ds(_mo(t * NH, NH), NH)
                a = a_ref[r, :]
                if post:
                    o = x_ref[r, :] + s
                    o_ref[r, :] = o
                    q_ref[r, :] = p
                    new.append((a * o, a * p))
                else:
                    o = a * s + x_ref[r, :]
                    p = a * p
                    o_ref[r, :] = o
                    q_ref[r, :] = p
                    new.append((o, p))
        return tuple(new)

    zero = jnp.zeros((NH, HD), F32)
    one = jnp.ones((NH, HD), F32)
    final = lax.fori_loop(0, blk, step, tuple((zero, one) for _ in range(len(chains) * SCAN_SUB)), unroll=2)
    for ci, (a_ref, x_ref, o_ref, q_ref, reverse) in enumerate(chains):
        carry = carry_ref[ci]
        for q in (range(SCAN_SUB - 1, -1, -1) if reverse else range(SCAN_SUB)):
            rows = pl.ds(q * blk * NH, blk * NH)
            fixed = o_ref[rows, :].reshape(blk, NH, HD) + q_ref[rows, :].reshape(blk, NH, HD) * carry[None]
            o_ref[rows, :] = fixed.reshape(blk * NH, HD)
            s_loc, p_loc = final[ci * SCAN_SUB + q]
            carry = s_loc + p_loc * carry
        carry_ref[ci] = carry


def _lru_fwd(xa, conv_wz, conv_bz, wcat, bcat, lamcat, s_f, s_b, name):
    lx = xa.shape[0]
    n = lx // T

    def body(xm_u, xp_u, xn_u, xm_d, xp_d, xn_d, cw, cb, w_ref, b_ref, lam_ref, su0, sd0,
             xaz_o, xcz_o, af_o, ab_o, hf_o, hb_o, gf_o, gb_o, fu, fd, pad, xc_d, x_u, x_d, q_u, q_d, carry):
        i = pl.program_id(0)

        @pl.when(i == 0)
        def _():
            carry[0] = su0[...]
            carry[1] = sd0[...]

        def conv_gates(xm, xp, xn, tile, d, xc_ref, a_ref, x_ref, xaz_ref, g_ref):
            pmask = jnp.where(tile == 0, 0.0, 1.0)
            nmask = jnp.where(tile == n - 1, 0.0, 1.0)
            for h in range(NH):
                cols = slice(HD * h, HD * h + HD)
                pad[_zrows(h, 8), :] = xp[:, cols] * pmask
                pad[pl.ds(8 * NH + h, T, stride=NH), :] = xm[:, cols]
                pad[pl.ds((T + 8) * NH + h, 8, stride=NH), :] = xn[:, cols] * nmask
            if xaz_ref is not None:
                xaz_ref[...] = pad[pl.ds(8 * NH, T * NH), :]

            def conv_chunk(ci, c_):
                base = pl.multiple_of(ci * (CONV_CHUNK * NH), CONV_CHUNK * NH)
                acc = None
                for k in range(4):
                    sl = pad[pl.ds(base + (7 + k) * NH, CONV_CHUNK * NH), :].reshape(CONV_CHUNK, NH, HD)
                    term = sl * cw[k][None]
                    acc = term if acc is None else acc + term
                acc = acc + cb[...][None]
                xc_ref[pl.ds(base, CONV_CHUNK * NH), :] = acc.reshape(CONV_CHUNK * NH, HD)
                return c_
            lax.fori_loop(0, T // CONV_CHUNK, conv_chunk, 0)

            for h in range(NH):
                xch = xc_ref[_zrows(h, T), :]
                pre = _dot(xch.astype(BF16), w_ref[h, :, 256 * d:256 * d + 256]) + b_ref[h:h + 1, 256 * d:256 * d + 256]
                r, gi, _, _, a, mult = _lru_gate(pre, lam_ref[h:h + 1, :], d, 0)
                a_ref[_zrows(h, T), :] = a
                x_ref[_zrows(h, T), :] = mult * gi * xch
                for q, val in enumerate((r, gi, mult)):
                    g_ref[:, q * D + HD * h:q * D + HD * h + HD] = val

        conv_gates(xm_u, xp_u, xn_u, i, 0, xcz_o, af_o, x_u, xaz_o, gf_o)
        conv_gates(xm_d, xp_d, xn_d, n - 1 - i, 1, xc_d, ab_o, x_d, None, gb_o)

        _scan_tile([(af_o, x_u, hf_o, q_u, False), (ab_o, x_d, hb_o, q_d, True)], False, carry)
        fu[...] = carry[0]
        fd[...] = carry[1]

    full = lambda shape: pl.BlockSpec(shape, lambda i: (0,) * len(shape))
    last = lx // 8 - 1
    rev = lambda i: n - 1 - i
    halo_dn = [pl.BlockSpec((T, D), lambda i: (rev(i), 0)),
               pl.BlockSpec((8, D), lambda i: (jnp.maximum(rev(i) * (T // 8) - 1, 0), 0)),
               pl.BlockSpec((8, D), lambda i: (jnp.minimum((rev(i) + 1) * (T // 8), last), 0))]
    st = full((NH, HD))
    in_specs = _halo_specs(lx) + halo_dn + [full((4, NH, HD)), st, full((NH, HD, 4 * HD)), full((NH, 4 * HD)),
                                            full((NH, 2 * HD)), st, st]
    dn = pl.BlockSpec((T * NH, HD), lambda i: (rev(i), 0))
    zs = jax.ShapeDtypeStruct((lx * NH, HD), F32)
    ss = jax.ShapeDtypeStruct((NH, HD), F32)
    zbuf = pltpu.VMEM((T * NH, HD), F32)
    gs = jax.ShapeDtypeStruct((lx, 3 * D), F32)
    g_up = pl.BlockSpec((T, 3 * D), lambda i: (i, 0))
    g_dn = pl.BlockSpec((T, 3 * D), lambda i: (rev(i), 0))
    return pl.pallas_call(
        body, name=name, grid=(n,), out_shape=(zs,) * 6 + (gs, gs, ss, ss), in_specs=in_specs,
        out_specs=(ZT, ZT, ZT, dn, ZT, dn, g_up, g_dn, st, st),
        scratch_shapes=[pltpu.VMEM(((T + 16) * NH, HD), F32), zbuf, zbuf, zbuf, zbuf, zbuf,
                        pltpu.VMEM((2, NH, HD), F32)],
        compiler_params=_cp(1, vmem_mb=48),
    )(xa, xa, xa, xa, xa, xa, conv_wz, conv_bz, wcat, bcat, lamcat, s_f, s_b)


def _scan_adjoint(a_up, x_up, a_dn, x_dn, name):
    lx = a_up.shape[0] // NH
    n = lx // T

    def body(au, xu, ad, xd, ou, od, fu, fd, q_u, q_d, carry):
        @pl.when(pl.program_id(0) == 0)
        def _():
            carry[...] = jnp.zeros_like(carry)

        _scan_tile([(au, xu, ou, q_u, False), (ad, xd, od, q_d, True)], True, carry)
        fu[...] = carry[0]
        fd[...] = carry[1]

    up = pl.BlockSpec((T * NH, HD), lambda i: (i, 0))
    dn = pl.BlockSpec((T * NH, HD), lambda i: (n - 1 - i, 0))
    st = pl.BlockSpec((NH, HD), lambda i: (0, 0))
    zs = jax.ShapeDtypeStruct((lx * NH, HD), F32)
    ss = jax.ShapeDtypeStruct((NH, HD), F32)
    zbuf = pltpu.VMEM((T * NH, HD), F32)
    return pl.pallas_call(
        body, name=name, grid=(n,), out_shape=(zs, zs, ss, ss), in_specs=[up, up, dn, dn],
        out_specs=(up, dn, st, st), scratch_shapes=[zbuf, zbuf, pltpu.VMEM((2, NH, HD), F32)],
        compiler_params=_cp(1, vmem_mb=48),
    )(a_up, x_up, a_dn, x_dn)


def _sgu_parts(u, v, lng, lnb, w_ref, bt_ref, mixed_s):
    ug, dug = _gelu_and_grad(u)
    vg, dvg = _gelu_and_grad(v)
    mu = jnp.mean(vg, axis=-1, keepdims=True)
    vc = vg - mu
    rstd = lax.rsqrt(jnp.mean(vc * vc, axis=-1, keepdims=True) + LN_EPS)
    vh = vc * rstd
    vn = (vh * lng + lnb).astype(BF16)
    for g in range(NH):
        cols = slice(HD * g, HD * g + HD)
        mixed_s[:, cols] = _dot(w_ref[g], vn[:, cols]) + bt_ref[:, g:g + 1]
    return ug, dug, dvg, rstd, vh, vn


def _sgu_fwd(u, v, ln_g, ln_b, sgu_w, sgu_bt):
    lx = u.shape[0]
    n = lx // HD

    def body(u_ref, v_ref, g_ref, b_ref, w_ref, bt_ref, y_ref, mixed_s):
        ug, _, _, _, _, _ = _sgu_parts(u_ref[...], v_ref[...], g_ref[...], b_ref[...], w_ref, bt_ref, mixed_s)
        y_ref[...] = ug * mixed_s[...]

    row = pl.BlockSpec((HD, D), lambda i: (i, 0))
    vec = pl.BlockSpec((1, D), lambda i: (0, 0))
    return pl.pallas_call(
        body, name="sgu_fwd", grid=(n,), out_shape=jax.ShapeDtypeStruct((lx, D), F32),
        in_specs=[row, row, vec, vec, pl.BlockSpec((NH, HD, HD), lambda i: (0, 0, 0)),
                  pl.BlockSpec((HD, NH), lambda i: (0, 0))],
        out_specs=row, scratch_shapes=[pltpu.VMEM((HD, D), F32)],
        compiler_params=_cp(1),
    )(u, v, ln_g, ln_b, sgu_w, sgu_bt)


def _sgu_bwd(u, v, dys, ln_g, ln_b, sgu_w, sgu_bt):
    lx = u.shape[0]
    n = lx // HD

    def body(u_ref, v_ref, dy_ref, g_ref, b_ref, w_ref, bt_ref, du_ref, dv_ref, dw_ref, db_ref, dg_ref, dbl_ref,
             mixed_s, dvn_s):
        i = pl.program_id(0)

        @pl.when(i == 0)
        def _():
            dw_ref[...] = jnp.zeros_like(dw_ref)
            db_ref[...] = jnp.zeros_like(db_ref)
            dg_ref[...] = jnp.zeros_like(dg_ref)
            dbl_ref[...] = jnp.zeros_like(dbl_ref)

        lng = g_ref[...]
        ug, dug, dvg, rstd, vh, vn = _sgu_parts(u_ref[...], v_ref[...], lng, b_ref[...], w_ref, bt_ref, mixed_s)
        dys_v = dy_ref[...]
        du_ref[...] = (dys_v * mixed_s[...] * dug).astype(BF16)
        dmix = dys_v * ug
        ones = jnp.ones((8, HD), BF16)
        for g in range(NH):
            cols = slice(HD * g, HD * g + HD)
            dm = dmix[:, cols]
            hi = dm.astype(BF16)
            lo = (dm - hi.astype(F32)).astype(BF16)
            dw_ref[g] += _dot_nt(hi, vn[:, cols])
            db_ref[g:g + 1, :] += (_dot_nt(ones, hi) + _dot_nt(ones, lo))[0:1, :]
            dvn_s[:, cols] = _dot_tn(w_ref[g], hi)
        dvn = dvn_s[...]
        dg_ref[...] += jnp.sum(dvn * vh, axis=0, keepdims=True)
        dbl_ref[...] += jnp.sum(dvn, axis=0, keepdims=True)
        dvh = dvn * lng
        dvg_in = rstd * (dvh - jnp.mean(dvh, axis=-1, keepdims=True)
                         - vh * jnp.mean(dvh * vh, axis=-1, keepdims=True))
        dv_ref[...] = (dvg_in * dvg).astype(BF16)

    row = pl.BlockSpec((HD, D), lambda i: (i, 0))
    vec = pl.BlockSpec((1, D), lambda i: (0, 0))
    wsp = pl.BlockSpec((NH, HD, HD), lambda i: (0, 0, 0))
    bsp = pl.BlockSpec((NH, HD), lambda i: (0, 0))
    return pl.pallas_call(
        body, name="sgu_bwd", grid=(n,),
        out_shape=(jax.ShapeDtypeStruct((lx, D), BF16), jax.ShapeDtypeStruct((lx, D), BF16),
                   jax.ShapeDtypeStruct((NH, HD, HD), F32), jax.ShapeDtypeStruct((NH, HD), F32),
                   jax.ShapeDtypeStruct((1, D), F32), jax.ShapeDtypeStruct((1, D), F32)),
        in_specs=[row, row, row, vec, vec, wsp, pl.BlockSpec((HD, NH), lambda i: (0, 0))],
        out_specs=(row, row, wsp, bsp, vec, vec),
        scratch_shapes=[pltpu.VMEM((HD, D), F32), pltpu.VMEM((HD, D), F32)],
        compiler_params=_cp(1),
    )(u, v, dys, ln_g, ln_b, sgu_w, sgu_bt)


def _out_fwd_bwd(hf_z, hb_z, ga, gb, ys, x, tgt, mods, final_g, w_out_full):
    lx = x.shape[0]
    n = lx // T

    def body(hf_ref, hb_ref, ga_ref, gb_ref, ys_ref, x_ref, t_ref, gx_ref, fg_ref, w_ref,
             loss_ref, dfg_ref, dgx_ref, dxn_ref, y_ref, do_ref, dga_ref, dgb_ref, dyl_ref, dys_ref, yl_s):
        i = pl.program_id(0)

        @pl.when(i == 0)
        def _():
            loss_ref[...] = jnp.zeros_like(loss_ref)
            dfg_ref[...] = jnp.zeros_like(dfg_ref)
            dgx_ref[...] = jnp.zeros_like(dgx_ref)

        for h in range(NH):
            yl_s[:, HD * h:HD * h + HD] = hf_ref[_zrows(h, T), :] + hb_ref[_zrows(h, T), :]
        yl = yl_s[...]
        gav = ga_ref[...]
        gbv = gb_ref[...]
        sa, dsa = _silu_and_grad(gav)
        sb, dsb = _silu_and_grad(gbv)
        ysv = ys_ref[...]
        y_ref[:, 0:D] = (yl * sa).astype(BF16)
        y_ref[:, D:2 * D] = (ysv * sb).astype(BF16)
        o = _dot(y_ref[...], w_ref[...])
        gx = gx_ref[0:1, :]
        xnew = x_ref[...] + gx * o
        r2 = lax.rsqrt(jnp.mean(xnew * xnew, axis=-1, keepdims=True) + NORM_EPS)
        xh = xnew * r2
        fg = fg_ref[...]
        err = xh * fg - t_ref[...]
        loss_ref[...] += 0.5 * jnp.sum(jnp.mean(err * err, axis=-1, keepdims=True), axis=0, keepdims=True)
        dout = err * (1.0 / D)
        dfg_ref[...] += jnp.sum(dout * xh, axis=0, keepdims=True)
        dxh = dout * fg
        dxn = r2 * (dxh - xh * jnp.mean(dxh * xh, axis=-1, keepdims=True))
        dxn_ref[...] = dxn
        dgx_ref[...] += jnp.sum(dxn * o, axis=0, keepdims=True)
        do = (dxn * gx).astype(BF16)
        do_ref[...] = do
        dy = _dot_nt(do, w_ref[...])
        dy1 = dy[:, 0:D]
        dy2 = dy[:, D:2 * D]
        dga_ref[...] = (dy1 * yl * dsa).astype(BF16)
        dgb_ref[...] = (dy2 * ysv * dsb).astype(BF16)
        dys_ref[...] = dy2 * sb
        yl_s[...] = dy1 * sa
        for h in range(NH):
            dyl_ref[_zrows(h, T), :] = yl_s[:, HD * h:HD * h + HD]

    row = pl.BlockSpec((T, D), lambda i: (i, 0))
    vec = pl.BlockSpec((1, D), lambda i: (0, 0))
    in_specs = [ZT, ZT, row, row, row, row, row, pl.BlockSpec((8, D), lambda i: (0, 2)), vec,
                pl.BlockSpec((2 * D, D), lambda i: (0, 0))]
    out_shape = (jax.ShapeDtypeStruct((1, 1), F32), jax.ShapeDtypeStruct((1, D), F32), jax.ShapeDtypeStruct((1, D), F32),
                 jax.ShapeDtypeStruct((lx, D), F32), jax.ShapeDtypeStruct((lx, 2 * D), BF16),
                 jax.ShapeDtypeStruct((lx, D), BF16), jax.ShapeDtypeStruct((lx, D), BF16),
                 jax.ShapeDtypeStruct((lx, D), BF16), jax.ShapeDtypeStruct((lx * NH, HD), F32),
                 jax.ShapeDtypeStruct((lx, D), F32))
    out_specs = (pl.BlockSpec((1, 1), lambda i: (0, 0)), vec, vec, row, pl.BlockSpec((T, 2 * D), lambda i: (i, 0)),
                 row, row, row, ZT, row)
    return pl.pallas_call(
        body, name="out_fwd_bwd", grid=(n,), out_shape=out_shape, in_specs=in_specs, out_specs=out_specs,
        scratch_shapes=[pltpu.VMEM((T, D), F32)],
        compiler_params=_cp(1, vmem_mb=56),
    )(hf_z, hb_z, ga, gb, ys, x, tgt, mods, final_g, w_out_full)


def _lru_gates_bwd(xc_z, lf_z, lb_z, hf_z, hb_z, af_z, ab_z, gf, gb, s_f, s_b, wcat, lamcat, dw0, db0, dl0, name):
    lx = xc_z.shape[0] // NH
    n = lx // T

    def body(xc_ref, lf_ref, lb_ref, hf_ref, hfp_ref, hb_ref, hbn_ref, af_ref, ab_ref, gf_ref, gb_ref, sf_ref, sb_ref,
             w_ref, lam_ref, dw0_ref, db0_ref, dl0_ref, dxc_ref, dw_ref, db_ref, dl_ref, dpre_s, pf_s, pb_s):
        i = pl.program_id(0)

        @pl.when(i == 0)
        def _():
            dw_ref[...] = dw0_ref[...]
            db_ref[...] = db0_ref[...]
            dl_ref[...] = dl0_ref[...]

        pf_s[pl.ds(0, NH), :] = jnp.where(i == 0, sf_ref[...], hfp_ref[pl.ds(7 * NH, NH), :])
        pf_s[pl.ds(NH, T * NH), :] = hf_ref[...]
        pb_s[pl.ds(0, T * NH), :] = hb_ref[...]
        pb_s[pl.ds(T * NH, NH), :] = jnp.where(i == n - 1, sb_ref[...], hbn_ref[pl.ds(0, NH), :])
        lam_refs = (lf_ref, lb_ref)
        prev = ((pf_s, 0), (pb_s, NH))
        a_refs = (af_ref, ab_ref)
        g_refs = (gf_ref, gb_ref)
        for h in range(NH):
            xch = xc_ref[_zrows(h, T), :]
            xcb = xch.astype(BF16)
            dxc = jnp.zeros((T, HD), F32)
            for d in range(2):
                r, gi, mult = (g_refs[d][:, q * D + HD * h:q * D + HD * h + HD] for q in range(3))
                a = a_refs[d][_zrows(h, T), :]
                lam = lam_ref[h:h + 1, HD * d:HD * d + HD]
                sp = _softplus(-lam)
                du = lam_refs[d][_zrows(h, T), :]
                da = du * prev[d][0][pl.ds(prev[d][1] + h, T, stride=NH), :]
                dgi = du * mult * xch
                dxc = dxc + du * mult * gi
                dmult = du * gi * xch
                dla = da * a - dmult * (a * a) / mult
                dr = dla * ((-LRU_C) * sp)
                dsp = jnp.sum(dla * ((-LRU_C) * r), axis=0, keepdims=True)
                dl_ref[h:h + 1, HD * d:HD * d + HD] += dsp * (-_sigmoid(-lam))
                dpre_s[:, 256 * d:256 * d + HD] = dr * r * (1.0 - r)
                dpre_s[:, 256 * d + HD:256 * d + 2 * HD] = dgi * gi * (1.0 - gi)
            dpre = dpre_s[...]
            dpb = dpre.astype(BF16)
            dw_ref[h] += _dot_tn(xcb, dpb)
            db_ref[h:h + 1, :] += jnp.sum(dpre, axis=0, keepdims=True)
            dxc_ref[_zrows(h, T), :] = dxc + _dot_nt(dpb, w_ref[h])

    full = lambda shape: pl.BlockSpec(shape, lambda i: (0,) * len(shape))
    wsp, bsp, lsp = full((NH, HD, 4 * HD)), full((NH, 4 * HD)), full((NH, 2 * HD))
    st = full((NH, HD))
    zh = _zhalo_specs(lx)
    gsp = pl.BlockSpec((T, 3 * D), lambda i: (i, 0))
    return pl.pallas_call(
        body, name=name, grid=(n,),
        out_shape=(jax.ShapeDtypeStruct((lx * NH, HD), F32), jax.ShapeDtypeStruct((NH, HD, 4 * HD), F32),
                   jax.ShapeDtypeStruct((NH, 4 * HD), F32), jax.ShapeDtypeStruct((NH, 2 * HD), F32)),
        in_specs=[ZT] * 3 + zh[0:2] + [zh[0], zh[2], ZT, ZT, gsp, gsp, st, st, wsp, lsp, wsp, bsp, lsp],
        out_specs=(ZT, wsp, bsp, lsp),
        scratch_shapes=[pltpu.VMEM((T, 4 * HD), F32), pltpu.VMEM(((T + 1) * NH, HD), F32),
                        pltpu.VMEM(((T + 1) * NH, HD), F32)],
        compiler_params=_cp(1, vmem_mb=56),
    )(xc_z, lf_z, lb_z, hf_z, hf_z, hb_z, hb_z, af_z, ab_z, gf, gb, s_f, s_b, wcat, lamcat, dw0, db0, dl0)


def _conv_bwd(dxc_z, xa_z, conv_wz, dcw0, dcb0, name):
    lx = dxc_z.shape[0] // NH
    n = lx // T

    def body(dm, dp, dn, xa_ref, cw, dcw0_ref, dcb0_ref, dxa_ref, dcw_ref, dcb_ref, pad, dxa_s):
        i = pl.program_id(0)

        @pl.when(i == 0)
        def _():
            dcw_ref[...] = dcw0_ref[...]
            dcb_ref[...] = dcb0_ref[...]

        pmask = jnp.where(i == 0, 0.0, 1.0)
        nmask = jnp.where(i == n - 1, 0.0, 1.0)
        pad[pl.ds(0, 8 * NH), :] = dp[...] * pmask
        pad[pl.ds(8 * NH, T * NH), :] = dm[...]
        pad[pl.ds((T + 8) * NH, 8 * NH), :] = dn[...] * nmask

        def chunk(ci, carry):
            base = pl.multiple_of(ci * (CONV_CHUNK * NH), CONV_CHUNK * NH)
            xav = xa_ref[pl.ds(base, CONV_CHUNK * NH), :].reshape(CONV_CHUNK, NH, HD)
            acc = None
            for k in range(4):
                sl = pad[pl.ds(base + (9 - k) * NH, CONV_CHUNK * NH), :].reshape(CONV_CHUNK, NH, HD)
                term = sl * cw[k][None]
                acc = term if acc is None else acc + term
                dcw_ref[k] += jnp.sum(sl * xav, axis=0)
                if k == 1:
                    dcb_ref[...] += jnp.sum(sl, axis=0)
            dxa_s[pl.ds(base, CONV_CHUNK * NH), :] = acc.reshape(CONV_CHUNK * NH, HD)
            return carry
        lax.fori_loop(0, T // CONV_CHUNK, chunk, 0)
        for h in range(NH):
            dxa_ref[:, HD * h:HD * h + HD] = dxa_s[_zrows(h, T), :].astype(BF16)

    full = lambda shape: pl.BlockSpec(shape, lambda i: (0,) * len(shape))
    return pl.pallas_call(
        body, name=name, grid=(n,),
        out_shape=(jax.ShapeDtypeStruct((lx, D), BF16), jax.ShapeDtypeStruct((4, NH, HD), F32),
                   jax.ShapeDtypeStruct((NH, HD), F32)),
        in_specs=_zhalo_specs(lx) + [ZT, full((4, NH, HD)), full((4, NH, HD)), full((NH, HD))],
        out_specs=(pl.BlockSpec((T, D), lambda i: (i, 0)), full((4, NH, HD)), full((NH, HD))),
        scratch_shapes=[pltpu.VMEM(((T + 16) * NH, HD), F32), pltpu.VMEM((T * NH, HD), F32)],
        compiler_params=_cp(1, vmem_mb=48),
    )(dxc_z, dxc_z, dxc_z, xa_z, conv_wz, dcw0, dcb0)


def _proj_bwd(dzs, x, dxn, mods, mrow, norm_g, w_full, dng0, name):
    lx = x.shape[0]
    tp = min(lx, TP)
    n = lx // tp
    nk = len(dzs)
    has_x = dxn is not None

    def body(*refs):
        dz_refs = refs[:nk]
        w_refs = refs[nk:2 * nk]
        x_ref, sc_ref, ng_ref, dng0_ref = refs[2 * nk:2 * nk + 4]
        rest = refs[2 * nk + 4:]
        if has_x:
            dxn_ref, gx_ref, dng_ref, dsc_ref, dsh_ref = rest
        else:
            dng_ref, dsc_ref, dsh_ref = rest
        i = pl.program_id(0)

        @pl.when(i == 0)
        def _():
            dng_ref[...] = dng0_ref[...]
            dsc_ref[...] = jnp.zeros_like(dsc_ref)
            dsh_ref[...] = jnp.zeros_like(dsh_ref)

        dhn = _dot_nt(dz_refs[0][...], w_refs[0][...])
        for k in range(1, nk):
            dhn = dhn + _dot_nt(dz_refs[k][...], w_refs[k][...])
        xv = x_ref[...]
        r = lax.rsqrt(jnp.mean(xv * xv, axis=-1, keepdims=True) + NORM_EPS)
        xn = xv * r
        ng = ng_ref[...]
        sc1 = 1.0 + sc_ref[mrow:mrow + 1, :]
        t = dhn * xn
        dng_ref[...] += jnp.sum(t * sc1, axis=0, keepdims=True)
        dsc_ref[...] += jnp.sum(t * ng, axis=0, keepdims=True)
        dsh_ref[...] += jnp.sum(dhn, axis=0, keepdims=True)
        if has_x:
            dxh = dhn * (ng * sc1)
            gx_ref[...] = dxn_ref[...] + r * (dxh - xn * jnp.mean(dxh * xn, axis=-1, keepdims=True))

    row = pl.BlockSpec((tp, D), lambda i: (i, 0))
    vec = pl.BlockSpec((1, D), lambda i: (0, 0))
    in_specs = [row] * nk + [pl.BlockSpec((D, D), lambda i, k=k: (0, k)) for k in range(nk)]
    in_specs += [row, pl.BlockSpec((8, D), lambda i: (0, 1)), vec, vec]
    args = list(dzs) + [w_full] * nk + [x, mods, norm_g, dng0]
    vs = jax.ShapeDtypeStruct((1, D), F32)
    out_shape, out_specs = (vs, vs, vs), (vec, vec, vec)
    if has_x:
        in_specs.append(row)
        args.append(dxn)
        out_shape = (jax.ShapeDtypeStruct((lx, D), F32),) + out_shape
        out_specs = (row,) + out_specs
    return pl.pallas_call(
        body, name=name, grid=(n,), out_shape=out_shape, in_specs=in_specs, out_specs=out_specs,
        compiler_params=_cp(1, vmem_mb=56),
    )(*args)


def _tn_matmul(a, bs, extra, name):
    lx, m = a.shape
    tm = min(lx, 1024)
    n = lx // tm
    widths = [b.shape[1] for b in bs]
    nb = len(bs)

    def body(a_ref, *rest):
        b_refs = rest[:nb]
        rest = rest[nb:]
        if extra is not None:
            ea_ref, eb_ref = rest[:2]
            rest = rest[2:]
        out_ref, acc = rest
        i = pl.program_id(0)
        av = a_ref[...]
        off = 0
        for k in range(nb):
            cols = slice(off, off + widths[k])
            part = _dot_tn(av, b_refs[k][...])

            @pl.when(i == 0)
            def _():
                acc[:, cols] = part

            @pl.when(i > 0)
            def _():
                acc[:, cols] += part
            off += widths[k]

        @pl.when(i == n - 1)
        def _():
            if extra is not None:
                acc[:, 0:widths[0]] += _dot_tn(ea_ref[...], eb_ref[...])
            pltpu.sync_copy(acc, out_ref)

    in_specs = [pl.BlockSpec((tm, m), lambda i: (i, 0))] + [pl.BlockSpec((tm, w), lambda i: (i, 0)) for w in widths]
    args = [a] + list(bs)
    if extra is not None:
        in_specs += [VMEM, VMEM]
        args += list(extra)
    return pl.pallas_call(
        body, name=name, grid=(n,), out_shape=jax.ShapeDtypeStruct((m, sum(widths)), F32),
        in_specs=in_specs, out_specs=ANY, scratch_shapes=[pltpu.VMEM((m, sum(widths)), F32)],
        compiler_params=_cp(1, vmem_mb=56),
    )(*args)


def _adam_math(w, g, m, v):
    m = ADAM_B1 * m + (1.0 - ADAM_B1) * g
    v = ADAM_B2 * v + (1.0 - ADAM_B2) * (g * g)
    m_hat = m / (1.0 - ADAM_B1 ** ADAM_STEP)
    v_hat = v / (1.0 - ADAM_B2 ** ADAM_STEP)
    delta = -ADAM_LR * (m_hat / (jnp.sqrt(v_hat) + ADAM_EPS) + ADAM_WD * w)
    return delta, m, v


def _adam_big(w, g, m, v, name):
    rows, cols = w.shape
    tr = 256

    def body(w_ref, g_ref, m_ref, v_ref, d_o, m_o, v_o):
        d, mm, vv = _adam_math(w_ref[...], g_ref[...], m_ref[...], v_ref[...])
        d_o[...] = d
        m_o[...] = mm
        v_o[...] = vv

    blk = pl.BlockSpec((tr, cols), lambda i: (i, 0))
    s = jax.ShapeDtypeStruct((rows, cols), F32)
    return pl.pallas_call(
        body, name=name, grid=(rows // tr,), out_shape=(s, s, s), in_specs=[blk] * 4, out_specs=(blk,) * 3,
        compiler_params=_cp(1, vmem_mb=48),
    )(w, g, m, v)


def _adam_small(items):
    ni = len(items)

    def body(*refs):
        ins, outs = refs[:4 * ni], refs[4 * ni:7 * ni]
        bufs_in, bufs_out = refs[7 * ni:11 * ni], refs[11 * ni:14 * ni]
        sem_in, sem_out = refs[14 * ni], refs[14 * ni + 1]
        loads = [pltpu.make_async_copy(ins[q], bufs_in[q], sem_in.at[q]) for q in range(4 * ni)]
        for cp in loads:
            cp.start()
        stores = []
        for k in range(ni):
            for q in range(4):
                loads[4 * k + q].wait()
            w_b, g_b, m_b, v_b = bufs_in[4 * k:4 * k + 4]
            res = _adam_math(w_b[...], g_b[...], m_b[...], v_b[...])
            for q in range(3):
                bufs_out[3 * k + q][...] = res[q]
                cp = pltpu.make_async_copy(bufs_out[3 * k + q], outs[3 * k + q], sem_out.at[3 * k + q])
                cp.start()
                stores.append(cp)
        for cp in stores:
            cp.wait()

    flat = [a for it in items for a in it]
    out_shape = tuple(jax.ShapeDtypeStruct(it[0].shape, F32) for it in items for _ in range(3))
    scratch = [pltpu.VMEM(a.shape, F32) for a in flat] + [pltpu.VMEM(s.shape, F32) for s in out_shape]
    scratch += [pltpu.SemaphoreType.DMA((4 * ni,)), pltpu.SemaphoreType.DMA((3 * ni,))]
    res = pl.pallas_call(
        body, name="adam_small", out_shape=out_shape, in_specs=[HBM] * (4 * ni), out_specs=(HBM,) * (3 * ni),
        scratch_shapes=scratch, compiler_params=_cp(vmem_mb=40),
    )(*flat)
    return [tuple(res[3 * k:3 * k + 3]) for k in range(ni)]


def kernel(x, c, ctx, c_ctx, ada_w, ada_b, norm_g, w_in, conv_w, conv_b, lru_wa, lru_ba, lru_wx, lru_bx, lru_lambda, sgu_ln_g, sgu_ln_b, sgu_w, sgu_b, w_out, final_g, loss_target, m_c_ctx, m_ada_w, m_ada_b, m_norm_g, m_w_in, m_conv_w, m_conv_b, m_lru_wa, m_lru_ba, m_lru_wx, m_lru_bx, m_lru_lambda, m_sgu_ln_g, m_sgu_ln_b, m_sgu_w, m_sgu_b, m_w_out, m_final_g, v_c_ctx, v_ada_w, v_ada_b, v_norm_g, v_w_in, v_conv_w, v_conv_b, v_lru_wa, v_lru_ba, v_lru_wx, v_lru_bx, v_lru_lambda, v_sgu_ln_g, v_sgu_ln_b, v_sgu_w, v_sgu_b, v_w_out, v_final_g):
    ix, iy, ic = lax.axis_index("x"), lax.axis_index("y"), lax.axis_index("c")
    chip = 2 * ix + iy
    dev = 2 * chip + ic
    lx = x.shape[1]
    lc = ctx.shape[1]

    smalls = jnp.concatenate([conv_w[0], lru_lambda[0], jnp.zeros((10, 256), F32)], axis=0)
    c_ctx2 = c_ctx.reshape(1, D)
    ada_b_j = lax.dynamic_slice(ada_b, (0, 768 * chip), (1, 768))
    mods, c_slots, sm_all, w_in_full, wo_land, ada_land = _gather_in(c, c_ctx2, ada_w[0], ada_b_j, w_in[0], w_out[0],
                                                                     smalls)
    wo_ss, wo_rs, ada_ss, ada_rs, wo_land, ada_land, token = _late_gather_start(wo_land, ada_land)
    mods = mods + token[0:1, 0:1]
    sm3 = sm_all.reshape(NCHIP, 16, 256)
    conv_w_full = sm3[:, 0:4, :].transpose(1, 0, 2).reshape(4, D)
    lam_full = sm3[:, 4:6, :].transpose(1, 0, 2).reshape(2, D)
    conv_wz = conv_w_full.reshape(4, NH, HD)
    conv_bz = conv_b.reshape(NH, HD)
    lamcat = lam_full.reshape(2, NH, HD).transpose(1, 0, 2).reshape(NH, 2 * HD)
    wa, wx, ba, bx = lru_wa[0], lru_wx[0], lru_ba[0], lru_bx[0]
    wcat = jnp.concatenate([wa[0], wx[0], wa[1], wx[1]], axis=-1).astype(BF16)
    bcat = jnp.concatenate([ba[0], bx[0], ba[1], bx[1]], axis=-1)
    sgu_wb = sgu_w[0].astype(BF16)
    sgu_bt = sgu_b[0].T
    final_g2 = final_g.reshape(1, D)

    zero_s = jnp.zeros((NH, HD), F32)
    hn_c, xa_c = _proj(ctx[0], mods, 1, norm_g, w_in_full, 1, "proj_ctx")
    xaz_c, xcz_c, af_c, ab_c, hf_c, hb_c, gf_c, gb_c, hf0, hb0 = _lru_fwd(xa_c, conv_wz, conv_bz, wcat, bcat, lamcat,
                                                                           zero_s, zero_s, "lru_fwd_ctx")

    hn, xa, ga, u, v, gb = _proj(x[0], mods, 0, norm_g, w_in_full, 5, "proj")
    xaz, xcz, af, ab, hf, hb, gf, gb_l, _, _ = _lru_fwd(xa, conv_wz, conv_bz, wcat, bcat, lamcat, hf0, hb0, "lru_fwd")
    ys = _sgu_fwd(u, v, sgu_ln_g, sgu_ln_b, sgu_wb, sgu_bt)

    w_out_full = _late_gather_wait(wo_land, wo_ss, wo_rs, "w_out", ys, "late_gather_wait_w_out")
    (loss_part, dfg, dgx, dxn, y, do, dga, dgb, dyl_z, dys) = _out_fwd_bwd(
        hf, hb, ga, gb, ys, x[0], loss_target[0], mods, final_g2, w_out_full)
    g_w_out_part = _tn_matmul(y, [do], None, "grad_w_out")

    du, dv, d_sgu_w, d_sgu_b, d_ln_g, d_ln_b = _sgu_bwd(u, v, dys, sgu_ln_g, sgu_ln_b, sgu_wb, sgu_bt)
    lb, lf, dh0b, dh0f = _scan_adjoint(ab, dyl_z, af, dyl_z, "scan_adj")
    zw = jnp.zeros((NH, HD, 4 * HD), F32)
    zb = jnp.zeros((NH, 4 * HD), F32)
    zl = jnp.zeros((NH, 2 * HD), F32)
    dxc_z, dwc, dbc, dlc = _lru_gates_bwd(xcz, lf, lb, hf, hb, af, ab, gf, gb_l, hf0, hb0, wcat, lamcat, zw, zb, zl,
                                          "lru_gates_bwd")
    dxa, dcw, dcb = _conv_bwd(dxc_z, xaz, conv_wz, jnp.zeros((4, NH, HD), F32), zero_s, "conv_bwd")

    zc = jnp.zeros((lc * NH, HD), F32)
    dhf_c = lax.dynamic_update_slice(zc, dh0f, ((lc - 1) * NH, 0))
    dhb_c = lax.dynamic_update_slice(zc, dh0b, (0, 0))
    lb_c, lf_c, _, _ = _scan_adjoint(ab_c, dhb_c, af_c, dhf_c, "scan_adj_ctx")
    dxc_zc, dwc, dbc, dlc = _lru_gates_bwd(xcz_c, lf_c, lb_c, hf_c, hb_c, af_c, ab_c, gf_c, gb_c, zero_s, zero_s,
                                           wcat, lamcat, dwc, dbc, dlc, "lru_gates_bwd_ctx")
    dxa_c, dcw, dcb = _conv_bwd(dxc_zc, xaz_c, conv_wz, dcw, dcb, "conv_bwd_ctx")

    dzs = [dxa, dga, du, dv, dgb]
    grad_x, dng, dsc_x, dsh_x = _proj_bwd(dzs, x[0], dxn, mods, 0, norm_g, w_in_full, jnp.zeros((1, D), F32), "proj_bwd")
    dng, dsc_c, dsh_c = _proj_bwd([dxa_c], ctx[0], None, mods, 1, norm_g, w_in_full, dng, "proj_bwd_ctx")

    dmx = jnp.concatenate([dsh_x, dsc_x, dgx], axis=0)
    dmc = jnp.concatenate([dsh_c, dsc_c, jnp.zeros((1, D), F32)], axis=0)
    lp = loss_part[0, 0]
    lp1 = lax.reduce_precision(lp, 8, 7)
    lp2 = lax.reduce_precision(lp - lp1, 8, 7)
    lp3 = lax.reduce_precision(lp - lp1 - lp2, 8, 7)
    loss_row = jnp.pad(jnp.stack([lp1, lp2, lp3]).reshape(1, 3), ((0, 0), (0, D - 3)))
    slot = jnp.concatenate([dmx, loss_row], axis=0)
    slots = lax.dynamic_update_slice(jnp.zeros((32, D), F32), slot, (4 * dev, 0))
    vecs = jnp.concatenate([dfg, dng, dcb.reshape(1, D), d_ln_g, d_ln_b, dcw.reshape(4, D), dmc,
                            jnp.zeros((4, D), F32), slots], axis=0)
    d_sgu_w4 = d_sgu_w.reshape(4, 256, HD).transpose(1, 0, 2).reshape(256, 4 * HD)
    pad8 = lambda a: jnp.pad(a, ((0, 8 - a.shape[0]), (0, 4 * HD - a.shape[1])))
    pack = jnp.concatenate([dwc.reshape(NH * HD, 4 * HD), pad8(dbc), pad8(dlc), d_sgu_w4, pad8(d_sgu_b),
                            vecs.reshape(96, 4 * HD), jnp.zeros((8, 4 * HD), F32)], axis=0)
    g_w_in, g_w_out, tot = _grads_reduce(hn, dzs, hn_c, dxa_c, g_w_out_part, pack)

    g_wc = tot[0:1024].reshape(NH, HD, 4 * HD)
    g_bc = tot[1024:1032]
    g_lc = tot[1032:1040, 0:2 * HD]
    g_sgu_w = tot[1040:1296].reshape(256, 4, HD).transpose(1, 0, 2).reshape(NH, HD, HD)
    g_sgu_b = tot[1296:1304, 0:HD]
    tv = tot[1304:1400].reshape(48, D)
    g_final_g, g_norm_g, g_conv_b, g_ln_g, g_ln_b = tv[0:1], tv[1:2], tv[2:3], tv[3:4], tv[4:5]
    g_conv_w_full = tv[5:9]
    dmc_tot = tv[9:12].reshape(1, 3 * D)
    slots_all = tv[16:48].reshape(8, 4, D)
    dmx_all = slots_all[:, 0:3, :].reshape(8, 3 * D)
    c_all = c_slots.reshape(8, 8, D)[:, 0, :]
    g_lru_wa = jnp.stack([g_wc[:, :, 0:HD], g_wc[:, :, 2 * HD:3 * HD]])
    g_lru_wx = jnp.stack([g_wc[:, :, HD:2 * HD], g_wc[:, :, 3 * HD:4 * HD]])
    g_lru_ba = jnp.stack([g_bc[:, 0:HD], g_bc[:, 2 * HD:3 * HD]])
    g_lru_bx = jnp.stack([g_bc[:, HD:2 * HD], g_bc[:, 3 * HD:4 * HD]])
    g_lam_full = jnp.stack([g_lc[:, 0:HD], g_lc[:, HD:2 * HD]]).reshape(2, D)
    g_conv_w = lax.dynamic_slice(g_conv_w_full, (0, 256 * chip), (4, 256))
    g_lam = lax.dynamic_slice(g_lam_full, (0, 256 * chip), (2, 256))
    dmx_all_j = lax.dynamic_slice(dmx_all, (0, 768 * chip), (8, 768))
    dmc_j = lax.dynamic_slice(dmc_tot, (0, 768 * chip), (1, 768))
    ada_full = _late_gather_wait(ada_land, ada_ss, ada_rs, "ada_w", tot, "late_gather_wait_ada_w")
    g_ada_w, g_ada_b, g_c_ctx = _ada_bwd(c_all, dmx_all_j, dmc_j, dmx_all, dmc_tot, c_ctx2, ada_full)

    big = {
        "ada_w": _adam_big(ada_w[0], g_ada_w, m_ada_w[0], v_ada_w[0], "adam_ada_w"),
        "w_in": _adam_big(w_in[0], g_w_in, m_w_in[0], v_w_in[0], "adam_w_in"),
        "w_out": _adam_big(w_out[0], g_w_out, m_w_out[0], v_w_out[0], "adam_w_out"),
    }
    small_in = {
        "c_ctx": (c_ctx, g_c_ctx, m_c_ctx, v_c_ctx, (1, D)),
        "ada_b": (ada_b, g_ada_b, m_ada_b, v_ada_b, (1, 3 * D)),
        "norm_g": (norm_g, g_norm_g, m_norm_g, v_norm_g, (1, D)),
        "conv_w": (conv_w, g_conv_w, m_conv_w, v_conv_w, (4, 256)),
        "conv_b": (conv_b, g_conv_b, m_conv_b, v_conv_b, (1, D)),
        "lru_wa": (lru_wa, g_lru_wa, m_lru_wa, v_lru_wa, (2 * NH * HD, HD)),
        "lru_ba": (lru_ba, g_lru_ba, m_lru_ba, v_lru_ba, (2 * NH, HD)),
        "lru_wx": (lru_wx, g_lru_wx, m_lru_wx, v_lru_wx, (2 * NH * HD, HD)),
        "lru_bx": (lru_bx, g_lru_bx, m_lru_bx, v_lru_bx, (2 * NH, HD)),
        "lru_lambda": (lru_lambda, g_lam, m_lru_lambda, v_lru_lambda, (2, 256)),
        "sgu_ln_g": (sgu_ln_g, g_ln_g, m_sgu_ln_g, v_sgu_ln_g, (1, D)),
        "sgu_ln_b": (sgu_ln_b, g_ln_b, m_sgu_ln_b, v_sgu_ln_b, (1, D)),
        "sgu_w": (sgu_w, g_sgu_w, m_sgu_w, v_sgu_w, (NH * HD, HD)),
        "sgu_b": (sgu_b, g_sgu_b, m_sgu_b, v_sgu_b, (NH, HD)),
        "final_g": (final_g, g_final_g, m_final_g, v_final_g, (1, D)),
    }
    names_small = list(small_in)
    res_small = _adam_small([tuple(a.reshape(small_in[k][4]) for a in small_in[k][:4]) for k in names_small])
    full_shapes = {"ada_w": ada_w.shape, "w_in": w_in.shape, "w_out": w_out.shape}
    grads, deltas, new_m, new_v = {}, {}, {}, {}
    for k in ("ada_w", "w_in", "w_out"):
        g = {"ada_w": g_ada_w, "w_in": g_w_in, "w_out": g_w_out}[k]
        grads[k] = g.reshape(full_shapes[k])
        deltas[k], new_m[k], new_v[k] = (a.reshape(full_shapes[k]) for a in big[k])
    for k, res in zip(names_small, res_small):
        shape = small_in[k][0].shape
        grads[k] = small_in[k][1].reshape(shape)
        deltas[k], new_m[k], new_v[k] = (a.reshape(shape) for a in res)

    loss = jnp.sum(slots_all[:, 3, 0:3])
    order = ["c_ctx", "ada_w", "ada_b", "norm_g", "w_in", "conv_w", "conv_b", "lru_wa", "lru_ba", "lru_wx", "lru_bx",
             "lru_lambda", "sgu_ln_g", "sgu_ln_b", "sgu_w", "sgu_b", "w_out", "final_g"]
    return (loss, grad_x.reshape(x.shape), *[grads[k] for k in order], *[deltas[k] for k in order],
            *[new_m[k] for k in order], *[new_v[k] for k in order])
```

```python
import functools

import jax
import jax.numpy as jnp
from jax import lax
from jax.experimental import pallas as pl
from jax.experimental.pallas import tpu as pltpu

F32 = jnp.float32
BF16 = jnp.bfloat16

D = 1024
NH = 8
HD = 128
NCHIP = 4
T = 256
NORM_EPS = 1e-6
LN_EPS = 1e-5
LRU_C = 8.0
ADAM_LR = 0.001
ADAM_B1 = 0.9
ADAM_B2 = 0.999
ADAM_EPS = 1e-08
ADAM_WD = 0.01
ADAM_STEP = 10

VMEM = pl.BlockSpec(memory_space=pltpu.VMEM)
ANY = pl.BlockSpec(memory_space=pl.ANY)
MESH = pl.DeviceIdType.MESH


def _cp(n_grid=0, vmem_mb=None):
    kw = {}
    if n_grid:
        kw["dimension_semantics"] = ("arbitrary",) * n_grid
    if vmem_mb:
        kw["vmem_limit_bytes"] = vmem_mb << 20
    return pltpu.CompilerParams(**kw)


def _sigmoid(x):
    return 1.0 / (1.0 + jnp.exp(-x))


def _silu_and_grad(x):
    s = _sigmoid(x)
    return x * s, s * (1.0 + x * (1.0 - s))


_GELU_K = 0.7978845608028654
_GELU_C = 0.044715


def _gelu_and_grad(x):
    x2 = x * x
    th = jnp.tanh(_GELU_K * (x + _GELU_C * x * x2))
    g = 0.5 * x * (1.0 + th)
    dg = 0.5 * (1.0 + th) + 0.5 * x * (1.0 - th * th) * (_GELU_K * (1.0 + 3.0 * _GELU_C * x2))
    return g, dg


def _softplus(x):
    return jnp.maximum(x, 0.0) + jnp.log1p(jnp.exp(-jnp.abs(x)))


def _lru_gate(pre, lam_row, d, off=None):
    off = 256 * d if off is None else off
    r = _sigmoid(pre[:, off:off + HD])
    gi = _sigmoid(pre[:, off + HD:off + 2 * HD])
    lam = lam_row[:, HD * d:HD * d + HD]
    sp = _softplus(-lam)
    la = (-LRU_C) * r * sp
    a = jnp.exp(la)
    x2 = 2.0 * la
    m2 = jnp.where(x2 > -1e-3, -x2 * (1.0 + 0.5 * x2), 1.0 - a * a)
    mult = jnp.sqrt(m2)
    return r, gi, lam, sp, a, mult


def _dot(a, b):
    return jnp.dot(a, b, preferred_element_type=F32)


def _dot_tn(a, b):
    return lax.dot_general(a, b, (((0,), (0,)), ((), ())), preferred_element_type=F32)


def _dot_nt(a, b):
    return lax.dot_general(a, b, (((1,), (1,)), ((), ())), preferred_element_type=F32)


def _mo(v, m):
    return v if isinstance(v, int) else pl.multiple_of(v, m)


def _zrows(h, n):
    return pl.ds(h, n, stride=NH)


def _gather_in(c, c_ctx, ada_w, ada_b_j, w_in, w_out, smalls):
    nch = [1, 4]
    wrows = lambda cc, q: (pl.ds(_mo(512 * cc, 16), 512) if q is None
                           else pl.ds(_mo(512 * cc + (512 // nch[1]) * q, 16), 512 // nch[1]))
    specs = [
        ((64, 256), F32, lambda r, jj, cc, q=None: r.at[pl.ds(_mo(16 * jj + 8 * cc, 8), 8), :]),
        ((D, 5120), BF16, lambda r, jj, cc, q=None: r.at[wrows(cc, q), pl.ds(_mo(1280 * jj, 128), 1280)]),
    ]
    halves = [lambda r, cc, q=None: r.at[pl.ds(_mo(8 * cc, 8), 8), :],
              lambda r, cc, q=None: r.at[wrows(cc, q), :]]
    na = len(specs)
    sem_base = [0, 6 * nch[0]]
    sidx = lambda a, q, k: sem_base[a] + 6 * q + k
    n_tiny = 6 * sum(nch)
    n_sem = n_tiny + 10

    def body(c_ref, cc_ref, ada_ref, adab_ref, win_ref, wout_ref, sm_ref,
             mods_o, call_o, sm_o, win_o, wol_o, adal_o, s_win, s_ada, s_wout, f_win, f_ada, f_wout, cslot, lhs, mbuf,
             send_sems, recv_sems, local_sems, load_sems):
        x, y, c = lax.axis_index("x"), lax.axis_index("y"), lax.axis_index("c")
        j = 2 * x + y
        dev = 2 * j + c
        sib = (x, y, 1 - c)
        chips = [(1 - x, y), (x, 1 - y), (1 - x, 1 - y)]
        cj = [2 * cx + cy for cx, cy in chips]
        outs = [sm_o, win_o]
        srcs = [sm_ref, s_win]

        def copy(idx, src, dst, to):
            return pltpu.make_async_remote_copy(src_ref=src, dst_ref=dst, send_sem=send_sems.at[idx],
                                                recv_sem=recv_sems.at[idx], device_id=to, device_id_type=MESH)

        sends = []

        def start(cp):
            cp.start()
            sends.append(cp)

        cslot[...] = jnp.zeros_like(cslot)
        cslot[0:1, :] = c_ref[...]
        my_slot = pl.ds(_mo(8 * dev, 8), 8)
        others = [sib] + [(*chips[k], c) for k in range(3)] + [(*chips[k], 1 - c) for k in range(3)]
        other_dev = [dev + 1 - 2 * c] + [2 * cj[k] + c for k in range(3)] + [2 * cj[k] + 1 - c for k in range(3)]
        base = n_tiny
        for r in range(7):
            start(copy(base + r, cslot, call_o.at[my_slot, :], others[r]))
        call_o[my_slot, :] = cslot[...]

        crow = 512 // nch[1]
        loads = []
        for cc in (c, 1 - c):
            for q in range(nch[1]):
                rows = pl.ds(_mo(512 * cc + crow * q, 16), crow)
                loads.append(pltpu.make_async_copy(win_ref.at[rows, :], f_win.at[rows, :], load_sems.at[len(loads)]))
        loads.append(pltpu.make_async_copy(ada_ref, f_ada, load_sems.at[len(loads)]))
        loads.append(pltpu.make_async_copy(wout_ref, f_wout, load_sems.at[len(loads)]))
        for ld in loads:
            ld.start()
        for k in range(2):
            start(copy(sidx(0, 0, k), halves[0](srcs[0], c), specs[0][2](outs[0], j, c), (*chips[k], c)))
        for q in range(nch[1]):
            loads[q].wait()
            rows = pl.ds(_mo(512 * c + crow * q, 16), crow)
            s_win[rows, :] = f_win[rows, :].astype(BF16)
            for k in range(2):
                start(copy(sidx(1, q, k), halves[1](s_win, c, q), specs[1][2](win_o, j, c, q), (*chips[k], c)))
        for q in range(nch[1]):
            loads[nch[1] + q].wait()
            rows = pl.ds(_mo(512 * (1 - c) + crow * q, 16), crow)
            s_win[rows, :] = f_win[rows, :].astype(BF16)
        local = []
        for a in range(na):
            for cc in range(2):
                lc = pltpu.make_async_copy(halves[a](srcs[a], cc), specs[a][2](outs[a], j, cc), local_sems.at[2 * a + cc])
                lc.start()
                local.append(lc)
        loads[2 * nch[1]].wait()
        s_ada[...] = f_ada[...].astype(BF16)
        loads[2 * nch[1] + 1].wait()
        s_wout[...] = f_wout[...].astype(BF16)
        for q, (src, dst) in enumerate([(s_wout, wol_o.at[pl.ds(_mo(512 * j, 16), 512), :]),
                                        (s_ada, adal_o.at[:, pl.ds(_mo(768 * j, 128), 768)])]):
            lc = pltpu.make_async_copy(src, dst, local_sems.at[2 * na + q])
            lc.start()
            local.append(lc)

        for r in range(7):
            slot = call_o.at[pl.ds(_mo(8 * other_dev[r], 8), 8), :]
            copy(base + r, slot, slot, sib).wait_recv()
        lhs[...] = jnp.zeros_like(lhs)
        for b in range(8):
            cv = call_o[8 * b:8 * b + 1, :]
            lhs[b:b + 1, :] = cv * _sigmoid(cv)
        cv = cc_ref[...]
        lhs[8:9, :] = cv * _sigmoid(cv)
        mbuf[j] = _dot(lhs[...].astype(BF16), s_ada[...]) + adab_ref[...]
        for k in range(3):
            start(copy(base + 7 + k, mbuf.at[j], mbuf.at[j], (*chips[k], c)))
        for k in range(3):
            copy(base + 7 + k, mbuf.at[cj[k]], mbuf.at[cj[k]], sib).wait_recv()
        mods_o[...] = jnp.zeros_like(mods_o)
        for jj in range(NCHIP):
            mods_o[0:1, 768 * jj:768 * jj + 768] = mbuf[jj, pl.ds(dev, 1), :]
            mods_o[1:2, 768 * jj:768 * jj + 768] = mbuf[jj, 8:9, :]

        kx = [1 - x, x, 1 - x]
        ky = [y, 1 - y, 1 - y]
        pick = lambda k, lst: jnp.where(k == 0, lst[0], jnp.where(k == 1, lst[1], lst[2]))
        for a in range(na):
            for q in range(nch[a]):
                for step, k in enumerate([c, 1 - c]):
                    reg = specs[a][2](outs[a], pick(k, cj), c, q)
                    copy(sidx(a, q, k), reg, reg, sib).wait_recv()
                    if step == 0:
                        start(copy(sidx(a, q, 2), reg, reg, (pick(1 - c, kx), pick(1 - c, ky), c)))
                    start(copy(sidx(a, q, 3 + k), reg, reg, sib))
        for a in range(na):
            for q in range(nch[a]):
                reg = specs[a][2](outs[a], cj[2], c, q)
                copy(sidx(a, q, 2), reg, reg, sib).wait_recv()
                start(copy(sidx(a, q, 5), reg, reg, sib))
        for a in range(na):
            for q in range(nch[a]):
                for k in range(3):
                    reg = specs[a][2](outs[a], cj[k], 1 - c, q)
                    copy(sidx(a, q, 3 + k), reg, reg, sib).wait_recv()
        for cp in sends:
            cp.wait_send()
        for lc in local:
            lc.wait()

    out_shape = (jax.ShapeDtypeStruct((8, 3 * D), F32), jax.ShapeDtypeStruct((64, D), F32),
                 jax.ShapeDtypeStruct(specs[0][0], F32), jax.ShapeDtypeStruct(specs[1][0], BF16),
                 jax.ShapeDtypeStruct((2048, D), BF16), jax.ShapeDtypeStruct((D, 3 * D), BF16))
    return pl.pallas_call(
        body, name="gather_in", out_shape=out_shape,
        in_specs=[VMEM, VMEM, ANY, VMEM, ANY, ANY, VMEM], out_specs=(VMEM, VMEM, VMEM, ANY, ANY, ANY),
        scratch_shapes=[pltpu.VMEM((D, 1280), BF16), pltpu.VMEM((D, 768), BF16), pltpu.VMEM((512, D), BF16),
                        pltpu.VMEM((D, 1280), F32), pltpu.VMEM((D, 768), F32), pltpu.VMEM((512, D), F32),
                        pltpu.VMEM((8, D), F32), pltpu.VMEM((16, D), F32), pltpu.VMEM((NCHIP, 16, 768), F32),
                        pltpu.SemaphoreType.DMA((n_sem,)), pltpu.SemaphoreType.DMA((n_sem,)),
                        pltpu.SemaphoreType.DMA((2 * na + 2,)), pltpu.SemaphoreType.DMA((2 * nch[1] + 2,))],
        compiler_params=_cp(vmem_mb=56),
    )(c, c_ctx, ada_w, ada_b_j, w_in, w_out, smalls)


HBM = pl.BlockSpec(memory_space=pltpu.HBM)
SEM = pl.BlockSpec(memory_space=pltpu.SEMAPHORE)


def _late_gather_regions(x, y, c):
    chips = [(1 - x, y), (x, 1 - y), (1 - x, 1 - y)]
    wo_reg = lambda r, jj, cc: r.at[pl.ds(_mo(512 * jj + 256 * cc, 16), 256), :]
    ada_reg = lambda r, jj, cc: r.at[pl.ds(_mo(512 * cc, 16), 512), pl.ds(_mo(768 * jj, 128), 768)]
    return chips, wo_reg, ada_reg


def _late_gather_start(wo_land, ada_land):
    def body(wol_ref, adal_ref, wo_ss, wo_rs, ada_ss, ada_rs, wol_thru, adal_thru, token):
        x, y, c = lax.axis_index("x"), lax.axis_index("y"), lax.axis_index("c")
        j = 2 * x + y
        chips, wo_reg, ada_reg = _late_gather_regions(x, y, c)
        for k in range(3):
            for cc in range(2):
                pltpu.make_async_remote_copy(src_ref=wo_reg(wol_ref, j, c), dst_ref=wo_reg(wol_ref, j, c),
                                             send_sem=wo_ss.at[2 * k + cc], recv_sem=wo_rs.at[2 * k + c],
                                             device_id=(*chips[k], cc), device_id_type=MESH).start()
        for k in range(3):
            for cc in range(2):
                pltpu.make_async_remote_copy(src_ref=ada_reg(adal_ref, j, c), dst_ref=ada_reg(adal_ref, j, c),
                                             send_sem=ada_ss.at[2 * k + cc], recv_sem=ada_rs.at[2 * k + c],
                                             device_id=(*chips[k], cc), device_id_type=MESH).start()
        token[...] = jnp.zeros_like(token)

    sems = pltpu.SemaphoreType.DMA((6,))
    return pl.pallas_call(
        body, name="late_gather_start",
        out_shape=(sems, sems, sems, sems, pltpu.HBM(wo_land.shape, BF16), pltpu.HBM(ada_land.shape, BF16),
                   jax.ShapeDtypeStruct((8, 128), F32)),
        in_specs=(HBM, HBM), out_specs=(SEM, SEM, SEM, SEM, HBM, HBM, VMEM), input_output_aliases={0: 4, 1: 5},
        compiler_params=pltpu.CompilerParams(has_side_effects=pltpu.SideEffectType.DATAFLOW_SIDE_EFFECTING),
    )(pltpu.with_memory_space_constraint(wo_land, pltpu.HBM), pltpu.with_memory_space_constraint(ada_land, pltpu.HBM))


def _late_gather_wait(land, send_sems, recv_sems, which, after, name):
    def body(land_ref, ss, rs, after_ref, land_out):
        x, y, c = lax.axis_index("x"), lax.axis_index("y"), lax.axis_index("c")
        j = 2 * x + y
        chips, wo_reg, ada_reg = _late_gather_regions(x, y, c)
        reg = wo_reg if which == "w_out" else ada_reg
        for k in range(3):
            kj = 2 * chips[k][0] + chips[k][1]
            for cc in range(2):
                cp = pltpu.make_async_remote_copy(src_ref=reg(land_ref, j, c), dst_ref=reg(land_ref, kj, cc),
                                                  send_sem=ss.at[2 * k + cc], recv_sem=rs.at[2 * k + cc],
                                                  device_id=(*chips[k], cc), device_id_type=MESH)
                cp.wait_send()
                cp.wait_recv()

    return pl.pallas_call(
        body, name=name, out_shape=pltpu.HBM(land.shape, land.dtype),
        in_specs=(HBM, SEM, SEM, ANY), out_specs=HBM, input_output_aliases={0: 0},
        compiler_params=pltpu.CompilerParams(has_side_effects=pltpu.SideEffectType.DATAFLOW_SIDE_EFFECTING),
    )(land, send_sems, recv_sems, after)


RCHUNK = 16


def _grads_reduce(hn, dzs, hn_c, dxa_c, g_w_out, pack):
    rp = pack.shape[0]
    hp = rp // 2
    assert hp % RCHUNK == 0
    wi_w = 1280
    lx, lc = hn.shape[0], hn_c.shape[0]
    lt = lx + lc
    n_dz = len(dzs)

    def body(*refs):
        hn_hbm, dz_hbm = refs[0], refs[1:1 + n_dz]
        hnc_hbm, dxac_hbm, wo_hbm, pk_hbm, wi_out, wo_out, pk_out = refs[1 + n_dz:8 + n_dz]
        (hn_mine, hn_other, dzbuf, wi_other, wi_mine, wi_recv, wi_send, wi_rb,
         wo_mine, wo_recv, wo_send, wo_rb, wo_own, pk_mine, pk_recv, pk_send, pk_rb, pk_own,
         send_sems, recv_sems, local_sems) = refs[8 + n_dz:]
        x, y, c = lax.axis_index("x"), lax.axis_index("y"), lax.axis_index("c")
        j = 2 * x + y
        sib = (x, y, 1 - c)
        chips = [(1 - x, y), (x, 1 - y), (1 - x, 1 - y)]
        cj = [2 * cx + cy for cx, cy in chips]
        near = (jnp.where(c == 0, 1 - x, x), jnp.where(c == 0, y, 1 - y), c)
        slabs = [cj[2], cj[0], cj[1], j]

        def copy(k, src, dst, to):
            return pltpu.make_async_remote_copy(src_ref=src, dst_ref=dst, send_sem=send_sems.at[k],
                                                recv_sem=recv_sems.at[k], device_id=to, device_id_type=MESH)

        def local(k, src, dst):
            cp = pltpu.make_async_copy(src, dst, local_sems.at[k])
            cp.start()
            return cp

        rows_half = lambda r, cc, n: r.at[pl.ds(_mo(cc * n, 16), n), :]
        cols_half = lambda r, cc, n: r.at[:, pl.ds(_mo(cc * n, 128), n)]
        pk_piece = lambda r, cc, jj: r.at[pl.ds(_mo(cc * hp, 16), hp), pl.ds(_mo(jj * 128, 128), 128)]

        sends = []

        def start(cp):
            cp.start()
            sends.append(cp)

        def dz_pieces(s):
            g0 = wi_w * s
            k0, off0 = g0 // D, g0 % D
            w0 = min(D - off0, wi_w)
            pieces = [(k0, off0, w0, 0)]
            if w0 < wi_w:
                pieces.append((k0 + 1, 0, wi_w - w0, w0))
            return pieces

        def dz_copies(s):
            cps = []
            for q, (k, off, w, dst) in enumerate(dz_pieces(s)):
                cps.append(pltpu.make_async_copy(dz_hbm[k].at[:, pl.ds(off, w)], dzbuf.at[pl.ds(0, lx), pl.ds(dst, w)],
                                                 local_sems.at[11 + q]))
            if s == 0:
                cps.append(pltpu.make_async_copy(dxac_hbm, dzbuf.at[pl.ds(lx, lc), pl.ds(0, D)], local_sems.at[13]))
            return cps

        def dz_load(sl):
            for s in range(NCHIP):
                @pl.when(sl == s)
                def _():
                    if s == 0:
                        dzbuf[pl.ds(lx, lc), pl.ds(D, wi_w - D)] = jnp.zeros((lc, wi_w - D), BF16)
                    else:
                        dzbuf[pl.ds(lx, lc), :] = jnp.zeros((lc, wi_w), BF16)
                    for cp in dz_copies(s):
                        cp.start()

        def dz_wait(sl):
            for s in range(NCHIP):
                @pl.when(sl == s)
                def _():
                    for cp in dz_copies(s):
                        cp.wait()

        l_pk = local(0, rows_half(pk_hbm, c, hp), pk_mine)
        start(copy(0, rows_half(pk_hbm, 1 - c, hp), pk_recv, sib))
        l_wo = local(1, cols_half(wo_hbm, c, 512), wo_mine)
        start(copy(1, cols_half(wo_hbm, 1 - c, 512), wo_recv, sib))
        hn_loads = [local(2, hn_hbm.at[:, pl.ds(_mo(c * 512, 128), 512)], hn_mine.at[pl.ds(0, lx), :]),
                    local(3, hnc_hbm.at[:, pl.ds(_mo(c * 512, 128), 512)], hn_mine.at[pl.ds(lx, lc), :]),
                    local(4, hn_hbm.at[:, pl.ds(_mo((1 - c) * 512, 128), 512)], hn_other.at[pl.ds(0, lx), :]),
                    local(5, hnc_hbm.at[:, pl.ds(_mo((1 - c) * 512, 128), 512)], hn_other.at[pl.ds(lx, lc), :])]
        dz_load(slabs[0])

        def pair_sum(mine, recv, send, nrows, keep, relayed=None):
            def step(i, carry):
                rows = pl.ds(_mo(i * RCHUNK, RCHUNK), RCHUNK)
                s = mine[rows, :] + recv[rows, :].astype(F32)
                if relayed is not None:
                    s = s + relayed[rows, :].astype(F32)
                if keep:
                    mine[rows, :] = s
                if send is not None:
                    send[rows, :] = s.astype(BF16)
                return carry
            lax.fori_loop(0, nrows // RCHUNK, step, 0)

        def chip_sum(own, rb, nrows, terms=(0, 1, 2)):
            def step(i, carry):
                rows = pl.ds(_mo(i * RCHUNK, RCHUNK), RCHUNK)
                acc = own[rows, :]
                for q in terms:
                    acc = acc + rb[q, rows, :].astype(F32)
                own[rows, :] = acc
                return carry
            lax.fori_loop(0, nrows // RCHUNK, step, 0)

        p1 = [None] * NCHIP

        def slab_matmuls(s):
            if s >= 2:
                p1[s - 2].wait_send()
            dz_wait(slabs[s])
            wi_other[s % 2] = _dot_tn(hn_other[...], dzbuf[...]).astype(BF16)
            p1[s] = copy(2 + s, wi_other.at[s % 2], wi_recv.at[s], sib)
            p1[s].start()
            wi_mine[s % 2] = _dot_tn(hn_mine[...], dzbuf[...])
            if s + 1 < NCHIP:
                dz_load(slabs[s + 1])

        def slab_finish(s):
            copy(2 + s, wi_recv.at[s], wi_recv.at[s], sib).wait_recv()
            if s == 3:
                pair_sum(wi_mine.at[s % 2], wi_recv.at[s], None, 512, True)
                return
            if s == 0:
                pair_sum(wi_mine.at[0], wi_recv.at[0], wi_send.at[0], 512, False)
                start(copy(12, wi_send.at[0], wi_rb.at[0], near))
                return
            adds_relayed = c == (1 if s == 1 else 0)

            @pl.when(adds_relayed)
            def _():
                copy(12, wi_rb.at[0], wi_rb.at[0], sib).wait_recv()
                pair_sum(wi_mine.at[s % 2], wi_recv.at[s], wi_send.at[s], 512, False, wi_rb.at[0])

            @pl.when(jnp.logical_not(adds_relayed))
            def _():
                pair_sum(wi_mine.at[s % 2], wi_recv.at[s], wi_send.at[s], 512, False)
            start(copy(12 + s, wi_send.at[s], wi_rb.at[s], (*chips[s - 1], c)))

        l_wo.wait()
        copy(1, wo_recv, wo_recv, sib).wait_recv()
        pair_sum(wo_mine, wo_recv, wo_send, 2048, True)
        for k in range(3):
            start(copy(9 + k, wo_send.at[pl.ds(_mo(cj[k] * 512, 16), 512), :], wo_rb.at[k], (*chips[k], c)))
        l_wo_own = local(7, wo_mine.at[pl.ds(_mo(j * 512, 16), 512), :], wo_own)

        for cp in hn_loads:
            cp.wait()
        slab_matmuls(0)

        l_pk.wait()
        copy(0, pk_recv, pk_recv, sib).wait_recv()
        pair_sum(pk_mine, pk_recv, pk_send, hp, True)
        for k in range(3):
            start(copy(6 + k, pk_send.at[:, pl.ds(_mo(cj[k] * 128, 128), 128)], pk_rb.at[k], (*chips[k], c)))
        l_pk_own = local(6, pk_mine.at[:, pl.ds(_mo(j * 128, 128), 128)], pk_own)

        slab_matmuls(1)
        slab_finish(0)

        l_pk_own.wait()
        for k in range(3):
            copy(6 + k, pk_rb.at[k], pk_rb.at[k], sib).wait_recv()
        chip_sum(pk_own, pk_rb, hp)
        l_pk_out = local(8, pk_own, pk_piece(pk_out, c, j))
        start(copy(15, pk_own, pk_piece(pk_out, c, j), sib))
        for k in range(3):
            start(copy(16 + k, pk_own, pk_piece(pk_out, c, j), (*chips[k], c)))

        slab_matmuls(2)
        slab_finish(1)
        slab_matmuls(3)
        slab_finish(2)

        l_wo_own.wait()
        for k in range(3):
            copy(9 + k, wo_rb.at[k], wo_rb.at[k], sib).wait_recv()
        chip_sum(wo_own, wo_rb, 512)
        l_wo_out = local(9, wo_own, cols_half(wo_out, c, 512))
        start(copy(22, wo_own, cols_half(wo_out, c, 512), sib))

        for k in range(3):
            reg = pk_piece(pk_out, c, cj[k])
            copy(16 + k, reg, reg, sib).wait_recv()
            start(copy(19 + k, reg, reg, sib))

        slab_finish(3)
        for k in (1, 2):
            copy(12 + k, wi_rb.at[k], wi_rb.at[k], sib).wait_recv()
        chip_sum(wi_mine.at[1], wi_rb, 512, (1, 2))
        l_wi_out = local(10, wi_mine.at[1], rows_half(wi_out, c, 512))
        start(copy(23, wi_mine.at[1], rows_half(wi_out, c, 512), sib))

        reg = pk_piece(pk_out, 1 - c, j)
        copy(15, reg, reg, sib).wait_recv()
        for k in range(3):
            reg = pk_piece(pk_out, 1 - c, cj[k])
            copy(19 + k, reg, reg, sib).wait_recv()
        reg = cols_half(wo_out, 1 - c, 512)
        copy(22, reg, reg, sib).wait_recv()
        reg = rows_half(wi_out, 1 - c, 512)
        copy(23, reg, reg, sib).wait_recv()
        for cp in sends + p1[2:]:
            cp.wait_send()
        for cp in (l_pk_out, l_wo_out, l_wi_out):
            cp.wait()

    return pl.pallas_call(
        body, name="grads_reduce",
        out_shape=(jax.ShapeDtypeStruct((D, wi_w), F32), jax.ShapeDtypeStruct((512, D), F32),
                   jax.ShapeDtypeStruct(pack.shape, F32)),
        in_specs=[ANY] * (5 + n_dz), out_specs=(ANY,) * 3,
        scratch_shapes=[
            pltpu.VMEM((lt, 512), BF16), pltpu.VMEM((lt, 512), BF16), pltpu.VMEM((lt, wi_w), BF16),
            pltpu.VMEM((2, 512, wi_w), BF16), pltpu.VMEM((2, 512, wi_w), F32), pltpu.VMEM((4, 512, wi_w), BF16),
            pltpu.VMEM((3, 512, wi_w), BF16), pltpu.VMEM((3, 512, wi_w), BF16),
            pltpu.VMEM((2048, 512), F32), pltpu.VMEM((2048, 512), F32), pltpu.VMEM((2048, 512), BF16),
            pltpu.VMEM((3, 512, 512), BF16), pltpu.VMEM((512, 512), F32),
            pltpu.VMEM((hp, 512), F32), pltpu.VMEM((hp, 512), F32), pltpu.VMEM((hp, 512), BF16),
            pltpu.VMEM((3, hp, 128), BF16), pltpu.VMEM((hp, 128), F32),
            pltpu.SemaphoreType.DMA((24,)), pltpu.SemaphoreType.DMA((24,)), pltpu.SemaphoreType.DMA((14,))],
        compiler_params=_cp(vmem_mb=56),
    )(hn, *dzs, hn_c, dxa_c, g_w_out, pack)


def _ada_bwd(c_all, dmx_all_j, dmc_j, dmx_all, dmc, c_ctx, ada_w_full):
    def body(c_ref, dmxj_ref, dmcj_ref, dmx_ref, dmc_ref, cc_ref, w_ref, gw_ref, gb_ref, gc_ref, lhs, rhs, dm8):
        lhs[...] = jnp.zeros_like(lhs)
        rhs[...] = jnp.zeros_like(rhs)
        cv = c_ref[...]
        lhs[0:8, :] = cv * _sigmoid(cv)
        cc = cc_ref[...]
        a_c, da_c = _silu_and_grad(cc)
        lhs[8:9, :] = a_c
        rhs[0:8, :] = dmxj_ref[...]
        rhs[8:9, :] = dmcj_ref[...]
        gw_ref[...] = _dot_tn(lhs[...].astype(BF16), rhs[...].astype(BF16))
        gb_ref[...] = jnp.sum(dmx_ref[...], axis=0, keepdims=True) + dmc_ref[...]
        dm8[...] = jnp.zeros_like(dm8)
        dm8[0:1, :] = dmc_ref[...]
        da = _dot_nt(dm8[...].astype(BF16), w_ref[...])
        gc_ref[...] = da[0:1, :] * da_c

    return pl.pallas_call(
        body, name="ada_bwd",
        out_shape=(jax.ShapeDtypeStruct((D, 768), F32), jax.ShapeDtypeStruct((1, 3 * D), F32),
                   jax.ShapeDtypeStruct((1, D), F32)),
        in_specs=[VMEM] * 7, out_specs=(VMEM,) * 3,
        scratch_shapes=[pltpu.VMEM((16, D), F32), pltpu.VMEM((16, 768), F32), pltpu.VMEM((8, 3 * D), F32)],
        compiler_params=_cp(vmem_mb=32),
    )(c_all, dmx_all_j, dmc_j, dmx_all, dmc, c_ctx, ada_w_full)


def _proj(x, mods, mrow, norm_g, w_full, nk, name, sgu=None):
    lx = x.shape[0]
    n = lx // T
    order = [2, 3, 0, 1, 4] if sgu is not None else list(range(nk))

    def body(x_ref, sh_ref, sc_ref, ng_ref, *rest):
        w_refs, rest = rest[:nk], rest[nk:]
        if sgu is not None:
            g_ref, b_ref, sw_ref, bt_ref = rest[:4]
            rest = rest[4:]
        hn_ref, z_refs = rest[0], rest[1:1 + nk]
        xv = x_ref[...]
        r = lax.rsqrt(jnp.mean(xv * xv, axis=-1, keepdims=True) + NORM_EPS)
        hn = (xv * r) * ng_ref[...] * (1.0 + sc_ref[mrow:mrow + 1, :]) + sh_ref[mrow:mrow + 1, :]
        hb = hn.astype(BF16)
        hn_ref[...] = hb
        for k in order[:2]:
            z_refs[k][...] = _dot(hb, w_refs[k][...])
        if sgu is not None:
            ys_ref, mixed_s = rest[1 + nk], rest[2 + nk]
            for ch in range(T // HD):
                rows = slice(HD * ch, HD * ch + HD)
                ug = _sgu_parts(z_refs[2][rows, :], z_refs[3][rows, :], g_ref[...], b_ref[...], sw_ref, bt_ref,
                                mixed_s)[0]
                ys_ref[rows, :] = ug * mixed_s[...]
        for k in order[2:]:
            z_refs[k][...] = _dot(hb, w_refs[k][...])

    row = pl.BlockSpec((T, D), lambda i: (i, 0))
    vec = pl.BlockSpec((1, D), lambda i: (0, 0))
    in_specs = [row, pl.BlockSpec((8, D), lambda i: (0, 0)), pl.BlockSpec((8, D), lambda i: (0, 1)), vec]
    in_specs += [pl.BlockSpec((D, D), lambda i, k=k: (0, k)) for k in range(nk)]
    args = [x, mods, mods, norm_g] + [w_full] * nk
    out_shape = (jax.ShapeDtypeStruct((lx, D), BF16),) + tuple(jax.ShapeDtypeStruct((lx, D), F32) for _ in range(nk))
    scratch = []
    if sgu is not None:
        in_specs += [vec, vec, pl.BlockSpec((NH, HD, HD), lambda i: (0, 0, 0)), pl.BlockSpec((HD, NH), lambda i: (0, 0))]
        args += list(sgu)
        out_shape += (jax.ShapeDtypeStruct((lx, D), F32),)
        scratch = [pltpu.VMEM((HD, D), F32)]
    return pl.pallas_call(
        body, name=name, grid=(n,), out_shape=out_shape, in_specs=in_specs, out_specs=(row,) * len(out_shape),
        scratch_shapes=scratch, compiler_params=_cp(1, vmem_mb=56),
    )(*args)


def _halo_specs(lx):
    last = lx // 8 - 1
    return [pl.BlockSpec((T, D), lambda i: (i, 0)),
            pl.BlockSpec((8, D), lambda i: (jnp.maximum(i * (T // 8) - 1, 0), 0)),
            pl.BlockSpec((8, D), lambda i: (jnp.minimum((i + 1) * (T // 8), last), 0))]


def _zhalo_specs(lx):
    last = lx // 8 - 1
    return [pl.BlockSpec((T * NH, HD), lambda i: (i, 0)),
            pl.BlockSpec((8 * NH, HD), lambda i: (jnp.maximum(i * (T // 8) - 1, 0), 0)),
            pl.BlockSpec((8 * NH, HD), lambda i: (jnp.minimum((i + 1) * (T // 8), last), 0))]


ZT = pl.BlockSpec((T * NH, HD), lambda i: (i, 0))
CONV_CHUNK = 32


SCAN_SUB = 4


def _scan_tile(chains, post, carry_ref):
    blk = T // SCAN_SUB

    def step(k, state):
        new = []
        for ci, (a_ref, x_ref, o_ref, q_ref, reverse) in enumerate(chains):
            for q in range(SCAN_SUB):
                s, p = state[ci * SCAN_SUB + q]
                t = (q + 1) * blk - 1 - k if reverse else q * blk + k
                r = pl.ds(_mo(t * NH, NH), NH)
                a = a_ref[r, :]
                if post:
                    o = x_ref[r, :] + s
                    o_ref[r, :] = o
                    q_ref[r, :] = p
                    new.append((a * o, a * p))
                else:
                    o = a * s + x_ref[r, :]
                    p = a * p
                    o_ref[r, :] = o
                    q_ref[r, :] = p
                    new.append((o, p))
        return tuple(new)

    zero = jnp.zeros((NH, HD), F32)
    one = jnp.ones((NH, HD), F32)
    final = lax.fori_loop(0, blk, step, tuple((zero, one) for _ in range(len(chains) * SCAN_SUB)), unroll=2)
    for ci, (a_ref, x_ref, o_ref, q_ref, reverse) in enumerate(chains):
        carry = carry_ref[ci]
        for q in (range(SCAN_SUB - 1, -1, -1) if reverse else range(SCAN_SUB)):
            rows = pl.ds(q * blk * NH, blk * NH)
            fixed = o_ref[rows, :].reshape(blk, NH, HD) + q_ref[rows, :].reshape(blk, NH, HD) * carry[None]
            o_ref[rows, :] = fixed.reshape(blk * NH, HD)
            s_loc, p_loc = final[ci * SCAN_SUB + q]
            carry = s_loc + p_loc * carry
        carry_ref[ci] = carry


def _lru_fwd(xa, conv_wz, conv_bz, wcat, bcat, lamcat, s_f, s_b, name):
    lx = xa.shape[0]
    n = lx // T

    def body(xm_u, xp_u, xn_u, xm_d, xp_d, xn_d, cw, cb, w_ref, b_ref, lam_ref, su0, sd0,
             xaz_o, xcz_o, af_o, ab_o, hf_o, hb_o, gf_o, gb_o, fu, fd, pad, xc_d, x_u, x_d, q_u, q_d, carry):
        i = pl.program_id(0)

        @pl.when(i == 0)
        def _():
            carry[0] = su0[...]
            carry[1] = sd0[...]

        def conv_gates(xm, xp, xn, tile, d, xc_ref, a_ref, x_ref, xaz_ref, g_ref):
            pmask = jnp.where(tile == 0, 0.0, 1.0)
            nmask = jnp.where(tile == n - 1, 0.0, 1.0)
            for h in range(NH):
                cols = slice(HD * h, HD * h + HD)
                pad[_zrows(h, 8), :] = xp[:, cols] * pmask
                pad[pl.ds(8 * NH + h, T, stride=NH), :] = xm[:, cols]
                pad[pl.ds((T + 8) * NH + h, 8, stride=NH), :] = xn[:, cols] * nmask
            if xaz_ref is not None:
                xaz_ref[...] = pad[pl.ds(8 * NH, T * NH), :]

            def conv_chunk(ci, c_):
                base = pl.multiple_of(ci * (CONV_CHUNK * NH), CONV_CHUNK * NH)
                acc = None
                for k in range(4):
                    sl = pad[pl.ds(base + (7 + k) * NH, CONV_CHUNK * NH), :].reshape(CONV_CHUNK, NH, HD)
                    term = sl * cw[k][None]
                    acc = term if acc is None else acc + term
                acc = acc + cb[...][None]
                xc_ref[pl.ds(base, CONV_CHUNK * NH), :] = acc.reshape(CONV_CHUNK * NH, HD)
                return c_
            lax.fori_loop(0, T // CONV_CHUNK, conv_chunk, 0)

            for h in range(NH):
                xch = xc_ref[_zrows(h, T), :]
                pre = _dot(xch.astype(BF16), w_ref[h, :, 256 * d:256 * d + 256]) + b_ref[h:h + 1, 256 * d:256 * d + 256]
                r, gi, _, _, a, mult = _lru_gate(pre, lam_ref[h:h + 1, :], d, 0)
                a_ref[_zrows(h, T), :] = a
                x_ref[_zrows(h, T), :] = mult * gi * xch
                for q, val in enumerate((r, gi, mult)):
                    g_ref[:, q * D + HD * h:q * D + HD * h + HD] = val

        conv_gates(xm_u, xp_u, xn_u, i, 0, xcz_o, af_o, x_u, xaz_o, gf_o)
        conv_gates(xm_d, xp_d, xn_d, n - 1 - i, 1, xc_d, ab_o, x_d, None, gb_o)

        _scan_tile([(af_o, x_u, hf_o, q_u, False), (ab_o, x_d, hb_o, q_d, True)], False, carry)
        fu[...] = carry[0]
        fd[...] = carry[1]

    full = lambda shape: pl.BlockSpec(shape, lambda i: (0,) * len(shape))
    last = lx // 8 - 1
    rev = lambda i: n - 1 - i
    halo_dn = [pl.BlockSpec((T, D), lambda i: (rev(i), 0)),
               pl.BlockSpec((8, D), lambda i: (jnp.maximum(rev(i) * (T // 8) - 1, 0), 0)),
               pl.BlockSpec((8, D), lambda i: (jnp.minimum((rev(i) + 1) * (T // 8), last), 0))]
    st = full((NH, HD))
    in_specs = _halo_specs(lx) + halo_dn + [full((4, NH, HD)), st, full((NH, HD, 4 * HD)), full((NH, 4 * HD)),
                                            full((NH, 2 * HD)), st, st]
    dn = pl.BlockSpec((T * NH, HD), lambda i: (rev(i), 0))
    zs = jax.ShapeDtypeStruct((lx * NH, HD), F32)
    ss = jax.ShapeDtypeStruct((NH, HD), F32)
    zbuf = pltpu.VMEM((T * NH, HD), F32)
    gs = jax.ShapeDtypeStruct((lx, 3 * D), F32)
    g_up = pl.BlockSpec((T, 3 * D), lambda i: (i, 0))
    g_dn = pl.BlockSpec((T, 3 * D), lambda i: (rev(i), 0))
    return pl.pallas_call(
        body, name=name, grid=(n,), out_shape=(zs,) * 6 + (gs, gs, ss, ss), in_specs=in_specs,
        out_specs=(ZT, ZT, ZT, dn, ZT, dn, g_up, g_dn, st, st),
        scratch_shapes=[pltpu.VMEM(((T + 16) * NH, HD), F32), zbuf, zbuf, zbuf, zbuf, zbuf,
                        pltpu.VMEM((2, NH, HD), F32)],
        compiler_params=_cp(1, vmem_mb=48),
    )(xa, xa, xa, xa, xa, xa, conv_wz, conv_bz, wcat, bcat, lamcat, s_f, s_b)


def _scan_adjoint(a_up, x_up, a_dn, x_dn, name):
    lx = a_up.shape[0] // NH
    n = lx // T

    def body(au, xu, ad, xd, ou, od, fu, fd, q_u, q_d, carry):
        @pl.when(pl.program_id(0) == 0)
        def _():
            carry[...] = jnp.zeros_like(carry)

        _scan_tile([(au, xu, ou, q_u, False), (ad, xd, od, q_d, True)], True, carry)
        fu[...] = carry[0]
        fd[...] = carry[1]

    up = pl.BlockSpec((T * NH, HD), lambda i: (i, 0))
    dn = pl.BlockSpec((T * NH, HD), lambda i: (n - 1 - i, 0))
    st = pl.BlockSpec((NH, HD), lambda i: (0, 0))
    zs = jax.ShapeDtypeStruct((lx * NH, HD), F32)
    ss = jax.ShapeDtypeStruct((NH, HD), F32)
    zbuf = pltpu.VMEM((T * NH, HD), F32)
    return pl.pallas_call(
        body, name=name, grid=(n,), out_shape=(zs, zs, ss, ss), in_specs=[up, up, dn, dn],
        out_specs=(up, dn, st, st), scratch_shapes=[zbuf, zbuf, pltpu.VMEM((2, NH, HD), F32)],
        compiler_params=_cp(1, vmem_mb=48),
    )(a_up, x_up, a_dn, x_dn)


def _sgu_parts(u, v, lng, lnb, w_ref, bt_ref, mixed_s):
    ug, dug = _gelu_and_grad(u)
    vg, dvg = _gelu_and_grad(v)
    mu = jnp.mean(vg, axis=-1, keepdims=True)
    vc = vg - mu
    rstd = lax.rsqrt(jnp.mean(vc * vc, axis=-1, keepdims=True) + LN_EPS)
    vh = vc * rstd
    vn = (vh * lng + lnb).astype(BF16)
    for g in range(NH):
        cols = slice(HD * g, HD * g + HD)
        mixed_s[:, cols] = _dot(w_ref[g], vn[:, cols]) + bt_ref[:, g:g + 1]
    return ug, dug, dvg, rstd, vh, vn


def _sgu_bwd_chunk(u, v, dys_v, lng, lnb, w_ref, bt_ref, mixed_s, dvn_s, dw_ref, db_ref, dg_ref, dbl_ref):
    ug, dug, dvg, rstd, vh, vn = _sgu_parts(u, v, lng, lnb, w_ref, bt_ref, mixed_s)
    du = (dys_v * mixed_s[...] * dug).astype(BF16)
    dmix = dys_v * ug
    ones = jnp.ones((8, HD), BF16)
    for g in range(NH):
        cols = slice(HD * g, HD * g + HD)
        dm = dmix[:, cols]
        hi = dm.astype(BF16)
        lo = (dm - hi.astype(F32)).astype(BF16)
        dw_ref[g] += _dot_nt(hi, vn[:, cols])
        db_ref[g:g + 1, :] += (_dot_nt(ones, hi) + _dot_nt(ones, lo))[0:1, :]
        dvn_s[:, cols] = _dot_tn(w_ref[g], hi)
    dvn = dvn_s[...]
    dg_ref[...] += jnp.sum(dvn * vh, axis=0, keepdims=True)
    dbl_ref[...] += jnp.sum(dvn, axis=0, keepdims=True)
    dvh = dvn * lng
    dvg_in = rstd * (dvh - jnp.mean(dvh, axis=-1, keepdims=True) - vh * jnp.mean(dvh * vh, axis=-1, keepdims=True))
    return du, (dvg_in * dvg).astype(BF16)


def _out_fwd_bwd(hf_z, hb_z, ga, gb, ys, x, tgt, mods, final_g, w_out_full):
    lx = x.shape[0]
    n = lx // T

    def body(hf_ref, hb_ref, ga_ref, gb_ref, ys_ref, x_ref, t_ref, gx_ref, fg_ref, w_ref,
             loss_ref, dfg_ref, dgx_ref, dxn_ref, y_ref, do_ref, dga_ref, dgb_ref, dyl_ref, dys_ref, yl_s):
        i = pl.program_id(0)

        @pl.when(i == 0)
        def _():
            loss_ref[...] = jnp.zeros_like(loss_ref)
            dfg_ref[...] = jnp.zeros_like(dfg_ref)
            dgx_ref[...] = jnp.zeros_like(dgx_ref)

        for h in range(NH):
            yl_s[:, HD * h:HD * h + HD] = hf_ref[_zrows(h, T), :] + hb_ref[_zrows(h, T), :]
        yl = yl_s[...]
        gav = ga_ref[...]
        gbv = gb_ref[...]
        sa, dsa = _silu_and_grad(gav)
        sb, dsb = _silu_and_grad(gbv)
        ysv = ys_ref[...]
        y_ref[:, 0:D] = (yl * sa).astype(BF16)
        y_ref[:, D:2 * D] = (ysv * sb).astype(BF16)
        o = _dot(y_ref[...], w_ref[...])
        gx = gx_ref[0:1, :]
        xnew = x_ref[...] + gx * o
        r2 = lax.rsqrt(jnp.mean(xnew * xnew, axis=-1, keepdims=True) + NORM_EPS)
        xh = xnew * r2
        fg = fg_ref[...]
        err = xh * fg - t_ref[...]
        loss_ref[...] += 0.5 * jnp.sum(jnp.mean(err * err, axis=-1, keepdims=True), axis=0, keepdims=True)
        dout = err * (1.0 / D)
        dfg_ref[...] += jnp.sum(dout * xh, axis=0, keepdims=True)
        dxh = dout * fg
        dxn = r2 * (dxh - xh * jnp.mean(dxh * xh, axis=-1, keepdims=True))
        dxn_ref[...] = dxn
        dgx_ref[...] += jnp.sum(dxn * o, axis=0, keepdims=True)
        do = (dxn * gx).astype(BF16)
        do_ref[...] = do
        dy = _dot_nt(do, w_ref[...])
        dy1 = dy[:, 0:D]
        dy2 = dy[:, D:2 * D]
        dga_ref[...] = (dy1 * yl * dsa).astype(BF16)
        dgb_ref[...] = (dy2 * ysv * dsb).astype(BF16)
        dys_ref[...] = dy2 * sb
        yl_s[...] = dy1 * sa
        for h in range(NH):
            dyl_ref[_zrows(h, T), :] = yl_s[:, HD * h:HD * h + HD]

    row = pl.BlockSpec((T, D), lambda i: (i, 0))
    vec = pl.BlockSpec((1, D), lambda i: (0, 0))
    in_specs = [ZT, ZT, row, row, row, row, row, pl.BlockSpec((8, D), lambda i: (0, 2)), vec,
                pl.BlockSpec((2 * D, D), lambda i: (0, 0))]
    out_shape = (jax.ShapeDtypeStruct((1, 1), F32), jax.ShapeDtypeStruct((1, D), F32), jax.ShapeDtypeStruct((1, D), F32),
                 jax.ShapeDtypeStruct((lx, D), F32), jax.ShapeDtypeStruct((lx, 2 * D), BF16),
                 jax.ShapeDtypeStruct((lx, D), BF16), jax.ShapeDtypeStruct((lx, D), BF16),
                 jax.ShapeDtypeStruct((lx, D), BF16), jax.ShapeDtypeStruct((lx * NH, HD), F32),
                 jax.ShapeDtypeStruct((lx, D), F32))
    out_specs = (pl.BlockSpec((1, 1), lambda i: (0, 0)), vec, vec, row, pl.BlockSpec((T, 2 * D), lambda i: (i, 0)),
                 row, row, row, ZT, row)
    return pl.pallas_call(
        body, name="out_fwd_bwd", grid=(n,), out_shape=out_shape, in_specs=in_specs, out_specs=out_specs,
        scratch_shapes=[pltpu.VMEM((T, D), F32)],
        compiler_params=_cp(1, vmem_mb=56),
    )(hf_z, hb_z, ga, gb, ys, x, tgt, mods, final_g, w_out_full)


def _lru_gates_bwd(xc_z, lf_z, lb_z, hf_z, hb_z, af_z, ab_z, gf, gb, s_f, s_b, wcat, lamcat, dw0, db0, dl0, name):
    lx = xc_z.shape[0] // NH
    n = lx // T

    def body(xc_ref, lf_ref, lb_ref, hf_ref, hfp_ref, hb_ref, hbn_ref, af_ref, ab_ref, gf_ref, gb_ref, sf_ref, sb_ref,
             w_ref, lam_ref, dw0_ref, db0_ref, dl0_ref, dxc_ref, dw_ref, db_ref, dl_ref, dpre_s, pf_s, pb_s):
        i = pl.program_id(0)

        @pl.when(i == 0)
        def _():
            dw_ref[...] = dw0_ref[...]
            db_ref[...] = db0_ref[...]
            dl_ref[...] = dl0_ref[...]

        pf_s[pl.ds(0, NH), :] = jnp.where(i == 0, sf_ref[...], hfp_ref[pl.ds(7 * NH, NH), :])
        pf_s[pl.ds(NH, T * NH), :] = hf_ref[...]
        pb_s[pl.ds(0, T * NH), :] = hb_ref[...]
        pb_s[pl.ds(T * NH, NH), :] = jnp.where(i == n - 1, sb_ref[...], hbn_ref[pl.ds(0, NH), :])
        lam_refs = (lf_ref, lb_ref)
        prev = ((pf_s, 0), (pb_s, NH))
        a_refs = (af_ref, ab_ref)
        g_refs = (gf_ref, gb_ref)
        for h in range(NH):
            xch = xc_ref[_zrows(h, T), :]
            xcb = xch.astype(BF16)
            dxc = jnp.zeros((T, HD), F32)
            for d in range(2):
                r, gi, mult = (g_refs[d][:, q * D + HD * h:q * D + HD * h + HD] for q in range(3))
                a = a_refs[d][_zrows(h, T), :]
                lam = lam_ref[h:h + 1, HD * d:HD * d + HD]
                sp = _softplus(-lam)
                du = lam_refs[d][_zrows(h, T), :]
                da = du * prev[d][0][pl.ds(prev[d][1] + h, T, stride=NH), :]
                dgi = du * mult * xch
                dxc = dxc + du * mult * gi
                dmult = du * gi * xch
                dla = da * a - dmult * (a * a) / mult
                dr = dla * ((-LRU_C) * sp)
                dsp = jnp.sum(dla * ((-LRU_C) * r), axis=0, keepdims=True)
                dl_ref[h:h + 1, HD * d:HD * d + HD] += dsp * (-_sigmoid(-lam))
                dpre_s[:, 256 * d:256 * d + HD] = dr * r * (1.0 - r)
                dpre_s[:, 256 * d + HD:256 * d + 2 * HD] = dgi * gi * (1.0 - gi)
            dpre = dpre_s[...]
            dpb = dpre.astype(BF16)
            dw_ref[h] += _dot_tn(xcb, dpb)
            db_ref[h:h + 1, :] += jnp.sum(dpre, axis=0, keepdims=True)
            dxc_ref[_zrows(h, T), :] = dxc + _dot_nt(dpb, w_ref[h])

    full = lambda shape: pl.BlockSpec(shape, lambda i: (0,) * len(shape))
    wsp, bsp, lsp = full((NH, HD, 4 * HD)), full((NH, 4 * HD)), full((NH, 2 * HD))
    st = full((NH, HD))
    zh = _zhalo_specs(lx)
    gsp = pl.BlockSpec((T, 3 * D), lambda i: (i, 0))
    return pl.pallas_call(
        body, name=name, grid=(n,),
        out_shape=(jax.ShapeDtypeStruct((lx * NH, HD), F32), jax.ShapeDtypeStruct((NH, HD, 4 * HD), F32),
                   jax.ShapeDtypeStruct((NH, 4 * HD), F32), jax.ShapeDtypeStruct((NH, 2 * HD), F32)),
        in_specs=[ZT] * 3 + zh[0:2] + [zh[0], zh[2], ZT, ZT, gsp, gsp, st, st, wsp, lsp, wsp, bsp, lsp],
        out_specs=(ZT, wsp, bsp, lsp),
        scratch_shapes=[pltpu.VMEM((T, 4 * HD), F32), pltpu.VMEM(((T + 1) * NH, HD), F32),
                        pltpu.VMEM(((T + 1) * NH, HD), F32)],
        compiler_params=_cp(1, vmem_mb=56),
    )(xc_z, lf_z, lb_z, hf_z, hf_z, hb_z, hb_z, af_z, ab_z, gf, gb, s_f, s_b, wcat, lamcat, dw0, db0, dl0)


def _conv_bwd(dxc_z, xa_z, conv_wz, dcw0, dcb0, name):
    lx = dxc_z.shape[0] // NH
    n = lx // T

    def body(dm, dp, dn, xa_ref, cw, dcw0_ref, dcb0_ref, dxa_ref, dcw_ref, dcb_ref, pad, dxa_s):
        i = pl.program_id(0)

        @pl.when(i == 0)
        def _():
            dcw_ref[...] = dcw0_ref[...]
            dcb_ref[...] = dcb0_ref[...]

        pmask = jnp.where(i == 0, 0.0, 1.0)
        nmask = jnp.where(i == n - 1, 0.0, 1.0)
        pad[pl.ds(0, 8 * NH), :] = dp[...] * pmask
        pad[pl.ds(8 * NH, T * NH), :] = dm[...]
        pad[pl.ds((T + 8) * NH, 8 * NH), :] = dn[...] * nmask

        def chunk(ci, carry):
            base = pl.multiple_of(ci * (CONV_CHUNK * NH), CONV_CHUNK * NH)
            xav = xa_ref[pl.ds(base, CONV_CHUNK * NH), :].reshape(CONV_CHUNK, NH, HD)
            acc = None
            for k in range(4):
                sl = pad[pl.ds(base + (9 - k) * NH, CONV_CHUNK * NH), :].reshape(CONV_CHUNK, NH, HD)
                term = sl * cw[k][None]
                acc = term if acc is None else acc + term
                dcw_ref[k] += jnp.sum(sl * xav, axis=0)
                if k == 1:
                    dcb_ref[...] += jnp.sum(sl, axis=0)
            dxa_s[pl.ds(base, CONV_CHUNK * NH), :] = acc.reshape(CONV_CHUNK * NH, HD)
            return carry
        lax.fori_loop(0, T // CONV_CHUNK, chunk, 0)
        for h in range(NH):
            dxa_ref[:, HD * h:HD * h + HD] = dxa_s[_zrows(h, T), :].astype(BF16)

    full = lambda shape: pl.BlockSpec(shape, lambda i: (0,) * len(shape))
    return pl.pallas_call(
        body, name=name, grid=(n,),
        out_shape=(jax.ShapeDtypeStruct((lx, D), BF16), jax.ShapeDtypeStruct((4, NH, HD), F32),
                   jax.ShapeDtypeStruct((NH, HD), F32)),
        in_specs=_zhalo_specs(lx) + [ZT, full((4, NH, HD)), full((4, NH, HD)), full((NH, HD))],
        out_specs=(pl.BlockSpec((T, D), lambda i: (i, 0)), full((4, NH, HD)), full((NH, HD))),
        scratch_shapes=[pltpu.VMEM(((T + 16) * NH, HD), F32), pltpu.VMEM((T * NH, HD), F32)],
        compiler_params=_cp(1, vmem_mb=48),
    )(dxc_z, dxc_z, dxc_z, xa_z, conv_wz, dcw0, dcb0)


def _proj_bwd(dzs, x, dxn, mods, mrow, norm_g, w_full, dng0, name, sgu=None):
    lx = x.shape[0]
    n = lx // T
    nz = len(dzs)
    has_x = dxn is not None
    wks = ([0, 1, 4, 2, 3] if sgu is not None else list(range(nz)))

    def body(*refs):
        it = iter(refs)
        take = lambda m: [next(it) for _ in range(m)]
        dz_refs, w_refs = take(nz), take(len(wks))
        x_ref, sc_ref, ng_ref, dng0_ref = take(4)
        dxn_ref = take(1)[0] if has_x else None
        if sgu is not None:
            u_ref, v_ref, dy_ref, g_ref, b_ref, sw_ref, bt_ref = take(7)
        gx_ref = take(1)[0] if has_x else None
        dng_ref, dsc_ref, dsh_ref = take(3)
        if sgu is not None:
            du_ref, dv_ref, dws_ref, dbs_ref, dlg_ref, dlb_ref, mixed_s, dvn_s = take(8)
        i = pl.program_id(0)

        @pl.when(i == 0)
        def _():
            dng_ref[...] = dng0_ref[...]
            dsc_ref[...] = jnp.zeros_like(dsc_ref)
            dsh_ref[...] = jnp.zeros_like(dsh_ref)
            if sgu is not None:
                for acc in (dws_ref, dbs_ref, dlg_ref, dlb_ref):
                    acc[...] = jnp.zeros_like(acc)

        dhn = _dot_nt(dz_refs[0][...], w_refs[0][...])
        for k in range(1, nz):
            dhn = dhn + _dot_nt(dz_refs[k][...], w_refs[k][...])
        if sgu is not None:
            for ch in range(T // HD):
                rows = slice(HD * ch, HD * ch + HD)
                du, dv = _sgu_bwd_chunk(u_ref[rows, :], v_ref[rows, :], dy_ref[rows, :], g_ref[...], b_ref[...],
                                        sw_ref, bt_ref, mixed_s, dvn_s, dws_ref, dbs_ref, dlg_ref, dlb_ref)
                du_ref[rows, :] = du
                dv_ref[rows, :] = dv
            dhn = dhn + _dot_nt(du_ref[...], w_refs[nz][...]) + _dot_nt(dv_ref[...], w_refs[nz + 1][...])
        xv = x_ref[...]
        r = lax.rsqrt(jnp.mean(xv * xv, axis=-1, keepdims=True) + NORM_EPS)
        xn = xv * r
        ng = ng_ref[...]
        sc1 = 1.0 + sc_ref[mrow:mrow + 1, :]
        t = dhn * xn
        dng_ref[...] += jnp.sum(t * sc1, axis=0, keepdims=True)
        dsc_ref[...] += jnp.sum(t * ng, axis=0, keepdims=True)
        dsh_ref[...] += jnp.sum(dhn, axis=0, keepdims=True)
        if has_x:
            dxh = dhn * (ng * sc1)
            gx_ref[...] = dxn_ref[...] + r * (dxh - xn * jnp.mean(dxh * xn, axis=-1, keepdims=True))

    row = pl.BlockSpec((T, D), lambda i: (i, 0))
    vec = pl.BlockSpec((1, D), lambda i: (0, 0))
    in_specs = [row] * nz + [pl.BlockSpec((D, D), lambda i, k=k: (0, k)) for k in wks]
    in_specs += [row, pl.BlockSpec((8, D), lambda i: (0, 1)), vec, vec]
    args = list(dzs) + [w_full] * len(wks) + [x, mods, norm_g, dng0]
    vs = jax.ShapeDtypeStruct((1, D), F32)
    out_shape, out_specs = (vs, vs, vs), (vec, vec, vec)
    scratch = []
    if has_x:
        in_specs.append(row)
        args.append(dxn)
        out_shape = (jax.ShapeDtypeStruct((lx, D), F32),) + out_shape
        out_specs = (row,) + out_specs
    if sgu is not None:
        wsp = pl.BlockSpec((NH, HD, HD), lambda i: (0, 0, 0))
        bsp = pl.BlockSpec((NH, HD), lambda i: (0, 0))
        in_specs += [row, row, row, vec, vec, wsp, pl.BlockSpec((HD, NH), lambda i: (0, 0))]
        args += list(sgu)
        zb = jax.ShapeDtypeStruct((lx, D), BF16)
        out_shape += (zb, zb, jax.ShapeDtypeStruct((NH, HD, HD), F32), jax.ShapeDtypeStruct((NH, HD), F32), vs, vs)
        out_specs += (row, row, wsp, bsp, vec, vec)
        scratch = [pltpu.VMEM((HD, D), F32), pltpu.VMEM((HD, D), F32)]
    return pl.pallas_call(
        body, name=name, grid=(n,), out_shape=out_shape, in_specs=in_specs, out_specs=out_specs,
        scratch_shapes=scratch, compiler_params=_cp(1, vmem_mb=56),
    )(*args)


def _tn_matmul(a, bs, extra, name):
    lx, m = a.shape
    tm = min(lx, 1024)
    n = lx // tm
    widths = [b.shape[1] for b in bs]
    nb = len(bs)

    def body(a_ref, *rest):
        b_refs = rest[:nb]
        rest = rest[nb:]
        if extra is not None:
            ea_ref, eb_ref = rest[:2]
            rest = rest[2:]
        out_ref, acc = rest
        i = pl.program_id(0)
        av = a_ref[...]
        off = 0
        for k in range(nb):
            cols = slice(off, off + widths[k])
            part = _dot_tn(av, b_refs[k][...])

            @pl.when(i == 0)
            def _():
                acc[:, cols] = part

            @pl.when(i > 0)
            def _():
                acc[:, cols] += part
            off += widths[k]

        @pl.when(i == n - 1)
        def _():
            if extra is not None:
                acc[:, 0:widths[0]] += _dot_tn(ea_ref[...], eb_ref[...])
            pltpu.sync_copy(acc, out_ref)

    in_specs = [pl.BlockSpec((tm, m), lambda i: (i, 0))] + [pl.BlockSpec((tm, w), lambda i: (i, 0)) for w in widths]
    args = [a] + list(bs)
    if extra is not None:
        in_specs += [VMEM, VMEM]
        args += list(extra)
    return pl.pallas_call(
        body, name=name, grid=(n,), out_shape=jax.ShapeDtypeStruct((m, sum(widths)), F32),
        in_specs=in_specs, out_specs=ANY, scratch_shapes=[pltpu.VMEM((m, sum(widths)), F32)],
        compiler_params=_cp(1, vmem_mb=56),
    )(*args)


def _adam_math(w, g, m, v):
    m = ADAM_B1 * m + (1.0 - ADAM_B1) * g
    v = ADAM_B2 * v + (1.0 - ADAM_B2) * (g * g)
    m_hat = m / (1.0 - ADAM_B1 ** ADAM_STEP)
    v_hat = v / (1.0 - ADAM_B2 ** ADAM_STEP)
    delta = -ADAM_LR * (m_hat / (jnp.sqrt(v_hat) + ADAM_EPS) + ADAM_WD * w)
    return delta, m, v


def _adam_big(w, g, m, v, name):
    rows, cols = w.shape
    tr = 256

    def body(w_ref, g_ref, m_ref, v_ref, d_o, m_o, v_o):
        d, mm, vv = _adam_math(w_ref[...], g_ref[...], m_ref[...], v_ref[...])
        d_o[...] = d
        m_o[...] = mm
        v_o[...] = vv

    blk = pl.BlockSpec((tr, cols), lambda i: (i, 0))
    s = jax.ShapeDtypeStruct((rows, cols), F32)
    return pl.pallas_call(
        body, name=name, grid=(rows // tr,), out_shape=(s, s, s), in_specs=[blk] * 4, out_specs=(blk,) * 3,
        compiler_params=_cp(1, vmem_mb=48),
    )(w, g, m, v)


def _adam_small(items):
    ni = len(items)

    def body(*refs):
        ins, outs = refs[:4 * ni], refs[4 * ni:7 * ni]
        bufs_in, bufs_out = refs[7 * ni:11 * ni], refs[11 * ni:14 * ni]
        sem_in, sem_out = refs[14 * ni], refs[14 * ni + 1]
        loads = [pltpu.make_async_copy(ins[q], bufs_in[q], sem_in.at[q]) for q in range(4 * ni)]
        for cp in loads:
            cp.start()
        stores = []
        for k in range(ni):
            for q in range(4):
                loads[4 * k + q].wait()
            w_b, g_b, m_b, v_b = bufs_in[4 * k:4 * k + 4]
            res = _adam_math(w_b[...], g_b[...], m_b[...], v_b[...])
            for q in range(3):
                bufs_out[3 * k + q][...] = res[q]
                cp = pltpu.make_async_copy(bufs_out[3 * k + q], outs[3 * k + q], sem_out.at[3 * k + q])
                cp.start()
                stores.append(cp)
        for cp in stores:
            cp.wait()

    flat = [a for it in items for a in it]
    out_shape = tuple(jax.ShapeDtypeStruct(it[0].shape, F32) for it in items for _ in range(3))
    scratch = [pltpu.VMEM(a.shape, F32) for a in flat] + [pltpu.VMEM(s.shape, F32) for s in out_shape]
    scratch += [pltpu.SemaphoreType.DMA((4 * ni,)), pltpu.SemaphoreType.DMA((3 * ni,))]
    res = pl.pallas_call(
        body, name="adam_small", out_shape=out_shape, in_specs=[HBM] * (4 * ni), out_specs=(HBM,) * (3 * ni),
        scratch_shapes=scratch, compiler_params=_cp(vmem_mb=40),
    )(*flat)
    return [tuple(res[3 * k:3 * k + 3]) for k in range(ni)]


def kernel(x, c, ctx, c_ctx, ada_w, ada_b, norm_g, w_in, conv_w, conv_b, lru_wa, lru_ba, lru_wx, lru_bx, lru_lambda, sgu_ln_g, sgu_ln_b, sgu_w, sgu_b, w_out, final_g, loss_target, m_c_ctx, m_ada_w, m_ada_b, m_norm_g, m_w_in, m_conv_w, m_conv_b, m_lru_wa, m_lru_ba, m_lru_wx, m_lru_bx, m_lru_lambda, m_sgu_ln_g, m_sgu_ln_b, m_sgu_w, m_sgu_b, m_w_out, m_final_g, v_c_ctx, v_ada_w, v_ada_b, v_norm_g, v_w_in, v_conv_w, v_conv_b, v_lru_wa, v_lru_ba, v_lru_wx, v_lru_bx, v_lru_lambda, v_sgu_ln_g, v_sgu_ln_b, v_sgu_w, v_sgu_b, v_w_out, v_final_g):
    ix, iy, ic = lax.axis_index("x"), lax.axis_index("y"), lax.axis_index("c")
    chip = 2 * ix + iy
    dev = 2 * chip + ic
    lx = x.shape[1]
    lc = ctx.shape[1]

    smalls = jnp.concatenate([conv_w[0], lru_lambda[0], jnp.zeros((10, 256), F32)], axis=0)
    c_ctx2 = c_ctx.reshape(1, D)
    ada_b_j = lax.dynamic_slice(ada_b, (0, 768 * chip), (1, 768))
    mods, c_slots, sm_all, w_in_full, wo_land, ada_land = _gather_in(c, c_ctx2, ada_w[0], ada_b_j, w_in[0], w_out[0],
                                                                     smalls)
    wo_ss, wo_rs, ada_ss, ada_rs, wo_land, ada_land, token = _late_gather_start(wo_land, ada_land)
    mods = mods + token[0:1, 0:1]
    sm3 = sm_all.reshape(NCHIP, 16, 256)
    conv_w_full = sm3[:, 0:4, :].transpose(1, 0, 2).reshape(4, D)
    lam_full = sm3[:, 4:6, :].transpose(1, 0, 2).reshape(2, D)
    conv_wz = conv_w_full.reshape(4, NH, HD)
    conv_bz = conv_b.reshape(NH, HD)
    lamcat = lam_full.reshape(2, NH, HD).transpose(1, 0, 2).reshape(NH, 2 * HD)
    wa, wx, ba, bx = lru_wa[0], lru_wx[0], lru_ba[0], lru_bx[0]
    wcat = jnp.concatenate([wa[0], wx[0], wa[1], wx[1]], axis=-1).astype(BF16)
    bcat = jnp.concatenate([ba[0], bx[0], ba[1], bx[1]], axis=-1)
    sgu_wb = sgu_w[0].astype(BF16)
    sgu_bt = sgu_b[0].T
    final_g2 = final_g.reshape(1, D)

    zero_s = jnp.zeros((NH, HD), F32)
    hn_c, xa_c = _proj(ctx[0], mods, 1, norm_g, w_in_full, 1, "proj_ctx")
    xaz_c, xcz_c, af_c, ab_c, hf_c, hb_c, gf_c, gb_c, hf0, hb0 = _lru_fwd(xa_c, conv_wz, conv_bz, wcat, bcat, lamcat,
                                                                           zero_s, zero_s, "lru_fwd_ctx")

    hn, xa, ga, u, v, gb, ys = _proj(x[0], mods, 0, norm_g, w_in_full, 5, "proj",
                                     (sgu_ln_g, sgu_ln_b, sgu_wb, sgu_bt))
    xaz, xcz, af, ab, hf, hb, gf, gb_l, _, _ = _lru_fwd(xa, conv_wz, conv_bz, wcat, bcat, lamcat, hf0, hb0, "lru_fwd")

    w_out_full = _late_gather_wait(wo_land, wo_ss, wo_rs, "w_out", hf, "late_gather_wait_w_out")
    (loss_part, dfg, dgx, dxn, y, do, dga, dgb, dyl_z, dys) = _out_fwd_bwd(
        hf, hb, ga, gb, ys, x[0], loss_target[0], mods, final_g2, w_out_full)
    g_w_out_part = _tn_matmul(y, [do], None, "grad_w_out")

    lb, lf, dh0b, dh0f = _scan_adjoint(ab, dyl_z, af, dyl_z, "scan_adj")
    zw = jnp.zeros((NH, HD, 4 * HD), F32)
    zb = jnp.zeros((NH, 4 * HD), F32)
    zl = jnp.zeros((NH, 2 * HD), F32)
    dxc_z, dwc, dbc, dlc = _lru_gates_bwd(xcz, lf, lb, hf, hb, af, ab, gf, gb_l, hf0, hb0, wcat, lamcat, zw, zb, zl,
                                          "lru_gates_bwd")
    dxa, dcw, dcb = _conv_bwd(dxc_z, xaz, conv_wz, jnp.zeros((4, NH, HD), F32), zero_s, "conv_bwd")

    zc = jnp.zeros((lc * NH, HD), F32)
    dhf_c = lax.dynamic_update_slice(zc, dh0f, ((lc - 1) * NH, 0))
    dhb_c = lax.dynamic_update_slice(zc, dh0b, (0, 0))
    lb_c, lf_c, _, _ = _scan_adjoint(ab_c, dhb_c, af_c, dhf_c, "scan_adj_ctx")
    dxc_zc, dwc, dbc, dlc = _lru_gates_bwd(xcz_c, lf_c, lb_c, hf_c, hb_c, af_c, ab_c, gf_c, gb_c, zero_s, zero_s,
                                           wcat, lamcat, dwc, dbc, dlc, "lru_gates_bwd_ctx")
    dxa_c, dcw, dcb = _conv_bwd(dxc_zc, xaz_c, conv_wz, dcw, dcb, "conv_bwd_ctx")

    grad_x, dng, dsc_x, dsh_x, du, dv, d_sgu_w, d_sgu_b, d_ln_g, d_ln_b = _proj_bwd(
        [dxa, dga, dgb], x[0], dxn, mods, 0, norm_g, w_in_full, jnp.zeros((1, D), F32), "proj_bwd",
        (u, v, dys, sgu_ln_g, sgu_ln_b, sgu_wb, sgu_bt))
    dzs = [dxa, dga, du, dv, dgb]
    dng, dsc_c, dsh_c = _proj_bwd([dxa_c], ctx[0], None, mods, 1, norm_g, w_in_full, dng, "proj_bwd_ctx")

    dmx = jnp.concatenate([dsh_x, dsc_x, dgx], axis=0)
    dmc = jnp.concatenate([dsh_c, dsc_c, jnp.zeros((1, D), F32)], axis=0)
    lp = loss_part[0, 0]
    lp1 = lax.reduce_precision(lp, 8, 7)
    lp2 = lax.reduce_precision(lp - lp1, 8, 7)
    lp3 = lax.reduce_precision(lp - lp1 - lp2, 8, 7)
    loss_row = jnp.pad(jnp.stack([lp1, lp2, lp3]).reshape(1, 3), ((0, 0), (0, D - 3)))
    slot = jnp.concatenate([dmx, loss_row], axis=0)
    slots = lax.dynamic_update_slice(jnp.zeros((32, D), F32), slot, (4 * dev, 0))
    vecs = jnp.concatenate([dfg, dng, dcb.reshape(1, D), d_ln_g, d_ln_b, dcw.reshape(4, D), dmc,
                            jnp.zeros((4, D), F32), slots], axis=0)
    d_sgu_w4 = d_sgu_w.reshape(4, 256, HD).transpose(1, 0, 2).reshape(256, 4 * HD)
    pad8 = lambda a: jnp.pad(a, ((0, 8 - a.shape[0]), (0, 4 * HD - a.shape[1])))
    pack = jnp.concatenate([dwc.reshape(NH * HD, 4 * HD), pad8(dbc), pad8(dlc), d_sgu_w4, pad8(d_sgu_b),
                            vecs.reshape(96, 4 * HD), jnp.zeros((8, 4 * HD), F32)], axis=0)
    g_w_in, g_w_out, tot = _grads_reduce(hn, dzs, hn_c, dxa_c, g_w_out_part, pack)

    g_wc = tot[0:1024].reshape(NH, HD, 4 * HD)
    g_bc = tot[1024:1032]
    g_lc = tot[1032:1040, 0:2 * HD]
    g_sgu_w = tot[1040:1296].reshape(256, 4, HD).transpose(1, 0, 2).reshape(NH, HD, HD)
    g_sgu_b = tot[1296:1304, 0:HD]
    tv = tot[1304:1400].reshape(48, D)
    g_final_g, g_norm_g, g_conv_b, g_ln_g, g_ln_b = tv[0:1], tv[1:2], tv[2:3], tv[3:4], tv[4:5]
    g_conv_w_full = tv[5:9]
    dmc_tot = tv[9:12].reshape(1, 3 * D)
    slots_all = tv[16:48].reshape(8, 4, D)
    dmx_all = slots_all[:, 0:3, :].reshape(8, 3 * D)
    c_all = c_slots.reshape(8, 8, D)[:, 0, :]
    g_lru_wa = jnp.stack([g_wc[:, :, 0:HD], g_wc[:, :, 2 * HD:3 * HD]])
    g_lru_wx = jnp.stack([g_wc[:, :, HD:2 * HD], g_wc[:, :, 3 * HD:4 * HD]])
    g_lru_ba = jnp.stack([g_bc[:, 0:HD], g_bc[:, 2 * HD:3 * HD]])
    g_lru_bx = jnp.stack([g_bc[:, HD:2 * HD], g_bc[:, 3 * HD:4 * HD]])
    g_lam_full = jnp.stack([g_lc[:, 0:HD], g_lc[:, HD:2 * HD]]).reshape(2, D)
    g_conv_w = lax.dynamic_slice(g_conv_w_full, (0, 256 * chip), (4, 256))
    g_lam = lax.dynamic_slice(g_lam_full, (0, 256 * chip), (2, 256))
    dmx_all_j = lax.dynamic_slice(dmx_all, (0, 768 * chip), (8, 768))
    dmc_j = lax.dynamic_slice(dmc_tot, (0, 768 * chip), (1, 768))
    ada_full = _late_gather_wait(ada_land, ada_ss, ada_rs, "ada_w", tot, "late_gather_wait_ada_w")
    g_ada_w, g_ada_b, g_c_ctx = _ada_bwd(c_all, dmx_all_j, dmc_j, dmx_all, dmc_tot, c_ctx2, ada_full)

    big = {
        "ada_w": _adam_big(ada_w[0], g_ada_w, m_ada_w[0], v_ada_w[0], "adam_ada_w"),
        "w_in": _adam_big(w_in[0], g_w_in, m_w_in[0], v_w_in[0], "adam_w_in"),
        "w_out": _adam_big(w_out[0], g_w_out, m_w_out[0], v_w_out[0], "adam_w_out"),
    }
    small_in = {
        "c_ctx": (c_ctx, g_c_ctx, m_c_ctx, v_c_ctx, (1, D)),
        "ada_b": (ada_b, g_ada_b, m_ada_b, v_ada_b, (1, 3 * D)),
        "norm_g": (norm_g, g_norm_g, m_norm_g, v_norm_g, (1, D)),
        "conv_w": (conv_w, g_conv_w, m_conv_w, v_conv_w, (4, 256)),
        "conv_b": (conv_b, g_conv_b, m_conv_b, v_conv_b, (1, D)),
        "lru_wa": (lru_wa, g_lru_wa, m_lru_wa, v_lru_wa, (2 * NH * HD, HD)),
        "lru_ba": (lru_ba, g_lru_ba, m_lru_ba, v_lru_ba, (2 * NH, HD)),
        "lru_wx": (lru_wx, g_lru_wx, m_lru_wx, v_lru_wx, (2 * NH * HD, HD)),
        "lru_bx": (lru_bx, g_lru_bx, m_lru_bx, v_lru_bx, (2 * NH, HD)),
        "lru_lambda": (lru_lambda, g_lam, m_lru_lambda, v_lru_lambda, (2, 256)),
        "sgu_ln_g": (sgu_ln_g, g_ln_g, m_sgu_ln_g, v_sgu_ln_g, (1, D)),
        "sgu_ln_b": (sgu_ln_b, g_ln_b, m_sgu_ln_b, v_sgu_ln_b, (1, D)),
        "sgu_w": (sgu_w, g_sgu_w, m_sgu_w, v_sgu_w, (NH * HD, HD)),
        "sgu_b": (sgu_b, g_sgu_b, m_sgu_b, v_sgu_b, (NH, HD)),
        "final_g": (final_g, g_final_g, m_final_g, v_final_g, (1, D)),
    }
    names_small = list(small_in)
    res_small = _adam_small([tuple(a.reshape(small_in[k][4]) for a in small_in[k][:4]) for k in names_small])
    full_shapes = {"ada_w": ada_w.shape, "w_in": w_in.shape, "w_out": w_out.shape}
    grads, deltas, new_m, new_v = {}, {}, {}, {}
    for k in ("ada_w", "w_in", "w_out"):
        g = {"ada_w": g_ada_w, "w_in": g_w_in, "w_out": g_w_out}[k]
        grads[k] = g.reshape(full_shapes[k])
        deltas[k], new_m[k], new_v[k] = (a.reshape(full_shapes[k]) for a in big[k])
    for k, res in zip(names_small, res_small):
        shape = small_in[k][0].shape
        grads[k] = small_in[k][1].reshape(shape)
        deltas[k], new_m[k], new_v[k] = (a.reshape(shape) for a in res)

    loss = jnp.sum(slots_all[:, 3, 0:3])
    order = ["c_ctx", "ada_w", "ada_b", "norm_g", "w_in", "conv_w", "conv_b", "lru_wa", "lru_ba", "lru_wx", "lru_bx",
             "lru_lambda", "sgu_ln_g", "sgu_ln_b", "sgu_w", "sgu_b", "w_out", "final_g"]
    return (loss, grad_x.reshape(x.shape), *[grads[k] for k in order], *[deltas[k] for k in order],
            *[new_m[k] for k in order], *[new_v[k] for k in order])
```

```python
import functools

import jax
import jax.numpy as jnp
from jax import lax
from jax.experimental import pallas as pl
from jax.experimental.pallas import tpu as pltpu

F32 = jnp.float32
BF16 = jnp.bfloat16

D = 1024
NH = 8
HD = 128
NCHIP = 4
T = 256
NORM_EPS = 1e-6
LN_EPS = 1e-5
LRU_C = 8.0
ADAM_LR = 0.001
ADAM_B1 = 0.9
ADAM_B2 = 0.999
ADAM_EPS = 1e-08
ADAM_WD = 0.01
ADAM_STEP = 10

VMEM = pl.BlockSpec(memory_space=pltpu.VMEM)
ANY = pl.BlockSpec(memory_space=pl.ANY)
MESH = pl.DeviceIdType.MESH


def _cp(n_grid=0, vmem_mb=None):
    kw = {}
    if n_grid:
        kw["dimension_semantics"] = ("arbitrary",) * n_grid
    if vmem_mb:
        kw["vmem_limit_bytes"] = vmem_mb << 20
    return pltpu.CompilerParams(**kw)


def _sigmoid(x):
    return 1.0 / (1.0 + jnp.exp(-x))


def _silu_and_grad(x):
    s = _sigmoid(x)
    return x * s, s * (1.0 + x * (1.0 - s))


_GELU_K = 0.7978845608028654
_GELU_C = 0.044715


def _gelu_and_grad(x):
    x2 = x * x
    th = jnp.tanh(_GELU_K * (x + _GELU_C * x * x2))
    g = 0.5 * x * (1.0 + th)
    dg = 0.5 * (1.0 + th) + 0.5 * x * (1.0 - th * th) * (_GELU_K * (1.0 + 3.0 * _GELU_C * x2))
    return g, dg


def _softplus(x):
    return jnp.maximum(x, 0.0) + jnp.log1p(jnp.exp(-jnp.abs(x)))


def _lru_gate(pre, lam_row, d, off=None):
    off = 256 * d if off is None else off
    r = _sigmoid(pre[:, off:off + HD])
    gi = _sigmoid(pre[:, off + HD:off + 2 * HD])
    lam = lam_row[:, HD * d:HD * d + HD]
    sp = _softplus(-lam)
    la = (-LRU_C) * r * sp
    a = jnp.exp(la)
    x2 = 2.0 * la
    m2 = jnp.where(x2 > -1e-3, -x2 * (1.0 + 0.5 * x2), 1.0 - a * a)
    mult = jnp.sqrt(m2)
    return r, gi, lam, sp, a, mult


def _dot(a, b):
    return jnp.dot(a, b, preferred_element_type=F32)


def _dot_tn(a, b):
    return lax.dot_general(a, b, (((0,), (0,)), ((), ())), preferred_element_type=F32)


def _dot_nt(a, b):
    return lax.dot_general(a, b, (((1,), (1,)), ((), ())), preferred_element_type=F32)


def _mo(v, m):
    return v if isinstance(v, int) else pl.multiple_of(v, m)


def _zrows(h, n):
    return pl.ds(h, n, stride=NH)


def _gather_in(c, c_ctx, ada_w, ada_b_j, w_in, w_out, smalls):
    nch = [1, 4]
    wrows = lambda cc, q: (pl.ds(_mo(512 * cc, 16), 512) if q is None
                           else pl.ds(_mo(512 * cc + (512 // nch[1]) * q, 16), 512 // nch[1]))
    specs = [
        ((64, 256), F32, lambda r, jj, cc, q=None: r.at[pl.ds(_mo(16 * jj + 8 * cc, 8), 8), :]),
        ((D, 5120), BF16, lambda r, jj, cc, q=None: r.at[wrows(cc, q), pl.ds(_mo(1280 * jj, 128), 1280)]),
    ]
    halves = [lambda r, cc, q=None: r.at[pl.ds(_mo(8 * cc, 8), 8), :],
              lambda r, cc, q=None: r.at[wrows(cc, q), :]]
    na = len(specs)
    sem_base = [0, 6 * nch[0]]
    sidx = lambda a, q, k: sem_base[a] + 6 * q + k
    n_tiny = 6 * sum(nch)
    n_sem = n_tiny + 10

    def body(c_ref, cc_ref, ada_ref, adab_ref, win_ref, wout_ref, sm_ref,
             mods_o, call_o, sm_o, win_o, wol_o, adal_o, s_win, s_ada, s_wout, f_win, f_ada, f_wout, cslot, lhs, mbuf,
             send_sems, recv_sems, local_sems, load_sems):
        x, y, c = lax.axis_index("x"), lax.axis_index("y"), lax.axis_index("c")
        j = 2 * x + y
        dev = 2 * j + c
        sib = (x, y, 1 - c)
        chips = [(1 - x, y), (x, 1 - y), (1 - x, 1 - y)]
        cj = [2 * cx + cy for cx, cy in chips]
        outs = [sm_o, win_o]
        srcs = [sm_ref, s_win]

        def copy(idx, src, dst, to):
            return pltpu.make_async_remote_copy(src_ref=src, dst_ref=dst, send_sem=send_sems.at[idx],
                                                recv_sem=recv_sems.at[idx], device_id=to, device_id_type=MESH)

        sends = []

        def start(cp):
            cp.start()
            sends.append(cp)

        cslot[...] = jnp.zeros_like(cslot)
        cslot[0:1, :] = c_ref[...]
        my_slot = pl.ds(_mo(8 * dev, 8), 8)
        others = [sib] + [(*chips[k], c) for k in range(3)] + [(*chips[k], 1 - c) for k in range(3)]
        other_dev = [dev + 1 - 2 * c] + [2 * cj[k] + c for k in range(3)] + [2 * cj[k] + 1 - c for k in range(3)]
        base = n_tiny
        for r in range(7):
            start(copy(base + r, cslot, call_o.at[my_slot, :], others[r]))
        call_o[my_slot, :] = cslot[...]

        crow = 512 // nch[1]
        loads = []
        for cc in (c, 1 - c):
            for q in range(nch[1]):
                rows = pl.ds(_mo(512 * cc + crow * q, 16), crow)
                loads.append(pltpu.make_async_copy(win_ref.at[rows, :], f_win.at[rows, :], load_sems.at[len(loads)]))
        loads.append(pltpu.make_async_copy(ada_ref, f_ada, load_sems.at[len(loads)]))
        loads.append(pltpu.make_async_copy(wout_ref, f_wout, load_sems.at[len(loads)]))
        for ld in loads:
            ld.start()
        for k in range(2):
            start(copy(sidx(0, 0, k), halves[0](srcs[0], c), specs[0][2](outs[0], j, c), (*chips[k], c)))
        for q in range(nch[1]):
            loads[q].wait()
            rows = pl.ds(_mo(512 * c + crow * q, 16), crow)
            s_win[rows, :] = f_win[rows, :].astype(BF16)
            for k in range(2):
                start(copy(sidx(1, q, k), halves[1](s_win, c, q), specs[1][2](win_o, j, c, q), (*chips[k], c)))
        for q in range(nch[1]):
            loads[nch[1] + q].wait()
            rows = pl.ds(_mo(512 * (1 - c) + crow * q, 16), crow)
            s_win[rows, :] = f_win[rows, :].astype(BF16)
        local = []
        for a in range(na):
            for cc in range(2):
                lc = pltpu.make_async_copy(halves[a](srcs[a], cc), specs[a][2](outs[a], j, cc), local_sems.at[2 * a + cc])
                lc.start()
                local.append(lc)
        loads[2 * nch[1]].wait()
        s_ada[...] = f_ada[...].astype(BF16)
        loads[2 * nch[1] + 1].wait()
        s_wout[...] = f_wout[...].astype(BF16)
        for q, (src, dst) in enumerate([(s_wout, wol_o.at[pl.ds(_mo(512 * j, 16), 512), :]),
                                        (s_ada, adal_o.at[:, pl.ds(_mo(768 * j, 128), 768)])]):
            lc = pltpu.make_async_copy(src, dst, local_sems.at[2 * na + q])
            lc.start()
            local.append(lc)

        for r in range(7):
            slot = call_o.at[pl.ds(_mo(8 * other_dev[r], 8), 8), :]
            copy(base + r, slot, slot, sib).wait_recv()
        lhs[...] = jnp.zeros_like(lhs)
        for b in range(8):
            cv = call_o[8 * b:8 * b + 1, :]
            lhs[b:b + 1, :] = cv * _sigmoid(cv)
        cv = cc_ref[...]
        lhs[8:9, :] = cv * _sigmoid(cv)
        mbuf[j] = _dot(lhs[...].astype(BF16), s_ada[...]) + adab_ref[...]
        for k in range(3):
            start(copy(base + 7 + k, mbuf.at[j], mbuf.at[j], (*chips[k], c)))
        for k in range(3):
            copy(base + 7 + k, mbuf.at[cj[k]], mbuf.at[cj[k]], sib).wait_recv()
        mods_o[...] = jnp.zeros_like(mods_o)
        for jj in range(NCHIP):
            mods_o[0:1, 768 * jj:768 * jj + 768] = mbuf[jj, pl.ds(dev, 1), :]
            mods_o[1:2, 768 * jj:768 * jj + 768] = mbuf[jj, 8:9, :]

        kx = [1 - x, x, 1 - x]
        ky = [y, 1 - y, 1 - y]
        pick = lambda k, lst: jnp.where(k == 0, lst[0], jnp.where(k == 1, lst[1], lst[2]))
        for a in range(na):
            for q in range(nch[a]):
                for step, k in enumerate([c, 1 - c]):
                    reg = specs[a][2](outs[a], pick(k, cj), c, q)
                    copy(sidx(a, q, k), reg, reg, sib).wait_recv()
                    if step == 0:
                        start(copy(sidx(a, q, 2), reg, reg, (pick(1 - c, kx), pick(1 - c, ky), c)))
                    start(copy(sidx(a, q, 3 + k), reg, reg, sib))
        for a in range(na):
            for q in range(nch[a]):
                reg = specs[a][2](outs[a], cj[2], c, q)
                copy(sidx(a, q, 2), reg, reg, sib).wait_recv()
                start(copy(sidx(a, q, 5), reg, reg, sib))
        for a in range(na):
            for q in range(nch[a]):
                for k in range(3):
                    reg = specs[a][2](outs[a], cj[k], 1 - c, q)
                    copy(sidx(a, q, 3 + k), reg, reg, sib).wait_recv()
        for cp in sends:
            cp.wait_send()
        for lc in local:
            lc.wait()

    out_shape = (jax.ShapeDtypeStruct((8, 3 * D), F32), jax.ShapeDtypeStruct((64, D), F32),
                 jax.ShapeDtypeStruct(specs[0][0], F32), jax.ShapeDtypeStruct(specs[1][0], BF16),
                 jax.ShapeDtypeStruct((2048, D), BF16), jax.ShapeDtypeStruct((D, 3 * D), BF16))
    return pl.pallas_call(
        body, name="gather_in", out_shape=out_shape,
        in_specs=[VMEM, VMEM, ANY, VMEM, ANY, ANY, VMEM], out_specs=(VMEM, VMEM, VMEM, ANY, ANY, ANY),
        scratch_shapes=[pltpu.VMEM((D, 1280), BF16), pltpu.VMEM((D, 768), BF16), pltpu.VMEM((512, D), BF16),
                        pltpu.VMEM((D, 1280), F32), pltpu.VMEM((D, 768), F32), pltpu.VMEM((512, D), F32),
                        pltpu.VMEM((8, D), F32), pltpu.VMEM((16, D), F32), pltpu.VMEM((NCHIP, 16, 768), F32),
                        pltpu.SemaphoreType.DMA((n_sem,)), pltpu.SemaphoreType.DMA((n_sem,)),
                        pltpu.SemaphoreType.DMA((2 * na + 2,)), pltpu.SemaphoreType.DMA((2 * nch[1] + 2,))],
        compiler_params=_cp(vmem_mb=56),
    )(c, c_ctx, ada_w, ada_b_j, w_in, w_out, smalls)


HBM = pl.BlockSpec(memory_space=pltpu.HBM)
SEM = pl.BlockSpec(memory_space=pltpu.SEMAPHORE)


def _late_gather_regions(x, y, c):
    chips = [(1 - x, y), (x, 1 - y), (1 - x, 1 - y)]
    wo_reg = lambda r, jj, cc: r.at[pl.ds(_mo(512 * jj + 256 * cc, 16), 256), :]
    ada_reg = lambda r, jj, cc: r.at[pl.ds(_mo(512 * cc, 16), 512), pl.ds(_mo(768 * jj, 128), 768)]
    return chips, wo_reg, ada_reg


def _late_gather_start(wo_land, ada_land):
    def body(wol_ref, adal_ref, wo_ss, wo_rs, ada_ss, ada_rs, wol_thru, adal_thru, token):
        x, y, c = lax.axis_index("x"), lax.axis_index("y"), lax.axis_index("c")
        j = 2 * x + y
        chips, wo_reg, ada_reg = _late_gather_regions(x, y, c)
        for k in range(3):
            for cc in range(2):
                pltpu.make_async_remote_copy(src_ref=wo_reg(wol_ref, j, c), dst_ref=wo_reg(wol_ref, j, c),
                                             send_sem=wo_ss.at[2 * k + cc], recv_sem=wo_rs.at[2 * k + c],
                                             device_id=(*chips[k], cc), device_id_type=MESH).start()
        for k in range(3):
            for cc in range(2):
                pltpu.make_async_remote_copy(src_ref=ada_reg(adal_ref, j, c), dst_ref=ada_reg(adal_ref, j, c),
                                             send_sem=ada_ss.at[2 * k + cc], recv_sem=ada_rs.at[2 * k + c],
                                             device_id=(*chips[k], cc), device_id_type=MESH).start()
        token[...] = jnp.zeros_like(token)

    sems = pltpu.SemaphoreType.DMA((6,))
    return pl.pallas_call(
        body, name="late_gather_start",
        out_shape=(sems, sems, sems, sems, pltpu.HBM(wo_land.shape, BF16), pltpu.HBM(ada_land.shape, BF16),
                   jax.ShapeDtypeStruct((8, 128), F32)),
        in_specs=(HBM, HBM), out_specs=(SEM, SEM, SEM, SEM, HBM, HBM, VMEM), input_output_aliases={0: 4, 1: 5},
        compiler_params=pltpu.CompilerParams(has_side_effects=pltpu.SideEffectType.DATAFLOW_SIDE_EFFECTING),
    )(pltpu.with_memory_space_constraint(wo_land, pltpu.HBM), pltpu.with_memory_space_constraint(ada_land, pltpu.HBM))


def _late_gather_wait(land, send_sems, recv_sems, which, after, name):
    def body(land_ref, ss, rs, after_ref, land_out):
        x, y, c = lax.axis_index("x"), lax.axis_index("y"), lax.axis_index("c")
        j = 2 * x + y
        chips, wo_reg, ada_reg = _late_gather_regions(x, y, c)
        reg = wo_reg if which == "w_out" else ada_reg
        for k in range(3):
            kj = 2 * chips[k][0] + chips[k][1]
            for cc in range(2):
                cp = pltpu.make_async_remote_copy(src_ref=reg(land_ref, j, c), dst_ref=reg(land_ref, kj, cc),
                                                  send_sem=ss.at[2 * k + cc], recv_sem=rs.at[2 * k + cc],
                                                  device_id=(*chips[k], cc), device_id_type=MESH)
                cp.wait_send()
                cp.wait_recv()

    return pl.pallas_call(
        body, name=name, out_shape=pltpu.HBM(land.shape, land.dtype),
        in_specs=(HBM, SEM, SEM, ANY), out_specs=HBM, input_output_aliases={0: 0},
        compiler_params=pltpu.CompilerParams(has_side_effects=pltpu.SideEffectType.DATAFLOW_SIDE_EFFECTING),
    )(land, send_sems, recv_sems, after)


RCHUNK = 16


def _grads_reduce(hn, dzs, hn_c, dxa_c, y, do, pack):
    rp = pack.shape[0]
    hp = rp // 2
    assert hp % RCHUNK == 0
    wi_w = 1280
    lx, lc = hn.shape[0], hn_c.shape[0]
    lt = lx + lc
    n_dz = len(dzs)

    def body(*refs):
        hn_hbm, dz_hbm = refs[0], refs[1:1 + n_dz]
        hnc_hbm, dxac_hbm, y_hbm, do_hbm, pk_hbm, wi_out, wo_out, pk_out = refs[1 + n_dz:9 + n_dz]
        (hn_mine, hn_other, dzbuf, wi_other, wi_mine, wi_recv, wi_send, wi_rb,
         y_blk, do_mine, do_other, wo_other, wo_mine, wo_recv, wo_send, wo_rb,
         pk_mine, pk_recv, pk_send, pk_rb, pk_own, send_sems, recv_sems, local_sems) = refs[9 + n_dz:]
        x, y, c = lax.axis_index("x"), lax.axis_index("y"), lax.axis_index("c")
        j = 2 * x + y
        sib = (x, y, 1 - c)
        chips = [(1 - x, y), (x, 1 - y), (1 - x, 1 - y)]
        cj = [2 * cx + cy for cx, cy in chips]
        near = (jnp.where(c == 0, 1 - x, x), jnp.where(c == 0, y, 1 - y), c)
        slabs = [cj[2], cj[0], cj[1], j]

        def copy(k, src, dst, to):
            return pltpu.make_async_remote_copy(src_ref=src, dst_ref=dst, send_sem=send_sems.at[k],
                                                recv_sem=recv_sems.at[k], device_id=to, device_id_type=MESH)

        def local(k, src, dst):
            cp = pltpu.make_async_copy(src, dst, local_sems.at[k])
            cp.start()
            return cp

        rows_half = lambda r, cc, n: r.at[pl.ds(_mo(cc * n, 16), n), :]
        cols_half = lambda r, cc, n: r.at[:, pl.ds(_mo(cc * n, 128), n)]
        pk_piece = lambda r, cc, jj: r.at[pl.ds(_mo(cc * hp, 16), hp), pl.ds(_mo(jj * 128, 128), 128)]

        sends = []

        def start(cp):
            cp.start()
            sends.append(cp)

        def dz_pieces(s):
            g0 = wi_w * s
            k0, off0 = g0 // D, g0 % D
            w0 = min(D - off0, wi_w)
            pieces = [(k0, off0, w0, 0)]
            if w0 < wi_w:
                pieces.append((k0 + 1, 0, wi_w - w0, w0))
            return pieces

        def dz_copies(s):
            cps = []
            for q, (k, off, w, dst) in enumerate(dz_pieces(s)):
                cps.append(pltpu.make_async_copy(dz_hbm[k].at[:, pl.ds(off, w)], dzbuf.at[pl.ds(0, lx), pl.ds(dst, w)],
                                                 local_sems.at[11 + q]))
            if s == 0:
                cps.append(pltpu.make_async_copy(dxac_hbm, dzbuf.at[pl.ds(lx, lc), pl.ds(0, D)], local_sems.at[13]))
            return cps

        def dz_load(sl):
            for s in range(NCHIP):
                @pl.when(sl == s)
                def _():
                    if s == 0:
                        dzbuf[pl.ds(lx, lc), pl.ds(D, wi_w - D)] = jnp.zeros((lc, wi_w - D), BF16)
                    else:
                        dzbuf[pl.ds(lx, lc), :] = jnp.zeros((lc, wi_w), BF16)
                    for cp in dz_copies(s):
                        cp.start()

        def dz_wait(sl):
            for s in range(NCHIP):
                @pl.when(sl == s)
                def _():
                    for cp in dz_copies(s):
                        cp.wait()

        l_pk = local(0, rows_half(pk_hbm, c, hp), pk_mine)
        start(copy(0, rows_half(pk_hbm, 1 - c, hp), pk_recv, sib))
        col = lambda r, cc: r.at[:, pl.ds(_mo(cc * 512, 128), 512)]
        do_loads = [local(7, col(do_hbm, c), do_mine), local(14, col(do_hbm, 1 - c), do_other)]
        hn_loads = [local(2, col(hn_hbm, c), hn_mine.at[pl.ds(0, lx), :]),
                    local(3, col(hnc_hbm, c), hn_mine.at[pl.ds(lx, lc), :]),
                    local(4, col(hn_hbm, 1 - c), hn_other.at[pl.ds(0, lx), :]),
                    local(5, col(hnc_hbm, 1 - c), hn_other.at[pl.ds(lx, lc), :])]
        y_copy = lambda s: pltpu.make_async_copy(col(y_hbm, slabs[s]), y_blk, local_sems.at[1])
        y_copy(0).start()
        dz_load(slabs[0])

        def pair_sum(mine, recv, send, nrows, keep, relayed=None):
            def step(i, carry):
                rows = pl.ds(_mo(i * RCHUNK, RCHUNK), RCHUNK)
                s = mine[rows, :] + recv[rows, :].astype(F32)
                if relayed is not None:
                    s = s + relayed[rows, :].astype(F32)
                if keep:
                    mine[rows, :] = s
                if send is not None:
                    send[rows, :] = s.astype(BF16)
                return carry
            lax.fori_loop(0, nrows // RCHUNK, step, 0)

        def chip_sum(own, rb, nrows, terms=(0, 1, 2)):
            def step(i, carry):
                rows = pl.ds(_mo(i * RCHUNK, RCHUNK), RCHUNK)
                acc = own[rows, :]
                for q in terms:
                    acc = acc + rb[q, rows, :].astype(F32)
                own[rows, :] = acc
                return carry
            lax.fori_loop(0, nrows // RCHUNK, step, 0)

        w_in_g = dict(other=wi_other, mine=wi_mine, recv=wi_recv, send=wi_send, rb=wi_rb, p1_sems=(2, 3, 4, 5), p2_sem=12,
                      p1=[None] * NCHIP, wait_load=lambda s: dz_wait(slabs[s]), load=lambda s: dz_load(slabs[s]),
                      dot_other=lambda: _dot_tn(hn_other[...], dzbuf[...]), dot_mine=lambda: _dot_tn(hn_mine[...], dzbuf[...]))
        w_out_g = dict(other=wo_other, mine=wo_mine, recv=wo_recv, send=wo_send, rb=wo_rb, p1_sems=(1, 24, 25, 26), p2_sem=9,
                       p1=[None] * NCHIP, wait_load=lambda s: y_copy(s).wait(), load=lambda s: y_copy(s).start(),
                       dot_other=lambda: _dot_tn(y_blk[...], do_other[...]), dot_mine=lambda: _dot_tn(y_blk[...], do_mine[...]))

        def piece_matmuls(g, s):
            if s >= 2:
                g["p1"][s - 2].wait_send()
            g["wait_load"](s)
            g["other"][s % 2] = g["dot_other"]().astype(BF16)
            g["p1"][s] = copy(g["p1_sems"][s], g["other"].at[s % 2], g["recv"].at[s], sib)
            g["p1"][s].start()
            g["mine"][s % 2] = g["dot_mine"]()
            if s + 1 < NCHIP:
                g["load"](s + 1)

        def piece_finish(g, s):
            mine, recv, send, rb, p2 = g["mine"].at[s % 2], g["recv"].at[s], g["send"], g["rb"], g["p2_sem"]
            nrows = mine.shape[0]
            copy(g["p1_sems"][s], recv, recv, sib).wait_recv()
            if s == 3:
                pair_sum(mine, recv, None, nrows, True)
                return
            if s == 0:
                pair_sum(mine, recv, send.at[0], nrows, False)
                start(copy(p2, send.at[0], rb.at[0], near))
                return
            adds_relayed = c == (1 if s == 1 else 0)

            @pl.when(adds_relayed)
            def _():
                copy(p2, rb.at[0], rb.at[0], sib).wait_recv()
                pair_sum(mine, recv, send.at[s], nrows, False, rb.at[0])

            @pl.when(jnp.logical_not(adds_relayed))
            def _():
                pair_sum(mine, recv, send.at[s], nrows, False)
            start(copy(p2 + s, send.at[s], rb.at[s], (*chips[s - 1], c)))

        def piece_total(g):
            for k in (1, 2):
                copy(g["p2_sem"] + k, g["rb"].at[k], g["rb"].at[k], sib).wait_recv()
            chip_sum(g["mine"].at[1], g["rb"], g["mine"].shape[1], (1, 2))

        for cp in do_loads:
            cp.wait()
        piece_matmuls(w_out_g, 0)
        piece_matmuls(w_out_g, 1)
        piece_finish(w_out_g, 0)
        piece_matmuls(w_out_g, 2)
        piece_finish(w_out_g, 1)
        piece_matmuls(w_out_g, 3)
        piece_finish(w_out_g, 2)

        for cp in hn_loads:
            cp.wait()
        piece_matmuls(w_in_g, 0)

        l_pk.wait()
        copy(0, pk_recv, pk_recv, sib).wait_recv()
        pair_sum(pk_mine, pk_recv, pk_send, hp, True)
        for k in range(3):
            start(copy(6 + k, pk_send.at[:, pl.ds(_mo(cj[k] * 128, 128), 128)], pk_rb.at[k], (*chips[k], c)))
        l_pk_own = local(6, pk_mine.at[:, pl.ds(_mo(j * 128, 128), 128)], pk_own)

        piece_matmuls(w_in_g, 1)
        piece_finish(w_in_g, 0)

        l_pk_own.wait()
        for k in range(3):
            copy(6 + k, pk_rb.at[k], pk_rb.at[k], sib).wait_recv()
        chip_sum(pk_own, pk_rb, hp)
        l_pk_out = local(8, pk_own, pk_piece(pk_out, c, j))
        start(copy(15, pk_own, pk_piece(pk_out, c, j), sib))
        for k in range(3):
            start(copy(16 + k, pk_own, pk_piece(pk_out, c, j), (*chips[k], c)))

        piece_matmuls(w_in_g, 2)
        piece_finish(w_in_g, 1)
        piece_matmuls(w_in_g, 3)
        piece_finish(w_in_g, 2)

        piece_finish(w_out_g, 3)
        piece_total(w_out_g)
        l_wo_out = local(9, wo_mine.at[1], cols_half(wo_out, c, 512))
        start(copy(22, wo_mine.at[1], cols_half(wo_out, c, 512), sib))

        for k in range(3):
            reg = pk_piece(pk_out, c, cj[k])
            copy(16 + k, reg, reg, sib).wait_recv()
            start(copy(19 + k, reg, reg, sib))

        piece_finish(w_in_g, 3)
        piece_total(w_in_g)
        l_wi_out = local(10, wi_mine.at[1], rows_half(wi_out, c, 512))
        start(copy(23, wi_mine.at[1], rows_half(wi_out, c, 512), sib))

        reg = pk_piece(pk_out, 1 - c, j)
        copy(15, reg, reg, sib).wait_recv()
        for k in range(3):
            reg = pk_piece(pk_out, 1 - c, cj[k])
            copy(19 + k, reg, reg, sib).wait_recv()
        reg = cols_half(wo_out, 1 - c, 512)
        copy(22, reg, reg, sib).wait_recv()
        reg = rows_half(wi_out, 1 - c, 512)
        copy(23, reg, reg, sib).wait_recv()
        for cp in sends + w_in_g["p1"][2:] + w_out_g["p1"][2:]:
            cp.wait_send()
        for cp in (l_pk_out, l_wo_out, l_wi_out):
            cp.wait()

    return pl.pallas_call(
        body, name="grads_reduce",
        out_shape=(jax.ShapeDtypeStruct((D, wi_w), F32), jax.ShapeDtypeStruct((512, D), F32),
                   jax.ShapeDtypeStruct(pack.shape, F32)),
        in_specs=[ANY] * (6 + n_dz), out_specs=(ANY,) * 3,
        scratch_shapes=[
            pltpu.VMEM((lt, 512), BF16), pltpu.VMEM((lt, 512), BF16), pltpu.VMEM((lt, wi_w), BF16),
            pltpu.VMEM((2, 512, wi_w), BF16), pltpu.VMEM((2, 512, wi_w), F32), pltpu.VMEM((4, 512, wi_w), BF16),
            pltpu.VMEM((3, 512, wi_w), BF16), pltpu.VMEM((3, 512, wi_w), BF16),
            pltpu.VMEM((lx, 512), BF16), pltpu.VMEM((lx, 512), BF16), pltpu.VMEM((lx, 512), BF16),
            pltpu.VMEM((2, 512, 512), BF16), pltpu.VMEM((2, 512, 512), F32), pltpu.VMEM((4, 512, 512), BF16),
            pltpu.VMEM((3, 512, 512), BF16), pltpu.VMEM((3, 512, 512), BF16),
            pltpu.VMEM((hp, 512), F32), pltpu.VMEM((hp, 512), F32), pltpu.VMEM((hp, 512), BF16),
            pltpu.VMEM((3, hp, 128), BF16), pltpu.VMEM((hp, 128), F32),
            pltpu.SemaphoreType.DMA((27,)), pltpu.SemaphoreType.DMA((27,)), pltpu.SemaphoreType.DMA((15,))],
        compiler_params=_cp(vmem_mb=56),
    )(hn, *dzs, hn_c, dxa_c, y, do, pack)


def _ada_bwd(c_all, dmx_all_j, dmc_j, dmx_all, dmc, c_ctx, ada_w_full):
    def body(c_ref, dmxj_ref, dmcj_ref, dmx_ref, dmc_ref, cc_ref, w_ref, gw_ref, gb_ref, gc_ref, lhs, rhs, dm8):
        lhs[...] = jnp.zeros_like(lhs)
        rhs[...] = jnp.zeros_like(rhs)
        cv = c_ref[...]
        lhs[0:8, :] = cv * _sigmoid(cv)
        cc = cc_ref[...]
        a_c, da_c = _silu_and_grad(cc)
        lhs[8:9, :] = a_c
        rhs[0:8, :] = dmxj_ref[...]
        rhs[8:9, :] = dmcj_ref[...]
        gw_ref[...] = _dot_tn(lhs[...].astype(BF16), rhs[...].astype(BF16))
        gb_ref[...] = jnp.sum(dmx_ref[...], axis=0, keepdims=True) + dmc_ref[...]
        dm8[...] = jnp.zeros_like(dm8)
        dm8[0:1, :] = dmc_ref[...]
        da = _dot_nt(dm8[...].astype(BF16), w_ref[...])
        gc_ref[...] = da[0:1, :] * da_c

    return pl.pallas_call(
        body, name="ada_bwd",
        out_shape=(jax.ShapeDtypeStruct((D, 768), F32), jax.ShapeDtypeStruct((1, 3 * D), F32),
                   jax.ShapeDtypeStruct((1, D), F32)),
        in_specs=[VMEM] * 7, out_specs=(VMEM,) * 3,
        scratch_shapes=[pltpu.VMEM((16, D), F32), pltpu.VMEM((16, 768), F32), pltpu.VMEM((8, 3 * D), F32)],
        compiler_params=_cp(vmem_mb=32),
    )(c_all, dmx_all_j, dmc_j, dmx_all, dmc, c_ctx, ada_w_full)


def _proj(x, mods, mrow, norm_g, w_full, nk, name, sgu=None):
    lx = x.shape[0]
    n = lx // T
    order = [2, 3, 0, 1, 4] if sgu is not None else list(range(nk))

    def body(x_ref, sh_ref, sc_ref, ng_ref, *rest):
        w_refs, rest = rest[:nk], rest[nk:]
        if sgu is not None:
            g_ref, b_ref, sw_ref, bt_ref = rest[:4]
            rest = rest[4:]
        hn_ref, z_refs = rest[0], rest[1:1 + nk]
        xv = x_ref[...]
        r = lax.rsqrt(jnp.mean(xv * xv, axis=-1, keepdims=True) + NORM_EPS)
        hn = (xv * r) * ng_ref[...] * (1.0 + sc_ref[mrow:mrow + 1, :]) + sh_ref[mrow:mrow + 1, :]
        hb = hn.astype(BF16)
        hn_ref[...] = hb
        for k in order[:2]:
            z_refs[k][...] = _dot(hb, w_refs[k][...])
        if sgu is not None:
            ys_ref, mixed_s = rest[1 + nk], rest[2 + nk]
            for ch in range(T // HD):
                rows = slice(HD * ch, HD * ch + HD)
                ug = _sgu_parts(z_refs[2][rows, :], z_refs[3][rows, :], g_ref[...], b_ref[...], sw_ref, bt_ref,
                                mixed_s)[0]
                ys_ref[rows, :] = ug * mixed_s[...]
        for k in order[2:]:
            z_refs[k][...] = _dot(hb, w_refs[k][...])

    row = pl.BlockSpec((T, D), lambda i: (i, 0))
    vec = pl.BlockSpec((1, D), lambda i: (0, 0))
    in_specs = [row, pl.BlockSpec((8, D), lambda i: (0, 0)), pl.BlockSpec((8, D), lambda i: (0, 1)), vec]
    in_specs += [pl.BlockSpec((D, D), lambda i, k=k: (0, k)) for k in range(nk)]
    args = [x, mods, mods, norm_g] + [w_full] * nk
    out_shape = (jax.ShapeDtypeStruct((lx, D), BF16),) + tuple(jax.ShapeDtypeStruct((lx, D), F32) for _ in range(nk))
    scratch = []
    if sgu is not None:
        in_specs += [vec, vec, pl.BlockSpec((NH, HD, HD), lambda i: (0, 0, 0)), pl.BlockSpec((HD, NH), lambda i: (0, 0))]
        args += list(sgu)
        out_shape += (jax.ShapeDtypeStruct((lx, D), F32),)
        scratch = [pltpu.VMEM((HD, D), F32)]
    return pl.pallas_call(
        body, name=name, grid=(n,), out_shape=out_shape, in_specs=in_specs, out_specs=(row,) * len(out_shape),
        scratch_shapes=scratch, compiler_params=_cp(1, vmem_mb=56),
    )(*args)


def _halo_specs(lx):
    last = lx // 8 - 1
    return [pl.BlockSpec((T, D), lambda i: (i, 0)),
            pl.BlockSpec((8, D), lambda i: (jnp.maximum(i * (T // 8) - 1, 0), 0)),
            pl.BlockSpec((8, D), lambda i: (jnp.minimum((i + 1) * (T // 8), last), 0))]


def _zhalo_specs(lx):
    last = lx // 8 - 1
    return [pl.BlockSpec((T * NH, HD), lambda i: (i, 0)),
            pl.BlockSpec((8 * NH, HD), lambda i: (jnp.maximum(i * (T // 8) - 1, 0), 0)),
            pl.BlockSpec((8 * NH, HD), lambda i: (jnp.minimum((i + 1) * (T // 8), last), 0))]


ZT = pl.BlockSpec((T * NH, HD), lambda i: (i, 0))
CONV_CHUNK = 32


SCAN_SUB = 4


def _scan_tile(chains, post, carry_ref):
    blk = T // SCAN_SUB

    def step(k, state):
        new = []
        for ci, (a_ref, x_ref, o_ref, q_ref, reverse) in enumerate(chains):
            for q in range(SCAN_SUB):
                s, p = state[ci * SCAN_SUB + q]
                t = (q + 1) * blk - 1 - k if reverse else q * blk + k
                r = pl.ds(_mo(t * NH, NH), NH)
                a = a_ref[r, :]
                if post:
                    o = x_ref[r, :] + s
                    o_ref[r, :] = o
                    q_ref[r, :] = p
                    new.append((a * o, a * p))
                else:
                    o = a * s + x_ref[r, :]
                    p = a * p
                    o_ref[r, :] = o
                    q_ref[r, :] = p
                    new.append((o, p))
        return tuple(new)

    zero = jnp.zeros((NH, HD), F32)
    one = jnp.ones((NH, HD), F32)
    final = lax.fori_loop(0, blk, step, tuple((zero, one) for _ in range(len(chains) * SCAN_SUB)), unroll=2)
    for ci, (a_ref, x_ref, o_ref, q_ref, reverse) in enumerate(chains):
        carry = carry_ref[ci]
        for q in (range(SCAN_SUB - 1, -1, -1) if reverse else range(SCAN_SUB)):
            rows = pl.ds(q * blk * NH, blk * NH)
            fixed = o_ref[rows, :].reshape(blk, NH, HD) + q_ref[rows, :].reshape(blk, NH, HD) * carry[None]
            o_ref[rows, :] = fixed.reshape(blk * NH, HD)
            s_loc, p_loc = final[ci * SCAN_SUB + q]
            carry = s_loc + p_loc * carry
        carry_ref[ci] = carry


def _lru_fwd(xa, conv_wz, conv_bz, wcat, bcat, lamcat, s_f, s_b, name):
    lx = xa.shape[0]
    n = lx // T

    def body(xm_u, xp_u, xn_u, xm_d, xp_d, xn_d, cw, cb, w_ref, b_ref, lam_ref, su0, sd0,
             xaz_o, xcz_o, af_o, ab_o, hf_o, hb_o, gf_o, gb_o, fu, fd, pad, xc_d, x_u, x_d, q_u, q_d, carry):
        i = pl.program_id(0)

        @pl.when(i == 0)
        def _():
            carry[0] = su0[...]
            carry[1] = sd0[...]

        def conv_gates(xm, xp, xn, tile, d, xc_ref, a_ref, x_ref, xaz_ref, g_ref):
            pmask = jnp.where(tile == 0, 0.0, 1.0)
            nmask = jnp.where(tile == n - 1, 0.0, 1.0)
            for h in range(NH):
                cols = slice(HD * h, HD * h + HD)
                pad[_zrows(h, 8), :] = xp[:, cols] * pmask
                pad[pl.ds(8 * NH + h, T, stride=NH), :] = xm[:, cols]
                pad[pl.ds((T + 8) * NH + h, 8, stride=NH), :] = xn[:, cols] * nmask
            if xaz_ref is not None:
                xaz_ref[...] = pad[pl.ds(8 * NH, T * NH), :]

            def conv_chunk(ci, c_):
                base = pl.multiple_of(ci * (CONV_CHUNK * NH), CONV_CHUNK * NH)
                acc = None
                for k in range(4):
                    sl = pad[pl.ds(base + (7 + k) * NH, CONV_CHUNK * NH), :].reshape(CONV_CHUNK, NH, HD)
                    term = sl * cw[k][None]
                    acc = term if acc is None else acc + term
                acc = acc + cb[...][None]
                xc_ref[pl.ds(base, CONV_CHUNK * NH), :] = acc.reshape(CONV_CHUNK * NH, HD)
                return c_
            lax.fori_loop(0, T // CONV_CHUNK, conv_chunk, 0)

            for h in range(NH):
                xch = xc_ref[_zrows(h, T), :]
                pre = _dot(xch.astype(BF16), w_ref[h, :, 256 * d:256 * d + 256]) + b_ref[h:h + 1, 256 * d:256 * d + 256]
                r, gi, _, _, a, mult = _lru_gate(pre, lam_ref[h:h + 1, :], d, 0)
                a_ref[_zrows(h, T), :] = a
                x_ref[_zrows(h, T), :] = mult * gi * xch
                for q, val in enumerate((r, gi, mult)):
                    g_ref[:, q * D + HD * h:q * D + HD * h + HD] = val

        conv_gates(xm_u, xp_u, xn_u, i, 0, xcz_o, af_o, x_u, xaz_o, gf_o)
        conv_gates(xm_d, xp_d, xn_d, n - 1 - i, 1, xc_d, ab_o, x_d, None, gb_o)

        _scan_tile([(af_o, x_u, hf_o, q_u, False), (ab_o, x_d, hb_o, q_d, True)], False, carry)
        fu[...] = carry[0]
        fd[...] = carry[1]

    full = lambda shape: pl.BlockSpec(shape, lambda i: (0,) * len(shape))
    last = lx // 8 - 1
    rev = lambda i: n - 1 - i
    halo_dn = [pl.BlockSpec((T, D), lambda i: (rev(i), 0)),
               pl.BlockSpec((8, D), lambda i: (jnp.maximum(rev(i) * (T // 8) - 1, 0), 0)),
               pl.BlockSpec((8, D), lambda i: (jnp.minimum((rev(i) + 1) * (T // 8), last), 0))]
    st = full((NH, HD))
    in_specs = _halo_specs(lx) + halo_dn + [full((4, NH, HD)), st, full((NH, HD, 4 * HD)), full((NH, 4 * HD)),
                                            full((NH, 2 * HD)), st, st]
    dn = pl.BlockSpec((T * NH, HD), lambda i: (rev(i), 0))
    zs = jax.ShapeDtypeStruct((lx * NH, HD), F32)
    ss = jax.ShapeDtypeStruct((NH, HD), F32)
    zbuf = pltpu.VMEM((T * NH, HD), F32)
    gs = jax.ShapeDtypeStruct((lx, 3 * D), F32)
    g_up = pl.BlockSpec((T, 3 * D), lambda i: (i, 0))
    g_dn = pl.BlockSpec((T, 3 * D), lambda i: (rev(i), 0))
    return pl.pallas_call(
        body, name=name, grid=(n,), out_shape=(zs,) * 6 + (gs, gs, ss, ss), in_specs=in_specs,
        out_specs=(ZT, ZT, ZT, dn, ZT, dn, g_up, g_dn, st, st),
        scratch_shapes=[pltpu.VMEM(((T + 16) * NH, HD), F32), zbuf, zbuf, zbuf, zbuf, zbuf,
                        pltpu.VMEM((2, NH, HD), F32)],
        compiler_params=_cp(1, vmem_mb=48),
    )(xa, xa, xa, xa, xa, xa, conv_wz, conv_bz, wcat, bcat, lamcat, s_f, s_b)


def _scan_adjoint(a_up, x_up, a_dn, x_dn, name):
    lx = a_up.shape[0] // NH
    n = lx // T

    def body(au, xu, ad, xd, ou, od, fu, fd, q_u, q_d, carry):
        @pl.when(pl.program_id(0) == 0)
        def _():
            carry[...] = jnp.zeros_like(carry)

        _scan_tile([(au, xu, ou, q_u, False), (ad, xd, od, q_d, True)], True, carry)
        fu[...] = carry[0]
        fd[...] = carry[1]

    up = pl.BlockSpec((T * NH, HD), lambda i: (i, 0))
    dn = pl.BlockSpec((T * NH, HD), lambda i: (n - 1 - i, 0))
    st = pl.BlockSpec((NH, HD), lambda i: (0, 0))
    zs = jax.ShapeDtypeStruct((lx * NH, HD), F32)
    ss = jax.ShapeDtypeStruct((NH, HD), F32)
    zbuf = pltpu.VMEM((T * NH, HD), F32)
    return pl.pallas_call(
        body, name=name, grid=(n,), out_shape=(zs, zs, ss, ss), in_specs=[up, up, dn, dn],
        out_specs=(up, dn, st, st), scratch_shapes=[zbuf, zbuf, pltpu.VMEM((2, NH, HD), F32)],
        compiler_params=_cp(1, vmem_mb=48),
    )(a_up, x_up, a_dn, x_dn)


def _sgu_parts(u, v, lng, lnb, w_ref, bt_ref, mixed_s):
    ug, dug = _gelu_and_grad(u)
    vg, dvg = _gelu_and_grad(v)
    mu = jnp.mean(vg, axis=-1, keepdims=True)
    vc = vg - mu
    rstd = lax.rsqrt(jnp.mean(vc * vc, axis=-1, keepdims=True) + LN_EPS)
    vh = vc * rstd
    vn = (vh * lng + lnb).astype(BF16)
    for g in range(NH):
        cols = slice(HD * g, HD * g + HD)
        mixed_s[:, cols] = _dot(w_ref[g], vn[:, cols]) + bt_ref[:, g:g + 1]
    return ug, dug, dvg, rstd, vh, vn


def _sgu_bwd_chunk(u, v, dys_v, lng, lnb, w_ref, bt_ref, mixed_s, dvn_s, dw_ref, db_ref, dg_ref, dbl_ref):
    ug, dug, dvg, rstd, vh, vn = _sgu_parts(u, v, lng, lnb, w_ref, bt_ref, mixed_s)
    du = (dys_v * mixed_s[...] * dug).astype(BF16)
    dmix = dys_v * ug
    ones = jnp.ones((8, HD), BF16)
    for g in range(NH):
        cols = slice(HD * g, HD * g + HD)
        dm = dmix[:, cols]
        hi = dm.astype(BF16)
        lo = (dm - hi.astype(F32)).astype(BF16)
        dw_ref[g] += _dot_nt(hi, vn[:, cols])
        db_ref[g:g + 1, :] += (_dot_nt(ones, hi) + _dot_nt(ones, lo))[0:1, :]
        dvn_s[:, cols] = _dot_tn(w_ref[g], hi)
    dvn = dvn_s[...]
    dg_ref[...] += jnp.sum(dvn * vh, axis=0, keepdims=True)
    dbl_ref[...] += jnp.sum(dvn, axis=0, keepdims=True)
    dvh = dvn * lng
    dvg_in = rstd * (dvh - jnp.mean(dvh, axis=-1, keepdims=True) - vh * jnp.mean(dvh * vh, axis=-1, keepdims=True))
    return du, (dvg_in * dvg).astype(BF16)


def _out_fwd_bwd(hf_z, hb_z, ga, gb, ys, x, tgt, mods, final_g, w_out_full):
    lx = x.shape[0]
    n = lx // T

    def body(hf_ref, hb_ref, ga_ref, gb_ref, ys_ref, x_ref, t_ref, gx_ref, fg_ref, w_ref,
             loss_ref, dfg_ref, dgx_ref, dxn_ref, y_ref, do_ref, dga_ref, dgb_ref, dyl_ref, dys_ref, yl_s):
        i = pl.program_id(0)

        @pl.when(i == 0)
        def _():
            loss_ref[...] = jnp.zeros_like(loss_ref)
            dfg_ref[...] = jnp.zeros_like(dfg_ref)
            dgx_ref[...] = jnp.zeros_like(dgx_ref)

        for h in range(NH):
            yl_s[:, HD * h:HD * h + HD] = hf_ref[_zrows(h, T), :] + hb_ref[_zrows(h, T), :]
        yl = yl_s[...]
        gav = ga_ref[...]
        gbv = gb_ref[...]
        sa, dsa = _silu_and_grad(gav)
        sb, dsb = _silu_and_grad(gbv)
        ysv = ys_ref[...]
        y_ref[:, 0:D] = (yl * sa).astype(BF16)
        y_ref[:, D:2 * D] = (ysv * sb).astype(BF16)
        o = _dot(y_ref[...], w_ref[...])
        gx = gx_ref[0:1, :]
        xnew = x_ref[...] + gx * o
        r2 = lax.rsqrt(jnp.mean(xnew * xnew, axis=-1, keepdims=True) + NORM_EPS)
        xh = xnew * r2
        fg = fg_ref[...]
        err = xh * fg - t_ref[...]
        loss_ref[...] += 0.5 * jnp.sum(jnp.mean(err * err, axis=-1, keepdims=True), axis=0, keepdims=True)
        dout = err * (1.0 / D)
        dfg_ref[...] += jnp.sum(dout * xh, axis=0, keepdims=True)
        dxh = dout * fg
        dxn = r2 * (dxh - xh * jnp.mean(dxh * xh, axis=-1, keepdims=True))
        dxn_ref[...] = dxn
        dgx_ref[...] += jnp.sum(dxn * o, axis=0, keepdims=True)
        do = (dxn * gx).astype(BF16)
        do_ref[...] = do
        dy = _dot_nt(do, w_ref[...])
        dy1 = dy[:, 0:D]
        dy2 = dy[:, D:2 * D]
        dga_ref[...] = (dy1 * yl * dsa).astype(BF16)
        dgb_ref[...] = (dy2 * ysv * dsb).astype(BF16)
        dys_ref[...] = dy2 * sb
        yl_s[...] = dy1 * sa
        for h in range(NH):
            dyl_ref[_zrows(h, T), :] = yl_s[:, HD * h:HD * h + HD]

    row = pl.BlockSpec((T, D), lambda i: (i, 0))
    vec = pl.BlockSpec((1, D), lambda i: (0, 0))
    in_specs = [ZT, ZT, row, row, row, row, row, pl.BlockSpec((8, D), lambda i: (0, 2)), vec,
                pl.BlockSpec((2 * D, D), lambda i: (0, 0))]
    out_shape = (jax.ShapeDtypeStruct((1, 1), F32), jax.ShapeDtypeStruct((1, D), F32), jax.ShapeDtypeStruct((1, D), F32),
                 jax.ShapeDtypeStruct((lx, D), F32), jax.ShapeDtypeStruct((lx, 2 * D), BF16),
                 jax.ShapeDtypeStruct((lx, D), BF16), jax.ShapeDtypeStruct((lx, D), BF16),
                 jax.ShapeDtypeStruct((lx, D), BF16), jax.ShapeDtypeStruct((lx * NH, HD), F32),
                 jax.ShapeDtypeStruct((lx, D), F32))
    out_specs = (pl.BlockSpec((1, 1), lambda i: (0, 0)), vec, vec, row, pl.BlockSpec((T, 2 * D), lambda i: (i, 0)),
                 row, row, row, ZT, row)
    return pl.pallas_call(
        body, name="out_fwd_bwd", grid=(n,), out_shape=out_shape, in_specs=in_specs, out_specs=out_specs,
        scratch_shapes=[pltpu.VMEM((T, D), F32)],
        compiler_params=_cp(1, vmem_mb=56),
    )(hf_z, hb_z, ga, gb, ys, x, tgt, mods, final_g, w_out_full)


def _lru_gates_bwd(xc_z, lf_z, lb_z, hf_z, hb_z, af_z, ab_z, gf, gb, s_f, s_b, wcat, lamcat, dw0, db0, dl0, name):
    lx = xc_z.shape[0] // NH
    n = lx // T

    def body(xc_ref, lf_ref, lb_ref, hf_ref, hfp_ref, hb_ref, hbn_ref, af_ref, ab_ref, gf_ref, gb_ref, sf_ref, sb_ref,
             w_ref, lam_ref, dw0_ref, db0_ref, dl0_ref, dxc_ref, dw_ref, db_ref, dl_ref, dpre_s, pf_s, pb_s):
        i = pl.program_id(0)

        @pl.when(i == 0)
        def _():
            dw_ref[...] = dw0_ref[...]
            db_ref[...] = db0_ref[...]
            dl_ref[...] = dl0_ref[...]

        pf_s[pl.ds(0, NH), :] = jnp.where(i == 0, sf_ref[...], hfp_ref[pl.ds(7 * NH, NH), :])
        pf_s[pl.ds(NH, T * NH), :] = hf_ref[...]
        pb_s[pl.ds(0, T * NH), :] = hb_ref[...]
        pb_s[pl.ds(T * NH, NH), :] = jnp.where(i == n - 1, sb_ref[...], hbn_ref[pl.ds(0, NH), :])
        lam_refs = (lf_ref, lb_ref)
        prev = ((pf_s, 0), (pb_s, NH))
        a_refs = (af_ref, ab_ref)
        g_refs = (gf_ref, gb_ref)
        for h in range(NH):
            xch = xc_ref[_zrows(h, T), :]
            xcb = xch.astype(BF16)
            dxc = jnp.zeros((T, HD), F32)
            for d in range(2):
                r, gi, mult = (g_refs[d][:, q * D + HD * h:q * D + HD * h + HD] for q in range(3))
                a = a_refs[d][_zrows(h, T), :]
                lam = lam_ref[h:h + 1, HD * d:HD * d + HD]
                sp = _softplus(-lam)
                du = lam_refs[d][_zrows(h, T), :]
                da = du * prev[d][0][pl.ds(prev[d][1] + h, T, stride=NH), :]
                dgi = du * mult * xch
                dxc = dxc + du * mult * gi
                dmult = du * gi * xch
                dla = da * a - dmult * (a * a) / mult
                dr = dla * ((-LRU_C) * sp)
                dsp = jnp.sum(dla * ((-LRU_C) * r), axis=0, keepdims=True)
                dl_ref[h:h + 1, HD * d:HD * d + HD] += dsp * (-_sigmoid(-lam))
                dpre_s[:, 256 * d:256 * d + HD] = dr * r * (1.0 - r)
                dpre_s[:, 256 * d + HD:256 * d + 2 * HD] = dgi * gi * (1.0 - gi)
            dpre = dpre_s[...]
            dpb = dpre.astype(BF16)
            dw_ref[h] += _dot_tn(xcb, dpb)
            db_ref[h:h + 1, :] += jnp.sum(dpre, axis=0, keepdims=True)
            dxc_ref[_zrows(h, T), :] = dxc + _dot_nt(dpb, w_ref[h])

    full = lambda shape: pl.BlockSpec(shape, lambda i: (0,) * len(shape))
    wsp, bsp, lsp = full((NH, HD, 4 * HD)), full((NH, 4 * HD)), full((NH, 2 * HD))
    st = full((NH, HD))
    zh = _zhalo_specs(lx)
    gsp = pl.BlockSpec((T, 3 * D), lambda i: (i, 0))
    return pl.pallas_call(
        body, name=name, grid=(n,),
        out_shape=(jax.ShapeDtypeStruct((lx * NH, HD), F32), jax.ShapeDtypeStruct((NH, HD, 4 * HD), F32),
                   jax.ShapeDtypeStruct((NH, 4 * HD), F32), jax.ShapeDtypeStruct((NH, 2 * HD), F32)),
        in_specs=[ZT] * 3 + zh[0:2] + [zh[0], zh[2], ZT, ZT, gsp, gsp, st, st, wsp, lsp, wsp, bsp, lsp],
        out_specs=(ZT, wsp, bsp, lsp),
        scratch_shapes=[pltpu.VMEM((T, 4 * HD), F32), pltpu.VMEM(((T + 1) * NH, HD), F32),
                        pltpu.VMEM(((T + 1) * NH, HD), F32)],
        compiler_params=_cp(1, vmem_mb=56),
    )(xc_z, lf_z, lb_z, hf_z, hf_z, hb_z, hb_z, af_z, ab_z, gf, gb, s_f, s_b, wcat, lamcat, dw0, db0, dl0)


def _conv_bwd(dxc_z, xa_z, conv_wz, dcw0, dcb0, name):
    lx = dxc_z.shape[0] // NH
    n = lx // T

    def body(dm, dp, dn, xa_ref, cw, dcw0_ref, dcb0_ref, dxa_ref, dcw_ref, dcb_ref, pad, dxa_s):
        i = pl.program_id(0)

        @pl.when(i == 0)
        def _():
            dcw_ref[...] = dcw0_ref[...]
            dcb_ref[...] = dcb0_ref[...]

        pmask = jnp.where(i == 0, 0.0, 1.0)
        nmask = jnp.where(i == n - 1, 0.0, 1.0)
        pad[pl.ds(0, 8 * NH), :] = dp[...] * pmask
        pad[pl.ds(8 * NH, T * NH), :] = dm[...]
        pad[pl.ds((T + 8) * NH, 8 * NH), :] = dn[...] * nmask

        def chunk(ci, carry):
            base = pl.multiple_of(ci * (CONV_CHUNK * NH), CONV_CHUNK * NH)
            xav = xa_ref[pl.ds(base, CONV_CHUNK * NH), :].reshape(CONV_CHUNK, NH, HD)
            acc = None
            for k in range(4):
                sl = pad[pl.ds(base + (9 - k) * NH, CONV_CHUNK * NH), :].reshape(CONV_CHUNK, NH, HD)
                term = sl * cw[k][None]
                acc = term if acc is None else acc + term
                dcw_ref[k] += jnp.sum(sl * xav, axis=0)
                if k == 1:
                    dcb_ref[...] += jnp.sum(sl, axis=0)
            dxa_s[pl.ds(base, CONV_CHUNK * NH), :] = acc.reshape(CONV_CHUNK * NH, HD)
            return carry
        lax.fori_loop(0, T // CONV_CHUNK, chunk, 0)
        for h in range(NH):
            dxa_ref[:, HD * h:HD * h + HD] = dxa_s[_zrows(h, T), :].astype(BF16)

    full = lambda shape: pl.BlockSpec(shape, lambda i: (0,) * len(shape))
    return pl.pallas_call(
        body, name=name, grid=(n,),
        out_shape=(jax.ShapeDtypeStruct((lx, D), BF16), jax.ShapeDtypeStruct((4, NH, HD), F32),
                   jax.ShapeDtypeStruct((NH, HD), F32)),
        in_specs=_zhalo_specs(lx) + [ZT, full((4, NH, HD)), full((4, NH, HD)), full((NH, HD))],
        out_specs=(pl.BlockSpec((T, D), lambda i: (i, 0)), full((4, NH, HD)), full((NH, HD))),
        scratch_shapes=[pltpu.VMEM(((T + 16) * NH, HD), F32), pltpu.VMEM((T * NH, HD), F32)],
        compiler_params=_cp(1, vmem_mb=48),
    )(dxc_z, dxc_z, dxc_z, xa_z, conv_wz, dcw0, dcb0)


def _proj_bwd(dzs, x, dxn, mods, mrow, norm_g, w_full, dng0, name, sgu=None):
    lx = x.shape[0]
    n = lx // T
    nz = len(dzs)
    has_x = dxn is not None
    wks = ([0, 1, 4, 2, 3] if sgu is not None else list(range(nz)))

    def body(*refs):
        it = iter(refs)
        take = lambda m: [next(it) for _ in range(m)]
        dz_refs, w_refs = take(nz), take(len(wks))
        x_ref, sc_ref, ng_ref, dng0_ref = take(4)
        dxn_ref = take(1)[0] if has_x else None
        if sgu is not None:
            u_ref, v_ref, dy_ref, g_ref, b_ref, sw_ref, bt_ref = take(7)
        gx_ref = take(1)[0] if has_x else None
        dng_ref, dsc_ref, dsh_ref = take(3)
        if sgu is not None:
            du_ref, dv_ref, dws_ref, dbs_ref, dlg_ref, dlb_ref, mixed_s, dvn_s = take(8)
        i = pl.program_id(0)

        @pl.when(i == 0)
        def _():
            dng_ref[...] = dng0_ref[...]
            dsc_ref[...] = jnp.zeros_like(dsc_ref)
            dsh_ref[...] = jnp.zeros_like(dsh_ref)
            if sgu is not None:
                for acc in (dws_ref, dbs_ref, dlg_ref, dlb_ref):
                    acc[...] = jnp.zeros_like(acc)

        dhn = _dot_nt(dz_refs[0][...], w_refs[0][...])
        for k in range(1, nz):
            dhn = dhn + _dot_nt(dz_refs[k][...], w_refs[k][...])
        if sgu is not None:
            for ch in range(T // HD):
                rows = slice(HD * ch, HD * ch + HD)
                du, dv = _sgu_bwd_chunk(u_ref[rows, :], v_ref[rows, :], dy_ref[rows, :], g_ref[...], b_ref[...],
                                        sw_ref, bt_ref, mixed_s, dvn_s, dws_ref, dbs_ref, dlg_ref, dlb_ref)
                du_ref[rows, :] = du
                dv_ref[rows, :] = dv
            dhn = dhn + _dot_nt(du_ref[...], w_refs[nz][...]) + _dot_nt(dv_ref[...], w_refs[nz + 1][...])
        xv = x_ref[...]
        r = lax.rsqrt(jnp.mean(xv * xv, axis=-1, keepdims=True) + NORM_EPS)
        xn = xv * r
        ng = ng_ref[...]
        sc1 = 1.0 + sc_ref[mrow:mrow + 1, :]
        t = dhn * xn
        dng_ref[...] += jnp.sum(t * sc1, axis=0, keepdims=True)
        dsc_ref[...] += jnp.sum(t * ng, axis=0, keepdims=True)
        dsh_ref[...] += jnp.sum(dhn, axis=0, keepdims=True)
        if has_x:
            dxh = dhn * (ng * sc1)
            gx_ref[...] = dxn_ref[...] + r * (dxh - xn * jnp.mean(dxh * xn, axis=-1, keepdims=True))

    row = pl.BlockSpec((T, D), lambda i: (i, 0))
    vec = pl.BlockSpec((1, D), lambda i: (0, 0))
    in_specs = [row] * nz + [pl.BlockSpec((D, D), lambda i, k=k: (0, k)) for k in wks]
    in_specs += [row, pl.BlockSpec((8, D), lambda i: (0, 1)), vec, vec]
    args = list(dzs) + [w_full] * len(wks) + [x, mods, norm_g, dng0]
    vs = jax.ShapeDtypeStruct((1, D), F32)
    out_shape, out_specs = (vs, vs, vs), (vec, vec, vec)
    scratch = []
    if has_x:
        in_specs.append(row)
        args.append(dxn)
        out_shape = (jax.ShapeDtypeStruct((lx, D), F32),) + out_shape
        out_specs = (row,) + out_specs
    if sgu is not None:
        wsp = pl.BlockSpec((NH, HD, HD), lambda i: (0, 0, 0))
        bsp = pl.BlockSpec((NH, HD), lambda i: (0, 0))
        in_specs += [row, row, row, vec, vec, wsp, pl.BlockSpec((HD, NH), lambda i: (0, 0))]
        args += list(sgu)
        zb = jax.ShapeDtypeStruct((lx, D), BF16)
        out_shape += (zb, zb, jax.ShapeDtypeStruct((NH, HD, HD), F32), jax.ShapeDtypeStruct((NH, HD), F32), vs, vs)
        out_specs += (row, row, wsp, bsp, vec, vec)
        scratch = [pltpu.VMEM((HD, D), F32), pltpu.VMEM((HD, D), F32)]
    return pl.pallas_call(
        body, name=name, grid=(n,), out_shape=out_shape, in_specs=in_specs, out_specs=out_specs,
        scratch_shapes=scratch, compiler_params=_cp(1, vmem_mb=56),
    )(*args)


def _adam_math(w, g, m, v):
    m = ADAM_B1 * m + (1.0 - ADAM_B1) * g
    v = ADAM_B2 * v + (1.0 - ADAM_B2) * (g * g)
    m_hat = m / (1.0 - ADAM_B1 ** ADAM_STEP)
    v_hat = v / (1.0 - ADAM_B2 ** ADAM_STEP)
    delta = -ADAM_LR * (m_hat / (jnp.sqrt(v_hat) + ADAM_EPS) + ADAM_WD * w)
    return delta, m, v


def _adam_big(w, g, m, v, name):
    rows, cols = w.shape
    tr = 256

    def body(w_ref, g_ref, m_ref, v_ref, d_o, m_o, v_o):
        d, mm, vv = _adam_math(w_ref[...], g_ref[...], m_ref[...], v_ref[...])
        d_o[...] = d
        m_o[...] = mm
        v_o[...] = vv

    blk = pl.BlockSpec((tr, cols), lambda i: (i, 0))
    s = jax.ShapeDtypeStruct((rows, cols), F32)
    return pl.pallas_call(
        body, name=name, grid=(rows // tr,), out_shape=(s, s, s), in_specs=[blk] * 4, out_specs=(blk,) * 3,
        compiler_params=_cp(1, vmem_mb=48),
    )(w, g, m, v)


def _adam_small(items):
    ni = len(items)

    def body(*refs):
        ins, outs = refs[:4 * ni], refs[4 * ni:7 * ni]
        bufs_in, bufs_out = refs[7 * ni:11 * ni], refs[11 * ni:14 * ni]
        sem_in, sem_out = refs[14 * ni], refs[14 * ni + 1]
        loads = [pltpu.make_async_copy(ins[q], bufs_in[q], sem_in.at[q]) for q in range(4 * ni)]
        for cp in loads:
            cp.start()
        stores = []
        for k in range(ni):
            for q in range(4):
                loads[4 * k + q].wait()
            w_b, g_b, m_b, v_b = bufs_in[4 * k:4 * k + 4]
            res = _adam_math(w_b[...], g_b[...], m_b[...], v_b[...])
            for q in range(3):
                bufs_out[3 * k + q][...] = res[q]
                cp = pltpu.make_async_copy(bufs_out[3 * k + q], outs[3 * k + q], sem_out.at[3 * k + q])
                cp.start()
                stores.append(cp)
        for cp in stores:
            cp.wait()

    flat = [a for it in items for a in it]
    out_shape = tuple(jax.ShapeDtypeStruct(it[0].shape, F32) for it in items for _ in range(3))
    scratch = [pltpu.VMEM(a.shape, F32) for a in flat] + [pltpu.VMEM(s.shape, F32) for s in out_shape]
    scratch += [pltpu.SemaphoreType.DMA((4 * ni,)), pltpu.SemaphoreType.DMA((3 * ni,))]
    res = pl.pallas_call(
        body, name="adam_small", out_shape=out_shape, in_specs=[HBM] * (4 * ni), out_specs=(HBM,) * (3 * ni),
        scratch_shapes=scratch, compiler_params=_cp(vmem_mb=40),
    )(*flat)
    return [tuple(res[3 * k:3 * k + 3]) for k in range(ni)]


def kernel(x, c, ctx, c_ctx, ada_w, ada_b, norm_g, w_in, conv_w, conv_b, lru_wa, lru_ba, lru_wx, lru_bx, lru_lambda, sgu_ln_g, sgu_ln_b, sgu_w, sgu_b, w_out, final_g, loss_target, m_c_ctx, m_ada_w, m_ada_b, m_norm_g, m_w_in, m_conv_w, m_conv_b, m_lru_wa, m_lru_ba, m_lru_wx, m_lru_bx, m_lru_lambda, m_sgu_ln_g, m_sgu_ln_b, m_sgu_w, m_sgu_b, m_w_out, m_final_g, v_c_ctx, v_ada_w, v_ada_b, v_norm_g, v_w_in, v_conv_w, v_conv_b, v_lru_wa, v_lru_ba, v_lru_wx, v_lru_bx, v_lru_lambda, v_sgu_ln_g, v_sgu_ln_b, v_sgu_w, v_sgu_b, v_w_out, v_final_g):
    ix, iy, ic = lax.axis_index("x"), lax.axis_index("y"), lax.axis_index("c")
    chip = 2 * ix + iy
    dev = 2 * chip + ic
    lx = x.shape[1]
    lc = ctx.shape[1]

    smalls = jnp.concatenate([conv_w[0], lru_lambda[0], jnp.zeros((10, 256), F32)], axis=0)
    c_ctx2 = c_ctx.reshape(1, D)
    ada_b_j = lax.dynamic_slice(ada_b, (0, 768 * chip), (1, 768))
    mods, c_slots, sm_all, w_in_full, wo_land, ada_land = _gather_in(c, c_ctx2, ada_w[0], ada_b_j, w_in[0], w_out[0],
                                                                     smalls)
    wo_ss, wo_rs, ada_ss, ada_rs, wo_land, ada_land, token = _late_gather_start(wo_land, ada_land)
    mods = mods + token[0:1, 0:1]
    sm3 = sm_all.reshape(NCHIP, 16, 256)
    conv_w_full = sm3[:, 0:4, :].transpose(1, 0, 2).reshape(4, D)
    lam_full = sm3[:, 4:6, :].transpose(1, 0, 2).reshape(2, D)
    conv_wz = conv_w_full.reshape(4, NH, HD)
    conv_bz = conv_b.reshape(NH, HD)
    lamcat = lam_full.reshape(2, NH, HD).transpose(1, 0, 2).reshape(NH, 2 * HD)
    wa, wx, ba, bx = lru_wa[0], lru_wx[0], lru_ba[0], lru_bx[0]
    wcat = jnp.concatenate([wa[0], wx[0], wa[1], wx[1]], axis=-1).astype(BF16)
    bcat = jnp.concatenate([ba[0], bx[0], ba[1], bx[1]], axis=-1)
    sgu_wb = sgu_w[0].astype(BF16)
    sgu_bt = sgu_b[0].T
    final_g2 = final_g.reshape(1, D)

    zero_s = jnp.zeros((NH, HD), F32)
    hn_c, xa_c = _proj(ctx[0], mods, 1, norm_g, w_in_full, 1, "proj_ctx")
    xaz_c, xcz_c, af_c, ab_c, hf_c, hb_c, gf_c, gb_c, hf0, hb0 = _lru_fwd(xa_c, conv_wz, conv_bz, wcat, bcat, lamcat,
                                                                           zero_s, zero_s, "lru_fwd_ctx")

    hn, xa, ga, u, v, gb, ys = _proj(x[0], mods, 0, norm_g, w_in_full, 5, "proj",
                                     (sgu_ln_g, sgu_ln_b, sgu_wb, sgu_bt))
    xaz, xcz, af, ab, hf, hb, gf, gb_l, _, _ = _lru_fwd(xa, conv_wz, conv_bz, wcat, bcat, lamcat, hf0, hb0, "lru_fwd")

    w_out_full = _late_gather_wait(wo_land, wo_ss, wo_rs, "w_out", hf, "late_gather_wait_w_out")
    (loss_part, dfg, dgx, dxn, y, do, dga, dgb, dyl_z, dys) = _out_fwd_bwd(
        hf, hb, ga, gb, ys, x[0], loss_target[0], mods, final_g2, w_out_full)

    lb, lf, dh0b, dh0f = _scan_adjoint(ab, dyl_z, af, dyl_z, "scan_adj")
    zw = jnp.zeros((NH, HD, 4 * HD), F32)
    zb = jnp.zeros((NH, 4 * HD), F32)
    zl = jnp.zeros((NH, 2 * HD), F32)
    dxc_z, dwc, dbc, dlc = _lru_gates_bwd(xcz, lf, lb, hf, hb, af, ab, gf, gb_l, hf0, hb0, wcat, lamcat, zw, zb, zl,
                                          "lru_gates_bwd")
    dxa, dcw, dcb = _conv_bwd(dxc_z, xaz, conv_wz, jnp.zeros((4, NH, HD), F32), zero_s, "conv_bwd")

    zc = jnp.zeros((lc * NH, HD), F32)
    dhf_c = lax.dynamic_update_slice(zc, dh0f, ((lc - 1) * NH, 0))
    dhb_c = lax.dynamic_update_slice(zc, dh0b, (0, 0))
    lb_c, lf_c, _, _ = _scan_adjoint(ab_c, dhb_c, af_c, dhf_c, "scan_adj_ctx")
    dxc_zc, dwc, dbc, dlc = _lru_gates_bwd(xcz_c, lf_c, lb_c, hf_c, hb_c, af_c, ab_c, gf_c, gb_c, zero_s, zero_s,
                                           wcat, lamcat, dwc, dbc, dlc, "lru_gates_bwd_ctx")
    dxa_c, dcw, dcb = _conv_bwd(dxc_zc, xaz_c, conv_wz, dcw, dcb, "conv_bwd_ctx")

    grad_x, dng, dsc_x, dsh_x, du, dv, d_sgu_w, d_sgu_b, d_ln_g, d_ln_b = _proj_bwd(
        [dxa, dga, dgb], x[0], dxn, mods, 0, norm_g, w_in_full, jnp.zeros((1, D), F32), "proj_bwd",
        (u, v, dys, sgu_ln_g, sgu_ln_b, sgu_wb, sgu_bt))
    dzs = [dxa, dga, du, dv, dgb]
    dng, dsc_c, dsh_c = _proj_bwd([dxa_c], ctx[0], None, mods, 1, norm_g, w_in_full, dng, "proj_bwd_ctx")

    dmx = jnp.concatenate([dsh_x, dsc_x, dgx], axis=0)
    dmc = jnp.concatenate([dsh_c, dsc_c, jnp.zeros((1, D), F32)], axis=0)
    lp = loss_part[0, 0]
    lp1 = lax.reduce_precision(lp, 8, 7)
    lp2 = lax.reduce_precision(lp - lp1, 8, 7)
    lp3 = lax.reduce_precision(lp - lp1 - lp2, 8, 7)
    loss_row = jnp.pad(jnp.stack([lp1, lp2, lp3]).reshape(1, 3), ((0, 0), (0, D - 3)))
    slot = jnp.concatenate([dmx, loss_row], axis=0)
    slots = lax.dynamic_update_slice(jnp.zeros((32, D), F32), slot, (4 * dev, 0))
    vecs = jnp.concatenate([dfg, dng, dcb.reshape(1, D), d_ln_g, d_ln_b, dcw.reshape(4, D), dmc,
                            jnp.zeros((4, D), F32), slots], axis=0)
    d_sgu_w4 = d_sgu_w.reshape(4, 256, HD).transpose(1, 0, 2).reshape(256, 4 * HD)
    pad8 = lambda a: jnp.pad(a, ((0, 8 - a.shape[0]), (0, 4 * HD - a.shape[1])))
    pack = jnp.concatenate([dwc.reshape(NH * HD, 4 * HD), pad8(dbc), pad8(dlc), d_sgu_w4, pad8(d_sgu_b),
                            vecs.reshape(96, 4 * HD), jnp.zeros((8, 4 * HD), F32)], axis=0)
    g_w_in, g_w_out, tot = _grads_reduce(hn, dzs, hn_c, dxa_c, y, do, pack)

    g_wc = tot[0:1024].reshape(NH, HD, 4 * HD)
    g_bc = tot[1024:1032]
    g_lc = tot[1032:1040, 0:2 * HD]
    g_sgu_w = tot[1040:1296].reshape(256, 4, HD).transpose(1, 0, 2).reshape(NH, HD, HD)
    g_sgu_b = tot[1296:1304, 0:HD]
    tv = tot[1304:1400].reshape(48, D)
    g_final_g, g_norm_g, g_conv_b, g_ln_g, g_ln_b = tv[0:1], tv[1:2], tv[2:3], tv[3:4], tv[4:5]
    g_conv_w_full = tv[5:9]
    dmc_tot = tv[9:12].reshape(1, 3 * D)
    slots_all = tv[16:48].reshape(8, 4, D)
    dmx_all = slots_all[:, 0:3, :].reshape(8, 3 * D)
    c_all = c_slots.reshape(8, 8, D)[:, 0, :]
    g_lru_wa = jnp.stack([g_wc[:, :, 0:HD], g_wc[:, :, 2 * HD:3 * HD]])
    g_lru_wx = jnp.stack([g_wc[:, :, HD:2 * HD], g_wc[:, :, 3 * HD:4 * HD]])
    g_lru_ba = jnp.stack([g_bc[:, 0:HD], g_bc[:, 2 * HD:3 * HD]])
    g_lru_bx = jnp.stack([g_bc[:, HD:2 * HD], g_bc[:, 3 * HD:4 * HD]])
    g_lam_full = jnp.stack([g_lc[:, 0:HD], g_lc[:, HD:2 * HD]]).reshape(2, D)
    g_conv_w = lax.dynamic_slice(g_conv_w_full, (0, 256 * chip), (4, 256))
    g_lam = lax.dynamic_slice(g_lam_full, (0, 256 * chip), (2, 256))
    dmx_all_j = lax.dynamic_slice(dmx_all, (0, 768 * chip), (8, 768))
    dmc_j = lax.dynamic_slice(dmc_tot, (0, 768 * chip), (1, 768))
    ada_full = _late_gather_wait(ada_land, ada_ss, ada_rs, "ada_w", tot, "late_gather_wait_ada_w")
    g_ada_w, g_ada_b, g_c_ctx = _ada_bwd(c_all, dmx_all_j, dmc_j, dmx_all, dmc_tot, c_ctx2, ada_full)

    big = {
        "ada_w": _adam_big(ada_w[0], g_ada_w, m_ada_w[0], v_ada_w[0], "adam_ada_w"),
        "w_in": _adam_big(w_in[0], g_w_in, m_w_in[0], v_w_in[0], "adam_w_in"),
        "w_out": _adam_big(w_out[0], g_w_out, m_w_out[0], v_w_out[0], "adam_w_out"),
    }
    small_in = {
        "c_ctx": (c_ctx, g_c_ctx, m_c_ctx, v_c_ctx, (1, D)),
        "ada_b": (ada_b, g_ada_b, m_ada_b, v_ada_b, (1, 3 * D)),
        "norm_g": (norm_g, g_norm_g, m_norm_g, v_norm_g, (1, D)),
        "conv_w": (conv_w, g_conv_w, m_conv_w, v_conv_w, (4, 256)),
        "conv_b": (conv_b, g_conv_b, m_conv_b, v_conv_b, (1, D)),
        "lru_wa": (lru_wa, g_lru_wa, m_lru_wa, v_lru_wa, (2 * NH * HD, HD)),
        "lru_ba": (lru_ba, g_lru_ba, m_lru_ba, v_lru_ba, (2 * NH, HD)),
        "lru_wx": (lru_wx, g_lru_wx, m_lru_wx, v_lru_wx, (2 * NH * HD, HD)),
        "lru_bx": (lru_bx, g_lru_bx, m_lru_bx, v_lru_bx, (2 * NH, HD)),
        "lru_lambda": (lru_lambda, g_lam, m_lru_lambda, v_lru_lambda, (2, 256)),
        "sgu_ln_g": (sgu_ln_g, g_ln_g, m_sgu_ln_g, v_sgu_ln_g, (1, D)),
        "sgu_ln_b": (sgu_ln_b, g_ln_b, m_sgu_ln_b, v_sgu_ln_b, (1, D)),
        "sgu_w": (sgu_w, g_sgu_w, m_sgu_w, v_sgu_w, (NH * HD, HD)),
        "sgu_b": (sgu_b, g_sgu_b, m_sgu_b, v_sgu_b, (NH, HD)),
        "final_g": (final_g, g_final_g, m_final_g, v_final_g, (1, D)),
    }
    names_small = list(small_in)
    res_small = _adam_small([tuple(a.reshape(small_in[k][4]) for a in small_in[k][:4]) for k in names_small])
    full_shapes = {"ada_w": ada_w.shape, "w_in": w_in.shape, "w_out": w_out.shape}
    grads, deltas, new_m, new_v = {}, {}, {}, {}
    for k in ("ada_w", "w_in", "w_out"):
        g = {"ada_w": g_ada_w, "w_in": g_w_in, "w_out": g_w_out}[k]
        grads[k] = g.reshape(full_shapes[k])
        deltas[k], new_m[k], new_v[k] = (a.reshape(full_shapes[k]) for a in big[k])
    for k, res in zip(names_small, res_small):
        shape = small_in[k][0].shape
        grads[k] = small_in[k][1].reshape(shape)
        deltas[k], new_m[k], new_v[k] = (a.reshape(shape) for a in res)

    loss = jnp.sum(slots_all[:, 3, 0:3])
    order = ["c_ctx", "ada_w", "ada_b", "norm_g", "w_in", "conv_w", "conv_b", "lru_wa", "lru_ba", "lru_wx", "lru_bx",
             "lru_lambda", "sgu_ln_g", "sgu_ln_b", "sgu_w", "sgu_b", "w_out", "final_g"]
    return (loss, grad_x.reshape(x.shape), *[grads[k] for k in order], *[deltas[k] for k in order],
            *[new_m[k] for k in order], *[new_v[k] for k in order])
```

```python
import functools

import jax
import jax.numpy as jnp
from jax import lax
from jax.experimental import pallas as pl
from jax.experimental.pallas import tpu as pltpu

F32 = jnp.float32
BF16 = jnp.bfloat16

D = 1024
NH = 8
HD = 128
NCHIP = 4
T = 256
NORM_EPS = 1e-6
LN_EPS = 1e-5
LRU_C = 8.0
ADAM_LR = 0.001
ADAM_B1 = 0.9
ADAM_B2 = 0.999
ADAM_EPS = 1e-08
ADAM_WD = 0.01
ADAM_STEP = 10

VMEM = pl.BlockSpec(memory_space=pltpu.VMEM)
ANY = pl.BlockSpec(memory_space=pl.ANY)
MESH = pl.DeviceIdType.MESH


def _cp(n_grid=0, vmem_mb=None):
    kw = {}
    if n_grid:
        kw["dimension_semantics"] = ("arbitrary",) * n_grid
    if vmem_mb:
        kw["vmem_limit_bytes"] = vmem_mb << 20
    return pltpu.CompilerParams(**kw)


def _sigmoid(x):
    return 1.0 / (1.0 + jnp.exp(-x))


def _silu_and_grad(x):
    s = _sigmoid(x)
    return x * s, s * (1.0 + x * (1.0 - s))


_GELU_K = 0.7978845608028654
_GELU_C = 0.044715


def _gelu_and_grad(x):
    x2 = x * x
    th = jnp.tanh(_GELU_K * (x + _GELU_C * x * x2))
    g = 0.5 * x * (1.0 + th)
    dg = 0.5 * (1.0 + th) + 0.5 * x * (1.0 - th * th) * (_GELU_K * (1.0 + 3.0 * _GELU_C * x2))
    return g, dg


def _softplus(x):
    return jnp.maximum(x, 0.0) + jnp.log1p(jnp.exp(-jnp.abs(x)))


def _lru_gate(pre, lam_row, d, off=None):
    off = 256 * d if off is None else off
    r = _sigmoid(pre[:, off:off + HD])
    gi = _sigmoid(pre[:, off + HD:off + 2 * HD])
    lam = lam_row[:, HD * d:HD * d + HD]
    sp = _softplus(-lam)
    la = (-LRU_C) * r * sp
    a = jnp.exp(la)
    x2 = 2.0 * la
    m2 = jnp.where(x2 > -1e-3, -x2 * (1.0 + 0.5 * x2), 1.0 - a * a)
    mult = jnp.sqrt(m2)
    return r, gi, lam, sp, a, mult


def _dot(a, b):
    return jnp.dot(a, b, preferred_element_type=F32)


def _dot_tn(a, b):
    return lax.dot_general(a, b, (((0,), (0,)), ((), ())), preferred_element_type=F32)


def _dot_nt(a, b):
    return lax.dot_general(a, b, (((1,), (1,)), ((), ())), preferred_element_type=F32)


def _mo(v, m):
    return v if isinstance(v, int) else pl.multiple_of(v, m)


def _zrows(h, n):
    return pl.ds(h, n, stride=NH)


def _gather_in(c, c_ctx, ada_w, ada_b_j, w_in, w_out, smalls):
    nch = [1, 4]
    wrows = lambda cc, q: (pl.ds(_mo(512 * cc, 16), 512) if q is None
                           else pl.ds(_mo(512 * cc + (512 // nch[1]) * q, 16), 512 // nch[1]))
    specs = [
        ((64, 256), F32, lambda r, jj, cc, q=None: r.at[pl.ds(_mo(16 * jj + 8 * cc, 8), 8), :]),
        ((D, 5120), BF16, lambda r, jj, cc, q=None: r.at[wrows(cc, q), pl.ds(_mo(1280 * jj, 128), 1280)]),
    ]
    halves = [lambda r, cc, q=None: r.at[pl.ds(_mo(8 * cc, 8), 8), :],
              lambda r, cc, q=None: r.at[wrows(cc, q), :]]
    na = len(specs)
    sem_base = [0, 6 * nch[0]]
    sidx = lambda a, q, k: sem_base[a] + 6 * q + k
    n_tiny = 6 * sum(nch)
    n_sem = n_tiny + 10

    def body(c_ref, cc_ref, ada_ref, adab_ref, win_ref, wout_ref, sm_ref,
             mods_o, call_o, sm_o, win_o, wol_o, adal_o, s_win, s_ada, s_wout, f_win, f_ada, f_wout, cslot, lhs, mbuf,
             send_sems, recv_sems, local_sems, load_sems):
        x, y, c = lax.axis_index("x"), lax.axis_index("y"), lax.axis_index("c")
        j = 2 * x + y
        dev = 2 * j + c
        sib = (x, y, 1 - c)
        chips = [(1 - x, y), (x, 1 - y), (1 - x, 1 - y)]
        cj = [2 * cx + cy for cx, cy in chips]
        outs = [sm_o, win_o]
        srcs = [sm_ref, s_win]

        def copy(idx, src, dst, to):
            return pltpu.make_async_remote_copy(src_ref=src, dst_ref=dst, send_sem=send_sems.at[idx],
                                                recv_sem=recv_sems.at[idx], device_id=to, device_id_type=MESH)

        sends = []

        def start(cp):
            cp.start()
            sends.append(cp)

        cslot[...] = jnp.zeros_like(cslot)
        cslot[0:1, :] = c_ref[...]
        my_slot = pl.ds(_mo(8 * dev, 8), 8)
        others = [sib] + [(*chips[k], c) for k in range(3)] + [(*chips[k], 1 - c) for k in range(3)]
        other_dev = [dev + 1 - 2 * c] + [2 * cj[k] + c for k in range(3)] + [2 * cj[k] + 1 - c for k in range(3)]
        base = n_tiny
        for r in range(7):
            start(copy(base + r, cslot, call_o.at[my_slot, :], others[r]))
        call_o[my_slot, :] = cslot[...]

        crow = 512 // nch[1]
        loads = []
        for cc in (c, 1 - c):
            for q in range(nch[1]):
                rows = pl.ds(_mo(512 * cc + crow * q, 16), crow)
                loads.append(pltpu.make_async_copy(win_ref.at[rows, :], f_win.at[rows, :], load_sems.at[len(loads)]))
        loads.append(pltpu.make_async_copy(ada_ref, f_ada, load_sems.at[len(loads)]))
        loads.append(pltpu.make_async_copy(wout_ref, f_wout, load_sems.at[len(loads)]))
        for ld in loads:
            ld.start()
        for k in range(2):
            start(copy(sidx(0, 0, k), halves[0](srcs[0], c), specs[0][2](outs[0], j, c), (*chips[k], c)))
        for q in range(nch[1]):
            loads[q].wait()
            rows = pl.ds(_mo(512 * c + crow * q, 16), crow)
            s_win[rows, :] = f_win[rows, :].astype(BF16)
            for k in range(2):
                start(copy(sidx(1, q, k), halves[1](s_win, c, q), specs[1][2](win_o, j, c, q), (*chips[k], c)))
        for q in range(nch[1]):
            loads[nch[1] + q].wait()
            rows = pl.ds(_mo(512 * (1 - c) + crow * q, 16), crow)
            s_win[rows, :] = f_win[rows, :].astype(BF16)
        local = []
        for a in range(na):
            for cc in range(2):
                lc = pltpu.make_async_copy(halves[a](srcs[a], cc), specs[a][2](outs[a], j, cc), local_sems.at[2 * a + cc])
                lc.start()
                local.append(lc)
        loads[2 * nch[1]].wait()
        s_ada[...] = f_ada[...].astype(BF16)
        loads[2 * nch[1] + 1].wait()
        s_wout[...] = f_wout[...].astype(BF16)
        for q, (src, dst) in enumerate([(s_wout, wol_o.at[pl.ds(_mo(512 * j, 16), 512), :]),
                                        (s_ada, adal_o.at[:, pl.ds(_mo(768 * j, 128), 768)])]):
            lc = pltpu.make_async_copy(src, dst, local_sems.at[2 * na + q])
            lc.start()
            local.append(lc)

        for r in range(7):
            slot = call_o.at[pl.ds(_mo(8 * other_dev[r], 8), 8), :]
            copy(base + r, slot, slot, sib).wait_recv()
        lhs[...] = jnp.zeros_like(lhs)
        for b in range(8):
            cv = call_o[8 * b:8 * b + 1, :]
            lhs[b:b + 1, :] = cv * _sigmoid(cv)
        cv = cc_ref[...]
        lhs[8:9, :] = cv * _sigmoid(cv)
        mbuf[j] = _dot(lhs[...].astype(BF16), s_ada[...]) + adab_ref[...]
        for k in range(3):
            start(copy(base + 7 + k, mbuf.at[j], mbuf.at[j], (*chips[k], c)))
        for k in range(3):
            copy(base + 7 + k, mbuf.at[cj[k]], mbuf.at[cj[k]], sib).wait_recv()
        mods_o[...] = jnp.zeros_like(mods_o)
        for jj in range(NCHIP):
            mods_o[0:1, 768 * jj:768 * jj + 768] = mbuf[jj, pl.ds(dev, 1), :]
            mods_o[1:2, 768 * jj:768 * jj + 768] = mbuf[jj, 8:9, :]

        kx = [1 - x, x, 1 - x]
        ky = [y, 1 - y, 1 - y]
        pick = lambda k, lst: jnp.where(k == 0, lst[0], jnp.where(k == 1, lst[1], lst[2]))
        for a in range(na):
            for q in range(nch[a]):
                for step, k in enumerate([c, 1 - c]):
                    reg = specs[a][2](outs[a], pick(k, cj), c, q)
                    copy(sidx(a, q, k), reg, reg, sib).wait_recv()
                    if step == 0:
                        start(copy(sidx(a, q, 2), reg, reg, (pick(1 - c, kx), pick(1 - c, ky), c)))
                    start(copy(sidx(a, q, 3 + k), reg, reg, sib))
        for a in range(na):
            for q in range(nch[a]):
                reg = specs[a][2](outs[a], cj[2], c, q)
                copy(sidx(a, q, 2), reg, reg, sib).wait_recv()
                start(copy(sidx(a, q, 5), reg, reg, sib))
        for a in range(na):
            for q in range(nch[a]):
                for k in range(3):
                    reg = specs[a][2](outs[a], cj[k], 1 - c, q)
                    copy(sidx(a, q, 3 + k), reg, reg, sib).wait_recv()
        for cp in sends:
            cp.wait_send()
        for lc in local:
            lc.wait()

    out_shape = (jax.ShapeDtypeStruct((8, 3 * D), F32), jax.ShapeDtypeStruct((64, D), F32),
                 jax.ShapeDtypeStruct(specs[0][0], F32), jax.ShapeDtypeStruct(specs[1][0], BF16),
                 jax.ShapeDtypeStruct((2048, D), BF16), jax.ShapeDtypeStruct((D, 3 * D), BF16))
    return pl.pallas_call(
        body, name="gather_in", out_shape=out_shape,
        in_specs=[VMEM, VMEM, ANY, VMEM, ANY, ANY, VMEM], out_specs=(VMEM, VMEM, VMEM, ANY, ANY, ANY),
        scratch_shapes=[pltpu.VMEM((D, 1280), BF16), pltpu.VMEM((D, 768), BF16), pltpu.VMEM((512, D), BF16),
                        pltpu.VMEM((D, 1280), F32), pltpu.VMEM((D, 768), F32), pltpu.VMEM((512, D), F32),
                        pltpu.VMEM((8, D), F32), pltpu.VMEM((16, D), F32), pltpu.VMEM((NCHIP, 16, 768), F32),
                        pltpu.SemaphoreType.DMA((n_sem,)), pltpu.SemaphoreType.DMA((n_sem,)),
                        pltpu.SemaphoreType.DMA((2 * na + 2,)), pltpu.SemaphoreType.DMA((2 * nch[1] + 2,))],
        compiler_params=_cp(vmem_mb=56),
    )(c, c_ctx, ada_w, ada_b_j, w_in, w_out, smalls)


HBM = pl.BlockSpec(memory_space=pltpu.HBM)
SEM = pl.BlockSpec(memory_space=pltpu.SEMAPHORE)


def _late_gather_regions(x, y, c):
    chips = [(1 - x, y), (x, 1 - y), (1 - x, 1 - y)]
    wo_reg = lambda r, jj, cc: r.at[pl.ds(_mo(512 * jj + 256 * cc, 16), 256), :]
    ada_reg = lambda r, jj, cc: r.at[pl.ds(_mo(512 * cc, 16), 512), pl.ds(_mo(768 * jj, 128), 768)]
    return chips, wo_reg, ada_reg


def _late_gather_start(wo_land, ada_land):
    def body(wol_ref, adal_ref, wo_ss, wo_rs, ada_ss, ada_rs, wol_thru, adal_thru, token):
        x, y, c = lax.axis_index("x"), lax.axis_index("y"), lax.axis_index("c")
        j = 2 * x + y
        chips, wo_reg, ada_reg = _late_gather_regions(x, y, c)
        for k in range(3):
            for cc in range(2):
                pltpu.make_async_remote_copy(src_ref=wo_reg(wol_ref, j, c), dst_ref=wo_reg(wol_ref, j, c),
                                             send_sem=wo_ss.at[2 * k + cc], recv_sem=wo_rs.at[2 * k + c],
                                             device_id=(*chips[k], cc), device_id_type=MESH).start()
        for k in range(3):
            for cc in range(2):
                pltpu.make_async_remote_copy(src_ref=ada_reg(adal_ref, j, c), dst_ref=ada_reg(adal_ref, j, c),
                                             send_sem=ada_ss.at[2 * k + cc], recv_sem=ada_rs.at[2 * k + c],
                                             device_id=(*chips[k], cc), device_id_type=MESH).start()
        token[...] = jnp.zeros_like(token)

    sems = pltpu.SemaphoreType.DMA((6,))
    return pl.pallas_call(
        body, name="late_gather_start",
        out_shape=(sems, sems, sems, sems, pltpu.HBM(wo_land.shape, BF16), pltpu.HBM(ada_land.shape, BF16),
                   jax.ShapeDtypeStruct((8, 128), F32)),
        in_specs=(HBM, HBM), out_specs=(SEM, SEM, SEM, SEM, HBM, HBM, VMEM), input_output_aliases={0: 4, 1: 5},
        compiler_params=pltpu.CompilerParams(has_side_effects=pltpu.SideEffectType.DATAFLOW_SIDE_EFFECTING),
    )(pltpu.with_memory_space_constraint(wo_land, pltpu.HBM), pltpu.with_memory_space_constraint(ada_land, pltpu.HBM))


def _late_gather_wait(land, send_sems, recv_sems, which, after, name):
    def body(land_ref, ss, rs, after_ref, land_out):
        x, y, c = lax.axis_index("x"), lax.axis_index("y"), lax.axis_index("c")
        j = 2 * x + y
        chips, wo_reg, ada_reg = _late_gather_regions(x, y, c)
        reg = wo_reg if which == "w_out" else ada_reg
        for k in range(3):
            kj = 2 * chips[k][0] + chips[k][1]
            for cc in range(2):
                cp = pltpu.make_async_remote_copy(src_ref=reg(land_ref, j, c), dst_ref=reg(land_ref, kj, cc),
                                                  send_sem=ss.at[2 * k + cc], recv_sem=rs.at[2 * k + cc],
                                                  device_id=(*chips[k], cc), device_id_type=MESH)
                cp.wait_send()
                cp.wait_recv()

    return pl.pallas_call(
        body, name=name, out_shape=pltpu.HBM(land.shape, land.dtype),
        in_specs=(HBM, SEM, SEM, ANY), out_specs=HBM, input_output_aliases={0: 0},
        compiler_params=pltpu.CompilerParams(has_side_effects=pltpu.SideEffectType.DATAFLOW_SIDE_EFFECTING),
    )(land, send_sems, recv_sems, after)


RCHUNK = 16


def _grads_reduce(hn, dzs, hn_c, dxa_c, y, do, pack):
    rp = pack.shape[0]
    hp = rp // 2
    assert hp % RCHUNK == 0
    wi_w = 1280
    lx, lc = hn.shape[0], hn_c.shape[0]
    lt = lx + lc
    n_dz = len(dzs)

    def body(*refs):
        hn_hbm, dz_hbm = refs[0], refs[1:1 + n_dz]
        hnc_hbm, dxac_hbm, y_hbm, do_hbm, pk_hbm, wi_out, wo_out, pk_out = refs[1 + n_dz:9 + n_dz]
        (hn_mine, hn_other, dzbuf, wi_other, wi_mine, wi_recv, wi_send, wi_rb,
         y_blk, do_mine, do_other, wo_other, wo_mine, wo_recv, wo_send, wo_rb,
         pk_mine, pk_recv, pk_send, pk_rb, pk_own, send_sems, recv_sems, local_sems) = refs[9 + n_dz:]
        x, y, c = lax.axis_index("x"), lax.axis_index("y"), lax.axis_index("c")
        j = 2 * x + y
        sib = (x, y, 1 - c)
        chips = [(1 - x, y), (x, 1 - y), (1 - x, 1 - y)]
        cj = [2 * cx + cy for cx, cy in chips]
        near = (jnp.where(c == 0, 1 - x, x), jnp.where(c == 0, y, 1 - y), c)
        slabs = [cj[2], cj[0], cj[1], j]

        def copy(k, src, dst, to):
            return pltpu.make_async_remote_copy(src_ref=src, dst_ref=dst, send_sem=send_sems.at[k],
                                                recv_sem=recv_sems.at[k], device_id=to, device_id_type=MESH)

        def local(k, src, dst):
            cp = pltpu.make_async_copy(src, dst, local_sems.at[k])
            cp.start()
            return cp

        rows_half = lambda r, cc, n: r.at[pl.ds(_mo(cc * n, 16), n), :]
        cols_half = lambda r, cc, n: r.at[:, pl.ds(_mo(cc * n, 128), n)]
        pk_piece = lambda r, cc, jj: r.at[pl.ds(_mo(cc * hp, 16), hp), pl.ds(_mo(jj * 128, 128), 128)]

        sends = []

        def start(cp):
            cp.start()
            sends.append(cp)

        def dz_pieces(s):
            g0 = wi_w * s
            k0, off0 = g0 // D, g0 % D
            w0 = min(D - off0, wi_w)
            pieces = [(k0, off0, w0, 0)]
            if w0 < wi_w:
                pieces.append((k0 + 1, 0, wi_w - w0, w0))
            return pieces

        def dz_copies(s):
            cps = []
            for q, (k, off, w, dst) in enumerate(dz_pieces(s)):
                cps.append(pltpu.make_async_copy(dz_hbm[k].at[:, pl.ds(off, w)], dzbuf.at[pl.ds(0, lx), pl.ds(dst, w)],
                                                 local_sems.at[11 + q]))
            if s == 0:
                cps.append(pltpu.make_async_copy(dxac_hbm, dzbuf.at[pl.ds(lx, lc), pl.ds(0, D)], local_sems.at[13]))
            return cps

        def dz_load(sl):
            for s in range(NCHIP):
                @pl.when(sl == s)
                def _():
                    if s == 0:
                        dzbuf[pl.ds(lx, lc), pl.ds(D, wi_w - D)] = jnp.zeros((lc, wi_w - D), BF16)
                    else:
                        dzbuf[pl.ds(lx, lc), :] = jnp.zeros((lc, wi_w), BF16)
                    for cp in dz_copies(s):
                        cp.start()

        def dz_wait(sl):
            for s in range(NCHIP):
                @pl.when(sl == s)
                def _():
                    for cp in dz_copies(s):
                        cp.wait()

        l_pk = local(0, rows_half(pk_hbm, c, hp), pk_mine)
        start(copy(0, rows_half(pk_hbm, 1 - c, hp), pk_recv, sib))
        col = lambda r, cc: r.at[:, pl.ds(_mo(cc * 512, 128), 512)]
        do_loads = [local(7, col(do_hbm, c), do_mine), local(14, col(do_hbm, 1 - c), do_other)]
        hn_loads = [local(2, col(hn_hbm, c), hn_mine.at[pl.ds(0, lx), :]),
                    local(3, col(hnc_hbm, c), hn_mine.at[pl.ds(lx, lc), :]),
                    local(4, col(hn_hbm, 1 - c), hn_other.at[pl.ds(0, lx), :]),
                    local(5, col(hnc_hbm, 1 - c), hn_other.at[pl.ds(lx, lc), :])]
        y_copy = lambda s: pltpu.make_async_copy(col(y_hbm, slabs[s]), y_blk, local_sems.at[1])
        y_copy(0).start()
        dz_load(slabs[0])

        def pair_sum(mine, recv, send, nrows, keep, relayed=None):
            def step(i, carry):
                rows = pl.ds(_mo(i * RCHUNK, RCHUNK), RCHUNK)
                s = mine[rows, :] + recv[rows, :].astype(F32)
                if relayed is not None:
                    s = s + relayed[rows, :].astype(F32)
                if keep:
                    mine[rows, :] = s
                if send is not None:
                    send[rows, :] = s.astype(BF16)
                return carry
            lax.fori_loop(0, nrows // RCHUNK, step, 0)

        def chip_sum(own, rb, nrows, terms=(0, 1, 2)):
            def step(i, carry):
                rows = pl.ds(_mo(i * RCHUNK, RCHUNK), RCHUNK)
                acc = own[rows, :]
                for q in terms:
                    acc = acc + rb[q, rows, :].astype(F32)
                own[rows, :] = acc
                return carry
            lax.fori_loop(0, nrows // RCHUNK, step, 0)

        w_in_g = dict(other=wi_other, mine=wi_mine, recv=wi_recv, send=wi_send, rb=wi_rb, p1_sems=(2, 3, 4, 5), p2_sem=12,
                      p1=[None] * NCHIP, wait_load=lambda s: dz_wait(slabs[s]), load=lambda s: dz_load(slabs[s]),
                      dot_other=lambda: _dot_tn(hn_other[...], dzbuf[...]), dot_mine=lambda: _dot_tn(hn_mine[...], dzbuf[...]))
        w_out_g = dict(other=wo_other, mine=wo_mine, recv=wo_recv, send=wo_send, rb=wo_rb, p1_sems=(1, 24, 25, 26), p2_sem=9,
                       p1=[None] * NCHIP, wait_load=lambda s: y_copy(s).wait(), load=lambda s: y_copy(s).start(),
                       dot_other=lambda: _dot_tn(y_blk[...], do_other[...]), dot_mine=lambda: _dot_tn(y_blk[...], do_mine[...]))

        def piece_matmuls(g, s):
            if s >= 2:
                g["p1"][s - 2].wait_send()
            g["wait_load"](s)
            g["other"][s % 2] = g["dot_other"]().astype(BF16)
            g["p1"][s] = copy(g["p1_sems"][s], g["other"].at[s % 2], g["recv"].at[s], sib)
            g["p1"][s].start()
            g["mine"][s % 2] = g["dot_mine"]()
            if s + 1 < NCHIP:
                g["load"](s + 1)

        def piece_finish(g, s):
            mine, recv, send, rb, p2 = g["mine"].at[s % 2], g["recv"].at[s], g["send"], g["rb"], g["p2_sem"]
            nrows = mine.shape[0]
            copy(g["p1_sems"][s], recv, recv, sib).wait_recv()
            if s == 3:
                pair_sum(mine, recv, None, nrows, True)
                return
            if s == 0:
                pair_sum(mine, recv, send.at[0], nrows, False)
                start(copy(p2, send.at[0], rb.at[0], near))
                return
            adds_relayed = c == (1 if s == 1 else 0)

            @pl.when(adds_relayed)
            def _():
                copy(p2, rb.at[0], rb.at[0], sib).wait_recv()
                pair_sum(mine, recv, send.at[s], nrows, False, rb.at[0])

            @pl.when(jnp.logical_not(adds_relayed))
            def _():
                pair_sum(mine, recv, send.at[s], nrows, False)
            start(copy(p2 + s, send.at[s], rb.at[s], (*chips[s - 1], c)))

        def piece_total(g):
            for k in (1, 2):
                copy(g["p2_sem"] + k, g["rb"].at[k], g["rb"].at[k], sib).wait_recv()
            chip_sum(g["mine"].at[1], g["rb"], g["mine"].shape[1], (1, 2))

        for cp in do_loads:
            cp.wait()
        piece_matmuls(w_out_g, 0)
        piece_matmuls(w_out_g, 1)
        piece_finish(w_out_g, 0)
        piece_matmuls(w_out_g, 2)
        piece_finish(w_out_g, 1)
        piece_matmuls(w_out_g, 3)
        piece_finish(w_out_g, 2)

        for cp in hn_loads:
            cp.wait()
        piece_matmuls(w_in_g, 0)

        l_pk.wait()
        copy(0, pk_recv, pk_recv, sib).wait_recv()
        pair_sum(pk_mine, pk_recv, pk_send, hp, True)
        for k in range(3):
            start(copy(6 + k, pk_send.at[:, pl.ds(_mo(cj[k] * 128, 128), 128)], pk_rb.at[k], (*chips[k], c)))
        l_pk_own = local(6, pk_mine.at[:, pl.ds(_mo(j * 128, 128), 128)], pk_own)

        piece_matmuls(w_in_g, 1)
        piece_finish(w_in_g, 0)

        l_pk_own.wait()
        for k in range(3):
            copy(6 + k, pk_rb.at[k], pk_rb.at[k], sib).wait_recv()
        chip_sum(pk_own, pk_rb, hp)
        l_pk_out = local(8, pk_own, pk_piece(pk_out, c, j))
        start(copy(15, pk_own, pk_piece(pk_out, c, j), sib))
        for k in range(3):
            start(copy(16 + k, pk_own, pk_piece(pk_out, c, j), (*chips[k], c)))

        piece_matmuls(w_in_g, 2)
        piece_finish(w_in_g, 1)
        piece_matmuls(w_in_g, 3)
        piece_finish(w_in_g, 2)

        piece_finish(w_out_g, 3)
        piece_total(w_out_g)
        l_wo_out = local(9, wo_mine.at[1], cols_half(wo_out, c, 512))
        start(copy(22, wo_mine.at[1], cols_half(wo_out, c, 512), sib))

        for k in range(3):
            reg = pk_piece(pk_out, c, cj[k])
            copy(16 + k, reg, reg, sib).wait_recv()
            start(copy(19 + k, reg, reg, sib))

        piece_finish(w_in_g, 3)
        piece_total(w_in_g)
        l_wi_out = local(10, wi_mine.at[1], rows_half(wi_out, c, 512))
        start(copy(23, wi_mine.at[1], rows_half(wi_out, c, 512), sib))

        reg = pk_piece(pk_out, 1 - c, j)
        copy(15, reg, reg, sib).wait_recv()
        for k in range(3):
            reg = pk_piece(pk_out, 1 - c, cj[k])
            copy(19 + k, reg, reg, sib).wait_recv()
        reg = cols_half(wo_out, 1 - c, 512)
        copy(22, reg, reg, sib).wait_recv()
        reg = rows_half(wi_out, 1 - c, 512)
        copy(23, reg, reg, sib).wait_recv()
        for cp in sends + w_in_g["p1"][2:] + w_out_g["p1"][2:]:
            cp.wait_send()
        for cp in (l_pk_out, l_wo_out, l_wi_out):
            cp.wait()

    return pl.pallas_call(
        body, name="grads_reduce",
        out_shape=(jax.ShapeDtypeStruct((D, wi_w), F32), jax.ShapeDtypeStruct((512, D), F32),
                   jax.ShapeDtypeStruct(pack.shape, F32)),
        in_specs=[ANY] * (6 + n_dz), out_specs=(ANY,) * 3,
        scratch_shapes=[
            pltpu.VMEM((lt, 512), BF16), pltpu.VMEM((lt, 512), BF16), pltpu.VMEM((lt, wi_w), BF16),
            pltpu.VMEM((2, 512, wi_w), BF16), pltpu.VMEM((2, 512, wi_w), F32), pltpu.VMEM((4, 512, wi_w), BF16),
            pltpu.VMEM((3, 512, wi_w), BF16), pltpu.VMEM((3, 512, wi_w), BF16),
            pltpu.VMEM((lx, 512), BF16), pltpu.VMEM((lx, 512), BF16), pltpu.VMEM((lx, 512), BF16),
            pltpu.VMEM((2, 512, 512), BF16), pltpu.VMEM((2, 512, 512), F32), pltpu.VMEM((4, 512, 512), BF16),
            pltpu.VMEM((3, 512, 512), BF16), pltpu.VMEM((3, 512, 512), BF16),
            pltpu.VMEM((hp, 512), F32), pltpu.VMEM((hp, 512), F32), pltpu.VMEM((hp, 512), BF16),
            pltpu.VMEM((3, hp, 128), BF16), pltpu.VMEM((hp, 128), F32),
            pltpu.SemaphoreType.DMA((27,)), pltpu.SemaphoreType.DMA((27,)), pltpu.SemaphoreType.DMA((15,))],
        compiler_params=_cp(vmem_mb=56),
    )(hn, *dzs, hn_c, dxa_c, y, do, pack)


def _ada_bwd(c_all, dmx_all_j, dmc_j, dmx_all, dmc, c_ctx, ada_w_full):
    def body(c_ref, dmxj_ref, dmcj_ref, dmx_ref, dmc_ref, cc_ref, w_ref, gw_ref, gb_ref, gc_ref, lhs, rhs, dm8):
        lhs[...] = jnp.zeros_like(lhs)
        rhs[...] = jnp.zeros_like(rhs)
        cv = c_ref[...]
        lhs[0:8, :] = cv * _sigmoid(cv)
        cc = cc_ref[...]
        a_c, da_c = _silu_and_grad(cc)
        lhs[8:9, :] = a_c
        rhs[0:8, :] = dmxj_ref[...]
        rhs[8:9, :] = dmcj_ref[...]
        gw_ref[...] = _dot_tn(lhs[...].astype(BF16), rhs[...].astype(BF16))
        gb_ref[...] = jnp.sum(dmx_ref[...], axis=0, keepdims=True) + dmc_ref[...]
        dm8[...] = jnp.zeros_like(dm8)
        dm8[0:1, :] = dmc_ref[...]
        da = _dot_nt(dm8[...].astype(BF16), w_ref[...])
        gc_ref[...] = da[0:1, :] * da_c

    return pl.pallas_call(
        body, name="ada_bwd",
        out_shape=(jax.ShapeDtypeStruct((D, 768), F32), jax.ShapeDtypeStruct((1, 3 * D), F32),
                   jax.ShapeDtypeStruct((1, D), F32)),
        in_specs=[VMEM] * 7, out_specs=(VMEM,) * 3,
        scratch_shapes=[pltpu.VMEM((16, D), F32), pltpu.VMEM((16, 768), F32), pltpu.VMEM((8, 3 * D), F32)],
        compiler_params=_cp(vmem_mb=32),
    )(c_all, dmx_all_j, dmc_j, dmx_all, dmc, c_ctx, ada_w_full)


def _proj(x, mods, mrow, norm_g, w_full, nk, name, sgu=None):
    lx = x.shape[0]
    n = lx // T
    order = [2, 3, 0, 1, 4] if sgu is not None else list(range(nk))

    def body(x_ref, sh_ref, sc_ref, ng_ref, *rest):
        w_refs, rest = rest[:nk], rest[nk:]
        if sgu is not None:
            g_ref, b_ref, sw_ref, bt_ref = rest[:4]
            rest = rest[4:]
        hn_ref, z_refs = rest[0], rest[1:1 + nk]
        xv = x_ref[...]
        r = lax.rsqrt(jnp.mean(xv * xv, axis=-1, keepdims=True) + NORM_EPS)
        hn = (xv * r) * ng_ref[...] * (1.0 + sc_ref[mrow:mrow + 1, :]) + sh_ref[mrow:mrow + 1, :]
        hb = hn.astype(BF16)
        hn_ref[...] = hb
        for k in order[:2]:
            z_refs[k][...] = _dot(hb, w_refs[k][...])
        if sgu is not None:
            ys_ref, mixed_s = rest[1 + nk], rest[2 + nk]
            for ch in range(T // HD):
                rows = slice(HD * ch, HD * ch + HD)
                ug = _sgu_parts(z_refs[2][rows, :], z_refs[3][rows, :], g_ref[...], b_ref[...], sw_ref, bt_ref,
                                mixed_s)[0]
                ys_ref[rows, :] = ug * mixed_s[...]
        for k in order[2:]:
            z_refs[k][...] = _dot(hb, w_refs[k][...])

    row = pl.BlockSpec((T, D), lambda i: (i, 0))
    vec = pl.BlockSpec((1, D), lambda i: (0, 0))
    in_specs = [row, pl.BlockSpec((8, D), lambda i: (0, 0)), pl.BlockSpec((8, D), lambda i: (0, 1)), vec]
    in_specs += [pl.BlockSpec((D, D), lambda i, k=k: (0, k)) for k in range(nk)]
    args = [x, mods, mods, norm_g] + [w_full] * nk
    out_shape = (jax.ShapeDtypeStruct((lx, D), BF16),) + tuple(jax.ShapeDtypeStruct((lx, D), F32) for _ in range(nk))
    scratch = []
    if sgu is not None:
        in_specs += [vec, vec, pl.BlockSpec((NH, HD, HD), lambda i: (0, 0, 0)), pl.BlockSpec((HD, NH), lambda i: (0, 0))]
        args += list(sgu)
        out_shape += (jax.ShapeDtypeStruct((lx, D), F32),)
        scratch = [pltpu.VMEM((HD, D), F32)]
    return pl.pallas_call(
        body, name=name, grid=(n,), out_shape=out_shape, in_specs=in_specs, out_specs=(row,) * len(out_shape),
        scratch_shapes=scratch, compiler_params=_cp(1, vmem_mb=56),
    )(*args)


def _halo_specs(lx):
    last = lx // 8 - 1
    return [pl.BlockSpec((T, D), lambda i: (i, 0)),
            pl.BlockSpec((8, D), lambda i: (jnp.maximum(i * (T // 8) - 1, 0), 0)),
            pl.BlockSpec((8, D), lambda i: (jnp.minimum((i + 1) * (T // 8), last), 0))]


def _zhalo_specs(lx):
    last = lx // 8 - 1
    return [pl.BlockSpec((T * NH, HD), lambda i: (i, 0)),
            pl.BlockSpec((8 * NH, HD), lambda i: (jnp.maximum(i * (T // 8) - 1, 0), 0)),
            pl.BlockSpec((8 * NH, HD), lambda i: (jnp.minimum((i + 1) * (T // 8), last), 0))]


ZT = pl.BlockSpec((T * NH, HD), lambda i: (i, 0))
CONV_CHUNK = 32


SCAN_SUB = 4


def _scan_tile(chains, post, carry_ref):
    blk = T // SCAN_SUB

    def step(k, state):
        new = []
        for ci, (a_ref, x_ref, o_ref, q_ref, reverse) in enumerate(chains):
            for q in range(SCAN_SUB):
                s, p = state[ci * SCAN_SUB + q]
                t = (q + 1) * blk - 1 - k if reverse else q * blk + k
                r = pl.ds(_mo(t * NH, NH), NH)
                a = a_ref[r, :]
                if post:
                    o = x_ref[r, :] + s
                    o_ref[r, :] = o
                    q_ref[r, :] = p
                    new.append((a * o, a * p))
                else:
                    o = a * s + x_ref[r, :]
                    p = a * p
                    o_ref[r, :] = o
                    q_ref[r, :] = p
                    new.append((o, p))
        return tuple(new)

    zero = jnp.zeros((NH, HD), F32)
    one = jnp.ones((NH, HD), F32)
    final = lax.fori_loop(0, blk, step, tuple((zero, one) for _ in range(len(chains) * SCAN_SUB)), unroll=2)
    for ci, (a_ref, x_ref, o_ref, q_ref, reverse) in enumerate(chains):
        carry = carry_ref[ci]
        for q in (range(SCAN_SUB - 1, -1, -1) if reverse else range(SCAN_SUB)):
            rows = pl.ds(q * blk * NH, blk * NH)
            fixed = o_ref[rows, :].reshape(blk, NH, HD) + q_ref[rows, :].reshape(blk, NH, HD) * carry[None]
            o_ref[rows, :] = fixed.reshape(blk * NH, HD)
            s_loc, p_loc = final[ci * SCAN_SUB + q]
            carry = s_loc + p_loc * carry
        carry_ref[ci] = carry


def _lru_fwd(xa, conv_wz, conv_bz, wcat, bcat, lamcat, s_f, s_b, name):
    lx = xa.shape[0]
    n = lx // T

    def body(xm_u, xp_u, xn_u, xm_d, xp_d, xn_d, cw, cb, w_ref, b_ref, lam_ref, su0, sd0,
             xaz_o, xcz_o, af_o, ab_o, hf_o, hb_o, gf_o, gb_o, fu, fd, pad, xc_d, x_u, x_d, q_u, q_d, carry):
        i = pl.program_id(0)

        @pl.when(i == 0)
        def _():
            carry[0] = su0[...]
            carry[1] = sd0[...]

        def conv_gates(xm, xp, xn, tile, d, xc_ref, a_ref, x_ref, xaz_ref, g_ref):
            pmask = jnp.where(tile == 0, 0.0, 1.0)
            nmask = jnp.where(tile == n - 1, 0.0, 1.0)
            for h in range(NH):
                cols = slice(HD * h, HD * h + HD)
                pad[_zrows(h, 8), :] = xp[:, cols] * pmask
                pad[pl.ds(8 * NH + h, T, stride=NH), :] = xm[:, cols]
                pad[pl.ds((T + 8) * NH + h, 8, stride=NH), :] = xn[:, cols] * nmask
            if xaz_ref is not None:
                xaz_ref[...] = pad[pl.ds(8 * NH, T * NH), :]

            def conv_chunk(ci, c_):
                base = pl.multiple_of(ci * (CONV_CHUNK * NH), CONV_CHUNK * NH)
                acc = None
                for k in range(4):
                    sl = pad[pl.ds(base + (7 + k) * NH, CONV_CHUNK * NH), :].reshape(CONV_CHUNK, NH, HD)
                    term = sl * cw[k][None]
                    acc = term if acc is None else acc + term
                acc = acc + cb[...][None]
                xc_ref[pl.ds(base, CONV_CHUNK * NH), :] = acc.reshape(CONV_CHUNK * NH, HD)
                return c_
            lax.fori_loop(0, T // CONV_CHUNK, conv_chunk, 0)

            for h in range(NH):
                xch = xc_ref[_zrows(h, T), :]
                pre = _dot(xch.astype(BF16), w_ref[h, :, 256 * d:256 * d + 256]) + b_ref[h:h + 1, 256 * d:256 * d + 256]
                r, gi, _, _, a, mult = _lru_gate(pre, lam_ref[h:h + 1, :], d, 0)
                a_ref[_zrows(h, T), :] = a
                x_ref[_zrows(h, T), :] = mult * gi * xch
                for q, val in enumerate((r, gi, mult)):
                    g_ref[:, q * D + HD * h:q * D + HD * h + HD] = val

        conv_gates(xm_u, xp_u, xn_u, i, 0, xcz_o, af_o, x_u, xaz_o, gf_o)
        conv_gates(xm_d, xp_d, xn_d, n - 1 - i, 1, xc_d, ab_o, x_d, None, gb_o)

        _scan_tile([(af_o, x_u, hf_o, q_u, False), (ab_o, x_d, hb_o, q_d, True)], False, carry)
        fu[...] = carry[0]
        fd[...] = carry[1]

    full = lambda shape: pl.BlockSpec(shape, lambda i: (0,) * len(shape))
    last = lx // 8 - 1
    rev = lambda i: n - 1 - i
    halo_dn = [pl.BlockSpec((T, D), lambda i: (rev(i), 0)),
               pl.BlockSpec((8, D), lambda i: (jnp.maximum(rev(i) * (T // 8) - 1, 0), 0)),
               pl.BlockSpec((8, D), lambda i: (jnp.minimum((rev(i) + 1) * (T // 8), last), 0))]
    st = full((NH, HD))
    in_specs = _halo_specs(lx) + halo_dn + [full((4, NH, HD)), st, full((NH, HD, 4 * HD)), full((NH, 4 * HD)),
                                            full((NH, 2 * HD)), st, st]
    dn = pl.BlockSpec((T * NH, HD), lambda i: (rev(i), 0))
    zs = jax.ShapeDtypeStruct((lx * NH, HD), F32)
    ss = jax.ShapeDtypeStruct((NH, HD), F32)
    zbuf = pltpu.VMEM((T * NH, HD), F32)
    gs = jax.ShapeDtypeStruct((lx, 3 * D), F32)
    g_up = pl.BlockSpec((T, 3 * D), lambda i: (i, 0))
    g_dn = pl.BlockSpec((T, 3 * D), lambda i: (rev(i), 0))
    return pl.pallas_call(
        body, name=name, grid=(n,), out_shape=(zs,) * 6 + (gs, gs, ss, ss), in_specs=in_specs,
        out_specs=(ZT, ZT, ZT, dn, ZT, dn, g_up, g_dn, st, st),
        scratch_shapes=[pltpu.VMEM(((T + 16) * NH, HD), F32), zbuf, zbuf, zbuf, zbuf, zbuf,
                        pltpu.VMEM((2, NH, HD), F32)],
        compiler_params=_cp(1, vmem_mb=48),
    )(xa, xa, xa, xa, xa, xa, conv_wz, conv_bz, wcat, bcat, lamcat, s_f, s_b)


def _sgu_parts(u, v, lng, lnb, w_ref, bt_ref, mixed_s):
    ug, dug = _gelu_and_grad(u)
    vg, dvg = _gelu_and_grad(v)
    mu = jnp.mean(vg, axis=-1, keepdims=True)
    vc = vg - mu
    rstd = lax.rsqrt(jnp.mean(vc * vc, axis=-1, keepdims=True) + LN_EPS)
    vh = vc * rstd
    vn = (vh * lng + lnb).astype(BF16)
    for g in range(NH):
        cols = slice(HD * g, HD * g + HD)
        mixed_s[:, cols] = _dot(w_ref[g], vn[:, cols]) + bt_ref[:, g:g + 1]
    return ug, dug, dvg, rstd, vh, vn


def _sgu_bwd_chunk(u, v, dys_v, lng, lnb, w_ref, bt_ref, mixed_s, dvn_s, dw_ref, db_ref, dg_ref, dbl_ref):
    ug, dug, dvg, rstd, vh, vn = _sgu_parts(u, v, lng, lnb, w_ref, bt_ref, mixed_s)
    du = (dys_v * mixed_s[...] * dug).astype(BF16)
    dmix = dys_v * ug
    ones = jnp.ones((8, HD), BF16)
    for g in range(NH):
        cols = slice(HD * g, HD * g + HD)
        dm = dmix[:, cols]
        hi = dm.astype(BF16)
        lo = (dm - hi.astype(F32)).astype(BF16)
        dw_ref[g] += _dot_nt(hi, vn[:, cols])
        db_ref[g:g + 1, :] += (_dot_nt(ones, hi) + _dot_nt(ones, lo))[0:1, :]
        dvn_s[:, cols] = _dot_tn(w_ref[g], hi)
    dvn = dvn_s[...]
    dg_ref[...] += jnp.sum(dvn * vh, axis=0, keepdims=True)
    dbl_ref[...] += jnp.sum(dvn, axis=0, keepdims=True)
    dvh = dvn * lng
    dvg_in = rstd * (dvh - jnp.mean(dvh, axis=-1, keepdims=True) - vh * jnp.mean(dvh * vh, axis=-1, keepdims=True))
    return du, (dvg_in * dvg).astype(BF16)


def _out_fwd_bwd(hf_z, hb_z, ga, gb, ys, x, tgt, mods, final_g, w_out_full):
    lx = x.shape[0]
    n = lx // T

    def body(hf_ref, hb_ref, ga_ref, gb_ref, ys_ref, x_ref, t_ref, gx_ref, fg_ref, w_ref,
             loss_ref, dfg_ref, dgx_ref, dxn_ref, y_ref, do_ref, dga_ref, dgb_ref, dyl_ref, dys_ref, yl_s):
        i = pl.program_id(0)

        @pl.when(i == 0)
        def _():
            loss_ref[...] = jnp.zeros_like(loss_ref)
            dfg_ref[...] = jnp.zeros_like(dfg_ref)
            dgx_ref[...] = jnp.zeros_like(dgx_ref)

        for h in range(NH):
            yl_s[:, HD * h:HD * h + HD] = hf_ref[_zrows(h, T), :] + hb_ref[_zrows(h, T), :]
        yl = yl_s[...]
        gav = ga_ref[...]
        gbv = gb_ref[...]
        sa, dsa = _silu_and_grad(gav)
        sb, dsb = _silu_and_grad(gbv)
        ysv = ys_ref[...]
        y_ref[:, 0:D] = (yl * sa).astype(BF16)
        y_ref[:, D:2 * D] = (ysv * sb).astype(BF16)
        o = _dot(y_ref[...], w_ref[...])
        gx = gx_ref[0:1, :]
        xnew = x_ref[...] + gx * o
        r2 = lax.rsqrt(jnp.mean(xnew * xnew, axis=-1, keepdims=True) + NORM_EPS)
        xh = xnew * r2
        fg = fg_ref[...]
        err = xh * fg - t_ref[...]
        loss_ref[...] += 0.5 * jnp.sum(jnp.mean(err * err, axis=-1, keepdims=True), axis=0, keepdims=True)
        dout = err * (1.0 / D)
        dfg_ref[...] += jnp.sum(dout * xh, axis=0, keepdims=True)
        dxh = dout * fg
        dxn = r2 * (dxh - xh * jnp.mean(dxh * xh, axis=-1, keepdims=True))
        dxn_ref[...] = dxn
        dgx_ref[...] += jnp.sum(dxn * o, axis=0, keepdims=True)
        do = (dxn * gx).astype(BF16)
        do_ref[...] = do
        dy = _dot_nt(do, w_ref[...])
        dy1 = dy[:, 0:D]
        dy2 = dy[:, D:2 * D]
        dga_ref[...] = (dy1 * yl * dsa).astype(BF16)
        dgb_ref[...] = (dy2 * ysv * dsb).astype(BF16)
        dys_ref[...] = dy2 * sb
        yl_s[...] = dy1 * sa
        for h in range(NH):
            dyl_ref[_zrows(h, T), :] = yl_s[:, HD * h:HD * h + HD]

    row = pl.BlockSpec((T, D), lambda i: (i, 0))
    vec = pl.BlockSpec((1, D), lambda i: (0, 0))
    in_specs = [ZT, ZT, row, row, row, row, row, pl.BlockSpec((8, D), lambda i: (0, 2)), vec,
                pl.BlockSpec((2 * D, D), lambda i: (0, 0))]
    out_shape = (jax.ShapeDtypeStruct((1, 1), F32), jax.ShapeDtypeStruct((1, D), F32), jax.ShapeDtypeStruct((1, D), F32),
                 jax.ShapeDtypeStruct((lx, D), F32), jax.ShapeDtypeStruct((lx, 2 * D), BF16),
                 jax.ShapeDtypeStruct((lx, D), BF16), jax.ShapeDtypeStruct((lx, D), BF16),
                 jax.ShapeDtypeStruct((lx, D), BF16), jax.ShapeDtypeStruct((lx * NH, HD), F32),
                 jax.ShapeDtypeStruct((lx, D), F32))
    out_specs = (pl.BlockSpec((1, 1), lambda i: (0, 0)), vec, vec, row, pl.BlockSpec((T, 2 * D), lambda i: (i, 0)),
                 row, row, row, ZT, row)
    return pl.pallas_call(
        body, name="out_fwd_bwd", grid=(n,), out_shape=out_shape, in_specs=in_specs, out_specs=out_specs,
        scratch_shapes=[pltpu.VMEM((T, D), F32)],
        compiler_params=_cp(1, vmem_mb=56),
    )(hf_z, hb_z, ga, gb, ys, x, tgt, mods, final_g, w_out_full)


def _lru_bwd(xc_z, dy_up, dy_dn, hf_z, hb_z, af_z, ab_z, gf, gb, s_f, s_b, wcat, lamcat, dw0, db0, dl0, name):
    lx = xc_z.shape[0] // NH
    n = lx // T

    def body(xc_u, dy_u, hb_ref, hbn_ref, ab_ref, gb_ref, xc_d, dy_d, hf_ref, hfp_ref, af_ref, gf_ref,
             sf_ref, sb_ref, w_ref, lam_ref, dw0_ref, db0_ref, dl0_ref,
             dxcb_ref, dxcf_ref, dw_ref, db_ref, dl_ref, fu, fd, lb_s, lf_s, q_u, q_d, pf_s, pb_s, dpre_s, carry):
        i = pl.program_id(0)

        @pl.when(i == 0)
        def _():
            dw_ref[...] = dw0_ref[...]
            db_ref[...] = db0_ref[...]
            dl_ref[...] = dl0_ref[...]
            carry[...] = jnp.zeros_like(carry)

        _scan_tile([(ab_ref, dy_u, lb_s, q_u, False), (af_ref, dy_d, lf_s, q_d, True)], True, carry)
        fu[...] = carry[0]
        fd[...] = carry[1]
        pb_s[pl.ds(0, T * NH), :] = hb_ref[...]
        pb_s[pl.ds(T * NH, NH), :] = jnp.where(i == n - 1, sb_ref[...], hbn_ref[pl.ds(0, NH), :])
        pf_s[pl.ds(0, NH), :] = jnp.where(i == n - 1, sf_ref[...], hfp_ref[pl.ds(7 * NH, NH), :])
        pf_s[pl.ds(NH, T * NH), :] = hf_ref[...]
        sides = ((1, xc_u, lb_s, pb_s, NH, ab_ref, gb_ref, dxcb_ref), (0, xc_d, lf_s, pf_s, 0, af_ref, gf_ref, dxcf_ref))
        for d, xc_ref, adj_s, prev_s, prev_off, a_ref, g_ref, dxc_ref in sides:
            wcols = slice(256 * d, 256 * d + 256)
            for h in range(NH):
                xch = xc_ref[_zrows(h, T), :]
                xcb = xch.astype(BF16)
                r, gi, mult = (g_ref[:, q * D + HD * h:q * D + HD * h + HD] for q in range(3))
                a = a_ref[_zrows(h, T), :]
                lam = lam_ref[h:h + 1, HD * d:HD * d + HD]
                sp = _softplus(-lam)
                du = adj_s[_zrows(h, T), :]
                da = du * prev_s[pl.ds(prev_off + h, T, stride=NH), :]
                dgi = du * mult * xch
                dmult = du * gi * xch
                dla = da * a - dmult * (a * a) / mult
                dr = dla * ((-LRU_C) * sp)
                dsp = jnp.sum(dla * ((-LRU_C) * r), axis=0, keepdims=True)
                dl_ref[h:h + 1, HD * d:HD * d + HD] += dsp * (-_sigmoid(-lam))
                dpre_s[:, 0:HD] = dr * r * (1.0 - r)
                dpre_s[:, HD:2 * HD] = dgi * gi * (1.0 - gi)
                dpre = dpre_s[...]
                dpb = dpre.astype(BF16)
                dw_ref[h, :, wcols] += _dot_tn(xcb, dpb)
                db_ref[h:h + 1, wcols] += jnp.sum(dpre, axis=0, keepdims=True)
                dxc_ref[_zrows(h, T), :] = du * mult * gi + _dot_nt(dpb, w_ref[h, :, wcols])

    full = lambda shape: pl.BlockSpec(shape, lambda i: (0,) * len(shape))
    wsp, bsp, lsp = full((NH, HD, 4 * HD)), full((NH, 4 * HD)), full((NH, 2 * HD))
    st = full((NH, HD))
    rev = lambda i: n - 1 - i
    up = ZT
    dn = pl.BlockSpec((T * NH, HD), lambda i: (rev(i), 0))
    nxt = _zhalo_specs(lx)[2]
    prv = pl.BlockSpec((8 * NH, HD), lambda i: (jnp.maximum(rev(i) * (T // 8) - 1, 0), 0))
    g_up = pl.BlockSpec((T, 3 * D), lambda i: (i, 0))
    g_dn = pl.BlockSpec((T, 3 * D), lambda i: (rev(i), 0))
    zs = jax.ShapeDtypeStruct((lx * NH, HD), F32)
    ss = jax.ShapeDtypeStruct((NH, HD), F32)
    zbuf = pltpu.VMEM((T * NH, HD), F32)
    zbuf1 = pltpu.VMEM(((T + 1) * NH, HD), F32)
    return pl.pallas_call(
        body, name=name, grid=(n,),
        out_shape=(zs, zs, jax.ShapeDtypeStruct((NH, HD, 4 * HD), F32), jax.ShapeDtypeStruct((NH, 4 * HD), F32),
                   jax.ShapeDtypeStruct((NH, 2 * HD), F32), ss, ss),
        in_specs=[up, up, up, nxt, up, g_up, dn, dn, dn, prv, dn, g_dn, st, st, wsp, lsp, wsp, bsp, lsp],
        out_specs=(up, dn, wsp, bsp, lsp, st, st),
        scratch_shapes=[zbuf, zbuf, zbuf, zbuf, zbuf1, zbuf1, pltpu.VMEM((T, 2 * HD), F32), pltpu.VMEM((2, NH, HD), F32)],
        compiler_params=_cp(1, vmem_mb=56),
    )(xc_z, dy_up, hb_z, hb_z, ab_z, gb, xc_z, dy_dn, hf_z, hf_z, af_z, gf, s_f, s_b, wcat, lamcat, dw0, db0, dl0)


def _conv_bwd(dxc_a, dxc_b, xa_z, conv_wz, dcw0, dcb0, name):
    lx = dxc_a.shape[0] // NH
    n = lx // T

    def body(dm_a, dp_a, dn_a, dm_b, dp_b, dn_b, xa_ref, cw, dcw0_ref, dcb0_ref, dxa_ref, dcw_ref, dcb_ref, pad, dxa_s):
        i = pl.program_id(0)

        @pl.when(i == 0)
        def _():
            dcw_ref[...] = dcw0_ref[...]
            dcb_ref[...] = dcb0_ref[...]

        pmask = jnp.where(i == 0, 0.0, 1.0)
        nmask = jnp.where(i == n - 1, 0.0, 1.0)
        pad[pl.ds(0, 8 * NH), :] = (dp_a[...] + dp_b[...]) * pmask
        pad[pl.ds(8 * NH, T * NH), :] = dm_a[...] + dm_b[...]
        pad[pl.ds((T + 8) * NH, 8 * NH), :] = (dn_a[...] + dn_b[...]) * nmask

        def chunk(ci, carry):
            base = pl.multiple_of(ci * (CONV_CHUNK * NH), CONV_CHUNK * NH)
            xav = xa_ref[pl.ds(base, CONV_CHUNK * NH), :].reshape(CONV_CHUNK, NH, HD)
            acc = None
            for k in range(4):
                sl = pad[pl.ds(base + (9 - k) * NH, CONV_CHUNK * NH), :].reshape(CONV_CHUNK, NH, HD)
                term = sl * cw[k][None]
                acc = term if acc is None else acc + term
                dcw_ref[k] += jnp.sum(sl * xav, axis=0)
                if k == 1:
                    dcb_ref[...] += jnp.sum(sl, axis=0)
            dxa_s[pl.ds(base, CONV_CHUNK * NH), :] = acc.reshape(CONV_CHUNK * NH, HD)
            return carry
        lax.fori_loop(0, T // CONV_CHUNK, chunk, 0)
        for h in range(NH):
            dxa_ref[:, HD * h:HD * h + HD] = dxa_s[_zrows(h, T), :].astype(BF16)

    full = lambda shape: pl.BlockSpec(shape, lambda i: (0,) * len(shape))
    return pl.pallas_call(
        body, name=name, grid=(n,),
        out_shape=(jax.ShapeDtypeStruct((lx, D), BF16), jax.ShapeDtypeStruct((4, NH, HD), F32),
                   jax.ShapeDtypeStruct((NH, HD), F32)),
        in_specs=_zhalo_specs(lx) * 2 + [ZT, full((4, NH, HD)), full((4, NH, HD)), full((NH, HD))],
        out_specs=(pl.BlockSpec((T, D), lambda i: (i, 0)), full((4, NH, HD)), full((NH, HD))),
        scratch_shapes=[pltpu.VMEM(((T + 16) * NH, HD), F32), pltpu.VMEM((T * NH, HD), F32)],
        compiler_params=_cp(1, vmem_mb=48),
    )(dxc_a, dxc_a, dxc_a, dxc_b, dxc_b, dxc_b, xa_z, conv_wz, dcw0, dcb0)


def _proj_bwd(dzs, x, dxn, mods, mrow, norm_g, w_full, dng0, name, sgu=None):
    lx = x.shape[0]
    n = lx // T
    nz = len(dzs)
    has_x = dxn is not None
    wks = ([0, 1, 4, 2, 3] if sgu is not None else list(range(nz)))

    def body(*refs):
        it = iter(refs)
        take = lambda m: [next(it) for _ in range(m)]
        dz_refs, w_refs = take(nz), take(len(wks))
        x_ref, sc_ref, ng_ref, dng0_ref = take(4)
        dxn_ref = take(1)[0] if has_x else None
        if sgu is not None:
            u_ref, v_ref, dy_ref, g_ref, b_ref, sw_ref, bt_ref = take(7)
        gx_ref = take(1)[0] if has_x else None
        dng_ref, dsc_ref, dsh_ref = take(3)
        if sgu is not None:
            du_ref, dv_ref, dws_ref, dbs_ref, dlg_ref, dlb_ref, mixed_s, dvn_s = take(8)
        i = pl.program_id(0)

        @pl.when(i == 0)
        def _():
            dng_ref[...] = dng0_ref[...]
            dsc_ref[...] = jnp.zeros_like(dsc_ref)
            dsh_ref[...] = jnp.zeros_like(dsh_ref)
            if sgu is not None:
                for acc in (dws_ref, dbs_ref, dlg_ref, dlb_ref):
                    acc[...] = jnp.zeros_like(acc)

        dhn = _dot_nt(dz_refs[0][...], w_refs[0][...])
        for k in range(1, nz):
            dhn = dhn + _dot_nt(dz_refs[k][...], w_refs[k][...])
        if sgu is not None:
            for ch in range(T // HD):
                rows = slice(HD * ch, HD * ch + HD)
                du, dv = _sgu_bwd_chunk(u_ref[rows, :], v_ref[rows, :], dy_ref[rows, :], g_ref[...], b_ref[...],
                                        sw_ref, bt_ref, mixed_s, dvn_s, dws_ref, dbs_ref, dlg_ref, dlb_ref)
                du_ref[rows, :] = du
                dv_ref[rows, :] = dv
            dhn = dhn + _dot_nt(du_ref[...], w_refs[nz][...]) + _dot_nt(dv_ref[...], w_refs[nz + 1][...])
        xv = x_ref[...]
        r = lax.rsqrt(jnp.mean(xv * xv, axis=-1, keepdims=True) + NORM_EPS)
        xn = xv * r
        ng = ng_ref[...]
        sc1 = 1.0 + sc_ref[mrow:mrow + 1, :]
        t = dhn * xn
        dng_ref[...] += jnp.sum(t * sc1, axis=0, keepdims=True)
        dsc_ref[...] += jnp.sum(t * ng, axis=0, keepdims=True)
        dsh_ref[...] += jnp.sum(dhn, axis=0, keepdims=True)
        if has_x:
            dxh = dhn * (ng * sc1)
            gx_ref[...] = dxn_ref[...] + r * (dxh - xn * jnp.mean(dxh * xn, axis=-1, keepdims=True))

    row = pl.BlockSpec((T, D), lambda i: (i, 0))
    vec = pl.BlockSpec((1, D), lambda i: (0, 0))
    in_specs = [row] * nz + [pl.BlockSpec((D, D), lambda i, k=k: (0, k)) for k in wks]
    in_specs += [row, pl.BlockSpec((8, D), lambda i: (0, 1)), vec, vec]
    args = list(dzs) + [w_full] * len(wks) + [x, mods, norm_g, dng0]
    vs = jax.ShapeDtypeStruct((1, D), F32)
    out_shape, out_specs = (vs, vs, vs), (vec, vec, vec)
    scratch = []
    if has_x:
        in_specs.append(row)
        args.append(dxn)
        out_shape = (jax.ShapeDtypeStruct((lx, D), F32),) + out_shape
        out_specs = (row,) + out_specs
    if sgu is not None:
        wsp = pl.BlockSpec((NH, HD, HD), lambda i: (0, 0, 0))
        bsp = pl.BlockSpec((NH, HD), lambda i: (0, 0))
        in_specs += [row, row, row, vec, vec, wsp, pl.BlockSpec((HD, NH), lambda i: (0, 0))]
        args += list(sgu)
        zb = jax.ShapeDtypeStruct((lx, D), BF16)
        out_shape += (zb, zb, jax.ShapeDtypeStruct((NH, HD, HD), F32), jax.ShapeDtypeStruct((NH, HD), F32), vs, vs)
        out_specs += (row, row, wsp, bsp, vec, vec)
        scratch = [pltpu.VMEM((HD, D), F32), pltpu.VMEM((HD, D), F32)]
    return pl.pallas_call(
        body, name=name, grid=(n,), out_shape=out_shape, in_specs=in_specs, out_specs=out_specs,
        scratch_shapes=scratch, compiler_params=_cp(1, vmem_mb=56),
    )(*args)


def _adam_math(w, g, m, v):
    m = ADAM_B1 * m + (1.0 - ADAM_B1) * g
    v = ADAM_B2 * v + (1.0 - ADAM_B2) * (g * g)
    m_hat = m / (1.0 - ADAM_B1 ** ADAM_STEP)
    v_hat = v / (1.0 - ADAM_B2 ** ADAM_STEP)
    delta = -ADAM_LR * (m_hat / (jnp.sqrt(v_hat) + ADAM_EPS) + ADAM_WD * w)
    return delta, m, v


def _adam_big(w, g, m, v, name):
    rows, cols = w.shape
    tr = 256

    def body(w_ref, g_ref, m_ref, v_ref, d_o, m_o, v_o):
        d, mm, vv = _adam_math(w_ref[...], g_ref[...], m_ref[...], v_ref[...])
        d_o[...] = d
        m_o[...] = mm
        v_o[...] = vv

    blk = pl.BlockSpec((tr, cols), lambda i: (i, 0))
    s = jax.ShapeDtypeStruct((rows, cols), F32)
    return pl.pallas_call(
        body, name=name, grid=(rows // tr,), out_shape=(s, s, s), in_specs=[blk] * 4, out_specs=(blk,) * 3,
        compiler_params=_cp(1, vmem_mb=48),
    )(w, g, m, v)


def _adam_small(items):
    ni = len(items)

    def body(*refs):
        ins, outs = refs[:4 * ni], refs[4 * ni:7 * ni]
        bufs_in, bufs_out = refs[7 * ni:11 * ni], refs[11 * ni:14 * ni]
        sem_in, sem_out = refs[14 * ni], refs[14 * ni + 1]
        loads = [pltpu.make_async_copy(ins[q], bufs_in[q], sem_in.at[q]) for q in range(4 * ni)]
        for cp in loads:
            cp.start()
        stores = []
        for k in range(ni):
            for q in range(4):
                loads[4 * k + q].wait()
            w_b, g_b, m_b, v_b = bufs_in[4 * k:4 * k + 4]
            res = _adam_math(w_b[...], g_b[...], m_b[...], v_b[...])
            for q in range(3):
                bufs_out[3 * k + q][...] = res[q]
                cp = pltpu.make_async_copy(bufs_out[3 * k + q], outs[3 * k + q], sem_out.at[3 * k + q])
                cp.start()
                stores.append(cp)
        for cp in stores:
            cp.wait()

    flat = [a for it in items for a in it]
    out_shape = tuple(jax.ShapeDtypeStruct(it[0].shape, F32) for it in items for _ in range(3))
    scratch = [pltpu.VMEM(a.shape, F32) for a in flat] + [pltpu.VMEM(s.shape, F32) for s in out_shape]
    scratch += [pltpu.SemaphoreType.DMA((4 * ni,)), pltpu.SemaphoreType.DMA((3 * ni,))]
    res = pl.pallas_call(
        body, name="adam_small", out_shape=out_shape, in_specs=[HBM] * (4 * ni), out_specs=(HBM,) * (3 * ni),
        scratch_shapes=scratch, compiler_params=_cp(vmem_mb=40),
    )(*flat)
    return [tuple(res[3 * k:3 * k + 3]) for k in range(ni)]


def kernel(x, c, ctx, c_ctx, ada_w, ada_b, norm_g, w_in, conv_w, conv_b, lru_wa, lru_ba, lru_wx, lru_bx, lru_lambda, sgu_ln_g, sgu_ln_b, sgu_w, sgu_b, w_out, final_g, loss_target, m_c_ctx, m_ada_w, m_ada_b, m_norm_g, m_w_in, m_conv_w, m_conv_b, m_lru_wa, m_lru_ba, m_lru_wx, m_lru_bx, m_lru_lambda, m_sgu_ln_g, m_sgu_ln_b, m_sgu_w, m_sgu_b, m_w_out, m_final_g, v_c_ctx, v_ada_w, v_ada_b, v_norm_g, v_w_in, v_conv_w, v_conv_b, v_lru_wa, v_lru_ba, v_lru_wx, v_lru_bx, v_lru_lambda, v_sgu_ln_g, v_sgu_ln_b, v_sgu_w, v_sgu_b, v_w_out, v_final_g):
    ix, iy, ic = lax.axis_index("x"), lax.axis_index("y"), lax.axis_index("c")
    chip = 2 * ix + iy
    dev = 2 * chip + ic
    lx = x.shape[1]
    lc = ctx.shape[1]

    smalls = jnp.concatenate([conv_w[0], lru_lambda[0], jnp.zeros((10, 256), F32)], axis=0)
    c_ctx2 = c_ctx.reshape(1, D)
    ada_b_j = lax.dynamic_slice(ada_b, (0, 768 * chip), (1, 768))
    mods, c_slots, sm_all, w_in_full, wo_land, ada_land = _gather_in(c, c_ctx2, ada_w[0], ada_b_j, w_in[0], w_out[0],
                                                                     smalls)
    wo_ss, wo_rs, ada_ss, ada_rs, wo_land, ada_land, token = _late_gather_start(wo_land, ada_land)
    mods = mods + token[0:1, 0:1]
    sm3 = sm_all.reshape(NCHIP, 16, 256)
    conv_w_full = sm3[:, 0:4, :].transpose(1, 0, 2).reshape(4, D)
    lam_full = sm3[:, 4:6, :].transpose(1, 0, 2).reshape(2, D)
    conv_wz = conv_w_full.reshape(4, NH, HD)
    conv_bz = conv_b.reshape(NH, HD)
    lamcat = lam_full.reshape(2, NH, HD).transpose(1, 0, 2).reshape(NH, 2 * HD)
    wa, wx, ba, bx = lru_wa[0], lru_wx[0], lru_ba[0], lru_bx[0]
    wcat = jnp.concatenate([wa[0], wx[0], wa[1], wx[1]], axis=-1).astype(BF16)
    bcat = jnp.concatenate([ba[0], bx[0], ba[1], bx[1]], axis=-1)
    sgu_wb = sgu_w[0].astype(BF16)
    sgu_bt = sgu_b[0].T
    final_g2 = final_g.reshape(1, D)

    zero_s = jnp.zeros((NH, HD), F32)
    hn_c, xa_c = _proj(ctx[0], mods, 1, norm_g, w_in_full, 1, "proj_ctx")
    xaz_c, xcz_c, af_c, ab_c, hf_c, hb_c, gf_c, gb_c, hf0, hb0 = _lru_fwd(xa_c, conv_wz, conv_bz, wcat, bcat, lamcat,
                                                                           zero_s, zero_s, "lru_fwd_ctx")

    hn, xa, ga, u, v, gb, ys = _proj(x[0], mods, 0, norm_g, w_in_full, 5, "proj",
                                     (sgu_ln_g, sgu_ln_b, sgu_wb, sgu_bt))
    xaz, xcz, af, ab, hf, hb, gf, gb_l, _, _ = _lru_fwd(xa, conv_wz, conv_bz, wcat, bcat, lamcat, hf0, hb0, "lru_fwd")

    w_out_full = _late_gather_wait(wo_land, wo_ss, wo_rs, "w_out", hf, "late_gather_wait_w_out")
    (loss_part, dfg, dgx, dxn, y, do, dga, dgb, dyl_z, dys) = _out_fwd_bwd(
        hf, hb, ga, gb, ys, x[0], loss_target[0], mods, final_g2, w_out_full)

    zw = jnp.zeros((NH, HD, 4 * HD), F32)
    zb = jnp.zeros((NH, 4 * HD), F32)
    zl = jnp.zeros((NH, 2 * HD), F32)
    dxc_b, dxc_f, dwc, dbc, dlc, dh0b, dh0f = _lru_bwd(xcz, dyl_z, dyl_z, hf, hb, af, ab, gf, gb_l, hf0, hb0,
                                                        wcat, lamcat, zw, zb, zl, "lru_bwd")
    dxa, dcw, dcb = _conv_bwd(dxc_b, dxc_f, xaz, conv_wz, jnp.zeros((4, NH, HD), F32), zero_s, "conv_bwd")

    zc = jnp.zeros((lc * NH, HD), F32)
    dhf_c = lax.dynamic_update_slice(zc, dh0f, ((lc - 1) * NH, 0))
    dhb_c = lax.dynamic_update_slice(zc, dh0b, (0, 0))
    dxc_bc, dxc_fc, dwc, dbc, dlc, _, _ = _lru_bwd(xcz_c, dhb_c, dhf_c, hf_c, hb_c, af_c, ab_c, gf_c, gb_c,
                                                    zero_s, zero_s, wcat, lamcat, dwc, dbc, dlc, "lru_bwd_ctx")
    dxa_c, dcw, dcb = _conv_bwd(dxc_bc, dxc_fc, xaz_c, conv_wz, dcw, dcb, "conv_bwd_ctx")

    grad_x, dng, dsc_x, dsh_x, du, dv, d_sgu_w, d_sgu_b, d_ln_g, d_ln_b = _proj_bwd(
        [dxa, dga, dgb], x[0], dxn, mods, 0, norm_g, w_in_full, jnp.zeros((1, D), F32), "proj_bwd",
        (u, v, dys, sgu_ln_g, sgu_ln_b, sgu_wb, sgu_bt))
    dzs = [dxa, dga, du, dv, dgb]
    dng, dsc_c, dsh_c = _proj_bwd([dxa_c], ctx[0], None, mods, 1, norm_g, w_in_full, dng, "proj_bwd_ctx")

    dmx = jnp.concatenate([dsh_x, dsc_x, dgx], axis=0)
    dmc = jnp.concatenate([dsh_c, dsc_c, jnp.zeros((1, D), F32)], axis=0)
    lp = loss_part[0, 0]
    lp1 = lax.reduce_precision(lp, 8, 7)
    lp2 = lax.reduce_precision(lp - lp1, 8, 7)
    lp3 = lax.reduce_precision(lp - lp1 - lp2, 8, 7)
    loss_row = jnp.pad(jnp.stack([lp1, lp2, lp3]).reshape(1, 3), ((0, 0), (0, D - 3)))
    slot = jnp.concatenate([dmx, loss_row], axis=0)
    slots = lax.dynamic_update_slice(jnp.zeros((32, D), F32), slot, (4 * dev, 0))
    vecs = jnp.concatenate([dfg, dng, dcb.reshape(1, D), d_ln_g, d_ln_b, dcw.reshape(4, D), dmc,
                            jnp.zeros((4, D), F32), slots], axis=0)
    d_sgu_w4 = d_sgu_w.reshape(4, 256, HD).transpose(1, 0, 2).reshape(256, 4 * HD)
    pad8 = lambda a: jnp.pad(a, ((0, 8 - a.shape[0]), (0, 4 * HD - a.shape[1])))
    pack = jnp.concatenate([dwc.reshape(NH * HD, 4 * HD), pad8(dbc), pad8(dlc), d_sgu_w4, pad8(d_sgu_b),
                            vecs.reshape(96, 4 * HD), jnp.zeros((8, 4 * HD), F32)], axis=0)
    g_w_in, g_w_out, tot = _grads_reduce(hn, dzs, hn_c, dxa_c, y, do, pack)

    g_wc = tot[0:1024].reshape(NH, HD, 4 * HD)
    g_bc = tot[1024:1032]
    g_lc = tot[1032:1040, 0:2 * HD]
    g_sgu_w = tot[1040:1296].reshape(256, 4, HD).transpose(1, 0, 2).reshape(NH, HD, HD)
    g_sgu_b = tot[1296:1304, 0:HD]
    tv = tot[1304:1400].reshape(48, D)
    g_final_g, g_norm_g, g_conv_b, g_ln_g, g_ln_b = tv[0:1], tv[1:2], tv[2:3], tv[3:4], tv[4:5]
    g_conv_w_full = tv[5:9]
    dmc_tot = tv[9:12].reshape(1, 3 * D)
    slots_all = tv[16:48].reshape(8, 4, D)
    dmx_all = slots_all[:, 0:3, :].reshape(8, 3 * D)
    c_all = c_slots.reshape(8, 8, D)[:, 0, :]
    g_lru_wa = jnp.stack([g_wc[:, :, 0:HD], g_wc[:, :, 2 * HD:3 * HD]])
    g_lru_wx = jnp.stack([g_wc[:, :, HD:2 * HD], g_wc[:, :, 3 * HD:4 * HD]])
    g_lru_ba = jnp.stack([g_bc[:, 0:HD], g_bc[:, 2 * HD:3 * HD]])
    g_lru_bx = jnp.stack([g_bc[:, HD:2 * HD], g_bc[:, 3 * HD:4 * HD]])
    g_lam_full = jnp.stack([g_lc[:, 0:HD], g_lc[:, HD:2 * HD]]).reshape(2, D)
    g_conv_w = lax.dynamic_slice(g_conv_w_full, (0, 256 * chip), (4, 256))
    g_lam = lax.dynamic_slice(g_lam_full, (0, 256 * chip), (2, 256))
    dmx_all_j = lax.dynamic_slice(dmx_all, (0, 768 * chip), (8, 768))
    dmc_j = lax.dynamic_slice(dmc_tot, (0, 768 * chip), (1, 768))
    ada_full = _late_gather_wait(ada_land, ada_ss, ada_rs, "ada_w", tot, "late_gather_wait_ada_w")
    g_ada_w, g_ada_b, g_c_ctx = _ada_bwd(c_all, dmx_all_j, dmc_j, dmx_all, dmc_tot, c_ctx2, ada_full)

    big = {
        "ada_w": _adam_big(ada_w[0], g_ada_w, m_ada_w[0], v_ada_w[0], "adam_ada_w"),
        "w_in": _adam_big(w_in[0], g_w_in, m_w_in[0], v_w_in[0], "adam_w_in"),
        "w_out": _adam_big(w_out[0], g_w_out, m_w_out[0], v_w_out[0], "adam_w_out"),
    }
    small_in = {
        "c_ctx": (c_ctx, g_c_ctx, m_c_ctx, v_c_ctx, (1, D)),
        "ada_b": (ada_b, g_ada_b, m_ada_b, v_ada_b, (1, 3 * D)),
        "norm_g": (norm_g, g_norm_g, m_norm_g, v_norm_g, (1, D)),
        "conv_w": (conv_w, g_conv_w, m_conv_w, v_conv_w, (4, 256)),
        "conv_b": (conv_b, g_conv_b, m_conv_b, v_conv_b, (1, D)),
        "lru_wa": (lru_wa, g_lru_wa, m_lru_wa, v_lru_wa, (2 * NH * HD, HD)),
        "lru_ba": (lru_ba, g_lru_ba, m_lru_ba, v_lru_ba, (2 * NH, HD)),
        "lru_wx": (lru_wx, g_lru_wx, m_lru_wx, v_lru_wx, (2 * NH * HD, HD)),
        "lru_bx": (lru_bx, g_lru_bx, m_lru_bx, v_lru_bx, (2 * NH, HD)),
        "lru_lambda": (lru_lambda, g_lam, m_lru_lambda, v_lru_lambda, (2, 256)),
        "sgu_ln_g": (sgu_ln_g, g_ln_g, m_sgu_ln_g, v_sgu_ln_g, (1, D)),
        "sgu_ln_b": (sgu_ln_b, g_ln_b, m_sgu_ln_b, v_sgu_ln_b, (1, D)),
        "sgu_w": (sgu_w, g_sgu_w, m_sgu_w, v_sgu_w, (NH * HD, HD)),
        "sgu_b": (sgu_b, g_sgu_b, m_sgu_b, v_sgu_b, (NH, HD)),
        "final_g": (final_g, g_final_g, m_final_g, v_final_g, (1, D)),
    }
    names_small = list(small_in)
    res_small = _adam_small([tuple(a.reshape(small_in[k][4]) for a in small_in[k][:4]) for k in names_small])
    full_shapes = {"ada_w": ada_w.shape, "w_in": w_in.shape, "w_out": w_out.shape}
    grads, deltas, new_m, new_v = {}, {}, {}, {}
    for k in ("ada_w", "w_in", "w_out"):
        g = {"ada_w": g_ada_w, "w_in": g_w_in, "w_out": g_w_out}[k]
        grads[k] = g.reshape(full_shapes[k])
        deltas[k], new_m[k], new_v[k] = (a.reshape(full_shapes[k]) for a in big[k])
    for k, res in zip(names_small, res_small):
        shape = small_in[k][0].shape
        grads[k] = small_in[k][1].reshape(shape)
        deltas[k], new_m[k], new_v[k] = (a.reshape(shape) for a in res)

    loss = jnp.sum(slots_all[:, 3, 0:3])
    order = ["c_ctx", "ada_w", "ada_b", "norm_g", "w_in", "conv_w", "conv_b", "lru_wa", "lru_ba", "lru_wx", "lru_bx",
             "lru_lambda", "sgu_ln_g", "sgu_ln_b", "sgu_w", "sgu_b", "w_out", "final_g"]
    return (loss, grad_x.reshape(x.shape), *[grads[k] for k in order], *[deltas[k] for k in order],
            *[new_m[k] for k in order], *[new_v[k] for k in order])
```

```python
import functools

import jax
import jax.numpy as jnp
from jax import lax
from jax.experimental import pallas as pl
from jax.experimental.pallas import tpu as pltpu

F32 = jnp.float32
BF16 = jnp.bfloat16

D = 1024
NH = 8
HD = 128
NCHIP = 4
T = 256
NORM_EPS = 1e-6
LN_EPS = 1e-5
LRU_C = 8.0
ADAM_LR = 0.001
ADAM_B1 = 0.9
ADAM_B2 = 0.999
ADAM_EPS = 1e-08
ADAM_WD = 0.01
ADAM_STEP = 10

VMEM = pl.BlockSpec(memory_space=pltpu.VMEM)
ANY = pl.BlockSpec(memory_space=pl.ANY)
MESH = pl.DeviceIdType.MESH


def _cp(n_grid=0, vmem_mb=None):
    kw = {}
    if n_grid:
        kw["dimension_semantics"] = ("arbitrary",) * n_grid
    if vmem_mb:
        kw["vmem_limit_bytes"] = vmem_mb << 20
    return pltpu.CompilerParams(**kw)


def _sigmoid(x):
    return 1.0 / (1.0 + jnp.exp(-x))


def _silu_and_grad(x):
    s = _sigmoid(x)
    return x * s, s * (1.0 + x * (1.0 - s))


_GELU_K = 0.7978845608028654
_GELU_C = 0.044715


def _gelu_and_grad(x):
    x2 = x * x
    th = jnp.tanh(_GELU_K * (x + _GELU_C * x * x2))
    g = 0.5 * x * (1.0 + th)
    dg = 0.5 * (1.0 + th) + 0.5 * x * (1.0 - th * th) * (_GELU_K * (1.0 + 3.0 * _GELU_C * x2))
    return g, dg


def _softplus(x):
    return jnp.maximum(x, 0.0) + jnp.log1p(jnp.exp(-jnp.abs(x)))


def _lru_gate(pre, lam_row, d, off=None):
    off = 256 * d if off is None else off
    r = _sigmoid(pre[:, off:off + HD])
    gi = _sigmoid(pre[:, off + HD:off + 2 * HD])
    lam = lam_row[:, HD * d:HD * d + HD]
    sp = _softplus(-lam)
    la = (-LRU_C) * r * sp
    a = jnp.exp(la)
    x2 = 2.0 * la
    m2 = jnp.where(x2 > -1e-3, -x2 * (1.0 + 0.5 * x2), 1.0 - a * a)
    mult = jnp.sqrt(m2)
    return r, gi, lam, sp, a, mult


def _dot(a, b):
    return jnp.dot(a, b, preferred_element_type=F32)


def _dot_tn(a, b):
    return lax.dot_general(a, b, (((0,), (0,)), ((), ())), preferred_element_type=F32)


def _dot_nt(a, b):
    return lax.dot_general(a, b, (((1,), (1,)), ((), ())), preferred_element_type=F32)


def _mo(v, m):
    return v if isinstance(v, int) else pl.multiple_of(v, m)


def _zrows(h, n):
    return pl.ds(h, n, stride=NH)


def _gather_in(c, c_ctx, ada_w, ada_b_j, w_in, w_out, smalls):
    nch = [1, 4]
    wrows = lambda cc, q: (pl.ds(_mo(512 * cc, 16), 512) if q is None
                           else pl.ds(_mo(512 * cc + (512 // nch[1]) * q, 16), 512 // nch[1]))
    specs = [
        ((64, 256), F32, lambda r, jj, cc, q=None: r.at[pl.ds(_mo(16 * jj + 8 * cc, 8), 8), :]),
        ((D, 5120), BF16, lambda r, jj, cc, q=None: r.at[wrows(cc, q), pl.ds(_mo(1280 * jj, 128), 1280)]),
    ]
    halves = [lambda r, cc, q=None: r.at[pl.ds(_mo(8 * cc, 8), 8), :],
              lambda r, cc, q=None: r.at[wrows(cc, q), :]]
    na = len(specs)
    sem_base = [0, 6 * nch[0]]
    sidx = lambda a, q, k: sem_base[a] + 6 * q + k
    n_tiny = 6 * sum(nch)
    n_sem = n_tiny + 10

    def body(c_ref, cc_ref, ada_ref, adab_ref, win_ref, wout_ref, sm_ref,
             mods_o, call_o, sm_o, win_o, wol_o, adal_o, s_win, s_ada, s_wout, f_win, f_ada, f_wout, cslot, lhs, mbuf,
             send_sems, recv_sems, local_sems, load_sems):
        x, y, c = lax.axis_index("x"), lax.axis_index("y"), lax.axis_index("c")
        j = 2 * x + y
        dev = 2 * j + c
        sib = (x, y, 1 - c)
        chips = [(1 - x, y), (x, 1 - y), (1 - x, 1 - y)]
        cj = [2 * cx + cy for cx, cy in chips]
        outs = [sm_o, win_o]
        srcs = [sm_ref, s_win]

        def copy(idx, src, dst, to):
            return pltpu.make_async_remote_copy(src_ref=src, dst_ref=dst, send_sem=send_sems.at[idx],
                                                recv_sem=recv_sems.at[idx], device_id=to, device_id_type=MESH)

        sends = []

        def start(cp):
            cp.start()
            sends.append(cp)

        cslot[...] = jnp.zeros_like(cslot)
        cslot[0:1, :] = c_ref[...]
        my_slot = pl.ds(_mo(8 * dev, 8), 8)
        others = [sib] + [(*chips[k], c) for k in range(3)] + [(*chips[k], 1 - c) for k in range(3)]
        other_dev = [dev + 1 - 2 * c] + [2 * cj[k] + c for k in range(3)] + [2 * cj[k] + 1 - c for k in range(3)]
        base = n_tiny
        for r in range(7):
            start(copy(base + r, cslot, call_o.at[my_slot, :], others[r]))
        call_o[my_slot, :] = cslot[...]

        crow = 512 // nch[1]
        loads = []
        for cc in (c, 1 - c):
            for q in range(nch[1]):
                rows = pl.ds(_mo(512 * cc + crow * q, 16), crow)
                loads.append(pltpu.make_async_copy(win_ref.at[rows, :], f_win.at[rows, :], load_sems.at[len(loads)]))
        loads.append(pltpu.make_async_copy(ada_ref, f_ada, load_sems.at[len(loads)]))
        loads.append(pltpu.make_async_copy(wout_ref, f_wout, load_sems.at[len(loads)]))
        for ld in loads:
            ld.start()
        for k in range(2):
            start(copy(sidx(0, 0, k), halves[0](srcs[0], c), specs[0][2](outs[0], j, c), (*chips[k], c)))
        for q in range(nch[1]):
            loads[q].wait()
            rows = pl.ds(_mo(512 * c + crow * q, 16), crow)
            s_win[rows, :] = f_win[rows, :].astype(BF16)
            for k in range(2):
                start(copy(sidx(1, q, k), halves[1](s_win, c, q), specs[1][2](win_o, j, c, q), (*chips[k], c)))
        for q in range(nch[1]):
            loads[nch[1] + q].wait()
            rows = pl.ds(_mo(512 * (1 - c) + crow * q, 16), crow)
            s_win[rows, :] = f_win[rows, :].astype(BF16)
        local = []
        for a in range(na):
            for cc in range(2):
                lc = pltpu.make_async_copy(halves[a](srcs[a], cc), specs[a][2](outs[a], j, cc), local_sems.at[2 * a + cc])
                lc.start()
                local.append(lc)
        loads[2 * nch[1]].wait()
        s_ada[...] = f_ada[...].astype(BF16)
        loads[2 * nch[1] + 1].wait()
        s_wout[...] = f_wout[...].astype(BF16)
        for q, (src, dst) in enumerate([(s_wout, wol_o.at[pl.ds(_mo(512 * j, 16), 512), :]),
                                        (s_ada, adal_o.at[:, pl.ds(_mo(768 * j, 128), 768)])]):
            lc = pltpu.make_async_copy(src, dst, local_sems.at[2 * na + q])
            lc.start()
            local.append(lc)

        for r in range(7):
            slot = call_o.at[pl.ds(_mo(8 * other_dev[r], 8), 8), :]
            copy(base + r, slot, slot, sib).wait_recv()
        lhs[...] = jnp.zeros_like(lhs)
        for b in range(8):
            cv = call_o[8 * b:8 * b + 1, :]
            lhs[b:b + 1, :] = cv * _sigmoid(cv)
        cv = cc_ref[...]
        lhs[8:9, :] = cv * _sigmoid(cv)
        mbuf[j] = _dot(lhs[...].astype(BF16), s_ada[...]) + adab_ref[...]
        for k in range(3):
            start(copy(base + 7 + k, mbuf.at[j], mbuf.at[j], (*chips[k], c)))
        for k in range(3):
            copy(base + 7 + k, mbuf.at[cj[k]], mbuf.at[cj[k]], sib).wait_recv()
        mods_o[...] = jnp.zeros_like(mods_o)
        for jj in range(NCHIP):
            mods_o[0:1, 768 * jj:768 * jj + 768] = mbuf[jj, pl.ds(dev, 1), :]
            mods_o[1:2, 768 * jj:768 * jj + 768] = mbuf[jj, 8:9, :]

        kx = [1 - x, x, 1 - x]
        ky = [y, 1 - y, 1 - y]
        pick = lambda k, lst: jnp.where(k == 0, lst[0], jnp.where(k == 1, lst[1], lst[2]))
        for a in range(na):
            for q in range(nch[a]):
                for step, k in enumerate([c, 1 - c]):
                    reg = specs[a][2](outs[a], pick(k, cj), c, q)
                    copy(sidx(a, q, k), reg, reg, sib).wait_recv()
                    if step == 0:
                        start(copy(sidx(a, q, 2), reg, reg, (pick(1 - c, kx), pick(1 - c, ky), c)))
                    start(copy(sidx(a, q, 3 + k), reg, reg, sib))
        for a in range(na):
            for q in range(nch[a]):
                reg = specs[a][2](outs[a], cj[2], c, q)
                copy(sidx(a, q, 2), reg, reg, sib).wait_recv()
                start(copy(sidx(a, q, 5), reg, reg, sib))
        for a in range(na):
            for q in range(nch[a]):
                for k in range(3):
                    reg = specs[a][2](outs[a], cj[k], 1 - c, q)
                    copy(sidx(a, q, 3 + k), reg, reg, sib).wait_recv()
        for cp in sends:
            cp.wait_send()
        for lc in local:
            lc.wait()

    out_shape = (jax.ShapeDtypeStruct((8, 3 * D), F32), jax.ShapeDtypeStruct((64, D), F32),
                 jax.ShapeDtypeStruct(specs[0][0], F32), jax.ShapeDtypeStruct(specs[1][0], BF16),
                 jax.ShapeDtypeStruct((2048, D), BF16), jax.ShapeDtypeStruct((D, 3 * D), BF16))
    return pl.pallas_call(
        body, name="gather_in", out_shape=out_shape,
        in_specs=[VMEM, VMEM, ANY, VMEM, ANY, ANY, VMEM], out_specs=(VMEM, VMEM, VMEM, ANY, ANY, ANY),
        scratch_shapes=[pltpu.VMEM((D, 1280), BF16), pltpu.VMEM((D, 768), BF16), pltpu.VMEM((512, D), BF16),
                        pltpu.VMEM((D, 1280), F32), pltpu.VMEM((D, 768), F32), pltpu.VMEM((512, D), F32),
                        pltpu.VMEM((8, D), F32), pltpu.VMEM((16, D), F32), pltpu.VMEM((NCHIP, 16, 768), F32),
                        pltpu.SemaphoreType.DMA((n_sem,)), pltpu.SemaphoreType.DMA((n_sem,)),
                        pltpu.SemaphoreType.DMA((2 * na + 2,)), pltpu.SemaphoreType.DMA((2 * nch[1] + 2,))],
        compiler_params=_cp(vmem_mb=56),
    )(c, c_ctx, ada_w, ada_b_j, w_in, w_out, smalls)


HBM = pl.BlockSpec(memory_space=pltpu.HBM)
SEM = pl.BlockSpec(memory_space=pltpu.SEMAPHORE)


def _late_gather_regions(x, y, c):
    chips = [(1 - x, y), (x, 1 - y), (1 - x, 1 - y)]
    wo_reg = lambda r, jj, cc: r.at[pl.ds(_mo(512 * jj + 256 * cc, 16), 256), :]
    ada_reg = lambda r, jj, cc: r.at[pl.ds(_mo(512 * cc, 16), 512), pl.ds(_mo(768 * jj, 128), 768)]
    return chips, wo_reg, ada_reg


def _late_gather_start(wo_land, ada_land):
    def body(wol_ref, adal_ref, wo_ss, wo_rs, ada_ss, ada_rs, wol_thru, adal_thru, token):
        x, y, c = lax.axis_index("x"), lax.axis_index("y"), lax.axis_index("c")
        j = 2 * x + y
        chips, wo_reg, ada_reg = _late_gather_regions(x, y, c)
        for k in range(3):
            for cc in range(2):
                pltpu.make_async_remote_copy(src_ref=wo_reg(wol_ref, j, c), dst_ref=wo_reg(wol_ref, j, c),
                                             send_sem=wo_ss.at[2 * k + cc], recv_sem=wo_rs.at[2 * k + c],
                                             device_id=(*chips[k], cc), device_id_type=MESH).start()
        for k in range(3):
            for cc in range(2):
                pltpu.make_async_remote_copy(src_ref=ada_reg(adal_ref, j, c), dst_ref=ada_reg(adal_ref, j, c),
                                             send_sem=ada_ss.at[2 * k + cc], recv_sem=ada_rs.at[2 * k + c],
                                             device_id=(*chips[k], cc), device_id_type=MESH).start()
        token[...] = jnp.zeros_like(token)

    sems = pltpu.SemaphoreType.DMA((6,))
    return pl.pallas_call(
        body, name="late_gather_start",
        out_shape=(sems, sems, sems, sems, pltpu.HBM(wo_land.shape, BF16), pltpu.HBM(ada_land.shape, BF16),
                   jax.ShapeDtypeStruct((8, 128), F32)),
        in_specs=(HBM, HBM), out_specs=(SEM, SEM, SEM, SEM, HBM, HBM, VMEM), input_output_aliases={0: 4, 1: 5},
        compiler_params=pltpu.CompilerParams(has_side_effects=pltpu.SideEffectType.DATAFLOW_SIDE_EFFECTING),
    )(pltpu.with_memory_space_constraint(wo_land, pltpu.HBM), pltpu.with_memory_space_constraint(ada_land, pltpu.HBM))


def _late_gather_wait(land, send_sems, recv_sems, which, after, name):
    def body(land_ref, ss, rs, after_ref, land_out):
        x, y, c = lax.axis_index("x"), lax.axis_index("y"), lax.axis_index("c")
        j = 2 * x + y
        chips, wo_reg, ada_reg = _late_gather_regions(x, y, c)
        reg = wo_reg if which == "w_out" else ada_reg
        for k in range(3):
            kj = 2 * chips[k][0] + chips[k][1]
            for cc in range(2):
                cp = pltpu.make_async_remote_copy(src_ref=reg(land_ref, j, c), dst_ref=reg(land_ref, kj, cc),
                                                  send_sem=ss.at[2 * k + cc], recv_sem=rs.at[2 * k + cc],
                                                  device_id=(*chips[k], cc), device_id_type=MESH)
                cp.wait_send()
                cp.wait_recv()

    return pl.pallas_call(
        body, name=name, out_shape=pltpu.HBM(land.shape, land.dtype),
        in_specs=(HBM, SEM, SEM, ANY), out_specs=HBM, input_output_aliases={0: 0},
        compiler_params=pltpu.CompilerParams(has_side_effects=pltpu.SideEffectType.DATAFLOW_SIDE_EFFECTING),
    )(land, send_sems, recv_sems, after)


RCHUNK = 16


def _grads_reduce(hn, dzs, hn_c, dxa_c, y, do, pack):
    rp = pack.shape[0]
    hp = rp // 2
    assert hp % RCHUNK == 0
    wi_w = 1280
    lx, lc = hn.shape[0], hn_c.shape[0]
    lt = lx + lc
    n_dz = len(dzs)

    def body(*refs):
        hn_hbm, dz_hbm = refs[0], refs[1:1 + n_dz]
        hnc_hbm, dxac_hbm, y_hbm, do_hbm, pk_hbm, wi_out, wo_out, pk_out = refs[1 + n_dz:9 + n_dz]
        (hn_mine, hn_other, dzbuf, wi_other, wi_mine, wi_recv, wi_send, wi_rb,
         y_blk, do_mine, do_other, wo_other, wo_mine, wo_recv, wo_send, wo_rb,
         pk_mine, pk_recv, pk_send, pk_rb, pk_own, send_sems, recv_sems, local_sems) = refs[9 + n_dz:]
        x, y, c = lax.axis_index("x"), lax.axis_index("y"), lax.axis_index("c")
        j = 2 * x + y
        sib = (x, y, 1 - c)
        chips = [(1 - x, y), (x, 1 - y), (1 - x, 1 - y)]
        cj = [2 * cx + cy for cx, cy in chips]
        near = (jnp.where(c == 0, 1 - x, x), jnp.where(c == 0, y, 1 - y), c)
        slabs = [cj[2], cj[0], cj[1], j]

        def copy(k, src, dst, to):
            return pltpu.make_async_remote_copy(src_ref=src, dst_ref=dst, send_sem=send_sems.at[k],
                                                recv_sem=recv_sems.at[k], device_id=to, device_id_type=MESH)

        def local(k, src, dst):
            cp = pltpu.make_async_copy(src, dst, local_sems.at[k])
            cp.start()
            return cp

        rows_half = lambda r, cc, n: r.at[pl.ds(_mo(cc * n, 16), n), :]
        cols_half = lambda r, cc, n: r.at[:, pl.ds(_mo(cc * n, 128), n)]
        pk_piece = lambda r, cc, jj: r.at[pl.ds(_mo(cc * hp, 16), hp), pl.ds(_mo(jj * 128, 128), 128)]

        sends = []

        def start(cp):
            cp.start()
            sends.append(cp)

        def dz_pieces(s):
            g0 = wi_w * s
            k0, off0 = g0 // D, g0 % D
            w0 = min(D - off0, wi_w)
            pieces = [(k0, off0, w0, 0)]
            if w0 < wi_w:
                pieces.append((k0 + 1, 0, wi_w - w0, w0))
            return pieces

        def dz_copies(s):
            cps = []
            for q, (k, off, w, dst) in enumerate(dz_pieces(s)):
                cps.append(pltpu.make_async_copy(dz_hbm[k].at[:, pl.ds(off, w)], dzbuf.at[pl.ds(0, lx), pl.ds(dst, w)],
                                                 local_sems.at[11 + q]))
            if s == 0:
                cps.append(pltpu.make_async_copy(dxac_hbm, dzbuf.at[pl.ds(lx, lc), pl.ds(0, D)], local_sems.at[13]))
            return cps

        def dz_load(sl):
            for s in range(NCHIP):
                @pl.when(sl == s)
                def _():
                    if s == 0:
                        dzbuf[pl.ds(lx, lc), pl.ds(D, wi_w - D)] = jnp.zeros((lc, wi_w - D), BF16)
                    else:
                        dzbuf[pl.ds(lx, lc), :] = jnp.zeros((lc, wi_w), BF16)
                    for cp in dz_copies(s):
                        cp.start()

        def dz_wait(sl):
            for s in range(NCHIP):
                @pl.when(sl == s)
                def _():
                    for cp in dz_copies(s):
                        cp.wait()

        l_pk = local(0, rows_half(pk_hbm, c, hp), pk_mine)
        start(copy(0, rows_half(pk_hbm, 1 - c, hp), pk_recv, sib))
        col = lambda r, cc: r.at[:, pl.ds(_mo(cc * 512, 128), 512)]
        do_loads = [local(7, col(do_hbm, c), do_mine), local(14, col(do_hbm, 1 - c), do_other)]
        hn_loads = [local(2, col(hn_hbm, c), hn_mine.at[pl.ds(0, lx), :]),
                    local(3, col(hnc_hbm, c), hn_mine.at[pl.ds(lx, lc), :]),
                    local(4, col(hn_hbm, 1 - c), hn_other.at[pl.ds(0, lx), :]),
                    local(5, col(hnc_hbm, 1 - c), hn_other.at[pl.ds(lx, lc), :])]
        y_copy = lambda s: pltpu.make_async_copy(col(y_hbm, slabs[s]), y_blk, local_sems.at[1])
        y_copy(0).start()
        dz_load(slabs[0])

        def pair_sum(mine, recv, send, nrows, keep, relayed=None):
            def step(i, carry):
                rows = pl.ds(_mo(i * RCHUNK, RCHUNK), RCHUNK)
                s = mine[rows, :] + recv[rows, :].astype(F32)
                if relayed is not None:
                    s = s + relayed[rows, :].astype(F32)
                if keep:
                    mine[rows, :] = s
                if send is not None:
                    send[rows, :] = s.astype(BF16)
                return carry
            lax.fori_loop(0, nrows // RCHUNK, step, 0)

        def chip_sum(own, rb, nrows, terms=(0, 1, 2)):
            def step(i, carry):
                rows = pl.ds(_mo(i * RCHUNK, RCHUNK), RCHUNK)
                acc = own[rows, :]
                for q in terms:
                    acc = acc + rb[q, rows, :].astype(F32)
                own[rows, :] = acc
                return carry
            lax.fori_loop(0, nrows // RCHUNK, step, 0)

        w_in_g = dict(other=wi_other, mine=wi_mine, recv=wi_recv, send=wi_send, rb=wi_rb, p1_sems=(2, 3, 4, 5), p2_sem=12,
                      p1=[None] * NCHIP, wait_load=lambda s: dz_wait(slabs[s]), load=lambda s: dz_load(slabs[s]),
                      dot_other=lambda: _dot_tn(hn_other[...], dzbuf[...]), dot_mine=lambda: _dot_tn(hn_mine[...], dzbuf[...]))
        w_out_g = dict(other=wo_other, mine=wo_mine, recv=wo_recv, send=wo_send, rb=wo_rb, p1_sems=(1, 24, 25, 26), p2_sem=9,
                       p1=[None] * NCHIP, wait_load=lambda s: y_copy(s).wait(), load=lambda s: y_copy(s).start(),
                       dot_other=lambda: _dot_tn(y_blk[...], do_other[...]), dot_mine=lambda: _dot_tn(y_blk[...], do_mine[...]))

        def piece_matmuls(g, s):
            if s >= 2:
                g["p1"][s - 2].wait_send()
            g["wait_load"](s)
            g["other"][s % 2] = g["dot_other"]().astype(BF16)
            g["p1"][s] = copy(g["p1_sems"][s], g["other"].at[s % 2], g["recv"].at[s], sib)
            g["p1"][s].start()
            g["mine"][s % 2] = g["dot_mine"]()
            if s + 1 < NCHIP:
                g["load"](s + 1)

        def piece_finish(g, s):
            mine, recv, send, rb, p2 = g["mine"].at[s % 2], g["recv"].at[s], g["send"], g["rb"], g["p2_sem"]
            nrows = mine.shape[0]
            copy(g["p1_sems"][s], recv, recv, sib).wait_recv()
            if s == 3:
                pair_sum(mine, recv, None, nrows, True)
                return
            if s == 0:
                pair_sum(mine, recv, send.at[0], nrows, False)
                start(copy(p2, send.at[0], rb.at[0], near))
                return
            adds_relayed = c == (1 if s == 1 else 0)

            @pl.when(adds_relayed)
            def _():
                copy(p2, rb.at[0], rb.at[0], sib).wait_recv()
                pair_sum(mine, recv, send.at[s], nrows, False, rb.at[0])

            @pl.when(jnp.logical_not(adds_relayed))
            def _():
                pair_sum(mine, recv, send.at[s], nrows, False)
            start(copy(p2 + s, send.at[s], rb.at[s], (*chips[s - 1], c)))

        def piece_total(g):
            for k in (1, 2):
                copy(g["p2_sem"] + k, g["rb"].at[k], g["rb"].at[k], sib).wait_recv()
            chip_sum(g["mine"].at[1], g["rb"], g["mine"].shape[1], (1, 2))

        for cp in do_loads:
            cp.wait()
        piece_matmuls(w_out_g, 0)
        piece_matmuls(w_out_g, 1)
        piece_finish(w_out_g, 0)
        piece_matmuls(w_out_g, 2)
        piece_finish(w_out_g, 1)
        piece_matmuls(w_out_g, 3)
        piece_finish(w_out_g, 2)

        for cp in hn_loads:
            cp.wait()
        piece_matmuls(w_in_g, 0)

        l_pk.wait()
        copy(0, pk_recv, pk_recv, sib).wait_recv()
        pair_sum(pk_mine, pk_recv, pk_send, hp, True)
        for k in range(3):
            start(copy(6 + k, pk_send.at[:, pl.ds(_mo(cj[k] * 128, 128), 128)], pk_rb.at[k], (*chips[k], c)))
        l_pk_own = local(6, pk_mine.at[:, pl.ds(_mo(j * 128, 128), 128)], pk_own)

        piece_matmuls(w_in_g, 1)
        piece_finish(w_in_g, 0)

        l_pk_own.wait()
        for k in range(3):
            copy(6 + k, pk_rb.at[k], pk_rb.at[k], sib).wait_recv()
        chip_sum(pk_own, pk_rb, hp)
        l_pk_out = local(8, pk_own, pk_piece(pk_out, c, j))
        start(copy(15, pk_own, pk_piece(pk_out, c, j), sib))
        for k in range(2):
            start(copy(16 + k, pk_own, pk_piece(pk_out, c, j), (*chips[k], c)))

        piece_matmuls(w_in_g, 2)
        piece_finish(w_in_g, 1)
        piece_matmuls(w_in_g, 3)
        piece_finish(w_in_g, 2)

        piece_finish(w_out_g, 3)
        piece_total(w_out_g)
        l_wo_out = local(9, wo_mine.at[1], cols_half(wo_out, c, 512))
        start(copy(22, wo_mine.at[1], cols_half(wo_out, c, 512), sib))

        far = (jnp.where(c == 0, x, 1 - x), jnp.where(c == 0, 1 - y, y), c)
        for step, k in enumerate([c, 1 - c, 2]):
            reg = pk_piece(pk_out, c, jnp.where(k == 0, cj[0], jnp.where(k == 1, cj[1], cj[2])))
            copy(16 + k, reg, reg, sib).wait_recv()
            if step == 0:
                start(copy(18, reg, reg, far))
            start(copy(19 + k, reg, reg, sib))

        piece_finish(w_in_g, 3)
        piece_total(w_in_g)
        l_wi_out = local(10, wi_mine.at[1], rows_half(wi_out, c, 512))
        start(copy(23, wi_mine.at[1], rows_half(wi_out, c, 512), sib))

        reg = pk_piece(pk_out, 1 - c, j)
        copy(15, reg, reg, sib).wait_recv()
        for k in range(3):
            reg = pk_piece(pk_out, 1 - c, cj[k])
            copy(19 + k, reg, reg, sib).wait_recv()
        reg = cols_half(wo_out, 1 - c, 512)
        copy(22, reg, reg, sib).wait_recv()
        reg = rows_half(wi_out, 1 - c, 512)
        copy(23, reg, reg, sib).wait_recv()
        for cp in sends + w_in_g["p1"][2:] + w_out_g["p1"][2:]:
            cp.wait_send()
        for cp in (l_pk_out, l_wo_out, l_wi_out):
            cp.wait()

    return pl.pallas_call(
        body, name="grads_reduce",
        out_shape=(jax.ShapeDtypeStruct((D, wi_w), F32), jax.ShapeDtypeStruct((512, D), F32),
                   jax.ShapeDtypeStruct(pack.shape, F32)),
        in_specs=[ANY] * (6 + n_dz), out_specs=(ANY,) * 3,
        scratch_shapes=[
            pltpu.VMEM((lt, 512), BF16), pltpu.VMEM((lt, 512), BF16), pltpu.VMEM((lt, wi_w), BF16),
            pltpu.VMEM((2, 512, wi_w), BF16), pltpu.VMEM((2, 512, wi_w), F32), pltpu.VMEM((4, 512, wi_w), BF16),
            pltpu.VMEM((3, 512, wi_w), BF16), pltpu.VMEM((3, 512, wi_w), BF16),
            pltpu.VMEM((lx, 512), BF16), pltpu.VMEM((lx, 512), BF16), pltpu.VMEM((lx, 512), BF16),
            pltpu.VMEM((2, 512, 512), BF16), pltpu.VMEM((2, 512, 512), F32), pltpu.VMEM((4, 512, 512), BF16),
            pltpu.VMEM((3, 512, 512), BF16), pltpu.VMEM((3, 512, 512), BF16),
            pltpu.VMEM((hp, 512), F32), pltpu.VMEM((hp, 512), F32), pltpu.VMEM((hp, 512), BF16),
            pltpu.VMEM((3, hp, 128), BF16), pltpu.VMEM((hp, 128), F32),
            pltpu.SemaphoreType.DMA((27,)), pltpu.SemaphoreType.DMA((27,)), pltpu.SemaphoreType.DMA((15,))],
        compiler_params=_cp(vmem_mb=56),
    )(hn, *dzs, hn_c, dxa_c, y, do, pack)


def _ada_bwd(c_all, dmx_all_j, dmc_j, dmx_all, dmc, c_ctx, ada_w_full):
    def body(c_ref, dmxj_ref, dmcj_ref, dmx_ref, dmc_ref, cc_ref, w_ref, gw_ref, gb_ref, gc_ref, lhs, rhs, dm8):
        lhs[...] = jnp.zeros_like(lhs)
        rhs[...] = jnp.zeros_like(rhs)
        cv = c_ref[...]
        lhs[0:8, :] = cv * _sigmoid(cv)
        cc = cc_ref[...]
        a_c, da_c = _silu_and_grad(cc)
        lhs[8:9, :] = a_c
        rhs[0:8, :] = dmxj_ref[...]
        rhs[8:9, :] = dmcj_ref[...]
        gw_ref[...] = _dot_tn(lhs[...].astype(BF16), rhs[...].astype(BF16))
        gb_ref[...] = jnp.sum(dmx_ref[...], axis=0, keepdims=True) + dmc_ref[...]
        dm8[...] = jnp.zeros_like(dm8)
        dm8[0:1, :] = dmc_ref[...]
        da = _dot_nt(dm8[...].astype(BF16), w_ref[...])
        gc_ref[...] = da[0:1, :] * da_c

    return pl.pallas_call(
        body, name="ada_bwd",
        out_shape=(jax.ShapeDtypeStruct((D, 768), F32), jax.ShapeDtypeStruct((1, 3 * D), F32),
                   jax.ShapeDtypeStruct((1, D), F32)),
        in_specs=[VMEM] * 7, out_specs=(VMEM,) * 3,
        scratch_shapes=[pltpu.VMEM((16, D), F32), pltpu.VMEM((16, 768), F32), pltpu.VMEM((8, 3 * D), F32)],
        compiler_params=_cp(vmem_mb=32),
    )(c_all, dmx_all_j, dmc_j, dmx_all, dmc, c_ctx, ada_w_full)


def _proj(x, mods, mrow, norm_g, w_full, nk, name, sgu=None):
    lx = x.shape[0]
    n = lx // T
    order = [2, 3, 0, 1, 4] if sgu is not None else list(range(nk))

    def body(x_ref, sh_ref, sc_ref, ng_ref, *rest):
        w_refs, rest = rest[:nk], rest[nk:]
        if sgu is not None:
            g_ref, b_ref, sw_ref, bt_ref = rest[:4]
            rest = rest[4:]
        hn_ref, z_refs = rest[0], rest[1:1 + nk]
        xv = x_ref[...]
        r = lax.rsqrt(jnp.mean(xv * xv, axis=-1, keepdims=True) + NORM_EPS)
        hn = (xv * r) * ng_ref[...] * (1.0 + sc_ref[mrow:mrow + 1, :]) + sh_ref[mrow:mrow + 1, :]
        hb = hn.astype(BF16)
        hn_ref[...] = hb
        for k in order[:2]:
            z_refs[k][...] = _dot(hb, w_refs[k][...])
        if sgu is not None:
            ys_ref, mixed_s = rest[1 + nk], rest[2 + nk]
            for ch in range(T // HD):
                rows = slice(HD * ch, HD * ch + HD)
                ug = _sgu_parts(z_refs[2][rows, :], z_refs[3][rows, :], g_ref[...], b_ref[...], sw_ref, bt_ref,
                                mixed_s)[0]
                ys_ref[rows, :] = ug * mixed_s[...]
        for k in order[2:]:
            z_refs[k][...] = _dot(hb, w_refs[k][...])

    row = pl.BlockSpec((T, D), lambda i: (i, 0))
    vec = pl.BlockSpec((1, D), lambda i: (0, 0))
    in_specs = [row, pl.BlockSpec((8, D), lambda i: (0, 0)), pl.BlockSpec((8, D), lambda i: (0, 1)), vec]
    in_specs += [pl.BlockSpec((D, D), lambda i, k=k: (0, k)) for k in range(nk)]
    args = [x, mods, mods, norm_g] + [w_full] * nk
    out_shape = (jax.ShapeDtypeStruct((lx, D), BF16),) + tuple(jax.ShapeDtypeStruct((lx, D), F32) for _ in range(nk))
    scratch = []
    if sgu is not None:
        in_specs += [vec, vec, pl.BlockSpec((NH, HD, HD), lambda i: (0, 0, 0)), pl.BlockSpec((HD, NH), lambda i: (0, 0))]
        args += list(sgu)
        out_shape += (jax.ShapeDtypeStruct((lx, D), F32),)
        scratch = [pltpu.VMEM((HD, D), F32)]
    return pl.pallas_call(
        body, name=name, grid=(n,), out_shape=out_shape, in_specs=in_specs, out_specs=(row,) * len(out_shape),
        scratch_shapes=scratch, compiler_params=_cp(1, vmem_mb=56),
    )(*args)


def _halo_specs(lx):
    last = lx // 8 - 1
    return [pl.BlockSpec((T, D), lambda i: (i, 0)),
            pl.BlockSpec((8, D), lambda i: (jnp.maximum(i * (T // 8) - 1, 0), 0)),
            pl.BlockSpec((8, D), lambda i: (jnp.minimum((i + 1) * (T // 8), last), 0))]


def _zhalo_specs(lx):
    last = lx // 8 - 1
    return [pl.BlockSpec((T * NH, HD), lambda i: (i, 0)),
            pl.BlockSpec((8 * NH, HD), lambda i: (jnp.maximum(i * (T // 8) - 1, 0), 0)),
            pl.BlockSpec((8 * NH, HD), lambda i: (jnp.minimum((i + 1) * (T // 8), last), 0))]


ZT = pl.BlockSpec((T * NH, HD), lambda i: (i, 0))
CONV_CHUNK = 32


SCAN_SUB = 4


def _scan_tile(chains, post, carry_ref):
    blk = T // SCAN_SUB

    def step(k, state):
        new = []
        for ci, (a_ref, x_ref, o_ref, q_ref, reverse) in enumerate(chains):
            for q in range(SCAN_SUB):
                s, p = state[ci * SCAN_SUB + q]
                t = (q + 1) * blk - 1 - k if reverse else q * blk + k
                r = pl.ds(_mo(t * NH, NH), NH)
                a = a_ref[r, :]
                if post:
                    o = x_ref[r, :] + s
                    o_ref[r, :] = o
                    q_ref[r, :] = p
                    new.append((a * o, a * p))
                else:
                    o = a * s + x_ref[r, :]
                    p = a * p
                    o_ref[r, :] = o
                    q_ref[r, :] = p
                    new.append((o, p))
        return tuple(new)

    zero = jnp.zeros((NH, HD), F32)
    one = jnp.ones((NH, HD), F32)
    final = lax.fori_loop(0, blk, step, tuple((zero, one) for _ in range(len(chains) * SCAN_SUB)), unroll=2)
    for ci, (a_ref, x_ref, o_ref, q_ref, reverse) in enumerate(chains):
        carry = carry_ref[ci]
        for q in (range(SCAN_SUB - 1, -1, -1) if reverse else range(SCAN_SUB)):
            rows = pl.ds(q * blk * NH, blk * NH)
            fixed = o_ref[rows, :].reshape(blk, NH, HD) + q_ref[rows, :].reshape(blk, NH, HD) * carry[None]
            o_ref[rows, :] = fixed.reshape(blk * NH, HD)
            s_loc, p_loc = final[ci * SCAN_SUB + q]
            carry = s_loc + p_loc * carry
        carry_ref[ci] = carry


def _lru_fwd(xa, conv_wz, conv_bz, wcat, bcat, lamcat, s_f, s_b, name):
    lx = xa.shape[0]
    n = lx // T

    def body(xm_u, xp_u, xn_u, xm_d, xp_d, xn_d, cw, cb, w_ref, b_ref, lam_ref, su0, sd0,
             xaz_o, xcz_o, af_o, ab_o, hf_o, hb_o, gf_o, gb_o, fu, fd, pad, xc_d, x_u, x_d, q_u, q_d, carry):
        i = pl.program_id(0)

        @pl.when(i == 0)
        def _():
            carry[0] = su0[...]
            carry[1] = sd0[...]

        def conv_gates(xm, xp, xn, tile, d, xc_ref, a_ref, x_ref, xaz_ref, g_ref):
            pmask = jnp.where(tile == 0, 0.0, 1.0)
            nmask = jnp.where(tile == n - 1, 0.0, 1.0)
            for h in range(NH):
                cols = slice(HD * h, HD * h + HD)
                pad[_zrows(h, 8), :] = xp[:, cols] * pmask
                pad[pl.ds(8 * NH + h, T, stride=NH), :] = xm[:, cols]
                pad[pl.ds((T + 8) * NH + h, 8, stride=NH), :] = xn[:, cols] * nmask
            if xaz_ref is not None:
                xaz_ref[...] = pad[pl.ds(8 * NH, T * NH), :]

            def conv_chunk(ci, c_):
                base = pl.multiple_of(ci * (CONV_CHUNK * NH), CONV_CHUNK * NH)
                acc = None
                for k in range(4):
                    sl = pad[pl.ds(base + (7 + k) * NH, CONV_CHUNK * NH), :].reshape(CONV_CHUNK, NH, HD)
                    term = sl * cw[k][None]
                    acc = term if acc is None else acc + term
                acc = acc + cb[...][None]
                xc_ref[pl.ds(base, CONV_CHUNK * NH), :] = acc.reshape(CONV_CHUNK * NH, HD)
                return c_
            lax.fori_loop(0, T // CONV_CHUNK, conv_chunk, 0)

            for h in range(NH):
                xch = xc_ref[_zrows(h, T), :]
                pre = _dot(xch.astype(BF16), w_ref[h, :, 256 * d:256 * d + 256]) + b_ref[h:h + 1, 256 * d:256 * d + 256]
                r, gi, _, _, a, mult = _lru_gate(pre, lam_ref[h:h + 1, :], d, 0)
                a_ref[_zrows(h, T), :] = a
                x_ref[_zrows(h, T), :] = mult * gi * xch
                for q, val in enumerate((r, gi, mult)):
                    g_ref[:, q * D + HD * h:q * D + HD * h + HD] = val

        conv_gates(xm_u, xp_u, xn_u, i, 0, xcz_o, af_o, x_u, xaz_o, gf_o)
        conv_gates(xm_d, xp_d, xn_d, n - 1 - i, 1, xc_d, ab_o, x_d, None, gb_o)

        _scan_tile([(af_o, x_u, hf_o, q_u, False), (ab_o, x_d, hb_o, q_d, True)], False, carry)
        fu[...] = carry[0]
        fd[...] = carry[1]

    full = lambda shape: pl.BlockSpec(shape, lambda i: (0,) * len(shape))
    last = lx // 8 - 1
    rev = lambda i: n - 1 - i
    halo_dn = [pl.BlockSpec((T, D), lambda i: (rev(i), 0)),
               pl.BlockSpec((8, D), lambda i: (jnp.maximum(rev(i) * (T // 8) - 1, 0), 0)),
               pl.BlockSpec((8, D), lambda i: (jnp.minimum((rev(i) + 1) * (T // 8), last), 0))]
    st = full((NH, HD))
    in_specs = _halo_specs(lx) + halo_dn + [full((4, NH, HD)), st, full((NH, HD, 4 * HD)), full((NH, 4 * HD)),
                                            full((NH, 2 * HD)), st, st]
    dn = pl.BlockSpec((T * NH, HD), lambda i: (rev(i), 0))
    zs = jax.ShapeDtypeStruct((lx * NH, HD), F32)
    ss = jax.ShapeDtypeStruct((NH, HD), F32)
    zbuf = pltpu.VMEM((T * NH, HD), F32)
    gs = jax.ShapeDtypeStruct((lx, 3 * D), F32)
    g_up = pl.BlockSpec((T, 3 * D), lambda i: (i, 0))
    g_dn = pl.BlockSpec((T, 3 * D), lambda i: (rev(i), 0))
    return pl.pallas_call(
        body, name=name, grid=(n,), out_shape=(zs,) * 6 + (gs, gs, ss, ss), in_specs=in_specs,
        out_specs=(ZT, ZT, ZT, dn, ZT, dn, g_up, g_dn, st, st),
        scratch_shapes=[pltpu.VMEM(((T + 16) * NH, HD), F32), zbuf, zbuf, zbuf, zbuf, zbuf,
                        pltpu.VMEM((2, NH, HD), F32)],
        compiler_params=_cp(1, vmem_mb=48),
    )(xa, xa, xa, xa, xa, xa, conv_wz, conv_bz, wcat, bcat, lamcat, s_f, s_b)


def _sgu_parts(u, v, lng, lnb, w_ref, bt_ref, mixed_s):
    ug, dug = _gelu_and_grad(u)
    vg, dvg = _gelu_and_grad(v)
    mu = jnp.mean(vg, axis=-1, keepdims=True)
    vc = vg - mu
    rstd = lax.rsqrt(jnp.mean(vc * vc, axis=-1, keepdims=True) + LN_EPS)
    vh = vc * rstd
    vn = (vh * lng + lnb).astype(BF16)
    for g in range(NH):
        cols = slice(HD * g, HD * g + HD)
        mixed_s[:, cols] = _dot(w_ref[g], vn[:, cols]) + bt_ref[:, g:g + 1]
    return ug, dug, dvg, rstd, vh, vn


def _sgu_bwd_chunk(u, v, dys_v, lng, lnb, w_ref, bt_ref, mixed_s, dvn_s, dw_ref, db_ref, dg_ref, dbl_ref):
    ug, dug, dvg, rstd, vh, vn = _sgu_parts(u, v, lng, lnb, w_ref, bt_ref, mixed_s)
    du = (dys_v * mixed_s[...] * dug).astype(BF16)
    dmix = dys_v * ug
    ones = jnp.ones((8, HD), BF16)
    for g in range(NH):
        cols = slice(HD * g, HD * g + HD)
        dm = dmix[:, cols]
        hi = dm.astype(BF16)
        lo = (dm - hi.astype(F32)).astype(BF16)
        dw_ref[g] += _dot_nt(hi, vn[:, cols])
        db_ref[g:g + 1, :] += (_dot_nt(ones, hi) + _dot_nt(ones, lo))[0:1, :]
        dvn_s[:, cols] = _dot_tn(w_ref[g], hi)
    dvn = dvn_s[...]
    dg_ref[...] += jnp.sum(dvn * vh, axis=0, keepdims=True)
    dbl_ref[...] += jnp.sum(dvn, axis=0, keepdims=True)
    dvh = dvn * lng
    dvg_in = rstd * (dvh - jnp.mean(dvh, axis=-1, keepdims=True) - vh * jnp.mean(dvh * vh, axis=-1, keepdims=True))
    return du, (dvg_in * dvg).astype(BF16)


def _out_fwd_bwd(hf_z, hb_z, ga, gb, ys, x, tgt, mods, final_g, w_out_full):
    lx = x.shape[0]
    n = lx // T

    def body(hf_ref, hb_ref, ga_ref, gb_ref, ys_ref, x_ref, t_ref, gx_ref, fg_ref, w_ref,
             loss_ref, dfg_ref, dgx_ref, dxn_ref, y_ref, do_ref, dga_ref, dgb_ref, dyl_ref, dys_ref, yl_s):
        i = pl.program_id(0)

        @pl.when(i == 0)
        def _():
            loss_ref[...] = jnp.zeros_like(loss_ref)
            dfg_ref[...] = jnp.zeros_like(dfg_ref)
            dgx_ref[...] = jnp.zeros_like(dgx_ref)

        for h in range(NH):
            yl_s[:, HD * h:HD * h + HD] = hf_ref[_zrows(h, T), :] + hb_ref[_zrows(h, T), :]
        yl = yl_s[...]
        gav = ga_ref[...]
        gbv = gb_ref[...]
        sa, dsa = _silu_and_grad(gav)
        sb, dsb = _silu_and_grad(gbv)
        ysv = ys_ref[...]
        y_ref[:, 0:D] = (yl * sa).astype(BF16)
        y_ref[:, D:2 * D] = (ysv * sb).astype(BF16)
        o = _dot(y_ref[...], w_ref[...])
        gx = gx_ref[0:1, :]
        xnew = x_ref[...] + gx * o
        r2 = lax.rsqrt(jnp.mean(xnew * xnew, axis=-1, keepdims=True) + NORM_EPS)
        xh = xnew * r2
        fg = fg_ref[...]
        err = xh * fg - t_ref[...]
        loss_ref[...] += 0.5 * jnp.sum(jnp.mean(err * err, axis=-1, keepdims=True), axis=0, keepdims=True)
        dout = err * (1.0 / D)
        dfg_ref[...] += jnp.sum(dout * xh, axis=0, keepdims=True)
        dxh = dout * fg
        dxn = r2 * (dxh - xh * jnp.mean(dxh * xh, axis=-1, keepdims=True))
        dxn_ref[...] = dxn
        dgx_ref[...] += jnp.sum(dxn * o, axis=0, keepdims=True)
        do = (dxn * gx).astype(BF16)
        do_ref[...] = do
        dy = _dot_nt(do, w_ref[...])
        dy1 = dy[:, 0:D]
        dy2 = dy[:, D:2 * D]
        dga_ref[...] = (dy1 * yl * dsa).astype(BF16)
        dgb_ref[...] = (dy2 * ysv * dsb).astype(BF16)
        dys_ref[...] = dy2 * sb
        yl_s[...] = dy1 * sa
        for h in range(NH):
            dyl_ref[_zrows(h, T), :] = yl_s[:, HD * h:HD * h + HD]

    row = pl.BlockSpec((T, D), lambda i: (i, 0))
    vec = pl.BlockSpec((1, D), lambda i: (0, 0))
    in_specs = [ZT, ZT, row, row, row, row, row, pl.BlockSpec((8, D), lambda i: (0, 2)), vec,
                pl.BlockSpec((2 * D, D), lambda i: (0, 0))]
    out_shape = (jax.ShapeDtypeStruct((1, 1), F32), jax.ShapeDtypeStruct((1, D), F32), jax.ShapeDtypeStruct((1, D), F32),
                 jax.ShapeDtypeStruct((lx, D), F32), jax.ShapeDtypeStruct((lx, 2 * D), BF16),
                 jax.ShapeDtypeStruct((lx, D), BF16), jax.ShapeDtypeStruct((lx, D), BF16),
                 jax.ShapeDtypeStruct((lx, D), BF16), jax.ShapeDtypeStruct((lx * NH, HD), F32),
                 jax.ShapeDtypeStruct((lx, D), F32))
    out_specs = (pl.BlockSpec((1, 1), lambda i: (0, 0)), vec, vec, row, pl.BlockSpec((T, 2 * D), lambda i: (i, 0)),
                 row, row, row, ZT, row)
    return pl.pallas_call(
        body, name="out_fwd_bwd", grid=(n,), out_shape=out_shape, in_specs=in_specs, out_specs=out_specs,
        scratch_shapes=[pltpu.VMEM((T, D), F32)],
        compiler_params=_cp(1, vmem_mb=56),
    )(hf_z, hb_z, ga, gb, ys, x, tgt, mods, final_g, w_out_full)


def _lru_bwd(xc_z, dy_up, dy_dn, hf_z, hb_z, af_z, ab_z, gf, gb, s_f, s_b, wcat, lamcat, dw0, db0, dl0, name):
    lx = xc_z.shape[0] // NH
    n = lx // T

    def body(xc_u, dy_u, hb_ref, hbn_ref, ab_ref, gb_ref, xc_d, dy_d, hf_ref, hfp_ref, af_ref, gf_ref,
             sf_ref, sb_ref, w_ref, lam_ref, dw0_ref, db0_ref, dl0_ref,
             dxcb_ref, dxcf_ref, dw_ref, db_ref, dl_ref, fu, fd, lb_s, lf_s, q_u, q_d, pf_s, pb_s, dpre_s, carry):
        i = pl.program_id(0)

        @pl.when(i == 0)
        def _():
            dw_ref[...] = dw0_ref[...]
            db_ref[...] = db0_ref[...]
            dl_ref[...] = dl0_ref[...]
            carry[...] = jnp.zeros_like(carry)

        _scan_tile([(ab_ref, dy_u, lb_s, q_u, False), (af_ref, dy_d, lf_s, q_d, True)], True, carry)
        fu[...] = carry[0]
        fd[...] = carry[1]
        pb_s[pl.ds(0, T * NH), :] = hb_ref[...]
        pb_s[pl.ds(T * NH, NH), :] = jnp.where(i == n - 1, sb_ref[...], hbn_ref[pl.ds(0, NH), :])
        pf_s[pl.ds(0, NH), :] = jnp.where(i == n - 1, sf_ref[...], hfp_ref[pl.ds(7 * NH, NH), :])
        pf_s[pl.ds(NH, T * NH), :] = hf_ref[...]
        sides = ((1, xc_u, lb_s, pb_s, NH, ab_ref, gb_ref, dxcb_ref), (0, xc_d, lf_s, pf_s, 0, af_ref, gf_ref, dxcf_ref))
        for d, xc_ref, adj_s, prev_s, prev_off, a_ref, g_ref, dxc_ref in sides:
            wcols = slice(256 * d, 256 * d + 256)
            for h in range(NH):
                xch = xc_ref[_zrows(h, T), :]
                xcb = xch.astype(BF16)
                r, gi, mult = (g_ref[:, q * D + HD * h:q * D + HD * h + HD] for q in range(3))
                a = a_ref[_zrows(h, T), :]
                lam = lam_ref[h:h + 1, HD * d:HD * d + HD]
                sp = _softplus(-lam)
                du = adj_s[_zrows(h, T), :]
                da = du * prev_s[pl.ds(prev_off + h, T, stride=NH), :]
                dgi = du * mult * xch
                dmult = du * gi * xch
                dla = da * a - dmult * (a * a) / mult
                dr = dla * ((-LRU_C) * sp)
                dsp = jnp.sum(dla * ((-LRU_C) * r), axis=0, keepdims=True)
                dl_ref[h:h + 1, HD * d:HD * d + HD] += dsp * (-_sigmoid(-lam))
                dpre_s[:, 0:HD] = dr * r * (1.0 - r)
                dpre_s[:, HD:2 * HD] = dgi * gi * (1.0 - gi)
                dpre = dpre_s[...]
                dpb = dpre.astype(BF16)
                dw_ref[h, :, wcols] += _dot_tn(xcb, dpb)
                db_ref[h:h + 1, wcols] += jnp.sum(dpre, axis=0, keepdims=True)
                dxc_ref[_zrows(h, T), :] = du * mult * gi + _dot_nt(dpb, w_ref[h, :, wcols])

    full = lambda shape: pl.BlockSpec(shape, lambda i: (0,) * len(shape))
    wsp, bsp, lsp = full((NH, HD, 4 * HD)), full((NH, 4 * HD)), full((NH, 2 * HD))
    st = full((NH, HD))
    rev = lambda i: n - 1 - i
    up = ZT
    dn = pl.BlockSpec((T * NH, HD), lambda i: (rev(i), 0))
    nxt = _zhalo_specs(lx)[2]
    prv = pl.BlockSpec((8 * NH, HD), lambda i: (jnp.maximum(rev(i) * (T // 8) - 1, 0), 0))
    g_up = pl.BlockSpec((T, 3 * D), lambda i: (i, 0))
    g_dn = pl.BlockSpec((T, 3 * D), lambda i: (rev(i), 0))
    zs = jax.ShapeDtypeStruct((lx * NH, HD), F32)
    ss = jax.ShapeDtypeStruct((NH, HD), F32)
    zbuf = pltpu.VMEM((T * NH, HD), F32)
    zbuf1 = pltpu.VMEM(((T + 1) * NH, HD), F32)
    return pl.pallas_call(
        body, name=name, grid=(n,),
        out_shape=(zs, zs, jax.ShapeDtypeStruct((NH, HD, 4 * HD), F32), jax.ShapeDtypeStruct((NH, 4 * HD), F32),
                   jax.ShapeDtypeStruct((NH, 2 * HD), F32), ss, ss),
        in_specs=[up, up, up, nxt, up, g_up, dn, dn, dn, prv, dn, g_dn, st, st, wsp, lsp, wsp, bsp, lsp],
        out_specs=(up, dn, wsp, bsp, lsp, st, st),
        scratch_shapes=[zbuf, zbuf, zbuf, zbuf, zbuf1, zbuf1, pltpu.VMEM((T, 2 * HD), F32), pltpu.VMEM((2, NH, HD), F32)],
        compiler_params=_cp(1, vmem_mb=56),
    )(xc_z, dy_up, hb_z, hb_z, ab_z, gb, xc_z, dy_dn, hf_z, hf_z, af_z, gf, s_f, s_b, wcat, lamcat, dw0, db0, dl0)


def _conv_bwd(dxc_a, dxc_b, xa_z, conv_wz, dcw0, dcb0, name):
    lx = dxc_a.shape[0] // NH
    n = lx // T

    def body(dm_a, dp_a, dn_a, dm_b, dp_b, dn_b, xa_ref, cw, dcw0_ref, dcb0_ref, dxa_ref, dcw_ref, dcb_ref, pad, dxa_s):
        i = pl.program_id(0)

        @pl.when(i == 0)
        def _():
            dcw_ref[...] = dcw0_ref[...]
            dcb_ref[...] = dcb0_ref[...]

        pmask = jnp.where(i == 0, 0.0, 1.0)
        nmask = jnp.where(i == n - 1, 0.0, 1.0)
        pad[pl.ds(0, 8 * NH), :] = (dp_a[...] + dp_b[...]) * pmask
        pad[pl.ds(8 * NH, T * NH), :] = dm_a[...] + dm_b[...]
        pad[pl.ds((T + 8) * NH, 8 * NH), :] = (dn_a[...] + dn_b[...]) * nmask

        def chunk(ci, carry):
            base = pl.multiple_of(ci * (CONV_CHUNK * NH), CONV_CHUNK * NH)
            xav = xa_ref[pl.ds(base, CONV_CHUNK * NH), :].reshape(CONV_CHUNK, NH, HD)
            acc = None
            for k in range(4):
                sl = pad[pl.ds(base + (9 - k) * NH, CONV_CHUNK * NH), :].reshape(CONV_CHUNK, NH, HD)
                term = sl * cw[k][None]
                acc = term if acc is None else acc + term
                dcw_ref[k] += jnp.sum(sl * xav, axis=0)
                if k == 1:
                    dcb_ref[...] += jnp.sum(sl, axis=0)
            dxa_s[pl.ds(base, CONV_CHUNK * NH), :] = acc.reshape(CONV_CHUNK * NH, HD)
            return carry
        lax.fori_loop(0, T // CONV_CHUNK, chunk, 0)
        for h in range(NH):
            dxa_ref[:, HD * h:HD * h + HD] = dxa_s[_zrows(h, T), :].astype(BF16)

    full = lambda shape: pl.BlockSpec(shape, lambda i: (0,) * len(shape))
    return pl.pallas_call(
        body, name=name, grid=(n,),
        out_shape=(jax.ShapeDtypeStruct((lx, D), BF16), jax.ShapeDtypeStruct((4, NH, HD), F32),
                   jax.ShapeDtypeStruct((NH, HD), F32)),
        in_specs=_zhalo_specs(lx) * 2 + [ZT, full((4, NH, HD)), full((4, NH, HD)), full((NH, HD))],
        out_specs=(pl.BlockSpec((T, D), lambda i: (i, 0)), full((4, NH, HD)), full((NH, HD))),
        scratch_shapes=[pltpu.VMEM(((T + 16) * NH, HD), F32), pltpu.VMEM((T * NH, HD), F32)],
        compiler_params=_cp(1, vmem_mb=48),
    )(dxc_a, dxc_a, dxc_a, dxc_b, dxc_b, dxc_b, xa_z, conv_wz, dcw0, dcb0)


def _proj_bwd(dzs, x, dxn, mods, mrow, norm_g, w_full, dng0, name, sgu=None):
    lx = x.shape[0]
    n = lx // T
    nz = len(dzs)
    has_x = dxn is not None
    wks = ([0, 1, 4, 2, 3] if sgu is not None else list(range(nz)))

    def body(*refs):
        it = iter(refs)
        take = lambda m: [next(it) for _ in range(m)]
        dz_refs, w_refs = take(nz), take(len(wks))
        x_ref, sc_ref, ng_ref, dng0_ref = take(4)
        dxn_ref = take(1)[0] if has_x else None
        if sgu is not None:
            u_ref, v_ref, dy_ref, g_ref, b_ref, sw_ref, bt_ref = take(7)
        gx_ref = take(1)[0] if has_x else None
        dng_ref, dsc_ref, dsh_ref = take(3)
        if sgu is not None:
            du_ref, dv_ref, dws_ref, dbs_ref, dlg_ref, dlb_ref, mixed_s, dvn_s = take(8)
        i = pl.program_id(0)

        @pl.when(i == 0)
        def _():
            dng_ref[...] = dng0_ref[...]
            dsc_ref[...] = jnp.zeros_like(dsc_ref)
            dsh_ref[...] = jnp.zeros_like(dsh_ref)
            if sgu is not None:
                for acc in (dws_ref, dbs_ref, dlg_ref, dlb_ref):
                    acc[...] = jnp.zeros_like(acc)

        dhn = _dot_nt(dz_refs[0][...], w_refs[0][...])
        for k in range(1, nz):
            dhn = dhn + _dot_nt(dz_refs[k][...], w_refs[k][...])
        if sgu is not None:
            for ch in range(T // HD):
                rows = slice(HD * ch, HD * ch + HD)
                du, dv = _sgu_bwd_chunk(u_ref[rows, :], v_ref[rows, :], dy_ref[rows, :], g_ref[...], b_ref[...],
                                        sw_ref, bt_ref, mixed_s, dvn_s, dws_ref, dbs_ref, dlg_ref, dlb_ref)
                du_ref[rows, :] = du
                dv_ref[rows, :] = dv
            dhn = dhn + _dot_nt(du_ref[...], w_refs[nz][...]) + _dot_nt(dv_ref[...], w_refs[nz + 1][...])
        xv = x_ref[...]
        r = lax.rsqrt(jnp.mean(xv * xv, axis=-1, keepdims=True) + NORM_EPS)
        xn = xv * r
        ng = ng_ref[...]
        sc1 = 1.0 + sc_ref[mrow:mrow + 1, :]
        t = dhn * xn
        dng_ref[...] += jnp.sum(t * sc1, axis=0, keepdims=True)
        dsc_ref[...] += jnp.sum(t * ng, axis=0, keepdims=True)
        dsh_ref[...] += jnp.sum(dhn, axis=0, keepdims=True)
        if has_x:
            dxh = dhn * (ng * sc1)
            gx_ref[...] = dxn_ref[...] + r * (dxh - xn * jnp.mean(dxh * xn, axis=-1, keepdims=True))

    row = pl.BlockSpec((T, D), lambda i: (i, 0))
    vec = pl.BlockSpec((1, D), lambda i: (0, 0))
    in_specs = [row] * nz + [pl.BlockSpec((D, D), lambda i, k=k: (0, k)) for k in wks]
    in_specs += [row, pl.BlockSpec((8, D), lambda i: (0, 1)), vec, vec]
    args = list(dzs) + [w_full] * len(wks) + [x, mods, norm_g, dng0]
    vs = jax.ShapeDtypeStruct((1, D), F32)
    out_shape, out_specs = (vs, vs, vs), (vec, vec, vec)
    scratch = []
    if has_x:
        in_specs.append(row)
        args.append(dxn)
        out_shape = (jax.ShapeDtypeStruct((lx, D), F32),) + out_shape
        out_specs = (row,) + out_specs
    if sgu is not None:
        wsp = pl.BlockSpec((NH, HD, HD), lambda i: (0, 0, 0))
        bsp = pl.BlockSpec((NH, HD), lambda i: (0, 0))
        in_specs += [row, row, row, vec, vec, wsp, pl.BlockSpec((HD, NH), lambda i: (0, 0))]
        args += list(sgu)
        zb = jax.ShapeDtypeStruct((lx, D), BF16)
        out_shape += (zb, zb, jax.ShapeDtypeStruct((NH, HD, HD), F32), jax.ShapeDtypeStruct((NH, HD), F32), vs, vs)
        out_specs += (row, row, wsp, bsp, vec, vec)
        scratch = [pltpu.VMEM((HD, D), F32), pltpu.VMEM((HD, D), F32)]
    return pl.pallas_call(
        body, name=name, grid=(n,), out_shape=out_shape, in_specs=in_specs, out_specs=out_specs,
        scratch_shapes=scratch, compiler_params=_cp(1, vmem_mb=56),
    )(*args)


def _adam_math(w, g, m, v):
    m = ADAM_B1 * m + (1.0 - ADAM_B1) * g
    v = ADAM_B2 * v + (1.0 - ADAM_B2) * (g * g)
    m_hat = m / (1.0 - ADAM_B1 ** ADAM_STEP)
    v_hat = v / (1.0 - ADAM_B2 ** ADAM_STEP)
    delta = -ADAM_LR * (m_hat / (jnp.sqrt(v_hat) + ADAM_EPS) + ADAM_WD * w)
    return delta, m, v


def _adam_big(w, g, m, v, name):
    rows, cols = w.shape
    tr = 256

    def body(w_ref, g_ref, m_ref, v_ref, d_o, m_o, v_o):
        d, mm, vv = _adam_math(w_ref[...], g_ref[...], m_ref[...], v_ref[...])
        d_o[...] = d
        m_o[...] = mm
        v_o[...] = vv

    blk = pl.BlockSpec((tr, cols), lambda i: (i, 0))
    s = jax.ShapeDtypeStruct((rows, cols), F32)
    return pl.pallas_call(
        body, name=name, grid=(rows // tr,), out_shape=(s, s, s), in_specs=[blk] * 4, out_specs=(blk,) * 3,
        compiler_params=_cp(1, vmem_mb=48),
    )(w, g, m, v)


def _adam_small(items):
    ni = len(items)

    def body(*refs):
        ins, outs = refs[:4 * ni], refs[4 * ni:7 * ni]
        bufs_in, bufs_out = refs[7 * ni:11 * ni], refs[11 * ni:14 * ni]
        sem_in, sem_out = refs[14 * ni], refs[14 * ni + 1]
        loads = [pltpu.make_async_copy(ins[q], bufs_in[q], sem_in.at[q]) for q in range(4 * ni)]
        for cp in loads:
            cp.start()
        stores = []
        for k in range(ni):
            for q in range(4):
                loads[4 * k + q].wait()
            w_b, g_b, m_b, v_b = bufs_in[4 * k:4 * k + 4]
            res = _adam_math(w_b[...], g_b[...], m_b[...], v_b[...])
            for q in range(3):
                bufs_out[3 * k + q][...] = res[q]
                cp = pltpu.make_async_copy(bufs_out[3 * k + q], outs[3 * k + q], sem_out.at[3 * k + q])
                cp.start()
                stores.append(cp)
        for cp in stores:
            cp.wait()

    flat = [a for it in items for a in it]
    out_shape = tuple(jax.ShapeDtypeStruct(it[0].shape, F32) for it in items for _ in range(3))
    scratch = [pltpu.VMEM(a.shape, F32) for a in flat] + [pltpu.VMEM(s.shape, F32) for s in out_shape]
    scratch += [pltpu.SemaphoreType.DMA((4 * ni,)), pltpu.SemaphoreType.DMA((3 * ni,))]
    res = pl.pallas_call(
        body, name="adam_small", out_shape=out_shape, in_specs=[HBM] * (4 * ni), out_specs=(HBM,) * (3 * ni),
        scratch_shapes=scratch, compiler_params=_cp(vmem_mb=40),
    )(*flat)
    return [tuple(res[3 * k:3 * k + 3]) for k in range(ni)]


def kernel(x, c, ctx, c_ctx, ada_w, ada_b, norm_g, w_in, conv_w, conv_b, lru_wa, lru_ba, lru_wx, lru_bx, lru_lambda, sgu_ln_g, sgu_ln_b, sgu_w, sgu_b, w_out, final_g, loss_target, m_c_ctx, m_ada_w, m_ada_b, m_norm_g, m_w_in, m_conv_w, m_conv_b, m_lru_wa, m_lru_ba, m_lru_wx, m_lru_bx, m_lru_lambda, m_sgu_ln_g, m_sgu_ln_b, m_sgu_w, m_sgu_b, m_w_out, m_final_g, v_c_ctx, v_ada_w, v_ada_b, v_norm_g, v_w_in, v_conv_w, v_conv_b, v_lru_wa, v_lru_ba, v_lru_wx, v_lru_bx, v_lru_lambda, v_sgu_ln_g, v_sgu_ln_b, v_sgu_w, v_sgu_b, v_w_out, v_final_g):
    ix, iy, ic = lax.axis_index("x"), lax.axis_index("y"), lax.axis_index("c")
    chip = 2 * ix + iy
    dev = 2 * chip + ic
    lx = x.shape[1]
    lc = ctx.shape[1]

    smalls = jnp.concatenate([conv_w[0], lru_lambda[0], jnp.zeros((10, 256), F32)], axis=0)
    c_ctx2 = c_ctx.reshape(1, D)
    ada_b_j = lax.dynamic_slice(ada_b, (0, 768 * chip), (1, 768))
    mods, c_slots, sm_all, w_in_full, wo_land, ada_land = _gather_in(c, c_ctx2, ada_w[0], ada_b_j, w_in[0], w_out[0],
                                                                     smalls)
    wo_ss, wo_rs, ada_ss, ada_rs, wo_land, ada_land, token = _late_gather_start(wo_land, ada_land)
    mods = mods + token[0:1, 0:1]
    sm3 = sm_all.reshape(NCHIP, 16, 256)
    conv_w_full = sm3[:, 0:4, :].transpose(1, 0, 2).reshape(4, D)
    lam_full = sm3[:, 4:6, :].transpose(1, 0, 2).reshape(2, D)
    conv_wz = conv_w_full.reshape(4, NH, HD)
    conv_bz = conv_b.reshape(NH, HD)
    lamcat = lam_full.reshape(2, NH, HD).transpose(1, 0, 2).reshape(NH, 2 * HD)
    wa, wx, ba, bx = lru_wa[0], lru_wx[0], lru_ba[0], lru_bx[0]
    wcat = jnp.concatenate([wa[0], wx[0], wa[1], wx[1]], axis=-1).astype(BF16)
    bcat = jnp.concatenate([ba[0], bx[0], ba[1], bx[1]], axis=-1)
    sgu_wb = sgu_w[0].astype(BF16)
    sgu_bt = sgu_b[0].T
    final_g2 = final_g.reshape(1, D)

    zero_s = jnp.zeros((NH, HD), F32)
    hn_c, xa_c = _proj(ctx[0], mods, 1, norm_g, w_in_full, 1, "proj_ctx")
    xaz_c, xcz_c, af_c, ab_c, hf_c, hb_c, gf_c, gb_c, hf0, hb0 = _lru_fwd(xa_c, conv_wz, conv_bz, wcat, bcat, lamcat,
                                                                           zero_s, zero_s, "lru_fwd_ctx")

    hn, xa, ga, u, v, gb, ys = _proj(x[0], mods, 0, norm_g, w_in_full, 5, "proj",
                                     (sgu_ln_g, sgu_ln_b, sgu_wb, sgu_bt))
    xaz, xcz, af, ab, hf, hb, gf, gb_l, _, _ = _lru_fwd(xa, conv_wz, conv_bz, wcat, bcat, lamcat, hf0, hb0, "lru_fwd")

    w_out_full = _late_gather_wait(wo_land, wo_ss, wo_rs, "w_out", hf, "late_gather_wait_w_out")
    (loss_part, dfg, dgx, dxn, y, do, dga, dgb, dyl_z, dys) = _out_fwd_bwd(
        hf, hb, ga, gb, ys, x[0], loss_target[0], mods, final_g2, w_out_full)

    zw = jnp.zeros((NH, HD, 4 * HD), F32)
    zb = jnp.zeros((NH, 4 * HD), F32)
    zl = jnp.zeros((NH, 2 * HD), F32)
    dxc_b, dxc_f, dwc, dbc, dlc, dh0b, dh0f = _lru_bwd(xcz, dyl_z, dyl_z, hf, hb, af, ab, gf, gb_l, hf0, hb0,
                                                        wcat, lamcat, zw, zb, zl, "lru_bwd")
    dxa, dcw, dcb = _conv_bwd(dxc_b, dxc_f, xaz, conv_wz, jnp.zeros((4, NH, HD), F32), zero_s, "conv_bwd")

    zc = jnp.zeros((lc * NH, HD), F32)
    dhf_c = lax.dynamic_update_slice(zc, dh0f, ((lc - 1) * NH, 0))
    dhb_c = lax.dynamic_update_slice(zc, dh0b, (0, 0))
    dxc_bc, dxc_fc, dwc, dbc, dlc, _, _ = _lru_bwd(xcz_c, dhb_c, dhf_c, hf_c, hb_c, af_c, ab_c, gf_c, gb_c,
                                                    zero_s, zero_s, wcat, lamcat, dwc, dbc, dlc, "lru_bwd_ctx")
    dxa_c, dcw, dcb = _conv_bwd(dxc_bc, dxc_fc, xaz_c, conv_wz, dcw, dcb, "conv_bwd_ctx")

    grad_x, dng, dsc_x, dsh_x, du, dv, d_sgu_w, d_sgu_b, d_ln_g, d_ln_b = _proj_bwd(
        [dxa, dga, dgb], x[0], dxn, mods, 0, norm_g, w_in_full, jnp.zeros((1, D), F32), "proj_bwd",
        (u, v, dys, sgu_ln_g, sgu_ln_b, sgu_wb, sgu_bt))
    dzs = [dxa, dga, du, dv, dgb]
    dng, dsc_c, dsh_c = _proj_bwd([dxa_c], ctx[0], None, mods, 1, norm_g, w_in_full, dng, "proj_bwd_ctx")

    dmx = jnp.concatenate([dsh_x, dsc_x, dgx], axis=0)
    dmc = jnp.concatenate([dsh_c, dsc_c, jnp.zeros((1, D), F32)], axis=0)
    lp = loss_part[0, 0]
    lp1 = lax.reduce_precision(lp, 8, 7)
    lp2 = lax.reduce_precision(lp - lp1, 8, 7)
    lp3 = lax.reduce_precision(lp - lp1 - lp2, 8, 7)
    loss_row = jnp.pad(jnp.stack([lp1, lp2, lp3]).reshape(1, 3), ((0, 0), (0, D - 3)))
    slot = jnp.concatenate([dmx, loss_row], axis=0)
    slots = lax.dynamic_update_slice(jnp.zeros((32, D), F32), slot, (4 * dev, 0))
    vecs = jnp.concatenate([dfg, dng, dcb.reshape(1, D), d_ln_g, d_ln_b, dcw.reshape(4, D), dmc,
                            jnp.zeros((4, D), F32), slots], axis=0)
    d_sgu_w4 = d_sgu_w.reshape(4, 256, HD).transpose(1, 0, 2).reshape(256, 4 * HD)
    pad8 = lambda a: jnp.pad(a, ((0, 8 - a.shape[0]), (0, 4 * HD - a.shape[1])))
    pack = jnp.concatenate([dwc.reshape(NH * HD, 4 * HD), pad8(dbc), pad8(dlc), d_sgu_w4, pad8(d_sgu_b),
                            vecs.reshape(96, 4 * HD), jnp.zeros((8, 4 * HD), F32)], axis=0)
    g_w_in, g_w_out, tot = _grads_reduce(hn, dzs, hn_c, dxa_c, y, do, pack)

    g_wc = tot[0:1024].reshape(NH, HD, 4 * HD)
    g_bc = tot[1024:1032]
    g_lc = tot[1032:1040, 0:2 * HD]
    g_sgu_w = tot[1040:1296].reshape(256, 4, HD).transpose(1, 0, 2).reshape(NH, HD, HD)
    g_sgu_b = tot[1296:1304, 0:HD]
    tv = tot[1304:1400].reshape(48, D)
    g_final_g, g_norm_g, g_conv_b, g_ln_g, g_ln_b = tv[0:1], tv[1:2], tv[2:3], tv[3:4], tv[4:5]
    g_conv_w_full = tv[5:9]
    dmc_tot = tv[9:12].reshape(1, 3 * D)
    slots_all = tv[16:48].reshape(8, 4, D)
    dmx_all = slots_all[:, 0:3, :].reshape(8, 3 * D)
    c_all = c_slots.reshape(8, 8, D)[:, 0, :]
    g_lru_wa = jnp.stack([g_wc[:, :, 0:HD], g_wc[:, :, 2 * HD:3 * HD]])
    g_lru_wx = jnp.stack([g_wc[:, :, HD:2 * HD], g_wc[:, :, 3 * HD:4 * HD]])
    g_lru_ba = jnp.stack([g_bc[:, 0:HD], g_bc[:, 2 * HD:3 * HD]])
    g_lru_bx = jnp.stack([g_bc[:, HD:2 * HD], g_bc[:, 3 * HD:4 * HD]])
    g_lam_full = jnp.stack([g_lc[:, 0:HD], g_lc[:, HD:2 * HD]]).reshape(2, D)
    g_conv_w = lax.dynamic_slice(g_conv_w_full, (0, 256 * chip), (4, 256))
    g_lam = lax.dynamic_slice(g_lam_full, (0, 256 * chip), (2, 256))
    dmx_all_j = lax.dynamic_slice(dmx_all, (0, 768 * chip), (8, 768))
    dmc_j = lax.dynamic_slice(dmc_tot, (0, 768 * chip), (1, 768))
    ada_full = _late_gather_wait(ada_land, ada_ss, ada_rs, "ada_w", tot, "late_gather_wait_ada_w")
    g_ada_w, g_ada_b, g_c_ctx = _ada_bwd(c_all, dmx_all_j, dmc_j, dmx_all, dmc_tot, c_ctx2, ada_full)

    big = {
        "ada_w": _adam_big(ada_w[0], g_ada_w, m_ada_w[0], v_ada_w[0], "adam_ada_w"),
        "w_in": _adam_big(w_in[0], g_w_in, m_w_in[0], v_w_in[0], "adam_w_in"),
        "w_out": _adam_big(w_out[0], g_w_out, m_w_out[0], v_w_out[0], "adam_w_out"),
    }
    small_in = {
        "c_ctx": (c_ctx, g_c_ctx, m_c_ctx, v_c_ctx, (1, D)),
        "ada_b": (ada_b, g_ada_b, m_ada_b, v_ada_b, (1, 3 * D)),
        "norm_g": (norm_g, g_norm_g, m_norm_g, v_norm_g, (1, D)),
        "conv_w": (conv_w, g_conv_w, m_conv_w, v_conv_w, (4, 256)),
        "conv_b": (conv_b, g_conv_b, m_conv_b, v_conv_b, (1, D)),
        "lru_wa": (lru_wa, g_lru_wa, m_lru_wa, v_lru_wa, (2 * NH * HD, HD)),
        "lru_ba": (lru_ba, g_lru_ba, m_lru_ba, v_lru_ba, (2 * NH, HD)),
        "lru_wx": (lru_wx, g_lru_wx, m_lru_wx, v_lru_wx, (2 * NH * HD, HD)),
        "lru_bx": (lru_bx, g_lru_bx, m_lru_bx, v_lru_bx, (2 * NH, HD)),
        "lru_lambda": (lru_lambda, g_lam, m_lru_lambda, v_lru_lambda, (2, 256)),
        "sgu_ln_g": (sgu_ln_g, g_ln_g, m_sgu_ln_g, v_sgu_ln_g, (1, D)),
        "sgu_ln_b": (sgu_ln_b, g_ln_b, m_sgu_ln_b, v_sgu_ln_b, (1, D)),
        "sgu_w": (sgu_w, g_sgu_w, m_sgu_w, v_sgu_w, (NH * HD, HD)),
        "sgu_b": (sgu_b, g_sgu_b, m_sgu_b, v_sgu_b, (NH, HD)),
        "final_g": (final_g, g_final_g, m_final_g, v_final_g, (1, D)),
    }
    names_small = list(small_in)
    res_small = _adam_small([tuple(a.reshape(small_in[k][4]) for a in small_in[k][:4]) for k in names_small])
    full_shapes = {"ada_w": ada_w.shape, "w_in": w_in.shape, "w_out": w_out.shape}
    grads, deltas, new_m, new_v = {}, {}, {}, {}
    for k in ("ada_w", "w_in", "w_out"):
        g = {"ada_w": g_ada_w, "w_in": g_w_in, "w_out": g_w_out}[k]
        grads[k] = g.reshape(full_shapes[k])
        deltas[k], new_m[k], new_v[k] = (a.reshape(full_shapes[k]) for a in big[k])
    for k, res in zip(names_small, res_small):
        shape = small_in[k][0].shape
        grads[k] = small_in[k][1].reshape(shape)
        deltas[k], new_m[k], new_v[k] = (a.reshape(shape) for a in res)

    loss = jnp.sum(slots_all[:, 3, 0:3])
    order = ["c_ctx", "ada_w", "ada_b", "norm_g", "w_in", "conv_w", "conv_b", "lru_wa", "lru_ba", "lru_wx", "lru_bx",
             "lru_lambda", "sgu_ln_g", "sgu_ln_b", "sgu_w", "sgu_b", "w_out", "final_g"]
    return (loss, grad_x.reshape(x.shape), *[grads[k] for k in order], *[deltas[k] for k in order],
            *[new_m[k] for k in order], *[new_v[k] for k in order])
```

```python
import functools

import jax
import jax.numpy as jnp
from jax import lax
from jax.experimental import pallas as pl
from jax.experimental.pallas import tpu as pltpu

F32 = jnp.float32
BF16 = jnp.bfloat16

D = 1024
NH = 8
HD = 128
NCHIP = 4
T = 256
NORM_EPS = 1e-6
LN_EPS = 1e-5
LRU_C = 8.0
ADAM_LR = 0.001
ADAM_B1 = 0.9
ADAM_B2 = 0.999
ADAM_EPS = 1e-08
ADAM_WD = 0.01
ADAM_STEP = 10

VMEM = pl.BlockSpec(memory_space=pltpu.VMEM)
ANY = pl.BlockSpec(memory_space=pl.ANY)
MESH = pl.DeviceIdType.MESH


def _cp(n_grid=0, vmem_mb=None):
    kw = {}
    if n_grid:
        kw["dimension_semantics"] = ("arbitrary",) * n_grid
    if vmem_mb:
        kw["vmem_limit_bytes"] = vmem_mb << 20
    return pltpu.CompilerParams(**kw)


def _sigmoid(x):
    return 0.5 * jnp.tanh(0.5 * x) + 0.5


def _silu_and_grad(x):
    s = _sigmoid(x)
    return x * s, s * (1.0 + x * (1.0 - s))


_GELU_K = 0.7978845608028654
_GELU_C = 0.044715


def _gelu_and_grad(x):
    x2 = x * x
    th = jnp.tanh(x * (_GELU_K + (_GELU_K * _GELU_C) * x2))
    p = 0.5 + 0.5 * th
    g = x * p
    dg = p + g * (1.0 - th) * (_GELU_K + (3.0 * _GELU_K * _GELU_C) * x2)
    return g, dg


def _softplus(x):
    return jnp.maximum(x, 0.0) + jnp.log1p(jnp.exp(-jnp.abs(x)))


def _lru_gate(pre, lam_row, d, off=None):
    off = 256 * d if off is None else off
    r = _sigmoid(pre[:, off:off + HD])
    gi = _sigmoid(pre[:, off + HD:off + 2 * HD])
    lam = lam_row[:, HD * d:HD * d + HD]
    sp = _softplus(-lam)
    la = (-LRU_C) * r * sp
    a = jnp.exp(la)
    x2 = 2.0 * la
    m2 = jnp.where(x2 > -1e-3, -x2 * (1.0 + 0.5 * x2), 1.0 - a * a)
    mult = jnp.sqrt(m2)
    return r, gi, lam, sp, a, mult


def _dot(a, b):
    return jnp.dot(a, b, preferred_element_type=F32)


def _dot_tn(a, b):
    return lax.dot_general(a, b, (((0,), (0,)), ((), ())), preferred_element_type=F32)


def _dot_nt(a, b):
    return lax.dot_general(a, b, (((1,), (1,)), ((), ())), preferred_element_type=F32)


def _mo(v, m):
    return v if isinstance(v, int) else pl.multiple_of(v, m)


def _zrows(h, n):
    return pl.ds(h, n, stride=NH)


def _gather_in(c, c_ctx, ada_w, ada_b_j, w_in, w_out, smalls):
    nch = [1, 4]
    wrows = lambda cc, q: (pl.ds(_mo(512 * cc, 16), 512) if q is None
                           else pl.ds(_mo(512 * cc + (512 // nch[1]) * q, 16), 512 // nch[1]))
    specs = [
        ((64, 256), F32, lambda r, jj, cc, q=None: r.at[pl.ds(_mo(16 * jj + 8 * cc, 8), 8), :]),
        ((D, 5120), BF16, lambda r, jj, cc, q=None: r.at[wrows(cc, q), pl.ds(_mo(1280 * jj, 128), 1280)]),
    ]
    halves = [lambda r, cc, q=None: r.at[pl.ds(_mo(8 * cc, 8), 8), :],
              lambda r, cc, q=None: r.at[wrows(cc, q), :]]
    na = len(specs)
    sem_base = [0, 6 * nch[0]]
    sidx = lambda a, q, k: sem_base[a] + 6 * q + k
    n_tiny = 6 * sum(nch)
    n_sem = n_tiny + 10

    def body(c_ref, cc_ref, ada_ref, adab_ref, win_ref, wout_ref, sm_ref,
             mods_o, call_o, sm_o, win_o, wol_o, adal_o, s_win, s_ada, s_wout, f_win, f_ada, f_wout, cslot, lhs, mbuf,
             send_sems, recv_sems, local_sems, load_sems):
        x, y, c = lax.axis_index("x"), lax.axis_index("y"), lax.axis_index("c")
        j = 2 * x + y
        dev = 2 * j + c
        sib = (x, y, 1 - c)
        chips = [(1 - x, y), (x, 1 - y), (1 - x, 1 - y)]
        cj = [2 * cx + cy for cx, cy in chips]
        outs = [sm_o, win_o]
        srcs = [sm_ref, s_win]

        def copy(idx, src, dst, to):
            return pltpu.make_async_remote_copy(src_ref=src, dst_ref=dst, send_sem=send_sems.at[idx],
                                                recv_sem=recv_sems.at[idx], device_id=to, device_id_type=MESH)

        sends = []

        def start(cp):
            cp.start()
            sends.append(cp)

        cslot[...] = jnp.zeros_like(cslot)
        cslot[0:1, :] = c_ref[...]
        my_slot = pl.ds(_mo(8 * dev, 8), 8)
        others = [sib] + [(*chips[k], c) for k in range(3)] + [(*chips[k], 1 - c) for k in range(3)]
        other_dev = [dev + 1 - 2 * c] + [2 * cj[k] + c for k in range(3)] + [2 * cj[k] + 1 - c for k in range(3)]
        base = n_tiny
        for r in range(7):
            start(copy(base + r, cslot, call_o.at[my_slot, :], others[r]))
        call_o[my_slot, :] = cslot[...]

        crow = 512 // nch[1]
        loads = []
        for cc in (c, 1 - c):
            for q in range(nch[1]):
                rows = pl.ds(_mo(512 * cc + crow * q, 16), crow)
                loads.append(pltpu.make_async_copy(win_ref.at[rows, :], f_win.at[rows, :], load_sems.at[len(loads)]))
        loads.append(pltpu.make_async_copy(ada_ref, f_ada, load_sems.at[len(loads)]))
        loads.append(pltpu.make_async_copy(wout_ref, f_wout, load_sems.at[len(loads)]))
        for ld in loads:
            ld.start()
        for k in range(2):
            start(copy(sidx(0, 0, k), halves[0](srcs[0], c), specs[0][2](outs[0], j, c), (*chips[k], c)))
        for q in range(nch[1]):
            loads[q].wait()
            rows = pl.ds(_mo(512 * c + crow * q, 16), crow)
            s_win[rows, :] = f_win[rows, :].astype(BF16)
            for k in range(2):
                start(copy(sidx(1, q, k), halves[1](s_win, c, q), specs[1][2](win_o, j, c, q), (*chips[k], c)))
        for q in range(nch[1]):
            loads[nch[1] + q].wait()
            rows = pl.ds(_mo(512 * (1 - c) + crow * q, 16), crow)
            s_win[rows, :] = f_win[rows, :].astype(BF16)
        local = []
        for a in range(na):
            for cc in range(2):
                lc = pltpu.make_async_copy(halves[a](srcs[a], cc), specs[a][2](outs[a], j, cc), local_sems.at[2 * a + cc])
                lc.start()
                local.append(lc)
        loads[2 * nch[1]].wait()
        s_ada[...] = f_ada[...].astype(BF16)
        loads[2 * nch[1] + 1].wait()
        s_wout[...] = f_wout[...].astype(BF16)
        for q, (src, dst) in enumerate([(s_wout, wol_o.at[pl.ds(_mo(512 * j, 16), 512), :]),
                                        (s_ada, adal_o.at[:, pl.ds(_mo(768 * j, 128), 768)])]):
            lc = pltpu.make_async_copy(src, dst, local_sems.at[2 * na + q])
            lc.start()
            local.append(lc)

        for r in range(7):
            slot = call_o.at[pl.ds(_mo(8 * other_dev[r], 8), 8), :]
            copy(base + r, slot, slot, sib).wait_recv()
        lhs[...] = jnp.zeros_like(lhs)
        for b in range(8):
            cv = call_o[8 * b:8 * b + 1, :]
            lhs[b:b + 1, :] = cv * _sigmoid(cv)
        cv = cc_ref[...]
        lhs[8:9, :] = cv * _sigmoid(cv)
        mbuf[j] = _dot(lhs[...].astype(BF16), s_ada[...]) + adab_ref[...]
        for k in range(3):
            start(copy(base + 7 + k, mbuf.at[j], mbuf.at[j], (*chips[k], c)))
        for k in range(3):
            copy(base + 7 + k, mbuf.at[cj[k]], mbuf.at[cj[k]], sib).wait_recv()
        mods_o[...] = jnp.zeros_like(mods_o)
        for jj in range(NCHIP):
            mods_o[0:1, 768 * jj:768 * jj + 768] = mbuf[jj, pl.ds(dev, 1), :]
            mods_o[1:2, 768 * jj:768 * jj + 768] = mbuf[jj, 8:9, :]

        kx = [1 - x, x, 1 - x]
        ky = [y, 1 - y, 1 - y]
        pick = lambda k, lst: jnp.where(k == 0, lst[0], jnp.where(k == 1, lst[1], lst[2]))
        for a in range(na):
            for q in range(nch[a]):
                for step, k in enumerate([c, 1 - c]):
                    reg = specs[a][2](outs[a], pick(k, cj), c, q)
                    copy(sidx(a, q, k), reg, reg, sib).wait_recv()
                    if step == 0:
                        start(copy(sidx(a, q, 2), reg, reg, (pick(1 - c, kx), pick(1 - c, ky), c)))
                    start(copy(sidx(a, q, 3 + k), reg, reg, sib))
        for a in range(na):
            for q in range(nch[a]):
                reg = specs[a][2](outs[a], cj[2], c, q)
                copy(sidx(a, q, 2), reg, reg, sib).wait_recv()
                start(copy(sidx(a, q, 5), reg, reg, sib))
        for a in range(na):
            for q in range(nch[a]):
                for k in range(3):
                    reg = specs[a][2](outs[a], cj[k], 1 - c, q)
                    copy(sidx(a, q, 3 + k), reg, reg, sib).wait_recv()
        for cp in sends:
            cp.wait_send()
        for lc in local:
            lc.wait()

    out_shape = (jax.ShapeDtypeStruct((8, 3 * D), F32), jax.ShapeDtypeStruct((64, D), F32),
                 jax.ShapeDtypeStruct(specs[0][0], F32), jax.ShapeDtypeStruct(specs[1][0], BF16),
                 jax.ShapeDtypeStruct((2048, D), BF16), jax.ShapeDtypeStruct((D, 3 * D), BF16))
    return pl.pallas_call(
        body, name="gather_in", out_shape=out_shape,
        in_specs=[VMEM, VMEM, ANY, VMEM, ANY, ANY, VMEM], out_specs=(VMEM, VMEM, VMEM, ANY, ANY, ANY),
        scratch_shapes=[pltpu.VMEM((D, 1280), BF16), pltpu.VMEM((D, 768), BF16), pltpu.VMEM((512, D), BF16),
                        pltpu.VMEM((D, 1280), F32), pltpu.VMEM((D, 768), F32), pltpu.VMEM((512, D), F32),
                        pltpu.VMEM((8, D), F32), pltpu.VMEM((16, D), F32), pltpu.VMEM((NCHIP, 16, 768), F32),
                        pltpu.SemaphoreType.DMA((n_sem,)), pltpu.SemaphoreType.DMA((n_sem,)),
                        pltpu.SemaphoreType.DMA((2 * na + 2,)), pltpu.SemaphoreType.DMA((2 * nch[1] + 2,))],
        compiler_params=_cp(vmem_mb=56),
    )(c, c_ctx, ada_w, ada_b_j, w_in, w_out, smalls)


HBM = pl.BlockSpec(memory_space=pltpu.HBM)
SEM = pl.BlockSpec(memory_space=pltpu.SEMAPHORE)


def _late_gather_regions(x, y, c):
    chips = [(1 - x, y), (x, 1 - y), (1 - x, 1 - y)]
    wo_reg = lambda r, jj, cc: r.at[pl.ds(_mo(512 * jj + 256 * cc, 16), 256), :]
    ada_reg = lambda r, jj, cc: r.at[pl.ds(_mo(512 * cc, 16), 512), pl.ds(_mo(768 * jj, 128), 768)]
    return chips, wo_reg, ada_reg


def _late_gather_start(wo_land, ada_land):
    def body(wol_ref, adal_ref, wo_ss, wo_rs, ada_ss, ada_rs, wol_thru, adal_thru, token):
        x, y, c = lax.axis_index("x"), lax.axis_index("y"), lax.axis_index("c")
        j = 2 * x + y
        chips, wo_reg, ada_reg = _late_gather_regions(x, y, c)
        for k in range(3):
            for cc in range(2):
                pltpu.make_async_remote_copy(src_ref=wo_reg(wol_ref, j, c), dst_ref=wo_reg(wol_ref, j, c),
                                             send_sem=wo_ss.at[2 * k + cc], recv_sem=wo_rs.at[2 * k + c],
                                             device_id=(*chips[k], cc), device_id_type=MESH).start()
        for k in range(3):
            for cc in range(2):
                pltpu.make_async_remote_copy(src_ref=ada_reg(adal_ref, j, c), dst_ref=ada_reg(adal_ref, j, c),
                                             send_sem=ada_ss.at[2 * k + cc], recv_sem=ada_rs.at[2 * k + c],
                                             device_id=(*chips[k], cc), device_id_type=MESH).start()
        token[...] = jnp.zeros_like(token)

    sems = pltpu.SemaphoreType.DMA((6,))
    return pl.pallas_call(
        body, name="late_gather_start",
        out_shape=(sems, sems, sems, sems, pltpu.HBM(wo_land.shape, BF16), pltpu.HBM(ada_land.shape, BF16),
                   jax.ShapeDtypeStruct((8, 128), F32)),
        in_specs=(HBM, HBM), out_specs=(SEM, SEM, SEM, SEM, HBM, HBM, VMEM), input_output_aliases={0: 4, 1: 5},
        compiler_params=pltpu.CompilerParams(has_side_effects=pltpu.SideEffectType.DATAFLOW_SIDE_EFFECTING),
    )(pltpu.with_memory_space_constraint(wo_land, pltpu.HBM), pltpu.with_memory_space_constraint(ada_land, pltpu.HBM))


def _late_gather_wait(land, send_sems, recv_sems, which, after, name):
    def body(land_ref, ss, rs, after_ref, land_out):
        x, y, c = lax.axis_index("x"), lax.axis_index("y"), lax.axis_index("c")
        j = 2 * x + y
        chips, wo_reg, ada_reg = _late_gather_regions(x, y, c)
        reg = wo_reg if which == "w_out" else ada_reg
        for k in range(3):
            kj = 2 * chips[k][0] + chips[k][1]
            for cc in range(2):
                cp = pltpu.make_async_remote_copy(src_ref=reg(land_ref, j, c), dst_ref=reg(land_ref, kj, cc),
                                                  send_sem=ss.at[2 * k + cc], recv_sem=rs.at[2 * k + cc],
                                                  device_id=(*chips[k], cc), device_id_type=MESH)
                cp.wait_send()
                cp.wait_recv()

    return pl.pallas_call(
        body, name=name, out_shape=pltpu.HBM(land.shape, land.dtype),
        in_specs=(HBM, SEM, SEM, ANY), out_specs=HBM, input_output_aliases={0: 0},
        compiler_params=pltpu.CompilerParams(has_side_effects=pltpu.SideEffectType.DATAFLOW_SIDE_EFFECTING),
    )(land, send_sems, recv_sems, after)


RCHUNK = 16


def _grads_reduce(hn, dzs, hn_c, dxa_c, y, do, pack):
    rp = pack.shape[0]
    hp = rp // 2
    assert hp % RCHUNK == 0
    wi_w = 1280
    lx, lc = hn.shape[0], hn_c.shape[0]
    lt = lx + lc
    n_dz = len(dzs)

    def body(*refs):
        hn_hbm, dz_hbm = refs[0], refs[1:1 + n_dz]
        hnc_hbm, dxac_hbm, y_hbm, do_hbm, pk_hbm, wi_out, wo_out, pk_out = refs[1 + n_dz:9 + n_dz]
        (hn_mine, hn_other, dzbuf, wi_other, wi_mine, wi_recv, wi_send, wi_rb,
         y_blk, do_mine, do_other, wo_other, wo_mine, wo_recv, wo_send, wo_rb,
         pk_mine, pk_recv, pk_send, pk_rb, pk_own, send_sems, recv_sems, local_sems) = refs[9 + n_dz:]
        x, y, c = lax.axis_index("x"), lax.axis_index("y"), lax.axis_index("c")
        j = 2 * x + y
        sib = (x, y, 1 - c)
        chips = [(1 - x, y), (x, 1 - y), (1 - x, 1 - y)]
        cj = [2 * cx + cy for cx, cy in chips]
        near = (jnp.where(c == 0, 1 - x, x), jnp.where(c == 0, y, 1 - y), c)
        slabs = [cj[2], cj[0], cj[1], j]

        def copy(k, src, dst, to):
            return pltpu.make_async_remote_copy(src_ref=src, dst_ref=dst, send_sem=send_sems.at[k],
                                                recv_sem=recv_sems.at[k], device_id=to, device_id_type=MESH)

        def local(k, src, dst):
            cp = pltpu.make_async_copy(src, dst, local_sems.at[k])
            cp.start()
            return cp

        rows_half = lambda r, cc, n: r.at[pl.ds(_mo(cc * n, 16), n), :]
        cols_half = lambda r, cc, n: r.at[:, pl.ds(_mo(cc * n, 128), n)]
        pk_piece = lambda r, cc, jj: r.at[pl.ds(_mo(cc * hp, 16), hp), pl.ds(_mo(jj * 128, 128), 128)]

        sends = []

        def start(cp):
            cp.start()
            sends.append(cp)

        def dz_pieces(s):
            g0 = wi_w * s
            k0, off0 = g0 // D, g0 % D
            w0 = min(D - off0, wi_w)
            pieces = [(k0, off0, w0, 0)]
            if w0 < wi_w:
                pieces.append((k0 + 1, 0, wi_w - w0, w0))
            return pieces

        def dz_copies(s):
            cps = []
            for q, (k, off, w, dst) in enumerate(dz_pieces(s)):
                cps.append(pltpu.make_async_copy(dz_hbm[k].at[:, pl.ds(off, w)], dzbuf.at[pl.ds(0, lx), pl.ds(dst, w)],
                                                 local_sems.at[11 + q]))
            if s == 0:
                cps.append(pltpu.make_async_copy(dxac_hbm, dzbuf.at[pl.ds(lx, lc), pl.ds(0, D)], local_sems.at[13]))
            return cps

        def dz_load(sl):
            for s in range(NCHIP):
                @pl.when(sl == s)
                def _():
                    if s == 0:
                        dzbuf[pl.ds(lx, lc), pl.ds(D, wi_w - D)] = jnp.zeros((lc, wi_w - D), BF16)
                    else:
                        dzbuf[pl.ds(lx, lc), :] = jnp.zeros((lc, wi_w), BF16)
                    for cp in dz_copies(s):
                        cp.start()

        def dz_wait(sl):
            for s in range(NCHIP):
                @pl.when(sl == s)
                def _():
                    for cp in dz_copies(s):
                        cp.wait()

        l_pk = local(0, rows_half(pk_hbm, c, hp), pk_mine)
        start(copy(0, rows_half(pk_hbm, 1 - c, hp), pk_recv, sib))
        col = lambda r, cc: r.at[:, pl.ds(_mo(cc * 512, 128), 512)]
        do_loads = [local(7, col(do_hbm, c), do_mine), local(14, col(do_hbm, 1 - c), do_other)]
        hn_loads = [local(2, col(hn_hbm, c), hn_mine.at[pl.ds(0, lx), :]),
                    local(3, col(hnc_hbm, c), hn_mine.at[pl.ds(lx, lc), :]),
                    local(4, col(hn_hbm, 1 - c), hn_other.at[pl.ds(0, lx), :]),
                    local(5, col(hnc_hbm, 1 - c), hn_other.at[pl.ds(lx, lc), :])]
        y_copy = lambda s: pltpu.make_async_copy(col(y_hbm, slabs[s]), y_blk, local_sems.at[1])
        y_copy(0).start()
        dz_load(slabs[0])

        def pair_sum(mine, recv, send, nrows, keep, relayed=None):
            def step(i, carry):
                rows = pl.ds(_mo(i * RCHUNK, RCHUNK), RCHUNK)
                s = mine[rows, :] + recv[rows, :].astype(F32)
                if relayed is not None:
                    s = s + relayed[rows, :].astype(F32)
                if keep:
                    mine[rows, :] = s
                if send is not None:
                    send[rows, :] = s.astype(BF16)
                return carry
            lax.fori_loop(0, nrows // RCHUNK, step, 0)

        def chip_sum(own, rb, nrows, terms=(0, 1, 2)):
            def step(i, carry):
                rows = pl.ds(_mo(i * RCHUNK, RCHUNK), RCHUNK)
                acc = own[rows, :]
                for q in terms:
                    acc = acc + rb[q, rows, :].astype(F32)
                own[rows, :] = acc
                return carry
            lax.fori_loop(0, nrows // RCHUNK, step, 0)

        w_in_g = dict(other=wi_other, mine=wi_mine, recv=wi_recv, send=wi_send, rb=wi_rb, p1_sems=(2, 3, 4, 5), p2_sem=12,
                      p1=[None] * NCHIP, wait_load=lambda s: dz_wait(slabs[s]), load=lambda s: dz_load(slabs[s]),
                      dot_other=lambda: _dot_tn(hn_other[...], dzbuf[...]), dot_mine=lambda: _dot_tn(hn_mine[...], dzbuf[...]))
        w_out_g = dict(other=wo_other, mine=wo_mine, recv=wo_recv, send=wo_send, rb=wo_rb, p1_sems=(1, 24, 25, 26), p2_sem=9,
                       p1=[None] * NCHIP, wait_load=lambda s: y_copy(s).wait(), load=lambda s: y_copy(s).start(),
                       dot_other=lambda: _dot_tn(y_blk[...], do_other[...]), dot_mine=lambda: _dot_tn(y_blk[...], do_mine[...]))

        def piece_matmuls(g, s):
            if s >= 2:
                g["p1"][s - 2].wait_send()
            g["wait_load"](s)
            g["other"][s % 2] = g["dot_other"]().astype(BF16)
            g["p1"][s] = copy(g["p1_sems"][s], g["other"].at[s % 2], g["recv"].at[s], sib)
            g["p1"][s].start()
            g["mine"][s % 2] = g["dot_mine"]()
            if s + 1 < NCHIP:
                g["load"](s + 1)

        def piece_finish(g, s):
            mine, recv, send, rb, p2 = g["mine"].at[s % 2], g["recv"].at[s], g["send"], g["rb"], g["p2_sem"]
            nrows = mine.shape[0]
            copy(g["p1_sems"][s], recv, recv, sib).wait_recv()
            if s == 3:
                pair_sum(mine, recv, None, nrows, True)
                return
            if s == 0:
                pair_sum(mine, recv, send.at[0], nrows, False)
                start(copy(p2, send.at[0], rb.at[0], near))
                return
            adds_relayed = c == (1 if s == 1 else 0)

            @pl.when(adds_relayed)
            def _():
                copy(p2, rb.at[0], rb.at[0], sib).wait_recv()
                pair_sum(mine, recv, send.at[s], nrows, False, rb.at[0])

            @pl.when(jnp.logical_not(adds_relayed))
            def _():
                pair_sum(mine, recv, send.at[s], nrows, False)
            start(copy(p2 + s, send.at[s], rb.at[s], (*chips[s - 1], c)))

        def piece_total(g):
            for k in (1, 2):
                copy(g["p2_sem"] + k, g["rb"].at[k], g["rb"].at[k], sib).wait_recv()
            chip_sum(g["mine"].at[1], g["rb"], g["mine"].shape[1], (1, 2))

        for cp in do_loads:
            cp.wait()
        piece_matmuls(w_out_g, 0)
        piece_matmuls(w_out_g, 1)
        piece_finish(w_out_g, 0)
        piece_matmuls(w_out_g, 2)
        piece_finish(w_out_g, 1)
        piece_matmuls(w_out_g, 3)
        piece_finish(w_out_g, 2)

        for cp in hn_loads:
            cp.wait()
        piece_matmuls(w_in_g, 0)

        l_pk.wait()
        copy(0, pk_recv, pk_recv, sib).wait_recv()
        pair_sum(pk_mine, pk_recv, pk_send, hp, True)
        for k in range(3):
            start(copy(6 + k, pk_send.at[:, pl.ds(_mo(cj[k] * 128, 128), 128)], pk_rb.at[k], (*chips[k], c)))
        l_pk_own = local(6, pk_mine.at[:, pl.ds(_mo(j * 128, 128), 128)], pk_own)

        piece_matmuls(w_in_g, 1)
        piece_finish(w_in_g, 0)

        l_pk_own.wait()
        for k in range(3):
            copy(6 + k, pk_rb.at[k], pk_rb.at[k], sib).wait_recv()
        chip_sum(pk_own, pk_rb, hp)
        l_pk_out = local(8, pk_own, pk_piece(pk_out, c, j))
        start(copy(15, pk_own, pk_piece(pk_out, c, j), sib))
        for k in range(2):
            start(copy(16 + k, pk_own, pk_piece(pk_out, c, j), (*chips[k], c)))

        piece_matmuls(w_in_g, 2)
        piece_finish(w_in_g, 1)
        piece_matmuls(w_in_g, 3)
        piece_finish(w_in_g, 2)

        piece_finish(w_out_g, 3)
        piece_total(w_out_g)
        l_wo_out = local(9, wo_mine.at[1], cols_half(wo_out, c, 512))
        start(copy(22, wo_mine.at[1], cols_half(wo_out, c, 512), sib))

        far = (jnp.where(c == 0, x, 1 - x), jnp.where(c == 0, 1 - y, y), c)
        for step, k in enumerate([c, 1 - c, 2]):
            reg = pk_piece(pk_out, c, jnp.where(k == 0, cj[0], jnp.where(k == 1, cj[1], cj[2])))
            copy(16 + k, reg, reg, sib).wait_recv()
            if step == 0:
                start(copy(18, reg, reg, far))
            start(copy(19 + k, reg, reg, sib))

        piece_finish(w_in_g, 3)
        piece_total(w_in_g)
        l_wi_out = local(10, wi_mine.at[1], rows_half(wi_out, c, 512))
        start(copy(23, wi_mine.at[1], rows_half(wi_out, c, 512), sib))

        reg = pk_piece(pk_out, 1 - c, j)
        copy(15, reg, reg, sib).wait_recv()
        for k in range(3):
            reg = pk_piece(pk_out, 1 - c, cj[k])
            copy(19 + k, reg, reg, sib).wait_recv()
        reg = cols_half(wo_out, 1 - c, 512)
        copy(22, reg, reg, sib).wait_recv()
        reg = rows_half(wi_out, 1 - c, 512)
        copy(23, reg, reg, sib).wait_recv()
        for cp in sends + w_in_g["p1"][2:] + w_out_g["p1"][2:]:
            cp.wait_send()
        for cp in (l_pk_out, l_wo_out, l_wi_out):
            cp.wait()

    return pl.pallas_call(
        body, name="grads_reduce",
        out_shape=(jax.ShapeDtypeStruct((D, wi_w), F32), jax.ShapeDtypeStruct((512, D), F32),
                   jax.ShapeDtypeStruct(pack.shape, F32)),
        in_specs=[ANY] * (6 + n_dz), out_specs=(ANY,) * 3,
        scratch_shapes=[
            pltpu.VMEM((lt, 512), BF16), pltpu.VMEM((lt, 512), BF16), pltpu.VMEM((lt, wi_w), BF16),
            pltpu.VMEM((2, 512, wi_w), BF16), pltpu.VMEM((2, 512, wi_w), F32), pltpu.VMEM((4, 512, wi_w), BF16),
            pltpu.VMEM((3, 512, wi_w), BF16), pltpu.VMEM((3, 512, wi_w), BF16),
            pltpu.VMEM((lx, 512), BF16), pltpu.VMEM((lx, 512), BF16), pltpu.VMEM((lx, 512), BF16),
            pltpu.VMEM((2, 512, 512), BF16), pltpu.VMEM((2, 512, 512), F32), pltpu.VMEM((4, 512, 512), BF16),
            pltpu.VMEM((3, 512, 512), BF16), pltpu.VMEM((3, 512, 512), BF16),
            pltpu.VMEM((hp, 512), F32), pltpu.VMEM((hp, 512), F32), pltpu.VMEM((hp, 512), BF16),
            pltpu.VMEM((3, hp, 128), BF16), pltpu.VMEM((hp, 128), F32),
            pltpu.SemaphoreType.DMA((27,)), pltpu.SemaphoreType.DMA((27,)), pltpu.SemaphoreType.DMA((15,))],
        compiler_params=_cp(vmem_mb=56),
    )(hn, *dzs, hn_c, dxa_c, y, do, pack)


def _ada_bwd(c_all, dmx_all_j, dmc_j, dmx_all, dmc, c_ctx, ada_w_full):
    def body(c_ref, dmxj_ref, dmcj_ref, dmx_ref, dmc_ref, cc_ref, w_ref, gw_ref, gb_ref, gc_ref, lhs, rhs, dm8):
        lhs[...] = jnp.zeros_like(lhs)
        rhs[...] = jnp.zeros_like(rhs)
        cv = c_ref[...]
        lhs[0:8, :] = cv * _sigmoid(cv)
        cc = cc_ref[...]
        a_c, da_c = _silu_and_grad(cc)
        lhs[8:9, :] = a_c
        rhs[0:8, :] = dmxj_ref[...]
        rhs[8:9, :] = dmcj_ref[...]
        gw_ref[...] = _dot_tn(lhs[...].astype(BF16), rhs[...].astype(BF16))
        gb_ref[...] = jnp.sum(dmx_ref[...], axis=0, keepdims=True) + dmc_ref[...]
        dm8[...] = jnp.zeros_like(dm8)
        dm8[0:1, :] = dmc_ref[...]
        da = _dot_nt(dm8[...].astype(BF16), w_ref[...])
        gc_ref[...] = da[0:1, :] * da_c

    return pl.pallas_call(
        body, name="ada_bwd",
        out_shape=(jax.ShapeDtypeStruct((D, 768), F32), jax.ShapeDtypeStruct((1, 3 * D), F32),
                   jax.ShapeDtypeStruct((1, D), F32)),
        in_specs=[VMEM] * 7, out_specs=(VMEM,) * 3,
        scratch_shapes=[pltpu.VMEM((16, D), F32), pltpu.VMEM((16, 768), F32), pltpu.VMEM((8, 3 * D), F32)],
        compiler_params=_cp(vmem_mb=32),
    )(c_all, dmx_all_j, dmc_j, dmx_all, dmc, c_ctx, ada_w_full)


def _proj(x, mods, mrow, norm_g, w_full, nk, name, sgu=None):
    lx = x.shape[0]
    n = lx // T
    order = [2, 3, 0, 1, 4] if sgu is not None else list(range(nk))

    def body(x_ref, sh_ref, sc_ref, ng_ref, *rest):
        w_refs, rest = rest[:nk], rest[nk:]
        if sgu is not None:
            g_ref, b_ref, sw_ref, bt_ref = rest[:4]
            rest = rest[4:]
        hn_ref, z_refs = rest[0], rest[1:1 + nk]
        xv = x_ref[...]
        r = lax.rsqrt(jnp.mean(xv * xv, axis=-1, keepdims=True) + NORM_EPS)
        hn = (xv * r) * ng_ref[...] * (1.0 + sc_ref[mrow:mrow + 1, :]) + sh_ref[mrow:mrow + 1, :]
        hb = hn.astype(BF16)
        hn_ref[...] = hb
        for k in order[:2]:
            z_refs[k][...] = _dot(hb, w_refs[k][...])
        if sgu is not None:
            ys_ref, mixed_s = rest[1 + nk], rest[2 + nk]
            for ch in range(T // HD):
                rows = slice(HD * ch, HD * ch + HD)
                ug = _sgu_parts(z_refs[2][rows, :], z_refs[3][rows, :], g_ref[...], b_ref[...], sw_ref, bt_ref,
                                mixed_s)[0]
                ys_ref[rows, :] = ug * mixed_s[...]
        for k in order[2:]:
            z_refs[k][...] = _dot(hb, w_refs[k][...])

    row = pl.BlockSpec((T, D), lambda i: (i, 0))
    vec = pl.BlockSpec((1, D), lambda i: (0, 0))
    in_specs = [row, pl.BlockSpec((8, D), lambda i: (0, 0)), pl.BlockSpec((8, D), lambda i: (0, 1)), vec]
    in_specs += [pl.BlockSpec((D, D), lambda i, k=k: (0, k)) for k in range(nk)]
    args = [x, mods, mods, norm_g] + [w_full] * nk
    out_shape = (jax.ShapeDtypeStruct((lx, D), BF16),) + tuple(jax.ShapeDtypeStruct((lx, D), F32) for _ in range(nk))
    scratch = []
    if sgu is not None:
        in_specs += [vec, vec, pl.BlockSpec((NH, HD, HD), lambda i: (0, 0, 0)), pl.BlockSpec((HD, NH), lambda i: (0, 0))]
        args += list(sgu)
        out_shape += (jax.ShapeDtypeStruct((lx, D), F32),)
        scratch = [pltpu.VMEM((HD, D), F32)]
    return pl.pallas_call(
        body, name=name, grid=(n,), out_shape=out_shape, in_specs=in_specs, out_specs=(row,) * len(out_shape),
        scratch_shapes=scratch, compiler_params=_cp(1, vmem_mb=56),
    )(*args)


def _halo_specs(lx):
    last = lx // 8 - 1
    return [pl.BlockSpec((T, D), lambda i: (i, 0)),
            pl.BlockSpec((8, D), lambda i: (jnp.maximum(i * (T // 8) - 1, 0), 0)),
            pl.BlockSpec((8, D), lambda i: (jnp.minimum((i + 1) * (T // 8), last), 0))]


def _zhalo_specs(lx):
    last = lx // 8 - 1
    return [pl.BlockSpec((T * NH, HD), lambda i: (i, 0)),
            pl.BlockSpec((8 * NH, HD), lambda i: (jnp.maximum(i * (T // 8) - 1, 0), 0)),
            pl.BlockSpec((8 * NH, HD), lambda i: (jnp.minimum((i + 1) * (T // 8), last), 0))]


ZT = pl.BlockSpec((T * NH, HD), lambda i: (i, 0))
CONV_CHUNK = 32


SCAN_SUB = 4


def _scan_tile(chains, post, carry_ref):
    blk = T // SCAN_SUB

    def step(k, state):
        new = []
        for ci, (a_ref, x_ref, o_ref, q_ref, reverse) in enumerate(chains):
            for q in range(SCAN_SUB):
                s, p = state[ci * SCAN_SUB + q]
                t = (q + 1) * blk - 1 - k if reverse else q * blk + k
                r = pl.ds(_mo(t * NH, NH), NH)
                a = a_ref[r, :]
                if post:
                    o = x_ref[r, :] + s
                    o_ref[r, :] = o
                    q_ref[r, :] = p
                    new.append((a * o, a * p))
                else:
                    o = a * s + x_ref[r, :]
                    p = a * p
                    o_ref[r, :] = o
                    q_ref[r, :] = p
                    new.append((o, p))
        return tuple(new)

    zero = jnp.zeros((NH, HD), F32)
    one = jnp.ones((NH, HD), F32)
    final = lax.fori_loop(0, blk, step, tuple((zero, one) for _ in range(len(chains) * SCAN_SUB)), unroll=2)
    for ci, (a_ref, x_ref, o_ref, q_ref, reverse) in enumerate(chains):
        carry = carry_ref[ci]
        for q in (range(SCAN_SUB - 1, -1, -1) if reverse else range(SCAN_SUB)):
            rows = pl.ds(q * blk * NH, blk * NH)
            fixed = o_ref[rows, :].reshape(blk, NH, HD) + q_ref[rows, :].reshape(blk, NH, HD) * carry[None]
            o_ref[rows, :] = fixed.reshape(blk * NH, HD)
            s_loc, p_loc = final[ci * SCAN_SUB + q]
            carry = s_loc + p_loc * carry
        carry_ref[ci] = carry


def _lru_fwd(xa, conv_wz, conv_bz, wcat, bcat, lamcat, s_f, s_b, name):
    lx = xa.shape[0]
    n = lx // T

    def body(xm_u, xp_u, xn_u, xm_d, xp_d, xn_d, cw, cb, w_ref, b_ref, lam_ref, su0, sd0,
             xaz_o, xcz_o, af_o, ab_o, hf_o, hb_o, gf_o, gb_o, fu, fd, pad, xc_d, x_u, x_d, q_u, q_d, carry):
        i = pl.program_id(0)

        @pl.when(i == 0)
        def _():
            carry[0] = su0[...]
            carry[1] = sd0[...]

        def conv_gates(xm, xp, xn, tile, d, xc_ref, a_ref, x_ref, xaz_ref, g_ref):
            pmask = jnp.where(tile == 0, 0.0, 1.0)
            nmask = jnp.where(tile == n - 1, 0.0, 1.0)
            for h in range(NH):
                cols = slice(HD * h, HD * h + HD)
                pad[_zrows(h, 8), :] = xp[:, cols] * pmask
                pad[pl.ds(8 * NH + h, T, stride=NH), :] = xm[:, cols]
                pad[pl.ds((T + 8) * NH + h, 8, stride=NH), :] = xn[:, cols] * nmask
            if xaz_ref is not None:
                xaz_ref[...] = pad[pl.ds(8 * NH, T * NH), :]

            def conv_chunk(ci, c_):
                base = pl.multiple_of(ci * (CONV_CHUNK * NH), CONV_CHUNK * NH)
                acc = None
                for k in range(4):
                    sl = pad[pl.ds(base + (7 + k) * NH, CONV_CHUNK * NH), :].reshape(CONV_CHUNK, NH, HD)
                    term = sl * cw[k][None]
                    acc = term if acc is None else acc + term
                acc = acc + cb[...][None]
                xc_ref[pl.ds(base, CONV_CHUNK * NH), :] = acc.reshape(CONV_CHUNK * NH, HD)
                return c_
            lax.fori_loop(0, T // CONV_CHUNK, conv_chunk, 0)

            for h in range(NH):
                xch = xc_ref[_zrows(h, T), :]
                pre = _dot(xch.astype(BF16), w_ref[h, :, 256 * d:256 * d + 256]) + b_ref[h:h + 1, 256 * d:256 * d + 256]
                r, gi, _, _, a, mult = _lru_gate(pre, lam_ref[h:h + 1, :], d, 0)
                a_ref[_zrows(h, T), :] = a
                x_ref[_zrows(h, T), :] = mult * gi * xch
                for q, val in enumerate((r, gi, mult)):
                    g_ref[:, q * D + HD * h:q * D + HD * h + HD] = val

        conv_gates(xm_u, xp_u, xn_u, i, 0, xcz_o, af_o, x_u, xaz_o, gf_o)
        conv_gates(xm_d, xp_d, xn_d, n - 1 - i, 1, xc_d, ab_o, x_d, None, gb_o)

        _scan_tile([(af_o, x_u, hf_o, q_u, False), (ab_o, x_d, hb_o, q_d, True)], False, carry)
        fu[...] = carry[0]
        fd[...] = carry[1]

    full = lambda shape: pl.BlockSpec(shape, lambda i: (0,) * len(shape))
    last = lx // 8 - 1
    rev = lambda i: n - 1 - i
    halo_dn = [pl.BlockSpec((T, D), lambda i: (rev(i), 0)),
               pl.BlockSpec((8, D), lambda i: (jnp.maximum(rev(i) * (T // 8) - 1, 0), 0)),
               pl.BlockSpec((8, D), lambda i: (jnp.minimum((rev(i) + 1) * (T // 8), last), 0))]
    st = full((NH, HD))
    in_specs = _halo_specs(lx) + halo_dn + [full((4, NH, HD)), st, full((NH, HD, 4 * HD)), full((NH, 4 * HD)),
                                            full((NH, 2 * HD)), st, st]
    dn = pl.BlockSpec((T * NH, HD), lambda i: (rev(i), 0))
    zs = jax.ShapeDtypeStruct((lx * NH, HD), F32)
    ss = jax.ShapeDtypeStruct((NH, HD), F32)
    zbuf = pltpu.VMEM((T * NH, HD), F32)
    gs = jax.ShapeDtypeStruct((lx, 3 * D), F32)
    g_up = pl.BlockSpec((T, 3 * D), lambda i: (i, 0))
    g_dn = pl.BlockSpec((T, 3 * D), lambda i: (rev(i), 0))
    return pl.pallas_call(
        body, name=name, grid=(n,), out_shape=(zs,) * 6 + (gs, gs, ss, ss), in_specs=in_specs,
        out_specs=(ZT, ZT, ZT, dn, ZT, dn, g_up, g_dn, st, st),
        scratch_shapes=[pltpu.VMEM(((T + 16) * NH, HD), F32), zbuf, zbuf, zbuf, zbuf, zbuf,
                        pltpu.VMEM((2, NH, HD), F32)],
        compiler_params=_cp(1, vmem_mb=48),
    )(xa, xa, xa, xa, xa, xa, conv_wz, conv_bz, wcat, bcat, lamcat, s_f, s_b)


def _sgu_parts(u, v, lng, lnb, w_ref, bt_ref, mixed_s):
    ug, dug = _gelu_and_grad(u)
    vg, dvg = _gelu_and_grad(v)
    mu = jnp.mean(vg, axis=-1, keepdims=True)
    vc = vg - mu
    rstd = lax.rsqrt(jnp.mean(vc * vc, axis=-1, keepdims=True) + LN_EPS)
    vh = vc * rstd
    vn = (vh * lng + lnb).astype(BF16)
    for g in range(NH):
        cols = slice(HD * g, HD * g + HD)
        mixed_s[:, cols] = _dot(w_ref[g], vn[:, cols]) + bt_ref[:, g:g + 1]
    return ug, dug, dvg, rstd, vh, vn


def _sgu_bwd_chunk(u, v, dys_v, lng, lnb, w_ref, bt_ref, mixed_s, dvn_s, dw_ref, db_ref, dg_ref, dbl_ref):
    ug, dug, dvg, rstd, vh, vn = _sgu_parts(u, v, lng, lnb, w_ref, bt_ref, mixed_s)
    du = (dys_v * mixed_s[...] * dug).astype(BF16)
    dmix = dys_v * ug
    ones = jnp.ones((8, HD), BF16)
    for g in range(NH):
        cols = slice(HD * g, HD * g + HD)
        dm = dmix[:, cols]
        hi = dm.astype(BF16)
        lo = (dm - hi.astype(F32)).astype(BF16)
        dw_ref[g] += _dot_nt(hi, vn[:, cols])
        db_ref[g:g + 1, :] += (_dot_nt(ones, hi) + _dot_nt(ones, lo))[0:1, :]
        dvn_s[:, cols] = _dot_tn(w_ref[g], hi)
    dvn = dvn_s[...]
    dg_ref[...] += jnp.sum(dvn * vh, axis=0, keepdims=True)
    dbl_ref[...] += jnp.sum(dvn, axis=0, keepdims=True)
    dvh = dvn * lng
    dvg_in = rstd * (dvh - jnp.mean(dvh, axis=-1, keepdims=True) - vh * jnp.mean(dvh * vh, axis=-1, keepdims=True))
    return du, (dvg_in * dvg).astype(BF16)


def _out_fwd_bwd(hf_z, hb_z, ga, gb, ys, x, tgt, mods, final_g, w_out_full):
    lx = x.shape[0]
    n = lx // T

    def body(hf_ref, hb_ref, ga_ref, gb_ref, ys_ref, x_ref, t_ref, gx_ref, fg_ref, w_ref,
             loss_ref, dfg_ref, dgx_ref, dxn_ref, y_ref, do_ref, dga_ref, dgb_ref, dyl_ref, dys_ref, yl_s):
        i = pl.program_id(0)

        @pl.when(i == 0)
        def _():
            loss_ref[...] = jnp.zeros_like(loss_ref)
            dfg_ref[...] = jnp.zeros_like(dfg_ref)
            dgx_ref[...] = jnp.zeros_like(dgx_ref)

        for h in range(NH):
            yl_s[:, HD * h:HD * h + HD] = hf_ref[_zrows(h, T), :] + hb_ref[_zrows(h, T), :]
        yl = yl_s[...]
        gav = ga_ref[...]
        gbv = gb_ref[...]
        sa, dsa = _silu_and_grad(gav)
        sb, dsb = _silu_and_grad(gbv)
        ysv = ys_ref[...]
        y_ref[:, 0:D] = (yl * sa).astype(BF16)
        y_ref[:, D:2 * D] = (ysv * sb).astype(BF16)
        o = _dot(y_ref[...], w_ref[...])
        gx = gx_ref[0:1, :]
        xnew = x_ref[...] + gx * o
        r2 = lax.rsqrt(jnp.mean(xnew * xnew, axis=-1, keepdims=True) + NORM_EPS)
        xh = xnew * r2
        fg = fg_ref[...]
        err = xh * fg - t_ref[...]
        loss_ref[...] += 0.5 * jnp.sum(jnp.mean(err * err, axis=-1, keepdims=True), axis=0, keepdims=True)
        dout = err * (1.0 / D)
        dfg_ref[...] += jnp.sum(dout * xh, axis=0, keepdims=True)
        dxh = dout * fg
        dxn = r2 * (dxh - xh * jnp.mean(dxh * xh, axis=-1, keepdims=True))
        dxn_ref[...] = dxn
        dgx_ref[...] += jnp.sum(dxn * o, axis=0, keepdims=True)
        do = (dxn * gx).astype(BF16)
        do_ref[...] = do
        dy = _dot_nt(do, w_ref[...])
        dy1 = dy[:, 0:D]
        dy2 = dy[:, D:2 * D]
        dga_ref[...] = (dy1 * yl * dsa).astype(BF16)
        dgb_ref[...] = (dy2 * ysv * dsb).astype(BF16)
        dys_ref[...] = dy2 * sb
        yl_s[...] = dy1 * sa
        for h in range(NH):
            dyl_ref[_zrows(h, T), :] = yl_s[:, HD * h:HD * h + HD]

    row = pl.BlockSpec((T, D), lambda i: (i, 0))
    vec = pl.BlockSpec((1, D), lambda i: (0, 0))
    in_specs = [ZT, ZT, row, row, row, row, row, pl.BlockSpec((8, D), lambda i: (0, 2)), vec,
                pl.BlockSpec((2 * D, D), lambda i: (0, 0))]
    out_shape = (jax.ShapeDtypeStruct((1, 1), F32), jax.ShapeDtypeStruct((1, D), F32), jax.ShapeDtypeStruct((1, D), F32),
                 jax.ShapeDtypeStruct((lx, D), F32), jax.ShapeDtypeStruct((lx, 2 * D), BF16),
                 jax.ShapeDtypeStruct((lx, D), BF16), jax.ShapeDtypeStruct((lx, D), BF16),
                 jax.ShapeDtypeStruct((lx, D), BF16), jax.ShapeDtypeStruct((lx * NH, HD), F32),
                 jax.ShapeDtypeStruct((lx, D), F32))
    out_specs = (pl.BlockSpec((1, 1), lambda i: (0, 0)), vec, vec, row, pl.BlockSpec((T, 2 * D), lambda i: (i, 0)),
                 row, row, row, ZT, row)
    return pl.pallas_call(
        body, name="out_fwd_bwd", grid=(n,), out_shape=out_shape, in_specs=in_specs, out_specs=out_specs,
        scratch_shapes=[pltpu.VMEM((T, D), F32)],
        compiler_params=_cp(1, vmem_mb=56),
    )(hf_z, hb_z, ga, gb, ys, x, tgt, mods, final_g, w_out_full)


def _lru_bwd(xc_z, dy_up, dy_dn, hf_z, hb_z, af_z, ab_z, gf, gb, s_f, s_b, wcat, lamcat, dw0, db0, dl0, name):
    lx = xc_z.shape[0] // NH
    n = lx // T

    def body(xc_u, dy_u, hb_ref, hbn_ref, ab_ref, gb_ref, xc_d, dy_d, hf_ref, hfp_ref, af_ref, gf_ref,
             sf_ref, sb_ref, w_ref, lam_ref, dw0_ref, db0_ref, dl0_ref,
             dxcb_ref, dxcf_ref, dw_ref, db_ref, dl_ref, fu, fd, lb_s, lf_s, q_u, q_d, pf_s, pb_s, dpre_s, carry):
        i = pl.program_id(0)

        @pl.when(i == 0)
        def _():
            dw_ref[...] = dw0_ref[...]
            db_ref[...] = db0_ref[...]
            dl_ref[...] = dl0_ref[...]
            carry[...] = jnp.zeros_like(carry)

        _scan_tile([(ab_ref, dy_u, lb_s, q_u, False), (af_ref, dy_d, lf_s, q_d, True)], True, carry)
        fu[...] = carry[0]
        fd[...] = carry[1]
        pb_s[pl.ds(0, T * NH), :] = hb_ref[...]
        pb_s[pl.ds(T * NH, NH), :] = jnp.where(i == n - 1, sb_ref[...], hbn_ref[pl.ds(0, NH), :])
        pf_s[pl.ds(0, NH), :] = jnp.where(i == n - 1, sf_ref[...], hfp_ref[pl.ds(7 * NH, NH), :])
        pf_s[pl.ds(NH, T * NH), :] = hf_ref[...]
        sides = ((1, xc_u, lb_s, pb_s, NH, ab_ref, gb_ref, dxcb_ref), (0, xc_d, lf_s, pf_s, 0, af_ref, gf_ref, dxcf_ref))
        for d, xc_ref, adj_s, prev_s, prev_off, a_ref, g_ref, dxc_ref in sides:
            wcols = slice(256 * d, 256 * d + 256)
            for h in range(NH):
                xch = xc_ref[_zrows(h, T), :]
                xcb = xch.astype(BF16)
                r, gi, mult = (g_ref[:, q * D + HD * h:q * D + HD * h + HD] for q in range(3))
                a = a_ref[_zrows(h, T), :]
                lam = lam_ref[h:h + 1, HD * d:HD * d + HD]
                sp = _softplus(-lam)
                du = adj_s[_zrows(h, T), :]
                da = du * prev_s[pl.ds(prev_off + h, T, stride=NH), :]
                dgi = du * mult * xch
                dmult = du * gi * xch
                dla = da * a - dmult * (a * a) / mult
                dr = dla * ((-LRU_C) * sp)
                dsp = jnp.sum(dla * ((-LRU_C) * r), axis=0, keepdims=True)
                dl_ref[h:h + 1, HD * d:HD * d + HD] += dsp * (-_sigmoid(-lam))
                dpre_s[:, 0:HD] = dr * r * (1.0 - r)
                dpre_s[:, HD:2 * HD] = dgi * gi * (1.0 - gi)
                dpre = dpre_s[...]
                dpb = dpre.astype(BF16)
                dw_ref[h, :, wcols] += _dot_tn(xcb, dpb)
                db_ref[h:h + 1, wcols] += jnp.sum(dpre, axis=0, keepdims=True)
                dxc_ref[_zrows(h, T), :] = du * mult * gi + _dot_nt(dpb, w_ref[h, :, wcols])

    full = lambda shape: pl.BlockSpec(shape, lambda i: (0,) * len(shape))
    wsp, bsp, lsp = full((NH, HD, 4 * HD)), full((NH, 4 * HD)), full((NH, 2 * HD))
    st = full((NH, HD))
    rev = lambda i: n - 1 - i
    up = ZT
    dn = pl.BlockSpec((T * NH, HD), lambda i: (rev(i), 0))
    nxt = _zhalo_specs(lx)[2]
    prv = pl.BlockSpec((8 * NH, HD), lambda i: (jnp.maximum(rev(i) * (T // 8) - 1, 0), 0))
    g_up = pl.BlockSpec((T, 3 * D), lambda i: (i, 0))
    g_dn = pl.BlockSpec((T, 3 * D), lambda i: (rev(i), 0))
    zs = jax.ShapeDtypeStruct((lx * NH, HD), F32)
    ss = jax.ShapeDtypeStruct((NH, HD), F32)
    zbuf = pltpu.VMEM((T * NH, HD), F32)
    zbuf1 = pltpu.VMEM(((T + 1) * NH, HD), F32)
    return pl.pallas_call(
        body, name=name, grid=(n,),
        out_shape=(zs, zs, jax.ShapeDtypeStruct((NH, HD, 4 * HD), F32), jax.ShapeDtypeStruct((NH, 4 * HD), F32),
                   jax.ShapeDtypeStruct((NH, 2 * HD), F32), ss, ss),
        in_specs=[up, up, up, nxt, up, g_up, dn, dn, dn, prv, dn, g_dn, st, st, wsp, lsp, wsp, bsp, lsp],
        out_specs=(up, dn, wsp, bsp, lsp, st, st),
        scratch_shapes=[zbuf, zbuf, zbuf, zbuf, zbuf1, zbuf1, pltpu.VMEM((T, 2 * HD), F32), pltpu.VMEM((2, NH, HD), F32)],
        compiler_params=_cp(1, vmem_mb=56),
    )(xc_z, dy_up, hb_z, hb_z, ab_z, gb, xc_z, dy_dn, hf_z, hf_z, af_z, gf, s_f, s_b, wcat, lamcat, dw0, db0, dl0)


def _conv_bwd(dxc_a, dxc_b, xa_z, conv_wz, dcw0, dcb0, name):
    lx = dxc_a.shape[0] // NH
    n = lx // T

    def body(dm_a, dp_a, dn_a, dm_b, dp_b, dn_b, xa_ref, cw, dcw0_ref, dcb0_ref, dxa_ref, dcw_ref, dcb_ref, pad, dxa_s):
        i = pl.program_id(0)

        @pl.when(i == 0)
        def _():
            dcw_ref[...] = dcw0_ref[...]
            dcb_ref[...] = dcb0_ref[...]

        pmask = jnp.where(i == 0, 0.0, 1.0)
        nmask = jnp.where(i == n - 1, 0.0, 1.0)
        pad[pl.ds(0, 8 * NH), :] = (dp_a[...] + dp_b[...]) * pmask
        pad[pl.ds(8 * NH, T * NH), :] = dm_a[...] + dm_b[...]
        pad[pl.ds((T + 8) * NH, 8 * NH), :] = (dn_a[...] + dn_b[...]) * nmask

        def chunk(ci, carry):
            base = pl.multiple_of(ci * (CONV_CHUNK * NH), CONV_CHUNK * NH)
            xav = xa_ref[pl.ds(base, CONV_CHUNK * NH), :].reshape(CONV_CHUNK, NH, HD)
            acc = None
            for k in range(4):
                sl = pad[pl.ds(base + (9 - k) * NH, CONV_CHUNK * NH), :].reshape(CONV_CHUNK, NH, HD)
                term = sl * cw[k][None]
                acc = term if acc is None else acc + term
                dcw_ref[k] += jnp.sum(sl * xav, axis=0)
                if k == 1:
                    dcb_ref[...] += jnp.sum(sl, axis=0)
            dxa_s[pl.ds(base, CONV_CHUNK * NH), :] = acc.reshape(CONV_CHUNK * NH, HD)
            return carry
        lax.fori_loop(0, T // CONV_CHUNK, chunk, 0)
        for h in range(NH):
            dxa_ref[:, HD * h:HD * h + HD] = dxa_s[_zrows(h, T), :].astype(BF16)

    full = lambda shape: pl.BlockSpec(shape, lambda i: (0,) * len(shape))
    return pl.pallas_call(
        body, name=name, grid=(n,),
        out_shape=(jax.ShapeDtypeStruct((lx, D), BF16), jax.ShapeDtypeStruct((4, NH, HD), F32),
                   jax.ShapeDtypeStruct((NH, HD), F32)),
        in_specs=_zhalo_specs(lx) * 2 + [ZT, full((4, NH, HD)), full((4, NH, HD)), full((NH, HD))],
        out_specs=(pl.BlockSpec((T, D), lambda i: (i, 0)), full((4, NH, HD)), full((NH, HD))),
        scratch_shapes=[pltpu.VMEM(((T + 16) * NH, HD), F32), pltpu.VMEM((T * NH, HD), F32)],
        compiler_params=_cp(1, vmem_mb=48),
    )(dxc_a, dxc_a, dxc_a, dxc_b, dxc_b, dxc_b, xa_z, conv_wz, dcw0, dcb0)


def _proj_bwd(dzs, x, dxn, mods, mrow, norm_g, w_full, dng0, name, sgu=None):
    lx = x.shape[0]
    n = lx // T
    nz = len(dzs)
    has_x = dxn is not None
    wks = ([0, 1, 4, 2, 3] if sgu is not None else list(range(nz)))

    def body(*refs):
        it = iter(refs)
        take = lambda m: [next(it) for _ in range(m)]
        dz_refs, w_refs = take(nz), take(len(wks))
        x_ref, sc_ref, ng_ref, dng0_ref = take(4)
        dxn_ref = take(1)[0] if has_x else None
        if sgu is not None:
            u_ref, v_ref, dy_ref, g_ref, b_ref, sw_ref, bt_ref = take(7)
        gx_ref = take(1)[0] if has_x else None
        dng_ref, dsc_ref, dsh_ref = take(3)
        if sgu is not None:
            du_ref, dv_ref, dws_ref, dbs_ref, dlg_ref, dlb_ref, mixed_s, dvn_s = take(8)
        i = pl.program_id(0)

        @pl.when(i == 0)
        def _():
            dng_ref[...] = dng0_ref[...]
            dsc_ref[...] = jnp.zeros_like(dsc_ref)
            dsh_ref[...] = jnp.zeros_like(dsh_ref)
            if sgu is not None:
                for acc in (dws_ref, dbs_ref, dlg_ref, dlb_ref):
                    acc[...] = jnp.zeros_like(acc)

        dhn = _dot_nt(dz_refs[0][...], w_refs[0][...])
        for k in range(1, nz):
            dhn = dhn + _dot_nt(dz_refs[k][...], w_refs[k][...])
        if sgu is not None:
            for ch in range(T // HD):
                rows = slice(HD * ch, HD * ch + HD)
                du, dv = _sgu_bwd_chunk(u_ref[rows, :], v_ref[rows, :], dy_ref[rows, :], g_ref[...], b_ref[...],
                                        sw_ref, bt_ref, mixed_s, dvn_s, dws_ref, dbs_ref, dlg_ref, dlb_ref)
                du_ref[rows, :] = du
                dv_ref[rows, :] = dv
            dhn = dhn + _dot_nt(du_ref[...], w_refs[nz][...]) + _dot_nt(dv_ref[...], w_refs[nz + 1][...])
        xv = x_ref[...]
        r = lax.rsqrt(jnp.mean(xv * xv, axis=-1, keepdims=True) + NORM_EPS)
        xn = xv * r
        ng = ng_ref[...]
        sc1 = 1.0 + sc_ref[mrow:mrow + 1, :]
        t = dhn * xn
        dng_ref[...] += jnp.sum(t * sc1, axis=0, keepdims=True)
        dsc_ref[...] += jnp.sum(t * ng, axis=0, keepdims=True)
        dsh_ref[...] += jnp.sum(dhn, axis=0, keepdims=True)
        if has_x:
            dxh = dhn * (ng * sc1)
            gx_ref[...] = dxn_ref[...] + r * (dxh - xn * jnp.mean(dxh * xn, axis=-1, keepdims=True))

    row = pl.BlockSpec((T, D), lambda i: (i, 0))
    vec = pl.BlockSpec((1, D), lambda i: (0, 0))
    in_specs = [row] * nz + [pl.BlockSpec((D, D), lambda i, k=k: (0, k)) for k in wks]
    in_specs += [row, pl.BlockSpec((8, D), lambda i: (0, 1)), vec, vec]
    args = list(dzs) + [w_full] * len(wks) + [x, mods, norm_g, dng0]
    vs = jax.ShapeDtypeStruct((1, D), F32)
    out_shape, out_specs = (vs, vs, vs), (vec, vec, vec)
    scratch = []
    if has_x:
        in_specs.append(row)
        args.append(dxn)
        out_shape = (jax.ShapeDtypeStruct((lx, D), F32),) + out_shape
        out_specs = (row,) + out_specs
    if sgu is not None:
        wsp = pl.BlockSpec((NH, HD, HD), lambda i: (0, 0, 0))
        bsp = pl.BlockSpec((NH, HD), lambda i: (0, 0))
        in_specs += [row, row, row, vec, vec, wsp, pl.BlockSpec((HD, NH), lambda i: (0, 0))]
        args += list(sgu)
        zb = jax.ShapeDtypeStruct((lx, D), BF16)
        out_shape += (zb, zb, jax.ShapeDtypeStruct((NH, HD, HD), F32), jax.ShapeDtypeStruct((NH, HD), F32), vs, vs)
        out_specs += (row, row, wsp, bsp, vec, vec)
        scratch = [pltpu.VMEM((HD, D), F32), pltpu.VMEM((HD, D), F32)]
    return pl.pallas_call(
        body, name=name, grid=(n,), out_shape=out_shape, in_specs=in_specs, out_specs=out_specs,
        scratch_shapes=scratch, compiler_params=_cp(1, vmem_mb=56),
    )(*args)


def _adam_math(w, g, m, v):
    m = ADAM_B1 * m + (1.0 - ADAM_B1) * g
    v = ADAM_B2 * v + (1.0 - ADAM_B2) * (g * g)
    m_hat = m / (1.0 - ADAM_B1 ** ADAM_STEP)
    v_hat = v / (1.0 - ADAM_B2 ** ADAM_STEP)
    delta = -ADAM_LR * (m_hat / (jnp.sqrt(v_hat) + ADAM_EPS) + ADAM_WD * w)
    return delta, m, v


def _adam_big(w, g, m, v, name):
    rows, cols = w.shape
    tr = 256

    def body(w_ref, g_ref, m_ref, v_ref, d_o, m_o, v_o):
        d, mm, vv = _adam_math(w_ref[...], g_ref[...], m_ref[...], v_ref[...])
        d_o[...] = d
        m_o[...] = mm
        v_o[...] = vv

    blk = pl.BlockSpec((tr, cols), lambda i: (i, 0))
    s = jax.ShapeDtypeStruct((rows, cols), F32)
    return pl.pallas_call(
        body, name=name, grid=(rows // tr,), out_shape=(s, s, s), in_specs=[blk] * 4, out_specs=(blk,) * 3,
        compiler_params=_cp(1, vmem_mb=48),
    )(w, g, m, v)


def _adam_small(items):
    ni = len(items)

    def body(*refs):
        ins, outs = refs[:4 * ni], refs[4 * ni:7 * ni]
        bufs_in, bufs_out = refs[7 * ni:11 * ni], refs[11 * ni:14 * ni]
        sem_in, sem_out = refs[14 * ni], refs[14 * ni + 1]
        loads = [pltpu.make_async_copy(ins[q], bufs_in[q], sem_in.at[q]) for q in range(4 * ni)]
        for cp in loads:
            cp.start()
        stores = []
        for k in range(ni):
            for q in range(4):
                loads[4 * k + q].wait()
            w_b, g_b, m_b, v_b = bufs_in[4 * k:4 * k + 4]
            res = _adam_math(w_b[...], g_b[...], m_b[...], v_b[...])
            for q in range(3):
                bufs_out[3 * k + q][...] = res[q]
                cp = pltpu.make_async_copy(bufs_out[3 * k + q], outs[3 * k + q], sem_out.at[3 * k + q])
                cp.start()
                stores.append(cp)
        for cp in stores:
            cp.wait()

    flat = [a for it in items for a in it]
    out_shape = tuple(jax.ShapeDtypeStruct(it[0].shape, F32) for it in items for _ in range(3))
    scratch = [pltpu.VMEM(a.shape, F32) for a in flat] + [pltpu.VMEM(s.shape, F32) for s in out_shape]
    scratch += [pltpu.SemaphoreType.DMA((4 * ni,)), pltpu.SemaphoreType.DMA((3 * ni,))]
    res = pl.pallas_call(
        body, name="adam_small", out_shape=out_shape, in_specs=[HBM] * (4 * ni), out_specs=(HBM,) * (3 * ni),
        scratch_shapes=scratch, compiler_params=_cp(vmem_mb=40),
    )(*flat)
    return [tuple(res[3 * k:3 * k + 3]) for k in range(ni)]


def kernel(x, c, ctx, c_ctx, ada_w, ada_b, norm_g, w_in, conv_w, conv_b, lru_wa, lru_ba, lru_wx, lru_bx, lru_lambda, sgu_ln_g, sgu_ln_b, sgu_w, sgu_b, w_out, final_g, loss_target, m_c_ctx, m_ada_w, m_ada_b, m_norm_g, m_w_in, m_conv_w, m_conv_b, m_lru_wa, m_lru_ba, m_lru_wx, m_lru_bx, m_lru_lambda, m_sgu_ln_g, m_sgu_ln_b, m_sgu_w, m_sgu_b, m_w_out, m_final_g, v_c_ctx, v_ada_w, v_ada_b, v_norm_g, v_w_in, v_conv_w, v_conv_b, v_lru_wa, v_lru_ba, v_lru_wx, v_lru_bx, v_lru_lambda, v_sgu_ln_g, v_sgu_ln_b, v_sgu_w, v_sgu_b, v_w_out, v_final_g):
    ix, iy, ic = lax.axis_index("x"), lax.axis_index("y"), lax.axis_index("c")
    chip = 2 * ix + iy
    dev = 2 * chip + ic
    lx = x.shape[1]
    lc = ctx.shape[1]

    smalls = jnp.concatenate([conv_w[0], lru_lambda[0], jnp.zeros((10, 256), F32)], axis=0)
    c_ctx2 = c_ctx.reshape(1, D)
    ada_b_j = lax.dynamic_slice(ada_b, (0, 768 * chip), (1, 768))
    mods, c_slots, sm_all, w_in_full, wo_land, ada_land = _gather_in(c, c_ctx2, ada_w[0], ada_b_j, w_in[0], w_out[0],
                                                                     smalls)
    wo_ss, wo_rs, ada_ss, ada_rs, wo_land, ada_land, token = _late_gather_start(wo_land, ada_land)
    mods = mods + token[0:1, 0:1]
    sm3 = sm_all.reshape(NCHIP, 16, 256)
    conv_w_full = sm3[:, 0:4, :].transpose(1, 0, 2).reshape(4, D)
    lam_full = sm3[:, 4:6, :].transpose(1, 0, 2).reshape(2, D)
    conv_wz = conv_w_full.reshape(4, NH, HD)
    conv_bz = conv_b.reshape(NH, HD)
    lamcat = lam_full.reshape(2, NH, HD).transpose(1, 0, 2).reshape(NH, 2 * HD)
    wa, wx, ba, bx = lru_wa[0], lru_wx[0], lru_ba[0], lru_bx[0]
    wcat = jnp.concatenate([wa[0], wx[0], wa[1], wx[1]], axis=-1).astype(BF16)
    bcat = jnp.concatenate([ba[0], bx[0], ba[1], bx[1]], axis=-1)
    sgu_wb = sgu_w[0].astype(BF16)
    sgu_bt = sgu_b[0].T
    final_g2 = final_g.reshape(1, D)

    zero_s = jnp.zeros((NH, HD), F32)
    hn_c, xa_c = _proj(ctx[0], mods, 1, norm_g, w_in_full, 1, "proj_ctx")
    xaz_c, xcz_c, af_c, ab_c, hf_c, hb_c, gf_c, gb_c, hf0, hb0 = _lru_fwd(xa_c, conv_wz, conv_bz, wcat, bcat, lamcat,
                                                                           zero_s, zero_s, "lru_fwd_ctx")

    hn, xa, ga, u, v, gb, ys = _proj(x[0], mods, 0, norm_g, w_in_full, 5, "proj",
                                     (sgu_ln_g, sgu_ln_b, sgu_wb, sgu_bt))
    xaz, xcz, af, ab, hf, hb, gf, gb_l, _, _ = _lru_fwd(xa, conv_wz, conv_bz, wcat, bcat, lamcat, hf0, hb0, "lru_fwd")

    w_out_full = _late_gather_wait(wo_land, wo_ss, wo_rs, "w_out", hf, "late_gather_wait_w_out")
    (loss_part, dfg, dgx, dxn, y, do, dga, dgb, dyl_z, dys) = _out_fwd_bwd(
        hf, hb, ga, gb, ys, x[0], loss_target[0], mods, final_g2, w_out_full)

    zw = jnp.zeros((NH, HD, 4 * HD), F32)
    zb = jnp.zeros((NH, 4 * HD), F32)
    zl = jnp.zeros((NH, 2 * HD), F32)
    dxc_b, dxc_f, dwc, dbc, dlc, dh0b, dh0f = _lru_bwd(xcz, dyl_z, dyl_z, hf, hb, af, ab, gf, gb_l, hf0, hb0,
                                                        wcat, lamcat, zw, zb, zl, "lru_bwd")
    dxa, dcw, dcb = _conv_bwd(dxc_b, dxc_f, xaz, conv_wz, jnp.zeros((4, NH, HD), F32), zero_s, "conv_bwd")

    zc = jnp.zeros((lc * NH, HD), F32)
    dhf_c = lax.dynamic_update_slice(zc, dh0f, ((lc - 1) * NH, 0))
    dhb_c = lax.dynamic_update_slice(zc, dh0b, (0, 0))
    dxc_bc, dxc_fc, dwc, dbc, dlc, _, _ = _lru_bwd(xcz_c, dhb_c, dhf_c, hf_c, hb_c, af_c, ab_c, gf_c, gb_c,
                                                    zero_s, zero_s, wcat, lamcat, dwc, dbc, dlc, "lru_bwd_ctx")
    dxa_c, dcw, dcb = _conv_bwd(dxc_bc, dxc_fc, xaz_c, conv_wz, dcw, dcb, "conv_bwd_ctx")

    grad_x, dng, dsc_x, dsh_x, du, dv, d_sgu_w, d_sgu_b, d_ln_g, d_ln_b = _proj_bwd(
        [dxa, dga, dgb], x[0], dxn, mods, 0, norm_g, w_in_full, jnp.zeros((1, D), F32), "proj_bwd",
        (u, v, dys, sgu_ln_g, sgu_ln_b, sgu_wb, sgu_bt))
    dzs = [dxa, dga, du, dv, dgb]
    dng, dsc_c, dsh_c = _proj_bwd([dxa_c], ctx[0], None, mods, 1, norm_g, w_in_full, dng, "proj_bwd_ctx")

    dmx = jnp.concatenate([dsh_x, dsc_x, dgx], axis=0)
    dmc = jnp.concatenate([dsh_c, dsc_c, jnp.zeros((1, D), F32)], axis=0)
    lp = loss_part[0, 0]
    lp1 = lax.reduce_precision(lp, 8, 7)
    lp2 = lax.reduce_precision(lp - lp1, 8, 7)
    lp3 = lax.reduce_precision(lp - lp1 - lp2, 8, 7)
    loss_row = jnp.pad(jnp.stack([lp1, lp2, lp3]).reshape(1, 3), ((0, 0), (0, D - 3)))
    slot = jnp.concatenate([dmx, loss_row], axis=0)
    slots = lax.dynamic_update_slice(jnp.zeros((32, D), F32), slot, (4 * dev, 0))
    vecs = jnp.concatenate([dfg, dng, dcb.reshape(1, D), d_ln_g, d_ln_b, dcw.reshape(4, D), dmc,
                            jnp.zeros((4, D), F32), slots], axis=0)
    d_sgu_w4 = d_sgu_w.reshape(4, 256, HD).transpose(1, 0, 2).reshape(256, 4 * HD)
    pad8 = lambda a: jnp.pad(a, ((0, 8 - a.shape[0]), (0, 4 * HD - a.shape[1])))
    pack = jnp.concatenate([dwc.reshape(NH * HD, 4 * HD), pad8(dbc), pad8(dlc), d_sgu_w4, pad8(d_sgu_b),
                            vecs.reshape(96, 4 * HD), jnp.zeros((8, 4 * HD), F32)], axis=0)
    g_w_in, g_w_out, tot = _grads_reduce(hn, dzs, hn_c, dxa_c, y, do, pack)

    g_wc = tot[0:1024].reshape(NH, HD, 4 * HD)
    g_bc = tot[1024:1032]
    g_lc = tot[1032:1040, 0:2 * HD]
    g_sgu_w = tot[1040:1296].reshape(256, 4, HD).transpose(1, 0, 2).reshape(NH, HD, HD)
    g_sgu_b = tot[1296:1304, 0:HD]
    tv = tot[1304:1400].reshape(48, D)
    g_final_g, g_norm_g, g_conv_b, g_ln_g, g_ln_b = tv[0:1], tv[1:2], tv[2:3], tv[3:4], tv[4:5]
    g_conv_w_full = tv[5:9]
    dmc_tot = tv[9:12].reshape(1, 3 * D)
    slots_all = tv[16:48].reshape(8, 4, D)
    dmx_all = slots_all[:, 0:3, :].reshape(8, 3 * D)
    c_all = c_slots.reshape(8, 8, D)[:, 0, :]
    g_lru_wa = jnp.stack([g_wc[:, :, 0:HD], g_wc[:, :, 2 * HD:3 * HD]])
    g_lru_wx = jnp.stack([g_wc[:, :, HD:2 * HD], g_wc[:, :, 3 * HD:4 * HD]])
    g_lru_ba = jnp.stack([g_bc[:, 0:HD], g_bc[:, 2 * HD:3 * HD]])
    g_lru_bx = jnp.stack([g_bc[:, HD:2 * HD], g_bc[:, 3 * HD:4 * HD]])
    g_lam_full = jnp.stack([g_lc[:, 0:HD], g_lc[:, HD:2 * HD]]).reshape(2, D)
    g_conv_w = lax.dynamic_slice(g_conv_w_full, (0, 256 * chip), (4, 256))
    g_lam = lax.dynamic_slice(g_lam_full, (0, 256 * chip), (2, 256))
    dmx_all_j = lax.dynamic_slice(dmx_all, (0, 768 * chip), (8, 768))
    dmc_j = lax.dynamic_slice(dmc_tot, (0, 768 * chip), (1, 768))
    ada_full = _late_gather_wait(ada_land, ada_ss, ada_rs, "ada_w", tot, "late_gather_wait_ada_w")
    g_ada_w, g_ada_b, g_c_ctx = _ada_bwd(c_all, dmx_all_j, dmc_j, dmx_all, dmc_tot, c_ctx2, ada_full)

    big = {
        "ada_w": _adam_big(ada_w[0], g_ada_w, m_ada_w[0], v_ada_w[0], "adam_ada_w"),
        "w_in": _adam_big(w_in[0], g_w_in, m_w_in[0], v_w_in[0], "adam_w_in"),
        "w_out": _adam_big(w_out[0], g_w_out, m_w_out[0], v_w_out[0], "adam_w_out"),
    }
    small_in = {
        "c_ctx": (c_ctx, g_c_ctx, m_c_ctx, v_c_ctx, (1, D)),
        "ada_b": (ada_b, g_ada_b, m_ada_b, v_ada_b, (1, 3 * D)),
        "norm_g": (norm_g, g_norm_g, m_norm_g, v_norm_g, (1, D)),
        "conv_w": (conv_w, g_conv_w, m_conv_w, v_conv_w, (4, 256)),
        "conv_b": (conv_b, g_conv_b, m_conv_b, v_conv_b, (1, D)),
        "lru_wa": (lru_wa, g_lru_wa, m_lru_wa, v_lru_wa, (2 * NH * HD, HD)),
        "lru_ba": (lru_ba, g_lru_ba, m_lru_ba, v_lru_ba, (2 * NH, HD)),
        "lru_wx": (lru_wx, g_lru_wx, m_lru_wx, v_lru_wx, (2 * NH * HD, HD)),
        "lru_bx": (lru_bx, g_lru_bx, m_lru_bx, v_lru_bx, (2 * NH, HD)),
        "lru_lambda": (lru_lambda, g_lam, m_lru_lambda, v_lru_lambda, (2, 256)),
        "sgu_ln_g": (sgu_ln_g, g_ln_g, m_sgu_ln_g, v_sgu_ln_g, (1, D)),
        "sgu_ln_b": (sgu_ln_b, g_ln_b, m_sgu_ln_b, v_sgu_ln_b, (1, D)),
        "sgu_w": (sgu_w, g_sgu_w, m_sgu_w, v_sgu_w, (NH * HD, HD)),
        "sgu_b": (sgu_b, g_sgu_b, m_sgu_b, v_sgu_b, (NH, HD)),
        "final_g": (final_g, g_final_g, m_final_g, v_final_g, (1, D)),
    }
    names_small = list(small_in)
    res_small = _adam_small([tuple(a.reshape(small_in[k][4]) for a in small_in[k][:4]) for k in names_small])
    full_shapes = {"ada_w": ada_w.shape, "w_in": w_in.shape, "w_out": w_out.shape}
    grads, deltas, new_m, new_v = {}, {}, {}, {}
    for k in ("ada_w", "w_in", "w_out"):
        g = {"ada_w": g_ada_w, "w_in": g_w_in, "w_out": g_w_out}[k]
        grads[k] = g.reshape(full_shapes[k])
        deltas[k], new_m[k], new_v[k] = (a.reshape(full_shapes[k]) for a in big[k])
    for k, res in zip(names_small, res_small):
        shape = small_in[k][0].shape
        grads[k] = small_in[k][1].reshape(shape)
        deltas[k], new_m[k], new_v[k] = (a.reshape(shape) for a in res)

    loss = jnp.sum(slots_all[:, 3, 0:3])
    order = ["c_ctx", "ada_w", "ada_b", "norm_g", "w_in", "conv_w", "conv_b", "lru_wa", "lru_ba", "lru_wx", "lru_bx",
             "lru_lambda", "sgu_ln_g", "sgu_ln_b", "sgu_w", "sgu_b", "w_out", "final_g"]
    return (loss, grad_x.reshape(x.shape), *[grads[k] for k in order], *[deltas[k] for k in order],
            *[new_m[k] for k in order], *[new_v[k] for k in order])
```

```python
import functools

import jax
import jax.numpy as jnp
from jax import lax
from jax.experimental import pallas as pl
from jax.experimental.pallas import tpu as pltpu

F32 = jnp.float32
BF16 = jnp.bfloat16

D = 1024
NH = 8
HD = 128
NCHIP = 4
T = 256
NORM_EPS = 1e-6
LN_EPS = 1e-5
LRU_C = 8.0
ADAM_LR = 0.001
ADAM_B1 = 0.9
ADAM_B2 = 0.999
ADAM_EPS = 1e-08
ADAM_WD = 0.01
ADAM_STEP = 10

VMEM = pl.BlockSpec(memory_space=pltpu.VMEM)
ANY = pl.BlockSpec(memory_space=pl.ANY)
MESH = pl.DeviceIdType.MESH


def _cp(n_grid=0, vmem_mb=None):
    kw = {}
    if n_grid:
        kw["dimension_semantics"] = ("arbitrary",) * n_grid
    if vmem_mb:
        kw["vmem_limit_bytes"] = vmem_mb << 20
    return pltpu.CompilerParams(**kw)


def _sigmoid(x):
    return 0.5 * jnp.tanh(0.5 * x) + 0.5


def _silu_and_grad(x):
    s = _sigmoid(x)
    return x * s, s * (1.0 + x * (1.0 - s))


_GELU_K = 0.7978845608028654
_GELU_C = 0.044715


def _gelu_and_grad(x):
    x2 = x * x
    th = jnp.tanh(x * (_GELU_K + (_GELU_K * _GELU_C) * x2))
    p = 0.5 + 0.5 * th
    g = x * p
    dg = p + g * (1.0 - th) * (_GELU_K + (3.0 * _GELU_K * _GELU_C) * x2)
    return g, dg


def _softplus(x):
    return jnp.maximum(x, 0.0) + jnp.log1p(jnp.exp(-jnp.abs(x)))


def _lru_gate(pre, lam_row, d, off=None):
    off = 256 * d if off is None else off
    r = _sigmoid(pre[:, off:off + HD])
    gi = _sigmoid(pre[:, off + HD:off + 2 * HD])
    lam = lam_row[:, HD * d:HD * d + HD]
    sp = _softplus(-lam)
    la = (-LRU_C) * r * sp
    a = jnp.exp(la)
    x2 = 2.0 * la
    m2 = jnp.where(x2 > -1e-3, -x2 * (1.0 + 0.5 * x2), 1.0 - a * a)
    mult = jnp.sqrt(m2)
    return r, gi, lam, sp, a, mult


def _dot(a, b):
    return jnp.dot(a, b, preferred_element_type=F32)


def _dot_tn(a, b):
    return lax.dot_general(a, b, (((0,), (0,)), ((), ())), preferred_element_type=F32)


def _dot_nt(a, b):
    return lax.dot_general(a, b, (((1,), (1,)), ((), ())), preferred_element_type=F32)


def _mo(v, m):
    return v if isinstance(v, int) else pl.multiple_of(v, m)


def _zrows(h, n):
    return pl.ds(h, n, stride=NH)


def _gather_in(c, c_ctx, ada_w, ada_b_j, w_in, w_out, smalls):
    nch = [1, 4]
    wrows = lambda cc, q: (pl.ds(_mo(512 * cc, 16), 512) if q is None
                           else pl.ds(_mo(512 * cc + (512 // nch[1]) * q, 16), 512 // nch[1]))
    specs = [
        ((64, 256), F32, lambda r, jj, cc, q=None: r.at[pl.ds(_mo(16 * jj + 8 * cc, 8), 8), :]),
        ((D, 5120), BF16, lambda r, jj, cc, q=None: r.at[wrows(cc, q), pl.ds(_mo(1280 * jj, 128), 1280)]),
    ]
    halves = [lambda r, cc, q=None: r.at[pl.ds(_mo(8 * cc, 8), 8), :],
              lambda r, cc, q=None: r.at[wrows(cc, q), :]]
    na = len(specs)
    sem_base = [0, 6 * nch[0]]
    sidx = lambda a, q, k: sem_base[a] + 6 * q + k
    n_tiny = 6 * sum(nch)
    n_sem = n_tiny + 10

    def body(c_ref, cc_ref, ada_ref, adab_ref, win_ref, wout_ref, sm_ref,
             mods_o, call_o, sm_o, win_o, wol_o, adal_o, s_win, s_ada, s_wout, f_win, f_ada, f_wout, cslot, lhs, mbuf,
             send_sems, recv_sems, local_sems, load_sems):
        x, y, c = lax.axis_index("x"), lax.axis_index("y"), lax.axis_index("c")
        j = 2 * x + y
        dev = 2 * j + c
        sib = (x, y, 1 - c)
        chips = [(1 - x, y), (x, 1 - y), (1 - x, 1 - y)]
        cj = [2 * cx + cy for cx, cy in chips]
        outs = [sm_o, win_o]
        srcs = [sm_ref, s_win]

        def copy(idx, src, dst, to):
            return pltpu.make_async_remote_copy(src_ref=src, dst_ref=dst, send_sem=send_sems.at[idx],
                                                recv_sem=recv_sems.at[idx], device_id=to, device_id_type=MESH)

        sends = []

        def start(cp):
            cp.start()
            sends.append(cp)

        cslot[...] = jnp.zeros_like(cslot)
        cslot[0:1, :] = c_ref[...]
        my_slot = pl.ds(_mo(8 * dev, 8), 8)
        others = [sib] + [(*chips[k], c) for k in range(3)] + [(*chips[k], 1 - c) for k in range(3)]
        other_dev = [dev + 1 - 2 * c] + [2 * cj[k] + c for k in range(3)] + [2 * cj[k] + 1 - c for k in range(3)]
        base = n_tiny
        for r in range(7):
            start(copy(base + r, cslot, call_o.at[my_slot, :], others[r]))
        call_o[my_slot, :] = cslot[...]

        crow = 512 // nch[1]
        loads = []
        for cc in (c, 1 - c):
            for q in range(nch[1]):
                rows = pl.ds(_mo(512 * cc + crow * q, 16), crow)
                loads.append(pltpu.make_async_copy(win_ref.at[rows, :], f_win.at[rows, :], load_sems.at[len(loads)]))
        loads.append(pltpu.make_async_copy(ada_ref, f_ada, load_sems.at[len(loads)]))
        loads.append(pltpu.make_async_copy(wout_ref, f_wout, load_sems.at[len(loads)]))
        for ld in loads:
            ld.start()
        for k in range(2):
            start(copy(sidx(0, 0, k), halves[0](srcs[0], c), specs[0][2](outs[0], j, c), (*chips[k], c)))
        for q in range(nch[1]):
            loads[q].wait()
            rows = pl.ds(_mo(512 * c + crow * q, 16), crow)
            s_win[rows, :] = f_win[rows, :].astype(BF16)
            for k in range(2):
                start(copy(sidx(1, q, k), halves[1](s_win, c, q), specs[1][2](win_o, j, c, q), (*chips[k], c)))
        for q in range(nch[1]):
            loads[nch[1] + q].wait()
            rows = pl.ds(_mo(512 * (1 - c) + crow * q, 16), crow)
            s_win[rows, :] = f_win[rows, :].astype(BF16)
        local = []
        for a in range(na):
            for cc in range(2):
                lc = pltpu.make_async_copy(halves[a](srcs[a], cc), specs[a][2](outs[a], j, cc), local_sems.at[2 * a + cc])
                lc.start()
                local.append(lc)
        loads[2 * nch[1]].wait()
        s_ada[...] = f_ada[...].astype(BF16)
        loads[2 * nch[1] + 1].wait()
        s_wout[...] = f_wout[...].astype(BF16)
        for q, (src, dst) in enumerate([(s_wout, wol_o.at[pl.ds(_mo(512 * j, 16), 512), :]),
                                        (s_ada, adal_o.at[:, pl.ds(_mo(768 * j, 128), 768)])]):
            lc = pltpu.make_async_copy(src, dst, local_sems.at[2 * na + q])
            lc.start()
            local.append(lc)

        for r in range(7):
            slot = call_o.at[pl.ds(_mo(8 * other_dev[r], 8), 8), :]
            copy(base + r, slot, slot, sib).wait_recv()
        lhs[...] = jnp.zeros_like(lhs)
        for b in range(8):
            cv = call_o[8 * b:8 * b + 1, :]
            lhs[b:b + 1, :] = cv * _sigmoid(cv)
        cv = cc_ref[...]
        lhs[8:9, :] = cv * _sigmoid(cv)
        mbuf[j] = _dot(lhs[...].astype(BF16), s_ada[...]) + adab_ref[...]
        for k in range(3):
            start(copy(base + 7 + k, mbuf.at[j], mbuf.at[j], (*chips[k], c)))
        for k in range(3):
            copy(base + 7 + k, mbuf.at[cj[k]], mbuf.at[cj[k]], sib).wait_recv()
        mods_o[...] = jnp.zeros_like(mods_o)
        for jj in range(NCHIP):
            mods_o[0:1, 768 * jj:768 * jj + 768] = mbuf[jj, pl.ds(dev, 1), :]
            mods_o[1:2, 768 * jj:768 * jj + 768] = mbuf[jj, 8:9, :]

        kx = [1 - x, x, 1 - x]
        ky = [y, 1 - y, 1 - y]
        pick = lambda k, lst: jnp.where(k == 0, lst[0], jnp.where(k == 1, lst[1], lst[2]))
        for a in range(na):
            for q in range(nch[a]):
                for step, k in enumerate([c, 1 - c]):
                    reg = specs[a][2](outs[a], pick(k, cj), c, q)
                    copy(sidx(a, q, k), reg, reg, sib).wait_recv()
                    if step == 0:
                        start(copy(sidx(a, q, 2), reg, reg, (pick(1 - c, kx), pick(1 - c, ky), c)))
                    start(copy(sidx(a, q, 3 + k), reg, reg, sib))
        for a in range(na):
            for q in range(nch[a]):
                reg = specs[a][2](outs[a], cj[2], c, q)
                copy(sidx(a, q, 2), reg, reg, sib).wait_recv()
                start(copy(sidx(a, q, 5), reg, reg, sib))
        for a in range(na):
            for q in range(nch[a]):
                for k in range(3):
                    reg = specs[a][2](outs[a], cj[k], 1 - c, q)
                    copy(sidx(a, q, 3 + k), reg, reg, sib).wait_recv()
        for cp in sends:
            cp.wait_send()
        for lc in local:
            lc.wait()

    out_shape = (jax.ShapeDtypeStruct((8, 3 * D), F32), jax.ShapeDtypeStruct((64, D), F32),
                 jax.ShapeDtypeStruct(specs[0][0], F32), jax.ShapeDtypeStruct(specs[1][0], BF16),
                 jax.ShapeDtypeStruct((2048, D), BF16), jax.ShapeDtypeStruct((D, 3 * D), BF16))
    return pl.pallas_call(
        body, name="gather_in", out_shape=out_shape,
        in_specs=[VMEM, VMEM, ANY, VMEM, ANY, ANY, VMEM], out_specs=(VMEM, VMEM, VMEM, ANY, ANY, ANY),
        scratch_shapes=[pltpu.VMEM((D, 1280), BF16), pltpu.VMEM((D, 768), BF16), pltpu.VMEM((512, D), BF16),
                        pltpu.VMEM((D, 1280), F32), pltpu.VMEM((D, 768), F32), pltpu.VMEM((512, D), F32),
                        pltpu.VMEM((8, D), F32), pltpu.VMEM((16, D), F32), pltpu.VMEM((NCHIP, 16, 768), F32),
                        pltpu.SemaphoreType.DMA((n_sem,)), pltpu.SemaphoreType.DMA((n_sem,)),
                        pltpu.SemaphoreType.DMA((2 * na + 2,)), pltpu.SemaphoreType.DMA((2 * nch[1] + 2,))],
        compiler_params=_cp(vmem_mb=56),
    )(c, c_ctx, ada_w, ada_b_j, w_in, w_out, smalls)


HBM = pl.BlockSpec(memory_space=pltpu.HBM)
SEM = pl.BlockSpec(memory_space=pltpu.SEMAPHORE)


def _late_gather_regions(x, y, c):
    chips = [(1 - x, y), (x, 1 - y), (1 - x, 1 - y)]
    wo_reg = lambda r, jj, cc: r.at[pl.ds(_mo(512 * jj + 256 * cc, 16), 256), :]
    ada_reg = lambda r, jj, cc: r.at[pl.ds(_mo(512 * cc, 16), 512), pl.ds(_mo(768 * jj, 128), 768)]
    return chips, wo_reg, ada_reg


def _late_gather_start(wo_land, ada_land):
    def body(wol_ref, adal_ref, wo_ss, wo_rs, ada_ss, ada_rs, wol_thru, adal_thru, token):
        x, y, c = lax.axis_index("x"), lax.axis_index("y"), lax.axis_index("c")
        j = 2 * x + y
        chips, wo_reg, ada_reg = _late_gather_regions(x, y, c)
        for k in range(3):
            for cc in range(2):
                pltpu.make_async_remote_copy(src_ref=wo_reg(wol_ref, j, c), dst_ref=wo_reg(wol_ref, j, c),
                                             send_sem=wo_ss.at[2 * k + cc], recv_sem=wo_rs.at[2 * k + c],
                                             device_id=(*chips[k], cc), device_id_type=MESH).start()
        for k in range(3):
            for cc in range(2):
                pltpu.make_async_remote_copy(src_ref=ada_reg(adal_ref, j, c), dst_ref=ada_reg(adal_ref, j, c),
                                             send_sem=ada_ss.at[2 * k + cc], recv_sem=ada_rs.at[2 * k + c],
                                             device_id=(*chips[k], cc), device_id_type=MESH).start()
        token[...] = jnp.zeros_like(token)

    sems = pltpu.SemaphoreType.DMA((6,))
    return pl.pallas_call(
        body, name="late_gather_start",
        out_shape=(sems, sems, sems, sems, pltpu.HBM(wo_land.shape, BF16), pltpu.HBM(ada_land.shape, BF16),
                   jax.ShapeDtypeStruct((8, 128), F32)),
        in_specs=(HBM, HBM), out_specs=(SEM, SEM, SEM, SEM, HBM, HBM, VMEM), input_output_aliases={0: 4, 1: 5},
        compiler_params=pltpu.CompilerParams(has_side_effects=pltpu.SideEffectType.DATAFLOW_SIDE_EFFECTING),
    )(pltpu.with_memory_space_constraint(wo_land, pltpu.HBM), pltpu.with_memory_space_constraint(ada_land, pltpu.HBM))


def _late_gather_wait(land, send_sems, recv_sems, which, after, name):
    def body(land_ref, ss, rs, after_ref, land_out):
        x, y, c = lax.axis_index("x"), lax.axis_index("y"), lax.axis_index("c")
        j = 2 * x + y
        chips, wo_reg, ada_reg = _late_gather_regions(x, y, c)
        reg = wo_reg if which == "w_out" else ada_reg
        for k in range(3):
            kj = 2 * chips[k][0] + chips[k][1]
            for cc in range(2):
                cp = pltpu.make_async_remote_copy(src_ref=reg(land_ref, j, c), dst_ref=reg(land_ref, kj, cc),
                                                  send_sem=ss.at[2 * k + cc], recv_sem=rs.at[2 * k + cc],
                                                  device_id=(*chips[k], cc), device_id_type=MESH)
                cp.wait_send()
                cp.wait_recv()

    return pl.pallas_call(
        body, name=name, out_shape=pltpu.HBM(land.shape, land.dtype),
        in_specs=(HBM, SEM, SEM, ANY), out_specs=HBM, input_output_aliases={0: 0},
        compiler_params=pltpu.CompilerParams(has_side_effects=pltpu.SideEffectType.DATAFLOW_SIDE_EFFECTING),
    )(land, send_sems, recv_sems, after)


RCHUNK = 16


def _grads_reduce(hn, dzs, hn_c, dxa_c, y, do, pack):
    rp = pack.shape[0]
    hp = rp // 2
    assert hp % RCHUNK == 0
    wi_w = 1280
    lx, lc = hn.shape[0], hn_c.shape[0]
    lt = lx + lc
    n_dz = len(dzs)

    def body(*refs):
        hn_hbm, dz_hbm = refs[0], refs[1:1 + n_dz]
        hnc_hbm, dxac_hbm, y_hbm, do_hbm, pk_hbm, wi_out, wo_out, pk_out = refs[1 + n_dz:9 + n_dz]
        (hn_mine, hn_other, dzbuf, wi_other, wi_mine, wi_recv, wi_send, wi_rb,
         y_blk, do_mine, do_other, wo_other, wo_mine, wo_recv, wo_send, wo_rb,
         pk_mine, pk_recv, pk_send, pk_rb, pk_own, send_sems, recv_sems, local_sems) = refs[9 + n_dz:]
        x, y, c = lax.axis_index("x"), lax.axis_index("y"), lax.axis_index("c")
        j = 2 * x + y
        sib = (x, y, 1 - c)
        chips = [(1 - x, y), (x, 1 - y), (1 - x, 1 - y)]
        cj = [2 * cx + cy for cx, cy in chips]
        near = (jnp.where(c == 0, 1 - x, x), jnp.where(c == 0, y, 1 - y), c)
        slabs = [cj[2], cj[0], cj[1], j]

        def copy(k, src, dst, to):
            return pltpu.make_async_remote_copy(src_ref=src, dst_ref=dst, send_sem=send_sems.at[k],
                                                recv_sem=recv_sems.at[k], device_id=to, device_id_type=MESH)

        def local(k, src, dst):
            cp = pltpu.make_async_copy(src, dst, local_sems.at[k])
            cp.start()
            return cp

        rows_half = lambda r, cc, n: r.at[pl.ds(_mo(cc * n, 16), n), :]
        cols_half = lambda r, cc, n: r.at[:, pl.ds(_mo(cc * n, 128), n)]
        pk_piece = lambda r, cc, jj: r.at[pl.ds(_mo(cc * hp, 16), hp), pl.ds(_mo(jj * 128, 128), 128)]

        sends = []

        def start(cp):
            cp.start()
            sends.append(cp)

        def dz_pieces(s):
            g0 = wi_w * s
            k0, off0 = g0 // D, g0 % D
            w0 = min(D - off0, wi_w)
            pieces = [(k0, off0, w0, 0)]
            if w0 < wi_w:
                pieces.append((k0 + 1, 0, wi_w - w0, w0))
            return pieces

        def dz_copies(s):
            cps = []
            for q, (k, off, w, dst) in enumerate(dz_pieces(s)):
                cps.append(pltpu.make_async_copy(dz_hbm[k].at[:, pl.ds(off, w)], dzbuf.at[pl.ds(0, lx), pl.ds(dst, w)],
                                                 local_sems.at[11 + q]))
            if s == 0:
                cps.append(pltpu.make_async_copy(dxac_hbm, dzbuf.at[pl.ds(lx, lc), pl.ds(0, D)], local_sems.at[13]))
            return cps

        def dz_load(sl):
            for s in range(NCHIP):
                @pl.when(sl == s)
                def _():
                    if s == 0:
                        dzbuf[pl.ds(lx, lc), pl.ds(D, wi_w - D)] = jnp.zeros((lc, wi_w - D), BF16)
                    else:
                        dzbuf[pl.ds(lx, lc), :] = jnp.zeros((lc, wi_w), BF16)
                    for cp in dz_copies(s):
                        cp.start()

        def dz_wait(sl):
            for s in range(NCHIP):
                @pl.when(sl == s)
                def _():
                    for cp in dz_copies(s):
                        cp.wait()

        l_pk = local(0, rows_half(pk_hbm, c, hp), pk_mine)
        start(copy(0, rows_half(pk_hbm, 1 - c, hp), pk_recv, sib))
        col = lambda r, cc: r.at[:, pl.ds(_mo(cc * 512, 128), 512)]
        do_loads = [local(7, col(do_hbm, c), do_mine), local(14, col(do_hbm, 1 - c), do_other)]
        hn_loads = [local(2, col(hn_hbm, c), hn_mine.at[pl.ds(0, lx), :]),
                    local(3, col(hnc_hbm, c), hn_mine.at[pl.ds(lx, lc), :]),
                    local(4, col(hn_hbm, 1 - c), hn_other.at[pl.ds(0, lx), :]),
                    local(5, col(hnc_hbm, 1 - c), hn_other.at[pl.ds(lx, lc), :])]
        y_copy = lambda s: pltpu.make_async_copy(col(y_hbm, slabs[s]), y_blk, local_sems.at[1])
        y_copy(0).start()
        dz_load(slabs[0])

        def pair_sum(mine, recv, send, nrows, keep, relayed=None):
            def step(i, carry):
                rows = pl.ds(_mo(i * RCHUNK, RCHUNK), RCHUNK)
                s = mine[rows, :] + recv[rows, :].astype(F32)
                if relayed is not None:
                    s = s + relayed[rows, :].astype(F32)
                if keep:
                    mine[rows, :] = s
                if send is not None:
                    send[rows, :] = s.astype(BF16)
                return carry
            lax.fori_loop(0, nrows // RCHUNK, step, 0)

        def chip_sum(own, rb, nrows, terms=(0, 1, 2)):
            def step(i, carry):
                rows = pl.ds(_mo(i * RCHUNK, RCHUNK), RCHUNK)
                acc = own[rows, :]
                for q in terms:
                    acc = acc + rb[q, rows, :].astype(F32)
                own[rows, :] = acc
                return carry
            lax.fori_loop(0, nrows // RCHUNK, step, 0)

        w_in_g = dict(other=wi_other, mine=wi_mine, recv=wi_recv, send=wi_send, rb=wi_rb, p1_sems=(2, 3, 4, 5), p2_sem=12,
                      p1=[None] * NCHIP, wait_load=lambda s: dz_wait(slabs[s]), load=lambda s: dz_load(slabs[s]),
                      dot_other=lambda: _dot_tn(hn_other[...], dzbuf[...]), dot_mine=lambda: _dot_tn(hn_mine[...], dzbuf[...]))
        w_out_g = dict(other=wo_other, mine=wo_mine, recv=wo_recv, send=wo_send, rb=wo_rb, p1_sems=(1, 24, 25, 26), p2_sem=9,
                       p1=[None] * NCHIP, wait_load=lambda s: y_copy(s).wait(), load=lambda s: y_copy(s).start(),
                       dot_other=lambda: _dot_tn(y_blk[...], do_other[...]), dot_mine=lambda: _dot_tn(y_blk[...], do_mine[...]))

        def piece_matmuls(g, s):
            if s >= 2:
                g["p1"][s - 2].wait_send()
            g["wait_load"](s)
            g["other"][s % 2] = g["dot_other"]().astype(BF16)
            g["p1"][s] = copy(g["p1_sems"][s], g["other"].at[s % 2], g["recv"].at[s], sib)
            g["p1"][s].start()
            g["mine"][s % 2] = g["dot_mine"]()
            if s + 1 < NCHIP:
                g["load"](s + 1)

        def piece_finish(g, s):
            mine, recv, send, rb, p2 = g["mine"].at[s % 2], g["recv"].at[s], g["send"], g["rb"], g["p2_sem"]
            nrows = mine.shape[0]
            copy(g["p1_sems"][s], recv, recv, sib).wait_recv()
            if s == 3:
                pair_sum(mine, recv, None, nrows, True)
                return
            if s == 0:
                pair_sum(mine, recv, send.at[0], nrows, False)
                start(copy(p2, send.at[0], rb.at[0], near))
                return
            adds_relayed = c == (1 if s == 1 else 0)

            @pl.when(adds_relayed)
            def _():
                copy(p2, rb.at[0], rb.at[0], sib).wait_recv()
                pair_sum(mine, recv, send.at[s], nrows, False, rb.at[0])

            @pl.when(jnp.logical_not(adds_relayed))
            def _():
                pair_sum(mine, recv, send.at[s], nrows, False)
            start(copy(p2 + s, send.at[s], rb.at[s], (*chips[s - 1], c)))

        def piece_total(g):
            for k in (1, 2):
                copy(g["p2_sem"] + k, g["rb"].at[k], g["rb"].at[k], sib).wait_recv()
            chip_sum(g["mine"].at[1], g["rb"], g["mine"].shape[1], (1, 2))

        for cp in do_loads:
            cp.wait()
        piece_matmuls(w_out_g, 0)
        piece_matmuls(w_out_g, 1)
        piece_finish(w_out_g, 0)
        piece_matmuls(w_out_g, 2)
        piece_finish(w_out_g, 1)
        piece_matmuls(w_out_g, 3)
        piece_finish(w_out_g, 2)

        for cp in hn_loads:
            cp.wait()
        piece_matmuls(w_in_g, 0)

        l_pk.wait()
        copy(0, pk_recv, pk_recv, sib).wait_recv()
        pair_sum(pk_mine, pk_recv, pk_send, hp, True)
        for k in range(3):
            start(copy(6 + k, pk_send.at[:, pl.ds(_mo(cj[k] * 128, 128), 128)], pk_rb.at[k], (*chips[k], c)))
        l_pk_own = local(6, pk_mine.at[:, pl.ds(_mo(j * 128, 128), 128)], pk_own)

        piece_matmuls(w_in_g, 1)
        piece_finish(w_in_g, 0)

        l_pk_own.wait()
        for k in range(3):
            copy(6 + k, pk_rb.at[k], pk_rb.at[k], sib).wait_recv()
        chip_sum(pk_own, pk_rb, hp)
        l_pk_out = local(8, pk_own, pk_piece(pk_out, c, j))
        start(copy(15, pk_own, pk_piece(pk_out, c, j), sib))
        for k in range(2):
            start(copy(16 + k, pk_own, pk_piece(pk_out, c, j), (*chips[k], c)))

        piece_matmuls(w_in_g, 2)
        piece_finish(w_in_g, 1)
        piece_matmuls(w_in_g, 3)
        piece_finish(w_in_g, 2)

        piece_finish(w_out_g, 3)
        piece_total(w_out_g)
        l_wo_out = local(9, wo_mine.at[1], cols_half(wo_out, c, 512))
        start(copy(22, wo_mine.at[1], cols_half(wo_out, c, 512), sib))

        far = (jnp.where(c == 0, x, 1 - x), jnp.where(c == 0, 1 - y, y), c)
        for step, k in enumerate([c, 1 - c, 2]):
            reg = pk_piece(pk_out, c, jnp.where(k == 0, cj[0], jnp.where(k == 1, cj[1], cj[2])))
            copy(16 + k, reg, reg, sib).wait_recv()
            if step == 0:
                start(copy(18, reg, reg, far))
            start(copy(19 + k, reg, reg, sib))

        piece_finish(w_in_g, 3)
        piece_total(w_in_g)
        l_wi_out = local(10, wi_mine.at[1], rows_half(wi_out, c, 512))
        start(copy(23, wi_mine.at[1], rows_half(wi_out, c, 512), sib))

        reg = pk_piece(pk_out, 1 - c, j)
        copy(15, reg, reg, sib).wait_recv()
        for k in range(3):
            reg = pk_piece(pk_out, 1 - c, cj[k])
            copy(19 + k, reg, reg, sib).wait_recv()
        reg = cols_half(wo_out, 1 - c, 512)
        copy(22, reg, reg, sib).wait_recv()
        reg = rows_half(wi_out, 1 - c, 512)
        copy(23, reg, reg, sib).wait_recv()
        for cp in sends + w_in_g["p1"][2:] + w_out_g["p1"][2:]:
            cp.wait_send()
        for cp in (l_pk_out, l_wo_out, l_wi_out):
            cp.wait()

    return pl.pallas_call(
        body, name="grads_reduce",
        out_shape=(jax.ShapeDtypeStruct((D, wi_w), F32), jax.ShapeDtypeStruct((512, D), F32),
                   jax.ShapeDtypeStruct(pack.shape, F32)),
        in_specs=[ANY] * (6 + n_dz), out_specs=(ANY,) * 3,
        scratch_shapes=[
            pltpu.VMEM((lt, 512), BF16), pltpu.VMEM((lt, 512), BF16), pltpu.VMEM((lt, wi_w), BF16),
            pltpu.VMEM((2, 512, wi_w), BF16), pltpu.VMEM((2, 512, wi_w), F32), pltpu.VMEM((4, 512, wi_w), BF16),
            pltpu.VMEM((3, 512, wi_w), BF16), pltpu.VMEM((3, 512, wi_w), BF16),
            pltpu.VMEM((lx, 512), BF16), pltpu.VMEM((lx, 512), BF16), pltpu.VMEM((lx, 512), BF16),
            pltpu.VMEM((2, 512, 512), BF16), pltpu.VMEM((2, 512, 512), F32), pltpu.VMEM((4, 512, 512), BF16),
            pltpu.VMEM((3, 512, 512), BF16), pltpu.VMEM((3, 512, 512), BF16),
            pltpu.VMEM((hp, 512), F32), pltpu.VMEM((hp, 512), F32), pltpu.VMEM((hp, 512), BF16),
            pltpu.VMEM((3, hp, 128), BF16), pltpu.VMEM((hp, 128), F32),
            pltpu.SemaphoreType.DMA((27,)), pltpu.SemaphoreType.DMA((27,)), pltpu.SemaphoreType.DMA((15,))],
        compiler_params=_cp(vmem_mb=56),
    )(hn, *dzs, hn_c, dxa_c, y, do, pack)


def _ada_bwd(c_all, dmx_all_j, dmc_j, dmx_all, dmc, c_ctx, ada_w_full):
    def body(c_ref, dmxj_ref, dmcj_ref, dmx_ref, dmc_ref, cc_ref, w_ref, gw_ref, gb_ref, gc_ref, lhs, rhs, dm8):
        lhs[...] = jnp.zeros_like(lhs)
        rhs[...] = jnp.zeros_like(rhs)
        cv = c_ref[...]
        lhs[0:8, :] = cv * _sigmoid(cv)
        cc = cc_ref[...]
        a_c, da_c = _silu_and_grad(cc)
        lhs[8:9, :] = a_c
        rhs[0:8, :] = dmxj_ref[...]
        rhs[8:9, :] = dmcj_ref[...]
        gw_ref[...] = _dot_tn(lhs[...].astype(BF16), rhs[...].astype(BF16))
        gb_ref[...] = jnp.sum(dmx_ref[...], axis=0, keepdims=True) + dmc_ref[...]
        dm8[...] = jnp.zeros_like(dm8)
        dm8[0:1, :] = dmc_ref[...]
        da = _dot_nt(dm8[...].astype(BF16), w_ref[...])
        gc_ref[...] = da[0:1, :] * da_c

    return pl.pallas_call(
        body, name="ada_bwd",
        out_shape=(jax.ShapeDtypeStruct((D, 768), F32), jax.ShapeDtypeStruct((1, 3 * D), F32),
                   jax.ShapeDtypeStruct((1, D), F32)),
        in_specs=[VMEM] * 7, out_specs=(VMEM,) * 3,
        scratch_shapes=[pltpu.VMEM((16, D), F32), pltpu.VMEM((16, 768), F32), pltpu.VMEM((8, 3 * D), F32)],
        compiler_params=_cp(vmem_mb=32),
    )(c_all, dmx_all_j, dmc_j, dmx_all, dmc, c_ctx, ada_w_full)


def _proj(x, mods, mrow, norm_g, w_full, nk, name, sgu=None):
    lx = x.shape[0]
    n = lx // T
    order = [2, 3, 0, 1, 4] if sgu is not None else list(range(nk))

    def body(x_ref, sh_ref, sc_ref, ng_ref, *rest):
        w_refs, rest = rest[:nk], rest[nk:]
        if sgu is not None:
            g_ref, b_ref, sw_ref, bt_ref = rest[:4]
            rest = rest[4:]
        hn_ref, z_refs = rest[0], rest[1:1 + nk]
        xv = x_ref[...]
        r = lax.rsqrt(jnp.mean(xv * xv, axis=-1, keepdims=True) + NORM_EPS)
        hn = (xv * r) * ng_ref[...] * (1.0 + sc_ref[mrow:mrow + 1, :]) + sh_ref[mrow:mrow + 1, :]
        hb = hn.astype(BF16)
        hn_ref[...] = hb
        for k in order[:2]:
            z_refs[k][...] = _dot(hb, w_refs[k][...])
        if sgu is not None:
            ys_ref, mixed_s = rest[1 + nk], rest[2 + nk]
            for ch in range(T // HD):
                rows = slice(HD * ch, HD * ch + HD)
                ug = _sgu_parts(z_refs[2][rows, :], z_refs[3][rows, :], g_ref[...], b_ref[...], sw_ref, bt_ref,
                                mixed_s)[0]
                ys_ref[rows, :] = ug * mixed_s[...]
        for k in order[2:]:
            z_refs[k][...] = _dot(hb, w_refs[k][...])

    row = pl.BlockSpec((T, D), lambda i: (i, 0))
    vec = pl.BlockSpec((1, D), lambda i: (0, 0))
    in_specs = [row, pl.BlockSpec((8, D), lambda i: (0, 0)), pl.BlockSpec((8, D), lambda i: (0, 1)), vec]
    in_specs += [pl.BlockSpec((D, D), lambda i, k=k: (0, k)) for k in range(nk)]
    args = [x, mods, mods, norm_g] + [w_full] * nk
    out_shape = (jax.ShapeDtypeStruct((lx, D), BF16),) + tuple(jax.ShapeDtypeStruct((lx, D), F32) for _ in range(nk))
    scratch = []
    if sgu is not None:
        in_specs += [vec, vec, pl.BlockSpec((NH, HD, HD), lambda i: (0, 0, 0)), pl.BlockSpec((HD, NH), lambda i: (0, 0))]
        args += list(sgu)
        out_shape += (jax.ShapeDtypeStruct((lx, D), F32),)
        scratch = [pltpu.VMEM((HD, D), F32)]
    return pl.pallas_call(
        body, name=name, grid=(n,), out_shape=out_shape, in_specs=in_specs, out_specs=(row,) * len(out_shape),
        scratch_shapes=scratch, compiler_params=_cp(1, vmem_mb=56),
    )(*args)


def _halo_specs(lx):
    last = lx // 8 - 1
    return [pl.BlockSpec((T, D), lambda i: (i, 0)),
            pl.BlockSpec((8, D), lambda i: (jnp.maximum(i * (T // 8) - 1, 0), 0)),
            pl.BlockSpec((8, D), lambda i: (jnp.minimum((i + 1) * (T // 8), last), 0))]


def _zhalo_specs(lx):
    last = lx // 8 - 1
    return [pl.BlockSpec((T * NH, HD), lambda i: (i, 0)),
            pl.BlockSpec((8 * NH, HD), lambda i: (jnp.maximum(i * (T // 8) - 1, 0), 0)),
            pl.BlockSpec((8 * NH, HD), lambda i: (jnp.minimum((i + 1) * (T // 8), last), 0))]


ZT = pl.BlockSpec((T * NH, HD), lambda i: (i, 0))
CONV_CHUNK = 32


SCAN_SUB = 4


def _scan_tile(chains, post, carry_ref):
    blk = T // SCAN_SUB

    def step(k, state):
        new = []
        for ci, (a_ref, x_ref, o_ref, q_ref, reverse) in enumerate(chains):
            for q in range(SCAN_SUB):
                s, p = state[ci * SCAN_SUB + q]
                t = (q + 1) * blk - 1 - k if reverse else q * blk + k
                r = pl.ds(_mo(t * NH, NH), NH)
                a = a_ref[r, :]
                if post:
                    o = x_ref[r, :] + s
                    o_ref[r, :] = o
                    q_ref[r, :] = p
                    new.append((a * o, a * p))
                else:
                    o = a * s + x_ref[r, :]
                    p = a * p
                    o_ref[r, :] = o
                    q_ref[r, :] = p
                    new.append((o, p))
        return tuple(new)

    zero = jnp.zeros((NH, HD), F32)
    one = jnp.ones((NH, HD), F32)
    final = lax.fori_loop(0, blk, step, tuple((zero, one) for _ in range(len(chains) * SCAN_SUB)), unroll=2)
    for ci, (a_ref, x_ref, o_ref, q_ref, reverse) in enumerate(chains):
        carry = carry_ref[ci]
        for q in (range(SCAN_SUB - 1, -1, -1) if reverse else range(SCAN_SUB)):
            rows = pl.ds(q * blk * NH, blk * NH)
            fixed = o_ref[rows, :].reshape(blk, NH, HD) + q_ref[rows, :].reshape(blk, NH, HD) * carry[None]
            o_ref[rows, :] = fixed.reshape(blk * NH, HD)
            s_loc, p_loc = final[ci * SCAN_SUB + q]
            carry = s_loc + p_loc * carry
        carry_ref[ci] = carry


def _lru_fwd(xa, conv_wz, conv_bz, wcat, bcat, lamcat, s_f, s_b, name):
    lx = xa.shape[0]
    n = lx // T

    def body(xm_u, xp_u, xn_u, xm_d, xp_d, xn_d, cw, cb, w_ref, b_ref, lam_ref, su0, sd0,
             xaz_o, xcz_o, af_o, ab_o, hf_o, hb_o, gf_o, gb_o, fu, fd, pad, xc_d, x_u, x_d, q_u, q_d, carry):
        i = pl.program_id(0)

        @pl.when(i == 0)
        def _():
            carry[0] = su0[...]
            carry[1] = sd0[...]

        def conv_gates(xm, xp, xn, tile, d, xc_ref, a_ref, x_ref, xaz_ref, g_ref):
            pmask = jnp.where(tile == 0, 0.0, 1.0)
            nmask = jnp.where(tile == n - 1, 0.0, 1.0)
            for h in range(NH):
                cols = slice(HD * h, HD * h + HD)
                pad[_zrows(h, 8), :] = xp[:, cols] * pmask
                pad[pl.ds(8 * NH + h, T, stride=NH), :] = xm[:, cols]
                pad[pl.ds((T + 8) * NH + h, 8, stride=NH), :] = xn[:, cols] * nmask
            if xaz_ref is not None:
                xaz_ref[...] = pad[pl.ds(8 * NH, T * NH), :]

            def conv_chunk(ci, c_):
                base = pl.multiple_of(ci * (CONV_CHUNK * NH), CONV_CHUNK * NH)
                acc = None
                for k in range(4):
                    sl = pad[pl.ds(base + (7 + k) * NH, CONV_CHUNK * NH), :].reshape(CONV_CHUNK, NH, HD)
                    term = sl * cw[k][None]
                    acc = term if acc is None else acc + term
                acc = acc + cb[...][None]
                xc_ref[pl.ds(base, CONV_CHUNK * NH), :] = acc.reshape(CONV_CHUNK * NH, HD)
                return c_
            lax.fori_loop(0, T // CONV_CHUNK, conv_chunk, 0)

            for h in range(NH):
                xch = xc_ref[_zrows(h, T), :]
                pre = _dot(xch.astype(BF16), w_ref[h, :, 256 * d:256 * d + 256]) + b_ref[h:h + 1, 256 * d:256 * d + 256]
                r, gi, _, _, a, mult = _lru_gate(pre, lam_ref[h:h + 1, :], d, 0)
                a_ref[_zrows(h, T), :] = a
                x_ref[_zrows(h, T), :] = mult * gi * xch
                for q, val in enumerate((r, gi, mult)):
                    g_ref[:, q * D + HD * h:q * D + HD * h + HD] = val

        conv_gates(xm_u, xp_u, xn_u, i, 0, xcz_o, af_o, x_u, xaz_o, gf_o)
        conv_gates(xm_d, xp_d, xn_d, n - 1 - i, 1, xc_d, ab_o, x_d, None, gb_o)

        _scan_tile([(af_o, x_u, hf_o, q_u, False), (ab_o, x_d, hb_o, q_d, True)], False, carry)
        fu[...] = carry[0]
        fd[...] = carry[1]

    full = lambda shape: pl.BlockSpec(shape, lambda i: (0,) * len(shape))
    last = lx // 8 - 1
    rev = lambda i: n - 1 - i
    halo_dn = [pl.BlockSpec((T, D), lambda i: (rev(i), 0)),
               pl.BlockSpec((8, D), lambda i: (jnp.maximum(rev(i) * (T // 8) - 1, 0), 0)),
               pl.BlockSpec((8, D), lambda i: (jnp.minimum((rev(i) + 1) * (T // 8), last), 0))]
    st = full((NH, HD))
    in_specs = _halo_specs(lx) + halo_dn + [full((4, NH, HD)), st, full((NH, HD, 4 * HD)), full((NH, 4 * HD)),
                                            full((NH, 2 * HD)), st, st]
    dn = pl.BlockSpec((T * NH, HD), lambda i: (rev(i), 0))
    zs = jax.ShapeDtypeStruct((lx * NH, HD), F32)
    ss = jax.ShapeDtypeStruct((NH, HD), F32)
    zbuf = pltpu.VMEM((T * NH, HD), F32)
    gs = jax.ShapeDtypeStruct((lx, 3 * D), F32)
    g_up = pl.BlockSpec((T, 3 * D), lambda i: (i, 0))
    g_dn = pl.BlockSpec((T, 3 * D), lambda i: (rev(i), 0))
    return pl.pallas_call(
        body, name=name, grid=(n,), out_shape=(zs,) * 6 + (gs, gs, ss, ss), in_specs=in_specs,
        out_specs=(ZT, ZT, ZT, dn, ZT, dn, g_up, g_dn, st, st),
        scratch_shapes=[pltpu.VMEM(((T + 16) * NH, HD), F32), zbuf, zbuf, zbuf, zbuf, zbuf,
                        pltpu.VMEM((2, NH, HD), F32)],
        compiler_params=_cp(1, vmem_mb=48),
    )(xa, xa, xa, xa, xa, xa, conv_wz, conv_bz, wcat, bcat, lamcat, s_f, s_b)


def _sgu_parts(u, v, lng, lnb, w_ref, bt_ref, mixed_s):
    ug, dug = _gelu_and_grad(u)
    vg, dvg = _gelu_and_grad(v)
    mu = jnp.mean(vg, axis=-1, keepdims=True)
    vc = vg - mu
    rstd = lax.rsqrt(jnp.mean(vc * vc, axis=-1, keepdims=True) + LN_EPS)
    vh = vc * rstd
    vn = (vh * lng + lnb).astype(BF16)
    for g in range(NH):
        cols = slice(HD * g, HD * g + HD)
        mixed_s[:, cols] = _dot(w_ref[g], vn[:, cols]) + bt_ref[:, g:g + 1]
    return ug, dug, dvg, rstd, vh, vn


def _sgu_bwd_chunk(u, v, dys_v, lng, lnb, w_ref, bt_ref, mixed_s, dvn_s, dw_ref, db_ref, dg_ref, dbl_ref):
    ug, dug, dvg, rstd, vh, vn = _sgu_parts(u, v, lng, lnb, w_ref, bt_ref, mixed_s)
    du = (dys_v * mixed_s[...] * dug).astype(BF16)
    dmix = dys_v * ug
    ones = jnp.ones((8, HD), BF16)
    for g in range(NH):
        cols = slice(HD * g, HD * g + HD)
        dm = dmix[:, cols]
        hi = dm.astype(BF16)
        lo = (dm - hi.astype(F32)).astype(BF16)
        dw_ref[g] += _dot_nt(hi, vn[:, cols])
        db_ref[g:g + 1, :] += (_dot_nt(ones, hi) + _dot_nt(ones, lo))[0:1, :]
        dvn_s[:, cols] = _dot_tn(w_ref[g], hi)
    dvn = dvn_s[...]
    dg_ref[...] += jnp.sum(dvn * vh, axis=0, keepdims=True)
    dbl_ref[...] += jnp.sum(dvn, axis=0, keepdims=True)
    dvh = dvn * lng
    dvg_in = rstd * (dvh - jnp.mean(dvh, axis=-1, keepdims=True) - vh * jnp.mean(dvh * vh, axis=-1, keepdims=True))
    return du, (dvg_in * dvg).astype(BF16)


def _out_fwd_bwd(hf_z, hb_z, ga, gb, ys, x, tgt, mods, final_g, w_out_full):
    lx = x.shape[0]
    n = lx // T

    def body(hf_ref, hb_ref, ga_ref, gb_ref, ys_ref, x_ref, t_ref, gx_ref, fg_ref, w_ref,
             loss_ref, dfg_ref, dgx_ref, dxn_ref, y_ref, do_ref, dga_ref, dgb_ref, dyl_ref, dys_ref, yl_s):
        i = pl.program_id(0)

        @pl.when(i == 0)
        def _():
            loss_ref[...] = jnp.zeros_like(loss_ref)
            dfg_ref[...] = jnp.zeros_like(dfg_ref)
            dgx_ref[...] = jnp.zeros_like(dgx_ref)

        for h in range(NH):
            yl_s[:, HD * h:HD * h + HD] = hf_ref[_zrows(h, T), :] + hb_ref[_zrows(h, T), :]
        yl = yl_s[...]
        gav = ga_ref[...]
        gbv = gb_ref[...]
        sa, dsa = _silu_and_grad(gav)
        sb, dsb = _silu_and_grad(gbv)
        ysv = ys_ref[...]
        y_ref[:, 0:D] = (yl * sa).astype(BF16)
        y_ref[:, D:2 * D] = (ysv * sb).astype(BF16)
        o = _dot(y_ref[...], w_ref[...])
        gx = gx_ref[0:1, :]
        xnew = x_ref[...] + gx * o
        r2 = lax.rsqrt(jnp.mean(xnew * xnew, axis=-1, keepdims=True) + NORM_EPS)
        xh = xnew * r2
        fg = fg_ref[...]
        err = xh * fg - t_ref[...]
        loss_ref[...] += 0.5 * jnp.sum(jnp.mean(err * err, axis=-1, keepdims=True), axis=0, keepdims=True)

        @pl.when(i == n - 1)
        def _():
            lp = loss_ref[...]
            lp1 = lp.astype(BF16).astype(F32)
            lp2 = (lp - lp1).astype(BF16).astype(F32)
            lp3 = (lp - lp1 - lp2).astype(BF16).astype(F32)
            lane = lax.broadcasted_iota(jnp.int32, lp.shape, 1)
            loss_ref[...] = jnp.where(lane == 0, lp1, jnp.where(lane == 1, lp2, jnp.where(lane == 2, lp3, 0.0)))
        dout = err * (1.0 / D)
        dfg_ref[...] += jnp.sum(dout * xh, axis=0, keepdims=True)
        dxh = dout * fg
        dxn = r2 * (dxh - xh * jnp.mean(dxh * xh, axis=-1, keepdims=True))
        dxn_ref[...] = dxn
        dgx_ref[...] += jnp.sum(dxn * o, axis=0, keepdims=True)
        do = (dxn * gx).astype(BF16)
        do_ref[...] = do
        dy = _dot_nt(do, w_ref[...])
        dy1 = dy[:, 0:D]
        dy2 = dy[:, D:2 * D]
        dga_ref[...] = (dy1 * yl * dsa).astype(BF16)
        dgb_ref[...] = (dy2 * ysv * dsb).astype(BF16)
        dys_ref[...] = dy2 * sb
        yl_s[...] = dy1 * sa
        for h in range(NH):
            dyl_ref[_zrows(h, T), :] = yl_s[:, HD * h:HD * h + HD]

    row = pl.BlockSpec((T, D), lambda i: (i, 0))
    vec = pl.BlockSpec((1, D), lambda i: (0, 0))
    in_specs = [ZT, ZT, row, row, row, row, row, pl.BlockSpec((8, D), lambda i: (0, 2)), vec,
                pl.BlockSpec((2 * D, D), lambda i: (0, 0))]
    out_shape = (jax.ShapeDtypeStruct((1, D), F32), jax.ShapeDtypeStruct((1, D), F32), jax.ShapeDtypeStruct((1, D), F32),
                 jax.ShapeDtypeStruct((lx, D), F32), jax.ShapeDtypeStruct((lx, 2 * D), BF16),
                 jax.ShapeDtypeStruct((lx, D), BF16), jax.ShapeDtypeStruct((lx, D), BF16),
                 jax.ShapeDtypeStruct((lx, D), BF16), jax.ShapeDtypeStruct((lx * NH, HD), F32),
                 jax.ShapeDtypeStruct((lx, D), F32))
    out_specs = (vec, vec, vec, row, pl.BlockSpec((T, 2 * D), lambda i: (i, 0)),
                 row, row, row, ZT, row)
    return pl.pallas_call(
        body, name="out_fwd_bwd", grid=(n,), out_shape=out_shape, in_specs=in_specs, out_specs=out_specs,
        scratch_shapes=[pltpu.VMEM((T, D), F32)],
        compiler_params=_cp(1, vmem_mb=56),
    )(hf_z, hb_z, ga, gb, ys, x, tgt, mods, final_g, w_out_full)


def _lru_bwd(xc_z, dy_up, dy_dn, hf_z, hb_z, af_z, ab_z, gf, gb, s_f, s_b, wcat, lamcat, dw0, db0, dl0, name):
    lx = xc_z.shape[0] // NH
    n = lx // T

    def body(xc_u, dy_u, hb_ref, hbn_ref, ab_ref, gb_ref, xc_d, dy_d, hf_ref, hfp_ref, af_ref, gf_ref,
             sf_ref, sb_ref, w_ref, lam_ref, dw0_ref, db0_ref, dl0_ref,
             dxcb_ref, dxcf_ref, dw_ref, db_ref, dl_ref, fu, fd, lb_s, lf_s, q_u, q_d, pf_s, pb_s, dpre_s, carry):
        i = pl.program_id(0)

        @pl.when(i == 0)
        def _():
            dw_ref[...] = dw0_ref[...]
            db_ref[...] = db0_ref[...]
            dl_ref[...] = dl0_ref[...]
            carry[...] = jnp.zeros_like(carry)

        _scan_tile([(ab_ref, dy_u, lb_s, q_u, False), (af_ref, dy_d, lf_s, q_d, True)], True, carry)
        fu[...] = carry[0]
        fd[...] = carry[1]
        pb_s[pl.ds(0, T * NH), :] = hb_ref[...]
        pb_s[pl.ds(T * NH, NH), :] = jnp.where(i == n - 1, sb_ref[...], hbn_ref[pl.ds(0, NH), :])
        pf_s[pl.ds(0, NH), :] = jnp.where(i == n - 1, sf_ref[...], hfp_ref[pl.ds(7 * NH, NH), :])
        pf_s[pl.ds(NH, T * NH), :] = hf_ref[...]
        sides = ((1, xc_u, lb_s, pb_s, NH, ab_ref, gb_ref, dxcb_ref), (0, xc_d, lf_s, pf_s, 0, af_ref, gf_ref, dxcf_ref))
        for d, xc_ref, adj_s, prev_s, prev_off, a_ref, g_ref, dxc_ref in sides:
            wcols = slice(256 * d, 256 * d + 256)
            for h in range(NH):
                xch = xc_ref[_zrows(h, T), :]
                xcb = xch.astype(BF16)
                r, gi, mult = (g_ref[:, q * D + HD * h:q * D + HD * h + HD] for q in range(3))
                a = a_ref[_zrows(h, T), :]
                lam = lam_ref[h:h + 1, HD * d:HD * d + HD]
                sp = _softplus(-lam)
                du = adj_s[_zrows(h, T), :]
                da = du * prev_s[pl.ds(prev_off + h, T, stride=NH), :]
                dgi = du * mult * xch
                dmult = du * gi * xch
                dla = da * a - dmult * (a * a) / mult
                dr = dla * ((-LRU_C) * sp)
                dsp = jnp.sum(dla * ((-LRU_C) * r), axis=0, keepdims=True)
                dl_ref[h:h + 1, HD * d:HD * d + HD] += dsp * (-_sigmoid(-lam))
                dpre_s[:, 0:HD] = dr * r * (1.0 - r)
                dpre_s[:, HD:2 * HD] = dgi * gi * (1.0 - gi)
                dpre = dpre_s[...]
                dpb = dpre.astype(BF16)
                dw_ref[h, :, wcols] += _dot_tn(xcb, dpb)
                db_ref[h:h + 1, wcols] += jnp.sum(dpre, axis=0, keepdims=True)
                dxc_ref[_zrows(h, T), :] = du * mult * gi + _dot_nt(dpb, w_ref[h, :, wcols])

    full = lambda shape: pl.BlockSpec(shape, lambda i: (0,) * len(shape))
    wsp, bsp, lsp = full((NH, HD, 4 * HD)), full((NH, 4 * HD)), full((NH, 2 * HD))
    st = full((NH, HD))
    rev = lambda i: n - 1 - i
    up = ZT
    dn = pl.BlockSpec((T * NH, HD), lambda i: (rev(i), 0))
    nxt = _zhalo_specs(lx)[2]
    prv = pl.BlockSpec((8 * NH, HD), lambda i: (jnp.maximum(rev(i) * (T // 8) - 1, 0), 0))
    g_up = pl.BlockSpec((T, 3 * D), lambda i: (i, 0))
    g_dn = pl.BlockSpec((T, 3 * D), lambda i: (rev(i), 0))
    zs = jax.ShapeDtypeStruct((lx * NH, HD), F32)
    ss = jax.ShapeDtypeStruct((NH, HD), F32)
    zbuf = pltpu.VMEM((T * NH, HD), F32)
    zbuf1 = pltpu.VMEM(((T + 1) * NH, HD), F32)
    return pl.pallas_call(
        body, name=name, grid=(n,),
        out_shape=(zs, zs, jax.ShapeDtypeStruct((NH, HD, 4 * HD), F32), jax.ShapeDtypeStruct((NH, 4 * HD), F32),
                   jax.ShapeDtypeStruct((NH, 2 * HD), F32), ss, ss),
        in_specs=[up, up, up, nxt, up, g_up, dn, dn, dn, prv, dn, g_dn, st, st, wsp, lsp, wsp, bsp, lsp],
        out_specs=(up, dn, wsp, bsp, lsp, st, st),
        scratch_shapes=[zbuf, zbuf, zbuf, zbuf, zbuf1, zbuf1, pltpu.VMEM((T, 2 * HD), F32), pltpu.VMEM((2, NH, HD), F32)],
        compiler_params=_cp(1, vmem_mb=56),
    )(xc_z, dy_up, hb_z, hb_z, ab_z, gb, xc_z, dy_dn, hf_z, hf_z, af_z, gf, s_f, s_b, wcat, lamcat, dw0, db0, dl0)


def _conv_bwd(dxc_a, dxc_b, xa_z, conv_wz, dcw0, dcb0, name):
    lx = dxc_a.shape[0] // NH
    n = lx // T

    def body(dm_a, dp_a, dn_a, dm_b, dp_b, dn_b, xa_ref, cw, dcw0_ref, dcb0_ref, dxa_ref, dcw_ref, dcb_ref, pad, dxa_s):
        i = pl.program_id(0)

        @pl.when(i == 0)
        def _():
            dcw_ref[...] = dcw0_ref[...]
            dcb_ref[...] = dcb0_ref[...]

        pmask = jnp.where(i == 0, 0.0, 1.0)
        nmask = jnp.where(i == n - 1, 0.0, 1.0)
        pad[pl.ds(0, 8 * NH), :] = (dp_a[...] + dp_b[...]) * pmask
        pad[pl.ds(8 * NH, T * NH), :] = dm_a[...] + dm_b[...]
        pad[pl.ds((T + 8) * NH, 8 * NH), :] = (dn_a[...] + dn_b[...]) * nmask

        def chunk(ci, carry):
            base = pl.multiple_of(ci * (CONV_CHUNK * NH), CONV_CHUNK * NH)
            xav = xa_ref[pl.ds(base, CONV_CHUNK * NH), :].reshape(CONV_CHUNK, NH, HD)
            acc = None
            for k in range(4):
                sl = pad[pl.ds(base + (9 - k) * NH, CONV_CHUNK * NH), :].reshape(CONV_CHUNK, NH, HD)
                term = sl * cw[k][None]
                acc = term if acc is None else acc + term
                dcw_ref[k] += jnp.sum(sl * xav, axis=0)
                if k == 1:
                    dcb_ref[...] += jnp.sum(sl, axis=0)
            dxa_s[pl.ds(base, CONV_CHUNK * NH), :] = acc.reshape(CONV_CHUNK * NH, HD)
            return carry
        lax.fori_loop(0, T // CONV_CHUNK, chunk, 0)
        for h in range(NH):
            dxa_ref[:, HD * h:HD * h + HD] = dxa_s[_zrows(h, T), :].astype(BF16)

    full = lambda shape: pl.BlockSpec(shape, lambda i: (0,) * len(shape))
    return pl.pallas_call(
        body, name=name, grid=(n,),
        out_shape=(jax.ShapeDtypeStruct((lx, D), BF16), jax.ShapeDtypeStruct((4, NH, HD), F32),
                   jax.ShapeDtypeStruct((NH, HD), F32)),
        in_specs=_zhalo_specs(lx) * 2 + [ZT, full((4, NH, HD)), full((4, NH, HD)), full((NH, HD))],
        out_specs=(pl.BlockSpec((T, D), lambda i: (i, 0)), full((4, NH, HD)), full((NH, HD))),
        scratch_shapes=[pltpu.VMEM(((T + 16) * NH, HD), F32), pltpu.VMEM((T * NH, HD), F32)],
        compiler_params=_cp(1, vmem_mb=48),
    )(dxc_a, dxc_a, dxc_a, dxc_b, dxc_b, dxc_b, xa_z, conv_wz, dcw0, dcb0)


def _proj_bwd(dzs, x, dxn, mods, mrow, norm_g, w_full, dng0, name, sgu=None):
    lx = x.shape[0]
    n = lx // T
    nz = len(dzs)
    has_x = dxn is not None
    wks = ([0, 1, 4, 2, 3] if sgu is not None else list(range(nz)))

    def body(*refs):
        it = iter(refs)
        take = lambda m: [next(it) for _ in range(m)]
        dz_refs, w_refs = take(nz), take(len(wks))
        x_ref, sc_ref, ng_ref, dng0_ref = take(4)
        dxn_ref = take(1)[0] if has_x else None
        if sgu is not None:
            u_ref, v_ref, dy_ref, g_ref, b_ref, sw_ref, bt_ref = take(7)
        gx_ref = take(1)[0] if has_x else None
        dng_ref, dsc_ref, dsh_ref = take(3)
        if sgu is not None:
            du_ref, dv_ref, dws_ref, dbs_ref, dlg_ref, dlb_ref, mixed_s, dvn_s = take(8)
        i = pl.program_id(0)

        @pl.when(i == 0)
        def _():
            dng_ref[...] = dng0_ref[...]
            dsc_ref[...] = jnp.zeros_like(dsc_ref)
            dsh_ref[...] = jnp.zeros_like(dsh_ref)
            if sgu is not None:
                for acc in (dws_ref, dbs_ref, dlg_ref, dlb_ref):
                    acc[...] = jnp.zeros_like(acc)

        dhn = _dot_nt(dz_refs[0][...], w_refs[0][...])
        for k in range(1, nz):
            dhn = dhn + _dot_nt(dz_refs[k][...], w_refs[k][...])
        if sgu is not None:
            for ch in range(T // HD):
                rows = slice(HD * ch, HD * ch + HD)
                du, dv = _sgu_bwd_chunk(u_ref[rows, :], v_ref[rows, :], dy_ref[rows, :], g_ref[...], b_ref[...],
                                        sw_ref, bt_ref, mixed_s, dvn_s, dws_ref, dbs_ref, dlg_ref, dlb_ref)
                du_ref[rows, :] = du
                dv_ref[rows, :] = dv
            dhn = dhn + _dot_nt(du_ref[...], w_refs[nz][...]) + _dot_nt(dv_ref[...], w_refs[nz + 1][...])
        xv = x_ref[...]
        r = lax.rsqrt(jnp.mean(xv * xv, axis=-1, keepdims=True) + NORM_EPS)
        xn = xv * r
        ng = ng_ref[...]
        sc1 = 1.0 + sc_ref[mrow:mrow + 1, :]
        t = dhn * xn
        dng_ref[...] += jnp.sum(t * sc1, axis=0, keepdims=True)
        dsc_ref[...] += jnp.sum(t * ng, axis=0, keepdims=True)
        dsh_ref[...] += jnp.sum(dhn, axis=0, keepdims=True)
        if has_x:
            dxh = dhn * (ng * sc1)
            gx_ref[...] = dxn_ref[...] + r * (dxh - xn * jnp.mean(dxh * xn, axis=-1, keepdims=True))

    row = pl.BlockSpec((T, D), lambda i: (i, 0))
    vec = pl.BlockSpec((1, D), lambda i: (0, 0))
    in_specs = [row] * nz + [pl.BlockSpec((D, D), lambda i, k=k: (0, k)) for k in wks]
    in_specs += [row, pl.BlockSpec((8, D), lambda i: (0, 1)), vec, vec]
    args = list(dzs) + [w_full] * len(wks) + [x, mods, norm_g, dng0]
    vs = jax.ShapeDtypeStruct((1, D), F32)
    out_shape, out_specs = (vs, vs, vs), (vec, vec, vec)
    scratch = []
    if has_x:
        in_specs.append(row)
        args.append(dxn)
        out_shape = (jax.ShapeDtypeStruct((lx, D), F32),) + out_shape
        out_specs = (row,) + out_specs
    if sgu is not None:
        wsp = pl.BlockSpec((NH, HD, HD), lambda i: (0, 0, 0))
        bsp = pl.BlockSpec((NH, HD), lambda i: (0, 0))
        in_specs += [row, row, row, vec, vec, wsp, pl.BlockSpec((HD, NH), lambda i: (0, 0))]
        args += list(sgu)
        zb = jax.ShapeDtypeStruct((lx, D), BF16)
        out_shape += (zb, zb, jax.ShapeDtypeStruct((NH, HD, HD), F32), jax.ShapeDtypeStruct((NH, HD), F32), vs, vs)
        out_specs += (row, row, wsp, bsp, vec, vec)
        scratch = [pltpu.VMEM((HD, D), F32), pltpu.VMEM((HD, D), F32)]
    return pl.pallas_call(
        body, name=name, grid=(n,), out_shape=out_shape, in_specs=in_specs, out_specs=out_specs,
        scratch_shapes=scratch, compiler_params=_cp(1, vmem_mb=56),
    )(*args)


def _adam_math(w, g, m, v):
    m = ADAM_B1 * m + (1.0 - ADAM_B1) * g
    v = ADAM_B2 * v + (1.0 - ADAM_B2) * (g * g)
    m_hat = m / (1.0 - ADAM_B1 ** ADAM_STEP)
    v_hat = v / (1.0 - ADAM_B2 ** ADAM_STEP)
    delta = -ADAM_LR * (m_hat / (jnp.sqrt(v_hat) + ADAM_EPS) + ADAM_WD * w)
    return delta, m, v


def _adam_big(w, g, m, v, name):
    rows, cols = w.shape
    tr = 256

    def body(w_ref, g_ref, m_ref, v_ref, d_o, m_o, v_o):
        d, mm, vv = _adam_math(w_ref[...], g_ref[...], m_ref[...], v_ref[...])
        d_o[...] = d
        m_o[...] = mm
        v_o[...] = vv

    blk = pl.BlockSpec((tr, cols), lambda i: (i, 0))
    s = jax.ShapeDtypeStruct((rows, cols), F32)
    return pl.pallas_call(
        body, name=name, grid=(rows // tr,), out_shape=(s, s, s), in_specs=[blk] * 4, out_specs=(blk,) * 3,
        compiler_params=_cp(1, vmem_mb=48),
    )(w, g, m, v)


def _adam_small(items, tot):
    ni = len(items)
    pieces = [it[1] if isinstance(it[1], list) else None for it in items]
    flat = [a for it, pc in zip(items, pieces) for a in ((it[0], it[2], it[3]) if pc is not None else it)]
    n_in = len(flat) + 1
    out_shape = tuple(jax.ShapeDtypeStruct(it[0].shape, F32) for it, pc in zip(items, pieces)
                      for _ in range(4 if pc is not None else 3))
    n_out = len(out_shape)
    n_loads = sum(3 + (len(pc) if pc is not None else 1) for pc in pieces)

    def body(*refs):
        ins, tot_ref, outs = refs[:n_in - 1], refs[n_in - 1], refs[n_in:n_in + n_out]
        bufs = refs[n_in + n_out:n_in + n_out + 7 * ni]
        sem_in, sem_out = refs[n_in + n_out + 7 * ni:]
        loads, q_in, q_sem = [], 0, 0
        for k, pc in enumerate(pieces):
            w_b, g_b, m_b, v_b = bufs[7 * k:7 * k + 4]
            srcs = [(ins[q_in], w_b)]
            if pc is None:
                srcs.append((ins[q_in + 1], g_b))
                q_in += 1
            else:
                srcs += [(tot_ref.at[pl.ds(r0, nr), pl.ds(c0, nc)], g_b.at[pl.ds(d0, nr), :]) for r0, nr, c0, nc, d0 in pc]
            srcs += [(ins[q_in + 1], m_b), (ins[q_in + 2], v_b)]
            q_in += 3
            mine = []
            for src, dst in srcs:
                mine.append(pltpu.make_async_copy(src, dst, sem_in.at[q_sem]))
                q_sem += 1
            loads.append(mine)
        for mine in loads:
            for cp in mine:
                cp.start()
        stores, q_out = [], 0
        for k, pc in enumerate(pieces):
            for cp in loads[k]:
                cp.wait()
            w_b, g_b, m_b, v_b = bufs[7 * k:7 * k + 4]
            res = _adam_math(w_b[...], g_b[...], m_b[...], v_b[...])
            srcs = []
            for q in range(3):
                bufs[7 * k + 4 + q][...] = res[q]
                srcs.append(bufs[7 * k + 4 + q])
            if pc is not None:
                srcs.append(g_b)
            for src in srcs:
                cp = pltpu.make_async_copy(src, outs[q_out], sem_out.at[q_out])
                cp.start()
                stores.append(cp)
                q_out += 1
        for cp in stores:
            cp.wait()

    scratch = [pltpu.VMEM(it[0].shape, F32) for it in items for _ in range(7)]
    scratch += [pltpu.SemaphoreType.DMA((n_loads,)), pltpu.SemaphoreType.DMA((n_out,))]
    res = pl.pallas_call(
        body, name="adam_small", out_shape=out_shape, in_specs=[HBM] * n_in, out_specs=(HBM,) * n_out,
        scratch_shapes=scratch, compiler_params=_cp(vmem_mb=40),
    )(*flat, tot)
    outs, q = [], 0
    for pc in pieces:
        outs.append(tuple(res[q:q + 3]) + ((res[q + 3],) if pc is not None else (None,)))
        q += 4 if pc is not None else 3
    return outs


def kernel(x, c, ctx, c_ctx, ada_w, ada_b, norm_g, w_in, conv_w, conv_b, lru_wa, lru_ba, lru_wx, lru_bx, lru_lambda, sgu_ln_g, sgu_ln_b, sgu_w, sgu_b, w_out, final_g, loss_target, m_c_ctx, m_ada_w, m_ada_b, m_norm_g, m_w_in, m_conv_w, m_conv_b, m_lru_wa, m_lru_ba, m_lru_wx, m_lru_bx, m_lru_lambda, m_sgu_ln_g, m_sgu_ln_b, m_sgu_w, m_sgu_b, m_w_out, m_final_g, v_c_ctx, v_ada_w, v_ada_b, v_norm_g, v_w_in, v_conv_w, v_conv_b, v_lru_wa, v_lru_ba, v_lru_wx, v_lru_bx, v_lru_lambda, v_sgu_ln_g, v_sgu_ln_b, v_sgu_w, v_sgu_b, v_w_out, v_final_g):
    ix, iy, ic = lax.axis_index("x"), lax.axis_index("y"), lax.axis_index("c")
    chip = 2 * ix + iy
    dev = 2 * chip + ic
    lx = x.shape[1]
    lc = ctx.shape[1]

    smalls = jnp.concatenate([conv_w[0], lru_lambda[0], jnp.zeros((10, 256), F32)], axis=0)
    c_ctx2 = c_ctx.reshape(1, D)
    ada_b_j = lax.dynamic_slice(ada_b, (0, 768 * chip), (1, 768))
    mods, c_slots, sm_all, w_in_full, wo_land, ada_land = _gather_in(c, c_ctx2, ada_w[0], ada_b_j, w_in[0], w_out[0],
                                                                     smalls)
    wo_ss, wo_rs, ada_ss, ada_rs, wo_land, ada_land, token = _late_gather_start(wo_land, ada_land)
    mods = mods + token[0:1, 0:1]
    sm3 = sm_all.reshape(NCHIP, 16, 256)
    conv_w_full = sm3[:, 0:4, :].transpose(1, 0, 2).reshape(4, D)
    lam_full = sm3[:, 4:6, :].transpose(1, 0, 2).reshape(2, D)
    conv_wz = conv_w_full.reshape(4, NH, HD)
    conv_bz = conv_b.reshape(NH, HD)
    lamcat = lam_full.reshape(2, NH, HD).transpose(1, 0, 2).reshape(NH, 2 * HD)
    wa, wx, ba, bx = lru_wa[0], lru_wx[0], lru_ba[0], lru_bx[0]
    wcat = jnp.concatenate([wa[0], wx[0], wa[1], wx[1]], axis=-1).astype(BF16)
    bcat = jnp.concatenate([ba[0], bx[0], ba[1], bx[1]], axis=-1)
    sgu_wb = sgu_w[0].astype(BF16)
    sgu_bt = sgu_b[0].T
    final_g2 = final_g.reshape(1, D)

    zero_s = jnp.zeros((NH, HD), F32)
    hn_c, xa_c = _proj(ctx[0], mods, 1, norm_g, w_in_full, 1, "proj_ctx")
    xaz_c, xcz_c, af_c, ab_c, hf_c, hb_c, gf_c, gb_c, hf0, hb0 = _lru_fwd(xa_c, conv_wz, conv_bz, wcat, bcat, lamcat,
                                                                           zero_s, zero_s, "lru_fwd_ctx")

    hn, xa, ga, u, v, gb, ys = _proj(x[0], mods, 0, norm_g, w_in_full, 5, "proj",
                                     (sgu_ln_g, sgu_ln_b, sgu_wb, sgu_bt))
    xaz, xcz, af, ab, hf, hb, gf, gb_l, _, _ = _lru_fwd(xa, conv_wz, conv_bz, wcat, bcat, lamcat, hf0, hb0, "lru_fwd")

    w_out_full = _late_gather_wait(wo_land, wo_ss, wo_rs, "w_out", hf, "late_gather_wait_w_out")
    (loss_part, dfg, dgx, dxn, y, do, dga, dgb, dyl_z, dys) = _out_fwd_bwd(
        hf, hb, ga, gb, ys, x[0], loss_target[0], mods, final_g2, w_out_full)

    zw = jnp.zeros((NH, HD, 4 * HD), F32)
    zb = jnp.zeros((NH, 4 * HD), F32)
    zl = jnp.zeros((NH, 2 * HD), F32)
    dxc_b, dxc_f, dwc, dbc, dlc, dh0b, dh0f = _lru_bwd(xcz, dyl_z, dyl_z, hf, hb, af, ab, gf, gb_l, hf0, hb0,
                                                        wcat, lamcat, zw, zb, zl, "lru_bwd")
    dxa, dcw, dcb = _conv_bwd(dxc_b, dxc_f, xaz, conv_wz, jnp.zeros((4, NH, HD), F32), zero_s, "conv_bwd")

    zc = jnp.zeros((lc * NH, HD), F32)
    dhf_c = lax.dynamic_update_slice(zc, dh0f, ((lc - 1) * NH, 0))
    dhb_c = lax.dynamic_update_slice(zc, dh0b, (0, 0))
    dxc_bc, dxc_fc, dwc, dbc, dlc, _, _ = _lru_bwd(xcz_c, dhb_c, dhf_c, hf_c, hb_c, af_c, ab_c, gf_c, gb_c,
                                                    zero_s, zero_s, wcat, lamcat, dwc, dbc, dlc, "lru_bwd_ctx")
    dxa_c, dcw, dcb = _conv_bwd(dxc_bc, dxc_fc, xaz_c, conv_wz, dcw, dcb, "conv_bwd_ctx")

    grad_x, dng, dsc_x, dsh_x, du, dv, d_sgu_w, d_sgu_b, d_ln_g, d_ln_b = _proj_bwd(
        [dxa, dga, dgb], x[0], dxn, mods, 0, norm_g, w_in_full, jnp.zeros((1, D), F32), "proj_bwd",
        (u, v, dys, sgu_ln_g, sgu_ln_b, sgu_wb, sgu_bt))
    dzs = [dxa, dga, du, dv, dgb]
    dng, dsc_c, dsh_c = _proj_bwd([dxa_c], ctx[0], None, mods, 1, norm_g, w_in_full, dng, "proj_bwd_ctx")

    dmx = jnp.concatenate([dsh_x, dsc_x, dgx], axis=0)
    dmc = jnp.concatenate([dsh_c, dsc_c, jnp.zeros((1, D), F32)], axis=0)
    slot = jnp.concatenate([dmx, loss_part], axis=0)
    slots = lax.dynamic_update_slice(jnp.zeros((32, D), F32), slot, (4 * dev, 0))
    vecs = jnp.concatenate([dfg, dng, dcb.reshape(1, D), d_ln_g, d_ln_b, dcw.reshape(4, D), dmc,
                            jnp.zeros((4, D), F32), slots], axis=0)
    d_sgu_w4 = d_sgu_w.reshape(4, 256, HD).transpose(1, 0, 2).reshape(256, 4 * HD)
    pad8 = lambda a: jnp.pad(a, ((0, 8 - a.shape[0]), (0, 4 * HD - a.shape[1])))
    pack = jnp.concatenate([dwc.reshape(NH * HD, 4 * HD), pad8(dbc), pad8(dlc), d_sgu_w4, pad8(d_sgu_b),
                            vecs.reshape(96, 4 * HD), jnp.zeros((8, 4 * HD), F32)], axis=0)
    g_w_in, g_w_out, tot = _grads_reduce(hn, dzs, hn_c, dxa_c, y, do, pack)

    n_w = NH * HD
    g_lru_wa = [(0, n_w, 2 * HD * d, HD, n_w * d) for d in range(2)]
    g_lru_wx = [(0, n_w, 2 * HD * d + HD, HD, n_w * d) for d in range(2)]
    g_lru_ba = [(n_w, NH, 2 * HD * d, HD, NH * d) for d in range(2)]
    g_lru_bx = [(n_w, NH, 2 * HD * d + HD, HD, NH * d) for d in range(2)]
    g_sgu_w = [(1040, 256, HD * q, HD, 256 * q) for q in range(4)]
    g_sgu_b = [(1296, NH, 0, HD, 0)]
    g_lc = tot[1032:1040, 0:2 * HD]
    tv = tot[1304:1400].reshape(48, D)
    g_final_g, g_norm_g, g_conv_b, g_ln_g, g_ln_b = tv[0:1], tv[1:2], tv[2:3], tv[3:4], tv[4:5]
    g_conv_w_full = tv[5:9]
    dmc_tot = tv[9:12].reshape(1, 3 * D)
    slots_all = tv[16:48].reshape(8, 4, D)
    dmx_all = slots_all[:, 0:3, :].reshape(8, 3 * D)
    c_all = c_slots.reshape(8, 8, D)[:, 0, :]
    g_lam_full = jnp.stack([g_lc[:, 0:HD], g_lc[:, HD:2 * HD]]).reshape(2, D)
    g_conv_w = lax.dynamic_slice(g_conv_w_full, (0, 256 * chip), (4, 256))
    g_lam = lax.dynamic_slice(g_lam_full, (0, 256 * chip), (2, 256))
    dmx_all_j = lax.dynamic_slice(dmx_all, (0, 768 * chip), (8, 768))
    dmc_j = lax.dynamic_slice(dmc_tot, (0, 768 * chip), (1, 768))
    ada_full = _late_gather_wait(ada_land, ada_ss, ada_rs, "ada_w", tot, "late_gather_wait_ada_w")
    g_ada_w, g_ada_b, g_c_ctx = _ada_bwd(c_all, dmx_all_j, dmc_j, dmx_all, dmc_tot, c_ctx2, ada_full)

    big = {
        "ada_w": _adam_big(ada_w[0], g_ada_w, m_ada_w[0], v_ada_w[0], "adam_ada_w"),
        "w_in": _adam_big(w_in[0], g_w_in, m_w_in[0], v_w_in[0], "adam_w_in"),
        "w_out": _adam_big(w_out[0], g_w_out, m_w_out[0], v_w_out[0], "adam_w_out"),
    }
    small_in = {
        "c_ctx": (c_ctx, g_c_ctx, m_c_ctx, v_c_ctx, (1, D)),
        "ada_b": (ada_b, g_ada_b, m_ada_b, v_ada_b, (1, 3 * D)),
        "norm_g": (norm_g, g_norm_g, m_norm_g, v_norm_g, (1, D)),
        "conv_w": (conv_w, g_conv_w, m_conv_w, v_conv_w, (4, 256)),
        "conv_b": (conv_b, g_conv_b, m_conv_b, v_conv_b, (1, D)),
        "lru_wa": (lru_wa, g_lru_wa, m_lru_wa, v_lru_wa, (2 * NH * HD, HD)),
        "lru_ba": (lru_ba, g_lru_ba, m_lru_ba, v_lru_ba, (2 * NH, HD)),
        "lru_wx": (lru_wx, g_lru_wx, m_lru_wx, v_lru_wx, (2 * NH * HD, HD)),
        "lru_bx": (lru_bx, g_lru_bx, m_lru_bx, v_lru_bx, (2 * NH, HD)),
        "lru_lambda": (lru_lambda, g_lam, m_lru_lambda, v_lru_lambda, (2, 256)),
        "sgu_ln_g": (sgu_ln_g, g_ln_g, m_sgu_ln_g, v_sgu_ln_g, (1, D)),
        "sgu_ln_b": (sgu_ln_b, g_ln_b, m_sgu_ln_b, v_sgu_ln_b, (1, D)),
        "sgu_w": (sgu_w, g_sgu_w, m_sgu_w, v_sgu_w, (NH * HD, HD)),
        "sgu_b": (sgu_b, g_sgu_b, m_sgu_b, v_sgu_b, (NH, HD)),
        "final_g": (final_g, g_final_g, m_final_g, v_final_g, (1, D)),
    }
    names_small = list(small_in)
    res_small = _adam_small([tuple(a if isinstance(a, list) else a.reshape(small_in[k][4]) for a in small_in[k][:4])
                             for k in names_small], tot)
    full_shapes = {"ada_w": ada_w.shape, "w_in": w_in.shape, "w_out": w_out.shape}
    grads, deltas, new_m, new_v = {}, {}, {}, {}
    for k in ("ada_w", "w_in", "w_out"):
        g = {"ada_w": g_ada_w, "w_in": g_w_in, "w_out": g_w_out}[k]
        grads[k] = g.reshape(full_shapes[k])
        deltas[k], new_m[k], new_v[k] = (a.reshape(full_shapes[k]) for a in big[k])
    for k, res in zip(names_small, res_small):
        shape = small_in[k][0].shape
        grads[k] = (small_in[k][1] if res[3] is None else res[3]).reshape(shape)
        deltas[k], new_m[k], new_v[k] = (a.reshape(shape) for a in res[:3])

    loss = jnp.sum(slots_all[:, 3, 0:3])
    order = ["c_ctx", "ada_w", "ada_b", "norm_g", "w_in", "conv_w", "conv_b", "lru_wa", "lru_ba", "lru_wx", "lru_bx",
             "lru_lambda", "sgu_ln_g", "sgu_ln_b", "sgu_w", "sgu_b", "w_out", "final_g"]
    return (loss, grad_x.reshape(x.shape), *[grads[k] for k in order], *[deltas[k] for k in order],
            *[new_m[k] for k in order], *[new_v[k] for k in order])
```

```python
import functools

import jax
import jax.numpy as jnp
from jax import lax
from jax.experimental import pallas as pl
from jax.experimental.pallas import tpu as pltpu

F32 = jnp.float32
BF16 = jnp.bfloat16

D = 1024
NH = 8
HD = 128
NCHIP = 4
T = 256
NORM_EPS = 1e-6
LN_EPS = 1e-5
LRU_C = 8.0
ADAM_LR = 0.001
ADAM_B1 = 0.9
ADAM_B2 = 0.999
ADAM_EPS = 1e-08
ADAM_WD = 0.01
ADAM_STEP = 10

VMEM = pl.BlockSpec(memory_space=pltpu.VMEM)
ANY = pl.BlockSpec(memory_space=pl.ANY)
MESH = pl.DeviceIdType.MESH


def _cp(n_grid=0, vmem_mb=None):
    kw = {}
    if n_grid:
        kw["dimension_semantics"] = ("arbitrary",) * n_grid
    if vmem_mb:
        kw["vmem_limit_bytes"] = vmem_mb << 20
    return pltpu.CompilerParams(**kw)


def _sigmoid(x):
    return 0.5 * jnp.tanh(0.5 * x) + 0.5


def _silu_and_grad(x):
    s = _sigmoid(x)
    return x * s, s * (1.0 + x * (1.0 - s))


_GELU_K = 0.7978845608028654
_GELU_C = 0.044715


def _gelu_and_grad(x):
    x2 = x * x
    th = jnp.tanh(x * (_GELU_K + (_GELU_K * _GELU_C) * x2))
    p = 0.5 + 0.5 * th
    g = x * p
    dg = p + g * (1.0 - th) * (_GELU_K + (3.0 * _GELU_K * _GELU_C) * x2)
    return g, dg


def _softplus(x):
    return jnp.maximum(x, 0.0) + jnp.log1p(jnp.exp(-jnp.abs(x)))


def _lru_gate(pre, lam_row, d, off=None):
    off = 256 * d if off is None else off
    r = _sigmoid(pre[:, off:off + HD])
    gi = _sigmoid(pre[:, off + HD:off + 2 * HD])
    lam = lam_row[:, HD * d:HD * d + HD]
    sp = _softplus(-lam)
    la = (-LRU_C) * r * sp
    a = jnp.exp(la)
    x2 = 2.0 * la
    m2 = jnp.where(x2 > -1e-3, -x2 * (1.0 + 0.5 * x2), 1.0 - a * a)
    mult = jnp.sqrt(m2)
    return r, gi, lam, sp, a, mult


def _dot(a, b):
    return jnp.dot(a, b, preferred_element_type=F32)


def _dot_tn(a, b):
    return lax.dot_general(a, b, (((0,), (0,)), ((), ())), preferred_element_type=F32)


def _dot_nt(a, b):
    return lax.dot_general(a, b, (((1,), (1,)), ((), ())), preferred_element_type=F32)


def _mo(v, m):
    return v if isinstance(v, int) else pl.multiple_of(v, m)


def _zrows(h, n):
    return pl.ds(h, n, stride=NH)


def _gather_in(c, c_ctx, ada_w, ada_b_j, w_in, w_out, smalls):
    nch = [1, 4]
    wrows = lambda cc, q: (pl.ds(_mo(512 * cc, 16), 512) if q is None
                           else pl.ds(_mo(512 * cc + (512 // nch[1]) * q, 16), 512 // nch[1]))
    specs = [
        ((64, 256), F32, lambda r, jj, cc, q=None: r.at[pl.ds(_mo(16 * jj + 8 * cc, 8), 8), :]),
        ((D, 5120), BF16, lambda r, jj, cc, q=None: r.at[wrows(cc, q), pl.ds(_mo(1280 * jj, 128), 1280)]),
    ]
    halves = [lambda r, cc, q=None: r.at[pl.ds(_mo(8 * cc, 8), 8), :],
              lambda r, cc, q=None: r.at[wrows(cc, q), :]]
    na = len(specs)
    sem_base = [0, 6 * nch[0]]
    sidx = lambda a, q, k: sem_base[a] + 6 * q + k
    n_tiny = 6 * sum(nch)
    n_sem = n_tiny + 10

    def body(c_ref, cc_ref, ada_ref, adab_ref, win_ref, wout_ref, sm_ref,
             mods_o, call_o, sm_o, win_o, wol_o, adal_o, s_win, s_ada, s_wout, f_win, f_ada, f_wout, cslot, lhs, mbuf,
             send_sems, recv_sems, local_sems, load_sems):
        x, y, c = lax.axis_index("x"), lax.axis_index("y"), lax.axis_index("c")
        j = 2 * x + y
        dev = 2 * j + c
        sib = (x, y, 1 - c)
        chips = [(1 - x, y), (x, 1 - y), (1 - x, 1 - y)]
        cj = [2 * cx + cy for cx, cy in chips]
        outs = [sm_o, win_o]
        srcs = [sm_ref, s_win]

        def copy(idx, src, dst, to):
            return pltpu.make_async_remote_copy(src_ref=src, dst_ref=dst, send_sem=send_sems.at[idx],
                                                recv_sem=recv_sems.at[idx], device_id=to, device_id_type=MESH)

        sends = []

        def start(cp):
            cp.start()
            sends.append(cp)

        cslot[...] = jnp.zeros_like(cslot)
        cslot[0:1, :] = c_ref[...]
        my_slot = pl.ds(_mo(8 * dev, 8), 8)
        others = [sib] + [(*chips[k], c) for k in range(3)] + [(*chips[k], 1 - c) for k in range(3)]
        other_dev = [dev + 1 - 2 * c] + [2 * cj[k] + c for k in range(3)] + [2 * cj[k] + 1 - c for k in range(3)]
        base = n_tiny
        for r in range(7):
            start(copy(base + r, cslot, call_o.at[my_slot, :], others[r]))
        call_o[my_slot, :] = cslot[...]

        crow = 512 // nch[1]
        loads = []
        for cc in (c, 1 - c):
            for q in range(nch[1]):
                rows = pl.ds(_mo(512 * cc + crow * q, 16), crow)
                loads.append(pltpu.make_async_copy(win_ref.at[rows, :], f_win.at[rows, :], load_sems.at[len(loads)]))
        loads.append(pltpu.make_async_copy(ada_ref, f_ada, load_sems.at[len(loads)]))
        loads.append(pltpu.make_async_copy(wout_ref, f_wout, load_sems.at[len(loads)]))
        for ld in loads:
            ld.start()
        for k in range(2):
            start(copy(sidx(0, 0, k), halves[0](srcs[0], c), specs[0][2](outs[0], j, c), (*chips[k], c)))
        for q in range(nch[1]):
            loads[q].wait()
            rows = pl.ds(_mo(512 * c + crow * q, 16), crow)
            s_win[rows, :] = f_win[rows, :].astype(BF16)
            for k in range(2):
                start(copy(sidx(1, q, k), halves[1](s_win, c, q), specs[1][2](win_o, j, c, q), (*chips[k], c)))
        for q in range(nch[1]):
            loads[nch[1] + q].wait()
            rows = pl.ds(_mo(512 * (1 - c) + crow * q, 16), crow)
            s_win[rows, :] = f_win[rows, :].astype(BF16)
        local = []
        for a in range(na):
            for cc in range(2):
                lc = pltpu.make_async_copy(halves[a](srcs[a], cc), specs[a][2](outs[a], j, cc), local_sems.at[2 * a + cc])
                lc.start()
                local.append(lc)
        loads[2 * nch[1]].wait()
        s_ada[...] = f_ada[...].astype(BF16)
        loads[2 * nch[1] + 1].wait()
        s_wout[...] = f_wout[...].astype(BF16)
        for q, (src, dst) in enumerate([(s_wout, wol_o.at[pl.ds(_mo(512 * j, 16), 512), :]),
                                        (s_ada, adal_o.at[:, pl.ds(_mo(768 * j, 128), 768)])]):
            lc = pltpu.make_async_copy(src, dst, local_sems.at[2 * na + q])
            lc.start()
            local.append(lc)

        for r in range(7):
            slot = call_o.at[pl.ds(_mo(8 * other_dev[r], 8), 8), :]
            copy(base + r, slot, slot, sib).wait_recv()
        lhs[...] = jnp.zeros_like(lhs)
        for b in range(8):
            cv = call_o[8 * b:8 * b + 1, :]
            lhs[b:b + 1, :] = cv * _sigmoid(cv)
        cv = cc_ref[...]
        lhs[8:9, :] = cv * _sigmoid(cv)
        mbuf[j] = _dot(lhs[...].astype(BF16), s_ada[...]) + adab_ref[...]
        for k in range(3):
            start(copy(base + 7 + k, mbuf.at[j], mbuf.at[j], (*chips[k], c)))
        for k in range(3):
            copy(base + 7 + k, mbuf.at[cj[k]], mbuf.at[cj[k]], sib).wait_recv()
        mods_o[...] = jnp.zeros_like(mods_o)
        for jj in range(NCHIP):
            mods_o[0:1, 768 * jj:768 * jj + 768] = mbuf[jj, pl.ds(dev, 1), :]
            mods_o[1:2, 768 * jj:768 * jj + 768] = mbuf[jj, 8:9, :]

        kx = [1 - x, x, 1 - x]
        ky = [y, 1 - y, 1 - y]
        pick = lambda k, lst: jnp.where(k == 0, lst[0], jnp.where(k == 1, lst[1], lst[2]))
        for a in range(na):
            for q in range(nch[a]):
                for step, k in enumerate([c, 1 - c]):
                    reg = specs[a][2](outs[a], pick(k, cj), c, q)
                    copy(sidx(a, q, k), reg, reg, sib).wait_recv()
                    if step == 0:
                        start(copy(sidx(a, q, 2), reg, reg, (pick(1 - c, kx), pick(1 - c, ky), c)))
                    start(copy(sidx(a, q, 3 + k), reg, reg, sib))
        for a in range(na):
            for q in range(nch[a]):
                reg = specs[a][2](outs[a], cj[2], c, q)
                copy(sidx(a, q, 2), reg, reg, sib).wait_recv()
                start(copy(sidx(a, q, 5), reg, reg, sib))
        for a in range(na):
            for q in range(nch[a]):
                for k in range(3):
                    reg = specs[a][2](outs[a], cj[k], 1 - c, q)
                    copy(sidx(a, q, 3 + k), reg, reg, sib).wait_recv()
        for cp in sends:
            cp.wait_send()
        for lc in local:
            lc.wait()

    out_shape = (jax.ShapeDtypeStruct((8, 3 * D), F32), jax.ShapeDtypeStruct((64, D), F32),
                 jax.ShapeDtypeStruct(specs[0][0], F32), jax.ShapeDtypeStruct(specs[1][0], BF16),
                 jax.ShapeDtypeStruct((2048, D), BF16), jax.ShapeDtypeStruct((D, 3 * D), BF16))
    return pl.pallas_call(
        body, name="gather_in", out_shape=out_shape,
        in_specs=[VMEM, VMEM, ANY, VMEM, ANY, ANY, VMEM], out_specs=(VMEM, VMEM, VMEM, ANY, ANY, ANY),
        scratch_shapes=[pltpu.VMEM((D, 1280), BF16), pltpu.VMEM((D, 768), BF16), pltpu.VMEM((512, D), BF16),
                        pltpu.VMEM((D, 1280), F32), pltpu.VMEM((D, 768), F32), pltpu.VMEM((512, D), F32),
                        pltpu.VMEM((8, D), F32), pltpu.VMEM((16, D), F32), pltpu.VMEM((NCHIP, 16, 768), F32),
                        pltpu.SemaphoreType.DMA((n_sem,)), pltpu.SemaphoreType.DMA((n_sem,)),
                        pltpu.SemaphoreType.DMA((2 * na + 2,)), pltpu.SemaphoreType.DMA((2 * nch[1] + 2,))],
        compiler_params=_cp(vmem_mb=56),
    )(c, c_ctx, ada_w, ada_b_j, w_in, w_out, smalls)


HBM = pl.BlockSpec(memory_space=pltpu.HBM)
SEM = pl.BlockSpec(memory_space=pltpu.SEMAPHORE)


def _late_gather_regions(x, y, c):
    chips = [(1 - x, y), (x, 1 - y), (1 - x, 1 - y)]
    wo_reg = lambda r, jj, cc: r.at[pl.ds(_mo(512 * jj + 256 * cc, 16), 256), :]
    ada_reg = lambda r, jj, cc: r.at[pl.ds(_mo(512 * cc, 16), 512), pl.ds(_mo(768 * jj, 128), 768)]
    return chips, wo_reg, ada_reg


def _late_gather_start(wo_land, ada_land):
    def body(wol_ref, adal_ref, wo_ss, wo_rs, ada_ss, ada_rs, wol_thru, adal_thru, token):
        x, y, c = lax.axis_index("x"), lax.axis_index("y"), lax.axis_index("c")
        j = 2 * x + y
        chips, wo_reg, ada_reg = _late_gather_regions(x, y, c)
        for k in range(3):
            for cc in range(2):
                pltpu.make_async_remote_copy(src_ref=wo_reg(wol_ref, j, c), dst_ref=wo_reg(wol_ref, j, c),
                                             send_sem=wo_ss.at[2 * k + cc], recv_sem=wo_rs.at[2 * k + c],
                                             device_id=(*chips[k], cc), device_id_type=MESH).start()
        for k in range(3):
            for cc in range(2):
                pltpu.make_async_remote_copy(src_ref=ada_reg(adal_ref, j, c), dst_ref=ada_reg(adal_ref, j, c),
                                             send_sem=ada_ss.at[2 * k + cc], recv_sem=ada_rs.at[2 * k + c],
                                             device_id=(*chips[k], cc), device_id_type=MESH).start()
        token[...] = jnp.zeros_like(token)

    sems = pltpu.SemaphoreType.DMA((6,))
    return pl.pallas_call(
        body, name="late_gather_start",
        out_shape=(sems, sems, sems, sems, pltpu.HBM(wo_land.shape, BF16), pltpu.HBM(ada_land.shape, BF16),
                   jax.ShapeDtypeStruct((8, 128), F32)),
        in_specs=(HBM, HBM), out_specs=(SEM, SEM, SEM, SEM, HBM, HBM, VMEM), input_output_aliases={0: 4, 1: 5},
        compiler_params=pltpu.CompilerParams(has_side_effects=pltpu.SideEffectType.DATAFLOW_SIDE_EFFECTING),
    )(pltpu.with_memory_space_constraint(wo_land, pltpu.HBM), pltpu.with_memory_space_constraint(ada_land, pltpu.HBM))


def _late_gather_wait(land, send_sems, recv_sems, which, after, name):
    def body(land_ref, ss, rs, after_ref, land_out):
        x, y, c = lax.axis_index("x"), lax.axis_index("y"), lax.axis_index("c")
        j = 2 * x + y
        chips, wo_reg, ada_reg = _late_gather_regions(x, y, c)
        reg = wo_reg if which == "w_out" else ada_reg
        for k in range(3):
            kj = 2 * chips[k][0] + chips[k][1]
            for cc in range(2):
                cp = pltpu.make_async_remote_copy(src_ref=reg(land_ref, j, c), dst_ref=reg(land_ref, kj, cc),
                                                  send_sem=ss.at[2 * k + cc], recv_sem=rs.at[2 * k + cc],
                                                  device_id=(*chips[k], cc), device_id_type=MESH)
                cp.wait_send()
                cp.wait_recv()

    return pl.pallas_call(
        body, name=name, out_shape=pltpu.HBM(land.shape, land.dtype),
        in_specs=(HBM, SEM, SEM, ANY), out_specs=HBM, input_output_aliases={0: 0},
        compiler_params=pltpu.CompilerParams(has_side_effects=pltpu.SideEffectType.DATAFLOW_SIDE_EFFECTING),
    )(land, send_sems, recv_sems, after)


RCHUNK = 16


def _grads_reduce(hn, dzs, hn_c, y, do, pack):
    rp = pack.shape[0]
    hp = rp // 2
    assert hp % RCHUNK == 0
    wi_w = 1280
    lx, lc = hn.shape[0], hn_c.shape[0]
    lt = lx + lc
    n_dz = len(dzs)

    def body(*refs):
        hn_hbm, dz_hbm = refs[0], refs[1:1 + n_dz]
        hnc_hbm, y_hbm, do_hbm, pk_hbm, wi_out, wo_out, pk_out = refs[1 + n_dz:8 + n_dz]
        (hn_mine, hn_other, dzbuf, wi_other, wi_mine, wi_recv, wi_send, wi_rb,
         y_blk, do_mine, do_other, wo_other, wo_mine, wo_recv, wo_send, wo_rb,
         pk_mine, pk_recv, pk_send, pk_rb, pk_own, send_sems, recv_sems, local_sems) = refs[8 + n_dz:]
        x, y, c = lax.axis_index("x"), lax.axis_index("y"), lax.axis_index("c")
        j = 2 * x + y
        sib = (x, y, 1 - c)
        chips = [(1 - x, y), (x, 1 - y), (1 - x, 1 - y)]
        cj = [2 * cx + cy for cx, cy in chips]
        near = (jnp.where(c == 0, 1 - x, x), jnp.where(c == 0, y, 1 - y), c)
        slabs = [cj[2], cj[0], cj[1], j]

        def copy(k, src, dst, to):
            return pltpu.make_async_remote_copy(src_ref=src, dst_ref=dst, send_sem=send_sems.at[k],
                                                recv_sem=recv_sems.at[k], device_id=to, device_id_type=MESH)

        def local(k, src, dst):
            cp = pltpu.make_async_copy(src, dst, local_sems.at[k])
            cp.start()
            return cp

        rows_half = lambda r, cc, n: r.at[pl.ds(_mo(cc * n, 16), n), :]
        cols_half = lambda r, cc, n: r.at[:, pl.ds(_mo(cc * n, 128), n)]
        pk_piece = lambda r, cc, jj: r.at[pl.ds(_mo(cc * hp, 16), hp), pl.ds(_mo(jj * 128, 128), 128)]

        sends = []

        def start(cp):
            cp.start()
            sends.append(cp)

        def dz_pieces(s):
            g0 = wi_w * s
            k0, off0 = g0 // D, g0 % D
            w0 = min(D - off0, wi_w)
            pieces = [(k0, off0, w0, 0)]
            if w0 < wi_w:
                pieces.append((k0 + 1, 0, wi_w - w0, w0))
            return pieces

        def dz_copies(s):
            cps = []
            for q, (k, off, w, dst) in enumerate(dz_pieces(s)):
                cps.append(pltpu.make_async_copy(dz_hbm[k].at[pl.ds(lc if k == 0 else 0, lx), pl.ds(off, w)],
                                                 dzbuf.at[pl.ds(0, lx), pl.ds(dst, w)], local_sems.at[11 + q]))
            if s == 0:
                cps.append(pltpu.make_async_copy(dz_hbm[0].at[pl.ds(0, lc), :], dzbuf.at[pl.ds(lx, lc), pl.ds(0, D)],
                                                 local_sems.at[13]))
            return cps

        def dz_load(sl):
            for s in range(NCHIP):
                @pl.when(sl == s)
                def _():
                    if s == 0:
                        dzbuf[pl.ds(lx, lc), pl.ds(D, wi_w - D)] = jnp.zeros((lc, wi_w - D), BF16)
                    else:
                        dzbuf[pl.ds(lx, lc), :] = jnp.zeros((lc, wi_w), BF16)
                    for cp in dz_copies(s):
                        cp.start()

        def dz_wait(sl):
            for s in range(NCHIP):
                @pl.when(sl == s)
                def _():
                    for cp in dz_copies(s):
                        cp.wait()

        l_pk = local(0, rows_half(pk_hbm, c, hp), pk_mine)
        start(copy(0, rows_half(pk_hbm, 1 - c, hp), pk_recv, sib))
        col = lambda r, cc: r.at[:, pl.ds(_mo(cc * 512, 128), 512)]
        do_loads = [local(7, col(do_hbm, c), do_mine), local(14, col(do_hbm, 1 - c), do_other)]
        hn_loads = [local(2, col(hn_hbm, c), hn_mine.at[pl.ds(0, lx), :]),
                    local(3, col(hnc_hbm, c), hn_mine.at[pl.ds(lx, lc), :]),
                    local(4, col(hn_hbm, 1 - c), hn_other.at[pl.ds(0, lx), :]),
                    local(5, col(hnc_hbm, 1 - c), hn_other.at[pl.ds(lx, lc), :])]
        y_copy = lambda s: pltpu.make_async_copy(col(y_hbm, slabs[s]), y_blk, local_sems.at[1])
        y_copy(0).start()
        dz_load(slabs[0])

        def pair_sum(mine, recv, send, nrows, keep, relayed=None):
            def step(i, carry):
                rows = pl.ds(_mo(i * RCHUNK, RCHUNK), RCHUNK)
                s = mine[rows, :] + recv[rows, :].astype(F32)
                if relayed is not None:
                    s = s + relayed[rows, :].astype(F32)
                if keep:
                    mine[rows, :] = s
                if send is not None:
                    send[rows, :] = s.astype(BF16)
                return carry
            lax.fori_loop(0, nrows // RCHUNK, step, 0)

        def chip_sum(own, rb, nrows, terms=(0, 1, 2)):
            def step(i, carry):
                rows = pl.ds(_mo(i * RCHUNK, RCHUNK), RCHUNK)
                acc = own[rows, :]
                for q in terms:
                    acc = acc + rb[q, rows, :].astype(F32)
                own[rows, :] = acc
                return carry
            lax.fori_loop(0, nrows // RCHUNK, step, 0)

        w_in_g = dict(other=wi_other, mine=wi_mine, recv=wi_recv, send=wi_send, rb=wi_rb, p1_sems=(2, 3, 4, 5), p2_sem=12,
                      p1=[None] * NCHIP, wait_load=lambda s: dz_wait(slabs[s]), load=lambda s: dz_load(slabs[s]),
                      dot_other=lambda: _dot_tn(hn_other[...], dzbuf[...]), dot_mine=lambda: _dot_tn(hn_mine[...], dzbuf[...]))
        w_out_g = dict(other=wo_other, mine=wo_mine, recv=wo_recv, send=wo_send, rb=wo_rb, p1_sems=(1, 24, 25, 26), p2_sem=9,
                       p1=[None] * NCHIP, wait_load=lambda s: y_copy(s).wait(), load=lambda s: y_copy(s).start(),
                       dot_other=lambda: _dot_tn(y_blk[...], do_other[...]), dot_mine=lambda: _dot_tn(y_blk[...], do_mine[...]))

        def piece_matmuls(g, s):
            if s >= 2:
                g["p1"][s - 2].wait_send()
            g["wait_load"](s)
            g["other"][s % 2] = g["dot_other"]().astype(BF16)
            g["p1"][s] = copy(g["p1_sems"][s], g["other"].at[s % 2], g["recv"].at[s], sib)
            g["p1"][s].start()
            g["mine"][s % 2] = g["dot_mine"]()
            if s + 1 < NCHIP:
                g["load"](s + 1)

        def piece_finish(g, s):
            mine, recv, send, rb, p2 = g["mine"].at[s % 2], g["recv"].at[s], g["send"], g["rb"], g["p2_sem"]
            nrows = mine.shape[0]
            copy(g["p1_sems"][s], recv, recv, sib).wait_recv()
            if s == 3:
                pair_sum(mine, recv, None, nrows, True)
                return
            if s == 0:
                pair_sum(mine, recv, send.at[0], nrows, False)
                start(copy(p2, send.at[0], rb.at[0], near))
                return
            adds_relayed = c == (1 if s == 1 else 0)

            @pl.when(adds_relayed)
            def _():
                copy(p2, rb.at[0], rb.at[0], sib).wait_recv()
                pair_sum(mine, recv, send.at[s], nrows, False, rb.at[0])

            @pl.when(jnp.logical_not(adds_relayed))
            def _():
                pair_sum(mine, recv, send.at[s], nrows, False)
            start(copy(p2 + s, send.at[s], rb.at[s], (*chips[s - 1], c)))

        def piece_total(g):
            for k in (1, 2):
                copy(g["p2_sem"] + k, g["rb"].at[k], g["rb"].at[k], sib).wait_recv()
            chip_sum(g["mine"].at[1], g["rb"], g["mine"].shape[1], (1, 2))

        for cp in do_loads:
            cp.wait()
        piece_matmuls(w_out_g, 0)
        piece_matmuls(w_out_g, 1)
        piece_finish(w_out_g, 0)
        piece_matmuls(w_out_g, 2)
        piece_finish(w_out_g, 1)
        piece_matmuls(w_out_g, 3)
        piece_finish(w_out_g, 2)

        for cp in hn_loads:
            cp.wait()
        piece_matmuls(w_in_g, 0)

        l_pk.wait()
        copy(0, pk_recv, pk_recv, sib).wait_recv()
        pair_sum(pk_mine, pk_recv, pk_send, hp, True)
        for k in range(3):
            start(copy(6 + k, pk_send.at[:, pl.ds(_mo(cj[k] * 128, 128), 128)], pk_rb.at[k], (*chips[k], c)))
        l_pk_own = local(6, pk_mine.at[:, pl.ds(_mo(j * 128, 128), 128)], pk_own)

        piece_matmuls(w_in_g, 1)
        piece_finish(w_in_g, 0)

        l_pk_own.wait()
        for k in range(3):
            copy(6 + k, pk_rb.at[k], pk_rb.at[k], sib).wait_recv()
        chip_sum(pk_own, pk_rb, hp)
        l_pk_out = local(8, pk_own, pk_piece(pk_out, c, j))
        start(copy(15, pk_own, pk_piece(pk_out, c, j), sib))
        for k in range(2):
            start(copy(16 + k, pk_own, pk_piece(pk_out, c, j), (*chips[k], c)))

        piece_matmuls(w_in_g, 2)
        piece_finish(w_in_g, 1)
        piece_matmuls(w_in_g, 3)
        piece_finish(w_in_g, 2)

        piece_finish(w_out_g, 3)
        piece_total(w_out_g)
        l_wo_out = local(9, wo_mine.at[1], cols_half(wo_out, c, 512))
        start(copy(22, wo_mine.at[1], cols_half(wo_out, c, 512), sib))

        far = (jnp.where(c == 0, x, 1 - x), jnp.where(c == 0, 1 - y, y), c)
        for step, k in enumerate([c, 1 - c, 2]):
            reg = pk_piece(pk_out, c, jnp.where(k == 0, cj[0], jnp.where(k == 1, cj[1], cj[2])))
            copy(16 + k, reg, reg, sib).wait_recv()
            if step == 0:
                start(copy(18, reg, reg, far))
            start(copy(19 + k, reg, reg, sib))

        piece_finish(w_in_g, 3)
        piece_total(w_in_g)
        l_wi_out = local(10, wi_mine.at[1], rows_half(wi_out, c, 512))
        start(copy(23, wi_mine.at[1], rows_half(wi_out, c, 512), sib))

        reg = pk_piece(pk_out, 1 - c, j)
        copy(15, reg, reg, sib).wait_recv()
        for k in range(3):
            reg = pk_piece(pk_out, 1 - c, cj[k])
            copy(19 + k, reg, reg, sib).wait_recv()
        reg = cols_half(wo_out, 1 - c, 512)
        copy(22, reg, reg, sib).wait_recv()
        reg = rows_half(wi_out, 1 - c, 512)
        copy(23, reg, reg, sib).wait_recv()
        for cp in sends + w_in_g["p1"][2:] + w_out_g["p1"][2:]:
            cp.wait_send()
        for cp in (l_pk_out, l_wo_out, l_wi_out):
            cp.wait()

    return pl.pallas_call(
        body, name="grads_reduce",
        out_shape=(jax.ShapeDtypeStruct((D, wi_w), F32), jax.ShapeDtypeStruct((512, D), F32),
                   jax.ShapeDtypeStruct(pack.shape, F32)),
        in_specs=[ANY] * (5 + n_dz), out_specs=(ANY,) * 3,
        scratch_shapes=[
            pltpu.VMEM((lt, 512), BF16), pltpu.VMEM((lt, 512), BF16), pltpu.VMEM((lt, wi_w), BF16),
            pltpu.VMEM((2, 512, wi_w), BF16), pltpu.VMEM((2, 512, wi_w), F32), pltpu.VMEM((4, 512, wi_w), BF16),
            pltpu.VMEM((3, 512, wi_w), BF16), pltpu.VMEM((3, 512, wi_w), BF16),
            pltpu.VMEM((lx, 512), BF16), pltpu.VMEM((lx, 512), BF16), pltpu.VMEM((lx, 512), BF16),
            pltpu.VMEM((2, 512, 512), BF16), pltpu.VMEM((2, 512, 512), F32), pltpu.VMEM((4, 512, 512), BF16),
            pltpu.VMEM((3, 512, 512), BF16), pltpu.VMEM((3, 512, 512), BF16),
            pltpu.VMEM((hp, 512), F32), pltpu.VMEM((hp, 512), F32), pltpu.VMEM((hp, 512), BF16),
            pltpu.VMEM((3, hp, 128), BF16), pltpu.VMEM((hp, 128), F32),
            pltpu.SemaphoreType.DMA((27,)), pltpu.SemaphoreType.DMA((27,)), pltpu.SemaphoreType.DMA((15,))],
        compiler_params=_cp(vmem_mb=56),
    )(hn, *dzs, hn_c, y, do, pack)


def _ada_bwd(c_all, dmx_all_j, dmc_j, dmx_all, dmc, c_ctx, ada_w_full):
    def body(c_ref, dmxj_ref, dmcj_ref, dmx_ref, dmc_ref, cc_ref, w_ref, gw_ref, gb_ref, gc_ref, lhs, rhs, dm8):
        lhs[...] = jnp.zeros_like(lhs)
        rhs[...] = jnp.zeros_like(rhs)
        cv = c_ref[...]
        lhs[0:8, :] = cv * _sigmoid(cv)
        cc = cc_ref[...]
        a_c, da_c = _silu_and_grad(cc)
        lhs[8:9, :] = a_c
        rhs[0:8, :] = dmxj_ref[...]
        rhs[8:9, :] = dmcj_ref[...]
        gw_ref[...] = _dot_tn(lhs[...].astype(BF16), rhs[...].astype(BF16))
        gb_ref[...] = jnp.sum(dmx_ref[...], axis=0, keepdims=True) + dmc_ref[...]
        dm8[...] = jnp.zeros_like(dm8)
        dm8[0:1, :] = dmc_ref[...]
        da = _dot_nt(dm8[...].astype(BF16), w_ref[...])
        gc_ref[...] = da[0:1, :] * da_c

    return pl.pallas_call(
        body, name="ada_bwd",
        out_shape=(jax.ShapeDtypeStruct((D, 768), F32), jax.ShapeDtypeStruct((1, 3 * D), F32),
                   jax.ShapeDtypeStruct((1, D), F32)),
        in_specs=[VMEM] * 7, out_specs=(VMEM,) * 3,
        scratch_shapes=[pltpu.VMEM((16, D), F32), pltpu.VMEM((16, 768), F32), pltpu.VMEM((8, 3 * D), F32)],
        compiler_params=_cp(vmem_mb=32),
    )(c_all, dmx_all_j, dmc_j, dmx_all, dmc, c_ctx, ada_w_full)


def _proj(x, mods, mrow, norm_g, w_full, nk, name, sgu=None):
    lx = x.shape[0]
    n = lx // T
    order = [2, 3, 0, 1, 4] if sgu is not None else list(range(nk))

    def body(x_ref, sh_ref, sc_ref, ng_ref, *rest):
        w_refs, rest = rest[:nk], rest[nk:]
        if sgu is not None:
            g_ref, b_ref, sw_ref, bt_ref = rest[:4]
            rest = rest[4:]
        hn_ref, z_refs = rest[0], rest[1:1 + nk]
        xv = x_ref[...]
        r = lax.rsqrt(jnp.mean(xv * xv, axis=-1, keepdims=True) + NORM_EPS)
        hn = (xv * r) * ng_ref[...] * (1.0 + sc_ref[mrow:mrow + 1, :]) + sh_ref[mrow:mrow + 1, :]
        hb = hn.astype(BF16)
        hn_ref[...] = hb
        for k in order[:2]:
            z_refs[k][...] = _dot(hb, w_refs[k][...])
        if sgu is not None:
            ys_ref, mixed_s = rest[1 + nk], rest[2 + nk]
            for ch in range(T // HD):
                rows = slice(HD * ch, HD * ch + HD)
                ug = _sgu_parts(z_refs[2][rows, :], z_refs[3][rows, :], g_ref[...], b_ref[...], sw_ref, bt_ref,
                                mixed_s)[0]
                ys_ref[rows, :] = ug * mixed_s[...]
        for k in order[2:]:
            z_refs[k][...] = _dot(hb, w_refs[k][...])

    row = pl.BlockSpec((T, D), lambda i: (i, 0))
    vec = pl.BlockSpec((1, D), lambda i: (0, 0))
    in_specs = [row, pl.BlockSpec((8, D), lambda i: (0, 0)), pl.BlockSpec((8, D), lambda i: (0, 1)), vec]
    in_specs += [pl.BlockSpec((D, D), lambda i, k=k: (0, k)) for k in range(nk)]
    args = [x, mods, mods, norm_g] + [w_full] * nk
    out_shape = (jax.ShapeDtypeStruct((lx, D), BF16),) + tuple(jax.ShapeDtypeStruct((lx, D), F32) for _ in range(nk))
    scratch = []
    if sgu is not None:
        in_specs += [vec, vec, pl.BlockSpec((NH, HD, HD), lambda i: (0, 0, 0)), pl.BlockSpec((HD, NH), lambda i: (0, 0))]
        args += list(sgu)
        out_shape += (jax.ShapeDtypeStruct((lx, D), F32),)
        scratch = [pltpu.VMEM((HD, D), F32)]
    return pl.pallas_call(
        body, name=name, grid=(n,), out_shape=out_shape, in_specs=in_specs, out_specs=(row,) * len(out_shape),
        scratch_shapes=scratch, compiler_params=_cp(1, vmem_mb=56),
    )(*args)


def _tile_specs(rows_per_pos, width, n_tiles, tile):
    last = n_tiles * (T // 8) - 1
    r = rows_per_pos
    return [pl.BlockSpec((T * r, width), lambda i: (tile(i), 0)),
            pl.BlockSpec((8 * r, width), lambda i: (jnp.maximum(tile(i) * (T // 8) - 1, 0), 0)),
            pl.BlockSpec((8 * r, width), lambda i: (jnp.minimum((tile(i) + 1) * (T // 8), last), 0))]


def _has_prev(tile):
    return tile >= 2


def _has_next(tile, nt):
    return jnp.logical_and(tile >= 1, tile < nt - 1)


ZT = pl.BlockSpec((T * NH, HD), lambda i: (i, 0))
CONV_CHUNK = 32


SCAN_SUB = 4


def _scan_tile(chains, post, carry_ref):
    blk = T // SCAN_SUB

    def step(k, state):
        new = []
        for ci, (a_ref, x_ref, o_ref, q_ref, reverse) in enumerate(chains):
            for q in range(SCAN_SUB):
                s, p = state[ci * SCAN_SUB + q]
                t = (q + 1) * blk - 1 - k if reverse else q * blk + k
                r = pl.ds(_mo(t * NH, NH), NH)
                a = a_ref[r, :]
                if post:
                    o = x_ref[r, :] + s
                    o_ref[r, :] = o
                    q_ref[r, :] = p
                    new.append((a * o, a * p))
                else:
                    o = a * s + x_ref[r, :]
                    p = a * p
                    o_ref[r, :] = o
                    q_ref[r, :] = p
                    new.append((o, p))
        return tuple(new)

    zero = jnp.zeros((NH, HD), F32)
    one = jnp.ones((NH, HD), F32)
    final = lax.fori_loop(0, blk, step, tuple((zero, one) for _ in range(len(chains) * SCAN_SUB)), unroll=2)
    for ci, (a_ref, x_ref, o_ref, q_ref, reverse) in enumerate(chains):
        carry = carry_ref[ci]
        for q in (range(SCAN_SUB - 1, -1, -1) if reverse else range(SCAN_SUB)):
            rows = pl.ds(q * blk * NH, blk * NH)
            fixed = o_ref[rows, :].reshape(blk, NH, HD) + q_ref[rows, :].reshape(blk, NH, HD) * carry[None]
            o_ref[rows, :] = fixed.reshape(blk * NH, HD)
            s_loc, p_loc = final[ci * SCAN_SUB + q]
            carry = s_loc + p_loc * carry
        carry_ref[ci] = carry


def _lru_fwd(xa, conv_wz, conv_bz, wcat, bcat, lamcat, name):
    lx = xa.shape[0]
    n = lx // T
    tile_u = lambda i: i
    tile_d = lambda i: jnp.where(i == 0, 0, n - i)

    def body(xm_u, xp_u, xn_u, xm_d, xp_d, xn_d, cw, cb, w_ref, b_ref, lam_ref,
             xaz_o, xcz_o, af_o, ab_o, hf_o, hb_o, gf_o, gb_o, fu, fd, pad, xc_d, x_u, x_d, q_u, q_d, carry):
        i = pl.program_id(0)

        @pl.when(i == 0)
        def _():
            carry[...] = jnp.zeros_like(carry)

        def conv_gates(xm, xp, xn, tile, d, xc_ref, a_ref, x_ref, xaz_ref, g_ref):
            pmask = jnp.where(_has_prev(tile), 1.0, 0.0)
            nmask = jnp.where(_has_next(tile, n), 1.0, 0.0)
            for h in range(NH):
                cols = slice(HD * h, HD * h + HD)
                pad[_zrows(h, 8), :] = xp[:, cols] * pmask
                pad[pl.ds(8 * NH + h, T, stride=NH), :] = xm[:, cols]
                pad[pl.ds((T + 8) * NH + h, 8, stride=NH), :] = xn[:, cols] * nmask
            if xaz_ref is not None:
                xaz_ref[...] = pad[pl.ds(8 * NH, T * NH), :]

            def conv_chunk(ci, c_):
                base = pl.multiple_of(ci * (CONV_CHUNK * NH), CONV_CHUNK * NH)
                acc = None
                for k in range(4):
                    sl = pad[pl.ds(base + (7 + k) * NH, CONV_CHUNK * NH), :].reshape(CONV_CHUNK, NH, HD)
                    term = sl * cw[k][None]
                    acc = term if acc is None else acc + term
                acc = acc + cb[...][None]
                xc_ref[pl.ds(base, CONV_CHUNK * NH), :] = acc.reshape(CONV_CHUNK * NH, HD)
                return c_
            lax.fori_loop(0, T // CONV_CHUNK, conv_chunk, 0)

            for h in range(NH):
                xch = xc_ref[_zrows(h, T), :]
                pre = _dot(xch.astype(BF16), w_ref[h, :, 256 * d:256 * d + 256]) + b_ref[h:h + 1, 256 * d:256 * d + 256]
                r, gi, _, _, a, mult = _lru_gate(pre, lam_ref[h:h + 1, :], d, 0)
                a_ref[_zrows(h, T), :] = a
                x_ref[_zrows(h, T), :] = mult * gi * xch
                for q, val in enumerate((r, gi, mult)):
                    g_ref[:, q * D + HD * h:q * D + HD * h + HD] = val

        conv_gates(xm_u, xp_u, xn_u, tile_u(i), 0, xcz_o, af_o, x_u, xaz_o, gf_o)
        conv_gates(xm_d, xp_d, xn_d, tile_d(i), 1, xc_d, ab_o, x_d, None, gb_o)

        _scan_tile([(af_o, x_u, hf_o, q_u, False), (ab_o, x_d, hb_o, q_d, True)], False, carry)

        @pl.when(i == 0)
        def _():
            fu[...] = carry[0]
            fd[...] = carry[1]

    full = lambda shape: pl.BlockSpec(shape, lambda i: (0,) * len(shape))
    st = full((NH, HD))
    in_specs = _tile_specs(1, D, n, tile_u) + _tile_specs(1, D, n, tile_d)
    in_specs += [full((4, NH, HD)), st, full((NH, HD, 4 * HD)), full((NH, 4 * HD)), full((NH, 2 * HD))]
    up = pl.BlockSpec((T * NH, HD), lambda i: (tile_u(i), 0))
    dn = pl.BlockSpec((T * NH, HD), lambda i: (tile_d(i), 0))
    zs = jax.ShapeDtypeStruct((lx * NH, HD), F32)
    ss = jax.ShapeDtypeStruct((NH, HD), F32)
    zbuf = pltpu.VMEM((T * NH, HD), F32)
    gs = jax.ShapeDtypeStruct((lx, 3 * D), F32)
    g_up = pl.BlockSpec((T, 3 * D), lambda i: (tile_u(i), 0))
    g_dn = pl.BlockSpec((T, 3 * D), lambda i: (tile_d(i), 0))
    return pl.pallas_call(
        body, name=name, grid=(n,), out_shape=(zs,) * 6 + (gs, gs, ss, ss), in_specs=in_specs,
        out_specs=(up, up, up, dn, up, dn, g_up, g_dn, st, st),
        scratch_shapes=[pltpu.VMEM(((T + 16) * NH, HD), F32), zbuf, zbuf, zbuf, zbuf, zbuf,
                        pltpu.VMEM((2, NH, HD), F32)],
        compiler_params=_cp(1, vmem_mb=48),
    )(xa, xa, xa, xa, xa, xa, conv_wz, conv_bz, wcat, bcat, lamcat)


def _sgu_parts(u, v, lng, lnb, w_ref, bt_ref, mixed_s):
    ug, dug = _gelu_and_grad(u)
    vg, dvg = _gelu_and_grad(v)
    mu = jnp.mean(vg, axis=-1, keepdims=True)
    vc = vg - mu
    rstd = lax.rsqrt(jnp.mean(vc * vc, axis=-1, keepdims=True) + LN_EPS)
    vh = vc * rstd
    vn = (vh * lng + lnb).astype(BF16)
    for g in range(NH):
        cols = slice(HD * g, HD * g + HD)
        mixed_s[:, cols] = _dot(w_ref[g], vn[:, cols]) + bt_ref[:, g:g + 1]
    return ug, dug, dvg, rstd, vh, vn


def _sgu_bwd_chunk(u, v, dys_v, lng, lnb, w_ref, bt_ref, mixed_s, dvn_s, dw_ref, db_ref, dg_ref, dbl_ref):
    ug, dug, dvg, rstd, vh, vn = _sgu_parts(u, v, lng, lnb, w_ref, bt_ref, mixed_s)
    du = (dys_v * mixed_s[...] * dug).astype(BF16)
    dmix = dys_v * ug
    ones = jnp.ones((8, HD), BF16)
    for g in range(NH):
        cols = slice(HD * g, HD * g + HD)
        dm = dmix[:, cols]
        hi = dm.astype(BF16)
        lo = (dm - hi.astype(F32)).astype(BF16)
        dw_ref[g] += _dot_nt(hi, vn[:, cols])
        db_ref[g:g + 1, :] += (_dot_nt(ones, hi) + _dot_nt(ones, lo))[0:1, :]
        dvn_s[:, cols] = _dot_tn(w_ref[g], hi)
    dvn = dvn_s[...]
    dg_ref[...] += jnp.sum(dvn * vh, axis=0, keepdims=True)
    dbl_ref[...] += jnp.sum(dvn, axis=0, keepdims=True)
    dvh = dvn * lng
    dvg_in = rstd * (dvh - jnp.mean(dvh, axis=-1, keepdims=True) - vh * jnp.mean(dvh * vh, axis=-1, keepdims=True))
    return du, (dvg_in * dvg).astype(BF16)


def _out_fwd_bwd(hf_z, hb_z, ga, gb, ys, x, tgt, mods, final_g, w_out_full):
    lx = x.shape[0]
    n = lx // T

    def body(hf_ref, hb_ref, ga_ref, gb_ref, ys_ref, x_ref, t_ref, gx_ref, fg_ref, w_ref,
             loss_ref, dfg_ref, dgx_ref, dxn_ref, y_ref, do_ref, dga_ref, dgb_ref, dyl_ref, dys_ref, yl_s):
        i = pl.program_id(0)

        @pl.when(i == 0)
        def _():
            loss_ref[...] = jnp.zeros_like(loss_ref)
            dfg_ref[...] = jnp.zeros_like(dfg_ref)
            dgx_ref[...] = jnp.zeros_like(dgx_ref)

        for h in range(NH):
            yl_s[:, HD * h:HD * h + HD] = hf_ref[_zrows(h, T), :] + hb_ref[_zrows(h, T), :]
        yl = yl_s[...]
        gav = ga_ref[...]
        gbv = gb_ref[...]
        sa, dsa = _silu_and_grad(gav)
        sb, dsb = _silu_and_grad(gbv)
        ysv = ys_ref[...]
        y_ref[:, 0:D] = (yl * sa).astype(BF16)
        y_ref[:, D:2 * D] = (ysv * sb).astype(BF16)
        o = _dot(y_ref[...], w_ref[...])
        gx = gx_ref[0:1, :]
        xnew = x_ref[...] + gx * o
        r2 = lax.rsqrt(jnp.mean(xnew * xnew, axis=-1, keepdims=True) + NORM_EPS)
        xh = xnew * r2
        fg = fg_ref[...]
        err = xh * fg - t_ref[...]
        loss_ref[...] += 0.5 * jnp.sum(jnp.mean(err * err, axis=-1, keepdims=True), axis=0, keepdims=True)

        @pl.when(i == n - 1)
        def _():
            lp = loss_ref[...]
            lp1 = lp.astype(BF16).astype(F32)
            lp2 = (lp - lp1).astype(BF16).astype(F32)
            lp3 = (lp - lp1 - lp2).astype(BF16).astype(F32)
            lane = lax.broadcasted_iota(jnp.int32, lp.shape, 1)
            loss_ref[...] = jnp.where(lane == 0, lp1, jnp.where(lane == 1, lp2, jnp.where(lane == 2, lp3, 0.0)))
        dout = err * (1.0 / D)
        dfg_ref[...] += jnp.sum(dout * xh, axis=0, keepdims=True)
        dxh = dout * fg
        dxn = r2 * (dxh - xh * jnp.mean(dxh * xh, axis=-1, keepdims=True))
        dxn_ref[...] = dxn
        dgx_ref[...] += jnp.sum(dxn * o, axis=0, keepdims=True)
        do = (dxn * gx).astype(BF16)
        do_ref[...] = do
        dy = _dot_nt(do, w_ref[...])
        dy1 = dy[:, 0:D]
        dy2 = dy[:, D:2 * D]
        dga_ref[...] = (dy1 * yl * dsa).astype(BF16)
        dgb_ref[...] = (dy2 * ysv * dsb).astype(BF16)
        dys_ref[...] = dy2 * sb
        yl_s[...] = dy1 * sa
        for h in range(NH):
            dyl_ref[_zrows(h, T), :] = yl_s[:, HD * h:HD * h + HD]

    row = pl.BlockSpec((T, D), lambda i: (i, 0))
    vec = pl.BlockSpec((1, D), lambda i: (0, 0))
    zlat = pl.BlockSpec((T * NH, HD), lambda i: (i + 1, 0))
    in_specs = [zlat, zlat, row, row, row, row, row, pl.BlockSpec((8, D), lambda i: (0, 2)), vec,
                pl.BlockSpec((2 * D, D), lambda i: (0, 0))]
    out_shape = (jax.ShapeDtypeStruct((1, D), F32), jax.ShapeDtypeStruct((1, D), F32), jax.ShapeDtypeStruct((1, D), F32),
                 jax.ShapeDtypeStruct((lx, D), F32), jax.ShapeDtypeStruct((lx, 2 * D), BF16),
                 jax.ShapeDtypeStruct((lx, D), BF16), jax.ShapeDtypeStruct((lx, D), BF16),
                 jax.ShapeDtypeStruct((lx, D), BF16), jax.ShapeDtypeStruct((lx * NH, HD), F32),
                 jax.ShapeDtypeStruct((lx, D), F32))
    out_specs = (vec, vec, vec, row, pl.BlockSpec((T, 2 * D), lambda i: (i, 0)),
                 row, row, row, ZT, row)
    return pl.pallas_call(
        body, name="out_fwd_bwd", grid=(n,), out_shape=out_shape, in_specs=in_specs, out_specs=out_specs,
        scratch_shapes=[pltpu.VMEM((T, D), F32)],
        compiler_params=_cp(1, vmem_mb=56),
    )(hf_z, hb_z, ga, gb, ys, x, tgt, mods, final_g, w_out_full)


def _lru_bwd(xc_z, dy_z, hf_z, hb_z, af_z, ab_z, gf, gb, s_b, wcat, lamcat, name):
    lx = xc_z.shape[0] // NH
    n = lx // T
    tile_u = lambda i: jnp.where(i == n - 1, 0, i + 1)
    tile_d = lambda i: n - 1 - i

    def body(xc_u, dy_u, hb_ref, hbn_ref, ab_ref, gb_ref, xc_d, dy_d, hf_ref, hfp_ref, af_ref, gf_ref,
             sb_ref, w_ref, lam_ref, dxcb_ref, dxcf_ref, dw_ref, db_ref, dl_ref,
             lb_s, lf_s, q_u, q_d, pf_s, pb_s, dpre_s, dyu_s, dyd_s, carry):
        i = pl.program_id(0)
        tu, td = tile_u(i), tile_d(i)

        @pl.when(i == 0)
        def _():
            dw_ref[...] = jnp.zeros_like(dw_ref)
            db_ref[...] = jnp.zeros_like(db_ref)
            dl_ref[...] = jnp.zeros_like(dl_ref)
            carry[...] = jnp.zeros_like(carry)

        dyu_s[...] = dy_u[...] * jnp.where(tu == 0, 0.0, 1.0)
        dyd_s[...] = dy_d[...] * jnp.where(td == 0, 0.0, 1.0)
        _scan_tile([(ab_ref, dyu_s, lb_s, q_u, False), (af_ref, dyd_s, lf_s, q_d, True)], True, carry)
        zero = jnp.zeros((NH, HD), F32)
        pb_s[pl.ds(0, T * NH), :] = hb_ref[...]
        pb_s[pl.ds(T * NH, NH), :] = jnp.where(tu == n - 1, sb_ref[...], jnp.where(tu == 0, zero, hbn_ref[pl.ds(0, NH), :]))
        pf_s[pl.ds(0, NH), :] = jnp.where(td == 0, zero, hfp_ref[pl.ds(7 * NH, NH), :])
        pf_s[pl.ds(NH, T * NH), :] = hf_ref[...]
        sides = ((1, xc_u, lb_s, pb_s, NH, ab_ref, gb_ref, dxcb_ref), (0, xc_d, lf_s, pf_s, 0, af_ref, gf_ref, dxcf_ref))
        for d, xc_ref, adj_s, prev_s, prev_off, a_ref, g_ref, dxc_ref in sides:
            wcols = slice(256 * d, 256 * d + 256)
            for h in range(NH):
                xch = xc_ref[_zrows(h, T), :]
                xcb = xch.astype(BF16)
                r, gi, mult = (g_ref[:, q * D + HD * h:q * D + HD * h + HD] for q in range(3))
                a = a_ref[_zrows(h, T), :]
                lam = lam_ref[h:h + 1, HD * d:HD * d + HD]
                sp = _softplus(-lam)
                du = adj_s[_zrows(h, T), :]
                da = du * prev_s[pl.ds(prev_off + h, T, stride=NH), :]
                dgi = du * mult * xch
                dmult = du * gi * xch
                dla = da * a - dmult * (a * a) / mult
                dr = dla * ((-LRU_C) * sp)
                dsp = jnp.sum(dla * ((-LRU_C) * r), axis=0, keepdims=True)
                dl_ref[h:h + 1, HD * d:HD * d + HD] += dsp * (-_sigmoid(-lam))
                dpre_s[:, 0:HD] = dr * r * (1.0 - r)
                dpre_s[:, HD:2 * HD] = dgi * gi * (1.0 - gi)
                dpre = dpre_s[...]
                dpb = dpre.astype(BF16)
                dw_ref[h, :, wcols] += _dot_tn(xcb, dpb)
                db_ref[h:h + 1, wcols] += jnp.sum(dpre, axis=0, keepdims=True)
                dxc_ref[_zrows(h, T), :] = du * mult * gi + _dot_nt(dpb, w_ref[h, :, wcols])

    full = lambda shape: pl.BlockSpec(shape, lambda i: (0,) * len(shape))
    wsp, bsp, lsp = full((NH, HD, 4 * HD)), full((NH, 4 * HD)), full((NH, 2 * HD))
    st = full((NH, HD))
    up = pl.BlockSpec((T * NH, HD), lambda i: (tile_u(i), 0))
    dn = pl.BlockSpec((T * NH, HD), lambda i: (tile_d(i), 0))
    dy_up = pl.BlockSpec((T * NH, HD), lambda i: (jnp.maximum(tile_u(i) - 1, 0), 0))
    dy_dn = pl.BlockSpec((T * NH, HD), lambda i: (jnp.maximum(tile_d(i) - 1, 0), 0))
    nxt = _tile_specs(NH, HD, n, tile_u)[2]
    prv = _tile_specs(NH, HD, n, tile_d)[1]
    g_up = pl.BlockSpec((T, 3 * D), lambda i: (tile_u(i), 0))
    g_dn = pl.BlockSpec((T, 3 * D), lambda i: (tile_d(i), 0))
    zs = jax.ShapeDtypeStruct((lx * NH, HD), F32)
    zbuf = pltpu.VMEM((T * NH, HD), F32)
    zbuf1 = pltpu.VMEM(((T + 1) * NH, HD), F32)
    return pl.pallas_call(
        body, name=name, grid=(n,),
        out_shape=(zs, zs, jax.ShapeDtypeStruct((NH, HD, 4 * HD), F32), jax.ShapeDtypeStruct((NH, 4 * HD), F32),
                   jax.ShapeDtypeStruct((NH, 2 * HD), F32)),
        in_specs=[up, dy_up, up, nxt, up, g_up, dn, dy_dn, dn, prv, dn, g_dn, st, wsp, lsp],
        out_specs=(up, dn, wsp, bsp, lsp),
        scratch_shapes=[zbuf, zbuf, zbuf, zbuf, zbuf1, zbuf1, pltpu.VMEM((T, 2 * HD), F32), zbuf, zbuf,
                        pltpu.VMEM((2, NH, HD), F32)],
        compiler_params=_cp(1, vmem_mb=56),
    )(xc_z, dy_z, hb_z, hb_z, ab_z, gb, xc_z, dy_z, hf_z, hf_z, af_z, gf, s_b, wcat, lamcat)


def _conv_bwd(dxc_a, dxc_b, xa_z, conv_wz, dcw0, dcb0, name):
    lx = dxc_a.shape[0] // NH
    n = lx // T

    def body(dm_a, dp_a, dn_a, dm_b, dp_b, dn_b, xa_ref, cw, dcw0_ref, dcb0_ref, dxa_ref, dcw_ref, dcb_ref, pad, dxa_s):
        i = pl.program_id(0)

        @pl.when(i == 0)
        def _():
            dcw_ref[...] = dcw0_ref[...]
            dcb_ref[...] = dcb0_ref[...]

        pmask = jnp.where(_has_prev(i), 1.0, 0.0)
        nmask = jnp.where(_has_next(i, n), 1.0, 0.0)
        pad[pl.ds(0, 8 * NH), :] = (dp_a[...] + dp_b[...]) * pmask
        pad[pl.ds(8 * NH, T * NH), :] = dm_a[...] + dm_b[...]
        pad[pl.ds((T + 8) * NH, 8 * NH), :] = (dn_a[...] + dn_b[...]) * nmask

        def chunk(ci, carry):
            base = pl.multiple_of(ci * (CONV_CHUNK * NH), CONV_CHUNK * NH)
            xav = xa_ref[pl.ds(base, CONV_CHUNK * NH), :].reshape(CONV_CHUNK, NH, HD)
            acc = None
            for k in range(4):
                sl = pad[pl.ds(base + (9 - k) * NH, CONV_CHUNK * NH), :].reshape(CONV_CHUNK, NH, HD)
                term = sl * cw[k][None]
                acc = term if acc is None else acc + term
                dcw_ref[k] += jnp.sum(sl * xav, axis=0)
                if k == 1:
                    dcb_ref[...] += jnp.sum(sl, axis=0)
            dxa_s[pl.ds(base, CONV_CHUNK * NH), :] = acc.reshape(CONV_CHUNK * NH, HD)
            return carry
        lax.fori_loop(0, T // CONV_CHUNK, chunk, 0)
        for h in range(NH):
            dxa_ref[:, HD * h:HD * h + HD] = dxa_s[_zrows(h, T), :].astype(BF16)

    full = lambda shape: pl.BlockSpec(shape, lambda i: (0,) * len(shape))
    return pl.pallas_call(
        body, name=name, grid=(n,),
        out_shape=(jax.ShapeDtypeStruct((lx, D), BF16), jax.ShapeDtypeStruct((4, NH, HD), F32),
                   jax.ShapeDtypeStruct((NH, HD), F32)),
        in_specs=_tile_specs(NH, HD, n, lambda i: i) * 2 + [ZT, full((4, NH, HD)), full((4, NH, HD)), full((NH, HD))],
        out_specs=(pl.BlockSpec((T, D), lambda i: (i, 0)), full((4, NH, HD)), full((NH, HD))),
        scratch_shapes=[pltpu.VMEM(((T + 16) * NH, HD), F32), pltpu.VMEM((T * NH, HD), F32)],
        compiler_params=_cp(1, vmem_mb=48),
    )(dxc_a, dxc_a, dxc_a, dxc_b, dxc_b, dxc_b, xa_z, conv_wz, dcw0, dcb0)


def _proj_bwd(dzs, x, dxn, mods, mrow, norm_g, w_full, dng0, name, sgu=None, dz0_off=0):
    lx = x.shape[0]
    n = lx // T
    nz = len(dzs)
    has_x = dxn is not None
    wks = ([0, 1, 4, 2, 3] if sgu is not None else list(range(nz)))

    def body(*refs):
        it = iter(refs)
        take = lambda m: [next(it) for _ in range(m)]
        dz_refs, w_refs = take(nz), take(len(wks))
        x_ref, sc_ref, ng_ref, dng0_ref = take(4)
        dxn_ref = take(1)[0] if has_x else None
        if sgu is not None:
            u_ref, v_ref, dy_ref, g_ref, b_ref, sw_ref, bt_ref = take(7)
        gx_ref = take(1)[0] if has_x else None
        dng_ref, dsc_ref, dsh_ref = take(3)
        if sgu is not None:
            du_ref, dv_ref, dws_ref, dbs_ref, dlg_ref, dlb_ref, mixed_s, dvn_s = take(8)
        i = pl.program_id(0)

        @pl.when(i == 0)
        def _():
            dng_ref[...] = dng0_ref[...]
            dsc_ref[...] = jnp.zeros_like(dsc_ref)
            dsh_ref[...] = jnp.zeros_like(dsh_ref)
            if sgu is not None:
                for acc in (dws_ref, dbs_ref, dlg_ref, dlb_ref):
                    acc[...] = jnp.zeros_like(acc)

        dhn = _dot_nt(dz_refs[0][...], w_refs[0][...])
        for k in range(1, nz):
            dhn = dhn + _dot_nt(dz_refs[k][...], w_refs[k][...])
        if sgu is not None:
            for ch in range(T // HD):
                rows = slice(HD * ch, HD * ch + HD)
                du, dv = _sgu_bwd_chunk(u_ref[rows, :], v_ref[rows, :], dy_ref[rows, :], g_ref[...], b_ref[...],
                                        sw_ref, bt_ref, mixed_s, dvn_s, dws_ref, dbs_ref, dlg_ref, dlb_ref)
                du_ref[rows, :] = du
                dv_ref[rows, :] = dv
            dhn = dhn + _dot_nt(du_ref[...], w_refs[nz][...]) + _dot_nt(dv_ref[...], w_refs[nz + 1][...])
        xv = x_ref[...]
        r = lax.rsqrt(jnp.mean(xv * xv, axis=-1, keepdims=True) + NORM_EPS)
        xn = xv * r
        ng = ng_ref[...]
        sc1 = 1.0 + sc_ref[mrow:mrow + 1, :]
        t = dhn * xn
        dng_ref[...] += jnp.sum(t * sc1, axis=0, keepdims=True)
        dsc_ref[...] += jnp.sum(t * ng, axis=0, keepdims=True)
        dsh_ref[...] += jnp.sum(dhn, axis=0, keepdims=True)
        if has_x:
            dxh = dhn * (ng * sc1)
            gx_ref[...] = dxn_ref[...] + r * (dxh - xn * jnp.mean(dxh * xn, axis=-1, keepdims=True))

    row = pl.BlockSpec((T, D), lambda i: (i, 0))
    vec = pl.BlockSpec((1, D), lambda i: (0, 0))
    in_specs = [pl.BlockSpec((T, D), lambda i: (i + dz0_off, 0))] + [row] * (nz - 1)
    in_specs += [pl.BlockSpec((D, D), lambda i, k=k: (0, k)) for k in wks]
    in_specs += [row, pl.BlockSpec((8, D), lambda i: (0, 1)), vec, vec]
    args = list(dzs) + [w_full] * len(wks) + [x, mods, norm_g, dng0]
    vs = jax.ShapeDtypeStruct((1, D), F32)
    out_shape, out_specs = (vs, vs, vs), (vec, vec, vec)
    scratch = []
    if has_x:
        in_specs.append(row)
        args.append(dxn)
        out_shape = (jax.ShapeDtypeStruct((lx, D), F32),) + out_shape
        out_specs = (row,) + out_specs
    if sgu is not None:
        wsp = pl.BlockSpec((NH, HD, HD), lambda i: (0, 0, 0))
        bsp = pl.BlockSpec((NH, HD), lambda i: (0, 0))
        in_specs += [row, row, row, vec, vec, wsp, pl.BlockSpec((HD, NH), lambda i: (0, 0))]
        args += list(sgu)
        zb = jax.ShapeDtypeStruct((lx, D), BF16)
        out_shape += (zb, zb, jax.ShapeDtypeStruct((NH, HD, HD), F32), jax.ShapeDtypeStruct((NH, HD), F32), vs, vs)
        out_specs += (row, row, wsp, bsp, vec, vec)
        scratch = [pltpu.VMEM((HD, D), F32), pltpu.VMEM((HD, D), F32)]
    return pl.pallas_call(
        body, name=name, grid=(n,), out_shape=out_shape, in_specs=in_specs, out_specs=out_specs,
        scratch_shapes=scratch, compiler_params=_cp(1, vmem_mb=56),
    )(*args)


def _adam_math(w, g, m, v):
    m = ADAM_B1 * m + (1.0 - ADAM_B1) * g
    v = ADAM_B2 * v + (1.0 - ADAM_B2) * (g * g)
    m_hat = m / (1.0 - ADAM_B1 ** ADAM_STEP)
    v_hat = v / (1.0 - ADAM_B2 ** ADAM_STEP)
    delta = -ADAM_LR * (m_hat / (jnp.sqrt(v_hat) + ADAM_EPS) + ADAM_WD * w)
    return delta, m, v


def _adam_big(w, g, m, v, name):
    rows, cols = w.shape
    tr = 256

    def body(w_ref, g_ref, m_ref, v_ref, d_o, m_o, v_o):
        d, mm, vv = _adam_math(w_ref[...], g_ref[...], m_ref[...], v_ref[...])
        d_o[...] = d
        m_o[...] = mm
        v_o[...] = vv

    blk = pl.BlockSpec((tr, cols), lambda i: (i, 0))
    s = jax.ShapeDtypeStruct((rows, cols), F32)
    return pl.pallas_call(
        body, name=name, grid=(rows // tr,), out_shape=(s, s, s), in_specs=[blk] * 4, out_specs=(blk,) * 3,
        compiler_params=_cp(1, vmem_mb=48),
    )(w, g, m, v)


def _adam_small(items, tot):
    ni = len(items)
    pieces = [it[1] if isinstance(it[1], list) else None for it in items]
    flat = [a for it, pc in zip(items, pieces) for a in ((it[0], it[2], it[3]) if pc is not None else it)]
    n_in = len(flat) + 1
    out_shape = tuple(jax.ShapeDtypeStruct(it[0].shape, F32) for it, pc in zip(items, pieces)
                      for _ in range(4 if pc is not None else 3))
    n_out = len(out_shape)
    n_loads = sum(3 + (len(pc) if pc is not None else 1) for pc in pieces)

    def body(*refs):
        ins, tot_ref, outs = refs[:n_in - 1], refs[n_in - 1], refs[n_in:n_in + n_out]
        bufs = refs[n_in + n_out:n_in + n_out + 7 * ni]
        sem_in, sem_out = refs[n_in + n_out + 7 * ni:]
        loads, q_in, q_sem = [], 0, 0
        for k, pc in enumerate(pieces):
            w_b, g_b, m_b, v_b = bufs[7 * k:7 * k + 4]
            srcs = [(ins[q_in], w_b)]
            if pc is None:
                srcs.append((ins[q_in + 1], g_b))
                q_in += 1
            else:
                srcs += [(tot_ref.at[pl.ds(r0, nr), pl.ds(c0, nc)], g_b.at[pl.ds(d0, nr), :]) for r0, nr, c0, nc, d0 in pc]
            srcs += [(ins[q_in + 1], m_b), (ins[q_in + 2], v_b)]
            q_in += 3
            mine = []
            for src, dst in srcs:
                mine.append(pltpu.make_async_copy(src, dst, sem_in.at[q_sem]))
                q_sem += 1
            loads.append(mine)
        for mine in loads:
            for cp in mine:
                cp.start()
        stores, q_out = [], 0
        for k, pc in enumerate(pieces):
            for cp in loads[k]:
                cp.wait()
            w_b, g_b, m_b, v_b = bufs[7 * k:7 * k + 4]
            res = _adam_math(w_b[...], g_b[...], m_b[...], v_b[...])
            srcs = []
            for q in range(3):
                bufs[7 * k + 4 + q][...] = res[q]
                srcs.append(bufs[7 * k + 4 + q])
            if pc is not None:
                srcs.append(g_b)
            for src in srcs:
                cp = pltpu.make_async_copy(src, outs[q_out], sem_out.at[q_out])
                cp.start()
                stores.append(cp)
                q_out += 1
        for cp in stores:
            cp.wait()

    scratch = [pltpu.VMEM(it[0].shape, F32) for it in items for _ in range(7)]
    scratch += [pltpu.SemaphoreType.DMA((n_loads,)), pltpu.SemaphoreType.DMA((n_out,))]
    res = pl.pallas_call(
        body, name="adam_small", out_shape=out_shape, in_specs=[HBM] * n_in, out_specs=(HBM,) * n_out,
        scratch_shapes=scratch, compiler_params=_cp(vmem_mb=40),
    )(*flat, tot)
    outs, q = [], 0
    for pc in pieces:
        outs.append(tuple(res[q:q + 3]) + ((res[q + 3],) if pc is not None else (None,)))
        q += 4 if pc is not None else 3
    return outs


def kernel(x, c, ctx, c_ctx, ada_w, ada_b, norm_g, w_in, conv_w, conv_b, lru_wa, lru_ba, lru_wx, lru_bx, lru_lambda, sgu_ln_g, sgu_ln_b, sgu_w, sgu_b, w_out, final_g, loss_target, m_c_ctx, m_ada_w, m_ada_b, m_norm_g, m_w_in, m_conv_w, m_conv_b, m_lru_wa, m_lru_ba, m_lru_wx, m_lru_bx, m_lru_lambda, m_sgu_ln_g, m_sgu_ln_b, m_sgu_w, m_sgu_b, m_w_out, m_final_g, v_c_ctx, v_ada_w, v_ada_b, v_norm_g, v_w_in, v_conv_w, v_conv_b, v_lru_wa, v_lru_ba, v_lru_wx, v_lru_bx, v_lru_lambda, v_sgu_ln_g, v_sgu_ln_b, v_sgu_w, v_sgu_b, v_w_out, v_final_g):
    ix, iy, ic = lax.axis_index("x"), lax.axis_index("y"), lax.axis_index("c")
    chip = 2 * ix + iy
    dev = 2 * chip + ic
    lx = x.shape[1]
    lc = ctx.shape[1]

    smalls = jnp.concatenate([conv_w[0], lru_lambda[0], jnp.zeros((10, 256), F32)], axis=0)
    c_ctx2 = c_ctx.reshape(1, D)
    ada_b_j = lax.dynamic_slice(ada_b, (0, 768 * chip), (1, 768))
    mods, c_slots, sm_all, w_in_full, wo_land, ada_land = _gather_in(c, c_ctx2, ada_w[0], ada_b_j, w_in[0], w_out[0],
                                                                     smalls)
    wo_ss, wo_rs, ada_ss, ada_rs, wo_land, ada_land, token = _late_gather_start(wo_land, ada_land)
    mods = mods + token[0:1, 0:1]
    sm3 = sm_all.reshape(NCHIP, 16, 256)
    conv_w_full = sm3[:, 0:4, :].transpose(1, 0, 2).reshape(4, D)
    lam_full = sm3[:, 4:6, :].transpose(1, 0, 2).reshape(2, D)
    conv_wz = conv_w_full.reshape(4, NH, HD)
    conv_bz = conv_b.reshape(NH, HD)
    lamcat = lam_full.reshape(2, NH, HD).transpose(1, 0, 2).reshape(NH, 2 * HD)
    wa, wx, ba, bx = lru_wa[0], lru_wx[0], lru_ba[0], lru_bx[0]
    wcat = jnp.concatenate([wa[0], wx[0], wa[1], wx[1]], axis=-1).astype(BF16)
    bcat = jnp.concatenate([ba[0], bx[0], ba[1], bx[1]], axis=-1)
    sgu_wb = sgu_w[0].astype(BF16)
    sgu_bt = sgu_b[0].T
    final_g2 = final_g.reshape(1, D)

    zero_s = jnp.zeros((NH, HD), F32)
    assert lc == T
    hn_c, xa_c = _proj(ctx[0], mods, 1, norm_g, w_in_full, 1, "proj_ctx")
    hn, xa, ga, u, v, gb, ys = _proj(x[0], mods, 0, norm_g, w_in_full, 5, "proj",
                                     (sgu_ln_g, sgu_ln_b, sgu_wb, sgu_bt))
    xa_all = jnp.concatenate([xa_c, xa], axis=0)
    xaz, xcz, af, ab, hf, hb, gf, gb_l, _, hb0 = _lru_fwd(xa_all, conv_wz, conv_bz, wcat, bcat, lamcat, "lru_fwd")

    w_out_full = _late_gather_wait(wo_land, wo_ss, wo_rs, "w_out", hf, "late_gather_wait_w_out")
    (loss_part, dfg, dgx, dxn, y, do, dga, dgb, dyl_z, dys) = _out_fwd_bwd(
        hf, hb, ga, gb, ys, x[0], loss_target[0], mods, final_g2, w_out_full)

    dxc_b, dxc_f, dwc, dbc, dlc = _lru_bwd(xcz, dyl_z, hf, hb, af, ab, gf, gb_l, hb0, wcat, lamcat, "lru_bwd")
    dxa, dcw, dcb = _conv_bwd(dxc_b, dxc_f, xaz, conv_wz, jnp.zeros((4, NH, HD), F32), zero_s, "conv_bwd")

    grad_x, dng, dsc_x, dsh_x, du, dv, d_sgu_w, d_sgu_b, d_ln_g, d_ln_b = _proj_bwd(
        [dxa, dga, dgb], x[0], dxn, mods, 0, norm_g, w_in_full, jnp.zeros((1, D), F32), "proj_bwd",
        (u, v, dys, sgu_ln_g, sgu_ln_b, sgu_wb, sgu_bt), dz0_off=1)
    dzs = [dxa, dga, du, dv, dgb]
    dng, dsc_c, dsh_c = _proj_bwd([dxa], ctx[0], None, mods, 1, norm_g, w_in_full, dng, "proj_bwd_ctx")

    dmx = jnp.concatenate([dsh_x, dsc_x, dgx], axis=0)
    dmc = jnp.concatenate([dsh_c, dsc_c, jnp.zeros((1, D), F32)], axis=0)
    slot = jnp.concatenate([dmx, loss_part], axis=0)
    slots = lax.dynamic_update_slice(jnp.zeros((32, D), F32), slot, (4 * dev, 0))
    vecs = jnp.concatenate([dfg, dng, dcb.reshape(1, D), d_ln_g, d_ln_b, dcw.reshape(4, D), dmc,
                            jnp.zeros((4, D), F32), slots], axis=0)
    d_sgu_w4 = d_sgu_w.reshape(4, 256, HD).transpose(1, 0, 2).reshape(256, 4 * HD)
    pad8 = lambda a: jnp.pad(a, ((0, 8 - a.shape[0]), (0, 4 * HD - a.shape[1])))
    pack = jnp.concatenate([dwc.reshape(NH * HD, 4 * HD), pad8(dbc), pad8(dlc), d_sgu_w4, pad8(d_sgu_b),
                            vecs.reshape(96, 4 * HD), jnp.zeros((8, 4 * HD), F32)], axis=0)
    g_w_in, g_w_out, tot = _grads_reduce(hn, dzs, hn_c, y, do, pack)

    n_w = NH * HD
    g_lru_wa = [(0, n_w, 2 * HD * d, HD, n_w * d) for d in range(2)]
    g_lru_wx = [(0, n_w, 2 * HD * d + HD, HD, n_w * d) for d in range(2)]
    g_lru_ba = [(n_w, NH, 2 * HD * d, HD, NH * d) for d in range(2)]
    g_lru_bx = [(n_w, NH, 2 * HD * d + HD, HD, NH * d) for d in range(2)]
    g_sgu_w = [(1040, 256, HD * q, HD, 256 * q) for q in range(4)]
    g_sgu_b = [(1296, NH, 0, HD, 0)]
    g_lc = tot[1032:1040, 0:2 * HD]
    tv = tot[1304:1400].reshape(48, D)
    g_final_g, g_norm_g, g_conv_b, g_ln_g, g_ln_b = tv[0:1], tv[1:2], tv[2:3], tv[3:4], tv[4:5]
    g_conv_w_full = tv[5:9]
    dmc_tot = tv[9:12].reshape(1, 3 * D)
    slots_all = tv[16:48].reshape(8, 4, D)
    dmx_all = slots_all[:, 0:3, :].reshape(8, 3 * D)
    c_all = c_slots.reshape(8, 8, D)[:, 0, :]
    g_lam_full = jnp.stack([g_lc[:, 0:HD], g_lc[:, HD:2 * HD]]).reshape(2, D)
    g_conv_w = lax.dynamic_slice(g_conv_w_full, (0, 256 * chip), (4, 256))
    g_lam = lax.dynamic_slice(g_lam_full, (0, 256 * chip), (2, 256))
    dmx_all_j = lax.dynamic_slice(dmx_all, (0, 768 * chip), (8, 768))
    dmc_j = lax.dynamic_slice(dmc_tot, (0, 768 * chip), (1, 768))
    ada_full = _late_gather_wait(ada_land, ada_ss, ada_rs, "ada_w", tot, "late_gather_wait_ada_w")
    g_ada_w, g_ada_b, g_c_ctx = _ada_bwd(c_all, dmx_all_j, dmc_j, dmx_all, dmc_tot, c_ctx2, ada_full)

    big = {
        "ada_w": _adam_big(ada_w[0], g_ada_w, m_ada_w[0], v_ada_w[0], "adam_ada_w"),
        "w_in": _adam_big(w_in[0], g_w_in, m_w_in[0], v_w_in[0], "adam_w_in"),
        "w_out": _adam_big(w_out[0], g_w_out, m_w_out[0], v_w_out[0], "adam_w_out"),
    }
    small_in = {
        "c_ctx": (c_ctx, g_c_ctx, m_c_ctx, v_c_ctx, (1, D)),
        "ada_b": (ada_b, g_ada_b, m_ada_b, v_ada_b, (1, 3 * D)),
        "norm_g": (norm_g, g_norm_g, m_norm_g, v_norm_g, (1, D)),
        "conv_w": (conv_w, g_conv_w, m_conv_w, v_conv_w, (4, 256)),
        "conv_b": (conv_b, g_conv_b, m_conv_b, v_conv_b, (1, D)),
        "lru_wa": (lru_wa, g_lru_wa, m_lru_wa, v_lru_wa, (2 * NH * HD, HD)),
        "lru_ba": (lru_ba, g_lru_ba, m_lru_ba, v_lru_ba, (2 * NH, HD)),
        "lru_wx": (lru_wx, g_lru_wx, m_lru_wx, v_lru_wx, (2 * NH * HD, HD)),
        "lru_bx": (lru_bx, g_lru_bx, m_lru_bx, v_lru_bx, (2 * NH, HD)),
        "lru_lambda": (lru_lambda, g_lam, m_lru_lambda, v_lru_lambda, (2, 256)),
        "sgu_ln_g": (sgu_ln_g, g_ln_g, m_sgu_ln_g, v_sgu_ln_g, (1, D)),
        "sgu_ln_b": (sgu_ln_b, g_ln_b, m_sgu_ln_b, v_sgu_ln_b, (1, D)),
        "sgu_w": (sgu_w, g_sgu_w, m_sgu_w, v_sgu_w, (NH * HD, HD)),
        "sgu_b": (sgu_b, g_sgu_b, m_sgu_b, v_sgu_b, (NH, HD)),
        "final_g": (final_g, g_final_g, m_final_g, v_final_g, (1, D)),
    }
    names_small = list(small_in)
    res_small = _adam_small([tuple(a if isinstance(a, list) else a.reshape(small_in[k][4]) for a in small_in[k][:4])
                             for k in names_small], tot)
    full_shapes = {"ada_w": ada_w.shape, "w_in": w_in.shape, "w_out": w_out.shape}
    grads, deltas, new_m, new_v = {}, {}, {}, {}
    for k in ("ada_w", "w_in", "w_out"):
        g = {"ada_w": g_ada_w, "w_in": g_w_in, "w_out": g_w_out}[k]
        grads[k] = g.reshape(full_shapes[k])
        deltas[k], new_m[k], new_v[k] = (a.reshape(full_shapes[k]) for a in big[k])
    for k, res in zip(names_small, res_small):
        shape = small_in[k][0].shape
        grads[k] = (small_in[k][1] if res[3] is None else res[3]).reshape(shape)
        deltas[k], new_m[k], new_v[k] = (a.reshape(shape) for a in res[:3])

    loss = jnp.sum(slots_all[:, 3, 0:3])
    order = ["c_ctx", "ada_w", "ada_b", "norm_g", "w_in", "conv_w", "conv_b", "lru_wa", "lru_ba", "lru_wx", "lru_bx",
             "lru_lambda", "sgu_ln_g", "sgu_ln_b", "sgu_w", "sgu_b", "w_out", "final_g"]
    return (loss, grad_x.reshape(x.shape), *[grads[k] for k in order], *[deltas[k] for k in order],
            *[new_m[k] for k in order], *[new_v[k] for k in order])
```

```python
import functools

import jax
import jax.numpy as jnp
from jax import lax
from jax.experimental import pallas as pl
from jax.experimental.pallas import tpu as pltpu

F32 = jnp.float32
BF16 = jnp.bfloat16

D = 1024
NH = 8
HD = 128
NCHIP = 4
T = 256
NORM_EPS = 1e-6
LN_EPS = 1e-5
LRU_C = 8.0
ADAM_LR = 0.001
ADAM_B1 = 0.9
ADAM_B2 = 0.999
ADAM_EPS = 1e-08
ADAM_WD = 0.01
ADAM_STEP = 10

VMEM = pl.BlockSpec(memory_space=pltpu.VMEM)
ANY = pl.BlockSpec(memory_space=pl.ANY)
MESH = pl.DeviceIdType.MESH


def _cp(n_grid=0, vmem_mb=None):
    kw = {}
    if n_grid:
        kw["dimension_semantics"] = ("arbitrary",) * n_grid
    if vmem_mb:
        kw["vmem_limit_bytes"] = vmem_mb << 20
    return pltpu.CompilerParams(**kw)


def _sigmoid(x):
    return 0.5 * jnp.tanh(0.5 * x) + 0.5


def _silu_and_grad(x):
    s = _sigmoid(x)
    return x * s, s * (1.0 + x * (1.0 - s))


_GELU_K = 0.7978845608028654
_GELU_C = 0.044715


def _gelu_and_grad(x):
    x2 = x * x
    th = jnp.tanh(x * (_GELU_K + (_GELU_K * _GELU_C) * x2))
    p = 0.5 + 0.5 * th
    g = x * p
    dg = p + g * (1.0 - th) * (_GELU_K + (3.0 * _GELU_K * _GELU_C) * x2)
    return g, dg


def _softplus(x):
    return jnp.maximum(x, 0.0) + jnp.log1p(jnp.exp(-jnp.abs(x)))


def _lru_gate(pre, lam_row, d, off=None):
    off = 256 * d if off is None else off
    r = _sigmoid(pre[:, off:off + HD])
    gi = _sigmoid(pre[:, off + HD:off + 2 * HD])
    lam = lam_row[:, HD * d:HD * d + HD]
    sp = _softplus(-lam)
    la = (-LRU_C) * r * sp
    a = jnp.exp(la)
    x2 = 2.0 * la
    m2 = jnp.where(x2 > -1e-3, -x2 * (1.0 + 0.5 * x2), 1.0 - a * a)
    mult = jnp.sqrt(m2)
    return r, gi, lam, sp, a, mult


def _dot(a, b):
    return jnp.dot(a, b, preferred_element_type=F32)


def _dot_tn(a, b):
    return lax.dot_general(a, b, (((0,), (0,)), ((), ())), preferred_element_type=F32)


def _dot_nt(a, b):
    return lax.dot_general(a, b, (((1,), (1,)), ((), ())), preferred_element_type=F32)


def _mo(v, m):
    return v if isinstance(v, int) else pl.multiple_of(v, m)


def _zrows(h, n):
    return pl.ds(h, n, stride=NH)


def _gather_in(c, c_ctx, ada_w, ada_b_j, w_in, w_out, smalls):
    nch = [1, 4]
    wrows = lambda cc, q: (pl.ds(_mo(512 * cc, 16), 512) if q is None
                           else pl.ds(_mo(512 * cc + (512 // nch[1]) * q, 16), 512 // nch[1]))
    specs = [
        ((64, 256), F32, lambda r, jj, cc, q=None: r.at[pl.ds(_mo(16 * jj + 8 * cc, 8), 8), :]),
        ((D, 5120), BF16, lambda r, jj, cc, q=None: r.at[wrows(cc, q), pl.ds(_mo(1280 * jj, 128), 1280)]),
    ]
    halves = [lambda r, cc, q=None: r.at[pl.ds(_mo(8 * cc, 8), 8), :],
              lambda r, cc, q=None: r.at[wrows(cc, q), :]]
    na = len(specs)
    sem_base = [0, 6 * nch[0]]
    sidx = lambda a, q, k: sem_base[a] + 6 * q + k
    n_tiny = 6 * sum(nch)
    n_sem = n_tiny + 10

    def body(c_ref, cc_ref, ada_ref, adab_ref, win_ref, wout_ref, sm_ref,
             mods_o, call_o, sm_o, win_o, wol_o, adal_o, s_win, s_ada, s_wout, f_win, f_ada, f_wout, cslot, lhs, mbuf,
             send_sems, recv_sems, local_sems, load_sems):
        x, y, c = lax.axis_index("x"), lax.axis_index("y"), lax.axis_index("c")
        j = 2 * x + y
        dev = 2 * j + c
        sib = (x, y, 1 - c)
        chips = [(1 - x, y), (x, 1 - y), (1 - x, 1 - y)]
        cj = [2 * cx + cy for cx, cy in chips]
        outs = [sm_o, win_o]
        srcs = [sm_ref, s_win]

        def copy(idx, src, dst, to):
            return pltpu.make_async_remote_copy(src_ref=src, dst_ref=dst, send_sem=send_sems.at[idx],
                                                recv_sem=recv_sems.at[idx], device_id=to, device_id_type=MESH)

        sends = []

        def start(cp):
            cp.start()
            sends.append(cp)

        cslot[...] = jnp.zeros_like(cslot)
        cslot[0:1, :] = c_ref[...]
        my_slot = pl.ds(_mo(8 * dev, 8), 8)
        others = [sib] + [(*chips[k], c) for k in range(3)] + [(*chips[k], 1 - c) for k in range(3)]
        other_dev = [dev + 1 - 2 * c] + [2 * cj[k] + c for k in range(3)] + [2 * cj[k] + 1 - c for k in range(3)]
        base = n_tiny
        for r in range(7):
            start(copy(base + r, cslot, call_o.at[my_slot, :], others[r]))
        call_o[my_slot, :] = cslot[...]

        crow = 512 // nch[1]
        loads = []
        for cc in (c, 1 - c):
            for q in range(nch[1]):
                rows = pl.ds(_mo(512 * cc + crow * q, 16), crow)
                loads.append(pltpu.make_async_copy(win_ref.at[rows, :], f_win.at[rows, :], load_sems.at[len(loads)]))
        loads.append(pltpu.make_async_copy(ada_ref, f_ada, load_sems.at[len(loads)]))
        loads.append(pltpu.make_async_copy(wout_ref, f_wout, load_sems.at[len(loads)]))
        for ld in loads:
            ld.start()
        for k in range(2):
            start(copy(sidx(0, 0, k), halves[0](srcs[0], c), specs[0][2](outs[0], j, c), (*chips[k], c)))
        for q in range(nch[1]):
            loads[q].wait()
            rows = pl.ds(_mo(512 * c + crow * q, 16), crow)
            s_win[rows, :] = f_win[rows, :].astype(BF16)
            for k in range(2):
                start(copy(sidx(1, q, k), halves[1](s_win, c, q), specs[1][2](win_o, j, c, q), (*chips[k], c)))
        for q in range(nch[1]):
            loads[nch[1] + q].wait()
            rows = pl.ds(_mo(512 * (1 - c) + crow * q, 16), crow)
            s_win[rows, :] = f_win[rows, :].astype(BF16)
        local = []
        for a in range(na):
            for cc in range(2):
                lc = pltpu.make_async_copy(halves[a](srcs[a], cc), specs[a][2](outs[a], j, cc), local_sems.at[2 * a + cc])
                lc.start()
                local.append(lc)
        loads[2 * nch[1]].wait()
        s_ada[...] = f_ada[...].astype(BF16)
        loads[2 * nch[1] + 1].wait()
        s_wout[...] = f_wout[...].astype(BF16)
        for q, (src, dst) in enumerate([(s_wout, wol_o.at[pl.ds(_mo(512 * j, 16), 512), :]),
                                        (s_ada, adal_o.at[:, pl.ds(_mo(768 * j, 128), 768)])]):
            lc = pltpu.make_async_copy(src, dst, local_sems.at[2 * na + q])
            lc.start()
            local.append(lc)

        for r in range(7):
            slot = call_o.at[pl.ds(_mo(8 * other_dev[r], 8), 8), :]
            copy(base + r, slot, slot, sib).wait_recv()
        lhs[...] = jnp.zeros_like(lhs)
        for b in range(8):
            cv = call_o[8 * b:8 * b + 1, :]
            lhs[b:b + 1, :] = cv * _sigmoid(cv)
        cv = cc_ref[...]
        lhs[8:9, :] = cv * _sigmoid(cv)
        mbuf[j] = _dot(lhs[...].astype(BF16), s_ada[...]) + adab_ref[...]
        for k in range(3):
            start(copy(base + 7 + k, mbuf.at[j], mbuf.at[j], (*chips[k], c)))
        for k in range(3):
            copy(base + 7 + k, mbuf.at[cj[k]], mbuf.at[cj[k]], sib).wait_recv()
        mods_o[...] = jnp.zeros_like(mods_o)
        for jj in range(NCHIP):
            mods_o[0:1, 768 * jj:768 * jj + 768] = mbuf[jj, pl.ds(dev, 1), :]
            mods_o[1:2, 768 * jj:768 * jj + 768] = mbuf[jj, 8:9, :]

        kx = [1 - x, x, 1 - x]
        ky = [y, 1 - y, 1 - y]
        pick = lambda k, lst: jnp.where(k == 0, lst[0], jnp.where(k == 1, lst[1], lst[2]))
        for a in range(na):
            for q in range(nch[a]):
                for step, k in enumerate([c, 1 - c]):
                    reg = specs[a][2](outs[a], pick(k, cj), c, q)
                    copy(sidx(a, q, k), reg, reg, sib).wait_recv()
                    if step == 0:
                        start(copy(sidx(a, q, 2), reg, reg, (pick(1 - c, kx), pick(1 - c, ky), c)))
                    start(copy(sidx(a, q, 3 + k), reg, reg, sib))
        for a in range(na):
            for q in range(nch[a]):
                reg = specs[a][2](outs[a], cj[2], c, q)
                copy(sidx(a, q, 2), reg, reg, sib).wait_recv()
                start(copy(sidx(a, q, 5), reg, reg, sib))
        for a in range(na):
            for q in range(nch[a]):
                for k in range(3):
                    reg = specs[a][2](outs[a], cj[k], 1 - c, q)
                    copy(sidx(a, q, 3 + k), reg, reg, sib).wait_recv()
        for cp in sends:
            cp.wait_send()
        for lc in local:
            lc.wait()

    out_shape = (jax.ShapeDtypeStruct((8, 3 * D), F32), jax.ShapeDtypeStruct((64, D), F32),
                 jax.ShapeDtypeStruct(specs[0][0], F32), jax.ShapeDtypeStruct(specs[1][0], BF16),
                 jax.ShapeDtypeStruct((2048, D), BF16), jax.ShapeDtypeStruct((D, 3 * D), BF16))
    return pl.pallas_call(
        body, name="gather_in", out_shape=out_shape,
        in_specs=[VMEM, VMEM, ANY, VMEM, ANY, ANY, VMEM], out_specs=(VMEM, VMEM, VMEM, ANY, ANY, ANY),
        scratch_shapes=[pltpu.VMEM((D, 1280), BF16), pltpu.VMEM((D, 768), BF16), pltpu.VMEM((512, D), BF16),
                        pltpu.VMEM((D, 1280), F32), pltpu.VMEM((D, 768), F32), pltpu.VMEM((512, D), F32),
                        pltpu.VMEM((8, D), F32), pltpu.VMEM((16, D), F32), pltpu.VMEM((NCHIP, 16, 768), F32),
                        pltpu.SemaphoreType.DMA((n_sem,)), pltpu.SemaphoreType.DMA((n_sem,)),
                        pltpu.SemaphoreType.DMA((2 * na + 2,)), pltpu.SemaphoreType.DMA((2 * nch[1] + 2,))],
        compiler_params=_cp(vmem_mb=56),
    )(c, c_ctx, ada_w, ada_b_j, w_in, w_out, smalls)


HBM = pl.BlockSpec(memory_space=pltpu.HBM)
SEM = pl.BlockSpec(memory_space=pltpu.SEMAPHORE)


def _late_gather_regions(x, y, c):
    chips = [(1 - x, y), (x, 1 - y), (1 - x, 1 - y)]
    wo_reg = lambda r, jj, cc: r.at[pl.ds(_mo(512 * jj + 256 * cc, 16), 256), :]
    ada_reg = lambda r, jj, cc: r.at[pl.ds(_mo(512 * cc, 16), 512), pl.ds(_mo(768 * jj, 128), 768)]
    return chips, wo_reg, ada_reg


def _late_gather_start(wo_land, ada_land):
    def body(wol_ref, adal_ref, wo_ss, wo_rs, ada_ss, ada_rs, wol_thru, adal_thru, token):
        x, y, c = lax.axis_index("x"), lax.axis_index("y"), lax.axis_index("c")
        j = 2 * x + y
        chips, wo_reg, ada_reg = _late_gather_regions(x, y, c)
        for k in range(3):
            for cc in range(2):
                pltpu.make_async_remote_copy(src_ref=wo_reg(wol_ref, j, c), dst_ref=wo_reg(wol_ref, j, c),
                                             send_sem=wo_ss.at[2 * k + cc], recv_sem=wo_rs.at[2 * k + c],
                                             device_id=(*chips[k], cc), device_id_type=MESH).start()
        for k in range(3):
            for cc in range(2):
                pltpu.make_async_remote_copy(src_ref=ada_reg(adal_ref, j, c), dst_ref=ada_reg(adal_ref, j, c),
                                             send_sem=ada_ss.at[2 * k + cc], recv_sem=ada_rs.at[2 * k + c],
                                             device_id=(*chips[k], cc), device_id_type=MESH).start()
        token[...] = jnp.zeros_like(token)

    sems = pltpu.SemaphoreType.DMA((6,))
    return pl.pallas_call(
        body, name="late_gather_start",
        out_shape=(sems, sems, sems, sems, pltpu.HBM(wo_land.shape, BF16), pltpu.HBM(ada_land.shape, BF16),
                   jax.ShapeDtypeStruct((8, 128), F32)),
        in_specs=(HBM, HBM), out_specs=(SEM, SEM, SEM, SEM, HBM, HBM, VMEM), input_output_aliases={0: 4, 1: 5},
        compiler_params=pltpu.CompilerParams(has_side_effects=pltpu.SideEffectType.DATAFLOW_SIDE_EFFECTING),
    )(pltpu.with_memory_space_constraint(wo_land, pltpu.HBM), pltpu.with_memory_space_constraint(ada_land, pltpu.HBM))


def _late_gather_wait(land, send_sems, recv_sems, which, after, name):
    def body(land_ref, ss, rs, after_ref, land_out):
        x, y, c = lax.axis_index("x"), lax.axis_index("y"), lax.axis_index("c")
        j = 2 * x + y
        chips, wo_reg, ada_reg = _late_gather_regions(x, y, c)
        reg = wo_reg if which == "w_out" else ada_reg
        for k in range(3):
            kj = 2 * chips[k][0] + chips[k][1]
            for cc in range(2):
                cp = pltpu.make_async_remote_copy(src_ref=reg(land_ref, j, c), dst_ref=reg(land_ref, kj, cc),
                                                  send_sem=ss.at[2 * k + cc], recv_sem=rs.at[2 * k + cc],
                                                  device_id=(*chips[k], cc), device_id_type=MESH)
                cp.wait_send()
                cp.wait_recv()

    return pl.pallas_call(
        body, name=name, out_shape=pltpu.HBM(land.shape, land.dtype),
        in_specs=(HBM, SEM, SEM, ANY), out_specs=HBM, input_output_aliases={0: 0},
        compiler_params=pltpu.CompilerParams(has_side_effects=pltpu.SideEffectType.DATAFLOW_SIDE_EFFECTING),
    )(land, send_sems, recv_sems, after)


RCHUNK = 16


def _grads_reduce(hn, dzs, hn_c, y, do, pack):
    rp = pack.shape[0]
    hp = rp // 2
    assert hp % RCHUNK == 0
    wi_w = 1280
    lx, lc = hn.shape[0], hn_c.shape[0]
    lt = lx + lc
    n_dz = len(dzs)

    def body(*refs):
        hn_hbm, dz_hbm = refs[0], refs[1:1 + n_dz]
        hnc_hbm, y_hbm, do_hbm, pk_hbm, wi_out, wo_out, pk_out = refs[1 + n_dz:8 + n_dz]
        (hn_mine, hn_other, dzbuf, wi_other, wi_mine, wi_recv, wi_send, wi_rb,
         y_blk, do_mine, do_other, wo_other, wo_mine, wo_recv, wo_send, wo_rb,
         pk_mine, pk_recv, pk_send, pk_rb, pk_own, send_sems, recv_sems, local_sems) = refs[8 + n_dz:]
        x, y, c = lax.axis_index("x"), lax.axis_index("y"), lax.axis_index("c")
        j = 2 * x + y
        sib = (x, y, 1 - c)
        chips = [(1 - x, y), (x, 1 - y), (1 - x, 1 - y)]
        cj = [2 * cx + cy for cx, cy in chips]
        near = (jnp.where(c == 0, 1 - x, x), jnp.where(c == 0, y, 1 - y), c)
        slabs = [cj[2], cj[0], cj[1], j]

        def copy(k, src, dst, to):
            return pltpu.make_async_remote_copy(src_ref=src, dst_ref=dst, send_sem=send_sems.at[k],
                                                recv_sem=recv_sems.at[k], device_id=to, device_id_type=MESH)

        def local(k, src, dst):
            cp = pltpu.make_async_copy(src, dst, local_sems.at[k])
            cp.start()
            return cp

        rows_half = lambda r, cc, n: r.at[pl.ds(_mo(cc * n, 16), n), :]
        cols_half = lambda r, cc, n: r.at[:, pl.ds(_mo(cc * n, 128), n)]
        pk_piece = lambda r, cc, jj: r.at[pl.ds(_mo(cc * hp, 16), hp), pl.ds(_mo(jj * 128, 128), 128)]

        sends = []

        def start(cp):
            cp.start()
            sends.append(cp)

        def dz_pieces(s):
            g0 = wi_w * s
            k0, off0 = g0 // D, g0 % D
            w0 = min(D - off0, wi_w)
            pieces = [(k0, off0, w0, 0)]
            if w0 < wi_w:
                pieces.append((k0 + 1, 0, wi_w - w0, w0))
            return pieces

        def dz_copies(s):
            cps = []
            for q, (k, off, w, dst) in enumerate(dz_pieces(s)):
                cps.append(pltpu.make_async_copy(dz_hbm[k].at[pl.ds(lc if k == 0 else 0, lx), pl.ds(off, w)],
                                                 dzbuf.at[pl.ds(0, lx), pl.ds(dst, w)], local_sems.at[11 + q]))
            if s == 0:
                cps.append(pltpu.make_async_copy(dz_hbm[0].at[pl.ds(0, lc), :], dzbuf.at[pl.ds(lx, lc), pl.ds(0, D)],
                                                 local_sems.at[13]))
            return cps

        def dz_load(sl):
            for s in range(NCHIP):
                @pl.when(sl == s)
                def _():
                    if s == 0:
                        dzbuf[pl.ds(lx, lc), pl.ds(D, wi_w - D)] = jnp.zeros((lc, wi_w - D), BF16)
                    else:
                        dzbuf[pl.ds(lx, lc), :] = jnp.zeros((lc, wi_w), BF16)
                    for cp in dz_copies(s):
                        cp.start()

        def dz_wait(sl):
            for s in range(NCHIP):
                @pl.when(sl == s)
                def _():
                    for cp in dz_copies(s):
                        cp.wait()

        l_pk = local(0, rows_half(pk_hbm, c, hp), pk_mine)
        start(copy(0, rows_half(pk_hbm, 1 - c, hp), pk_recv, sib))
        col = lambda r, cc: r.at[:, pl.ds(_mo(cc * 512, 128), 512)]
        do_loads = [local(7, col(do_hbm, c), do_mine), local(14, col(do_hbm, 1 - c), do_other)]
        hn_loads = [local(2, col(hn_hbm, c), hn_mine.at[pl.ds(0, lx), :]),
                    local(3, col(hnc_hbm, c), hn_mine.at[pl.ds(lx, lc), :]),
                    local(4, col(hn_hbm, 1 - c), hn_other.at[pl.ds(0, lx), :]),
                    local(5, col(hnc_hbm, 1 - c), hn_other.at[pl.ds(lx, lc), :])]
        y_copy = lambda s: pltpu.make_async_copy(col(y_hbm, slabs[s]), y_blk, local_sems.at[1])
        y_copy(0).start()
        dz_load(slabs[0])

        def pair_sum(mine, recv, send, nrows, keep, relayed=None):
            def step(i, carry):
                rows = pl.ds(_mo(i * RCHUNK, RCHUNK), RCHUNK)
                s = mine[rows, :] + recv[rows, :].astype(F32)
                if relayed is not None:
                    s = s + relayed[rows, :].astype(F32)
                if keep:
                    mine[rows, :] = s
                if send is not None:
                    send[rows, :] = s.astype(BF16)
                return carry
            lax.fori_loop(0, nrows // RCHUNK, step, 0)

        def chip_sum(own, rb, nrows, terms=(0, 1, 2)):
            def step(i, carry):
                rows = pl.ds(_mo(i * RCHUNK, RCHUNK), RCHUNK)
                acc = own[rows, :]
                for q in terms:
                    acc = acc + rb[q, rows, :].astype(F32)
                own[rows, :] = acc
                return carry
            lax.fori_loop(0, nrows // RCHUNK, step, 0)

        w_in_g = dict(other=wi_other, mine=wi_mine, recv=wi_recv, send=wi_send, rb=wi_rb, p1_sems=(2, 3, 4, 5), p2_sem=12,
                      p1=[None] * NCHIP, wait_load=lambda s: dz_wait(slabs[s]), load=lambda s: dz_load(slabs[s]),
                      dot_other=lambda: _dot_tn(hn_other[...], dzbuf[...]), dot_mine=lambda: _dot_tn(hn_mine[...], dzbuf[...]))
        w_out_g = dict(other=wo_other, mine=wo_mine, recv=wo_recv, send=wo_send, rb=wo_rb, p1_sems=(1, 24, 25, 26), p2_sem=9,
                       p1=[None] * NCHIP, wait_load=lambda s: y_copy(s).wait(), load=lambda s: y_copy(s).start(),
                       dot_other=lambda: _dot_tn(y_blk[...], do_other[...]), dot_mine=lambda: _dot_tn(y_blk[...], do_mine[...]))

        def piece_matmuls(g, s):
            if s >= 2:
                g["p1"][s - 2].wait_send()
            g["wait_load"](s)
            g["other"][s % 2] = g["dot_other"]().astype(BF16)
            g["p1"][s] = copy(g["p1_sems"][s], g["other"].at[s % 2], g["recv"].at[s], sib)
            g["p1"][s].start()
            g["mine"][s % 2] = g["dot_mine"]()
            if s + 1 < NCHIP:
                g["load"](s + 1)

        def piece_finish(g, s):
            mine, recv, send, rb, p2 = g["mine"].at[s % 2], g["recv"].at[s], g["send"], g["rb"], g["p2_sem"]
            nrows = mine.shape[0]
            copy(g["p1_sems"][s], recv, recv, sib).wait_recv()
            if s == 3:
                pair_sum(mine, recv, None, nrows, True)
                return
            if s == 0:
                pair_sum(mine, recv, send.at[0], nrows, False)
                start(copy(p2, send.at[0], rb.at[0], near))
                return
            adds_relayed = c == (1 if s == 1 else 0)

            @pl.when(adds_relayed)
            def _():
                copy(p2, rb.at[0], rb.at[0], sib).wait_recv()
                pair_sum(mine, recv, send.at[s], nrows, False, rb.at[0])

            @pl.when(jnp.logical_not(adds_relayed))
            def _():
                pair_sum(mine, recv, send.at[s], nrows, False)
            start(copy(p2 + s, send.at[s], rb.at[s], (*chips[s - 1], c)))

        def piece_total(g):
            for k in (1, 2):
                copy(g["p2_sem"] + k, g["rb"].at[k], g["rb"].at[k], sib).wait_recv()
            chip_sum(g["mine"].at[1], g["rb"], g["mine"].shape[1], (1, 2))

        for cp in do_loads:
            cp.wait()
        piece_matmuls(w_out_g, 0)
        piece_matmuls(w_out_g, 1)
        piece_finish(w_out_g, 0)
        piece_matmuls(w_out_g, 2)
        piece_finish(w_out_g, 1)
        piece_matmuls(w_out_g, 3)
        piece_finish(w_out_g, 2)

        for cp in hn_loads:
            cp.wait()
        piece_matmuls(w_in_g, 0)

        l_pk.wait()
        copy(0, pk_recv, pk_recv, sib).wait_recv()
        pair_sum(pk_mine, pk_recv, pk_send, hp, True)
        for k in range(3):
            start(copy(6 + k, pk_send.at[:, pl.ds(_mo(cj[k] * 128, 128), 128)], pk_rb.at[k], (*chips[k], c)))
        l_pk_own = local(6, pk_mine.at[:, pl.ds(_mo(j * 128, 128), 128)], pk_own)

        piece_matmuls(w_in_g, 1)
        piece_finish(w_in_g, 0)

        l_pk_own.wait()
        for k in range(3):
            copy(6 + k, pk_rb.at[k], pk_rb.at[k], sib).wait_recv()
        chip_sum(pk_own, pk_rb, hp)
        l_pk_out = local(8, pk_own, pk_piece(pk_out, c, j))
        start(copy(15, pk_own, pk_piece(pk_out, c, j), sib))
        for k in range(2):
            start(copy(16 + k, pk_own, pk_piece(pk_out, c, j), (*chips[k], c)))

        piece_matmuls(w_in_g, 2)
        piece_finish(w_in_g, 1)
        piece_matmuls(w_in_g, 3)
        piece_finish(w_in_g, 2)

        piece_finish(w_out_g, 3)
        piece_total(w_out_g)
        l_wo_out = local(9, wo_mine.at[1], cols_half(wo_out, c, 512))
        start(copy(22, wo_mine.at[1], cols_half(wo_out, c, 512), sib))

        far = (jnp.where(c == 0, x, 1 - x), jnp.where(c == 0, 1 - y, y), c)
        for step, k in enumerate([c, 1 - c, 2]):
            reg = pk_piece(pk_out, c, jnp.where(k == 0, cj[0], jnp.where(k == 1, cj[1], cj[2])))
            copy(16 + k, reg, reg, sib).wait_recv()
            if step == 0:
                start(copy(18, reg, reg, far))
            start(copy(19 + k, reg, reg, sib))

        piece_finish(w_in_g, 3)
        piece_total(w_in_g)
        l_wi_out = local(10, wi_mine.at[1], rows_half(wi_out, c, 512))
        start(copy(23, wi_mine.at[1], rows_half(wi_out, c, 512), sib))

        reg = pk_piece(pk_out, 1 - c, j)
        copy(15, reg, reg, sib).wait_recv()
        for k in range(3):
            reg = pk_piece(pk_out, 1 - c, cj[k])
            copy(19 + k, reg, reg, sib).wait_recv()
        reg = cols_half(wo_out, 1 - c, 512)
        copy(22, reg, reg, sib).wait_recv()
        reg = rows_half(wi_out, 1 - c, 512)
        copy(23, reg, reg, sib).wait_recv()
        for cp in sends + w_in_g["p1"][2:] + w_out_g["p1"][2:]:
            cp.wait_send()
        for cp in (l_pk_out, l_wo_out, l_wi_out):
            cp.wait()

    return pl.pallas_call(
        body, name="grads_reduce",
        out_shape=(jax.ShapeDtypeStruct((D, wi_w), F32), jax.ShapeDtypeStruct((512, D), F32),
                   jax.ShapeDtypeStruct(pack.shape, F32)),
        in_specs=[ANY] * (5 + n_dz), out_specs=(ANY,) * 3,
        scratch_shapes=[
            pltpu.VMEM((lt, 512), BF16), pltpu.VMEM((lt, 512), BF16), pltpu.VMEM((lt, wi_w), BF16),
            pltpu.VMEM((2, 512, wi_w), BF16), pltpu.VMEM((2, 512, wi_w), F32), pltpu.VMEM((4, 512, wi_w), BF16),
            pltpu.VMEM((3, 512, wi_w), BF16), pltpu.VMEM((3, 512, wi_w), BF16),
            pltpu.VMEM((lx, 512), BF16), pltpu.VMEM((lx, 512), BF16), pltpu.VMEM((lx, 512), BF16),
            pltpu.VMEM((2, 512, 512), BF16), pltpu.VMEM((2, 512, 512), F32), pltpu.VMEM((4, 512, 512), BF16),
            pltpu.VMEM((3, 512, 512), BF16), pltpu.VMEM((3, 512, 512), BF16),
            pltpu.VMEM((hp, 512), F32), pltpu.VMEM((hp, 512), F32), pltpu.VMEM((hp, 512), BF16),
            pltpu.VMEM((3, hp, 128), BF16), pltpu.VMEM((hp, 128), F32),
            pltpu.SemaphoreType.DMA((27,)), pltpu.SemaphoreType.DMA((27,)), pltpu.SemaphoreType.DMA((15,))],
        compiler_params=_cp(vmem_mb=56),
    )(hn, *dzs, hn_c, y, do, pack)


def _ada_bwd(c_all, dmx_all_j, dmc_j, dmx_all, dmc, c_ctx, ada_w_full):
    def body(c_ref, dmxj_ref, dmcj_ref, dmx_ref, dmc_ref, cc_ref, w_ref, gw_ref, gb_ref, gc_ref, lhs, rhs, dm8):
        lhs[...] = jnp.zeros_like(lhs)
        rhs[...] = jnp.zeros_like(rhs)
        cv = c_ref[...]
        lhs[0:8, :] = cv * _sigmoid(cv)
        cc = cc_ref[...]
        a_c, da_c = _silu_and_grad(cc)
        lhs[8:9, :] = a_c
        rhs[0:8, :] = dmxj_ref[...]
        rhs[8:9, :] = dmcj_ref[...]
        gw_ref[...] = _dot_tn(lhs[...].astype(BF16), rhs[...].astype(BF16))
        gb_ref[...] = jnp.sum(dmx_ref[...], axis=0, keepdims=True) + dmc_ref[...]
        dm8[...] = jnp.zeros_like(dm8)
        dm8[0:1, :] = dmc_ref[...]
        da = _dot_nt(dm8[...].astype(BF16), w_ref[...])
        gc_ref[...] = da[0:1, :] * da_c

    return pl.pallas_call(
        body, name="ada_bwd",
        out_shape=(jax.ShapeDtypeStruct((D, 768), F32), jax.ShapeDtypeStruct((1, 3 * D), F32),
                   jax.ShapeDtypeStruct((1, D), F32)),
        in_specs=[VMEM] * 7, out_specs=(VMEM,) * 3,
        scratch_shapes=[pltpu.VMEM((16, D), F32), pltpu.VMEM((16, 768), F32), pltpu.VMEM((8, 3 * D), F32)],
        compiler_params=_cp(vmem_mb=32),
    )(c_all, dmx_all_j, dmc_j, dmx_all, dmc, c_ctx, ada_w_full)


def _proj(x, mods, mrow, norm_g, w_full, nk, name, sgu=None, z0_rows=None, z0_off=0, z0_into=None):
    lx = x.shape[0]
    n = lx // T
    order = [2, 3, 0, 1, 4] if sgu is not None else list(range(nk))

    def body(x_ref, sh_ref, sc_ref, ng_ref, *rest):
        w_refs, rest = rest[:nk], rest[nk:]
        if sgu is not None:
            g_ref, b_ref, sw_ref, bt_ref = rest[:4]
            rest = rest[4:]
        if z0_into is not None:
            rest = rest[1:]
        hn_ref, z_refs = rest[0], rest[1:1 + nk]
        xv = x_ref[...]
        r = lax.rsqrt(jnp.mean(xv * xv, axis=-1, keepdims=True) + NORM_EPS)
        hn = (xv * r) * ng_ref[...] * (1.0 + sc_ref[mrow:mrow + 1, :]) + sh_ref[mrow:mrow + 1, :]
        hb = hn.astype(BF16)
        hn_ref[...] = hb
        for k in order[:2]:
            z_refs[k][...] = _dot(hb, w_refs[k][...])
        if sgu is not None:
            ys_ref, mixed_s = rest[1 + nk], rest[2 + nk]
            for ch in range(T // HD):
                rows = slice(HD * ch, HD * ch + HD)
                ug = _sgu_parts(z_refs[2][rows, :], z_refs[3][rows, :], g_ref[...], b_ref[...], sw_ref, bt_ref,
                                mixed_s)[0]
                ys_ref[rows, :] = ug * mixed_s[...]
        for k in order[2:]:
            z_refs[k][...] = _dot(hb, w_refs[k][...])

    row = pl.BlockSpec((T, D), lambda i: (i, 0))
    vec = pl.BlockSpec((1, D), lambda i: (0, 0))
    in_specs = [row, pl.BlockSpec((8, D), lambda i: (0, 0)), pl.BlockSpec((8, D), lambda i: (0, 1)), vec]
    in_specs += [pl.BlockSpec((D, D), lambda i, k=k: (0, k)) for k in range(nk)]
    args = [x, mods, mods, norm_g] + [w_full] * nk
    out_shape = (jax.ShapeDtypeStruct((lx, D), BF16),) + tuple(jax.ShapeDtypeStruct((lx, D), F32) for _ in range(nk))
    out_specs = [row] * len(out_shape)
    scratch = []
    aliases = {}
    if sgu is not None:
        in_specs += [vec, vec, pl.BlockSpec((NH, HD, HD), lambda i: (0, 0, 0)), pl.BlockSpec((HD, NH), lambda i: (0, 0))]
        args += list(sgu)
        out_shape += (jax.ShapeDtypeStruct((lx, D), F32),)
        out_specs.append(row)
        scratch = [pltpu.VMEM((HD, D), F32)]
    if z0_rows is not None:
        out_shape = out_shape[:1] + (jax.ShapeDtypeStruct((z0_rows, D), F32),) + out_shape[2:]
        out_specs[1] = pl.BlockSpec((T, D), lambda i: (i + z0_off, 0))
    if z0_into is not None:
        out_shape = out_shape[:1] + (jax.ShapeDtypeStruct(z0_into.shape, F32),) + out_shape[2:]
        aliases = {len(args): 1}
        in_specs.append(ANY)
        args.append(z0_into)
    return pl.pallas_call(
        body, name=name, grid=(n,), out_shape=out_shape, in_specs=in_specs, out_specs=tuple(out_specs),
        scratch_shapes=scratch, input_output_aliases=aliases, compiler_params=_cp(1, vmem_mb=56),
    )(*args)


def _tile_specs(rows_per_pos, width, n_tiles, tile):
    last = n_tiles * (T // 8) - 1
    r = rows_per_pos
    return [pl.BlockSpec((T * r, width), lambda i: (tile(i), 0)),
            pl.BlockSpec((8 * r, width), lambda i: (jnp.maximum(tile(i) * (T // 8) - 1, 0), 0)),
            pl.BlockSpec((8 * r, width), lambda i: (jnp.minimum((tile(i) + 1) * (T // 8), last), 0))]


def _has_prev(tile):
    return tile >= 2


def _has_next(tile, nt):
    return jnp.logical_and(tile >= 1, tile < nt - 1)


ZT = pl.BlockSpec((T * NH, HD), lambda i: (i, 0))
CONV_CHUNK = 32


SCAN_SUB = 4


def _scan_tile(chains, post, carry_ref):
    blk = T // SCAN_SUB

    def step(k, state):
        new = []
        for ci, (a_ref, x_ref, o_ref, q_ref, reverse) in enumerate(chains):
            for q in range(SCAN_SUB):
                s, p = state[ci * SCAN_SUB + q]
                t = (q + 1) * blk - 1 - k if reverse else q * blk + k
                r = pl.ds(_mo(t * NH, NH), NH)
                a = a_ref[r, :]
                if post:
                    o = x_ref[r, :] + s
                    o_ref[r, :] = o
                    q_ref[r, :] = p
                    new.append((a * o, a * p))
                else:
                    o = a * s + x_ref[r, :]
                    p = a * p
                    o_ref[r, :] = o
                    q_ref[r, :] = p
                    new.append((o, p))
        return tuple(new)

    zero = jnp.zeros((NH, HD), F32)
    one = jnp.ones((NH, HD), F32)
    final = lax.fori_loop(0, blk, step, tuple((zero, one) for _ in range(len(chains) * SCAN_SUB)), unroll=2)
    for ci, (a_ref, x_ref, o_ref, q_ref, reverse) in enumerate(chains):
        carry = carry_ref[ci]
        for q in (range(SCAN_SUB - 1, -1, -1) if reverse else range(SCAN_SUB)):
            rows = pl.ds(q * blk * NH, blk * NH)
            fixed = o_ref[rows, :].reshape(blk, NH, HD) + q_ref[rows, :].reshape(blk, NH, HD) * carry[None]
            o_ref[rows, :] = fixed.reshape(blk * NH, HD)
            s_loc, p_loc = final[ci * SCAN_SUB + q]
            carry = s_loc + p_loc * carry
        carry_ref[ci] = carry


def _lru_fwd(xa, conv_wz, conv_bz, wcat, bcat, lamcat, name):
    lx = xa.shape[0]
    n = lx // T
    tile_u = lambda i: i
    tile_d = lambda i: jnp.where(i == 0, 0, n - i)

    def body(xm_u, xp_u, xn_u, xm_d, xp_d, xn_d, cw, cb, w_ref, b_ref, lam_ref,
             xaz_o, xcz_o, af_o, ab_o, hf_o, hb_o, gf_o, gb_o, fu, fd, pad, xc_d, x_u, x_d, q_u, q_d, carry):
        i = pl.program_id(0)

        @pl.when(i == 0)
        def _():
            carry[...] = jnp.zeros_like(carry)

        def conv_gates(xm, xp, xn, tile, d, xc_ref, a_ref, x_ref, xaz_ref, g_ref):
            pmask = jnp.where(_has_prev(tile), 1.0, 0.0)
            nmask = jnp.where(_has_next(tile, n), 1.0, 0.0)
            for h in range(NH):
                cols = slice(HD * h, HD * h + HD)
                pad[_zrows(h, 8), :] = xp[:, cols] * pmask
                pad[pl.ds(8 * NH + h, T, stride=NH), :] = xm[:, cols]
                pad[pl.ds((T + 8) * NH + h, 8, stride=NH), :] = xn[:, cols] * nmask
            if xaz_ref is not None:
                xaz_ref[...] = pad[pl.ds(8 * NH, T * NH), :]

            def conv_chunk(ci, c_):
                base = pl.multiple_of(ci * (CONV_CHUNK * NH), CONV_CHUNK * NH)
                acc = None
                for k in range(4):
                    sl = pad[pl.ds(base + (7 + k) * NH, CONV_CHUNK * NH), :].reshape(CONV_CHUNK, NH, HD)
                    term = sl * cw[k][None]
                    acc = term if acc is None else acc + term
                acc = acc + cb[...][None]
                xc_ref[pl.ds(base, CONV_CHUNK * NH), :] = acc.reshape(CONV_CHUNK * NH, HD)
                return c_
            lax.fori_loop(0, T // CONV_CHUNK, conv_chunk, 0)

            for h in range(NH):
                xch = xc_ref[_zrows(h, T), :]
                pre = _dot(xch.astype(BF16), w_ref[h, :, 256 * d:256 * d + 256]) + b_ref[h:h + 1, 256 * d:256 * d + 256]
                r, gi, _, _, a, mult = _lru_gate(pre, lam_ref[h:h + 1, :], d, 0)
                a_ref[_zrows(h, T), :] = a
                x_ref[_zrows(h, T), :] = mult * gi * xch
                for q, val in enumerate((r, gi, mult)):
                    g_ref[:, q * D + HD * h:q * D + HD * h + HD] = val

        conv_gates(xm_u, xp_u, xn_u, tile_u(i), 0, xcz_o, af_o, x_u, xaz_o, gf_o)
        conv_gates(xm_d, xp_d, xn_d, tile_d(i), 1, xc_d, ab_o, x_d, None, gb_o)

        _scan_tile([(af_o, x_u, hf_o, q_u, False), (ab_o, x_d, hb_o, q_d, True)], False, carry)

        @pl.when(i == 0)
        def _():
            fu[...] = carry[0]
            fd[...] = carry[1]

    full = lambda shape: pl.BlockSpec(shape, lambda i: (0,) * len(shape))
    st = full((NH, HD))
    in_specs = _tile_specs(1, D, n, tile_u) + _tile_specs(1, D, n, tile_d)
    in_specs += [full((4, NH, HD)), st, full((NH, HD, 4 * HD)), full((NH, 4 * HD)), full((NH, 2 * HD))]
    up = pl.BlockSpec((T * NH, HD), lambda i: (tile_u(i), 0))
    dn = pl.BlockSpec((T * NH, HD), lambda i: (tile_d(i), 0))
    zs = jax.ShapeDtypeStruct((lx * NH, HD), F32)
    ss = jax.ShapeDtypeStruct((NH, HD), F32)
    zbuf = pltpu.VMEM((T * NH, HD), F32)
    gs = jax.ShapeDtypeStruct((lx, 3 * D), F32)
    g_up = pl.BlockSpec((T, 3 * D), lambda i: (tile_u(i), 0))
    g_dn = pl.BlockSpec((T, 3 * D), lambda i: (tile_d(i), 0))
    return pl.pallas_call(
        body, name=name, grid=(n,), out_shape=(zs,) * 6 + (gs, gs, ss, ss), in_specs=in_specs,
        out_specs=(up, up, up, dn, up, dn, g_up, g_dn, st, st),
        scratch_shapes=[pltpu.VMEM(((T + 16) * NH, HD), F32), zbuf, zbuf, zbuf, zbuf, zbuf,
                        pltpu.VMEM((2, NH, HD), F32)],
        compiler_params=_cp(1, vmem_mb=48),
    )(xa, xa, xa, xa, xa, xa, conv_wz, conv_bz, wcat, bcat, lamcat)


def _sgu_parts(u, v, lng, lnb, w_ref, bt_ref, mixed_s):
    ug, dug = _gelu_and_grad(u)
    vg, dvg = _gelu_and_grad(v)
    mu = jnp.mean(vg, axis=-1, keepdims=True)
    vc = vg - mu
    rstd = lax.rsqrt(jnp.mean(vc * vc, axis=-1, keepdims=True) + LN_EPS)
    vh = vc * rstd
    vn = (vh * lng + lnb).astype(BF16)
    for g in range(NH):
        cols = slice(HD * g, HD * g + HD)
        mixed_s[:, cols] = _dot(w_ref[g], vn[:, cols]) + bt_ref[:, g:g + 1]
    return ug, dug, dvg, rstd, vh, vn


def _sgu_bwd_chunk(u, v, dys_v, lng, lnb, w_ref, bt_ref, mixed_s, dvn_s, dw_ref, db_ref, dg_ref, dbl_ref):
    ug, dug, dvg, rstd, vh, vn = _sgu_parts(u, v, lng, lnb, w_ref, bt_ref, mixed_s)
    du = (dys_v * mixed_s[...] * dug).astype(BF16)
    dmix = dys_v * ug
    ones = jnp.ones((8, HD), BF16)
    for g in range(NH):
        cols = slice(HD * g, HD * g + HD)
        dm = dmix[:, cols]
        hi = dm.astype(BF16)
        lo = (dm - hi.astype(F32)).astype(BF16)
        dw_ref[g] += _dot_nt(hi, vn[:, cols])
        db_ref[g:g + 1, :] += (_dot_nt(ones, hi) + _dot_nt(ones, lo))[0:1, :]
        dvn_s[:, cols] = _dot_tn(w_ref[g], hi)
    dvn = dvn_s[...]
    dg_ref[...] += jnp.sum(dvn * vh, axis=0, keepdims=True)
    dbl_ref[...] += jnp.sum(dvn, axis=0, keepdims=True)
    dvh = dvn * lng
    dvg_in = rstd * (dvh - jnp.mean(dvh, axis=-1, keepdims=True) - vh * jnp.mean(dvh * vh, axis=-1, keepdims=True))
    return du, (dvg_in * dvg).astype(BF16)


def _out_fwd_bwd(hf_z, hb_z, ga, gb, ys, x, tgt, mods, final_g, w_out_full):
    lx = x.shape[0]
    n = lx // T

    def body(hf_ref, hb_ref, ga_ref, gb_ref, ys_ref, x_ref, t_ref, gx_ref, fg_ref, w_ref,
             loss_ref, dfg_ref, dgx_ref, dxn_ref, y_ref, do_ref, dga_ref, dgb_ref, dyl_ref, dys_ref, yl_s):
        i = pl.program_id(0)

        @pl.when(i == 0)
        def _():
            loss_ref[...] = jnp.zeros_like(loss_ref)
            dfg_ref[...] = jnp.zeros_like(dfg_ref)
            dgx_ref[...] = jnp.zeros_like(dgx_ref)

        for h in range(NH):
            yl_s[:, HD * h:HD * h + HD] = hf_ref[_zrows(h, T), :] + hb_ref[_zrows(h, T), :]
        yl = yl_s[...]
        gav = ga_ref[...]
        gbv = gb_ref[...]
        sa, dsa = _silu_and_grad(gav)
        sb, dsb = _silu_and_grad(gbv)
        ysv = ys_ref[...]
        y_ref[:, 0:D] = (yl * sa).astype(BF16)
        y_ref[:, D:2 * D] = (ysv * sb).astype(BF16)
        o = _dot(y_ref[...], w_ref[...])
        gx = gx_ref[0:1, :]
        xnew = x_ref[...] + gx * o
        r2 = lax.rsqrt(jnp.mean(xnew * xnew, axis=-1, keepdims=True) + NORM_EPS)
        xh = xnew * r2
        fg = fg_ref[...]
        err = xh * fg - t_ref[...]
        loss_ref[...] += 0.5 * jnp.sum(jnp.mean(err * err, axis=-1, keepdims=True), axis=0, keepdims=True)

        @pl.when(i == n - 1)
        def _():
            lp = loss_ref[...]
            lp1 = lp.astype(BF16).astype(F32)
            lp2 = (lp - lp1).astype(BF16).astype(F32)
            lp3 = (lp - lp1 - lp2).astype(BF16).astype(F32)
            lane = lax.broadcasted_iota(jnp.int32, lp.shape, 1)
            loss_ref[...] = jnp.where(lane == 0, lp1, jnp.where(lane == 1, lp2, jnp.where(lane == 2, lp3, 0.0)))
        dout = err * (1.0 / D)
        dfg_ref[...] += jnp.sum(dout * xh, axis=0, keepdims=True)
        dxh = dout * fg
        dxn = r2 * (dxh - xh * jnp.mean(dxh * xh, axis=-1, keepdims=True))
        dxn_ref[...] = dxn
        dgx_ref[...] += jnp.sum(dxn * o, axis=0, keepdims=True)
        do = (dxn * gx).astype(BF16)
        do_ref[...] = do
        dy = _dot_nt(do, w_ref[...])
        dy1 = dy[:, 0:D]
        dy2 = dy[:, D:2 * D]
        dga_ref[...] = (dy1 * yl * dsa).astype(BF16)
        dgb_ref[...] = (dy2 * ysv * dsb).astype(BF16)
        dys_ref[...] = dy2 * sb
        yl_s[...] = dy1 * sa
        for h in range(NH):
            dyl_ref[_zrows(h, T), :] = yl_s[:, HD * h:HD * h + HD]

    row = pl.BlockSpec((T, D), lambda i: (i, 0))
    vec = pl.BlockSpec((1, D), lambda i: (0, 0))
    zlat = pl.BlockSpec((T * NH, HD), lambda i: (i + 1, 0))
    in_specs = [zlat, zlat, row, row, row, row, row, pl.BlockSpec((8, D), lambda i: (0, 2)), vec,
                pl.BlockSpec((2 * D, D), lambda i: (0, 0))]
    out_shape = (jax.ShapeDtypeStruct((1, D), F32), jax.ShapeDtypeStruct((1, D), F32), jax.ShapeDtypeStruct((1, D), F32),
                 jax.ShapeDtypeStruct((lx, D), F32), jax.ShapeDtypeStruct((lx, 2 * D), BF16),
                 jax.ShapeDtypeStruct((lx, D), BF16), jax.ShapeDtypeStruct((lx, D), BF16),
                 jax.ShapeDtypeStruct((lx, D), BF16), jax.ShapeDtypeStruct((lx * NH, HD), F32),
                 jax.ShapeDtypeStruct((lx, D), F32))
    out_specs = (vec, vec, vec, row, pl.BlockSpec((T, 2 * D), lambda i: (i, 0)),
                 row, row, row, ZT, row)
    return pl.pallas_call(
        body, name="out_fwd_bwd", grid=(n,), out_shape=out_shape, in_specs=in_specs, out_specs=out_specs,
        scratch_shapes=[pltpu.VMEM((T, D), F32)],
        compiler_params=_cp(1, vmem_mb=56),
    )(hf_z, hb_z, ga, gb, ys, x, tgt, mods, final_g, w_out_full)


def _lru_bwd(xc_z, dy_z, hf_z, hb_z, af_z, ab_z, gf, gb, s_b, wcat, lamcat, name):
    lx = xc_z.shape[0] // NH
    n = lx // T
    tile_u = lambda i: jnp.where(i == n - 1, 0, i + 1)
    tile_d = lambda i: n - 1 - i

    def body(xc_u, dy_u, hb_ref, hbn_ref, ab_ref, gb_ref, xc_d, dy_d, hf_ref, hfp_ref, af_ref, gf_ref,
             sb_ref, w_ref, lam_ref, dxcb_ref, dxcf_ref, dw_ref, db_ref, dl_ref,
             lb_s, lf_s, q_u, q_d, pf_s, pb_s, dpre_s, dyu_s, dyd_s, carry):
        i = pl.program_id(0)
        tu, td = tile_u(i), tile_d(i)

        @pl.when(i == 0)
        def _():
            dw_ref[...] = jnp.zeros_like(dw_ref)
            db_ref[...] = jnp.zeros_like(db_ref)
            dl_ref[...] = jnp.zeros_like(dl_ref)
            carry[...] = jnp.zeros_like(carry)

        dyu_s[...] = dy_u[...] * jnp.where(tu == 0, 0.0, 1.0)
        dyd_s[...] = dy_d[...] * jnp.where(td == 0, 0.0, 1.0)
        _scan_tile([(ab_ref, dyu_s, lb_s, q_u, False), (af_ref, dyd_s, lf_s, q_d, True)], True, carry)
        zero = jnp.zeros((NH, HD), F32)
        pb_s[pl.ds(0, T * NH), :] = hb_ref[...]
        pb_s[pl.ds(T * NH, NH), :] = jnp.where(tu == n - 1, sb_ref[...], jnp.where(tu == 0, zero, hbn_ref[pl.ds(0, NH), :]))
        pf_s[pl.ds(0, NH), :] = jnp.where(td == 0, zero, hfp_ref[pl.ds(7 * NH, NH), :])
        pf_s[pl.ds(NH, T * NH), :] = hf_ref[...]
        sides = ((1, xc_u, lb_s, pb_s, NH, ab_ref, gb_ref, dxcb_ref), (0, xc_d, lf_s, pf_s, 0, af_ref, gf_ref, dxcf_ref))
        for d, xc_ref, adj_s, prev_s, prev_off, a_ref, g_ref, dxc_ref in sides:
            wcols = slice(256 * d, 256 * d + 256)
            for h in range(NH):
                xch = xc_ref[_zrows(h, T), :]
                xcb = xch.astype(BF16)
                r, gi, mult = (g_ref[:, q * D + HD * h:q * D + HD * h + HD] for q in range(3))
                a = a_ref[_zrows(h, T), :]
                lam = lam_ref[h:h + 1, HD * d:HD * d + HD]
                sp = _softplus(-lam)
                du = adj_s[_zrows(h, T), :]
                da = du * prev_s[pl.ds(prev_off + h, T, stride=NH), :]
                dgi = du * mult * xch
                dmult = du * gi * xch
                dla = da * a - dmult * (a * a) / mult
                dr = dla * ((-LRU_C) * sp)
                dsp = jnp.sum(dla * ((-LRU_C) * r), axis=0, keepdims=True)
                dl_ref[h:h + 1, HD * d:HD * d + HD] += dsp * (-_sigmoid(-lam))
                dpre_s[:, 0:HD] = dr * r * (1.0 - r)
                dpre_s[:, HD:2 * HD] = dgi * gi * (1.0 - gi)
                dpre = dpre_s[...]
                dpb = dpre.astype(BF16)
                dw_ref[h, :, wcols] += _dot_tn(xcb, dpb)
                db_ref[h:h + 1, wcols] += jnp.sum(dpre, axis=0, keepdims=True)
                dxc_ref[_zrows(h, T), :] = du * mult * gi + _dot_nt(dpb, w_ref[h, :, wcols])

    full = lambda shape: pl.BlockSpec(shape, lambda i: (0,) * len(shape))
    wsp, bsp, lsp = full((NH, HD, 4 * HD)), full((NH, 4 * HD)), full((NH, 2 * HD))
    st = full((NH, HD))
    up = pl.BlockSpec((T * NH, HD), lambda i: (tile_u(i), 0))
    dn = pl.BlockSpec((T * NH, HD), lambda i: (tile_d(i), 0))
    dy_up = pl.BlockSpec((T * NH, HD), lambda i: (jnp.maximum(tile_u(i) - 1, 0), 0))
    dy_dn = pl.BlockSpec((T * NH, HD), lambda i: (jnp.maximum(tile_d(i) - 1, 0), 0))
    nxt = _tile_specs(NH, HD, n, tile_u)[2]
    prv = _tile_specs(NH, HD, n, tile_d)[1]
    g_up = pl.BlockSpec((T, 3 * D), lambda i: (tile_u(i), 0))
    g_dn = pl.BlockSpec((T, 3 * D), lambda i: (tile_d(i), 0))
    zs = jax.ShapeDtypeStruct((lx * NH, HD), F32)
    zbuf = pltpu.VMEM((T * NH, HD), F32)
    zbuf1 = pltpu.VMEM(((T + 1) * NH, HD), F32)
    return pl.pallas_call(
        body, name=name, grid=(n,),
        out_shape=(zs, zs, jax.ShapeDtypeStruct((NH, HD, 4 * HD), F32), jax.ShapeDtypeStruct((NH, 4 * HD), F32),
                   jax.ShapeDtypeStruct((NH, 2 * HD), F32)),
        in_specs=[up, dy_up, up, nxt, up, g_up, dn, dy_dn, dn, prv, dn, g_dn, st, wsp, lsp],
        out_specs=(up, dn, wsp, bsp, lsp),
        scratch_shapes=[zbuf, zbuf, zbuf, zbuf, zbuf1, zbuf1, pltpu.VMEM((T, 2 * HD), F32), zbuf, zbuf,
                        pltpu.VMEM((2, NH, HD), F32)],
        compiler_params=_cp(1, vmem_mb=56),
    )(xc_z, dy_z, hb_z, hb_z, ab_z, gb, xc_z, dy_z, hf_z, hf_z, af_z, gf, s_b, wcat, lamcat)


def _conv_bwd(dxc_a, dxc_b, xa_z, conv_wz, dcw0, dcb0, name):
    lx = dxc_a.shape[0] // NH
    n = lx // T

    def body(dm_a, dp_a, dn_a, dm_b, dp_b, dn_b, xa_ref, cw, dcw0_ref, dcb0_ref, dxa_ref, dcw_ref, dcb_ref, pad, dxa_s):
        i = pl.program_id(0)

        @pl.when(i == 0)
        def _():
            dcw_ref[...] = dcw0_ref[...]
            dcb_ref[...] = dcb0_ref[...]

        pmask = jnp.where(_has_prev(i), 1.0, 0.0)
        nmask = jnp.where(_has_next(i, n), 1.0, 0.0)
        pad[pl.ds(0, 8 * NH), :] = (dp_a[...] + dp_b[...]) * pmask
        pad[pl.ds(8 * NH, T * NH), :] = dm_a[...] + dm_b[...]
        pad[pl.ds((T + 8) * NH, 8 * NH), :] = (dn_a[...] + dn_b[...]) * nmask

        def chunk(ci, carry):
            base = pl.multiple_of(ci * (CONV_CHUNK * NH), CONV_CHUNK * NH)
            xav = xa_ref[pl.ds(base, CONV_CHUNK * NH), :].reshape(CONV_CHUNK, NH, HD)
            acc = None
            for k in range(4):
                sl = pad[pl.ds(base + (9 - k) * NH, CONV_CHUNK * NH), :].reshape(CONV_CHUNK, NH, HD)
                term = sl * cw[k][None]
                acc = term if acc is None else acc + term
                dcw_ref[k] += jnp.sum(sl * xav, axis=0)
                if k == 1:
                    dcb_ref[...] += jnp.sum(sl, axis=0)
            dxa_s[pl.ds(base, CONV_CHUNK * NH), :] = acc.reshape(CONV_CHUNK * NH, HD)
            return carry
        lax.fori_loop(0, T // CONV_CHUNK, chunk, 0)
        for h in range(NH):
            dxa_ref[:, HD * h:HD * h + HD] = dxa_s[_zrows(h, T), :].astype(BF16)

    full = lambda shape: pl.BlockSpec(shape, lambda i: (0,) * len(shape))
    return pl.pallas_call(
        body, name=name, grid=(n,),
        out_shape=(jax.ShapeDtypeStruct((lx, D), BF16), jax.ShapeDtypeStruct((4, NH, HD), F32),
                   jax.ShapeDtypeStruct((NH, HD), F32)),
        in_specs=_tile_specs(NH, HD, n, lambda i: i) * 2 + [ZT, full((4, NH, HD)), full((4, NH, HD)), full((NH, HD))],
        out_specs=(pl.BlockSpec((T, D), lambda i: (i, 0)), full((4, NH, HD)), full((NH, HD))),
        scratch_shapes=[pltpu.VMEM(((T + 16) * NH, HD), F32), pltpu.VMEM((T * NH, HD), F32)],
        compiler_params=_cp(1, vmem_mb=48),
    )(dxc_a, dxc_a, dxc_a, dxc_b, dxc_b, dxc_b, xa_z, conv_wz, dcw0, dcb0)


def _proj_bwd(dzs, x, dxn, mods, mrow, norm_g, w_full, dng0, name, sgu=None, dz0_off=0):
    lx = x.shape[0]
    n = lx // T
    nz = len(dzs)
    has_x = dxn is not None
    wks = ([0, 1, 4, 2, 3] if sgu is not None else list(range(nz)))

    def body(*refs):
        it = iter(refs)
        take = lambda m: [next(it) for _ in range(m)]
        dz_refs, w_refs = take(nz), take(len(wks))
        x_ref, sc_ref, ng_ref, dng0_ref = take(4)
        dxn_ref = take(1)[0] if has_x else None
        if sgu is not None:
            u_ref, v_ref, dy_ref, g_ref, b_ref, sw_ref, bt_ref = take(7)
        gx_ref = take(1)[0] if has_x else None
        dng_ref, dsc_ref, dsh_ref = take(3)
        if sgu is not None:
            du_ref, dv_ref, dws_ref, dbs_ref, dlg_ref, dlb_ref, mixed_s, dvn_s = take(8)
        i = pl.program_id(0)

        @pl.when(i == 0)
        def _():
            dng_ref[...] = dng0_ref[...]
            dsc_ref[...] = jnp.zeros_like(dsc_ref)
            dsh_ref[...] = jnp.zeros_like(dsh_ref)
            if sgu is not None:
                for acc in (dws_ref, dbs_ref, dlg_ref, dlb_ref):
                    acc[...] = jnp.zeros_like(acc)

        dhn = _dot_nt(dz_refs[0][...], w_refs[0][...])
        for k in range(1, nz):
            dhn = dhn + _dot_nt(dz_refs[k][...], w_refs[k][...])
        if sgu is not None:
            for ch in range(T // HD):
                rows = slice(HD * ch, HD * ch + HD)
                du, dv = _sgu_bwd_chunk(u_ref[rows, :], v_ref[rows, :], dy_ref[rows, :], g_ref[...], b_ref[...],
                                        sw_ref, bt_ref, mixed_s, dvn_s, dws_ref, dbs_ref, dlg_ref, dlb_ref)
                du_ref[rows, :] = du
                dv_ref[rows, :] = dv
            dhn = dhn + _dot_nt(du_ref[...], w_refs[nz][...]) + _dot_nt(dv_ref[...], w_refs[nz + 1][...])
        xv = x_ref[...]
        r = lax.rsqrt(jnp.mean(xv * xv, axis=-1, keepdims=True) + NORM_EPS)
        xn = xv * r
        ng = ng_ref[...]
        sc1 = 1.0 + sc_ref[mrow:mrow + 1, :]
        t = dhn * xn
        dng_ref[...] += jnp.sum(t * sc1, axis=0, keepdims=True)
        dsc_ref[...] += jnp.sum(t * ng, axis=0, keepdims=True)
        dsh_ref[...] += jnp.sum(dhn, axis=0, keepdims=True)
        if has_x:
            dxh = dhn * (ng * sc1)
            gx_ref[...] = dxn_ref[...] + r * (dxh - xn * jnp.mean(dxh * xn, axis=-1, keepdims=True))

    row = pl.BlockSpec((T, D), lambda i: (i, 0))
    vec = pl.BlockSpec((1, D), lambda i: (0, 0))
    in_specs = [pl.BlockSpec((T, D), lambda i: (i + dz0_off, 0))] + [row] * (nz - 1)
    in_specs += [pl.BlockSpec((D, D), lambda i, k=k: (0, k)) for k in wks]
    in_specs += [row, pl.BlockSpec((8, D), lambda i: (0, 1)), vec, vec]
    args = list(dzs) + [w_full] * len(wks) + [x, mods, norm_g, dng0]
    vs = jax.ShapeDtypeStruct((1, D), F32)
    out_shape, out_specs = (vs, vs, vs), (vec, vec, vec)
    scratch = []
    if has_x:
        in_specs.append(row)
        args.append(dxn)
        out_shape = (jax.ShapeDtypeStruct((lx, D), F32),) + out_shape
        out_specs = (row,) + out_specs
    if sgu is not None:
        wsp = pl.BlockSpec((NH, HD, HD), lambda i: (0, 0, 0))
        bsp = pl.BlockSpec((NH, HD), lambda i: (0, 0))
        in_specs += [row, row, row, vec, vec, wsp, pl.BlockSpec((HD, NH), lambda i: (0, 0))]
        args += list(sgu)
        zb = jax.ShapeDtypeStruct((lx, D), BF16)
        out_shape += (zb, zb, jax.ShapeDtypeStruct((NH, HD, HD), F32), jax.ShapeDtypeStruct((NH, HD), F32), vs, vs)
        out_specs += (row, row, wsp, bsp, vec, vec)
        scratch = [pltpu.VMEM((HD, D), F32), pltpu.VMEM((HD, D), F32)]
    return pl.pallas_call(
        body, name=name, grid=(n,), out_shape=out_shape, in_specs=in_specs, out_specs=out_specs,
        scratch_shapes=scratch, compiler_params=_cp(1, vmem_mb=56),
    )(*args)


def _adam_math(w, g, m, v):
    m = ADAM_B1 * m + (1.0 - ADAM_B1) * g
    v = ADAM_B2 * v + (1.0 - ADAM_B2) * (g * g)
    m_hat = m / (1.0 - ADAM_B1 ** ADAM_STEP)
    v_hat = v / (1.0 - ADAM_B2 ** ADAM_STEP)
    delta = -ADAM_LR * (m_hat / (jnp.sqrt(v_hat) + ADAM_EPS) + ADAM_WD * w)
    return delta, m, v


def _adam_big(w, g, m, v, name):
    rows, cols = w.shape
    tr = 256

    def body(w_ref, g_ref, m_ref, v_ref, d_o, m_o, v_o):
        d, mm, vv = _adam_math(w_ref[...], g_ref[...], m_ref[...], v_ref[...])
        d_o[...] = d
        m_o[...] = mm
        v_o[...] = vv

    blk = pl.BlockSpec((tr, cols), lambda i: (i, 0))
    s = jax.ShapeDtypeStruct((rows, cols), F32)
    return pl.pallas_call(
        body, name=name, grid=(rows // tr,), out_shape=(s, s, s), in_specs=[blk] * 4, out_specs=(blk,) * 3,
        compiler_params=_cp(1, vmem_mb=48),
    )(w, g, m, v)


def _adam_small(items, tot):
    ni = len(items)
    pieces = [it[1] if isinstance(it[1], list) else None for it in items]
    flat = [a for it, pc in zip(items, pieces) for a in ((it[0], it[2], it[3]) if pc is not None else it)]
    n_in = len(flat) + 1
    out_shape = tuple(jax.ShapeDtypeStruct(it[0].shape, F32) for it, pc in zip(items, pieces)
                      for _ in range(4 if pc is not None else 3))
    n_out = len(out_shape)
    n_loads = sum(3 + (len(pc) if pc is not None else 1) for pc in pieces)

    def body(*refs):
        ins, tot_ref, outs = refs[:n_in - 1], refs[n_in - 1], refs[n_in:n_in + n_out]
        bufs = refs[n_in + n_out:n_in + n_out + 7 * ni]
        sem_in, sem_out = refs[n_in + n_out + 7 * ni:]
        loads, q_in, q_sem = [], 0, 0
        for k, pc in enumerate(pieces):
            w_b, g_b, m_b, v_b = bufs[7 * k:7 * k + 4]
            srcs = [(ins[q_in], w_b)]
            if pc is None:
                srcs.append((ins[q_in + 1], g_b))
                q_in += 1
            else:
                srcs += [(tot_ref.at[pl.ds(r0, nr), pl.ds(c0, nc)], g_b.at[pl.ds(d0, nr), :]) for r0, nr, c0, nc, d0 in pc]
            srcs += [(ins[q_in + 1], m_b), (ins[q_in + 2], v_b)]
            q_in += 3
            mine = []
            for src, dst in srcs:
                mine.append(pltpu.make_async_copy(src, dst, sem_in.at[q_sem]))
                q_sem += 1
            loads.append(mine)
        for mine in loads:
            for cp in mine:
                cp.start()
        stores, q_out = [], 0
        for k, pc in enumerate(pieces):
            for cp in loads[k]:
                cp.wait()
            w_b, g_b, m_b, v_b = bufs[7 * k:7 * k + 4]
            res = _adam_math(w_b[...], g_b[...], m_b[...], v_b[...])
            srcs = []
            for q in range(3):
                bufs[7 * k + 4 + q][...] = res[q]
                srcs.append(bufs[7 * k + 4 + q])
            if pc is not None:
                srcs.append(g_b)
            for src in srcs:
                cp = pltpu.make_async_copy(src, outs[q_out], sem_out.at[q_out])
                cp.start()
                stores.append(cp)
                q_out += 1
        for cp in stores:
            cp.wait()

    scratch = [pltpu.VMEM(it[0].shape, F32) for it in items for _ in range(7)]
    scratch += [pltpu.SemaphoreType.DMA((n_loads,)), pltpu.SemaphoreType.DMA((n_out,))]
    res = pl.pallas_call(
        body, name="adam_small", out_shape=out_shape, in_specs=[HBM] * n_in, out_specs=(HBM,) * n_out,
        scratch_shapes=scratch, compiler_params=_cp(vmem_mb=40),
    )(*flat, tot)
    outs, q = [], 0
    for pc in pieces:
        outs.append(tuple(res[q:q + 3]) + ((res[q + 3],) if pc is not None else (None,)))
        q += 4 if pc is not None else 3
    return outs


def kernel(x, c, ctx, c_ctx, ada_w, ada_b, norm_g, w_in, conv_w, conv_b, lru_wa, lru_ba, lru_wx, lru_bx, lru_lambda, sgu_ln_g, sgu_ln_b, sgu_w, sgu_b, w_out, final_g, loss_target, m_c_ctx, m_ada_w, m_ada_b, m_norm_g, m_w_in, m_conv_w, m_conv_b, m_lru_wa, m_lru_ba, m_lru_wx, m_lru_bx, m_lru_lambda, m_sgu_ln_g, m_sgu_ln_b, m_sgu_w, m_sgu_b, m_w_out, m_final_g, v_c_ctx, v_ada_w, v_ada_b, v_norm_g, v_w_in, v_conv_w, v_conv_b, v_lru_wa, v_lru_ba, v_lru_wx, v_lru_bx, v_lru_lambda, v_sgu_ln_g, v_sgu_ln_b, v_sgu_w, v_sgu_b, v_w_out, v_final_g):
    ix, iy, ic = lax.axis_index("x"), lax.axis_index("y"), lax.axis_index("c")
    chip = 2 * ix + iy
    dev = 2 * chip + ic
    lx = x.shape[1]
    lc = ctx.shape[1]

    smalls = jnp.concatenate([conv_w[0], lru_lambda[0], jnp.zeros((10, 256), F32)], axis=0)
    c_ctx2 = c_ctx.reshape(1, D)
    ada_b_j = lax.dynamic_slice(ada_b, (0, 768 * chip), (1, 768))
    mods, c_slots, sm_all, w_in_full, wo_land, ada_land = _gather_in(c, c_ctx2, ada_w[0], ada_b_j, w_in[0], w_out[0],
                                                                     smalls)
    wo_ss, wo_rs, ada_ss, ada_rs, wo_land, ada_land, token = _late_gather_start(wo_land, ada_land)
    mods = mods + token[0:1, 0:1]
    sm3 = sm_all.reshape(NCHIP, 16, 256)
    conv_w_full = sm3[:, 0:4, :].transpose(1, 0, 2).reshape(4, D)
    lam_full = sm3[:, 4:6, :].transpose(1, 0, 2).reshape(2, D)
    conv_wz = conv_w_full.reshape(4, NH, HD)
    conv_bz = conv_b.reshape(NH, HD)
    lamcat = lam_full.reshape(2, NH, HD).transpose(1, 0, 2).reshape(NH, 2 * HD)
    wa, wx, ba, bx = lru_wa[0], lru_wx[0], lru_ba[0], lru_bx[0]
    wcat = jnp.concatenate([wa[0], wx[0], wa[1], wx[1]], axis=-1).astype(BF16)
    bcat = jnp.concatenate([ba[0], bx[0], ba[1], bx[1]], axis=-1)
    sgu_wb = sgu_w[0].astype(BF16)
    sgu_bt = sgu_b[0].T
    final_g2 = final_g.reshape(1, D)

    zero_s = jnp.zeros((NH, HD), F32)
    assert lc == T
    hn, xa_lat, ga, u, v, gb, ys = _proj(x[0], mods, 0, norm_g, w_in_full, 5, "proj",
                                         (sgu_ln_g, sgu_ln_b, sgu_wb, sgu_bt), z0_rows=lc + lx, z0_off=1)
    hn_c, xa_all = _proj(ctx[0], mods, 1, norm_g, w_in_full, 1, "proj_ctx", z0_into=xa_lat)
    xaz, xcz, af, ab, hf, hb, gf, gb_l, _, hb0 = _lru_fwd(xa_all, conv_wz, conv_bz, wcat, bcat, lamcat, "lru_fwd")

    w_out_full = _late_gather_wait(wo_land, wo_ss, wo_rs, "w_out", hf, "late_gather_wait_w_out")
    (loss_part, dfg, dgx, dxn, y, do, dga, dgb, dyl_z, dys) = _out_fwd_bwd(
        hf, hb, ga, gb, ys, x[0], loss_target[0], mods, final_g2, w_out_full)

    dxc_b, dxc_f, dwc, dbc, dlc = _lru_bwd(xcz, dyl_z, hf, hb, af, ab, gf, gb_l, hb0, wcat, lamcat, "lru_bwd")
    dxa, dcw, dcb = _conv_bwd(dxc_b, dxc_f, xaz, conv_wz, jnp.zeros((4, NH, HD), F32), zero_s, "conv_bwd")

    grad_x, dng, dsc_x, dsh_x, du, dv, d_sgu_w, d_sgu_b, d_ln_g, d_ln_b = _proj_bwd(
        [dxa, dga, dgb], x[0], dxn, mods, 0, norm_g, w_in_full, jnp.zeros((1, D), F32), "proj_bwd",
        (u, v, dys, sgu_ln_g, sgu_ln_b, sgu_wb, sgu_bt), dz0_off=1)
    dzs = [dxa, dga, du, dv, dgb]
    dng, dsc_c, dsh_c = _proj_bwd([dxa], ctx[0], None, mods, 1, norm_g, w_in_full, dng, "proj_bwd_ctx")

    dmx = jnp.concatenate([dsh_x, dsc_x, dgx], axis=0)
    dmc = jnp.concatenate([dsh_c, dsc_c, jnp.zeros((1, D), F32)], axis=0)
    slot = jnp.concatenate([dmx, loss_part], axis=0)
    slots = lax.dynamic_update_slice(jnp.zeros((32, D), F32), slot, (4 * dev, 0))
    vecs = jnp.concatenate([dfg, dng, dcb.reshape(1, D), d_ln_g, d_ln_b, dcw.reshape(4, D), dmc,
                            jnp.zeros((4, D), F32), slots], axis=0)
    d_sgu_w4 = d_sgu_w.reshape(4, 256, HD).transpose(1, 0, 2).reshape(256, 4 * HD)
    pad8 = lambda a: jnp.pad(a, ((0, 8 - a.shape[0]), (0, 4 * HD - a.shape[1])))
    pack = jnp.concatenate([dwc.reshape(NH * HD, 4 * HD), pad8(dbc), pad8(dlc), d_sgu_w4, pad8(d_sgu_b),
                            vecs.reshape(96, 4 * HD), jnp.zeros((8, 4 * HD), F32)], axis=0)
    g_w_in, g_w_out, tot = _grads_reduce(hn, dzs, hn_c, y, do, pack)

    n_w = NH * HD
    g_lru_wa = [(0, n_w, 2 * HD * d, HD, n_w * d) for d in range(2)]
    g_lru_wx = [(0, n_w, 2 * HD * d + HD, HD, n_w * d) for d in range(2)]
    g_lru_ba = [(n_w, NH, 2 * HD * d, HD, NH * d) for d in range(2)]
    g_lru_bx = [(n_w, NH, 2 * HD * d + HD, HD, NH * d) for d in range(2)]
    g_sgu_w = [(1040, 256, HD * q, HD, 256 * q) for q in range(4)]
    g_sgu_b = [(1296, NH, 0, HD, 0)]
    g_lc = tot[1032:1040, 0:2 * HD]
    tv = tot[1304:1400].reshape(48, D)
    g_final_g, g_norm_g, g_conv_b, g_ln_g, g_ln_b = tv[0:1], tv[1:2], tv[2:3], tv[3:4], tv[4:5]
    g_conv_w_full = tv[5:9]
    dmc_tot = tv[9:12].reshape(1, 3 * D)
    slots_all = tv[16:48].reshape(8, 4, D)
    dmx_all = slots_all[:, 0:3, :].reshape(8, 3 * D)
    c_all = c_slots.reshape(8, 8, D)[:, 0, :]
    g_lam_full = jnp.stack([g_lc[:, 0:HD], g_lc[:, HD:2 * HD]]).reshape(2, D)
    g_conv_w = lax.dynamic_slice(g_conv_w_full, (0, 256 * chip), (4, 256))
    g_lam = lax.dynamic_slice(g_lam_full, (0, 256 * chip), (2, 256))
    dmx_all_j = lax.dynamic_slice(dmx_all, (0, 768 * chip), (8, 768))
    dmc_j = lax.dynamic_slice(dmc_tot, (0, 768 * chip), (1, 768))
    ada_full = _late_gather_wait(ada_land, ada_ss, ada_rs, "ada_w", tot, "late_gather_wait_ada_w")
    g_ada_w, g_ada_b, g_c_ctx = _ada_bwd(c_all, dmx_all_j, dmc_j, dmx_all, dmc_tot, c_ctx2, ada_full)

    big = {
        "ada_w": _adam_big(ada_w[0], g_ada_w, m_ada_w[0], v_ada_w[0], "adam_ada_w"),
        "w_in": _adam_big(w_in[0], g_w_in, m_w_in[0], v_w_in[0], "adam_w_in"),
        "w_out": _adam_big(w_out[0], g_w_out, m_w_out[0], v_w_out[0], "adam_w_out"),
    }
    small_in = {
        "c_ctx": (c_ctx, g_c_ctx, m_c_ctx, v_c_ctx, (1, D)),
        "ada_b": (ada_b, g_ada_b, m_ada_b, v_ada_b, (1, 3 * D)),
        "norm_g": (norm_g, g_norm_g, m_norm_g, v_norm_g, (1, D)),
        "conv_w": (conv_w, g_conv_w, m_conv_w, v_conv_w, (4, 256)),
        "conv_b": (conv_b, g_conv_b, m_conv_b, v_conv_b, (1, D)),
        "lru_wa": (lru_wa, g_lru_wa, m_lru_wa, v_lru_wa, (2 * NH * HD, HD)),
        "lru_ba": (lru_ba, g_lru_ba, m_lru_ba, v_lru_ba, (2 * NH, HD)),
        "lru_wx": (lru_wx, g_lru_wx, m_lru_wx, v_lru_wx, (2 * NH * HD, HD)),
        "lru_bx": (lru_bx, g_lru_bx, m_lru_bx, v_lru_bx, (2 * NH, HD)),
        "lru_lambda": (lru_lambda, g_lam, m_lru_lambda, v_lru_lambda, (2, 256)),
        "sgu_ln_g": (sgu_ln_g, g_ln_g, m_sgu_ln_g, v_sgu_ln_g, (1, D)),
        "sgu_ln_b": (sgu_ln_b, g_ln_b, m_sgu_ln_b, v_sgu_ln_b, (1, D)),
        "sgu_w": (sgu_w, g_sgu_w, m_sgu_w, v_sgu_w, (NH * HD, HD)),
        "sgu_b": (sgu_b, g_sgu_b, m_sgu_b, v_sgu_b, (NH, HD)),
        "final_g": (final_g, g_final_g, m_final_g, v_final_g, (1, D)),
    }
    names_small = list(small_in)
    res_small = _adam_small([tuple(a if isinstance(a, list) else a.reshape(small_in[k][4]) for a in small_in[k][:4])
                             for k in names_small], tot)
    full_shapes = {"ada_w": ada_w.shape, "w_in": w_in.shape, "w_out": w_out.shape}
    grads, deltas, new_m, new_v = {}, {}, {}, {}
    for k in ("ada_w", "w_in", "w_out"):
        g = {"ada_w": g_ada_w, "w_in": g_w_in, "w_out": g_w_out}[k]
        grads[k] = g.reshape(full_shapes[k])
        deltas[k], new_m[k], new_v[k] = (a.reshape(full_shapes[k]) for a in big[k])
    for k, res in zip(names_small, res_small):
        shape = small_in[k][0].shape
        grads[k] = (small_in[k][1] if res[3] is None else res[3]).reshape(shape)
        deltas[k], new_m[k], new_v[k] = (a.reshape(shape) for a in res[:3])

    loss = jnp.sum(slots_all[:, 3, 0:3])
    order = ["c_ctx", "ada_w", "ada_b", "norm_g", "w_in", "conv_w", "conv_b", "lru_wa", "lru_ba", "lru_wx", "lru_bx",
             "lru_lambda", "sgu_ln_g", "sgu_ln_b", "sgu_w", "sgu_b", "w_out", "final_g"]
    return (loss, grad_x.reshape(x.shape), *[grads[k] for k in order], *[deltas[k] for k in order],
            *[new_m[k] for k in order], *[new_v[k] for k in order])
```

```python
import jax
import jax.numpy as jnp
from jax import lax
from jax.experimental import pallas as pl
from jax.experimental.pallas import tpu as pltpu

F32 = jnp.float32
BF16 = jnp.bfloat16

D = 1024
NH = 8
HD = 128
NCHIP = 4
T = 256
NORM_EPS = 1e-6
LN_EPS = 1e-5
LRU_C = 8.0
ADAM_LR = 0.001
ADAM_B1 = 0.9
ADAM_B2 = 0.999
ADAM_EPS = 1e-08
ADAM_WD = 0.01
ADAM_STEP = 10

VMEM = pl.BlockSpec(memory_space=pltpu.VMEM)
ANY = pl.BlockSpec(memory_space=pl.ANY)
MESH = pl.DeviceIdType.MESH


def _cp(n_grid=0, vmem_mb=None):
    kw = {}
    if n_grid:
        kw["dimension_semantics"] = ("arbitrary",) * n_grid
    if vmem_mb:
        kw["vmem_limit_bytes"] = vmem_mb << 20
    return pltpu.CompilerParams(**kw)


def _sigmoid(x):
    return 0.5 * jnp.tanh(0.5 * x) + 0.5


def _silu_and_grad(x):
    s = _sigmoid(x)
    return x * s, s * (1.0 + x * (1.0 - s))


_GELU_K = 0.7978845608028654
_GELU_C = 0.044715


def _gelu_and_grad(x):
    x2 = x * x
    th = jnp.tanh(x * (_GELU_K + (_GELU_K * _GELU_C) * x2))
    p = 0.5 + 0.5 * th
    g = x * p
    dg = p + g * (1.0 - th) * (_GELU_K + (3.0 * _GELU_K * _GELU_C) * x2)
    return g, dg


def _softplus(x):
    return jnp.maximum(x, 0.0) + jnp.log1p(jnp.exp(-jnp.abs(x)))


def _lru_gate(pre, lam_row, d, off=None):
    off = 256 * d if off is None else off
    r = _sigmoid(pre[:, off:off + HD])
    gi = _sigmoid(pre[:, off + HD:off + 2 * HD])
    lam = lam_row[:, HD * d:HD * d + HD]
    sp = _softplus(-lam)
    la = (-LRU_C) * r * sp
    a = jnp.exp(la)
    x2 = 2.0 * la
    m2 = jnp.where(x2 > -1e-3, -x2 * (1.0 + 0.5 * x2), 1.0 - a * a)
    mult = jnp.sqrt(m2)
    return r, gi, lam, sp, a, mult


def _dot(a, b):
    return jnp.dot(a, b, preferred_element_type=F32)


def _dot_tn(a, b):
    return lax.dot_general(a, b, (((0,), (0,)), ((), ())), preferred_element_type=F32)


def _dot_nt(a, b):
    return lax.dot_general(a, b, (((1,), (1,)), ((), ())), preferred_element_type=F32)


def _mo(v, m):
    return v if isinstance(v, int) else pl.multiple_of(v, m)


def _zrows(h, n):
    return pl.ds(h, n, stride=NH)


def _gather_in(c, c_ctx, ada_w, ada_b_j, w_in, w_out, smalls):
    nch = [1, 4]
    wrows = lambda cc, q: (pl.ds(_mo(512 * cc, 16), 512) if q is None
                           else pl.ds(_mo(512 * cc + (512 // nch[1]) * q, 16), 512 // nch[1]))
    specs = [
        ((64, 256), F32, lambda r, jj, cc, q=None: r.at[pl.ds(_mo(16 * jj + 8 * cc, 8), 8), :]),
        ((D, 5120), BF16, lambda r, jj, cc, q=None: r.at[wrows(cc, q), pl.ds(_mo(1280 * jj, 128), 1280)]),
    ]
    halves = [lambda r, cc, q=None: r.at[pl.ds(_mo(8 * cc, 8), 8), :],
              lambda r, cc, q=None: r.at[wrows(cc, q), :]]
    na = len(specs)
    sem_base = [0, 6 * nch[0]]
    sidx = lambda a, q, k: sem_base[a] + 6 * q + k
    n_tiny = 6 * sum(nch)
    n_sem = n_tiny + 10

    def body(c_ref, cc_ref, ada_ref, adab_ref, win_ref, wout_ref, sm_ref,
             mods_o, call_o, sm_o, win_o, wol_o, adal_o, s_win, s_ada, s_wout, f_win, f_ada, f_wout, cslot, lhs, mbuf,
             send_sems, recv_sems, local_sems, load_sems):
        x, y, c = lax.axis_index("x"), lax.axis_index("y"), lax.axis_index("c")
        j = 2 * x + y
        dev = 2 * j + c
        sib = (x, y, 1 - c)
        chips = [(1 - x, y), (x, 1 - y), (1 - x, 1 - y)]
        cj = [2 * cx + cy for cx, cy in chips]
        outs = [sm_o, win_o]
        srcs = [sm_ref, s_win]

        def copy(idx, src, dst, to):
            return pltpu.make_async_remote_copy(src_ref=src, dst_ref=dst, send_sem=send_sems.at[idx],
                                                recv_sem=recv_sems.at[idx], device_id=to, device_id_type=MESH)

        sends = []

        def start(cp):
            cp.start()
            sends.append(cp)

        cslot[...] = jnp.zeros_like(cslot)
        cslot[0:1, :] = c_ref[...]
        my_slot = pl.ds(_mo(8 * dev, 8), 8)
        others = [sib] + [(*chips[k], c) for k in range(3)] + [(*chips[k], 1 - c) for k in range(3)]
        other_dev = [dev + 1 - 2 * c] + [2 * cj[k] + c for k in range(3)] + [2 * cj[k] + 1 - c for k in range(3)]
        base = n_tiny
        for r in range(7):
            start(copy(base + r, cslot, call_o.at[my_slot, :], others[r]))
        call_o[my_slot, :] = cslot[...]

        crow = 512 // nch[1]
        loads = []
        for cc in (c, 1 - c):
            for q in range(nch[1]):
                rows = pl.ds(_mo(512 * cc + crow * q, 16), crow)
                loads.append(pltpu.make_async_copy(win_ref.at[rows, :], f_win.at[rows, :], load_sems.at[len(loads)]))
        loads.append(pltpu.make_async_copy(ada_ref, f_ada, load_sems.at[len(loads)]))
        loads.append(pltpu.make_async_copy(wout_ref, f_wout, load_sems.at[len(loads)]))
        for ld in loads:
            ld.start()
        for k in range(2):
            start(copy(sidx(0, 0, k), halves[0](srcs[0], c), specs[0][2](outs[0], j, c), (*chips[k], c)))
        for q in range(nch[1]):
            loads[q].wait()
            rows = pl.ds(_mo(512 * c + crow * q, 16), crow)
            s_win[rows, :] = f_win[rows, :].astype(BF16)
            for k in range(2):
                start(copy(sidx(1, q, k), halves[1](s_win, c, q), specs[1][2](win_o, j, c, q), (*chips[k], c)))
        for q in range(nch[1]):
            loads[nch[1] + q].wait()
            rows = pl.ds(_mo(512 * (1 - c) + crow * q, 16), crow)
            s_win[rows, :] = f_win[rows, :].astype(BF16)
        local = []
        for a in range(na):
            for cc in range(2):
                lc = pltpu.make_async_copy(halves[a](srcs[a], cc), specs[a][2](outs[a], j, cc), local_sems.at[2 * a + cc])
                lc.start()
                local.append(lc)
        loads[2 * nch[1]].wait()
        s_ada[...] = f_ada[...].astype(BF16)
        loads[2 * nch[1] + 1].wait()
        s_wout[...] = f_wout[...].astype(BF16)
        for q, (src, dst) in enumerate([(s_wout, wol_o.at[pl.ds(_mo(512 * j, 16), 512), :]),
                                        (s_ada, adal_o.at[:, pl.ds(_mo(768 * j, 128), 768)])]):
            lc = pltpu.make_async_copy(src, dst, local_sems.at[2 * na + q])
            lc.start()
            local.append(lc)

        for r in range(7):
            slot = call_o.at[pl.ds(_mo(8 * other_dev[r], 8), 8), :]
            copy(base + r, slot, slot, sib).wait_recv()
        lhs[...] = jnp.zeros_like(lhs)
        for b in range(8):
            cv = call_o[8 * b:8 * b + 1, :]
            lhs[b:b + 1, :] = cv * _sigmoid(cv)
        cv = cc_ref[...]
        lhs[8:9, :] = cv * _sigmoid(cv)
        mbuf[j] = _dot(lhs[...].astype(BF16), s_ada[...]) + adab_ref[...]
        for k in range(3):
            start(copy(base + 7 + k, mbuf.at[j], mbuf.at[j], (*chips[k], c)))
        for k in range(3):
            copy(base + 7 + k, mbuf.at[cj[k]], mbuf.at[cj[k]], sib).wait_recv()
        mods_o[...] = jnp.zeros_like(mods_o)
        for jj in range(NCHIP):
            mods_o[0:1, 768 * jj:768 * jj + 768] = mbuf[jj, pl.ds(dev, 1), :]
            mods_o[1:2, 768 * jj:768 * jj + 768] = mbuf[jj, 8:9, :]

        kx = [1 - x, x, 1 - x]
        ky = [y, 1 - y, 1 - y]
        pick = lambda k, lst: jnp.where(k == 0, lst[0], jnp.where(k == 1, lst[1], lst[2]))
        for a in range(na):
            for q in range(nch[a]):
                for step, k in enumerate([c, 1 - c]):
                    reg = specs[a][2](outs[a], pick(k, cj), c, q)
                    copy(sidx(a, q, k), reg, reg, sib).wait_recv()
                    if step == 0:
                        start(copy(sidx(a, q, 2), reg, reg, (pick(1 - c, kx), pick(1 - c, ky), c)))
                    start(copy(sidx(a, q, 3 + k), reg, reg, sib))
        for a in range(na):
            for q in range(nch[a]):
                reg = specs[a][2](outs[a], cj[2], c, q)
                copy(sidx(a, q, 2), reg, reg, sib).wait_recv()
                start(copy(sidx(a, q, 5), reg, reg, sib))
        for a in range(na):
            for q in range(nch[a]):
                for k in range(3):
                    reg = specs[a][2](outs[a], cj[k], 1 - c, q)
                    copy(sidx(a, q, 3 + k), reg, reg, sib).wait_recv()
        for cp in sends:
            cp.wait_send()
        for lc in local:
            lc.wait()

    out_shape = (jax.ShapeDtypeStruct((8, 3 * D), F32), jax.ShapeDtypeStruct((64, D), F32),
                 jax.ShapeDtypeStruct(specs[0][0], F32), jax.ShapeDtypeStruct(specs[1][0], BF16),
                 jax.ShapeDtypeStruct((2048, D), BF16), jax.ShapeDtypeStruct((D, 3 * D), BF16))
    return pl.pallas_call(
        body, name="gather_in", out_shape=out_shape,
        in_specs=[VMEM, VMEM, ANY, VMEM, ANY, ANY, VMEM], out_specs=(VMEM, VMEM, VMEM, ANY, ANY, ANY),
        scratch_shapes=[pltpu.VMEM((D, 1280), BF16), pltpu.VMEM((D, 768), BF16), pltpu.VMEM((512, D), BF16),
                        pltpu.VMEM((D, 1280), F32), pltpu.VMEM((D, 768), F32), pltpu.VMEM((512, D), F32),
                        pltpu.VMEM((8, D), F32), pltpu.VMEM((16, D), F32), pltpu.VMEM((NCHIP, 16, 768), F32),
                        pltpu.SemaphoreType.DMA((n_sem,)), pltpu.SemaphoreType.DMA((n_sem,)),
                        pltpu.SemaphoreType.DMA((2 * na + 2,)), pltpu.SemaphoreType.DMA((2 * nch[1] + 2,))],
        compiler_params=_cp(vmem_mb=56),
    )(c, c_ctx, ada_w, ada_b_j, w_in, w_out, smalls)


HBM = pl.BlockSpec(memory_space=pltpu.HBM)
SEM = pl.BlockSpec(memory_space=pltpu.SEMAPHORE)


def _late_gather_regions(x, y, c):
    chips = [(1 - x, y), (x, 1 - y), (1 - x, 1 - y)]
    wo_reg = lambda r, jj, cc: r.at[pl.ds(_mo(512 * jj + 256 * cc, 16), 256), :]
    ada_reg = lambda r, jj, cc: r.at[pl.ds(_mo(512 * cc, 16), 512), pl.ds(_mo(768 * jj, 128), 768)]
    return chips, wo_reg, ada_reg


def _late_gather_start(wo_land, ada_land):
    def body(wol_ref, adal_ref, wo_ss, wo_rs, ada_ss, ada_rs, wol_thru, adal_thru, token):
        x, y, c = lax.axis_index("x"), lax.axis_index("y"), lax.axis_index("c")
        j = 2 * x + y
        chips, wo_reg, ada_reg = _late_gather_regions(x, y, c)
        for k in range(3):
            for cc in range(2):
                pltpu.make_async_remote_copy(src_ref=wo_reg(wol_ref, j, c), dst_ref=wo_reg(wol_ref, j, c),
                                             send_sem=wo_ss.at[2 * k + cc], recv_sem=wo_rs.at[2 * k + c],
                                             device_id=(*chips[k], cc), device_id_type=MESH).start()
        for k in range(3):
            for cc in range(2):
                pltpu.make_async_remote_copy(src_ref=ada_reg(adal_ref, j, c), dst_ref=ada_reg(adal_ref, j, c),
                                             send_sem=ada_ss.at[2 * k + cc], recv_sem=ada_rs.at[2 * k + c],
                                             device_id=(*chips[k], cc), device_id_type=MESH).start()
        token[...] = jnp.zeros_like(token)

    sems = pltpu.SemaphoreType.DMA((6,))
    return pl.pallas_call(
        body, name="late_gather_start",
        out_shape=(sems, sems, sems, sems, pltpu.HBM(wo_land.shape, BF16), pltpu.HBM(ada_land.shape, BF16),
                   jax.ShapeDtypeStruct((8, 128), F32)),
        in_specs=(HBM, HBM), out_specs=(SEM, SEM, SEM, SEM, HBM, HBM, VMEM), input_output_aliases={0: 4, 1: 5},
        compiler_params=pltpu.CompilerParams(has_side_effects=pltpu.SideEffectType.DATAFLOW_SIDE_EFFECTING),
    )(pltpu.with_memory_space_constraint(wo_land, pltpu.HBM), pltpu.with_memory_space_constraint(ada_land, pltpu.HBM))


def _late_gather_wait(land, send_sems, recv_sems, which, after, name):
    def body(land_ref, ss, rs, after_ref, land_out):
        x, y, c = lax.axis_index("x"), lax.axis_index("y"), lax.axis_index("c")
        j = 2 * x + y
        chips, wo_reg, ada_reg = _late_gather_regions(x, y, c)
        reg = wo_reg if which == "w_out" else ada_reg
        for k in range(3):
            kj = 2 * chips[k][0] + chips[k][1]
            for cc in range(2):
                cp = pltpu.make_async_remote_copy(src_ref=reg(land_ref, j, c), dst_ref=reg(land_ref, kj, cc),
                                                  send_sem=ss.at[2 * k + cc], recv_sem=rs.at[2 * k + cc],
                                                  device_id=(*chips[k], cc), device_id_type=MESH)
                cp.wait_send()
                cp.wait_recv()

    return pl.pallas_call(
        body, name=name, out_shape=pltpu.HBM(land.shape, land.dtype),
        in_specs=(HBM, SEM, SEM, ANY), out_specs=HBM, input_output_aliases={0: 0},
        compiler_params=pltpu.CompilerParams(has_side_effects=pltpu.SideEffectType.DATAFLOW_SIDE_EFFECTING),
    )(land, send_sems, recv_sems, after)


RCHUNK = 16


def _grads_reduce(hn, dzs, hn_c, y, do, pack):
    rp = pack.shape[0]
    hp = rp // 2
    assert hp % RCHUNK == 0
    wi_w = 1280
    lx, lc = hn.shape[0], hn_c.shape[0]
    lt = lx + lc
    n_dz = len(dzs)

    def body(*refs):
        hn_hbm, dz_hbm = refs[0], refs[1:1 + n_dz]
        hnc_hbm, y_hbm, do_hbm, pk_hbm, wi_out, wo_out, pk_out = refs[1 + n_dz:8 + n_dz]
        (hn_mine, hn_other, dzbuf, wi_other, wi_mine, wi_recv, wi_send, wi_rb,
         y_blk, do_mine, do_other, wo_other, wo_mine, wo_recv, wo_send, wo_rb,
         pk_mine, pk_recv, pk_send, pk_rb, pk_own, send_sems, recv_sems, local_sems) = refs[8 + n_dz:]
        x, y, c = lax.axis_index("x"), lax.axis_index("y"), lax.axis_index("c")
        j = 2 * x + y
        sib = (x, y, 1 - c)
        chips = [(1 - x, y), (x, 1 - y), (1 - x, 1 - y)]
        cj = [2 * cx + cy for cx, cy in chips]
        near = (jnp.where(c == 0, 1 - x, x), jnp.where(c == 0, y, 1 - y), c)
        slabs = [cj[2], cj[0], cj[1], j]

        def copy(k, src, dst, to):
            return pltpu.make_async_remote_copy(src_ref=src, dst_ref=dst, send_sem=send_sems.at[k],
                                                recv_sem=recv_sems.at[k], device_id=to, device_id_type=MESH)

        def local(k, src, dst):
            cp = pltpu.make_async_copy(src, dst, local_sems.at[k])
            cp.start()
            return cp

        rows_half = lambda r, cc, n: r.at[pl.ds(_mo(cc * n, 16), n), :]
        cols_half = lambda r, cc, n: r.at[:, pl.ds(_mo(cc * n, 128), n)]
        pk_piece = lambda r, cc, jj: r.at[pl.ds(_mo(cc * hp, 16), hp), pl.ds(_mo(jj * 128, 128), 128)]

        sends = []

        def start(cp):
            cp.start()
            sends.append(cp)

        def dz_pieces(s):
            g0 = wi_w * s
            k0, off0 = g0 // D, g0 % D
            w0 = min(D - off0, wi_w)
            pieces = [(k0, off0, w0, 0)]
            if w0 < wi_w:
                pieces.append((k0 + 1, 0, wi_w - w0, w0))
            return pieces

        def dz_copies(s):
            cps = []
            for q, (k, off, w, dst) in enumerate(dz_pieces(s)):
                cps.append(pltpu.make_async_copy(dz_hbm[k].at[pl.ds(lc if k == 0 else 0, lx), pl.ds(off, w)],
                                                 dzbuf.at[pl.ds(0, lx), pl.ds(dst, w)], local_sems.at[11 + q]))
            if s == 0:
                cps.append(pltpu.make_async_copy(dz_hbm[0].at[pl.ds(0, lc), :], dzbuf.at[pl.ds(lx, lc), pl.ds(0, D)],
                                                 local_sems.at[13]))
            return cps

        def dz_load(sl):
            for s in range(NCHIP):
                @pl.when(sl == s)
                def _():
                    if s == 0:
                        dzbuf[pl.ds(lx, lc), pl.ds(D, wi_w - D)] = jnp.zeros((lc, wi_w - D), BF16)
                    else:
                        dzbuf[pl.ds(lx, lc), :] = jnp.zeros((lc, wi_w), BF16)
                    for cp in dz_copies(s):
                        cp.start()

        def dz_wait(sl):
            for s in range(NCHIP):
                @pl.when(sl == s)
                def _():
                    for cp in dz_copies(s):
                        cp.wait()

        l_pk = local(0, rows_half(pk_hbm, c, hp), pk_mine)
        start(copy(0, rows_half(pk_hbm, 1 - c, hp), pk_recv, sib))
        col = lambda r, cc: r.at[:, pl.ds(_mo(cc * 512, 128), 512)]
        do_loads = [local(7, col(do_hbm, c), do_mine), local(14, col(do_hbm, 1 - c), do_other)]
        hn_loads = [local(2, col(hn_hbm, c), hn_mine.at[pl.ds(0, lx), :]),
                    local(3, col(hnc_hbm, c), hn_mine.at[pl.ds(lx, lc), :]),
                    local(4, col(hn_hbm, 1 - c), hn_other.at[pl.ds(0, lx), :]),
                    local(5, col(hnc_hbm, 1 - c), hn_other.at[pl.ds(lx, lc), :])]
        y_copy = lambda s: pltpu.make_async_copy(col(y_hbm, slabs[s]), y_blk, local_sems.at[1])
        y_copy(0).start()
        dz_load(slabs[0])

        def pair_sum(mine, recv, send, nrows, keep, relayed=None):
            def step(i, carry):
                rows = pl.ds(_mo(i * RCHUNK, RCHUNK), RCHUNK)
                s = mine[rows, :] + recv[rows, :].astype(F32)
                if relayed is not None:
                    s = s + relayed[rows, :].astype(F32)
                if keep:
                    mine[rows, :] = s
                if send is not None:
                    send[rows, :] = s.astype(BF16)
                return carry
            lax.fori_loop(0, nrows // RCHUNK, step, 0)

        def chip_sum(own, rb, nrows, terms=(0, 1, 2)):
            def step(i, carry):
                rows = pl.ds(_mo(i * RCHUNK, RCHUNK), RCHUNK)
                acc = own[rows, :]
                for q in terms:
                    acc = acc + rb[q, rows, :].astype(F32)
                own[rows, :] = acc
                return carry
            lax.fori_loop(0, nrows // RCHUNK, step, 0)

        w_in_g = dict(other=wi_other, mine=wi_mine, recv=wi_recv, send=wi_send, rb=wi_rb, p1_sems=(2, 3, 4, 5), p2_sem=12,
                      p1=[None] * NCHIP, wait_load=lambda s: dz_wait(slabs[s]), load=lambda s: dz_load(slabs[s]),
                      dot_other=lambda: _dot_tn(hn_other[...], dzbuf[...]), dot_mine=lambda: _dot_tn(hn_mine[...], dzbuf[...]))
        w_out_g = dict(other=wo_other, mine=wo_mine, recv=wo_recv, send=wo_send, rb=wo_rb, p1_sems=(1, 24, 25, 26), p2_sem=9,
                       p1=[None] * NCHIP, wait_load=lambda s: y_copy(s).wait(), load=lambda s: y_copy(s).start(),
                       dot_other=lambda: _dot_tn(y_blk[...], do_other[...]), dot_mine=lambda: _dot_tn(y_blk[...], do_mine[...]))

        def piece_matmuls(g, s):
            if s >= 2:
                g["p1"][s - 2].wait_send()
            g["wait_load"](s)
            g["other"][s % 2] = g["dot_other"]().astype(BF16)
            g["p1"][s] = copy(g["p1_sems"][s], g["other"].at[s % 2], g["recv"].at[s], sib)
            g["p1"][s].start()
            g["mine"][s % 2] = g["dot_mine"]()
            if s + 1 < NCHIP:
                g["load"](s + 1)

        def piece_finish(g, s):
            mine, recv, send, rb, p2 = g["mine"].at[s % 2], g["recv"].at[s], g["send"], g["rb"], g["p2_sem"]
            nrows = mine.shape[0]
            copy(g["p1_sems"][s], recv, recv, sib).wait_recv()
            if s == 3:
                pair_sum(mine, recv, None, nrows, True)
                return
            if s == 0:
                pair_sum(mine, recv, send.at[0], nrows, False)
                start(copy(p2, send.at[0], rb.at[0], near))
                return
            adds_relayed = c == (1 if s == 1 else 0)

            @pl.when(adds_relayed)
            def _():
                copy(p2, rb.at[0], rb.at[0], sib).wait_recv()
                pair_sum(mine, recv, send.at[s], nrows, False, rb.at[0])

            @pl.when(jnp.logical_not(adds_relayed))
            def _():
                pair_sum(mine, recv, send.at[s], nrows, False)
            start(copy(p2 + s, send.at[s], rb.at[s], (*chips[s - 1], c)))

        def piece_total(g):
            for k in (1, 2):
                copy(g["p2_sem"] + k, g["rb"].at[k], g["rb"].at[k], sib).wait_recv()
            chip_sum(g["mine"].at[1], g["rb"], g["mine"].shape[1], (1, 2))

        for cp in do_loads:
            cp.wait()
        piece_matmuls(w_out_g, 0)
        piece_matmuls(w_out_g, 1)
        piece_finish(w_out_g, 0)
        piece_matmuls(w_out_g, 2)
        piece_finish(w_out_g, 1)
        piece_matmuls(w_out_g, 3)
        piece_finish(w_out_g, 2)

        for cp in hn_loads:
            cp.wait()
        piece_matmuls(w_in_g, 0)

        l_pk.wait()
        copy(0, pk_recv, pk_recv, sib).wait_recv()
        pair_sum(pk_mine, pk_recv, pk_send, hp, True)
        for k in range(3):
            start(copy(6 + k, pk_send.at[:, pl.ds(_mo(cj[k] * 128, 128), 128)], pk_rb.at[k], (*chips[k], c)))
        l_pk_own = local(6, pk_mine.at[:, pl.ds(_mo(j * 128, 128), 128)], pk_own)

        piece_matmuls(w_in_g, 1)
        piece_finish(w_in_g, 0)

        l_pk_own.wait()
        for k in range(3):
            copy(6 + k, pk_rb.at[k], pk_rb.at[k], sib).wait_recv()
        chip_sum(pk_own, pk_rb, hp)
        l_pk_out = local(8, pk_own, pk_piece(pk_out, c, j))
        start(copy(15, pk_own, pk_piece(pk_out, c, j), sib))
        for k in range(2):
            start(copy(16 + k, pk_own, pk_piece(pk_out, c, j), (*chips[k], c)))

        piece_matmuls(w_in_g, 2)
        piece_finish(w_in_g, 1)
        piece_matmuls(w_in_g, 3)
        piece_finish(w_in_g, 2)

        piece_finish(w_out_g, 3)
        piece_total(w_out_g)
        l_wo_out = local(9, wo_mine.at[1], cols_half(wo_out, c, 512))
        start(copy(22, wo_mine.at[1], cols_half(wo_out, c, 512), sib))

        far = (jnp.where(c == 0, x, 1 - x), jnp.where(c == 0, 1 - y, y), c)
        for step, k in enumerate([c, 1 - c, 2]):
            reg = pk_piece(pk_out, c, jnp.where(k == 0, cj[0], jnp.where(k == 1, cj[1], cj[2])))
            copy(16 + k, reg, reg, sib).wait_recv()
            if step == 0:
                start(copy(18, reg, reg, far))
            start(copy(19 + k, reg, reg, sib))

        piece_finish(w_in_g, 3)
        piece_total(w_in_g)
        l_wi_out = local(10, wi_mine.at[1], rows_half(wi_out, c, 512))
        start(copy(23, wi_mine.at[1], rows_half(wi_out, c, 512), sib))

        reg = pk_piece(pk_out, 1 - c, j)
        copy(15, reg, reg, sib).wait_recv()
        for k in range(3):
            reg = pk_piece(pk_out, 1 - c, cj[k])
            copy(19 + k, reg, reg, sib).wait_recv()
        reg = cols_half(wo_out, 1 - c, 512)
        copy(22, reg, reg, sib).wait_recv()
        reg = rows_half(wi_out, 1 - c, 512)
        copy(23, reg, reg, sib).wait_recv()
        for cp in sends + w_in_g["p1"][2:] + w_out_g["p1"][2:]:
            cp.wait_send()
        for cp in (l_pk_out, l_wo_out, l_wi_out):
            cp.wait()

    return pl.pallas_call(
        body, name="grads_reduce",
        out_shape=(jax.ShapeDtypeStruct((D, wi_w), F32), jax.ShapeDtypeStruct((512, D), F32),
                   jax.ShapeDtypeStruct(pack.shape, F32)),
        in_specs=[ANY] * (5 + n_dz), out_specs=(ANY,) * 3,
        scratch_shapes=[
            pltpu.VMEM((lt, 512), BF16), pltpu.VMEM((lt, 512), BF16), pltpu.VMEM((lt, wi_w), BF16),
            pltpu.VMEM((2, 512, wi_w), BF16), pltpu.VMEM((2, 512, wi_w), F32), pltpu.VMEM((4, 512, wi_w), BF16),
            pltpu.VMEM((3, 512, wi_w), BF16), pltpu.VMEM((3, 512, wi_w), BF16),
            pltpu.VMEM((lx, 512), BF16), pltpu.VMEM((lx, 512), BF16), pltpu.VMEM((lx, 512), BF16),
            pltpu.VMEM((2, 512, 512), BF16), pltpu.VMEM((2, 512, 512), F32), pltpu.VMEM((4, 512, 512), BF16),
            pltpu.VMEM((3, 512, 512), BF16), pltpu.VMEM((3, 512, 512), BF16),
            pltpu.VMEM((hp, 512), F32), pltpu.VMEM((hp, 512), F32), pltpu.VMEM((hp, 512), BF16),
            pltpu.VMEM((3, hp, 128), BF16), pltpu.VMEM((hp, 128), F32),
            pltpu.SemaphoreType.DMA((27,)), pltpu.SemaphoreType.DMA((27,)), pltpu.SemaphoreType.DMA((15,))],
        compiler_params=_cp(vmem_mb=56),
    )(hn, *dzs, hn_c, y, do, pack)


def _ada_bwd(c_all, dmx_all_j, dmc_j, dmx_all, dmc, c_ctx, ada_w_full):
    def body(c_ref, dmxj_ref, dmcj_ref, dmx_ref, dmc_ref, cc_ref, w_ref, gw_ref, gb_ref, gc_ref, lhs, rhs, dm8):
        lhs[...] = jnp.zeros_like(lhs)
        rhs[...] = jnp.zeros_like(rhs)
        cv = c_ref[...]
        lhs[0:8, :] = cv * _sigmoid(cv)
        cc = cc_ref[...]
        a_c, da_c = _silu_and_grad(cc)
        lhs[8:9, :] = a_c
        rhs[0:8, :] = dmxj_ref[...]
        rhs[8:9, :] = dmcj_ref[...]
        gw_ref[...] = _dot_tn(lhs[...].astype(BF16), rhs[...].astype(BF16))
        gb_ref[...] = jnp.sum(dmx_ref[...], axis=0, keepdims=True) + dmc_ref[...]
        dm8[...] = jnp.zeros_like(dm8)
        dm8[0:1, :] = dmc_ref[...]
        da = _dot_nt(dm8[...].astype(BF16), w_ref[...])
        gc_ref[...] = da[0:1, :] * da_c

    return pl.pallas_call(
        body, name="ada_bwd",
        out_shape=(jax.ShapeDtypeStruct((D, 768), F32), jax.ShapeDtypeStruct((1, 3 * D), F32),
                   jax.ShapeDtypeStruct((1, D), F32)),
        in_specs=[VMEM] * 7, out_specs=(VMEM,) * 3,
        scratch_shapes=[pltpu.VMEM((16, D), F32), pltpu.VMEM((16, 768), F32), pltpu.VMEM((8, 3 * D), F32)],
        compiler_params=_cp(vmem_mb=32),
    )(c_all, dmx_all_j, dmc_j, dmx_all, dmc, c_ctx, ada_w_full)


def _proj(x, mods, mrow, norm_g, w_full, nk, name, sgu=None, z0_rows=None, z0_off=0, z0_into=None):
    lx = x.shape[0]
    n = lx // T
    order = [2, 3, 0, 1, 4] if sgu is not None else list(range(nk))

    def body(x_ref, sh_ref, sc_ref, ng_ref, *rest):
        w_refs, rest = rest[:nk], rest[nk:]
        if sgu is not None:
            g_ref, b_ref, sw_ref, bt_ref = rest[:4]
            rest = rest[4:]
        if z0_into is not None:
            rest = rest[1:]
        hn_ref, z_refs = rest[0], rest[1:1 + nk]
        xv = x_ref[...]
        r = lax.rsqrt(jnp.mean(xv * xv, axis=-1, keepdims=True) + NORM_EPS)
        hn = (xv * r) * ng_ref[...] * (1.0 + sc_ref[mrow:mrow + 1, :]) + sh_ref[mrow:mrow + 1, :]
        hb = hn.astype(BF16)
        hn_ref[...] = hb
        for k in order[:2]:
            z_refs[k][...] = _dot(hb, w_refs[k][...])
        if sgu is not None:
            ys_ref, mixed_s = rest[1 + nk], rest[2 + nk]
            for ch in range(T // HD):
                rows = slice(HD * ch, HD * ch + HD)
                ug = _sgu_parts(z_refs[2][rows, :], z_refs[3][rows, :], g_ref[...], b_ref[...], sw_ref, bt_ref,
                                mixed_s)[0]
                ys_ref[rows, :] = ug * mixed_s[...]
        for k in order[2:]:
            z_refs[k][...] = _dot(hb, w_refs[k][...])

    row = pl.BlockSpec((T, D), lambda i: (i, 0))
    vec = pl.BlockSpec((1, D), lambda i: (0, 0))
    in_specs = [row, pl.BlockSpec((8, D), lambda i: (0, 0)), pl.BlockSpec((8, D), lambda i: (0, 1)), vec]
    in_specs += [pl.BlockSpec((D, D), lambda i, k=k: (0, k)) for k in range(nk)]
    args = [x, mods, mods, norm_g] + [w_full] * nk
    out_shape = (jax.ShapeDtypeStruct((lx, D), BF16),) + tuple(jax.ShapeDtypeStruct((lx, D), F32) for _ in range(nk))
    out_specs = [row] * len(out_shape)
    scratch = []
    aliases = {}
    if sgu is not None:
        in_specs += [vec, vec, pl.BlockSpec((NH, HD, HD), lambda i: (0, 0, 0)), pl.BlockSpec((HD, NH), lambda i: (0, 0))]
        args += list(sgu)
        out_shape += (jax.ShapeDtypeStruct((lx, D), F32),)
        out_specs.append(row)
        scratch = [pltpu.VMEM((HD, D), F32)]
    if z0_rows is not None:
        out_shape = out_shape[:1] + (jax.ShapeDtypeStruct((z0_rows, D), F32),) + out_shape[2:]
        out_specs[1] = pl.BlockSpec((T, D), lambda i: (i + z0_off, 0))
    if z0_into is not None:
        out_shape = out_shape[:1] + (jax.ShapeDtypeStruct(z0_into.shape, F32),) + out_shape[2:]
        aliases = {len(args): 1}
        in_specs.append(ANY)
        args.append(z0_into)
    return pl.pallas_call(
        body, name=name, grid=(n,), out_shape=out_shape, in_specs=in_specs, out_specs=tuple(out_specs),
        scratch_shapes=scratch, input_output_aliases=aliases, compiler_params=_cp(1, vmem_mb=56),
    )(*args)


def _tile_specs(rows_per_pos, width, n_tiles, tile):
    last = n_tiles * (T // 8) - 1
    r = rows_per_pos
    return [pl.BlockSpec((T * r, width), lambda i: (tile(i), 0)),
            pl.BlockSpec((8 * r, width), lambda i: (jnp.maximum(tile(i) * (T // 8) - 1, 0), 0)),
            pl.BlockSpec((8 * r, width), lambda i: (jnp.minimum((tile(i) + 1) * (T // 8), last), 0))]


def _has_prev(tile):
    return tile >= 2


def _has_next(tile, nt):
    return jnp.logical_and(tile >= 1, tile < nt - 1)


ZT = pl.BlockSpec((T * NH, HD), lambda i: (i, 0))
CONV_CHUNK = 32


SCAN_SUB = 8


def _scan_tile(chains, post, carry_ref):
    blk = T // SCAN_SUB

    def step(k, state):
        new = []
        for ci, (a_ref, x_ref, o_ref, q_ref, reverse) in enumerate(chains):
            for q in range(SCAN_SUB):
                s, p = state[ci * SCAN_SUB + q]
                t = (q + 1) * blk - 1 - k if reverse else q * blk + k
                r = pl.ds(_mo(t * NH, NH), NH)
                a = a_ref[r, :]
                if post:
                    o = x_ref[r, :] + s
                    o_ref[r, :] = o
                    q_ref[r, :] = p
                    new.append((a * o, a * p))
                else:
                    o = a * s + x_ref[r, :]
                    p = a * p
                    o_ref[r, :] = o
                    q_ref[r, :] = p
                    new.append((o, p))
        return tuple(new)

    zero = jnp.zeros((NH, HD), F32)
    one = jnp.ones((NH, HD), F32)
    final = lax.fori_loop(0, blk, step, tuple((zero, one) for _ in range(len(chains) * SCAN_SUB)))
    for ci, (a_ref, x_ref, o_ref, q_ref, reverse) in enumerate(chains):
        carry = carry_ref[ci]
        for q in (range(SCAN_SUB - 1, -1, -1) if reverse else range(SCAN_SUB)):
            rows = pl.ds(q * blk * NH, blk * NH)
            fixed = o_ref[rows, :].reshape(blk, NH, HD) + q_ref[rows, :].reshape(blk, NH, HD) * carry[None]
            o_ref[rows, :] = fixed.reshape(blk * NH, HD)
            s_loc, p_loc = final[ci * SCAN_SUB + q]
            carry = s_loc + p_loc * carry
        carry_ref[ci] = carry


def _lru_fwd(xa, conv_wz, conv_bz, wcat, bcat, lamcat, name):
    lx = xa.shape[0]
    n = lx // T
    tile_u = lambda i: i
    tile_d = lambda i: jnp.where(i == 0, 0, n - i)

    def body(xm_u, xp_u, xn_u, xm_d, xp_d, xn_d, cw, cb, w_ref, b_ref, lam_ref,
             xaz_o, xcz_o, af_o, ab_o, hf_o, hb_o, gf_o, gb_o, fu, fd, pad, xc_d, x_u, x_d, q_u, q_d, carry):
        i = pl.program_id(0)

        @pl.when(i == 0)
        def _():
            carry[...] = jnp.zeros_like(carry)

        def conv_gates(xm, xp, xn, tile, d, xc_ref, a_ref, x_ref, xaz_ref, g_ref):
            pmask = jnp.where(_has_prev(tile), 1.0, 0.0)
            nmask = jnp.where(_has_next(tile, n), 1.0, 0.0)
            for h in range(NH):
                cols = slice(HD * h, HD * h + HD)
                pad[_zrows(h, 8), :] = xp[:, cols] * pmask
                pad[pl.ds(8 * NH + h, T, stride=NH), :] = xm[:, cols]
                pad[pl.ds((T + 8) * NH + h, 8, stride=NH), :] = xn[:, cols] * nmask
            if xaz_ref is not None:
                xaz_ref[...] = pad[pl.ds(8 * NH, T * NH), :]

            def conv_chunk(ci, c_):
                base = pl.multiple_of(ci * (CONV_CHUNK * NH), CONV_CHUNK * NH)
                acc = None
                for k in range(4):
                    sl = pad[pl.ds(base + (7 + k) * NH, CONV_CHUNK * NH), :].reshape(CONV_CHUNK, NH, HD)
                    term = sl * cw[k][None]
                    acc = term if acc is None else acc + term
                acc = acc + cb[...][None]
                xc_ref[pl.ds(base, CONV_CHUNK * NH), :] = acc.reshape(CONV_CHUNK * NH, HD)
                return c_
            lax.fori_loop(0, T // CONV_CHUNK, conv_chunk, 0)

            for h in range(NH):
                xch = xc_ref[_zrows(h, T), :]
                pre = _dot(xch.astype(BF16), w_ref[h, :, 256 * d:256 * d + 256]) + b_ref[h:h + 1, 256 * d:256 * d + 256]
                r, gi, _, _, a, mult = _lru_gate(pre, lam_ref[h:h + 1, :], d, 0)
                a_ref[_zrows(h, T), :] = a
                x_ref[_zrows(h, T), :] = mult * gi * xch
                for q, val in enumerate((r, gi, mult)):
                    g_ref[:, q * D + HD * h:q * D + HD * h + HD] = val

        conv_gates(xm_u, xp_u, xn_u, tile_u(i), 0, xcz_o, af_o, x_u, xaz_o, gf_o)
        conv_gates(xm_d, xp_d, xn_d, tile_d(i), 1, xc_d, ab_o, x_d, None, gb_o)

        _scan_tile([(af_o, x_u, hf_o, q_u, False), (ab_o, x_d, hb_o, q_d, True)], False, carry)

        @pl.when(i == 0)
        def _():
            fu[...] = carry[0]
            fd[...] = carry[1]

    full = lambda shape: pl.BlockSpec(shape, lambda i: (0,) * len(shape))
    st = full((NH, HD))
    in_specs = _tile_specs(1, D, n, tile_u) + _tile_specs(1, D, n, tile_d)
    in_specs += [full((4, NH, HD)), st, full((NH, HD, 4 * HD)), full((NH, 4 * HD)), full((NH, 2 * HD))]
    up = pl.BlockSpec((T * NH, HD), lambda i: (tile_u(i), 0))
    dn = pl.BlockSpec((T * NH, HD), lambda i: (tile_d(i), 0))
    zs = jax.ShapeDtypeStruct((lx * NH, HD), F32)
    ss = jax.ShapeDtypeStruct((NH, HD), F32)
    zbuf = pltpu.VMEM((T * NH, HD), F32)
    gs = jax.ShapeDtypeStruct((lx, 3 * D), F32)
    g_up = pl.BlockSpec((T, 3 * D), lambda i: (tile_u(i), 0))
    g_dn = pl.BlockSpec((T, 3 * D), lambda i: (tile_d(i), 0))
    return pl.pallas_call(
        body, name=name, grid=(n,), out_shape=(zs,) * 6 + (gs, gs, ss, ss), in_specs=in_specs,
        out_specs=(up, up, up, dn, up, dn, g_up, g_dn, st, st),
        scratch_shapes=[pltpu.VMEM(((T + 16) * NH, HD), F32), zbuf, zbuf, zbuf, zbuf, zbuf,
                        pltpu.VMEM((2, NH, HD), F32)],
        compiler_params=_cp(1, vmem_mb=48),
    )(xa, xa, xa, xa, xa, xa, conv_wz, conv_bz, wcat, bcat, lamcat)


def _sgu_parts(u, v, lng, lnb, w_ref, bt_ref, mixed_s):
    ug, dug = _gelu_and_grad(u)
    vg, dvg = _gelu_and_grad(v)
    mu = jnp.mean(vg, axis=-1, keepdims=True)
    vc = vg - mu
    rstd = lax.rsqrt(jnp.mean(vc * vc, axis=-1, keepdims=True) + LN_EPS)
    vh = vc * rstd
    vn = (vh * lng + lnb).astype(BF16)
    for g in range(NH):
        cols = slice(HD * g, HD * g + HD)
        mixed_s[:, cols] = _dot(w_ref[g], vn[:, cols]) + bt_ref[:, g:g + 1]
    return ug, dug, dvg, rstd, vh, vn


def _sgu_bwd_chunk(u, v, dys_v, lng, lnb, w_ref, bt_ref, mixed_s, dvn_s, dw_ref, db_ref, dg_ref, dbl_ref):
    ug, dug, dvg, rstd, vh, vn = _sgu_parts(u, v, lng, lnb, w_ref, bt_ref, mixed_s)
    du = (dys_v * mixed_s[...] * dug).astype(BF16)
    dmix = dys_v * ug
    ones = jnp.ones((8, HD), BF16)
    for g in range(NH):
        cols = slice(HD * g, HD * g + HD)
        dm = dmix[:, cols]
        hi = dm.astype(BF16)
        lo = (dm - hi.astype(F32)).astype(BF16)
        dw_ref[g] += _dot_nt(hi, vn[:, cols])
        db_ref[g:g + 1, :] += (_dot_nt(ones, hi) + _dot_nt(ones, lo))[0:1, :]
        dvn_s[:, cols] = _dot_tn(w_ref[g], hi)
    dvn = dvn_s[...]
    dg_ref[...] += jnp.sum(dvn * vh, axis=0, keepdims=True)
    dbl_ref[...] += jnp.sum(dvn, axis=0, keepdims=True)
    dvh = dvn * lng
    dvg_in = rstd * (dvh - jnp.mean(dvh, axis=-1, keepdims=True) - vh * jnp.mean(dvh * vh, axis=-1, keepdims=True))
    return du, (dvg_in * dvg).astype(BF16)


def _out_fwd_bwd(hf_z, hb_z, ga, gb, ys, x, tgt, mods, final_g, w_out_full):
    lx = x.shape[0]
    n = lx // T

    def body(hf_ref, hb_ref, ga_ref, gb_ref, ys_ref, x_ref, t_ref, gx_ref, fg_ref, w_ref,
             loss_ref, dfg_ref, dgx_ref, dxn_ref, y_ref, do_ref, dga_ref, dgb_ref, dyl_ref, dys_ref, yl_s):
        i = pl.program_id(0)

        @pl.when(i == 0)
        def _():
            loss_ref[...] = jnp.zeros_like(loss_ref)
            dfg_ref[...] = jnp.zeros_like(dfg_ref)
            dgx_ref[...] = jnp.zeros_like(dgx_ref)

        for h in range(NH):
            yl_s[:, HD * h:HD * h + HD] = hf_ref[_zrows(h, T), :] + hb_ref[_zrows(h, T), :]
        yl = yl_s[...]
        gav = ga_ref[...]
        gbv = gb_ref[...]
        sa, dsa = _silu_and_grad(gav)
        sb, dsb = _silu_and_grad(gbv)
        ysv = ys_ref[...]
        y_ref[:, 0:D] = (yl * sa).astype(BF16)
        y_ref[:, D:2 * D] = (ysv * sb).astype(BF16)
        o = _dot(y_ref[...], w_ref[...])
        gx = gx_ref[0:1, :]
        xnew = x_ref[...] + gx * o
        r2 = lax.rsqrt(jnp.mean(xnew * xnew, axis=-1, keepdims=True) + NORM_EPS)
        xh = xnew * r2
        fg = fg_ref[...]
        err = xh * fg - t_ref[...]
        loss_ref[...] += 0.5 * jnp.sum(jnp.mean(err * err, axis=-1, keepdims=True), axis=0, keepdims=True)

        @pl.when(i == n - 1)
        def _():
            lp = loss_ref[...]
            lp1 = lp.astype(BF16).astype(F32)
            lp2 = (lp - lp1).astype(BF16).astype(F32)
            lp3 = (lp - lp1 - lp2).astype(BF16).astype(F32)
            lane = lax.broadcasted_iota(jnp.int32, lp.shape, 1)
            loss_ref[...] = jnp.where(lane == 0, lp1, jnp.where(lane == 1, lp2, jnp.where(lane == 2, lp3, 0.0)))
        dout = err * (1.0 / D)
        dfg_ref[...] += jnp.sum(dout * xh, axis=0, keepdims=True)
        dxh = dout * fg
        dxn = r2 * (dxh - xh * jnp.mean(dxh * xh, axis=-1, keepdims=True))
        dxn_ref[...] = dxn
        dgx_ref[...] += jnp.sum(dxn * o, axis=0, keepdims=True)
        do = (dxn * gx).astype(BF16)
        do_ref[...] = do
        dy = _dot_nt(do, w_ref[...])
        dy1 = dy[:, 0:D]
        dy2 = dy[:, D:2 * D]
        dga_ref[...] = (dy1 * yl * dsa).astype(BF16)
        dgb_ref[...] = (dy2 * ysv * dsb).astype(BF16)
        dys_ref[...] = dy2 * sb
        yl_s[...] = dy1 * sa
        for h in range(NH):
            dyl_ref[_zrows(h, T), :] = yl_s[:, HD * h:HD * h + HD]

    row = pl.BlockSpec((T, D), lambda i: (i, 0))
    vec = pl.BlockSpec((1, D), lambda i: (0, 0))
    zlat = pl.BlockSpec((T * NH, HD), lambda i: (i + 1, 0))
    in_specs = [zlat, zlat, row, row, row, row, row, pl.BlockSpec((8, D), lambda i: (0, 2)), vec,
                pl.BlockSpec((2 * D, D), lambda i: (0, 0))]
    out_shape = (jax.ShapeDtypeStruct((1, D), F32), jax.ShapeDtypeStruct((1, D), F32), jax.ShapeDtypeStruct((1, D), F32),
                 jax.ShapeDtypeStruct((lx, D), F32), jax.ShapeDtypeStruct((lx, 2 * D), BF16),
                 jax.ShapeDtypeStruct((lx, D), BF16), jax.ShapeDtypeStruct((lx, D), BF16),
                 jax.ShapeDtypeStruct((lx, D), BF16), jax.ShapeDtypeStruct((lx * NH, HD), F32),
                 jax.ShapeDtypeStruct((lx, D), F32))
    out_specs = (vec, vec, vec, row, pl.BlockSpec((T, 2 * D), lambda i: (i, 0)),
                 row, row, row, ZT, row)
    return pl.pallas_call(
        body, name="out_fwd_bwd", grid=(n,), out_shape=out_shape, in_specs=in_specs, out_specs=out_specs,
        scratch_shapes=[pltpu.VMEM((T, D), F32)],
        compiler_params=_cp(1, vmem_mb=56),
    )(hf_z, hb_z, ga, gb, ys, x, tgt, mods, final_g, w_out_full)


def _lru_bwd(xc_z, dy_z, hf_z, hb_z, af_z, ab_z, gf, gb, s_b, wcat, lamcat, name):
    lx = xc_z.shape[0] // NH
    n = lx // T
    tile_u = lambda i: jnp.where(i == n - 1, 0, i + 1)
    tile_d = lambda i: n - 1 - i

    def body(xc_u, dy_u, hb_ref, hbn_ref, ab_ref, gb_ref, xc_d, dy_d, hf_ref, hfp_ref, af_ref, gf_ref,
             sb_ref, w_ref, lam_ref, dxcb_ref, dxcf_ref, dw_ref, db_ref, dl_ref,
             lb_s, lf_s, q_u, q_d, pf_s, pb_s, dpre_s, dyu_s, dyd_s, carry):
        i = pl.program_id(0)
        tu, td = tile_u(i), tile_d(i)

        @pl.when(i == 0)
        def _():
            dw_ref[...] = jnp.zeros_like(dw_ref)
            db_ref[...] = jnp.zeros_like(db_ref)
            dl_ref[...] = jnp.zeros_like(dl_ref)
            carry[...] = jnp.zeros_like(carry)

        dyu_s[...] = dy_u[...] * jnp.where(tu == 0, 0.0, 1.0)
        dyd_s[...] = dy_d[...] * jnp.where(td == 0, 0.0, 1.0)
        _scan_tile([(ab_ref, dyu_s, lb_s, q_u, False), (af_ref, dyd_s, lf_s, q_d, True)], True, carry)
        zero = jnp.zeros((NH, HD), F32)
        pb_s[pl.ds(0, T * NH), :] = hb_ref[...]
        pb_s[pl.ds(T * NH, NH), :] = jnp.where(tu == n - 1, sb_ref[...], jnp.where(tu == 0, zero, hbn_ref[pl.ds(0, NH), :]))
        pf_s[pl.ds(0, NH), :] = jnp.where(td == 0, zero, hfp_ref[pl.ds(7 * NH, NH), :])
        pf_s[pl.ds(NH, T * NH), :] = hf_ref[...]
        sides = ((1, xc_u, lb_s, pb_s, NH, ab_ref, gb_ref, dxcb_ref), (0, xc_d, lf_s, pf_s, 0, af_ref, gf_ref, dxcf_ref))
        for d, xc_ref, adj_s, prev_s, prev_off, a_ref, g_ref, dxc_ref in sides:
            wcols = slice(256 * d, 256 * d + 256)
            for h in range(NH):
                xch = xc_ref[_zrows(h, T), :]
                xcb = xch.astype(BF16)
                r, gi, mult = (g_ref[:, q * D + HD * h:q * D + HD * h + HD] for q in range(3))
                a = a_ref[_zrows(h, T), :]
                lam = lam_ref[h:h + 1, HD * d:HD * d + HD]
                sp = _softplus(-lam)
                du = adj_s[_zrows(h, T), :]
                da = du * prev_s[pl.ds(prev_off + h, T, stride=NH), :]
                dgi = du * mult * xch
                dmult = du * gi * xch
                dla = da * a - dmult * (a * a) / mult
                dr = dla * ((-LRU_C) * sp)
                dsp = jnp.sum(dla * ((-LRU_C) * r), axis=0, keepdims=True)
                dl_ref[h:h + 1, HD * d:HD * d + HD] += dsp * (-_sigmoid(-lam))
                dpre_s[:, 0:HD] = dr * r * (1.0 - r)
                dpre_s[:, HD:2 * HD] = dgi * gi * (1.0 - gi)
                dpre = dpre_s[...]
                dpb = dpre.astype(BF16)
                dw_ref[h, :, wcols] += _dot_tn(xcb, dpb)
                db_ref[h:h + 1, wcols] += jnp.sum(dpre, axis=0, keepdims=True)
                dxc_ref[_zrows(h, T), :] = du * mult * gi + _dot_nt(dpb, w_ref[h, :, wcols])

    full = lambda shape: pl.BlockSpec(shape, lambda i: (0,) * len(shape))
    wsp, bsp, lsp = full((NH, HD, 4 * HD)), full((NH, 4 * HD)), full((NH, 2 * HD))
    st = full((NH, HD))
    up = pl.BlockSpec((T * NH, HD), lambda i: (tile_u(i), 0))
    dn = pl.BlockSpec((T * NH, HD), lambda i: (tile_d(i), 0))
    dy_up = pl.BlockSpec((T * NH, HD), lambda i: (jnp.maximum(tile_u(i) - 1, 0), 0))
    dy_dn = pl.BlockSpec((T * NH, HD), lambda i: (jnp.maximum(tile_d(i) - 1, 0), 0))
    nxt = _tile_specs(NH, HD, n, tile_u)[2]
    prv = _tile_specs(NH, HD, n, tile_d)[1]
    g_up = pl.BlockSpec((T, 3 * D), lambda i: (tile_u(i), 0))
    g_dn = pl.BlockSpec((T, 3 * D), lambda i: (tile_d(i), 0))
    zs = jax.ShapeDtypeStruct((lx * NH, HD), F32)
    zbuf = pltpu.VMEM((T * NH, HD), F32)
    zbuf1 = pltpu.VMEM(((T + 1) * NH, HD), F32)
    return pl.pallas_call(
        body, name=name, grid=(n,),
        out_shape=(zs, zs, jax.ShapeDtypeStruct((NH, HD, 4 * HD), F32), jax.ShapeDtypeStruct((NH, 4 * HD), F32),
                   jax.ShapeDtypeStruct((NH, 2 * HD), F32)),
        in_specs=[up, dy_up, up, nxt, up, g_up, dn, dy_dn, dn, prv, dn, g_dn, st, wsp, lsp],
        out_specs=(up, dn, wsp, bsp, lsp),
        scratch_shapes=[zbuf, zbuf, zbuf, zbuf, zbuf1, zbuf1, pltpu.VMEM((T, 2 * HD), F32), zbuf, zbuf,
                        pltpu.VMEM((2, NH, HD), F32)],
        compiler_params=_cp(1, vmem_mb=56),
    )(xc_z, dy_z, hb_z, hb_z, ab_z, gb, xc_z, dy_z, hf_z, hf_z, af_z, gf, s_b, wcat, lamcat)


def _conv_bwd(dxc_a, dxc_b, xa_z, conv_wz, dcw0, dcb0, name):
    lx = dxc_a.shape[0] // NH
    n = lx // T

    def body(dm_a, dp_a, dn_a, dm_b, dp_b, dn_b, xa_ref, cw, dcw0_ref, dcb0_ref, dxa_ref, dcw_ref, dcb_ref, pad, dxa_s):
        i = pl.program_id(0)

        @pl.when(i == 0)
        def _():
            dcw_ref[...] = dcw0_ref[...]
            dcb_ref[...] = dcb0_ref[...]

        pmask = jnp.where(_has_prev(i), 1.0, 0.0)
        nmask = jnp.where(_has_next(i, n), 1.0, 0.0)
        pad[pl.ds(0, 8 * NH), :] = (dp_a[...] + dp_b[...]) * pmask
        pad[pl.ds(8 * NH, T * NH), :] = dm_a[...] + dm_b[...]
        pad[pl.ds((T + 8) * NH, 8 * NH), :] = (dn_a[...] + dn_b[...]) * nmask

        def chunk(ci, carry):
            base = pl.multiple_of(ci * (CONV_CHUNK * NH), CONV_CHUNK * NH)
            xav = xa_ref[pl.ds(base, CONV_CHUNK * NH), :].reshape(CONV_CHUNK, NH, HD)
            acc = None
            for k in range(4):
                sl = pad[pl.ds(base + (9 - k) * NH, CONV_CHUNK * NH), :].reshape(CONV_CHUNK, NH, HD)
                term = sl * cw[k][None]
                acc = term if acc is None else acc + term
                dcw_ref[k] += jnp.sum(sl * xav, axis=0)
                if k == 1:
                    dcb_ref[...] += jnp.sum(sl, axis=0)
            dxa_s[pl.ds(base, CONV_CHUNK * NH), :] = acc.reshape(CONV_CHUNK * NH, HD)
            return carry
        lax.fori_loop(0, T // CONV_CHUNK, chunk, 0)
        for h in range(NH):
            dxa_ref[:, HD * h:HD * h + HD] = dxa_s[_zrows(h, T), :].astype(BF16)

    full = lambda shape: pl.BlockSpec(shape, lambda i: (0,) * len(shape))
    return pl.pallas_call(
        body, name=name, grid=(n,),
        out_shape=(jax.ShapeDtypeStruct((lx, D), BF16), jax.ShapeDtypeStruct((4, NH, HD), F32),
                   jax.ShapeDtypeStruct((NH, HD), F32)),
        in_specs=_tile_specs(NH, HD, n, lambda i: i) * 2 + [ZT, full((4, NH, HD)), full((4, NH, HD)), full((NH, HD))],
        out_specs=(pl.BlockSpec((T, D), lambda i: (i, 0)), full((4, NH, HD)), full((NH, HD))),
        scratch_shapes=[pltpu.VMEM(((T + 16) * NH, HD), F32), pltpu.VMEM((T * NH, HD), F32)],
        compiler_params=_cp(1, vmem_mb=48),
    )(dxc_a, dxc_a, dxc_a, dxc_b, dxc_b, dxc_b, xa_z, conv_wz, dcw0, dcb0)


def _proj_bwd(dzs, x, dxn, mods, mrow, norm_g, w_full, dng0, name, sgu=None, dz0_off=0):
    lx = x.shape[0]
    n = lx // T
    nz = len(dzs)
    has_x = dxn is not None
    wks = ([0, 1, 4, 2, 3] if sgu is not None else list(range(nz)))

    def body(*refs):
        it = iter(refs)
        take = lambda m: [next(it) for _ in range(m)]
        dz_refs, w_refs = take(nz), take(len(wks))
        x_ref, sc_ref, ng_ref, dng0_ref = take(4)
        dxn_ref = take(1)[0] if has_x else None
        if sgu is not None:
            u_ref, v_ref, dy_ref, g_ref, b_ref, sw_ref, bt_ref = take(7)
        gx_ref = take(1)[0] if has_x else None
        dng_ref, dsc_ref, dsh_ref = take(3)
        if sgu is not None:
            du_ref, dv_ref, dws_ref, dbs_ref, dlg_ref, dlb_ref, mixed_s, dvn_s = take(8)
        i = pl.program_id(0)

        @pl.when(i == 0)
        def _():
            dng_ref[...] = dng0_ref[...]
            dsc_ref[...] = jnp.zeros_like(dsc_ref)
            dsh_ref[...] = jnp.zeros_like(dsh_ref)
            if sgu is not None:
                for acc in (dws_ref, dbs_ref, dlg_ref, dlb_ref):
                    acc[...] = jnp.zeros_like(acc)

        dhn = _dot_nt(dz_refs[0][...], w_refs[0][...])
        for k in range(1, nz):
            dhn = dhn + _dot_nt(dz_refs[k][...], w_refs[k][...])
        if sgu is not None:
            for ch in range(T // HD):
                rows = slice(HD * ch, HD * ch + HD)
                du, dv = _sgu_bwd_chunk(u_ref[rows, :], v_ref[rows, :], dy_ref[rows, :], g_ref[...], b_ref[...],
                                        sw_ref, bt_ref, mixed_s, dvn_s, dws_ref, dbs_ref, dlg_ref, dlb_ref)
                du_ref[rows, :] = du
                dv_ref[rows, :] = dv
            dhn = dhn + _dot_nt(du_ref[...], w_refs[nz][...]) + _dot_nt(dv_ref[...], w_refs[nz + 1][...])
        xv = x_ref[...]
        r = lax.rsqrt(jnp.mean(xv * xv, axis=-1, keepdims=True) + NORM_EPS)
        xn = xv * r
        ng = ng_ref[...]
        sc1 = 1.0 + sc_ref[mrow:mrow + 1, :]
        t = dhn * xn
        dng_ref[...] += jnp.sum(t * sc1, axis=0, keepdims=True)
        dsc_ref[...] += jnp.sum(t * ng, axis=0, keepdims=True)
        dsh_ref[...] += jnp.sum(dhn, axis=0, keepdims=True)
        if has_x:
            dxh = dhn * (ng * sc1)
            gx_ref[...] = dxn_ref[...] + r * (dxh - xn * jnp.mean(dxh * xn, axis=-1, keepdims=True))

    row = pl.BlockSpec((T, D), lambda i: (i, 0))
    vec = pl.BlockSpec((1, D), lambda i: (0, 0))
    in_specs = [pl.BlockSpec((T, D), lambda i: (i + dz0_off, 0))] + [row] * (nz - 1)
    in_specs += [pl.BlockSpec((D, D), lambda i, k=k: (0, k)) for k in wks]
    in_specs += [row, pl.BlockSpec((8, D), lambda i: (0, 1)), vec, vec]
    args = list(dzs) + [w_full] * len(wks) + [x, mods, norm_g, dng0]
    vs = jax.ShapeDtypeStruct((1, D), F32)
    out_shape, out_specs = (vs, vs, vs), (vec, vec, vec)
    scratch = []
    if has_x:
        in_specs.append(row)
        args.append(dxn)
        out_shape = (jax.ShapeDtypeStruct((lx, D), F32),) + out_shape
        out_specs = (row,) + out_specs
    if sgu is not None:
        wsp = pl.BlockSpec((NH, HD, HD), lambda i: (0, 0, 0))
        bsp = pl.BlockSpec((NH, HD), lambda i: (0, 0))
        in_specs += [row, row, row, vec, vec, wsp, pl.BlockSpec((HD, NH), lambda i: (0, 0))]
        args += list(sgu)
        zb = jax.ShapeDtypeStruct((lx, D), BF16)
        out_shape += (zb, zb, jax.ShapeDtypeStruct((NH, HD, HD), F32), jax.ShapeDtypeStruct((NH, HD), F32), vs, vs)
        out_specs += (row, row, wsp, bsp, vec, vec)
        scratch = [pltpu.VMEM((HD, D), F32), pltpu.VMEM((HD, D), F32)]
    return pl.pallas_call(
        body, name=name, grid=(n,), out_shape=out_shape, in_specs=in_specs, out_specs=out_specs,
        scratch_shapes=scratch, compiler_params=_cp(1, vmem_mb=56),
    )(*args)


def _adam_math(w, g, m, v):
    m = ADAM_B1 * m + (1.0 - ADAM_B1) * g
    v = ADAM_B2 * v + (1.0 - ADAM_B2) * (g * g)
    m_hat = m / (1.0 - ADAM_B1 ** ADAM_STEP)
    v_hat = v / (1.0 - ADAM_B2 ** ADAM_STEP)
    delta = -ADAM_LR * (m_hat / (jnp.sqrt(v_hat) + ADAM_EPS) + ADAM_WD * w)
    return delta, m, v


def _adam_big(w, g, m, v, name):
    rows, cols = w.shape
    tr = 256

    def body(w_ref, g_ref, m_ref, v_ref, d_o, m_o, v_o):
        d, mm, vv = _adam_math(w_ref[...], g_ref[...], m_ref[...], v_ref[...])
        d_o[...] = d
        m_o[...] = mm
        v_o[...] = vv

    blk = pl.BlockSpec((tr, cols), lambda i: (i, 0))
    s = jax.ShapeDtypeStruct((rows, cols), F32)
    return pl.pallas_call(
        body, name=name, grid=(rows // tr,), out_shape=(s, s, s), in_specs=[blk] * 4, out_specs=(blk,) * 3,
        compiler_params=_cp(1, vmem_mb=48),
    )(w, g, m, v)


def _adam_small(items, tot):
    ni = len(items)
    pieces = [it[1] if isinstance(it[1], list) else None for it in items]
    flat = [a for it, pc in zip(items, pieces) for a in ((it[0], it[2], it[3]) if pc is not None else it)]
    n_in = len(flat) + 1
    out_shape = tuple(jax.ShapeDtypeStruct(it[0].shape, F32) for it, pc in zip(items, pieces)
                      for _ in range(4 if pc is not None else 3))
    n_out = len(out_shape)
    n_loads = sum(3 + (len(pc) if pc is not None else 1) for pc in pieces)

    def body(*refs):
        ins, tot_ref, outs = refs[:n_in - 1], refs[n_in - 1], refs[n_in:n_in + n_out]
        bufs = refs[n_in + n_out:n_in + n_out + 7 * ni]
        sem_in, sem_out = refs[n_in + n_out + 7 * ni:]
        loads, q_in, q_sem = [], 0, 0
        for k, pc in enumerate(pieces):
            w_b, g_b, m_b, v_b = bufs[7 * k:7 * k + 4]
            srcs = [(ins[q_in], w_b)]
            if pc is None:
                srcs.append((ins[q_in + 1], g_b))
                q_in += 1
            else:
                srcs += [(tot_ref.at[pl.ds(r0, nr), pl.ds(c0, nc)], g_b.at[pl.ds(d0, nr), :]) for r0, nr, c0, nc, d0 in pc]
            srcs += [(ins[q_in + 1], m_b), (ins[q_in + 2], v_b)]
            q_in += 3
            mine = []
            for src, dst in srcs:
                mine.append(pltpu.make_async_copy(src, dst, sem_in.at[q_sem]))
                q_sem += 1
            loads.append(mine)
        for mine in loads:
            for cp in mine:
                cp.start()
        stores, q_out = [], 0
        for k, pc in enumerate(pieces):
            for cp in loads[k]:
                cp.wait()
            w_b, g_b, m_b, v_b = bufs[7 * k:7 * k + 4]
            res = _adam_math(w_b[...], g_b[...], m_b[...], v_b[...])
            srcs = []
            for q in range(3):
                bufs[7 * k + 4 + q][...] = res[q]
                srcs.append(bufs[7 * k + 4 + q])
            if pc is not None:
                srcs.append(g_b)
            for src in srcs:
                cp = pltpu.make_async_copy(src, outs[q_out], sem_out.at[q_out])
                cp.start()
                stores.append(cp)
                q_out += 1
        for cp in stores:
            cp.wait()

    scratch = [pltpu.VMEM(it[0].shape, F32) for it in items for _ in range(7)]
    scratch += [pltpu.SemaphoreType.DMA((n_loads,)), pltpu.SemaphoreType.DMA((n_out,))]
    res = pl.pallas_call(
        body, name="adam_small", out_shape=out_shape, in_specs=[HBM] * n_in, out_specs=(HBM,) * n_out,
        scratch_shapes=scratch, compiler_params=_cp(vmem_mb=40),
    )(*flat, tot)
    outs, q = [], 0
    for pc in pieces:
        outs.append(tuple(res[q:q + 3]) + ((res[q + 3],) if pc is not None else (None,)))
        q += 4 if pc is not None else 3
    return outs


def kernel(x, c, ctx, c_ctx, ada_w, ada_b, norm_g, w_in, conv_w, conv_b, lru_wa, lru_ba, lru_wx, lru_bx, lru_lambda, sgu_ln_g, sgu_ln_b, sgu_w, sgu_b, w_out, final_g, loss_target, m_c_ctx, m_ada_w, m_ada_b, m_norm_g, m_w_in, m_conv_w, m_conv_b, m_lru_wa, m_lru_ba, m_lru_wx, m_lru_bx, m_lru_lambda, m_sgu_ln_g, m_sgu_ln_b, m_sgu_w, m_sgu_b, m_w_out, m_final_g, v_c_ctx, v_ada_w, v_ada_b, v_norm_g, v_w_in, v_conv_w, v_conv_b, v_lru_wa, v_lru_ba, v_lru_wx, v_lru_bx, v_lru_lambda, v_sgu_ln_g, v_sgu_ln_b, v_sgu_w, v_sgu_b, v_w_out, v_final_g):
    ix, iy, ic = lax.axis_index("x"), lax.axis_index("y"), lax.axis_index("c")
    chip = 2 * ix + iy
    dev = 2 * chip + ic
    lx = x.shape[1]
    lc = ctx.shape[1]

    smalls = jnp.concatenate([conv_w[0], lru_lambda[0], jnp.zeros((10, 256), F32)], axis=0)
    c_ctx2 = c_ctx.reshape(1, D)
    ada_b_j = lax.dynamic_slice(ada_b, (0, 768 * chip), (1, 768))
    mods, c_slots, sm_all, w_in_full, wo_land, ada_land = _gather_in(c, c_ctx2, ada_w[0], ada_b_j, w_in[0], w_out[0],
                                                                     smalls)
    wo_ss, wo_rs, ada_ss, ada_rs, wo_land, ada_land, token = _late_gather_start(wo_land, ada_land)
    mods = mods + token[0:1, 0:1]
    sm3 = sm_all.reshape(NCHIP, 16, 256)
    conv_w_full = sm3[:, 0:4, :].transpose(1, 0, 2).reshape(4, D)
    lam_full = sm3[:, 4:6, :].transpose(1, 0, 2).reshape(2, D)
    conv_wz = conv_w_full.reshape(4, NH, HD)
    conv_bz = conv_b.reshape(NH, HD)
    lamcat = lam_full.reshape(2, NH, HD).transpose(1, 0, 2).reshape(NH, 2 * HD)
    wa, wx, ba, bx = lru_wa[0], lru_wx[0], lru_ba[0], lru_bx[0]
    wcat = jnp.concatenate([wa[0], wx[0], wa[1], wx[1]], axis=-1).astype(BF16)
    bcat = jnp.concatenate([ba[0], bx[0], ba[1], bx[1]], axis=-1)
    sgu_wb = sgu_w[0].astype(BF16)
    sgu_bt = sgu_b[0].T
    final_g2 = final_g.reshape(1, D)

    zero_s = jnp.zeros((NH, HD), F32)
    assert lc == T
    hn, xa_lat, ga, u, v, gb, ys = _proj(x[0], mods, 0, norm_g, w_in_full, 5, "proj",
                                         (sgu_ln_g, sgu_ln_b, sgu_wb, sgu_bt), z0_rows=lc + lx, z0_off=1)
    hn_c, xa_all = _proj(ctx[0], mods, 1, norm_g, w_in_full, 1, "proj_ctx", z0_into=xa_lat)
    xaz, xcz, af, ab, hf, hb, gf, gb_l, _, hb0 = _lru_fwd(xa_all, conv_wz, conv_bz, wcat, bcat, lamcat, "lru_fwd")

    w_out_full = _late_gather_wait(wo_land, wo_ss, wo_rs, "w_out", hf, "late_gather_wait_w_out")
    (loss_part, dfg, dgx, dxn, y, do, dga, dgb, dyl_z, dys) = _out_fwd_bwd(
        hf, hb, ga, gb, ys, x[0], loss_target[0], mods, final_g2, w_out_full)

    dxc_b, dxc_f, dwc, dbc, dlc = _lru_bwd(xcz, dyl_z, hf, hb, af, ab, gf, gb_l, hb0, wcat, lamcat, "lru_bwd")
    dxa, dcw, dcb = _conv_bwd(dxc_b, dxc_f, xaz, conv_wz, jnp.zeros((4, NH, HD), F32), zero_s, "conv_bwd")

    grad_x, dng, dsc_x, dsh_x, du, dv, d_sgu_w, d_sgu_b, d_ln_g, d_ln_b = _proj_bwd(
        [dxa, dga, dgb], x[0], dxn, mods, 0, norm_g, w_in_full, jnp.zeros((1, D), F32), "proj_bwd",
        (u, v, dys, sgu_ln_g, sgu_ln_b, sgu_wb, sgu_bt), dz0_off=1)
    dzs = [dxa, dga, du, dv, dgb]
    dng, dsc_c, dsh_c = _proj_bwd([dxa], ctx[0], None, mods, 1, norm_g, w_in_full, dng, "proj_bwd_ctx")

    dmx = jnp.concatenate([dsh_x, dsc_x, dgx], axis=0)
    dmc = jnp.concatenate([dsh_c, dsc_c, jnp.zeros((1, D), F32)], axis=0)
    slot = jnp.concatenate([dmx, loss_part], axis=0)
    slots = lax.dynamic_update_slice(jnp.zeros((32, D), F32), slot, (4 * dev, 0))
    vecs = jnp.concatenate([dfg, dng, dcb.reshape(1, D), d_ln_g, d_ln_b, dcw.reshape(4, D), dmc,
                            jnp.zeros((4, D), F32), slots], axis=0)
    d_sgu_w4 = d_sgu_w.reshape(4, 256, HD).transpose(1, 0, 2).reshape(256, 4 * HD)
    pad8 = lambda a: jnp.pad(a, ((0, 8 - a.shape[0]), (0, 4 * HD - a.shape[1])))
    pack = jnp.concatenate([dwc.reshape(NH * HD, 4 * HD), pad8(dbc), pad8(dlc), d_sgu_w4, pad8(d_sgu_b),
                            vecs.reshape(96, 4 * HD), jnp.zeros((8, 4 * HD), F32)], axis=0)
    g_w_in, g_w_out, tot = _grads_reduce(hn, dzs, hn_c, y, do, pack)

    n_w = NH * HD
    g_lru_wa = [(0, n_w, 2 * HD * d, HD, n_w * d) for d in range(2)]
    g_lru_wx = [(0, n_w, 2 * HD * d + HD, HD, n_w * d) for d in range(2)]
    g_lru_ba = [(n_w, NH, 2 * HD * d, HD, NH * d) for d in range(2)]
    g_lru_bx = [(n_w, NH, 2 * HD * d + HD, HD, NH * d) for d in range(2)]
    g_sgu_w = [(1040, 256, HD * q, HD, 256 * q) for q in range(4)]
    g_sgu_b = [(1296, NH, 0, HD, 0)]
    g_lc = tot[1032:1040, 0:2 * HD]
    tv = tot[1304:1400].reshape(48, D)
    g_final_g, g_norm_g, g_conv_b, g_ln_g, g_ln_b = tv[0:1], tv[1:2], tv[2:3], tv[3:4], tv[4:5]
    g_conv_w_full = tv[5:9]
    dmc_tot = tv[9:12].reshape(1, 3 * D)
    slots_all = tv[16:48].reshape(8, 4, D)
    dmx_all = slots_all[:, 0:3, :].reshape(8, 3 * D)
    c_all = c_slots.reshape(8, 8, D)[:, 0, :]
    g_lam_full = jnp.stack([g_lc[:, 0:HD], g_lc[:, HD:2 * HD]]).reshape(2, D)
    g_conv_w = lax.dynamic_slice(g_conv_w_full, (0, 256 * chip), (4, 256))
    g_lam = lax.dynamic_slice(g_lam_full, (0, 256 * chip), (2, 256))
    dmx_all_j = lax.dynamic_slice(dmx_all, (0, 768 * chip), (8, 768))
    dmc_j = lax.dynamic_slice(dmc_tot, (0, 768 * chip), (1, 768))
    ada_full = _late_gather_wait(ada_land, ada_ss, ada_rs, "ada_w", tot, "late_gather_wait_ada_w")
    g_ada_w, g_ada_b, g_c_ctx = _ada_bwd(c_all, dmx_all_j, dmc_j, dmx_all, dmc_tot, c_ctx2, ada_full)

    big = {
        "ada_w": _adam_big(ada_w[0], g_ada_w, m_ada_w[0], v_ada_w[0], "adam_ada_w"),
        "w_in": _adam_big(w_in[0], g_w_in, m_w_in[0], v_w_in[0], "adam_w_in"),
        "w_out": _adam_big(w_out[0], g_w_out, m_w_out[0], v_w_out[0], "adam_w_out"),
    }
    small_in = {
        "c_ctx": (c_ctx, g_c_ctx, m_c_ctx, v_c_ctx, (1, D)),
        "ada_b": (ada_b, g_ada_b, m_ada_b, v_ada_b, (1, 3 * D)),
        "norm_g": (norm_g, g_norm_g, m_norm_g, v_norm_g, (1, D)),
        "conv_w": (conv_w, g_conv_w, m_conv_w, v_conv_w, (4, 256)),
        "conv_b": (conv_b, g_conv_b, m_conv_b, v_conv_b, (1, D)),
        "lru_wa": (lru_wa, g_lru_wa, m_lru_wa, v_lru_wa, (2 * NH * HD, HD)),
        "lru_ba": (lru_ba, g_lru_ba, m_lru_ba, v_lru_ba, (2 * NH, HD)),
        "lru_wx": (lru_wx, g_lru_wx, m_lru_wx, v_lru_wx, (2 * NH * HD, HD)),
        "lru_bx": (lru_bx, g_lru_bx, m_lru_bx, v_lru_bx, (2 * NH, HD)),
        "lru_lambda": (lru_lambda, g_lam, m_lru_lambda, v_lru_lambda, (2, 256)),
        "sgu_ln_g": (sgu_ln_g, g_ln_g, m_sgu_ln_g, v_sgu_ln_g, (1, D)),
        "sgu_ln_b": (sgu_ln_b, g_ln_b, m_sgu_ln_b, v_sgu_ln_b, (1, D)),
        "sgu_w": (sgu_w, g_sgu_w, m_sgu_w, v_sgu_w, (NH * HD, HD)),
        "sgu_b": (sgu_b, g_sgu_b, m_sgu_b, v_sgu_b, (NH, HD)),
        "final_g": (final_g, g_final_g, m_final_g, v_final_g, (1, D)),
    }
    names_small = list(small_in)
    res_small = _adam_small([tuple(a if isinstance(a, list) else a.reshape(small_in[k][4]) for a in small_in[k][:4])
                             for k in names_small], tot)
    full_shapes = {"ada_w": ada_w.shape, "w_in": w_in.shape, "w_out": w_out.shape}
    grads, deltas, new_m, new_v = {}, {}, {}, {}
    for k in ("ada_w", "w_in", "w_out"):
        g = {"ada_w": g_ada_w, "w_in": g_w_in, "w_out": g_w_out}[k]
        grads[k] = g.reshape(full_shapes[k])
        deltas[k], new_m[k], new_v[k] = (a.reshape(full_shapes[k]) for a in big[k])
    for k, res in zip(names_small, res_small):
        shape = small_in[k][0].shape
        grads[k] = (small_in[k][1] if res[3] is None else res[3]).reshape(shape)
        deltas[k], new_m[k], new_v[k] = (a.reshape(shape) for a in res[:3])

    loss = jnp.sum(slots_all[:, 3, 0:3])
    order = ["c_ctx", "ada_w", "ada_b", "norm_g", "w_in", "conv_w", "conv_b", "lru_wa", "lru_ba", "lru_wx", "lru_bx",
             "lru_lambda", "sgu_ln_g", "sgu_ln_b", "sgu_w", "sgu_b", "w_out", "final_g"]
    return (loss, grad_x.reshape(x.shape), *[grads[k] for k in order], *[deltas[k] for k in order],
            *[new_m[k] for k in order], *[new_v[k] for k in order])
```

```python
import jax
import jax.numpy as jnp
from jax import lax
from jax.experimental import pallas as pl
from jax.experimental.pallas import tpu as pltpu

F32 = jnp.float32
BF16 = jnp.bfloat16

D = 1024
NH = 8
HD = 128
NCHIP = 4
T = 256
NORM_EPS = 1e-6
LN_EPS = 1e-5
LRU_C = 8.0
ADAM_LR = 0.001
ADAM_B1 = 0.9
ADAM_B2 = 0.999
ADAM_EPS = 1e-08
ADAM_WD = 0.01
ADAM_STEP = 10

VMEM = pl.BlockSpec(memory_space=pltpu.VMEM)
ANY = pl.BlockSpec(memory_space=pl.ANY)
MESH = pl.DeviceIdType.MESH


def _cp(n_grid=0, vmem_mb=None):
    kw = {}
    if n_grid:
        kw["dimension_semantics"] = ("arbitrary",) * n_grid
    if vmem_mb:
        kw["vmem_limit_bytes"] = vmem_mb << 20
    return pltpu.CompilerParams(**kw)


def _sigmoid(x):
    return 0.5 * jnp.tanh(0.5 * x) + 0.5


def _silu_and_grad(x):
    s = _sigmoid(x)
    return x * s, s * (1.0 + x * (1.0 - s))


_GELU_K = 0.7978845608028654
_GELU_C = 0.044715


def _gelu_and_grad(x):
    x2 = x * x
    th = jnp.tanh(x * (_GELU_K + (_GELU_K * _GELU_C) * x2))
    p = 0.5 + 0.5 * th
    g = x * p
    dg = p + g * (1.0 - th) * (_GELU_K + (3.0 * _GELU_K * _GELU_C) * x2)
    return g, dg


def _softplus(x):
    return jnp.maximum(x, 0.0) + jnp.log1p(jnp.exp(-jnp.abs(x)))


def _lru_gate(pre, lam_row, d, off=None):
    off = 256 * d if off is None else off
    r = _sigmoid(pre[:, off:off + HD])
    gi = _sigmoid(pre[:, off + HD:off + 2 * HD])
    lam = lam_row[:, HD * d:HD * d + HD]
    sp = _softplus(-lam)
    la = (-LRU_C) * r * sp
    a = jnp.exp(la)
    x2 = 2.0 * la
    m2 = jnp.where(x2 > -1e-3, -x2 * (1.0 + 0.5 * x2), 1.0 - a * a)
    mult = jnp.sqrt(m2)
    return r, gi, lam, sp, a, mult


def _dot(a, b):
    return jnp.dot(a, b, preferred_element_type=F32)


def _dot_tn(a, b):
    return lax.dot_general(a, b, (((0,), (0,)), ((), ())), preferred_element_type=F32)


def _dot_nt(a, b):
    return lax.dot_general(a, b, (((1,), (1,)), ((), ())), preferred_element_type=F32)


def _mo(v, m):
    return v if isinstance(v, int) else pl.multiple_of(v, m)


def _zrows(h, n):
    return pl.ds(h, n, stride=NH)


def _gather_in(c, c_ctx, ada_w, ada_b_j, w_in, w_out, smalls):
    nch = [1, 4]
    wrows = lambda cc, q: (pl.ds(_mo(512 * cc, 16), 512) if q is None
                           else pl.ds(_mo(512 * cc + (512 // nch[1]) * q, 16), 512 // nch[1]))
    specs = [
        ((64, 256), F32, lambda r, jj, cc, q=None: r.at[pl.ds(_mo(16 * jj + 8 * cc, 8), 8), :]),
        ((D, 5120), BF16, lambda r, jj, cc, q=None: r.at[wrows(cc, q), pl.ds(_mo(1280 * jj, 128), 1280)]),
    ]
    halves = [lambda r, cc, q=None: r.at[pl.ds(_mo(8 * cc, 8), 8), :],
              lambda r, cc, q=None: r.at[wrows(cc, q), :]]
    na = len(specs)
    sem_base = [0, 6 * nch[0]]
    sidx = lambda a, q, k: sem_base[a] + 6 * q + k
    n_tiny = 6 * sum(nch)
    n_sem = n_tiny + 10

    def body(c_ref, cc_ref, ada_ref, adab_ref, win_ref, wout_ref, sm_ref,
             mods_o, call_o, sm_o, win_o, wol_o, adal_o, s_win, s_ada, s_wout, f_win, f_ada, f_wout, cslot, lhs, mbuf,
             send_sems, recv_sems, local_sems, load_sems):
        x, y, c = lax.axis_index("x"), lax.axis_index("y"), lax.axis_index("c")
        j = 2 * x + y
        dev = 2 * j + c
        sib = (x, y, 1 - c)
        chips = [(1 - x, y), (x, 1 - y), (1 - x, 1 - y)]
        cj = [2 * cx + cy for cx, cy in chips]
        outs = [sm_o, win_o]
        srcs = [sm_ref, s_win]

        def copy(idx, src, dst, to):
            return pltpu.make_async_remote_copy(src_ref=src, dst_ref=dst, send_sem=send_sems.at[idx],
                                                recv_sem=recv_sems.at[idx], device_id=to, device_id_type=MESH)

        sends = []

        def start(cp):
            cp.start()
            sends.append(cp)

        cslot[...] = jnp.zeros_like(cslot)
        cslot[0:1, :] = c_ref[...]
        my_slot = pl.ds(_mo(8 * dev, 8), 8)
        others = [sib] + [(*chips[k], c) for k in range(3)] + [(*chips[k], 1 - c) for k in range(3)]
        other_dev = [dev + 1 - 2 * c] + [2 * cj[k] + c for k in range(3)] + [2 * cj[k] + 1 - c for k in range(3)]
        base = n_tiny
        for r in range(7):
            start(copy(base + r, cslot, call_o.at[my_slot, :], others[r]))
        call_o[my_slot, :] = cslot[...]

        crow = 512 // nch[1]
        loads = []
        for cc in (c, 1 - c):
            for q in range(nch[1]):
                rows = pl.ds(_mo(512 * cc + crow * q, 16), crow)
                loads.append(pltpu.make_async_copy(win_ref.at[rows, :], f_win.at[rows, :], load_sems.at[len(loads)]))
        loads.append(pltpu.make_async_copy(ada_ref, f_ada, load_sems.at[len(loads)]))
        loads.append(pltpu.make_async_copy(wout_ref, f_wout, load_sems.at[len(loads)]))
        for ld in loads:
            ld.start()
        for k in range(2):
            start(copy(sidx(0, 0, k), halves[0](srcs[0], c), specs[0][2](outs[0], j, c), (*chips[k], c)))
        for q in range(nch[1]):
            loads[q].wait()
            rows = pl.ds(_mo(512 * c + crow * q, 16), crow)
            s_win[rows, :] = f_win[rows, :].astype(BF16)
            for k in range(2):
                start(copy(sidx(1, q, k), halves[1](s_win, c, q), specs[1][2](win_o, j, c, q), (*chips[k], c)))
        for q in range(nch[1]):
            loads[nch[1] + q].wait()
            rows = pl.ds(_mo(512 * (1 - c) + crow * q, 16), crow)
            s_win[rows, :] = f_win[rows, :].astype(BF16)
        local = []
        for a in range(na):
            for cc in range(2):
                lc = pltpu.make_async_copy(halves[a](srcs[a], cc), specs[a][2](outs[a], j, cc), local_sems.at[2 * a + cc])
                lc.start()
                local.append(lc)
        loads[2 * nch[1]].wait()
        s_ada[...] = f_ada[...].astype(BF16)
        loads[2 * nch[1] + 1].wait()
        s_wout[...] = f_wout[...].astype(BF16)
        for q, (src, dst) in enumerate([(s_wout, wol_o.at[pl.ds(_mo(512 * j, 16), 512), :]),
                                        (s_ada, adal_o.at[:, pl.ds(_mo(768 * j, 128), 768)])]):
            lc = pltpu.make_async_copy(src, dst, local_sems.at[2 * na + q])
            lc.start()
            local.append(lc)

        for r in range(7):
            slot = call_o.at[pl.ds(_mo(8 * other_dev[r], 8), 8), :]
            copy(base + r, slot, slot, sib).wait_recv()
        lhs[...] = jnp.zeros_like(lhs)
        for b in range(8):
            cv = call_o[8 * b:8 * b + 1, :]
            lhs[b:b + 1, :] = cv * _sigmoid(cv)
        cv = cc_ref[...]
        lhs[8:9, :] = cv * _sigmoid(cv)
        mbuf[j] = _dot(lhs[...].astype(BF16), s_ada[...]) + adab_ref[...]
        for k in range(3):
            start(copy(base + 7 + k, mbuf.at[j], mbuf.at[j], (*chips[k], c)))
        for k in range(3):
            copy(base + 7 + k, mbuf.at[cj[k]], mbuf.at[cj[k]], sib).wait_recv()
        mods_o[...] = jnp.zeros_like(mods_o)
        for jj in range(NCHIP):
            mods_o[0:1, 768 * jj:768 * jj + 768] = mbuf[jj, pl.ds(dev, 1), :]
            mods_o[1:2, 768 * jj:768 * jj + 768] = mbuf[jj, 8:9, :]

        kx = [1 - x, x, 1 - x]
        ky = [y, 1 - y, 1 - y]
        pick = lambda k, lst: jnp.where(k == 0, lst[0], jnp.where(k == 1, lst[1], lst[2]))
        for a in range(na):
            for q in range(nch[a]):
                for step, k in enumerate([c, 1 - c]):
                    reg = specs[a][2](outs[a], pick(k, cj), c, q)
                    copy(sidx(a, q, k), reg, reg, sib).wait_recv()
                    if step == 0:
                        start(copy(sidx(a, q, 2), reg, reg, (pick(1 - c, kx), pick(1 - c, ky), c)))
                    start(copy(sidx(a, q, 3 + k), reg, reg, sib))
        for a in range(na):
            for q in range(nch[a]):
                reg = specs[a][2](outs[a], cj[2], c, q)
                copy(sidx(a, q, 2), reg, reg, sib).wait_recv()
                start(copy(sidx(a, q, 5), reg, reg, sib))
        for a in range(na):
            for q in range(nch[a]):
                for k in range(3):
                    reg = specs[a][2](outs[a], cj[k], 1 - c, q)
                    copy(sidx(a, q, 3 + k), reg, reg, sib).wait_recv()
        for cp in sends:
            cp.wait_send()
        for lc in local:
            lc.wait()

    out_shape = (jax.ShapeDtypeStruct((8, 3 * D), F32), jax.ShapeDtypeStruct((64, D), F32),
                 jax.ShapeDtypeStruct(specs[0][0], F32), jax.ShapeDtypeStruct(specs[1][0], BF16),
                 jax.ShapeDtypeStruct((2048, D), BF16), jax.ShapeDtypeStruct((D, 3 * D), BF16))
    return pl.pallas_call(
        body, name="gather_in", out_shape=out_shape,
        in_specs=[VMEM, VMEM, ANY, VMEM, ANY, ANY, VMEM], out_specs=(VMEM, VMEM, VMEM, ANY, ANY, ANY),
        scratch_shapes=[pltpu.VMEM((D, 1280), BF16), pltpu.VMEM((D, 768), BF16), pltpu.VMEM((512, D), BF16),
                        pltpu.VMEM((D, 1280), F32), pltpu.VMEM((D, 768), F32), pltpu.VMEM((512, D), F32),
                        pltpu.VMEM((8, D), F32), pltpu.VMEM((16, D), F32), pltpu.VMEM((NCHIP, 16, 768), F32),
                        pltpu.SemaphoreType.DMA((n_sem,)), pltpu.SemaphoreType.DMA((n_sem,)),
                        pltpu.SemaphoreType.DMA((2 * na + 2,)), pltpu.SemaphoreType.DMA((2 * nch[1] + 2,))],
        compiler_params=_cp(vmem_mb=56),
    )(c, c_ctx, ada_w, ada_b_j, w_in, w_out, smalls)


HBM = pl.BlockSpec(memory_space=pltpu.HBM)
SEM = pl.BlockSpec(memory_space=pltpu.SEMAPHORE)


def _late_gather_regions(x, y, c):
    chips = [(1 - x, y), (x, 1 - y), (1 - x, 1 - y)]
    wo_reg = lambda r, jj, cc: r.at[pl.ds(_mo(512 * jj + 256 * cc, 16), 256), :]
    ada_reg = lambda r, jj, cc: r.at[pl.ds(_mo(512 * cc, 16), 512), pl.ds(_mo(768 * jj, 128), 768)]
    return chips, wo_reg, ada_reg


def _late_gather_start(wo_land, ada_land):
    def body(wol_ref, adal_ref, wo_ss, wo_rs, ada_ss, ada_rs, wol_thru, adal_thru, token):
        x, y, c = lax.axis_index("x"), lax.axis_index("y"), lax.axis_index("c")
        j = 2 * x + y
        chips, wo_reg, ada_reg = _late_gather_regions(x, y, c)
        for k in range(3):
            for cc in range(2):
                pltpu.make_async_remote_copy(src_ref=wo_reg(wol_ref, j, c), dst_ref=wo_reg(wol_ref, j, c),
                                             send_sem=wo_ss.at[2 * k + cc], recv_sem=wo_rs.at[2 * k + c],
                                             device_id=(*chips[k], cc), device_id_type=MESH).start()
        for k in range(3):
            for cc in range(2):
                pltpu.make_async_remote_copy(src_ref=ada_reg(adal_ref, j, c), dst_ref=ada_reg(adal_ref, j, c),
                                             send_sem=ada_ss.at[2 * k + cc], recv_sem=ada_rs.at[2 * k + c],
                                             device_id=(*chips[k], cc), device_id_type=MESH).start()
        token[...] = jnp.zeros_like(token)

    sems = pltpu.SemaphoreType.DMA((6,))
    return pl.pallas_call(
        body, name="late_gather_start",
        out_shape=(sems, sems, sems, sems, pltpu.HBM(wo_land.shape, BF16), pltpu.HBM(ada_land.shape, BF16),
                   jax.ShapeDtypeStruct((8, 128), F32)),
        in_specs=(HBM, HBM), out_specs=(SEM, SEM, SEM, SEM, HBM, HBM, VMEM), input_output_aliases={0: 4, 1: 5},
        compiler_params=pltpu.CompilerParams(has_side_effects=pltpu.SideEffectType.DATAFLOW_SIDE_EFFECTING),
    )(pltpu.with_memory_space_constraint(wo_land, pltpu.HBM), pltpu.with_memory_space_constraint(ada_land, pltpu.HBM))


def _late_gather_wait(land, send_sems, recv_sems, which, after, name):
    def body(land_ref, ss, rs, after_ref, land_out):
        x, y, c = lax.axis_index("x"), lax.axis_index("y"), lax.axis_index("c")
        j = 2 * x + y
        chips, wo_reg, ada_reg = _late_gather_regions(x, y, c)
        reg = wo_reg if which == "w_out" else ada_reg
        for k in range(3):
            kj = 2 * chips[k][0] + chips[k][1]
            for cc in range(2):
                cp = pltpu.make_async_remote_copy(src_ref=reg(land_ref, j, c), dst_ref=reg(land_ref, kj, cc),
                                                  send_sem=ss.at[2 * k + cc], recv_sem=rs.at[2 * k + cc],
                                                  device_id=(*chips[k], cc), device_id_type=MESH)
                cp.wait_send()
                cp.wait_recv()

    return pl.pallas_call(
        body, name=name, out_shape=pltpu.HBM(land.shape, land.dtype),
        in_specs=(HBM, SEM, SEM, ANY), out_specs=HBM, input_output_aliases={0: 0},
        compiler_params=pltpu.CompilerParams(has_side_effects=pltpu.SideEffectType.DATAFLOW_SIDE_EFFECTING),
    )(land, send_sems, recv_sems, after)


RCHUNK = 16


def _grads_reduce(hn, dzs, hn_c, y, do, pack):
    rp = pack.shape[0]
    hp = rp // 2
    assert hp % RCHUNK == 0
    wi_w = 1280
    lx, lc = hn.shape[0], hn_c.shape[0]
    lt = lx + lc
    n_dz = len(dzs)

    def body(*refs):
        hn_hbm, dz_hbm = refs[0], refs[1:1 + n_dz]
        hnc_hbm, y_hbm, do_hbm, pk_hbm, wi_out, wo_out, pk_out = refs[1 + n_dz:8 + n_dz]
        (hn_mine, hn_other, dzbuf, wi_other, wi_mine, wi_recv, wi_send, wi_rb,
         y_blk, do_mine, do_other, wo_other, wo_mine, wo_recv, wo_send, wo_rb,
         pk_mine, pk_recv, pk_send, pk_rb, pk_own, send_sems, recv_sems, local_sems) = refs[8 + n_dz:]
        x, y, c = lax.axis_index("x"), lax.axis_index("y"), lax.axis_index("c")
        j = 2 * x + y
        sib = (x, y, 1 - c)
        chips = [(1 - x, y), (x, 1 - y), (1 - x, 1 - y)]
        cj = [2 * cx + cy for cx, cy in chips]
        near = (jnp.where(c == 0, 1 - x, x), jnp.where(c == 0, y, 1 - y), c)
        slabs = [cj[2], cj[0], cj[1], j]

        def copy(k, src, dst, to):
            return pltpu.make_async_remote_copy(src_ref=src, dst_ref=dst, send_sem=send_sems.at[k],
                                                recv_sem=recv_sems.at[k], device_id=to, device_id_type=MESH)

        def local(k, src, dst):
            cp = pltpu.make_async_copy(src, dst, local_sems.at[k])
            cp.start()
            return cp

        rows_half = lambda r, cc, n: r.at[pl.ds(_mo(cc * n, 16), n), :]
        cols_half = lambda r, cc, n: r.at[:, pl.ds(_mo(cc * n, 128), n)]
        pk_piece = lambda r, cc, jj: r.at[pl.ds(_mo(cc * hp, 16), hp), pl.ds(_mo(jj * 128, 128), 128)]

        sends = []

        def start(cp):
            cp.start()
            sends.append(cp)

        def dz_pieces(s):
            g0 = wi_w * s
            k0, off0 = g0 // D, g0 % D
            w0 = min(D - off0, wi_w)
            pieces = [(k0, off0, w0, 0)]
            if w0 < wi_w:
                pieces.append((k0 + 1, 0, wi_w - w0, w0))
            return pieces

        def dz_copies(s):
            cps = []
            for q, (k, off, w, dst) in enumerate(dz_pieces(s)):
                cps.append(pltpu.make_async_copy(dz_hbm[k].at[pl.ds(lc if k == 0 else 0, lx), pl.ds(off, w)],
                                                 dzbuf.at[pl.ds(0, lx), pl.ds(dst, w)], local_sems.at[11 + q]))
            if s == 0:
                cps.append(pltpu.make_async_copy(dz_hbm[0].at[pl.ds(0, lc), :], dzbuf.at[pl.ds(lx, lc), pl.ds(0, D)],
                                                 local_sems.at[13]))
            return cps

        def dz_load(sl):
            for s in range(NCHIP):
                @pl.when(sl == s)
                def _():
                    if s == 0:
                        dzbuf[pl.ds(lx, lc), pl.ds(D, wi_w - D)] = jnp.zeros((lc, wi_w - D), BF16)
                    else:
                        dzbuf[pl.ds(lx, lc), :] = jnp.zeros((lc, wi_w), BF16)
                    for cp in dz_copies(s):
                        cp.start()

        def dz_wait(sl):
            for s in range(NCHIP):
                @pl.when(sl == s)
                def _():
                    for cp in dz_copies(s):
                        cp.wait()

        l_pk = local(0, rows_half(pk_hbm, c, hp), pk_mine)
        start(copy(0, rows_half(pk_hbm, 1 - c, hp), pk_recv, sib))
        col = lambda r, cc: r.at[:, pl.ds(_mo(cc * 512, 128), 512)]
        do_loads = [local(7, col(do_hbm, c), do_mine), local(14, col(do_hbm, 1 - c), do_other)]
        hn_loads = [local(2, col(hn_hbm, c), hn_mine.at[pl.ds(0, lx), :]),
                    local(3, col(hnc_hbm, c), hn_mine.at[pl.ds(lx, lc), :]),
                    local(4, col(hn_hbm, 1 - c), hn_other.at[pl.ds(0, lx), :]),
                    local(5, col(hnc_hbm, 1 - c), hn_other.at[pl.ds(lx, lc), :])]
        y_copy = lambda s: pltpu.make_async_copy(col(y_hbm, slabs[s]), y_blk, local_sems.at[1])
        y_copy(0).start()
        dz_load(slabs[0])

        def pair_sum(mine, recv, send, nrows, keep, relayed=None):
            def step(i, carry):
                rows = pl.ds(_mo(i * RCHUNK, RCHUNK), RCHUNK)
                s = mine[rows, :] + recv[rows, :].astype(F32)
                if relayed is not None:
                    s = s + relayed[rows, :].astype(F32)
                if keep:
                    mine[rows, :] = s
                if send is not None:
                    send[rows, :] = s.astype(BF16)
                return carry
            lax.fori_loop(0, nrows // RCHUNK, step, 0)

        def chip_sum(own, rb, nrows, terms=(0, 1, 2)):
            def step(i, carry):
                rows = pl.ds(_mo(i * RCHUNK, RCHUNK), RCHUNK)
                acc = own[rows, :]
                for q in terms:
                    acc = acc + rb[q, rows, :].astype(F32)
                own[rows, :] = acc
                return carry
            lax.fori_loop(0, nrows // RCHUNK, step, 0)

        w_in_g = dict(other=wi_other, mine=wi_mine, recv=wi_recv, send=wi_send, rb=wi_rb, p1_sems=(2, 3, 4, 5), p2_sem=12,
                      p1=[None] * NCHIP, wait_load=lambda s: dz_wait(slabs[s]), load=lambda s: dz_load(slabs[s]),
                      dot_other=lambda: _dot_tn(hn_other[...], dzbuf[...]), dot_mine=lambda: _dot_tn(hn_mine[...], dzbuf[...]))
        w_out_g = dict(other=wo_other, mine=wo_mine, recv=wo_recv, send=wo_send, rb=wo_rb, p1_sems=(1, 24, 25, 26), p2_sem=9,
                       p1=[None] * NCHIP, wait_load=lambda s: y_copy(s).wait(), load=lambda s: y_copy(s).start(),
                       dot_other=lambda: _dot_tn(y_blk[...], do_other[...]), dot_mine=lambda: _dot_tn(y_blk[...], do_mine[...]))

        def piece_matmuls(g, s):
            if s >= 2:
                g["p1"][s - 2].wait_send()
            g["wait_load"](s)
            g["other"][s % 2] = g["dot_other"]().astype(BF16)
            g["p1"][s] = copy(g["p1_sems"][s], g["other"].at[s % 2], g["recv"].at[s], sib)
            g["p1"][s].start()
            g["mine"][s % 2] = g["dot_mine"]()
            if s + 1 < NCHIP:
                g["load"](s + 1)

        def piece_finish(g, s):
            mine, recv, send, rb, p2 = g["mine"].at[s % 2], g["recv"].at[s], g["send"], g["rb"], g["p2_sem"]
            nrows = mine.shape[0]
            copy(g["p1_sems"][s], recv, recv, sib).wait_recv()
            if s == 3:
                pair_sum(mine, recv, None, nrows, True)
                return
            if s == 0:
                pair_sum(mine, recv, send.at[0], nrows, False)
                start(copy(p2, send.at[0], rb.at[0], near))
                return
            adds_relayed = c == (1 if s == 1 else 0)

            @pl.when(adds_relayed)
            def _():
                copy(p2, rb.at[0], rb.at[0], sib).wait_recv()
                pair_sum(mine, recv, send.at[s], nrows, False, rb.at[0])

            @pl.when(jnp.logical_not(adds_relayed))
            def _():
                pair_sum(mine, recv, send.at[s], nrows, False)
            start(copy(p2 + s, send.at[s], rb.at[s], (*chips[s - 1], c)))

        def piece_total(g):
            for k in (1, 2):
                copy(g["p2_sem"] + k, g["rb"].at[k], g["rb"].at[k], sib).wait_recv()
            chip_sum(g["mine"].at[1], g["rb"], g["mine"].shape[1], (1, 2))

        for cp in do_loads:
            cp.wait()
        piece_matmuls(w_out_g, 0)
        piece_matmuls(w_out_g, 1)
        piece_finish(w_out_g, 0)
        piece_matmuls(w_out_g, 2)
        piece_finish(w_out_g, 1)
        piece_matmuls(w_out_g, 3)
        piece_finish(w_out_g, 2)

        for cp in hn_loads:
            cp.wait()
        piece_matmuls(w_in_g, 0)

        l_pk.wait()
        copy(0, pk_recv, pk_recv, sib).wait_recv()
        pair_sum(pk_mine, pk_recv, pk_send, hp, True)
        for k in range(3):
            start(copy(6 + k, pk_send.at[:, pl.ds(_mo(cj[k] * 128, 128), 128)], pk_rb.at[k], (*chips[k], c)))
        l_pk_own = local(6, pk_mine.at[:, pl.ds(_mo(j * 128, 128), 128)], pk_own)

        piece_matmuls(w_in_g, 1)
        piece_finish(w_in_g, 0)

        l_pk_own.wait()
        for k in range(3):
            copy(6 + k, pk_rb.at[k], pk_rb.at[k], sib).wait_recv()
        chip_sum(pk_own, pk_rb, hp)
        l_pk_out = local(8, pk_own, pk_piece(pk_out, c, j))
        start(copy(15, pk_own, pk_piece(pk_out, c, j), sib))
        for k in range(2):
            start(copy(16 + k, pk_own, pk_piece(pk_out, c, j), (*chips[k], c)))

        piece_matmuls(w_in_g, 2)
        piece_finish(w_in_g, 1)
        piece_matmuls(w_in_g, 3)
        piece_finish(w_in_g, 2)

        piece_finish(w_out_g, 3)
        piece_total(w_out_g)
        l_wo_out = local(9, wo_mine.at[1], cols_half(wo_out, c, 512))
        start(copy(22, wo_mine.at[1], cols_half(wo_out, c, 512), sib))

        far = (jnp.where(c == 0, x, 1 - x), jnp.where(c == 0, 1 - y, y), c)
        for step, k in enumerate([c, 1 - c, 2]):
            reg = pk_piece(pk_out, c, jnp.where(k == 0, cj[0], jnp.where(k == 1, cj[1], cj[2])))
            copy(16 + k, reg, reg, sib).wait_recv()
            if step == 0:
                start(copy(18, reg, reg, far))
            start(copy(19 + k, reg, reg, sib))

        piece_finish(w_in_g, 3)
        piece_total(w_in_g)
        l_wi_out = local(10, wi_mine.at[1], rows_half(wi_out, c, 512))
        start(copy(23, wi_mine.at[1], rows_half(wi_out, c, 512), sib))

        reg = pk_piece(pk_out, 1 - c, j)
        copy(15, reg, reg, sib).wait_recv()
        for k in range(3):
            reg = pk_piece(pk_out, 1 - c, cj[k])
            copy(19 + k, reg, reg, sib).wait_recv()
        reg = cols_half(wo_out, 1 - c, 512)
        copy(22, reg, reg, sib).wait_recv()
        reg = rows_half(wi_out, 1 - c, 512)
        copy(23, reg, reg, sib).wait_recv()
        for cp in sends + w_in_g["p1"][2:] + w_out_g["p1"][2:]:
            cp.wait_send()
        for cp in (l_pk_out, l_wo_out, l_wi_out):
            cp.wait()

    return pl.pallas_call(
        body, name="grads_reduce",
        out_shape=(jax.ShapeDtypeStruct((D, wi_w), F32), jax.ShapeDtypeStruct((512, D), F32),
                   jax.ShapeDtypeStruct(pack.shape, F32)),
        in_specs=[ANY] * (5 + n_dz), out_specs=(ANY,) * 3,
        scratch_shapes=[
            pltpu.VMEM((lt, 512), BF16), pltpu.VMEM((lt, 512), BF16), pltpu.VMEM((lt, wi_w), BF16),
            pltpu.VMEM((2, 512, wi_w), BF16), pltpu.VMEM((2, 512, wi_w), F32), pltpu.VMEM((4, 512, wi_w), BF16),
            pltpu.VMEM((3, 512, wi_w), BF16), pltpu.VMEM((3, 512, wi_w), BF16),
            pltpu.VMEM((lx, 512), BF16), pltpu.VMEM((lx, 512), BF16), pltpu.VMEM((lx, 512), BF16),
            pltpu.VMEM((2, 512, 512), BF16), pltpu.VMEM((2, 512, 512), F32), pltpu.VMEM((4, 512, 512), BF16),
            pltpu.VMEM((3, 512, 512), BF16), pltpu.VMEM((3, 512, 512), BF16),
            pltpu.VMEM((hp, 512), F32), pltpu.VMEM((hp, 512), F32), pltpu.VMEM((hp, 512), BF16),
            pltpu.VMEM((3, hp, 128), BF16), pltpu.VMEM((hp, 128), F32),
            pltpu.SemaphoreType.DMA((27,)), pltpu.SemaphoreType.DMA((27,)), pltpu.SemaphoreType.DMA((15,))],
        compiler_params=_cp(vmem_mb=56),
    )(hn, *dzs, hn_c, y, do, pack)


def _ada_bwd(c_all, dmx_all_j, dmc_j, dmx_all, dmc, c_ctx, ada_w_full):
    def body(c_ref, dmxj_ref, dmcj_ref, dmx_ref, dmc_ref, cc_ref, w_ref, gw_ref, gb_ref, gc_ref, lhs, rhs, dm8):
        lhs[...] = jnp.zeros_like(lhs)
        rhs[...] = jnp.zeros_like(rhs)
        cv = c_ref[...]
        lhs[0:8, :] = cv * _sigmoid(cv)
        cc = cc_ref[...]
        a_c, da_c = _silu_and_grad(cc)
        lhs[8:9, :] = a_c
        rhs[0:8, :] = dmxj_ref[...]
        rhs[8:9, :] = dmcj_ref[...]
        gw_ref[...] = _dot_tn(lhs[...].astype(BF16), rhs[...].astype(BF16))
        gb_ref[...] = jnp.sum(dmx_ref[...], axis=0, keepdims=True) + dmc_ref[...]
        dm8[...] = jnp.zeros_like(dm8)
        dm8[0:1, :] = dmc_ref[...]
        da = _dot_nt(dm8[...].astype(BF16), w_ref[...])
        gc_ref[...] = da[0:1, :] * da_c

    return pl.pallas_call(
        body, name="ada_bwd",
        out_shape=(jax.ShapeDtypeStruct((D, 768), F32), jax.ShapeDtypeStruct((1, 3 * D), F32),
                   jax.ShapeDtypeStruct((1, D), F32)),
        in_specs=[VMEM] * 7, out_specs=(VMEM,) * 3,
        scratch_shapes=[pltpu.VMEM((16, D), F32), pltpu.VMEM((16, 768), F32), pltpu.VMEM((8, 3 * D), F32)],
        compiler_params=_cp(vmem_mb=32),
    )(c_all, dmx_all_j, dmc_j, dmx_all, dmc, c_ctx, ada_w_full)


def _proj(x, mods, mrow, norm_g, w_full, nk, name, sgu=None, z0_rows=None, z0_off=0, z0_into=None):
    lx = x.shape[0]
    n = lx // T
    order = [2, 3, 0, 1, 4] if sgu is not None else list(range(nk))

    def body(x_ref, sh_ref, sc_ref, ng_ref, *rest):
        w_refs, rest = rest[:nk], rest[nk:]
        if sgu is not None:
            g_ref, b_ref, sw_ref, bt_ref = rest[:4]
            rest = rest[4:]
        if z0_into is not None:
            rest = rest[1:]
        hn_ref, z_refs = rest[0], rest[1:1 + nk]
        xv = x_ref[...]
        r = lax.rsqrt(jnp.mean(xv * xv, axis=-1, keepdims=True) + NORM_EPS)
        hn = (xv * r) * ng_ref[...] * (1.0 + sc_ref[mrow:mrow + 1, :]) + sh_ref[mrow:mrow + 1, :]
        hb = hn.astype(BF16)
        hn_ref[...] = hb
        for k in order[:2]:
            z_refs[k][...] = _dot(hb, w_refs[k][...])
        if sgu is not None:
            ys_ref, mixed_s = rest[1 + nk], rest[2 + nk]
            for ch in range(T // HD):
                rows = slice(HD * ch, HD * ch + HD)
                ug = _sgu_parts(z_refs[2][rows, :], z_refs[3][rows, :], g_ref[...], b_ref[...], sw_ref, bt_ref,
                                mixed_s)[0]
                ys_ref[rows, :] = ug * mixed_s[...]
        for k in order[2:]:
            z_refs[k][...] = _dot(hb, w_refs[k][...])

    row = pl.BlockSpec((T, D), lambda i: (i, 0))
    vec = pl.BlockSpec((1, D), lambda i: (0, 0))
    in_specs = [row, pl.BlockSpec((8, D), lambda i: (0, 0)), pl.BlockSpec((8, D), lambda i: (0, 1)), vec]
    in_specs += [pl.BlockSpec((D, D), lambda i, k=k: (0, k)) for k in range(nk)]
    args = [x, mods, mods, norm_g] + [w_full] * nk
    out_shape = (jax.ShapeDtypeStruct((lx, D), BF16),) + tuple(jax.ShapeDtypeStruct((lx, D), F32) for _ in range(nk))
    out_specs = [row] * len(out_shape)
    scratch = []
    aliases = {}
    if sgu is not None:
        in_specs += [vec, vec, pl.BlockSpec((NH, HD, HD), lambda i: (0, 0, 0)), pl.BlockSpec((HD, NH), lambda i: (0, 0))]
        args += list(sgu)
        out_shape += (jax.ShapeDtypeStruct((lx, D), F32),)
        out_specs.append(row)
        scratch = [pltpu.VMEM((HD, D), F32)]
    if z0_rows is not None:
        out_shape = out_shape[:1] + (jax.ShapeDtypeStruct((z0_rows, D), F32),) + out_shape[2:]
        out_specs[1] = pl.BlockSpec((T, D), lambda i: (i + z0_off, 0))
    if z0_into is not None:
        out_shape = out_shape[:1] + (jax.ShapeDtypeStruct(z0_into.shape, F32),) + out_shape[2:]
        aliases = {len(args): 1}
        in_specs.append(ANY)
        args.append(z0_into)
    return pl.pallas_call(
        body, name=name, grid=(n,), out_shape=out_shape, in_specs=in_specs, out_specs=tuple(out_specs),
        scratch_shapes=scratch, input_output_aliases=aliases, compiler_params=_cp(1, vmem_mb=56),
    )(*args)


def _tile_specs(rows_per_pos, width, n_tiles, tile):
    last = n_tiles * (T // 8) - 1
    r = rows_per_pos
    return [pl.BlockSpec((T * r, width), lambda i: (tile(i), 0)),
            pl.BlockSpec((8 * r, width), lambda i: (jnp.maximum(tile(i) * (T // 8) - 1, 0), 0)),
            pl.BlockSpec((8 * r, width), lambda i: (jnp.minimum((tile(i) + 1) * (T // 8), last), 0))]


def _has_prev(tile):
    return tile >= 2


def _has_next(tile, nt):
    return jnp.logical_and(tile >= 1, tile < nt - 1)


ZT = pl.BlockSpec((T * NH, HD), lambda i: (i, 0))
CONV_CHUNK = 32


SCAN_SUB = 8


def _scan_tile(chains, post, carry_ref):
    blk = T // SCAN_SUB

    def step(k, state):
        new = []
        for ci, (a_ref, x_ref, o_ref, q_ref, reverse, xscale) in enumerate(chains):
            for q in range(SCAN_SUB):
                s, p = state[ci * SCAN_SUB + q]
                t = (q + 1) * blk - 1 - k if reverse else q * blk + k
                r = pl.ds(_mo(t * NH, NH), NH)
                a = a_ref[r, :]
                x = x_ref[r, :] if xscale is None else x_ref[r, :] * xscale
                if post:
                    o = x + s
                    o_ref[r, :] = o
                    q_ref[r, :] = p
                    new.append((a * o, a * p))
                else:
                    o = a * s + x
                    p = a * p
                    o_ref[r, :] = o
                    q_ref[r, :] = p
                    new.append((o, p))
        return tuple(new)

    zero = jnp.zeros((NH, HD), F32)
    one = jnp.ones((NH, HD), F32)
    final = lax.fori_loop(0, blk, step, tuple((zero, one) for _ in range(len(chains) * SCAN_SUB)))
    for ci, (a_ref, x_ref, o_ref, q_ref, reverse, xscale) in enumerate(chains):
        carry = carry_ref[ci]
        for q in (range(SCAN_SUB - 1, -1, -1) if reverse else range(SCAN_SUB)):
            rows = pl.ds(q * blk * NH, blk * NH)
            fixed = o_ref[rows, :].reshape(blk, NH, HD) + q_ref[rows, :].reshape(blk, NH, HD) * carry[None]
            o_ref[rows, :] = fixed.reshape(blk * NH, HD)
            s_loc, p_loc = final[ci * SCAN_SUB + q]
            carry = s_loc + p_loc * carry
        carry_ref[ci] = carry


def _lru_fwd(xa, conv_wz, conv_bz, wcat, bcat, lamcat, name):
    lx = xa.shape[0]
    n = lx // T
    tile_u = lambda i: i
    tile_d = lambda i: jnp.where(i == 0, 0, n - i)

    def body(xm_u, xp_u, xn_u, xm_d, xp_d, xn_d, cw, cb, w_ref, b_ref, lam_ref,
             xcz_o, af_o, ab_o, hf_o, hb_o, gf_o, gb_o, fu, fd, pad, xc_d, x_u, x_d, q_u, q_d, carry):
        i = pl.program_id(0)

        @pl.when(i == 0)
        def _():
            carry[...] = jnp.zeros_like(carry)

        def conv_gates(xm, xp, xn, tile, d, xc_ref, a_ref, x_ref, g_ref):
            pmask = jnp.where(_has_prev(tile), 1.0, 0.0)
            nmask = jnp.where(_has_next(tile, n), 1.0, 0.0)
            for h in range(NH):
                cols = slice(HD * h, HD * h + HD)
                pad[_zrows(h, 8), :] = xp[:, cols] * pmask
                pad[pl.ds(8 * NH + h, T, stride=NH), :] = xm[:, cols]
                pad[pl.ds((T + 8) * NH + h, 8, stride=NH), :] = xn[:, cols] * nmask

            def conv_chunk(ci, c_):
                base = pl.multiple_of(ci * (CONV_CHUNK * NH), CONV_CHUNK * NH)
                acc = None
                for k in range(4):
                    sl = pad[pl.ds(base + (7 + k) * NH, CONV_CHUNK * NH), :].reshape(CONV_CHUNK, NH, HD)
                    term = sl * cw[k][None]
                    acc = term if acc is None else acc + term
                acc = acc + cb[...][None]
                xc_ref[pl.ds(base, CONV_CHUNK * NH), :] = acc.reshape(CONV_CHUNK * NH, HD)
                return c_
            lax.fori_loop(0, T // CONV_CHUNK, conv_chunk, 0)

            for h in range(NH):
                xch = xc_ref[_zrows(h, T), :]
                pre = _dot(xch.astype(BF16), w_ref[h, :, 256 * d:256 * d + 256]) + b_ref[h:h + 1, 256 * d:256 * d + 256]
                r, gi, _, _, a, mult = _lru_gate(pre, lam_ref[h:h + 1, :], d, 0)
                a_ref[_zrows(h, T), :] = a
                x_ref[_zrows(h, T), :] = mult * gi * xch
                for q, val in enumerate((r, gi, mult)):
                    g_ref[:, q * D + HD * h:q * D + HD * h + HD] = val

        conv_gates(xm_u, xp_u, xn_u, tile_u(i), 0, xcz_o, af_o, x_u, gf_o)
        conv_gates(xm_d, xp_d, xn_d, tile_d(i), 1, xc_d, ab_o, x_d, gb_o)

        _scan_tile([(af_o, x_u, hf_o, q_u, False, None), (ab_o, x_d, hb_o, q_d, True, None)], False, carry)

        @pl.when(i == 0)
        def _():
            fu[...] = carry[0]
            fd[...] = carry[1]

    full = lambda shape: pl.BlockSpec(shape, lambda i: (0,) * len(shape))
    st = full((NH, HD))
    in_specs = _tile_specs(1, D, n, tile_u) + _tile_specs(1, D, n, tile_d)
    in_specs += [full((4, NH, HD)), st, full((NH, HD, 4 * HD)), full((NH, 4 * HD)), full((NH, 2 * HD))]
    up = pl.BlockSpec((T * NH, HD), lambda i: (tile_u(i), 0))
    dn = pl.BlockSpec((T * NH, HD), lambda i: (tile_d(i), 0))
    zs = jax.ShapeDtypeStruct((lx * NH, HD), F32)
    ss = jax.ShapeDtypeStruct((NH, HD), F32)
    zbuf = pltpu.VMEM((T * NH, HD), F32)
    gs = jax.ShapeDtypeStruct((lx, 3 * D), F32)
    g_up = pl.BlockSpec((T, 3 * D), lambda i: (tile_u(i), 0))
    g_dn = pl.BlockSpec((T, 3 * D), lambda i: (tile_d(i), 0))
    return pl.pallas_call(
        body, name=name, grid=(n,), out_shape=(zs,) * 5 + (gs, gs, ss, ss), in_specs=in_specs,
        out_specs=(up, up, dn, up, dn, g_up, g_dn, st, st),
        scratch_shapes=[pltpu.VMEM(((T + 16) * NH, HD), F32), zbuf, zbuf, zbuf, zbuf, zbuf,
                        pltpu.VMEM((2, NH, HD), F32)],
        compiler_params=_cp(1, vmem_mb=48),
    )(xa, xa, xa, xa, xa, xa, conv_wz, conv_bz, wcat, bcat, lamcat)


def _sgu_parts(u, v, lng, lnb, w_ref, bt_ref, mixed_s):
    ug, dug = _gelu_and_grad(u)
    vg, dvg = _gelu_and_grad(v)
    mu = jnp.mean(vg, axis=-1, keepdims=True)
    vc = vg - mu
    rstd = lax.rsqrt(jnp.mean(vc * vc, axis=-1, keepdims=True) + LN_EPS)
    vh = vc * rstd
    vn = (vh * lng + lnb).astype(BF16)
    for g in range(NH):
        cols = slice(HD * g, HD * g + HD)
        mixed_s[:, cols] = _dot(w_ref[g], vn[:, cols]) + bt_ref[:, g:g + 1]
    return ug, dug, dvg, rstd, vh, vn


def _sgu_bwd_chunk(u, v, dys_v, lng, lnb, w_ref, bt_ref, mixed_s, dvn_s, dw_ref, db_ref, dg_ref, dbl_ref):
    ug, dug, dvg, rstd, vh, vn = _sgu_parts(u, v, lng, lnb, w_ref, bt_ref, mixed_s)
    du = (dys_v * mixed_s[...] * dug).astype(BF16)
    dmix = dys_v * ug
    ones = jnp.ones((8, HD), BF16)
    for g in range(NH):
        cols = slice(HD * g, HD * g + HD)
        dm = dmix[:, cols]
        hi = dm.astype(BF16)
        lo = (dm - hi.astype(F32)).astype(BF16)
        dw_ref[g] += _dot_nt(hi, vn[:, cols])
        db_ref[g:g + 1, :] += (_dot_nt(ones, hi) + _dot_nt(ones, lo))[0:1, :]
        dvn_s[:, cols] = _dot_tn(w_ref[g], hi)
    dvn = dvn_s[...]
    dg_ref[...] += jnp.sum(dvn * vh, axis=0, keepdims=True)
    dbl_ref[...] += jnp.sum(dvn, axis=0, keepdims=True)
    dvh = dvn * lng
    dvg_in = rstd * (dvh - jnp.mean(dvh, axis=-1, keepdims=True) - vh * jnp.mean(dvh * vh, axis=-1, keepdims=True))
    return du, (dvg_in * dvg).astype(BF16)


def _out_fwd_bwd(hf_z, hb_z, ga, gb, ys, x, tgt, mods, final_g, w_out_full):
    lx = x.shape[0]
    n = lx // T

    def body(hf_ref, hb_ref, ga_ref, gb_ref, ys_ref, x_ref, t_ref, gx_ref, fg_ref, w_ref,
             loss_ref, dfg_ref, dgx_ref, dxn_ref, y_ref, do_ref, dga_ref, dgb_ref, dyl_ref, dys_ref, yl_s):
        i = pl.program_id(0)

        @pl.when(i == 0)
        def _():
            loss_ref[...] = jnp.zeros_like(loss_ref)
            dfg_ref[...] = jnp.zeros_like(dfg_ref)
            dgx_ref[...] = jnp.zeros_like(dgx_ref)

        for h in range(NH):
            yl_s[:, HD * h:HD * h + HD] = hf_ref[_zrows(h, T), :] + hb_ref[_zrows(h, T), :]
        yl = yl_s[...]
        gav = ga_ref[...]
        gbv = gb_ref[...]
        sa, dsa = _silu_and_grad(gav)
        sb, dsb = _silu_and_grad(gbv)
        ysv = ys_ref[...]
        y_ref[:, 0:D] = (yl * sa).astype(BF16)
        y_ref[:, D:2 * D] = (ysv * sb).astype(BF16)
        o = _dot(y_ref[...], w_ref[...])
        gx = gx_ref[0:1, :]
        xnew = x_ref[...] + gx * o
        r2 = lax.rsqrt(jnp.mean(xnew * xnew, axis=-1, keepdims=True) + NORM_EPS)
        xh = xnew * r2
        fg = fg_ref[...]
        err = xh * fg - t_ref[...]
        loss_ref[...] += 0.5 * jnp.sum(jnp.mean(err * err, axis=-1, keepdims=True), axis=0, keepdims=True)

        @pl.when(i == n - 1)
        def _():
            lp = loss_ref[...]
            lp1 = lp.astype(BF16).astype(F32)
            lp2 = (lp - lp1).astype(BF16).astype(F32)
            lp3 = (lp - lp1 - lp2).astype(BF16).astype(F32)
            lane = lax.broadcasted_iota(jnp.int32, lp.shape, 1)
            loss_ref[...] = jnp.where(lane == 0, lp1, jnp.where(lane == 1, lp2, jnp.where(lane == 2, lp3, 0.0)))
        dout = err * (1.0 / D)
        dfg_ref[...] += jnp.sum(dout * xh, axis=0, keepdims=True)
        dxh = dout * fg
        dxn = r2 * (dxh - xh * jnp.mean(dxh * xh, axis=-1, keepdims=True))
        dxn_ref[...] = dxn
        dgx_ref[...] += jnp.sum(dxn * o, axis=0, keepdims=True)
        do = (dxn * gx).astype(BF16)
        do_ref[...] = do
        dy = _dot_nt(do, w_ref[...])
        dy1 = dy[:, 0:D]
        dy2 = dy[:, D:2 * D]
        dga_ref[...] = (dy1 * yl * dsa).astype(BF16)
        dgb_ref[...] = (dy2 * ysv * dsb).astype(BF16)
        dys_ref[...] = dy2 * sb
        yl_s[...] = dy1 * sa
        for h in range(NH):
            dyl_ref[_zrows(h, T), :] = yl_s[:, HD * h:HD * h + HD]

    row = pl.BlockSpec((T, D), lambda i: (i, 0))
    vec = pl.BlockSpec((1, D), lambda i: (0, 0))
    zlat = pl.BlockSpec((T * NH, HD), lambda i: (i + 1, 0))
    in_specs = [zlat, zlat, row, row, row, row, row, pl.BlockSpec((8, D), lambda i: (0, 2)), vec,
                pl.BlockSpec((2 * D, D), lambda i: (0, 0))]
    out_shape = (jax.ShapeDtypeStruct((1, D), F32), jax.ShapeDtypeStruct((1, D), F32), jax.ShapeDtypeStruct((1, D), F32),
                 jax.ShapeDtypeStruct((lx, D), F32), jax.ShapeDtypeStruct((lx, 2 * D), BF16),
                 jax.ShapeDtypeStruct((lx, D), BF16), jax.ShapeDtypeStruct((lx, D), BF16),
                 jax.ShapeDtypeStruct((lx, D), BF16), jax.ShapeDtypeStruct((lx * NH, HD), F32),
                 jax.ShapeDtypeStruct((lx, D), F32))
    out_specs = (vec, vec, vec, row, pl.BlockSpec((T, 2 * D), lambda i: (i, 0)),
                 row, row, row, ZT, row)
    return pl.pallas_call(
        body, name="out_fwd_bwd", grid=(n,), out_shape=out_shape, in_specs=in_specs, out_specs=out_specs,
        scratch_shapes=[pltpu.VMEM((T, D), F32)],
        compiler_params=_cp(1, vmem_mb=56),
    )(hf_z, hb_z, ga, gb, ys, x, tgt, mods, final_g, w_out_full)


def _lru_bwd(xc_z, dy_z, hf_z, hb_z, af_z, ab_z, gf, gb, s_b, wcat, lamcat, name):
    lx = xc_z.shape[0] // NH
    n = lx // T
    tile_u = lambda i: jnp.where(i == n - 1, 0, i + 1)
    tile_d = lambda i: n - 1 - i

    def body(xc_u, dy_u, hb_ref, hbn_ref, ab_ref, gb_ref, xc_d, dy_d, hf_ref, hfp_ref, af_ref, gf_ref,
             sb_ref, w_ref, lam_ref, dxcb_ref, dxcf_ref, dw_ref, db_ref, dl_ref,
             lb_s, lf_s, q_u, q_d, pf_s, pb_s, dpre_s, carry):
        i = pl.program_id(0)
        tu, td = tile_u(i), tile_d(i)

        @pl.when(i == 0)
        def _():
            dw_ref[...] = jnp.zeros_like(dw_ref)
            db_ref[...] = jnp.zeros_like(db_ref)
            dl_ref[...] = jnp.zeros_like(dl_ref)
            carry[...] = jnp.zeros_like(carry)

        _scan_tile([(ab_ref, dy_u, lb_s, q_u, False, jnp.where(tu == 0, 0.0, 1.0)),
                    (af_ref, dy_d, lf_s, q_d, True, jnp.where(td == 0, 0.0, 1.0))], True, carry)
        zero = jnp.zeros((NH, HD), F32)
        pb_s[pl.ds(0, T * NH), :] = hb_ref[...]
        pb_s[pl.ds(T * NH, NH), :] = jnp.where(tu == n - 1, sb_ref[...], jnp.where(tu == 0, zero, hbn_ref[pl.ds(0, NH), :]))
        pf_s[pl.ds(0, NH), :] = jnp.where(td == 0, zero, hfp_ref[pl.ds(7 * NH, NH), :])
        pf_s[pl.ds(NH, T * NH), :] = hf_ref[...]
        sides = ((1, xc_u, lb_s, pb_s, NH, ab_ref, gb_ref, dxcb_ref), (0, xc_d, lf_s, pf_s, 0, af_ref, gf_ref, dxcf_ref))
        for d, xc_ref, adj_s, prev_s, prev_off, a_ref, g_ref, dxc_ref in sides:
            wcols = slice(256 * d, 256 * d + 256)
            for h in range(NH):
                xch = xc_ref[_zrows(h, T), :]
                xcb = xch.astype(BF16)
                r, gi, mult = (g_ref[:, q * D + HD * h:q * D + HD * h + HD] for q in range(3))
                a = a_ref[_zrows(h, T), :]
                lam = lam_ref[h:h + 1, HD * d:HD * d + HD]
                sp = _softplus(-lam)
                du = adj_s[_zrows(h, T), :]
                da = du * prev_s[pl.ds(prev_off + h, T, stride=NH), :]
                dgi = du * mult * xch
                dmult = du * gi * xch
                dla = da * a - dmult * (a * a) / mult
                dr = dla * ((-LRU_C) * sp)
                dsp = jnp.sum(dla * ((-LRU_C) * r), axis=0, keepdims=True)
                dl_ref[h:h + 1, HD * d:HD * d + HD] += dsp * (-_sigmoid(-lam))
                dpre_s[:, 0:HD] = dr * r * (1.0 - r)
                dpre_s[:, HD:2 * HD] = dgi * gi * (1.0 - gi)
                dpre = dpre_s[...]
                dpb = dpre.astype(BF16)
                dw_ref[h, :, wcols] += _dot_tn(xcb, dpb)
                db_ref[h:h + 1, wcols] += jnp.sum(dpre, axis=0, keepdims=True)
                dxc_ref[_zrows(h, T), :] = du * mult * gi + _dot_nt(dpb, w_ref[h, :, wcols])

    full = lambda shape: pl.BlockSpec(shape, lambda i: (0,) * len(shape))
    wsp, bsp, lsp = full((NH, HD, 4 * HD)), full((NH, 4 * HD)), full((NH, 2 * HD))
    st = full((NH, HD))
    up = pl.BlockSpec((T * NH, HD), lambda i: (tile_u(i), 0))
    dn = pl.BlockSpec((T * NH, HD), lambda i: (tile_d(i), 0))
    dy_up = pl.BlockSpec((T * NH, HD), lambda i: (jnp.maximum(tile_u(i) - 1, 0), 0))
    dy_dn = pl.BlockSpec((T * NH, HD), lambda i: (jnp.maximum(tile_d(i) - 1, 0), 0))
    nxt = _tile_specs(NH, HD, n, tile_u)[2]
    prv = _tile_specs(NH, HD, n, tile_d)[1]
    g_up = pl.BlockSpec((T, 3 * D), lambda i: (tile_u(i), 0))
    g_dn = pl.BlockSpec((T, 3 * D), lambda i: (tile_d(i), 0))
    zs = jax.ShapeDtypeStruct((lx * NH, HD), F32)
    zbuf = pltpu.VMEM((T * NH, HD), F32)
    zbuf1 = pltpu.VMEM(((T + 1) * NH, HD), F32)
    return pl.pallas_call(
        body, name=name, grid=(n,),
        out_shape=(zs, zs, jax.ShapeDtypeStruct((NH, HD, 4 * HD), F32), jax.ShapeDtypeStruct((NH, 4 * HD), F32),
                   jax.ShapeDtypeStruct((NH, 2 * HD), F32)),
        in_specs=[up, dy_up, up, nxt, up, g_up, dn, dy_dn, dn, prv, dn, g_dn, st, wsp, lsp],
        out_specs=(up, dn, wsp, bsp, lsp),
        scratch_shapes=[zbuf, zbuf, zbuf, zbuf, zbuf1, zbuf1, pltpu.VMEM((T, 2 * HD), F32),
                        pltpu.VMEM((2, NH, HD), F32)],
        compiler_params=_cp(1, vmem_mb=56),
    )(xc_z, dy_z, hb_z, hb_z, ab_z, gb, xc_z, dy_z, hf_z, hf_z, af_z, gf, s_b, wcat, lamcat)


def _conv_bwd(dxc_a, dxc_b, xa, conv_wz, dcw0, dcb0, name):
    lx = dxc_a.shape[0] // NH
    n = lx // T

    def body(dm_a, dp_a, dn_a, dm_b, dp_b, dn_b, xan_ref, cw, dcw0_ref, dcb0_ref, dxa_ref, dcw_ref, dcb_ref,
             pad, dxa_s, xa_ref):
        i = pl.program_id(0)

        @pl.when(i == 0)
        def _():
            dcw_ref[...] = dcw0_ref[...]
            dcb_ref[...] = dcb0_ref[...]

        for h in range(NH):
            xa_ref[_zrows(h, T), :] = xan_ref[:, HD * h:HD * h + HD]

        pmask = jnp.where(_has_prev(i), 1.0, 0.0)
        nmask = jnp.where(_has_next(i, n), 1.0, 0.0)
        pad[pl.ds(0, 8 * NH), :] = (dp_a[...] + dp_b[...]) * pmask
        pad[pl.ds(8 * NH, T * NH), :] = dm_a[...] + dm_b[...]
        pad[pl.ds((T + 8) * NH, 8 * NH), :] = (dn_a[...] + dn_b[...]) * nmask

        def chunk(ci, carry):
            base = pl.multiple_of(ci * (CONV_CHUNK * NH), CONV_CHUNK * NH)
            xav = xa_ref[pl.ds(base, CONV_CHUNK * NH), :].reshape(CONV_CHUNK, NH, HD)
            acc = None
            for k in range(4):
                sl = pad[pl.ds(base + (9 - k) * NH, CONV_CHUNK * NH), :].reshape(CONV_CHUNK, NH, HD)
                term = sl * cw[k][None]
                acc = term if acc is None else acc + term
                dcw_ref[k] += jnp.sum(sl * xav, axis=0)
                if k == 1:
                    dcb_ref[...] += jnp.sum(sl, axis=0)
            dxa_s[pl.ds(base, CONV_CHUNK * NH), :] = acc.reshape(CONV_CHUNK * NH, HD)
            return carry
        lax.fori_loop(0, T // CONV_CHUNK, chunk, 0)
        for h in range(NH):
            dxa_ref[:, HD * h:HD * h + HD] = dxa_s[_zrows(h, T), :].astype(BF16)

    full = lambda shape: pl.BlockSpec(shape, lambda i: (0,) * len(shape))
    return pl.pallas_call(
        body, name=name, grid=(n,),
        out_shape=(jax.ShapeDtypeStruct((lx, D), BF16), jax.ShapeDtypeStruct((4, NH, HD), F32),
                   jax.ShapeDtypeStruct((NH, HD), F32)),
        in_specs=_tile_specs(NH, HD, n, lambda i: i) * 2 + [pl.BlockSpec((T, D), lambda i: (i, 0)), full((4, NH, HD)),
                                                            full((4, NH, HD)), full((NH, HD))],
        out_specs=(pl.BlockSpec((T, D), lambda i: (i, 0)), full((4, NH, HD)), full((NH, HD))),
        scratch_shapes=[pltpu.VMEM(((T + 16) * NH, HD), F32), pltpu.VMEM((T * NH, HD), F32),
                        pltpu.VMEM((T * NH, HD), F32)],
        compiler_params=_cp(1, vmem_mb=48),
    )(dxc_a, dxc_a, dxc_a, dxc_b, dxc_b, dxc_b, xa, conv_wz, dcw0, dcb0)


def _proj_bwd(dzs, x, dxn, mods, mrow, norm_g, w_full, dng0, name, sgu=None, dz0_off=0):
    lx = x.shape[0]
    n = lx // T
    nz = len(dzs)
    has_x = dxn is not None
    wks = ([0, 1, 4, 2, 3] if sgu is not None else list(range(nz)))

    def body(*refs):
        it = iter(refs)
        take = lambda m: [next(it) for _ in range(m)]
        dz_refs, w_refs = take(nz), take(len(wks))
        x_ref, sc_ref, ng_ref, dng0_ref = take(4)
        dxn_ref = take(1)[0] if has_x else None
        if sgu is not None:
            u_ref, v_ref, dy_ref, g_ref, b_ref, sw_ref, bt_ref = take(7)
        gx_ref = take(1)[0] if has_x else None
        dng_ref, dsc_ref, dsh_ref = take(3)
        if sgu is not None:
            du_ref, dv_ref, dws_ref, dbs_ref, dlg_ref, dlb_ref, mixed_s, dvn_s = take(8)
        i = pl.program_id(0)

        @pl.when(i == 0)
        def _():
            dng_ref[...] = dng0_ref[...]
            dsc_ref[...] = jnp.zeros_like(dsc_ref)
            dsh_ref[...] = jnp.zeros_like(dsh_ref)
            if sgu is not None:
                for acc in (dws_ref, dbs_ref, dlg_ref, dlb_ref):
                    acc[...] = jnp.zeros_like(acc)

        dhn = _dot_nt(dz_refs[0][...], w_refs[0][...])
        for k in range(1, nz):
            dhn = dhn + _dot_nt(dz_refs[k][...], w_refs[k][...])
        if sgu is not None:
            for ch in range(T // HD):
                rows = slice(HD * ch, HD * ch + HD)
                du, dv = _sgu_bwd_chunk(u_ref[rows, :], v_ref[rows, :], dy_ref[rows, :], g_ref[...], b_ref[...],
                                        sw_ref, bt_ref, mixed_s, dvn_s, dws_ref, dbs_ref, dlg_ref, dlb_ref)
                du_ref[rows, :] = du
                dv_ref[rows, :] = dv
            dhn = dhn + _dot_nt(du_ref[...], w_refs[nz][...]) + _dot_nt(dv_ref[...], w_refs[nz + 1][...])
        xv = x_ref[...]
        r = lax.rsqrt(jnp.mean(xv * xv, axis=-1, keepdims=True) + NORM_EPS)
        xn = xv * r
        ng = ng_ref[...]
        sc1 = 1.0 + sc_ref[mrow:mrow + 1, :]
        t = dhn * xn
        dng_ref[...] += jnp.sum(t * sc1, axis=0, keepdims=True)
        dsc_ref[...] += jnp.sum(t * ng, axis=0, keepdims=True)
        dsh_ref[...] += jnp.sum(dhn, axis=0, keepdims=True)
        if has_x:
            dxh = dhn * (ng * sc1)
            gx_ref[...] = dxn_ref[...] + r * (dxh - xn * jnp.mean(dxh * xn, axis=-1, keepdims=True))

    row = pl.BlockSpec((T, D), lambda i: (i, 0))
    vec = pl.BlockSpec((1, D), lambda i: (0, 0))
    in_specs = [pl.BlockSpec((T, D), lambda i: (i + dz0_off, 0))] + [row] * (nz - 1)
    in_specs += [pl.BlockSpec((D, D), lambda i, k=k: (0, k)) for k in wks]
    in_specs += [row, pl.BlockSpec((8, D), lambda i: (0, 1)), vec, vec]
    args = list(dzs) + [w_full] * len(wks) + [x, mods, norm_g, dng0]
    vs = jax.ShapeDtypeStruct((1, D), F32)
    out_shape, out_specs = (vs, vs, vs), (vec, vec, vec)
    scratch = []
    if has_x:
        in_specs.append(row)
        args.append(dxn)
        out_shape = (jax.ShapeDtypeStruct((lx, D), F32),) + out_shape
        out_specs = (row,) + out_specs
    if sgu is not None:
        wsp = pl.BlockSpec((NH, HD, HD), lambda i: (0, 0, 0))
        bsp = pl.BlockSpec((NH, HD), lambda i: (0, 0))
        in_specs += [row, row, row, vec, vec, wsp, pl.BlockSpec((HD, NH), lambda i: (0, 0))]
        args += list(sgu)
        zb = jax.ShapeDtypeStruct((lx, D), BF16)
        out_shape += (zb, zb, jax.ShapeDtypeStruct((NH, HD, HD), F32), jax.ShapeDtypeStruct((NH, HD), F32), vs, vs)
        out_specs += (row, row, wsp, bsp, vec, vec)
        scratch = [pltpu.VMEM((HD, D), F32), pltpu.VMEM((HD, D), F32)]
    return pl.pallas_call(
        body, name=name, grid=(n,), out_shape=out_shape, in_specs=in_specs, out_specs=out_specs,
        scratch_shapes=scratch, compiler_params=_cp(1, vmem_mb=56),
    )(*args)


def _adam_math(w, g, m, v):
    m = ADAM_B1 * m + (1.0 - ADAM_B1) * g
    v = ADAM_B2 * v + (1.0 - ADAM_B2) * (g * g)
    m_hat = m / (1.0 - ADAM_B1 ** ADAM_STEP)
    v_hat = v / (1.0 - ADAM_B2 ** ADAM_STEP)
    delta = -ADAM_LR * (m_hat / (jnp.sqrt(v_hat) + ADAM_EPS) + ADAM_WD * w)
    return delta, m, v


def _adam_big(w, g, m, v, name):
    rows, cols = w.shape
    tr = 256

    def body(w_ref, g_ref, m_ref, v_ref, d_o, m_o, v_o):
        d, mm, vv = _adam_math(w_ref[...], g_ref[...], m_ref[...], v_ref[...])
        d_o[...] = d
        m_o[...] = mm
        v_o[...] = vv

    blk = pl.BlockSpec((tr, cols), lambda i: (i, 0))
    s = jax.ShapeDtypeStruct((rows, cols), F32)
    return pl.pallas_call(
        body, name=name, grid=(rows // tr,), out_shape=(s, s, s), in_specs=[blk] * 4, out_specs=(blk,) * 3,
        compiler_params=_cp(1, vmem_mb=48),
    )(w, g, m, v)


def _adam_small(items, tot):
    ni = len(items)
    pieces = [it[1] if isinstance(it[1], list) else None for it in items]
    flat = [a for it, pc in zip(items, pieces) for a in ((it[0], it[2], it[3]) if pc is not None else it)]
    n_in = len(flat) + 1
    out_shape = tuple(jax.ShapeDtypeStruct(it[0].shape, F32) for it, pc in zip(items, pieces)
                      for _ in range(4 if pc is not None else 3))
    n_out = len(out_shape)
    n_loads = sum(3 + (len(pc) if pc is not None else 1) for pc in pieces)

    def body(*refs):
        ins, tot_ref, outs = refs[:n_in - 1], refs[n_in - 1], refs[n_in:n_in + n_out]
        bufs = refs[n_in + n_out:n_in + n_out + 7 * ni]
        sem_in, sem_out = refs[n_in + n_out + 7 * ni:]
        loads, q_in, q_sem = [], 0, 0
        for k, pc in enumerate(pieces):
            w_b, g_b, m_b, v_b = bufs[7 * k:7 * k + 4]
            srcs = [(ins[q_in], w_b)]
            if pc is None:
                srcs.append((ins[q_in + 1], g_b))
                q_in += 1
            else:
                srcs += [(tot_ref.at[pl.ds(r0, nr), pl.ds(c0, nc)], g_b.at[pl.ds(d0, nr), :]) for r0, nr, c0, nc, d0 in pc]
            srcs += [(ins[q_in + 1], m_b), (ins[q_in + 2], v_b)]
            q_in += 3
            mine = []
            for src, dst in srcs:
                mine.append(pltpu.make_async_copy(src, dst, sem_in.at[q_sem]))
                q_sem += 1
            loads.append(mine)
        for mine in loads:
            for cp in mine:
                cp.start()
        stores, q_out = [], 0
        for k, pc in enumerate(pieces):
            for cp in loads[k]:
                cp.wait()
            w_b, g_b, m_b, v_b = bufs[7 * k:7 * k + 4]
            res = _adam_math(w_b[...], g_b[...], m_b[...], v_b[...])
            srcs = []
            for q in range(3):
                bufs[7 * k + 4 + q][...] = res[q]
                srcs.append(bufs[7 * k + 4 + q])
            if pc is not None:
                srcs.append(g_b)
            for src in srcs:
                cp = pltpu.make_async_copy(src, outs[q_out], sem_out.at[q_out])
                cp.start()
                stores.append(cp)
                q_out += 1
        for cp in stores:
            cp.wait()

    scratch = [pltpu.VMEM(it[0].shape, F32) for it in items for _ in range(7)]
    scratch += [pltpu.SemaphoreType.DMA((n_loads,)), pltpu.SemaphoreType.DMA((n_out,))]
    res = pl.pallas_call(
        body, name="adam_small", out_shape=out_shape, in_specs=[HBM] * n_in, out_specs=(HBM,) * n_out,
        scratch_shapes=scratch, compiler_params=_cp(vmem_mb=40),
    )(*flat, tot)
    outs, q = [], 0
    for pc in pieces:
        outs.append(tuple(res[q:q + 3]) + ((res[q + 3],) if pc is not None else (None,)))
        q += 4 if pc is not None else 3
    return outs


def kernel(x, c, ctx, c_ctx, ada_w, ada_b, norm_g, w_in, conv_w, conv_b, lru_wa, lru_ba, lru_wx, lru_bx, lru_lambda, sgu_ln_g, sgu_ln_b, sgu_w, sgu_b, w_out, final_g, loss_target, m_c_ctx, m_ada_w, m_ada_b, m_norm_g, m_w_in, m_conv_w, m_conv_b, m_lru_wa, m_lru_ba, m_lru_wx, m_lru_bx, m_lru_lambda, m_sgu_ln_g, m_sgu_ln_b, m_sgu_w, m_sgu_b, m_w_out, m_final_g, v_c_ctx, v_ada_w, v_ada_b, v_norm_g, v_w_in, v_conv_w, v_conv_b, v_lru_wa, v_lru_ba, v_lru_wx, v_lru_bx, v_lru_lambda, v_sgu_ln_g, v_sgu_ln_b, v_sgu_w, v_sgu_b, v_w_out, v_final_g):
    ix, iy, ic = lax.axis_index("x"), lax.axis_index("y"), lax.axis_index("c")
    chip = 2 * ix + iy
    dev = 2 * chip + ic
    lx = x.shape[1]
    lc = ctx.shape[1]

    smalls = jnp.concatenate([conv_w[0], lru_lambda[0], jnp.zeros((10, 256), F32)], axis=0)
    c_ctx2 = c_ctx.reshape(1, D)
    ada_b_j = lax.dynamic_slice(ada_b, (0, 768 * chip), (1, 768))
    mods, c_slots, sm_all, w_in_full, wo_land, ada_land = _gather_in(c, c_ctx2, ada_w[0], ada_b_j, w_in[0], w_out[0],
                                                                     smalls)
    wo_ss, wo_rs, ada_ss, ada_rs, wo_land, ada_land, token = _late_gather_start(wo_land, ada_land)
    mods = mods + token[0:1, 0:1]
    sm3 = sm_all.reshape(NCHIP, 16, 256)
    conv_w_full = sm3[:, 0:4, :].transpose(1, 0, 2).reshape(4, D)
    lam_full = sm3[:, 4:6, :].transpose(1, 0, 2).reshape(2, D)
    conv_wz = conv_w_full.reshape(4, NH, HD)
    conv_bz = conv_b.reshape(NH, HD)
    lamcat = lam_full.reshape(2, NH, HD).transpose(1, 0, 2).reshape(NH, 2 * HD)
    wa, wx, ba, bx = lru_wa[0], lru_wx[0], lru_ba[0], lru_bx[0]
    wcat = jnp.concatenate([wa[0], wx[0], wa[1], wx[1]], axis=-1).astype(BF16)
    bcat = jnp.concatenate([ba[0], bx[0], ba[1], bx[1]], axis=-1)
    sgu_wb = sgu_w[0].astype(BF16)
    sgu_bt = sgu_b[0].T
    final_g2 = final_g.reshape(1, D)

    zero_s = jnp.zeros((NH, HD), F32)
    assert lc == T
    hn, xa_lat, ga, u, v, gb, ys = _proj(x[0], mods, 0, norm_g, w_in_full, 5, "proj",
                                         (sgu_ln_g, sgu_ln_b, sgu_wb, sgu_bt), z0_rows=lc + lx, z0_off=1)
    hn_c, xa_all = _proj(ctx[0], mods, 1, norm_g, w_in_full, 1, "proj_ctx", z0_into=xa_lat)
    xcz, af, ab, hf, hb, gf, gb_l, _, hb0 = _lru_fwd(xa_all, conv_wz, conv_bz, wcat, bcat, lamcat, "lru_fwd")

    w_out_full = _late_gather_wait(wo_land, wo_ss, wo_rs, "w_out", hf, "late_gather_wait_w_out")
    (loss_part, dfg, dgx, dxn, y, do, dga, dgb, dyl_z, dys) = _out_fwd_bwd(
        hf, hb, ga, gb, ys, x[0], loss_target[0], mods, final_g2, w_out_full)

    dxc_b, dxc_f, dwc, dbc, dlc = _lru_bwd(xcz, dyl_z, hf, hb, af, ab, gf, gb_l, hb0, wcat, lamcat, "lru_bwd")
    dxa, dcw, dcb = _conv_bwd(dxc_b, dxc_f, xa_all, conv_wz, jnp.zeros((4, NH, HD), F32), zero_s, "conv_bwd")

    grad_x, dng, dsc_x, dsh_x, du, dv, d_sgu_w, d_sgu_b, d_ln_g, d_ln_b = _proj_bwd(
        [dxa, dga, dgb], x[0], dxn, mods, 0, norm_g, w_in_full, jnp.zeros((1, D), F32), "proj_bwd",
        (u, v, dys, sgu_ln_g, sgu_ln_b, sgu_wb, sgu_bt), dz0_off=1)
    dzs = [dxa, dga, du, dv, dgb]
    dng, dsc_c, dsh_c = _proj_bwd([dxa], ctx[0], None, mods, 1, norm_g, w_in_full, dng, "proj_bwd_ctx")

    dmx = jnp.concatenate([dsh_x, dsc_x, dgx], axis=0)
    dmc = jnp.concatenate([dsh_c, dsc_c, jnp.zeros((1, D), F32)], axis=0)
    slot = jnp.concatenate([dmx, loss_part], axis=0)
    slots = lax.dynamic_update_slice(jnp.zeros((32, D), F32), slot, (4 * dev, 0))
    vecs = jnp.concatenate([dfg, dng, dcb.reshape(1, D), d_ln_g, d_ln_b, dcw.reshape(4, D), dmc,
                            jnp.zeros((4, D), F32), slots], axis=0)
    d_sgu_w4 = d_sgu_w.reshape(4, 256, HD).transpose(1, 0, 2).reshape(256, 4 * HD)
    pad8 = lambda a: jnp.pad(a, ((0, 8 - a.shape[0]), (0, 4 * HD - a.shape[1])))
    pack = jnp.concatenate([dwc.reshape(NH * HD, 4 * HD), pad8(dbc), pad8(dlc), d_sgu_w4, pad8(d_sgu_b),
                            vecs.reshape(96, 4 * HD), jnp.zeros((8, 4 * HD), F32)], axis=0)
    g_w_in, g_w_out, tot = _grads_reduce(hn, dzs, hn_c, y, do, pack)

    n_w = NH * HD
    g_lru_wa = [(0, n_w, 2 * HD * d, HD, n_w * d) for d in range(2)]
    g_lru_wx = [(0, n_w, 2 * HD * d + HD, HD, n_w * d) for d in range(2)]
    g_lru_ba = [(n_w, NH, 2 * HD * d, HD, NH * d) for d in range(2)]
    g_lru_bx = [(n_w, NH, 2 * HD * d + HD, HD, NH * d) for d in range(2)]
    g_sgu_w = [(1040, 256, HD * q, HD, 256 * q) for q in range(4)]
    g_sgu_b = [(1296, NH, 0, HD, 0)]
    g_lc = tot[1032:1040, 0:2 * HD]
    tv = tot[1304:1400].reshape(48, D)
    g_final_g, g_norm_g, g_conv_b, g_ln_g, g_ln_b = tv[0:1], tv[1:2], tv[2:3], tv[3:4], tv[4:5]
    g_conv_w_full = tv[5:9]
    dmc_tot = tv[9:12].reshape(1, 3 * D)
    slots_all = tv[16:48].reshape(8, 4, D)
    dmx_all = slots_all[:, 0:3, :].reshape(8, 3 * D)
    c_all = c_slots.reshape(8, 8, D)[:, 0, :]
    g_lam_full = jnp.stack([g_lc[:, 0:HD], g_lc[:, HD:2 * HD]]).reshape(2, D)
    g_conv_w = lax.dynamic_slice(g_conv_w_full, (0, 256 * chip), (4, 256))
    g_lam = lax.dynamic_slice(g_lam_full, (0, 256 * chip), (2, 256))
    dmx_all_j = lax.dynamic_slice(dmx_all, (0, 768 * chip), (8, 768))
    dmc_j = lax.dynamic_slice(dmc_tot, (0, 768 * chip), (1, 768))
    ada_full = _late_gather_wait(ada_land, ada_ss, ada_rs, "ada_w", tot, "late_gather_wait_ada_w")
    g_ada_w, g_ada_b, g_c_ctx = _ada_bwd(c_all, dmx_all_j, dmc_j, dmx_all, dmc_tot, c_ctx2, ada_full)

    big = {
        "ada_w": _adam_big(ada_w[0], g_ada_w, m_ada_w[0], v_ada_w[0], "adam_ada_w"),
        "w_in": _adam_big(w_in[0], g_w_in, m_w_in[0], v_w_in[0], "adam_w_in"),
        "w_out": _adam_big(w_out[0], g_w_out, m_w_out[0], v_w_out[0], "adam_w_out"),
    }
    small_in = {
        "c_ctx": (c_ctx, g_c_ctx, m_c_ctx, v_c_ctx, (1, D)),
        "ada_b": (ada_b, g_ada_b, m_ada_b, v_ada_b, (1, 3 * D)),
        "norm_g": (norm_g, g_norm_g, m_norm_g, v_norm_g, (1, D)),
        "conv_w": (conv_w, g_conv_w, m_conv_w, v_conv_w, (4, 256)),
        "conv_b": (conv_b, g_conv_b, m_conv_b, v_conv_b, (1, D)),
        "lru_wa": (lru_wa, g_lru_wa, m_lru_wa, v_lru_wa, (2 * NH * HD, HD)),
        "lru_ba": (lru_ba, g_lru_ba, m_lru_ba, v_lru_ba, (2 * NH, HD)),
        "lru_wx": (lru_wx, g_lru_wx, m_lru_wx, v_lru_wx, (2 * NH * HD, HD)),
        "lru_bx": (lru_bx, g_lru_bx, m_lru_bx, v_lru_bx, (2 * NH, HD)),
        "lru_lambda": (lru_lambda, g_lam, m_lru_lambda, v_lru_lambda, (2, 256)),
        "sgu_ln_g": (sgu_ln_g, g_ln_g, m_sgu_ln_g, v_sgu_ln_g, (1, D)),
        "sgu_ln_b": (sgu_ln_b, g_ln_b, m_sgu_ln_b, v_sgu_ln_b, (1, D)),
        "sgu_w": (sgu_w, g_sgu_w, m_sgu_w, v_sgu_w, (NH * HD, HD)),
        "sgu_b": (sgu_b, g_sgu_b, m_sgu_b, v_sgu_b, (NH, HD)),
        "final_g": (final_g, g_final_g, m_final_g, v_final_g, (1, D)),
    }
    names_small = list(small_in)
    res_small = _adam_small([tuple(a if isinstance(a, list) else a.reshape(small_in[k][4]) for a in small_in[k][:4])
                             for k in names_small], tot)
    full_shapes = {"ada_w": ada_w.shape, "w_in": w_in.shape, "w_out": w_out.shape}
    grads, deltas, new_m, new_v = {}, {}, {}, {}
    for k in ("ada_w", "w_in", "w_out"):
        g = {"ada_w": g_ada_w, "w_in": g_w_in, "w_out": g_w_out}[k]
        grads[k] = g.reshape(full_shapes[k])
        deltas[k], new_m[k], new_v[k] = (a.reshape(full_shapes[k]) for a in big[k])
    for k, res in zip(names_small, res_small):
        shape = small_in[k][0].shape
        grads[k] = (small_in[k][1] if res[3] is None else res[3]).reshape(shape)
        deltas[k], new_m[k], new_v[k] = (a.reshape(shape) for a in res[:3])

    loss = jnp.sum(slots_all[:, 3, 0:3])
    order = ["c_ctx", "ada_w", "ada_b", "norm_g", "w_in", "conv_w", "conv_b", "lru_wa", "lru_ba", "lru_wx", "lru_bx",
             "lru_lambda", "sgu_ln_g", "sgu_ln_b", "sgu_w", "sgu_b", "w_out", "final_g"]
    return (loss, grad_x.reshape(x.shape), *[grads[k] for k in order], *[deltas[k] for k in order],
            *[new_m[k] for k in order], *[new_v[k] for k in order])
```

```python
import jax
import jax.numpy as jnp
from jax import lax
from jax.experimental import pallas as pl
from jax.experimental.pallas import tpu as pltpu

F32 = jnp.float32
BF16 = jnp.bfloat16

D = 1024
NH = 8
HD = 128
NCHIP = 4
T = 256
NORM_EPS = 1e-6
LN_EPS = 1e-5
LRU_C = 8.0
ADAM_LR = 0.001
ADAM_B1 = 0.9
ADAM_B2 = 0.999
ADAM_EPS = 1e-08
ADAM_WD = 0.01
ADAM_STEP = 10

VMEM = pl.BlockSpec(memory_space=pltpu.VMEM)
ANY = pl.BlockSpec(memory_space=pl.ANY)
MESH = pl.DeviceIdType.MESH


def _cp(n_grid=0, vmem_mb=None):
    kw = {}
    if n_grid:
        kw["dimension_semantics"] = ("arbitrary",) * n_grid
    if vmem_mb:
        kw["vmem_limit_bytes"] = vmem_mb << 20
    return pltpu.CompilerParams(**kw)


def _sigmoid(x):
    return 0.5 * jnp.tanh(0.5 * x) + 0.5


def _silu_and_grad(x):
    s = _sigmoid(x)
    return x * s, s * (1.0 + x * (1.0 - s))


_GELU_K = 0.7978845608028654
_GELU_C = 0.044715


def _gelu_and_grad(x):
    x2 = x * x
    th = jnp.tanh(x * (_GELU_K + (_GELU_K * _GELU_C) * x2))
    p = 0.5 + 0.5 * th
    g = x * p
    dg = p + g * (1.0 - th) * (_GELU_K + (3.0 * _GELU_K * _GELU_C) * x2)
    return g, dg


def _softplus(x):
    return jnp.maximum(x, 0.0) + jnp.log1p(jnp.exp(-jnp.abs(x)))


def _lru_gate(pre, lam_row, d, off=None):
    off = 256 * d if off is None else off
    r = _sigmoid(pre[:, off:off + HD])
    gi = _sigmoid(pre[:, off + HD:off + 2 * HD])
    lam = lam_row[:, HD * d:HD * d + HD]
    sp = _softplus(-lam)
    la = (-LRU_C) * r * sp
    a = jnp.exp(la)
    x2 = 2.0 * la
    m2 = jnp.where(x2 > -1e-3, -x2 * (1.0 + 0.5 * x2), 1.0 - a * a)
    mult = jnp.sqrt(m2)
    return r, gi, lam, sp, a, mult


def _dot(a, b):
    return jnp.dot(a, b, preferred_element_type=F32)


def _dot_tn(a, b):
    return lax.dot_general(a, b, (((0,), (0,)), ((), ())), preferred_element_type=F32)


def _dot_nt(a, b):
    return lax.dot_general(a, b, (((1,), (1,)), ((), ())), preferred_element_type=F32)


def _mo(v, m):
    return v if isinstance(v, int) else pl.multiple_of(v, m)


def _zrows(h, n):
    return pl.ds(h, n, stride=NH)


def _gather_in(c, c_ctx, ada_w, ada_b_j, w_in, w_out, smalls):
    nch = [1, 4]
    wrows = lambda cc, q: (pl.ds(_mo(512 * cc, 16), 512) if q is None
                           else pl.ds(_mo(512 * cc + (512 // nch[1]) * q, 16), 512 // nch[1]))
    specs = [
        ((64, 256), F32, lambda r, jj, cc, q=None: r.at[pl.ds(_mo(16 * jj + 8 * cc, 8), 8), :]),
        ((D, 5120), BF16, lambda r, jj, cc, q=None: r.at[wrows(cc, q), pl.ds(_mo(1280 * jj, 128), 1280)]),
    ]
    halves = [lambda r, cc, q=None: r.at[pl.ds(_mo(8 * cc, 8), 8), :],
              lambda r, cc, q=None: r.at[wrows(cc, q), :]]
    na = len(specs)
    sem_base = [0, 6 * nch[0]]
    sidx = lambda a, q, k: sem_base[a] + 6 * q + k
    n_tiny = 6 * sum(nch)
    n_sem = n_tiny + 10

    def body(c_ref, cc_ref, ada_ref, adab_ref, win_ref, wout_ref, sm_ref,
             mods_o, call_o, sm_o, win_o, wol_o, adal_o, s_win, s_ada, s_wout, f_win, f_ada, f_wout, cslot, lhs, mbuf,
             send_sems, recv_sems, local_sems, load_sems):
        x, y, c = lax.axis_index("x"), lax.axis_index("y"), lax.axis_index("c")
        j = 2 * x + y
        dev = 2 * j + c
        sib = (x, y, 1 - c)
        chips = [(1 - x, y), (x, 1 - y), (1 - x, 1 - y)]
        cj = [2 * cx + cy for cx, cy in chips]
        outs = [sm_o, win_o]
        srcs = [sm_ref, s_win]

        def copy(idx, src, dst, to):
            return pltpu.make_async_remote_copy(src_ref=src, dst_ref=dst, send_sem=send_sems.at[idx],
                                                recv_sem=recv_sems.at[idx], device_id=to, device_id_type=MESH)

        sends = []

        def start(cp):
            cp.start()
            sends.append(cp)

        cslot[...] = jnp.zeros_like(cslot)
        cslot[0:1, :] = c_ref[...]
        my_slot = pl.ds(_mo(8 * dev, 8), 8)
        others = [sib] + [(*chips[k], c) for k in range(3)] + [(*chips[k], 1 - c) for k in range(3)]
        other_dev = [dev + 1 - 2 * c] + [2 * cj[k] + c for k in range(3)] + [2 * cj[k] + 1 - c for k in range(3)]
        base = n_tiny
        for r in range(7):
            start(copy(base + r, cslot, call_o.at[my_slot, :], others[r]))
        call_o[my_slot, :] = cslot[...]

        crow = 512 // nch[1]
        loads = []
        for cc in (c, 1 - c):
            for q in range(nch[1]):
                rows = pl.ds(_mo(512 * cc + crow * q, 16), crow)
                loads.append(pltpu.make_async_copy(win_ref.at[rows, :], f_win.at[rows, :], load_sems.at[len(loads)]))
        loads.append(pltpu.make_async_copy(ada_ref, f_ada, load_sems.at[len(loads)]))
        loads.append(pltpu.make_async_copy(wout_ref, f_wout, load_sems.at[len(loads)]))
        for ld in loads:
            ld.start()
        for k in range(2):
            start(copy(sidx(0, 0, k), halves[0](srcs[0], c), specs[0][2](outs[0], j, c), (*chips[k], c)))
        for q in range(nch[1]):
            loads[q].wait()
            rows = pl.ds(_mo(512 * c + crow * q, 16), crow)
            s_win[rows, :] = f_win[rows, :].astype(BF16)
            for k in range(2):
                start(copy(sidx(1, q, k), halves[1](s_win, c, q), specs[1][2](win_o, j, c, q), (*chips[k], c)))
        for q in range(nch[1]):
            loads[nch[1] + q].wait()
            rows = pl.ds(_mo(512 * (1 - c) + crow * q, 16), crow)
            s_win[rows, :] = f_win[rows, :].astype(BF16)
        local = []
        for a in range(na):
            for cc in range(2):
                lc = pltpu.make_async_copy(halves[a](srcs[a], cc), specs[a][2](outs[a], j, cc), local_sems.at[2 * a + cc])
                lc.start()
                local.append(lc)
        loads[2 * nch[1]].wait()
        s_ada[...] = f_ada[...].astype(BF16)
        loads[2 * nch[1] + 1].wait()
        s_wout[...] = f_wout[...].astype(BF16)
        for q, (src, dst) in enumerate([(s_wout, wol_o.at[pl.ds(_mo(512 * j, 16), 512), :]),
                                        (s_ada, adal_o.at[:, pl.ds(_mo(768 * j, 128), 768)])]):
            lc = pltpu.make_async_copy(src, dst, local_sems.at[2 * na + q])
            lc.start()
            local.append(lc)

        for r in range(7):
            slot = call_o.at[pl.ds(_mo(8 * other_dev[r], 8), 8), :]
            copy(base + r, slot, slot, sib).wait_recv()
        lhs[...] = jnp.zeros_like(lhs)
        for b in range(8):
            cv = call_o[8 * b:8 * b + 1, :]
            lhs[b:b + 1, :] = cv * _sigmoid(cv)
        cv = cc_ref[...]
        lhs[8:9, :] = cv * _sigmoid(cv)
        mbuf[j] = _dot(lhs[...].astype(BF16), s_ada[...]) + adab_ref[...]
        for k in range(3):
            start(copy(base + 7 + k, mbuf.at[j], mbuf.at[j], (*chips[k], c)))
        for k in range(3):
            copy(base + 7 + k, mbuf.at[cj[k]], mbuf.at[cj[k]], sib).wait_recv()
        mods_o[...] = jnp.zeros_like(mods_o)
        for jj in range(NCHIP):
            mods_o[0:1, 768 * jj:768 * jj + 768] = mbuf[jj, pl.ds(dev, 1), :]
            mods_o[1:2, 768 * jj:768 * jj + 768] = mbuf[jj, 8:9, :]

        kx = [1 - x, x, 1 - x]
        ky = [y, 1 - y, 1 - y]
        pick = lambda k, lst: jnp.where(k == 0, lst[0], jnp.where(k == 1, lst[1], lst[2]))
        for a in range(na):
            for q in range(nch[a]):
                for step, k in enumerate([c, 1 - c]):
                    reg = specs[a][2](outs[a], pick(k, cj), c, q)
                    copy(sidx(a, q, k), reg, reg, sib).wait_recv()
                    if step == 0:
                        start(copy(sidx(a, q, 2), reg, reg, (pick(1 - c, kx), pick(1 - c, ky), c)))
                    start(copy(sidx(a, q, 3 + k), reg, reg, sib))
        for a in range(na):
            for q in range(nch[a]):
                reg = specs[a][2](outs[a], cj[2], c, q)
                copy(sidx(a, q, 2), reg, reg, sib).wait_recv()
                start(copy(sidx(a, q, 5), reg, reg, sib))
        for a in range(na):
            for q in range(nch[a]):
                for k in range(3):
                    reg = specs[a][2](outs[a], cj[k], 1 - c, q)
                    copy(sidx(a, q, 3 + k), reg, reg, sib).wait_recv()
        for cp in sends:
            cp.wait_send()
        for lc in local:
            lc.wait()

    out_shape = (jax.ShapeDtypeStruct((8, 3 * D), F32), jax.ShapeDtypeStruct((64, D), F32),
                 jax.ShapeDtypeStruct(specs[0][0], F32), jax.ShapeDtypeStruct(specs[1][0], BF16),
                 jax.ShapeDtypeStruct((2048, D), BF16), jax.ShapeDtypeStruct((D, 3 * D), BF16))
    return pl.pallas_call(
        body, name="gather_in", out_shape=out_shape,
        in_specs=[VMEM, VMEM, ANY, VMEM, ANY, ANY, VMEM], out_specs=(VMEM, VMEM, VMEM, ANY, ANY, ANY),
        scratch_shapes=[pltpu.VMEM((D, 1280), BF16), pltpu.VMEM((D, 768), BF16), pltpu.VMEM((512, D), BF16),
                        pltpu.VMEM((D, 1280), F32), pltpu.VMEM((D, 768), F32), pltpu.VMEM((512, D), F32),
                        pltpu.VMEM((8, D), F32), pltpu.VMEM((16, D), F32), pltpu.VMEM((NCHIP, 16, 768), F32),
                        pltpu.SemaphoreType.DMA((n_sem,)), pltpu.SemaphoreType.DMA((n_sem,)),
                        pltpu.SemaphoreType.DMA((2 * na + 2,)), pltpu.SemaphoreType.DMA((2 * nch[1] + 2,))],
        compiler_params=_cp(vmem_mb=56),
    )(c, c_ctx, ada_w, ada_b_j, w_in, w_out, smalls)


HBM = pl.BlockSpec(memory_space=pltpu.HBM)
SEM = pl.BlockSpec(memory_space=pltpu.SEMAPHORE)


def _late_gather_regions(x, y, c):
    chips = [(1 - x, y), (x, 1 - y), (1 - x, 1 - y)]
    wo_reg = lambda r, jj, cc: r.at[pl.ds(_mo(512 * jj + 256 * cc, 16), 256), :]
    ada_reg = lambda r, jj, cc: r.at[pl.ds(_mo(512 * cc, 16), 512), pl.ds(_mo(768 * jj, 128), 768)]
    return chips, wo_reg, ada_reg


def _late_gather_start(wo_land, ada_land):
    def body(wol_ref, adal_ref, wo_ss, wo_rs, ada_ss, ada_rs, wol_thru, adal_thru, token):
        x, y, c = lax.axis_index("x"), lax.axis_index("y"), lax.axis_index("c")
        j = 2 * x + y
        chips, wo_reg, ada_reg = _late_gather_regions(x, y, c)
        for k in range(3):
            for cc in range(2):
                pltpu.make_async_remote_copy(src_ref=wo_reg(wol_ref, j, c), dst_ref=wo_reg(wol_ref, j, c),
                                             send_sem=wo_ss.at[2 * k + cc], recv_sem=wo_rs.at[2 * k + c],
                                             device_id=(*chips[k], cc), device_id_type=MESH).start()
        for k in range(3):
            for cc in range(2):
                pltpu.make_async_remote_copy(src_ref=ada_reg(adal_ref, j, c), dst_ref=ada_reg(adal_ref, j, c),
                                             send_sem=ada_ss.at[2 * k + cc], recv_sem=ada_rs.at[2 * k + c],
                                             device_id=(*chips[k], cc), device_id_type=MESH).start()
        token[...] = jnp.zeros_like(token)

    sems = pltpu.SemaphoreType.DMA((6,))
    return pl.pallas_call(
        body, name="late_gather_start",
        out_shape=(sems, sems, sems, sems, pltpu.HBM(wo_land.shape, BF16), pltpu.HBM(ada_land.shape, BF16),
                   jax.ShapeDtypeStruct((8, 128), F32)),
        in_specs=(HBM, HBM), out_specs=(SEM, SEM, SEM, SEM, HBM, HBM, VMEM), input_output_aliases={0: 4, 1: 5},
        compiler_params=pltpu.CompilerParams(has_side_effects=pltpu.SideEffectType.DATAFLOW_SIDE_EFFECTING),
    )(pltpu.with_memory_space_constraint(wo_land, pltpu.HBM), pltpu.with_memory_space_constraint(ada_land, pltpu.HBM))


def _late_gather_wait(land, send_sems, recv_sems, which, after, name):
    def body(land_ref, ss, rs, after_ref, land_out):
        x, y, c = lax.axis_index("x"), lax.axis_index("y"), lax.axis_index("c")
        j = 2 * x + y
        chips, wo_reg, ada_reg = _late_gather_regions(x, y, c)
        reg = wo_reg if which == "w_out" else ada_reg
        for k in range(3):
            kj = 2 * chips[k][0] + chips[k][1]
            for cc in range(2):
                cp = pltpu.make_async_remote_copy(src_ref=reg(land_ref, j, c), dst_ref=reg(land_ref, kj, cc),
                                                  send_sem=ss.at[2 * k + cc], recv_sem=rs.at[2 * k + cc],
                                                  device_id=(*chips[k], cc), device_id_type=MESH)
                cp.wait_send()
                cp.wait_recv()

    return pl.pallas_call(
        body, name=name, out_shape=pltpu.HBM(land.shape, land.dtype),
        in_specs=(HBM, SEM, SEM, ANY), out_specs=HBM, input_output_aliases={0: 0},
        compiler_params=pltpu.CompilerParams(has_side_effects=pltpu.SideEffectType.DATAFLOW_SIDE_EFFECTING),
    )(land, send_sems, recv_sems, after)


RCHUNK = 16


def _grads_reduce(hn, dzs, hn_c, y, do, pack):
    rp = pack.shape[0]
    hp = rp // 2
    assert hp % RCHUNK == 0
    wi_w = 1280
    lx, lc = hn.shape[0], hn_c.shape[0]
    lt = lx + lc
    n_dz = len(dzs)

    def body(*refs):
        hn_hbm, dz_hbm = refs[0], refs[1:1 + n_dz]
        hnc_hbm, y_hbm, do_hbm, pk_hbm, wi_out, wo_out, pk_out = refs[1 + n_dz:8 + n_dz]
        (hn_mine, hn_other, dzbuf, wi_other, wi_mine, wi_recv, wi_send, wi_rb,
         y_blk, do_mine, do_other, wo_other, wo_mine, wo_recv, wo_send, wo_rb,
         pk_mine, pk_recv, pk_send, pk_rb, pk_own, send_sems, recv_sems, local_sems) = refs[8 + n_dz:]
        x, y, c = lax.axis_index("x"), lax.axis_index("y"), lax.axis_index("c")
        j = 2 * x + y
        sib = (x, y, 1 - c)
        chips = [(1 - x, y), (x, 1 - y), (1 - x, 1 - y)]
        cj = [2 * cx + cy for cx, cy in chips]
        near = (jnp.where(c == 0, 1 - x, x), jnp.where(c == 0, y, 1 - y), c)
        slabs = [cj[2], cj[0], cj[1], j]

        def copy(k, src, dst, to):
            return pltpu.make_async_remote_copy(src_ref=src, dst_ref=dst, send_sem=send_sems.at[k],
                                                recv_sem=recv_sems.at[k], device_id=to, device_id_type=MESH)

        def local(k, src, dst):
            cp = pltpu.make_async_copy(src, dst, local_sems.at[k])
            cp.start()
            return cp

        rows_half = lambda r, cc, n: r.at[pl.ds(_mo(cc * n, 16), n), :]
        cols_half = lambda r, cc, n: r.at[:, pl.ds(_mo(cc * n, 128), n)]
        pk_piece = lambda r, cc, jj: r.at[pl.ds(_mo(cc * hp, 16), hp), pl.ds(_mo(jj * 128, 128), 128)]

        sends = []

        def start(cp):
            cp.start()
            sends.append(cp)

        def dz_pieces(s):
            g0 = wi_w * s
            k0, off0 = g0 // D, g0 % D
            w0 = min(D - off0, wi_w)
            pieces = [(k0, off0, w0, 0)]
            if w0 < wi_w:
                pieces.append((k0 + 1, 0, wi_w - w0, w0))
            return pieces

        def dz_copies(s):
            cps = []
            for q, (k, off, w, dst) in enumerate(dz_pieces(s)):
                cps.append(pltpu.make_async_copy(dz_hbm[k].at[pl.ds(lc if k == 0 else 0, lx), pl.ds(off, w)],
                                                 dzbuf.at[pl.ds(0, lx), pl.ds(dst, w)], local_sems.at[11 + q]))
            if s == 0:
                cps.append(pltpu.make_async_copy(dz_hbm[0].at[pl.ds(0, lc), :], dzbuf.at[pl.ds(lx, lc), pl.ds(0, D)],
                                                 local_sems.at[13]))
            return cps

        def dz_load(sl):
            for s in range(NCHIP):
                @pl.when(sl == s)
                def _():
                    if s == 0:
                        dzbuf[pl.ds(lx, lc), pl.ds(D, wi_w - D)] = jnp.zeros((lc, wi_w - D), BF16)
                    else:
                        dzbuf[pl.ds(lx, lc), :] = jnp.zeros((lc, wi_w), BF16)
                    for cp in dz_copies(s):
                        cp.start()

        def dz_wait(sl):
            for s in range(NCHIP):
                @pl.when(sl == s)
                def _():
                    for cp in dz_copies(s):
                        cp.wait()

        l_pk = local(0, rows_half(pk_hbm, c, hp), pk_mine)
        start(copy(0, rows_half(pk_hbm, 1 - c, hp), pk_recv, sib))
        col = lambda r, cc: r.at[:, pl.ds(_mo(cc * 512, 128), 512)]
        do_loads = [local(7, col(do_hbm, c), do_mine), local(14, col(do_hbm, 1 - c), do_other)]
        hn_loads = [local(2, col(hn_hbm, c), hn_mine.at[pl.ds(0, lx), :]),
                    local(3, col(hnc_hbm, c), hn_mine.at[pl.ds(lx, lc), :]),
                    local(4, col(hn_hbm, 1 - c), hn_other.at[pl.ds(0, lx), :]),
                    local(5, col(hnc_hbm, 1 - c), hn_other.at[pl.ds(lx, lc), :])]
        y_copy = lambda s: pltpu.make_async_copy(col(y_hbm, slabs[s]), y_blk, local_sems.at[1])
        y_copy(0).start()
        dz_load(slabs[0])

        def pair_sum(mine, recv, send, nrows, keep, relayed=None):
            def step(i, carry):
                rows = pl.ds(_mo(i * RCHUNK, RCHUNK), RCHUNK)
                s = mine[rows, :] + recv[rows, :].astype(F32)
                if relayed is not None:
                    s = s + relayed[rows, :].astype(F32)
                if keep:
                    mine[rows, :] = s
                if send is not None:
                    send[rows, :] = s.astype(BF16)
                return carry
            lax.fori_loop(0, nrows // RCHUNK, step, 0)

        def chip_sum(own, rb, nrows, terms=(0, 1, 2)):
            def step(i, carry):
                rows = pl.ds(_mo(i * RCHUNK, RCHUNK), RCHUNK)
                acc = own[rows, :]
                for q in terms:
                    acc = acc + rb[q, rows, :].astype(F32)
                own[rows, :] = acc
                return carry
            lax.fori_loop(0, nrows // RCHUNK, step, 0)

        w_in_g = dict(other=wi_other, mine=wi_mine, recv=wi_recv, send=wi_send, rb=wi_rb, p1_sems=(2, 3, 4, 5), p2_sem=12,
                      p1=[None] * NCHIP, wait_load=lambda s: dz_wait(slabs[s]), load=lambda s: dz_load(slabs[s]),
                      dot_other=lambda: _dot_tn(hn_other[...], dzbuf[...]), dot_mine=lambda: _dot_tn(hn_mine[...], dzbuf[...]))
        w_out_g = dict(other=wo_other, mine=wo_mine, recv=wo_recv, send=wo_send, rb=wo_rb, p1_sems=(1, 24, 25, 26), p2_sem=9,
                       p1=[None] * NCHIP, wait_load=lambda s: y_copy(s).wait(), load=lambda s: y_copy(s).start(),
                       dot_other=lambda: _dot_tn(y_blk[...], do_other[...]), dot_mine=lambda: _dot_tn(y_blk[...], do_mine[...]))

        def piece_matmuls(g, s):
            if s >= 2:
                g["p1"][s - 2].wait_send()
            g["wait_load"](s)
            g["other"][s % 2] = g["dot_other"]().astype(BF16)
            g["p1"][s] = copy(g["p1_sems"][s], g["other"].at[s % 2], g["recv"].at[s], sib)
            g["p1"][s].start()
            g["mine"][s % 2] = g["dot_mine"]()
            if s + 1 < NCHIP:
                g["load"](s + 1)

        def piece_finish(g, s):
            mine, recv, send, rb, p2 = g["mine"].at[s % 2], g["recv"].at[s], g["send"], g["rb"], g["p2_sem"]
            nrows = mine.shape[0]
            copy(g["p1_sems"][s], recv, recv, sib).wait_recv()
            if s == 3:
                pair_sum(mine, recv, None, nrows, True)
                return
            if s == 0:
                pair_sum(mine, recv, send.at[0], nrows, False)
                start(copy(p2, send.at[0], rb.at[0], near))
                return
            adds_relayed = c == (1 if s == 1 else 0)

            @pl.when(adds_relayed)
            def _():
                copy(p2, rb.at[0], rb.at[0], sib).wait_recv()
                pair_sum(mine, recv, send.at[s], nrows, False, rb.at[0])

            @pl.when(jnp.logical_not(adds_relayed))
            def _():
                pair_sum(mine, recv, send.at[s], nrows, False)
            start(copy(p2 + s, send.at[s], rb.at[s], (*chips[s - 1], c)))

        def piece_total(g):
            for k in (1, 2):
                copy(g["p2_sem"] + k, g["rb"].at[k], g["rb"].at[k], sib).wait_recv()
            chip_sum(g["mine"].at[1], g["rb"], g["mine"].shape[1], (1, 2))

        for cp in do_loads:
            cp.wait()
        piece_matmuls(w_out_g, 0)
        piece_matmuls(w_out_g, 1)
        piece_finish(w_out_g, 0)
        piece_matmuls(w_out_g, 2)
        piece_finish(w_out_g, 1)
        piece_matmuls(w_out_g, 3)
        piece_finish(w_out_g, 2)

        for cp in hn_loads:
            cp.wait()
        piece_matmuls(w_in_g, 0)

        l_pk.wait()
        copy(0, pk_recv, pk_recv, sib).wait_recv()
        pair_sum(pk_mine, pk_recv, pk_send, hp, True)
        for k in range(3):
            start(copy(6 + k, pk_send.at[:, pl.ds(_mo(cj[k] * 128, 128), 128)], pk_rb.at[k], (*chips[k], c)))
        l_pk_own = local(6, pk_mine.at[:, pl.ds(_mo(j * 128, 128), 128)], pk_own)

        piece_matmuls(w_in_g, 1)
        piece_finish(w_in_g, 0)

        l_pk_own.wait()
        for k in range(3):
            copy(6 + k, pk_rb.at[k], pk_rb.at[k], sib).wait_recv()
        chip_sum(pk_own, pk_rb, hp)
        l_pk_out = local(8, pk_own, pk_piece(pk_out, c, j))
        start(copy(15, pk_own, pk_piece(pk_out, c, j), sib))
        for k in range(2):
            start(copy(16 + k, pk_own, pk_piece(pk_out, c, j), (*chips[k], c)))

        piece_matmuls(w_in_g, 2)
        piece_finish(w_in_g, 1)
        piece_matmuls(w_in_g, 3)
        piece_finish(w_in_g, 2)

        piece_finish(w_out_g, 3)
        piece_total(w_out_g)
        l_wo_out = local(9, wo_mine.at[1], cols_half(wo_out, c, 512))
        start(copy(22, wo_mine.at[1], cols_half(wo_out, c, 512), sib))

        far = (jnp.where(c == 0, x, 1 - x), jnp.where(c == 0, 1 - y, y), c)
        for step, k in enumerate([c, 1 - c, 2]):
            reg = pk_piece(pk_out, c, jnp.where(k == 0, cj[0], jnp.where(k == 1, cj[1], cj[2])))
            copy(16 + k, reg, reg, sib).wait_recv()
            if step == 0:
                start(copy(18, reg, reg, far))
            start(copy(19 + k, reg, reg, sib))

        piece_finish(w_in_g, 3)
        piece_total(w_in_g)
        l_wi_out = local(10, wi_mine.at[1], rows_half(wi_out, c, 512))
        start(copy(23, wi_mine.at[1], rows_half(wi_out, c, 512), sib))

        reg = pk_piece(pk_out, 1 - c, j)
        copy(15, reg, reg, sib).wait_recv()
        for k in range(3):
            reg = pk_piece(pk_out, 1 - c, cj[k])
            copy(19 + k, reg, reg, sib).wait_recv()
        reg = cols_half(wo_out, 1 - c, 512)
        copy(22, reg, reg, sib).wait_recv()
        reg = rows_half(wi_out, 1 - c, 512)
        copy(23, reg, reg, sib).wait_recv()
        for cp in sends + w_in_g["p1"][2:] + w_out_g["p1"][2:]:
            cp.wait_send()
        for cp in (l_pk_out, l_wo_out, l_wi_out):
            cp.wait()

    return pl.pallas_call(
        body, name="grads_reduce",
        out_shape=(jax.ShapeDtypeStruct((D, wi_w), F32), jax.ShapeDtypeStruct((512, D), F32),
                   jax.ShapeDtypeStruct(pack.shape, F32)),
        in_specs=[ANY] * (5 + n_dz), out_specs=(ANY,) * 3,
        scratch_shapes=[
            pltpu.VMEM((lt, 512), BF16), pltpu.VMEM((lt, 512), BF16), pltpu.VMEM((lt, wi_w), BF16),
            pltpu.VMEM((2, 512, wi_w), BF16), pltpu.VMEM((2, 512, wi_w), F32), pltpu.VMEM((4, 512, wi_w), BF16),
            pltpu.VMEM((3, 512, wi_w), BF16), pltpu.VMEM((3, 512, wi_w), BF16),
            pltpu.VMEM((lx, 512), BF16), pltpu.VMEM((lx, 512), BF16), pltpu.VMEM((lx, 512), BF16),
            pltpu.VMEM((2, 512, 512), BF16), pltpu.VMEM((2, 512, 512), F32), pltpu.VMEM((4, 512, 512), BF16),
            pltpu.VMEM((3, 512, 512), BF16), pltpu.VMEM((3, 512, 512), BF16),
            pltpu.VMEM((hp, 512), F32), pltpu.VMEM((hp, 512), F32), pltpu.VMEM((hp, 512), BF16),
            pltpu.VMEM((3, hp, 128), BF16), pltpu.VMEM((hp, 128), F32),
            pltpu.SemaphoreType.DMA((27,)), pltpu.SemaphoreType.DMA((27,)), pltpu.SemaphoreType.DMA((15,))],
        compiler_params=_cp(vmem_mb=56),
    )(hn, *dzs, hn_c, y, do, pack)


def _ada_bwd(c_all, dmx_all_j, dmc_j, dmx_all, dmc, c_ctx, ada_w_full):
    def body(c_ref, dmxj_ref, dmcj_ref, dmx_ref, dmc_ref, cc_ref, w_ref, gw_ref, gb_ref, gc_ref, lhs, rhs, dm8):
        lhs[...] = jnp.zeros_like(lhs)
        rhs[...] = jnp.zeros_like(rhs)
        cv = c_ref[...]
        lhs[0:8, :] = cv * _sigmoid(cv)
        cc = cc_ref[...]
        a_c, da_c = _silu_and_grad(cc)
        lhs[8:9, :] = a_c
        rhs[0:8, :] = dmxj_ref[...]
        rhs[8:9, :] = dmcj_ref[...]
        gw_ref[...] = _dot_tn(lhs[...].astype(BF16), rhs[...].astype(BF16))
        gb_ref[...] = jnp.sum(dmx_ref[...], axis=0, keepdims=True) + dmc_ref[...]
        dm8[...] = jnp.zeros_like(dm8)
        dm8[0:1, :] = dmc_ref[...]
        da = _dot_nt(dm8[...].astype(BF16), w_ref[...])
        gc_ref[...] = da[0:1, :] * da_c

    return pl.pallas_call(
        body, name="ada_bwd",
        out_shape=(jax.ShapeDtypeStruct((D, 768), F32), jax.ShapeDtypeStruct((1, 3 * D), F32),
                   jax.ShapeDtypeStruct((1, D), F32)),
        in_specs=[VMEM] * 7, out_specs=(VMEM,) * 3,
        scratch_shapes=[pltpu.VMEM((16, D), F32), pltpu.VMEM((16, 768), F32), pltpu.VMEM((8, 3 * D), F32)],
        compiler_params=_cp(vmem_mb=32),
    )(c_all, dmx_all_j, dmc_j, dmx_all, dmc, c_ctx, ada_w_full)


def _proj(x, mods, mrow, norm_g, w_full, nk, name, sgu=None, z0_rows=None, z0_off=0, z0_into=None):
    lx = x.shape[0]
    n = lx // T
    order = [2, 3, 0, 1, 4] if sgu is not None else list(range(nk))

    def body(x_ref, sh_ref, sc_ref, ng_ref, *rest):
        w_refs, rest = rest[:nk], rest[nk:]
        if sgu is not None:
            g_ref, b_ref, sw_ref, bt_ref = rest[:4]
            rest = rest[4:]
        if z0_into is not None:
            rest = rest[1:]
        hn_ref, z_refs = rest[0], rest[1:1 + nk]
        xv = x_ref[...]
        r = lax.rsqrt(jnp.mean(xv * xv, axis=-1, keepdims=True) + NORM_EPS)
        hn = (xv * r) * ng_ref[...] * (1.0 + sc_ref[mrow:mrow + 1, :]) + sh_ref[mrow:mrow + 1, :]
        hb = hn.astype(BF16)
        hn_ref[...] = hb
        for k in order[:2]:
            z_refs[k][...] = _dot(hb, w_refs[k][...])
        if sgu is not None:
            ys_ref, mixed_s = rest[1 + nk], rest[2 + nk]
            for ch in range(T // HD):
                rows = slice(HD * ch, HD * ch + HD)
                ug = _sgu_parts(z_refs[2][rows, :], z_refs[3][rows, :], g_ref[...], b_ref[...], sw_ref, bt_ref,
                                mixed_s)[0]
                ys_ref[rows, :] = ug * mixed_s[...]
        for k in order[2:]:
            z_refs[k][...] = _dot(hb, w_refs[k][...])

    row = pl.BlockSpec((T, D), lambda i: (i, 0))
    vec = pl.BlockSpec((1, D), lambda i: (0, 0))
    in_specs = [row, pl.BlockSpec((8, D), lambda i: (0, 0)), pl.BlockSpec((8, D), lambda i: (0, 1)), vec]
    in_specs += [pl.BlockSpec((D, D), lambda i, k=k: (0, k)) for k in range(nk)]
    args = [x, mods, mods, norm_g] + [w_full] * nk
    out_shape = (jax.ShapeDtypeStruct((lx, D), BF16),) + tuple(jax.ShapeDtypeStruct((lx, D), F32) for _ in range(nk))
    out_specs = [row] * len(out_shape)
    scratch = []
    aliases = {}
    if sgu is not None:
        in_specs += [vec, vec, pl.BlockSpec((NH, HD, HD), lambda i: (0, 0, 0)), pl.BlockSpec((HD, NH), lambda i: (0, 0))]
        args += list(sgu)
        out_shape += (jax.ShapeDtypeStruct((lx, D), F32),)
        out_specs.append(row)
        scratch = [pltpu.VMEM((HD, D), F32)]
    if z0_rows is not None:
        out_shape = out_shape[:1] + (jax.ShapeDtypeStruct((z0_rows, D), F32),) + out_shape[2:]
        out_specs[1] = pl.BlockSpec((T, D), lambda i: (i + z0_off, 0))
    if z0_into is not None:
        out_shape = out_shape[:1] + (jax.ShapeDtypeStruct(z0_into.shape, F32),) + out_shape[2:]
        aliases = {len(args): 1}
        in_specs.append(ANY)
        args.append(z0_into)
    return pl.pallas_call(
        body, name=name, grid=(n,), out_shape=out_shape, in_specs=in_specs, out_specs=tuple(out_specs),
        scratch_shapes=scratch, input_output_aliases=aliases, compiler_params=_cp(1, vmem_mb=56),
    )(*args)


def _tile_specs(rows_per_pos, width, n_tiles, tile):
    last = n_tiles * (T // 8) - 1
    r = rows_per_pos
    return [pl.BlockSpec((T * r, width), lambda i: (tile(i), 0)),
            pl.BlockSpec((8 * r, width), lambda i: (jnp.maximum(tile(i) * (T // 8) - 1, 0), 0)),
            pl.BlockSpec((8 * r, width), lambda i: (jnp.minimum((tile(i) + 1) * (T // 8), last), 0))]


def _has_prev(tile):
    return tile >= 2


def _has_next(tile, nt):
    return jnp.logical_and(tile >= 1, tile < nt - 1)


ZT = pl.BlockSpec((T * NH, HD), lambda i: (i, 0))
CONV_CHUNK = 32


SCAN_SUB = 8


def _scan_tile(chains, post, carry_ref):
    blk = T // SCAN_SUB

    def step(k, state):
        new = []
        for ci, (a_ref, x_ref, o_ref, q_ref, reverse, xscale) in enumerate(chains):
            for q in range(SCAN_SUB):
                s, p = state[ci * SCAN_SUB + q]
                t = (q + 1) * blk - 1 - k if reverse else q * blk + k
                r = pl.ds(_mo(t * NH, NH), NH)
                a = a_ref[r, :]
                x = x_ref[r, :] if xscale is None else x_ref[r, :] * xscale
                if post:
                    o = x + s
                    o_ref[r, :] = o
                    q_ref[r, :] = p
                    new.append((a * o, a * p))
                else:
                    o = a * s + x
                    p = a * p
                    o_ref[r, :] = o
                    q_ref[r, :] = p
                    new.append((o, p))
        return tuple(new)

    zero = jnp.zeros((NH, HD), F32)
    one = jnp.ones((NH, HD), F32)
    final = lax.fori_loop(0, blk, step, tuple((zero, one) for _ in range(len(chains) * SCAN_SUB)))
    for ci, (a_ref, x_ref, o_ref, q_ref, reverse, xscale) in enumerate(chains):
        carry = carry_ref[ci]
        for q in (range(SCAN_SUB - 1, -1, -1) if reverse else range(SCAN_SUB)):
            rows = pl.ds(q * blk * NH, blk * NH)
            fixed = o_ref[rows, :].reshape(blk, NH, HD) + q_ref[rows, :].reshape(blk, NH, HD) * carry[None]
            o_ref[rows, :] = fixed.reshape(blk * NH, HD)
            s_loc, p_loc = final[ci * SCAN_SUB + q]
            carry = s_loc + p_loc * carry
        carry_ref[ci] = carry


def _lru_fwd(xa, conv_wz, conv_bz, wcat, bcat, lamcat, name):
    lx = xa.shape[0]
    n = lx // T
    tile_u = lambda i: i
    tile_d = lambda i: jnp.where(i == 0, 0, n - i)

    def body(xm_u, xp_u, xn_u, xm_d, xp_d, xn_d, cw, cb, w_ref, b_ref, lam_ref,
             xcz_o, af_o, ab_o, hf_o, hb_o, gf_o, gb_o, fu, fd, pad, xc_d, x_u, x_d, q_u, q_d, carry):
        i = pl.program_id(0)

        @pl.when(i == 0)
        def _():
            carry[...] = jnp.zeros_like(carry)

        def conv_gates(xm, xp, xn, tile, d, xc_ref, a_ref, x_ref, g_ref):
            pmask = jnp.where(_has_prev(tile), 1.0, 0.0)
            nmask = jnp.where(_has_next(tile, n), 1.0, 0.0)
            for h in range(NH):
                cols = slice(HD * h, HD * h + HD)
                pad[_zrows(h, 8), :] = xp[:, cols] * pmask
                pad[pl.ds(8 * NH + h, T, stride=NH), :] = xm[:, cols]
                pad[pl.ds((T + 8) * NH + h, 8, stride=NH), :] = xn[:, cols] * nmask

            def conv_chunk(ci, c_):
                base = pl.multiple_of(ci * (CONV_CHUNK * NH), CONV_CHUNK * NH)
                acc = None
                for k in range(4):
                    sl = pad[pl.ds(base + (7 + k) * NH, CONV_CHUNK * NH), :].reshape(CONV_CHUNK, NH, HD)
                    term = sl * cw[k][None]
                    acc = term if acc is None else acc + term
                acc = acc + cb[...][None]
                xc_ref[pl.ds(base, CONV_CHUNK * NH), :] = acc.reshape(CONV_CHUNK * NH, HD)
                return c_
            lax.fori_loop(0, T // CONV_CHUNK, conv_chunk, 0)

            for h in range(NH):
                xch = xc_ref[_zrows(h, T), :]
                pre = _dot(xch.astype(BF16), w_ref[h, :, 256 * d:256 * d + 256]) + b_ref[h:h + 1, 256 * d:256 * d + 256]
                r, gi, _, _, a, mult = _lru_gate(pre, lam_ref[h:h + 1, :], d, 0)
                a_ref[_zrows(h, T), :] = a
                x_ref[_zrows(h, T), :] = mult * gi * xch
                for q, val in enumerate((r, gi, mult)):
                    g_ref[:, q * D + HD * h:q * D + HD * h + HD] = val

        conv_gates(xm_u, xp_u, xn_u, tile_u(i), 0, xcz_o, af_o, x_u, gf_o)
        conv_gates(xm_d, xp_d, xn_d, tile_d(i), 1, xc_d, ab_o, x_d, gb_o)

        _scan_tile([(af_o, x_u, hf_o, q_u, False, None), (ab_o, x_d, hb_o, q_d, True, None)], False, carry)

        @pl.when(i == 0)
        def _():
            fu[...] = carry[0]
            fd[...] = carry[1]

    full = lambda shape: pl.BlockSpec(shape, lambda i: (0,) * len(shape))
    st = full((NH, HD))
    in_specs = _tile_specs(1, D, n, tile_u) + _tile_specs(1, D, n, tile_d)
    in_specs += [full((4, NH, HD)), st, full((NH, HD, 4 * HD)), full((NH, 4 * HD)), full((NH, 2 * HD))]
    up = pl.BlockSpec((T * NH, HD), lambda i: (tile_u(i), 0))
    dn = pl.BlockSpec((T * NH, HD), lambda i: (tile_d(i), 0))
    zs = jax.ShapeDtypeStruct((lx * NH, HD), F32)
    ss = jax.ShapeDtypeStruct((NH, HD), F32)
    zbuf = pltpu.VMEM((T * NH, HD), F32)
    gs = jax.ShapeDtypeStruct((lx, 3 * D), F32)
    g_up = pl.BlockSpec((T, 3 * D), lambda i: (tile_u(i), 0))
    g_dn = pl.BlockSpec((T, 3 * D), lambda i: (tile_d(i), 0))
    return pl.pallas_call(
        body, name=name, grid=(n,), out_shape=(zs,) * 5 + (gs, gs, ss, ss), in_specs=in_specs,
        out_specs=(up, up, dn, up, dn, g_up, g_dn, st, st),
        scratch_shapes=[pltpu.VMEM(((T + 16) * NH, HD), F32), zbuf, zbuf, zbuf, zbuf, zbuf,
                        pltpu.VMEM((2, NH, HD), F32)],
        compiler_params=_cp(1, vmem_mb=48),
    )(xa, xa, xa, xa, xa, xa, conv_wz, conv_bz, wcat, bcat, lamcat)


def _sgu_parts(u, v, lng, lnb, w_ref, bt_ref, mixed_s):
    ug, dug = _gelu_and_grad(u)
    vg, dvg = _gelu_and_grad(v)
    mu = jnp.mean(vg, axis=-1, keepdims=True)
    vc = vg - mu
    rstd = lax.rsqrt(jnp.mean(vc * vc, axis=-1, keepdims=True) + LN_EPS)
    vh = vc * rstd
    vn = (vh * lng + lnb).astype(BF16)
    for g in range(NH):
        cols = slice(HD * g, HD * g + HD)
        mixed_s[:, cols] = _dot(w_ref[g], vn[:, cols]) + bt_ref[:, g:g + 1]
    return ug, dug, dvg, rstd, vh, vn


def _sgu_bwd_chunk(u, v, dys_v, lng, lnb, w_ref, bt_ref, mixed_s, dvn_s, dw_ref, db_ref, dg_ref, dbl_ref):
    ug, dug, dvg, rstd, vh, vn = _sgu_parts(u, v, lng, lnb, w_ref, bt_ref, mixed_s)
    du = (dys_v * mixed_s[...] * dug).astype(BF16)
    dmix = dys_v * ug
    ones = jnp.ones((8, HD), BF16)
    for g in range(NH):
        cols = slice(HD * g, HD * g + HD)
        dm = dmix[:, cols]
        hi = dm.astype(BF16)
        lo = (dm - hi.astype(F32)).astype(BF16)
        dw_ref[g] += _dot_nt(hi, vn[:, cols])
        db_ref[g:g + 1, :] += (_dot_nt(ones, hi) + _dot_nt(ones, lo))[0:1, :]
        dvn_s[:, cols] = _dot_tn(w_ref[g], hi)
    dvn = dvn_s[...]
    dg_ref[...] += jnp.sum(dvn * vh, axis=0, keepdims=True)
    dbl_ref[...] += jnp.sum(dvn, axis=0, keepdims=True)
    dvh = dvn * lng
    dvg_in = rstd * (dvh - jnp.mean(dvh, axis=-1, keepdims=True) - vh * jnp.mean(dvh * vh, axis=-1, keepdims=True))
    return du, (dvg_in * dvg).astype(BF16)


def _out_fwd_bwd(hf_z, hb_z, ga, gb, ys, x, tgt, mods, final_g, w_out_full):
    lx = x.shape[0]
    n = lx // T

    def body(hf_ref, hb_ref, ga_ref, gb_ref, ys_ref, x_ref, t_ref, gx_ref, fg_ref, w_ref,
             loss_ref, dfg_ref, dgx_ref, dxn_ref, y_ref, do_ref, dga_ref, dgb_ref, dyl_ref, dys_ref, yl_s):
        i = pl.program_id(0)

        @pl.when(i == 0)
        def _():
            loss_ref[...] = jnp.zeros_like(loss_ref)
            dfg_ref[...] = jnp.zeros_like(dfg_ref)
            dgx_ref[...] = jnp.zeros_like(dgx_ref)

        for h in range(NH):
            yl_s[:, HD * h:HD * h + HD] = hf_ref[_zrows(h, T), :] + hb_ref[_zrows(h, T), :]
        yl = yl_s[...]
        gav = ga_ref[...]
        gbv = gb_ref[...]
        sa, dsa = _silu_and_grad(gav)
        sb, dsb = _silu_and_grad(gbv)
        ysv = ys_ref[...]
        y_ref[:, 0:D] = (yl * sa).astype(BF16)
        y_ref[:, D:2 * D] = (ysv * sb).astype(BF16)
        o = _dot(y_ref[...], w_ref[...])
        gx = gx_ref[0:1, :]
        xnew = x_ref[...] + gx * o
        r2 = lax.rsqrt(jnp.mean(xnew * xnew, axis=-1, keepdims=True) + NORM_EPS)
        xh = xnew * r2
        fg = fg_ref[...]
        err = xh * fg - t_ref[...]
        loss_ref[...] += 0.5 * jnp.sum(jnp.mean(err * err, axis=-1, keepdims=True), axis=0, keepdims=True)

        @pl.when(i == n - 1)
        def _():
            lp = loss_ref[...]
            lp1 = lp.astype(BF16).astype(F32)
            lp2 = (lp - lp1).astype(BF16).astype(F32)
            lp3 = (lp - lp1 - lp2).astype(BF16).astype(F32)
            lane = lax.broadcasted_iota(jnp.int32, lp.shape, 1)
            loss_ref[...] = jnp.where(lane == 0, lp1, jnp.where(lane == 1, lp2, jnp.where(lane == 2, lp3, 0.0)))
        dout = err * (1.0 / D)
        dfg_ref[...] += jnp.sum(dout * xh, axis=0, keepdims=True)
        dxh = dout * fg
        dxn = r2 * (dxh - xh * jnp.mean(dxh * xh, axis=-1, keepdims=True))
        dxn_ref[...] = dxn
        dgx_ref[...] += jnp.sum(dxn * o, axis=0, keepdims=True)
        do = (dxn * gx).astype(BF16)
        do_ref[...] = do
        dy = _dot_nt(do, w_ref[...])
        dy1 = dy[:, 0:D]
        dy2 = dy[:, D:2 * D]
        dga_ref[...] = (dy1 * yl * dsa).astype(BF16)
        dgb_ref[...] = (dy2 * ysv * dsb).astype(BF16)
        dys_ref[...] = dy2 * sb
        yl_s[...] = dy1 * sa
        for h in range(NH):
            dyl_ref[_zrows(h, T), :] = yl_s[:, HD * h:HD * h + HD]

    row = pl.BlockSpec((T, D), lambda i: (i, 0))
    vec = pl.BlockSpec((1, D), lambda i: (0, 0))
    zlat = pl.BlockSpec((T * NH, HD), lambda i: (i + 1, 0))
    in_specs = [zlat, zlat, row, row, row, row, row, pl.BlockSpec((8, D), lambda i: (0, 2)), vec,
                pl.BlockSpec((2 * D, D), lambda i: (0, 0))]
    out_shape = (jax.ShapeDtypeStruct((1, D), F32), jax.ShapeDtypeStruct((1, D), F32), jax.ShapeDtypeStruct((1, D), F32),
                 jax.ShapeDtypeStruct((lx, D), F32), jax.ShapeDtypeStruct((lx, 2 * D), BF16),
                 jax.ShapeDtypeStruct((lx, D), BF16), jax.ShapeDtypeStruct((lx, D), BF16),
                 jax.ShapeDtypeStruct((lx, D), BF16), jax.ShapeDtypeStruct((lx * NH, HD), F32),
                 jax.ShapeDtypeStruct((lx, D), F32))
    out_specs = (vec, vec, vec, row, pl.BlockSpec((T, 2 * D), lambda i: (i, 0)),
                 row, row, row, ZT, row)
    return pl.pallas_call(
        body, name="out_fwd_bwd", grid=(n,), out_shape=out_shape, in_specs=in_specs, out_specs=out_specs,
        scratch_shapes=[pltpu.VMEM((T, D), F32)],
        compiler_params=_cp(1, vmem_mb=56),
    )(hf_z, hb_z, ga, gb, ys, x, tgt, mods, final_g, w_out_full)


def _lru_bwd(xc_z, dy_z, hf_z, hb_z, af_z, ab_z, gf, gb, s_b, wcat, lamcat, name):
    lx = xc_z.shape[0] // NH
    n = lx // T
    tile_u = lambda i: jnp.where(i == n - 1, 0, i + 1)
    tile_d = lambda i: n - 1 - i

    def body(xc_u, dy_u, hb_ref, hbn_ref, ab_ref, gb_ref, xc_d, dy_d, hf_ref, hfp_ref, af_ref, gf_ref,
             sb_ref, w_ref, lam_ref, dxcb_ref, dxcf_ref, dw_ref, db_ref, dl_ref,
             lb_s, lf_s, q_u, q_d, pf_s, pb_s, dpre_s, carry):
        i = pl.program_id(0)
        tu, td = tile_u(i), tile_d(i)

        @pl.when(i == 0)
        def _():
            dw_ref[...] = jnp.zeros_like(dw_ref)
            db_ref[...] = jnp.zeros_like(db_ref)
            dl_ref[...] = jnp.zeros_like(dl_ref)
            carry[...] = jnp.zeros_like(carry)

        _scan_tile([(ab_ref, dy_u, lb_s, q_u, False, jnp.where(tu == 0, 0.0, 1.0)),
                    (af_ref, dy_d, lf_s, q_d, True, jnp.where(td == 0, 0.0, 1.0))], True, carry)
        zero = jnp.zeros((NH, HD), F32)
        pb_s[pl.ds(0, T * NH), :] = hb_ref[...]
        pb_s[pl.ds(T * NH, NH), :] = jnp.where(tu == n - 1, sb_ref[...], jnp.where(tu == 0, zero, hbn_ref[pl.ds(0, NH), :]))
        pf_s[pl.ds(0, NH), :] = jnp.where(td == 0, zero, hfp_ref[pl.ds(7 * NH, NH), :])
        pf_s[pl.ds(NH, T * NH), :] = hf_ref[...]
        sides = ((1, xc_u, lb_s, pb_s, NH, ab_ref, gb_ref, dxcb_ref), (0, xc_d, lf_s, pf_s, 0, af_ref, gf_ref, dxcf_ref))
        for d, xc_ref, adj_s, prev_s, prev_off, a_ref, g_ref, dxc_ref in sides:
            wcols = slice(256 * d, 256 * d + 256)
            for h in range(NH):
                xch = xc_ref[_zrows(h, T), :]
                xcb = xch.astype(BF16)
                r, gi, mult = (g_ref[:, q * D + HD * h:q * D + HD * h + HD] for q in range(3))
                a = a_ref[_zrows(h, T), :]
                lam = lam_ref[h:h + 1, HD * d:HD * d + HD]
                sp = _softplus(-lam)
                du = adj_s[_zrows(h, T), :]
                da = du * prev_s[pl.ds(prev_off + h, T, stride=NH), :]
                dgi = du * mult * xch
                dmult = du * gi * xch
                dla = da * a - dmult * (a * a) / mult
                dr = dla * ((-LRU_C) * sp)
                dsp = jnp.sum(dla * ((-LRU_C) * r), axis=0, keepdims=True)
                dl_ref[h:h + 1, HD * d:HD * d + HD] += dsp * (-_sigmoid(-lam))
                dpre_s[:, 0:HD] = dr * r * (1.0 - r)
                dpre_s[:, HD:2 * HD] = dgi * gi * (1.0 - gi)
                dpre = dpre_s[...]
                dpb = dpre.astype(BF16)
                dw_ref[h, :, wcols] += _dot_tn(xcb, dpb)
                db_ref[h:h + 1, wcols] += jnp.sum(dpre, axis=0, keepdims=True)
                dxc_ref[_zrows(h, T), :] = du * mult * gi + _dot_nt(dpb, w_ref[h, :, wcols])

    full = lambda shape: pl.BlockSpec(shape, lambda i: (0,) * len(shape))
    wsp, bsp, lsp = full((NH, HD, 4 * HD)), full((NH, 4 * HD)), full((NH, 2 * HD))
    st = full((NH, HD))
    up = pl.BlockSpec((T * NH, HD), lambda i: (tile_u(i), 0))
    dn = pl.BlockSpec((T * NH, HD), lambda i: (tile_d(i), 0))
    dy_up = pl.BlockSpec((T * NH, HD), lambda i: (jnp.maximum(tile_u(i) - 1, 0), 0))
    dy_dn = pl.BlockSpec((T * NH, HD), lambda i: (jnp.maximum(tile_d(i) - 1, 0), 0))
    nxt = _tile_specs(NH, HD, n, tile_u)[2]
    prv = _tile_specs(NH, HD, n, tile_d)[1]
    g_up = pl.BlockSpec((T, 3 * D), lambda i: (tile_u(i), 0))
    g_dn = pl.BlockSpec((T, 3 * D), lambda i: (tile_d(i), 0))
    zs = jax.ShapeDtypeStruct((lx * NH, HD), F32)
    zbuf = pltpu.VMEM((T * NH, HD), F32)
    zbuf1 = pltpu.VMEM(((T + 1) * NH, HD), F32)
    return pl.pallas_call(
        body, name=name, grid=(n,),
        out_shape=(zs, zs, jax.ShapeDtypeStruct((NH, HD, 4 * HD), F32), jax.ShapeDtypeStruct((NH, 4 * HD), F32),
                   jax.ShapeDtypeStruct((NH, 2 * HD), F32)),
        in_specs=[up, dy_up, up, nxt, up, g_up, dn, dy_dn, dn, prv, dn, g_dn, st, wsp, lsp],
        out_specs=(up, dn, wsp, bsp, lsp),
        scratch_shapes=[zbuf, zbuf, zbuf, zbuf, zbuf1, zbuf1, pltpu.VMEM((T, 2 * HD), F32),
                        pltpu.VMEM((2, NH, HD), F32)],
        compiler_params=_cp(1, vmem_mb=56),
    )(xc_z, dy_z, hb_z, hb_z, ab_z, gb, xc_z, dy_z, hf_z, hf_z, af_z, gf, s_b, wcat, lamcat)


def _conv_bwd(dxc_a, dxc_b, xa, conv_wz, dcw0, dcb0, name):
    lx = dxc_a.shape[0] // NH
    n = lx // T

    def body(dm_a, dp_a, dn_a, dm_b, dp_b, dn_b, xan_ref, cw, dcw0_ref, dcb0_ref, dxa_ref, dcw_ref, dcb_ref,
             pad, dxa_s, xa_ref):
        i = pl.program_id(0)

        @pl.when(i == 0)
        def _():
            dcw_ref[...] = dcw0_ref[...]
            dcb_ref[...] = dcb0_ref[...]

        for h in range(NH):
            xa_ref[_zrows(h, T), :] = xan_ref[:, HD * h:HD * h + HD]

        pmask = jnp.where(_has_prev(i), 1.0, 0.0)
        nmask = jnp.where(_has_next(i, n), 1.0, 0.0)
        pad[pl.ds(0, 8 * NH), :] = (dp_a[...] + dp_b[...]) * pmask
        pad[pl.ds(8 * NH, T * NH), :] = dm_a[...] + dm_b[...]
        pad[pl.ds((T + 8) * NH, 8 * NH), :] = (dn_a[...] + dn_b[...]) * nmask

        def chunk(ci, carry):
            base = pl.multiple_of(ci * (CONV_CHUNK * NH), CONV_CHUNK * NH)
            xav = xa_ref[pl.ds(base, CONV_CHUNK * NH), :].reshape(CONV_CHUNK, NH, HD)
            acc = None
            for k in range(4):
                sl = pad[pl.ds(base + (9 - k) * NH, CONV_CHUNK * NH), :].reshape(CONV_CHUNK, NH, HD)
                term = sl * cw[k][None]
                acc = term if acc is None else acc + term
                dcw_ref[k] += jnp.sum(sl * xav, axis=0)
                if k == 1:
                    dcb_ref[...] += jnp.sum(sl, axis=0)
            dxa_s[pl.ds(base, CONV_CHUNK * NH), :] = acc.reshape(CONV_CHUNK * NH, HD)
            return carry
        lax.fori_loop(0, T // CONV_CHUNK, chunk, 0)
        for h in range(NH):
            dxa_ref[:, HD * h:HD * h + HD] = dxa_s[_zrows(h, T), :].astype(BF16)

    full = lambda shape: pl.BlockSpec(shape, lambda i: (0,) * len(shape))
    return pl.pallas_call(
        body, name=name, grid=(n,),
        out_shape=(jax.ShapeDtypeStruct((lx, D), BF16), jax.ShapeDtypeStruct((4, NH, HD), F32),
                   jax.ShapeDtypeStruct((NH, HD), F32)),
        in_specs=_tile_specs(NH, HD, n, lambda i: i) * 2 + [pl.BlockSpec((T, D), lambda i: (i, 0)), full((4, NH, HD)),
                                                            full((4, NH, HD)), full((NH, HD))],
        out_specs=(pl.BlockSpec((T, D), lambda i: (i, 0)), full((4, NH, HD)), full((NH, HD))),
        scratch_shapes=[pltpu.VMEM(((T + 16) * NH, HD), F32), pltpu.VMEM((T * NH, HD), F32),
                        pltpu.VMEM((T * NH, HD), F32)],
        compiler_params=_cp(1, vmem_mb=48),
    )(dxc_a, dxc_a, dxc_a, dxc_b, dxc_b, dxc_b, xa, conv_wz, dcw0, dcb0)


def _proj_bwd(dzs, x, dxn, mods, mrow, norm_g, w_full, dng0, name, sgu=None, dz0_off=0):
    lx = x.shape[0]
    n = lx // T
    nz = len(dzs)
    has_x = dxn is not None
    wks = ([0, 1, 4, 2, 3] if sgu is not None else list(range(nz)))

    def body(*refs):
        it = iter(refs)
        take = lambda m: [next(it) for _ in range(m)]
        dz_refs, w_refs = take(nz), take(len(wks))
        x_ref, sc_ref, ng_ref, dng0_ref = take(4)
        dxn_ref = take(1)[0] if has_x else None
        if sgu is not None:
            u_ref, v_ref, dy_ref, g_ref, b_ref, sw_ref, bt_ref = take(7)
        gx_ref = take(1)[0] if has_x else None
        dng_ref, dsc_ref, dsh_ref = take(3)
        if sgu is not None:
            du_ref, dv_ref, dws_ref, dbs_ref, dlg_ref, dlb_ref, mixed_s, dvn_s = take(8)
        i = pl.program_id(0)

        @pl.when(i == 0)
        def _():
            dng_ref[...] = dng0_ref[...]
            dsc_ref[...] = jnp.zeros_like(dsc_ref)
            dsh_ref[...] = jnp.zeros_like(dsh_ref)
            if sgu is not None:
                for acc in (dws_ref, dbs_ref, dlg_ref, dlb_ref):
                    acc[...] = jnp.zeros_like(acc)

        def sgu_chunk(ch):
            rows = slice(HD * ch, HD * ch + HD)
            du, dv = _sgu_bwd_chunk(u_ref[rows, :], v_ref[rows, :], dy_ref[rows, :], g_ref[...], b_ref[...],
                                    sw_ref, bt_ref, mixed_s, dvn_s, dws_ref, dbs_ref, dlg_ref, dlb_ref)
            du_ref[rows, :] = du
            dv_ref[rows, :] = dv

        dhn = _dot_nt(dz_refs[0][...], w_refs[0][...])
        for k in range(1, nz):
            if sgu is not None and k - 1 < T // HD:
                sgu_chunk(k - 1)
            dhn = dhn + _dot_nt(dz_refs[k][...], w_refs[k][...])
        if sgu is not None:
            for ch in range(nz - 1, T // HD):
                sgu_chunk(ch)
            dhn = dhn + _dot_nt(du_ref[...], w_refs[nz][...]) + _dot_nt(dv_ref[...], w_refs[nz + 1][...])
        xv = x_ref[...]
        r = lax.rsqrt(jnp.mean(xv * xv, axis=-1, keepdims=True) + NORM_EPS)
        xn = xv * r
        ng = ng_ref[...]
        sc1 = 1.0 + sc_ref[mrow:mrow + 1, :]
        t = dhn * xn
        dng_ref[...] += jnp.sum(t * sc1, axis=0, keepdims=True)
        dsc_ref[...] += jnp.sum(t * ng, axis=0, keepdims=True)
        dsh_ref[...] += jnp.sum(dhn, axis=0, keepdims=True)
        if has_x:
            dxh = dhn * (ng * sc1)
            gx_ref[...] = dxn_ref[...] + r * (dxh - xn * jnp.mean(dxh * xn, axis=-1, keepdims=True))

    row = pl.BlockSpec((T, D), lambda i: (i, 0))
    vec = pl.BlockSpec((1, D), lambda i: (0, 0))
    in_specs = [pl.BlockSpec((T, D), lambda i: (i + dz0_off, 0))] + [row] * (nz - 1)
    in_specs += [pl.BlockSpec((D, D), lambda i, k=k: (0, k)) for k in wks]
    in_specs += [row, pl.BlockSpec((8, D), lambda i: (0, 1)), vec, vec]
    args = list(dzs) + [w_full] * len(wks) + [x, mods, norm_g, dng0]
    vs = jax.ShapeDtypeStruct((1, D), F32)
    out_shape, out_specs = (vs, vs, vs), (vec, vec, vec)
    scratch = []
    if has_x:
        in_specs.append(row)
        args.append(dxn)
        out_shape = (jax.ShapeDtypeStruct((lx, D), F32),) + out_shape
        out_specs = (row,) + out_specs
    if sgu is not None:
        wsp = pl.BlockSpec((NH, HD, HD), lambda i: (0, 0, 0))
        bsp = pl.BlockSpec((NH, HD), lambda i: (0, 0))
        in_specs += [row, row, row, vec, vec, wsp, pl.BlockSpec((HD, NH), lambda i: (0, 0))]
        args += list(sgu)
        zb = jax.ShapeDtypeStruct((lx, D), BF16)
        out_shape += (zb, zb, jax.ShapeDtypeStruct((NH, HD, HD), F32), jax.ShapeDtypeStruct((NH, HD), F32), vs, vs)
        out_specs += (row, row, wsp, bsp, vec, vec)
        scratch = [pltpu.VMEM((HD, D), F32), pltpu.VMEM((HD, D), F32)]
    return pl.pallas_call(
        body, name=name, grid=(n,), out_shape=out_shape, in_specs=in_specs, out_specs=out_specs,
        scratch_shapes=scratch, compiler_params=_cp(1, vmem_mb=56),
    )(*args)


def _adam_math(w, g, m, v):
    m = ADAM_B1 * m + (1.0 - ADAM_B1) * g
    v = ADAM_B2 * v + (1.0 - ADAM_B2) * (g * g)
    m_hat = m / (1.0 - ADAM_B1 ** ADAM_STEP)
    v_hat = v / (1.0 - ADAM_B2 ** ADAM_STEP)
    delta = -ADAM_LR * (m_hat / (jnp.sqrt(v_hat) + ADAM_EPS) + ADAM_WD * w)
    return delta, m, v


def _adam_big(w, g, m, v, name):
    rows, cols = w.shape
    tr = 256

    def body(w_ref, g_ref, m_ref, v_ref, d_o, m_o, v_o):
        d, mm, vv = _adam_math(w_ref[...], g_ref[...], m_ref[...], v_ref[...])
        d_o[...] = d
        m_o[...] = mm
        v_o[...] = vv

    blk = pl.BlockSpec((tr, cols), lambda i: (i, 0))
    s = jax.ShapeDtypeStruct((rows, cols), F32)
    return pl.pallas_call(
        body, name=name, grid=(rows // tr,), out_shape=(s, s, s), in_specs=[blk] * 4, out_specs=(blk,) * 3,
        compiler_params=_cp(1, vmem_mb=48),
    )(w, g, m, v)


def _adam_small(items, tot):
    ni = len(items)
    pieces = [it[1] if isinstance(it[1], list) else None for it in items]
    flat = [a for it, pc in zip(items, pieces) for a in ((it[0], it[2], it[3]) if pc is not None else it)]
    n_in = len(flat) + 1
    out_shape = tuple(jax.ShapeDtypeStruct(it[0].shape, F32) for it, pc in zip(items, pieces)
                      for _ in range(4 if pc is not None else 3))
    n_out = len(out_shape)
    n_loads = sum(3 + (len(pc) if pc is not None else 1) for pc in pieces)

    def body(*refs):
        ins, tot_ref, outs = refs[:n_in - 1], refs[n_in - 1], refs[n_in:n_in + n_out]
        bufs = refs[n_in + n_out:n_in + n_out + 7 * ni]
        sem_in, sem_out = refs[n_in + n_out + 7 * ni:]
        loads, q_in, q_sem = [], 0, 0
        for k, pc in enumerate(pieces):
            w_b, g_b, m_b, v_b = bufs[7 * k:7 * k + 4]
            srcs = [(ins[q_in], w_b)]
            if pc is None:
                srcs.append((ins[q_in + 1], g_b))
                q_in += 1
            else:
                srcs += [(tot_ref.at[pl.ds(r0, nr), pl.ds(c0, nc)], g_b.at[pl.ds(d0, nr), :]) for r0, nr, c0, nc, d0 in pc]
            srcs += [(ins[q_in + 1], m_b), (ins[q_in + 2], v_b)]
            q_in += 3
            mine = []
            for src, dst in srcs:
                mine.append(pltpu.make_async_copy(src, dst, sem_in.at[q_sem]))
                q_sem += 1
            loads.append(mine)
        for mine in loads:
            for cp in mine:
                cp.start()
        stores, q_out = [], 0
        for k, pc in enumerate(pieces):
            for cp in loads[k]:
                cp.wait()
            w_b, g_b, m_b, v_b = bufs[7 * k:7 * k + 4]
            res = _adam_math(w_b[...], g_b[...], m_b[...], v_b[...])
            srcs = []
            for q in range(3):
                bufs[7 * k + 4 + q][...] = res[q]
                srcs.append(bufs[7 * k + 4 + q])
            if pc is not None:
                srcs.append(g_b)
            for src in srcs:
                cp = pltpu.make_async_copy(src, outs[q_out], sem_out.at[q_out])
                cp.start()
                stores.append(cp)
                q_out += 1
        for cp in stores:
            cp.wait()

    scratch = [pltpu.VMEM(it[0].shape, F32) for it in items for _ in range(7)]
    scratch += [pltpu.SemaphoreType.DMA((n_loads,)), pltpu.SemaphoreType.DMA((n_out,))]
    res = pl.pallas_call(
        body, name="adam_small", out_shape=out_shape, in_specs=[HBM] * n_in, out_specs=(HBM,) * n_out,
        scratch_shapes=scratch, compiler_params=_cp(vmem_mb=40),
    )(*flat, tot)
    outs, q = [], 0
    for pc in pieces:
        outs.append(tuple(res[q:q + 3]) + ((res[q + 3],) if pc is not None else (None,)))
        q += 4 if pc is not None else 3
    return outs


def kernel(x, c, ctx, c_ctx, ada_w, ada_b, norm_g, w_in, conv_w, conv_b, lru_wa, lru_ba, lru_wx, lru_bx, lru_lambda, sgu_ln_g, sgu_ln_b, sgu_w, sgu_b, w_out, final_g, loss_target, m_c_ctx, m_ada_w, m_ada_b, m_norm_g, m_w_in, m_conv_w, m_conv_b, m_lru_wa, m_lru_ba, m_lru_wx, m_lru_bx, m_lru_lambda, m_sgu_ln_g, m_sgu_ln_b, m_sgu_w, m_sgu_b, m_w_out, m_final_g, v_c_ctx, v_ada_w, v_ada_b, v_norm_g, v_w_in, v_conv_w, v_conv_b, v_lru_wa, v_lru_ba, v_lru_wx, v_lru_bx, v_lru_lambda, v_sgu_ln_g, v_sgu_ln_b, v_sgu_w, v_sgu_b, v_w_out, v_final_g):
    ix, iy, ic = lax.axis_index("x"), lax.axis_index("y"), lax.axis_index("c")
    chip = 2 * ix + iy
    dev = 2 * chip + ic
    lx = x.shape[1]
    lc = ctx.shape[1]

    smalls = jnp.concatenate([conv_w[0], lru_lambda[0], jnp.zeros((10, 256), F32)], axis=0)
    c_ctx2 = c_ctx.reshape(1, D)
    ada_b_j = lax.dynamic_slice(ada_b, (0, 768 * chip), (1, 768))
    mods, c_slots, sm_all, w_in_full, wo_land, ada_land = _gather_in(c, c_ctx2, ada_w[0], ada_b_j, w_in[0], w_out[0],
                                                                     smalls)
    wo_ss, wo_rs, ada_ss, ada_rs, wo_land, ada_land, token = _late_gather_start(wo_land, ada_land)
    mods = mods + token[0:1, 0:1]
    sm3 = sm_all.reshape(NCHIP, 16, 256)
    conv_w_full = sm3[:, 0:4, :].transpose(1, 0, 2).reshape(4, D)
    lam_full = sm3[:, 4:6, :].transpose(1, 0, 2).reshape(2, D)
    conv_wz = conv_w_full.reshape(4, NH, HD)
    conv_bz = conv_b.reshape(NH, HD)
    lamcat = lam_full.reshape(2, NH, HD).transpose(1, 0, 2).reshape(NH, 2 * HD)
    wa, wx, ba, bx = lru_wa[0], lru_wx[0], lru_ba[0], lru_bx[0]
    wcat = jnp.concatenate([wa[0], wx[0], wa[1], wx[1]], axis=-1).astype(BF16)
    bcat = jnp.concatenate([ba[0], bx[0], ba[1], bx[1]], axis=-1)
    sgu_wb = sgu_w[0].astype(BF16)
    sgu_bt = sgu_b[0].T
    final_g2 = final_g.reshape(1, D)

    zero_s = jnp.zeros((NH, HD), F32)
    assert lc == T
    hn, xa_lat, ga, u, v, gb, ys = _proj(x[0], mods, 0, norm_g, w_in_full, 5, "proj",
                                         (sgu_ln_g, sgu_ln_b, sgu_wb, sgu_bt), z0_rows=lc + lx, z0_off=1)
    hn_c, xa_all = _proj(ctx[0], mods, 1, norm_g, w_in_full, 1, "proj_ctx", z0_into=xa_lat)
    xcz, af, ab, hf, hb, gf, gb_l, _, hb0 = _lru_fwd(xa_all, conv_wz, conv_bz, wcat, bcat, lamcat, "lru_fwd")

    w_out_full = _late_gather_wait(wo_land, wo_ss, wo_rs, "w_out", hf, "late_gather_wait_w_out")
    (loss_part, dfg, dgx, dxn, y, do, dga, dgb, dyl_z, dys) = _out_fwd_bwd(
        hf, hb, ga, gb, ys, x[0], loss_target[0], mods, final_g2, w_out_full)

    dxc_b, dxc_f, dwc, dbc, dlc = _lru_bwd(xcz, dyl_z, hf, hb, af, ab, gf, gb_l, hb0, wcat, lamcat, "lru_bwd")
    dxa, dcw, dcb = _conv_bwd(dxc_b, dxc_f, xa_all, conv_wz, jnp.zeros((4, NH, HD), F32), zero_s, "conv_bwd")

    grad_x, dng, dsc_x, dsh_x, du, dv, d_sgu_w, d_sgu_b, d_ln_g, d_ln_b = _proj_bwd(
        [dxa, dga, dgb], x[0], dxn, mods, 0, norm_g, w_in_full, jnp.zeros((1, D), F32), "proj_bwd",
        (u, v, dys, sgu_ln_g, sgu_ln_b, sgu_wb, sgu_bt), dz0_off=1)
    dzs = [dxa, dga, du, dv, dgb]
    dng, dsc_c, dsh_c = _proj_bwd([dxa], ctx[0], None, mods, 1, norm_g, w_in_full, dng, "proj_bwd_ctx")

    dmx = jnp.concatenate([dsh_x, dsc_x, dgx], axis=0)
    dmc = jnp.concatenate([dsh_c, dsc_c, jnp.zeros((1, D), F32)], axis=0)
    slot = jnp.concatenate([dmx, loss_part], axis=0)
    slots = lax.dynamic_update_slice(jnp.zeros((32, D), F32), slot, (4 * dev, 0))
    vecs = jnp.concatenate([dfg, dng, dcb.reshape(1, D), d_ln_g, d_ln_b, dcw.reshape(4, D), dmc,
                            jnp.zeros((4, D), F32), slots], axis=0)
    d_sgu_w4 = d_sgu_w.reshape(4, 256, HD).transpose(1, 0, 2).reshape(256, 4 * HD)
    pad8 = lambda a: jnp.pad(a, ((0, 8 - a.shape[0]), (0, 4 * HD - a.shape[1])))
    pack = jnp.concatenate([dwc.reshape(NH * HD, 4 * HD), pad8(dbc), pad8(dlc), d_sgu_w4, pad8(d_sgu_b),
                            vecs.reshape(96, 4 * HD), jnp.zeros((8, 4 * HD), F32)], axis=0)
    g_w_in, g_w_out, tot = _grads_reduce(hn, dzs, hn_c, y, do, pack)

    n_w = NH * HD
    g_lru_wa = [(0, n_w, 2 * HD * d, HD, n_w * d) for d in range(2)]
    g_lru_wx = [(0, n_w, 2 * HD * d + HD, HD, n_w * d) for d in range(2)]
    g_lru_ba = [(n_w, NH, 2 * HD * d, HD, NH * d) for d in range(2)]
    g_lru_bx = [(n_w, NH, 2 * HD * d + HD, HD, NH * d) for d in range(2)]
    g_sgu_w = [(1040, 256, HD * q, HD, 256 * q) for q in range(4)]
    g_sgu_b = [(1296, NH, 0, HD, 0)]
    g_lc = tot[1032:1040, 0:2 * HD]
    tv = tot[1304:1400].reshape(48, D)
    g_final_g, g_norm_g, g_conv_b, g_ln_g, g_ln_b = tv[0:1], tv[1:2], tv[2:3], tv[3:4], tv[4:5]
    g_conv_w_full = tv[5:9]
    dmc_tot = tv[9:12].reshape(1, 3 * D)
    slots_all = tv[16:48].reshape(8, 4, D)
    dmx_all = slots_all[:, 0:3, :].reshape(8, 3 * D)
    c_all = c_slots.reshape(8, 8, D)[:, 0, :]
    g_lam_full = jnp.stack([g_lc[:, 0:HD], g_lc[:, HD:2 * HD]]).reshape(2, D)
    g_conv_w = lax.dynamic_slice(g_conv_w_full, (0, 256 * chip), (4, 256))
    g_lam = lax.dynamic_slice(g_lam_full, (0, 256 * chip), (2, 256))
    dmx_all_j = lax.dynamic_slice(dmx_all, (0, 768 * chip), (8, 768))
    dmc_j = lax.dynamic_slice(dmc_tot, (0, 768 * chip), (1, 768))
    ada_full = _late_gather_wait(ada_land, ada_ss, ada_rs, "ada_w", tot, "late_gather_wait_ada_w")
    g_ada_w, g_ada_b, g_c_ctx = _ada_bwd(c_all, dmx_all_j, dmc_j, dmx_all, dmc_tot, c_ctx2, ada_full)

    big = {
        "ada_w": _adam_big(ada_w[0], g_ada_w, m_ada_w[0], v_ada_w[0], "adam_ada_w"),
        "w_in": _adam_big(w_in[0], g_w_in, m_w_in[0], v_w_in[0], "adam_w_in"),
        "w_out": _adam_big(w_out[0], g_w_out, m_w_out[0], v_w_out[0], "adam_w_out"),
    }
    small_in = {
        "c_ctx": (c_ctx, g_c_ctx, m_c_ctx, v_c_ctx, (1, D)),
        "ada_b": (ada_b, g_ada_b, m_ada_b, v_ada_b, (1, 3 * D)),
        "norm_g": (norm_g, g_norm_g, m_norm_g, v_norm_g, (1, D)),
        "conv_w": (conv_w, g_conv_w, m_conv_w, v_conv_w, (4, 256)),
        "conv_b": (conv_b, g_conv_b, m_conv_b, v_conv_b, (1, D)),
        "lru_wa": (lru_wa, g_lru_wa, m_lru_wa, v_lru_wa, (2 * NH * HD, HD)),
        "lru_ba": (lru_ba, g_lru_ba, m_lru_ba, v_lru_ba, (2 * NH, HD)),
        "lru_wx": (lru_wx, g_lru_wx, m_lru_wx, v_lru_wx, (2 * NH * HD, HD)),
        "lru_bx": (lru_bx, g_lru_bx, m_lru_bx, v_lru_bx, (2 * NH, HD)),
        "lru_lambda": (lru_lambda, g_lam, m_lru_lambda, v_lru_lambda, (2, 256)),
        "sgu_ln_g": (sgu_ln_g, g_ln_g, m_sgu_ln_g, v_sgu_ln_g, (1, D)),
        "sgu_ln_b": (sgu_ln_b, g_ln_b, m_sgu_ln_b, v_sgu_ln_b, (1, D)),
        "sgu_w": (sgu_w, g_sgu_w, m_sgu_w, v_sgu_w, (NH * HD, HD)),
        "sgu_b": (sgu_b, g_sgu_b, m_sgu_b, v_sgu_b, (NH, HD)),
        "final_g": (final_g, g_final_g, m_final_g, v_final_g, (1, D)),
    }
    names_small = list(small_in)
    res_small = _adam_small([tuple(a if isinstance(a, list) else a.reshape(small_in[k][4]) for a in small_in[k][:4])
                             for k in names_small], tot)
    full_shapes = {"ada_w": ada_w.shape, "w_in": w_in.shape, "w_out": w_out.shape}
    grads, deltas, new_m, new_v = {}, {}, {}, {}
    for k in ("ada_w", "w_in", "w_out"):
        g = {"ada_w": g_ada_w, "w_in": g_w_in, "w_out": g_w_out}[k]
        grads[k] = g.reshape(full_shapes[k])
        deltas[k], new_m[k], new_v[k] = (a.reshape(full_shapes[k]) for a in big[k])
    for k, res in zip(names_small, res_small):
        shape = small_in[k][0].shape
        grads[k] = (small_in[k][1] if res[3] is None else res[3]).reshape(shape)
        deltas[k], new_m[k], new_v[k] = (a.reshape(shape) for a in res[:3])

    loss = jnp.sum(slots_all[:, 3, 0:3])
    order = ["c_ctx", "ada_w", "ada_b", "norm_g", "w_in", "conv_w", "conv_b", "lru_wa", "lru_ba", "lru_wx", "lru_bx",
             "lru_lambda", "sgu_ln_g", "sgu_ln_b", "sgu_w", "sgu_b", "w_out", "final_g"]
    return (loss, grad_x.reshape(x.shape), *[grads[k] for k in order], *[deltas[k] for k in order],
            *[new_m[k] for k in order], *[new_v[k] for k in order])
```

```python
import jax
import jax.numpy as jnp
from jax import lax
from jax.experimental import pallas as pl
from jax.experimental.pallas import tpu as pltpu

F32 = jnp.float32
BF16 = jnp.bfloat16

D = 1024
NH = 8
HD = 128
NCHIP = 4
T = 256
NORM_EPS = 1e-6
LN_EPS = 1e-5
LRU_C = 8.0
ADAM_LR = 0.001
ADAM_B1 = 0.9
ADAM_B2 = 0.999
ADAM_EPS = 1e-08
ADAM_WD = 0.01
ADAM_STEP = 10

VMEM = pl.BlockSpec(memory_space=pltpu.VMEM)
ANY = pl.BlockSpec(memory_space=pl.ANY)
MESH = pl.DeviceIdType.MESH


def _cp(n_grid=0, vmem_mb=None):
    kw = {}
    if n_grid:
        kw["dimension_semantics"] = ("arbitrary",) * n_grid
    if vmem_mb:
        kw["vmem_limit_bytes"] = vmem_mb << 20
    return pltpu.CompilerParams(**kw)


def _sigmoid(x):
    return 0.5 * jnp.tanh(0.5 * x) + 0.5


def _silu_and_grad(x):
    s = _sigmoid(x)
    return x * s, s * (1.0 + x * (1.0 - s))


_GELU_K = 0.7978845608028654
_GELU_C = 0.044715


def _gelu_and_grad(x):
    x2 = x * x
    th = jnp.tanh(x * (_GELU_K + (_GELU_K * _GELU_C) * x2))
    p = 0.5 + 0.5 * th
    g = x * p
    dg = p + g * (1.0 - th) * (_GELU_K + (3.0 * _GELU_K * _GELU_C) * x2)
    return g, dg


def _softplus(x):
    return jnp.maximum(x, 0.0) + jnp.log1p(jnp.exp(-jnp.abs(x)))


def _lru_gate(pre, lam_row, d, off=None):
    off = 256 * d if off is None else off
    r = _sigmoid(pre[:, off:off + HD])
    gi = _sigmoid(pre[:, off + HD:off + 2 * HD])
    lam = lam_row[:, HD * d:HD * d + HD]
    sp = _softplus(-lam)
    la = (-LRU_C) * r * sp
    a = jnp.exp(la)
    x2 = 2.0 * la
    m2 = jnp.where(x2 > -1e-3, -x2 * (1.0 + 0.5 * x2), 1.0 - a * a)
    mult = jnp.sqrt(m2)
    return r, gi, lam, sp, a, mult


def _dot(a, b):
    return jnp.dot(a, b, preferred_element_type=F32)


def _dot_tn(a, b):
    return lax.dot_general(a, b, (((0,), (0,)), ((), ())), preferred_element_type=F32)


def _dot_nt(a, b):
    return lax.dot_general(a, b, (((1,), (1,)), ((), ())), preferred_element_type=F32)


def _mo(v, m):
    return v if isinstance(v, int) else pl.multiple_of(v, m)


def _zrows(h, n):
    return pl.ds(h, n, stride=NH)


def _gather_in(c, c_ctx, ada_w, ada_b_j, w_in, w_out, smalls):
    nch = [1, 4]
    wrows = lambda cc, q: (pl.ds(_mo(512 * cc, 16), 512) if q is None
                           else pl.ds(_mo(512 * cc + (512 // nch[1]) * q, 16), 512 // nch[1]))
    specs = [
        ((64, 256), F32, lambda r, jj, cc, q=None: r.at[pl.ds(_mo(16 * jj + 8 * cc, 8), 8), :]),
        ((D, 5120), BF16, lambda r, jj, cc, q=None: r.at[wrows(cc, q), pl.ds(_mo(1280 * jj, 128), 1280)]),
    ]
    halves = [lambda r, cc, q=None: r.at[pl.ds(_mo(8 * cc, 8), 8), :],
              lambda r, cc, q=None: r.at[wrows(cc, q), :]]
    na = len(specs)
    sem_base = [0, 6 * nch[0]]
    sidx = lambda a, q, k: sem_base[a] + 6 * q + k
    n_tiny = 6 * sum(nch)
    n_sem = n_tiny + 10

    def body(c_ref, cc_ref, ada_ref, adab_ref, win_ref, wout_ref, sm_ref,
             mods_o, call_o, sm_o, win_o, wol_o, adal_o, s_win, s_ada, s_wout, f_win, f_ada, f_wout, cslot, lhs, mbuf,
             send_sems, recv_sems, local_sems, load_sems):
        x, y, c = lax.axis_index("x"), lax.axis_index("y"), lax.axis_index("c")
        j = 2 * x + y
        dev = 2 * j + c
        sib = (x, y, 1 - c)
        chips = [(1 - x, y), (x, 1 - y), (1 - x, 1 - y)]
        cj = [2 * cx + cy for cx, cy in chips]
        outs = [sm_o, win_o]
        srcs = [sm_ref, s_win]

        def copy(idx, src, dst, to):
            return pltpu.make_async_remote_copy(src_ref=src, dst_ref=dst, send_sem=send_sems.at[idx],
                                                recv_sem=recv_sems.at[idx], device_id=to, device_id_type=MESH)

        sends = []

        def start(cp):
            cp.start()
            sends.append(cp)

        cslot[...] = jnp.zeros_like(cslot)
        cslot[0:1, :] = c_ref[...]
        my_slot = pl.ds(_mo(8 * dev, 8), 8)
        others = [sib] + [(*chips[k], c) for k in range(3)] + [(*chips[k], 1 - c) for k in range(3)]
        other_dev = [dev + 1 - 2 * c] + [2 * cj[k] + c for k in range(3)] + [2 * cj[k] + 1 - c for k in range(3)]
        base = n_tiny
        for r in range(7):
            start(copy(base + r, cslot, call_o.at[my_slot, :], others[r]))
        call_o[my_slot, :] = cslot[...]

        crow = 512 // nch[1]
        loads = []
        for cc in (c, 1 - c):
            for q in range(nch[1]):
                rows = pl.ds(_mo(512 * cc + crow * q, 16), crow)
                loads.append(pltpu.make_async_copy(win_ref.at[rows, :], f_win.at[rows, :], load_sems.at[len(loads)]))
        loads.append(pltpu.make_async_copy(ada_ref, f_ada, load_sems.at[len(loads)]))
        loads.append(pltpu.make_async_copy(wout_ref, f_wout, load_sems.at[len(loads)]))
        for ld in loads:
            ld.start()
        for k in range(2):
            start(copy(sidx(0, 0, k), halves[0](srcs[0], c), specs[0][2](outs[0], j, c), (*chips[k], c)))
        for q in range(nch[1]):
            loads[q].wait()
            rows = pl.ds(_mo(512 * c + crow * q, 16), crow)
            s_win[rows, :] = f_win[rows, :].astype(BF16)
            for k in range(2):
                start(copy(sidx(1, q, k), halves[1](s_win, c, q), specs[1][2](win_o, j, c, q), (*chips[k], c)))
        for q in range(nch[1]):
            loads[nch[1] + q].wait()
            rows = pl.ds(_mo(512 * (1 - c) + crow * q, 16), crow)
            s_win[rows, :] = f_win[rows, :].astype(BF16)
        local = []
        for a in range(na):
            for cc in range(2):
                lc = pltpu.make_async_copy(halves[a](srcs[a], cc), specs[a][2](outs[a], j, cc), local_sems.at[2 * a + cc])
                lc.start()
                local.append(lc)
        loads[2 * nch[1]].wait()
        s_ada[...] = f_ada[...].astype(BF16)
        loads[2 * nch[1] + 1].wait()
        s_wout[...] = f_wout[...].astype(BF16)
        for q, (src, dst) in enumerate([(s_wout, wol_o.at[pl.ds(_mo(512 * j, 16), 512), :]),
                                        (s_ada, adal_o.at[:, pl.ds(_mo(768 * j, 128), 768)])]):
            lc = pltpu.make_async_copy(src, dst, local_sems.at[2 * na + q])
            lc.start()
            local.append(lc)

        for r in range(7):
            slot = call_o.at[pl.ds(_mo(8 * other_dev[r], 8), 8), :]
            copy(base + r, slot, slot, sib).wait_recv()
        lhs[...] = jnp.zeros_like(lhs)
        for b in range(8):
            cv = call_o[8 * b:8 * b + 1, :]
            lhs[b:b + 1, :] = cv * _sigmoid(cv)
        cv = cc_ref[...]
        lhs[8:9, :] = cv * _sigmoid(cv)
        mbuf[j] = _dot(lhs[...].astype(BF16), s_ada[...]) + adab_ref[...]
        for k in range(3):
            start(copy(base + 7 + k, mbuf.at[j], mbuf.at[j], (*chips[k], c)))
        for k in range(3):
            copy(base + 7 + k, mbuf.at[cj[k]], mbuf.at[cj[k]], sib).wait_recv()
        mods_o[...] = jnp.zeros_like(mods_o)
        for jj in range(NCHIP):
            mods_o[0:1, 768 * jj:768 * jj + 768] = mbuf[jj, pl.ds(dev, 1), :]
            mods_o[1:2, 768 * jj:768 * jj + 768] = mbuf[jj, 8:9, :]

        kx = [1 - x, x, 1 - x]
        ky = [y, 1 - y, 1 - y]
        pick = lambda k, lst: jnp.where(k == 0, lst[0], jnp.where(k == 1, lst[1], lst[2]))
        for a in range(na):
            for q in range(nch[a]):
                for step, k in enumerate([c, 1 - c]):
                    reg = specs[a][2](outs[a], pick(k, cj), c, q)
                    copy(sidx(a, q, k), reg, reg, sib).wait_recv()
                    if step == 0:
                        start(copy(sidx(a, q, 2), reg, reg, (pick(1 - c, kx), pick(1 - c, ky), c)))
                    start(copy(sidx(a, q, 3 + k), reg, reg, sib))
        for a in range(na):
            for q in range(nch[a]):
                reg = specs[a][2](outs[a], cj[2], c, q)
                copy(sidx(a, q, 2), reg, reg, sib).wait_recv()
                start(copy(sidx(a, q, 5), reg, reg, sib))
        for a in range(na):
            for q in range(nch[a]):
                for k in range(3):
                    reg = specs[a][2](outs[a], cj[k], 1 - c, q)
                    copy(sidx(a, q, 3 + k), reg, reg, sib).wait_recv()
        for cp in sends:
            cp.wait_send()
        for lc in local:
            lc.wait()

    out_shape = (jax.ShapeDtypeStruct((8, 3 * D), F32), jax.ShapeDtypeStruct((64, D), F32),
                 jax.ShapeDtypeStruct(specs[0][0], F32), jax.ShapeDtypeStruct(specs[1][0], BF16),
                 jax.ShapeDtypeStruct((2048, D), BF16), jax.ShapeDtypeStruct((D, 3 * D), BF16))
    return pl.pallas_call(
        body, name="gather_in", out_shape=out_shape,
        in_specs=[VMEM, VMEM, ANY, VMEM, ANY, ANY, VMEM], out_specs=(VMEM, VMEM, VMEM, ANY, ANY, ANY),
        scratch_shapes=[pltpu.VMEM((D, 1280), BF16), pltpu.VMEM((D, 768), BF16), pltpu.VMEM((512, D), BF16),
                        pltpu.VMEM((D, 1280), F32), pltpu.VMEM((D, 768), F32), pltpu.VMEM((512, D), F32),
                        pltpu.VMEM((8, D), F32), pltpu.VMEM((16, D), F32), pltpu.VMEM((NCHIP, 16, 768), F32),
                        pltpu.SemaphoreType.DMA((n_sem,)), pltpu.SemaphoreType.DMA((n_sem,)),
                        pltpu.SemaphoreType.DMA((2 * na + 2,)), pltpu.SemaphoreType.DMA((2 * nch[1] + 2,))],
        compiler_params=_cp(vmem_mb=56),
    )(c, c_ctx, ada_w, ada_b_j, w_in, w_out, smalls)


HBM = pl.BlockSpec(memory_space=pltpu.HBM)
SEM = pl.BlockSpec(memory_space=pltpu.SEMAPHORE)


def _late_gather_regions(x, y, c):
    chips = [(1 - x, y), (x, 1 - y), (1 - x, 1 - y)]
    wo_reg = lambda r, jj, cc: r.at[pl.ds(_mo(512 * jj + 256 * cc, 16), 256), :]
    ada_reg = lambda r, jj, cc: r.at[pl.ds(_mo(512 * cc, 16), 512), pl.ds(_mo(768 * jj, 128), 768)]
    return chips, wo_reg, ada_reg


def _late_gather_start(wo_land, ada_land):
    def body(wol_ref, adal_ref, wo_ss, wo_rs, ada_ss, ada_rs, wol_thru, adal_thru, token):
        x, y, c = lax.axis_index("x"), lax.axis_index("y"), lax.axis_index("c")
        j = 2 * x + y
        chips, wo_reg, ada_reg = _late_gather_regions(x, y, c)
        for k in range(3):
            for cc in range(2):
                pltpu.make_async_remote_copy(src_ref=wo_reg(wol_ref, j, c), dst_ref=wo_reg(wol_ref, j, c),
                                             send_sem=wo_ss.at[2 * k + cc], recv_sem=wo_rs.at[2 * k + c],
                                             device_id=(*chips[k], cc), device_id_type=MESH).start()
        for k in range(3):
            for cc in range(2):
                pltpu.make_async_remote_copy(src_ref=ada_reg(adal_ref, j, c), dst_ref=ada_reg(adal_ref, j, c),
                                             send_sem=ada_ss.at[2 * k + cc], recv_sem=ada_rs.at[2 * k + c],
                                             device_id=(*chips[k], cc), device_id_type=MESH).start()
        token[...] = jnp.zeros_like(token)

    sems = pltpu.SemaphoreType.DMA((6,))
    return pl.pallas_call(
        body, name="late_gather_start",
        out_shape=(sems, sems, sems, sems, pltpu.HBM(wo_land.shape, BF16), pltpu.HBM(ada_land.shape, BF16),
                   jax.ShapeDtypeStruct((8, 128), F32)),
        in_specs=(HBM, HBM), out_specs=(SEM, SEM, SEM, SEM, HBM, HBM, VMEM), input_output_aliases={0: 4, 1: 5},
        compiler_params=pltpu.CompilerParams(has_side_effects=pltpu.SideEffectType.DATAFLOW_SIDE_EFFECTING),
    )(pltpu.with_memory_space_constraint(wo_land, pltpu.HBM), pltpu.with_memory_space_constraint(ada_land, pltpu.HBM))


def _late_gather_wait(land, send_sems, recv_sems, which, after, name):
    def body(land_ref, ss, rs, after_ref, land_out):
        x, y, c = lax.axis_index("x"), lax.axis_index("y"), lax.axis_index("c")
        j = 2 * x + y
        chips, wo_reg, ada_reg = _late_gather_regions(x, y, c)
        reg = wo_reg if which == "w_out" else ada_reg
        for k in range(3):
            kj = 2 * chips[k][0] + chips[k][1]
            for cc in range(2):
                cp = pltpu.make_async_remote_copy(src_ref=reg(land_ref, j, c), dst_ref=reg(land_ref, kj, cc),
                                                  send_sem=ss.at[2 * k + cc], recv_sem=rs.at[2 * k + cc],
                                                  device_id=(*chips[k], cc), device_id_type=MESH)
                cp.wait_send()
                cp.wait_recv()

    return pl.pallas_call(
        body, name=name, out_shape=pltpu.HBM(land.shape, land.dtype),
        in_specs=(HBM, SEM, SEM, ANY), out_specs=HBM, input_output_aliases={0: 0},
        compiler_params=pltpu.CompilerParams(has_side_effects=pltpu.SideEffectType.DATAFLOW_SIDE_EFFECTING),
    )(land, send_sems, recv_sems, after)


RCHUNK = 16


def _grads_reduce(hn, dzs, lc, y, do, pack):
    rp = pack.shape[0]
    hp = rp // 2
    assert hp % RCHUNK == 0
    wi_w = 1280
    lx = hn.shape[0] - lc
    lt = lx + lc
    n_dz = len(dzs)

    def body(*refs):
        hn_hbm, dz_hbm = refs[0], refs[1:1 + n_dz]
        y_hbm, do_hbm, pk_hbm, wi_out, wo_out, pk_out = refs[1 + n_dz:7 + n_dz]
        (hn_mine, hn_other, dzbuf, wi_other, wi_mine, wi_recv, wi_send, wi_rb,
         y_blk, do_mine, do_other, wo_other, wo_mine, wo_recv, wo_send, wo_rb,
         pk_mine, pk_recv, pk_send, pk_rb, pk_own, send_sems, recv_sems, local_sems) = refs[7 + n_dz:]
        x, y, c = lax.axis_index("x"), lax.axis_index("y"), lax.axis_index("c")
        j = 2 * x + y
        sib = (x, y, 1 - c)
        chips = [(1 - x, y), (x, 1 - y), (1 - x, 1 - y)]
        cj = [2 * cx + cy for cx, cy in chips]
        near = (jnp.where(c == 0, 1 - x, x), jnp.where(c == 0, y, 1 - y), c)
        slabs = [cj[2], cj[0], cj[1], j]

        def copy(k, src, dst, to):
            return pltpu.make_async_remote_copy(src_ref=src, dst_ref=dst, send_sem=send_sems.at[k],
                                                recv_sem=recv_sems.at[k], device_id=to, device_id_type=MESH)

        def local(k, src, dst):
            cp = pltpu.make_async_copy(src, dst, local_sems.at[k])
            cp.start()
            return cp

        rows_half = lambda r, cc, n: r.at[pl.ds(_mo(cc * n, 16), n), :]
        cols_half = lambda r, cc, n: r.at[:, pl.ds(_mo(cc * n, 128), n)]
        pk_piece = lambda r, cc, jj: r.at[pl.ds(_mo(cc * hp, 16), hp), pl.ds(_mo(jj * 128, 128), 128)]

        sends = []

        def start(cp):
            cp.start()
            sends.append(cp)

        def dz_pieces(s):
            g0 = wi_w * s
            k0, off0 = g0 // D, g0 % D
            w0 = min(D - off0, wi_w)
            pieces = [(k0, off0, w0, 0)]
            if w0 < wi_w:
                pieces.append((k0 + 1, 0, wi_w - w0, w0))
            return pieces

        def dz_copies(s):
            cps = []
            for q, (k, off, w, dst) in enumerate(dz_pieces(s)):
                cps.append(pltpu.make_async_copy(dz_hbm[k].at[pl.ds(lc if k == 0 else 0, lx), pl.ds(off, w)],
                                                 dzbuf.at[pl.ds(lc, lx), pl.ds(dst, w)], local_sems.at[11 + q]))
            if s == 0:
                cps.append(pltpu.make_async_copy(dz_hbm[0].at[pl.ds(0, lc), :], dzbuf.at[pl.ds(0, lc), pl.ds(0, D)],
                                                 local_sems.at[13]))
            return cps

        def dz_load(sl):
            for s in range(NCHIP):
                @pl.when(sl == s)
                def _():
                    if s == 0:
                        dzbuf[pl.ds(0, lc), pl.ds(D, wi_w - D)] = jnp.zeros((lc, wi_w - D), BF16)
                    else:
                        dzbuf[pl.ds(0, lc), :] = jnp.zeros((lc, wi_w), BF16)
                    for cp in dz_copies(s):
                        cp.start()

        def dz_wait(sl):
            for s in range(NCHIP):
                @pl.when(sl == s)
                def _():
                    for cp in dz_copies(s):
                        cp.wait()

        l_pk = local(0, rows_half(pk_hbm, c, hp), pk_mine)
        start(copy(0, rows_half(pk_hbm, 1 - c, hp), pk_recv, sib))
        col = lambda r, cc: r.at[:, pl.ds(_mo(cc * 512, 128), 512)]
        do_loads = [local(7, col(do_hbm, c), do_mine), local(14, col(do_hbm, 1 - c), do_other)]
        hn_loads = [local(2, col(hn_hbm, c), hn_mine), local(4, col(hn_hbm, 1 - c), hn_other)]
        y_copy = lambda s: pltpu.make_async_copy(col(y_hbm, slabs[s]), y_blk, local_sems.at[1])
        y_copy(0).start()
        dz_load(slabs[0])

        def pair_sum(mine, recv, send, nrows, keep, relayed=None):
            def step(i, carry):
                rows = pl.ds(_mo(i * RCHUNK, RCHUNK), RCHUNK)
                s = mine[rows, :] + recv[rows, :].astype(F32)
                if relayed is not None:
                    s = s + relayed[rows, :].astype(F32)
                if keep:
                    mine[rows, :] = s
                if send is not None:
                    send[rows, :] = s.astype(BF16)
                return carry
            lax.fori_loop(0, nrows // RCHUNK, step, 0)

        def chip_sum(own, rb, nrows, terms=(0, 1, 2)):
            def step(i, carry):
                rows = pl.ds(_mo(i * RCHUNK, RCHUNK), RCHUNK)
                acc = own[rows, :]
                for q in terms:
                    acc = acc + rb[q, rows, :].astype(F32)
                own[rows, :] = acc
                return carry
            lax.fori_loop(0, nrows // RCHUNK, step, 0)

        w_in_g = dict(other=wi_other, mine=wi_mine, recv=wi_recv, send=wi_send, rb=wi_rb, p1_sems=(2, 3, 4, 5), p2_sem=12,
                      p1=[None] * NCHIP, wait_load=lambda s: dz_wait(slabs[s]), load=lambda s: dz_load(slabs[s]),
                      dot_other=lambda: _dot_tn(hn_other[...], dzbuf[...]), dot_mine=lambda: _dot_tn(hn_mine[...], dzbuf[...]))
        w_out_g = dict(other=wo_other, mine=wo_mine, recv=wo_recv, send=wo_send, rb=wo_rb, p1_sems=(1, 24, 25, 26), p2_sem=9,
                       p1=[None] * NCHIP, wait_load=lambda s: y_copy(s).wait(), load=lambda s: y_copy(s).start(),
                       dot_other=lambda: _dot_tn(y_blk[...], do_other[...]), dot_mine=lambda: _dot_tn(y_blk[...], do_mine[...]))

        def piece_matmuls(g, s):
            if s >= 2:
                g["p1"][s - 2].wait_send()
            g["wait_load"](s)
            g["other"][s % 2] = g["dot_other"]().astype(BF16)
            g["p1"][s] = copy(g["p1_sems"][s], g["other"].at[s % 2], g["recv"].at[s], sib)
            g["p1"][s].start()
            g["mine"][s % 2] = g["dot_mine"]()
            if s + 1 < NCHIP:
                g["load"](s + 1)

        def piece_finish(g, s):
            mine, recv, send, rb, p2 = g["mine"].at[s % 2], g["recv"].at[s], g["send"], g["rb"], g["p2_sem"]
            nrows = mine.shape[0]
            copy(g["p1_sems"][s], recv, recv, sib).wait_recv()
            if s == 3:
                pair_sum(mine, recv, None, nrows, True)
                return
            if s == 0:
                pair_sum(mine, recv, send.at[0], nrows, False)
                start(copy(p2, send.at[0], rb.at[0], near))
                return
            adds_relayed = c == (1 if s == 1 else 0)

            @pl.when(adds_relayed)
            def _():
                copy(p2, rb.at[0], rb.at[0], sib).wait_recv()
                pair_sum(mine, recv, send.at[s], nrows, False, rb.at[0])

            @pl.when(jnp.logical_not(adds_relayed))
            def _():
                pair_sum(mine, recv, send.at[s], nrows, False)
            start(copy(p2 + s, send.at[s], rb.at[s], (*chips[s - 1], c)))

        def piece_total(g):
            for k in (1, 2):
                copy(g["p2_sem"] + k, g["rb"].at[k], g["rb"].at[k], sib).wait_recv()
            chip_sum(g["mine"].at[1], g["rb"], g["mine"].shape[1], (1, 2))

        for cp in do_loads:
            cp.wait()
        piece_matmuls(w_out_g, 0)
        piece_matmuls(w_out_g, 1)
        piece_finish(w_out_g, 0)
        piece_matmuls(w_out_g, 2)
        piece_finish(w_out_g, 1)
        piece_matmuls(w_out_g, 3)
        piece_finish(w_out_g, 2)

        for cp in hn_loads:
            cp.wait()
        piece_matmuls(w_in_g, 0)

        l_pk.wait()
        copy(0, pk_recv, pk_recv, sib).wait_recv()
        pair_sum(pk_mine, pk_recv, pk_send, hp, True)
        for k in range(3):
            start(copy(6 + k, pk_send.at[:, pl.ds(_mo(cj[k] * 128, 128), 128)], pk_rb.at[k], (*chips[k], c)))
        l_pk_own = local(6, pk_mine.at[:, pl.ds(_mo(j * 128, 128), 128)], pk_own)

        piece_matmuls(w_in_g, 1)
        piece_finish(w_in_g, 0)

        l_pk_own.wait()
        for k in range(3):
            copy(6 + k, pk_rb.at[k], pk_rb.at[k], sib).wait_recv()
        chip_sum(pk_own, pk_rb, hp)
        l_pk_out = local(8, pk_own, pk_piece(pk_out, c, j))
        start(copy(15, pk_own, pk_piece(pk_out, c, j), sib))
        for k in range(2):
            start(copy(16 + k, pk_own, pk_piece(pk_out, c, j), (*chips[k], c)))

        piece_matmuls(w_in_g, 2)
        piece_finish(w_in_g, 1)
        piece_matmuls(w_in_g, 3)
        piece_finish(w_in_g, 2)

        piece_finish(w_out_g, 3)
        piece_total(w_out_g)
        l_wo_out = local(9, wo_mine.at[1], cols_half(wo_out, c, 512))
        start(copy(22, wo_mine.at[1], cols_half(wo_out, c, 512), sib))

        far = (jnp.where(c == 0, x, 1 - x), jnp.where(c == 0, 1 - y, y), c)
        for step, k in enumerate([c, 1 - c, 2]):
            reg = pk_piece(pk_out, c, jnp.where(k == 0, cj[0], jnp.where(k == 1, cj[1], cj[2])))
            copy(16 + k, reg, reg, sib).wait_recv()
            if step == 0:
                start(copy(18, reg, reg, far))
            start(copy(19 + k, reg, reg, sib))

        piece_finish(w_in_g, 3)
        piece_total(w_in_g)
        l_wi_out = local(10, wi_mine.at[1], rows_half(wi_out, c, 512))
        start(copy(23, wi_mine.at[1], rows_half(wi_out, c, 512), sib))

        reg = pk_piece(pk_out, 1 - c, j)
        copy(15, reg, reg, sib).wait_recv()
        for k in range(3):
            reg = pk_piece(pk_out, 1 - c, cj[k])
            copy(19 + k, reg, reg, sib).wait_recv()
        reg = cols_half(wo_out, 1 - c, 512)
        copy(22, reg, reg, sib).wait_recv()
        reg = rows_half(wi_out, 1 - c, 512)
        copy(23, reg, reg, sib).wait_recv()
        for cp in sends + w_in_g["p1"][2:] + w_out_g["p1"][2:]:
            cp.wait_send()
        for cp in (l_pk_out, l_wo_out, l_wi_out):
            cp.wait()

    return pl.pallas_call(
        body, name="grads_reduce",
        out_shape=(jax.ShapeDtypeStruct((D, wi_w), F32), jax.ShapeDtypeStruct((512, D), F32),
                   jax.ShapeDtypeStruct(pack.shape, F32)),
        in_specs=[ANY] * (4 + n_dz), out_specs=(ANY,) * 3,
        scratch_shapes=[
            pltpu.VMEM((lt, 512), BF16), pltpu.VMEM((lt, 512), BF16), pltpu.VMEM((lt, wi_w), BF16),
            pltpu.VMEM((2, 512, wi_w), BF16), pltpu.VMEM((2, 512, wi_w), F32), pltpu.VMEM((4, 512, wi_w), BF16),
            pltpu.VMEM((3, 512, wi_w), BF16), pltpu.VMEM((3, 512, wi_w), BF16),
            pltpu.VMEM((lx, 512), BF16), pltpu.VMEM((lx, 512), BF16), pltpu.VMEM((lx, 512), BF16),
            pltpu.VMEM((2, 512, 512), BF16), pltpu.VMEM((2, 512, 512), F32), pltpu.VMEM((4, 512, 512), BF16),
            pltpu.VMEM((3, 512, 512), BF16), pltpu.VMEM((3, 512, 512), BF16),
            pltpu.VMEM((hp, 512), F32), pltpu.VMEM((hp, 512), F32), pltpu.VMEM((hp, 512), BF16),
            pltpu.VMEM((3, hp, 128), BF16), pltpu.VMEM((hp, 128), F32),
            pltpu.SemaphoreType.DMA((27,)), pltpu.SemaphoreType.DMA((27,)), pltpu.SemaphoreType.DMA((15,))],
        compiler_params=_cp(vmem_mb=56),
    )(hn, *dzs, y, do, pack)


def _ada_bwd(c_all, dmx_all_j, dmc_j, dmx_all, dmc, c_ctx, ada_w_full):
    def body(c_ref, dmxj_ref, dmcj_ref, dmx_ref, dmc_ref, cc_ref, w_ref, gw_ref, gb_ref, gc_ref, lhs, rhs, dm8):
        lhs[...] = jnp.zeros_like(lhs)
        rhs[...] = jnp.zeros_like(rhs)
        cv = c_ref[...]
        lhs[0:8, :] = cv * _sigmoid(cv)
        cc = cc_ref[...]
        a_c, da_c = _silu_and_grad(cc)
        lhs[8:9, :] = a_c
        rhs[0:8, :] = dmxj_ref[...]
        rhs[8:9, :] = dmcj_ref[...]
        gw_ref[...] = _dot_tn(lhs[...].astype(BF16), rhs[...].astype(BF16))
        gb_ref[...] = jnp.sum(dmx_ref[...], axis=0, keepdims=True) + dmc_ref[...]
        dm8[...] = jnp.zeros_like(dm8)
        dm8[0:1, :] = dmc_ref[...]
        da = _dot_nt(dm8[...].astype(BF16), w_ref[...])
        gc_ref[...] = da[0:1, :] * da_c

    return pl.pallas_call(
        body, name="ada_bwd",
        out_shape=(jax.ShapeDtypeStruct((D, 768), F32), jax.ShapeDtypeStruct((1, 3 * D), F32),
                   jax.ShapeDtypeStruct((1, D), F32)),
        in_specs=[VMEM] * 7, out_specs=(VMEM,) * 3,
        scratch_shapes=[pltpu.VMEM((16, D), F32), pltpu.VMEM((16, 768), F32), pltpu.VMEM((8, 3 * D), F32)],
        compiler_params=_cp(vmem_mb=32),
    )(c_all, dmx_all_j, dmc_j, dmx_all, dmc, c_ctx, ada_w_full)


def _proj(x, ctx, mods, norm_g, w_full, sgu):
    lx, lc = x.shape[0], ctx.shape[0]
    assert lc == T
    n = 1 + lx // T

    def body(x_ref, c_ref, sh_ref, sc_ref, ng_ref, w0, w1, w2, w3, w4, g_ref, b_ref, sw_ref, bt_ref,
             hn_ref, xa_ref, ga_ref, u_ref, v_ref, gb_ref, ys_ref, mixed_s):
        i = pl.program_id(0)
        is_ctx = i == 0
        xv = jnp.where(is_ctx, c_ref[...], x_ref[...])
        sc = jnp.where(is_ctx, sc_ref[1:2, :], sc_ref[0:1, :])
        sh = jnp.where(is_ctx, sh_ref[1:2, :], sh_ref[0:1, :])
        r = lax.rsqrt(jnp.mean(xv * xv, axis=-1, keepdims=True) + NORM_EPS)
        hb = ((xv * r) * ng_ref[...] * (1.0 + sc) + sh).astype(BF16)
        hn_ref[...] = hb
        xa_ref[...] = _dot(hb, w0[...])

        @pl.when(i > 0)
        def _():
            u_ref[...] = _dot(hb, w2[...])
            v_ref[...] = _dot(hb, w3[...])
            for ch in range(T // HD):
                rows = slice(HD * ch, HD * ch + HD)
                ug = _sgu_parts(u_ref[rows, :], v_ref[rows, :], g_ref[...], b_ref[...], sw_ref, bt_ref, mixed_s)[0]
                ys_ref[rows, :] = ug * mixed_s[...]
            ga_ref[...] = _dot(hb, w1[...])
            gb_ref[...] = _dot(hb, w4[...])

    every = pl.BlockSpec((T, D), lambda i: (i, 0))
    lat = pl.BlockSpec((T, D), lambda i: (jnp.maximum(i - 1, 0), 0))
    vec = pl.BlockSpec((1, D), lambda i: (0, 0))
    in_specs = [lat, pl.BlockSpec((T, D), lambda i: (0, 0)), pl.BlockSpec((8, D), lambda i: (0, 0)),
                pl.BlockSpec((8, D), lambda i: (0, 1)), vec]
    in_specs += [pl.BlockSpec((D, D), lambda i, k=k: (0, k)) for k in range(5)]
    in_specs += [vec, vec, pl.BlockSpec((NH, HD, HD), lambda i: (0, 0, 0)), pl.BlockSpec((HD, NH), lambda i: (0, 0))]
    full_s = jax.ShapeDtypeStruct((lc + lx, D), F32)
    lat_s = jax.ShapeDtypeStruct((lx, D), F32)
    return pl.pallas_call(
        body, name="proj", grid=(n,),
        out_shape=(jax.ShapeDtypeStruct((lc + lx, D), BF16), full_s, lat_s, lat_s, lat_s, lat_s, lat_s),
        in_specs=in_specs, out_specs=(every, every, lat, lat, lat, lat, lat),
        scratch_shapes=[pltpu.VMEM((HD, D), F32)], compiler_params=_cp(1, vmem_mb=56),
    )(x, ctx, mods, mods, norm_g, *([w_full] * 5), *sgu)


def _tile_specs(rows_per_pos, width, n_tiles, tile):
    last = n_tiles * (T // 8) - 1
    r = rows_per_pos
    return [pl.BlockSpec((T * r, width), lambda i: (tile(i), 0)),
            pl.BlockSpec((8 * r, width), lambda i: (jnp.maximum(tile(i) * (T // 8) - 1, 0), 0)),
            pl.BlockSpec((8 * r, width), lambda i: (jnp.minimum((tile(i) + 1) * (T // 8), last), 0))]


def _has_prev(tile):
    return tile >= 2


def _has_next(tile, nt):
    return jnp.logical_and(tile >= 1, tile < nt - 1)


ZT = pl.BlockSpec((T * NH, HD), lambda i: (i, 0))
CONV_CHUNK = 32


SCAN_SUB = 8


def _scan_tile(chains, post, carry_ref):
    blk = T // SCAN_SUB

    def step(k, state):
        new = []
        for ci, (a_ref, x_ref, o_ref, q_ref, reverse, xscale) in enumerate(chains):
            for q in range(SCAN_SUB):
                s, p = state[ci * SCAN_SUB + q]
                t = (q + 1) * blk - 1 - k if reverse else q * blk + k
                r = pl.ds(_mo(t * NH, NH), NH)
                a = a_ref[r, :]
                x = x_ref[r, :] if xscale is None else x_ref[r, :] * xscale
                if post:
                    o = x + s
                    o_ref[r, :] = o
                    q_ref[r, :] = p
                    new.append((a * o, a * p))
                else:
                    o = a * s + x
                    p = a * p
                    o_ref[r, :] = o
                    q_ref[r, :] = p
                    new.append((o, p))
        return tuple(new)

    zero = jnp.zeros((NH, HD), F32)
    one = jnp.ones((NH, HD), F32)
    final = lax.fori_loop(0, blk, step, tuple((zero, one) for _ in range(len(chains) * SCAN_SUB)))
    for ci, (a_ref, x_ref, o_ref, q_ref, reverse, xscale) in enumerate(chains):
        carry = carry_ref[ci]
        for q in (range(SCAN_SUB - 1, -1, -1) if reverse else range(SCAN_SUB)):
            rows = pl.ds(q * blk * NH, blk * NH)
            fixed = o_ref[rows, :].reshape(blk, NH, HD) + q_ref[rows, :].reshape(blk, NH, HD) * carry[None]
            o_ref[rows, :] = fixed.reshape(blk * NH, HD)
            s_loc, p_loc = final[ci * SCAN_SUB + q]
            carry = s_loc + p_loc * carry
        carry_ref[ci] = carry


def _lru_fwd(xa, conv_wz, conv_bz, wcat, bcat, lamcat, name):
    lx = xa.shape[0]
    n = lx // T
    tile_u = lambda i: i
    tile_d = lambda i: jnp.where(i == 0, 0, n - i)

    def body(xm_u, xp_u, xn_u, xm_d, xp_d, xn_d, cw, cb, w_ref, b_ref, lam_ref,
             xcz_o, af_o, ab_o, hf_o, hb_o, gf_o, gb_o, fu, fd, pad, xc_d, x_u, x_d, q_u, q_d, carry):
        i = pl.program_id(0)

        @pl.when(i == 0)
        def _():
            carry[...] = jnp.zeros_like(carry)

        def conv_gates(xm, xp, xn, tile, d, xc_ref, a_ref, x_ref, g_ref):
            pmask = jnp.where(_has_prev(tile), 1.0, 0.0)
            nmask = jnp.where(_has_next(tile, n), 1.0, 0.0)
            for h in range(NH):
                cols = slice(HD * h, HD * h + HD)
                pad[_zrows(h, 8), :] = xp[:, cols] * pmask
                pad[pl.ds(8 * NH + h, T, stride=NH), :] = xm[:, cols]
                pad[pl.ds((T + 8) * NH + h, 8, stride=NH), :] = xn[:, cols] * nmask

            def conv_chunk(ci, c_):
                base = pl.multiple_of(ci * (CONV_CHUNK * NH), CONV_CHUNK * NH)
                acc = None
                for k in range(4):
                    sl = pad[pl.ds(base + (7 + k) * NH, CONV_CHUNK * NH), :].reshape(CONV_CHUNK, NH, HD)
                    term = sl * cw[k][None]
                    acc = term if acc is None else acc + term
                acc = acc + cb[...][None]
                xc_ref[pl.ds(base, CONV_CHUNK * NH), :] = acc.reshape(CONV_CHUNK * NH, HD)
                return c_
            lax.fori_loop(0, T // CONV_CHUNK, conv_chunk, 0)

            for h in range(NH):
                xch = xc_ref[_zrows(h, T), :]
                pre = _dot(xch.astype(BF16), w_ref[h, :, 256 * d:256 * d + 256]) + b_ref[h:h + 1, 256 * d:256 * d + 256]
                r, gi, _, _, a, mult = _lru_gate(pre, lam_ref[h:h + 1, :], d, 0)
                a_ref[_zrows(h, T), :] = a
                x_ref[_zrows(h, T), :] = mult * gi * xch
                for q, val in enumerate((r, gi, mult)):
                    g_ref[:, q * D + HD * h:q * D + HD * h + HD] = val

        conv_gates(xm_u, xp_u, xn_u, tile_u(i), 0, xcz_o, af_o, x_u, gf_o)
        conv_gates(xm_d, xp_d, xn_d, tile_d(i), 1, xc_d, ab_o, x_d, gb_o)

        _scan_tile([(af_o, x_u, hf_o, q_u, False, None), (ab_o, x_d, hb_o, q_d, True, None)], False, carry)

        @pl.when(i == 0)
        def _():
            fu[...] = carry[0]
            fd[...] = carry[1]

    full = lambda shape: pl.BlockSpec(shape, lambda i: (0,) * len(shape))
    st = full((NH, HD))
    in_specs = _tile_specs(1, D, n, tile_u) + _tile_specs(1, D, n, tile_d)
    in_specs += [full((4, NH, HD)), st, full((NH, HD, 4 * HD)), full((NH, 4 * HD)), full((NH, 2 * HD))]
    up = pl.BlockSpec((T * NH, HD), lambda i: (tile_u(i), 0))
    dn = pl.BlockSpec((T * NH, HD), lambda i: (tile_d(i), 0))
    zs = jax.ShapeDtypeStruct((lx * NH, HD), F32)
    ss = jax.ShapeDtypeStruct((NH, HD), F32)
    zbuf = pltpu.VMEM((T * NH, HD), F32)
    gs = jax.ShapeDtypeStruct((lx, 3 * D), F32)
    g_up = pl.BlockSpec((T, 3 * D), lambda i: (tile_u(i), 0))
    g_dn = pl.BlockSpec((T, 3 * D), lambda i: (tile_d(i), 0))
    return pl.pallas_call(
        body, name=name, grid=(n,), out_shape=(zs,) * 5 + (gs, gs, ss, ss), in_specs=in_specs,
        out_specs=(up, up, dn, up, dn, g_up, g_dn, st, st),
        scratch_shapes=[pltpu.VMEM(((T + 16) * NH, HD), F32), zbuf, zbuf, zbuf, zbuf, zbuf,
                        pltpu.VMEM((2, NH, HD), F32)],
        compiler_params=_cp(1, vmem_mb=48),
    )(xa, xa, xa, xa, xa, xa, conv_wz, conv_bz, wcat, bcat, lamcat)


def _sgu_parts(u, v, lng, lnb, w_ref, bt_ref, mixed_s):
    ug, dug = _gelu_and_grad(u)
    vg, dvg = _gelu_and_grad(v)
    mu = jnp.mean(vg, axis=-1, keepdims=True)
    vc = vg - mu
    rstd = lax.rsqrt(jnp.mean(vc * vc, axis=-1, keepdims=True) + LN_EPS)
    vh = vc * rstd
    vn = (vh * lng + lnb).astype(BF16)
    for g in range(NH):
        cols = slice(HD * g, HD * g + HD)
        mixed_s[:, cols] = _dot(w_ref[g], vn[:, cols]) + bt_ref[:, g:g + 1]
    return ug, dug, dvg, rstd, vh, vn


def _sgu_bwd_chunk(u, v, dys_v, lng, lnb, w_ref, bt_ref, mixed_s, dvn_s, dw_ref, db_ref, dg_ref, dbl_ref):
    ug, dug, dvg, rstd, vh, vn = _sgu_parts(u, v, lng, lnb, w_ref, bt_ref, mixed_s)
    du = (dys_v * mixed_s[...] * dug).astype(BF16)
    dmix = dys_v * ug
    ones = jnp.ones((8, HD), BF16)
    for g in range(NH):
        cols = slice(HD * g, HD * g + HD)
        dm = dmix[:, cols]
        hi = dm.astype(BF16)
        lo = (dm - hi.astype(F32)).astype(BF16)
        dw_ref[g] += _dot_nt(hi, vn[:, cols])
        db_ref[g:g + 1, :] += (_dot_nt(ones, hi) + _dot_nt(ones, lo))[0:1, :]
        dvn_s[:, cols] = _dot_tn(w_ref[g], hi)
    dvn = dvn_s[...]
    dg_ref[...] += jnp.sum(dvn * vh, axis=0, keepdims=True)
    dbl_ref[...] += jnp.sum(dvn, axis=0, keepdims=True)
    dvh = dvn * lng
    dvg_in = rstd * (dvh - jnp.mean(dvh, axis=-1, keepdims=True) - vh * jnp.mean(dvh * vh, axis=-1, keepdims=True))
    return du, (dvg_in * dvg).astype(BF16)


def _out_fwd_bwd(hf_z, hb_z, ga, gb, ys, x, tgt, mods, final_g, w_out_full):
    lx = x.shape[0]
    n = lx // T

    def body(hf_ref, hb_ref, ga_ref, gb_ref, ys_ref, x_ref, t_ref, gx_ref, fg_ref, w_ref,
             loss_ref, dfg_ref, dgx_ref, dxn_ref, y_ref, do_ref, dga_ref, dgb_ref, dyl_ref, dys_ref, yl_s):
        i = pl.program_id(0)

        @pl.when(i == 0)
        def _():
            loss_ref[...] = jnp.zeros_like(loss_ref)
            dfg_ref[...] = jnp.zeros_like(dfg_ref)
            dgx_ref[...] = jnp.zeros_like(dgx_ref)

        for h in range(NH):
            yl_s[:, HD * h:HD * h + HD] = hf_ref[_zrows(h, T), :] + hb_ref[_zrows(h, T), :]
        yl = yl_s[...]
        gav = ga_ref[...]
        gbv = gb_ref[...]
        sa, dsa = _silu_and_grad(gav)
        sb, dsb = _silu_and_grad(gbv)
        ysv = ys_ref[...]
        y_ref[:, 0:D] = (yl * sa).astype(BF16)
        y_ref[:, D:2 * D] = (ysv * sb).astype(BF16)
        o = _dot(y_ref[...], w_ref[...])
        gx = gx_ref[0:1, :]
        xnew = x_ref[...] + gx * o
        r2 = lax.rsqrt(jnp.mean(xnew * xnew, axis=-1, keepdims=True) + NORM_EPS)
        xh = xnew * r2
        fg = fg_ref[...]
        err = xh * fg - t_ref[...]
        loss_ref[...] += 0.5 * jnp.sum(jnp.mean(err * err, axis=-1, keepdims=True), axis=0, keepdims=True)

        @pl.when(i == n - 1)
        def _():
            lp = loss_ref[...]
            lp1 = lp.astype(BF16).astype(F32)
            lp2 = (lp - lp1).astype(BF16).astype(F32)
            lp3 = (lp - lp1 - lp2).astype(BF16).astype(F32)
            lane = lax.broadcasted_iota(jnp.int32, lp.shape, 1)
            loss_ref[...] = jnp.where(lane == 0, lp1, jnp.where(lane == 1, lp2, jnp.where(lane == 2, lp3, 0.0)))
        dout = err * (1.0 / D)
        dfg_ref[...] += jnp.sum(dout * xh, axis=0, keepdims=True)
        dxh = dout * fg
        dxn = r2 * (dxh - xh * jnp.mean(dxh * xh, axis=-1, keepdims=True))
        dxn_ref[...] = dxn
        dgx_ref[...] += jnp.sum(dxn * o, axis=0, keepdims=True)
        do = (dxn * gx).astype(BF16)
        do_ref[...] = do
        dy = _dot_nt(do, w_ref[...])
        dy1 = dy[:, 0:D]
        dy2 = dy[:, D:2 * D]
        dga_ref[...] = (dy1 * yl * dsa).astype(BF16)
        dgb_ref[...] = (dy2 * ysv * dsb).astype(BF16)
        dys_ref[...] = dy2 * sb
        yl_s[...] = dy1 * sa
        for h in range(NH):
            dyl_ref[_zrows(h, T), :] = yl_s[:, HD * h:HD * h + HD]

    row = pl.BlockSpec((T, D), lambda i: (i, 0))
    vec = pl.BlockSpec((1, D), lambda i: (0, 0))
    zlat = pl.BlockSpec((T * NH, HD), lambda i: (i + 1, 0))
    in_specs = [zlat, zlat, row, row, row, row, row, pl.BlockSpec((8, D), lambda i: (0, 2)), vec,
                pl.BlockSpec((2 * D, D), lambda i: (0, 0))]
    out_shape = (jax.ShapeDtypeStruct((1, D), F32), jax.ShapeDtypeStruct((1, D), F32), jax.ShapeDtypeStruct((1, D), F32),
                 jax.ShapeDtypeStruct((lx, D), F32), jax.ShapeDtypeStruct((lx, 2 * D), BF16),
                 jax.ShapeDtypeStruct((lx, D), BF16), jax.ShapeDtypeStruct((lx, D), BF16),
                 jax.ShapeDtypeStruct((lx, D), BF16), jax.ShapeDtypeStruct((lx * NH, HD), F32),
                 jax.ShapeDtypeStruct((lx, D), F32))
    out_specs = (vec, vec, vec, row, pl.BlockSpec((T, 2 * D), lambda i: (i, 0)),
                 row, row, row, ZT, row)
    return pl.pallas_call(
        body, name="out_fwd_bwd", grid=(n,), out_shape=out_shape, in_specs=in_specs, out_specs=out_specs,
        scratch_shapes=[pltpu.VMEM((T, D), F32)],
        compiler_params=_cp(1, vmem_mb=56),
    )(hf_z, hb_z, ga, gb, ys, x, tgt, mods, final_g, w_out_full)


def _lru_bwd(xc_z, dy_z, hf_z, hb_z, af_z, ab_z, gf, gb, s_b, wcat, lamcat, name):
    lx = xc_z.shape[0] // NH
    n = lx // T
    tile_u = lambda i: jnp.where(i == n - 1, 0, i + 1)
    tile_d = lambda i: n - 1 - i

    def body(xc_u, dy_u, hb_ref, hbn_ref, ab_ref, gb_ref, xc_d, dy_d, hf_ref, hfp_ref, af_ref, gf_ref,
             sb_ref, w_ref, lam_ref, dxcb_ref, dxcf_ref, dw_ref, db_ref, dl_ref,
             lb_s, lf_s, q_u, q_d, pf_s, pb_s, dpre_s, carry):
        i = pl.program_id(0)
        tu, td = tile_u(i), tile_d(i)

        @pl.when(i == 0)
        def _():
            dw_ref[...] = jnp.zeros_like(dw_ref)
            db_ref[...] = jnp.zeros_like(db_ref)
            dl_ref[...] = jnp.zeros_like(dl_ref)
            carry[...] = jnp.zeros_like(carry)

        _scan_tile([(ab_ref, dy_u, lb_s, q_u, False, jnp.where(tu == 0, 0.0, 1.0)),
                    (af_ref, dy_d, lf_s, q_d, True, jnp.where(td == 0, 0.0, 1.0))], True, carry)
        zero = jnp.zeros((NH, HD), F32)
        pb_s[pl.ds(0, T * NH), :] = hb_ref[...]
        pb_s[pl.ds(T * NH, NH), :] = jnp.where(tu == n - 1, sb_ref[...], jnp.where(tu == 0, zero, hbn_ref[pl.ds(0, NH), :]))
        pf_s[pl.ds(0, NH), :] = jnp.where(td == 0, zero, hfp_ref[pl.ds(7 * NH, NH), :])
        pf_s[pl.ds(NH, T * NH), :] = hf_ref[...]
        sides = ((1, xc_u, lb_s, pb_s, NH, ab_ref, gb_ref, dxcb_ref), (0, xc_d, lf_s, pf_s, 0, af_ref, gf_ref, dxcf_ref))
        for d, xc_ref, adj_s, prev_s, prev_off, a_ref, g_ref, dxc_ref in sides:
            wcols = slice(256 * d, 256 * d + 256)
            for h in range(NH):
                xch = xc_ref[_zrows(h, T), :]
                xcb = xch.astype(BF16)
                r, gi, mult = (g_ref[:, q * D + HD * h:q * D + HD * h + HD] for q in range(3))
                a = a_ref[_zrows(h, T), :]
                lam = lam_ref[h:h + 1, HD * d:HD * d + HD]
                sp = _softplus(-lam)
                du = adj_s[_zrows(h, T), :]
                da = du * prev_s[pl.ds(prev_off + h, T, stride=NH), :]
                dgi = du * mult * xch
                dmult = du * gi * xch
                dla = da * a - dmult * (a * a) / mult
                dr = dla * ((-LRU_C) * sp)
                dsp = jnp.sum(dla * ((-LRU_C) * r), axis=0, keepdims=True)
                dl_ref[h:h + 1, HD * d:HD * d + HD] += dsp * (-_sigmoid(-lam))
                dpre_s[:, 0:HD] = dr * r * (1.0 - r)
                dpre_s[:, HD:2 * HD] = dgi * gi * (1.0 - gi)
                dpre = dpre_s[...]
                dpb = dpre.astype(BF16)
                dw_ref[h, :, wcols] += _dot_tn(xcb, dpb)
                db_ref[h:h + 1, wcols] += jnp.sum(dpre, axis=0, keepdims=True)
                dxc_ref[_zrows(h, T), :] = du * mult * gi + _dot_nt(dpb, w_ref[h, :, wcols])

    full = lambda shape: pl.BlockSpec(shape, lambda i: (0,) * len(shape))
    wsp, bsp, lsp = full((NH, HD, 4 * HD)), full((NH, 4 * HD)), full((NH, 2 * HD))
    st = full((NH, HD))
    up = pl.BlockSpec((T * NH, HD), lambda i: (tile_u(i), 0))
    dn = pl.BlockSpec((T * NH, HD), lambda i: (tile_d(i), 0))
    dy_up = pl.BlockSpec((T * NH, HD), lambda i: (jnp.maximum(tile_u(i) - 1, 0), 0))
    dy_dn = pl.BlockSpec((T * NH, HD), lambda i: (jnp.maximum(tile_d(i) - 1, 0), 0))
    nxt = _tile_specs(NH, HD, n, tile_u)[2]
    prv = _tile_specs(NH, HD, n, tile_d)[1]
    g_up = pl.BlockSpec((T, 3 * D), lambda i: (tile_u(i), 0))
    g_dn = pl.BlockSpec((T, 3 * D), lambda i: (tile_d(i), 0))
    zs = jax.ShapeDtypeStruct((lx * NH, HD), F32)
    zbuf = pltpu.VMEM((T * NH, HD), F32)
    zbuf1 = pltpu.VMEM(((T + 1) * NH, HD), F32)
    return pl.pallas_call(
        body, name=name, grid=(n,),
        out_shape=(zs, zs, jax.ShapeDtypeStruct((NH, HD, 4 * HD), F32), jax.ShapeDtypeStruct((NH, 4 * HD), F32),
                   jax.ShapeDtypeStruct((NH, 2 * HD), F32)),
        in_specs=[up, dy_up, up, nxt, up, g_up, dn, dy_dn, dn, prv, dn, g_dn, st, wsp, lsp],
        out_specs=(up, dn, wsp, bsp, lsp),
        scratch_shapes=[zbuf, zbuf, zbuf, zbuf, zbuf1, zbuf1, pltpu.VMEM((T, 2 * HD), F32),
                        pltpu.VMEM((2, NH, HD), F32)],
        compiler_params=_cp(1, vmem_mb=56),
    )(xc_z, dy_z, hb_z, hb_z, ab_z, gb, xc_z, dy_z, hf_z, hf_z, af_z, gf, s_b, wcat, lamcat)


def _conv_bwd(dxc_a, dxc_b, xa, conv_wz, dcw0, dcb0, name):
    lx = dxc_a.shape[0] // NH
    n = lx // T

    def body(dm_a, dp_a, dn_a, dm_b, dp_b, dn_b, xan_ref, cw, dcw0_ref, dcb0_ref, dxa_ref, dcw_ref, dcb_ref,
             pad, dxa_s, xa_ref):
        i = pl.program_id(0)

        @pl.when(i == 0)
        def _():
            dcw_ref[...] = dcw0_ref[...]
            dcb_ref[...] = dcb0_ref[...]

        for h in range(NH):
            xa_ref[_zrows(h, T), :] = xan_ref[:, HD * h:HD * h + HD]

        pmask = jnp.where(_has_prev(i), 1.0, 0.0)
        nmask = jnp.where(_has_next(i, n), 1.0, 0.0)
        pad[pl.ds(0, 8 * NH), :] = (dp_a[...] + dp_b[...]) * pmask
        pad[pl.ds(8 * NH, T * NH), :] = dm_a[...] + dm_b[...]
        pad[pl.ds((T + 8) * NH, 8 * NH), :] = (dn_a[...] + dn_b[...]) * nmask

        def chunk(ci, carry):
            base = pl.multiple_of(ci * (CONV_CHUNK * NH), CONV_CHUNK * NH)
            xav = xa_ref[pl.ds(base, CONV_CHUNK * NH), :].reshape(CONV_CHUNK, NH, HD)
            acc = None
            for k in range(4):
                sl = pad[pl.ds(base + (9 - k) * NH, CONV_CHUNK * NH), :].reshape(CONV_CHUNK, NH, HD)
                term = sl * cw[k][None]
                acc = term if acc is None else acc + term
                dcw_ref[k] += jnp.sum(sl * xav, axis=0)
                if k == 1:
                    dcb_ref[...] += jnp.sum(sl, axis=0)
            dxa_s[pl.ds(base, CONV_CHUNK * NH), :] = acc.reshape(CONV_CHUNK * NH, HD)
            return carry
        lax.fori_loop(0, T // CONV_CHUNK, chunk, 0)
        for h in range(NH):
            dxa_ref[:, HD * h:HD * h + HD] = dxa_s[_zrows(h, T), :].astype(BF16)

    full = lambda shape: pl.BlockSpec(shape, lambda i: (0,) * len(shape))
    return pl.pallas_call(
        body, name=name, grid=(n,),
        out_shape=(jax.ShapeDtypeStruct((lx, D), BF16), jax.ShapeDtypeStruct((4, NH, HD), F32),
                   jax.ShapeDtypeStruct((NH, HD), F32)),
        in_specs=_tile_specs(NH, HD, n, lambda i: i) * 2 + [pl.BlockSpec((T, D), lambda i: (i, 0)), full((4, NH, HD)),
                                                            full((4, NH, HD)), full((NH, HD))],
        out_specs=(pl.BlockSpec((T, D), lambda i: (i, 0)), full((4, NH, HD)), full((NH, HD))),
        scratch_shapes=[pltpu.VMEM(((T + 16) * NH, HD), F32), pltpu.VMEM((T * NH, HD), F32),
                        pltpu.VMEM((T * NH, HD), F32)],
        compiler_params=_cp(1, vmem_mb=48),
    )(dxc_a, dxc_a, dxc_a, dxc_b, dxc_b, dxc_b, xa, conv_wz, dcw0, dcb0)


def _proj_bwd(dxa, dga, dgb, x, ctx, dxn, mods, norm_g, w_full, sgu):
    lx, lc = x.shape[0], ctx.shape[0]
    assert lc == T
    n = 1 + lx // T

    def body(dxa_ref, dga_ref, dgb_ref, w0, w1, w4, w2, w3, x_ref, c_ref, sc_ref, ng_ref, dxn_ref,
             u_ref, v_ref, dy_ref, g_ref, b_ref, sw_ref, bt_ref,
             gx_ref, dng_ref, dscx_ref, dshx_ref, dscc_ref, dshc_ref, du_ref, dv_ref, dws_ref, dbs_ref, dlg_ref, dlb_ref,
             mixed_s, dvn_s):
        i = pl.program_id(0)
        is_ctx = i == 0

        @pl.when(is_ctx)
        def _():
            for acc in (dng_ref, dscx_ref, dshx_ref, dscc_ref, dshc_ref, dws_ref, dbs_ref, dlg_ref, dlb_ref):
                acc[...] = jnp.zeros_like(acc)

        xv = jnp.where(is_ctx, c_ref[...], x_ref[...])
        sc1 = 1.0 + jnp.where(is_ctx, sc_ref[1:2, :], sc_ref[0:1, :])
        r = lax.rsqrt(jnp.mean(xv * xv, axis=-1, keepdims=True) + NORM_EPS)
        xn = xv * r
        ng = ng_ref[...]

        def norm_bwd(dhn, dsc_ref, dsh_ref, with_x):
            t = dhn * xn
            dng_ref[...] += jnp.sum(t * sc1, axis=0, keepdims=True)
            dsc_ref[...] += jnp.sum(t * ng, axis=0, keepdims=True)
            dsh_ref[...] += jnp.sum(dhn, axis=0, keepdims=True)
            if with_x:
                dxh = dhn * (ng * sc1)
                gx_ref[...] = dxn_ref[...] + r * (dxh - xn * jnp.mean(dxh * xn, axis=-1, keepdims=True))

        @pl.when(is_ctx)
        def _():
            norm_bwd(_dot_nt(dxa_ref[...], w0[...]), dscc_ref, dshc_ref, False)

        @pl.when(i > 0)
        def _():
            def sgu_chunk(ch):
                rows = slice(HD * ch, HD * ch + HD)
                du, dv = _sgu_bwd_chunk(u_ref[rows, :], v_ref[rows, :], dy_ref[rows, :], g_ref[...], b_ref[...],
                                        sw_ref, bt_ref, mixed_s, dvn_s, dws_ref, dbs_ref, dlg_ref, dlb_ref)
                du_ref[rows, :] = du
                dv_ref[rows, :] = dv

            dhn = _dot_nt(dxa_ref[...], w0[...])
            sgu_chunk(0)
            dhn = dhn + _dot_nt(dga_ref[...], w1[...])
            for ch in range(1, T // HD):
                sgu_chunk(ch)
            dhn = dhn + _dot_nt(dgb_ref[...], w4[...])
            dhn = dhn + _dot_nt(du_ref[...], w2[...]) + _dot_nt(dv_ref[...], w3[...])
            norm_bwd(dhn, dscx_ref, dshx_ref, True)

    every = pl.BlockSpec((T, D), lambda i: (i, 0))
    lat = pl.BlockSpec((T, D), lambda i: (jnp.maximum(i - 1, 0), 0))
    vec = pl.BlockSpec((1, D), lambda i: (0, 0))
    wsp = pl.BlockSpec((NH, HD, HD), lambda i: (0, 0, 0))
    bsp = pl.BlockSpec((NH, HD), lambda i: (0, 0))
    in_specs = [every, lat, lat] + [pl.BlockSpec((D, D), lambda i, k=k: (0, k)) for k in (0, 1, 4, 2, 3)]
    in_specs += [lat, pl.BlockSpec((T, D), lambda i: (0, 0)), pl.BlockSpec((8, D), lambda i: (0, 1)), vec, lat]
    in_specs += [lat, lat, lat, vec, vec, wsp, pl.BlockSpec((HD, NH), lambda i: (0, 0))]
    vs = jax.ShapeDtypeStruct((1, D), F32)
    zb = jax.ShapeDtypeStruct((lx, D), BF16)
    return pl.pallas_call(
        body, name="proj_bwd", grid=(n,),
        out_shape=(jax.ShapeDtypeStruct((lx, D), F32), vs, vs, vs, vs, vs, zb, zb,
                   jax.ShapeDtypeStruct((NH, HD, HD), F32), jax.ShapeDtypeStruct((NH, HD), F32), vs, vs),
        in_specs=in_specs, out_specs=(lat, vec, vec, vec, vec, vec, lat, lat, wsp, bsp, vec, vec),
        scratch_shapes=[pltpu.VMEM((HD, D), F32), pltpu.VMEM((HD, D), F32)], compiler_params=_cp(1, vmem_mb=56),
    )(dxa, dga, dgb, *([w_full] * 5), x, ctx, mods, norm_g, dxn, *sgu)


def _adam_math(w, g, m, v):
    m = ADAM_B1 * m + (1.0 - ADAM_B1) * g
    v = ADAM_B2 * v + (1.0 - ADAM_B2) * (g * g)
    m_hat = m / (1.0 - ADAM_B1 ** ADAM_STEP)
    v_hat = v / (1.0 - ADAM_B2 ** ADAM_STEP)
    delta = -ADAM_LR * (m_hat / (jnp.sqrt(v_hat) + ADAM_EPS) + ADAM_WD * w)
    return delta, m, v


def _adam_big(w, g, m, v, name):
    rows, cols = w.shape
    tr = 256

    def body(w_ref, g_ref, m_ref, v_ref, d_o, m_o, v_o):
        d, mm, vv = _adam_math(w_ref[...], g_ref[...], m_ref[...], v_ref[...])
        d_o[...] = d
        m_o[...] = mm
        v_o[...] = vv

    blk = pl.BlockSpec((tr, cols), lambda i: (i, 0))
    s = jax.ShapeDtypeStruct((rows, cols), F32)
    return pl.pallas_call(
        body, name=name, grid=(rows // tr,), out_shape=(s, s, s), in_specs=[blk] * 4, out_specs=(blk,) * 3,
        compiler_params=_cp(1, vmem_mb=48),
    )(w, g, m, v)


def _adam_small(items, tot):
    ni = len(items)
    pieces = [it[1] if isinstance(it[1], list) else None for it in items]
    flat = [a for it, pc in zip(items, pieces) for a in ((it[0], it[2], it[3]) if pc is not None else it)]
    n_in = len(flat) + 1
    out_shape = tuple(jax.ShapeDtypeStruct(it[0].shape, F32) for it, pc in zip(items, pieces)
                      for _ in range(4 if pc is not None else 3))
    n_out = len(out_shape)
    n_loads = sum(3 + (len(pc) if pc is not None else 1) for pc in pieces)

    def body(*refs):
        ins, tot_ref, outs = refs[:n_in - 1], refs[n_in - 1], refs[n_in:n_in + n_out]
        bufs = refs[n_in + n_out:n_in + n_out + 7 * ni]
        sem_in, sem_out = refs[n_in + n_out + 7 * ni:]
        loads, q_in, q_sem = [], 0, 0
        for k, pc in enumerate(pieces):
            w_b, g_b, m_b, v_b = bufs[7 * k:7 * k + 4]
            srcs = [(ins[q_in], w_b)]
            if pc is None:
                srcs.append((ins[q_in + 1], g_b))
                q_in += 1
            else:
                srcs += [(tot_ref.at[pl.ds(r0, nr), pl.ds(c0, nc)], g_b.at[pl.ds(d0, nr), :]) for r0, nr, c0, nc, d0 in pc]
            srcs += [(ins[q_in + 1], m_b), (ins[q_in + 2], v_b)]
            q_in += 3
            mine = []
            for src, dst in srcs:
                mine.append(pltpu.make_async_copy(src, dst, sem_in.at[q_sem]))
                q_sem += 1
            loads.append(mine)
        for mine in loads:
            for cp in mine:
                cp.start()
        stores, q_out = [], 0
        for k, pc in enumerate(pieces):
            for cp in loads[k]:
                cp.wait()
            w_b, g_b, m_b, v_b = bufs[7 * k:7 * k + 4]
            res = _adam_math(w_b[...], g_b[...], m_b[...], v_b[...])
            srcs = []
            for q in range(3):
                bufs[7 * k + 4 + q][...] = res[q]
                srcs.append(bufs[7 * k + 4 + q])
            if pc is not None:
                srcs.append(g_b)
            for src in srcs:
                cp = pltpu.make_async_copy(src, outs[q_out], sem_out.at[q_out])
                cp.start()
                stores.append(cp)
                q_out += 1
        for cp in stores:
            cp.wait()

    scratch = [pltpu.VMEM(it[0].shape, F32) for it in items for _ in range(7)]
    scratch += [pltpu.SemaphoreType.DMA((n_loads,)), pltpu.SemaphoreType.DMA((n_out,))]
    res = pl.pallas_call(
        body, name="adam_small", out_shape=out_shape, in_specs=[HBM] * n_in, out_specs=(HBM,) * n_out,
        scratch_shapes=scratch, compiler_params=_cp(vmem_mb=40),
    )(*flat, tot)
    outs, q = [], 0
    for pc in pieces:
        outs.append(tuple(res[q:q + 3]) + ((res[q + 3],) if pc is not None else (None,)))
        q += 4 if pc is not None else 3
    return outs


def kernel(x, c, ctx, c_ctx, ada_w, ada_b, norm_g, w_in, conv_w, conv_b, lru_wa, lru_ba, lru_wx, lru_bx, lru_lambda, sgu_ln_g, sgu_ln_b, sgu_w, sgu_b, w_out, final_g, loss_target, m_c_ctx, m_ada_w, m_ada_b, m_norm_g, m_w_in, m_conv_w, m_conv_b, m_lru_wa, m_lru_ba, m_lru_wx, m_lru_bx, m_lru_lambda, m_sgu_ln_g, m_sgu_ln_b, m_sgu_w, m_sgu_b, m_w_out, m_final_g, v_c_ctx, v_ada_w, v_ada_b, v_norm_g, v_w_in, v_conv_w, v_conv_b, v_lru_wa, v_lru_ba, v_lru_wx, v_lru_bx, v_lru_lambda, v_sgu_ln_g, v_sgu_ln_b, v_sgu_w, v_sgu_b, v_w_out, v_final_g):
    ix, iy, ic = lax.axis_index("x"), lax.axis_index("y"), lax.axis_index("c")
    chip = 2 * ix + iy
    dev = 2 * chip + ic
    lx = x.shape[1]
    lc = ctx.shape[1]

    smalls = jnp.concatenate([conv_w[0], lru_lambda[0], jnp.zeros((10, 256), F32)], axis=0)
    c_ctx2 = c_ctx.reshape(1, D)
    ada_b_j = lax.dynamic_slice(ada_b, (0, 768 * chip), (1, 768))
    mods, c_slots, sm_all, w_in_full, wo_land, ada_land = _gather_in(c, c_ctx2, ada_w[0], ada_b_j, w_in[0], w_out[0],
                                                                     smalls)
    wo_ss, wo_rs, ada_ss, ada_rs, wo_land, ada_land, token = _late_gather_start(wo_land, ada_land)
    mods = mods + token[0:1, 0:1]
    sm3 = sm_all.reshape(NCHIP, 16, 256)
    conv_w_full = sm3[:, 0:4, :].transpose(1, 0, 2).reshape(4, D)
    lam_full = sm3[:, 4:6, :].transpose(1, 0, 2).reshape(2, D)
    conv_wz = conv_w_full.reshape(4, NH, HD)
    conv_bz = conv_b.reshape(NH, HD)
    lamcat = lam_full.reshape(2, NH, HD).transpose(1, 0, 2).reshape(NH, 2 * HD)
    wa, wx, ba, bx = lru_wa[0], lru_wx[0], lru_ba[0], lru_bx[0]
    wcat = jnp.concatenate([wa[0], wx[0], wa[1], wx[1]], axis=-1).astype(BF16)
    bcat = jnp.concatenate([ba[0], bx[0], ba[1], bx[1]], axis=-1)
    sgu_wb = sgu_w[0].astype(BF16)
    sgu_bt = sgu_b[0].T
    final_g2 = final_g.reshape(1, D)

    zero_s = jnp.zeros((NH, HD), F32)
    hn, xa_all, ga, u, v, gb, ys = _proj(x[0], ctx[0], mods, norm_g, w_in_full, (sgu_ln_g, sgu_ln_b, sgu_wb, sgu_bt))
    xcz, af, ab, hf, hb, gf, gb_l, _, hb0 = _lru_fwd(xa_all, conv_wz, conv_bz, wcat, bcat, lamcat, "lru_fwd")

    w_out_full = _late_gather_wait(wo_land, wo_ss, wo_rs, "w_out", hf, "late_gather_wait_w_out")
    (loss_part, dfg, dgx, dxn, y, do, dga, dgb, dyl_z, dys) = _out_fwd_bwd(
        hf, hb, ga, gb, ys, x[0], loss_target[0], mods, final_g2, w_out_full)

    dxc_b, dxc_f, dwc, dbc, dlc = _lru_bwd(xcz, dyl_z, hf, hb, af, ab, gf, gb_l, hb0, wcat, lamcat, "lru_bwd")
    dxa, dcw, dcb = _conv_bwd(dxc_b, dxc_f, xa_all, conv_wz, jnp.zeros((4, NH, HD), F32), zero_s, "conv_bwd")

    grad_x, dng, dsc_x, dsh_x, dsc_c, dsh_c, du, dv, d_sgu_w, d_sgu_b, d_ln_g, d_ln_b = _proj_bwd(
        dxa, dga, dgb, x[0], ctx[0], dxn, mods, norm_g, w_in_full, (u, v, dys, sgu_ln_g, sgu_ln_b, sgu_wb, sgu_bt))
    dzs = [dxa, dga, du, dv, dgb]

    dmx = jnp.concatenate([dsh_x, dsc_x, dgx], axis=0)
    dmc = jnp.concatenate([dsh_c, dsc_c, jnp.zeros((1, D), F32)], axis=0)
    slot = jnp.concatenate([dmx, loss_part], axis=0)
    slots = lax.dynamic_update_slice(jnp.zeros((32, D), F32), slot, (4 * dev, 0))
    vecs = jnp.concatenate([dfg, dng, dcb.reshape(1, D), d_ln_g, d_ln_b, dcw.reshape(4, D), dmc,
                            jnp.zeros((4, D), F32), slots], axis=0)
    d_sgu_w4 = d_sgu_w.reshape(4, 256, HD).transpose(1, 0, 2).reshape(256, 4 * HD)
    pad8 = lambda a: jnp.pad(a, ((0, 8 - a.shape[0]), (0, 4 * HD - a.shape[1])))
    pack = jnp.concatenate([dwc.reshape(NH * HD, 4 * HD), pad8(dbc), pad8(dlc), d_sgu_w4, pad8(d_sgu_b),
                            vecs.reshape(96, 4 * HD), jnp.zeros((8, 4 * HD), F32)], axis=0)
    g_w_in, g_w_out, tot = _grads_reduce(hn, dzs, lc, y, do, pack)

    n_w = NH * HD
    g_lru_wa = [(0, n_w, 2 * HD * d, HD, n_w * d) for d in range(2)]
    g_lru_wx = [(0, n_w, 2 * HD * d + HD, HD, n_w * d) for d in range(2)]
    g_lru_ba = [(n_w, NH, 2 * HD * d, HD, NH * d) for d in range(2)]
    g_lru_bx = [(n_w, NH, 2 * HD * d + HD, HD, NH * d) for d in range(2)]
    g_sgu_w = [(1040, 256, HD * q, HD, 256 * q) for q in range(4)]
    g_sgu_b = [(1296, NH, 0, HD, 0)]
    g_lc = tot[1032:1040, 0:2 * HD]
    tv = tot[1304:1400].reshape(48, D)
    g_final_g, g_norm_g, g_conv_b, g_ln_g, g_ln_b = tv[0:1], tv[1:2], tv[2:3], tv[3:4], tv[4:5]
    g_conv_w_full = tv[5:9]
    dmc_tot = tv[9:12].reshape(1, 3 * D)
    slots_all = tv[16:48].reshape(8, 4, D)
    dmx_all = slots_all[:, 0:3, :].reshape(8, 3 * D)
    c_all = c_slots.reshape(8, 8, D)[:, 0, :]
    g_lam_full = jnp.stack([g_lc[:, 0:HD], g_lc[:, HD:2 * HD]]).reshape(2, D)
    g_conv_w = lax.dynamic_slice(g_conv_w_full, (0, 256 * chip), (4, 256))
    g_lam = lax.dynamic_slice(g_lam_full, (0, 256 * chip), (2, 256))
    dmx_all_j = lax.dynamic_slice(dmx_all, (0, 768 * chip), (8, 768))
    dmc_j = lax.dynamic_slice(dmc_tot, (0, 768 * chip), (1, 768))
    ada_full = _late_gather_wait(ada_land, ada_ss, ada_rs, "ada_w", tot, "late_gather_wait_ada_w")
    g_ada_w, g_ada_b, g_c_ctx = _ada_bwd(c_all, dmx_all_j, dmc_j, dmx_all, dmc_tot, c_ctx2, ada_full)

    big = {
        "ada_w": _adam_big(ada_w[0], g_ada_w, m_ada_w[0], v_ada_w[0], "adam_ada_w"),
        "w_in": _adam_big(w_in[0], g_w_in, m_w_in[0], v_w_in[0], "adam_w_in"),
        "w_out": _adam_big(w_out[0], g_w_out, m_w_out[0], v_w_out[0], "adam_w_out"),
    }
    small_in = {
        "c_ctx": (c_ctx, g_c_ctx, m_c_ctx, v_c_ctx, (1, D)),
        "ada_b": (ada_b, g_ada_b, m_ada_b, v_ada_b, (1, 3 * D)),
        "norm_g": (norm_g, g_norm_g, m_norm_g, v_norm_g, (1, D)),
        "conv_w": (conv_w, g_conv_w, m_conv_w, v_conv_w, (4, 256)),
        "conv_b": (conv_b, g_conv_b, m_conv_b, v_conv_b, (1, D)),
        "lru_wa": (lru_wa, g_lru_wa, m_lru_wa, v_lru_wa, (2 * NH * HD, HD)),
        "lru_ba": (lru_ba, g_lru_ba, m_lru_ba, v_lru_ba, (2 * NH, HD)),
        "lru_wx": (lru_wx, g_lru_wx, m_lru_wx, v_lru_wx, (2 * NH * HD, HD)),
        "lru_bx": (lru_bx, g_lru_bx, m_lru_bx, v_lru_bx, (2 * NH, HD)),
        "lru_lambda": (lru_lambda, g_lam, m_lru_lambda, v_lru_lambda, (2, 256)),
        "sgu_ln_g": (sgu_ln_g, g_ln_g, m_sgu_ln_g, v_sgu_ln_g, (1, D)),
        "sgu_ln_b": (sgu_ln_b, g_ln_b, m_sgu_ln_b, v_sgu_ln_b, (1, D)),
        "sgu_w": (sgu_w, g_sgu_w, m_sgu_w, v_sgu_w, (NH * HD, HD)),
        "sgu_b": (sgu_b, g_sgu_b, m_sgu_b, v_sgu_b, (NH, HD)),
        "final_g": (final_g, g_final_g, m_final_g, v_final_g, (1, D)),
    }
    names_small = list(small_in)
    res_small = _adam_small([tuple(a if isinstance(a, list) else a.reshape(small_in[k][4]) for a in small_in[k][:4])
                             for k in names_small], tot)
    full_shapes = {"ada_w": ada_w.shape, "w_in": w_in.shape, "w_out": w_out.shape}
    grads, deltas, new_m, new_v = {}, {}, {}, {}
    for k in ("ada_w", "w_in", "w_out"):
        g = {"ada_w": g_ada_w, "w_in": g_w_in, "w_out": g_w_out}[k]
        grads[k] = g.reshape(full_shapes[k])
        deltas[k], new_m[k], new_v[k] = (a.reshape(full_shapes[k]) for a in big[k])
    for k, res in zip(names_small, res_small):
        shape = small_in[k][0].shape
        grads[k] = (small_in[k][1] if res[3] is None else res[3]).reshape(shape)
        deltas[k], new_m[k], new_v[k] = (a.reshape(shape) for a in res[:3])

    loss = jnp.sum(slots_all[:, 3, 0:3])
    order = ["c_ctx", "ada_w", "ada_b", "norm_g", "w_in", "conv_w", "conv_b", "lru_wa", "lru_ba", "lru_wx", "lru_bx",
             "lru_lambda", "sgu_ln_g", "sgu_ln_b", "sgu_w", "sgu_b", "w_out", "final_g"]
    return (loss, grad_x.reshape(x.shape), *[grads[k] for k in order], *[deltas[k] for k in order],
            *[new_m[k] for k in order], *[new_v[k] for k in order])
```

```python
import jax
import jax.numpy as jnp
from jax import lax
from jax.experimental import pallas as pl
from jax.experimental.pallas import tpu as pltpu

F32 = jnp.float32
BF16 = jnp.bfloat16

D = 1024
NH = 8
HD = 128
NCHIP = 4
T = 256
NORM_EPS = 1e-6
LN_EPS = 1e-5
LRU_C = 8.0
ADAM_LR = 0.001
ADAM_B1 = 0.9
ADAM_B2 = 0.999
ADAM_EPS = 1e-08
ADAM_WD = 0.01
ADAM_STEP = 10

VMEM = pl.BlockSpec(memory_space=pltpu.VMEM)
ANY = pl.BlockSpec(memory_space=pl.ANY)
MESH = pl.DeviceIdType.MESH


def _cp(n_grid=0, vmem_mb=None):
    kw = {}
    if n_grid:
        kw["dimension_semantics"] = ("arbitrary",) * n_grid
    if vmem_mb:
        kw["vmem_limit_bytes"] = vmem_mb << 20
    return pltpu.CompilerParams(**kw)


def _sigmoid(x):
    return 0.5 * jnp.tanh(0.5 * x) + 0.5


def _silu_and_grad(x):
    s = _sigmoid(x)
    return x * s, s * (1.0 + x * (1.0 - s))


_GELU_K = 0.7978845608028654
_GELU_C = 0.044715


def _gelu_and_grad(x):
    x2 = x * x
    th = jnp.tanh(x * (_GELU_K + (_GELU_K * _GELU_C) * x2))
    p = 0.5 + 0.5 * th
    g = x * p
    dg = p + g * (1.0 - th) * (_GELU_K + (3.0 * _GELU_K * _GELU_C) * x2)
    return g, dg


def _softplus(x):
    return jnp.maximum(x, 0.0) + jnp.log1p(jnp.exp(-jnp.abs(x)))


def _lru_gate(pre, lam_row, d, off=None):
    off = 256 * d if off is None else off
    r = _sigmoid(pre[:, off:off + HD])
    gi = _sigmoid(pre[:, off + HD:off + 2 * HD])
    lam = lam_row[:, HD * d:HD * d + HD]
    sp = _softplus(-lam)
    la = (-LRU_C) * r * sp
    a = jnp.exp(la)
    x2 = 2.0 * la
    m2 = jnp.where(x2 > -1e-3, -x2 * (1.0 + 0.5 * x2), 1.0 - a * a)
    mult = jnp.sqrt(m2)
    return r, gi, lam, sp, a, mult


def _dot(a, b):
    return jnp.dot(a, b, preferred_element_type=F32)


def _dot_tn(a, b):
    return lax.dot_general(a, b, (((0,), (0,)), ((), ())), preferred_element_type=F32)


def _dot_nt(a, b):
    return lax.dot_general(a, b, (((1,), (1,)), ((), ())), preferred_element_type=F32)


def _mo(v, m):
    return v if isinstance(v, int) else pl.multiple_of(v, m)


def _zrows(h, n):
    return pl.ds(h, n, stride=NH)


def _gather_in(c, c_ctx, ada_w, ada_b_j, w_in, w_out, smalls):
    nch = [1, 4]
    wrows = lambda cc, q: (pl.ds(_mo(512 * cc, 16), 512) if q is None
                           else pl.ds(_mo(512 * cc + (512 // nch[1]) * q, 16), 512 // nch[1]))
    specs = [
        ((64, 256), F32, lambda r, jj, cc, q=None: r.at[pl.ds(_mo(16 * jj + 8 * cc, 8), 8), :]),
        ((D, 5120), BF16, lambda r, jj, cc, q=None: r.at[wrows(cc, q), pl.ds(_mo(1280 * jj, 128), 1280)]),
    ]
    halves = [lambda r, cc, q=None: r.at[pl.ds(_mo(8 * cc, 8), 8), :],
              lambda r, cc, q=None: r.at[wrows(cc, q), :]]
    na = len(specs)
    sem_base = [0, 6 * nch[0]]
    sidx = lambda a, q, k: sem_base[a] + 6 * q + k
    n_tiny = 6 * sum(nch)
    n_sem = n_tiny + 10

    def body(c_ref, cc_ref, ada_ref, adab_ref, win_ref, wout_ref, sm_ref,
             mods_o, call_o, sm_o, win_o, wol_o, adal_o, s_win, s_ada, s_wout, f_win, f_ada, f_wout, cslot, lhs, mbuf,
             send_sems, recv_sems, local_sems, load_sems):
        x, y, c = lax.axis_index("x"), lax.axis_index("y"), lax.axis_index("c")
        j = 2 * x + y
        dev = 2 * j + c
        sib = (x, y, 1 - c)
        chips = [(1 - x, y), (x, 1 - y), (1 - x, 1 - y)]
        cj = [2 * cx + cy for cx, cy in chips]
        outs = [sm_o, win_o]
        srcs = [sm_ref, s_win]

        def copy(idx, src, dst, to):
            return pltpu.make_async_remote_copy(src_ref=src, dst_ref=dst, send_sem=send_sems.at[idx],
                                                recv_sem=recv_sems.at[idx], device_id=to, device_id_type=MESH)

        sends = []

        def start(cp):
            cp.start()
            sends.append(cp)

        cslot[...] = jnp.zeros_like(cslot)
        cslot[0:1, :] = c_ref[...]
        my_slot = pl.ds(_mo(8 * dev, 8), 8)
        others = [sib] + [(*chips[k], c) for k in range(3)] + [(*chips[k], 1 - c) for k in range(3)]
        other_dev = [dev + 1 - 2 * c] + [2 * cj[k] + c for k in range(3)] + [2 * cj[k] + 1 - c for k in range(3)]
        base = n_tiny
        for r in range(7):
            start(copy(base + r, cslot, call_o.at[my_slot, :], others[r]))
        call_o[my_slot, :] = cslot[...]

        crow = 512 // nch[1]
        loads = []
        for cc in (c, 1 - c):
            for q in range(nch[1]):
                rows = pl.ds(_mo(512 * cc + crow * q, 16), crow)
                loads.append(pltpu.make_async_copy(win_ref.at[rows, :], f_win.at[rows, :], load_sems.at[len(loads)]))
        loads.append(pltpu.make_async_copy(ada_ref, f_ada, load_sems.at[len(loads)]))
        loads.append(pltpu.make_async_copy(wout_ref, f_wout, load_sems.at[len(loads)]))
        for ld in loads:
            ld.start()
        for k in range(2):
            start(copy(sidx(0, 0, k), halves[0](srcs[0], c), specs[0][2](outs[0], j, c), (*chips[k], c)))
        for q in range(nch[1]):
            loads[q].wait()
            rows = pl.ds(_mo(512 * c + crow * q, 16), crow)
            s_win[rows, :] = f_win[rows, :].astype(BF16)
            for k in range(2):
                start(copy(sidx(1, q, k), halves[1](s_win, c, q), specs[1][2](win_o, j, c, q), (*chips[k], c)))
        for q in range(nch[1]):
            loads[nch[1] + q].wait()
            rows = pl.ds(_mo(512 * (1 - c) + crow * q, 16), crow)
            s_win[rows, :] = f_win[rows, :].astype(BF16)
        local = []
        for a in range(na):
            for cc in range(2):
                lc = pltpu.make_async_copy(halves[a](srcs[a], cc), specs[a][2](outs[a], j, cc), local_sems.at[2 * a + cc])
                lc.start()
                local.append(lc)
        loads[2 * nch[1]].wait()
        s_ada[...] = f_ada[...].astype(BF16)
        loads[2 * nch[1] + 1].wait()
        s_wout[...] = f_wout[...].astype(BF16)
        for q, (src, dst) in enumerate([(s_wout, wol_o.at[pl.ds(_mo(512 * j, 16), 512), :]),
                                        (s_ada, adal_o.at[:, pl.ds(_mo(768 * j, 128), 768)])]):
            lc = pltpu.make_async_copy(src, dst, local_sems.at[2 * na + q])
            lc.start()
            local.append(lc)

        for r in range(7):
            slot = call_o.at[pl.ds(_mo(8 * other_dev[r], 8), 8), :]
            copy(base + r, slot, slot, sib).wait_recv()
        lhs[...] = jnp.zeros_like(lhs)
        for b in range(8):
            cv = call_o[8 * b:8 * b + 1, :]
            lhs[b:b + 1, :] = cv * _sigmoid(cv)
        cv = cc_ref[...]
        lhs[8:9, :] = cv * _sigmoid(cv)
        mbuf[j] = _dot(lhs[...].astype(BF16), s_ada[...]) + adab_ref[...]
        for k in range(3):
            start(copy(base + 7 + k, mbuf.at[j], mbuf.at[j], (*chips[k], c)))
        for k in range(3):
            copy(base + 7 + k, mbuf.at[cj[k]], mbuf.at[cj[k]], sib).wait_recv()
        mods_o[...] = jnp.zeros_like(mods_o)
        for jj in range(NCHIP):
            mods_o[0:1, 768 * jj:768 * jj + 768] = mbuf[jj, pl.ds(dev, 1), :]
            mods_o[1:2, 768 * jj:768 * jj + 768] = mbuf[jj, 8:9, :]

        kx = [1 - x, x, 1 - x]
        ky = [y, 1 - y, 1 - y]
        pick = lambda k, lst: jnp.where(k == 0, lst[0], jnp.where(k == 1, lst[1], lst[2]))
        for a in range(na):
            for q in range(nch[a]):
                for step, k in enumerate([c, 1 - c]):
                    reg = specs[a][2](outs[a], pick(k, cj), c, q)
                    copy(sidx(a, q, k), reg, reg, sib).wait_recv()
                    if step == 0:
                        start(copy(sidx(a, q, 2), reg, reg, (pick(1 - c, kx), pick(1 - c, ky), c)))
                    start(copy(sidx(a, q, 3 + k), reg, reg, sib))
        for a in range(na):
            for q in range(nch[a]):
                reg = specs[a][2](outs[a], cj[2], c, q)
                copy(sidx(a, q, 2), reg, reg, sib).wait_recv()
                start(copy(sidx(a, q, 5), reg, reg, sib))
        for a in range(na):
            for q in range(nch[a]):
                for k in range(3):
                    reg = specs[a][2](outs[a], cj[k], 1 - c, q)
                    copy(sidx(a, q, 3 + k), reg, reg, sib).wait_recv()
        for cp in sends:
            cp.wait_send()
        for lc in local:
            lc.wait()

    out_shape = (jax.ShapeDtypeStruct((8, 3 * D), F32), jax.ShapeDtypeStruct((64, D), F32),
                 jax.ShapeDtypeStruct(specs[0][0], F32), jax.ShapeDtypeStruct(specs[1][0], BF16),
                 jax.ShapeDtypeStruct((2048, D), BF16), jax.ShapeDtypeStruct((D, 3 * D), BF16))
    return pl.pallas_call(
        body, name="gather_in", out_shape=out_shape,
        in_specs=[VMEM, VMEM, ANY, VMEM, ANY, ANY, VMEM], out_specs=(VMEM, VMEM, VMEM, ANY, ANY, ANY),
        scratch_shapes=[pltpu.VMEM((D, 1280), BF16), pltpu.VMEM((D, 768), BF16), pltpu.VMEM((512, D), BF16),
                        pltpu.VMEM((D, 1280), F32), pltpu.VMEM((D, 768), F32), pltpu.VMEM((512, D), F32),
                        pltpu.VMEM((8, D), F32), pltpu.VMEM((16, D), F32), pltpu.VMEM((NCHIP, 16, 768), F32),
                        pltpu.SemaphoreType.DMA((n_sem,)), pltpu.SemaphoreType.DMA((n_sem,)),
                        pltpu.SemaphoreType.DMA((2 * na + 2,)), pltpu.SemaphoreType.DMA((2 * nch[1] + 2,))],
        compiler_params=_cp(vmem_mb=56),
    )(c, c_ctx, ada_w, ada_b_j, w_in, w_out, smalls)


HBM = pl.BlockSpec(memory_space=pltpu.HBM)
SEM = pl.BlockSpec(memory_space=pltpu.SEMAPHORE)


def _late_gather_regions(x, y, c):
    chips = [(1 - x, y), (x, 1 - y), (1 - x, 1 - y)]
    wo_reg = lambda r, jj, cc: r.at[pl.ds(_mo(512 * jj + 256 * cc, 16), 256), :]
    ada_reg = lambda r, jj, cc: r.at[pl.ds(_mo(512 * cc, 16), 512), pl.ds(_mo(768 * jj, 128), 768)]
    return chips, wo_reg, ada_reg


def _late_gather_start(wo_land, ada_land):
    def body(wol_ref, adal_ref, wo_ss, wo_rs, ada_ss, ada_rs, wol_thru, adal_thru, token):
        x, y, c = lax.axis_index("x"), lax.axis_index("y"), lax.axis_index("c")
        j = 2 * x + y
        chips, wo_reg, ada_reg = _late_gather_regions(x, y, c)
        for k in range(3):
            for cc in range(2):
                pltpu.make_async_remote_copy(src_ref=wo_reg(wol_ref, j, c), dst_ref=wo_reg(wol_ref, j, c),
                                             send_sem=wo_ss.at[2 * k + cc], recv_sem=wo_rs.at[2 * k + c],
                                             device_id=(*chips[k], cc), device_id_type=MESH).start()
        for k in range(3):
            for cc in range(2):
                pltpu.make_async_remote_copy(src_ref=ada_reg(adal_ref, j, c), dst_ref=ada_reg(adal_ref, j, c),
                                             send_sem=ada_ss.at[2 * k + cc], recv_sem=ada_rs.at[2 * k + c],
                                             device_id=(*chips[k], cc), device_id_type=MESH).start()
        token[...] = jnp.zeros_like(token)

    sems = pltpu.SemaphoreType.DMA((6,))
    return pl.pallas_call(
        body, name="late_gather_start",
        out_shape=(sems, sems, sems, sems, pltpu.HBM(wo_land.shape, BF16), pltpu.HBM(ada_land.shape, BF16),
                   jax.ShapeDtypeStruct((8, 128), F32)),
        in_specs=(HBM, HBM), out_specs=(SEM, SEM, SEM, SEM, HBM, HBM, VMEM), input_output_aliases={0: 4, 1: 5},
        compiler_params=pltpu.CompilerParams(has_side_effects=pltpu.SideEffectType.DATAFLOW_SIDE_EFFECTING),
    )(pltpu.with_memory_space_constraint(wo_land, pltpu.HBM), pltpu.with_memory_space_constraint(ada_land, pltpu.HBM))


def _late_gather_wait(land, send_sems, recv_sems, which, after, name):
    def body(land_ref, ss, rs, after_ref, land_out):
        x, y, c = lax.axis_index("x"), lax.axis_index("y"), lax.axis_index("c")
        j = 2 * x + y
        chips, wo_reg, ada_reg = _late_gather_regions(x, y, c)
        reg = wo_reg if which == "w_out" else ada_reg
        for k in range(3):
            kj = 2 * chips[k][0] + chips[k][1]
            for cc in range(2):
                cp = pltpu.make_async_remote_copy(src_ref=reg(land_ref, j, c), dst_ref=reg(land_ref, kj, cc),
                                                  send_sem=ss.at[2 * k + cc], recv_sem=rs.at[2 * k + cc],
                                                  device_id=(*chips[k], cc), device_id_type=MESH)
                cp.wait_send()
                cp.wait_recv()

    return pl.pallas_call(
        body, name=name, out_shape=pltpu.HBM(land.shape, land.dtype),
        in_specs=(HBM, SEM, SEM, ANY), out_specs=HBM, input_output_aliases={0: 0},
        compiler_params=pltpu.CompilerParams(has_side_effects=pltpu.SideEffectType.DATAFLOW_SIDE_EFFECTING),
    )(land, send_sems, recv_sems, after)


RCHUNK = 16


def _grads_reduce(hn, dzs, lc, y, do, pack):
    rp = pack.shape[0]
    hp = rp // 2
    assert hp % RCHUNK == 0
    wi_w = 1280
    lx = hn.shape[0] - lc
    lt = lx + lc
    n_dz = len(dzs)

    def body(*refs):
        hn_hbm, dz_hbm = refs[0], refs[1:1 + n_dz]
        y_hbm, do_hbm, pk_hbm, wi_out, wo_out, pk_out = refs[1 + n_dz:7 + n_dz]
        (hn_mine, hn_other, dzbuf, wi_other, wi_mine, wi_recv, wi_send, wi_rb,
         y_blk, do_mine, do_other, wo_other, wo_mine, wo_recv, wo_send, wo_rb,
         pk_mine, pk_recv, pk_send, pk_rb, pk_own, send_sems, recv_sems, local_sems) = refs[7 + n_dz:]
        x, y, c = lax.axis_index("x"), lax.axis_index("y"), lax.axis_index("c")
        j = 2 * x + y
        sib = (x, y, 1 - c)
        chips = [(1 - x, y), (x, 1 - y), (1 - x, 1 - y)]
        cj = [2 * cx + cy for cx, cy in chips]
        near = (jnp.where(c == 0, 1 - x, x), jnp.where(c == 0, y, 1 - y), c)
        slabs = [cj[2], cj[0], cj[1], j]

        def copy(k, src, dst, to):
            return pltpu.make_async_remote_copy(src_ref=src, dst_ref=dst, send_sem=send_sems.at[k],
                                                recv_sem=recv_sems.at[k], device_id=to, device_id_type=MESH)

        def local(k, src, dst):
            cp = pltpu.make_async_copy(src, dst, local_sems.at[k])
            cp.start()
            return cp

        rows_half = lambda r, cc, n: r.at[pl.ds(_mo(cc * n, 16), n), :]
        cols_half = lambda r, cc, n: r.at[:, pl.ds(_mo(cc * n, 128), n)]
        pk_piece = lambda r, cc, jj: r.at[pl.ds(_mo(cc * hp, 16), hp), pl.ds(_mo(jj * 128, 128), 128)]

        sends = []

        def start(cp):
            cp.start()
            sends.append(cp)

        def dz_pieces(s):
            g0 = wi_w * s
            k0, off0 = g0 // D, g0 % D
            w0 = min(D - off0, wi_w)
            pieces = [(k0, off0, w0, 0)]
            if w0 < wi_w:
                pieces.append((k0 + 1, 0, wi_w - w0, w0))
            return pieces

        def dz_copies(s):
            cps = []
            for q, (k, off, w, dst) in enumerate(dz_pieces(s)):
                cps.append(pltpu.make_async_copy(dz_hbm[k].at[pl.ds(lc if k == 0 else 0, lx), pl.ds(off, w)],
                                                 dzbuf.at[pl.ds(lc, lx), pl.ds(dst, w)], local_sems.at[11 + q]))
            if s == 0:
                cps.append(pltpu.make_async_copy(dz_hbm[0].at[pl.ds(0, lc), :], dzbuf.at[pl.ds(0, lc), pl.ds(0, D)],
                                                 local_sems.at[13]))
            return cps

        def dz_load(sl):
            for s in range(NCHIP):
                @pl.when(sl == s)
                def _():
                    if s == 0:
                        dzbuf[pl.ds(0, lc), pl.ds(D, wi_w - D)] = jnp.zeros((lc, wi_w - D), BF16)
                    else:
                        dzbuf[pl.ds(0, lc), :] = jnp.zeros((lc, wi_w), BF16)
                    for cp in dz_copies(s):
                        cp.start()

        def dz_wait(sl):
            for s in range(NCHIP):
                @pl.when(sl == s)
                def _():
                    for cp in dz_copies(s):
                        cp.wait()

        l_pk = local(0, rows_half(pk_hbm, c, hp), pk_mine)
        start(copy(0, rows_half(pk_hbm, 1 - c, hp), pk_recv, sib))
        col = lambda r, cc: r.at[:, pl.ds(_mo(cc * 512, 128), 512)]
        do_loads = [local(7, col(do_hbm, c), do_mine), local(14, col(do_hbm, 1 - c), do_other)]
        hn_loads = [local(2, col(hn_hbm, c), hn_mine), local(4, col(hn_hbm, 1 - c), hn_other)]
        y_copy = lambda s: pltpu.make_async_copy(col(y_hbm, slabs[s]), y_blk, local_sems.at[1])
        y_copy(0).start()
        dz_load(slabs[0])

        def pair_sum(mine, recv, send, nrows, keep, relayed=None):
            def step(i, carry):
                rows = pl.ds(_mo(i * RCHUNK, RCHUNK), RCHUNK)
                s = mine[rows, :] + recv[rows, :].astype(F32)
                if relayed is not None:
                    s = s + relayed[rows, :].astype(F32)
                if keep:
                    mine[rows, :] = s
                if send is not None:
                    send[rows, :] = s.astype(BF16)
                return carry
            lax.fori_loop(0, nrows // RCHUNK, step, 0)

        def chip_sum(own, rb, nrows, terms=(0, 1, 2)):
            def step(i, carry):
                rows = pl.ds(_mo(i * RCHUNK, RCHUNK), RCHUNK)
                acc = own[rows, :]
                for q in terms:
                    acc = acc + rb[q, rows, :].astype(F32)
                own[rows, :] = acc
                return carry
            lax.fori_loop(0, nrows // RCHUNK, step, 0)

        w_in_g = dict(other=wi_other, mine=wi_mine, recv=wi_recv, send=wi_send, rb=wi_rb, p1_sems=(2, 3, 4, 5), p2_sem=12,
                      p1=[None] * NCHIP, wait_load=lambda s: dz_wait(slabs[s]), load=lambda s: dz_load(slabs[s]),
                      dot_other=lambda: _dot_tn(hn_other[...], dzbuf[...]), dot_mine=lambda: _dot_tn(hn_mine[...], dzbuf[...]))
        w_out_g = dict(other=wo_other, mine=wo_mine, recv=wo_recv, send=wo_send, rb=wo_rb, p1_sems=(1, 24, 25, 26), p2_sem=9,
                       p1=[None] * NCHIP, wait_load=lambda s: y_copy(s).wait(), load=lambda s: y_copy(s).start(),
                       dot_other=lambda: _dot_tn(y_blk[...], do_other[...]), dot_mine=lambda: _dot_tn(y_blk[...], do_mine[...]))

        def piece_matmuls(g, s):
            if s >= 2:
                g["p1"][s - 2].wait_send()
            g["wait_load"](s)
            g["other"][s % 2] = g["dot_other"]().astype(BF16)
            g["p1"][s] = copy(g["p1_sems"][s], g["other"].at[s % 2], g["recv"].at[s], sib)
            g["p1"][s].start()
            g["mine"][s % 2] = g["dot_mine"]()
            if s + 1 < NCHIP:
                g["load"](s + 1)

        def piece_finish(g, s):
            mine, recv, send, rb, p2 = g["mine"].at[s % 2], g["recv"].at[s], g["send"], g["rb"], g["p2_sem"]
            nrows = mine.shape[0]
            copy(g["p1_sems"][s], recv, recv, sib).wait_recv()
            if s == 3:
                pair_sum(mine, recv, None, nrows, True)
                return
            if s == 0:
                pair_sum(mine, recv, send.at[0], nrows, False)
                start(copy(p2, send.at[0], rb.at[0], near))
                return
            adds_relayed = c == (1 if s == 1 else 0)

            @pl.when(adds_relayed)
            def _():
                copy(p2, rb.at[0], rb.at[0], sib).wait_recv()
                pair_sum(mine, recv, send.at[s], nrows, False, rb.at[0])

            @pl.when(jnp.logical_not(adds_relayed))
            def _():
                pair_sum(mine, recv, send.at[s], nrows, False)
            start(copy(p2 + s, send.at[s], rb.at[s], (*chips[s - 1], c)))

        def piece_total(g):
            for k in (1, 2):
                copy(g["p2_sem"] + k, g["rb"].at[k], g["rb"].at[k], sib).wait_recv()
            chip_sum(g["mine"].at[1], g["rb"], g["mine"].shape[1], (1, 2))

        for cp in do_loads:
            cp.wait()
        piece_matmuls(w_out_g, 0)
        piece_matmuls(w_out_g, 1)
        piece_finish(w_out_g, 0)
        piece_matmuls(w_out_g, 2)
        piece_finish(w_out_g, 1)
        piece_matmuls(w_out_g, 3)
        piece_finish(w_out_g, 2)

        for cp in hn_loads:
            cp.wait()
        piece_matmuls(w_in_g, 0)

        l_pk.wait()
        copy(0, pk_recv, pk_recv, sib).wait_recv()
        pair_sum(pk_mine, pk_recv, pk_send, hp, True)
        for k in range(3):
            start(copy(6 + k, pk_send.at[:, pl.ds(_mo(cj[k] * 128, 128), 128)], pk_rb.at[k], (*chips[k], c)))
        l_pk_own = local(6, pk_mine.at[:, pl.ds(_mo(j * 128, 128), 128)], pk_own)

        piece_matmuls(w_in_g, 1)
        piece_finish(w_in_g, 0)

        l_pk_own.wait()
        for k in range(3):
            copy(6 + k, pk_rb.at[k], pk_rb.at[k], sib).wait_recv()
        chip_sum(pk_own, pk_rb, hp)
        l_pk_out = local(8, pk_own, pk_piece(pk_out, c, j))
        start(copy(15, pk_own, pk_piece(pk_out, c, j), sib))
        for k in range(2):
            start(copy(16 + k, pk_own, pk_piece(pk_out, c, j), (*chips[k], c)))

        piece_matmuls(w_in_g, 2)
        piece_finish(w_in_g, 1)
        piece_matmuls(w_in_g, 3)
        piece_finish(w_in_g, 2)

        piece_finish(w_out_g, 3)
        piece_total(w_out_g)
        l_wo_out = local(9, wo_mine.at[1], cols_half(wo_out, c, 512))
        start(copy(22, wo_mine.at[1], cols_half(wo_out, c, 512), sib))

        far = (jnp.where(c == 0, x, 1 - x), jnp.where(c == 0, 1 - y, y), c)
        for step, k in enumerate([c, 1 - c, 2]):
            reg = pk_piece(pk_out, c, jnp.where(k == 0, cj[0], jnp.where(k == 1, cj[1], cj[2])))
            copy(16 + k, reg, reg, sib).wait_recv()
            if step == 0:
                start(copy(18, reg, reg, far))
            start(copy(19 + k, reg, reg, sib))

        piece_finish(w_in_g, 3)
        piece_total(w_in_g)
        l_wi_out = local(10, wi_mine.at[1], rows_half(wi_out, c, 512))
        start(copy(23, wi_mine.at[1], rows_half(wi_out, c, 512), sib))

        reg = pk_piece(pk_out, 1 - c, j)
        copy(15, reg, reg, sib).wait_recv()
        for k in range(3):
            reg = pk_piece(pk_out, 1 - c, cj[k])
            copy(19 + k, reg, reg, sib).wait_recv()
        reg = cols_half(wo_out, 1 - c, 512)
        copy(22, reg, reg, sib).wait_recv()
        reg = rows_half(wi_out, 1 - c, 512)
        copy(23, reg, reg, sib).wait_recv()
        for cp in sends + w_in_g["p1"][2:] + w_out_g["p1"][2:]:
            cp.wait_send()
        for cp in (l_pk_out, l_wo_out, l_wi_out):
            cp.wait()

    return pl.pallas_call(
        body, name="grads_reduce",
        out_shape=(jax.ShapeDtypeStruct((D, wi_w), F32), jax.ShapeDtypeStruct((512, D), F32),
                   jax.ShapeDtypeStruct(pack.shape, F32)),
        in_specs=[ANY] * (4 + n_dz), out_specs=(ANY,) * 3,
        scratch_shapes=[
            pltpu.VMEM((lt, 512), BF16), pltpu.VMEM((lt, 512), BF16), pltpu.VMEM((lt, wi_w), BF16),
            pltpu.VMEM((2, 512, wi_w), BF16), pltpu.VMEM((2, 512, wi_w), F32), pltpu.VMEM((4, 512, wi_w), BF16),
            pltpu.VMEM((3, 512, wi_w), BF16), pltpu.VMEM((3, 512, wi_w), BF16),
            pltpu.VMEM((lx, 512), BF16), pltpu.VMEM((lx, 512), BF16), pltpu.VMEM((lx, 512), BF16),
            pltpu.VMEM((2, 512, 512), BF16), pltpu.VMEM((2, 512, 512), F32), pltpu.VMEM((4, 512, 512), BF16),
            pltpu.VMEM((3, 512, 512), BF16), pltpu.VMEM((3, 512, 512), BF16),
            pltpu.VMEM((hp, 512), F32), pltpu.VMEM((hp, 512), F32), pltpu.VMEM((hp, 512), BF16),
            pltpu.VMEM((3, hp, 128), BF16), pltpu.VMEM((hp, 128), F32),
            pltpu.SemaphoreType.DMA((27,)), pltpu.SemaphoreType.DMA((27,)), pltpu.SemaphoreType.DMA((15,))],
        compiler_params=_cp(vmem_mb=56),
    )(hn, *dzs, y, do, pack)


def _ada_bwd(c_all, dmx_all_j, dmc_j, dmx_all, dmc, c_ctx, ada_w_full):
    def body(c_ref, dmxj_ref, dmcj_ref, dmx_ref, dmc_ref, cc_ref, w_ref, gw_ref, gb_ref, gc_ref, lhs, rhs, dm8):
        lhs[...] = jnp.zeros_like(lhs)
        rhs[...] = jnp.zeros_like(rhs)
        cv = c_ref[...]
        lhs[0:8, :] = cv * _sigmoid(cv)
        cc = cc_ref[...]
        a_c, da_c = _silu_and_grad(cc)
        lhs[8:9, :] = a_c
        rhs[0:8, :] = dmxj_ref[...]
        rhs[8:9, :] = dmcj_ref[...]
        gw_ref[...] = _dot_tn(lhs[...].astype(BF16), rhs[...].astype(BF16))
        gb_ref[...] = jnp.sum(dmx_ref[...], axis=0, keepdims=True) + dmc_ref[...]
        dm8[...] = jnp.zeros_like(dm8)
        dm8[0:1, :] = dmc_ref[...]
        da = _dot_nt(dm8[...].astype(BF16), w_ref[...])
        gc_ref[...] = da[0:1, :] * da_c

    return pl.pallas_call(
        body, name="ada_bwd",
        out_shape=(jax.ShapeDtypeStruct((D, 768), F32), jax.ShapeDtypeStruct((1, 3 * D), F32),
                   jax.ShapeDtypeStruct((1, D), F32)),
        in_specs=[VMEM] * 7, out_specs=(VMEM,) * 3,
        scratch_shapes=[pltpu.VMEM((16, D), F32), pltpu.VMEM((16, 768), F32), pltpu.VMEM((8, 3 * D), F32)],
        compiler_params=_cp(vmem_mb=32),
    )(c_all, dmx_all_j, dmc_j, dmx_all, dmc, c_ctx, ada_w_full)


def _late_weights(w_hbm, w_late, w_sem, order, i):
    def copy(slot, k):
        return pltpu.make_async_copy(w_hbm.at[:, pl.ds(D * k, D)], w_late.at[slot], w_sem.at[slot])

    @pl.when(i == 0)
    def _():
        for slot, k in enumerate(order):
            copy(slot, k).start()

    def loader(slot, k):
        def load():
            @pl.when(i == 1)
            def _():
                copy(slot, k).wait()
            return w_late[slot]
        return load

    return [loader(slot, k) for slot, k in enumerate(order)]


def _proj(x, ctx, mods, norm_g, w_full, sgu):
    lx, lc = x.shape[0], ctx.shape[0]
    assert lc == T
    n = 1 + lx // T

    def body(x_ref, c_ref, sh_ref, sc_ref, ng_ref, w0, w_hbm, g_ref, b_ref, sw_ref, bt_ref,
             hn_ref, xa_ref, ga_ref, u_ref, v_ref, gb_ref, ys_ref, mixed_s, w_late, w_sem):
        i = pl.program_id(0)
        is_ctx = i == 0
        w2, w3, w1, w4 = _late_weights(w_hbm, w_late, w_sem, (2, 3, 1, 4), i)
        xv = jnp.where(is_ctx, c_ref[...], x_ref[...])
        sc = jnp.where(is_ctx, sc_ref[1:2, :], sc_ref[0:1, :])
        sh = jnp.where(is_ctx, sh_ref[1:2, :], sh_ref[0:1, :])
        r = lax.rsqrt(jnp.mean(xv * xv, axis=-1, keepdims=True) + NORM_EPS)
        hb = ((xv * r) * ng_ref[...] * (1.0 + sc) + sh).astype(BF16)
        hn_ref[...] = hb
        xa_ref[...] = _dot(hb, w0[...])

        @pl.when(i > 0)
        def _():
            u_ref[...] = _dot(hb, w2())
            v_ref[...] = _dot(hb, w3())
            for ch in range(T // HD):
                rows = slice(HD * ch, HD * ch + HD)
                ug = _sgu_parts(u_ref[rows, :], v_ref[rows, :], g_ref[...], b_ref[...], sw_ref, bt_ref, mixed_s)[0]
                ys_ref[rows, :] = ug * mixed_s[...]
            ga_ref[...] = _dot(hb, w1())
            gb_ref[...] = _dot(hb, w4())

    every = pl.BlockSpec((T, D), lambda i: (i, 0))
    lat = pl.BlockSpec((T, D), lambda i: (jnp.maximum(i - 1, 0), 0))
    vec = pl.BlockSpec((1, D), lambda i: (0, 0))
    in_specs = [lat, pl.BlockSpec((T, D), lambda i: (0, 0)), pl.BlockSpec((8, D), lambda i: (0, 0)),
                pl.BlockSpec((8, D), lambda i: (0, 1)), vec, pl.BlockSpec((D, D), lambda i: (0, 0)), ANY]
    in_specs += [vec, vec, pl.BlockSpec((NH, HD, HD), lambda i: (0, 0, 0)), pl.BlockSpec((HD, NH), lambda i: (0, 0))]
    full_s = jax.ShapeDtypeStruct((lc + lx, D), F32)
    lat_s = jax.ShapeDtypeStruct((lx, D), F32)
    return pl.pallas_call(
        body, name="proj", grid=(n,),
        out_shape=(jax.ShapeDtypeStruct((lc + lx, D), BF16), full_s, lat_s, lat_s, lat_s, lat_s, lat_s),
        in_specs=in_specs, out_specs=(every, every, lat, lat, lat, lat, lat),
        scratch_shapes=[pltpu.VMEM((HD, D), F32), pltpu.VMEM((4, D, D), BF16), pltpu.SemaphoreType.DMA((4,))],
        compiler_params=_cp(1, vmem_mb=56),
    )(x, ctx, mods, mods, norm_g, w_full, w_full, *sgu)


def _tile_specs(rows_per_pos, width, n_tiles, tile):
    last = n_tiles * (T // 8) - 1
    r = rows_per_pos
    return [pl.BlockSpec((T * r, width), lambda i: (tile(i), 0)),
            pl.BlockSpec((8 * r, width), lambda i: (jnp.maximum(tile(i) * (T // 8) - 1, 0), 0)),
            pl.BlockSpec((8 * r, width), lambda i: (jnp.minimum((tile(i) + 1) * (T // 8), last), 0))]


def _has_prev(tile):
    return tile >= 2


def _has_next(tile, nt):
    return jnp.logical_and(tile >= 1, tile < nt - 1)


ZT = pl.BlockSpec((T * NH, HD), lambda i: (i, 0))
CONV_CHUNK = 32


SCAN_SUB = 8


def _scan_tile(chains, post, carry_ref):
    blk = T // SCAN_SUB

    def step(k, state):
        new = []
        for ci, (a_ref, x_ref, o_ref, q_ref, reverse, xscale) in enumerate(chains):
            for q in range(SCAN_SUB):
                s, p = state[ci * SCAN_SUB + q]
                t = (q + 1) * blk - 1 - k if reverse else q * blk + k
                r = pl.ds(_mo(t * NH, NH), NH)
                a = a_ref[r, :]
                x = x_ref[r, :] if xscale is None else x_ref[r, :] * xscale
                if post:
                    o = x + s
                    o_ref[r, :] = o
                    q_ref[r, :] = p
                    new.append((a * o, a * p))
                else:
                    o = a * s + x
                    p = a * p
                    o_ref[r, :] = o
                    q_ref[r, :] = p
                    new.append((o, p))
        return tuple(new)

    zero = jnp.zeros((NH, HD), F32)
    one = jnp.ones((NH, HD), F32)
    final = lax.fori_loop(0, blk, step, tuple((zero, one) for _ in range(len(chains) * SCAN_SUB)))
    for ci, (a_ref, x_ref, o_ref, q_ref, reverse, xscale) in enumerate(chains):
        carry = carry_ref[ci]
        for q in (range(SCAN_SUB - 1, -1, -1) if reverse else range(SCAN_SUB)):
            rows = pl.ds(q * blk * NH, blk * NH)
            fixed = o_ref[rows, :].reshape(blk, NH, HD) + q_ref[rows, :].reshape(blk, NH, HD) * carry[None]
            o_ref[rows, :] = fixed.reshape(blk * NH, HD)
            s_loc, p_loc = final[ci * SCAN_SUB + q]
            carry = s_loc + p_loc * carry
        carry_ref[ci] = carry


def _lru_fwd(xa, conv_wz, conv_bz, wcat, bcat, lamcat, name):
    lx = xa.shape[0]
    n = lx // T
    tile_u = lambda i: i
    tile_d = lambda i: jnp.where(i == 0, 0, n - i)

    def body(xm_u, xp_u, xn_u, xm_d, xp_d, xn_d, cw, cb, w_ref, b_ref, lam_ref,
             xcz_o, af_o, ab_o, hf_o, hb_o, gf_o, gb_o, fu, fd, pad, xc_d, x_u, x_d, q_u, q_d, carry):
        i = pl.program_id(0)

        @pl.when(i == 0)
        def _():
            carry[...] = jnp.zeros_like(carry)

        def conv_gates(xm, xp, xn, tile, d, xc_ref, a_ref, x_ref, g_ref):
            pmask = jnp.where(_has_prev(tile), 1.0, 0.0)
            nmask = jnp.where(_has_next(tile, n), 1.0, 0.0)
            for h in range(NH):
                cols = slice(HD * h, HD * h + HD)
                pad[_zrows(h, 8), :] = xp[:, cols] * pmask
                pad[pl.ds(8 * NH + h, T, stride=NH), :] = xm[:, cols]
                pad[pl.ds((T + 8) * NH + h, 8, stride=NH), :] = xn[:, cols] * nmask

            def conv_chunk(ci, c_):
                base = pl.multiple_of(ci * (CONV_CHUNK * NH), CONV_CHUNK * NH)
                acc = None
                for k in range(4):
                    sl = pad[pl.ds(base + (7 + k) * NH, CONV_CHUNK * NH), :].reshape(CONV_CHUNK, NH, HD)
                    term = sl * cw[k][None]
                    acc = term if acc is None else acc + term
                acc = acc + cb[...][None]
                xc_ref[pl.ds(base, CONV_CHUNK * NH), :] = acc.reshape(CONV_CHUNK * NH, HD)
                return c_
            lax.fori_loop(0, T // CONV_CHUNK, conv_chunk, 0)

            for h in range(NH):
                xch = xc_ref[_zrows(h, T), :]
                pre = _dot(xch.astype(BF16), w_ref[h, :, 256 * d:256 * d + 256]) + b_ref[h:h + 1, 256 * d:256 * d + 256]
                r, gi, _, _, a, mult = _lru_gate(pre, lam_ref[h:h + 1, :], d, 0)
                a_ref[_zrows(h, T), :] = a
                x_ref[_zrows(h, T), :] = mult * gi * xch
                for q, val in enumerate((r, gi, mult)):
                    g_ref[:, q * D + HD * h:q * D + HD * h + HD] = val

        conv_gates(xm_u, xp_u, xn_u, tile_u(i), 0, xcz_o, af_o, x_u, gf_o)
        conv_gates(xm_d, xp_d, xn_d, tile_d(i), 1, xc_d, ab_o, x_d, gb_o)

        _scan_tile([(af_o, x_u, hf_o, q_u, False, None), (ab_o, x_d, hb_o, q_d, True, None)], False, carry)

        @pl.when(i == 0)
        def _():
            fu[...] = carry[0]
            fd[...] = carry[1]

    full = lambda shape: pl.BlockSpec(shape, lambda i: (0,) * len(shape))
    st = full((NH, HD))
    in_specs = _tile_specs(1, D, n, tile_u) + _tile_specs(1, D, n, tile_d)
    in_specs += [full((4, NH, HD)), st, full((NH, HD, 4 * HD)), full((NH, 4 * HD)), full((NH, 2 * HD))]
    up = pl.BlockSpec((T * NH, HD), lambda i: (tile_u(i), 0))
    dn = pl.BlockSpec((T * NH, HD), lambda i: (tile_d(i), 0))
    zs = jax.ShapeDtypeStruct((lx * NH, HD), F32)
    ss = jax.ShapeDtypeStruct((NH, HD), F32)
    zbuf = pltpu.VMEM((T * NH, HD), F32)
    gs = jax.ShapeDtypeStruct((lx, 3 * D), F32)
    g_up = pl.BlockSpec((T, 3 * D), lambda i: (tile_u(i), 0))
    g_dn = pl.BlockSpec((T, 3 * D), lambda i: (tile_d(i), 0))
    return pl.pallas_call(
        body, name=name, grid=(n,), out_shape=(zs,) * 5 + (gs, gs, ss, ss), in_specs=in_specs,
        out_specs=(up, up, dn, up, dn, g_up, g_dn, st, st),
        scratch_shapes=[pltpu.VMEM(((T + 16) * NH, HD), F32), zbuf, zbuf, zbuf, zbuf, zbuf,
                        pltpu.VMEM((2, NH, HD), F32)],
        compiler_params=_cp(1, vmem_mb=48),
    )(xa, xa, xa, xa, xa, xa, conv_wz, conv_bz, wcat, bcat, lamcat)


def _sgu_parts(u, v, lng, lnb, w_ref, bt_ref, mixed_s):
    ug, dug = _gelu_and_grad(u)
    vg, dvg = _gelu_and_grad(v)
    mu = jnp.mean(vg, axis=-1, keepdims=True)
    vc = vg - mu
    rstd = lax.rsqrt(jnp.mean(vc * vc, axis=-1, keepdims=True) + LN_EPS)
    vh = vc * rstd
    vn = (vh * lng + lnb).astype(BF16)
    for g in range(NH):
        cols = slice(HD * g, HD * g + HD)
        mixed_s[:, cols] = _dot(w_ref[g], vn[:, cols]) + bt_ref[:, g:g + 1]
    return ug, dug, dvg, rstd, vh, vn


def _sgu_bwd_chunk(u, v, dys_v, lng, lnb, w_ref, bt_ref, mixed_s, dvn_s, dw_ref, db_ref, dg_ref, dbl_ref):
    ug, dug, dvg, rstd, vh, vn = _sgu_parts(u, v, lng, lnb, w_ref, bt_ref, mixed_s)
    du = (dys_v * mixed_s[...] * dug).astype(BF16)
    dmix = dys_v * ug
    ones = jnp.ones((8, HD), BF16)
    for g in range(NH):
        cols = slice(HD * g, HD * g + HD)
        dm = dmix[:, cols]
        hi = dm.astype(BF16)
        lo = (dm - hi.astype(F32)).astype(BF16)
        dw_ref[g] += _dot_nt(hi, vn[:, cols])
        db_ref[g:g + 1, :] += (_dot_nt(ones, hi) + _dot_nt(ones, lo))[0:1, :]
        dvn_s[:, cols] = _dot_tn(w_ref[g], hi)
    dvn = dvn_s[...]
    dg_ref[...] += jnp.sum(dvn * vh, axis=0, keepdims=True)
    dbl_ref[...] += jnp.sum(dvn, axis=0, keepdims=True)
    dvh = dvn * lng
    dvg_in = rstd * (dvh - jnp.mean(dvh, axis=-1, keepdims=True) - vh * jnp.mean(dvh * vh, axis=-1, keepdims=True))
    return du, (dvg_in * dvg).astype(BF16)


def _out_fwd_bwd(hf_z, hb_z, ga, gb, ys, x, tgt, mods, final_g, w_out_full):
    lx = x.shape[0]
    n = lx // T

    def body(hf_ref, hb_ref, ga_ref, gb_ref, ys_ref, x_ref, t_ref, gx_ref, fg_ref, w_ref,
             loss_ref, dfg_ref, dgx_ref, dxn_ref, y_ref, do_ref, dga_ref, dgb_ref, dyl_ref, dys_ref, yl_s):
        i = pl.program_id(0)

        @pl.when(i == 0)
        def _():
            loss_ref[...] = jnp.zeros_like(loss_ref)
            dfg_ref[...] = jnp.zeros_like(dfg_ref)
            dgx_ref[...] = jnp.zeros_like(dgx_ref)

        for h in range(NH):
            yl_s[:, HD * h:HD * h + HD] = hf_ref[_zrows(h, T), :] + hb_ref[_zrows(h, T), :]
        yl = yl_s[...]
        gav = ga_ref[...]
        gbv = gb_ref[...]
        sa, dsa = _silu_and_grad(gav)
        sb, dsb = _silu_and_grad(gbv)
        ysv = ys_ref[...]
        y_ref[:, 0:D] = (yl * sa).astype(BF16)
        y_ref[:, D:2 * D] = (ysv * sb).astype(BF16)
        o = _dot(y_ref[...], w_ref[...])
        gx = gx_ref[0:1, :]
        xnew = x_ref[...] + gx * o
        r2 = lax.rsqrt(jnp.mean(xnew * xnew, axis=-1, keepdims=True) + NORM_EPS)
        xh = xnew * r2
        fg = fg_ref[...]
        err = xh * fg - t_ref[...]
        loss_ref[...] += 0.5 * jnp.sum(jnp.mean(err * err, axis=-1, keepdims=True), axis=0, keepdims=True)

        @pl.when(i == n - 1)
        def _():
            lp = loss_ref[...]
            lp1 = lp.astype(BF16).astype(F32)
            lp2 = (lp - lp1).astype(BF16).astype(F32)
            lp3 = (lp - lp1 - lp2).astype(BF16).astype(F32)
            lane = lax.broadcasted_iota(jnp.int32, lp.shape, 1)
            loss_ref[...] = jnp.where(lane == 0, lp1, jnp.where(lane == 1, lp2, jnp.where(lane == 2, lp3, 0.0)))
        dout = err * (1.0 / D)
        dfg_ref[...] += jnp.sum(dout * xh, axis=0, keepdims=True)
        dxh = dout * fg
        dxn = r2 * (dxh - xh * jnp.mean(dxh * xh, axis=-1, keepdims=True))
        dxn_ref[...] = dxn
        dgx_ref[...] += jnp.sum(dxn * o, axis=0, keepdims=True)
        do = (dxn * gx).astype(BF16)
        do_ref[...] = do
        dy = _dot_nt(do, w_ref[...])
        dy1 = dy[:, 0:D]
        dy2 = dy[:, D:2 * D]
        dga_ref[...] = (dy1 * yl * dsa).astype(BF16)
        dgb_ref[...] = (dy2 * ysv * dsb).astype(BF16)
        dys_ref[...] = dy2 * sb
        yl_s[...] = dy1 * sa
        for h in range(NH):
            dyl_ref[_zrows(h, T), :] = yl_s[:, HD * h:HD * h + HD]

    row = pl.BlockSpec((T, D), lambda i: (i, 0))
    vec = pl.BlockSpec((1, D), lambda i: (0, 0))
    zlat = pl.BlockSpec((T * NH, HD), lambda i: (i + 1, 0))
    in_specs = [zlat, zlat, row, row, row, row, row, pl.BlockSpec((8, D), lambda i: (0, 2)), vec,
                pl.BlockSpec((2 * D, D), lambda i: (0, 0))]
    out_shape = (jax.ShapeDtypeStruct((1, D), F32), jax.ShapeDtypeStruct((1, D), F32), jax.ShapeDtypeStruct((1, D), F32),
                 jax.ShapeDtypeStruct((lx, D), F32), jax.ShapeDtypeStruct((lx, 2 * D), BF16),
                 jax.ShapeDtypeStruct((lx, D), BF16), jax.ShapeDtypeStruct((lx, D), BF16),
                 jax.ShapeDtypeStruct((lx, D), BF16), jax.ShapeDtypeStruct((lx * NH, HD), F32),
                 jax.ShapeDtypeStruct((lx, D), F32))
    out_specs = (vec, vec, vec, row, pl.BlockSpec((T, 2 * D), lambda i: (i, 0)),
                 row, row, row, ZT, row)
    return pl.pallas_call(
        body, name="out_fwd_bwd", grid=(n,), out_shape=out_shape, in_specs=in_specs, out_specs=out_specs,
        scratch_shapes=[pltpu.VMEM((T, D), F32)],
        compiler_params=_cp(1, vmem_mb=56),
    )(hf_z, hb_z, ga, gb, ys, x, tgt, mods, final_g, w_out_full)


def _lru_bwd(xc_z, dy_z, hf_z, hb_z, af_z, ab_z, gf, gb, s_b, wcat, lamcat, name):
    lx = xc_z.shape[0] // NH
    n = lx // T
    tile_u = lambda i: jnp.where(i == n - 1, 0, i + 1)
    tile_d = lambda i: n - 1 - i

    def body(xc_u, dy_u, hb_ref, hbn_ref, ab_ref, gb_ref, xc_d, dy_d, hf_ref, hfp_ref, af_ref, gf_ref,
             sb_ref, w_ref, lam_ref, dxcb_ref, dxcf_ref, dw_ref, db_ref, dl_ref,
             lb_s, lf_s, q_u, q_d, pf_s, pb_s, dpre_s, carry):
        i = pl.program_id(0)
        tu, td = tile_u(i), tile_d(i)

        @pl.when(i == 0)
        def _():
            dw_ref[...] = jnp.zeros_like(dw_ref)
            db_ref[...] = jnp.zeros_like(db_ref)
            dl_ref[...] = jnp.zeros_like(dl_ref)
            carry[...] = jnp.zeros_like(carry)

        _scan_tile([(ab_ref, dy_u, lb_s, q_u, False, jnp.where(tu == 0, 0.0, 1.0)),
                    (af_ref, dy_d, lf_s, q_d, True, jnp.where(td == 0, 0.0, 1.0))], True, carry)
        zero = jnp.zeros((NH, HD), F32)
        pb_s[pl.ds(0, T * NH), :] = hb_ref[...]
        pb_s[pl.ds(T * NH, NH), :] = jnp.where(tu == n - 1, sb_ref[...], jnp.where(tu == 0, zero, hbn_ref[pl.ds(0, NH), :]))
        pf_s[pl.ds(0, NH), :] = jnp.where(td == 0, zero, hfp_ref[pl.ds(7 * NH, NH), :])
        pf_s[pl.ds(NH, T * NH), :] = hf_ref[...]
        sides = ((1, xc_u, lb_s, pb_s, NH, ab_ref, gb_ref, dxcb_ref), (0, xc_d, lf_s, pf_s, 0, af_ref, gf_ref, dxcf_ref))
        for d, xc_ref, adj_s, prev_s, prev_off, a_ref, g_ref, dxc_ref in sides:
            wcols = slice(256 * d, 256 * d + 256)
            for h in range(NH):
                xch = xc_ref[_zrows(h, T), :]
                xcb = xch.astype(BF16)
                r, gi, mult = (g_ref[:, q * D + HD * h:q * D + HD * h + HD] for q in range(3))
                a = a_ref[_zrows(h, T), :]
                lam = lam_ref[h:h + 1, HD * d:HD * d + HD]
                sp = _softplus(-lam)
                du = adj_s[_zrows(h, T), :]
                da = du * prev_s[pl.ds(prev_off + h, T, stride=NH), :]
                dgi = du * mult * xch
                dmult = du * gi * xch
                dla = da * a - dmult * (a * a) / mult
                dr = dla * ((-LRU_C) * sp)
                dsp = jnp.sum(dla * ((-LRU_C) * r), axis=0, keepdims=True)
                dl_ref[h:h + 1, HD * d:HD * d + HD] += dsp * (-_sigmoid(-lam))
                dpre_s[:, 0:HD] = dr * r * (1.0 - r)
                dpre_s[:, HD:2 * HD] = dgi * gi * (1.0 - gi)
                dpre = dpre_s[...]
                dpb = dpre.astype(BF16)
                dw_ref[h, :, wcols] += _dot_tn(xcb, dpb)
                db_ref[h:h + 1, wcols] += jnp.sum(dpre, axis=0, keepdims=True)
                dxc_ref[_zrows(h, T), :] = du * mult * gi + _dot_nt(dpb, w_ref[h, :, wcols])

    full = lambda shape: pl.BlockSpec(shape, lambda i: (0,) * len(shape))
    wsp, bsp, lsp = full((NH, HD, 4 * HD)), full((NH, 4 * HD)), full((NH, 2 * HD))
    st = full((NH, HD))
    up = pl.BlockSpec((T * NH, HD), lambda i: (tile_u(i), 0))
    dn = pl.BlockSpec((T * NH, HD), lambda i: (tile_d(i), 0))
    dy_up = pl.BlockSpec((T * NH, HD), lambda i: (jnp.maximum(tile_u(i) - 1, 0), 0))
    dy_dn = pl.BlockSpec((T * NH, HD), lambda i: (jnp.maximum(tile_d(i) - 1, 0), 0))
    nxt = _tile_specs(NH, HD, n, tile_u)[2]
    prv = _tile_specs(NH, HD, n, tile_d)[1]
    g_up = pl.BlockSpec((T, 3 * D), lambda i: (tile_u(i), 0))
    g_dn = pl.BlockSpec((T, 3 * D), lambda i: (tile_d(i), 0))
    zs = jax.ShapeDtypeStruct((lx * NH, HD), F32)
    zbuf = pltpu.VMEM((T * NH, HD), F32)
    zbuf1 = pltpu.VMEM(((T + 1) * NH, HD), F32)
    return pl.pallas_call(
        body, name=name, grid=(n,),
        out_shape=(zs, zs, jax.ShapeDtypeStruct((NH, HD, 4 * HD), F32), jax.ShapeDtypeStruct((NH, 4 * HD), F32),
                   jax.ShapeDtypeStruct((NH, 2 * HD), F32)),
        in_specs=[up, dy_up, up, nxt, up, g_up, dn, dy_dn, dn, prv, dn, g_dn, st, wsp, lsp],
        out_specs=(up, dn, wsp, bsp, lsp),
        scratch_shapes=[zbuf, zbuf, zbuf, zbuf, zbuf1, zbuf1, pltpu.VMEM((T, 2 * HD), F32),
                        pltpu.VMEM((2, NH, HD), F32)],
        compiler_params=_cp(1, vmem_mb=56),
    )(xc_z, dy_z, hb_z, hb_z, ab_z, gb, xc_z, dy_z, hf_z, hf_z, af_z, gf, s_b, wcat, lamcat)


def _conv_bwd(dxc_a, dxc_b, xa, conv_wz, dcw0, dcb0, name):
    lx = dxc_a.shape[0] // NH
    n = lx // T

    def body(dm_a, dp_a, dn_a, dm_b, dp_b, dn_b, xan_ref, cw, dcw0_ref, dcb0_ref, dxa_ref, dcw_ref, dcb_ref,
             pad, dxa_s, xa_ref):
        i = pl.program_id(0)

        @pl.when(i == 0)
        def _():
            dcw_ref[...] = dcw0_ref[...]
            dcb_ref[...] = dcb0_ref[...]

        for h in range(NH):
            xa_ref[_zrows(h, T), :] = xan_ref[:, HD * h:HD * h + HD]

        pmask = jnp.where(_has_prev(i), 1.0, 0.0)
        nmask = jnp.where(_has_next(i, n), 1.0, 0.0)
        pad[pl.ds(0, 8 * NH), :] = (dp_a[...] + dp_b[...]) * pmask
        pad[pl.ds(8 * NH, T * NH), :] = dm_a[...] + dm_b[...]
        pad[pl.ds((T + 8) * NH, 8 * NH), :] = (dn_a[...] + dn_b[...]) * nmask

        def chunk(ci, carry):
            base = pl.multiple_of(ci * (CONV_CHUNK * NH), CONV_CHUNK * NH)
            xav = xa_ref[pl.ds(base, CONV_CHUNK * NH), :].reshape(CONV_CHUNK, NH, HD)
            acc = None
            for k in range(4):
                sl = pad[pl.ds(base + (9 - k) * NH, CONV_CHUNK * NH), :].reshape(CONV_CHUNK, NH, HD)
                term = sl * cw[k][None]
                acc = term if acc is None else acc + term
                dcw_ref[k] += jnp.sum(sl * xav, axis=0)
                if k == 1:
                    dcb_ref[...] += jnp.sum(sl, axis=0)
            dxa_s[pl.ds(base, CONV_CHUNK * NH), :] = acc.reshape(CONV_CHUNK * NH, HD)
            return carry
        lax.fori_loop(0, T // CONV_CHUNK, chunk, 0)
        for h in range(NH):
            dxa_ref[:, HD * h:HD * h + HD] = dxa_s[_zrows(h, T), :].astype(BF16)

    full = lambda shape: pl.BlockSpec(shape, lambda i: (0,) * len(shape))
    return pl.pallas_call(
        body, name=name, grid=(n,),
        out_shape=(jax.ShapeDtypeStruct((lx, D), BF16), jax.ShapeDtypeStruct((4, NH, HD), F32),
                   jax.ShapeDtypeStruct((NH, HD), F32)),
        in_specs=_tile_specs(NH, HD, n, lambda i: i) * 2 + [pl.BlockSpec((T, D), lambda i: (i, 0)), full((4, NH, HD)),
                                                            full((4, NH, HD)), full((NH, HD))],
        out_specs=(pl.BlockSpec((T, D), lambda i: (i, 0)), full((4, NH, HD)), full((NH, HD))),
        scratch_shapes=[pltpu.VMEM(((T + 16) * NH, HD), F32), pltpu.VMEM((T * NH, HD), F32),
                        pltpu.VMEM((T * NH, HD), F32)],
        compiler_params=_cp(1, vmem_mb=48),
    )(dxc_a, dxc_a, dxc_a, dxc_b, dxc_b, dxc_b, xa, conv_wz, dcw0, dcb0)


def _proj_bwd(dxa, dga, dgb, x, ctx, dxn, mods, norm_g, w_full, sgu):
    lx, lc = x.shape[0], ctx.shape[0]
    assert lc == T
    n = 1 + lx // T

    def body(dxa_ref, dga_ref, dgb_ref, w0, w_hbm, x_ref, c_ref, sc_ref, ng_ref, dxn_ref,
             u_ref, v_ref, dy_ref, g_ref, b_ref, sw_ref, bt_ref,
             gx_ref, dng_ref, dscx_ref, dshx_ref, dscc_ref, dshc_ref, du_ref, dv_ref, dws_ref, dbs_ref, dlg_ref, dlb_ref,
             mixed_s, dvn_s, w_late, w_sem):
        i = pl.program_id(0)
        is_ctx = i == 0
        w1, w4, w2, w3 = _late_weights(w_hbm, w_late, w_sem, (1, 4, 2, 3), i)

        @pl.when(is_ctx)
        def _():
            for acc in (dng_ref, dscx_ref, dshx_ref, dscc_ref, dshc_ref, dws_ref, dbs_ref, dlg_ref, dlb_ref):
                acc[...] = jnp.zeros_like(acc)

        xv = jnp.where(is_ctx, c_ref[...], x_ref[...])
        sc1 = 1.0 + jnp.where(is_ctx, sc_ref[1:2, :], sc_ref[0:1, :])
        r = lax.rsqrt(jnp.mean(xv * xv, axis=-1, keepdims=True) + NORM_EPS)
        xn = xv * r
        ng = ng_ref[...]

        def norm_bwd(dhn, dsc_ref, dsh_ref, with_x):
            t = dhn * xn
            dng_ref[...] += jnp.sum(t * sc1, axis=0, keepdims=True)
            dsc_ref[...] += jnp.sum(t * ng, axis=0, keepdims=True)
            dsh_ref[...] += jnp.sum(dhn, axis=0, keepdims=True)
            if with_x:
                dxh = dhn * (ng * sc1)
                gx_ref[...] = dxn_ref[...] + r * (dxh - xn * jnp.mean(dxh * xn, axis=-1, keepdims=True))

        @pl.when(is_ctx)
        def _():
            norm_bwd(_dot_nt(dxa_ref[...], w0[...]), dscc_ref, dshc_ref, False)

        @pl.when(i > 0)
        def _():
            def sgu_chunk(ch):
                rows = slice(HD * ch, HD * ch + HD)
                du, dv = _sgu_bwd_chunk(u_ref[rows, :], v_ref[rows, :], dy_ref[rows, :], g_ref[...], b_ref[...],
                                        sw_ref, bt_ref, mixed_s, dvn_s, dws_ref, dbs_ref, dlg_ref, dlb_ref)
                du_ref[rows, :] = du
                dv_ref[rows, :] = dv

            dhn = _dot_nt(dxa_ref[...], w0[...])
            sgu_chunk(0)
            dhn = dhn + _dot_nt(dga_ref[...], w1())
            for ch in range(1, T // HD):
                sgu_chunk(ch)
            dhn = dhn + _dot_nt(dgb_ref[...], w4())
            dhn = dhn + _dot_nt(du_ref[...], w2()) + _dot_nt(dv_ref[...], w3())
            norm_bwd(dhn, dscx_ref, dshx_ref, True)

    every = pl.BlockSpec((T, D), lambda i: (i, 0))
    lat = pl.BlockSpec((T, D), lambda i: (jnp.maximum(i - 1, 0), 0))
    vec = pl.BlockSpec((1, D), lambda i: (0, 0))
    wsp = pl.BlockSpec((NH, HD, HD), lambda i: (0, 0, 0))
    bsp = pl.BlockSpec((NH, HD), lambda i: (0, 0))
    in_specs = [every, lat, lat, pl.BlockSpec((D, D), lambda i: (0, 0)), ANY]
    in_specs += [lat, pl.BlockSpec((T, D), lambda i: (0, 0)), pl.BlockSpec((8, D), lambda i: (0, 1)), vec, lat]
    in_specs += [lat, lat, lat, vec, vec, wsp, pl.BlockSpec((HD, NH), lambda i: (0, 0))]
    vs = jax.ShapeDtypeStruct((1, D), F32)
    zb = jax.ShapeDtypeStruct((lx, D), BF16)
    return pl.pallas_call(
        body, name="proj_bwd", grid=(n,),
        out_shape=(jax.ShapeDtypeStruct((lx, D), F32), vs, vs, vs, vs, vs, zb, zb,
                   jax.ShapeDtypeStruct((NH, HD, HD), F32), jax.ShapeDtypeStruct((NH, HD), F32), vs, vs),
        in_specs=in_specs, out_specs=(lat, vec, vec, vec, vec, vec, lat, lat, wsp, bsp, vec, vec),
        scratch_shapes=[pltpu.VMEM((HD, D), F32), pltpu.VMEM((HD, D), F32), pltpu.VMEM((4, D, D), BF16),
                        pltpu.SemaphoreType.DMA((4,))],
        compiler_params=_cp(1, vmem_mb=56),
    )(dxa, dga, dgb, w_full, w_full, x, ctx, mods, norm_g, dxn, *sgu)


def _adam_math(w, g, m, v):
    m = ADAM_B1 * m + (1.0 - ADAM_B1) * g
    v = ADAM_B2 * v + (1.0 - ADAM_B2) * (g * g)
    m_hat = m / (1.0 - ADAM_B1 ** ADAM_STEP)
    v_hat = v / (1.0 - ADAM_B2 ** ADAM_STEP)
    delta = -ADAM_LR * (m_hat / (jnp.sqrt(v_hat) + ADAM_EPS) + ADAM_WD * w)
    return delta, m, v


def _adam_big(w, g, m, v, name):
    rows, cols = w.shape
    tr = 256

    def body(w_ref, g_ref, m_ref, v_ref, d_o, m_o, v_o):
        d, mm, vv = _adam_math(w_ref[...], g_ref[...], m_ref[...], v_ref[...])
        d_o[...] = d
        m_o[...] = mm
        v_o[...] = vv

    blk = pl.BlockSpec((tr, cols), lambda i: (i, 0))
    s = jax.ShapeDtypeStruct((rows, cols), F32)
    return pl.pallas_call(
        body, name=name, grid=(rows // tr,), out_shape=(s, s, s), in_specs=[blk] * 4, out_specs=(blk,) * 3,
        compiler_params=_cp(1, vmem_mb=48),
    )(w, g, m, v)


def _adam_small(items, tot):
    ni = len(items)
    pieces = [it[1] if isinstance(it[1], list) else None for it in items]
    flat = [a for it, pc in zip(items, pieces) for a in ((it[0], it[2], it[3]) if pc is not None else it)]
    n_in = len(flat) + 1
    out_shape = tuple(jax.ShapeDtypeStruct(it[0].shape, F32) for it, pc in zip(items, pieces)
                      for _ in range(4 if pc is not None else 3))
    n_out = len(out_shape)
    n_loads = sum(3 + (len(pc) if pc is not None else 1) for pc in pieces)

    def body(*refs):
        ins, tot_ref, outs = refs[:n_in - 1], refs[n_in - 1], refs[n_in:n_in + n_out]
        bufs = refs[n_in + n_out:n_in + n_out + 7 * ni]
        sem_in, sem_out = refs[n_in + n_out + 7 * ni:]
        loads, q_in, q_sem = [], 0, 0
        for k, pc in enumerate(pieces):
            w_b, g_b, m_b, v_b = bufs[7 * k:7 * k + 4]
            srcs = [(ins[q_in], w_b)]
            if pc is None:
                srcs.append((ins[q_in + 1], g_b))
                q_in += 1
            else:
                srcs += [(tot_ref.at[pl.ds(r0, nr), pl.ds(c0, nc)], g_b.at[pl.ds(d0, nr), :]) for r0, nr, c0, nc, d0 in pc]
            srcs += [(ins[q_in + 1], m_b), (ins[q_in + 2], v_b)]
            q_in += 3
            mine = []
            for src, dst in srcs:
                mine.append(pltpu.make_async_copy(src, dst, sem_in.at[q_sem]))
                q_sem += 1
            loads.append(mine)
        for mine in loads:
            for cp in mine:
                cp.start()
        stores, q_out = [], 0
        for k, pc in enumerate(pieces):
            for cp in loads[k]:
                cp.wait()
            w_b, g_b, m_b, v_b = bufs[7 * k:7 * k + 4]
            res = _adam_math(w_b[...], g_b[...], m_b[...], v_b[...])
            srcs = []
            for q in range(3):
                bufs[7 * k + 4 + q][...] = res[q]
                srcs.append(bufs[7 * k + 4 + q])
            if pc is not None:
                srcs.append(g_b)
            for src in srcs:
                cp = pltpu.make_async_copy(src, outs[q_out], sem_out.at[q_out])
                cp.start()
                stores.append(cp)
                q_out += 1
        for cp in stores:
            cp.wait()

    scratch = [pltpu.VMEM(it[0].shape, F32) for it in items for _ in range(7)]
    scratch += [pltpu.SemaphoreType.DMA((n_loads,)), pltpu.SemaphoreType.DMA((n_out,))]
    res = pl.pallas_call(
        body, name="adam_small", out_shape=out_shape, in_specs=[HBM] * n_in, out_specs=(HBM,) * n_out,
        scratch_shapes=scratch, compiler_params=_cp(vmem_mb=40),
    )(*flat, tot)
    outs, q = [], 0
    for pc in pieces:
        outs.append(tuple(res[q:q + 3]) + ((res[q + 3],) if pc is not None else (None,)))
        q += 4 if pc is not None else 3
    return outs


def kernel(x, c, ctx, c_ctx, ada_w, ada_b, norm_g, w_in, conv_w, conv_b, lru_wa, lru_ba, lru_wx, lru_bx, lru_lambda, sgu_ln_g, sgu_ln_b, sgu_w, sgu_b, w_out, final_g, loss_target, m_c_ctx, m_ada_w, m_ada_b, m_norm_g, m_w_in, m_conv_w, m_conv_b, m_lru_wa, m_lru_ba, m_lru_wx, m_lru_bx, m_lru_lambda, m_sgu_ln_g, m_sgu_ln_b, m_sgu_w, m_sgu_b, m_w_out, m_final_g, v_c_ctx, v_ada_w, v_ada_b, v_norm_g, v_w_in, v_conv_w, v_conv_b, v_lru_wa, v_lru_ba, v_lru_wx, v_lru_bx, v_lru_lambda, v_sgu_ln_g, v_sgu_ln_b, v_sgu_w, v_sgu_b, v_w_out, v_final_g):
    ix, iy, ic = lax.axis_index("x"), lax.axis_index("y"), lax.axis_index("c")
    chip = 2 * ix + iy
    dev = 2 * chip + ic
    lx = x.shape[1]
    lc = ctx.shape[1]

    smalls = jnp.concatenate([conv_w[0], lru_lambda[0], jnp.zeros((10, 256), F32)], axis=0)
    c_ctx2 = c_ctx.reshape(1, D)
    ada_b_j = lax.dynamic_slice(ada_b, (0, 768 * chip), (1, 768))
    mods, c_slots, sm_all, w_in_full, wo_land, ada_land = _gather_in(c, c_ctx2, ada_w[0], ada_b_j, w_in[0], w_out[0],
                                                                     smalls)
    wo_ss, wo_rs, ada_ss, ada_rs, wo_land, ada_land, token = _late_gather_start(wo_land, ada_land)
    mods = mods + token[0:1, 0:1]
    sm3 = sm_all.reshape(NCHIP, 16, 256)
    conv_w_full = sm3[:, 0:4, :].transpose(1, 0, 2).reshape(4, D)
    lam_full = sm3[:, 4:6, :].transpose(1, 0, 2).reshape(2, D)
    conv_wz = conv_w_full.reshape(4, NH, HD)
    conv_bz = conv_b.reshape(NH, HD)
    lamcat = lam_full.reshape(2, NH, HD).transpose(1, 0, 2).reshape(NH, 2 * HD)
    wa, wx, ba, bx = lru_wa[0], lru_wx[0], lru_ba[0], lru_bx[0]
    wcat = jnp.concatenate([wa[0], wx[0], wa[1], wx[1]], axis=-1).astype(BF16)
    bcat = jnp.concatenate([ba[0], bx[0], ba[1], bx[1]], axis=-1)
    sgu_wb = sgu_w[0].astype(BF16)
    sgu_bt = sgu_b[0].T
    final_g2 = final_g.reshape(1, D)

    zero_s = jnp.zeros((NH, HD), F32)
    hn, xa_all, ga, u, v, gb, ys = _proj(x[0], ctx[0], mods, norm_g, w_in_full, (sgu_ln_g, sgu_ln_b, sgu_wb, sgu_bt))
    xcz, af, ab, hf, hb, gf, gb_l, _, hb0 = _lru_fwd(xa_all, conv_wz, conv_bz, wcat, bcat, lamcat, "lru_fwd")

    w_out_full = _late_gather_wait(wo_land, wo_ss, wo_rs, "w_out", hf, "late_gather_wait_w_out")
    (loss_part, dfg, dgx, dxn, y, do, dga, dgb, dyl_z, dys) = _out_fwd_bwd(
        hf, hb, ga, gb, ys, x[0], loss_target[0], mods, final_g2, w_out_full)

    dxc_b, dxc_f, dwc, dbc, dlc = _lru_bwd(xcz, dyl_z, hf, hb, af, ab, gf, gb_l, hb0, wcat, lamcat, "lru_bwd")
    dxa, dcw, dcb = _conv_bwd(dxc_b, dxc_f, xa_all, conv_wz, jnp.zeros((4, NH, HD), F32), zero_s, "conv_bwd")

    grad_x, dng, dsc_x, dsh_x, dsc_c, dsh_c, du, dv, d_sgu_w, d_sgu_b, d_ln_g, d_ln_b = _proj_bwd(
        dxa, dga, dgb, x[0], ctx[0], dxn, mods, norm_g, w_in_full, (u, v, dys, sgu_ln_g, sgu_ln_b, sgu_wb, sgu_bt))
    dzs = [dxa, dga, du, dv, dgb]

    dmx = jnp.concatenate([dsh_x, dsc_x, dgx], axis=0)
    dmc = jnp.concatenate([dsh_c, dsc_c, jnp.zeros((1, D), F32)], axis=0)
    slot = jnp.concatenate([dmx, loss_part], axis=0)
    slots = lax.dynamic_update_slice(jnp.zeros((32, D), F32), slot, (4 * dev, 0))
    vecs = jnp.concatenate([dfg, dng, dcb.reshape(1, D), d_ln_g, d_ln_b, dcw.reshape(4, D), dmc,
                            jnp.zeros((4, D), F32), slots], axis=0)
    d_sgu_w4 = d_sgu_w.reshape(4, 256, HD).transpose(1, 0, 2).reshape(256, 4 * HD)
    pad8 = lambda a: jnp.pad(a, ((0, 8 - a.shape[0]), (0, 4 * HD - a.shape[1])))
    pack = jnp.concatenate([dwc.reshape(NH * HD, 4 * HD), pad8(dbc), pad8(dlc), d_sgu_w4, pad8(d_sgu_b),
                            vecs.reshape(96, 4 * HD), jnp.zeros((8, 4 * HD), F32)], axis=0)
    g_w_in, g_w_out, tot = _grads_reduce(hn, dzs, lc, y, do, pack)

    n_w = NH * HD
    g_lru_wa = [(0, n_w, 2 * HD * d, HD, n_w * d) for d in range(2)]
    g_lru_wx = [(0, n_w, 2 * HD * d + HD, HD, n_w * d) for d in range(2)]
    g_lru_ba = [(n_w, NH, 2 * HD * d, HD, NH * d) for d in range(2)]
    g_lru_bx = [(n_w, NH, 2 * HD * d + HD, HD, NH * d) for d in range(2)]
    g_sgu_w = [(1040, 256, HD * q, HD, 256 * q) for q in range(4)]
    g_sgu_b = [(1296, NH, 0, HD, 0)]
    g_lc = tot[1032:1040, 0:2 * HD]
    tv = tot[1304:1400].reshape(48, D)
    g_final_g, g_norm_g, g_conv_b, g_ln_g, g_ln_b = tv[0:1], tv[1:2], tv[2:3], tv[3:4], tv[4:5]
    g_conv_w_full = tv[5:9]
    dmc_tot = tv[9:12].reshape(1, 3 * D)
    slots_all = tv[16:48].reshape(8, 4, D)
    dmx_all = slots_all[:, 0:3, :].reshape(8, 3 * D)
    c_all = c_slots.reshape(8, 8, D)[:, 0, :]
    g_lam_full = jnp.stack([g_lc[:, 0:HD], g_lc[:, HD:2 * HD]]).reshape(2, D)
    g_conv_w = lax.dynamic_slice(g_conv_w_full, (0, 256 * chip), (4, 256))
    g_lam = lax.dynamic_slice(g_lam_full, (0, 256 * chip), (2, 256))
    dmx_all_j = lax.dynamic_slice(dmx_all, (0, 768 * chip), (8, 768))
    dmc_j = lax.dynamic_slice(dmc_tot, (0, 768 * chip), (1, 768))
    ada_full = _late_gather_wait(ada_land, ada_ss, ada_rs, "ada_w", tot, "late_gather_wait_ada_w")
    g_ada_w, g_ada_b, g_c_ctx = _ada_bwd(c_all, dmx_all_j, dmc_j, dmx_all, dmc_tot, c_ctx2, ada_full)

    big = {
        "ada_w": _adam_big(ada_w[0], g_ada_w, m_ada_w[0], v_ada_w[0], "adam_ada_w"),
        "w_in": _adam_big(w_in[0], g_w_in, m_w_in[0], v_w_in[0], "adam_w_in"),
        "w_out": _adam_big(w_out[0], g_w_out, m_w_out[0], v_w_out[0], "adam_w_out"),
    }
    small_in = {
        "c_ctx": (c_ctx, g_c_ctx, m_c_ctx, v_c_ctx, (1, D)),
        "ada_b": (ada_b, g_ada_b, m_ada_b, v_ada_b, (1, 3 * D)),
        "norm_g": (norm_g, g_norm_g, m_norm_g, v_norm_g, (1, D)),
        "conv_w": (conv_w, g_conv_w, m_conv_w, v_conv_w, (4, 256)),
        "conv_b": (conv_b, g_conv_b, m_conv_b, v_conv_b, (1, D)),
        "lru_wa": (lru_wa, g_lru_wa, m_lru_wa, v_lru_wa, (2 * NH * HD, HD)),
        "lru_ba": (lru_ba, g_lru_ba, m_lru_ba, v_lru_ba, (2 * NH, HD)),
        "lru_wx": (lru_wx, g_lru_wx, m_lru_wx, v_lru_wx, (2 * NH * HD, HD)),
        "lru_bx": (lru_bx, g_lru_bx, m_lru_bx, v_lru_bx, (2 * NH, HD)),
        "lru_lambda": (lru_lambda, g_lam, m_lru_lambda, v_lru_lambda, (2, 256)),
        "sgu_ln_g": (sgu_ln_g, g_ln_g, m_sgu_ln_g, v_sgu_ln_g, (1, D)),
        "sgu_ln_b": (sgu_ln_b, g_ln_b, m_sgu_ln_b, v_sgu_ln_b, (1, D)),
        "sgu_w": (sgu_w, g_sgu_w, m_sgu_w, v_sgu_w, (NH * HD, HD)),
        "sgu_b": (sgu_b, g_sgu_b, m_sgu_b, v_sgu_b, (NH, HD)),
        "final_g": (final_g, g_final_g, m_final_g, v_final_g, (1, D)),
    }
    names_small = list(small_in)
    res_small = _adam_small([tuple(a if isinstance(a, list) else a.reshape(small_in[k][4]) for a in small_in[k][:4])
                             for k in names_small], tot)
    full_shapes = {"ada_w": ada_w.shape, "w_in": w_in.shape, "w_out": w_out.shape}
    grads, deltas, new_m, new_v = {}, {}, {}, {}
    for k in ("ada_w", "w_in", "w_out"):
        g = {"ada_w": g_ada_w, "w_in": g_w_in, "w_out": g_w_out}[k]
        grads[k] = g.reshape(full_shapes[k])
        deltas[k], new_m[k], new_v[k] = (a.reshape(full_shapes[k]) for a in big[k])
    for k, res in zip(names_small, res_small):
        shape = small_in[k][0].shape
        grads[k] = (small_in[k][1] if res[3] is None else res[3]).reshape(shape)
        deltas[k], new_m[k], new_v[k] = (a.reshape(shape) for a in res[:3])

    loss = jnp.sum(slots_all[:, 3, 0:3])
    order = ["c_ctx", "ada_w", "ada_b", "norm_g", "w_in", "conv_w", "conv_b", "lru_wa", "lru_ba", "lru_wx", "lru_bx",
             "lru_lambda", "sgu_ln_g", "sgu_ln_b", "sgu_w", "sgu_b", "w_out", "final_g"]
    return (loss, grad_x.reshape(x.shape), *[grads[k] for k in order], *[deltas[k] for k in order],
            *[new_m[k] for k in order], *[new_v[k] for k in order])
```

```python
import jax
import jax.numpy as jnp
from jax import lax
from jax.experimental import pallas as pl
from jax.experimental.pallas import tpu as pltpu

F32 = jnp.float32
BF16 = jnp.bfloat16

D = 1024
NH = 8
HD = 128
NCHIP = 4
T = 256
NORM_EPS = 1e-6
LN_EPS = 1e-5
LRU_C = 8.0
ADAM_LR = 0.001
ADAM_B1 = 0.9
ADAM_B2 = 0.999
ADAM_EPS = 1e-08
ADAM_WD = 0.01
ADAM_STEP = 10

VMEM = pl.BlockSpec(memory_space=pltpu.VMEM)
ANY = pl.BlockSpec(memory_space=pl.ANY)
MESH = pl.DeviceIdType.MESH


def _cp(n_grid=0, vmem_mb=None):
    kw = {}
    if n_grid:
        kw["dimension_semantics"] = ("arbitrary",) * n_grid
    if vmem_mb:
        kw["vmem_limit_bytes"] = vmem_mb << 20
    return pltpu.CompilerParams(**kw)


def _sigmoid(x):
    return 0.5 * jnp.tanh(0.5 * x) + 0.5


def _silu_and_grad(x):
    s = _sigmoid(x)
    return x * s, s * (1.0 + x * (1.0 - s))


_GELU_K = 0.7978845608028654
_GELU_C = 0.044715


def _gelu_and_grad(x):
    x2 = x * x
    th = jnp.tanh(x * (_GELU_K + (_GELU_K * _GELU_C) * x2))
    p = 0.5 + 0.5 * th
    g = x * p
    dg = p + g * (1.0 - th) * (_GELU_K + (3.0 * _GELU_K * _GELU_C) * x2)
    return g, dg


def _softplus(x):
    return jnp.maximum(x, 0.0) + jnp.log1p(jnp.exp(-jnp.abs(x)))


def _lru_gate(pre, lam_row, d, off=None):
    off = 256 * d if off is None else off
    r = _sigmoid(pre[:, off:off + HD])
    gi = _sigmoid(pre[:, off + HD:off + 2 * HD])
    lam = lam_row[:, HD * d:HD * d + HD]
    sp = _softplus(-lam)
    la = (-LRU_C) * r * sp
    a = jnp.exp(la)
    x2 = 2.0 * la
    m2 = jnp.where(x2 > -1e-3, -x2 * (1.0 + 0.5 * x2), 1.0 - a * a)
    mult = jnp.sqrt(m2)
    return r, gi, lam, sp, a, mult


def _dot(a, b):
    return jnp.dot(a, b, preferred_element_type=F32)


def _dot_tn(a, b):
    return lax.dot_general(a, b, (((0,), (0,)), ((), ())), preferred_element_type=F32)


def _dot_nt(a, b):
    return lax.dot_general(a, b, (((1,), (1,)), ((), ())), preferred_element_type=F32)


def _mo(v, m):
    return v if isinstance(v, int) else pl.multiple_of(v, m)


def _zrows(h, n):
    return pl.ds(h, n, stride=NH)


def _gather_in(c, c_ctx, ada_w, ada_b_j, w_in, w_out, smalls):
    nch = [1, 4]
    wrows = lambda cc, q: (pl.ds(_mo(512 * cc, 16), 512) if q is None
                           else pl.ds(_mo(512 * cc + (512 // nch[1]) * q, 16), 512 // nch[1]))
    specs = [
        ((64, 256), F32, lambda r, jj, cc, q=None: r.at[pl.ds(_mo(16 * jj + 8 * cc, 8), 8), :]),
        ((D, 5120), BF16, lambda r, jj, cc, q=None: r.at[wrows(cc, q), pl.ds(_mo(1280 * jj, 128), 1280)]),
    ]
    halves = [lambda r, cc, q=None: r.at[pl.ds(_mo(8 * cc, 8), 8), :],
              lambda r, cc, q=None: r.at[wrows(cc, q), :]]
    na = len(specs)
    sem_base = [0, 6 * nch[0]]
    sidx = lambda a, q, k: sem_base[a] + 6 * q + k
    n_tiny = 6 * sum(nch)
    n_sem = n_tiny + 10

    def body(c_ref, cc_ref, ada_ref, adab_ref, win_ref, wout_ref, sm_ref,
             mods_o, call_o, sm_o, win_o, wol_o, adal_o, s_win, s_ada, s_wout, f_win, f_ada, f_wout, cslot, lhs, mbuf,
             send_sems, recv_sems, local_sems, load_sems):
        x, y, c = lax.axis_index("x"), lax.axis_index("y"), lax.axis_index("c")
        j = 2 * x + y
        dev = 2 * j + c
        sib = (x, y, 1 - c)
        chips = [(1 - x, y), (x, 1 - y), (1 - x, 1 - y)]
        cj = [2 * cx + cy for cx, cy in chips]
        outs = [sm_o, win_o]
        srcs = [sm_ref, s_win]

        def copy(idx, src, dst, to):
            return pltpu.make_async_remote_copy(src_ref=src, dst_ref=dst, send_sem=send_sems.at[idx],
                                                recv_sem=recv_sems.at[idx], device_id=to, device_id_type=MESH)

        sends = []

        def start(cp):
            cp.start()
            sends.append(cp)

        cslot[...] = jnp.zeros_like(cslot)
        cslot[0:1, :] = c_ref[...]
        my_slot = pl.ds(_mo(8 * dev, 8), 8)
        others = [sib] + [(*chips[k], c) for k in range(3)] + [(*chips[k], 1 - c) for k in range(3)]
        other_dev = [dev + 1 - 2 * c] + [2 * cj[k] + c for k in range(3)] + [2 * cj[k] + 1 - c for k in range(3)]
        base = n_tiny
        for r in range(7):
            start(copy(base + r, cslot, call_o.at[my_slot, :], others[r]))
        call_o[my_slot, :] = cslot[...]

        crow = 512 // nch[1]
        loads = []
        for cc in (c, 1 - c):
            for q in range(nch[1]):
                rows = pl.ds(_mo(512 * cc + crow * q, 16), crow)
                loads.append(pltpu.make_async_copy(win_ref.at[rows, :], f_win.at[rows, :], load_sems.at[len(loads)]))
        loads.append(pltpu.make_async_copy(ada_ref, f_ada, load_sems.at[len(loads)]))
        loads.append(pltpu.make_async_copy(wout_ref, f_wout, load_sems.at[len(loads)]))
        for ld in loads:
            ld.start()
        for k in range(2):
            start(copy(sidx(0, 0, k), halves[0](srcs[0], c), specs[0][2](outs[0], j, c), (*chips[k], c)))
        for q in range(nch[1]):
            loads[q].wait()
            rows = pl.ds(_mo(512 * c + crow * q, 16), crow)
            s_win[rows, :] = f_win[rows, :].astype(BF16)
            for k in range(2):
                start(copy(sidx(1, q, k), halves[1](s_win, c, q), specs[1][2](win_o, j, c, q), (*chips[k], c)))
        for q in range(nch[1]):
            loads[nch[1] + q].wait()
            rows = pl.ds(_mo(512 * (1 - c) + crow * q, 16), crow)
            s_win[rows, :] = f_win[rows, :].astype(BF16)
        local = []
        for a in range(na):
            for cc in range(2):
                lc = pltpu.make_async_copy(halves[a](srcs[a], cc), specs[a][2](outs[a], j, cc), local_sems.at[2 * a + cc])
                lc.start()
                local.append(lc)
        loads[2 * nch[1]].wait()
        s_ada[...] = f_ada[...].astype(BF16)
        loads[2 * nch[1] + 1].wait()
        s_wout[...] = f_wout[...].astype(BF16)
        for q, (src, dst) in enumerate([(s_wout, wol_o.at[pl.ds(_mo(512 * j, 16), 512), :]),
                                        (s_ada, adal_o.at[:, pl.ds(_mo(768 * j, 128), 768)])]):
            lc = pltpu.make_async_copy(src, dst, local_sems.at[2 * na + q])
            lc.start()
            local.append(lc)

        for r in range(7):
            slot = call_o.at[pl.ds(_mo(8 * other_dev[r], 8), 8), :]
            copy(base + r, slot, slot, sib).wait_recv()
        lhs[...] = jnp.zeros_like(lhs)
        for b in range(8):
            cv = call_o[8 * b:8 * b + 1, :]
            lhs[b:b + 1, :] = cv * _sigmoid(cv)
        cv = cc_ref[...]
        lhs[8:9, :] = cv * _sigmoid(cv)
        mbuf[j] = _dot(lhs[...].astype(BF16), s_ada[...]) + adab_ref[...]
        for k in range(3):
            start(copy(base + 7 + k, mbuf.at[j], mbuf.at[j], (*chips[k], c)))
        for k in range(3):
            copy(base + 7 + k, mbuf.at[cj[k]], mbuf.at[cj[k]], sib).wait_recv()
        mods_o[...] = jnp.zeros_like(mods_o)
        for jj in range(NCHIP):
            mods_o[0:1, 768 * jj:768 * jj + 768] = mbuf[jj, pl.ds(dev, 1), :]
            mods_o[1:2, 768 * jj:768 * jj + 768] = mbuf[jj, 8:9, :]

        kx = [1 - x, x, 1 - x]
        ky = [y, 1 - y, 1 - y]
        pick = lambda k, lst: jnp.where(k == 0, lst[0], jnp.where(k == 1, lst[1], lst[2]))
        for a in range(na):
            for q in range(nch[a]):
                for step, k in enumerate([c, 1 - c]):
                    reg = specs[a][2](outs[a], pick(k, cj), c, q)
                    copy(sidx(a, q, k), reg, reg, sib).wait_recv()
                    if step == 0:
                        start(copy(sidx(a, q, 2), reg, reg, (pick(1 - c, kx), pick(1 - c, ky), c)))
                    start(copy(sidx(a, q, 3 + k), reg, reg, sib))
        for a in range(na):
            for q in range(nch[a]):
                reg = specs[a][2](outs[a], cj[2], c, q)
                copy(sidx(a, q, 2), reg, reg, sib).wait_recv()
                start(copy(sidx(a, q, 5), reg, reg, sib))
        for a in range(na):
            for q in range(nch[a]):
                for k in range(3):
                    reg = specs[a][2](outs[a], cj[k], 1 - c, q)
                    copy(sidx(a, q, 3 + k), reg, reg, sib).wait_recv()
        for cp in sends:
            cp.wait_send()
        for lc in local:
            lc.wait()

    out_shape = (jax.ShapeDtypeStruct((8, 3 * D), F32), jax.ShapeDtypeStruct((64, D), F32),
                 jax.ShapeDtypeStruct(specs[0][0], F32), jax.ShapeDtypeStruct(specs[1][0], BF16),
                 jax.ShapeDtypeStruct((2048, D), BF16), jax.ShapeDtypeStruct((D, 3 * D), BF16))
    return pl.pallas_call(
        body, name="gather_in", out_shape=out_shape,
        in_specs=[VMEM, VMEM, ANY, VMEM, ANY, ANY, VMEM], out_specs=(VMEM, VMEM, VMEM, ANY, ANY, ANY),
        scratch_shapes=[pltpu.VMEM((D, 1280), BF16), pltpu.VMEM((D, 768), BF16), pltpu.VMEM((512, D), BF16),
                        pltpu.VMEM((D, 1280), F32), pltpu.VMEM((D, 768), F32), pltpu.VMEM((512, D), F32),
                        pltpu.VMEM((8, D), F32), pltpu.VMEM((16, D), F32), pltpu.VMEM((NCHIP, 16, 768), F32),
                        pltpu.SemaphoreType.DMA((n_sem,)), pltpu.SemaphoreType.DMA((n_sem,)),
                        pltpu.SemaphoreType.DMA((2 * na + 2,)), pltpu.SemaphoreType.DMA((2 * nch[1] + 2,))],
        compiler_params=_cp(vmem_mb=56),
    )(c, c_ctx, ada_w, ada_b_j, w_in, w_out, smalls)


HBM = pl.BlockSpec(memory_space=pltpu.HBM)
SEM = pl.BlockSpec(memory_space=pltpu.SEMAPHORE)


def _late_gather_regions(x, y, c):
    chips = [(1 - x, y), (x, 1 - y), (1 - x, 1 - y)]
    wo_reg = lambda r, jj, cc: r.at[pl.ds(_mo(512 * jj + 256 * cc, 16), 256), :]
    ada_reg = lambda r, jj, cc: r.at[pl.ds(_mo(512 * cc, 16), 512), pl.ds(_mo(768 * jj, 128), 768)]
    return chips, wo_reg, ada_reg


def _late_gather_start(wo_land, ada_land):
    def body(wol_ref, adal_ref, wo_ss, wo_rs, ada_ss, ada_rs, wol_thru, adal_thru, token):
        x, y, c = lax.axis_index("x"), lax.axis_index("y"), lax.axis_index("c")
        j = 2 * x + y
        chips, wo_reg, ada_reg = _late_gather_regions(x, y, c)
        for k in range(3):
            for cc in range(2):
                pltpu.make_async_remote_copy(src_ref=wo_reg(wol_ref, j, c), dst_ref=wo_reg(wol_ref, j, c),
                                             send_sem=wo_ss.at[2 * k + cc], recv_sem=wo_rs.at[2 * k + c],
                                             device_id=(*chips[k], cc), device_id_type=MESH).start()
        for k in range(3):
            for cc in range(2):
                pltpu.make_async_remote_copy(src_ref=ada_reg(adal_ref, j, c), dst_ref=ada_reg(adal_ref, j, c),
                                             send_sem=ada_ss.at[2 * k + cc], recv_sem=ada_rs.at[2 * k + c],
                                             device_id=(*chips[k], cc), device_id_type=MESH).start()
        token[...] = jnp.zeros_like(token)

    sems = pltpu.SemaphoreType.DMA((6,))
    return pl.pallas_call(
        body, name="late_gather_start",
        out_shape=(sems, sems, sems, sems, pltpu.HBM(wo_land.shape, BF16), pltpu.HBM(ada_land.shape, BF16),
                   jax.ShapeDtypeStruct((8, 128), F32)),
        in_specs=(HBM, HBM), out_specs=(SEM, SEM, SEM, SEM, HBM, HBM, VMEM), input_output_aliases={0: 4, 1: 5},
        compiler_params=pltpu.CompilerParams(has_side_effects=pltpu.SideEffectType.DATAFLOW_SIDE_EFFECTING),
    )(pltpu.with_memory_space_constraint(wo_land, pltpu.HBM), pltpu.with_memory_space_constraint(ada_land, pltpu.HBM))


def _late_gather_wait(land, send_sems, recv_sems, which, after, name):
    def body(land_ref, ss, rs, after_ref, land_out):
        x, y, c = lax.axis_index("x"), lax.axis_index("y"), lax.axis_index("c")
        j = 2 * x + y
        chips, wo_reg, ada_reg = _late_gather_regions(x, y, c)
        reg = wo_reg if which == "w_out" else ada_reg
        for k in range(3):
            kj = 2 * chips[k][0] + chips[k][1]
            for cc in range(2):
                cp = pltpu.make_async_remote_copy(src_ref=reg(land_ref, j, c), dst_ref=reg(land_ref, kj, cc),
                                                  send_sem=ss.at[2 * k + cc], recv_sem=rs.at[2 * k + cc],
                                                  device_id=(*chips[k], cc), device_id_type=MESH)
                cp.wait_send()
                cp.wait_recv()

    return pl.pallas_call(
        body, name=name, out_shape=pltpu.HBM(land.shape, land.dtype),
        in_specs=(HBM, SEM, SEM, ANY), out_specs=HBM, input_output_aliases={0: 0},
        compiler_params=pltpu.CompilerParams(has_side_effects=pltpu.SideEffectType.DATAFLOW_SIDE_EFFECTING),
    )(land, send_sems, recv_sems, after)


RCHUNK = 16


def _grads_reduce(hn, dzs, lc, y, do, pack):
    rp = pack.shape[0]
    hp = rp // 2
    assert hp % RCHUNK == 0
    wi_w = 1280
    lx = hn.shape[0] - lc
    lt = lx + lc
    n_dz = len(dzs)

    def body(*refs):
        hn_hbm, dz_hbm = refs[0], refs[1:1 + n_dz]
        y_hbm, do_hbm, pk_hbm, wi_out, wo_out, pk_out = refs[1 + n_dz:7 + n_dz]
        (hn_mine, hn_other, dzbuf, wi_other, wi_mine, wi_recv, wi_send, wi_rb,
         y_blk, do_mine, do_other, wo_other, wo_mine, wo_recv, wo_send, wo_rb,
         pk_mine, pk_recv, pk_send, pk_rb, pk_own, send_sems, recv_sems, local_sems) = refs[7 + n_dz:]
        x, y, c = lax.axis_index("x"), lax.axis_index("y"), lax.axis_index("c")
        j = 2 * x + y
        sib = (x, y, 1 - c)
        chips = [(1 - x, y), (x, 1 - y), (1 - x, 1 - y)]
        cj = [2 * cx + cy for cx, cy in chips]
        near = (jnp.where(c == 0, 1 - x, x), jnp.where(c == 0, y, 1 - y), c)
        slabs = [cj[2], cj[0], cj[1], j]

        def copy(k, src, dst, to):
            return pltpu.make_async_remote_copy(src_ref=src, dst_ref=dst, send_sem=send_sems.at[k],
                                                recv_sem=recv_sems.at[k], device_id=to, device_id_type=MESH)

        def local(k, src, dst):
            cp = pltpu.make_async_copy(src, dst, local_sems.at[k])
            cp.start()
            return cp

        rows_half = lambda r, cc, n: r.at[pl.ds(_mo(cc * n, 16), n), :]
        cols_half = lambda r, cc, n: r.at[:, pl.ds(_mo(cc * n, 128), n)]
        pk_piece = lambda r, cc, jj: r.at[pl.ds(_mo(cc * hp, 16), hp), pl.ds(_mo(jj * 128, 128), 128)]

        sends = []

        def start(cp):
            cp.start()
            sends.append(cp)

        def dz_pieces(s):
            g0 = wi_w * s
            k0, off0 = g0 // D, g0 % D
            w0 = min(D - off0, wi_w)
            pieces = [(k0, off0, w0, 0)]
            if w0 < wi_w:
                pieces.append((k0 + 1, 0, wi_w - w0, w0))
            return pieces

        def dz_copies(s):
            cps = []
            for q, (k, off, w, dst) in enumerate(dz_pieces(s)):
                cps.append(pltpu.make_async_copy(dz_hbm[k].at[pl.ds(lc if k == 0 else 0, lx), pl.ds(off, w)],
                                                 dzbuf.at[pl.ds(lc, lx), pl.ds(dst, w)], local_sems.at[11 + q]))
            if s == 0:
                cps.append(pltpu.make_async_copy(dz_hbm[0].at[pl.ds(0, lc), :], dzbuf.at[pl.ds(0, lc), pl.ds(0, D)],
                                                 local_sems.at[13]))
            return cps

        def dz_load(sl):
            for s in range(NCHIP):
                @pl.when(sl == s)
                def _():
                    if s == 0:
                        dzbuf[pl.ds(0, lc), pl.ds(D, wi_w - D)] = jnp.zeros((lc, wi_w - D), BF16)
                    else:
                        dzbuf[pl.ds(0, lc), :] = jnp.zeros((lc, wi_w), BF16)
                    for cp in dz_copies(s):
                        cp.start()

        def dz_wait(sl):
            for s in range(NCHIP):
                @pl.when(sl == s)
                def _():
                    for cp in dz_copies(s):
                        cp.wait()

        l_pk = local(0, rows_half(pk_hbm, c, hp), pk_mine)
        start(copy(0, rows_half(pk_hbm, 1 - c, hp), pk_recv, sib))
        col = lambda r, cc: r.at[:, pl.ds(_mo(cc * 512, 128), 512)]
        do_loads = [local(7, col(do_hbm, c), do_mine), local(14, col(do_hbm, 1 - c), do_other)]
        hn_loads = [local(2, col(hn_hbm, c), hn_mine), local(4, col(hn_hbm, 1 - c), hn_other)]
        y_copy = lambda s: pltpu.make_async_copy(col(y_hbm, slabs[s]), y_blk, local_sems.at[1])
        y_copy(0).start()
        dz_load(slabs[0])

        def pair_sum(mine, recv, send, nrows, keep, relayed=None):
            def step(i, carry):
                rows = pl.ds(_mo(i * RCHUNK, RCHUNK), RCHUNK)
                s = mine[rows, :] + recv[rows, :].astype(F32)
                if relayed is not None:
                    s = s + relayed[rows, :].astype(F32)
                if keep:
                    mine[rows, :] = s
                if send is not None:
                    send[rows, :] = s.astype(BF16)
                return carry
            lax.fori_loop(0, nrows // RCHUNK, step, 0)

        def chip_sum(own, rb, nrows, terms=(0, 1, 2)):
            def step(i, carry):
                rows = pl.ds(_mo(i * RCHUNK, RCHUNK), RCHUNK)
                acc = own[rows, :]
                for q in terms:
                    acc = acc + rb[q, rows, :].astype(F32)
                own[rows, :] = acc
                return carry
            lax.fori_loop(0, nrows // RCHUNK, step, 0)

        w_in_g = dict(other=wi_other, mine=wi_mine, recv=wi_recv, send=wi_send, rb=wi_rb, p1_sems=(2, 3, 4, 5), p2_sem=12,
                      p1=[None] * NCHIP, wait_load=lambda s: dz_wait(slabs[s]), load=lambda s: dz_load(slabs[s]),
                      dot_other=lambda: _dot_tn(hn_other[...], dzbuf[...]), dot_mine=lambda: _dot_tn(hn_mine[...], dzbuf[...]))
        w_out_g = dict(other=wo_other, mine=wo_mine, recv=wo_recv, send=wo_send, rb=wo_rb, p1_sems=(1, 24, 25, 26), p2_sem=9,
                       p1=[None] * NCHIP, wait_load=lambda s: y_copy(s).wait(), load=lambda s: y_copy(s).start(),
                       dot_other=lambda: _dot_tn(y_blk[...], do_other[...]), dot_mine=lambda: _dot_tn(y_blk[...], do_mine[...]))

        def piece_matmuls(g, s):
            if s >= 2:
                g["p1"][s - 2].wait_send()
            g["wait_load"](s)
            g["other"][s % 2] = g["dot_other"]().astype(BF16)
            g["p1"][s] = copy(g["p1_sems"][s], g["other"].at[s % 2], g["recv"].at[s], sib)
            g["p1"][s].start()
            g["mine"][s % 2] = g["dot_mine"]()
            if s + 1 < NCHIP:
                g["load"](s + 1)

        def piece_finish(g, s):
            mine, recv, send, rb, p2 = g["mine"].at[s % 2], g["recv"].at[s], g["send"], g["rb"], g["p2_sem"]
            nrows = mine.shape[0]
            copy(g["p1_sems"][s], recv, recv, sib).wait_recv()
            if s == 3:
                pair_sum(mine, recv, None, nrows, True)
                return
            if s == 0:
                pair_sum(mine, recv, send.at[0], nrows, False)
                start(copy(p2, send.at[0], rb.at[0], near))
                return
            adds_relayed = c == (1 if s == 1 else 0)

            @pl.when(adds_relayed)
            def _():
                copy(p2, rb.at[0], rb.at[0], sib).wait_recv()
                pair_sum(mine, recv, send.at[s], nrows, False, rb.at[0])

            @pl.when(jnp.logical_not(adds_relayed))
            def _():
                pair_sum(mine, recv, send.at[s], nrows, False)
            start(copy(p2 + s, send.at[s], rb.at[s], (*chips[s - 1], c)))

        def piece_total(g):
            for k in (1, 2):
                copy(g["p2_sem"] + k, g["rb"].at[k], g["rb"].at[k], sib).wait_recv()
            chip_sum(g["mine"].at[1], g["rb"], g["mine"].shape[1], (1, 2))

        for cp in do_loads:
            cp.wait()
        piece_matmuls(w_out_g, 0)
        piece_matmuls(w_out_g, 1)
        piece_finish(w_out_g, 0)
        piece_matmuls(w_out_g, 2)
        piece_finish(w_out_g, 1)
        piece_matmuls(w_out_g, 3)
        piece_finish(w_out_g, 2)

        for cp in hn_loads:
            cp.wait()
        piece_matmuls(w_in_g, 0)

        l_pk.wait()
        copy(0, pk_recv, pk_recv, sib).wait_recv()
        pair_sum(pk_mine, pk_recv, pk_send, hp, True)
        for k in range(3):
            start(copy(6 + k, pk_send.at[:, pl.ds(_mo(cj[k] * 128, 128), 128)], pk_rb.at[k], (*chips[k], c)))
        l_pk_own = local(6, pk_mine.at[:, pl.ds(_mo(j * 128, 128), 128)], pk_own)

        piece_matmuls(w_in_g, 1)
        piece_finish(w_in_g, 0)

        l_pk_own.wait()
        for k in range(3):
            copy(6 + k, pk_rb.at[k], pk_rb.at[k], sib).wait_recv()
        chip_sum(pk_own, pk_rb, hp)
        l_pk_out = local(8, pk_own, pk_piece(pk_out, c, j))
        start(copy(15, pk_own, pk_piece(pk_out, c, j), sib))
        for k in range(2):
            start(copy(16 + k, pk_own, pk_piece(pk_out, c, j), (*chips[k], c)))

        piece_matmuls(w_in_g, 2)
        piece_finish(w_in_g, 1)
        piece_matmuls(w_in_g, 3)
        piece_finish(w_in_g, 2)

        piece_finish(w_out_g, 3)
        piece_total(w_out_g)
        l_wo_out = local(9, wo_mine.at[1], cols_half(wo_out, c, 512))
        start(copy(22, wo_mine.at[1], cols_half(wo_out, c, 512), sib))

        far = (jnp.where(c == 0, x, 1 - x), jnp.where(c == 0, 1 - y, y), c)
        for step, k in enumerate([c, 1 - c, 2]):
            reg = pk_piece(pk_out, c, jnp.where(k == 0, cj[0], jnp.where(k == 1, cj[1], cj[2])))
            copy(16 + k, reg, reg, sib).wait_recv()
            if step == 0:
                start(copy(18, reg, reg, far))
            start(copy(19 + k, reg, reg, sib))

        piece_finish(w_in_g, 3)
        piece_total(w_in_g)
        l_wi_out = local(10, wi_mine.at[1], rows_half(wi_out, c, 512))
        start(copy(23, wi_mine.at[1], rows_half(wi_out, c, 512), sib))

        reg = pk_piece(pk_out, 1 - c, j)
        copy(15, reg, reg, sib).wait_recv()
        for k in range(3):
            reg = pk_piece(pk_out, 1 - c, cj[k])
            copy(19 + k, reg, reg, sib).wait_recv()
        reg = cols_half(wo_out, 1 - c, 512)
        copy(22, reg, reg, sib).wait_recv()
        reg = rows_half(wi_out, 1 - c, 512)
        copy(23, reg, reg, sib).wait_recv()
        for cp in sends + w_in_g["p1"][2:] + w_out_g["p1"][2:]:
            cp.wait_send()
        for cp in (l_pk_out, l_wo_out, l_wi_out):
            cp.wait()

    return pl.pallas_call(
        body, name="grads_reduce",
        out_shape=(jax.ShapeDtypeStruct((D, wi_w), F32), jax.ShapeDtypeStruct((512, D), F32),
                   jax.ShapeDtypeStruct(pack.shape, F32)),
        in_specs=[ANY] * (4 + n_dz), out_specs=(ANY,) * 3,
        scratch_shapes=[
            pltpu.VMEM((lt, 512), BF16), pltpu.VMEM((lt, 512), BF16), pltpu.VMEM((lt, wi_w), BF16),
            pltpu.VMEM((2, 512, wi_w), BF16), pltpu.VMEM((2, 512, wi_w), F32), pltpu.VMEM((4, 512, wi_w), BF16),
            pltpu.VMEM((3, 512, wi_w), BF16), pltpu.VMEM((3, 512, wi_w), BF16),
            pltpu.VMEM((lx, 512), BF16), pltpu.VMEM((lx, 512), BF16), pltpu.VMEM((lx, 512), BF16),
            pltpu.VMEM((2, 512, 512), BF16), pltpu.VMEM((2, 512, 512), F32), pltpu.VMEM((4, 512, 512), BF16),
            pltpu.VMEM((3, 512, 512), BF16), pltpu.VMEM((3, 512, 512), BF16),
            pltpu.VMEM((hp, 512), F32), pltpu.VMEM((hp, 512), F32), pltpu.VMEM((hp, 512), BF16),
            pltpu.VMEM((3, hp, 128), BF16), pltpu.VMEM((hp, 128), F32),
            pltpu.SemaphoreType.DMA((27,)), pltpu.SemaphoreType.DMA((27,)), pltpu.SemaphoreType.DMA((15,))],
        compiler_params=_cp(vmem_mb=56),
    )(hn, *dzs, y, do, pack)


def _ada_bwd(c_all, dmx_all_j, dmc_j, dmx_all, dmc, c_ctx, ada_w_full):
    def body(c_ref, dmxj_ref, dmcj_ref, dmx_ref, dmc_ref, cc_ref, w_ref, gw_ref, gb_ref, gc_ref, lhs, rhs, dm8):
        lhs[...] = jnp.zeros_like(lhs)
        rhs[...] = jnp.zeros_like(rhs)
        cv = c_ref[...]
        lhs[0:8, :] = cv * _sigmoid(cv)
        cc = cc_ref[...]
        a_c, da_c = _silu_and_grad(cc)
        lhs[8:9, :] = a_c
        rhs[0:8, :] = dmxj_ref[...]
        rhs[8:9, :] = dmcj_ref[...]
        gw_ref[...] = _dot_tn(lhs[...].astype(BF16), rhs[...].astype(BF16))
        gb_ref[...] = jnp.sum(dmx_ref[...], axis=0, keepdims=True) + dmc_ref[...]
        dm8[...] = jnp.zeros_like(dm8)
        dm8[0:1, :] = dmc_ref[...]
        da = _dot_nt(dm8[...].astype(BF16), w_ref[...])
        gc_ref[...] = da[0:1, :] * da_c

    return pl.pallas_call(
        body, name="ada_bwd",
        out_shape=(jax.ShapeDtypeStruct((D, 768), F32), jax.ShapeDtypeStruct((1, 3 * D), F32),
                   jax.ShapeDtypeStruct((1, D), F32)),
        in_specs=[VMEM] * 7, out_specs=(VMEM,) * 3,
        scratch_shapes=[pltpu.VMEM((16, D), F32), pltpu.VMEM((16, 768), F32), pltpu.VMEM((8, 3 * D), F32)],
        compiler_params=_cp(vmem_mb=32),
    )(c_all, dmx_all_j, dmc_j, dmx_all, dmc, c_ctx, ada_w_full)


def _late_weights(w_hbm, w_late, w_sem, order, i):
    def copy(slot, k):
        return pltpu.make_async_copy(w_hbm.at[:, pl.ds(D * k, D)], w_late.at[slot], w_sem.at[slot])

    @pl.when(i == 0)
    def _():
        for slot, k in enumerate(order):
            copy(slot, k).start()

    @pl.when(i == 1)
    def _():
        for slot, k in enumerate(order):
            copy(slot, k).wait()

    return [lambda slot=slot: w_late[slot] for slot in range(len(order))]


def _proj(x, ctx, mods, norm_g, w_full, sgu):
    lx, lc = x.shape[0], ctx.shape[0]
    assert lc == T
    n = 1 + lx // T

    def body(x_ref, c_ref, sh_ref, sc_ref, ng_ref, w0, w_hbm, g_ref, b_ref, sw_ref, bt_ref,
             hn_ref, xa_ref, ga_ref, u_ref, v_ref, gb_ref, ys_ref, mixed_s, w_late, w_sem):
        i = pl.program_id(0)
        is_ctx = i == 0
        w2, w3, w1, w4 = _late_weights(w_hbm, w_late, w_sem, (2, 3, 1, 4), i)
        xv = jnp.where(is_ctx, c_ref[...], x_ref[...])
        sc = jnp.where(is_ctx, sc_ref[1:2, :], sc_ref[0:1, :])
        sh = jnp.where(is_ctx, sh_ref[1:2, :], sh_ref[0:1, :])
        r = lax.rsqrt(jnp.mean(xv * xv, axis=-1, keepdims=True) + NORM_EPS)
        hb = ((xv * r) * ng_ref[...] * (1.0 + sc) + sh).astype(BF16)
        hn_ref[...] = hb
        xa_ref[...] = _dot(hb, w0[...])

        @pl.when(i > 0)
        def _():
            u_ref[...] = _dot(hb, w2())
            v_ref[...] = _dot(hb, w3())
            for ch in range(T // HD):
                rows = slice(HD * ch, HD * ch + HD)
                ug = _sgu_parts(u_ref[rows, :], v_ref[rows, :], g_ref[...], b_ref[...], sw_ref, bt_ref, mixed_s)[0]
                ys_ref[rows, :] = ug * mixed_s[...]
            ga_ref[...] = _dot(hb, w1())
            gb_ref[...] = _dot(hb, w4())

    every = pl.BlockSpec((T, D), lambda i: (i, 0))
    lat = pl.BlockSpec((T, D), lambda i: (jnp.maximum(i - 1, 0), 0))
    vec = pl.BlockSpec((1, D), lambda i: (0, 0))
    in_specs = [lat, pl.BlockSpec((T, D), lambda i: (0, 0)), pl.BlockSpec((8, D), lambda i: (0, 0)),
                pl.BlockSpec((8, D), lambda i: (0, 1)), vec, pl.BlockSpec((D, D), lambda i: (0, 0)), ANY]
    in_specs += [vec, vec, pl.BlockSpec((NH, HD, HD), lambda i: (0, 0, 0)), pl.BlockSpec((HD, NH), lambda i: (0, 0))]
    full_s = jax.ShapeDtypeStruct((lc + lx, D), F32)
    lat_s = jax.ShapeDtypeStruct((lx, D), F32)
    return pl.pallas_call(
        body, name="proj", grid=(n,),
        out_shape=(jax.ShapeDtypeStruct((lc + lx, D), BF16), full_s, lat_s, lat_s, lat_s, lat_s, lat_s),
        in_specs=in_specs, out_specs=(every, every, lat, lat, lat, lat, lat),
        scratch_shapes=[pltpu.VMEM((HD, D), F32), pltpu.VMEM((4, D, D), BF16), pltpu.SemaphoreType.DMA((4,))],
        compiler_params=_cp(1, vmem_mb=56),
    )(x, ctx, mods, mods, norm_g, w_full, w_full, *sgu)


def _tile_specs(rows_per_pos, width, n_tiles, tile):
    last = n_tiles * (T // 8) - 1
    r = rows_per_pos
    return [pl.BlockSpec((T * r, width), lambda i: (tile(i), 0)),
            pl.BlockSpec((8 * r, width), lambda i: (jnp.maximum(tile(i) * (T // 8) - 1, 0), 0)),
            pl.BlockSpec((8 * r, width), lambda i: (jnp.minimum((tile(i) + 1) * (T // 8), last), 0))]


def _has_prev(tile):
    return tile >= 2


def _has_next(tile, nt):
    return jnp.logical_and(tile >= 1, tile < nt - 1)


ZT = pl.BlockSpec((T * NH, HD), lambda i: (i, 0))
CONV_CHUNK = 32


SCAN_SUB = 8


def _scan_tile(chains, post, carry_ref):
    blk = T // SCAN_SUB

    def step(k, state):
        new = []
        for ci, (a_ref, x_ref, o_ref, q_ref, reverse, xscale) in enumerate(chains):
            for q in range(SCAN_SUB):
                s, p = state[ci * SCAN_SUB + q]
                t = (q + 1) * blk - 1 - k if reverse else q * blk + k
                r = pl.ds(_mo(t * NH, NH), NH)
                a = a_ref[r, :]
                x = x_ref[r, :] if xscale is None else x_ref[r, :] * xscale
                if post:
                    o = x + s
                    o_ref[r, :] = o
                    q_ref[r, :] = p
                    new.append((a * o, a * p))
                else:
                    o = a * s + x
                    p = a * p
                    o_ref[r, :] = o
                    q_ref[r, :] = p
                    new.append((o, p))
        return tuple(new)

    zero = jnp.zeros((NH, HD), F32)
    one = jnp.ones((NH, HD), F32)
    final = lax.fori_loop(0, blk, step, tuple((zero, one) for _ in range(len(chains) * SCAN_SUB)))
    for ci, (a_ref, x_ref, o_ref, q_ref, reverse, xscale) in enumerate(chains):
        carry = carry_ref[ci]
        for q in (range(SCAN_SUB - 1, -1, -1) if reverse else range(SCAN_SUB)):
            rows = pl.ds(q * blk * NH, blk * NH)
            fixed = o_ref[rows, :].reshape(blk, NH, HD) + q_ref[rows, :].reshape(blk, NH, HD) * carry[None]
            o_ref[rows, :] = fixed.reshape(blk * NH, HD)
            s_loc, p_loc = final[ci * SCAN_SUB + q]
            carry = s_loc + p_loc * carry
        carry_ref[ci] = carry


def _lru_fwd(xa, conv_wz, conv_bz, wcat, bcat, lamcat, name):
    lx = xa.shape[0]
    n = lx // T
    tile_u = lambda i: i
    tile_d = lambda i: jnp.where(i == 0, 0, n - i)

    def body(xm_u, xp_u, xn_u, xm_d, xp_d, xn_d, cw, cb, w_ref, b_ref, lam_ref,
             xcz_o, af_o, ab_o, hf_o, hb_o, gf_o, gb_o, fu, fd, pad, xc_d, x_u, x_d, q_u, q_d, carry):
        i = pl.program_id(0)

        @pl.when(i == 0)
        def _():
            carry[...] = jnp.zeros_like(carry)

        def conv_gates(xm, xp, xn, tile, d, xc_ref, a_ref, x_ref, g_ref):
            pmask = jnp.where(_has_prev(tile), 1.0, 0.0)
            nmask = jnp.where(_has_next(tile, n), 1.0, 0.0)
            for h in range(NH):
                cols = slice(HD * h, HD * h + HD)
                pad[_zrows(h, 8), :] = xp[:, cols] * pmask
                pad[pl.ds(8 * NH + h, T, stride=NH), :] = xm[:, cols]
                pad[pl.ds((T + 8) * NH + h, 8, stride=NH), :] = xn[:, cols] * nmask

            def conv_chunk(ci, c_):
                base = pl.multiple_of(ci * (CONV_CHUNK * NH), CONV_CHUNK * NH)
                acc = None
                for k in range(4):
                    sl = pad[pl.ds(base + (7 + k) * NH, CONV_CHUNK * NH), :].reshape(CONV_CHUNK, NH, HD)
                    term = sl * cw[k][None]
                    acc = term if acc is None else acc + term
                acc = acc + cb[...][None]
                xc_ref[pl.ds(base, CONV_CHUNK * NH), :] = acc.reshape(CONV_CHUNK * NH, HD)
                return c_
            lax.fori_loop(0, T // CONV_CHUNK, conv_chunk, 0)

            for h in range(NH):
                xch = xc_ref[_zrows(h, T), :]
                pre = _dot(xch.astype(BF16), w_ref[h, :, 256 * d:256 * d + 256]) + b_ref[h:h + 1, 256 * d:256 * d + 256]
                r, gi, _, _, a, mult = _lru_gate(pre, lam_ref[h:h + 1, :], d, 0)
                a_ref[_zrows(h, T), :] = a
                x_ref[_zrows(h, T), :] = mult * gi * xch
                for q, val in enumerate((r, gi, mult)):
                    g_ref[:, q * D + HD * h:q * D + HD * h + HD] = val

        conv_gates(xm_u, xp_u, xn_u, tile_u(i), 0, xcz_o, af_o, x_u, gf_o)
        conv_gates(xm_d, xp_d, xn_d, tile_d(i), 1, xc_d, ab_o, x_d, gb_o)

        _scan_tile([(af_o, x_u, hf_o, q_u, False, None), (ab_o, x_d, hb_o, q_d, True, None)], False, carry)

        @pl.when(i == 0)
        def _():
            fu[...] = carry[0]
            fd[...] = carry[1]

    full = lambda shape: pl.BlockSpec(shape, lambda i: (0,) * len(shape))
    st = full((NH, HD))
    in_specs = _tile_specs(1, D, n, tile_u) + _tile_specs(1, D, n, tile_d)
    in_specs += [full((4, NH, HD)), st, full((NH, HD, 4 * HD)), full((NH, 4 * HD)), full((NH, 2 * HD))]
    up = pl.BlockSpec((T * NH, HD), lambda i: (tile_u(i), 0))
    dn = pl.BlockSpec((T * NH, HD), lambda i: (tile_d(i), 0))
    zs = jax.ShapeDtypeStruct((lx * NH, HD), F32)
    ss = jax.ShapeDtypeStruct((NH, HD), F32)
    zbuf = pltpu.VMEM((T * NH, HD), F32)
    gs = jax.ShapeDtypeStruct((lx, 3 * D), F32)
    g_up = pl.BlockSpec((T, 3 * D), lambda i: (tile_u(i), 0))
    g_dn = pl.BlockSpec((T, 3 * D), lambda i: (tile_d(i), 0))
    return pl.pallas_call(
        body, name=name, grid=(n,), out_shape=(zs,) * 5 + (gs, gs, ss, ss), in_specs=in_specs,
        out_specs=(up, up, dn, up, dn, g_up, g_dn, st, st),
        scratch_shapes=[pltpu.VMEM(((T + 16) * NH, HD), F32), zbuf, zbuf, zbuf, zbuf, zbuf,
                        pltpu.VMEM((2, NH, HD), F32)],
        compiler_params=_cp(1, vmem_mb=48),
    )(xa, xa, xa, xa, xa, xa, conv_wz, conv_bz, wcat, bcat, lamcat)


def _sgu_parts(u, v, lng, lnb, w_ref, bt_ref, mixed_s):
    ug, dug = _gelu_and_grad(u)
    vg, dvg = _gelu_and_grad(v)
    mu = jnp.mean(vg, axis=-1, keepdims=True)
    vc = vg - mu
    rstd = lax.rsqrt(jnp.mean(vc * vc, axis=-1, keepdims=True) + LN_EPS)
    vh = vc * rstd
    vn = (vh * lng + lnb).astype(BF16)
    for g in range(NH):
        cols = slice(HD * g, HD * g + HD)
        mixed_s[:, cols] = _dot(w_ref[g], vn[:, cols]) + bt_ref[:, g:g + 1]
    return ug, dug, dvg, rstd, vh, vn


def _sgu_bwd_chunk(u, v, dys_v, lng, lnb, w_ref, bt_ref, mixed_s, dvn_s, dw_ref, db_ref, dg_ref, dbl_ref):
    ug, dug, dvg, rstd, vh, vn = _sgu_parts(u, v, lng, lnb, w_ref, bt_ref, mixed_s)
    du = (dys_v * mixed_s[...] * dug).astype(BF16)
    dmix = dys_v * ug
    ones = jnp.ones((8, HD), BF16)
    for g in range(NH):
        cols = slice(HD * g, HD * g + HD)
        dm = dmix[:, cols]
        hi = dm.astype(BF16)
        lo = (dm - hi.astype(F32)).astype(BF16)
        dw_ref[g] += _dot_nt(hi, vn[:, cols])
        db_ref[g:g + 1, :] += (_dot_nt(ones, hi) + _dot_nt(ones, lo))[0:1, :]
        dvn_s[:, cols] = _dot_tn(w_ref[g], hi)
    dvn = dvn_s[...]
    dg_ref[...] += jnp.sum(dvn * vh, axis=0, keepdims=True)
    dbl_ref[...] += jnp.sum(dvn, axis=0, keepdims=True)
    dvh = dvn * lng
    dvg_in = rstd * (dvh - jnp.mean(dvh, axis=-1, keepdims=True) - vh * jnp.mean(dvh * vh, axis=-1, keepdims=True))
    return du, (dvg_in * dvg).astype(BF16)


def _out_fwd_bwd(hf_z, hb_z, ga, gb, ys, x, tgt, mods, final_g, w_out_full):
    lx = x.shape[0]
    n = lx // T

    def body(hf_ref, hb_ref, ga_ref, gb_ref, ys_ref, x_ref, t_ref, gx_ref, fg_ref, w_ref,
             loss_ref, dfg_ref, dgx_ref, dxn_ref, y_ref, do_ref, dga_ref, dgb_ref, dyl_ref, dys_ref, yl_s):
        i = pl.program_id(0)

        @pl.when(i == 0)
        def _():
            loss_ref[...] = jnp.zeros_like(loss_ref)
            dfg_ref[...] = jnp.zeros_like(dfg_ref)
            dgx_ref[...] = jnp.zeros_like(dgx_ref)

        for h in range(NH):
            yl_s[:, HD * h:HD * h + HD] = hf_ref[_zrows(h, T), :] + hb_ref[_zrows(h, T), :]
        yl = yl_s[...]
        gav = ga_ref[...]
        gbv = gb_ref[...]
        sa, dsa = _silu_and_grad(gav)
        sb, dsb = _silu_and_grad(gbv)
        ysv = ys_ref[...]
        y_ref[:, 0:D] = (yl * sa).astype(BF16)
        y_ref[:, D:2 * D] = (ysv * sb).astype(BF16)
        o = _dot(y_ref[...], w_ref[...])
        gx = gx_ref[0:1, :]
        xnew = x_ref[...] + gx * o
        r2 = lax.rsqrt(jnp.mean(xnew * xnew, axis=-1, keepdims=True) + NORM_EPS)
        xh = xnew * r2
        fg = fg_ref[...]
        err = xh * fg - t_ref[...]
        loss_ref[...] += 0.5 * jnp.sum(jnp.mean(err * err, axis=-1, keepdims=True), axis=0, keepdims=True)

        @pl.when(i == n - 1)
        def _():
            lp = loss_ref[...]
            lp1 = lp.astype(BF16).astype(F32)
            lp2 = (lp - lp1).astype(BF16).astype(F32)
            lp3 = (lp - lp1 - lp2).astype(BF16).astype(F32)
            lane = lax.broadcasted_iota(jnp.int32, lp.shape, 1)
            loss_ref[...] = jnp.where(lane == 0, lp1, jnp.where(lane == 1, lp2, jnp.where(lane == 2, lp3, 0.0)))
        dout = err * (1.0 / D)
        dfg_ref[...] += jnp.sum(dout * xh, axis=0, keepdims=True)
        dxh = dout * fg
        dxn = r2 * (dxh - xh * jnp.mean(dxh * xh, axis=-1, keepdims=True))
        dxn_ref[...] = dxn
        dgx_ref[...] += jnp.sum(dxn * o, axis=0, keepdims=True)
        do = (dxn * gx).astype(BF16)
        do_ref[...] = do
        dy = _dot_nt(do, w_ref[...])
        dy1 = dy[:, 0:D]
        dy2 = dy[:, D:2 * D]
        dga_ref[...] = (dy1 * yl * dsa).astype(BF16)
        dgb_ref[...] = (dy2 * ysv * dsb).astype(BF16)
        dys_ref[...] = dy2 * sb
        yl_s[...] = dy1 * sa
        for h in range(NH):
            dyl_ref[_zrows(h, T), :] = yl_s[:, HD * h:HD * h + HD]

    row = pl.BlockSpec((T, D), lambda i: (i, 0))
    vec = pl.BlockSpec((1, D), lambda i: (0, 0))
    zlat = pl.BlockSpec((T * NH, HD), lambda i: (i + 1, 0))
    in_specs = [zlat, zlat, row, row, row, row, row, pl.BlockSpec((8, D), lambda i: (0, 2)), vec,
                pl.BlockSpec((2 * D, D), lambda i: (0, 0))]
    out_shape = (jax.ShapeDtypeStruct((1, D), F32), jax.ShapeDtypeStruct((1, D), F32), jax.ShapeDtypeStruct((1, D), F32),
                 jax.ShapeDtypeStruct((lx, D), F32), jax.ShapeDtypeStruct((lx, 2 * D), BF16),
                 jax.ShapeDtypeStruct((lx, D), BF16), jax.ShapeDtypeStruct((lx, D), BF16),
                 jax.ShapeDtypeStruct((lx, D), BF16), jax.ShapeDtypeStruct((lx * NH, HD), F32),
                 jax.ShapeDtypeStruct((lx, D), F32))
    out_specs = (vec, vec, vec, row, pl.BlockSpec((T, 2 * D), lambda i: (i, 0)),
                 row, row, row, ZT, row)
    return pl.pallas_call(
        body, name="out_fwd_bwd", grid=(n,), out_shape=out_shape, in_specs=in_specs, out_specs=out_specs,
        scratch_shapes=[pltpu.VMEM((T, D), F32)],
        compiler_params=_cp(1, vmem_mb=56),
    )(hf_z, hb_z, ga, gb, ys, x, tgt, mods, final_g, w_out_full)


def _lru_bwd(xc_z, dy_z, hf_z, hb_z, af_z, ab_z, gf, gb, s_b, wcat, lamcat, name):
    lx = xc_z.shape[0] // NH
    n = lx // T
    tile_u = lambda i: jnp.where(i == n - 1, 0, i + 1)
    tile_d = lambda i: n - 1 - i

    def body(xc_u, dy_u, hb_ref, hbn_ref, ab_ref, gb_ref, xc_d, dy_d, hf_ref, hfp_ref, af_ref, gf_ref,
             sb_ref, w_ref, lam_ref, dxcb_ref, dxcf_ref, dw_ref, db_ref, dl_ref,
             lb_s, lf_s, q_u, q_d, pf_s, pb_s, dpre_s, carry):
        i = pl.program_id(0)
        tu, td = tile_u(i), tile_d(i)

        @pl.when(i == 0)
        def _():
            dw_ref[...] = jnp.zeros_like(dw_ref)
            db_ref[...] = jnp.zeros_like(db_ref)
            dl_ref[...] = jnp.zeros_like(dl_ref)
            carry[...] = jnp.zeros_like(carry)

        _scan_tile([(ab_ref, dy_u, lb_s, q_u, False, jnp.where(tu == 0, 0.0, 1.0)),
                    (af_ref, dy_d, lf_s, q_d, True, jnp.where(td == 0, 0.0, 1.0))], True, carry)
        zero = jnp.zeros((NH, HD), F32)
        pb_s[pl.ds(0, T * NH), :] = hb_ref[...]
        pb_s[pl.ds(T * NH, NH), :] = jnp.where(tu == n - 1, sb_ref[...], jnp.where(tu == 0, zero, hbn_ref[pl.ds(0, NH), :]))
        pf_s[pl.ds(0, NH), :] = jnp.where(td == 0, zero, hfp_ref[pl.ds(7 * NH, NH), :])
        pf_s[pl.ds(NH, T * NH), :] = hf_ref[...]
        sides = ((1, xc_u, lb_s, pb_s, NH, ab_ref, gb_ref, dxcb_ref), (0, xc_d, lf_s, pf_s, 0, af_ref, gf_ref, dxcf_ref))
        for d, xc_ref, adj_s, prev_s, prev_off, a_ref, g_ref, dxc_ref in sides:
            wcols = slice(256 * d, 256 * d + 256)
            for h in range(NH):
                xch = xc_ref[_zrows(h, T), :]
                xcb = xch.astype(BF16)
                r, gi, mult = (g_ref[:, q * D + HD * h:q * D + HD * h + HD] for q in range(3))
                a = a_ref[_zrows(h, T), :]
                lam = lam_ref[h:h + 1, HD * d:HD * d + HD]
                sp = _softplus(-lam)
                du = adj_s[_zrows(h, T), :]
                da = du * prev_s[pl.ds(prev_off + h, T, stride=NH), :]
                dgi = du * mult * xch
                dmult = du * gi * xch
                dla = da * a - dmult * (a * a) / mult
                dr = dla * ((-LRU_C) * sp)
                dsp = jnp.sum(dla * ((-LRU_C) * r), axis=0, keepdims=True)
                dl_ref[h:h + 1, HD * d:HD * d + HD] += dsp * (-_sigmoid(-lam))
                dpre_s[:, 0:HD] = dr * r * (1.0 - r)
                dpre_s[:, HD:2 * HD] = dgi * gi * (1.0 - gi)
                dpre = dpre_s[...]
                dpb = dpre.astype(BF16)
                dw_ref[h, :, wcols] += _dot_tn(xcb, dpb)
                db_ref[h:h + 1, wcols] += jnp.sum(dpre, axis=0, keepdims=True)
                dxc_ref[_zrows(h, T), :] = du * mult * gi + _dot_nt(dpb, w_ref[h, :, wcols])

    full = lambda shape: pl.BlockSpec(shape, lambda i: (0,) * len(shape))
    wsp, bsp, lsp = full((NH, HD, 4 * HD)), full((NH, 4 * HD)), full((NH, 2 * HD))
    st = full((NH, HD))
    up = pl.BlockSpec((T * NH, HD), lambda i: (tile_u(i), 0))
    dn = pl.BlockSpec((T * NH, HD), lambda i: (tile_d(i), 0))
    dy_up = pl.BlockSpec((T * NH, HD), lambda i: (jnp.maximum(tile_u(i) - 1, 0), 0))
    dy_dn = pl.BlockSpec((T * NH, HD), lambda i: (jnp.maximum(tile_d(i) - 1, 0), 0))
    nxt = _tile_specs(NH, HD, n, tile_u)[2]
    prv = _tile_specs(NH, HD, n, tile_d)[1]
    g_up = pl.BlockSpec((T, 3 * D), lambda i: (tile_u(i), 0))
    g_dn = pl.BlockSpec((T, 3 * D), lambda i: (tile_d(i), 0))
    zs = jax.ShapeDtypeStruct((lx * NH, HD), F32)
    zbuf = pltpu.VMEM((T * NH, HD), F32)
    zbuf1 = pltpu.VMEM(((T + 1) * NH, HD), F32)
    return pl.pallas_call(
        body, name=name, grid=(n,),
        out_shape=(zs, zs, jax.ShapeDtypeStruct((NH, HD, 4 * HD), F32), jax.ShapeDtypeStruct((NH, 4 * HD), F32),
                   jax.ShapeDtypeStruct((NH, 2 * HD), F32)),
        in_specs=[up, dy_up, up, nxt, up, g_up, dn, dy_dn, dn, prv, dn, g_dn, st, wsp, lsp],
        out_specs=(up, dn, wsp, bsp, lsp),
        scratch_shapes=[zbuf, zbuf, zbuf, zbuf, zbuf1, zbuf1, pltpu.VMEM((T, 2 * HD), F32),
                        pltpu.VMEM((2, NH, HD), F32)],
        compiler_params=_cp(1, vmem_mb=56),
    )(xc_z, dy_z, hb_z, hb_z, ab_z, gb, xc_z, dy_z, hf_z, hf_z, af_z, gf, s_b, wcat, lamcat)


def _conv_bwd(dxc_a, dxc_b, xa, conv_wz, dcw0, dcb0, name):
    lx = dxc_a.shape[0] // NH
    n = lx // T

    def body(dm_a, dp_a, dn_a, dm_b, dp_b, dn_b, xan_ref, cw, dcw0_ref, dcb0_ref, dxa_ref, dcw_ref, dcb_ref,
             pad, dxa_s, xa_ref):
        i = pl.program_id(0)

        @pl.when(i == 0)
        def _():
            dcw_ref[...] = dcw0_ref[...]
            dcb_ref[...] = dcb0_ref[...]

        for h in range(NH):
            xa_ref[_zrows(h, T), :] = xan_ref[:, HD * h:HD * h + HD]

        pmask = jnp.where(_has_prev(i), 1.0, 0.0)
        nmask = jnp.where(_has_next(i, n), 1.0, 0.0)
        pad[pl.ds(0, 8 * NH), :] = (dp_a[...] + dp_b[...]) * pmask
        pad[pl.ds(8 * NH, T * NH), :] = dm_a[...] + dm_b[...]
        pad[pl.ds((T + 8) * NH, 8 * NH), :] = (dn_a[...] + dn_b[...]) * nmask

        def chunk(ci, carry):
            base = pl.multiple_of(ci * (CONV_CHUNK * NH), CONV_CHUNK * NH)
            xav = xa_ref[pl.ds(base, CONV_CHUNK * NH), :].reshape(CONV_CHUNK, NH, HD)
            acc = None
            for k in range(4):
                sl = pad[pl.ds(base + (9 - k) * NH, CONV_CHUNK * NH), :].reshape(CONV_CHUNK, NH, HD)
                term = sl * cw[k][None]
                acc = term if acc is None else acc + term
                dcw_ref[k] += jnp.sum(sl * xav, axis=0)
                if k == 1:
                    dcb_ref[...] += jnp.sum(sl, axis=0)
            dxa_s[pl.ds(base, CONV_CHUNK * NH), :] = acc.reshape(CONV_CHUNK * NH, HD)
            return carry
        lax.fori_loop(0, T // CONV_CHUNK, chunk, 0)
        for h in range(NH):
            dxa_ref[:, HD * h:HD * h + HD] = dxa_s[_zrows(h, T), :].astype(BF16)

    full = lambda shape: pl.BlockSpec(shape, lambda i: (0,) * len(shape))
    return pl.pallas_call(
        body, name=name, grid=(n,),
        out_shape=(jax.ShapeDtypeStruct((lx, D), BF16), jax.ShapeDtypeStruct((4, NH, HD), F32),
                   jax.ShapeDtypeStruct((NH, HD), F32)),
        in_specs=_tile_specs(NH, HD, n, lambda i: i) * 2 + [pl.BlockSpec((T, D), lambda i: (i, 0)), full((4, NH, HD)),
                                                            full((4, NH, HD)), full((NH, HD))],
        out_specs=(pl.BlockSpec((T, D), lambda i: (i, 0)), full((4, NH, HD)), full((NH, HD))),
        scratch_shapes=[pltpu.VMEM(((T + 16) * NH, HD), F32), pltpu.VMEM((T * NH, HD), F32),
                        pltpu.VMEM((T * NH, HD), F32)],
        compiler_params=_cp(1, vmem_mb=48),
    )(dxc_a, dxc_a, dxc_a, dxc_b, dxc_b, dxc_b, xa, conv_wz, dcw0, dcb0)


def _proj_bwd(dxa, dga, dgb, x, ctx, dxn, mods, norm_g, w_full, sgu):
    lx, lc = x.shape[0], ctx.shape[0]
    assert lc == T
    n = 1 + lx // T

    def body(dxa_ref, dga_ref, dgb_ref, w0, w_hbm, x_ref, c_ref, sc_ref, ng_ref, dxn_ref,
             u_ref, v_ref, dy_ref, g_ref, b_ref, sw_ref, bt_ref,
             gx_ref, dng_ref, dscx_ref, dshx_ref, dscc_ref, dshc_ref, du_ref, dv_ref, dws_ref, dbs_ref, dlg_ref, dlb_ref,
             mixed_s, dvn_s, w_late, w_sem):
        i = pl.program_id(0)
        is_ctx = i == 0
        w1, w4, w2, w3 = _late_weights(w_hbm, w_late, w_sem, (1, 4, 2, 3), i)

        @pl.when(is_ctx)
        def _():
            for acc in (dng_ref, dscx_ref, dshx_ref, dscc_ref, dshc_ref, dws_ref, dbs_ref, dlg_ref, dlb_ref):
                acc[...] = jnp.zeros_like(acc)

        xv = jnp.where(is_ctx, c_ref[...], x_ref[...])
        sc1 = 1.0 + jnp.where(is_ctx, sc_ref[1:2, :], sc_ref[0:1, :])
        r = lax.rsqrt(jnp.mean(xv * xv, axis=-1, keepdims=True) + NORM_EPS)
        xn = xv * r
        ng = ng_ref[...]

        def norm_bwd(dhn, dsc_ref, dsh_ref, with_x):
            t = dhn * xn
            dng_ref[...] += jnp.sum(t * sc1, axis=0, keepdims=True)
            dsc_ref[...] += jnp.sum(t * ng, axis=0, keepdims=True)
            dsh_ref[...] += jnp.sum(dhn, axis=0, keepdims=True)
            if with_x:
                dxh = dhn * (ng * sc1)
                gx_ref[...] = dxn_ref[...] + r * (dxh - xn * jnp.mean(dxh * xn, axis=-1, keepdims=True))

        @pl.when(is_ctx)
        def _():
            norm_bwd(_dot_nt(dxa_ref[...], w0[...]), dscc_ref, dshc_ref, False)

        @pl.when(i > 0)
        def _():
            def sgu_chunk(ch):
                rows = slice(HD * ch, HD * ch + HD)
                du, dv = _sgu_bwd_chunk(u_ref[rows, :], v_ref[rows, :], dy_ref[rows, :], g_ref[...], b_ref[...],
                                        sw_ref, bt_ref, mixed_s, dvn_s, dws_ref, dbs_ref, dlg_ref, dlb_ref)
                du_ref[rows, :] = du
                dv_ref[rows, :] = dv

            dhn = _dot_nt(dxa_ref[...], w0[...])
            sgu_chunk(0)
            dhn = dhn + _dot_nt(dga_ref[...], w1())
            for ch in range(1, T // HD):
                sgu_chunk(ch)
            dhn = dhn + _dot_nt(dgb_ref[...], w4())
            dhn = dhn + _dot_nt(du_ref[...], w2()) + _dot_nt(dv_ref[...], w3())
            norm_bwd(dhn, dscx_ref, dshx_ref, True)

    every = pl.BlockSpec((T, D), lambda i: (i, 0))
    lat = pl.BlockSpec((T, D), lambda i: (jnp.maximum(i - 1, 0), 0))
    vec = pl.BlockSpec((1, D), lambda i: (0, 0))
    wsp = pl.BlockSpec((NH, HD, HD), lambda i: (0, 0, 0))
    bsp = pl.BlockSpec((NH, HD), lambda i: (0, 0))
    in_specs = [every, lat, lat, pl.BlockSpec((D, D), lambda i: (0, 0)), ANY]
    in_specs += [lat, pl.BlockSpec((T, D), lambda i: (0, 0)), pl.BlockSpec((8, D), lambda i: (0, 1)), vec, lat]
    in_specs += [lat, lat, lat, vec, vec, wsp, pl.BlockSpec((HD, NH), lambda i: (0, 0))]
    vs = jax.ShapeDtypeStruct((1, D), F32)
    zb = jax.ShapeDtypeStruct((lx, D), BF16)
    return pl.pallas_call(
        body, name="proj_bwd", grid=(n,),
        out_shape=(jax.ShapeDtypeStruct((lx, D), F32), vs, vs, vs, vs, vs, zb, zb,
                   jax.ShapeDtypeStruct((NH, HD, HD), F32), jax.ShapeDtypeStruct((NH, HD), F32), vs, vs),
        in_specs=in_specs, out_specs=(lat, vec, vec, vec, vec, vec, lat, lat, wsp, bsp, vec, vec),
        scratch_shapes=[pltpu.VMEM((HD, D), F32), pltpu.VMEM((HD, D), F32), pltpu.VMEM((4, D, D), BF16),
                        pltpu.SemaphoreType.DMA((4,))],
        compiler_params=_cp(1, vmem_mb=56),
    )(dxa, dga, dgb, w_full, w_full, x, ctx, mods, norm_g, dxn, *sgu)


def _adam_math(w, g, m, v):
    m = ADAM_B1 * m + (1.0 - ADAM_B1) * g
    v = ADAM_B2 * v + (1.0 - ADAM_B2) * (g * g)
    m_hat = m / (1.0 - ADAM_B1 ** ADAM_STEP)
    v_hat = v / (1.0 - ADAM_B2 ** ADAM_STEP)
    delta = -ADAM_LR * (m_hat / (jnp.sqrt(v_hat) + ADAM_EPS) + ADAM_WD * w)
    return delta, m, v


def _adam_big(w, g, m, v, name):
    rows, cols = w.shape
    tr = 256

    def body(w_ref, g_ref, m_ref, v_ref, d_o, m_o, v_o):
        d, mm, vv = _adam_math(w_ref[...], g_ref[...], m_ref[...], v_ref[...])
        d_o[...] = d
        m_o[...] = mm
        v_o[...] = vv

    blk = pl.BlockSpec((tr, cols), lambda i: (i, 0))
    s = jax.ShapeDtypeStruct((rows, cols), F32)
    return pl.pallas_call(
        body, name=name, grid=(rows // tr,), out_shape=(s, s, s), in_specs=[blk] * 4, out_specs=(blk,) * 3,
        compiler_params=_cp(1, vmem_mb=48),
    )(w, g, m, v)


def _adam_small(items, tot):
    ni = len(items)
    pieces = [it[1] if isinstance(it[1], list) else None for it in items]
    flat = [a for it, pc in zip(items, pieces) for a in ((it[0], it[2], it[3]) if pc is not None else it)]
    n_in = len(flat) + 1
    out_shape = tuple(jax.ShapeDtypeStruct(it[0].shape, F32) for it, pc in zip(items, pieces)
                      for _ in range(4 if pc is not None else 3))
    n_out = len(out_shape)
    n_loads = sum(3 + (len(pc) if pc is not None else 1) for pc in pieces)

    def body(*refs):
        ins, tot_ref, outs = refs[:n_in - 1], refs[n_in - 1], refs[n_in:n_in + n_out]
        bufs = refs[n_in + n_out:n_in + n_out + 7 * ni]
        sem_in, sem_out = refs[n_in + n_out + 7 * ni:]
        loads, q_in, q_sem = [], 0, 0
        for k, pc in enumerate(pieces):
            w_b, g_b, m_b, v_b = bufs[7 * k:7 * k + 4]
            srcs = [(ins[q_in], w_b)]
            if pc is None:
                srcs.append((ins[q_in + 1], g_b))
                q_in += 1
            else:
                srcs += [(tot_ref.at[pl.ds(r0, nr), pl.ds(c0, nc)], g_b.at[pl.ds(d0, nr), :]) for r0, nr, c0, nc, d0 in pc]
            srcs += [(ins[q_in + 1], m_b), (ins[q_in + 2], v_b)]
            q_in += 3
            mine = []
            for src, dst in srcs:
                mine.append(pltpu.make_async_copy(src, dst, sem_in.at[q_sem]))
                q_sem += 1
            loads.append(mine)
        for mine in loads:
            for cp in mine:
                cp.start()
        stores, q_out = [], 0
        for k, pc in enumerate(pieces):
            for cp in loads[k]:
                cp.wait()
            w_b, g_b, m_b, v_b = bufs[7 * k:7 * k + 4]
            res = _adam_math(w_b[...], g_b[...], m_b[...], v_b[...])
            srcs = []
            for q in range(3):
                bufs[7 * k + 4 + q][...] = res[q]
                srcs.append(bufs[7 * k + 4 + q])
            if pc is not None:
                srcs.append(g_b)
            for src in srcs:
                cp = pltpu.make_async_copy(src, outs[q_out], sem_out.at[q_out])
                cp.start()
                stores.append(cp)
                q_out += 1
        for cp in stores:
            cp.wait()

    scratch = [pltpu.VMEM(it[0].shape, F32) for it in items for _ in range(7)]
    scratch += [pltpu.SemaphoreType.DMA((n_loads,)), pltpu.SemaphoreType.DMA((n_out,))]
    res = pl.pallas_call(
        body, name="adam_small", out_shape=out_shape, in_specs=[HBM] * n_in, out_specs=(HBM,) * n_out,
        scratch_shapes=scratch, compiler_params=_cp(vmem_mb=40),
    )(*flat, tot)
    outs, q = [], 0
    for pc in pieces:
        outs.append(tuple(res[q:q + 3]) + ((res[q + 3],) if pc is not None else (None,)))
        q += 4 if pc is not None else 3
    return outs


def kernel(x, c, ctx, c_ctx, ada_w, ada_b, norm_g, w_in, conv_w, conv_b, lru_wa, lru_ba, lru_wx, lru_bx, lru_lambda, sgu_ln_g, sgu_ln_b, sgu_w, sgu_b, w_out, final_g, loss_target, m_c_ctx, m_ada_w, m_ada_b, m_norm_g, m_w_in, m_conv_w, m_conv_b, m_lru_wa, m_lru_ba, m_lru_wx, m_lru_bx, m_lru_lambda, m_sgu_ln_g, m_sgu_ln_b, m_sgu_w, m_sgu_b, m_w_out, m_final_g, v_c_ctx, v_ada_w, v_ada_b, v_norm_g, v_w_in, v_conv_w, v_conv_b, v_lru_wa, v_lru_ba, v_lru_wx, v_lru_bx, v_lru_lambda, v_sgu_ln_g, v_sgu_ln_b, v_sgu_w, v_sgu_b, v_w_out, v_final_g):
    ix, iy, ic = lax.axis_index("x"), lax.axis_index("y"), lax.axis_index("c")
    chip = 2 * ix + iy
    dev = 2 * chip + ic
    lx = x.shape[1]
    lc = ctx.shape[1]

    smalls = jnp.concatenate([conv_w[0], lru_lambda[0], jnp.zeros((10, 256), F32)], axis=0)
    c_ctx2 = c_ctx.reshape(1, D)
    ada_b_j = lax.dynamic_slice(ada_b, (0, 768 * chip), (1, 768))
    mods, c_slots, sm_all, w_in_full, wo_land, ada_land = _gather_in(c, c_ctx2, ada_w[0], ada_b_j, w_in[0], w_out[0],
                                                                     smalls)
    wo_ss, wo_rs, ada_ss, ada_rs, wo_land, ada_land, token = _late_gather_start(wo_land, ada_land)
    mods = mods + token[0:1, 0:1]
    sm3 = sm_all.reshape(NCHIP, 16, 256)
    conv_w_full = sm3[:, 0:4, :].transpose(1, 0, 2).reshape(4, D)
    lam_full = sm3[:, 4:6, :].transpose(1, 0, 2).reshape(2, D)
    conv_wz = conv_w_full.reshape(4, NH, HD)
    conv_bz = conv_b.reshape(NH, HD)
    lamcat = lam_full.reshape(2, NH, HD).transpose(1, 0, 2).reshape(NH, 2 * HD)
    wa, wx, ba, bx = lru_wa[0], lru_wx[0], lru_ba[0], lru_bx[0]
    wcat = jnp.concatenate([wa[0], wx[0], wa[1], wx[1]], axis=-1).astype(BF16)
    bcat = jnp.concatenate([ba[0], bx[0], ba[1], bx[1]], axis=-1)
    sgu_wb = sgu_w[0].astype(BF16)
    sgu_bt = sgu_b[0].T
    final_g2 = final_g.reshape(1, D)

    zero_s = jnp.zeros((NH, HD), F32)
    hn, xa_all, ga, u, v, gb, ys = _proj(x[0], ctx[0], mods, norm_g, w_in_full, (sgu_ln_g, sgu_ln_b, sgu_wb, sgu_bt))
    xcz, af, ab, hf, hb, gf, gb_l, _, hb0 = _lru_fwd(xa_all, conv_wz, conv_bz, wcat, bcat, lamcat, "lru_fwd")

    w_out_full = _late_gather_wait(wo_land, wo_ss, wo_rs, "w_out", hf, "late_gather_wait_w_out")
    (loss_part, dfg, dgx, dxn, y, do, dga, dgb, dyl_z, dys) = _out_fwd_bwd(
        hf, hb, ga, gb, ys, x[0], loss_target[0], mods, final_g2, w_out_full)

    dxc_b, dxc_f, dwc, dbc, dlc = _lru_bwd(xcz, dyl_z, hf, hb, af, ab, gf, gb_l, hb0, wcat, lamcat, "lru_bwd")
    dxa, dcw, dcb = _conv_bwd(dxc_b, dxc_f, xa_all, conv_wz, jnp.zeros((4, NH, HD), F32), zero_s, "conv_bwd")

    grad_x, dng, dsc_x, dsh_x, dsc_c, dsh_c, du, dv, d_sgu_w, d_sgu_b, d_ln_g, d_ln_b = _proj_bwd(
        dxa, dga, dgb, x[0], ctx[0], dxn, mods, norm_g, w_in_full, (u, v, dys, sgu_ln_g, sgu_ln_b, sgu_wb, sgu_bt))
    dzs = [dxa, dga, du, dv, dgb]

    dmx = jnp.concatenate([dsh_x, dsc_x, dgx], axis=0)
    dmc = jnp.concatenate([dsh_c, dsc_c, jnp.zeros((1, D), F32)], axis=0)
    slot = jnp.concatenate([dmx, loss_part], axis=0)
    slots = lax.dynamic_update_slice(jnp.zeros((32, D), F32), slot, (4 * dev, 0))
    vecs = jnp.concatenate([dfg, dng, dcb.reshape(1, D), d_ln_g, d_ln_b, dcw.reshape(4, D), dmc,
                            jnp.zeros((4, D), F32), slots], axis=0)
    d_sgu_w4 = d_sgu_w.reshape(4, 256, HD).transpose(1, 0, 2).reshape(256, 4 * HD)
    pad8 = lambda a: jnp.pad(a, ((0, 8 - a.shape[0]), (0, 4 * HD - a.shape[1])))
    pack = jnp.concatenate([dwc.reshape(NH * HD, 4 * HD), pad8(dbc), pad8(dlc), d_sgu_w4, pad8(d_sgu_b),
                            vecs.reshape(96, 4 * HD), jnp.zeros((8, 4 * HD), F32)], axis=0)
    g_w_in, g_w_out, tot = _grads_reduce(hn, dzs, lc, y, do, pack)

    n_w = NH * HD
    g_lru_wa = [(0, n_w, 2 * HD * d, HD, n_w * d) for d in range(2)]
    g_lru_wx = [(0, n_w, 2 * HD * d + HD, HD, n_w * d) for d in range(2)]
    g_lru_ba = [(n_w, NH, 2 * HD * d, HD, NH * d) for d in range(2)]
    g_lru_bx = [(n_w, NH, 2 * HD * d + HD, HD, NH * d) for d in range(2)]
    g_sgu_w = [(1040, 256, HD * q, HD, 256 * q) for q in range(4)]
    g_sgu_b = [(1296, NH, 0, HD, 0)]
    g_lc = tot[1032:1040, 0:2 * HD]
    tv = tot[1304:1400].reshape(48, D)
    g_final_g, g_norm_g, g_conv_b, g_ln_g, g_ln_b = tv[0:1], tv[1:2], tv[2:3], tv[3:4], tv[4:5]
    g_conv_w_full = tv[5:9]
    dmc_tot = tv[9:12].reshape(1, 3 * D)
    slots_all = tv[16:48].reshape(8, 4, D)
    dmx_all = slots_all[:, 0:3, :].reshape(8, 3 * D)
    c_all = c_slots.reshape(8, 8, D)[:, 0, :]
    g_lam_full = jnp.stack([g_lc[:, 0:HD], g_lc[:, HD:2 * HD]]).reshape(2, D)
    g_conv_w = lax.dynamic_slice(g_conv_w_full, (0, 256 * chip), (4, 256))
    g_lam = lax.dynamic_slice(g_lam_full, (0, 256 * chip), (2, 256))
    dmx_all_j = lax.dynamic_slice(dmx_all, (0, 768 * chip), (8, 768))
    dmc_j = lax.dynamic_slice(dmc_tot, (0, 768 * chip), (1, 768))
    ada_full = _late_gather_wait(ada_land, ada_ss, ada_rs, "ada_w", tot, "late_gather_wait_ada_w")
    g_ada_w, g_ada_b, g_c_ctx = _ada_bwd(c_all, dmx_all_j, dmc_j, dmx_all, dmc_tot, c_ctx2, ada_full)

    big = {
        "ada_w": _adam_big(ada_w[0], g_ada_w, m_ada_w[0], v_ada_w[0], "adam_ada_w"),
        "w_in": _adam_big(w_in[0], g_w_in, m_w_in[0], v_w_in[0], "adam_w_in"),
        "w_out": _adam_big(w_out[0], g_w_out, m_w_out[0], v_w_out[0], "adam_w_out"),
    }
    small_in = {
        "c_ctx": (c_ctx, g_c_ctx, m_c_ctx, v_c_ctx, (1, D)),
        "ada_b": (ada_b, g_ada_b, m_ada_b, v_ada_b, (1, 3 * D)),
        "norm_g": (norm_g, g_norm_g, m_norm_g, v_norm_g, (1, D)),
        "conv_w": (conv_w, g_conv_w, m_conv_w, v_conv_w, (4, 256)),
        "conv_b": (conv_b, g_conv_b, m_conv_b, v_conv_b, (1, D)),
        "lru_wa": (lru_wa, g_lru_wa, m_lru_wa, v_lru_wa, (2 * NH * HD, HD)),
        "lru_ba": (lru_ba, g_lru_ba, m_lru_ba, v_lru_ba, (2 * NH, HD)),
        "lru_wx": (lru_wx, g_lru_wx, m_lru_wx, v_lru_wx, (2 * NH * HD, HD)),
        "lru_bx": (lru_bx, g_lru_bx, m_lru_bx, v_lru_bx, (2 * NH, HD)),
        "lru_lambda": (lru_lambda, g_lam, m_lru_lambda, v_lru_lambda, (2, 256)),
        "sgu_ln_g": (sgu_ln_g, g_ln_g, m_sgu_ln_g, v_sgu_ln_g, (1, D)),
        "sgu_ln_b": (sgu_ln_b, g_ln_b, m_sgu_ln_b, v_sgu_ln_b, (1, D)),
        "sgu_w": (sgu_w, g_sgu_w, m_sgu_w, v_sgu_w, (NH * HD, HD)),
        "sgu_b": (sgu_b, g_sgu_b, m_sgu_b, v_sgu_b, (NH, HD)),
        "final_g": (final_g, g_final_g, m_final_g, v_final_g, (1, D)),
    }
    names_small = list(small_in)
    res_small = _adam_small([tuple(a if isinstance(a, list) else a.reshape(small_in[k][4]) for a in small_in[k][:4])
                             for k in names_small], tot)
    full_shapes = {"ada_w": ada_w.shape, "w_in": w_in.shape, "w_out": w_out.shape}
    grads, deltas, new_m, new_v = {}, {}, {}, {}
    for k in ("ada_w", "w_in", "w_out"):
        g = {"ada_w": g_ada_w, "w_in": g_w_in, "w_out": g_w_out}[k]
        grads[k] = g.reshape(full_shapes[k])
        deltas[k], new_m[k], new_v[k] = (a.reshape(full_shapes[k]) for a in big[k])
    for k, res in zip(names_small, res_small):
        shape = small_in[k][0].shape
        grads[k] = (small_in[k][1] if res[3] is None else res[3]).reshape(shape)
        deltas[k], new_m[k], new_v[k] = (a.reshape(shape) for a in res[:3])

    loss = jnp.sum(slots_all[:, 3, 0:3])
    order = ["c_ctx", "ada_w", "ada_b", "norm_g", "w_in", "conv_w", "conv_b", "lru_wa", "lru_ba", "lru_wx", "lru_bx",
             "lru_lambda", "sgu_ln_g", "sgu_ln_b", "sgu_w", "sgu_b", "w_out", "final_g"]
    return (loss, grad_x.reshape(x.shape), *[grads[k] for k in order], *[deltas[k] for k in order],
            *[new_m[k] for k in order], *[new_v[k] for k in order])
```

```python
import jax
import jax.numpy as jnp
from jax import lax
from jax.experimental import pallas as pl
from jax.experimental.pallas import tpu as pltpu

F32 = jnp.float32
BF16 = jnp.bfloat16

D = 1024
NH = 8
HD = 128
NCHIP = 4
T = 256
NORM_EPS = 1e-6
LN_EPS = 1e-5
LRU_C = 8.0
ADAM_LR = 0.001
ADAM_B1 = 0.9
ADAM_B2 = 0.999
ADAM_EPS = 1e-08
ADAM_WD = 0.01
ADAM_STEP = 10

VMEM = pl.BlockSpec(memory_space=pltpu.VMEM)
ANY = pl.BlockSpec(memory_space=pl.ANY)
MESH = pl.DeviceIdType.MESH


def _cp(n_grid=0, vmem_mb=None):
    kw = {}
    if n_grid:
        kw["dimension_semantics"] = ("arbitrary",) * n_grid
    if vmem_mb:
        kw["vmem_limit_bytes"] = vmem_mb << 20
    return pltpu.CompilerParams(**kw)


def _sigmoid(x):
    return 0.5 * jnp.tanh(0.5 * x) + 0.5


def _silu_and_grad(x):
    s = _sigmoid(x)
    return x * s, s * (1.0 + x * (1.0 - s))


_GELU_K = 0.7978845608028654
_GELU_C = 0.044715


def _gelu_and_grad(x):
    x2 = x * x
    th = jnp.tanh(x * (_GELU_K + (_GELU_K * _GELU_C) * x2))
    p = 0.5 + 0.5 * th
    g = x * p
    dg = p + g * (1.0 - th) * (_GELU_K + (3.0 * _GELU_K * _GELU_C) * x2)
    return g, dg


def _softplus(x):
    return jnp.maximum(x, 0.0) + jnp.log1p(jnp.exp(-jnp.abs(x)))


def _lru_gate(pre, lam_row, d, off=None):
    off = 256 * d if off is None else off
    r = _sigmoid(pre[:, off:off + HD])
    gi = _sigmoid(pre[:, off + HD:off + 2 * HD])
    lam = lam_row[:, HD * d:HD * d + HD]
    sp = _softplus(-lam)
    la = (-LRU_C) * r * sp
    a = jnp.exp(la)
    x2 = 2.0 * la
    m2 = jnp.where(x2 > -1e-3, -x2 * (1.0 + 0.5 * x2), 1.0 - a * a)
    mult = jnp.sqrt(m2)
    return r, gi, lam, sp, a, mult


def _dot(a, b):
    return jnp.dot(a, b, preferred_element_type=F32)


def _dot_tn(a, b):
    return lax.dot_general(a, b, (((0,), (0,)), ((), ())), preferred_element_type=F32)


def _dot_nt(a, b):
    return lax.dot_general(a, b, (((1,), (1,)), ((), ())), preferred_element_type=F32)


def _mo(v, m):
    return v if isinstance(v, int) else pl.multiple_of(v, m)


def _zrows(h, n):
    return pl.ds(h, n, stride=NH)


def _gather_in(c, c_ctx, ada_w, ada_b_j, w_in, w_out, smalls):
    nch = [1, 4]
    wrows = lambda cc, q: (pl.ds(_mo(512 * cc, 16), 512) if q is None
                           else pl.ds(_mo(512 * cc + (512 // nch[1]) * q, 16), 512 // nch[1]))
    specs = [
        ((64, 256), F32, lambda r, jj, cc, q=None: r.at[pl.ds(_mo(16 * jj + 8 * cc, 8), 8), :]),
        ((D, 5120), BF16, lambda r, jj, cc, q=None: r.at[wrows(cc, q), pl.ds(_mo(1280 * jj, 128), 1280)]),
    ]
    halves = [lambda r, cc, q=None: r.at[pl.ds(_mo(8 * cc, 8), 8), :],
              lambda r, cc, q=None: r.at[wrows(cc, q), :]]
    na = len(specs)
    sem_base = [0, 6 * nch[0]]
    sidx = lambda a, q, k: sem_base[a] + 6 * q + k
    n_tiny = 6 * sum(nch)
    n_sem = n_tiny + 10

    def body(c_ref, cc_ref, ada_ref, adab_ref, win_ref, wout_ref, sm_ref,
             mods_o, call_o, sm_o, win_o, wol_o, adal_o, s_win, s_ada, s_wout, f_win, f_ada, f_wout, cslot, lhs, mbuf,
             send_sems, recv_sems, local_sems, load_sems):
        x, y, c = lax.axis_index("x"), lax.axis_index("y"), lax.axis_index("c")
        j = 2 * x + y
        dev = 2 * j + c
        sib = (x, y, 1 - c)
        chips = [(1 - x, y), (x, 1 - y), (1 - x, 1 - y)]
        cj = [2 * cx + cy for cx, cy in chips]
        outs = [sm_o, win_o]
        srcs = [sm_ref, s_win]

        def copy(idx, src, dst, to):
            return pltpu.make_async_remote_copy(src_ref=src, dst_ref=dst, send_sem=send_sems.at[idx],
                                                recv_sem=recv_sems.at[idx], device_id=to, device_id_type=MESH)

        sends = []

        def start(cp):
            cp.start()
            sends.append(cp)

        cslot[...] = jnp.zeros_like(cslot)
        cslot[0:1, :] = c_ref[...]
        my_slot = pl.ds(_mo(8 * dev, 8), 8)
        others = [sib] + [(*chips[k], c) for k in range(3)] + [(*chips[k], 1 - c) for k in range(3)]
        other_dev = [dev + 1 - 2 * c] + [2 * cj[k] + c for k in range(3)] + [2 * cj[k] + 1 - c for k in range(3)]
        base = n_tiny
        for r in range(7):
            start(copy(base + r, cslot, call_o.at[my_slot, :], others[r]))
        call_o[my_slot, :] = cslot[...]

        crow = 512 // nch[1]
        loads = []
        for cc in (c, 1 - c):
            for q in range(nch[1]):
                rows = pl.ds(_mo(512 * cc + crow * q, 16), crow)
                loads.append(pltpu.make_async_copy(win_ref.at[rows, :], f_win.at[rows, :], load_sems.at[len(loads)]))
        loads.append(pltpu.make_async_copy(ada_ref, f_ada, load_sems.at[len(loads)]))
        loads.append(pltpu.make_async_copy(wout_ref, f_wout, load_sems.at[len(loads)]))
        for ld in loads:
            ld.start()
        for k in range(2):
            start(copy(sidx(0, 0, k), halves[0](srcs[0], c), specs[0][2](outs[0], j, c), (*chips[k], c)))
        for q in range(nch[1]):
            loads[q].wait()
            rows = pl.ds(_mo(512 * c + crow * q, 16), crow)
            s_win[rows, :] = f_win[rows, :].astype(BF16)
            for k in range(2):
                start(copy(sidx(1, q, k), halves[1](s_win, c, q), specs[1][2](win_o, j, c, q), (*chips[k], c)))
        for q in range(nch[1]):
            loads[nch[1] + q].wait()
            rows = pl.ds(_mo(512 * (1 - c) + crow * q, 16), crow)
            s_win[rows, :] = f_win[rows, :].astype(BF16)
        local = []
        for a in range(na):
            for cc in range(2):
                lc = pltpu.make_async_copy(halves[a](srcs[a], cc), specs[a][2](outs[a], j, cc), local_sems.at[2 * a + cc])
                lc.start()
                local.append(lc)
        loads[2 * nch[1]].wait()
        s_ada[...] = f_ada[...].astype(BF16)
        loads[2 * nch[1] + 1].wait()
        s_wout[...] = f_wout[...].astype(BF16)
        for q, (src, dst) in enumerate([(s_wout, wol_o.at[pl.ds(_mo(512 * j, 16), 512), :]),
                                        (s_ada, adal_o.at[:, pl.ds(_mo(768 * j, 128), 768)])]):
            lc = pltpu.make_async_copy(src, dst, local_sems.at[2 * na + q])
            lc.start()
            local.append(lc)

        for r in range(7):
            slot = call_o.at[pl.ds(_mo(8 * other_dev[r], 8), 8), :]
            copy(base + r, slot, slot, sib).wait_recv()
        lhs[...] = jnp.zeros_like(lhs)
        for b in range(8):
            cv = call_o[8 * b:8 * b + 1, :]
            lhs[b:b + 1, :] = cv * _sigmoid(cv)
        cv = cc_ref[...]
        lhs[8:9, :] = cv * _sigmoid(cv)
        mbuf[j] = _dot(lhs[...].astype(BF16), s_ada[...]) + adab_ref[...]
        for k in range(3):
            start(copy(base + 7 + k, mbuf.at[j], mbuf.at[j], (*chips[k], c)))
        for k in range(3):
            copy(base + 7 + k, mbuf.at[cj[k]], mbuf.at[cj[k]], sib).wait_recv()
        mods_o[...] = jnp.zeros_like(mods_o)
        for jj in range(NCHIP):
            mods_o[0:1, 768 * jj:768 * jj + 768] = mbuf[jj, pl.ds(dev, 1), :]
            mods_o[1:2, 768 * jj:768 * jj + 768] = mbuf[jj, 8:9, :]

        kx = [1 - x, x, 1 - x]
        ky = [y, 1 - y, 1 - y]
        pick = lambda k, lst: jnp.where(k == 0, lst[0], jnp.where(k == 1, lst[1], lst[2]))
        for a in range(na):
            for q in range(nch[a]):
                for step, k in enumerate([c, 1 - c]):
                    reg = specs[a][2](outs[a], pick(k, cj), c, q)
                    copy(sidx(a, q, k), reg, reg, sib).wait_recv()
                    if step == 0:
                        start(copy(sidx(a, q, 2), reg, reg, (pick(1 - c, kx), pick(1 - c, ky), c)))
                    start(copy(sidx(a, q, 3 + k), reg, reg, sib))
        for a in range(na):
            for q in range(nch[a]):
                reg = specs[a][2](outs[a], cj[2], c, q)
                copy(sidx(a, q, 2), reg, reg, sib).wait_recv()
                start(copy(sidx(a, q, 5), reg, reg, sib))
        for a in range(na):
            for q in range(nch[a]):
                for k in range(3):
                    reg = specs[a][2](outs[a], cj[k], 1 - c, q)
                    copy(sidx(a, q, 3 + k), reg, reg, sib).wait_recv()
        for cp in sends:
            cp.wait_send()
        for lc in local:
            lc.wait()

    out_shape = (jax.ShapeDtypeStruct((8, 3 * D), F32), jax.ShapeDtypeStruct((64, D), F32),
                 jax.ShapeDtypeStruct(specs[0][0], F32), jax.ShapeDtypeStruct(specs[1][0], BF16),
                 jax.ShapeDtypeStruct((2048, D), BF16), jax.ShapeDtypeStruct((D, 3 * D), BF16))
    return pl.pallas_call(
        body, name="gather_in", out_shape=out_shape,
        in_specs=[VMEM, VMEM, ANY, VMEM, ANY, ANY, VMEM], out_specs=(VMEM, VMEM, VMEM, ANY, ANY, ANY),
        scratch_shapes=[pltpu.VMEM((D, 1280), BF16), pltpu.VMEM((D, 768), BF16), pltpu.VMEM((512, D), BF16),
                        pltpu.VMEM((D, 1280), F32), pltpu.VMEM((D, 768), F32), pltpu.VMEM((512, D), F32),
                        pltpu.VMEM((8, D), F32), pltpu.VMEM((16, D), F32), pltpu.VMEM((NCHIP, 16, 768), F32),
                        pltpu.SemaphoreType.DMA((n_sem,)), pltpu.SemaphoreType.DMA((n_sem,)),
                        pltpu.SemaphoreType.DMA((2 * na + 2,)), pltpu.SemaphoreType.DMA((2 * nch[1] + 2,))],
        compiler_params=_cp(vmem_mb=56),
    )(c, c_ctx, ada_w, ada_b_j, w_in, w_out, smalls)


HBM = pl.BlockSpec(memory_space=pltpu.HBM)
SEM = pl.BlockSpec(memory_space=pltpu.SEMAPHORE)


def _in_hbm(a):
    return pltpu.with_memory_space_constraint(a, pltpu.HBM)


def _late_gather_regions(x, y, c):
    chips = [(1 - x, y), (x, 1 - y), (1 - x, 1 - y)]
    wo_reg = lambda r, jj, cc: r.at[pl.ds(_mo(512 * jj + 256 * cc, 16), 256), :]
    ada_reg = lambda r, jj, cc: r.at[pl.ds(_mo(512 * cc, 16), 512), pl.ds(_mo(768 * jj, 128), 768)]
    return chips, wo_reg, ada_reg


def _late_gather_start(wo_land, ada_land):
    def body(wol_ref, adal_ref, wo_ss, wo_rs, ada_ss, ada_rs, wol_thru, adal_thru, token):
        x, y, c = lax.axis_index("x"), lax.axis_index("y"), lax.axis_index("c")
        j = 2 * x + y
        chips, wo_reg, ada_reg = _late_gather_regions(x, y, c)
        for k in range(3):
            for cc in range(2):
                pltpu.make_async_remote_copy(src_ref=wo_reg(wol_ref, j, c), dst_ref=wo_reg(wol_ref, j, c),
                                             send_sem=wo_ss.at[2 * k + cc], recv_sem=wo_rs.at[2 * k + c],
                                             device_id=(*chips[k], cc), device_id_type=MESH).start()
        for k in range(3):
            for cc in range(2):
                pltpu.make_async_remote_copy(src_ref=ada_reg(adal_ref, j, c), dst_ref=ada_reg(adal_ref, j, c),
                                             send_sem=ada_ss.at[2 * k + cc], recv_sem=ada_rs.at[2 * k + c],
                                             device_id=(*chips[k], cc), device_id_type=MESH).start()
        token[...] = jnp.zeros_like(token)

    sems = pltpu.SemaphoreType.DMA((6,))
    return pl.pallas_call(
        body, name="late_gather_start",
        out_shape=(sems, sems, sems, sems, pltpu.HBM(wo_land.shape, BF16), pltpu.HBM(ada_land.shape, BF16),
                   jax.ShapeDtypeStruct((8, 128), F32)),
        in_specs=(HBM, HBM), out_specs=(SEM, SEM, SEM, SEM, HBM, HBM, VMEM), input_output_aliases={0: 4, 1: 5},
        compiler_params=pltpu.CompilerParams(has_side_effects=pltpu.SideEffectType.DATAFLOW_SIDE_EFFECTING),
    )(_in_hbm(wo_land), _in_hbm(ada_land))


def _late_gather_wait(land, send_sems, recv_sems, which, after, name):
    def body(land_ref, ss, rs, after_ref, land_out):
        x, y, c = lax.axis_index("x"), lax.axis_index("y"), lax.axis_index("c")
        j = 2 * x + y
        chips, wo_reg, ada_reg = _late_gather_regions(x, y, c)
        reg = wo_reg if which == "w_out" else ada_reg
        for k in range(3):
            kj = 2 * chips[k][0] + chips[k][1]
            for cc in range(2):
                cp = pltpu.make_async_remote_copy(src_ref=reg(land_ref, j, c), dst_ref=reg(land_ref, kj, cc),
                                                  send_sem=ss.at[2 * k + cc], recv_sem=rs.at[2 * k + cc],
                                                  device_id=(*chips[k], cc), device_id_type=MESH)
                cp.wait_send()
                cp.wait_recv()

    return pl.pallas_call(
        body, name=name, out_shape=pltpu.HBM(land.shape, land.dtype),
        in_specs=(HBM, SEM, SEM, ANY), out_specs=HBM, input_output_aliases={0: 0},
        compiler_params=pltpu.CompilerParams(has_side_effects=pltpu.SideEffectType.DATAFLOW_SIDE_EFFECTING),
    )(land, send_sems, recv_sems, after)


RCHUNK = 16


def _grads_reduce(hn, dzs, lc, y, do, pack):
    rp = pack.shape[0]
    hp = rp // 2
    assert hp % RCHUNK == 0
    wi_w = 1280
    lx = hn.shape[0] - lc
    lt = lx + lc
    n_dz = len(dzs)

    def body(*refs):
        hn_hbm, dz_hbm = refs[0], refs[1:1 + n_dz]
        y_hbm, do_hbm, pk_hbm, wi_out, wo_out, pk_out = refs[1 + n_dz:7 + n_dz]
        (hn_mine, hn_other, dzbuf, wi_other, wi_mine, wi_recv, wi_send, wi_rb,
         y_blk, do_mine, do_other, wo_other, wo_mine, wo_recv, wo_send, wo_rb,
         pk_mine, pk_recv, pk_send, pk_rb, pk_own, send_sems, recv_sems, local_sems) = refs[7 + n_dz:]
        x, y, c = lax.axis_index("x"), lax.axis_index("y"), lax.axis_index("c")
        j = 2 * x + y
        sib = (x, y, 1 - c)
        chips = [(1 - x, y), (x, 1 - y), (1 - x, 1 - y)]
        cj = [2 * cx + cy for cx, cy in chips]
        near = (jnp.where(c == 0, 1 - x, x), jnp.where(c == 0, y, 1 - y), c)
        slabs = [cj[2], cj[0], cj[1], j]

        def copy(k, src, dst, to):
            return pltpu.make_async_remote_copy(src_ref=src, dst_ref=dst, send_sem=send_sems.at[k],
                                                recv_sem=recv_sems.at[k], device_id=to, device_id_type=MESH)

        def local(k, src, dst):
            cp = pltpu.make_async_copy(src, dst, local_sems.at[k])
            cp.start()
            return cp

        rows_half = lambda r, cc, n: r.at[pl.ds(_mo(cc * n, 16), n), :]
        cols_half = lambda r, cc, n: r.at[:, pl.ds(_mo(cc * n, 128), n)]
        pk_piece = lambda r, cc, jj: r.at[pl.ds(_mo(cc * hp, 16), hp), pl.ds(_mo(jj * 128, 128), 128)]

        sends = []

        def start(cp):
            cp.start()
            sends.append(cp)

        def dz_pieces(s):
            g0 = wi_w * s
            k0, off0 = g0 // D, g0 % D
            w0 = min(D - off0, wi_w)
            pieces = [(k0, off0, w0, 0)]
            if w0 < wi_w:
                pieces.append((k0 + 1, 0, wi_w - w0, w0))
            return pieces

        def dz_copies(s):
            cps = []
            for q, (k, off, w, dst) in enumerate(dz_pieces(s)):
                cps.append(pltpu.make_async_copy(dz_hbm[k].at[pl.ds(lc if k == 0 else 0, lx), pl.ds(off, w)],
                                                 dzbuf.at[pl.ds(lc, lx), pl.ds(dst, w)], local_sems.at[11 + q]))
            if s == 0:
                cps.append(pltpu.make_async_copy(dz_hbm[0].at[pl.ds(0, lc), :], dzbuf.at[pl.ds(0, lc), pl.ds(0, D)],
                                                 local_sems.at[13]))
            return cps

        def dz_load(sl):
            for s in range(NCHIP):
                @pl.when(sl == s)
                def _():
                    if s == 0:
                        dzbuf[pl.ds(0, lc), pl.ds(D, wi_w - D)] = jnp.zeros((lc, wi_w - D), BF16)
                    else:
                        dzbuf[pl.ds(0, lc), :] = jnp.zeros((lc, wi_w), BF16)
                    for cp in dz_copies(s):
                        cp.start()

        def dz_wait(sl):
            for s in range(NCHIP):
                @pl.when(sl == s)
                def _():
                    for cp in dz_copies(s):
                        cp.wait()

        l_pk = local(0, rows_half(pk_hbm, c, hp), pk_mine)
        start(copy(0, rows_half(pk_hbm, 1 - c, hp), pk_recv, sib))
        col = lambda r, cc: r.at[:, pl.ds(_mo(cc * 512, 128), 512)]
        do_loads = [local(7, col(do_hbm, c), do_mine), local(14, col(do_hbm, 1 - c), do_other)]
        hn_loads = [local(2, col(hn_hbm, c), hn_mine), local(4, col(hn_hbm, 1 - c), hn_other)]
        y_copy = lambda s: pltpu.make_async_copy(col(y_hbm, slabs[s]), y_blk, local_sems.at[1])
        y_copy(0).start()
        dz_load(slabs[0])

        def pair_sum(mine, recv, send, nrows, keep, relayed=None):
            def step(i, carry):
                rows = pl.ds(_mo(i * RCHUNK, RCHUNK), RCHUNK)
                s = mine[rows, :] + recv[rows, :].astype(F32)
                if relayed is not None:
                    s = s + relayed[rows, :].astype(F32)
                if keep:
                    mine[rows, :] = s
                if send is not None:
                    send[rows, :] = s.astype(BF16)
                return carry
            lax.fori_loop(0, nrows // RCHUNK, step, 0)

        def chip_sum(own, rb, nrows, terms=(0, 1, 2)):
            def step(i, carry):
                rows = pl.ds(_mo(i * RCHUNK, RCHUNK), RCHUNK)
                acc = own[rows, :]
                for q in terms:
                    acc = acc + rb[q, rows, :].astype(F32)
                own[rows, :] = acc
                return carry
            lax.fori_loop(0, nrows // RCHUNK, step, 0)

        w_in_g = dict(other=wi_other, mine=wi_mine, recv=wi_recv, send=wi_send, rb=wi_rb, p1_sems=(2, 3, 4, 5), p2_sem=12,
                      p1=[None] * NCHIP, wait_load=lambda s: dz_wait(slabs[s]), load=lambda s: dz_load(slabs[s]),
                      dot_other=lambda: _dot_tn(hn_other[...], dzbuf[...]), dot_mine=lambda: _dot_tn(hn_mine[...], dzbuf[...]))
        w_out_g = dict(other=wo_other, mine=wo_mine, recv=wo_recv, send=wo_send, rb=wo_rb, p1_sems=(1, 24, 25, 26), p2_sem=9,
                       p1=[None] * NCHIP, wait_load=lambda s: y_copy(s).wait(), load=lambda s: y_copy(s).start(),
                       dot_other=lambda: _dot_tn(y_blk[...], do_other[...]), dot_mine=lambda: _dot_tn(y_blk[...], do_mine[...]))

        def piece_matmuls(g, s):
            if s >= 2:
                g["p1"][s - 2].wait_send()
            g["wait_load"](s)
            g["other"][s % 2] = g["dot_other"]().astype(BF16)
            g["p1"][s] = copy(g["p1_sems"][s], g["other"].at[s % 2], g["recv"].at[s], sib)
            g["p1"][s].start()
            g["mine"][s % 2] = g["dot_mine"]()
            if s + 1 < NCHIP:
                g["load"](s + 1)

        def piece_finish(g, s):
            mine, recv, send, rb, p2 = g["mine"].at[s % 2], g["recv"].at[s], g["send"], g["rb"], g["p2_sem"]
            nrows = mine.shape[0]
            copy(g["p1_sems"][s], recv, recv, sib).wait_recv()
            if s == 3:
                pair_sum(mine, recv, None, nrows, True)
                return
            if s == 0:
                pair_sum(mine, recv, send.at[0], nrows, False)
                start(copy(p2, send.at[0], rb.at[0], near))
                return
            adds_relayed = c == (1 if s == 1 else 0)

            @pl.when(adds_relayed)
            def _():
                copy(p2, rb.at[0], rb.at[0], sib).wait_recv()
                pair_sum(mine, recv, send.at[s], nrows, False, rb.at[0])

            @pl.when(jnp.logical_not(adds_relayed))
            def _():
                pair_sum(mine, recv, send.at[s], nrows, False)
            start(copy(p2 + s, send.at[s], rb.at[s], (*chips[s - 1], c)))

        def piece_total(g):
            for k in (1, 2):
                copy(g["p2_sem"] + k, g["rb"].at[k], g["rb"].at[k], sib).wait_recv()
            chip_sum(g["mine"].at[1], g["rb"], g["mine"].shape[1], (1, 2))

        for cp in do_loads:
            cp.wait()
        piece_matmuls(w_out_g, 0)
        piece_matmuls(w_out_g, 1)
        piece_finish(w_out_g, 0)
        piece_matmuls(w_out_g, 2)
        piece_finish(w_out_g, 1)
        piece_matmuls(w_out_g, 3)
        piece_finish(w_out_g, 2)

        for cp in hn_loads:
            cp.wait()
        piece_matmuls(w_in_g, 0)

        l_pk.wait()
        copy(0, pk_recv, pk_recv, sib).wait_recv()
        pair_sum(pk_mine, pk_recv, pk_send, hp, True)
        for k in range(3):
            start(copy(6 + k, pk_send.at[:, pl.ds(_mo(cj[k] * 128, 128), 128)], pk_rb.at[k], (*chips[k], c)))
        l_pk_own = local(6, pk_mine.at[:, pl.ds(_mo(j * 128, 128), 128)], pk_own)

        piece_matmuls(w_in_g, 1)
        piece_finish(w_in_g, 0)

        l_pk_own.wait()
        for k in range(3):
            copy(6 + k, pk_rb.at[k], pk_rb.at[k], sib).wait_recv()
        chip_sum(pk_own, pk_rb, hp)
        l_pk_out = local(8, pk_own, pk_piece(pk_out, c, j))
        start(copy(15, pk_own, pk_piece(pk_out, c, j), sib))
        for k in range(2):
            start(copy(16 + k, pk_own, pk_piece(pk_out, c, j), (*chips[k], c)))

        piece_matmuls(w_in_g, 2)
        piece_finish(w_in_g, 1)
        piece_matmuls(w_in_g, 3)
        piece_finish(w_in_g, 2)

        piece_finish(w_out_g, 3)
        piece_total(w_out_g)
        l_wo_out = local(9, wo_mine.at[1], cols_half(wo_out, c, 512))
        start(copy(22, wo_mine.at[1], cols_half(wo_out, c, 512), sib))

        far = (jnp.where(c == 0, x, 1 - x), jnp.where(c == 0, 1 - y, y), c)
        for step, k in enumerate([c, 1 - c, 2]):
            reg = pk_piece(pk_out, c, jnp.where(k == 0, cj[0], jnp.where(k == 1, cj[1], cj[2])))
            copy(16 + k, reg, reg, sib).wait_recv()
            if step == 0:
                start(copy(18, reg, reg, far))
            start(copy(19 + k, reg, reg, sib))

        piece_finish(w_in_g, 3)
        piece_total(w_in_g)
        l_wi_out = local(10, wi_mine.at[1], rows_half(wi_out, c, 512))
        start(copy(23, wi_mine.at[1], rows_half(wi_out, c, 512), sib))

        reg = pk_piece(pk_out, 1 - c, j)
        copy(15, reg, reg, sib).wait_recv()
        for k in range(3):
            reg = pk_piece(pk_out, 1 - c, cj[k])
            copy(19 + k, reg, reg, sib).wait_recv()
        reg = cols_half(wo_out, 1 - c, 512)
        copy(22, reg, reg, sib).wait_recv()
        reg = rows_half(wi_out, 1 - c, 512)
        copy(23, reg, reg, sib).wait_recv()
        for cp in sends + w_in_g["p1"][2:] + w_out_g["p1"][2:]:
            cp.wait_send()
        for cp in (l_pk_out, l_wo_out, l_wi_out):
            cp.wait()

    return pl.pallas_call(
        body, name="grads_reduce",
        out_shape=(jax.ShapeDtypeStruct((D, wi_w), F32), jax.ShapeDtypeStruct((512, D), F32),
                   jax.ShapeDtypeStruct(pack.shape, F32)),
        in_specs=[ANY] * (4 + n_dz), out_specs=(ANY,) * 3,
        scratch_shapes=[
            pltpu.VMEM((lt, 512), BF16), pltpu.VMEM((lt, 512), BF16), pltpu.VMEM((lt, wi_w), BF16),
            pltpu.VMEM((2, 512, wi_w), BF16), pltpu.VMEM((2, 512, wi_w), F32), pltpu.VMEM((4, 512, wi_w), BF16),
            pltpu.VMEM((3, 512, wi_w), BF16), pltpu.VMEM((3, 512, wi_w), BF16),
            pltpu.VMEM((lx, 512), BF16), pltpu.VMEM((lx, 512), BF16), pltpu.VMEM((lx, 512), BF16),
            pltpu.VMEM((2, 512, 512), BF16), pltpu.VMEM((2, 512, 512), F32), pltpu.VMEM((4, 512, 512), BF16),
            pltpu.VMEM((3, 512, 512), BF16), pltpu.VMEM((3, 512, 512), BF16),
            pltpu.VMEM((hp, 512), F32), pltpu.VMEM((hp, 512), F32), pltpu.VMEM((hp, 512), BF16),
            pltpu.VMEM((3, hp, 128), BF16), pltpu.VMEM((hp, 128), F32),
            pltpu.SemaphoreType.DMA((27,)), pltpu.SemaphoreType.DMA((27,)), pltpu.SemaphoreType.DMA((15,))],
        compiler_params=_cp(vmem_mb=56),
    )(hn, *dzs, y, do, pack)


def _ada_bwd(c_all, dmx_all_j, dmc_j, dmx_all, dmc, c_ctx, ada_w_full):
    def body(c_ref, dmxj_ref, dmcj_ref, dmx_ref, dmc_ref, cc_ref, w_ref, gw_ref, gb_ref, gc_ref, lhs, rhs, dm8):
        lhs[...] = jnp.zeros_like(lhs)
        rhs[...] = jnp.zeros_like(rhs)
        cv = c_ref[...]
        lhs[0:8, :] = cv * _sigmoid(cv)
        cc = cc_ref[...]
        a_c, da_c = _silu_and_grad(cc)
        lhs[8:9, :] = a_c
        rhs[0:8, :] = dmxj_ref[...]
        rhs[8:9, :] = dmcj_ref[...]
        gw_ref[...] = _dot_tn(lhs[...].astype(BF16), rhs[...].astype(BF16))
        gb_ref[...] = jnp.sum(dmx_ref[...], axis=0, keepdims=True) + dmc_ref[...]
        dm8[...] = jnp.zeros_like(dm8)
        dm8[0:1, :] = dmc_ref[...]
        da = _dot_nt(dm8[...].astype(BF16), w_ref[...])
        gc_ref[...] = da[0:1, :] * da_c

    return pl.pallas_call(
        body, name="ada_bwd",
        out_shape=(jax.ShapeDtypeStruct((D, 768), F32), jax.ShapeDtypeStruct((1, 3 * D), F32),
                   jax.ShapeDtypeStruct((1, D), F32)),
        in_specs=[VMEM] * 7, out_specs=(VMEM,) * 3,
        scratch_shapes=[pltpu.VMEM((16, D), F32), pltpu.VMEM((16, 768), F32), pltpu.VMEM((8, 3 * D), F32)],
        compiler_params=_cp(vmem_mb=32),
    )(c_all, dmx_all_j, dmc_j, dmx_all, dmc, c_ctx, ada_w_full)


def _proj(x, ctx, mods, norm_g, w_full, sgu):
    lx, lc = x.shape[0], ctx.shape[0]
    assert lc == T
    n = 1 + lx // T

    def body(x_ref, c_ref, sh_ref, sc_ref, ng_ref, w0, w1, w2, w3, w4, g_ref, b_ref, sw_ref, bt_ref,
             hn_ref, xa_ref, ga_ref, u_ref, v_ref, gb_ref, ys_ref, mixed_s):
        i = pl.program_id(0)
        is_ctx = i == 0
        xv = jnp.where(is_ctx, c_ref[...], x_ref[...])
        sc = jnp.where(is_ctx, sc_ref[1:2, :], sc_ref[0:1, :])
        sh = jnp.where(is_ctx, sh_ref[1:2, :], sh_ref[0:1, :])
        r = lax.rsqrt(jnp.mean(xv * xv, axis=-1, keepdims=True) + NORM_EPS)
        hb = ((xv * r) * ng_ref[...] * (1.0 + sc) + sh).astype(BF16)
        hn_ref[...] = hb
        xa_ref[...] = _dot(hb, w0[...])

        @pl.when(i > 0)
        def _():
            u_ref[...] = _dot(hb, w2[...])
            v_ref[...] = _dot(hb, w3[...])
            for ch in range(T // HD):
                rows = slice(HD * ch, HD * ch + HD)
                ug = _sgu_parts(u_ref[rows, :], v_ref[rows, :], g_ref[...], b_ref[...], sw_ref, bt_ref, mixed_s)[0]
                ys_ref[rows, :] = ug * mixed_s[...]
            ga_ref[...] = _dot(hb, w1[...])
            gb_ref[...] = _dot(hb, w4[...])

    every = pl.BlockSpec((T, D), lambda i: (i, 0))
    lat = pl.BlockSpec((T, D), lambda i: (jnp.maximum(i - 1, 0), 0))
    vec = pl.BlockSpec((1, D), lambda i: (0, 0))
    in_specs = [lat, pl.BlockSpec((T, D), lambda i: (0, 0)), pl.BlockSpec((8, D), lambda i: (0, 0)),
                pl.BlockSpec((8, D), lambda i: (0, 1)), vec]
    in_specs += [pl.BlockSpec((D, D), lambda i, k=k: (0, k)) for k in range(5)]
    in_specs += [vec, vec, pl.BlockSpec((NH, HD, HD), lambda i: (0, 0, 0)), pl.BlockSpec((HD, NH), lambda i: (0, 0))]
    full_s = jax.ShapeDtypeStruct((lc + lx, D), F32)
    lat_s = jax.ShapeDtypeStruct((lx, D), F32)
    return pl.pallas_call(
        body, name="proj", grid=(n,),
        out_shape=(jax.ShapeDtypeStruct((lc + lx, D), BF16), full_s, lat_s, lat_s, lat_s, lat_s, lat_s),
        in_specs=in_specs, out_specs=(every, every, lat, lat, lat, lat, lat),
        scratch_shapes=[pltpu.VMEM((HD, D), F32)], compiler_params=_cp(1, vmem_mb=56),
    )(x, ctx, mods, mods, norm_g, *([w_full] * 5), *sgu)


def _tile_specs(rows_per_pos, width, n_tiles, tile):
    last = n_tiles * (T // 8) - 1
    r = rows_per_pos
    return [pl.BlockSpec((T * r, width), lambda i: (tile(i), 0)),
            pl.BlockSpec((8 * r, width), lambda i: (jnp.maximum(tile(i) * (T // 8) - 1, 0), 0)),
            pl.BlockSpec((8 * r, width), lambda i: (jnp.minimum((tile(i) + 1) * (T // 8), last), 0))]


def _has_prev(tile):
    return tile >= 2


def _has_next(tile, nt):
    return jnp.logical_and(tile >= 1, tile < nt - 1)


ZT = pl.BlockSpec((T * NH, HD), lambda i: (i, 0))
CONV_CHUNK = 32


SCAN_SUB = 8


def _scan_tile(chains, post, carry_ref):
    blk = T // SCAN_SUB

    def step(k, state):
        new = []
        for ci, (a_ref, x_ref, o_ref, q_ref, reverse, xscale) in enumerate(chains):
            for q in range(SCAN_SUB):
                s, p = state[ci * SCAN_SUB + q]
                t = (q + 1) * blk - 1 - k if reverse else q * blk + k
                r = pl.ds(_mo(t * NH, NH), NH)
                a = a_ref[r, :]
                x = x_ref[r, :] if xscale is None else x_ref[r, :] * xscale
                if post:
                    o = x + s
                    o_ref[r, :] = o
                    q_ref[r, :] = p
                    new.append((a * o, a * p))
                else:
                    o = a * s + x
                    p = a * p
                    o_ref[r, :] = o
                    q_ref[r, :] = p
                    new.append((o, p))
        return tuple(new)

    zero = jnp.zeros((NH, HD), F32)
    one = jnp.ones((NH, HD), F32)
    final = lax.fori_loop(0, blk, step, tuple((zero, one) for _ in range(len(chains) * SCAN_SUB)))
    for ci, (a_ref, x_ref, o_ref, q_ref, reverse, xscale) in enumerate(chains):
        carry = carry_ref[ci]
        for q in (range(SCAN_SUB - 1, -1, -1) if reverse else range(SCAN_SUB)):
            rows = pl.ds(q * blk * NH, blk * NH)
            fixed = o_ref[rows, :].reshape(blk, NH, HD) + q_ref[rows, :].reshape(blk, NH, HD) * carry[None]
            o_ref[rows, :] = fixed.reshape(blk * NH, HD)
            s_loc, p_loc = final[ci * SCAN_SUB + q]
            carry = s_loc + p_loc * carry
        carry_ref[ci] = carry


def _lru_fwd(xa, conv_wz, conv_bz, wcat, bcat, lamcat, name):
    lx = xa.shape[0]
    n = lx // T
    tile_u = lambda i: i
    tile_d = lambda i: jnp.where(i == 0, 0, n - i)

    def body(xm_u, xp_u, xn_u, xm_d, xp_d, xn_d, cw, cb, w_ref, b_ref, lam_ref,
             xcz_o, af_o, ab_o, hf_o, hb_o, gf_o, gb_o, fu, fd, pad, xc_d, x_u, x_d, q_u, q_d, carry):
        i = pl.program_id(0)

        @pl.when(i == 0)
        def _():
            carry[...] = jnp.zeros_like(carry)

        def conv_gates(xm, xp, xn, tile, d, xc_ref, a_ref, x_ref, g_ref):
            pmask = jnp.where(_has_prev(tile), 1.0, 0.0)
            nmask = jnp.where(_has_next(tile, n), 1.0, 0.0)
            for h in range(NH):
                cols = slice(HD * h, HD * h + HD)
                pad[_zrows(h, 8), :] = xp[:, cols] * pmask
                pad[pl.ds(8 * NH + h, T, stride=NH), :] = xm[:, cols]
                pad[pl.ds((T + 8) * NH + h, 8, stride=NH), :] = xn[:, cols] * nmask

            def conv_chunk(ci, c_):
                base = pl.multiple_of(ci * (CONV_CHUNK * NH), CONV_CHUNK * NH)
                acc = None
                for k in range(4):
                    sl = pad[pl.ds(base + (7 + k) * NH, CONV_CHUNK * NH), :].reshape(CONV_CHUNK, NH, HD)
                    term = sl * cw[k][None]
                    acc = term if acc is None else acc + term
                acc = acc + cb[...][None]
                xc_ref[pl.ds(base, CONV_CHUNK * NH), :] = acc.reshape(CONV_CHUNK * NH, HD)
                return c_
            lax.fori_loop(0, T // CONV_CHUNK, conv_chunk, 0)

            for h in range(NH):
                xch = xc_ref[_zrows(h, T), :]
                pre = _dot(xch.astype(BF16), w_ref[h, :, 256 * d:256 * d + 256]) + b_ref[h:h + 1, 256 * d:256 * d + 256]
                r, gi, _, _, a, mult = _lru_gate(pre, lam_ref[h:h + 1, :], d, 0)
                a_ref[_zrows(h, T), :] = a
                x_ref[_zrows(h, T), :] = mult * gi * xch
                for q, val in enumerate((r, gi, mult)):
                    g_ref[:, q * D + HD * h:q * D + HD * h + HD] = val

        conv_gates(xm_u, xp_u, xn_u, tile_u(i), 0, xcz_o, af_o, x_u, gf_o)
        conv_gates(xm_d, xp_d, xn_d, tile_d(i), 1, xc_d, ab_o, x_d, gb_o)

        _scan_tile([(af_o, x_u, hf_o, q_u, False, None), (ab_o, x_d, hb_o, q_d, True, None)], False, carry)

        @pl.when(i == 0)
        def _():
            fu[...] = carry[0]
            fd[...] = carry[1]

    full = lambda shape: pl.BlockSpec(shape, lambda i: (0,) * len(shape))
    st = full((NH, HD))
    in_specs = _tile_specs(1, D, n, tile_u) + _tile_specs(1, D, n, tile_d)
    in_specs += [full((4, NH, HD)), st, full((NH, HD, 4 * HD)), full((NH, 4 * HD)), full((NH, 2 * HD))]
    up = pl.BlockSpec((T * NH, HD), lambda i: (tile_u(i), 0))
    dn = pl.BlockSpec((T * NH, HD), lambda i: (tile_d(i), 0))
    zs = jax.ShapeDtypeStruct((lx * NH, HD), F32)
    ss = jax.ShapeDtypeStruct((NH, HD), F32)
    zbuf = pltpu.VMEM((T * NH, HD), F32)
    gs = jax.ShapeDtypeStruct((lx, 3 * D), F32)
    g_up = pl.BlockSpec((T, 3 * D), lambda i: (tile_u(i), 0))
    g_dn = pl.BlockSpec((T, 3 * D), lambda i: (tile_d(i), 0))
    return pl.pallas_call(
        body, name=name, grid=(n,), out_shape=(zs,) * 5 + (gs, gs, ss, ss), in_specs=in_specs,
        out_specs=(up, up, dn, up, dn, g_up, g_dn, st, st),
        scratch_shapes=[pltpu.VMEM(((T + 16) * NH, HD), F32), zbuf, zbuf, zbuf, zbuf, zbuf,
                        pltpu.VMEM((2, NH, HD), F32)],
        compiler_params=_cp(1, vmem_mb=48),
    )(xa, xa, xa, xa, xa, xa, conv_wz, conv_bz, wcat, bcat, lamcat)


def _sgu_parts(u, v, lng, lnb, w_ref, bt_ref, mixed_s):
    ug, dug = _gelu_and_grad(u)
    vg, dvg = _gelu_and_grad(v)
    mu = jnp.mean(vg, axis=-1, keepdims=True)
    vc = vg - mu
    rstd = lax.rsqrt(jnp.mean(vc * vc, axis=-1, keepdims=True) + LN_EPS)
    vh = vc * rstd
    vn = (vh * lng + lnb).astype(BF16)
    for g in range(NH):
        cols = slice(HD * g, HD * g + HD)
        mixed_s[:, cols] = _dot(w_ref[g], vn[:, cols]) + bt_ref[:, g:g + 1]
    return ug, dug, dvg, rstd, vh, vn


def _sgu_bwd_chunk(u, v, dys_v, lng, lnb, w_ref, bt_ref, mixed_s, dvn_s, dw_ref, db_ref, dg_ref, dbl_ref):
    ug, dug, dvg, rstd, vh, vn = _sgu_parts(u, v, lng, lnb, w_ref, bt_ref, mixed_s)
    du = (dys_v * mixed_s[...] * dug).astype(BF16)
    dmix = dys_v * ug
    ones = jnp.ones((8, HD), BF16)
    for g in range(NH):
        cols = slice(HD * g, HD * g + HD)
        dm = dmix[:, cols]
        hi = dm.astype(BF16)
        lo = (dm - hi.astype(F32)).astype(BF16)
        dw_ref[g] += _dot_nt(hi, vn[:, cols])
        db_ref[g:g + 1, :] += (_dot_nt(ones, hi) + _dot_nt(ones, lo))[0:1, :]
        dvn_s[:, cols] = _dot_tn(w_ref[g], hi)
    dvn = dvn_s[...]
    dg_ref[...] += jnp.sum(dvn * vh, axis=0, keepdims=True)
    dbl_ref[...] += jnp.sum(dvn, axis=0, keepdims=True)
    dvh = dvn * lng
    dvg_in = rstd * (dvh - jnp.mean(dvh, axis=-1, keepdims=True) - vh * jnp.mean(dvh * vh, axis=-1, keepdims=True))
    return du, (dvg_in * dvg).astype(BF16)


def _out_fwd_bwd(hf_z, hb_z, ga, gb, ys, x, tgt, mods, final_g, w_out_full):
    lx = x.shape[0]
    n = lx // T

    def body(hf_ref, hb_ref, ga_ref, gb_ref, ys_ref, x_ref, t_ref, gx_ref, fg_ref, w_ref,
             loss_ref, dfg_ref, dgx_ref, dxn_ref, y_ref, do_ref, dga_ref, dgb_ref, dyl_ref, dys_ref, yl_s):
        i = pl.program_id(0)

        @pl.when(i == 0)
        def _():
            loss_ref[...] = jnp.zeros_like(loss_ref)
            dfg_ref[...] = jnp.zeros_like(dfg_ref)
            dgx_ref[...] = jnp.zeros_like(dgx_ref)

        for h in range(NH):
            yl_s[:, HD * h:HD * h + HD] = hf_ref[_zrows(h, T), :] + hb_ref[_zrows(h, T), :]
        yl = yl_s[...]
        gav = ga_ref[...]
        gbv = gb_ref[...]
        sa, dsa = _silu_and_grad(gav)
        sb, dsb = _silu_and_grad(gbv)
        ysv = ys_ref[...]
        y_ref[:, 0:D] = (yl * sa).astype(BF16)
        y_ref[:, D:2 * D] = (ysv * sb).astype(BF16)
        o = _dot(y_ref[...], w_ref[...])
        gx = gx_ref[0:1, :]
        xnew = x_ref[...] + gx * o
        r2 = lax.rsqrt(jnp.mean(xnew * xnew, axis=-1, keepdims=True) + NORM_EPS)
        xh = xnew * r2
        fg = fg_ref[...]
        err = xh * fg - t_ref[...]
        loss_ref[...] += 0.5 * jnp.sum(jnp.mean(err * err, axis=-1, keepdims=True), axis=0, keepdims=True)

        @pl.when(i == n - 1)
        def _():
            lp = loss_ref[...]
            lp1 = lp.astype(BF16).astype(F32)
            lp2 = (lp - lp1).astype(BF16).astype(F32)
            lp3 = (lp - lp1 - lp2).astype(BF16).astype(F32)
            lane = lax.broadcasted_iota(jnp.int32, lp.shape, 1)
            loss_ref[...] = jnp.where(lane == 0, lp1, jnp.where(lane == 1, lp2, jnp.where(lane == 2, lp3, 0.0)))
        dout = err * (1.0 / D)
        dfg_ref[...] += jnp.sum(dout * xh, axis=0, keepdims=True)
        dxh = dout * fg
        dxn = r2 * (dxh - xh * jnp.mean(dxh * xh, axis=-1, keepdims=True))
        dxn_ref[...] = dxn
        dgx_ref[...] += jnp.sum(dxn * o, axis=0, keepdims=True)
        do = (dxn * gx).astype(BF16)
        do_ref[...] = do
        dy = _dot_nt(do, w_ref[...])
        dy1 = dy[:, 0:D]
        dy2 = dy[:, D:2 * D]
        dga_ref[...] = (dy1 * yl * dsa).astype(BF16)
        dgb_ref[...] = (dy2 * ysv * dsb).astype(BF16)
        dys_ref[...] = dy2 * sb
        yl_s[...] = dy1 * sa
        for h in range(NH):
            dyl_ref[_zrows(h, T), :] = yl_s[:, HD * h:HD * h + HD]

    row = pl.BlockSpec((T, D), lambda i: (i, 0))
    vec = pl.BlockSpec((1, D), lambda i: (0, 0))
    zlat = pl.BlockSpec((T * NH, HD), lambda i: (i + 1, 0))
    in_specs = [zlat, zlat, row, row, row, row, row, pl.BlockSpec((8, D), lambda i: (0, 2)), vec,
                pl.BlockSpec((2 * D, D), lambda i: (0, 0))]
    out_shape = (jax.ShapeDtypeStruct((1, D), F32), jax.ShapeDtypeStruct((1, D), F32), jax.ShapeDtypeStruct((1, D), F32),
                 jax.ShapeDtypeStruct((lx, D), F32), jax.ShapeDtypeStruct((lx, 2 * D), BF16),
                 jax.ShapeDtypeStruct((lx, D), BF16), jax.ShapeDtypeStruct((lx, D), BF16),
                 jax.ShapeDtypeStruct((lx, D), BF16), jax.ShapeDtypeStruct((lx * NH, HD), F32),
                 jax.ShapeDtypeStruct((lx, D), F32))
    out_specs = (vec, vec, vec, row, pl.BlockSpec((T, 2 * D), lambda i: (i, 0)),
                 row, row, row, ZT, row)
    return pl.pallas_call(
        body, name="out_fwd_bwd", grid=(n,), out_shape=out_shape, in_specs=in_specs, out_specs=out_specs,
        scratch_shapes=[pltpu.VMEM((T, D), F32)],
        compiler_params=_cp(1, vmem_mb=56),
    )(hf_z, hb_z, ga, gb, ys, x, tgt, mods, final_g, w_out_full)


def _lru_bwd(xc_z, dy_z, hf_z, hb_z, af_z, ab_z, gf, gb, s_b, wcat, lamcat, name):
    lx = xc_z.shape[0] // NH
    n = lx // T
    tile_u = lambda i: jnp.where(i == n - 1, 0, i + 1)
    tile_d = lambda i: n - 1 - i

    def body(xc_u, dy_u, hb_ref, hbn_ref, ab_ref, gb_ref, xc_d, dy_d, hf_ref, hfp_ref, af_ref, gf_ref,
             sb_ref, w_ref, lam_ref, dxcb_ref, dxcf_ref, dw_ref, db_ref, dl_ref,
             lb_s, lf_s, q_u, q_d, pf_s, pb_s, dpre_s, carry):
        i = pl.program_id(0)
        tu, td = tile_u(i), tile_d(i)

        @pl.when(i == 0)
        def _():
            dw_ref[...] = jnp.zeros_like(dw_ref)
            db_ref[...] = jnp.zeros_like(db_ref)
            dl_ref[...] = jnp.zeros_like(dl_ref)
            carry[...] = jnp.zeros_like(carry)

        _scan_tile([(ab_ref, dy_u, lb_s, q_u, False, jnp.where(tu == 0, 0.0, 1.0)),
                    (af_ref, dy_d, lf_s, q_d, True, jnp.where(td == 0, 0.0, 1.0))], True, carry)
        zero = jnp.zeros((NH, HD), F32)
        pb_s[pl.ds(0, T * NH), :] = hb_ref[...]
        pb_s[pl.ds(T * NH, NH), :] = jnp.where(tu == n - 1, sb_ref[...], jnp.where(tu == 0, zero, hbn_ref[pl.ds(0, NH), :]))
        pf_s[pl.ds(0, NH), :] = jnp.where(td == 0, zero, hfp_ref[pl.ds(7 * NH, NH), :])
        pf_s[pl.ds(NH, T * NH), :] = hf_ref[...]
        sides = ((1, xc_u, lb_s, pb_s, NH, ab_ref, gb_ref, dxcb_ref), (0, xc_d, lf_s, pf_s, 0, af_ref, gf_ref, dxcf_ref))
        for d, xc_ref, adj_s, prev_s, prev_off, a_ref, g_ref, dxc_ref in sides:
            wcols = slice(256 * d, 256 * d + 256)
            for h in range(NH):
                xch = xc_ref[_zrows(h, T), :]
                xcb = xch.astype(BF16)
                r, gi, mult = (g_ref[:, q * D + HD * h:q * D + HD * h + HD] for q in range(3))
                a = a_ref[_zrows(h, T), :]
                lam = lam_ref[h:h + 1, HD * d:HD * d + HD]
                sp = _softplus(-lam)
                du = adj_s[_zrows(h, T), :]
                da = du * prev_s[pl.ds(prev_off + h, T, stride=NH), :]
                dgi = du * mult * xch
                dmult = du * gi * xch
                dla = da * a - dmult * (a * a) / mult
                dr = dla * ((-LRU_C) * sp)
                dsp = jnp.sum(dla * ((-LRU_C) * r), axis=0, keepdims=True)
                dl_ref[h:h + 1, HD * d:HD * d + HD] += dsp * (-_sigmoid(-lam))
                dpre_s[:, 0:HD] = dr * r * (1.0 - r)
                dpre_s[:, HD:2 * HD] = dgi * gi * (1.0 - gi)
                dpre = dpre_s[...]
                dpb = dpre.astype(BF16)
                dw_ref[h, :, wcols] += _dot_tn(xcb, dpb)
                db_ref[h:h + 1, wcols] += jnp.sum(dpre, axis=0, keepdims=True)
                dxc_ref[_zrows(h, T), :] = du * mult * gi + _dot_nt(dpb, w_ref[h, :, wcols])

    full = lambda shape: pl.BlockSpec(shape, lambda i: (0,) * len(shape))
    wsp, bsp, lsp = full((NH, HD, 4 * HD)), full((NH, 4 * HD)), full((NH, 2 * HD))
    st = full((NH, HD))
    up = pl.BlockSpec((T * NH, HD), lambda i: (tile_u(i), 0))
    dn = pl.BlockSpec((T * NH, HD), lambda i: (tile_d(i), 0))
    dy_up = pl.BlockSpec((T * NH, HD), lambda i: (jnp.maximum(tile_u(i) - 1, 0), 0))
    dy_dn = pl.BlockSpec((T * NH, HD), lambda i: (jnp.maximum(tile_d(i) - 1, 0), 0))
    nxt = _tile_specs(NH, HD, n, tile_u)[2]
    prv = _tile_specs(NH, HD, n, tile_d)[1]
    g_up = pl.BlockSpec((T, 3 * D), lambda i: (tile_u(i), 0))
    g_dn = pl.BlockSpec((T, 3 * D), lambda i: (tile_d(i), 0))
    zs = jax.ShapeDtypeStruct((lx * NH, HD), F32)
    zbuf = pltpu.VMEM((T * NH, HD), F32)
    zbuf1 = pltpu.VMEM(((T + 1) * NH, HD), F32)
    return pl.pallas_call(
        body, name=name, grid=(n,),
        out_shape=(zs, zs, jax.ShapeDtypeStruct((NH, HD, 4 * HD), F32), jax.ShapeDtypeStruct((NH, 4 * HD), F32),
                   jax.ShapeDtypeStruct((NH, 2 * HD), F32)),
        in_specs=[up, dy_up, up, nxt, up, g_up, dn, dy_dn, dn, prv, dn, g_dn, st, wsp, lsp],
        out_specs=(up, dn, wsp, bsp, lsp),
        scratch_shapes=[zbuf, zbuf, zbuf, zbuf, zbuf1, zbuf1, pltpu.VMEM((T, 2 * HD), F32),
                        pltpu.VMEM((2, NH, HD), F32)],
        compiler_params=_cp(1, vmem_mb=56),
    )(xc_z, dy_z, hb_z, hb_z, ab_z, gb, xc_z, dy_z, hf_z, hf_z, af_z, gf, s_b, wcat, lamcat)


def _conv_bwd(dxc_a, dxc_b, xa, conv_wz, dcw0, dcb0, name):
    lx = dxc_a.shape[0] // NH
    n = lx // T

    def body(dm_a, dp_a, dn_a, dm_b, dp_b, dn_b, xan_ref, cw, dcw0_ref, dcb0_ref, dxa_ref, dcw_ref, dcb_ref,
             pad, dxa_s, xa_ref):
        i = pl.program_id(0)

        @pl.when(i == 0)
        def _():
            dcw_ref[...] = dcw0_ref[...]
            dcb_ref[...] = dcb0_ref[...]

        for h in range(NH):
            xa_ref[_zrows(h, T), :] = xan_ref[:, HD * h:HD * h + HD]

        pmask = jnp.where(_has_prev(i), 1.0, 0.0)
        nmask = jnp.where(_has_next(i, n), 1.0, 0.0)
        pad[pl.ds(0, 8 * NH), :] = (dp_a[...] + dp_b[...]) * pmask
        pad[pl.ds(8 * NH, T * NH), :] = dm_a[...] + dm_b[...]
        pad[pl.ds((T + 8) * NH, 8 * NH), :] = (dn_a[...] + dn_b[...]) * nmask

        def chunk(ci, carry):
            base = pl.multiple_of(ci * (CONV_CHUNK * NH), CONV_CHUNK * NH)
            xav = xa_ref[pl.ds(base, CONV_CHUNK * NH), :].reshape(CONV_CHUNK, NH, HD)
            acc = None
            for k in range(4):
                sl = pad[pl.ds(base + (9 - k) * NH, CONV_CHUNK * NH), :].reshape(CONV_CHUNK, NH, HD)
                term = sl * cw[k][None]
                acc = term if acc is None else acc + term
                dcw_ref[k] += jnp.sum(sl * xav, axis=0)
                if k == 1:
                    dcb_ref[...] += jnp.sum(sl, axis=0)
            dxa_s[pl.ds(base, CONV_CHUNK * NH), :] = acc.reshape(CONV_CHUNK * NH, HD)
            return carry
        lax.fori_loop(0, T // CONV_CHUNK, chunk, 0)
        for h in range(NH):
            dxa_ref[:, HD * h:HD * h + HD] = dxa_s[_zrows(h, T), :].astype(BF16)

    full = lambda shape: pl.BlockSpec(shape, lambda i: (0,) * len(shape))
    return pl.pallas_call(
        body, name=name, grid=(n,),
        out_shape=(jax.ShapeDtypeStruct((lx, D), BF16), jax.ShapeDtypeStruct((4, NH, HD), F32),
                   jax.ShapeDtypeStruct((NH, HD), F32)),
        in_specs=_tile_specs(NH, HD, n, lambda i: i) * 2 + [pl.BlockSpec((T, D), lambda i: (i, 0)), full((4, NH, HD)),
                                                            full((4, NH, HD)), full((NH, HD))],
        out_specs=(pl.BlockSpec((T, D), lambda i: (i, 0)), full((4, NH, HD)), full((NH, HD))),
        scratch_shapes=[pltpu.VMEM(((T + 16) * NH, HD), F32), pltpu.VMEM((T * NH, HD), F32),
                        pltpu.VMEM((T * NH, HD), F32)],
        compiler_params=_cp(1, vmem_mb=48),
    )(dxc_a, dxc_a, dxc_a, dxc_b, dxc_b, dxc_b, xa, conv_wz, dcw0, dcb0)


def _proj_bwd(dxa, dga, dgb, x, ctx, dxn, mods, norm_g, w_full, sgu):
    lx, lc = x.shape[0], ctx.shape[0]
    assert lc == T
    n = 1 + lx // T

    def body(dxa_ref, dga_ref, dgb_ref, w0, w1, w4, w2, w3, x_ref, c_ref, sc_ref, ng_ref, dxn_ref,
             u_ref, v_ref, dy_ref, g_ref, b_ref, sw_ref, bt_ref,
             gx_ref, dng_ref, dscx_ref, dshx_ref, dscc_ref, dshc_ref, du_ref, dv_ref, dws_ref, dbs_ref, dlg_ref, dlb_ref,
             mixed_s, dvn_s):
        i = pl.program_id(0)
        is_ctx = i == 0

        @pl.when(is_ctx)
        def _():
            for acc in (dng_ref, dscx_ref, dshx_ref, dscc_ref, dshc_ref, dws_ref, dbs_ref, dlg_ref, dlb_ref):
                acc[...] = jnp.zeros_like(acc)

        xv = jnp.where(is_ctx, c_ref[...], x_ref[...])
        sc1 = 1.0 + jnp.where(is_ctx, sc_ref[1:2, :], sc_ref[0:1, :])
        r = lax.rsqrt(jnp.mean(xv * xv, axis=-1, keepdims=True) + NORM_EPS)
        xn = xv * r
        ng = ng_ref[...]

        def norm_bwd(dhn, dsc_ref, dsh_ref, with_x):
            t = dhn * xn
            dng_ref[...] += jnp.sum(t * sc1, axis=0, keepdims=True)
            dsc_ref[...] += jnp.sum(t * ng, axis=0, keepdims=True)
            dsh_ref[...] += jnp.sum(dhn, axis=0, keepdims=True)
            if with_x:
                dxh = dhn * (ng * sc1)
                gx_ref[...] = dxn_ref[...] + r * (dxh - xn * jnp.mean(dxh * xn, axis=-1, keepdims=True))

        @pl.when(is_ctx)
        def _():
            norm_bwd(_dot_nt(dxa_ref[...], w0[...]), dscc_ref, dshc_ref, False)

        @pl.when(i > 0)
        def _():
            def sgu_chunk(ch):
                rows = slice(HD * ch, HD * ch + HD)
                du, dv = _sgu_bwd_chunk(u_ref[rows, :], v_ref[rows, :], dy_ref[rows, :], g_ref[...], b_ref[...],
                                        sw_ref, bt_ref, mixed_s, dvn_s, dws_ref, dbs_ref, dlg_ref, dlb_ref)
                du_ref[rows, :] = du
                dv_ref[rows, :] = dv

            dhn = _dot_nt(dxa_ref[...], w0[...])
            sgu_chunk(0)
            dhn = dhn + _dot_nt(dga_ref[...], w1[...])
            for ch in range(1, T // HD):
                sgu_chunk(ch)
            dhn = dhn + _dot_nt(dgb_ref[...], w4[...])
            dhn = dhn + _dot_nt(du_ref[...], w2[...]) + _dot_nt(dv_ref[...], w3[...])
            norm_bwd(dhn, dscx_ref, dshx_ref, True)

    every = pl.BlockSpec((T, D), lambda i: (i, 0))
    lat = pl.BlockSpec((T, D), lambda i: (jnp.maximum(i - 1, 0), 0))
    vec = pl.BlockSpec((1, D), lambda i: (0, 0))
    wsp = pl.BlockSpec((NH, HD, HD), lambda i: (0, 0, 0))
    bsp = pl.BlockSpec((NH, HD), lambda i: (0, 0))
    in_specs = [every, lat, lat] + [pl.BlockSpec((D, D), lambda i, k=k: (0, k)) for k in (0, 1, 4, 2, 3)]
    in_specs += [lat, pl.BlockSpec((T, D), lambda i: (0, 0)), pl.BlockSpec((8, D), lambda i: (0, 1)), vec, lat]
    in_specs += [lat, lat, lat, vec, vec, wsp, pl.BlockSpec((HD, NH), lambda i: (0, 0))]
    vs = jax.ShapeDtypeStruct((1, D), F32)
    zb = jax.ShapeDtypeStruct((lx, D), BF16)
    return pl.pallas_call(
        body, name="proj_bwd", grid=(n,),
        out_shape=(jax.ShapeDtypeStruct((lx, D), F32), vs, vs, vs, vs, vs, zb, zb,
                   jax.ShapeDtypeStruct((NH, HD, HD), F32), jax.ShapeDtypeStruct((NH, HD), F32), vs, vs),
        in_specs=in_specs, out_specs=(lat, vec, vec, vec, vec, vec, lat, lat, wsp, bsp, vec, vec),
        scratch_shapes=[pltpu.VMEM((HD, D), F32), pltpu.VMEM((HD, D), F32)], compiler_params=_cp(1, vmem_mb=56),
    )(dxa, dga, dgb, *([w_full] * 5), x, ctx, mods, norm_g, dxn, *sgu)


def _adam_math(w, g, m, v):
    m = ADAM_B1 * m + (1.0 - ADAM_B1) * g
    v = ADAM_B2 * v + (1.0 - ADAM_B2) * (g * g)
    m_hat = m / (1.0 - ADAM_B1 ** ADAM_STEP)
    v_hat = v / (1.0 - ADAM_B2 ** ADAM_STEP)
    delta = -ADAM_LR * (m_hat / (jnp.sqrt(v_hat) + ADAM_EPS) + ADAM_WD * w)
    return delta, m, v


def _adam_big(w, g, m, v, name):
    rows, cols = w.shape
    tr = 256

    def body(w_ref, g_ref, m_ref, v_ref, d_o, m_o, v_o):
        d, mm, vv = _adam_math(w_ref[...], g_ref[...], m_ref[...], v_ref[...])
        d_o[...] = d
        m_o[...] = mm
        v_o[...] = vv

    blk = pl.BlockSpec((tr, cols), lambda i: (i, 0))
    s = jax.ShapeDtypeStruct((rows, cols), F32)
    return pl.pallas_call(
        body, name=name, grid=(rows // tr,), out_shape=(s, s, s), in_specs=[blk] * 4, out_specs=(blk,) * 3,
        compiler_params=_cp(1, vmem_mb=48),
    )(w, g, m, v)


def _adam_small(items, tot):
    ni = len(items)
    pieces = [it[1] if isinstance(it[1], list) else None for it in items]
    flat = [a for it, pc in zip(items, pieces) for a in ((it[0], it[2], it[3]) if pc is not None else it)]
    n_in = len(flat) + 1
    out_shape = tuple(jax.ShapeDtypeStruct(it[0].shape, F32) for it, pc in zip(items, pieces)
                      for _ in range(4 if pc is not None else 3))
    n_out = len(out_shape)
    n_loads = sum(3 + (len(pc) if pc is not None else 1) for pc in pieces)

    def body(*refs):
        ins, tot_ref, outs = refs[:n_in - 1], refs[n_in - 1], refs[n_in:n_in + n_out]
        bufs = refs[n_in + n_out:n_in + n_out + 7 * ni]
        sem_in, sem_out = refs[n_in + n_out + 7 * ni:]
        loads, q_in, q_sem = [], 0, 0
        for k, pc in enumerate(pieces):
            w_b, g_b, m_b, v_b = bufs[7 * k:7 * k + 4]
            srcs = [(ins[q_in], w_b)]
            if pc is None:
                srcs.append((ins[q_in + 1], g_b))
                q_in += 1
            else:
                srcs += [(tot_ref.at[pl.ds(r0, nr), pl.ds(c0, nc)], g_b.at[pl.ds(d0, nr), :]) for r0, nr, c0, nc, d0 in pc]
            srcs += [(ins[q_in + 1], m_b), (ins[q_in + 2], v_b)]
            q_in += 3
            mine = []
            for src, dst in srcs:
                mine.append(pltpu.make_async_copy(src, dst, sem_in.at[q_sem]))
                q_sem += 1
            loads.append(mine)
        for mine in loads:
            for cp in mine:
                cp.start()
        stores, q_out = [], 0
        for k, pc in enumerate(pieces):
            for cp in loads[k]:
                cp.wait()
            w_b, g_b, m_b, v_b = bufs[7 * k:7 * k + 4]
            res = _adam_math(w_b[...], g_b[...], m_b[...], v_b[...])
            srcs = []
            for q in range(3):
                bufs[7 * k + 4 + q][...] = res[q]
                srcs.append(bufs[7 * k + 4 + q])
            if pc is not None:
                srcs.append(g_b)
            for src in srcs:
                cp = pltpu.make_async_copy(src, outs[q_out], sem_out.at[q_out])
                cp.start()
                stores.append(cp)
                q_out += 1
        for cp in stores:
            cp.wait()

    scratch = [pltpu.VMEM(it[0].shape, F32) for it in items for _ in range(7)]
    scratch += [pltpu.SemaphoreType.DMA((n_loads,)), pltpu.SemaphoreType.DMA((n_out,))]
    res = pl.pallas_call(
        body, name="adam_small", out_shape=out_shape, in_specs=[HBM] * n_in, out_specs=(HBM,) * n_out,
        scratch_shapes=scratch, compiler_params=_cp(vmem_mb=40),
    )(*[_in_hbm(a) for a in flat], _in_hbm(tot))
    outs, q = [], 0
    for pc in pieces:
        outs.append(tuple(res[q:q + 3]) + ((res[q + 3],) if pc is not None else (None,)))
        q += 4 if pc is not None else 3
    return outs


def kernel(x, c, ctx, c_ctx, ada_w, ada_b, norm_g, w_in, conv_w, conv_b, lru_wa, lru_ba, lru_wx, lru_bx, lru_lambda, sgu_ln_g, sgu_ln_b, sgu_w, sgu_b, w_out, final_g, loss_target, m_c_ctx, m_ada_w, m_ada_b, m_norm_g, m_w_in, m_conv_w, m_conv_b, m_lru_wa, m_lru_ba, m_lru_wx, m_lru_bx, m_lru_lambda, m_sgu_ln_g, m_sgu_ln_b, m_sgu_w, m_sgu_b, m_w_out, m_final_g, v_c_ctx, v_ada_w, v_ada_b, v_norm_g, v_w_in, v_conv_w, v_conv_b, v_lru_wa, v_lru_ba, v_lru_wx, v_lru_bx, v_lru_lambda, v_sgu_ln_g, v_sgu_ln_b, v_sgu_w, v_sgu_b, v_w_out, v_final_g):
    ix, iy, ic = lax.axis_index("x"), lax.axis_index("y"), lax.axis_index("c")
    chip = 2 * ix + iy
    dev = 2 * chip + ic
    lx = x.shape[1]
    lc = ctx.shape[1]

    smalls = jnp.concatenate([conv_w[0], lru_lambda[0], jnp.zeros((10, 256), F32)], axis=0)
    c_ctx2 = c_ctx.reshape(1, D)
    ada_b_j = lax.dynamic_slice(ada_b, (0, 768 * chip), (1, 768))
    mods, c_slots, sm_all, w_in_full, wo_land, ada_land = _gather_in(c, c_ctx2, ada_w[0], ada_b_j, w_in[0], w_out[0],
                                                                     smalls)
    wo_ss, wo_rs, ada_ss, ada_rs, wo_land, ada_land, token = _late_gather_start(wo_land, ada_land)
    mods = mods + token[0:1, 0:1]
    sm3 = sm_all.reshape(NCHIP, 16, 256)
    conv_w_full = sm3[:, 0:4, :].transpose(1, 0, 2).reshape(4, D)
    lam_full = sm3[:, 4:6, :].transpose(1, 0, 2).reshape(2, D)
    conv_wz = conv_w_full.reshape(4, NH, HD)
    conv_bz = conv_b.reshape(NH, HD)
    lamcat = lam_full.reshape(2, NH, HD).transpose(1, 0, 2).reshape(NH, 2 * HD)
    wa, wx, ba, bx = lru_wa[0], lru_wx[0], lru_ba[0], lru_bx[0]
    wcat = jnp.concatenate([wa[0], wx[0], wa[1], wx[1]], axis=-1).astype(BF16)
    bcat = jnp.concatenate([ba[0], bx[0], ba[1], bx[1]], axis=-1)
    sgu_wb = sgu_w[0].astype(BF16)
    sgu_bt = sgu_b[0].T
    final_g2 = final_g.reshape(1, D)

    zero_s = jnp.zeros((NH, HD), F32)
    hn, xa_all, ga, u, v, gb, ys = _proj(x[0], ctx[0], mods, norm_g, w_in_full, (sgu_ln_g, sgu_ln_b, sgu_wb, sgu_bt))
    xcz, af, ab, hf, hb, gf, gb_l, _, hb0 = _lru_fwd(xa_all, conv_wz, conv_bz, wcat, bcat, lamcat, "lru_fwd")

    w_out_full = _late_gather_wait(wo_land, wo_ss, wo_rs, "w_out", hf, "late_gather_wait_w_out")
    (loss_part, dfg, dgx, dxn, y, do, dga, dgb, dyl_z, dys) = _out_fwd_bwd(
        hf, hb, ga, gb, ys, x[0], loss_target[0], mods, final_g2, w_out_full)

    dxc_b, dxc_f, dwc, dbc, dlc = _lru_bwd(xcz, dyl_z, hf, hb, af, ab, gf, gb_l, hb0, wcat, lamcat, "lru_bwd")
    dxa, dcw, dcb = _conv_bwd(dxc_b, dxc_f, xa_all, conv_wz, jnp.zeros((4, NH, HD), F32), zero_s, "conv_bwd")
    dxa = _in_hbm(dxa)

    grad_x, dng, dsc_x, dsh_x, dsc_c, dsh_c, du, dv, d_sgu_w, d_sgu_b, d_ln_g, d_ln_b = _proj_bwd(
        dxa, dga, dgb, x[0], ctx[0], dxn, mods, norm_g, w_in_full, (u, v, dys, sgu_ln_g, sgu_ln_b, sgu_wb, sgu_bt))
    dzs = [dxa, dga, du, dv, dgb]

    dmx = jnp.concatenate([dsh_x, dsc_x, dgx], axis=0)
    dmc = jnp.concatenate([dsh_c, dsc_c, jnp.zeros((1, D), F32)], axis=0)
    slot = jnp.concatenate([dmx, loss_part], axis=0)
    slots = lax.dynamic_update_slice(jnp.zeros((32, D), F32), slot, (4 * dev, 0))
    vecs = jnp.concatenate([dfg, dng, dcb.reshape(1, D), d_ln_g, d_ln_b, dcw.reshape(4, D), dmc,
                            jnp.zeros((4, D), F32), slots], axis=0)
    d_sgu_w4 = d_sgu_w.reshape(4, 256, HD).transpose(1, 0, 2).reshape(256, 4 * HD)
    pad8 = lambda a: jnp.pad(a, ((0, 8 - a.shape[0]), (0, 4 * HD - a.shape[1])))
    pack = jnp.concatenate([dwc.reshape(NH * HD, 4 * HD), pad8(dbc), pad8(dlc), d_sgu_w4, pad8(d_sgu_b),
                            vecs.reshape(96, 4 * HD), jnp.zeros((8, 4 * HD), F32)], axis=0)
    g_w_in, g_w_out, tot = _grads_reduce(hn, dzs, lc, y, do, _in_hbm(pack))

    n_w = NH * HD
    g_lru_wa = [(0, n_w, 2 * HD * d, HD, n_w * d) for d in range(2)]
    g_lru_wx = [(0, n_w, 2 * HD * d + HD, HD, n_w * d) for d in range(2)]
    g_lru_ba = [(n_w, NH, 2 * HD * d, HD, NH * d) for d in range(2)]
    g_lru_bx = [(n_w, NH, 2 * HD * d + HD, HD, NH * d) for d in range(2)]
    g_sgu_w = [(1040, 256, HD * q, HD, 256 * q) for q in range(4)]
    g_sgu_b = [(1296, NH, 0, HD, 0)]
    g_lc = tot[1032:1040, 0:2 * HD]
    tv = tot[1304:1400].reshape(48, D)
    g_final_g, g_norm_g, g_conv_b, g_ln_g, g_ln_b = tv[0:1], tv[1:2], tv[2:3], tv[3:4], tv[4:5]
    g_conv_w_full = tv[5:9]
    dmc_tot = tv[9:12].reshape(1, 3 * D)
    slots_all = tv[16:48].reshape(8, 4, D)
    dmx_all = slots_all[:, 0:3, :].reshape(8, 3 * D)
    c_all = c_slots.reshape(8, 8, D)[:, 0, :]
    g_lam_full = jnp.stack([g_lc[:, 0:HD], g_lc[:, HD:2 * HD]]).reshape(2, D)
    g_conv_w = lax.dynamic_slice(g_conv_w_full, (0, 256 * chip), (4, 256))
    g_lam = lax.dynamic_slice(g_lam_full, (0, 256 * chip), (2, 256))
    dmx_all_j = lax.dynamic_slice(dmx_all, (0, 768 * chip), (8, 768))
    dmc_j = lax.dynamic_slice(dmc_tot, (0, 768 * chip), (1, 768))
    ada_full = _late_gather_wait(ada_land, ada_ss, ada_rs, "ada_w", _in_hbm(tot), "late_gather_wait_ada_w")
    g_ada_w, g_ada_b, g_c_ctx = _ada_bwd(c_all, dmx_all_j, dmc_j, dmx_all, dmc_tot, c_ctx2, ada_full)

    big = {
        "ada_w": _adam_big(ada_w[0], g_ada_w, m_ada_w[0], v_ada_w[0], "adam_ada_w"),
        "w_in": _adam_big(w_in[0], g_w_in, m_w_in[0], v_w_in[0], "adam_w_in"),
        "w_out": _adam_big(w_out[0], g_w_out, m_w_out[0], v_w_out[0], "adam_w_out"),
    }
    small_in = {
        "c_ctx": (c_ctx, g_c_ctx, m_c_ctx, v_c_ctx, (1, D)),
        "ada_b": (ada_b, g_ada_b, m_ada_b, v_ada_b, (1, 3 * D)),
        "norm_g": (norm_g, g_norm_g, m_norm_g, v_norm_g, (1, D)),
        "conv_w": (conv_w, g_conv_w, m_conv_w, v_conv_w, (4, 256)),
        "conv_b": (conv_b, g_conv_b, m_conv_b, v_conv_b, (1, D)),
        "lru_wa": (lru_wa, g_lru_wa, m_lru_wa, v_lru_wa, (2 * NH * HD, HD)),
        "lru_ba": (lru_ba, g_lru_ba, m_lru_ba, v_lru_ba, (2 * NH, HD)),
        "lru_wx": (lru_wx, g_lru_wx, m_lru_wx, v_lru_wx, (2 * NH * HD, HD)),
        "lru_bx": (lru_bx, g_lru_bx, m_lru_bx, v_lru_bx, (2 * NH, HD)),
        "lru_lambda": (lru_lambda, g_lam, m_lru_lambda, v_lru_lambda, (2, 256)),
        "sgu_ln_g": (sgu_ln_g, g_ln_g, m_sgu_ln_g, v_sgu_ln_g, (1, D)),
        "sgu_ln_b": (sgu_ln_b, g_ln_b, m_sgu_ln_b, v_sgu_ln_b, (1, D)),
        "sgu_w": (sgu_w, g_sgu_w, m_sgu_w, v_sgu_w, (NH * HD, HD)),
        "sgu_b": (sgu_b, g_sgu_b, m_sgu_b, v_sgu_b, (NH, HD)),
        "final_g": (final_g, g_final_g, m_final_g, v_final_g, (1, D)),
    }
    names_small = list(small_in)
    res_small = _adam_small([tuple(a if isinstance(a, list) else a.reshape(small_in[k][4]) for a in small_in[k][:4])
                             for k in names_small], tot)
    full_shapes = {"ada_w": ada_w.shape, "w_in": w_in.shape, "w_out": w_out.shape}
    grads, deltas, new_m, new_v = {}, {}, {}, {}
    for k in ("ada_w", "w_in", "w_out"):
        g = {"ada_w": g_ada_w, "w_in": g_w_in, "w_out": g_w_out}[k]
        grads[k] = g.reshape(full_shapes[k])
        deltas[k], new_m[k], new_v[k] = (a.reshape(full_shapes[k]) for a in big[k])
    for k, res in zip(names_small, res_small):
        shape = small_in[k][0].shape
        grads[k] = (small_in[k][1] if res[3] is None else res[3]).reshape(shape)
        deltas[k], new_m[k], new_v[k] = (a.reshape(shape) for a in res[:3])

    loss = jnp.sum(slots_all[:, 3, 0:3])
    order = ["c_ctx", "ada_w", "ada_b", "norm_g", "w_in", "conv_w", "conv_b", "lru_wa", "lru_ba", "lru_wx", "lru_bx",
             "lru_lambda", "sgu_ln_g", "sgu_ln_b", "sgu_w", "sgu_b", "w_out", "final_g"]
    return (loss, grad_x.reshape(x.shape), *[grads[k] for k in order], *[deltas[k] for k in order],
            *[new_m[k] for k in order], *[new_v[k] for k in order])
```

```python
import jax
import jax.numpy as jnp
from jax import lax
from jax.experimental import pallas as pl
from jax.experimental.pallas import tpu as pltpu

F32 = jnp.float32
BF16 = jnp.bfloat16

D = 1024
NH = 8
HD = 128
NCHIP = 4
T = 256
NORM_EPS = 1e-6
LN_EPS = 1e-5
LRU_C = 8.0
ADAM_LR = 0.001
ADAM_B1 = 0.9
ADAM_B2 = 0.999
ADAM_EPS = 1e-08
ADAM_WD = 0.01
ADAM_STEP = 10

VMEM = pl.BlockSpec(memory_space=pltpu.VMEM)
ANY = pl.BlockSpec(memory_space=pl.ANY)
MESH = pl.DeviceIdType.MESH


def _cp(n_grid=0, vmem_mb=None):
    kw = {}
    if n_grid:
        kw["dimension_semantics"] = ("arbitrary",) * n_grid
    if vmem_mb:
        kw["vmem_limit_bytes"] = vmem_mb << 20
    return pltpu.CompilerParams(**kw)


def _sigmoid(x):
    return 0.5 * jnp.tanh(0.5 * x) + 0.5


def _silu_and_grad(x):
    s = _sigmoid(x)
    return x * s, s * (1.0 + x * (1.0 - s))


_GELU_K = 0.7978845608028654
_GELU_C = 0.044715


def _gelu_and_grad(x):
    x2 = x * x
    th = jnp.tanh(x * (_GELU_K + (_GELU_K * _GELU_C) * x2))
    p = 0.5 + 0.5 * th
    g = x * p
    dg = p + g * (1.0 - th) * (_GELU_K + (3.0 * _GELU_K * _GELU_C) * x2)
    return g, dg


def _softplus(x):
    return jnp.maximum(x, 0.0) + jnp.log1p(jnp.exp(-jnp.abs(x)))


def _lru_gate(pre, lam_row, d, off=None):
    off = 256 * d if off is None else off
    r = _sigmoid(pre[:, off:off + HD])
    gi = _sigmoid(pre[:, off + HD:off + 2 * HD])
    lam = lam_row[:, HD * d:HD * d + HD]
    sp = _softplus(-lam)
    la = (-LRU_C) * r * sp
    a = jnp.exp(la)
    x2 = 2.0 * la
    m2 = jnp.where(x2 > -1e-3, -x2 * (1.0 + 0.5 * x2), 1.0 - a * a)
    mult = jnp.sqrt(m2)
    return r, gi, lam, sp, a, mult


def _dot(a, b):
    return jnp.dot(a, b, preferred_element_type=F32)


def _dot_tn(a, b):
    return lax.dot_general(a, b, (((0,), (0,)), ((), ())), preferred_element_type=F32)


def _dot_nt(a, b):
    return lax.dot_general(a, b, (((1,), (1,)), ((), ())), preferred_element_type=F32)


def _mo(v, m):
    return v if isinstance(v, int) else pl.multiple_of(v, m)


def _zrows(h, n):
    return pl.ds(h, n, stride=NH)


def _gather_in(c, c_ctx, ada_w, ada_b_j, w_in, w_out, smalls):
    nch = [1, 4]
    wrows = lambda cc, q: (pl.ds(_mo(512 * cc, 16), 512) if q is None
                           else pl.ds(_mo(512 * cc + (512 // nch[1]) * q, 16), 512 // nch[1]))
    specs = [
        ((64, 256), F32, lambda r, jj, cc, q=None: r.at[pl.ds(_mo(16 * jj + 8 * cc, 8), 8), :]),
        ((D, 5120), BF16, lambda r, jj, cc, q=None: r.at[wrows(cc, q), pl.ds(_mo(1280 * jj, 128), 1280)]),
    ]
    halves = [lambda r, cc, q=None: r.at[pl.ds(_mo(8 * cc, 8), 8), :],
              lambda r, cc, q=None: r.at[wrows(cc, q), :]]
    na = len(specs)
    sem_base = [0, 6 * nch[0]]
    sidx = lambda a, q, k: sem_base[a] + 6 * q + k
    n_tiny = 6 * sum(nch)
    n_sem = n_tiny + 10

    def body(c_ref, cc_ref, ada_ref, adab_ref, win_ref, wout_ref, sm_ref,
             mods_o, call_o, sm_o, win_o, wol_o, adal_o, s_win, s_ada, s_wout, f_win, f_ada, f_wout, cslot, lhs, mbuf,
             send_sems, recv_sems, local_sems, load_sems):
        x, y, c = lax.axis_index("x"), lax.axis_index("y"), lax.axis_index("c")
        j = 2 * x + y
        dev = 2 * j + c
        sib = (x, y, 1 - c)
        chips = [(1 - x, y), (x, 1 - y), (1 - x, 1 - y)]
        cj = [2 * cx + cy for cx, cy in chips]
        outs = [sm_o, win_o]
        srcs = [sm_ref, s_win]

        def copy(idx, src, dst, to):
            return pltpu.make_async_remote_copy(src_ref=src, dst_ref=dst, send_sem=send_sems.at[idx],
                                                recv_sem=recv_sems.at[idx], device_id=to, device_id_type=MESH)

        sends = []

        def start(cp):
            cp.start()
            sends.append(cp)

        cslot[...] = jnp.zeros_like(cslot)
        cslot[0:1, :] = c_ref[...]
        my_slot = pl.ds(_mo(8 * dev, 8), 8)
        others = [sib] + [(*chips[k], c) for k in range(3)] + [(*chips[k], 1 - c) for k in range(3)]
        other_dev = [dev + 1 - 2 * c] + [2 * cj[k] + c for k in range(3)] + [2 * cj[k] + 1 - c for k in range(3)]
        base = n_tiny
        for r in range(7):
            start(copy(base + r, cslot, call_o.at[my_slot, :], others[r]))
        call_o[my_slot, :] = cslot[...]

        crow = 512 // nch[1]
        loads = []
        for cc in (c, 1 - c):
            for q in range(nch[1]):
                rows = pl.ds(_mo(512 * cc + crow * q, 16), crow)
                loads.append(pltpu.make_async_copy(win_ref.at[rows, :], f_win.at[rows, :], load_sems.at[len(loads)]))
        loads.append(pltpu.make_async_copy(ada_ref, f_ada, load_sems.at[len(loads)]))
        loads.append(pltpu.make_async_copy(wout_ref, f_wout, load_sems.at[len(loads)]))
        for ld in loads:
            ld.start()
        for k in range(2):
            start(copy(sidx(0, 0, k), halves[0](srcs[0], c), specs[0][2](outs[0], j, c), (*chips[k], c)))
        for q in range(nch[1]):
            loads[q].wait()
            rows = pl.ds(_mo(512 * c + crow * q, 16), crow)
            s_win[rows, :] = f_win[rows, :].astype(BF16)
            for k in range(2):
                start(copy(sidx(1, q, k), halves[1](s_win, c, q), specs[1][2](win_o, j, c, q), (*chips[k], c)))
        for q in range(nch[1]):
            loads[nch[1] + q].wait()
            rows = pl.ds(_mo(512 * (1 - c) + crow * q, 16), crow)
            s_win[rows, :] = f_win[rows, :].astype(BF16)
        local = []
        for a in range(na):
            for cc in range(2):
                lc = pltpu.make_async_copy(halves[a](srcs[a], cc), specs[a][2](outs[a], j, cc), local_sems.at[2 * a + cc])
                lc.start()
                local.append(lc)
        loads[2 * nch[1]].wait()
        s_ada[...] = f_ada[...].astype(BF16)
        loads[2 * nch[1] + 1].wait()
        s_wout[...] = f_wout[...].astype(BF16)
        for q, (src, dst) in enumerate([(s_wout, wol_o.at[pl.ds(_mo(512 * j, 16), 512), :]),
                                        (s_ada, adal_o.at[:, pl.ds(_mo(768 * j, 128), 768)])]):
            lc = pltpu.make_async_copy(src, dst, local_sems.at[2 * na + q])
            lc.start()
            local.append(lc)

        for r in range(7):
            slot = call_o.at[pl.ds(_mo(8 * other_dev[r], 8), 8), :]
            copy(base + r, slot, slot, sib).wait_recv()
        lhs[...] = jnp.zeros_like(lhs)
        for b in range(8):
            cv = call_o[8 * b:8 * b + 1, :]
            lhs[b:b + 1, :] = cv * _sigmoid(cv)
        cv = cc_ref[...]
        lhs[8:9, :] = cv * _sigmoid(cv)
        mbuf[j] = _dot(lhs[...].astype(BF16), s_ada[...]) + adab_ref[...]
        for k in range(3):
            start(copy(base + 7 + k, mbuf.at[j], mbuf.at[j], (*chips[k], c)))
        for k in range(3):
            copy(base + 7 + k, mbuf.at[cj[k]], mbuf.at[cj[k]], sib).wait_recv()
        mods_o[...] = jnp.zeros_like(mods_o)
        for jj in range(NCHIP):
            mods_o[0:1, 768 * jj:768 * jj + 768] = mbuf[jj, pl.ds(dev, 1), :]
            mods_o[1:2, 768 * jj:768 * jj + 768] = mbuf[jj, 8:9, :]

        kx = [1 - x, x, 1 - x]
        ky = [y, 1 - y, 1 - y]
        pick = lambda k, lst: jnp.where(k == 0, lst[0], jnp.where(k == 1, lst[1], lst[2]))
        for a in range(na):
            for q in range(nch[a]):
                for step, k in enumerate([c, 1 - c]):
                    reg = specs[a][2](outs[a], pick(k, cj), c, q)
                    copy(sidx(a, q, k), reg, reg, sib).wait_recv()
                    if step == 0:
                        start(copy(sidx(a, q, 2), reg, reg, (pick(1 - c, kx), pick(1 - c, ky), c)))
                    start(copy(sidx(a, q, 3 + k), reg, reg, sib))
        for a in range(na):
            for q in range(nch[a]):
                reg = specs[a][2](outs[a], cj[2], c, q)
                copy(sidx(a, q, 2), reg, reg, sib).wait_recv()
                start(copy(sidx(a, q, 5), reg, reg, sib))
        for a in range(na):
            for q in range(nch[a]):
                for k in range(3):
                    reg = specs[a][2](outs[a], cj[k], 1 - c, q)
                    copy(sidx(a, q, 3 + k), reg, reg, sib).wait_recv()
        for cp in sends:
            cp.wait_send()
        for lc in local:
            lc.wait()

    out_shape = (jax.ShapeDtypeStruct((8, 3 * D), F32), jax.ShapeDtypeStruct((64, D), F32),
                 jax.ShapeDtypeStruct(specs[0][0], F32), jax.ShapeDtypeStruct(specs[1][0], BF16),
                 jax.ShapeDtypeStruct((2048, D), BF16), jax.ShapeDtypeStruct((D, 3 * D), BF16))
    return pl.pallas_call(
        body, name="gather_in", out_shape=out_shape,
        in_specs=[VMEM, VMEM, ANY, VMEM, ANY, ANY, VMEM], out_specs=(VMEM, VMEM, VMEM, ANY, ANY, ANY),
        scratch_shapes=[pltpu.VMEM((D, 1280), BF16), pltpu.VMEM((D, 768), BF16), pltpu.VMEM((512, D), BF16),
                        pltpu.VMEM((D, 1280), F32), pltpu.VMEM((D, 768), F32), pltpu.VMEM((512, D), F32),
                        pltpu.VMEM((8, D), F32), pltpu.VMEM((16, D), F32), pltpu.VMEM((NCHIP, 16, 768), F32),
                        pltpu.SemaphoreType.DMA((n_sem,)), pltpu.SemaphoreType.DMA((n_sem,)),
                        pltpu.SemaphoreType.DMA((2 * na + 2,)), pltpu.SemaphoreType.DMA((2 * nch[1] + 2,))],
        compiler_params=_cp(vmem_mb=56),
    )(c, c_ctx, ada_w, ada_b_j, w_in, w_out, smalls)


HBM = pl.BlockSpec(memory_space=pltpu.HBM)
SEM = pl.BlockSpec(memory_space=pltpu.SEMAPHORE)


def _in_hbm(a):
    return pltpu.with_memory_space_constraint(a, pltpu.HBM)


def _late_gather_regions(x, y, c):
    chips = [(1 - x, y), (x, 1 - y), (1 - x, 1 - y)]
    wo_reg = lambda r, jj, cc: r.at[pl.ds(_mo(512 * jj + 256 * cc, 16), 256), :]
    ada_reg = lambda r, jj, cc: r.at[pl.ds(_mo(512 * cc, 16), 512), pl.ds(_mo(768 * jj, 128), 768)]
    return chips, wo_reg, ada_reg


def _late_gather_start(wo_land, ada_land):
    def body(wol_ref, adal_ref, wo_ss, wo_rs, ada_ss, ada_rs, wol_thru, adal_thru, token):
        x, y, c = lax.axis_index("x"), lax.axis_index("y"), lax.axis_index("c")
        j = 2 * x + y
        chips, wo_reg, ada_reg = _late_gather_regions(x, y, c)
        for k in range(3):
            for cc in range(2):
                pltpu.make_async_remote_copy(src_ref=wo_reg(wol_ref, j, c), dst_ref=wo_reg(wol_ref, j, c),
                                             send_sem=wo_ss.at[2 * k + cc], recv_sem=wo_rs.at[2 * k + c],
                                             device_id=(*chips[k], cc), device_id_type=MESH).start()
        for k in range(3):
            for cc in range(2):
                pltpu.make_async_remote_copy(src_ref=ada_reg(adal_ref, j, c), dst_ref=ada_reg(adal_ref, j, c),
                                             send_sem=ada_ss.at[2 * k + cc], recv_sem=ada_rs.at[2 * k + c],
                                             device_id=(*chips[k], cc), device_id_type=MESH).start()
        token[...] = jnp.zeros_like(token)

    sems = pltpu.SemaphoreType.DMA((6,))
    return pl.pallas_call(
        body, name="late_gather_start",
        out_shape=(sems, sems, sems, sems, pltpu.HBM(wo_land.shape, BF16), pltpu.HBM(ada_land.shape, BF16),
                   jax.ShapeDtypeStruct((8, 128), F32)),
        in_specs=(HBM, HBM), out_specs=(SEM, SEM, SEM, SEM, HBM, HBM, VMEM), input_output_aliases={0: 4, 1: 5},
        compiler_params=pltpu.CompilerParams(has_side_effects=pltpu.SideEffectType.DATAFLOW_SIDE_EFFECTING),
    )(_in_hbm(wo_land), _in_hbm(ada_land))


def _late_gather_wait(land, send_sems, recv_sems, which, after, name):
    def body(land_ref, ss, rs, after_ref, land_out):
        x, y, c = lax.axis_index("x"), lax.axis_index("y"), lax.axis_index("c")
        j = 2 * x + y
        chips, wo_reg, ada_reg = _late_gather_regions(x, y, c)
        reg = wo_reg if which == "w_out" else ada_reg
        for k in range(3):
            kj = 2 * chips[k][0] + chips[k][1]
            for cc in range(2):
                cp = pltpu.make_async_remote_copy(src_ref=reg(land_ref, j, c), dst_ref=reg(land_ref, kj, cc),
                                                  send_sem=ss.at[2 * k + cc], recv_sem=rs.at[2 * k + cc],
                                                  device_id=(*chips[k], cc), device_id_type=MESH)
                cp.wait_send()
                cp.wait_recv()

    return pl.pallas_call(
        body, name=name, out_shape=pltpu.HBM(land.shape, land.dtype),
        in_specs=(HBM, SEM, SEM, ANY), out_specs=HBM, input_output_aliases={0: 0},
        compiler_params=pltpu.CompilerParams(has_side_effects=pltpu.SideEffectType.DATAFLOW_SIDE_EFFECTING),
    )(land, send_sems, recv_sems, after)


RCHUNK = 16


def _grads_reduce(hn, dzs, lc, y, do, pack):
    rp = pack.shape[0]
    hp = rp // 2
    assert hp % RCHUNK == 0
    wi_w = 1280
    lx = hn.shape[0] - lc
    lt = lx + lc
    n_dz = len(dzs)

    def body(*refs):
        hn_hbm, dz_hbm = refs[0], refs[1:1 + n_dz]
        y_hbm, do_hbm, pk_hbm, wi_out, wo_out, pk_out = refs[1 + n_dz:7 + n_dz]
        (hn_mine, hn_other, dzbuf, wi_other, wi_mine, wi_recv, wi_send, wi_rb,
         y_blk, do_mine, do_other, wo_other, wo_mine, wo_recv, wo_send, wo_rb,
         pk_mine, pk_recv, pk_send, pk_rb, pk_own, send_sems, recv_sems, local_sems) = refs[7 + n_dz:]
        x, y, c = lax.axis_index("x"), lax.axis_index("y"), lax.axis_index("c")
        j = 2 * x + y
        sib = (x, y, 1 - c)
        chips = [(1 - x, y), (x, 1 - y), (1 - x, 1 - y)]
        cj = [2 * cx + cy for cx, cy in chips]
        near = (jnp.where(c == 0, 1 - x, x), jnp.where(c == 0, y, 1 - y), c)
        slabs = [cj[2], cj[0], cj[1], j]

        def copy(k, src, dst, to):
            return pltpu.make_async_remote_copy(src_ref=src, dst_ref=dst, send_sem=send_sems.at[k],
                                                recv_sem=recv_sems.at[k], device_id=to, device_id_type=MESH)

        def local(k, src, dst):
            cp = pltpu.make_async_copy(src, dst, local_sems.at[k])
            cp.start()
            return cp

        rows_half = lambda r, cc, n: r.at[pl.ds(_mo(cc * n, 16), n), :]
        cols_half = lambda r, cc, n: r.at[:, pl.ds(_mo(cc * n, 128), n)]
        pk_piece = lambda r, cc, jj: r.at[pl.ds(_mo(cc * hp, 16), hp), pl.ds(_mo(jj * 128, 128), 128)]

        sends = []

        def start(cp):
            cp.start()
            sends.append(cp)

        def dz_pieces(s):
            g0 = wi_w * s
            k0, off0 = g0 // D, g0 % D
            w0 = min(D - off0, wi_w)
            pieces = [(k0, off0, w0, 0)]
            if w0 < wi_w:
                pieces.append((k0 + 1, 0, wi_w - w0, w0))
            return pieces

        def dz_copies(s):
            cps = []
            for q, (k, off, w, dst) in enumerate(dz_pieces(s)):
                cps.append(pltpu.make_async_copy(dz_hbm[k].at[pl.ds(lc if k == 0 else 0, lx), pl.ds(off, w)],
                                                 dzbuf.at[pl.ds(lc, lx), pl.ds(dst, w)], local_sems.at[11 + q]))
            if s == 0:
                cps.append(pltpu.make_async_copy(dz_hbm[0].at[pl.ds(0, lc), :], dzbuf.at[pl.ds(0, lc), pl.ds(0, D)],
                                                 local_sems.at[13]))
            return cps

        def dz_load(sl):
            for s in range(NCHIP):
                @pl.when(sl == s)
                def _():
                    if s == 0:
                        dzbuf[pl.ds(0, lc), pl.ds(D, wi_w - D)] = jnp.zeros((lc, wi_w - D), BF16)
                    else:
                        dzbuf[pl.ds(0, lc), :] = jnp.zeros((lc, wi_w), BF16)
                    for cp in dz_copies(s):
                        cp.start()

        def dz_wait(sl):
            for s in range(NCHIP):
                @pl.when(sl == s)
                def _():
                    for cp in dz_copies(s):
                        cp.wait()

        l_pk = local(0, rows_half(pk_hbm, c, hp), pk_mine)
        start(copy(0, rows_half(pk_hbm, 1 - c, hp), pk_recv, sib))
        col = lambda r, cc: r.at[:, pl.ds(_mo(cc * 512, 128), 512)]
        do_loads = [local(7, col(do_hbm, c), do_mine), local(14, col(do_hbm, 1 - c), do_other)]
        hn_loads = [local(2, col(hn_hbm, c), hn_mine), local(4, col(hn_hbm, 1 - c), hn_other)]
        y_copy = lambda s: pltpu.make_async_copy(col(y_hbm, slabs[s]), y_blk, local_sems.at[1])
        y_copy(0).start()
        dz_load(slabs[0])

        def pair_sum(mine, recv, send, nrows, keep, relayed=None):
            def step(i, carry):
                rows = pl.ds(_mo(i * RCHUNK, RCHUNK), RCHUNK)
                s = mine[rows, :] + recv[rows, :].astype(F32)
                if relayed is not None:
                    s = s + relayed[rows, :].astype(F32)
                if keep:
                    mine[rows, :] = s
                if send is not None:
                    send[rows, :] = s.astype(BF16)
                return carry
            lax.fori_loop(0, nrows // RCHUNK, step, 0)

        def chip_sum(own, rb, nrows, terms=(0, 1, 2)):
            def step(i, carry):
                rows = pl.ds(_mo(i * RCHUNK, RCHUNK), RCHUNK)
                acc = own[rows, :]
                for q in terms:
                    acc = acc + rb[q, rows, :].astype(F32)
                own[rows, :] = acc
                return carry
            lax.fori_loop(0, nrows // RCHUNK, step, 0)

        w_in_g = dict(other=wi_other, mine=wi_mine, recv=wi_recv, send=wi_send, rb=wi_rb, p1_sems=(2, 3, 4, 5), p2_sem=12,
                      p1=[None] * NCHIP, wait_load=lambda s: dz_wait(slabs[s]), load=lambda s: dz_load(slabs[s]),
                      dot_other=lambda: _dot_tn(hn_other[...], dzbuf[...]), dot_mine=lambda: _dot_tn(hn_mine[...], dzbuf[...]))
        w_out_g = dict(other=wo_other, mine=wo_mine, recv=wo_recv, send=wo_send, rb=wo_rb, p1_sems=(1, 24, 25, 26), p2_sem=9,
                       p1=[None] * NCHIP, wait_load=lambda s: y_copy(s).wait(), load=lambda s: y_copy(s).start(),
                       dot_other=lambda: _dot_tn(y_blk[...], do_other[...]), dot_mine=lambda: _dot_tn(y_blk[...], do_mine[...]))

        def piece_matmuls(g, s):
            if s >= 2:
                g["p1"][s - 2].wait_send()
            g["wait_load"](s)
            g["other"][s % 2] = g["dot_other"]().astype(BF16)
            g["p1"][s] = copy(g["p1_sems"][s], g["other"].at[s % 2], g["recv"].at[s], sib)
            g["p1"][s].start()
            g["mine"][s % 2] = g["dot_mine"]()
            if s + 1 < NCHIP:
                g["load"](s + 1)

        def piece_finish(g, s):
            mine, recv, send, rb, p2 = g["mine"].at[s % 2], g["recv"].at[s], g["send"], g["rb"], g["p2_sem"]
            nrows = mine.shape[0]
            copy(g["p1_sems"][s], recv, recv, sib).wait_recv()
            if s == 3:
                pair_sum(mine, recv, None, nrows, True)
                return
            if s == 0:
                pair_sum(mine, recv, send.at[0], nrows, False)
                start(copy(p2, send.at[0], rb.at[0], near))
                return
            adds_relayed = c == (1 if s == 1 else 0)

            @pl.when(adds_relayed)
            def _():
                copy(p2, rb.at[0], rb.at[0], sib).wait_recv()
                pair_sum(mine, recv, send.at[s], nrows, False, rb.at[0])

            @pl.when(jnp.logical_not(adds_relayed))
            def _():
                pair_sum(mine, recv, send.at[s], nrows, False)
            start(copy(p2 + s, send.at[s], rb.at[s], (*chips[s - 1], c)))

        def piece_total(g):
            for k in (1, 2):
                copy(g["p2_sem"] + k, g["rb"].at[k], g["rb"].at[k], sib).wait_recv()
            chip_sum(g["mine"].at[1], g["rb"], g["mine"].shape[1], (1, 2))

        for cp in do_loads:
            cp.wait()
        piece_matmuls(w_out_g, 0)
        piece_matmuls(w_out_g, 1)
        piece_finish(w_out_g, 0)
        piece_matmuls(w_out_g, 2)
        piece_finish(w_out_g, 1)
        piece_matmuls(w_out_g, 3)
        piece_finish(w_out_g, 2)

        for cp in hn_loads:
            cp.wait()
        piece_matmuls(w_in_g, 0)

        l_pk.wait()
        copy(0, pk_recv, pk_recv, sib).wait_recv()
        pair_sum(pk_mine, pk_recv, pk_send, hp, True)
        for k in range(3):
            start(copy(6 + k, pk_send.at[:, pl.ds(_mo(cj[k] * 128, 128), 128)], pk_rb.at[k], (*chips[k], c)))
        l_pk_own = local(6, pk_mine.at[:, pl.ds(_mo(j * 128, 128), 128)], pk_own)

        piece_matmuls(w_in_g, 1)
        piece_finish(w_in_g, 0)

        l_pk_own.wait()
        for k in range(3):
            copy(6 + k, pk_rb.at[k], pk_rb.at[k], sib).wait_recv()
        chip_sum(pk_own, pk_rb, hp)
        l_pk_out = local(8, pk_own, pk_piece(pk_out, c, j))
        start(copy(15, pk_own, pk_piece(pk_out, c, j), sib))
        for k in range(2):
            start(copy(16 + k, pk_own, pk_piece(pk_out, c, j), (*chips[k], c)))

        piece_matmuls(w_in_g, 2)
        piece_finish(w_in_g, 1)
        piece_matmuls(w_in_g, 3)
        piece_finish(w_in_g, 2)

        piece_finish(w_out_g, 3)
        piece_total(w_out_g)
        l_wo_out = local(9, wo_mine.at[1], cols_half(wo_out, c, 512))
        start(copy(22, wo_mine.at[1], cols_half(wo_out, c, 512), sib))

        far = (jnp.where(c == 0, x, 1 - x), jnp.where(c == 0, 1 - y, y), c)
        for step, k in enumerate([c, 1 - c, 2]):
            reg = pk_piece(pk_out, c, jnp.where(k == 0, cj[0], jnp.where(k == 1, cj[1], cj[2])))
            copy(16 + k, reg, reg, sib).wait_recv()
            if step == 0:
                start(copy(18, reg, reg, far))
            start(copy(19 + k, reg, reg, sib))

        piece_finish(w_in_g, 3)
        piece_total(w_in_g)
        l_wi_out = local(10, wi_mine.at[1], rows_half(wi_out, c, 512))
        start(copy(23, wi_mine.at[1], rows_half(wi_out, c, 512), sib))

        reg = pk_piece(pk_out, 1 - c, j)
        copy(15, reg, reg, sib).wait_recv()
        for k in range(3):
            reg = pk_piece(pk_out, 1 - c, cj[k])
            copy(19 + k, reg, reg, sib).wait_recv()
        reg = cols_half(wo_out, 1 - c, 512)
        copy(22, reg, reg, sib).wait_recv()
        reg = rows_half(wi_out, 1 - c, 512)
        copy(23, reg, reg, sib).wait_recv()
        for cp in sends + w_in_g["p1"][2:] + w_out_g["p1"][2:]:
            cp.wait_send()
        for cp in (l_pk_out, l_wo_out, l_wi_out):
            cp.wait()

    return pl.pallas_call(
        body, name="grads_reduce",
        out_shape=(jax.ShapeDtypeStruct((D, wi_w), F32), jax.ShapeDtypeStruct((512, D), F32),
                   jax.ShapeDtypeStruct(pack.shape, F32)),
        in_specs=[ANY] * (4 + n_dz), out_specs=(ANY,) * 3,
        scratch_shapes=[
            pltpu.VMEM((lt, 512), BF16), pltpu.VMEM((lt, 512), BF16), pltpu.VMEM((lt, wi_w), BF16),
            pltpu.VMEM((2, 512, wi_w), BF16), pltpu.VMEM((2, 512, wi_w), F32), pltpu.VMEM((4, 512, wi_w), BF16),
            pltpu.VMEM((3, 512, wi_w), BF16), pltpu.VMEM((3, 512, wi_w), BF16),
            pltpu.VMEM((lx, 512), BF16), pltpu.VMEM((lx, 512), BF16), pltpu.VMEM((lx, 512), BF16),
            pltpu.VMEM((2, 512, 512), BF16), pltpu.VMEM((2, 512, 512), F32), pltpu.VMEM((4, 512, 512), BF16),
            pltpu.VMEM((3, 512, 512), BF16), pltpu.VMEM((3, 512, 512), BF16),
            pltpu.VMEM((hp, 512), F32), pltpu.VMEM((hp, 512), F32), pltpu.VMEM((hp, 512), BF16),
            pltpu.VMEM((3, hp, 128), BF16), pltpu.VMEM((hp, 128), F32),
            pltpu.SemaphoreType.DMA((27,)), pltpu.SemaphoreType.DMA((27,)), pltpu.SemaphoreType.DMA((15,))],
        compiler_params=_cp(vmem_mb=56),
    )(hn, *dzs, y, do, pack)


def _ada_bwd(c_all, dmx_all_j, dmc_j, dmx_all, dmc, c_ctx, ada_w_full):
    def body(c_ref, dmxj_ref, dmcj_ref, dmx_ref, dmc_ref, cc_ref, w_hbm, gw_ref, gb_ref, gc_ref, lhs, rhs, dm8,
             w_v, w_sem):
        w_loads = [pltpu.make_async_copy(w_hbm.at[:, pl.ds(D * k, D)], w_v.at[k], w_sem.at[k]) for k in range(3)]
        for cp in w_loads:
            cp.start()
        lhs[...] = jnp.zeros_like(lhs)
        rhs[...] = jnp.zeros_like(rhs)
        cv = c_ref[...]
        lhs[0:8, :] = cv * _sigmoid(cv)
        cc = cc_ref[...]
        a_c, da_c = _silu_and_grad(cc)
        lhs[8:9, :] = a_c
        rhs[0:8, :] = dmxj_ref[...]
        rhs[8:9, :] = dmcj_ref[...]
        gw_ref[...] = _dot_tn(lhs[...].astype(BF16), rhs[...].astype(BF16))
        gb_ref[...] = jnp.sum(dmx_ref[...], axis=0, keepdims=True) + dmc_ref[...]
        dm8[...] = jnp.zeros_like(dm8)
        dm8[0:1, :] = dmc_ref[...]
        da = jnp.zeros((8, D), F32)
        for k in range(3):
            w_loads[k].wait()
            da = da + _dot_nt(dm8[:, D * k:D * k + D].astype(BF16), w_v[k])
        gc_ref[...] = da[0:1, :] * da_c

    return pl.pallas_call(
        body, name="ada_bwd",
        out_shape=(jax.ShapeDtypeStruct((D, 768), F32), jax.ShapeDtypeStruct((1, 3 * D), F32),
                   jax.ShapeDtypeStruct((1, D), F32)),
        in_specs=[VMEM] * 6 + [ANY], out_specs=(VMEM,) * 3,
        scratch_shapes=[pltpu.VMEM((16, D), F32), pltpu.VMEM((16, 768), F32), pltpu.VMEM((8, 3 * D), F32),
                        pltpu.VMEM((3, D, D), BF16), pltpu.SemaphoreType.DMA((3,))],
        compiler_params=_cp(vmem_mb=32),
    )(c_all, dmx_all_j, dmc_j, dmx_all, dmc, c_ctx, ada_w_full)


def _proj(x, ctx, mods, norm_g, w_full, sgu):
    lx, lc = x.shape[0], ctx.shape[0]
    assert lc == T
    n = 1 + lx // T

    def body(x_ref, c_ref, sh_ref, sc_ref, ng_ref, w0, w1, w2, w3, w4, g_ref, b_ref, sw_ref, bt_ref,
             hn_ref, xa_ref, ga_ref, u_ref, v_ref, gb_ref, ys_ref, mixed_s):
        i = pl.program_id(0)
        is_ctx = i == 0
        xv = jnp.where(is_ctx, c_ref[...], x_ref[...])
        sc = jnp.where(is_ctx, sc_ref[1:2, :], sc_ref[0:1, :])
        sh = jnp.where(is_ctx, sh_ref[1:2, :], sh_ref[0:1, :])
        r = lax.rsqrt(jnp.mean(xv * xv, axis=-1, keepdims=True) + NORM_EPS)
        hb = ((xv * r) * ng_ref[...] * (1.0 + sc) + sh).astype(BF16)
        hn_ref[...] = hb
        xa_ref[...] = _dot(hb, w0[...])

        @pl.when(i > 0)
        def _():
            u_ref[...] = _dot(hb, w2[...])
            v_ref[...] = _dot(hb, w3[...])
            for ch in range(T // HD):
                rows = slice(HD * ch, HD * ch + HD)
                ug = _sgu_parts(u_ref[rows, :], v_ref[rows, :], g_ref[...], b_ref[...], sw_ref, bt_ref, mixed_s)[0]
                ys_ref[rows, :] = ug * mixed_s[...]
            ga_ref[...] = _dot(hb, w1[...])
            gb_ref[...] = _dot(hb, w4[...])

    every = pl.BlockSpec((T, D), lambda i: (i, 0))
    lat = pl.BlockSpec((T, D), lambda i: (jnp.maximum(i - 1, 0), 0))
    vec = pl.BlockSpec((1, D), lambda i: (0, 0))
    in_specs = [lat, pl.BlockSpec((T, D), lambda i: (0, 0)), pl.BlockSpec((8, D), lambda i: (0, 0)),
                pl.BlockSpec((8, D), lambda i: (0, 1)), vec]
    in_specs += [pl.BlockSpec((D, D), lambda i, k=k: (0, k)) for k in range(5)]
    in_specs += [vec, vec, pl.BlockSpec((NH, HD, HD), lambda i: (0, 0, 0)), pl.BlockSpec((HD, NH), lambda i: (0, 0))]
    full_s = jax.ShapeDtypeStruct((lc + lx, D), F32)
    lat_s = jax.ShapeDtypeStruct((lx, D), F32)
    return pl.pallas_call(
        body, name="proj", grid=(n,),
        out_shape=(jax.ShapeDtypeStruct((lc + lx, D), BF16), full_s, lat_s, lat_s, lat_s, lat_s, lat_s),
        in_specs=in_specs, out_specs=(every, every, lat, lat, lat, lat, lat),
        scratch_shapes=[pltpu.VMEM((HD, D), F32)], compiler_params=_cp(1, vmem_mb=56),
    )(x, ctx, mods, mods, norm_g, *([w_full] * 5), *sgu)


def _tile_specs(rows_per_pos, width, n_tiles, tile):
    last = n_tiles * (T // 8) - 1
    r = rows_per_pos
    return [pl.BlockSpec((T * r, width), lambda i: (tile(i), 0)),
            pl.BlockSpec((8 * r, width), lambda i: (jnp.maximum(tile(i) * (T // 8) - 1, 0), 0)),
            pl.BlockSpec((8 * r, width), lambda i: (jnp.minimum((tile(i) + 1) * (T // 8), last), 0))]


def _has_prev(tile):
    return tile >= 2


def _has_next(tile, nt):
    return jnp.logical_and(tile >= 1, tile < nt - 1)


ZT = pl.BlockSpec((T * NH, HD), lambda i: (i, 0))
CONV_CHUNK = 32


SCAN_SUB = 8


def _scan_tile(chains, post, carry_ref):
    blk = T // SCAN_SUB

    def step(k, state):
        new = []
        for ci, (a_ref, x_ref, o_ref, q_ref, reverse, xscale) in enumerate(chains):
            for q in range(SCAN_SUB):
                s, p = state[ci * SCAN_SUB + q]
                t = (q + 1) * blk - 1 - k if reverse else q * blk + k
                r = pl.ds(_mo(t * NH, NH), NH)
                a = a_ref[r, :]
                x = x_ref[r, :] if xscale is None else x_ref[r, :] * xscale
                if post:
                    o = x + s
                    o_ref[r, :] = o
                    q_ref[r, :] = p
                    new.append((a * o, a * p))
                else:
                    o = a * s + x
                    p = a * p
                    o_ref[r, :] = o
                    q_ref[r, :] = p
                    new.append((o, p))
        return tuple(new)

    zero = jnp.zeros((NH, HD), F32)
    one = jnp.ones((NH, HD), F32)
    final = lax.fori_loop(0, blk, step, tuple((zero, one) for _ in range(len(chains) * SCAN_SUB)))
    for ci, (a_ref, x_ref, o_ref, q_ref, reverse, xscale) in enumerate(chains):
        carry = carry_ref[ci]
        for q in (range(SCAN_SUB - 1, -1, -1) if reverse else range(SCAN_SUB)):
            rows = pl.ds(q * blk * NH, blk * NH)
            fixed = o_ref[rows, :].reshape(blk, NH, HD) + q_ref[rows, :].reshape(blk, NH, HD) * carry[None]
            o_ref[rows, :] = fixed.reshape(blk * NH, HD)
            s_loc, p_loc = final[ci * SCAN_SUB + q]
            carry = s_loc + p_loc * carry
        carry_ref[ci] = carry


def _lru_fwd(xa, conv_wz, conv_bz, wcat, bcat, lamcat, name):
    lx = xa.shape[0]
    n = lx // T
    tile_u = lambda i: i
    tile_d = lambda i: jnp.where(i == 0, 0, n - i)

    def body(xm_u, xp_u, xn_u, xm_d, xp_d, xn_d, cw, cb, w_ref, b_ref, lam_ref,
             xcz_o, af_o, ab_o, hf_o, hb_o, gf_o, gb_o, fu, fd, pad, xc_d, x_u, x_d, q_u, q_d, carry):
        i = pl.program_id(0)

        @pl.when(i == 0)
        def _():
            carry[...] = jnp.zeros_like(carry)

        def conv_gates(xm, xp, xn, tile, d, xc_ref, a_ref, x_ref, g_ref):
            pmask = jnp.where(_has_prev(tile), 1.0, 0.0)
            nmask = jnp.where(_has_next(tile, n), 1.0, 0.0)
            for h in range(NH):
                cols = slice(HD * h, HD * h + HD)
                pad[_zrows(h, 8), :] = xp[:, cols] * pmask
                pad[pl.ds(8 * NH + h, T, stride=NH), :] = xm[:, cols]
                pad[pl.ds((T + 8) * NH + h, 8, stride=NH), :] = xn[:, cols] * nmask

            def conv_chunk(ci, c_):
                base = pl.multiple_of(ci * (CONV_CHUNK * NH), CONV_CHUNK * NH)
                acc = None
                for k in range(4):
                    sl = pad[pl.ds(base + (7 + k) * NH, CONV_CHUNK * NH), :].reshape(CONV_CHUNK, NH, HD)
                    term = sl * cw[k][None]
                    acc = term if acc is None else acc + term
                acc = acc + cb[...][None]
                xc_ref[pl.ds(base, CONV_CHUNK * NH), :] = acc.reshape(CONV_CHUNK * NH, HD)
                return c_
            lax.fori_loop(0, T // CONV_CHUNK, conv_chunk, 0)

            for h in range(NH):
                xch = xc_ref[_zrows(h, T), :]
                pre = _dot(xch.astype(BF16), w_ref[h, :, 256 * d:256 * d + 256]) + b_ref[h:h + 1, 256 * d:256 * d + 256]
                r, gi, _, _, a, mult = _lru_gate(pre, lam_ref[h:h + 1, :], d, 0)
                a_ref[_zrows(h, T), :] = a
                x_ref[_zrows(h, T), :] = mult * gi * xch
                for q, val in enumerate((r, gi, mult)):
                    g_ref[:, q * D + HD * h:q * D + HD * h + HD] = val

        conv_gates(xm_u, xp_u, xn_u, tile_u(i), 0, xcz_o, af_o, x_u, gf_o)
        conv_gates(xm_d, xp_d, xn_d, tile_d(i), 1, xc_d, ab_o, x_d, gb_o)

        _scan_tile([(af_o, x_u, hf_o, q_u, False, None), (ab_o, x_d, hb_o, q_d, True, None)], False, carry)

        @pl.when(i == 0)
        def _():
            fu[...] = carry[0]
            fd[...] = carry[1]

    full = lambda shape: pl.BlockSpec(shape, lambda i: (0,) * len(shape))
    st = full((NH, HD))
    in_specs = _tile_specs(1, D, n, tile_u) + _tile_specs(1, D, n, tile_d)
    in_specs += [full((4, NH, HD)), st, full((NH, HD, 4 * HD)), full((NH, 4 * HD)), full((NH, 2 * HD))]
    up = pl.BlockSpec((T * NH, HD), lambda i: (tile_u(i), 0))
    dn = pl.BlockSpec((T * NH, HD), lambda i: (tile_d(i), 0))
    zs = jax.ShapeDtypeStruct((lx * NH, HD), F32)
    ss = jax.ShapeDtypeStruct((NH, HD), F32)
    zbuf = pltpu.VMEM((T * NH, HD), F32)
    gs = jax.ShapeDtypeStruct((lx, 3 * D), F32)
    g_up = pl.BlockSpec((T, 3 * D), lambda i: (tile_u(i), 0))
    g_dn = pl.BlockSpec((T, 3 * D), lambda i: (tile_d(i), 0))
    return pl.pallas_call(
        body, name=name, grid=(n,), out_shape=(zs,) * 5 + (gs, gs, ss, ss), in_specs=in_specs,
        out_specs=(up, up, dn, up, dn, g_up, g_dn, st, st),
        scratch_shapes=[pltpu.VMEM(((T + 16) * NH, HD), F32), zbuf, zbuf, zbuf, zbuf, zbuf,
                        pltpu.VMEM((2, NH, HD), F32)],
        compiler_params=_cp(1, vmem_mb=48),
    )(xa, xa, xa, xa, xa, xa, conv_wz, conv_bz, wcat, bcat, lamcat)


def _sgu_parts(u, v, lng, lnb, w_ref, bt_ref, mixed_s):
    ug, dug = _gelu_and_grad(u)
    vg, dvg = _gelu_and_grad(v)
    mu = jnp.mean(vg, axis=-1, keepdims=True)
    vc = vg - mu
    rstd = lax.rsqrt(jnp.mean(vc * vc, axis=-1, keepdims=True) + LN_EPS)
    vh = vc * rstd
    vn = (vh * lng + lnb).astype(BF16)
    for g in range(NH):
        cols = slice(HD * g, HD * g + HD)
        mixed_s[:, cols] = _dot(w_ref[g], vn[:, cols]) + bt_ref[:, g:g + 1]
    return ug, dug, dvg, rstd, vh, vn


def _sgu_bwd_chunk(u, v, dys_v, lng, lnb, w_ref, bt_ref, mixed_s, dvn_s, dw_ref, db_ref, dg_ref, dbl_ref):
    ug, dug, dvg, rstd, vh, vn = _sgu_parts(u, v, lng, lnb, w_ref, bt_ref, mixed_s)
    du = (dys_v * mixed_s[...] * dug).astype(BF16)
    dmix = dys_v * ug
    ones = jnp.ones((8, HD), BF16)
    for g in range(NH):
        cols = slice(HD * g, HD * g + HD)
        dm = dmix[:, cols]
        hi = dm.astype(BF16)
        lo = (dm - hi.astype(F32)).astype(BF16)
        dw_ref[g] += _dot_nt(hi, vn[:, cols])
        db_ref[g:g + 1, :] += (_dot_nt(ones, hi) + _dot_nt(ones, lo))[0:1, :]
        dvn_s[:, cols] = _dot_tn(w_ref[g], hi)
    dvn = dvn_s[...]
    dg_ref[...] += jnp.sum(dvn * vh, axis=0, keepdims=True)
    dbl_ref[...] += jnp.sum(dvn, axis=0, keepdims=True)
    dvh = dvn * lng
    dvg_in = rstd * (dvh - jnp.mean(dvh, axis=-1, keepdims=True) - vh * jnp.mean(dvh * vh, axis=-1, keepdims=True))
    return du, (dvg_in * dvg).astype(BF16)


def _out_fwd_bwd(hf_z, hb_z, ga, gb, ys, x, tgt, mods, final_g, w_out_full):
    lx = x.shape[0]
    n = lx // T

    def body(hf_ref, hb_ref, ga_ref, gb_ref, ys_ref, x_ref, t_ref, gx_ref, fg_ref, w_ref,
             loss_ref, dfg_ref, dgx_ref, dxn_ref, y_ref, do_ref, dga_ref, dgb_ref, dyl_ref, dys_ref, yl_s):
        i = pl.program_id(0)

        @pl.when(i == 0)
        def _():
            loss_ref[...] = jnp.zeros_like(loss_ref)
            dfg_ref[...] = jnp.zeros_like(dfg_ref)
            dgx_ref[...] = jnp.zeros_like(dgx_ref)

        for h in range(NH):
            yl_s[:, HD * h:HD * h + HD] = hf_ref[_zrows(h, T), :] + hb_ref[_zrows(h, T), :]
        yl = yl_s[...]
        gav = ga_ref[...]
        gbv = gb_ref[...]
        sa, dsa = _silu_and_grad(gav)
        sb, dsb = _silu_and_grad(gbv)
        ysv = ys_ref[...]
        y_ref[:, 0:D] = (yl * sa).astype(BF16)
        y_ref[:, D:2 * D] = (ysv * sb).astype(BF16)
        o = _dot(y_ref[...], w_ref[...])
        gx = gx_ref[0:1, :]
        xnew = x_ref[...] + gx * o
        r2 = lax.rsqrt(jnp.mean(xnew * xnew, axis=-1, keepdims=True) + NORM_EPS)
        xh = xnew * r2
        fg = fg_ref[...]
        err = xh * fg - t_ref[...]
        loss_ref[...] += 0.5 * jnp.sum(jnp.mean(err * err, axis=-1, keepdims=True), axis=0, keepdims=True)

        @pl.when(i == n - 1)
        def _():
            lp = loss_ref[...]
            lp1 = lp.astype(BF16).astype(F32)
            lp2 = (lp - lp1).astype(BF16).astype(F32)
            lp3 = (lp - lp1 - lp2).astype(BF16).astype(F32)
            lane = lax.broadcasted_iota(jnp.int32, lp.shape, 1)
            loss_ref[...] = jnp.where(lane == 0, lp1, jnp.where(lane == 1, lp2, jnp.where(lane == 2, lp3, 0.0)))
        dout = err * (1.0 / D)
        dfg_ref[...] += jnp.sum(dout * xh, axis=0, keepdims=True)
        dxh = dout * fg
        dxn = r2 * (dxh - xh * jnp.mean(dxh * xh, axis=-1, keepdims=True))
        dxn_ref[...] = dxn
        dgx_ref[...] += jnp.sum(dxn * o, axis=0, keepdims=True)
        do = (dxn * gx).astype(BF16)
        do_ref[...] = do
        dy = _dot_nt(do, w_ref[...])
        dy1 = dy[:, 0:D]
        dy2 = dy[:, D:2 * D]
        dga_ref[...] = (dy1 * yl * dsa).astype(BF16)
        dgb_ref[...] = (dy2 * ysv * dsb).astype(BF16)
        dys_ref[...] = dy2 * sb
        yl_s[...] = dy1 * sa
        for h in range(NH):
            dyl_ref[_zrows(h, T), :] = yl_s[:, HD * h:HD * h + HD]

    row = pl.BlockSpec((T, D), lambda i: (i, 0))
    vec = pl.BlockSpec((1, D), lambda i: (0, 0))
    zlat = pl.BlockSpec((T * NH, HD), lambda i: (i + 1, 0))
    in_specs = [zlat, zlat, row, row, row, row, row, pl.BlockSpec((8, D), lambda i: (0, 2)), vec,
                pl.BlockSpec((2 * D, D), lambda i: (0, 0))]
    out_shape = (jax.ShapeDtypeStruct((1, D), F32), jax.ShapeDtypeStruct((1, D), F32), jax.ShapeDtypeStruct((1, D), F32),
                 jax.ShapeDtypeStruct((lx, D), F32), jax.ShapeDtypeStruct((lx, 2 * D), BF16),
                 jax.ShapeDtypeStruct((lx, D), BF16), jax.ShapeDtypeStruct((lx, D), BF16),
                 jax.ShapeDtypeStruct((lx, D), BF16), jax.ShapeDtypeStruct((lx * NH, HD), F32),
                 jax.ShapeDtypeStruct((lx, D), F32))
    out_specs = (vec, vec, vec, row, pl.BlockSpec((T, 2 * D), lambda i: (i, 0)),
                 row, row, row, ZT, row)
    return pl.pallas_call(
        body, name="out_fwd_bwd", grid=(n,), out_shape=out_shape, in_specs=in_specs, out_specs=out_specs,
        scratch_shapes=[pltpu.VMEM((T, D), F32)],
        compiler_params=_cp(1, vmem_mb=56),
    )(hf_z, hb_z, ga, gb, ys, x, tgt, mods, final_g, w_out_full)


def _lru_bwd(xc_z, dy_z, hf_z, hb_z, af_z, ab_z, gf, gb, s_b, wcat, lamcat, name):
    lx = xc_z.shape[0] // NH
    n = lx // T
    tile_u = lambda i: jnp.where(i == n - 1, 0, i + 1)
    tile_d = lambda i: n - 1 - i

    def body(xc_u, dy_u, hb_ref, hbn_ref, ab_ref, gb_ref, xc_d, dy_d, hf_ref, hfp_ref, af_ref, gf_ref,
             sb_ref, w_ref, lam_ref, dxcb_ref, dxcf_ref, dw_ref, db_ref, dl_ref,
             lb_s, lf_s, q_u, q_d, pf_s, pb_s, dpre_s, carry):
        i = pl.program_id(0)
        tu, td = tile_u(i), tile_d(i)

        @pl.when(i == 0)
        def _():
            dw_ref[...] = jnp.zeros_like(dw_ref)
            db_ref[...] = jnp.zeros_like(db_ref)
            dl_ref[...] = jnp.zeros_like(dl_ref)
            carry[...] = jnp.zeros_like(carry)

        _scan_tile([(ab_ref, dy_u, lb_s, q_u, False, jnp.where(tu == 0, 0.0, 1.0)),
                    (af_ref, dy_d, lf_s, q_d, True, jnp.where(td == 0, 0.0, 1.0))], True, carry)
        zero = jnp.zeros((NH, HD), F32)
        pb_s[pl.ds(0, T * NH), :] = hb_ref[...]
        pb_s[pl.ds(T * NH, NH), :] = jnp.where(tu == n - 1, sb_ref[...], jnp.where(tu == 0, zero, hbn_ref[pl.ds(0, NH), :]))
        pf_s[pl.ds(0, NH), :] = jnp.where(td == 0, zero, hfp_ref[pl.ds(7 * NH, NH), :])
        pf_s[pl.ds(NH, T * NH), :] = hf_ref[...]
        sides = ((1, xc_u, lb_s, pb_s, NH, ab_ref, gb_ref, dxcb_ref), (0, xc_d, lf_s, pf_s, 0, af_ref, gf_ref, dxcf_ref))
        for d, xc_ref, adj_s, prev_s, prev_off, a_ref, g_ref, dxc_ref in sides:
            wcols = slice(256 * d, 256 * d + 256)
            for h in range(NH):
                xch = xc_ref[_zrows(h, T), :]
                xcb = xch.astype(BF16)
                r, gi, mult = (g_ref[:, q * D + HD * h:q * D + HD * h + HD] for q in range(3))
                a = a_ref[_zrows(h, T), :]
                lam = lam_ref[h:h + 1, HD * d:HD * d + HD]
                sp = _softplus(-lam)
                du = adj_s[_zrows(h, T), :]
                da = du * prev_s[pl.ds(prev_off + h, T, stride=NH), :]
                dgi = du * mult * xch
                dmult = du * gi * xch
                dla = da * a - dmult * (a * a) / mult
                dr = dla * ((-LRU_C) * sp)
                dsp = jnp.sum(dla * ((-LRU_C) * r), axis=0, keepdims=True)
                dl_ref[h:h + 1, HD * d:HD * d + HD] += dsp * (-_sigmoid(-lam))
                dpre_s[:, 0:HD] = dr * r * (1.0 - r)
                dpre_s[:, HD:2 * HD] = dgi * gi * (1.0 - gi)
                dpre = dpre_s[...]
                dpb = dpre.astype(BF16)
                dw_ref[h, :, wcols] += _dot_tn(xcb, dpb)
                db_ref[h:h + 1, wcols] += jnp.sum(dpre, axis=0, keepdims=True)
                dxc_ref[_zrows(h, T), :] = du * mult * gi + _dot_nt(dpb, w_ref[h, :, wcols])

    full = lambda shape: pl.BlockSpec(shape, lambda i: (0,) * len(shape))
    wsp, bsp, lsp = full((NH, HD, 4 * HD)), full((NH, 4 * HD)), full((NH, 2 * HD))
    st = full((NH, HD))
    up = pl.BlockSpec((T * NH, HD), lambda i: (tile_u(i), 0))
    dn = pl.BlockSpec((T * NH, HD), lambda i: (tile_d(i), 0))
    dy_up = pl.BlockSpec((T * NH, HD), lambda i: (jnp.maximum(tile_u(i) - 1, 0), 0))
    dy_dn = pl.BlockSpec((T * NH, HD), lambda i: (jnp.maximum(tile_d(i) - 1, 0), 0))
    nxt = _tile_specs(NH, HD, n, tile_u)[2]
    prv = _tile_specs(NH, HD, n, tile_d)[1]
    g_up = pl.BlockSpec((T, 3 * D), lambda i: (tile_u(i), 0))
    g_dn = pl.BlockSpec((T, 3 * D), lambda i: (tile_d(i), 0))
    zs = jax.ShapeDtypeStruct((lx * NH, HD), F32)
    zbuf = pltpu.VMEM((T * NH, HD), F32)
    zbuf1 = pltpu.VMEM(((T + 1) * NH, HD), F32)
    return pl.pallas_call(
        body, name=name, grid=(n,),
        out_shape=(zs, zs, jax.ShapeDtypeStruct((NH, HD, 4 * HD), F32), jax.ShapeDtypeStruct((NH, 4 * HD), F32),
                   jax.ShapeDtypeStruct((NH, 2 * HD), F32)),
        in_specs=[up, dy_up, up, nxt, up, g_up, dn, dy_dn, dn, prv, dn, g_dn, st, wsp, lsp],
        out_specs=(up, dn, wsp, bsp, lsp),
        scratch_shapes=[zbuf, zbuf, zbuf, zbuf, zbuf1, zbuf1, pltpu.VMEM((T, 2 * HD), F32),
                        pltpu.VMEM((2, NH, HD), F32)],
        compiler_params=_cp(1, vmem_mb=56),
    )(xc_z, dy_z, hb_z, hb_z, ab_z, gb, xc_z, dy_z, hf_z, hf_z, af_z, gf, s_b, wcat, lamcat)


def _conv_bwd(dxc_a, dxc_b, xa, conv_wz, dcw0, dcb0, name):
    lx = dxc_a.shape[0] // NH
    n = lx // T

    def body(dm_a, dp_a, dn_a, dm_b, dp_b, dn_b, xan_ref, cw, dcw0_ref, dcb0_ref, dxa_ref, dcw_ref, dcb_ref,
             pad, dxa_s, xa_ref):
        i = pl.program_id(0)

        @pl.when(i == 0)
        def _():
            dcw_ref[...] = dcw0_ref[...]
            dcb_ref[...] = dcb0_ref[...]

        for h in range(NH):
            xa_ref[_zrows(h, T), :] = xan_ref[:, HD * h:HD * h + HD]

        pmask = jnp.where(_has_prev(i), 1.0, 0.0)
        nmask = jnp.where(_has_next(i, n), 1.0, 0.0)
        pad[pl.ds(0, 8 * NH), :] = (dp_a[...] + dp_b[...]) * pmask
        pad[pl.ds(8 * NH, T * NH), :] = dm_a[...] + dm_b[...]
        pad[pl.ds((T + 8) * NH, 8 * NH), :] = (dn_a[...] + dn_b[...]) * nmask

        def chunk(ci, carry):
            base = pl.multiple_of(ci * (CONV_CHUNK * NH), CONV_CHUNK * NH)
            xav = xa_ref[pl.ds(base, CONV_CHUNK * NH), :].reshape(CONV_CHUNK, NH, HD)
            acc = None
            for k in range(4):
                sl = pad[pl.ds(base + (9 - k) * NH, CONV_CHUNK * NH), :].reshape(CONV_CHUNK, NH, HD)
                term = sl * cw[k][None]
                acc = term if acc is None else acc + term
                dcw_ref[k] += jnp.sum(sl * xav, axis=0)
                if k == 1:
                    dcb_ref[...] += jnp.sum(sl, axis=0)
            dxa_s[pl.ds(base, CONV_CHUNK * NH), :] = acc.reshape(CONV_CHUNK * NH, HD)
            return carry
        lax.fori_loop(0, T // CONV_CHUNK, chunk, 0)
        for h in range(NH):
            dxa_ref[:, HD * h:HD * h + HD] = dxa_s[_zrows(h, T), :].astype(BF16)

    full = lambda shape: pl.BlockSpec(shape, lambda i: (0,) * len(shape))
    return pl.pallas_call(
        body, name=name, grid=(n,),
        out_shape=(pltpu.HBM((lx, D), BF16), jax.ShapeDtypeStruct((4, NH, HD), F32),
                   jax.ShapeDtypeStruct((NH, HD), F32)),
        in_specs=_tile_specs(NH, HD, n, lambda i: i) * 2 + [pl.BlockSpec((T, D), lambda i: (i, 0)), full((4, NH, HD)),
                                                            full((4, NH, HD)), full((NH, HD))],
        out_specs=(pl.BlockSpec((T, D), lambda i: (i, 0)), full((4, NH, HD)), full((NH, HD))),
        scratch_shapes=[pltpu.VMEM(((T + 16) * NH, HD), F32), pltpu.VMEM((T * NH, HD), F32),
                        pltpu.VMEM((T * NH, HD), F32)],
        compiler_params=_cp(1, vmem_mb=48),
    )(dxc_a, dxc_a, dxc_a, dxc_b, dxc_b, dxc_b, xa, conv_wz, dcw0, dcb0)


def _proj_bwd(dxa, dga, dgb, x, ctx, dxn, mods, norm_g, w_full, sgu):
    lx, lc = x.shape[0], ctx.shape[0]
    assert lc == T
    n = 1 + lx // T

    def body(dxa_ref, dga_ref, dgb_ref, w0, w1, w4, w2, w3, x_ref, c_ref, sc_ref, ng_ref, dxn_ref,
             u_ref, v_ref, dy_ref, g_ref, b_ref, sw_ref, bt_ref,
             gx_ref, dng_ref, dscx_ref, dshx_ref, dscc_ref, dshc_ref, du_ref, dv_ref, dws_ref, dbs_ref, dlg_ref, dlb_ref,
             mixed_s, dvn_s):
        i = pl.program_id(0)
        is_ctx = i == 0

        @pl.when(is_ctx)
        def _():
            for acc in (dng_ref, dscx_ref, dshx_ref, dscc_ref, dshc_ref, dws_ref, dbs_ref, dlg_ref, dlb_ref):
                acc[...] = jnp.zeros_like(acc)

        xv = jnp.where(is_ctx, c_ref[...], x_ref[...])
        sc1 = 1.0 + jnp.where(is_ctx, sc_ref[1:2, :], sc_ref[0:1, :])
        r = lax.rsqrt(jnp.mean(xv * xv, axis=-1, keepdims=True) + NORM_EPS)
        xn = xv * r
        ng = ng_ref[...]

        def norm_bwd(dhn, dsc_ref, dsh_ref, with_x):
            t = dhn * xn
            dng_ref[...] += jnp.sum(t * sc1, axis=0, keepdims=True)
            dsc_ref[...] += jnp.sum(t * ng, axis=0, keepdims=True)
            dsh_ref[...] += jnp.sum(dhn, axis=0, keepdims=True)
            if with_x:
                dxh = dhn * (ng * sc1)
                gx_ref[...] = dxn_ref[...] + r * (dxh - xn * jnp.mean(dxh * xn, axis=-1, keepdims=True))

        @pl.when(is_ctx)
        def _():
            norm_bwd(_dot_nt(dxa_ref[...], w0[...]), dscc_ref, dshc_ref, False)

        @pl.when(i > 0)
        def _():
            def sgu_chunk(ch):
                rows = slice(HD * ch, HD * ch + HD)
                du, dv = _sgu_bwd_chunk(u_ref[rows, :], v_ref[rows, :], dy_ref[rows, :], g_ref[...], b_ref[...],
                                        sw_ref, bt_ref, mixed_s, dvn_s, dws_ref, dbs_ref, dlg_ref, dlb_ref)
                du_ref[rows, :] = du
                dv_ref[rows, :] = dv

            dhn = _dot_nt(dxa_ref[...], w0[...])
            sgu_chunk(0)
            dhn = dhn + _dot_nt(dga_ref[...], w1[...])
            for ch in range(1, T // HD):
                sgu_chunk(ch)
            dhn = dhn + _dot_nt(dgb_ref[...], w4[...])
            dhn = dhn + _dot_nt(du_ref[...], w2[...]) + _dot_nt(dv_ref[...], w3[...])
            norm_bwd(dhn, dscx_ref, dshx_ref, True)

    every = pl.BlockSpec((T, D), lambda i: (i, 0))
    lat = pl.BlockSpec((T, D), lambda i: (jnp.maximum(i - 1, 0), 0))
    vec = pl.BlockSpec((1, D), lambda i: (0, 0))
    wsp = pl.BlockSpec((NH, HD, HD), lambda i: (0, 0, 0))
    bsp = pl.BlockSpec((NH, HD), lambda i: (0, 0))
    in_specs = [every, lat, lat] + [pl.BlockSpec((D, D), lambda i, k=k: (0, k)) for k in (0, 1, 4, 2, 3)]
    in_specs += [lat, pl.BlockSpec((T, D), lambda i: (0, 0)), pl.BlockSpec((8, D), lambda i: (0, 1)), vec, lat]
    in_specs += [lat, lat, lat, vec, vec, wsp, pl.BlockSpec((HD, NH), lambda i: (0, 0))]
    vs = jax.ShapeDtypeStruct((1, D), F32)
    zb = jax.ShapeDtypeStruct((lx, D), BF16)
    return pl.pallas_call(
        body, name="proj_bwd", grid=(n,),
        out_shape=(jax.ShapeDtypeStruct((lx, D), F32), vs, vs, vs, vs, vs, zb, zb,
                   jax.ShapeDtypeStruct((NH, HD, HD), F32), jax.ShapeDtypeStruct((NH, HD), F32), vs, vs),
        in_specs=in_specs, out_specs=(lat, vec, vec, vec, vec, vec, lat, lat, wsp, bsp, vec, vec),
        scratch_shapes=[pltpu.VMEM((HD, D), F32), pltpu.VMEM((HD, D), F32)], compiler_params=_cp(1, vmem_mb=56),
    )(dxa, dga, dgb, *([w_full] * 5), x, ctx, mods, norm_g, dxn, *sgu)


def _adam_math(w, g, m, v):
    m = ADAM_B1 * m + (1.0 - ADAM_B1) * g
    v = ADAM_B2 * v + (1.0 - ADAM_B2) * (g * g)
    m_hat = m / (1.0 - ADAM_B1 ** ADAM_STEP)
    v_hat = v / (1.0 - ADAM_B2 ** ADAM_STEP)
    delta = -ADAM_LR * (m_hat / (jnp.sqrt(v_hat) + ADAM_EPS) + ADAM_WD * w)
    return delta, m, v


def _adam_big(w, g, m, v, name):
    rows, cols = w.shape
    tr = 256

    def body(w_ref, g_ref, m_ref, v_ref, d_o, m_o, v_o):
        d, mm, vv = _adam_math(w_ref[...], g_ref[...], m_ref[...], v_ref[...])
        d_o[...] = d
        m_o[...] = mm
        v_o[...] = vv

    blk = pl.BlockSpec((tr, cols), lambda i: (i, 0))
    s = jax.ShapeDtypeStruct((rows, cols), F32)
    return pl.pallas_call(
        body, name=name, grid=(rows // tr,), out_shape=(s, s, s), in_specs=[blk] * 4, out_specs=(blk,) * 3,
        compiler_params=_cp(1, vmem_mb=48),
    )(w, g, m, v)


def _adam_small(items, tot):
    ni = len(items)
    pieces = [it[1] if isinstance(it[1], list) else None for it in items]
    flat = [a for it, pc in zip(items, pieces) for a in ((it[0], it[2], it[3]) if pc is not None else it)]
    n_in = len(flat) + 1
    out_shape = tuple(jax.ShapeDtypeStruct(it[0].shape, F32) for it, pc in zip(items, pieces)
                      for _ in range(4 if pc is not None else 3))
    n_out = len(out_shape)
    n_loads = sum(3 + (len(pc) if pc is not None else 1) for pc in pieces)

    def body(*refs):
        ins, tot_ref, outs = refs[:n_in - 1], refs[n_in - 1], refs[n_in:n_in + n_out]
        bufs = refs[n_in + n_out:n_in + n_out + 7 * ni]
        sem_in, sem_out = refs[n_in + n_out + 7 * ni:]
        loads, q_in, q_sem = [], 0, 0
        for k, pc in enumerate(pieces):
            w_b, g_b, m_b, v_b = bufs[7 * k:7 * k + 4]
            srcs = [(ins[q_in], w_b)]
            if pc is None:
                srcs.append((ins[q_in + 1], g_b))
                q_in += 1
            else:
                srcs += [(tot_ref.at[pl.ds(r0, nr), pl.ds(c0, nc)], g_b.at[pl.ds(d0, nr), :]) for r0, nr, c0, nc, d0 in pc]
            srcs += [(ins[q_in + 1], m_b), (ins[q_in + 2], v_b)]
            q_in += 3
            mine = []
            for src, dst in srcs:
                mine.append(pltpu.make_async_copy(src, dst, sem_in.at[q_sem]))
                q_sem += 1
            loads.append(mine)
        for mine in loads:
            for cp in mine:
                cp.start()
        stores, q_out = [], 0
        for k, pc in enumerate(pieces):
            for cp in loads[k]:
                cp.wait()
            w_b, g_b, m_b, v_b = bufs[7 * k:7 * k + 4]
            res = _adam_math(w_b[...], g_b[...], m_b[...], v_b[...])
            srcs = []
            for q in range(3):
                bufs[7 * k + 4 + q][...] = res[q]
                srcs.append(bufs[7 * k + 4 + q])
            if pc is not None:
                srcs.append(g_b)
            for src in srcs:
                cp = pltpu.make_async_copy(src, outs[q_out], sem_out.at[q_out])
                cp.start()
                stores.append(cp)
                q_out += 1
        for cp in stores:
            cp.wait()

    scratch = [pltpu.VMEM(it[0].shape, F32) for it in items for _ in range(7)]
    scratch += [pltpu.SemaphoreType.DMA((n_loads,)), pltpu.SemaphoreType.DMA((n_out,))]
    res = pl.pallas_call(
        body, name="adam_small", out_shape=out_shape, in_specs=[HBM] * n_in, out_specs=(HBM,) * n_out,
        scratch_shapes=scratch, compiler_params=_cp(vmem_mb=40),
    )(*[_in_hbm(a) for a in flat], _in_hbm(tot))
    outs, q = [], 0
    for pc in pieces:
        outs.append(tuple(res[q:q + 3]) + ((res[q + 3],) if pc is not None else (None,)))
        q += 4 if pc is not None else 3
    return outs


def kernel(x, c, ctx, c_ctx, ada_w, ada_b, norm_g, w_in, conv_w, conv_b, lru_wa, lru_ba, lru_wx, lru_bx, lru_lambda, sgu_ln_g, sgu_ln_b, sgu_w, sgu_b, w_out, final_g, loss_target, m_c_ctx, m_ada_w, m_ada_b, m_norm_g, m_w_in, m_conv_w, m_conv_b, m_lru_wa, m_lru_ba, m_lru_wx, m_lru_bx, m_lru_lambda, m_sgu_ln_g, m_sgu_ln_b, m_sgu_w, m_sgu_b, m_w_out, m_final_g, v_c_ctx, v_ada_w, v_ada_b, v_norm_g, v_w_in, v_conv_w, v_conv_b, v_lru_wa, v_lru_ba, v_lru_wx, v_lru_bx, v_lru_lambda, v_sgu_ln_g, v_sgu_ln_b, v_sgu_w, v_sgu_b, v_w_out, v_final_g):
    ix, iy, ic = lax.axis_index("x"), lax.axis_index("y"), lax.axis_index("c")
    chip = 2 * ix + iy
    dev = 2 * chip + ic
    lx = x.shape[1]
    lc = ctx.shape[1]

    smalls = jnp.concatenate([conv_w[0], lru_lambda[0], jnp.zeros((10, 256), F32)], axis=0)
    c_ctx2 = c_ctx.reshape(1, D)
    ada_b_j = lax.dynamic_slice(ada_b, (0, 768 * chip), (1, 768))
    mods, c_slots, sm_all, w_in_full, wo_land, ada_land = _gather_in(c, c_ctx2, ada_w[0], ada_b_j, w_in[0], w_out[0],
                                                                     smalls)
    wo_ss, wo_rs, ada_ss, ada_rs, wo_land, ada_land, token = _late_gather_start(wo_land, ada_land)
    mods = mods + token[0:1, 0:1]
    sm3 = sm_all.reshape(NCHIP, 16, 256)
    conv_w_full = sm3[:, 0:4, :].transpose(1, 0, 2).reshape(4, D)
    lam_full = sm3[:, 4:6, :].transpose(1, 0, 2).reshape(2, D)
    conv_wz = conv_w_full.reshape(4, NH, HD)
    conv_bz = conv_b.reshape(NH, HD)
    lamcat = lam_full.reshape(2, NH, HD).transpose(1, 0, 2).reshape(NH, 2 * HD)
    wa, wx, ba, bx = lru_wa[0], lru_wx[0], lru_ba[0], lru_bx[0]
    wcat = jnp.concatenate([wa[0], wx[0], wa[1], wx[1]], axis=-1).astype(BF16)
    bcat = jnp.concatenate([ba[0], bx[0], ba[1], bx[1]], axis=-1)
    sgu_wb = sgu_w[0].astype(BF16)
    sgu_bt = sgu_b[0].T
    final_g2 = final_g.reshape(1, D)

    zero_s = jnp.zeros((NH, HD), F32)
    hn, xa_all, ga, u, v, gb, ys = _proj(x[0], ctx[0], mods, norm_g, w_in_full, (sgu_ln_g, sgu_ln_b, sgu_wb, sgu_bt))
    xcz, af, ab, hf, hb, gf, gb_l, _, hb0 = _lru_fwd(xa_all, conv_wz, conv_bz, wcat, bcat, lamcat, "lru_fwd")

    w_out_full = _late_gather_wait(wo_land, wo_ss, wo_rs, "w_out", hf, "late_gather_wait_w_out")
    (loss_part, dfg, dgx, dxn, y, do, dga, dgb, dyl_z, dys) = _out_fwd_bwd(
        hf, hb, ga, gb, ys, x[0], loss_target[0], mods, final_g2, w_out_full)

    dxc_b, dxc_f, dwc, dbc, dlc = _lru_bwd(xcz, dyl_z, hf, hb, af, ab, gf, gb_l, hb0, wcat, lamcat, "lru_bwd")
    dxa, dcw, dcb = _conv_bwd(dxc_b, dxc_f, xa_all, conv_wz, jnp.zeros((4, NH, HD), F32), zero_s, "conv_bwd")
    dxa = _in_hbm(dxa)

    grad_x, dng, dsc_x, dsh_x, dsc_c, dsh_c, du, dv, d_sgu_w, d_sgu_b, d_ln_g, d_ln_b = _proj_bwd(
        dxa, dga, dgb, x[0], ctx[0], dxn, mods, norm_g, w_in_full, (u, v, dys, sgu_ln_g, sgu_ln_b, sgu_wb, sgu_bt))
    dzs = [dxa, dga, du, dv, dgb]

    dmx = jnp.concatenate([dsh_x, dsc_x, dgx], axis=0)
    dmc = jnp.concatenate([dsh_c, dsc_c, jnp.zeros((1, D), F32)], axis=0)
    slot = jnp.concatenate([dmx, loss_part], axis=0)
    slots = lax.dynamic_update_slice(jnp.zeros((32, D), F32), slot, (4 * dev, 0))
    vecs = jnp.concatenate([dfg, dng, dcb.reshape(1, D), d_ln_g, d_ln_b, dcw.reshape(4, D), dmc,
                            jnp.zeros((4, D), F32), slots], axis=0)
    d_sgu_w4 = d_sgu_w.reshape(4, 256, HD).transpose(1, 0, 2).reshape(256, 4 * HD)
    pad8 = lambda a: jnp.pad(a, ((0, 8 - a.shape[0]), (0, 4 * HD - a.shape[1])))
    pack = jnp.concatenate([dwc.reshape(NH * HD, 4 * HD), pad8(dbc), pad8(dlc), d_sgu_w4, pad8(d_sgu_b),
                            vecs.reshape(96, 4 * HD), jnp.zeros((8, 4 * HD), F32)], axis=0)
    g_w_in, g_w_out, tot = _grads_reduce(hn, dzs, lc, y, do, _in_hbm(pack))

    n_w = NH * HD
    g_lru_wa = [(0, n_w, 2 * HD * d, HD, n_w * d) for d in range(2)]
    g_lru_wx = [(0, n_w, 2 * HD * d + HD, HD, n_w * d) for d in range(2)]
    g_lru_ba = [(n_w, NH, 2 * HD * d, HD, NH * d) for d in range(2)]
    g_lru_bx = [(n_w, NH, 2 * HD * d + HD, HD, NH * d) for d in range(2)]
    g_sgu_w = [(1040, 256, HD * q, HD, 256 * q) for q in range(4)]
    g_sgu_b = [(1296, NH, 0, HD, 0)]
    g_lc = tot[1032:1040, 0:2 * HD]
    tv = tot[1304:1400].reshape(48, D)
    g_final_g, g_norm_g, g_conv_b, g_ln_g, g_ln_b = tv[0:1], tv[1:2], tv[2:3], tv[3:4], tv[4:5]
    g_conv_w_full = tv[5:9]
    dmc_tot = tv[9:12].reshape(1, 3 * D)
    slots_all = tv[16:48].reshape(8, 4, D)
    dmx_all = slots_all[:, 0:3, :].reshape(8, 3 * D)
    c_all = c_slots.reshape(8, 8, D)[:, 0, :]
    g_lam_full = jnp.stack([g_lc[:, 0:HD], g_lc[:, HD:2 * HD]]).reshape(2, D)
    g_conv_w = lax.dynamic_slice(g_conv_w_full, (0, 256 * chip), (4, 256))
    g_lam = lax.dynamic_slice(g_lam_full, (0, 256 * chip), (2, 256))
    dmx_all_j = lax.dynamic_slice(dmx_all, (0, 768 * chip), (8, 768))
    dmc_j = lax.dynamic_slice(dmc_tot, (0, 768 * chip), (1, 768))
    ada_full = _late_gather_wait(ada_land, ada_ss, ada_rs, "ada_w", _in_hbm(tot), "late_gather_wait_ada_w")
    g_ada_w, g_ada_b, g_c_ctx = _ada_bwd(c_all, dmx_all_j, dmc_j, dmx_all, dmc_tot, c_ctx2, ada_full)

    big = {
        "ada_w": _adam_big(ada_w[0], g_ada_w, m_ada_w[0], v_ada_w[0], "adam_ada_w"),
        "w_in": _adam_big(w_in[0], g_w_in, m_w_in[0], v_w_in[0], "adam_w_in"),
        "w_out": _adam_big(w_out[0], g_w_out, m_w_out[0], v_w_out[0], "adam_w_out"),
    }
    small_in = {
        "c_ctx": (c_ctx, g_c_ctx, m_c_ctx, v_c_ctx, (1, D)),
        "ada_b": (ada_b, g_ada_b, m_ada_b, v_ada_b, (1, 3 * D)),
        "norm_g": (norm_g, g_norm_g, m_norm_g, v_norm_g, (1, D)),
        "conv_w": (conv_w, g_conv_w, m_conv_w, v_conv_w, (4, 256)),
        "conv_b": (conv_b, g_conv_b, m_conv_b, v_conv_b, (1, D)),
        "lru_wa": (lru_wa, g_lru_wa, m_lru_wa, v_lru_wa, (2 * NH * HD, HD)),
        "lru_ba": (lru_ba, g_lru_ba, m_lru_ba, v_lru_ba, (2 * NH, HD)),
        "lru_wx": (lru_wx, g_lru_wx, m_lru_wx, v_lru_wx, (2 * NH * HD, HD)),
        "lru_bx": (lru_bx, g_lru_bx, m_lru_bx, v_lru_bx, (2 * NH, HD)),
        "lru_lambda": (lru_lambda, g_lam, m_lru_lambda, v_lru_lambda, (2, 256)),
        "sgu_ln_g": (sgu_ln_g, g_ln_g, m_sgu_ln_g, v_sgu_ln_g, (1, D)),
        "sgu_ln_b": (sgu_ln_b, g_ln_b, m_sgu_ln_b, v_sgu_ln_b, (1, D)),
        "sgu_w": (sgu_w, g_sgu_w, m_sgu_w, v_sgu_w, (NH * HD, HD)),
        "sgu_b": (sgu_b, g_sgu_b, m_sgu_b, v_sgu_b, (NH, HD)),
        "final_g": (final_g, g_final_g, m_final_g, v_final_g, (1, D)),
    }
    names_small = list(small_in)
    res_small = _adam_small([tuple(a if isinstance(a, list) else a.reshape(small_in[k][4]) for a in small_in[k][:4])
                             for k in names_small], tot)
    full_shapes = {"ada_w": ada_w.shape, "w_in": w_in.shape, "w_out": w_out.shape}
    grads, deltas, new_m, new_v = {}, {}, {}, {}
    for k in ("ada_w", "w_in", "w_out"):
        g = {"ada_w": g_ada_w, "w_in": g_w_in, "w_out": g_w_out}[k]
        grads[k] = g.reshape(full_shapes[k])
        deltas[k], new_m[k], new_v[k] = (a.reshape(full_shapes[k]) for a in big[k])
    for k, res in zip(names_small, res_small):
        shape = small_in[k][0].shape
        grads[k] = (small_in[k][1] if res[3] is None else res[3]).reshape(shape)
        deltas[k], new_m[k], new_v[k] = (a.reshape(shape) for a in res[:3])

    loss = jnp.sum(slots_all[:, 3, 0:3])
    order = ["c_ctx", "ada_w", "ada_b", "norm_g", "w_in", "conv_w", "conv_b", "lru_wa", "lru_ba", "lru_wx", "lru_bx",
             "lru_lambda", "sgu_ln_g", "sgu_ln_b", "sgu_w", "sgu_b", "w_out", "final_g"]
    return (loss, grad_x.reshape(x.shape), *[grads[k] for k in order], *[deltas[k] for k in order],
            *[new_m[k] for k in order], *[new_v[k] for k in order])
```

```python
import jax
import jax.numpy as jnp
from jax import lax
from jax.experimental import pallas as pl
from jax.experimental.pallas import tpu as pltpu

F32 = jnp.float32
BF16 = jnp.bfloat16

D = 1024
NH = 8
HD = 128
NCHIP = 4
T = 256
NORM_EPS = 1e-6
LN_EPS = 1e-5
LRU_C = 8.0
ADAM_LR = 0.001
ADAM_B1 = 0.9
ADAM_B2 = 0.999
ADAM_EPS = 1e-08
ADAM_WD = 0.01
ADAM_STEP = 10

VMEM = pl.BlockSpec(memory_space=pltpu.VMEM)
ANY = pl.BlockSpec(memory_space=pl.ANY)
MESH = pl.DeviceIdType.MESH


def _cp(n_grid=0, vmem_mb=None):
    kw = {}
    if n_grid:
        kw["dimension_semantics"] = ("arbitrary",) * n_grid
    if vmem_mb:
        kw["vmem_limit_bytes"] = vmem_mb << 20
    return pltpu.CompilerParams(**kw)


def _sigmoid(x):
    return 0.5 * jnp.tanh(0.5 * x) + 0.5


def _silu_and_grad(x):
    s = _sigmoid(x)
    return x * s, s * (1.0 + x * (1.0 - s))


_GELU_K = 0.7978845608028654
_GELU_C = 0.044715


def _gelu_and_grad(x):
    x2 = x * x
    th = jnp.tanh(x * (_GELU_K + (_GELU_K * _GELU_C) * x2))
    p = 0.5 + 0.5 * th
    g = x * p
    dg = p + g * (1.0 - th) * (_GELU_K + (3.0 * _GELU_K * _GELU_C) * x2)
    return g, dg


def _softplus(x):
    return jnp.maximum(x, 0.0) + jnp.log1p(jnp.exp(-jnp.abs(x)))


def _lru_gate(pre, lam_row, d, off=None):
    off = 256 * d if off is None else off
    r = _sigmoid(pre[:, off:off + HD])
    gi = _sigmoid(pre[:, off + HD:off + 2 * HD])
    lam = lam_row[:, HD * d:HD * d + HD]
    sp = _softplus(-lam)
    la = (-LRU_C) * r * sp
    a = jnp.exp(la)
    x2 = 2.0 * la
    m2 = jnp.where(x2 > -1e-3, -x2 * (1.0 + 0.5 * x2), 1.0 - a * a)
    mult = jnp.sqrt(m2)
    return r, gi, lam, sp, a, mult


def _dot(a, b):
    return jnp.dot(a, b, preferred_element_type=F32)


def _dot_tn(a, b):
    return lax.dot_general(a, b, (((0,), (0,)), ((), ())), preferred_element_type=F32)


def _dot_nt(a, b):
    return lax.dot_general(a, b, (((1,), (1,)), ((), ())), preferred_element_type=F32)


def _mo(v, m):
    return v if isinstance(v, int) else pl.multiple_of(v, m)


def _zrows(h, n):
    return pl.ds(h, n, stride=NH)


def _gather_in(c, c_ctx, ada_w, ada_b_j, w_in, w_out, smalls, lru_sgu):
    nch = [1, 4]
    wrows = lambda cc, q: (pl.ds(_mo(512 * cc, 16), 512) if q is None
                           else pl.ds(_mo(512 * cc + (512 // nch[1]) * q, 16), 512 // nch[1]))
    specs = [
        ((64, 256), F32, lambda r, jj, cc, q=None: r.at[pl.ds(_mo(16 * jj + 8 * cc, 8), 8), :]),
        ((D, 5120), BF16, lambda r, jj, cc, q=None: r.at[wrows(cc, q), pl.ds(_mo(1280 * jj, 128), 1280)]),
    ]
    halves = [lambda r, cc, q=None: r.at[pl.ds(_mo(8 * cc, 8), 8), :],
              lambda r, cc, q=None: r.at[wrows(cc, q), :]]
    na = len(specs)
    sem_base = [0, 6 * nch[0]]
    sidx = lambda a, q, k: sem_base[a] + 6 * q + k
    n_tiny = 6 * sum(nch)
    n_sem = n_tiny + 10

    def body(c_ref, cc_ref, ada_ref, adab_ref, win_ref, wout_ref, sm_ref, wa_ref, wx_ref, ba_ref, bx_ref, sw_ref,
             mods_o, call_o, sm_o, win_o, wol_o, adal_o, wcat_o, bcat_o, swb_o,
             s_win, s_ada, s_wout, f_win, f_ada, f_wout, cslot, lhs, mbuf,
             send_sems, recv_sems, local_sems, load_sems, f_w, f_sw, s_wcat, s_swb, prep_sems):
        x, y, c = lax.axis_index("x"), lax.axis_index("y"), lax.axis_index("c")
        j = 2 * x + y
        dev = 2 * j + c
        sib = (x, y, 1 - c)
        chips = [(1 - x, y), (x, 1 - y), (1 - x, 1 - y)]
        cj = [2 * cx + cy for cx, cy in chips]
        outs = [sm_o, win_o]
        srcs = [sm_ref, s_win]

        def copy(idx, src, dst, to):
            return pltpu.make_async_remote_copy(src_ref=src, dst_ref=dst, send_sem=send_sems.at[idx],
                                                recv_sem=recv_sems.at[idx], device_id=to, device_id_type=MESH)

        sends = []

        def start(cp):
            cp.start()
            sends.append(cp)

        cslot[...] = jnp.zeros_like(cslot)
        cslot[0:1, :] = c_ref[...]
        my_slot = pl.ds(_mo(8 * dev, 8), 8)
        others = [sib] + [(*chips[k], c) for k in range(3)] + [(*chips[k], 1 - c) for k in range(3)]
        other_dev = [dev + 1 - 2 * c] + [2 * cj[k] + c for k in range(3)] + [2 * cj[k] + 1 - c for k in range(3)]
        base = n_tiny
        for r in range(7):
            start(copy(base + r, cslot, call_o.at[my_slot, :], others[r]))
        call_o[my_slot, :] = cslot[...]

        crow = 512 // nch[1]
        loads = []
        for cc in (c, 1 - c):
            for q in range(nch[1]):
                rows = pl.ds(_mo(512 * cc + crow * q, 16), crow)
                loads.append(pltpu.make_async_copy(win_ref.at[rows, :], f_win.at[rows, :], load_sems.at[len(loads)]))
        loads.append(pltpu.make_async_copy(ada_ref, f_ada, load_sems.at[len(loads)]))
        loads.append(pltpu.make_async_copy(wout_ref, f_wout, load_sems.at[len(loads)]))
        for ld in loads:
            ld.start()
        prep = []
        for d in range(2):
            for q, (w_src, b_src) in enumerate([(wa_ref, ba_ref), (wx_ref, bx_ref)]):
                prep.append(pltpu.make_async_copy(w_src.at[d], f_w.at[2 * d + q], prep_sems.at[len(prep)]))
                prep.append(pltpu.make_async_copy(b_src.at[d], bcat_o.at[:, pl.ds(HD * (2 * d + q), HD)],
                                                  prep_sems.at[len(prep)]))
        prep.append(pltpu.make_async_copy(sw_ref, f_sw, prep_sems.at[len(prep)]))
        for cp in prep:
            cp.start()
        for k in range(2):
            start(copy(sidx(0, 0, k), halves[0](srcs[0], c), specs[0][2](outs[0], j, c), (*chips[k], c)))
        for q in range(nch[1]):
            loads[q].wait()
            rows = pl.ds(_mo(512 * c + crow * q, 16), crow)
            s_win[rows, :] = f_win[rows, :].astype(BF16)
            for k in range(2):
                start(copy(sidx(1, q, k), halves[1](s_win, c, q), specs[1][2](win_o, j, c, q), (*chips[k], c)))
        for q in range(nch[1]):
            loads[nch[1] + q].wait()
            rows = pl.ds(_mo(512 * (1 - c) + crow * q, 16), crow)
            s_win[rows, :] = f_win[rows, :].astype(BF16)
        local = []
        for a in range(na):
            for cc in range(2):
                lc = pltpu.make_async_copy(halves[a](srcs[a], cc), specs[a][2](outs[a], j, cc), local_sems.at[2 * a + cc])
                lc.start()
                local.append(lc)
        loads[2 * nch[1]].wait()
        s_ada[...] = f_ada[...].astype(BF16)
        loads[2 * nch[1] + 1].wait()
        s_wout[...] = f_wout[...].astype(BF16)
        for q, (src, dst) in enumerate([(s_wout, wol_o.at[pl.ds(_mo(512 * j, 16), 512), :]),
                                        (s_ada, adal_o.at[:, pl.ds(_mo(768 * j, 128), 768)])]):
            lc = pltpu.make_async_copy(src, dst, local_sems.at[2 * na + q])
            lc.start()
            local.append(lc)
        for cp in prep:
            cp.wait()
        for q in range(4):
            s_wcat[:, :, HD * q:HD * q + HD] = f_w[q].astype(BF16)
        s_swb[...] = f_sw[...].astype(BF16)
        for n, (src, dst) in enumerate([(s_wcat, wcat_o), (s_swb, swb_o)]):
            lc = pltpu.make_async_copy(src, dst, prep_sems.at[len(prep) + n])
            lc.start()
            local.append(lc)

        for r in range(7):
            slot = call_o.at[pl.ds(_mo(8 * other_dev[r], 8), 8), :]
            copy(base + r, slot, slot, sib).wait_recv()
        lhs[...] = jnp.zeros_like(lhs)
        for b in range(8):
            cv = call_o[8 * b:8 * b + 1, :]
            lhs[b:b + 1, :] = cv * _sigmoid(cv)
        cv = cc_ref[...]
        lhs[8:9, :] = cv * _sigmoid(cv)
        mbuf[j] = _dot(lhs[...].astype(BF16), s_ada[...]) + adab_ref[...]
        for k in range(3):
            start(copy(base + 7 + k, mbuf.at[j], mbuf.at[j], (*chips[k], c)))
        for k in range(3):
            copy(base + 7 + k, mbuf.at[cj[k]], mbuf.at[cj[k]], sib).wait_recv()
        mods_o[...] = jnp.zeros_like(mods_o)
        for jj in range(NCHIP):
            mods_o[0:1, 768 * jj:768 * jj + 768] = mbuf[jj, pl.ds(dev, 1), :]
            mods_o[1:2, 768 * jj:768 * jj + 768] = mbuf[jj, 8:9, :]

        kx = [1 - x, x, 1 - x]
        ky = [y, 1 - y, 1 - y]
        pick = lambda k, lst: jnp.where(k == 0, lst[0], jnp.where(k == 1, lst[1], lst[2]))
        for a in range(na):
            for q in range(nch[a]):
                for step, k in enumerate([c, 1 - c]):
                    reg = specs[a][2](outs[a], pick(k, cj), c, q)
                    copy(sidx(a, q, k), reg, reg, sib).wait_recv()
                    if step == 0:
                        start(copy(sidx(a, q, 2), reg, reg, (pick(1 - c, kx), pick(1 - c, ky), c)))
                    start(copy(sidx(a, q, 3 + k), reg, reg, sib))
        for a in range(na):
            for q in range(nch[a]):
                reg = specs[a][2](outs[a], cj[2], c, q)
                copy(sidx(a, q, 2), reg, reg, sib).wait_recv()
                start(copy(sidx(a, q, 5), reg, reg, sib))
        for a in range(na):
            for q in range(nch[a]):
                for k in range(3):
                    reg = specs[a][2](outs[a], cj[k], 1 - c, q)
                    copy(sidx(a, q, 3 + k), reg, reg, sib).wait_recv()
        for cp in sends:
            cp.wait_send()
        for lc in local:
            lc.wait()

    out_shape = (jax.ShapeDtypeStruct((8, 3 * D), F32), jax.ShapeDtypeStruct((64, D), F32),
                 jax.ShapeDtypeStruct(specs[0][0], F32), jax.ShapeDtypeStruct(specs[1][0], BF16),
                 jax.ShapeDtypeStruct((2048, D), BF16), jax.ShapeDtypeStruct((D, 3 * D), BF16),
                 jax.ShapeDtypeStruct((NH, HD, 4 * HD), BF16), jax.ShapeDtypeStruct((NH, 4 * HD), F32),
                 jax.ShapeDtypeStruct((NH, HD, HD), BF16))
    return pl.pallas_call(
        body, name="gather_in", out_shape=out_shape,
        in_specs=[VMEM, VMEM, ANY, VMEM, ANY, ANY, VMEM] + [ANY] * 5,
        out_specs=(VMEM, VMEM, VMEM, ANY, ANY, ANY, ANY, ANY, ANY),
        scratch_shapes=[pltpu.VMEM((D, 1280), BF16), pltpu.VMEM((D, 768), BF16), pltpu.VMEM((512, D), BF16),
                        pltpu.VMEM((D, 1280), F32), pltpu.VMEM((D, 768), F32), pltpu.VMEM((512, D), F32),
                        pltpu.VMEM((8, D), F32), pltpu.VMEM((16, D), F32), pltpu.VMEM((NCHIP, 16, 768), F32),
                        pltpu.SemaphoreType.DMA((n_sem,)), pltpu.SemaphoreType.DMA((n_sem,)),
                        pltpu.SemaphoreType.DMA((2 * na + 2,)), pltpu.SemaphoreType.DMA((2 * nch[1] + 2,)),
                        pltpu.VMEM((4, NH, HD, HD), F32), pltpu.VMEM((NH, HD, HD), F32),
                        pltpu.VMEM((NH, HD, 4 * HD), BF16), pltpu.VMEM((NH, HD, HD), BF16),
                        pltpu.SemaphoreType.DMA((11,))],
        compiler_params=_cp(vmem_mb=56),
    )(c, c_ctx, ada_w, ada_b_j, w_in, w_out, smalls, *[_in_hbm(a) for a in lru_sgu])


HBM = pl.BlockSpec(memory_space=pltpu.HBM)
SEM = pl.BlockSpec(memory_space=pltpu.SEMAPHORE)


def _in_hbm(a):
    return pltpu.with_memory_space_constraint(a, pltpu.HBM)


def _late_gather_regions(x, y, c):
    chips = [(1 - x, y), (x, 1 - y), (1 - x, 1 - y)]
    wo_reg = lambda r, jj, cc: r.at[pl.ds(_mo(512 * jj + 256 * cc, 16), 256), :]
    ada_reg = lambda r, jj, cc: r.at[pl.ds(_mo(512 * cc, 16), 512), pl.ds(_mo(768 * jj, 128), 768)]
    return chips, wo_reg, ada_reg


def _late_gather_start(wo_land, ada_land):
    def body(wol_ref, adal_ref, wo_ss, wo_rs, ada_ss, ada_rs, wol_thru, adal_thru, token):
        x, y, c = lax.axis_index("x"), lax.axis_index("y"), lax.axis_index("c")
        j = 2 * x + y
        chips, wo_reg, ada_reg = _late_gather_regions(x, y, c)
        for k in range(3):
            for cc in range(2):
                pltpu.make_async_remote_copy(src_ref=wo_reg(wol_ref, j, c), dst_ref=wo_reg(wol_ref, j, c),
                                             send_sem=wo_ss.at[2 * k + cc], recv_sem=wo_rs.at[2 * k + c],
                                             device_id=(*chips[k], cc), device_id_type=MESH).start()
        for k in range(3):
            for cc in range(2):
                pltpu.make_async_remote_copy(src_ref=ada_reg(adal_ref, j, c), dst_ref=ada_reg(adal_ref, j, c),
                                             send_sem=ada_ss.at[2 * k + cc], recv_sem=ada_rs.at[2 * k + c],
                                             device_id=(*chips[k], cc), device_id_type=MESH).start()
        token[...] = jnp.zeros_like(token)

    sems = pltpu.SemaphoreType.DMA((6,))
    return pl.pallas_call(
        body, name="late_gather_start",
        out_shape=(sems, sems, sems, sems, pltpu.HBM(wo_land.shape, BF16), pltpu.HBM(ada_land.shape, BF16),
                   jax.ShapeDtypeStruct((8, 128), F32)),
        in_specs=(HBM, HBM), out_specs=(SEM, SEM, SEM, SEM, HBM, HBM, VMEM), input_output_aliases={0: 4, 1: 5},
        compiler_params=pltpu.CompilerParams(has_side_effects=pltpu.SideEffectType.DATAFLOW_SIDE_EFFECTING),
    )(_in_hbm(wo_land), _in_hbm(ada_land))


def _late_gather_wait(land, send_sems, recv_sems, which, after, name):
    def body(land_ref, ss, rs, after_ref, land_out):
        x, y, c = lax.axis_index("x"), lax.axis_index("y"), lax.axis_index("c")
        j = 2 * x + y
        chips, wo_reg, ada_reg = _late_gather_regions(x, y, c)
        reg = wo_reg if which == "w_out" else ada_reg
        for k in range(3):
            kj = 2 * chips[k][0] + chips[k][1]
            for cc in range(2):
                cp = pltpu.make_async_remote_copy(src_ref=reg(land_ref, j, c), dst_ref=reg(land_ref, kj, cc),
                                                  send_sem=ss.at[2 * k + cc], recv_sem=rs.at[2 * k + cc],
                                                  device_id=(*chips[k], cc), device_id_type=MESH)
                cp.wait_send()
                cp.wait_recv()

    return pl.pallas_call(
        body, name=name, out_shape=pltpu.HBM(land.shape, land.dtype),
        in_specs=(HBM, SEM, SEM, ANY), out_specs=HBM, input_output_aliases={0: 0},
        compiler_params=pltpu.CompilerParams(has_side_effects=pltpu.SideEffectType.DATAFLOW_SIDE_EFFECTING),
    )(land, send_sems, recv_sems, after)


RCHUNK = 16


def _grads_reduce(hn, dzs, lc, y, do, pack):
    rp = pack.shape[0]
    hp = rp // 2
    assert hp % RCHUNK == 0
    wi_w = 1280
    lx = hn.shape[0] - lc
    lt = lx + lc
    n_dz = len(dzs)

    def body(*refs):
        hn_hbm, dz_hbm = refs[0], refs[1:1 + n_dz]
        y_hbm, do_hbm, pk_hbm, wi_out, wo_out, pk_out = refs[1 + n_dz:7 + n_dz]
        (hn_mine, hn_other, dzbuf, wi_other, wi_mine, wi_recv, wi_send, wi_rb,
         y_blk, do_mine, do_other, wo_other, wo_mine, wo_recv, wo_send, wo_rb,
         pk_mine, pk_recv, pk_send, pk_rb, pk_own, send_sems, recv_sems, local_sems) = refs[7 + n_dz:]
        x, y, c = lax.axis_index("x"), lax.axis_index("y"), lax.axis_index("c")
        j = 2 * x + y
        sib = (x, y, 1 - c)
        chips = [(1 - x, y), (x, 1 - y), (1 - x, 1 - y)]
        cj = [2 * cx + cy for cx, cy in chips]
        near = (jnp.where(c == 0, 1 - x, x), jnp.where(c == 0, y, 1 - y), c)
        slabs = [cj[2], cj[0], cj[1], j]

        def copy(k, src, dst, to):
            return pltpu.make_async_remote_copy(src_ref=src, dst_ref=dst, send_sem=send_sems.at[k],
                                                recv_sem=recv_sems.at[k], device_id=to, device_id_type=MESH)

        def local(k, src, dst):
            cp = pltpu.make_async_copy(src, dst, local_sems.at[k])
            cp.start()
            return cp

        rows_half = lambda r, cc, n: r.at[pl.ds(_mo(cc * n, 16), n), :]
        cols_half = lambda r, cc, n: r.at[:, pl.ds(_mo(cc * n, 128), n)]
        pk_piece = lambda r, cc, jj: r.at[pl.ds(_mo(cc * hp, 16), hp), pl.ds(_mo(jj * 128, 128), 128)]

        sends = []

        def start(cp):
            cp.start()
            sends.append(cp)

        def dz_pieces(s):
            g0 = wi_w * s
            k0, off0 = g0 // D, g0 % D
            w0 = min(D - off0, wi_w)
            pieces = [(k0, off0, w0, 0)]
            if w0 < wi_w:
                pieces.append((k0 + 1, 0, wi_w - w0, w0))
            return pieces

        def dz_copies(s):
            cps = []
            for q, (k, off, w, dst) in enumerate(dz_pieces(s)):
                cps.append(pltpu.make_async_copy(dz_hbm[k].at[pl.ds(lc if k == 0 else 0, lx), pl.ds(off, w)],
                                                 dzbuf.at[pl.ds(lc, lx), pl.ds(dst, w)], local_sems.at[11 + q]))
            if s == 0:
                cps.append(pltpu.make_async_copy(dz_hbm[0].at[pl.ds(0, lc), :], dzbuf.at[pl.ds(0, lc), pl.ds(0, D)],
                                                 local_sems.at[13]))
            return cps

        def dz_load(sl):
            for s in range(NCHIP):
                @pl.when(sl == s)
                def _():
                    if s == 0:
                        dzbuf[pl.ds(0, lc), pl.ds(D, wi_w - D)] = jnp.zeros((lc, wi_w - D), BF16)
                    else:
                        dzbuf[pl.ds(0, lc), :] = jnp.zeros((lc, wi_w), BF16)
                    for cp in dz_copies(s):
                        cp.start()

        def dz_wait(sl):
            for s in range(NCHIP):
                @pl.when(sl == s)
                def _():
                    for cp in dz_copies(s):
                        cp.wait()

        l_pk = local(0, rows_half(pk_hbm, c, hp), pk_mine)
        start(copy(0, rows_half(pk_hbm, 1 - c, hp), pk_recv, sib))
        col = lambda r, cc: r.at[:, pl.ds(_mo(cc * 512, 128), 512)]
        do_loads = [local(7, col(do_hbm, c), do_mine), local(14, col(do_hbm, 1 - c), do_other)]
        hn_loads = [local(2, col(hn_hbm, c), hn_mine), local(4, col(hn_hbm, 1 - c), hn_other)]
        y_copy = lambda s: pltpu.make_async_copy(col(y_hbm, slabs[s]), y_blk, local_sems.at[1])
        y_copy(0).start()
        dz_load(slabs[0])

        def pair_sum(mine, recv, send, nrows, keep, relayed=None):
            def step(i, carry):
                rows = pl.ds(_mo(i * RCHUNK, RCHUNK), RCHUNK)
                s = mine[rows, :] + recv[rows, :].astype(F32)
                if relayed is not None:
                    s = s + relayed[rows, :].astype(F32)
                if keep:
                    mine[rows, :] = s
                if send is not None:
                    send[rows, :] = s.astype(BF16)
                return carry
            lax.fori_loop(0, nrows // RCHUNK, step, 0)

        def chip_sum(own, rb, nrows, terms=(0, 1, 2)):
            def step(i, carry):
                rows = pl.ds(_mo(i * RCHUNK, RCHUNK), RCHUNK)
                acc = own[rows, :]
                for q in terms:
                    acc = acc + rb[q, rows, :].astype(F32)
                own[rows, :] = acc
                return carry
            lax.fori_loop(0, nrows // RCHUNK, step, 0)

        w_in_g = dict(other=wi_other, mine=wi_mine, recv=wi_recv, send=wi_send, rb=wi_rb, p1_sems=(2, 3, 4, 5), p2_sem=12,
                      p1=[None] * NCHIP, wait_load=lambda s: dz_wait(slabs[s]), load=lambda s: dz_load(slabs[s]),
                      dot_other=lambda: _dot_tn(hn_other[...], dzbuf[...]), dot_mine=lambda: _dot_tn(hn_mine[...], dzbuf[...]))
        w_out_g = dict(other=wo_other, mine=wo_mine, recv=wo_recv, send=wo_send, rb=wo_rb, p1_sems=(1, 24, 25, 26), p2_sem=9,
                       p1=[None] * NCHIP, wait_load=lambda s: y_copy(s).wait(), load=lambda s: y_copy(s).start(),
                       dot_other=lambda: _dot_tn(y_blk[...], do_other[...]), dot_mine=lambda: _dot_tn(y_blk[...], do_mine[...]))

        def piece_matmuls(g, s):
            if s >= 2:
                g["p1"][s - 2].wait_send()
            g["wait_load"](s)
            g["other"][s % 2] = g["dot_other"]().astype(BF16)
            g["p1"][s] = copy(g["p1_sems"][s], g["other"].at[s % 2], g["recv"].at[s], sib)
            g["p1"][s].start()
            g["mine"][s % 2] = g["dot_mine"]()
            if s + 1 < NCHIP:
                g["load"](s + 1)

        def piece_finish(g, s):
            mine, recv, send, rb, p2 = g["mine"].at[s % 2], g["recv"].at[s], g["send"], g["rb"], g["p2_sem"]
            nrows = mine.shape[0]
            copy(g["p1_sems"][s], recv, recv, sib).wait_recv()
            if s == 3:
                pair_sum(mine, recv, None, nrows, True)
                return
            if s == 0:
                pair_sum(mine, recv, send.at[0], nrows, False)
                start(copy(p2, send.at[0], rb.at[0], near))
                return
            adds_relayed = c == (1 if s == 1 else 0)

            @pl.when(adds_relayed)
            def _():
                copy(p2, rb.at[0], rb.at[0], sib).wait_recv()
                pair_sum(mine, recv, send.at[s], nrows, False, rb.at[0])

            @pl.when(jnp.logical_not(adds_relayed))
            def _():
                pair_sum(mine, recv, send.at[s], nrows, False)
            start(copy(p2 + s, send.at[s], rb.at[s], (*chips[s - 1], c)))

        def piece_total(g):
            for k in (1, 2):
                copy(g["p2_sem"] + k, g["rb"].at[k], g["rb"].at[k], sib).wait_recv()
            chip_sum(g["mine"].at[1], g["rb"], g["mine"].shape[1], (1, 2))

        for cp in do_loads:
            cp.wait()
        piece_matmuls(w_out_g, 0)
        piece_matmuls(w_out_g, 1)
        piece_finish(w_out_g, 0)
        piece_matmuls(w_out_g, 2)
        piece_finish(w_out_g, 1)
        piece_matmuls(w_out_g, 3)
        piece_finish(w_out_g, 2)

        for cp in hn_loads:
            cp.wait()
        piece_matmuls(w_in_g, 0)

        l_pk.wait()
        copy(0, pk_recv, pk_recv, sib).wait_recv()
        pair_sum(pk_mine, pk_recv, pk_send, hp, True)
        for k in range(3):
            start(copy(6 + k, pk_send.at[:, pl.ds(_mo(cj[k] * 128, 128), 128)], pk_rb.at[k], (*chips[k], c)))
        l_pk_own = local(6, pk_mine.at[:, pl.ds(_mo(j * 128, 128), 128)], pk_own)

        piece_matmuls(w_in_g, 1)
        piece_finish(w_in_g, 0)

        l_pk_own.wait()
        for k in range(3):
            copy(6 + k, pk_rb.at[k], pk_rb.at[k], sib).wait_recv()
        chip_sum(pk_own, pk_rb, hp)
        l_pk_out = local(8, pk_own, pk_piece(pk_out, c, j))
        start(copy(15, pk_own, pk_piece(pk_out, c, j), sib))
        for k in range(2):
            start(copy(16 + k, pk_own, pk_piece(pk_out, c, j), (*chips[k], c)))

        piece_matmuls(w_in_g, 2)
        piece_finish(w_in_g, 1)
        piece_matmuls(w_in_g, 3)
        piece_finish(w_in_g, 2)

        piece_finish(w_out_g, 3)
        piece_total(w_out_g)
        l_wo_out = local(9, wo_mine.at[1], cols_half(wo_out, c, 512))
        start(copy(22, wo_mine.at[1], cols_half(wo_out, c, 512), sib))

        far = (jnp.where(c == 0, x, 1 - x), jnp.where(c == 0, 1 - y, y), c)
        for step, k in enumerate([c, 1 - c, 2]):
            reg = pk_piece(pk_out, c, jnp.where(k == 0, cj[0], jnp.where(k == 1, cj[1], cj[2])))
            copy(16 + k, reg, reg, sib).wait_recv()
            if step == 0:
                start(copy(18, reg, reg, far))
            start(copy(19 + k, reg, reg, sib))

        piece_finish(w_in_g, 3)
        piece_total(w_in_g)
        l_wi_out = local(10, wi_mine.at[1], rows_half(wi_out, c, 512))
        start(copy(23, wi_mine.at[1], rows_half(wi_out, c, 512), sib))

        reg = pk_piece(pk_out, 1 - c, j)
        copy(15, reg, reg, sib).wait_recv()
        for k in range(3):
            reg = pk_piece(pk_out, 1 - c, cj[k])
            copy(19 + k, reg, reg, sib).wait_recv()
        reg = cols_half(wo_out, 1 - c, 512)
        copy(22, reg, reg, sib).wait_recv()
        reg = rows_half(wi_out, 1 - c, 512)
        copy(23, reg, reg, sib).wait_recv()
        for cp in sends + w_in_g["p1"][2:] + w_out_g["p1"][2:]:
            cp.wait_send()
        for cp in (l_pk_out, l_wo_out, l_wi_out):
            cp.wait()

    return pl.pallas_call(
        body, name="grads_reduce",
        out_shape=(jax.ShapeDtypeStruct((D, wi_w), F32), jax.ShapeDtypeStruct((512, D), F32),
                   jax.ShapeDtypeStruct(pack.shape, F32)),
        in_specs=[ANY] * (4 + n_dz), out_specs=(ANY,) * 3,
        scratch_shapes=[
            pltpu.VMEM((lt, 512), BF16), pltpu.VMEM((lt, 512), BF16), pltpu.VMEM((lt, wi_w), BF16),
            pltpu.VMEM((2, 512, wi_w), BF16), pltpu.VMEM((2, 512, wi_w), F32), pltpu.VMEM((4, 512, wi_w), BF16),
            pltpu.VMEM((3, 512, wi_w), BF16), pltpu.VMEM((3, 512, wi_w), BF16),
            pltpu.VMEM((lx, 512), BF16), pltpu.VMEM((lx, 512), BF16), pltpu.VMEM((lx, 512), BF16),
            pltpu.VMEM((2, 512, 512), BF16), pltpu.VMEM((2, 512, 512), F32), pltpu.VMEM((4, 512, 512), BF16),
            pltpu.VMEM((3, 512, 512), BF16), pltpu.VMEM((3, 512, 512), BF16),
            pltpu.VMEM((hp, 512), F32), pltpu.VMEM((hp, 512), F32), pltpu.VMEM((hp, 512), BF16),
            pltpu.VMEM((3, hp, 128), BF16), pltpu.VMEM((hp, 128), F32),
            pltpu.SemaphoreType.DMA((27,)), pltpu.SemaphoreType.DMA((27,)), pltpu.SemaphoreType.DMA((15,))],
        compiler_params=_cp(vmem_mb=56),
    )(hn, *dzs, y, do, pack)


def _ada_bwd(c_all, dmx_all_j, dmc_j, dmx_all, dmc, c_ctx, ada_w_full):
    def body(c_ref, dmxj_ref, dmcj_ref, dmx_ref, dmc_ref, cc_ref, w_hbm, gw_ref, gb_ref, gc_ref, lhs, rhs, dm8,
             w_v, w_sem):
        w_loads = [pltpu.make_async_copy(w_hbm.at[:, pl.ds(D * k, D)], w_v.at[k], w_sem.at[k]) for k in range(3)]
        for cp in w_loads:
            cp.start()
        lhs[...] = jnp.zeros_like(lhs)
        rhs[...] = jnp.zeros_like(rhs)
        cv = c_ref[...]
        lhs[0:8, :] = cv * _sigmoid(cv)
        cc = cc_ref[...]
        a_c, da_c = _silu_and_grad(cc)
        lhs[8:9, :] = a_c
        rhs[0:8, :] = dmxj_ref[...]
        rhs[8:9, :] = dmcj_ref[...]
        gw_ref[...] = _dot_tn(lhs[...].astype(BF16), rhs[...].astype(BF16))
        gb_ref[...] = jnp.sum(dmx_ref[...], axis=0, keepdims=True) + dmc_ref[...]
        dm8[...] = jnp.zeros_like(dm8)
        dm8[0:1, :] = dmc_ref[...]
        da = jnp.zeros((8, D), F32)
        for k in range(3):
            w_loads[k].wait()
            da = da + _dot_nt(dm8[:, D * k:D * k + D].astype(BF16), w_v[k])
        gc_ref[...] = da[0:1, :] * da_c

    return pl.pallas_call(
        body, name="ada_bwd",
        out_shape=(jax.ShapeDtypeStruct((D, 768), F32), jax.ShapeDtypeStruct((1, 3 * D), F32),
                   jax.ShapeDtypeStruct((1, D), F32)),
        in_specs=[VMEM] * 6 + [ANY], out_specs=(VMEM,) * 3,
        scratch_shapes=[pltpu.VMEM((16, D), F32), pltpu.VMEM((16, 768), F32), pltpu.VMEM((8, 3 * D), F32),
                        pltpu.VMEM((3, D, D), BF16), pltpu.SemaphoreType.DMA((3,))],
        compiler_params=_cp(vmem_mb=32),
    )(c_all, dmx_all_j, dmc_j, dmx_all, dmc, c_ctx, _in_hbm(ada_w_full))


def _proj(x, ctx, mods, norm_g, w_full, sgu):
    lx, lc = x.shape[0], ctx.shape[0]
    assert lc == T
    n = 1 + lx // T

    def body(x_ref, c_ref, sh_ref, sc_ref, ng_ref, w0, w1, w2, w3, w4, g_ref, b_ref, sw_ref, bt_ref,
             hn_ref, xa_ref, ga_ref, u_ref, v_ref, gb_ref, ys_ref, mixed_s):
        i = pl.program_id(0)
        is_ctx = i == 0
        xv = jnp.where(is_ctx, c_ref[...], x_ref[...])
        sc = jnp.where(is_ctx, sc_ref[1:2, :], sc_ref[0:1, :])
        sh = jnp.where(is_ctx, sh_ref[1:2, :], sh_ref[0:1, :])
        r = lax.rsqrt(jnp.mean(xv * xv, axis=-1, keepdims=True) + NORM_EPS)
        hb = ((xv * r) * ng_ref[...] * (1.0 + sc) + sh).astype(BF16)
        hn_ref[...] = hb
        xa_ref[...] = _dot(hb, w0[...])

        @pl.when(i > 0)
        def _():
            u_ref[...] = _dot(hb, w2[...])
            v_ref[...] = _dot(hb, w3[...])
            for ch in range(T // HD):
                rows = slice(HD * ch, HD * ch + HD)
                ug = _sgu_parts(u_ref[rows, :], v_ref[rows, :], g_ref[...], b_ref[...], sw_ref, bt_ref, mixed_s)[0]
                ys_ref[rows, :] = ug * mixed_s[...]
            ga_ref[...] = _dot(hb, w1[...])
            gb_ref[...] = _dot(hb, w4[...])

    every = pl.BlockSpec((T, D), lambda i: (i, 0))
    lat = pl.BlockSpec((T, D), lambda i: (jnp.maximum(i - 1, 0), 0))
    vec = pl.BlockSpec((1, D), lambda i: (0, 0))
    in_specs = [lat, pl.BlockSpec((T, D), lambda i: (0, 0)), pl.BlockSpec((8, D), lambda i: (0, 0)),
                pl.BlockSpec((8, D), lambda i: (0, 1)), vec]
    in_specs += [pl.BlockSpec((D, D), lambda i, k=k: (0, k)) for k in range(5)]
    in_specs += [vec, vec, pl.BlockSpec((NH, HD, HD), lambda i: (0, 0, 0)), pl.BlockSpec((HD, NH), lambda i: (0, 0))]
    full_s = jax.ShapeDtypeStruct((lc + lx, D), F32)
    lat_s = jax.ShapeDtypeStruct((lx, D), F32)
    return pl.pallas_call(
        body, name="proj", grid=(n,),
        out_shape=(jax.ShapeDtypeStruct((lc + lx, D), BF16), full_s, lat_s, lat_s, lat_s, lat_s, lat_s),
        in_specs=in_specs, out_specs=(every, every, lat, lat, lat, lat, lat),
        scratch_shapes=[pltpu.VMEM((HD, D), F32)], compiler_params=_cp(1, vmem_mb=56),
    )(x, ctx, mods, mods, norm_g, *([w_full] * 5), *sgu)


def _tile_specs(rows_per_pos, width, n_tiles, tile):
    last = n_tiles * (T // 8) - 1
    r = rows_per_pos
    return [pl.BlockSpec((T * r, width), lambda i: (tile(i), 0)),
            pl.BlockSpec((8 * r, width), lambda i: (jnp.maximum(tile(i) * (T // 8) - 1, 0), 0)),
            pl.BlockSpec((8 * r, width), lambda i: (jnp.minimum((tile(i) + 1) * (T // 8), last), 0))]


def _has_prev(tile):
    return tile >= 2


def _has_next(tile, nt):
    return jnp.logical_and(tile >= 1, tile < nt - 1)


ZT = pl.BlockSpec((T * NH, HD), lambda i: (i, 0))
CONV_CHUNK = 32


SCAN_SUB = 8


def _scan_tile(chains, post, carry_ref):
    blk = T // SCAN_SUB

    def step(k, state):
        new = []
        for ci, (a_ref, x_ref, o_ref, q_ref, reverse, xscale) in enumerate(chains):
            for q in range(SCAN_SUB):
                s, p = state[ci * SCAN_SUB + q]
                t = (q + 1) * blk - 1 - k if reverse else q * blk + k
                r = pl.ds(_mo(t * NH, NH), NH)
                a = a_ref[r, :]
                x = x_ref[r, :] if xscale is None else x_ref[r, :] * xscale
                if post:
                    o = x + s
                    o_ref[r, :] = o
                    q_ref[r, :] = p
                    new.append((a * o, a * p))
                else:
                    o = a * s + x
                    p = a * p
                    o_ref[r, :] = o
                    q_ref[r, :] = p
                    new.append((o, p))
        return tuple(new)

    zero = jnp.zeros((NH, HD), F32)
    one = jnp.ones((NH, HD), F32)
    final = lax.fori_loop(0, blk, step, tuple((zero, one) for _ in range(len(chains) * SCAN_SUB)))
    for ci, (a_ref, x_ref, o_ref, q_ref, reverse, xscale) in enumerate(chains):
        carry = carry_ref[ci]
        for q in (range(SCAN_SUB - 1, -1, -1) if reverse else range(SCAN_SUB)):
            rows = pl.ds(q * blk * NH, blk * NH)
            fixed = o_ref[rows, :].reshape(blk, NH, HD) + q_ref[rows, :].reshape(blk, NH, HD) * carry[None]
            o_ref[rows, :] = fixed.reshape(blk * NH, HD)
            s_loc, p_loc = final[ci * SCAN_SUB + q]
            carry = s_loc + p_loc * carry
        carry_ref[ci] = carry


def _lru_fwd(xa, conv_wz, conv_bz, wcat, bcat, lamcat, name):
    lx = xa.shape[0]
    n = lx // T
    tile_u = lambda i: i
    tile_d = lambda i: jnp.where(i == 0, 0, n - i)

    def body(xm_u, xp_u, xn_u, xm_d, xp_d, xn_d, cw, cb, w_ref, b_ref, lam_ref,
             xcz_o, af_o, ab_o, hf_o, hb_o, gf_o, gb_o, fu, fd, pad, xc_d, x_u, x_d, q_u, q_d, carry):
        i = pl.program_id(0)

        @pl.when(i == 0)
        def _():
            carry[...] = jnp.zeros_like(carry)

        def conv_gates(xm, xp, xn, tile, d, xc_ref, a_ref, x_ref, g_ref):
            pmask = jnp.where(_has_prev(tile), 1.0, 0.0)
            nmask = jnp.where(_has_next(tile, n), 1.0, 0.0)
            for h in range(NH):
                cols = slice(HD * h, HD * h + HD)
                pad[_zrows(h, 8), :] = xp[:, cols] * pmask
                pad[pl.ds(8 * NH + h, T, stride=NH), :] = xm[:, cols]
                pad[pl.ds((T + 8) * NH + h, 8, stride=NH), :] = xn[:, cols] * nmask

            def conv_chunk(ci, c_):
                base = pl.multiple_of(ci * (CONV_CHUNK * NH), CONV_CHUNK * NH)
                acc = None
                for k in range(4):
                    sl = pad[pl.ds(base + (7 + k) * NH, CONV_CHUNK * NH), :].reshape(CONV_CHUNK, NH, HD)
                    term = sl * cw[k][None]
                    acc = term if acc is None else acc + term
                acc = acc + cb[...][None]
                xc_ref[pl.ds(base, CONV_CHUNK * NH), :] = acc.reshape(CONV_CHUNK * NH, HD)
                return c_
            lax.fori_loop(0, T // CONV_CHUNK, conv_chunk, 0)

            for h in range(NH):
                xch = xc_ref[_zrows(h, T), :]
                pre = _dot(xch.astype(BF16), w_ref[h, :, 256 * d:256 * d + 256]) + b_ref[h:h + 1, 256 * d:256 * d + 256]
                r, gi, _, _, a, mult = _lru_gate(pre, lam_ref[h:h + 1, :], d, 0)
                a_ref[_zrows(h, T), :] = a
                x_ref[_zrows(h, T), :] = mult * gi * xch
                for q, val in enumerate((r, gi, mult)):
                    g_ref[:, q * D + HD * h:q * D + HD * h + HD] = val

        conv_gates(xm_u, xp_u, xn_u, tile_u(i), 0, xcz_o, af_o, x_u, gf_o)
        conv_gates(xm_d, xp_d, xn_d, tile_d(i), 1, xc_d, ab_o, x_d, gb_o)

        _scan_tile([(af_o, x_u, hf_o, q_u, False, None), (ab_o, x_d, hb_o, q_d, True, None)], False, carry)

        @pl.when(i == 0)
        def _():
            fu[...] = carry[0]
            fd[...] = carry[1]

    full = lambda shape: pl.BlockSpec(shape, lambda i: (0,) * len(shape))
    st = full((NH, HD))
    in_specs = _tile_specs(1, D, n, tile_u) + _tile_specs(1, D, n, tile_d)
    in_specs += [full((4, NH, HD)), st, full((NH, HD, 4 * HD)), full((NH, 4 * HD)), full((NH, 2 * HD))]
    up = pl.BlockSpec((T * NH, HD), lambda i: (tile_u(i), 0))
    dn = pl.BlockSpec((T * NH, HD), lambda i: (tile_d(i), 0))
    zs = jax.ShapeDtypeStruct((lx * NH, HD), F32)
    ss = jax.ShapeDtypeStruct((NH, HD), F32)
    zbuf = pltpu.VMEM((T * NH, HD), F32)
    gs = jax.ShapeDtypeStruct((lx, 3 * D), F32)
    g_up = pl.BlockSpec((T, 3 * D), lambda i: (tile_u(i), 0))
    g_dn = pl.BlockSpec((T, 3 * D), lambda i: (tile_d(i), 0))
    return pl.pallas_call(
        body, name=name, grid=(n,), out_shape=(zs,) * 5 + (gs, gs, ss, ss), in_specs=in_specs,
        out_specs=(up, up, dn, up, dn, g_up, g_dn, st, st),
        scratch_shapes=[pltpu.VMEM(((T + 16) * NH, HD), F32), zbuf, zbuf, zbuf, zbuf, zbuf,
                        pltpu.VMEM((2, NH, HD), F32)],
        compiler_params=_cp(1, vmem_mb=48),
    )(xa, xa, xa, xa, xa, xa, conv_wz, conv_bz, wcat, bcat, lamcat)


def _sgu_parts(u, v, lng, lnb, w_ref, bt_ref, mixed_s):
    ug, dug = _gelu_and_grad(u)
    vg, dvg = _gelu_and_grad(v)
    mu = jnp.mean(vg, axis=-1, keepdims=True)
    vc = vg - mu
    rstd = lax.rsqrt(jnp.mean(vc * vc, axis=-1, keepdims=True) + LN_EPS)
    vh = vc * rstd
    vn = (vh * lng + lnb).astype(BF16)
    for g in range(NH):
        cols = slice(HD * g, HD * g + HD)
        mixed_s[:, cols] = _dot(w_ref[g], vn[:, cols]) + bt_ref[:, g:g + 1]
    return ug, dug, dvg, rstd, vh, vn


def _sgu_bwd_chunk(u, v, dys_v, lng, lnb, w_ref, bt_ref, mixed_s, dvn_s, dw_ref, db_ref, dg_ref, dbl_ref):
    ug, dug, dvg, rstd, vh, vn = _sgu_parts(u, v, lng, lnb, w_ref, bt_ref, mixed_s)
    du = (dys_v * mixed_s[...] * dug).astype(BF16)
    dmix = dys_v * ug
    ones = jnp.ones((8, HD), BF16)
    for g in range(NH):
        cols = slice(HD * g, HD * g + HD)
        dm = dmix[:, cols]
        hi = dm.astype(BF16)
        lo = (dm - hi.astype(F32)).astype(BF16)
        dw_ref[g] += _dot_nt(hi, vn[:, cols])
        db_ref[g:g + 1, :] += (_dot_nt(ones, hi) + _dot_nt(ones, lo))[0:1, :]
        dvn_s[:, cols] = _dot_tn(w_ref[g], hi)
    dvn = dvn_s[...]
    dg_ref[...] += jnp.sum(dvn * vh, axis=0, keepdims=True)
    dbl_ref[...] += jnp.sum(dvn, axis=0, keepdims=True)
    dvh = dvn * lng
    dvg_in = rstd * (dvh - jnp.mean(dvh, axis=-1, keepdims=True) - vh * jnp.mean(dvh * vh, axis=-1, keepdims=True))
    return du, (dvg_in * dvg).astype(BF16)


def _out_fwd_bwd(hf_z, hb_z, ga, gb, ys, x, tgt, mods, final_g, w_out_full):
    lx = x.shape[0]
    n = lx // T

    def body(hf_ref, hb_ref, ga_ref, gb_ref, ys_ref, x_ref, t_ref, gx_ref, fg_ref, w_ref,
             loss_ref, dfg_ref, dgx_ref, dxn_ref, y_ref, do_ref, dga_ref, dgb_ref, dyl_ref, dys_ref, yl_s):
        i = pl.program_id(0)

        @pl.when(i == 0)
        def _():
            loss_ref[...] = jnp.zeros_like(loss_ref)
            dfg_ref[...] = jnp.zeros_like(dfg_ref)
            dgx_ref[...] = jnp.zeros_like(dgx_ref)

        for h in range(NH):
            yl_s[:, HD * h:HD * h + HD] = hf_ref[_zrows(h, T), :] + hb_ref[_zrows(h, T), :]
        yl = yl_s[...]
        gav = ga_ref[...]
        gbv = gb_ref[...]
        sa, dsa = _silu_and_grad(gav)
        sb, dsb = _silu_and_grad(gbv)
        ysv = ys_ref[...]
        y_ref[:, 0:D] = (yl * sa).astype(BF16)
        y_ref[:, D:2 * D] = (ysv * sb).astype(BF16)
        o = _dot(y_ref[...], w_ref[...])
        gx = gx_ref[0:1, :]
        xnew = x_ref[...] + gx * o
        r2 = lax.rsqrt(jnp.mean(xnew * xnew, axis=-1, keepdims=True) + NORM_EPS)
        xh = xnew * r2
        fg = fg_ref[...]
        err = xh * fg - t_ref[...]
        loss_ref[...] += 0.5 * jnp.sum(jnp.mean(err * err, axis=-1, keepdims=True), axis=0, keepdims=True)

        @pl.when(i == n - 1)
        def _():
            lp = loss_ref[...]
            lp1 = lp.astype(BF16).astype(F32)
            lp2 = (lp - lp1).astype(BF16).astype(F32)
            lp3 = (lp - lp1 - lp2).astype(BF16).astype(F32)
            lane = lax.broadcasted_iota(jnp.int32, lp.shape, 1)
            loss_ref[...] = jnp.where(lane == 0, lp1, jnp.where(lane == 1, lp2, jnp.where(lane == 2, lp3, 0.0)))
        dout = err * (1.0 / D)
        dfg_ref[...] += jnp.sum(dout * xh, axis=0, keepdims=True)
        dxh = dout * fg
        dxn = r2 * (dxh - xh * jnp.mean(dxh * xh, axis=-1, keepdims=True))
        dxn_ref[...] = dxn
        dgx_ref[...] += jnp.sum(dxn * o, axis=0, keepdims=True)
        do = (dxn * gx).astype(BF16)
        do_ref[...] = do
        dy = _dot_nt(do, w_ref[...])
        dy1 = dy[:, 0:D]
        dy2 = dy[:, D:2 * D]
        dga_ref[...] = (dy1 * yl * dsa).astype(BF16)
        dgb_ref[...] = (dy2 * ysv * dsb).astype(BF16)
        dys_ref[...] = dy2 * sb
        yl_s[...] = dy1 * sa
        for h in range(NH):
            dyl_ref[_zrows(h, T), :] = yl_s[:, HD * h:HD * h + HD]

    row = pl.BlockSpec((T, D), lambda i: (i, 0))
    vec = pl.BlockSpec((1, D), lambda i: (0, 0))
    zlat = pl.BlockSpec((T * NH, HD), lambda i: (i + 1, 0))
    in_specs = [zlat, zlat, row, row, row, row, row, pl.BlockSpec((8, D), lambda i: (0, 2)), vec,
                pl.BlockSpec((2 * D, D), lambda i: (0, 0))]
    out_shape = (jax.ShapeDtypeStruct((1, D), F32), jax.ShapeDtypeStruct((1, D), F32), jax.ShapeDtypeStruct((1, D), F32),
                 jax.ShapeDtypeStruct((lx, D), F32), jax.ShapeDtypeStruct((lx, 2 * D), BF16),
                 jax.ShapeDtypeStruct((lx, D), BF16), jax.ShapeDtypeStruct((lx, D), BF16),
                 jax.ShapeDtypeStruct((lx, D), BF16), jax.ShapeDtypeStruct((lx * NH, HD), F32),
                 jax.ShapeDtypeStruct((lx, D), F32))
    out_specs = (vec, vec, vec, row, pl.BlockSpec((T, 2 * D), lambda i: (i, 0)),
                 row, row, row, ZT, row)
    return pl.pallas_call(
        body, name="out_fwd_bwd", grid=(n,), out_shape=out_shape, in_specs=in_specs, out_specs=out_specs,
        scratch_shapes=[pltpu.VMEM((T, D), F32)],
        compiler_params=_cp(1, vmem_mb=56),
    )(hf_z, hb_z, ga, gb, ys, x, tgt, mods, final_g, w_out_full)


def _lru_bwd(xc_z, dy_z, hf_z, hb_z, af_z, ab_z, gf, gb, s_b, wcat, lamcat, name):
    lx = xc_z.shape[0] // NH
    n = lx // T
    tile_u = lambda i: jnp.where(i == n - 1, 0, i + 1)
    tile_d = lambda i: n - 1 - i

    def body(xc_u, dy_u, hb_ref, hbn_ref, ab_ref, gb_ref, xc_d, dy_d, hf_ref, hfp_ref, af_ref, gf_ref,
             sb_ref, w_ref, lam_ref, dxcb_ref, dxcf_ref, dw_ref, db_ref, dl_ref,
             lb_s, lf_s, q_u, q_d, pf_s, pb_s, dpre_s, carry):
        i = pl.program_id(0)
        tu, td = tile_u(i), tile_d(i)

        @pl.when(i == 0)
        def _():
            dw_ref[...] = jnp.zeros_like(dw_ref)
            db_ref[...] = jnp.zeros_like(db_ref)
            dl_ref[...] = jnp.zeros_like(dl_ref)
            carry[...] = jnp.zeros_like(carry)

        _scan_tile([(ab_ref, dy_u, lb_s, q_u, False, jnp.where(tu == 0, 0.0, 1.0)),
                    (af_ref, dy_d, lf_s, q_d, True, jnp.where(td == 0, 0.0, 1.0))], True, carry)
        zero = jnp.zeros((NH, HD), F32)
        pb_s[pl.ds(0, T * NH), :] = hb_ref[...]
        pb_s[pl.ds(T * NH, NH), :] = jnp.where(tu == n - 1, sb_ref[...], jnp.where(tu == 0, zero, hbn_ref[pl.ds(0, NH), :]))
        pf_s[pl.ds(0, NH), :] = jnp.where(td == 0, zero, hfp_ref[pl.ds(7 * NH, NH), :])
        pf_s[pl.ds(NH, T * NH), :] = hf_ref[...]
        sides = ((1, xc_u, lb_s, pb_s, NH, ab_ref, gb_ref, dxcb_ref), (0, xc_d, lf_s, pf_s, 0, af_ref, gf_ref, dxcf_ref))
        for d, xc_ref, adj_s, prev_s, prev_off, a_ref, g_ref, dxc_ref in sides:
            wcols = slice(256 * d, 256 * d + 256)
            for h in range(NH):
                xch = xc_ref[_zrows(h, T), :]
                xcb = xch.astype(BF16)
                r, gi, mult = (g_ref[:, q * D + HD * h:q * D + HD * h + HD] for q in range(3))
                a = a_ref[_zrows(h, T), :]
                lam = lam_ref[h:h + 1, HD * d:HD * d + HD]
                sp = _softplus(-lam)
                du = adj_s[_zrows(h, T), :]
                da = du * prev_s[pl.ds(prev_off + h, T, stride=NH), :]
                dgi = du * mult * xch
                dmult = du * gi * xch
                dla = da * a - dmult * (a * a) / mult
                dr = dla * ((-LRU_C) * sp)
                dsp = jnp.sum(dla * ((-LRU_C) * r), axis=0, keepdims=True)
                dl_ref[h:h + 1, HD * d:HD * d + HD] += dsp * (-_sigmoid(-lam))
                dpre_s[:, 0:HD] = dr * r * (1.0 - r)
                dpre_s[:, HD:2 * HD] = dgi * gi * (1.0 - gi)
                dpre = dpre_s[...]
                dpb = dpre.astype(BF16)
                dw_ref[h, :, wcols] += _dot_tn(xcb, dpb)
                db_ref[h:h + 1, wcols] += jnp.sum(dpre, axis=0, keepdims=True)
                dxc_ref[_zrows(h, T), :] = du * mult * gi + _dot_nt(dpb, w_ref[h, :, wcols])

    full = lambda shape: pl.BlockSpec(shape, lambda i: (0,) * len(shape))
    wsp, bsp, lsp = full((NH, HD, 4 * HD)), full((NH, 4 * HD)), full((NH, 2 * HD))
    st = full((NH, HD))
    up = pl.BlockSpec((T * NH, HD), lambda i: (tile_u(i), 0))
    dn = pl.BlockSpec((T * NH, HD), lambda i: (tile_d(i), 0))
    dy_up = pl.BlockSpec((T * NH, HD), lambda i: (jnp.maximum(tile_u(i) - 1, 0), 0))
    dy_dn = pl.BlockSpec((T * NH, HD), lambda i: (jnp.maximum(tile_d(i) - 1, 0), 0))
    nxt = _tile_specs(NH, HD, n, tile_u)[2]
    prv = _tile_specs(NH, HD, n, tile_d)[1]
    g_up = pl.BlockSpec((T, 3 * D), lambda i: (tile_u(i), 0))
    g_dn = pl.BlockSpec((T, 3 * D), lambda i: (tile_d(i), 0))
    zs = jax.ShapeDtypeStruct((lx * NH, HD), F32)
    zbuf = pltpu.VMEM((T * NH, HD), F32)
    zbuf1 = pltpu.VMEM(((T + 1) * NH, HD), F32)
    return pl.pallas_call(
        body, name=name, grid=(n,),
        out_shape=(zs, zs, jax.ShapeDtypeStruct((NH, HD, 4 * HD), F32), jax.ShapeDtypeStruct((NH, 4 * HD), F32),
                   jax.ShapeDtypeStruct((NH, 2 * HD), F32)),
        in_specs=[up, dy_up, up, nxt, up, g_up, dn, dy_dn, dn, prv, dn, g_dn, st, wsp, lsp],
        out_specs=(up, dn, wsp, bsp, lsp),
        scratch_shapes=[zbuf, zbuf, zbuf, zbuf, zbuf1, zbuf1, pltpu.VMEM((T, 2 * HD), F32),
                        pltpu.VMEM((2, NH, HD), F32)],
        compiler_params=_cp(1, vmem_mb=56),
    )(xc_z, dy_z, hb_z, hb_z, ab_z, gb, xc_z, dy_z, hf_z, hf_z, af_z, gf, s_b, wcat, lamcat)


def _conv_bwd(dxc_a, dxc_b, xa, conv_wz, dcw0, dcb0, name):
    lx = dxc_a.shape[0] // NH
    n = lx // T

    def body(dm_a, dp_a, dn_a, dm_b, dp_b, dn_b, xan_ref, cw, dcw0_ref, dcb0_ref, dxa_ref, dcw_ref, dcb_ref,
             pad, dxa_s, xa_ref):
        i = pl.program_id(0)

        @pl.when(i == 0)
        def _():
            dcw_ref[...] = dcw0_ref[...]
            dcb_ref[...] = dcb0_ref[...]

        for h in range(NH):
            xa_ref[_zrows(h, T), :] = xan_ref[:, HD * h:HD * h + HD]

        pmask = jnp.where(_has_prev(i), 1.0, 0.0)
        nmask = jnp.where(_has_next(i, n), 1.0, 0.0)
        pad[pl.ds(0, 8 * NH), :] = (dp_a[...] + dp_b[...]) * pmask
        pad[pl.ds(8 * NH, T * NH), :] = dm_a[...] + dm_b[...]
        pad[pl.ds((T + 8) * NH, 8 * NH), :] = (dn_a[...] + dn_b[...]) * nmask

        def chunk(ci, carry):
            base = pl.multiple_of(ci * (CONV_CHUNK * NH), CONV_CHUNK * NH)
            xav = xa_ref[pl.ds(base, CONV_CHUNK * NH), :].reshape(CONV_CHUNK, NH, HD)
            acc = None
            for k in range(4):
                sl = pad[pl.ds(base + (9 - k) * NH, CONV_CHUNK * NH), :].reshape(CONV_CHUNK, NH, HD)
                term = sl * cw[k][None]
                acc = term if acc is None else acc + term
                dcw_ref[k] += jnp.sum(sl * xav, axis=0)
                if k == 1:
                    dcb_ref[...] += jnp.sum(sl, axis=0)
            dxa_s[pl.ds(base, CONV_CHUNK * NH), :] = acc.reshape(CONV_CHUNK * NH, HD)
            return carry
        lax.fori_loop(0, T // CONV_CHUNK, chunk, 0)
        for h in range(NH):
            dxa_ref[:, HD * h:HD * h + HD] = dxa_s[_zrows(h, T), :].astype(BF16)

    full = lambda shape: pl.BlockSpec(shape, lambda i: (0,) * len(shape))
    return pl.pallas_call(
        body, name=name, grid=(n,),
        out_shape=(jax.ShapeDtypeStruct((lx, D), BF16), jax.ShapeDtypeStruct((4, NH, HD), F32),
                   jax.ShapeDtypeStruct((NH, HD), F32)),
        in_specs=_tile_specs(NH, HD, n, lambda i: i) * 2 + [pl.BlockSpec((T, D), lambda i: (i, 0)), full((4, NH, HD)),
                                                            full((4, NH, HD)), full((NH, HD))],
        out_specs=(pl.BlockSpec((T, D), lambda i: (i, 0)), full((4, NH, HD)), full((NH, HD))),
        scratch_shapes=[pltpu.VMEM(((T + 16) * NH, HD), F32), pltpu.VMEM((T * NH, HD), F32),
                        pltpu.VMEM((T * NH, HD), F32)],
        compiler_params=_cp(1, vmem_mb=48),
    )(dxc_a, dxc_a, dxc_a, dxc_b, dxc_b, dxc_b, xa, conv_wz, dcw0, dcb0)


def _proj_bwd(dxa, dga, dgb, x, ctx, dxn, mods, norm_g, w_full, sgu):
    lx, lc = x.shape[0], ctx.shape[0]
    assert lc == T
    n = 1 + lx // T

    def body(dxa_ref, dga_ref, dgb_ref, w0, w1, w4, w2, w3, x_ref, c_ref, sc_ref, ng_ref, dxn_ref,
             u_ref, v_ref, dy_ref, g_ref, b_ref, sw_ref, bt_ref,
             gx_ref, dng_ref, dscx_ref, dshx_ref, dscc_ref, dshc_ref, du_ref, dv_ref, dws_ref, dbs_ref, dlg_ref, dlb_ref,
             mixed_s, dvn_s):
        i = pl.program_id(0)
        is_ctx = i == 0

        @pl.when(is_ctx)
        def _():
            for acc in (dng_ref, dscx_ref, dshx_ref, dscc_ref, dshc_ref, dws_ref, dbs_ref, dlg_ref, dlb_ref):
                acc[...] = jnp.zeros_like(acc)

        xv = jnp.where(is_ctx, c_ref[...], x_ref[...])
        sc1 = 1.0 + jnp.where(is_ctx, sc_ref[1:2, :], sc_ref[0:1, :])
        r = lax.rsqrt(jnp.mean(xv * xv, axis=-1, keepdims=True) + NORM_EPS)
        xn = xv * r
        ng = ng_ref[...]

        def norm_bwd(dhn, dsc_ref, dsh_ref, with_x):
            t = dhn * xn
            dng_ref[...] += jnp.sum(t * sc1, axis=0, keepdims=True)
            dsc_ref[...] += jnp.sum(t * ng, axis=0, keepdims=True)
            dsh_ref[...] += jnp.sum(dhn, axis=0, keepdims=True)
            if with_x:
                dxh = dhn * (ng * sc1)
                gx_ref[...] = dxn_ref[...] + r * (dxh - xn * jnp.mean(dxh * xn, axis=-1, keepdims=True))

        @pl.when(is_ctx)
        def _():
            norm_bwd(_dot_nt(dxa_ref[...], w0[...]), dscc_ref, dshc_ref, False)

        @pl.when(i > 0)
        def _():
            def sgu_chunk(ch):
                rows = slice(HD * ch, HD * ch + HD)
                du, dv = _sgu_bwd_chunk(u_ref[rows, :], v_ref[rows, :], dy_ref[rows, :], g_ref[...], b_ref[...],
                                        sw_ref, bt_ref, mixed_s, dvn_s, dws_ref, dbs_ref, dlg_ref, dlb_ref)
                du_ref[rows, :] = du
                dv_ref[rows, :] = dv

            dhn = _dot_nt(dxa_ref[...], w0[...])
            sgu_chunk(0)
            dhn = dhn + _dot_nt(dga_ref[...], w1[...])
            for ch in range(1, T // HD):
                sgu_chunk(ch)
            dhn = dhn + _dot_nt(dgb_ref[...], w4[...])
            dhn = dhn + _dot_nt(du_ref[...], w2[...]) + _dot_nt(dv_ref[...], w3[...])
            norm_bwd(dhn, dscx_ref, dshx_ref, True)

    every = pl.BlockSpec((T, D), lambda i: (i, 0))
    lat = pl.BlockSpec((T, D), lambda i: (jnp.maximum(i - 1, 0), 0))
    vec = pl.BlockSpec((1, D), lambda i: (0, 0))
    wsp = pl.BlockSpec((NH, HD, HD), lambda i: (0, 0, 0))
    bsp = pl.BlockSpec((NH, HD), lambda i: (0, 0))
    in_specs = [every, lat, lat] + [pl.BlockSpec((D, D), lambda i, k=k: (0, k)) for k in (0, 1, 4, 2, 3)]
    in_specs += [lat, pl.BlockSpec((T, D), lambda i: (0, 0)), pl.BlockSpec((8, D), lambda i: (0, 1)), vec, lat]
    in_specs += [lat, lat, lat, vec, vec, wsp, pl.BlockSpec((HD, NH), lambda i: (0, 0))]
    vs = jax.ShapeDtypeStruct((1, D), F32)
    zb = jax.ShapeDtypeStruct((lx, D), BF16)
    return pl.pallas_call(
        body, name="proj_bwd", grid=(n,),
        out_shape=(jax.ShapeDtypeStruct((lx, D), F32), vs, vs, vs, vs, vs, zb, zb,
                   jax.ShapeDtypeStruct((NH, HD, HD), F32), jax.ShapeDtypeStruct((NH, HD), F32), vs, vs),
        in_specs=in_specs, out_specs=(lat, vec, vec, vec, vec, vec, lat, lat, wsp, bsp, vec, vec),
        scratch_shapes=[pltpu.VMEM((HD, D), F32), pltpu.VMEM((HD, D), F32)], compiler_params=_cp(1, vmem_mb=56),
    )(dxa, dga, dgb, *([w_full] * 5), x, ctx, mods, norm_g, dxn, *sgu)


def _adam_math(w, g, m, v):
    m = ADAM_B1 * m + (1.0 - ADAM_B1) * g
    v = ADAM_B2 * v + (1.0 - ADAM_B2) * (g * g)
    m_hat = m / (1.0 - ADAM_B1 ** ADAM_STEP)
    v_hat = v / (1.0 - ADAM_B2 ** ADAM_STEP)
    delta = -ADAM_LR * (m_hat / (jnp.sqrt(v_hat) + ADAM_EPS) + ADAM_WD * w)
    return delta, m, v


def _adam_big(w, g, m, v, name):
    rows, cols = w.shape
    tr = 256

    def body(w_ref, g_ref, m_ref, v_ref, d_o, m_o, v_o):
        d, mm, vv = _adam_math(w_ref[...], g_ref[...], m_ref[...], v_ref[...])
        d_o[...] = d
        m_o[...] = mm
        v_o[...] = vv

    blk = pl.BlockSpec((tr, cols), lambda i: (i, 0))
    s = jax.ShapeDtypeStruct((rows, cols), F32)
    return pl.pallas_call(
        body, name=name, grid=(rows // tr,), out_shape=(s, s, s), in_specs=[blk] * 4, out_specs=(blk,) * 3,
        compiler_params=_cp(1, vmem_mb=48),
    )(w, g, m, v)


def _adam_small(items, tot):
    ni = len(items)
    pieces = [it[1] if isinstance(it[1], list) else None for it in items]
    flat = [a for it, pc in zip(items, pieces) for a in ((it[0], it[2], it[3]) if pc is not None else it)]
    n_in = len(flat) + 1
    out_shape = tuple(jax.ShapeDtypeStruct(it[0].shape, F32) for it, pc in zip(items, pieces)
                      for _ in range(4 if pc is not None else 3))
    n_out = len(out_shape)
    n_loads = sum(3 + (len(pc) if pc is not None else 1) for pc in pieces)

    def body(*refs):
        ins, tot_ref, outs = refs[:n_in - 1], refs[n_in - 1], refs[n_in:n_in + n_out]
        bufs = refs[n_in + n_out:n_in + n_out + 7 * ni]
        sem_in, sem_out = refs[n_in + n_out + 7 * ni:]
        loads, q_in, q_sem = [], 0, 0
        for k, pc in enumerate(pieces):
            w_b, g_b, m_b, v_b = bufs[7 * k:7 * k + 4]
            srcs = [(ins[q_in], w_b)]
            if pc is None:
                srcs.append((ins[q_in + 1], g_b))
                q_in += 1
            else:
                srcs += [(tot_ref.at[pl.ds(r0, nr), pl.ds(c0, nc)], g_b.at[pl.ds(d0, nr), :]) for r0, nr, c0, nc, d0 in pc]
            srcs += [(ins[q_in + 1], m_b), (ins[q_in + 2], v_b)]
            q_in += 3
            mine = []
            for src, dst in srcs:
                mine.append(pltpu.make_async_copy(src, dst, sem_in.at[q_sem]))
                q_sem += 1
            loads.append(mine)
        for mine in loads:
            for cp in mine:
                cp.start()
        stores, q_out = [], 0
        for k, pc in enumerate(pieces):
            for cp in loads[k]:
                cp.wait()
            w_b, g_b, m_b, v_b = bufs[7 * k:7 * k + 4]
            res = _adam_math(w_b[...], g_b[...], m_b[...], v_b[...])
            srcs = []
            for q in range(3):
                bufs[7 * k + 4 + q][...] = res[q]
                srcs.append(bufs[7 * k + 4 + q])
            if pc is not None:
                srcs.append(g_b)
            for src in srcs:
                cp = pltpu.make_async_copy(src, outs[q_out], sem_out.at[q_out])
                cp.start()
                stores.append(cp)
                q_out += 1
        for cp in stores:
            cp.wait()

    scratch = [pltpu.VMEM(it[0].shape, F32) for it in items for _ in range(7)]
    scratch += [pltpu.SemaphoreType.DMA((n_loads,)), pltpu.SemaphoreType.DMA((n_out,))]
    res = pl.pallas_call(
        body, name="adam_small", out_shape=out_shape, in_specs=[HBM] * n_in, out_specs=(HBM,) * n_out,
        scratch_shapes=scratch, compiler_params=_cp(vmem_mb=40),
    )(*[_in_hbm(a) for a in flat], _in_hbm(tot))
    outs, q = [], 0
    for pc in pieces:
        outs.append(tuple(res[q:q + 3]) + ((res[q + 3],) if pc is not None else (None,)))
        q += 4 if pc is not None else 3
    return outs


def kernel(x, c, ctx, c_ctx, ada_w, ada_b, norm_g, w_in, conv_w, conv_b, lru_wa, lru_ba, lru_wx, lru_bx, lru_lambda, sgu_ln_g, sgu_ln_b, sgu_w, sgu_b, w_out, final_g, loss_target, m_c_ctx, m_ada_w, m_ada_b, m_norm_g, m_w_in, m_conv_w, m_conv_b, m_lru_wa, m_lru_ba, m_lru_wx, m_lru_bx, m_lru_lambda, m_sgu_ln_g, m_sgu_ln_b, m_sgu_w, m_sgu_b, m_w_out, m_final_g, v_c_ctx, v_ada_w, v_ada_b, v_norm_g, v_w_in, v_conv_w, v_conv_b, v_lru_wa, v_lru_ba, v_lru_wx, v_lru_bx, v_lru_lambda, v_sgu_ln_g, v_sgu_ln_b, v_sgu_w, v_sgu_b, v_w_out, v_final_g):
    ix, iy, ic = lax.axis_index("x"), lax.axis_index("y"), lax.axis_index("c")
    chip = 2 * ix + iy
    dev = 2 * chip + ic
    lx = x.shape[1]
    lc = ctx.shape[1]

    smalls = jnp.concatenate([conv_w[0], lru_lambda[0], jnp.zeros((10, 256), F32)], axis=0)
    c_ctx2 = c_ctx.reshape(1, D)
    ada_b_j = lax.dynamic_slice(ada_b, (0, 768 * chip), (1, 768))
    mods, c_slots, sm_all, w_in_full, wo_land, ada_land, wcat, bcat, sgu_wb = _gather_in(
        c, c_ctx2, ada_w[0], ada_b_j, w_in[0], w_out[0], smalls,
        (lru_wa[0], lru_wx[0], lru_ba[0], lru_bx[0], sgu_w[0]))
    wo_ss, wo_rs, ada_ss, ada_rs, wo_land, ada_land, token = _late_gather_start(wo_land, ada_land)
    mods = mods + token[0:1, 0:1]
    sm3 = sm_all.reshape(NCHIP, 16, 256)
    conv_w_full = sm3[:, 0:4, :].transpose(1, 0, 2).reshape(4, D)
    lam_full = sm3[:, 4:6, :].transpose(1, 0, 2).reshape(2, D)
    conv_wz = conv_w_full.reshape(4, NH, HD)
    conv_bz = conv_b.reshape(NH, HD)
    lamcat = lam_full.reshape(2, NH, HD).transpose(1, 0, 2).reshape(NH, 2 * HD)
    sgu_bt = sgu_b[0].T
    final_g2 = final_g.reshape(1, D)

    zero_s = jnp.zeros((NH, HD), F32)
    hn, xa_all, ga, u, v, gb, ys = _proj(x[0], ctx[0], mods, norm_g, w_in_full, (sgu_ln_g, sgu_ln_b, sgu_wb, sgu_bt))
    xcz, af, ab, hf, hb, gf, gb_l, _, hb0 = _lru_fwd(xa_all, conv_wz, conv_bz, wcat, bcat, lamcat, "lru_fwd")

    w_out_full = _late_gather_wait(wo_land, wo_ss, wo_rs, "w_out", hf, "late_gather_wait_w_out")
    (loss_part, dfg, dgx, dxn, y, do, dga, dgb, dyl_z, dys) = _out_fwd_bwd(
        hf, hb, ga, gb, ys, x[0], loss_target[0], mods, final_g2, w_out_full)

    dxc_b, dxc_f, dwc, dbc, dlc = _lru_bwd(xcz, dyl_z, hf, hb, af, ab, gf, gb_l, hb0, wcat, lamcat, "lru_bwd")
    dxa, dcw, dcb = _conv_bwd(dxc_b, dxc_f, xa_all, conv_wz, jnp.zeros((4, NH, HD), F32), zero_s, "conv_bwd")
    dxa = _in_hbm(dxa)

    grad_x, dng, dsc_x, dsh_x, dsc_c, dsh_c, du, dv, d_sgu_w, d_sgu_b, d_ln_g, d_ln_b = _proj_bwd(
        dxa, dga, dgb, x[0], ctx[0], dxn, mods, norm_g, w_in_full, (u, v, dys, sgu_ln_g, sgu_ln_b, sgu_wb, sgu_bt))
    dzs = [dxa, dga, du, dv, dgb]

    dmx = jnp.concatenate([dsh_x, dsc_x, dgx], axis=0)
    dmc = jnp.concatenate([dsh_c, dsc_c, jnp.zeros((1, D), F32)], axis=0)
    slot = jnp.concatenate([dmx, loss_part], axis=0)
    slots = lax.dynamic_update_slice(jnp.zeros((32, D), F32), slot, (4 * dev, 0))
    vecs = jnp.concatenate([dfg, dng, dcb.reshape(1, D), d_ln_g, d_ln_b, dcw.reshape(4, D), dmc,
                            jnp.zeros((4, D), F32), slots], axis=0)
    d_sgu_w4 = d_sgu_w.reshape(4, 256, HD).transpose(1, 0, 2).reshape(256, 4 * HD)
    pad8 = lambda a: jnp.pad(a, ((0, 8 - a.shape[0]), (0, 4 * HD - a.shape[1])))
    pack = jnp.concatenate([dwc.reshape(NH * HD, 4 * HD), pad8(dbc), pad8(dlc), d_sgu_w4, pad8(d_sgu_b),
                            vecs.reshape(96, 4 * HD), jnp.zeros((8, 4 * HD), F32)], axis=0)
    g_w_in, g_w_out, tot = _grads_reduce(hn, dzs, lc, y, do, _in_hbm(pack))

    n_w = NH * HD
    g_lru_wa = [(0, n_w, 2 * HD * d, HD, n_w * d) for d in range(2)]
    g_lru_wx = [(0, n_w, 2 * HD * d + HD, HD, n_w * d) for d in range(2)]
    g_lru_ba = [(n_w, NH, 2 * HD * d, HD, NH * d) for d in range(2)]
    g_lru_bx = [(n_w, NH, 2 * HD * d + HD, HD, NH * d) for d in range(2)]
    g_sgu_w = [(1040, 256, HD * q, HD, 256 * q) for q in range(4)]
    g_sgu_b = [(1296, NH, 0, HD, 0)]
    g_lc = tot[1032:1040, 0:2 * HD]
    tv = tot[1304:1400].reshape(48, D)
    g_final_g, g_norm_g, g_conv_b, g_ln_g, g_ln_b = tv[0:1], tv[1:2], tv[2:3], tv[3:4], tv[4:5]
    g_conv_w_full = tv[5:9]
    dmc_tot = tv[9:12].reshape(1, 3 * D)
    slots_all = tv[16:48].reshape(8, 4, D)
    dmx_all = slots_all[:, 0:3, :].reshape(8, 3 * D)
    c_all = c_slots.reshape(8, 8, D)[:, 0, :]
    g_lam_full = jnp.stack([g_lc[:, 0:HD], g_lc[:, HD:2 * HD]]).reshape(2, D)
    g_conv_w = lax.dynamic_slice(g_conv_w_full, (0, 256 * chip), (4, 256))
    g_lam = lax.dynamic_slice(g_lam_full, (0, 256 * chip), (2, 256))
    dmx_all_j = lax.dynamic_slice(dmx_all, (0, 768 * chip), (8, 768))
    dmc_j = lax.dynamic_slice(dmc_tot, (0, 768 * chip), (1, 768))
    ada_full = _late_gather_wait(ada_land, ada_ss, ada_rs, "ada_w", _in_hbm(tot), "late_gather_wait_ada_w")
    g_ada_w, g_ada_b, g_c_ctx = _ada_bwd(c_all, dmx_all_j, dmc_j, dmx_all, dmc_tot, c_ctx2, ada_full)

    big = {
        "ada_w": _adam_big(ada_w[0], g_ada_w, m_ada_w[0], v_ada_w[0], "adam_ada_w"),
        "w_in": _adam_big(w_in[0], g_w_in, m_w_in[0], v_w_in[0], "adam_w_in"),
        "w_out": _adam_big(w_out[0], g_w_out, m_w_out[0], v_w_out[0], "adam_w_out"),
    }
    small_in = {
        "c_ctx": (c_ctx, g_c_ctx, m_c_ctx, v_c_ctx, (1, D)),
        "ada_b": (ada_b, g_ada_b, m_ada_b, v_ada_b, (1, 3 * D)),
        "norm_g": (norm_g, g_norm_g, m_norm_g, v_norm_g, (1, D)),
        "conv_w": (conv_w, g_conv_w, m_conv_w, v_conv_w, (4, 256)),
        "conv_b": (conv_b, g_conv_b, m_conv_b, v_conv_b, (1, D)),
        "lru_wa": (lru_wa, g_lru_wa, m_lru_wa, v_lru_wa, (2 * NH * HD, HD)),
        "lru_ba": (lru_ba, g_lru_ba, m_lru_ba, v_lru_ba, (2 * NH, HD)),
        "lru_wx": (lru_wx, g_lru_wx, m_lru_wx, v_lru_wx, (2 * NH * HD, HD)),
        "lru_bx": (lru_bx, g_lru_bx, m_lru_bx, v_lru_bx, (2 * NH, HD)),
        "lru_lambda": (lru_lambda, g_lam, m_lru_lambda, v_lru_lambda, (2, 256)),
        "sgu_ln_g": (sgu_ln_g, g_ln_g, m_sgu_ln_g, v_sgu_ln_g, (1, D)),
        "sgu_ln_b": (sgu_ln_b, g_ln_b, m_sgu_ln_b, v_sgu_ln_b, (1, D)),
        "sgu_w": (sgu_w, g_sgu_w, m_sgu_w, v_sgu_w, (NH * HD, HD)),
        "sgu_b": (sgu_b, g_sgu_b, m_sgu_b, v_sgu_b, (NH, HD)),
        "final_g": (final_g, g_final_g, m_final_g, v_final_g, (1, D)),
    }
    names_small = list(small_in)
    res_small = _adam_small([tuple(a if isinstance(a, list) else a.reshape(small_in[k][4]) for a in small_in[k][:4])
                             for k in names_small], tot)
    full_shapes = {"ada_w": ada_w.shape, "w_in": w_in.shape, "w_out": w_out.shape}
    grads, deltas, new_m, new_v = {}, {}, {}, {}
    for k in ("ada_w", "w_in", "w_out"):
        g = {"ada_w": g_ada_w, "w_in": g_w_in, "w_out": g_w_out}[k]
        grads[k] = g.reshape(full_shapes[k])
        deltas[k], new_m[k], new_v[k] = (a.reshape(full_shapes[k]) for a in big[k])
    for k, res in zip(names_small, res_small):
        shape = small_in[k][0].shape
        grads[k] = (small_in[k][1] if res[3] is None else res[3]).reshape(shape)
        deltas[k], new_m[k], new_v[k] = (a.reshape(shape) for a in res[:3])

    loss = jnp.sum(slots_all[:, 3, 0:3])
    order = ["c_ctx", "ada_w", "ada_b", "norm_g", "w_in", "conv_w", "conv_b", "lru_wa", "lru_ba", "lru_wx", "lru_bx",
             "lru_lambda", "sgu_ln_g", "sgu_ln_b", "sgu_w", "sgu_b", "w_out", "final_g"]
    return (loss, grad_x.reshape(x.shape), *[grads[k] for k in order], *[deltas[k] for k in order],
            *[new_m[k] for k in order], *[new_v[k] for k in order])
```

```python
import jax
import jax.numpy as jnp
from jax import lax
from jax.experimental import pallas as pl
from jax.experimental.pallas import tpu as pltpu

F32 = jnp.float32
BF16 = jnp.bfloat16

D = 1024
NH = 8
HD = 128
NCHIP = 4
T = 256
NORM_EPS = 1e-6
LN_EPS = 1e-5
LRU_C = 8.0
ADAM_LR = 0.001
ADAM_B1 = 0.9
ADAM_B2 = 0.999
ADAM_EPS = 1e-08
ADAM_WD = 0.01
ADAM_STEP = 10

VMEM = pl.BlockSpec(memory_space=pltpu.VMEM)
ANY = pl.BlockSpec(memory_space=pl.ANY)
MESH = pl.DeviceIdType.MESH


def _cp(n_grid=0, vmem_mb=None):
    kw = {}
    if n_grid:
        kw["dimension_semantics"] = ("arbitrary",) * n_grid
    if vmem_mb:
        kw["vmem_limit_bytes"] = vmem_mb << 20
    return pltpu.CompilerParams(**kw)


def _sigmoid(x):
    return 0.5 * jnp.tanh(0.5 * x) + 0.5


def _silu_and_grad(x):
    s = _sigmoid(x)
    return x * s, s * (1.0 + x * (1.0 - s))


_GELU_K = 0.7978845608028654
_GELU_C = 0.044715


def _gelu_and_grad(x):
    x2 = x * x
    th = jnp.tanh(x * (_GELU_K + (_GELU_K * _GELU_C) * x2))
    p = 0.5 + 0.5 * th
    g = x * p
    dg = p + g * (1.0 - th) * (_GELU_K + (3.0 * _GELU_K * _GELU_C) * x2)
    return g, dg


def _softplus(x):
    return jnp.maximum(x, 0.0) + jnp.log1p(jnp.exp(-jnp.abs(x)))


def _lru_gate(pre, lam_row, d, off=None):
    off = 256 * d if off is None else off
    r = _sigmoid(pre[:, off:off + HD])
    gi = _sigmoid(pre[:, off + HD:off + 2 * HD])
    lam = lam_row[:, HD * d:HD * d + HD]
    sp = _softplus(-lam)
    la = (-LRU_C) * r * sp
    a = jnp.exp(la)
    x2 = 2.0 * la
    m2 = jnp.where(x2 > -1e-3, -x2 * (1.0 + 0.5 * x2), 1.0 - a * a)
    mult = jnp.sqrt(m2)
    return r, gi, lam, sp, a, mult


def _dot(a, b):
    return jnp.dot(a, b, preferred_element_type=F32)


def _dot_tn(a, b):
    return lax.dot_general(a, b, (((0,), (0,)), ((), ())), preferred_element_type=F32)


def _dot_nt(a, b):
    return lax.dot_general(a, b, (((1,), (1,)), ((), ())), preferred_element_type=F32)


def _mo(v, m):
    return v if isinstance(v, int) else pl.multiple_of(v, m)


def _zrows(h, n):
    return pl.ds(h, n, stride=NH)


def _gather_in(c, c_ctx, ada_w, ada_b_j, w_in, w_out, smalls, lru_sgu):
    nch = [1, 4]
    wrows = lambda cc, q: (pl.ds(_mo(512 * cc, 16), 512) if q is None
                           else pl.ds(_mo(512 * cc + (512 // nch[1]) * q, 16), 512 // nch[1]))
    specs = [
        ((64, 256), F32, lambda r, jj, cc, q=None: r.at[pl.ds(_mo(16 * jj + 8 * cc, 8), 8), :]),
        ((D, 5120), BF16, lambda r, jj, cc, q=None: r.at[wrows(cc, q), pl.ds(_mo(1280 * jj, 128), 1280)]),
    ]
    halves = [lambda r, cc, q=None: r.at[pl.ds(_mo(8 * cc, 8), 8), :],
              lambda r, cc, q=None: r.at[wrows(cc, q), :]]
    na = len(specs)
    sem_base = [0, 6 * nch[0]]
    sidx = lambda a, q, k: sem_base[a] + 6 * q + k
    n_tiny = 6 * sum(nch)
    n_sem = n_tiny + 10

    def body(c_ref, cc_ref, ada_ref, adab_ref, win_ref, wout_ref, sm_ref, wa_ref, wx_ref, ba_ref, bx_ref, sw_ref,
             mods_o, call_o, sm_o, win_o, wol_o, adal_o, wcat_o, bcat_o, swb_o, cwz_o, lam_o,
             s_win, s_ada, s_wout, f_win, f_ada, f_wout, cslot, lhs, mbuf,
             send_sems, recv_sems, local_sems, load_sems, f_w, f_sw, s_wcat, s_swb, prep_sems):
        x, y, c = lax.axis_index("x"), lax.axis_index("y"), lax.axis_index("c")
        j = 2 * x + y
        dev = 2 * j + c
        sib = (x, y, 1 - c)
        chips = [(1 - x, y), (x, 1 - y), (1 - x, 1 - y)]
        cj = [2 * cx + cy for cx, cy in chips]
        outs = [sm_o, win_o]
        srcs = [sm_ref, s_win]

        def copy(idx, src, dst, to):
            return pltpu.make_async_remote_copy(src_ref=src, dst_ref=dst, send_sem=send_sems.at[idx],
                                                recv_sem=recv_sems.at[idx], device_id=to, device_id_type=MESH)

        sends = []

        def start(cp):
            cp.start()
            sends.append(cp)

        cslot[...] = jnp.zeros_like(cslot)
        cslot[0:1, :] = c_ref[...]
        my_slot = pl.ds(_mo(8 * dev, 8), 8)
        others = [sib] + [(*chips[k], c) for k in range(3)] + [(*chips[k], 1 - c) for k in range(3)]
        other_dev = [dev + 1 - 2 * c] + [2 * cj[k] + c for k in range(3)] + [2 * cj[k] + 1 - c for k in range(3)]
        base = n_tiny
        for r in range(7):
            start(copy(base + r, cslot, call_o.at[my_slot, :], others[r]))
        call_o[my_slot, :] = cslot[...]

        crow = 512 // nch[1]
        loads = []
        for cc in (c, 1 - c):
            for q in range(nch[1]):
                rows = pl.ds(_mo(512 * cc + crow * q, 16), crow)
                loads.append(pltpu.make_async_copy(win_ref.at[rows, :], f_win.at[rows, :], load_sems.at[len(loads)]))
        loads.append(pltpu.make_async_copy(ada_ref, f_ada, load_sems.at[len(loads)]))
        loads.append(pltpu.make_async_copy(wout_ref, f_wout, load_sems.at[len(loads)]))
        for ld in loads:
            ld.start()
        prep = []
        for d in range(2):
            for q, (w_src, b_src) in enumerate([(wa_ref, ba_ref), (wx_ref, bx_ref)]):
                prep.append(pltpu.make_async_copy(w_src.at[d], f_w.at[2 * d + q], prep_sems.at[len(prep)]))
                prep.append(pltpu.make_async_copy(b_src.at[d], bcat_o.at[:, pl.ds(HD * (2 * d + q), HD)],
                                                  prep_sems.at[len(prep)]))
        prep.append(pltpu.make_async_copy(sw_ref, f_sw, prep_sems.at[len(prep)]))
        for cp in prep:
            cp.start()
        for k in range(2):
            start(copy(sidx(0, 0, k), halves[0](srcs[0], c), specs[0][2](outs[0], j, c), (*chips[k], c)))
        for q in range(nch[1]):
            loads[q].wait()
            rows = pl.ds(_mo(512 * c + crow * q, 16), crow)
            s_win[rows, :] = f_win[rows, :].astype(BF16)
            for k in range(2):
                start(copy(sidx(1, q, k), halves[1](s_win, c, q), specs[1][2](win_o, j, c, q), (*chips[k], c)))
        for q in range(nch[1]):
            loads[nch[1] + q].wait()
            rows = pl.ds(_mo(512 * (1 - c) + crow * q, 16), crow)
            s_win[rows, :] = f_win[rows, :].astype(BF16)
        local = []
        for a in range(na):
            for cc in range(2):
                lc = pltpu.make_async_copy(halves[a](srcs[a], cc), specs[a][2](outs[a], j, cc), local_sems.at[2 * a + cc])
                lc.start()
                local.append(lc)
        loads[2 * nch[1]].wait()
        s_ada[...] = f_ada[...].astype(BF16)
        loads[2 * nch[1] + 1].wait()
        s_wout[...] = f_wout[...].astype(BF16)
        for q, (src, dst) in enumerate([(s_wout, wol_o.at[pl.ds(_mo(512 * j, 16), 512), :]),
                                        (s_ada, adal_o.at[:, pl.ds(_mo(768 * j, 128), 768)])]):
            lc = pltpu.make_async_copy(src, dst, local_sems.at[2 * na + q])
            lc.start()
            local.append(lc)
        for cp in prep:
            cp.wait()
        for q in range(4):
            s_wcat[:, :, HD * q:HD * q + HD] = f_w[q].astype(BF16)
        s_swb[...] = f_sw[...].astype(BF16)
        for n, (src, dst) in enumerate([(s_wcat, wcat_o), (s_swb, swb_o)]):
            lc = pltpu.make_async_copy(src, dst, prep_sems.at[len(prep) + n])
            lc.start()
            local.append(lc)

        for r in range(7):
            slot = call_o.at[pl.ds(_mo(8 * other_dev[r], 8), 8), :]
            copy(base + r, slot, slot, sib).wait_recv()
        lhs[...] = jnp.zeros_like(lhs)
        for b in range(8):
            cv = call_o[8 * b:8 * b + 1, :]
            lhs[b:b + 1, :] = cv * _sigmoid(cv)
        cv = cc_ref[...]
        lhs[8:9, :] = cv * _sigmoid(cv)
        mbuf[j] = _dot(lhs[...].astype(BF16), s_ada[...]) + adab_ref[...]
        for k in range(3):
            start(copy(base + 7 + k, mbuf.at[j], mbuf.at[j], (*chips[k], c)))
        for k in range(3):
            copy(base + 7 + k, mbuf.at[cj[k]], mbuf.at[cj[k]], sib).wait_recv()
        mods_o[...] = jnp.zeros_like(mods_o)
        for jj in range(NCHIP):
            mods_o[0:1, 768 * jj:768 * jj + 768] = mbuf[jj, pl.ds(dev, 1), :]
            mods_o[1:2, 768 * jj:768 * jj + 768] = mbuf[jj, 8:9, :]

        kx = [1 - x, x, 1 - x]
        ky = [y, 1 - y, 1 - y]
        pick = lambda k, lst: jnp.where(k == 0, lst[0], jnp.where(k == 1, lst[1], lst[2]))
        for a in range(na):
            for q in range(nch[a]):
                for step, k in enumerate([c, 1 - c]):
                    reg = specs[a][2](outs[a], pick(k, cj), c, q)
                    copy(sidx(a, q, k), reg, reg, sib).wait_recv()
                    if step == 0:
                        start(copy(sidx(a, q, 2), reg, reg, (pick(1 - c, kx), pick(1 - c, ky), c)))
                    start(copy(sidx(a, q, 3 + k), reg, reg, sib))
        for a in range(na):
            for q in range(nch[a]):
                reg = specs[a][2](outs[a], cj[2], c, q)
                copy(sidx(a, q, 2), reg, reg, sib).wait_recv()
                start(copy(sidx(a, q, 5), reg, reg, sib))
        for a in range(na):
            for q in range(nch[a]):
                for k in range(3):
                    reg = specs[a][2](outs[a], cj[k], 1 - c, q)
                    copy(sidx(a, q, 3 + k), reg, reg, sib).wait_recv()
        for cp in sends:
            cp.wait_send()
        for lc in local:
            lc.wait()
        for jj in range(NCHIP):
            for mh in range(2):
                h = 2 * jj + mh
                cols = slice(HD * mh, HD * mh + HD)
                for r in range(4):
                    cwz_o[r, h:h + 1, :] = sm_o[16 * jj + r:16 * jj + r + 1, cols]
                for d in range(2):
                    lam_o[h:h + 1, HD * d:HD * d + HD] = sm_o[16 * jj + 4 + d:16 * jj + 5 + d, cols]

    out_shape = (jax.ShapeDtypeStruct((8, 3 * D), F32), jax.ShapeDtypeStruct((64, D), F32),
                 jax.ShapeDtypeStruct(specs[0][0], F32), jax.ShapeDtypeStruct(specs[1][0], BF16),
                 jax.ShapeDtypeStruct((2048, D), BF16), jax.ShapeDtypeStruct((D, 3 * D), BF16),
                 jax.ShapeDtypeStruct((NH, HD, 4 * HD), BF16), jax.ShapeDtypeStruct((NH, 4 * HD), F32),
                 jax.ShapeDtypeStruct((NH, HD, HD), BF16), jax.ShapeDtypeStruct((4, NH, HD), F32),
                 jax.ShapeDtypeStruct((NH, 2 * HD), F32))
    return pl.pallas_call(
        body, name="gather_in", out_shape=out_shape,
        in_specs=[VMEM, VMEM, ANY, VMEM, ANY, ANY, VMEM] + [ANY] * 5,
        out_specs=(VMEM, VMEM, VMEM, ANY, ANY, ANY, ANY, ANY, ANY, VMEM, VMEM),
        scratch_shapes=[pltpu.VMEM((D, 1280), BF16), pltpu.VMEM((D, 768), BF16), pltpu.VMEM((512, D), BF16),
                        pltpu.VMEM((D, 1280), F32), pltpu.VMEM((D, 768), F32), pltpu.VMEM((512, D), F32),
                        pltpu.VMEM((8, D), F32), pltpu.VMEM((16, D), F32), pltpu.VMEM((NCHIP, 16, 768), F32),
                        pltpu.SemaphoreType.DMA((n_sem,)), pltpu.SemaphoreType.DMA((n_sem,)),
                        pltpu.SemaphoreType.DMA((2 * na + 2,)), pltpu.SemaphoreType.DMA((2 * nch[1] + 2,)),
                        pltpu.VMEM((4, NH, HD, HD), F32), pltpu.VMEM((NH, HD, HD), F32),
                        pltpu.VMEM((NH, HD, 4 * HD), BF16), pltpu.VMEM((NH, HD, HD), BF16),
                        pltpu.SemaphoreType.DMA((11,))],
        compiler_params=_cp(vmem_mb=56),
    )(c, c_ctx, ada_w, ada_b_j, w_in, w_out, smalls, *[_in_hbm(a) for a in lru_sgu])


HBM = pl.BlockSpec(memory_space=pltpu.HBM)
SEM = pl.BlockSpec(memory_space=pltpu.SEMAPHORE)


def _in_hbm(a):
    return pltpu.with_memory_space_constraint(a, pltpu.HBM)


def _late_gather_regions(x, y, c):
    chips = [(1 - x, y), (x, 1 - y), (1 - x, 1 - y)]
    wo_reg = lambda r, jj, cc: r.at[pl.ds(_mo(512 * jj + 256 * cc, 16), 256), :]
    ada_reg = lambda r, jj, cc: r.at[pl.ds(_mo(512 * cc, 16), 512), pl.ds(_mo(768 * jj, 128), 768)]
    return chips, wo_reg, ada_reg


def _late_gather_start(wo_land, ada_land):
    def body(wol_ref, adal_ref, wo_ss, wo_rs, ada_ss, ada_rs, wol_thru, adal_thru, token):
        x, y, c = lax.axis_index("x"), lax.axis_index("y"), lax.axis_index("c")
        j = 2 * x + y
        chips, wo_reg, ada_reg = _late_gather_regions(x, y, c)
        for k in range(3):
            for cc in range(2):
                pltpu.make_async_remote_copy(src_ref=wo_reg(wol_ref, j, c), dst_ref=wo_reg(wol_ref, j, c),
                                             send_sem=wo_ss.at[2 * k + cc], recv_sem=wo_rs.at[2 * k + c],
                                             device_id=(*chips[k], cc), device_id_type=MESH).start()
        for k in range(3):
            for cc in range(2):
                pltpu.make_async_remote_copy(src_ref=ada_reg(adal_ref, j, c), dst_ref=ada_reg(adal_ref, j, c),
                                             send_sem=ada_ss.at[2 * k + cc], recv_sem=ada_rs.at[2 * k + c],
                                             device_id=(*chips[k], cc), device_id_type=MESH).start()
        token[...] = jnp.zeros_like(token)

    sems = pltpu.SemaphoreType.DMA((6,))
    return pl.pallas_call(
        body, name="late_gather_start",
        out_shape=(sems, sems, sems, sems, pltpu.HBM(wo_land.shape, BF16), pltpu.HBM(ada_land.shape, BF16),
                   jax.ShapeDtypeStruct((8, 128), F32)),
        in_specs=(HBM, HBM), out_specs=(SEM, SEM, SEM, SEM, HBM, HBM, VMEM), input_output_aliases={0: 4, 1: 5},
        compiler_params=pltpu.CompilerParams(has_side_effects=pltpu.SideEffectType.DATAFLOW_SIDE_EFFECTING),
    )(_in_hbm(wo_land), _in_hbm(ada_land))


def _late_gather_wait(land, send_sems, recv_sems, which, after, name):
    def body(land_ref, ss, rs, after_ref, land_out):
        x, y, c = lax.axis_index("x"), lax.axis_index("y"), lax.axis_index("c")
        j = 2 * x + y
        chips, wo_reg, ada_reg = _late_gather_regions(x, y, c)
        reg = wo_reg if which == "w_out" else ada_reg
        for k in range(3):
            kj = 2 * chips[k][0] + chips[k][1]
            for cc in range(2):
                cp = pltpu.make_async_remote_copy(src_ref=reg(land_ref, j, c), dst_ref=reg(land_ref, kj, cc),
                                                  send_sem=ss.at[2 * k + cc], recv_sem=rs.at[2 * k + cc],
                                                  device_id=(*chips[k], cc), device_id_type=MESH)
                cp.wait_send()
                cp.wait_recv()

    return pl.pallas_call(
        body, name=name, out_shape=pltpu.HBM(land.shape, land.dtype),
        in_specs=(HBM, SEM, SEM, ANY), out_specs=HBM, input_output_aliases={0: 0},
        compiler_params=pltpu.CompilerParams(has_side_effects=pltpu.SideEffectType.DATAFLOW_SIDE_EFFECTING),
    )(land, send_sems, recv_sems, after)


RCHUNK = 16


def _grads_reduce(hn, dzs, lc, y, do, pack):
    rp = pack.shape[0]
    hp = rp // 2
    assert hp % RCHUNK == 0
    wi_w = 1280
    lx = hn.shape[0] - lc
    lt = lx + lc
    n_dz = len(dzs)

    def body(*refs):
        hn_hbm, dz_hbm = refs[0], refs[1:1 + n_dz]
        y_hbm, do_hbm, pk_hbm, wi_out, wo_out, pk_out = refs[1 + n_dz:7 + n_dz]
        (hn_mine, hn_other, dzbuf, wi_other, wi_mine, wi_recv, wi_send, wi_rb,
         y_blk, do_mine, do_other, wo_other, wo_mine, wo_recv, wo_send, wo_rb,
         pk_mine, pk_recv, pk_send, pk_rb, pk_own, send_sems, recv_sems, local_sems) = refs[7 + n_dz:]
        x, y, c = lax.axis_index("x"), lax.axis_index("y"), lax.axis_index("c")
        j = 2 * x + y
        sib = (x, y, 1 - c)
        chips = [(1 - x, y), (x, 1 - y), (1 - x, 1 - y)]
        cj = [2 * cx + cy for cx, cy in chips]
        near = (jnp.where(c == 0, 1 - x, x), jnp.where(c == 0, y, 1 - y), c)
        slabs = [cj[2], cj[0], cj[1], j]

        def copy(k, src, dst, to):
            return pltpu.make_async_remote_copy(src_ref=src, dst_ref=dst, send_sem=send_sems.at[k],
                                                recv_sem=recv_sems.at[k], device_id=to, device_id_type=MESH)

        def local(k, src, dst):
            cp = pltpu.make_async_copy(src, dst, local_sems.at[k])
            cp.start()
            return cp

        rows_half = lambda r, cc, n: r.at[pl.ds(_mo(cc * n, 16), n), :]
        cols_half = lambda r, cc, n: r.at[:, pl.ds(_mo(cc * n, 128), n)]
        pk_piece = lambda r, cc, jj: r.at[pl.ds(_mo(cc * hp, 16), hp), pl.ds(_mo(jj * 128, 128), 128)]

        sends = []

        def start(cp):
            cp.start()
            sends.append(cp)

        def dz_pieces(s):
            g0 = wi_w * s
            k0, off0 = g0 // D, g0 % D
            w0 = min(D - off0, wi_w)
            pieces = [(k0, off0, w0, 0)]
            if w0 < wi_w:
                pieces.append((k0 + 1, 0, wi_w - w0, w0))
            return pieces

        def dz_copies(s):
            cps = []
            for q, (k, off, w, dst) in enumerate(dz_pieces(s)):
                cps.append(pltpu.make_async_copy(dz_hbm[k].at[pl.ds(lc if k == 0 else 0, lx), pl.ds(off, w)],
                                                 dzbuf.at[pl.ds(lc, lx), pl.ds(dst, w)], local_sems.at[11 + q]))
            if s == 0:
                cps.append(pltpu.make_async_copy(dz_hbm[0].at[pl.ds(0, lc), :], dzbuf.at[pl.ds(0, lc), pl.ds(0, D)],
                                                 local_sems.at[13]))
            return cps

        def dz_load(sl):
            for s in range(NCHIP):
                @pl.when(sl == s)
                def _():
                    if s == 0:
                        dzbuf[pl.ds(0, lc), pl.ds(D, wi_w - D)] = jnp.zeros((lc, wi_w - D), BF16)
                    else:
                        dzbuf[pl.ds(0, lc), :] = jnp.zeros((lc, wi_w), BF16)
                    for cp in dz_copies(s):
                        cp.start()

        def dz_wait(sl):
            for s in range(NCHIP):
                @pl.when(sl == s)
                def _():
                    for cp in dz_copies(s):
                        cp.wait()

        l_pk = local(0, rows_half(pk_hbm, c, hp), pk_mine)
        start(copy(0, rows_half(pk_hbm, 1 - c, hp), pk_recv, sib))
        col = lambda r, cc: r.at[:, pl.ds(_mo(cc * 512, 128), 512)]
        do_loads = [local(7, col(do_hbm, c), do_mine), local(14, col(do_hbm, 1 - c), do_other)]
        hn_loads = [local(2, col(hn_hbm, c), hn_mine), local(4, col(hn_hbm, 1 - c), hn_other)]
        y_copy = lambda s: pltpu.make_async_copy(col(y_hbm, slabs[s]), y_blk, local_sems.at[1])
        y_copy(0).start()
        dz_load(slabs[0])

        def pair_sum(mine, recv, send, nrows, keep, relayed=None):
            def step(i, carry):
                rows = pl.ds(_mo(i * RCHUNK, RCHUNK), RCHUNK)
                s = mine[rows, :] + recv[rows, :].astype(F32)
                if relayed is not None:
                    s = s + relayed[rows, :].astype(F32)
                if keep:
                    mine[rows, :] = s
                if send is not None:
                    send[rows, :] = s.astype(BF16)
                return carry
            lax.fori_loop(0, nrows // RCHUNK, step, 0)

        def chip_sum(own, rb, nrows, terms=(0, 1, 2)):
            def step(i, carry):
                rows = pl.ds(_mo(i * RCHUNK, RCHUNK), RCHUNK)
                acc = own[rows, :]
                for q in terms:
                    acc = acc + rb[q, rows, :].astype(F32)
                own[rows, :] = acc
                return carry
            lax.fori_loop(0, nrows // RCHUNK, step, 0)

        w_in_g = dict(other=wi_other, mine=wi_mine, recv=wi_recv, send=wi_send, rb=wi_rb, p1_sems=(2, 3, 4, 5), p2_sem=12,
                      p1=[None] * NCHIP, wait_load=lambda s: dz_wait(slabs[s]), load=lambda s: dz_load(slabs[s]),
                      dot_other=lambda: _dot_tn(hn_other[...], dzbuf[...]), dot_mine=lambda: _dot_tn(hn_mine[...], dzbuf[...]))
        w_out_g = dict(other=wo_other, mine=wo_mine, recv=wo_recv, send=wo_send, rb=wo_rb, p1_sems=(1, 24, 25, 26), p2_sem=9,
                       p1=[None] * NCHIP, wait_load=lambda s: y_copy(s).wait(), load=lambda s: y_copy(s).start(),
                       dot_other=lambda: _dot_tn(y_blk[...], do_other[...]), dot_mine=lambda: _dot_tn(y_blk[...], do_mine[...]))

        def piece_matmuls(g, s):
            if s >= 2:
                g["p1"][s - 2].wait_send()
            g["wait_load"](s)
            g["other"][s % 2] = g["dot_other"]().astype(BF16)
            g["p1"][s] = copy(g["p1_sems"][s], g["other"].at[s % 2], g["recv"].at[s], sib)
            g["p1"][s].start()
            g["mine"][s % 2] = g["dot_mine"]()
            if s + 1 < NCHIP:
                g["load"](s + 1)

        def piece_finish(g, s):
            mine, recv, send, rb, p2 = g["mine"].at[s % 2], g["recv"].at[s], g["send"], g["rb"], g["p2_sem"]
            nrows = mine.shape[0]
            copy(g["p1_sems"][s], recv, recv, sib).wait_recv()
            if s == 3:
                pair_sum(mine, recv, None, nrows, True)
                return
            if s == 0:
                pair_sum(mine, recv, send.at[0], nrows, False)
                start(copy(p2, send.at[0], rb.at[0], near))
                return
            adds_relayed = c == (1 if s == 1 else 0)

            @pl.when(adds_relayed)
            def _():
                copy(p2, rb.at[0], rb.at[0], sib).wait_recv()
                pair_sum(mine, recv, send.at[s], nrows, False, rb.at[0])

            @pl.when(jnp.logical_not(adds_relayed))
            def _():
                pair_sum(mine, recv, send.at[s], nrows, False)
            start(copy(p2 + s, send.at[s], rb.at[s], (*chips[s - 1], c)))

        def piece_total(g):
            for k in (1, 2):
                copy(g["p2_sem"] + k, g["rb"].at[k], g["rb"].at[k], sib).wait_recv()
            chip_sum(g["mine"].at[1], g["rb"], g["mine"].shape[1], (1, 2))

        for cp in do_loads:
            cp.wait()
        piece_matmuls(w_out_g, 0)
        piece_matmuls(w_out_g, 1)
        piece_finish(w_out_g, 0)
        piece_matmuls(w_out_g, 2)
        piece_finish(w_out_g, 1)
        piece_matmuls(w_out_g, 3)
        piece_finish(w_out_g, 2)

        for cp in hn_loads:
            cp.wait()
        piece_matmuls(w_in_g, 0)

        l_pk.wait()
        copy(0, pk_recv, pk_recv, sib).wait_recv()
        pair_sum(pk_mine, pk_recv, pk_send, hp, True)
        for k in range(3):
            start(copy(6 + k, pk_send.at[:, pl.ds(_mo(cj[k] * 128, 128), 128)], pk_rb.at[k], (*chips[k], c)))
        l_pk_own = local(6, pk_mine.at[:, pl.ds(_mo(j * 128, 128), 128)], pk_own)

        piece_matmuls(w_in_g, 1)
        piece_finish(w_in_g, 0)

        l_pk_own.wait()
        for k in range(3):
            copy(6 + k, pk_rb.at[k], pk_rb.at[k], sib).wait_recv()
        chip_sum(pk_own, pk_rb, hp)
        l_pk_out = local(8, pk_own, pk_piece(pk_out, c, j))
        start(copy(15, pk_own, pk_piece(pk_out, c, j), sib))
        for k in range(2):
            start(copy(16 + k, pk_own, pk_piece(pk_out, c, j), (*chips[k], c)))

        piece_matmuls(w_in_g, 2)
        piece_finish(w_in_g, 1)
        piece_matmuls(w_in_g, 3)
        piece_finish(w_in_g, 2)

        piece_finish(w_out_g, 3)
        piece_total(w_out_g)
        l_wo_out = local(9, wo_mine.at[1], cols_half(wo_out, c, 512))
        start(copy(22, wo_mine.at[1], cols_half(wo_out, c, 512), sib))

        far = (jnp.where(c == 0, x, 1 - x), jnp.where(c == 0, 1 - y, y), c)
        for step, k in enumerate([c, 1 - c, 2]):
            reg = pk_piece(pk_out, c, jnp.where(k == 0, cj[0], jnp.where(k == 1, cj[1], cj[2])))
            copy(16 + k, reg, reg, sib).wait_recv()
            if step == 0:
                start(copy(18, reg, reg, far))
            start(copy(19 + k, reg, reg, sib))

        piece_finish(w_in_g, 3)
        piece_total(w_in_g)
        l_wi_out = local(10, wi_mine.at[1], rows_half(wi_out, c, 512))
        start(copy(23, wi_mine.at[1], rows_half(wi_out, c, 512), sib))

        reg = pk_piece(pk_out, 1 - c, j)
        copy(15, reg, reg, sib).wait_recv()
        for k in range(3):
            reg = pk_piece(pk_out, 1 - c, cj[k])
            copy(19 + k, reg, reg, sib).wait_recv()
        reg = cols_half(wo_out, 1 - c, 512)
        copy(22, reg, reg, sib).wait_recv()
        reg = rows_half(wi_out, 1 - c, 512)
        copy(23, reg, reg, sib).wait_recv()
        for cp in sends + w_in_g["p1"][2:] + w_out_g["p1"][2:]:
            cp.wait_send()
        for cp in (l_pk_out, l_wo_out, l_wi_out):
            cp.wait()

    return pl.pallas_call(
        body, name="grads_reduce",
        out_shape=(jax.ShapeDtypeStruct((D, wi_w), F32), jax.ShapeDtypeStruct((512, D), F32),
                   jax.ShapeDtypeStruct(pack.shape, F32)),
        in_specs=[ANY] * (4 + n_dz), out_specs=(ANY,) * 3,
        scratch_shapes=[
            pltpu.VMEM((lt, 512), BF16), pltpu.VMEM((lt, 512), BF16), pltpu.VMEM((lt, wi_w), BF16),
            pltpu.VMEM((2, 512, wi_w), BF16), pltpu.VMEM((2, 512, wi_w), F32), pltpu.VMEM((4, 512, wi_w), BF16),
            pltpu.VMEM((3, 512, wi_w), BF16), pltpu.VMEM((3, 512, wi_w), BF16),
            pltpu.VMEM((lx, 512), BF16), pltpu.VMEM((lx, 512), BF16), pltpu.VMEM((lx, 512), BF16),
            pltpu.VMEM((2, 512, 512), BF16), pltpu.VMEM((2, 512, 512), F32), pltpu.VMEM((4, 512, 512), BF16),
            pltpu.VMEM((3, 512, 512), BF16), pltpu.VMEM((3, 512, 512), BF16),
            pltpu.VMEM((hp, 512), F32), pltpu.VMEM((hp, 512), F32), pltpu.VMEM((hp, 512), BF16),
            pltpu.VMEM((3, hp, 128), BF16), pltpu.VMEM((hp, 128), F32),
            pltpu.SemaphoreType.DMA((27,)), pltpu.SemaphoreType.DMA((27,)), pltpu.SemaphoreType.DMA((15,))],
        compiler_params=_cp(vmem_mb=56),
    )(hn, *dzs, y, do, pack)


def _ada_bwd(c_all, dmx_all_j, dmc_j, dmx_all, dmc, c_ctx, ada_w_full):
    def body(c_ref, dmxj_ref, dmcj_ref, dmx_ref, dmc_ref, cc_ref, w_hbm, gw_ref, gb_ref, gc_ref, lhs, rhs, dm8,
             w_v, w_sem):
        w_loads = [pltpu.make_async_copy(w_hbm.at[:, pl.ds(D * k, D)], w_v.at[k], w_sem.at[k]) for k in range(3)]
        for cp in w_loads:
            cp.start()
        lhs[...] = jnp.zeros_like(lhs)
        rhs[...] = jnp.zeros_like(rhs)
        cv = c_ref[...]
        lhs[0:8, :] = cv * _sigmoid(cv)
        cc = cc_ref[...]
        a_c, da_c = _silu_and_grad(cc)
        lhs[8:9, :] = a_c
        rhs[0:8, :] = dmxj_ref[...]
        rhs[8:9, :] = dmcj_ref[...]
        gw_ref[...] = _dot_tn(lhs[...].astype(BF16), rhs[...].astype(BF16))
        gb_ref[...] = jnp.sum(dmx_ref[...], axis=0, keepdims=True) + dmc_ref[...]
        dm8[...] = jnp.zeros_like(dm8)
        dm8[0:1, :] = dmc_ref[...]
        da = jnp.zeros((8, D), F32)
        for k in range(3):
            w_loads[k].wait()
            da = da + _dot_nt(dm8[:, D * k:D * k + D].astype(BF16), w_v[k])
        gc_ref[...] = da[0:1, :] * da_c

    return pl.pallas_call(
        body, name="ada_bwd",
        out_shape=(jax.ShapeDtypeStruct((D, 768), F32), jax.ShapeDtypeStruct((1, 3 * D), F32),
                   jax.ShapeDtypeStruct((1, D), F32)),
        in_specs=[VMEM] * 6 + [ANY], out_specs=(VMEM,) * 3,
        scratch_shapes=[pltpu.VMEM((16, D), F32), pltpu.VMEM((16, 768), F32), pltpu.VMEM((8, 3 * D), F32),
                        pltpu.VMEM((3, D, D), BF16), pltpu.SemaphoreType.DMA((3,))],
        compiler_params=_cp(vmem_mb=32),
    )(c_all, dmx_all_j, dmc_j, dmx_all, dmc, c_ctx, _in_hbm(ada_w_full))


def _proj(x, ctx, mods, norm_g, w_full, sgu):
    lx, lc = x.shape[0], ctx.shape[0]
    assert lc == T
    n = 1 + lx // T

    def body(x_ref, c_ref, sh_ref, sc_ref, ng_ref, w0, w1, w2, w3, w4, g_ref, b_ref, sw_ref, bt_ref,
             hn_ref, xa_ref, ga_ref, u_ref, v_ref, gb_ref, ys_ref, mixed_s):
        i = pl.program_id(0)
        is_ctx = i == 0
        xv = jnp.where(is_ctx, c_ref[...], x_ref[...])
        sc = jnp.where(is_ctx, sc_ref[1:2, :], sc_ref[0:1, :])
        sh = jnp.where(is_ctx, sh_ref[1:2, :], sh_ref[0:1, :])
        r = lax.rsqrt(jnp.mean(xv * xv, axis=-1, keepdims=True) + NORM_EPS)
        hb = ((xv * r) * ng_ref[...] * (1.0 + sc) + sh).astype(BF16)
        hn_ref[...] = hb
        xa_ref[...] = _dot(hb, w0[...])

        @pl.when(i > 0)
        def _():
            u_ref[...] = _dot(hb, w2[...])
            v_ref[...] = _dot(hb, w3[...])
            for ch in range(T // HD):
                rows = slice(HD * ch, HD * ch + HD)
                ug = _sgu_parts(u_ref[rows, :], v_ref[rows, :], g_ref[...], b_ref[...], sw_ref, bt_ref, mixed_s)[0]
                ys_ref[rows, :] = ug * mixed_s[...]
            ga_ref[...] = _dot(hb, w1[...])
            gb_ref[...] = _dot(hb, w4[...])

    every = pl.BlockSpec((T, D), lambda i: (i, 0))
    lat = pl.BlockSpec((T, D), lambda i: (jnp.maximum(i - 1, 0), 0))
    vec = pl.BlockSpec((1, D), lambda i: (0, 0))
    in_specs = [lat, pl.BlockSpec((T, D), lambda i: (0, 0)), pl.BlockSpec((8, D), lambda i: (0, 0)),
                pl.BlockSpec((8, D), lambda i: (0, 1)), vec]
    in_specs += [pl.BlockSpec((D, D), lambda i, k=k: (0, k)) for k in range(5)]
    in_specs += [vec, vec, pl.BlockSpec((NH, HD, HD), lambda i: (0, 0, 0)), pl.BlockSpec((HD, NH), lambda i: (0, 0))]
    full_s = jax.ShapeDtypeStruct((lc + lx, D), F32)
    lat_s = jax.ShapeDtypeStruct((lx, D), F32)
    return pl.pallas_call(
        body, name="proj", grid=(n,),
        out_shape=(jax.ShapeDtypeStruct((lc + lx, D), BF16), full_s, lat_s, lat_s, lat_s, lat_s, lat_s),
        in_specs=in_specs, out_specs=(every, every, lat, lat, lat, lat, lat),
        scratch_shapes=[pltpu.VMEM((HD, D), F32)], compiler_params=_cp(1, vmem_mb=56),
    )(x, ctx, mods, mods, norm_g, *([w_full] * 5), *sgu)


def _tile_specs(rows_per_pos, width, n_tiles, tile):
    last = n_tiles * (T // 8) - 1
    r = rows_per_pos
    return [pl.BlockSpec((T * r, width), lambda i: (tile(i), 0)),
            pl.BlockSpec((8 * r, width), lambda i: (jnp.maximum(tile(i) * (T // 8) - 1, 0), 0)),
            pl.BlockSpec((8 * r, width), lambda i: (jnp.minimum((tile(i) + 1) * (T // 8), last), 0))]


def _has_prev(tile):
    return tile >= 2


def _has_next(tile, nt):
    return jnp.logical_and(tile >= 1, tile < nt - 1)


ZT = pl.BlockSpec((T * NH, HD), lambda i: (i, 0))
CONV_CHUNK = 32


SCAN_SUB = 8


def _scan_tile(chains, post, carry_ref):
    blk = T // SCAN_SUB

    def step(k, state):
        new = []
        for ci, (a_ref, x_ref, o_ref, q_ref, reverse, xscale) in enumerate(chains):
            for q in range(SCAN_SUB):
                s, p = state[ci * SCAN_SUB + q]
                t = (q + 1) * blk - 1 - k if reverse else q * blk + k
                r = pl.ds(_mo(t * NH, NH), NH)
                a = a_ref[r, :]
                x = x_ref[r, :] if xscale is None else x_ref[r, :] * xscale
                if post:
                    o = x + s
                    o_ref[r, :] = o
                    q_ref[r, :] = p
                    new.append((a * o, a * p))
                else:
                    o = a * s + x
                    p = a * p
                    o_ref[r, :] = o
                    q_ref[r, :] = p
                    new.append((o, p))
        return tuple(new)

    zero = jnp.zeros((NH, HD), F32)
    one = jnp.ones((NH, HD), F32)
    final = lax.fori_loop(0, blk, step, tuple((zero, one) for _ in range(len(chains) * SCAN_SUB)))
    for ci, (a_ref, x_ref, o_ref, q_ref, reverse, xscale) in enumerate(chains):
        carry = carry_ref[ci]
        for q in (range(SCAN_SUB - 1, -1, -1) if reverse else range(SCAN_SUB)):
            rows = pl.ds(q * blk * NH, blk * NH)
            fixed = o_ref[rows, :].reshape(blk, NH, HD) + q_ref[rows, :].reshape(blk, NH, HD) * carry[None]
            o_ref[rows, :] = fixed.reshape(blk * NH, HD)
            s_loc, p_loc = final[ci * SCAN_SUB + q]
            carry = s_loc + p_loc * carry
        carry_ref[ci] = carry


def _lru_fwd(xa, conv_wz, conv_bz, wcat, bcat, lamcat, name):
    lx = xa.shape[0]
    n = lx // T
    tile_u = lambda i: i
    tile_d = lambda i: jnp.where(i == 0, 0, n - i)

    def body(xm_u, xp_u, xn_u, xm_d, xp_d, xn_d, cw, cb, w_ref, b_ref, lam_ref,
             xcz_o, af_o, ab_o, hf_o, hb_o, gf_o, gb_o, fu, fd, pad, xc_d, x_u, x_d, q_u, q_d, carry):
        i = pl.program_id(0)

        @pl.when(i == 0)
        def _():
            carry[...] = jnp.zeros_like(carry)

        def conv_gates(xm, xp, xn, tile, d, xc_ref, a_ref, x_ref, g_ref):
            pmask = jnp.where(_has_prev(tile), 1.0, 0.0)
            nmask = jnp.where(_has_next(tile, n), 1.0, 0.0)
            for h in range(NH):
                cols = slice(HD * h, HD * h + HD)
                pad[_zrows(h, 8), :] = xp[:, cols] * pmask
                pad[pl.ds(8 * NH + h, T, stride=NH), :] = xm[:, cols]
                pad[pl.ds((T + 8) * NH + h, 8, stride=NH), :] = xn[:, cols] * nmask

            def conv_chunk(ci, c_):
                base = pl.multiple_of(ci * (CONV_CHUNK * NH), CONV_CHUNK * NH)
                acc = None
                for k in range(4):
                    sl = pad[pl.ds(base + (7 + k) * NH, CONV_CHUNK * NH), :].reshape(CONV_CHUNK, NH, HD)
                    term = sl * cw[k][None]
                    acc = term if acc is None else acc + term
                acc = acc + cb[...][None]
                xc_ref[pl.ds(base, CONV_CHUNK * NH), :] = acc.reshape(CONV_CHUNK * NH, HD)
                return c_
            lax.fori_loop(0, T // CONV_CHUNK, conv_chunk, 0)

            for h in range(NH):
                xch = xc_ref[_zrows(h, T), :]
                pre = _dot(xch.astype(BF16), w_ref[h, :, 256 * d:256 * d + 256]) + b_ref[h:h + 1, 256 * d:256 * d + 256]
                r, gi, _, _, a, mult = _lru_gate(pre, lam_ref[h:h + 1, :], d, 0)
                a_ref[_zrows(h, T), :] = a
                x_ref[_zrows(h, T), :] = mult * gi * xch
                for q, val in enumerate((r, gi, mult)):
                    g_ref[:, q * D + HD * h:q * D + HD * h + HD] = val

        conv_gates(xm_u, xp_u, xn_u, tile_u(i), 0, xcz_o, af_o, x_u, gf_o)
        conv_gates(xm_d, xp_d, xn_d, tile_d(i), 1, xc_d, ab_o, x_d, gb_o)

        _scan_tile([(af_o, x_u, hf_o, q_u, False, None), (ab_o, x_d, hb_o, q_d, True, None)], False, carry)

        @pl.when(i == 0)
        def _():
            fu[...] = carry[0]
            fd[...] = carry[1]

    full = lambda shape: pl.BlockSpec(shape, lambda i: (0,) * len(shape))
    st = full((NH, HD))
    in_specs = _tile_specs(1, D, n, tile_u) + _tile_specs(1, D, n, tile_d)
    in_specs += [full((4, NH, HD)), st, full((NH, HD, 4 * HD)), full((NH, 4 * HD)), full((NH, 2 * HD))]
    up = pl.BlockSpec((T * NH, HD), lambda i: (tile_u(i), 0))
    dn = pl.BlockSpec((T * NH, HD), lambda i: (tile_d(i), 0))
    zs = jax.ShapeDtypeStruct((lx * NH, HD), F32)
    ss = jax.ShapeDtypeStruct((NH, HD), F32)
    zbuf = pltpu.VMEM((T * NH, HD), F32)
    gs = jax.ShapeDtypeStruct((lx, 3 * D), F32)
    g_up = pl.BlockSpec((T, 3 * D), lambda i: (tile_u(i), 0))
    g_dn = pl.BlockSpec((T, 3 * D), lambda i: (tile_d(i), 0))
    return pl.pallas_call(
        body, name=name, grid=(n,), out_shape=(zs,) * 5 + (gs, gs, ss, ss), in_specs=in_specs,
        out_specs=(up, up, dn, up, dn, g_up, g_dn, st, st),
        scratch_shapes=[pltpu.VMEM(((T + 16) * NH, HD), F32), zbuf, zbuf, zbuf, zbuf, zbuf,
                        pltpu.VMEM((2, NH, HD), F32)],
        compiler_params=_cp(1, vmem_mb=48),
    )(xa, xa, xa, xa, xa, xa, conv_wz, conv_bz, wcat, bcat, lamcat)


def _sgu_parts(u, v, lng, lnb, w_ref, bt_ref, mixed_s):
    ug, dug = _gelu_and_grad(u)
    vg, dvg = _gelu_and_grad(v)
    mu = jnp.mean(vg, axis=-1, keepdims=True)
    vc = vg - mu
    rstd = lax.rsqrt(jnp.mean(vc * vc, axis=-1, keepdims=True) + LN_EPS)
    vh = vc * rstd
    vn = (vh * lng + lnb).astype(BF16)
    for g in range(NH):
        cols = slice(HD * g, HD * g + HD)
        mixed_s[:, cols] = _dot(w_ref[g], vn[:, cols]) + bt_ref[:, g:g + 1]
    return ug, dug, dvg, rstd, vh, vn


def _sgu_bwd_chunk(u, v, dys_v, lng, lnb, w_ref, bt_ref, mixed_s, dvn_s, dw_ref, db_ref, dg_ref, dbl_ref):
    ug, dug, dvg, rstd, vh, vn = _sgu_parts(u, v, lng, lnb, w_ref, bt_ref, mixed_s)
    du = (dys_v * mixed_s[...] * dug).astype(BF16)
    dmix = dys_v * ug
    ones = jnp.ones((8, HD), BF16)
    for g in range(NH):
        cols = slice(HD * g, HD * g + HD)
        dm = dmix[:, cols]
        hi = dm.astype(BF16)
        lo = (dm - hi.astype(F32)).astype(BF16)
        dw_ref[g] += _dot_nt(hi, vn[:, cols])
        db_ref[g:g + 1, :] += (_dot_nt(ones, hi) + _dot_nt(ones, lo))[0:1, :]
        dvn_s[:, cols] = _dot_tn(w_ref[g], hi)
    dvn = dvn_s[...]
    dg_ref[...] += jnp.sum(dvn * vh, axis=0, keepdims=True)
    dbl_ref[...] += jnp.sum(dvn, axis=0, keepdims=True)
    dvh = dvn * lng
    dvg_in = rstd * (dvh - jnp.mean(dvh, axis=-1, keepdims=True) - vh * jnp.mean(dvh * vh, axis=-1, keepdims=True))
    return du, (dvg_in * dvg).astype(BF16)


def _out_fwd_bwd(hf_z, hb_z, ga, gb, ys, x, tgt, mods, final_g, w_out_full):
    lx = x.shape[0]
    n = lx // T

    def body(hf_ref, hb_ref, ga_ref, gb_ref, ys_ref, x_ref, t_ref, gx_ref, fg_ref, w_ref,
             loss_ref, dfg_ref, dgx_ref, dxn_ref, y_ref, do_ref, dga_ref, dgb_ref, dyl_ref, dys_ref, yl_s):
        i = pl.program_id(0)

        @pl.when(i == 0)
        def _():
            loss_ref[...] = jnp.zeros_like(loss_ref)
            dfg_ref[...] = jnp.zeros_like(dfg_ref)
            dgx_ref[...] = jnp.zeros_like(dgx_ref)

        for h in range(NH):
            yl_s[:, HD * h:HD * h + HD] = hf_ref[_zrows(h, T), :] + hb_ref[_zrows(h, T), :]
        yl = yl_s[...]
        gav = ga_ref[...]
        gbv = gb_ref[...]
        sa, dsa = _silu_and_grad(gav)
        sb, dsb = _silu_and_grad(gbv)
        ysv = ys_ref[...]
        y_ref[:, 0:D] = (yl * sa).astype(BF16)
        y_ref[:, D:2 * D] = (ysv * sb).astype(BF16)
        o = _dot(y_ref[...], w_ref[...])
        gx = gx_ref[0:1, :]
        xnew = x_ref[...] + gx * o
        r2 = lax.rsqrt(jnp.mean(xnew * xnew, axis=-1, keepdims=True) + NORM_EPS)
        xh = xnew * r2
        fg = fg_ref[...]
        err = xh * fg - t_ref[...]
        loss_ref[...] += 0.5 * jnp.sum(jnp.mean(err * err, axis=-1, keepdims=True), axis=0, keepdims=True)

        @pl.when(i == n - 1)
        def _():
            lp = loss_ref[...]
            lp1 = lp.astype(BF16).astype(F32)
            lp2 = (lp - lp1).astype(BF16).astype(F32)
            lp3 = (lp - lp1 - lp2).astype(BF16).astype(F32)
            lane = lax.broadcasted_iota(jnp.int32, lp.shape, 1)
            loss_ref[...] = jnp.where(lane == 0, lp1, jnp.where(lane == 1, lp2, jnp.where(lane == 2, lp3, 0.0)))
        dout = err * (1.0 / D)
        dfg_ref[...] += jnp.sum(dout * xh, axis=0, keepdims=True)
        dxh = dout * fg
        dxn = r2 * (dxh - xh * jnp.mean(dxh * xh, axis=-1, keepdims=True))
        dxn_ref[...] = dxn
        dgx_ref[...] += jnp.sum(dxn * o, axis=0, keepdims=True)
        do = (dxn * gx).astype(BF16)
        do_ref[...] = do
        dy = _dot_nt(do, w_ref[...])
        dy1 = dy[:, 0:D]
        dy2 = dy[:, D:2 * D]
        dga_ref[...] = (dy1 * yl * dsa).astype(BF16)
        dgb_ref[...] = (dy2 * ysv * dsb).astype(BF16)
        dys_ref[...] = dy2 * sb
        yl_s[...] = dy1 * sa
        for h in range(NH):
            dyl_ref[_zrows(h, T), :] = yl_s[:, HD * h:HD * h + HD]

    row = pl.BlockSpec((T, D), lambda i: (i, 0))
    vec = pl.BlockSpec((1, D), lambda i: (0, 0))
    zlat = pl.BlockSpec((T * NH, HD), lambda i: (i + 1, 0))
    in_specs = [zlat, zlat, row, row, row, row, row, pl.BlockSpec((8, D), lambda i: (0, 2)), vec,
                pl.BlockSpec((2 * D, D), lambda i: (0, 0))]
    out_shape = (jax.ShapeDtypeStruct((1, D), F32), jax.ShapeDtypeStruct((1, D), F32), jax.ShapeDtypeStruct((1, D), F32),
                 jax.ShapeDtypeStruct((lx, D), F32), jax.ShapeDtypeStruct((lx, 2 * D), BF16),
                 jax.ShapeDtypeStruct((lx, D), BF16), jax.ShapeDtypeStruct((lx, D), BF16),
                 jax.ShapeDtypeStruct((lx, D), BF16), jax.ShapeDtypeStruct((lx * NH, HD), F32),
                 jax.ShapeDtypeStruct((lx, D), F32))
    out_specs = (vec, vec, vec, row, pl.BlockSpec((T, 2 * D), lambda i: (i, 0)),
                 row, row, row, ZT, row)
    return pl.pallas_call(
        body, name="out_fwd_bwd", grid=(n,), out_shape=out_shape, in_specs=in_specs, out_specs=out_specs,
        scratch_shapes=[pltpu.VMEM((T, D), F32)],
        compiler_params=_cp(1, vmem_mb=56),
    )(hf_z, hb_z, ga, gb, ys, x, tgt, mods, final_g, w_out_full)


def _lru_bwd(xc_z, dy_z, hf_z, hb_z, af_z, ab_z, gf, gb, s_b, wcat, lamcat, name):
    lx = xc_z.shape[0] // NH
    n = lx // T
    tile_u = lambda i: jnp.where(i == n - 1, 0, i + 1)
    tile_d = lambda i: n - 1 - i

    def body(xc_u, dy_u, hb_ref, hbn_ref, ab_ref, gb_ref, xc_d, dy_d, hf_ref, hfp_ref, af_ref, gf_ref,
             sb_ref, w_ref, lam_ref, dxcb_ref, dxcf_ref, dw_ref, db_ref, dl_ref,
             lb_s, lf_s, q_u, q_d, pf_s, pb_s, dpre_s, carry):
        i = pl.program_id(0)
        tu, td = tile_u(i), tile_d(i)

        @pl.when(i == 0)
        def _():
            dw_ref[...] = jnp.zeros_like(dw_ref)
            db_ref[...] = jnp.zeros_like(db_ref)
            dl_ref[...] = jnp.zeros_like(dl_ref)
            carry[...] = jnp.zeros_like(carry)

        _scan_tile([(ab_ref, dy_u, lb_s, q_u, False, jnp.where(tu == 0, 0.0, 1.0)),
                    (af_ref, dy_d, lf_s, q_d, True, jnp.where(td == 0, 0.0, 1.0))], True, carry)
        zero = jnp.zeros((NH, HD), F32)
        pb_s[pl.ds(0, T * NH), :] = hb_ref[...]
        pb_s[pl.ds(T * NH, NH), :] = jnp.where(tu == n - 1, sb_ref[...], jnp.where(tu == 0, zero, hbn_ref[pl.ds(0, NH), :]))
        pf_s[pl.ds(0, NH), :] = jnp.where(td == 0, zero, hfp_ref[pl.ds(7 * NH, NH), :])
        pf_s[pl.ds(NH, T * NH), :] = hf_ref[...]
        sides = ((1, xc_u, lb_s, pb_s, NH, ab_ref, gb_ref, dxcb_ref), (0, xc_d, lf_s, pf_s, 0, af_ref, gf_ref, dxcf_ref))
        for d, xc_ref, adj_s, prev_s, prev_off, a_ref, g_ref, dxc_ref in sides:
            wcols = slice(256 * d, 256 * d + 256)
            for h in range(NH):
                xch = xc_ref[_zrows(h, T), :]
                xcb = xch.astype(BF16)
                r, gi, mult = (g_ref[:, q * D + HD * h:q * D + HD * h + HD] for q in range(3))
                a = a_ref[_zrows(h, T), :]
                lam = lam_ref[h:h + 1, HD * d:HD * d + HD]
                sp = _softplus(-lam)
                du = adj_s[_zrows(h, T), :]
                da = du * prev_s[pl.ds(prev_off + h, T, stride=NH), :]
                dgi = du * mult * xch
                dmult = du * gi * xch
                dla = da * a - dmult * (a * a) / mult
                dr = dla * ((-LRU_C) * sp)
                dsp = jnp.sum(dla * ((-LRU_C) * r), axis=0, keepdims=True)
                dl_ref[h:h + 1, HD * d:HD * d + HD] += dsp * (-_sigmoid(-lam))
                dpre_s[:, 0:HD] = dr * r * (1.0 - r)
                dpre_s[:, HD:2 * HD] = dgi * gi * (1.0 - gi)
                dpre = dpre_s[...]
                dpb = dpre.astype(BF16)
                dw_ref[h, :, wcols] += _dot_tn(xcb, dpb)
                db_ref[h:h + 1, wcols] += jnp.sum(dpre, axis=0, keepdims=True)
                dxc_ref[_zrows(h, T), :] = du * mult * gi + _dot_nt(dpb, w_ref[h, :, wcols])

    full = lambda shape: pl.BlockSpec(shape, lambda i: (0,) * len(shape))
    wsp, bsp, lsp = full((NH, HD, 4 * HD)), full((NH, 4 * HD)), full((NH, 2 * HD))
    st = full((NH, HD))
    up = pl.BlockSpec((T * NH, HD), lambda i: (tile_u(i), 0))
    dn = pl.BlockSpec((T * NH, HD), lambda i: (tile_d(i), 0))
    dy_up = pl.BlockSpec((T * NH, HD), lambda i: (jnp.maximum(tile_u(i) - 1, 0), 0))
    dy_dn = pl.BlockSpec((T * NH, HD), lambda i: (jnp.maximum(tile_d(i) - 1, 0), 0))
    nxt = _tile_specs(NH, HD, n, tile_u)[2]
    prv = _tile_specs(NH, HD, n, tile_d)[1]
    g_up = pl.BlockSpec((T, 3 * D), lambda i: (tile_u(i), 0))
    g_dn = pl.BlockSpec((T, 3 * D), lambda i: (tile_d(i), 0))
    zs = jax.ShapeDtypeStruct((lx * NH, HD), F32)
    zbuf = pltpu.VMEM((T * NH, HD), F32)
    zbuf1 = pltpu.VMEM(((T + 1) * NH, HD), F32)
    return pl.pallas_call(
        body, name=name, grid=(n,),
        out_shape=(zs, zs, jax.ShapeDtypeStruct((NH, HD, 4 * HD), F32), jax.ShapeDtypeStruct((NH, 4 * HD), F32),
                   jax.ShapeDtypeStruct((NH, 2 * HD), F32)),
        in_specs=[up, dy_up, up, nxt, up, g_up, dn, dy_dn, dn, prv, dn, g_dn, st, wsp, lsp],
        out_specs=(up, dn, wsp, bsp, lsp),
        scratch_shapes=[zbuf, zbuf, zbuf, zbuf, zbuf1, zbuf1, pltpu.VMEM((T, 2 * HD), F32),
                        pltpu.VMEM((2, NH, HD), F32)],
        compiler_params=_cp(1, vmem_mb=56),
    )(xc_z, dy_z, hb_z, hb_z, ab_z, gb, xc_z, dy_z, hf_z, hf_z, af_z, gf, s_b, wcat, lamcat)


def _conv_bwd(dxc_a, dxc_b, xa, conv_wz, dcw0, dcb0, name):
    lx = dxc_a.shape[0] // NH
    n = lx // T

    def body(dm_a, dp_a, dn_a, dm_b, dp_b, dn_b, xan_ref, cw, dcw0_ref, dcb0_ref, dxa_ref, dcw_ref, dcb_ref,
             pad, dxa_s, xa_ref):
        i = pl.program_id(0)

        @pl.when(i == 0)
        def _():
            dcw_ref[...] = dcw0_ref[...]
            dcb_ref[...] = dcb0_ref[...]

        for h in range(NH):
            xa_ref[_zrows(h, T), :] = xan_ref[:, HD * h:HD * h + HD]

        pmask = jnp.where(_has_prev(i), 1.0, 0.0)
        nmask = jnp.where(_has_next(i, n), 1.0, 0.0)
        pad[pl.ds(0, 8 * NH), :] = (dp_a[...] + dp_b[...]) * pmask
        pad[pl.ds(8 * NH, T * NH), :] = dm_a[...] + dm_b[...]
        pad[pl.ds((T + 8) * NH, 8 * NH), :] = (dn_a[...] + dn_b[...]) * nmask

        def chunk(ci, carry):
            base = pl.multiple_of(ci * (CONV_CHUNK * NH), CONV_CHUNK * NH)
            xav = xa_ref[pl.ds(base, CONV_CHUNK * NH), :].reshape(CONV_CHUNK, NH, HD)
            acc = None
            for k in range(4):
                sl = pad[pl.ds(base + (9 - k) * NH, CONV_CHUNK * NH), :].reshape(CONV_CHUNK, NH, HD)
                term = sl * cw[k][None]
                acc = term if acc is None else acc + term
                dcw_ref[k] += jnp.sum(sl * xav, axis=0)
                if k == 1:
                    dcb_ref[...] += jnp.sum(sl, axis=0)
            dxa_s[pl.ds(base, CONV_CHUNK * NH), :] = acc.reshape(CONV_CHUNK * NH, HD)
            return carry
        lax.fori_loop(0, T // CONV_CHUNK, chunk, 0)
        for h in range(NH):
            dxa_ref[:, HD * h:HD * h + HD] = dxa_s[_zrows(h, T), :].astype(BF16)

    full = lambda shape: pl.BlockSpec(shape, lambda i: (0,) * len(shape))
    return pl.pallas_call(
        body, name=name, grid=(n,),
        out_shape=(jax.ShapeDtypeStruct((lx, D), BF16), jax.ShapeDtypeStruct((4, NH, HD), F32),
                   jax.ShapeDtypeStruct((NH, HD), F32)),
        in_specs=_tile_specs(NH, HD, n, lambda i: i) * 2 + [pl.BlockSpec((T, D), lambda i: (i, 0)), full((4, NH, HD)),
                                                            full((4, NH, HD)), full((NH, HD))],
        out_specs=(pl.BlockSpec((T, D), lambda i: (i, 0)), full((4, NH, HD)), full((NH, HD))),
        scratch_shapes=[pltpu.VMEM(((T + 16) * NH, HD), F32), pltpu.VMEM((T * NH, HD), F32),
                        pltpu.VMEM((T * NH, HD), F32)],
        compiler_params=_cp(1, vmem_mb=48),
    )(dxc_a, dxc_a, dxc_a, dxc_b, dxc_b, dxc_b, xa, conv_wz, dcw0, dcb0)


def _proj_bwd(dxa, dga, dgb, x, ctx, dxn, mods, norm_g, w_full, sgu):
    lx, lc = x.shape[0], ctx.shape[0]
    assert lc == T
    n = 1 + lx // T

    def body(dxa_ref, dga_ref, dgb_ref, w0, w1, w4, w2, w3, x_ref, c_ref, sc_ref, ng_ref, dxn_ref,
             u_ref, v_ref, dy_ref, g_ref, b_ref, sw_ref, bt_ref,
             gx_ref, dng_ref, dscx_ref, dshx_ref, dscc_ref, dshc_ref, du_ref, dv_ref, dws_ref, dbs_ref, dlg_ref, dlb_ref,
             mixed_s, dvn_s):
        i = pl.program_id(0)
        is_ctx = i == 0

        @pl.when(is_ctx)
        def _():
            for acc in (dng_ref, dscx_ref, dshx_ref, dscc_ref, dshc_ref, dws_ref, dbs_ref, dlg_ref, dlb_ref):
                acc[...] = jnp.zeros_like(acc)

        xv = jnp.where(is_ctx, c_ref[...], x_ref[...])
        sc1 = 1.0 + jnp.where(is_ctx, sc_ref[1:2, :], sc_ref[0:1, :])
        r = lax.rsqrt(jnp.mean(xv * xv, axis=-1, keepdims=True) + NORM_EPS)
        xn = xv * r
        ng = ng_ref[...]

        def norm_bwd(dhn, dsc_ref, dsh_ref, with_x):
            t = dhn * xn
            dng_ref[...] += jnp.sum(t * sc1, axis=0, keepdims=True)
            dsc_ref[...] += jnp.sum(t * ng, axis=0, keepdims=True)
            dsh_ref[...] += jnp.sum(dhn, axis=0, keepdims=True)
            if with_x:
                dxh = dhn * (ng * sc1)
                gx_ref[...] = dxn_ref[...] + r * (dxh - xn * jnp.mean(dxh * xn, axis=-1, keepdims=True))

        @pl.when(is_ctx)
        def _():
            norm_bwd(_dot_nt(dxa_ref[...], w0[...]), dscc_ref, dshc_ref, False)

        @pl.when(i > 0)
        def _():
            def sgu_chunk(ch):
                rows = slice(HD * ch, HD * ch + HD)
                du, dv = _sgu_bwd_chunk(u_ref[rows, :], v_ref[rows, :], dy_ref[rows, :], g_ref[...], b_ref[...],
                                        sw_ref, bt_ref, mixed_s, dvn_s, dws_ref, dbs_ref, dlg_ref, dlb_ref)
                du_ref[rows, :] = du
                dv_ref[rows, :] = dv

            dhn = _dot_nt(dxa_ref[...], w0[...])
            sgu_chunk(0)
            dhn = dhn + _dot_nt(dga_ref[...], w1[...])
            for ch in range(1, T // HD):
                sgu_chunk(ch)
            dhn = dhn + _dot_nt(dgb_ref[...], w4[...])
            dhn = dhn + _dot_nt(du_ref[...], w2[...]) + _dot_nt(dv_ref[...], w3[...])
            norm_bwd(dhn, dscx_ref, dshx_ref, True)

    every = pl.BlockSpec((T, D), lambda i: (i, 0))
    lat = pl.BlockSpec((T, D), lambda i: (jnp.maximum(i - 1, 0), 0))
    vec = pl.BlockSpec((1, D), lambda i: (0, 0))
    wsp = pl.BlockSpec((NH, HD, HD), lambda i: (0, 0, 0))
    bsp = pl.BlockSpec((NH, HD), lambda i: (0, 0))
    in_specs = [every, lat, lat] + [pl.BlockSpec((D, D), lambda i, k=k: (0, k)) for k in (0, 1, 4, 2, 3)]
    in_specs += [lat, pl.BlockSpec((T, D), lambda i: (0, 0)), pl.BlockSpec((8, D), lambda i: (0, 1)), vec, lat]
    in_specs += [lat, lat, lat, vec, vec, wsp, pl.BlockSpec((HD, NH), lambda i: (0, 0))]
    vs = jax.ShapeDtypeStruct((1, D), F32)
    zb = jax.ShapeDtypeStruct((lx, D), BF16)
    return pl.pallas_call(
        body, name="proj_bwd", grid=(n,),
        out_shape=(jax.ShapeDtypeStruct((lx, D), F32), vs, vs, vs, vs, vs, zb, zb,
                   jax.ShapeDtypeStruct((NH, HD, HD), F32), jax.ShapeDtypeStruct((NH, HD), F32), vs, vs),
        in_specs=in_specs, out_specs=(lat, vec, vec, vec, vec, vec, lat, lat, wsp, bsp, vec, vec),
        scratch_shapes=[pltpu.VMEM((HD, D), F32), pltpu.VMEM((HD, D), F32)], compiler_params=_cp(1, vmem_mb=56),
    )(dxa, dga, dgb, *([w_full] * 5), x, ctx, mods, norm_g, dxn, *sgu)


def _adam_math(w, g, m, v):
    m = ADAM_B1 * m + (1.0 - ADAM_B1) * g
    v = ADAM_B2 * v + (1.0 - ADAM_B2) * (g * g)
    m_hat = m / (1.0 - ADAM_B1 ** ADAM_STEP)
    v_hat = v / (1.0 - ADAM_B2 ** ADAM_STEP)
    delta = -ADAM_LR * (m_hat / (jnp.sqrt(v_hat) + ADAM_EPS) + ADAM_WD * w)
    return delta, m, v


def _adam_big(w, g, m, v, name):
    rows, cols = w.shape
    tr = 256

    def body(w_ref, g_ref, m_ref, v_ref, d_o, m_o, v_o):
        d, mm, vv = _adam_math(w_ref[...], g_ref[...], m_ref[...], v_ref[...])
        d_o[...] = d
        m_o[...] = mm
        v_o[...] = vv

    blk = pl.BlockSpec((tr, cols), lambda i: (i, 0))
    s = jax.ShapeDtypeStruct((rows, cols), F32)
    return pl.pallas_call(
        body, name=name, grid=(rows // tr,), out_shape=(s, s, s), in_specs=[blk] * 4, out_specs=(blk,) * 3,
        compiler_params=_cp(1, vmem_mb=48),
    )(w, g, m, v)


def _adam_small(items, tot):
    ni = len(items)
    pieces = [it[1] if isinstance(it[1], list) else None for it in items]
    flat = [a for it, pc in zip(items, pieces) for a in ((it[0], it[2], it[3]) if pc is not None else it)]
    n_in = len(flat) + 1
    out_shape = tuple(jax.ShapeDtypeStruct(it[0].shape, F32) for it, pc in zip(items, pieces)
                      for _ in range(4 if pc is not None else 3))
    n_out = len(out_shape)
    n_loads = sum(3 + (len(pc) if pc is not None else 1) for pc in pieces)

    def body(*refs):
        ins, tot_ref, outs = refs[:n_in - 1], refs[n_in - 1], refs[n_in:n_in + n_out]
        bufs = refs[n_in + n_out:n_in + n_out + 7 * ni]
        sem_in, sem_out = refs[n_in + n_out + 7 * ni:]
        loads, q_in, q_sem = [], 0, 0
        for k, pc in enumerate(pieces):
            w_b, g_b, m_b, v_b = bufs[7 * k:7 * k + 4]
            srcs = [(ins[q_in], w_b)]
            if pc is None:
                srcs.append((ins[q_in + 1], g_b))
                q_in += 1
            else:
                srcs += [(tot_ref.at[pl.ds(r0, nr), pl.ds(c0, nc)], g_b.at[pl.ds(d0, nr), :]) for r0, nr, c0, nc, d0 in pc]
            srcs += [(ins[q_in + 1], m_b), (ins[q_in + 2], v_b)]
            q_in += 3
            mine = []
            for src, dst in srcs:
                mine.append(pltpu.make_async_copy(src, dst, sem_in.at[q_sem]))
                q_sem += 1
            loads.append(mine)
        for mine in loads:
            for cp in mine:
                cp.start()
        stores, q_out = [], 0
        for k, pc in enumerate(pieces):
            for cp in loads[k]:
                cp.wait()
            w_b, g_b, m_b, v_b = bufs[7 * k:7 * k + 4]
            res = _adam_math(w_b[...], g_b[...], m_b[...], v_b[...])
            srcs = []
            for q in range(3):
                bufs[7 * k + 4 + q][...] = res[q]
                srcs.append(bufs[7 * k + 4 + q])
            if pc is not None:
                srcs.append(g_b)
            for src in srcs:
                cp = pltpu.make_async_copy(src, outs[q_out], sem_out.at[q_out])
                cp.start()
                stores.append(cp)
                q_out += 1
        for cp in stores:
            cp.wait()

    scratch = [pltpu.VMEM(it[0].shape, F32) for it in items for _ in range(7)]
    scratch += [pltpu.SemaphoreType.DMA((n_loads,)), pltpu.SemaphoreType.DMA((n_out,))]
    res = pl.pallas_call(
        body, name="adam_small", out_shape=out_shape, in_specs=[HBM] * n_in, out_specs=(HBM,) * n_out,
        scratch_shapes=scratch, compiler_params=_cp(vmem_mb=40),
    )(*[_in_hbm(a) for a in flat], _in_hbm(tot))
    outs, q = [], 0
    for pc in pieces:
        outs.append(tuple(res[q:q + 3]) + ((res[q + 3],) if pc is not None else (None,)))
        q += 4 if pc is not None else 3
    return outs


def kernel(x, c, ctx, c_ctx, ada_w, ada_b, norm_g, w_in, conv_w, conv_b, lru_wa, lru_ba, lru_wx, lru_bx, lru_lambda, sgu_ln_g, sgu_ln_b, sgu_w, sgu_b, w_out, final_g, loss_target, m_c_ctx, m_ada_w, m_ada_b, m_norm_g, m_w_in, m_conv_w, m_conv_b, m_lru_wa, m_lru_ba, m_lru_wx, m_lru_bx, m_lru_lambda, m_sgu_ln_g, m_sgu_ln_b, m_sgu_w, m_sgu_b, m_w_out, m_final_g, v_c_ctx, v_ada_w, v_ada_b, v_norm_g, v_w_in, v_conv_w, v_conv_b, v_lru_wa, v_lru_ba, v_lru_wx, v_lru_bx, v_lru_lambda, v_sgu_ln_g, v_sgu_ln_b, v_sgu_w, v_sgu_b, v_w_out, v_final_g):
    ix, iy, ic = lax.axis_index("x"), lax.axis_index("y"), lax.axis_index("c")
    chip = 2 * ix + iy
    dev = 2 * chip + ic
    lx = x.shape[1]
    lc = ctx.shape[1]

    smalls = jnp.concatenate([conv_w[0], lru_lambda[0], jnp.zeros((10, 256), F32)], axis=0)
    c_ctx2 = c_ctx.reshape(1, D)
    ada_b_j = lax.dynamic_slice(ada_b, (0, 768 * chip), (1, 768))
    mods, c_slots, _, w_in_full, wo_land, ada_land, wcat, bcat, sgu_wb, conv_wz, lamcat = _gather_in(
        c, c_ctx2, ada_w[0], ada_b_j, w_in[0], w_out[0], smalls,
        (lru_wa[0], lru_wx[0], lru_ba[0], lru_bx[0], sgu_w[0]))
    wo_ss, wo_rs, ada_ss, ada_rs, wo_land, ada_land, token = _late_gather_start(wo_land, ada_land)
    mods = mods + token[0:1, 0:1]
    conv_bz = conv_b.reshape(NH, HD)
    sgu_bt = sgu_b[0].T
    final_g2 = final_g.reshape(1, D)

    zero_s = jnp.zeros((NH, HD), F32)
    hn, xa_all, ga, u, v, gb, ys = _proj(x[0], ctx[0], mods, norm_g, w_in_full, (sgu_ln_g, sgu_ln_b, sgu_wb, sgu_bt))
    xcz, af, ab, hf, hb, gf, gb_l, _, hb0 = _lru_fwd(xa_all, conv_wz, conv_bz, wcat, bcat, lamcat, "lru_fwd")

    w_out_full = _late_gather_wait(wo_land, wo_ss, wo_rs, "w_out", hf, "late_gather_wait_w_out")
    (loss_part, dfg, dgx, dxn, y, do, dga, dgb, dyl_z, dys) = _out_fwd_bwd(
        hf, hb, ga, gb, ys, x[0], loss_target[0], mods, final_g2, w_out_full)

    dxc_b, dxc_f, dwc, dbc, dlc = _lru_bwd(xcz, dyl_z, hf, hb, af, ab, gf, gb_l, hb0, wcat, lamcat, "lru_bwd")
    dxa, dcw, dcb = _conv_bwd(dxc_b, dxc_f, xa_all, conv_wz, jnp.zeros((4, NH, HD), F32), zero_s, "conv_bwd")
    dxa = _in_hbm(dxa)

    grad_x, dng, dsc_x, dsh_x, dsc_c, dsh_c, du, dv, d_sgu_w, d_sgu_b, d_ln_g, d_ln_b = _proj_bwd(
        dxa, dga, dgb, x[0], ctx[0], dxn, mods, norm_g, w_in_full, (u, v, dys, sgu_ln_g, sgu_ln_b, sgu_wb, sgu_bt))
    dzs = [dxa, dga, du, dv, dgb]

    dmx = jnp.concatenate([dsh_x, dsc_x, dgx], axis=0)
    dmc = jnp.concatenate([dsh_c, dsc_c, jnp.zeros((1, D), F32)], axis=0)
    slot = jnp.concatenate([dmx, loss_part], axis=0)
    slots = lax.dynamic_update_slice(jnp.zeros((32, D), F32), slot, (4 * dev, 0))
    vecs = jnp.concatenate([dfg, dng, dcb.reshape(1, D), d_ln_g, d_ln_b, dcw.reshape(4, D), dmc,
                            jnp.zeros((4, D), F32), slots], axis=0)
    d_sgu_w4 = d_sgu_w.reshape(4, 256, HD).transpose(1, 0, 2).reshape(256, 4 * HD)
    pad8 = lambda a: jnp.pad(a, ((0, 8 - a.shape[0]), (0, 4 * HD - a.shape[1])))
    pack = jnp.concatenate([dwc.reshape(NH * HD, 4 * HD), pad8(dbc), pad8(dlc), d_sgu_w4, pad8(d_sgu_b),
                            vecs.reshape(96, 4 * HD), jnp.zeros((8, 4 * HD), F32)], axis=0)
    g_w_in, g_w_out, tot = _grads_reduce(hn, dzs, lc, y, do, _in_hbm(pack))

    n_w = NH * HD
    g_lru_wa = [(0, n_w, 2 * HD * d, HD, n_w * d) for d in range(2)]
    g_lru_wx = [(0, n_w, 2 * HD * d + HD, HD, n_w * d) for d in range(2)]
    g_lru_ba = [(n_w, NH, 2 * HD * d, HD, NH * d) for d in range(2)]
    g_lru_bx = [(n_w, NH, 2 * HD * d + HD, HD, NH * d) for d in range(2)]
    g_sgu_w = [(1040, 256, HD * q, HD, 256 * q) for q in range(4)]
    g_sgu_b = [(1296, NH, 0, HD, 0)]
    g_lc = tot[1032:1040, 0:2 * HD]
    tv = tot[1304:1400].reshape(48, D)
    g_final_g, g_norm_g, g_conv_b, g_ln_g, g_ln_b = tv[0:1], tv[1:2], tv[2:3], tv[3:4], tv[4:5]
    g_conv_w_full = tv[5:9]
    dmc_tot = tv[9:12].reshape(1, 3 * D)
    slots_all = tv[16:48].reshape(8, 4, D)
    dmx_all = slots_all[:, 0:3, :].reshape(8, 3 * D)
    c_all = c_slots.reshape(8, 8, D)[:, 0, :]
    g_lam_full = jnp.stack([g_lc[:, 0:HD], g_lc[:, HD:2 * HD]]).reshape(2, D)
    g_conv_w = lax.dynamic_slice(g_conv_w_full, (0, 256 * chip), (4, 256))
    g_lam = lax.dynamic_slice(g_lam_full, (0, 256 * chip), (2, 256))
    dmx_all_j = lax.dynamic_slice(dmx_all, (0, 768 * chip), (8, 768))
    dmc_j = lax.dynamic_slice(dmc_tot, (0, 768 * chip), (1, 768))
    ada_full = _late_gather_wait(ada_land, ada_ss, ada_rs, "ada_w", _in_hbm(tot), "late_gather_wait_ada_w")
    g_ada_w, g_ada_b, g_c_ctx = _ada_bwd(c_all, dmx_all_j, dmc_j, dmx_all, dmc_tot, c_ctx2, ada_full)

    big = {
        "ada_w": _adam_big(ada_w[0], g_ada_w, m_ada_w[0], v_ada_w[0], "adam_ada_w"),
        "w_in": _adam_big(w_in[0], g_w_in, m_w_in[0], v_w_in[0], "adam_w_in"),
        "w_out": _adam_big(w_out[0], g_w_out, m_w_out[0], v_w_out[0], "adam_w_out"),
    }
    small_in = {
        "c_ctx": (c_ctx, g_c_ctx, m_c_ctx, v_c_ctx, (1, D)),
        "ada_b": (ada_b, g_ada_b, m_ada_b, v_ada_b, (1, 3 * D)),
        "norm_g": (norm_g, g_norm_g, m_norm_g, v_norm_g, (1, D)),
        "conv_w": (conv_w, g_conv_w, m_conv_w, v_conv_w, (4, 256)),
        "conv_b": (conv_b, g_conv_b, m_conv_b, v_conv_b, (1, D)),
        "lru_wa": (lru_wa, g_lru_wa, m_lru_wa, v_lru_wa, (2 * NH * HD, HD)),
        "lru_ba": (lru_ba, g_lru_ba, m_lru_ba, v_lru_ba, (2 * NH, HD)),
        "lru_wx": (lru_wx, g_lru_wx, m_lru_wx, v_lru_wx, (2 * NH * HD, HD)),
        "lru_bx": (lru_bx, g_lru_bx, m_lru_bx, v_lru_bx, (2 * NH, HD)),
        "lru_lambda": (lru_lambda, g_lam, m_lru_lambda, v_lru_lambda, (2, 256)),
        "sgu_ln_g": (sgu_ln_g, g_ln_g, m_sgu_ln_g, v_sgu_ln_g, (1, D)),
        "sgu_ln_b": (sgu_ln_b, g_ln_b, m_sgu_ln_b, v_sgu_ln_b, (1, D)),
        "sgu_w": (sgu_w, g_sgu_w, m_sgu_w, v_sgu_w, (NH * HD, HD)),
        "sgu_b": (sgu_b, g_sgu_b, m_sgu_b, v_sgu_b, (NH, HD)),
        "final_g": (final_g, g_final_g, m_final_g, v_final_g, (1, D)),
    }
    names_small = list(small_in)
    res_small = _adam_small([tuple(a if isinstance(a, list) else a.reshape(small_in[k][4]) for a in small_in[k][:4])
                             for k in names_small], tot)
    full_shapes = {"ada_w": ada_w.shape, "w_in": w_in.shape, "w_out": w_out.shape}
    grads, deltas, new_m, new_v = {}, {}, {}, {}
    for k in ("ada_w", "w_in", "w_out"):
        g = {"ada_w": g_ada_w, "w_in": g_w_in, "w_out": g_w_out}[k]
        grads[k] = g.reshape(full_shapes[k])
        deltas[k], new_m[k], new_v[k] = (a.reshape(full_shapes[k]) for a in big[k])
    for k, res in zip(names_small, res_small):
        shape = small_in[k][0].shape
        grads[k] = (small_in[k][1] if res[3] is None else res[3]).reshape(shape)
        deltas[k], new_m[k], new_v[k] = (a.reshape(shape) for a in res[:3])

    loss = jnp.sum(slots_all[:, 3, 0:3])
    order = ["c_ctx", "ada_w", "ada_b", "norm_g", "w_in", "conv_w", "conv_b", "lru_wa", "lru_ba", "lru_wx", "lru_bx",
             "lru_lambda", "sgu_ln_g", "sgu_ln_b", "sgu_w", "sgu_b", "w_out", "final_g"]
    return (loss, grad_x.reshape(x.shape), *[grads[k] for k in order], *[deltas[k] for k in order],
            *[new_m[k] for k in order], *[new_v[k] for k in order])
```

```python
import jax
import jax.numpy as jnp
from jax import lax
from jax.experimental import pallas as pl
from jax.experimental.pallas import tpu as pltpu

F32 = jnp.float32
BF16 = jnp.bfloat16

D = 1024
NH = 8
HD = 128
NCHIP = 4
T = 256
NORM_EPS = 1e-6
LN_EPS = 1e-5
LRU_C = 8.0
ADAM_LR = 0.001
ADAM_B1 = 0.9
ADAM_B2 = 0.999
ADAM_EPS = 1e-08
ADAM_WD = 0.01
ADAM_STEP = 10

VMEM = pl.BlockSpec(memory_space=pltpu.VMEM)
ANY = pl.BlockSpec(memory_space=pl.ANY)
MESH = pl.DeviceIdType.MESH


def _cp(n_grid=0, vmem_mb=None):
    kw = {}
    if n_grid:
        kw["dimension_semantics"] = ("arbitrary",) * n_grid
    if vmem_mb:
        kw["vmem_limit_bytes"] = vmem_mb << 20
    return pltpu.CompilerParams(**kw)


def _sigmoid(x):
    return 0.5 * jnp.tanh(0.5 * x) + 0.5


def _silu_and_grad(x):
    s = _sigmoid(x)
    return x * s, s * (1.0 + x * (1.0 - s))


_GELU_K = 0.7978845608028654
_GELU_C = 0.044715


def _gelu_and_grad(x):
    x2 = x * x
    th = jnp.tanh(x * (_GELU_K + (_GELU_K * _GELU_C) * x2))
    p = 0.5 + 0.5 * th
    g = x * p
    dg = p + g * (1.0 - th) * (_GELU_K + (3.0 * _GELU_K * _GELU_C) * x2)
    return g, dg


def _softplus(x):
    return jnp.maximum(x, 0.0) + jnp.log1p(jnp.exp(-jnp.abs(x)))


def _lru_gate(pre, lam_row, d, off=None):
    off = 256 * d if off is None else off
    r = _sigmoid(pre[:, off:off + HD])
    gi = _sigmoid(pre[:, off + HD:off + 2 * HD])
    lam = lam_row[:, HD * d:HD * d + HD]
    sp = _softplus(-lam)
    la = (-LRU_C) * r * sp
    a = jnp.exp(la)
    x2 = 2.0 * la
    m2 = jnp.where(x2 > -1e-3, -x2 * (1.0 + 0.5 * x2), 1.0 - a * a)
    mult = jnp.sqrt(m2)
    return r, gi, lam, sp, a, mult


def _dot(a, b):
    return jnp.dot(a, b, preferred_element_type=F32)


def _dot_tn(a, b):
    return lax.dot_general(a, b, (((0,), (0,)), ((), ())), preferred_element_type=F32)


def _dot_nt(a, b):
    return lax.dot_general(a, b, (((1,), (1,)), ((), ())), preferred_element_type=F32)


def _mo(v, m):
    return v if isinstance(v, int) else pl.multiple_of(v, m)


def _zrows(h, n):
    return pl.ds(h, n, stride=NH)


def _gather_in(c, c_ctx, ada_w, ada_b, w_in, w_out, smalls, lru_sgu):
    nch = [1, 4]
    wrows = lambda cc, q: (pl.ds(_mo(512 * cc, 16), 512) if q is None
                           else pl.ds(_mo(512 * cc + (512 // nch[1]) * q, 16), 512 // nch[1]))
    specs = [
        ((64, 256), F32, lambda r, jj, cc, q=None: r.at[pl.ds(_mo(16 * jj + 8 * cc, 8), 8), :]),
        ((D, 5120), BF16, lambda r, jj, cc, q=None: r.at[wrows(cc, q), pl.ds(_mo(1280 * jj, 128), 1280)]),
    ]
    halves = [lambda r, cc, q=None: r.at[pl.ds(_mo(8 * cc, 8), 8), :],
              lambda r, cc, q=None: r.at[wrows(cc, q), :]]
    na = len(specs)
    sem_base = [0, 6 * nch[0]]
    sidx = lambda a, q, k: sem_base[a] + 6 * q + k
    n_tiny = 6 * sum(nch)
    n_sem = n_tiny + 10

    def body(c_ref, cc_ref, ada_ref, adab_ref, win_ref, wout_ref, sm_ref, wa_ref, wx_ref, ba_ref, bx_ref, sw_ref,
             mods_o, call_o, sm_o, win_o, wol_o, adal_o, wcat_o, bcat_o, swb_o, cwz_o, lam_o,
             s_win, s_ada, s_wout, f_win, f_ada, f_wout, cslot, lhs, mbuf,
             send_sems, recv_sems, local_sems, load_sems, f_w, f_sw, s_wcat, s_swb, prep_sems):
        x, y, c = lax.axis_index("x"), lax.axis_index("y"), lax.axis_index("c")
        j = 2 * x + y
        dev = 2 * j + c
        sib = (x, y, 1 - c)
        chips = [(1 - x, y), (x, 1 - y), (1 - x, 1 - y)]
        cj = [2 * cx + cy for cx, cy in chips]
        outs = [sm_o, win_o]
        srcs = [sm_ref, s_win]

        def copy(idx, src, dst, to):
            return pltpu.make_async_remote_copy(src_ref=src, dst_ref=dst, send_sem=send_sems.at[idx],
                                                recv_sem=recv_sems.at[idx], device_id=to, device_id_type=MESH)

        sends = []

        def start(cp):
            cp.start()
            sends.append(cp)

        cslot[...] = jnp.zeros_like(cslot)
        cslot[0:1, :] = c_ref[...]
        my_slot = pl.ds(_mo(8 * dev, 8), 8)
        others = [sib] + [(*chips[k], c) for k in range(3)] + [(*chips[k], 1 - c) for k in range(3)]
        other_dev = [dev + 1 - 2 * c] + [2 * cj[k] + c for k in range(3)] + [2 * cj[k] + 1 - c for k in range(3)]
        base = n_tiny
        for r in range(7):
            start(copy(base + r, cslot, call_o.at[my_slot, :], others[r]))
        call_o[my_slot, :] = cslot[...]

        crow = 512 // nch[1]
        loads = []
        for cc in (c, 1 - c):
            for q in range(nch[1]):
                rows = pl.ds(_mo(512 * cc + crow * q, 16), crow)
                loads.append(pltpu.make_async_copy(win_ref.at[rows, :], f_win.at[rows, :], load_sems.at[len(loads)]))
        loads.append(pltpu.make_async_copy(ada_ref, f_ada, load_sems.at[len(loads)]))
        loads.append(pltpu.make_async_copy(wout_ref, f_wout, load_sems.at[len(loads)]))
        for ld in loads:
            ld.start()
        prep = []
        for d in range(2):
            for q, (w_src, b_src) in enumerate([(wa_ref, ba_ref), (wx_ref, bx_ref)]):
                prep.append(pltpu.make_async_copy(w_src.at[d], f_w.at[2 * d + q], prep_sems.at[len(prep)]))
                prep.append(pltpu.make_async_copy(b_src.at[d], bcat_o.at[:, pl.ds(HD * (2 * d + q), HD)],
                                                  prep_sems.at[len(prep)]))
        prep.append(pltpu.make_async_copy(sw_ref, f_sw, prep_sems.at[len(prep)]))
        for cp in prep:
            cp.start()
        for k in range(2):
            start(copy(sidx(0, 0, k), halves[0](srcs[0], c), specs[0][2](outs[0], j, c), (*chips[k], c)))
        for q in range(nch[1]):
            loads[q].wait()
            rows = pl.ds(_mo(512 * c + crow * q, 16), crow)
            s_win[rows, :] = f_win[rows, :].astype(BF16)
            for k in range(2):
                start(copy(sidx(1, q, k), halves[1](s_win, c, q), specs[1][2](win_o, j, c, q), (*chips[k], c)))
        for q in range(nch[1]):
            loads[nch[1] + q].wait()
            rows = pl.ds(_mo(512 * (1 - c) + crow * q, 16), crow)
            s_win[rows, :] = f_win[rows, :].astype(BF16)
        local = []
        for a in range(na):
            for cc in range(2):
                lc = pltpu.make_async_copy(halves[a](srcs[a], cc), specs[a][2](outs[a], j, cc), local_sems.at[2 * a + cc])
                lc.start()
                local.append(lc)
        loads[2 * nch[1]].wait()
        s_ada[...] = f_ada[...].astype(BF16)
        loads[2 * nch[1] + 1].wait()
        s_wout[...] = f_wout[...].astype(BF16)
        for q, (src, dst) in enumerate([(s_wout, wol_o.at[pl.ds(_mo(512 * j, 16), 512), :]),
                                        (s_ada, adal_o.at[:, pl.ds(_mo(768 * j, 128), 768)])]):
            lc = pltpu.make_async_copy(src, dst, local_sems.at[2 * na + q])
            lc.start()
            local.append(lc)
        for cp in prep:
            cp.wait()
        for q in range(4):
            s_wcat[:, :, HD * q:HD * q + HD] = f_w[q].astype(BF16)
        s_swb[...] = f_sw[...].astype(BF16)
        for n, (src, dst) in enumerate([(s_wcat, wcat_o), (s_swb, swb_o)]):
            lc = pltpu.make_async_copy(src, dst, prep_sems.at[len(prep) + n])
            lc.start()
            local.append(lc)

        for r in range(7):
            slot = call_o.at[pl.ds(_mo(8 * other_dev[r], 8), 8), :]
            copy(base + r, slot, slot, sib).wait_recv()
        lhs[...] = jnp.zeros_like(lhs)
        for b in range(8):
            cv = call_o[8 * b:8 * b + 1, :]
            lhs[b:b + 1, :] = cv * _sigmoid(cv)
        cv = cc_ref[...]
        lhs[8:9, :] = cv * _sigmoid(cv)
        adab = adab_ref[:, 0:768]
        for jj in range(1, NCHIP):
            adab = jnp.where(j == jj, adab_ref[:, 768 * jj:768 * jj + 768], adab)
        mbuf[j] = _dot(lhs[...].astype(BF16), s_ada[...]) + adab
        for k in range(3):
            start(copy(base + 7 + k, mbuf.at[j], mbuf.at[j], (*chips[k], c)))
        for k in range(3):
            copy(base + 7 + k, mbuf.at[cj[k]], mbuf.at[cj[k]], sib).wait_recv()
        mods_o[...] = jnp.zeros_like(mods_o)
        for jj in range(NCHIP):
            mods_o[0:1, 768 * jj:768 * jj + 768] = mbuf[jj, pl.ds(dev, 1), :]
            mods_o[1:2, 768 * jj:768 * jj + 768] = mbuf[jj, 8:9, :]

        kx = [1 - x, x, 1 - x]
        ky = [y, 1 - y, 1 - y]
        pick = lambda k, lst: jnp.where(k == 0, lst[0], jnp.where(k == 1, lst[1], lst[2]))
        for a in range(na):
            for q in range(nch[a]):
                for step, k in enumerate([c, 1 - c]):
                    reg = specs[a][2](outs[a], pick(k, cj), c, q)
                    copy(sidx(a, q, k), reg, reg, sib).wait_recv()
                    if step == 0:
                        start(copy(sidx(a, q, 2), reg, reg, (pick(1 - c, kx), pick(1 - c, ky), c)))
                    start(copy(sidx(a, q, 3 + k), reg, reg, sib))
        for a in range(na):
            for q in range(nch[a]):
                reg = specs[a][2](outs[a], cj[2], c, q)
                copy(sidx(a, q, 2), reg, reg, sib).wait_recv()
                start(copy(sidx(a, q, 5), reg, reg, sib))
        for a in range(na):
            for q in range(nch[a]):
                for k in range(3):
                    reg = specs[a][2](outs[a], cj[k], 1 - c, q)
                    copy(sidx(a, q, 3 + k), reg, reg, sib).wait_recv()
        for cp in sends:
            cp.wait_send()
        for lc in local:
            lc.wait()
        for jj in range(NCHIP):
            for mh in range(2):
                h = 2 * jj + mh
                cols = slice(HD * mh, HD * mh + HD)
                for r in range(4):
                    cwz_o[r, h:h + 1, :] = sm_o[16 * jj + r:16 * jj + r + 1, cols]
                for d in range(2):
                    lam_o[h:h + 1, HD * d:HD * d + HD] = sm_o[16 * jj + 4 + d:16 * jj + 5 + d, cols]

    out_shape = (jax.ShapeDtypeStruct((8, 3 * D), F32), jax.ShapeDtypeStruct((64, D), F32),
                 jax.ShapeDtypeStruct(specs[0][0], F32), jax.ShapeDtypeStruct(specs[1][0], BF16),
                 jax.ShapeDtypeStruct((2048, D), BF16), jax.ShapeDtypeStruct((D, 3 * D), BF16),
                 jax.ShapeDtypeStruct((NH, HD, 4 * HD), BF16), jax.ShapeDtypeStruct((NH, 4 * HD), F32),
                 jax.ShapeDtypeStruct((NH, HD, HD), BF16), jax.ShapeDtypeStruct((4, NH, HD), F32),
                 jax.ShapeDtypeStruct((NH, 2 * HD), F32))
    return pl.pallas_call(
        body, name="gather_in", out_shape=out_shape,
        in_specs=[VMEM, VMEM, ANY, VMEM, ANY, ANY, VMEM] + [ANY] * 5,
        out_specs=(VMEM, VMEM, VMEM, ANY, ANY, ANY, ANY, ANY, ANY, VMEM, VMEM),
        scratch_shapes=[pltpu.VMEM((D, 1280), BF16), pltpu.VMEM((D, 768), BF16), pltpu.VMEM((512, D), BF16),
                        pltpu.VMEM((D, 1280), F32), pltpu.VMEM((D, 768), F32), pltpu.VMEM((512, D), F32),
                        pltpu.VMEM((8, D), F32), pltpu.VMEM((16, D), F32), pltpu.VMEM((NCHIP, 16, 768), F32),
                        pltpu.SemaphoreType.DMA((n_sem,)), pltpu.SemaphoreType.DMA((n_sem,)),
                        pltpu.SemaphoreType.DMA((2 * na + 2,)), pltpu.SemaphoreType.DMA((2 * nch[1] + 2,)),
                        pltpu.VMEM((4, NH, HD, HD), F32), pltpu.VMEM((NH, HD, HD), F32),
                        pltpu.VMEM((NH, HD, 4 * HD), BF16), pltpu.VMEM((NH, HD, HD), BF16),
                        pltpu.SemaphoreType.DMA((11,))],
        compiler_params=_cp(vmem_mb=56),
    )(c, c_ctx, ada_w, ada_b, w_in, w_out, smalls, *[_in_hbm(a) for a in lru_sgu])


HBM = pl.BlockSpec(memory_space=pltpu.HBM)
SEM = pl.BlockSpec(memory_space=pltpu.SEMAPHORE)


def _in_hbm(a):
    return pltpu.with_memory_space_constraint(a, pltpu.HBM)


def _late_gather_regions(x, y, c):
    chips = [(1 - x, y), (x, 1 - y), (1 - x, 1 - y)]
    wo_reg = lambda r, jj, cc: r.at[pl.ds(_mo(512 * jj + 256 * cc, 16), 256), :]
    ada_reg = lambda r, jj, cc: r.at[pl.ds(_mo(512 * cc, 16), 512), pl.ds(_mo(768 * jj, 128), 768)]
    return chips, wo_reg, ada_reg


def _late_gather_start(wo_land, ada_land):
    def body(wol_ref, adal_ref, wo_ss, wo_rs, ada_ss, ada_rs, wol_thru, adal_thru, token):
        x, y, c = lax.axis_index("x"), lax.axis_index("y"), lax.axis_index("c")
        j = 2 * x + y
        chips, wo_reg, ada_reg = _late_gather_regions(x, y, c)
        for k in range(3):
            for cc in range(2):
                pltpu.make_async_remote_copy(src_ref=wo_reg(wol_ref, j, c), dst_ref=wo_reg(wol_ref, j, c),
                                             send_sem=wo_ss.at[2 * k + cc], recv_sem=wo_rs.at[2 * k + c],
                                             device_id=(*chips[k], cc), device_id_type=MESH).start()
        for k in range(3):
            for cc in range(2):
                pltpu.make_async_remote_copy(src_ref=ada_reg(adal_ref, j, c), dst_ref=ada_reg(adal_ref, j, c),
                                             send_sem=ada_ss.at[2 * k + cc], recv_sem=ada_rs.at[2 * k + c],
                                             device_id=(*chips[k], cc), device_id_type=MESH).start()
        token[...] = jnp.zeros_like(token)

    sems = pltpu.SemaphoreType.DMA((6,))
    return pl.pallas_call(
        body, name="late_gather_start",
        out_shape=(sems, sems, sems, sems, pltpu.HBM(wo_land.shape, BF16), pltpu.HBM(ada_land.shape, BF16),
                   jax.ShapeDtypeStruct((8, 128), F32)),
        in_specs=(HBM, HBM), out_specs=(SEM, SEM, SEM, SEM, HBM, HBM, VMEM), input_output_aliases={0: 4, 1: 5},
        compiler_params=pltpu.CompilerParams(has_side_effects=pltpu.SideEffectType.DATAFLOW_SIDE_EFFECTING),
    )(_in_hbm(wo_land), _in_hbm(ada_land))


def _late_gather_wait(land, send_sems, recv_sems, which, after, name):
    def body(land_ref, ss, rs, after_ref, land_out):
        x, y, c = lax.axis_index("x"), lax.axis_index("y"), lax.axis_index("c")
        j = 2 * x + y
        chips, wo_reg, ada_reg = _late_gather_regions(x, y, c)
        reg = wo_reg if which == "w_out" else ada_reg
        for k in range(3):
            kj = 2 * chips[k][0] + chips[k][1]
            for cc in range(2):
                cp = pltpu.make_async_remote_copy(src_ref=reg(land_ref, j, c), dst_ref=reg(land_ref, kj, cc),
                                                  send_sem=ss.at[2 * k + cc], recv_sem=rs.at[2 * k + cc],
                                                  device_id=(*chips[k], cc), device_id_type=MESH)
                cp.wait_send()
                cp.wait_recv()

    return pl.pallas_call(
        body, name=name, out_shape=pltpu.HBM(land.shape, land.dtype),
        in_specs=(HBM, SEM, SEM, ANY), out_specs=HBM, input_output_aliases={0: 0},
        compiler_params=pltpu.CompilerParams(has_side_effects=pltpu.SideEffectType.DATAFLOW_SIDE_EFFECTING),
    )(land, send_sems, recv_sems, after)


RCHUNK = 16


def _grads_reduce(hn, dzs, lc, y, do, pack):
    rp = pack.shape[0]
    hp = rp // 2
    assert hp % RCHUNK == 0
    wi_w = 1280
    lx = hn.shape[0] - lc
    lt = lx + lc
    n_dz = len(dzs)

    def body(*refs):
        hn_hbm, dz_hbm = refs[0], refs[1:1 + n_dz]
        y_hbm, do_hbm, pk_hbm, wi_out, wo_out, pk_out = refs[1 + n_dz:7 + n_dz]
        (hn_mine, hn_other, dzbuf, wi_other, wi_mine, wi_recv, wi_send, wi_rb,
         y_blk, do_mine, do_other, wo_other, wo_mine, wo_recv, wo_send, wo_rb,
         pk_mine, pk_recv, pk_send, pk_rb, pk_own, send_sems, recv_sems, local_sems) = refs[7 + n_dz:]
        x, y, c = lax.axis_index("x"), lax.axis_index("y"), lax.axis_index("c")
        j = 2 * x + y
        sib = (x, y, 1 - c)
        chips = [(1 - x, y), (x, 1 - y), (1 - x, 1 - y)]
        cj = [2 * cx + cy for cx, cy in chips]
        near = (jnp.where(c == 0, 1 - x, x), jnp.where(c == 0, y, 1 - y), c)
        slabs = [cj[2], cj[0], cj[1], j]

        def copy(k, src, dst, to):
            return pltpu.make_async_remote_copy(src_ref=src, dst_ref=dst, send_sem=send_sems.at[k],
                                                recv_sem=recv_sems.at[k], device_id=to, device_id_type=MESH)

        def local(k, src, dst):
            cp = pltpu.make_async_copy(src, dst, local_sems.at[k])
            cp.start()
            return cp

        rows_half = lambda r, cc, n: r.at[pl.ds(_mo(cc * n, 16), n), :]
        cols_half = lambda r, cc, n: r.at[:, pl.ds(_mo(cc * n, 128), n)]
        pk_piece = lambda r, cc, jj: r.at[pl.ds(_mo(cc * hp, 16), hp), pl.ds(_mo(jj * 128, 128), 128)]

        sends = []

        def start(cp):
            cp.start()
            sends.append(cp)

        def dz_pieces(s):
            g0 = wi_w * s
            k0, off0 = g0 // D, g0 % D
            w0 = min(D - off0, wi_w)
            pieces = [(k0, off0, w0, 0)]
            if w0 < wi_w:
                pieces.append((k0 + 1, 0, wi_w - w0, w0))
            return pieces

        def dz_copies(s):
            cps = []
            for q, (k, off, w, dst) in enumerate(dz_pieces(s)):
                cps.append(pltpu.make_async_copy(dz_hbm[k].at[pl.ds(lc if k == 0 else 0, lx), pl.ds(off, w)],
                                                 dzbuf.at[pl.ds(lc, lx), pl.ds(dst, w)], local_sems.at[11 + q]))
            if s == 0:
                cps.append(pltpu.make_async_copy(dz_hbm[0].at[pl.ds(0, lc), :], dzbuf.at[pl.ds(0, lc), pl.ds(0, D)],
                                                 local_sems.at[13]))
            return cps

        def dz_load(sl):
            for s in range(NCHIP):
                @pl.when(sl == s)
                def _():
                    if s == 0:
                        dzbuf[pl.ds(0, lc), pl.ds(D, wi_w - D)] = jnp.zeros((lc, wi_w - D), BF16)
                    else:
                        dzbuf[pl.ds(0, lc), :] = jnp.zeros((lc, wi_w), BF16)
                    for cp in dz_copies(s):
                        cp.start()

        def dz_wait(sl):
            for s in range(NCHIP):
                @pl.when(sl == s)
                def _():
                    for cp in dz_copies(s):
                        cp.wait()

        l_pk = local(0, rows_half(pk_hbm, c, hp), pk_mine)
        start(copy(0, rows_half(pk_hbm, 1 - c, hp), pk_recv, sib))
        col = lambda r, cc: r.at[:, pl.ds(_mo(cc * 512, 128), 512)]
        do_loads = [local(7, col(do_hbm, c), do_mine), local(14, col(do_hbm, 1 - c), do_other)]
        hn_loads = [local(2, col(hn_hbm, c), hn_mine), local(4, col(hn_hbm, 1 - c), hn_other)]
        y_copy = lambda s: pltpu.make_async_copy(col(y_hbm, slabs[s]), y_blk, local_sems.at[1])
        y_copy(0).start()
        dz_load(slabs[0])

        def pair_sum(mine, recv, send, nrows, keep, relayed=None):
            def step(i, carry):
                rows = pl.ds(_mo(i * RCHUNK, RCHUNK), RCHUNK)
                s = mine[rows, :] + recv[rows, :].astype(F32)
                if relayed is not None:
                    s = s + relayed[rows, :].astype(F32)
                if keep:
                    mine[rows, :] = s
                if send is not None:
                    send[rows, :] = s.astype(BF16)
                return carry
            lax.fori_loop(0, nrows // RCHUNK, step, 0)

        def chip_sum(own, rb, nrows, terms=(0, 1, 2)):
            def step(i, carry):
                rows = pl.ds(_mo(i * RCHUNK, RCHUNK), RCHUNK)
                acc = own[rows, :]
                for q in terms:
                    acc = acc + rb[q, rows, :].astype(F32)
                own[rows, :] = acc
                return carry
            lax.fori_loop(0, nrows // RCHUNK, step, 0)

        w_in_g = dict(other=wi_other, mine=wi_mine, recv=wi_recv, send=wi_send, rb=wi_rb, p1_sems=(2, 3, 4, 5), p2_sem=12,
                      p1=[None] * NCHIP, wait_load=lambda s: dz_wait(slabs[s]), load=lambda s: dz_load(slabs[s]),
                      dot_other=lambda: _dot_tn(hn_other[...], dzbuf[...]), dot_mine=lambda: _dot_tn(hn_mine[...], dzbuf[...]))
        w_out_g = dict(other=wo_other, mine=wo_mine, recv=wo_recv, send=wo_send, rb=wo_rb, p1_sems=(1, 24, 25, 26), p2_sem=9,
                       p1=[None] * NCHIP, wait_load=lambda s: y_copy(s).wait(), load=lambda s: y_copy(s).start(),
                       dot_other=lambda: _dot_tn(y_blk[...], do_other[...]), dot_mine=lambda: _dot_tn(y_blk[...], do_mine[...]))

        def piece_matmuls(g, s):
            if s >= 2:
                g["p1"][s - 2].wait_send()
            g["wait_load"](s)
            g["other"][s % 2] = g["dot_other"]().astype(BF16)
            g["p1"][s] = copy(g["p1_sems"][s], g["other"].at[s % 2], g["recv"].at[s], sib)
            g["p1"][s].start()
            g["mine"][s % 2] = g["dot_mine"]()
            if s + 1 < NCHIP:
                g["load"](s + 1)

        def piece_finish(g, s):
            mine, recv, send, rb, p2 = g["mine"].at[s % 2], g["recv"].at[s], g["send"], g["rb"], g["p2_sem"]
            nrows = mine.shape[0]
            copy(g["p1_sems"][s], recv, recv, sib).wait_recv()
            if s == 3:
                pair_sum(mine, recv, None, nrows, True)
                return
            if s == 0:
                pair_sum(mine, recv, send.at[0], nrows, False)
                start(copy(p2, send.at[0], rb.at[0], near))
                return
            adds_relayed = c == (1 if s == 1 else 0)

            @pl.when(adds_relayed)
            def _():
                copy(p2, rb.at[0], rb.at[0], sib).wait_recv()
                pair_sum(mine, recv, send.at[s], nrows, False, rb.at[0])

            @pl.when(jnp.logical_not(adds_relayed))
            def _():
                pair_sum(mine, recv, send.at[s], nrows, False)
            start(copy(p2 + s, send.at[s], rb.at[s], (*chips[s - 1], c)))

        def piece_total(g):
            for k in (1, 2):
                copy(g["p2_sem"] + k, g["rb"].at[k], g["rb"].at[k], sib).wait_recv()
            chip_sum(g["mine"].at[1], g["rb"], g["mine"].shape[1], (1, 2))

        for cp in do_loads:
            cp.wait()
        piece_matmuls(w_out_g, 0)
        piece_matmuls(w_out_g, 1)
        piece_finish(w_out_g, 0)
        piece_matmuls(w_out_g, 2)
        piece_finish(w_out_g, 1)
        piece_matmuls(w_out_g, 3)
        piece_finish(w_out_g, 2)

        for cp in hn_loads:
            cp.wait()
        piece_matmuls(w_in_g, 0)

        l_pk.wait()
        copy(0, pk_recv, pk_recv, sib).wait_recv()
        pair_sum(pk_mine, pk_recv, pk_send, hp, True)
        for k in range(3):
            start(copy(6 + k, pk_send.at[:, pl.ds(_mo(cj[k] * 128, 128), 128)], pk_rb.at[k], (*chips[k], c)))
        l_pk_own = local(6, pk_mine.at[:, pl.ds(_mo(j * 128, 128), 128)], pk_own)

        piece_matmuls(w_in_g, 1)
        piece_finish(w_in_g, 0)

        l_pk_own.wait()
        for k in range(3):
            copy(6 + k, pk_rb.at[k], pk_rb.at[k], sib).wait_recv()
        chip_sum(pk_own, pk_rb, hp)
        l_pk_out = local(8, pk_own, pk_piece(pk_out, c, j))
        start(copy(15, pk_own, pk_piece(pk_out, c, j), sib))
        for k in range(2):
            start(copy(16 + k, pk_own, pk_piece(pk_out, c, j), (*chips[k], c)))

        piece_matmuls(w_in_g, 2)
        piece_finish(w_in_g, 1)
        piece_matmuls(w_in_g, 3)
        piece_finish(w_in_g, 2)

        piece_finish(w_out_g, 3)
        piece_total(w_out_g)
        l_wo_out = local(9, wo_mine.at[1], cols_half(wo_out, c, 512))
        start(copy(22, wo_mine.at[1], cols_half(wo_out, c, 512), sib))

        far = (jnp.where(c == 0, x, 1 - x), jnp.where(c == 0, 1 - y, y), c)
        for step, k in enumerate([c, 1 - c, 2]):
            reg = pk_piece(pk_out, c, jnp.where(k == 0, cj[0], jnp.where(k == 1, cj[1], cj[2])))
            copy(16 + k, reg, reg, sib).wait_recv()
            if step == 0:
                start(copy(18, reg, reg, far))
            start(copy(19 + k, reg, reg, sib))

        piece_finish(w_in_g, 3)
        piece_total(w_in_g)
        l_wi_out = local(10, wi_mine.at[1], rows_half(wi_out, c, 512))
        start(copy(23, wi_mine.at[1], rows_half(wi_out, c, 512), sib))

        reg = pk_piece(pk_out, 1 - c, j)
        copy(15, reg, reg, sib).wait_recv()
        for k in range(3):
            reg = pk_piece(pk_out, 1 - c, cj[k])
            copy(19 + k, reg, reg, sib).wait_recv()
        reg = cols_half(wo_out, 1 - c, 512)
        copy(22, reg, reg, sib).wait_recv()
        reg = rows_half(wi_out, 1 - c, 512)
        copy(23, reg, reg, sib).wait_recv()
        for cp in sends + w_in_g["p1"][2:] + w_out_g["p1"][2:]:
            cp.wait_send()
        for cp in (l_pk_out, l_wo_out, l_wi_out):
            cp.wait()

    return pl.pallas_call(
        body, name="grads_reduce",
        out_shape=(jax.ShapeDtypeStruct((D, wi_w), F32), jax.ShapeDtypeStruct((512, D), F32),
                   jax.ShapeDtypeStruct(pack.shape, F32)),
        in_specs=[ANY] * (4 + n_dz), out_specs=(ANY,) * 3,
        scratch_shapes=[
            pltpu.VMEM((lt, 512), BF16), pltpu.VMEM((lt, 512), BF16), pltpu.VMEM((lt, wi_w), BF16),
            pltpu.VMEM((2, 512, wi_w), BF16), pltpu.VMEM((2, 512, wi_w), F32), pltpu.VMEM((4, 512, wi_w), BF16),
            pltpu.VMEM((3, 512, wi_w), BF16), pltpu.VMEM((3, 512, wi_w), BF16),
            pltpu.VMEM((lx, 512), BF16), pltpu.VMEM((lx, 512), BF16), pltpu.VMEM((lx, 512), BF16),
            pltpu.VMEM((2, 512, 512), BF16), pltpu.VMEM((2, 512, 512), F32), pltpu.VMEM((4, 512, 512), BF16),
            pltpu.VMEM((3, 512, 512), BF16), pltpu.VMEM((3, 512, 512), BF16),
            pltpu.VMEM((hp, 512), F32), pltpu.VMEM((hp, 512), F32), pltpu.VMEM((hp, 512), BF16),
            pltpu.VMEM((3, hp, 128), BF16), pltpu.VMEM((hp, 128), F32),
            pltpu.SemaphoreType.DMA((27,)), pltpu.SemaphoreType.DMA((27,)), pltpu.SemaphoreType.DMA((15,))],
        compiler_params=_cp(vmem_mb=56),
    )(hn, *dzs, y, do, pack)


def _ada_bwd(c_all, dmx_all_j, dmc_j, dmx_all, dmc, c_ctx, ada_w_full):
    def body(c_ref, dmxj_ref, dmcj_ref, dmx_ref, dmc_ref, cc_ref, w_hbm, gw_ref, gb_ref, gc_ref, lhs, rhs, dm8,
             w_v, w_sem):
        w_loads = [pltpu.make_async_copy(w_hbm.at[:, pl.ds(D * k, D)], w_v.at[k], w_sem.at[k]) for k in range(3)]
        for cp in w_loads:
            cp.start()
        lhs[...] = jnp.zeros_like(lhs)
        rhs[...] = jnp.zeros_like(rhs)
        cv = c_ref[...]
        lhs[0:8, :] = cv * _sigmoid(cv)
        cc = cc_ref[...]
        a_c, da_c = _silu_and_grad(cc)
        lhs[8:9, :] = a_c
        rhs[0:8, :] = dmxj_ref[...]
        rhs[8:9, :] = dmcj_ref[...]
        gw_ref[...] = _dot_tn(lhs[...].astype(BF16), rhs[...].astype(BF16))
        gb_ref[...] = jnp.sum(dmx_ref[...], axis=0, keepdims=True) + dmc_ref[...]
        dm8[...] = jnp.zeros_like(dm8)
        dm8[0:1, :] = dmc_ref[...]
        da = jnp.zeros((8, D), F32)
        for k in range(3):
            w_loads[k].wait()
            da = da + _dot_nt(dm8[:, D * k:D * k + D].astype(BF16), w_v[k])
        gc_ref[...] = da[0:1, :] * da_c

    return pl.pallas_call(
        body, name="ada_bwd",
        out_shape=(jax.ShapeDtypeStruct((D, 768), F32), jax.ShapeDtypeStruct((1, 3 * D), F32),
                   jax.ShapeDtypeStruct((1, D), F32)),
        in_specs=[VMEM] * 6 + [ANY], out_specs=(VMEM,) * 3,
        scratch_shapes=[pltpu.VMEM((16, D), F32), pltpu.VMEM((16, 768), F32), pltpu.VMEM((8, 3 * D), F32),
                        pltpu.VMEM((3, D, D), BF16), pltpu.SemaphoreType.DMA((3,))],
        compiler_params=_cp(vmem_mb=32),
    )(c_all, dmx_all_j, dmc_j, dmx_all, dmc, c_ctx, _in_hbm(ada_w_full))


def _proj(x, ctx, mods, norm_g, w_full, sgu, after):
    lx, lc = x.shape[0], ctx.shape[0]
    assert lc == T
    n = 1 + lx // T

    def body(x_ref, c_ref, sh_ref, sc_ref, ng_ref, w0, w1, w2, w3, w4, g_ref, b_ref, sw_ref, bt_ref, after_ref,
             hn_ref, xa_ref, ga_ref, u_ref, v_ref, gb_ref, ys_ref, mixed_s):
        i = pl.program_id(0)
        is_ctx = i == 0
        xv = jnp.where(is_ctx, c_ref[...], x_ref[...])
        sc = jnp.where(is_ctx, sc_ref[1:2, :], sc_ref[0:1, :])
        sh = jnp.where(is_ctx, sh_ref[1:2, :], sh_ref[0:1, :])
        r = lax.rsqrt(jnp.mean(xv * xv, axis=-1, keepdims=True) + NORM_EPS)
        hb = ((xv * r) * ng_ref[...] * (1.0 + sc) + sh).astype(BF16)
        hn_ref[...] = hb
        xa_ref[...] = _dot(hb, w0[...])

        @pl.when(i > 0)
        def _():
            u_ref[...] = _dot(hb, w2[...])
            v_ref[...] = _dot(hb, w3[...])
            for ch in range(T // HD):
                rows = slice(HD * ch, HD * ch + HD)
                ug = _sgu_parts(u_ref[rows, :], v_ref[rows, :], g_ref[...], b_ref[...], sw_ref, bt_ref, mixed_s)[0]
                ys_ref[rows, :] = ug * mixed_s[...]
            ga_ref[...] = _dot(hb, w1[...])
            gb_ref[...] = _dot(hb, w4[...])

    every = pl.BlockSpec((T, D), lambda i: (i, 0))
    lat = pl.BlockSpec((T, D), lambda i: (jnp.maximum(i - 1, 0), 0))
    vec = pl.BlockSpec((1, D), lambda i: (0, 0))
    in_specs = [lat, pl.BlockSpec((T, D), lambda i: (0, 0)), pl.BlockSpec((8, D), lambda i: (0, 0)),
                pl.BlockSpec((8, D), lambda i: (0, 1)), vec]
    in_specs += [pl.BlockSpec((D, D), lambda i, k=k: (0, k)) for k in range(5)]
    in_specs += [vec, vec, pl.BlockSpec((NH, HD, HD), lambda i: (0, 0, 0)), pl.BlockSpec((HD, NH), lambda i: (0, 0))]
    in_specs += [ANY]
    full_s = jax.ShapeDtypeStruct((lc + lx, D), F32)
    lat_s = jax.ShapeDtypeStruct((lx, D), F32)
    return pl.pallas_call(
        body, name="proj", grid=(n,),
        out_shape=(jax.ShapeDtypeStruct((lc + lx, D), BF16), full_s, lat_s, lat_s, lat_s, lat_s, lat_s),
        in_specs=in_specs, out_specs=(every, every, lat, lat, lat, lat, lat),
        scratch_shapes=[pltpu.VMEM((HD, D), F32)], compiler_params=_cp(1, vmem_mb=56),
    )(x, ctx, mods, mods, norm_g, *([w_full] * 5), *sgu, after)


def _tile_specs(rows_per_pos, width, n_tiles, tile):
    last = n_tiles * (T // 8) - 1
    r = rows_per_pos
    return [pl.BlockSpec((T * r, width), lambda i: (tile(i), 0)),
            pl.BlockSpec((8 * r, width), lambda i: (jnp.maximum(tile(i) * (T // 8) - 1, 0), 0)),
            pl.BlockSpec((8 * r, width), lambda i: (jnp.minimum((tile(i) + 1) * (T // 8), last), 0))]


def _has_prev(tile):
    return tile >= 2


def _has_next(tile, nt):
    return jnp.logical_and(tile >= 1, tile < nt - 1)


ZT = pl.BlockSpec((T * NH, HD), lambda i: (i, 0))
CONV_CHUNK = 32


SCAN_SUB = 8


def _scan_tile(chains, post, carry_ref):
    blk = T // SCAN_SUB

    def step(k, state):
        new = []
        for ci, (a_ref, x_ref, o_ref, q_ref, reverse, xscale) in enumerate(chains):
            for q in range(SCAN_SUB):
                s, p = state[ci * SCAN_SUB + q]
                t = (q + 1) * blk - 1 - k if reverse else q * blk + k
                r = pl.ds(_mo(t * NH, NH), NH)
                a = a_ref[r, :]
                x = x_ref[r, :] if xscale is None else x_ref[r, :] * xscale
                if post:
                    o = x + s
                    o_ref[r, :] = o
                    q_ref[r, :] = p
                    new.append((a * o, a * p))
                else:
                    o = a * s + x
                    p = a * p
                    o_ref[r, :] = o
                    q_ref[r, :] = p
                    new.append((o, p))
        return tuple(new)

    zero = jnp.zeros((NH, HD), F32)
    one = jnp.ones((NH, HD), F32)
    final = lax.fori_loop(0, blk, step, tuple((zero, one) for _ in range(len(chains) * SCAN_SUB)))
    for ci, (a_ref, x_ref, o_ref, q_ref, reverse, xscale) in enumerate(chains):
        carry = carry_ref[ci]
        for q in (range(SCAN_SUB - 1, -1, -1) if reverse else range(SCAN_SUB)):
            rows = pl.ds(q * blk * NH, blk * NH)
            fixed = o_ref[rows, :].reshape(blk, NH, HD) + q_ref[rows, :].reshape(blk, NH, HD) * carry[None]
            o_ref[rows, :] = fixed.reshape(blk * NH, HD)
            s_loc, p_loc = final[ci * SCAN_SUB + q]
            carry = s_loc + p_loc * carry
        carry_ref[ci] = carry


def _lru_fwd(xa, conv_wz, conv_bz, wcat, bcat, lamcat, name):
    lx = xa.shape[0]
    n = lx // T
    tile_u = lambda i: i
    tile_d = lambda i: jnp.where(i == 0, 0, n - i)

    def body(xm_u, xp_u, xn_u, xm_d, xp_d, xn_d, cw, cb, w_ref, b_ref, lam_ref,
             xcz_o, af_o, ab_o, hf_o, hb_o, gf_o, gb_o, fu, fd, pad, xc_d, x_u, x_d, q_u, q_d, carry):
        i = pl.program_id(0)

        @pl.when(i == 0)
        def _():
            carry[...] = jnp.zeros_like(carry)

        def conv_gates(xm, xp, xn, tile, d, xc_ref, a_ref, x_ref, g_ref):
            pmask = jnp.where(_has_prev(tile), 1.0, 0.0)
            nmask = jnp.where(_has_next(tile, n), 1.0, 0.0)
            for h in range(NH):
                cols = slice(HD * h, HD * h + HD)
                pad[_zrows(h, 8), :] = xp[:, cols] * pmask
                pad[pl.ds(8 * NH + h, T, stride=NH), :] = xm[:, cols]
                pad[pl.ds((T + 8) * NH + h, 8, stride=NH), :] = xn[:, cols] * nmask

            def conv_chunk(ci, c_):
                base = pl.multiple_of(ci * (CONV_CHUNK * NH), CONV_CHUNK * NH)
                acc = None
                for k in range(4):
                    sl = pad[pl.ds(base + (7 + k) * NH, CONV_CHUNK * NH), :].reshape(CONV_CHUNK, NH, HD)
                    term = sl * cw[k][None]
                    acc = term if acc is None else acc + term
                acc = acc + cb[...][None]
                xc_ref[pl.ds(base, CONV_CHUNK * NH), :] = acc.reshape(CONV_CHUNK * NH, HD)
                return c_
            lax.fori_loop(0, T // CONV_CHUNK, conv_chunk, 0)

            for h in range(NH):
                xch = xc_ref[_zrows(h, T), :]
                pre = _dot(xch.astype(BF16), w_ref[h, :, 256 * d:256 * d + 256]) + b_ref[h:h + 1, 256 * d:256 * d + 256]
                r, gi, _, _, a, mult = _lru_gate(pre, lam_ref[h:h + 1, :], d, 0)
                a_ref[_zrows(h, T), :] = a
                x_ref[_zrows(h, T), :] = mult * gi * xch
                for q, val in enumerate((r, gi, mult)):
                    g_ref[:, q * D + HD * h:q * D + HD * h + HD] = val

        conv_gates(xm_u, xp_u, xn_u, tile_u(i), 0, xcz_o, af_o, x_u, gf_o)
        conv_gates(xm_d, xp_d, xn_d, tile_d(i), 1, xc_d, ab_o, x_d, gb_o)

        _scan_tile([(af_o, x_u, hf_o, q_u, False, None), (ab_o, x_d, hb_o, q_d, True, None)], False, carry)

        @pl.when(i == 0)
        def _():
            fu[...] = carry[0]
            fd[...] = carry[1]

    full = lambda shape: pl.BlockSpec(shape, lambda i: (0,) * len(shape))
    st = full((NH, HD))
    in_specs = _tile_specs(1, D, n, tile_u) + _tile_specs(1, D, n, tile_d)
    in_specs += [full((4, NH, HD)), st, full((NH, HD, 4 * HD)), full((NH, 4 * HD)), full((NH, 2 * HD))]
    up = pl.BlockSpec((T * NH, HD), lambda i: (tile_u(i), 0))
    dn = pl.BlockSpec((T * NH, HD), lambda i: (tile_d(i), 0))
    zs = jax.ShapeDtypeStruct((lx * NH, HD), F32)
    ss = jax.ShapeDtypeStruct((NH, HD), F32)
    zbuf = pltpu.VMEM((T * NH, HD), F32)
    gs = jax.ShapeDtypeStruct((lx, 3 * D), F32)
    g_up = pl.BlockSpec((T, 3 * D), lambda i: (tile_u(i), 0))
    g_dn = pl.BlockSpec((T, 3 * D), lambda i: (tile_d(i), 0))
    return pl.pallas_call(
        body, name=name, grid=(n,), out_shape=(zs,) * 5 + (gs, gs, ss, ss), in_specs=in_specs,
        out_specs=(up, up, dn, up, dn, g_up, g_dn, st, st),
        scratch_shapes=[pltpu.VMEM(((T + 16) * NH, HD), F32), zbuf, zbuf, zbuf, zbuf, zbuf,
                        pltpu.VMEM((2, NH, HD), F32)],
        compiler_params=_cp(1, vmem_mb=48),
    )(xa, xa, xa, xa, xa, xa, conv_wz, conv_bz, wcat, bcat, lamcat)


def _sgu_parts(u, v, lng, lnb, w_ref, bt_ref, mixed_s):
    ug, dug = _gelu_and_grad(u)
    vg, dvg = _gelu_and_grad(v)
    mu = jnp.mean(vg, axis=-1, keepdims=True)
    vc = vg - mu
    rstd = lax.rsqrt(jnp.mean(vc * vc, axis=-1, keepdims=True) + LN_EPS)
    vh = vc * rstd
    vn = (vh * lng + lnb).astype(BF16)
    for g in range(NH):
        cols = slice(HD * g, HD * g + HD)
        mixed_s[:, cols] = _dot(w_ref[g], vn[:, cols]) + bt_ref[:, g:g + 1]
    return ug, dug, dvg, rstd, vh, vn


def _sgu_bwd_chunk(u, v, dys_v, lng, lnb, w_ref, bt_ref, mixed_s, dvn_s, dw_ref, db_ref, dg_ref, dbl_ref):
    ug, dug, dvg, rstd, vh, vn = _sgu_parts(u, v, lng, lnb, w_ref, bt_ref, mixed_s)
    du = (dys_v * mixed_s[...] * dug).astype(BF16)
    dmix = dys_v * ug
    ones = jnp.ones((8, HD), BF16)
    for g in range(NH):
        cols = slice(HD * g, HD * g + HD)
        dm = dmix[:, cols]
        hi = dm.astype(BF16)
        lo = (dm - hi.astype(F32)).astype(BF16)
        dw_ref[g] += _dot_nt(hi, vn[:, cols])
        db_ref[g:g + 1, :] += (_dot_nt(ones, hi) + _dot_nt(ones, lo))[0:1, :]
        dvn_s[:, cols] = _dot_tn(w_ref[g], hi)
    dvn = dvn_s[...]
    dg_ref[...] += jnp.sum(dvn * vh, axis=0, keepdims=True)
    dbl_ref[...] += jnp.sum(dvn, axis=0, keepdims=True)
    dvh = dvn * lng
    dvg_in = rstd * (dvh - jnp.mean(dvh, axis=-1, keepdims=True) - vh * jnp.mean(dvh * vh, axis=-1, keepdims=True))
    return du, (dvg_in * dvg).astype(BF16)


def _out_fwd_bwd(hf_z, hb_z, ga, gb, ys, x, tgt, mods, final_g, w_out_full):
    lx = x.shape[0]
    n = lx // T

    def body(hf_ref, hb_ref, ga_ref, gb_ref, ys_ref, x_ref, t_ref, gx_ref, fg_ref, w_ref,
             loss_ref, dfg_ref, dgx_ref, dxn_ref, y_ref, do_ref, dga_ref, dgb_ref, dyl_ref, dys_ref, yl_s):
        i = pl.program_id(0)

        @pl.when(i == 0)
        def _():
            loss_ref[...] = jnp.zeros_like(loss_ref)
            dfg_ref[...] = jnp.zeros_like(dfg_ref)
            dgx_ref[...] = jnp.zeros_like(dgx_ref)

        for h in range(NH):
            yl_s[:, HD * h:HD * h + HD] = hf_ref[_zrows(h, T), :] + hb_ref[_zrows(h, T), :]
        yl = yl_s[...]
        gav = ga_ref[...]
        gbv = gb_ref[...]
        sa, dsa = _silu_and_grad(gav)
        sb, dsb = _silu_and_grad(gbv)
        ysv = ys_ref[...]
        y_ref[:, 0:D] = (yl * sa).astype(BF16)
        y_ref[:, D:2 * D] = (ysv * sb).astype(BF16)
        o = _dot(y_ref[...], w_ref[...])
        gx = gx_ref[0:1, :]
        xnew = x_ref[...] + gx * o
        r2 = lax.rsqrt(jnp.mean(xnew * xnew, axis=-1, keepdims=True) + NORM_EPS)
        xh = xnew * r2
        fg = fg_ref[...]
        err = xh * fg - t_ref[...]
        loss_ref[...] += 0.5 * jnp.sum(jnp.mean(err * err, axis=-1, keepdims=True), axis=0, keepdims=True)

        @pl.when(i == n - 1)
        def _():
            lp = loss_ref[...]
            lp1 = lp.astype(BF16).astype(F32)
            lp2 = (lp - lp1).astype(BF16).astype(F32)
            lp3 = (lp - lp1 - lp2).astype(BF16).astype(F32)
            lane = lax.broadcasted_iota(jnp.int32, lp.shape, 1)
            loss_ref[...] = jnp.where(lane == 0, lp1, jnp.where(lane == 1, lp2, jnp.where(lane == 2, lp3, 0.0)))
        dout = err * (1.0 / D)
        dfg_ref[...] += jnp.sum(dout * xh, axis=0, keepdims=True)
        dxh = dout * fg
        dxn = r2 * (dxh - xh * jnp.mean(dxh * xh, axis=-1, keepdims=True))
        dxn_ref[...] = dxn
        dgx_ref[...] += jnp.sum(dxn * o, axis=0, keepdims=True)
        do = (dxn * gx).astype(BF16)
        do_ref[...] = do
        dy = _dot_nt(do, w_ref[...])
        dy1 = dy[:, 0:D]
        dy2 = dy[:, D:2 * D]
        dga_ref[...] = (dy1 * yl * dsa).astype(BF16)
        dgb_ref[...] = (dy2 * ysv * dsb).astype(BF16)
        dys_ref[...] = dy2 * sb
        yl_s[...] = dy1 * sa
        for h in range(NH):
            dyl_ref[_zrows(h, T), :] = yl_s[:, HD * h:HD * h + HD]

    row = pl.BlockSpec((T, D), lambda i: (i, 0))
    vec = pl.BlockSpec((1, D), lambda i: (0, 0))
    zlat = pl.BlockSpec((T * NH, HD), lambda i: (i + 1, 0))
    in_specs = [zlat, zlat, row, row, row, row, row, pl.BlockSpec((8, D), lambda i: (0, 2)), vec,
                pl.BlockSpec((2 * D, D), lambda i: (0, 0))]
    out_shape = (jax.ShapeDtypeStruct((1, D), F32), jax.ShapeDtypeStruct((1, D), F32), jax.ShapeDtypeStruct((1, D), F32),
                 jax.ShapeDtypeStruct((lx, D), F32), jax.ShapeDtypeStruct((lx, 2 * D), BF16),
                 jax.ShapeDtypeStruct((lx, D), BF16), jax.ShapeDtypeStruct((lx, D), BF16),
                 jax.ShapeDtypeStruct((lx, D), BF16), jax.ShapeDtypeStruct((lx * NH, HD), F32),
                 jax.ShapeDtypeStruct((lx, D), F32))
    out_specs = (vec, vec, vec, row, pl.BlockSpec((T, 2 * D), lambda i: (i, 0)),
                 row, row, row, ZT, row)
    return pl.pallas_call(
        body, name="out_fwd_bwd", grid=(n,), out_shape=out_shape, in_specs=in_specs, out_specs=out_specs,
        scratch_shapes=[pltpu.VMEM((T, D), F32)],
        compiler_params=_cp(1, vmem_mb=56),
    )(hf_z, hb_z, ga, gb, ys, x, tgt, mods, final_g, w_out_full)


def _lru_bwd(xc_z, dy_z, hf_z, hb_z, af_z, ab_z, gf, gb, s_b, wcat, lamcat, name):
    lx = xc_z.shape[0] // NH
    n = lx // T
    tile_u = lambda i: jnp.where(i == n - 1, 0, i + 1)
    tile_d = lambda i: n - 1 - i

    def body(xc_u, dy_u, hb_ref, hbn_ref, ab_ref, gb_ref, xc_d, dy_d, hf_ref, hfp_ref, af_ref, gf_ref,
             sb_ref, w_ref, lam_ref, dxcb_ref, dxcf_ref, dw_ref, db_ref, dl_ref,
             lb_s, lf_s, q_u, q_d, pf_s, pb_s, dpre_s, carry):
        i = pl.program_id(0)
        tu, td = tile_u(i), tile_d(i)

        @pl.when(i == 0)
        def _():
            dw_ref[...] = jnp.zeros_like(dw_ref)
            db_ref[...] = jnp.zeros_like(db_ref)
            dl_ref[...] = jnp.zeros_like(dl_ref)
            carry[...] = jnp.zeros_like(carry)

        _scan_tile([(ab_ref, dy_u, lb_s, q_u, False, jnp.where(tu == 0, 0.0, 1.0)),
                    (af_ref, dy_d, lf_s, q_d, True, jnp.where(td == 0, 0.0, 1.0))], True, carry)
        zero = jnp.zeros((NH, HD), F32)
        pb_s[pl.ds(0, T * NH), :] = hb_ref[...]
        pb_s[pl.ds(T * NH, NH), :] = jnp.where(tu == n - 1, sb_ref[...], jnp.where(tu == 0, zero, hbn_ref[pl.ds(0, NH), :]))
        pf_s[pl.ds(0, NH), :] = jnp.where(td == 0, zero, hfp_ref[pl.ds(7 * NH, NH), :])
        pf_s[pl.ds(NH, T * NH), :] = hf_ref[...]
        sides = ((1, xc_u, lb_s, pb_s, NH, ab_ref, gb_ref, dxcb_ref), (0, xc_d, lf_s, pf_s, 0, af_ref, gf_ref, dxcf_ref))
        for d, xc_ref, adj_s, prev_s, prev_off, a_ref, g_ref, dxc_ref in sides:
            wcols = slice(256 * d, 256 * d + 256)
            for h in range(NH):
                xch = xc_ref[_zrows(h, T), :]
                xcb = xch.astype(BF16)
                r, gi, mult = (g_ref[:, q * D + HD * h:q * D + HD * h + HD] for q in range(3))
                a = a_ref[_zrows(h, T), :]
                lam = lam_ref[h:h + 1, HD * d:HD * d + HD]
                sp = _softplus(-lam)
                du = adj_s[_zrows(h, T), :]
                da = du * prev_s[pl.ds(prev_off + h, T, stride=NH), :]
                dgi = du * mult * xch
                dmult = du * gi * xch
                dla = da * a - dmult * (a * a) / mult
                dr = dla * ((-LRU_C) * sp)
                dsp = jnp.sum(dla * ((-LRU_C) * r), axis=0, keepdims=True)
                dl_ref[h:h + 1, HD * d:HD * d + HD] += dsp * (-_sigmoid(-lam))
                dpre_s[:, 0:HD] = dr * r * (1.0 - r)
                dpre_s[:, HD:2 * HD] = dgi * gi * (1.0 - gi)
                dpre = dpre_s[...]
                dpb = dpre.astype(BF16)
                dw_ref[h, :, wcols] += _dot_tn(xcb, dpb)
                db_ref[h:h + 1, wcols] += jnp.sum(dpre, axis=0, keepdims=True)
                dxc_ref[_zrows(h, T), :] = du * mult * gi + _dot_nt(dpb, w_ref[h, :, wcols])

    full = lambda shape: pl.BlockSpec(shape, lambda i: (0,) * len(shape))
    wsp, bsp, lsp = full((NH, HD, 4 * HD)), full((NH, 4 * HD)), full((NH, 2 * HD))
    st = full((NH, HD))
    up = pl.BlockSpec((T * NH, HD), lambda i: (tile_u(i), 0))
    dn = pl.BlockSpec((T * NH, HD), lambda i: (tile_d(i), 0))
    dy_up = pl.BlockSpec((T * NH, HD), lambda i: (jnp.maximum(tile_u(i) - 1, 0), 0))
    dy_dn = pl.BlockSpec((T * NH, HD), lambda i: (jnp.maximum(tile_d(i) - 1, 0), 0))
    nxt = _tile_specs(NH, HD, n, tile_u)[2]
    prv = _tile_specs(NH, HD, n, tile_d)[1]
    g_up = pl.BlockSpec((T, 3 * D), lambda i: (tile_u(i), 0))
    g_dn = pl.BlockSpec((T, 3 * D), lambda i: (tile_d(i), 0))
    zs = jax.ShapeDtypeStruct((lx * NH, HD), F32)
    zbuf = pltpu.VMEM((T * NH, HD), F32)
    zbuf1 = pltpu.VMEM(((T + 1) * NH, HD), F32)
    return pl.pallas_call(
        body, name=name, grid=(n,),
        out_shape=(zs, zs, jax.ShapeDtypeStruct((NH, HD, 4 * HD), F32), jax.ShapeDtypeStruct((NH, 4 * HD), F32),
                   jax.ShapeDtypeStruct((NH, 2 * HD), F32)),
        in_specs=[up, dy_up, up, nxt, up, g_up, dn, dy_dn, dn, prv, dn, g_dn, st, wsp, lsp],
        out_specs=(up, dn, wsp, bsp, lsp),
        scratch_shapes=[zbuf, zbuf, zbuf, zbuf, zbuf1, zbuf1, pltpu.VMEM((T, 2 * HD), F32),
                        pltpu.VMEM((2, NH, HD), F32)],
        compiler_params=_cp(1, vmem_mb=56),
    )(xc_z, dy_z, hb_z, hb_z, ab_z, gb, xc_z, dy_z, hf_z, hf_z, af_z, gf, s_b, wcat, lamcat)


def _conv_bwd(dxc_a, dxc_b, xa, conv_wz, dcw0, dcb0, name):
    lx = dxc_a.shape[0] // NH
    n = lx // T

    def body(dm_a, dp_a, dn_a, dm_b, dp_b, dn_b, xan_ref, cw, dcw0_ref, dcb0_ref, dxa_ref, dcw_ref, dcb_ref,
             pad, dxa_s, xa_ref):
        i = pl.program_id(0)

        @pl.when(i == 0)
        def _():
            dcw_ref[...] = dcw0_ref[...]
            dcb_ref[...] = dcb0_ref[...]

        for h in range(NH):
            xa_ref[_zrows(h, T), :] = xan_ref[:, HD * h:HD * h + HD]

        pmask = jnp.where(_has_prev(i), 1.0, 0.0)
        nmask = jnp.where(_has_next(i, n), 1.0, 0.0)
        pad[pl.ds(0, 8 * NH), :] = (dp_a[...] + dp_b[...]) * pmask
        pad[pl.ds(8 * NH, T * NH), :] = dm_a[...] + dm_b[...]
        pad[pl.ds((T + 8) * NH, 8 * NH), :] = (dn_a[...] + dn_b[...]) * nmask

        def chunk(ci, carry):
            base = pl.multiple_of(ci * (CONV_CHUNK * NH), CONV_CHUNK * NH)
            xav = xa_ref[pl.ds(base, CONV_CHUNK * NH), :].reshape(CONV_CHUNK, NH, HD)
            acc = None
            for k in range(4):
                sl = pad[pl.ds(base + (9 - k) * NH, CONV_CHUNK * NH), :].reshape(CONV_CHUNK, NH, HD)
                term = sl * cw[k][None]
                acc = term if acc is None else acc + term
                dcw_ref[k] += jnp.sum(sl * xav, axis=0)
                if k == 1:
                    dcb_ref[...] += jnp.sum(sl, axis=0)
            dxa_s[pl.ds(base, CONV_CHUNK * NH), :] = acc.reshape(CONV_CHUNK * NH, HD)
            return carry
        lax.fori_loop(0, T // CONV_CHUNK, chunk, 0)
        for h in range(NH):
            dxa_ref[:, HD * h:HD * h + HD] = dxa_s[_zrows(h, T), :].astype(BF16)

    full = lambda shape: pl.BlockSpec(shape, lambda i: (0,) * len(shape))
    return pl.pallas_call(
        body, name=name, grid=(n,),
        out_shape=(jax.ShapeDtypeStruct((lx, D), BF16), jax.ShapeDtypeStruct((4, NH, HD), F32),
                   jax.ShapeDtypeStruct((NH, HD), F32)),
        in_specs=_tile_specs(NH, HD, n, lambda i: i) * 2 + [pl.BlockSpec((T, D), lambda i: (i, 0)), full((4, NH, HD)),
                                                            full((4, NH, HD)), full((NH, HD))],
        out_specs=(pl.BlockSpec((T, D), lambda i: (i, 0)), full((4, NH, HD)), full((NH, HD))),
        scratch_shapes=[pltpu.VMEM(((T + 16) * NH, HD), F32), pltpu.VMEM((T * NH, HD), F32),
                        pltpu.VMEM((T * NH, HD), F32)],
        compiler_params=_cp(1, vmem_mb=48),
    )(dxc_a, dxc_a, dxc_a, dxc_b, dxc_b, dxc_b, xa, conv_wz, dcw0, dcb0)


def _proj_bwd(dxa, dga, dgb, x, ctx, dxn, mods, norm_g, w_full, sgu):
    lx, lc = x.shape[0], ctx.shape[0]
    assert lc == T
    n = 1 + lx // T

    def body(dxa_ref, dga_ref, dgb_ref, w0, w1, w4, w2, w3, x_ref, c_ref, sc_ref, ng_ref, dxn_ref,
             u_ref, v_ref, dy_ref, g_ref, b_ref, sw_ref, bt_ref,
             gx_ref, dng_ref, dscx_ref, dshx_ref, dscc_ref, dshc_ref, du_ref, dv_ref, dws_ref, dbs_ref, dlg_ref, dlb_ref,
             mixed_s, dvn_s):
        i = pl.program_id(0)
        is_ctx = i == 0

        @pl.when(is_ctx)
        def _():
            for acc in (dng_ref, dscx_ref, dshx_ref, dscc_ref, dshc_ref, dws_ref, dbs_ref, dlg_ref, dlb_ref):
                acc[...] = jnp.zeros_like(acc)

        xv = jnp.where(is_ctx, c_ref[...], x_ref[...])
        sc1 = 1.0 + jnp.where(is_ctx, sc_ref[1:2, :], sc_ref[0:1, :])
        r = lax.rsqrt(jnp.mean(xv * xv, axis=-1, keepdims=True) + NORM_EPS)
        xn = xv * r
        ng = ng_ref[...]

        def norm_bwd(dhn, dsc_ref, dsh_ref, with_x):
            t = dhn * xn
            dng_ref[...] += jnp.sum(t * sc1, axis=0, keepdims=True)
            dsc_ref[...] += jnp.sum(t * ng, axis=0, keepdims=True)
            dsh_ref[...] += jnp.sum(dhn, axis=0, keepdims=True)
            if with_x:
                dxh = dhn * (ng * sc1)
                gx_ref[...] = dxn_ref[...] + r * (dxh - xn * jnp.mean(dxh * xn, axis=-1, keepdims=True))

        @pl.when(is_ctx)
        def _():
            norm_bwd(_dot_nt(dxa_ref[...], w0[...]), dscc_ref, dshc_ref, False)

        @pl.when(i > 0)
        def _():
            def sgu_chunk(ch):
                rows = slice(HD * ch, HD * ch + HD)
                du, dv = _sgu_bwd_chunk(u_ref[rows, :], v_ref[rows, :], dy_ref[rows, :], g_ref[...], b_ref[...],
                                        sw_ref, bt_ref, mixed_s, dvn_s, dws_ref, dbs_ref, dlg_ref, dlb_ref)
                du_ref[rows, :] = du
                dv_ref[rows, :] = dv

            dhn = _dot_nt(dxa_ref[...], w0[...])
            sgu_chunk(0)
            dhn = dhn + _dot_nt(dga_ref[...], w1[...])
            for ch in range(1, T // HD):
                sgu_chunk(ch)
            dhn = dhn + _dot_nt(dgb_ref[...], w4[...])
            dhn = dhn + _dot_nt(du_ref[...], w2[...]) + _dot_nt(dv_ref[...], w3[...])
            norm_bwd(dhn, dscx_ref, dshx_ref, True)

    every = pl.BlockSpec((T, D), lambda i: (i, 0))
    lat = pl.BlockSpec((T, D), lambda i: (jnp.maximum(i - 1, 0), 0))
    vec = pl.BlockSpec((1, D), lambda i: (0, 0))
    wsp = pl.BlockSpec((NH, HD, HD), lambda i: (0, 0, 0))
    bsp = pl.BlockSpec((NH, HD), lambda i: (0, 0))
    in_specs = [every, lat, lat] + [pl.BlockSpec((D, D), lambda i, k=k: (0, k)) for k in (0, 1, 4, 2, 3)]
    in_specs += [lat, pl.BlockSpec((T, D), lambda i: (0, 0)), pl.BlockSpec((8, D), lambda i: (0, 1)), vec, lat]
    in_specs += [lat, lat, lat, vec, vec, wsp, pl.BlockSpec((HD, NH), lambda i: (0, 0))]
    vs = jax.ShapeDtypeStruct((1, D), F32)
    zb = jax.ShapeDtypeStruct((lx, D), BF16)
    return pl.pallas_call(
        body, name="proj_bwd", grid=(n,),
        out_shape=(jax.ShapeDtypeStruct((lx, D), F32), vs, vs, vs, vs, vs, zb, zb,
                   jax.ShapeDtypeStruct((NH, HD, HD), F32), jax.ShapeDtypeStruct((NH, HD), F32), vs, vs),
        in_specs=in_specs, out_specs=(lat, vec, vec, vec, vec, vec, lat, lat, wsp, bsp, vec, vec),
        scratch_shapes=[pltpu.VMEM((HD, D), F32), pltpu.VMEM((HD, D), F32)], compiler_params=_cp(1, vmem_mb=56),
    )(dxa, dga, dgb, *([w_full] * 5), x, ctx, mods, norm_g, dxn, *sgu)


def _adam_math(w, g, m, v):
    m = ADAM_B1 * m + (1.0 - ADAM_B1) * g
    v = ADAM_B2 * v + (1.0 - ADAM_B2) * (g * g)
    m_hat = m / (1.0 - ADAM_B1 ** ADAM_STEP)
    v_hat = v / (1.0 - ADAM_B2 ** ADAM_STEP)
    delta = -ADAM_LR * (m_hat / (jnp.sqrt(v_hat) + ADAM_EPS) + ADAM_WD * w)
    return delta, m, v


def _adam_big(w, g, m, v, name):
    rows, cols = w.shape
    tr = 256

    def body(w_ref, g_ref, m_ref, v_ref, d_o, m_o, v_o):
        d, mm, vv = _adam_math(w_ref[...], g_ref[...], m_ref[...], v_ref[...])
        d_o[...] = d
        m_o[...] = mm
        v_o[...] = vv

    blk = pl.BlockSpec((tr, cols), lambda i: (i, 0))
    s = jax.ShapeDtypeStruct((rows, cols), F32)
    return pl.pallas_call(
        body, name=name, grid=(rows // tr,), out_shape=(s, s, s), in_specs=[blk] * 4, out_specs=(blk,) * 3,
        compiler_params=_cp(1, vmem_mb=48),
    )(w, g, m, v)


def _adam_small(items, tot):
    ni = len(items)
    pieces = [it[1] if isinstance(it[1], list) else None for it in items]
    flat = [a for it, pc in zip(items, pieces) for a in ((it[0], it[2], it[3]) if pc is not None else it)]
    n_in = len(flat) + 1
    out_shape = tuple(jax.ShapeDtypeStruct(it[0].shape, F32) for it, pc in zip(items, pieces)
                      for _ in range(4 if pc is not None else 3))
    n_out = len(out_shape)
    n_loads = sum(3 + (len(pc) if pc is not None else 1) for pc in pieces)

    def body(*refs):
        ins, tot_ref, outs = refs[:n_in - 1], refs[n_in - 1], refs[n_in:n_in + n_out]
        bufs = refs[n_in + n_out:n_in + n_out + 7 * ni]
        sem_in, sem_out = refs[n_in + n_out + 7 * ni:]
        loads, q_in, q_sem = [], 0, 0
        for k, pc in enumerate(pieces):
            w_b, g_b, m_b, v_b = bufs[7 * k:7 * k + 4]
            srcs = [(ins[q_in], w_b)]
            if pc is None:
                srcs.append((ins[q_in + 1], g_b))
                q_in += 1
            else:
                srcs += [(tot_ref.at[pl.ds(r0, nr), pl.ds(c0, nc)], g_b.at[pl.ds(d0, nr), :]) for r0, nr, c0, nc, d0 in pc]
            srcs += [(ins[q_in + 1], m_b), (ins[q_in + 2], v_b)]
            q_in += 3
            mine = []
            for src, dst in srcs:
                mine.append(pltpu.make_async_copy(src, dst, sem_in.at[q_sem]))
                q_sem += 1
            loads.append(mine)
        for mine in loads:
            for cp in mine:
                cp.start()
        stores, q_out = [], 0
        for k, pc in enumerate(pieces):
            for cp in loads[k]:
                cp.wait()
            w_b, g_b, m_b, v_b = bufs[7 * k:7 * k + 4]
            res = _adam_math(w_b[...], g_b[...], m_b[...], v_b[...])
            srcs = []
            for q in range(3):
                bufs[7 * k + 4 + q][...] = res[q]
                srcs.append(bufs[7 * k + 4 + q])
            if pc is not None:
                srcs.append(g_b)
            for src in srcs:
                cp = pltpu.make_async_copy(src, outs[q_out], sem_out.at[q_out])
                cp.start()
                stores.append(cp)
                q_out += 1
        for cp in stores:
            cp.wait()

    scratch = [pltpu.VMEM(it[0].shape, F32) for it in items for _ in range(7)]
    scratch += [pltpu.SemaphoreType.DMA((n_loads,)), pltpu.SemaphoreType.DMA((n_out,))]
    res = pl.pallas_call(
        body, name="adam_small", out_shape=out_shape, in_specs=[HBM] * n_in, out_specs=(HBM,) * n_out,
        scratch_shapes=scratch, compiler_params=_cp(vmem_mb=40),
    )(*[_in_hbm(a) for a in flat], _in_hbm(tot))
    outs, q = [], 0
    for pc in pieces:
        outs.append(tuple(res[q:q + 3]) + ((res[q + 3],) if pc is not None else (None,)))
        q += 4 if pc is not None else 3
    return outs


def kernel(x, c, ctx, c_ctx, ada_w, ada_b, norm_g, w_in, conv_w, conv_b, lru_wa, lru_ba, lru_wx, lru_bx, lru_lambda, sgu_ln_g, sgu_ln_b, sgu_w, sgu_b, w_out, final_g, loss_target, m_c_ctx, m_ada_w, m_ada_b, m_norm_g, m_w_in, m_conv_w, m_conv_b, m_lru_wa, m_lru_ba, m_lru_wx, m_lru_bx, m_lru_lambda, m_sgu_ln_g, m_sgu_ln_b, m_sgu_w, m_sgu_b, m_w_out, m_final_g, v_c_ctx, v_ada_w, v_ada_b, v_norm_g, v_w_in, v_conv_w, v_conv_b, v_lru_wa, v_lru_ba, v_lru_wx, v_lru_bx, v_lru_lambda, v_sgu_ln_g, v_sgu_ln_b, v_sgu_w, v_sgu_b, v_w_out, v_final_g):
    ix, iy, ic = lax.axis_index("x"), lax.axis_index("y"), lax.axis_index("c")
    chip = 2 * ix + iy
    dev = 2 * chip + ic
    lx = x.shape[1]
    lc = ctx.shape[1]

    smalls = jnp.concatenate([conv_w[0], lru_lambda[0], jnp.zeros((10, 256), F32)], axis=0)
    c_ctx2 = c_ctx.reshape(1, D)
    mods, c_slots, _, w_in_full, wo_land, ada_land, wcat, bcat, sgu_wb, conv_wz, lamcat = _gather_in(
        c, c_ctx2, ada_w[0], ada_b, w_in[0], w_out[0], smalls,
        (lru_wa[0], lru_wx[0], lru_ba[0], lru_bx[0], sgu_w[0]))
    wo_ss, wo_rs, ada_ss, ada_rs, wo_land, ada_land, token = _late_gather_start(wo_land, ada_land)
    conv_bz = conv_b.reshape(NH, HD)
    sgu_bt = sgu_b[0].T
    final_g2 = final_g.reshape(1, D)

    zero_s = jnp.zeros((NH, HD), F32)
    hn, xa_all, ga, u, v, gb, ys = _proj(x[0], ctx[0], mods, norm_g, w_in_full, (sgu_ln_g, sgu_ln_b, sgu_wb, sgu_bt),
                                         token)
    xcz, af, ab, hf, hb, gf, gb_l, _, hb0 = _lru_fwd(xa_all, conv_wz, conv_bz, wcat, bcat, lamcat, "lru_fwd")

    w_out_full = _late_gather_wait(wo_land, wo_ss, wo_rs, "w_out", hf, "late_gather_wait_w_out")
    (loss_part, dfg, dgx, dxn, y, do, dga, dgb, dyl_z, dys) = _out_fwd_bwd(
        hf, hb, ga, gb, ys, x[0], loss_target[0], mods, final_g2, w_out_full)

    dxc_b, dxc_f, dwc, dbc, dlc = _lru_bwd(xcz, dyl_z, hf, hb, af, ab, gf, gb_l, hb0, wcat, lamcat, "lru_bwd")
    dxa, dcw, dcb = _conv_bwd(dxc_b, dxc_f, xa_all, conv_wz, jnp.zeros((4, NH, HD), F32), zero_s, "conv_bwd")
    dxa = _in_hbm(dxa)

    grad_x, dng, dsc_x, dsh_x, dsc_c, dsh_c, du, dv, d_sgu_w, d_sgu_b, d_ln_g, d_ln_b = _proj_bwd(
        dxa, dga, dgb, x[0], ctx[0], dxn, mods, norm_g, w_in_full, (u, v, dys, sgu_ln_g, sgu_ln_b, sgu_wb, sgu_bt))
    dzs = [dxa, dga, du, dv, dgb]

    dmx = jnp.concatenate([dsh_x, dsc_x, dgx], axis=0)
    dmc = jnp.concatenate([dsh_c, dsc_c, jnp.zeros((1, D), F32)], axis=0)
    slot = jnp.concatenate([dmx, loss_part], axis=0)
    slots = lax.dynamic_update_slice(jnp.zeros((32, D), F32), slot, (4 * dev, 0))
    vecs = jnp.concatenate([dfg, dng, dcb.reshape(1, D), d_ln_g, d_ln_b, dcw.reshape(4, D), dmc,
                            jnp.zeros((4, D), F32), slots], axis=0)
    d_sgu_w4 = d_sgu_w.reshape(4, 256, HD).transpose(1, 0, 2).reshape(256, 4 * HD)
    pad8 = lambda a: jnp.pad(a, ((0, 8 - a.shape[0]), (0, 4 * HD - a.shape[1])))
    pack = jnp.concatenate([dwc.reshape(NH * HD, 4 * HD), pad8(dbc), pad8(dlc), d_sgu_w4, pad8(d_sgu_b),
                            vecs.reshape(96, 4 * HD), jnp.zeros((8, 4 * HD), F32)], axis=0)
    g_w_in, g_w_out, tot = _grads_reduce(hn, dzs, lc, y, do, _in_hbm(pack))

    n_w = NH * HD
    g_lru_wa = [(0, n_w, 2 * HD * d, HD, n_w * d) for d in range(2)]
    g_lru_wx = [(0, n_w, 2 * HD * d + HD, HD, n_w * d) for d in range(2)]
    g_lru_ba = [(n_w, NH, 2 * HD * d, HD, NH * d) for d in range(2)]
    g_lru_bx = [(n_w, NH, 2 * HD * d + HD, HD, NH * d) for d in range(2)]
    g_sgu_w = [(1040, 256, HD * q, HD, 256 * q) for q in range(4)]
    g_sgu_b = [(1296, NH, 0, HD, 0)]
    g_lc = tot[1032:1040, 0:2 * HD]
    tv = tot[1304:1400].reshape(48, D)
    g_final_g, g_norm_g, g_conv_b, g_ln_g, g_ln_b = tv[0:1], tv[1:2], tv[2:3], tv[3:4], tv[4:5]
    g_conv_w_full = tv[5:9]
    dmc_tot = tv[9:12].reshape(1, 3 * D)
    slots_all = tv[16:48].reshape(8, 4, D)
    dmx_all = slots_all[:, 0:3, :].reshape(8, 3 * D)
    c_all = c_slots.reshape(8, 8, D)[:, 0, :]
    g_lam_full = jnp.stack([g_lc[:, 0:HD], g_lc[:, HD:2 * HD]]).reshape(2, D)
    g_conv_w = lax.dynamic_slice(g_conv_w_full, (0, 256 * chip), (4, 256))
    g_lam = lax.dynamic_slice(g_lam_full, (0, 256 * chip), (2, 256))
    dmx_all_j = lax.dynamic_slice(dmx_all, (0, 768 * chip), (8, 768))
    dmc_j = lax.dynamic_slice(dmc_tot, (0, 768 * chip), (1, 768))
    ada_full = _late_gather_wait(ada_land, ada_ss, ada_rs, "ada_w", _in_hbm(tot), "late_gather_wait_ada_w")
    g_ada_w, g_ada_b, g_c_ctx = _ada_bwd(c_all, dmx_all_j, dmc_j, dmx_all, dmc_tot, c_ctx2, ada_full)

    big = {
        "ada_w": _adam_big(ada_w[0], g_ada_w, m_ada_w[0], v_ada_w[0], "adam_ada_w"),
        "w_in": _adam_big(w_in[0], g_w_in, m_w_in[0], v_w_in[0], "adam_w_in"),
        "w_out": _adam_big(w_out[0], g_w_out, m_w_out[0], v_w_out[0], "adam_w_out"),
    }
    small_in = {
        "c_ctx": (c_ctx, g_c_ctx, m_c_ctx, v_c_ctx, (1, D)),
        "ada_b": (ada_b, g_ada_b, m_ada_b, v_ada_b, (1, 3 * D)),
        "norm_g": (norm_g, g_norm_g, m_norm_g, v_norm_g, (1, D)),
        "conv_w": (conv_w, g_conv_w, m_conv_w, v_conv_w, (4, 256)),
        "conv_b": (conv_b, g_conv_b, m_conv_b, v_conv_b, (1, D)),
        "lru_wa": (lru_wa, g_lru_wa, m_lru_wa, v_lru_wa, (2 * NH * HD, HD)),
        "lru_ba": (lru_ba, g_lru_ba, m_lru_ba, v_lru_ba, (2 * NH, HD)),
        "lru_wx": (lru_wx, g_lru_wx, m_lru_wx, v_lru_wx, (2 * NH * HD, HD)),
        "lru_bx": (lru_bx, g_lru_bx, m_lru_bx, v_lru_bx, (2 * NH, HD)),
        "lru_lambda": (lru_lambda, g_lam, m_lru_lambda, v_lru_lambda, (2, 256)),
        "sgu_ln_g": (sgu_ln_g, g_ln_g, m_sgu_ln_g, v_sgu_ln_g, (1, D)),
        "sgu_ln_b": (sgu_ln_b, g_ln_b, m_sgu_ln_b, v_sgu_ln_b, (1, D)),
        "sgu_w": (sgu_w, g_sgu_w, m_sgu_w, v_sgu_w, (NH * HD, HD)),
        "sgu_b": (sgu_b, g_sgu_b, m_sgu_b, v_sgu_b, (NH, HD)),
        "final_g": (final_g, g_final_g, m_final_g, v_final_g, (1, D)),
    }
    names_small = list(small_in)
    res_small = _adam_small([tuple(a if isinstance(a, list) else a.reshape(small_in[k][4]) for a in small_in[k][:4])
                             for k in names_small], tot)
    full_shapes = {"ada_w": ada_w.shape, "w_in": w_in.shape, "w_out": w_out.shape}
    grads, deltas, new_m, new_v = {}, {}, {}, {}
    for k in ("ada_w", "w_in", "w_out"):
        g = {"ada_w": g_ada_w, "w_in": g_w_in, "w_out": g_w_out}[k]
        grads[k] = g.reshape(full_shapes[k])
        deltas[k], new_m[k], new_v[k] = (a.reshape(full_shapes[k]) for a in big[k])
    for k, res in zip(names_small, res_small):
        shape = small_in[k][0].shape
        grads[k] = (small_in[k][1] if res[3] is None else res[3]).reshape(shape)
        deltas[k], new_m[k], new_v[k] = (a.reshape(shape) for a in res[:3])

    loss = jnp.sum(slots_all[:, 3, 0:3])
    order = ["c_ctx", "ada_w", "ada_b", "norm_g", "w_in", "conv_w", "conv_b", "lru_wa", "lru_ba", "lru_wx", "lru_bx",
             "lru_lambda", "sgu_ln_g", "sgu_ln_b", "sgu_w", "sgu_b", "w_out", "final_g"]
    return (loss, grad_x.reshape(x.shape), *[grads[k] for k in order], *[deltas[k] for k in order],
            *[new_m[k] for k in order], *[new_v[k] for k in order])
```

```python
import jax
import jax.numpy as jnp
from jax import lax
from jax.experimental import pallas as pl
from jax.experimental.pallas import tpu as pltpu

F32 = jnp.float32
BF16 = jnp.bfloat16

D = 1024
NH = 8
HD = 128
NCHIP = 4
T = 256
NORM_EPS = 1e-6
LN_EPS = 1e-5
LRU_C = 8.0
ADAM_LR = 0.001
ADAM_B1 = 0.9
ADAM_B2 = 0.999
ADAM_EPS = 1e-08
ADAM_WD = 0.01
ADAM_STEP = 10

VMEM = pl.BlockSpec(memory_space=pltpu.VMEM)
ANY = pl.BlockSpec(memory_space=pl.ANY)
MESH = pl.DeviceIdType.MESH


def _cp(n_grid=0, vmem_mb=None):
    kw = {}
    if n_grid:
        kw["dimension_semantics"] = ("arbitrary",) * n_grid
    if vmem_mb:
        kw["vmem_limit_bytes"] = vmem_mb << 20
    return pltpu.CompilerParams(**kw)


def _sigmoid(x):
    return 0.5 * jnp.tanh(0.5 * x) + 0.5


def _silu_and_grad(x):
    s = _sigmoid(x)
    return x * s, s * (1.0 + x * (1.0 - s))


_GELU_K = 0.7978845608028654
_GELU_C = 0.044715


def _gelu_and_grad(x):
    x2 = x * x
    th = jnp.tanh(x * (_GELU_K + (_GELU_K * _GELU_C) * x2))
    p = 0.5 + 0.5 * th
    g = x * p
    dg = p + g * (1.0 - th) * (_GELU_K + (3.0 * _GELU_K * _GELU_C) * x2)
    return g, dg


def _softplus(x):
    return jnp.maximum(x, 0.0) + jnp.log1p(jnp.exp(-jnp.abs(x)))


def _lru_gate(pre, lam_row, d, off=None):
    off = 256 * d if off is None else off
    r = _sigmoid(pre[:, off:off + HD])
    gi = _sigmoid(pre[:, off + HD:off + 2 * HD])
    lam = lam_row[:, HD * d:HD * d + HD]
    sp = _softplus(-lam)
    la = (-LRU_C) * r * sp
    a = jnp.exp(la)
    x2 = 2.0 * la
    m2 = jnp.where(x2 > -1e-3, -x2 * (1.0 + 0.5 * x2), 1.0 - a * a)
    mult = jnp.sqrt(m2)
    return r, gi, lam, sp, a, mult


def _dot(a, b):
    return jnp.dot(a, b, preferred_element_type=F32)


def _dot_tn(a, b):
    return lax.dot_general(a, b, (((0,), (0,)), ((), ())), preferred_element_type=F32)


def _dot_nt(a, b):
    return lax.dot_general(a, b, (((1,), (1,)), ((), ())), preferred_element_type=F32)


def _mo(v, m):
    return v if isinstance(v, int) else pl.multiple_of(v, m)


def _zrows(h, n):
    return pl.ds(h, n, stride=NH)


def _gather_in(c, c_ctx, ada_w, ada_b, w_in, w_out, smalls, lru_sgu):
    nch = [1, 4]
    wrows = lambda cc, q: (pl.ds(_mo(512 * cc, 16), 512) if q is None
                           else pl.ds(_mo(512 * cc + (512 // nch[1]) * q, 16), 512 // nch[1]))
    specs = [
        ((64, 256), F32, lambda r, jj, cc, q=None: r.at[pl.ds(_mo(16 * jj + 8 * cc, 8), 8), :]),
        ((D, 5120), BF16, lambda r, jj, cc, q=None: r.at[wrows(cc, q), pl.ds(_mo(1280 * jj, 128), 1280)]),
    ]
    halves = [lambda r, cc, q=None: r.at[pl.ds(_mo(8 * cc, 8), 8), :],
              lambda r, cc, q=None: r.at[wrows(cc, q), :]]
    na = len(specs)
    sem_base = [0, 6 * nch[0]]
    sidx = lambda a, q, k: sem_base[a] + 6 * q + k
    n_tiny = 6 * sum(nch)
    n_sem = n_tiny + 10

    def body(c_ref, cc_ref, ada_ref, adab_ref, win_ref, wout_ref, sm_ref, wa_ref, wx_ref, ba_ref, bx_ref, sw_ref, sb_ref,
             mods_o, call_o, sm_o, win_o, wol_o, adal_o, wcat_o, bcat_o, swb_o, cwz_o, lam_o, sbt_o,
             s_win, s_ada, s_wout, f_win, f_ada, f_wout, cslot, lhs, mbuf,
             send_sems, recv_sems, local_sems, load_sems, f_w, f_sw, s_wcat, s_swb, prep_sems, f_sb):
        x, y, c = lax.axis_index("x"), lax.axis_index("y"), lax.axis_index("c")
        j = 2 * x + y
        dev = 2 * j + c
        sib = (x, y, 1 - c)
        chips = [(1 - x, y), (x, 1 - y), (1 - x, 1 - y)]
        cj = [2 * cx + cy for cx, cy in chips]
        outs = [sm_o, win_o]
        srcs = [sm_ref, s_win]

        def copy(idx, src, dst, to):
            return pltpu.make_async_remote_copy(src_ref=src, dst_ref=dst, send_sem=send_sems.at[idx],
                                                recv_sem=recv_sems.at[idx], device_id=to, device_id_type=MESH)

        sends = []

        def start(cp):
            cp.start()
            sends.append(cp)

        cslot[...] = jnp.zeros_like(cslot)
        cslot[0:1, :] = c_ref[...]
        my_slot = pl.ds(_mo(8 * dev, 8), 8)
        others = [sib] + [(*chips[k], c) for k in range(3)] + [(*chips[k], 1 - c) for k in range(3)]
        other_dev = [dev + 1 - 2 * c] + [2 * cj[k] + c for k in range(3)] + [2 * cj[k] + 1 - c for k in range(3)]
        base = n_tiny
        for r in range(7):
            start(copy(base + r, cslot, call_o.at[my_slot, :], others[r]))
        call_o[my_slot, :] = cslot[...]

        crow = 512 // nch[1]
        loads = []
        for cc in (c, 1 - c):
            for q in range(nch[1]):
                rows = pl.ds(_mo(512 * cc + crow * q, 16), crow)
                loads.append(pltpu.make_async_copy(win_ref.at[rows, :], f_win.at[rows, :], load_sems.at[len(loads)]))
        loads.append(pltpu.make_async_copy(ada_ref, f_ada, load_sems.at[len(loads)]))
        loads.append(pltpu.make_async_copy(wout_ref, f_wout, load_sems.at[len(loads)]))
        for ld in loads:
            ld.start()
        prep = []
        for d in range(2):
            for q, (w_src, b_src) in enumerate([(wa_ref, ba_ref), (wx_ref, bx_ref)]):
                prep.append(pltpu.make_async_copy(w_src.at[d], f_w.at[2 * d + q], prep_sems.at[len(prep)]))
                prep.append(pltpu.make_async_copy(b_src.at[d], bcat_o.at[:, pl.ds(HD * (2 * d + q), HD)],
                                                  prep_sems.at[len(prep)]))
        prep.append(pltpu.make_async_copy(sw_ref, f_sw, prep_sems.at[len(prep)]))
        prep.append(pltpu.make_async_copy(sb_ref, f_sb, prep_sems.at[len(prep)]))
        for cp in prep:
            cp.start()
        for k in range(2):
            start(copy(sidx(0, 0, k), halves[0](srcs[0], c), specs[0][2](outs[0], j, c), (*chips[k], c)))
        for q in range(nch[1]):
            loads[q].wait()
            rows = pl.ds(_mo(512 * c + crow * q, 16), crow)
            s_win[rows, :] = f_win[rows, :].astype(BF16)
            for k in range(2):
                start(copy(sidx(1, q, k), halves[1](s_win, c, q), specs[1][2](win_o, j, c, q), (*chips[k], c)))
        for q in range(nch[1]):
            loads[nch[1] + q].wait()
            rows = pl.ds(_mo(512 * (1 - c) + crow * q, 16), crow)
            s_win[rows, :] = f_win[rows, :].astype(BF16)
        local = []
        for a in range(na):
            for cc in range(2):
                lc = pltpu.make_async_copy(halves[a](srcs[a], cc), specs[a][2](outs[a], j, cc), local_sems.at[2 * a + cc])
                lc.start()
                local.append(lc)
        loads[2 * nch[1]].wait()
        s_ada[...] = f_ada[...].astype(BF16)
        loads[2 * nch[1] + 1].wait()
        s_wout[...] = f_wout[...].astype(BF16)
        for q, (src, dst) in enumerate([(s_wout, wol_o.at[pl.ds(_mo(512 * j, 16), 512), :]),
                                        (s_ada, adal_o.at[:, pl.ds(_mo(768 * j, 128), 768)])]):
            lc = pltpu.make_async_copy(src, dst, local_sems.at[2 * na + q])
            lc.start()
            local.append(lc)
        for cp in prep:
            cp.wait()
        for q in range(4):
            s_wcat[:, :, HD * q:HD * q + HD] = f_w[q].astype(BF16)
        s_swb[...] = f_sw[...].astype(BF16)
        sbt_o[...] = f_sb[...].T
        for n, (src, dst) in enumerate([(s_wcat, wcat_o), (s_swb, swb_o)]):
            lc = pltpu.make_async_copy(src, dst, prep_sems.at[len(prep) + n])
            lc.start()
            local.append(lc)

        for r in range(7):
            slot = call_o.at[pl.ds(_mo(8 * other_dev[r], 8), 8), :]
            copy(base + r, slot, slot, sib).wait_recv()
        lhs[...] = jnp.zeros_like(lhs)
        for b in range(8):
            cv = call_o[8 * b:8 * b + 1, :]
            lhs[b:b + 1, :] = cv * _sigmoid(cv)
        cv = cc_ref[...]
        lhs[8:9, :] = cv * _sigmoid(cv)
        adab = adab_ref[:, 0:768]
        for jj in range(1, NCHIP):
            adab = jnp.where(j == jj, adab_ref[:, 768 * jj:768 * jj + 768], adab)
        mbuf[j] = _dot(lhs[...].astype(BF16), s_ada[...]) + adab
        for k in range(3):
            start(copy(base + 7 + k, mbuf.at[j], mbuf.at[j], (*chips[k], c)))
        for k in range(3):
            copy(base + 7 + k, mbuf.at[cj[k]], mbuf.at[cj[k]], sib).wait_recv()
        mods_o[...] = jnp.zeros_like(mods_o)
        for jj in range(NCHIP):
            mods_o[0:1, 768 * jj:768 * jj + 768] = mbuf[jj, pl.ds(dev, 1), :]
            mods_o[1:2, 768 * jj:768 * jj + 768] = mbuf[jj, 8:9, :]

        kx = [1 - x, x, 1 - x]
        ky = [y, 1 - y, 1 - y]
        pick = lambda k, lst: jnp.where(k == 0, lst[0], jnp.where(k == 1, lst[1], lst[2]))
        for a in range(na):
            for q in range(nch[a]):
                for step, k in enumerate([c, 1 - c]):
                    reg = specs[a][2](outs[a], pick(k, cj), c, q)
                    copy(sidx(a, q, k), reg, reg, sib).wait_recv()
                    if step == 0:
                        start(copy(sidx(a, q, 2), reg, reg, (pick(1 - c, kx), pick(1 - c, ky), c)))
                    start(copy(sidx(a, q, 3 + k), reg, reg, sib))
        for a in range(na):
            for q in range(nch[a]):
                reg = specs[a][2](outs[a], cj[2], c, q)
                copy(sidx(a, q, 2), reg, reg, sib).wait_recv()
                start(copy(sidx(a, q, 5), reg, reg, sib))
        for a in range(na):
            for q in range(nch[a]):
                for k in range(3):
                    reg = specs[a][2](outs[a], cj[k], 1 - c, q)
                    copy(sidx(a, q, 3 + k), reg, reg, sib).wait_recv()
        for cp in sends:
            cp.wait_send()
        for lc in local:
            lc.wait()
        for jj in range(NCHIP):
            for mh in range(2):
                h = 2 * jj + mh
                cols = slice(HD * mh, HD * mh + HD)
                for r in range(4):
                    cwz_o[r, h:h + 1, :] = sm_o[16 * jj + r:16 * jj + r + 1, cols]
                for d in range(2):
                    lam_o[h:h + 1, HD * d:HD * d + HD] = sm_o[16 * jj + 4 + d:16 * jj + 5 + d, cols]

    out_shape = (jax.ShapeDtypeStruct((8, 3 * D), F32), jax.ShapeDtypeStruct((64, D), F32),
                 jax.ShapeDtypeStruct(specs[0][0], F32), jax.ShapeDtypeStruct(specs[1][0], BF16),
                 jax.ShapeDtypeStruct((2048, D), BF16), jax.ShapeDtypeStruct((D, 3 * D), BF16),
                 jax.ShapeDtypeStruct((NH, HD, 4 * HD), BF16), jax.ShapeDtypeStruct((NH, 4 * HD), F32),
                 jax.ShapeDtypeStruct((NH, HD, HD), BF16), jax.ShapeDtypeStruct((4, NH, HD), F32),
                 jax.ShapeDtypeStruct((NH, 2 * HD), F32), jax.ShapeDtypeStruct((HD, NH), F32))
    return pl.pallas_call(
        body, name="gather_in", out_shape=out_shape,
        in_specs=[VMEM, VMEM, ANY, VMEM, ANY, ANY, VMEM] + [ANY] * 6,
        out_specs=(VMEM, VMEM, VMEM, ANY, ANY, ANY, ANY, ANY, ANY, VMEM, VMEM, VMEM),
        scratch_shapes=[pltpu.VMEM((D, 1280), BF16), pltpu.VMEM((D, 768), BF16), pltpu.VMEM((512, D), BF16),
                        pltpu.VMEM((D, 1280), F32), pltpu.VMEM((D, 768), F32), pltpu.VMEM((512, D), F32),
                        pltpu.VMEM((8, D), F32), pltpu.VMEM((16, D), F32), pltpu.VMEM((NCHIP, 16, 768), F32),
                        pltpu.SemaphoreType.DMA((n_sem,)), pltpu.SemaphoreType.DMA((n_sem,)),
                        pltpu.SemaphoreType.DMA((2 * na + 2,)), pltpu.SemaphoreType.DMA((2 * nch[1] + 2,)),
                        pltpu.VMEM((4, NH, HD, HD), F32), pltpu.VMEM((NH, HD, HD), F32),
                        pltpu.VMEM((NH, HD, 4 * HD), BF16), pltpu.VMEM((NH, HD, HD), BF16),
                        pltpu.SemaphoreType.DMA((12,)), pltpu.VMEM((NH, HD), F32)],
        compiler_params=_cp(vmem_mb=56),
    )(c, c_ctx, ada_w, ada_b, w_in, w_out, smalls, *[_in_hbm(a) for a in lru_sgu])


HBM = pl.BlockSpec(memory_space=pltpu.HBM)
SEM = pl.BlockSpec(memory_space=pltpu.SEMAPHORE)


def _in_hbm(a):
    return pltpu.with_memory_space_constraint(a, pltpu.HBM)


def _late_gather_regions(x, y, c):
    chips = [(1 - x, y), (x, 1 - y), (1 - x, 1 - y)]
    wo_reg = lambda r, jj, cc: r.at[pl.ds(_mo(512 * jj + 256 * cc, 16), 256), :]
    ada_reg = lambda r, jj, cc: r.at[pl.ds(_mo(512 * cc, 16), 512), pl.ds(_mo(768 * jj, 128), 768)]
    return chips, wo_reg, ada_reg


def _late_gather_start(wo_land, ada_land):
    def body(wol_ref, adal_ref, wo_ss, wo_rs, ada_ss, ada_rs, wol_thru, adal_thru, token):
        x, y, c = lax.axis_index("x"), lax.axis_index("y"), lax.axis_index("c")
        j = 2 * x + y
        chips, wo_reg, ada_reg = _late_gather_regions(x, y, c)
        for k in range(3):
            for cc in range(2):
                pltpu.make_async_remote_copy(src_ref=wo_reg(wol_ref, j, c), dst_ref=wo_reg(wol_ref, j, c),
                                             send_sem=wo_ss.at[2 * k + cc], recv_sem=wo_rs.at[2 * k + c],
                                             device_id=(*chips[k], cc), device_id_type=MESH).start()
        for k in range(3):
            for cc in range(2):
                pltpu.make_async_remote_copy(src_ref=ada_reg(adal_ref, j, c), dst_ref=ada_reg(adal_ref, j, c),
                                             send_sem=ada_ss.at[2 * k + cc], recv_sem=ada_rs.at[2 * k + c],
                                             device_id=(*chips[k], cc), device_id_type=MESH).start()
        token[...] = jnp.zeros_like(token)

    sems = pltpu.SemaphoreType.DMA((6,))
    return pl.pallas_call(
        body, name="late_gather_start",
        out_shape=(sems, sems, sems, sems, pltpu.HBM(wo_land.shape, BF16), pltpu.HBM(ada_land.shape, BF16),
                   jax.ShapeDtypeStruct((8, 128), F32)),
        in_specs=(HBM, HBM), out_specs=(SEM, SEM, SEM, SEM, HBM, HBM, VMEM), input_output_aliases={0: 4, 1: 5},
        compiler_params=pltpu.CompilerParams(has_side_effects=pltpu.SideEffectType.DATAFLOW_SIDE_EFFECTING),
    )(_in_hbm(wo_land), _in_hbm(ada_land))


def _late_gather_wait(land, send_sems, recv_sems, which, after, name):
    def body(land_ref, ss, rs, after_ref, land_out):
        x, y, c = lax.axis_index("x"), lax.axis_index("y"), lax.axis_index("c")
        j = 2 * x + y
        chips, wo_reg, ada_reg = _late_gather_regions(x, y, c)
        reg = wo_reg if which == "w_out" else ada_reg
        for k in range(3):
            kj = 2 * chips[k][0] + chips[k][1]
            for cc in range(2):
                cp = pltpu.make_async_remote_copy(src_ref=reg(land_ref, j, c), dst_ref=reg(land_ref, kj, cc),
                                                  send_sem=ss.at[2 * k + cc], recv_sem=rs.at[2 * k + cc],
                                                  device_id=(*chips[k], cc), device_id_type=MESH)
                cp.wait_send()
                cp.wait_recv()

    return pl.pallas_call(
        body, name=name, out_shape=pltpu.HBM(land.shape, land.dtype),
        in_specs=(HBM, SEM, SEM, ANY), out_specs=HBM, input_output_aliases={0: 0},
        compiler_params=pltpu.CompilerParams(has_side_effects=pltpu.SideEffectType.DATAFLOW_SIDE_EFFECTING),
    )(land, send_sems, recv_sems, after)


RCHUNK = 16


def _grads_reduce(hn, dzs, lc, y, do, pack):
    rp = pack.shape[0]
    hp = rp // 2
    assert hp % RCHUNK == 0
    wi_w = 1280
    lx = hn.shape[0] - lc
    lt = lx + lc
    n_dz = len(dzs)

    def body(*refs):
        hn_hbm, dz_hbm = refs[0], refs[1:1 + n_dz]
        y_hbm, do_hbm, pk_hbm, wi_out, wo_out, pk_out = refs[1 + n_dz:7 + n_dz]
        (hn_mine, hn_other, dzbuf, wi_other, wi_mine, wi_recv, wi_send, wi_rb,
         y_blk, do_mine, do_other, wo_other, wo_mine, wo_recv, wo_send, wo_rb,
         pk_mine, pk_recv, pk_send, pk_rb, pk_own, send_sems, recv_sems, local_sems) = refs[7 + n_dz:]
        x, y, c = lax.axis_index("x"), lax.axis_index("y"), lax.axis_index("c")
        j = 2 * x + y
        sib = (x, y, 1 - c)
        chips = [(1 - x, y), (x, 1 - y), (1 - x, 1 - y)]
        cj = [2 * cx + cy for cx, cy in chips]
        near = (jnp.where(c == 0, 1 - x, x), jnp.where(c == 0, y, 1 - y), c)
        slabs = [cj[2], cj[0], cj[1], j]

        def copy(k, src, dst, to):
            return pltpu.make_async_remote_copy(src_ref=src, dst_ref=dst, send_sem=send_sems.at[k],
                                                recv_sem=recv_sems.at[k], device_id=to, device_id_type=MESH)

        def local(k, src, dst):
            cp = pltpu.make_async_copy(src, dst, local_sems.at[k])
            cp.start()
            return cp

        rows_half = lambda r, cc, n: r.at[pl.ds(_mo(cc * n, 16), n), :]
        cols_half = lambda r, cc, n: r.at[:, pl.ds(_mo(cc * n, 128), n)]
        pk_piece = lambda r, cc, jj: r.at[pl.ds(_mo(cc * hp, 16), hp), pl.ds(_mo(jj * 128, 128), 128)]

        sends = []

        def start(cp):
            cp.start()
            sends.append(cp)

        def dz_pieces(s):
            g0 = wi_w * s
            k0, off0 = g0 // D, g0 % D
            w0 = min(D - off0, wi_w)
            pieces = [(k0, off0, w0, 0)]
            if w0 < wi_w:
                pieces.append((k0 + 1, 0, wi_w - w0, w0))
            return pieces

        def dz_copies(s):
            cps = []
            for q, (k, off, w, dst) in enumerate(dz_pieces(s)):
                cps.append(pltpu.make_async_copy(dz_hbm[k].at[pl.ds(lc if k == 0 else 0, lx), pl.ds(off, w)],
                                                 dzbuf.at[pl.ds(lc, lx), pl.ds(dst, w)], local_sems.at[11 + q]))
            if s == 0:
                cps.append(pltpu.make_async_copy(dz_hbm[0].at[pl.ds(0, lc), :], dzbuf.at[pl.ds(0, lc), pl.ds(0, D)],
                                                 local_sems.at[13]))
            return cps

        def dz_load(sl):
            for s in range(NCHIP):
                @pl.when(sl == s)
                def _():
                    if s == 0:
                        dzbuf[pl.ds(0, lc), pl.ds(D, wi_w - D)] = jnp.zeros((lc, wi_w - D), BF16)
                    else:
                        dzbuf[pl.ds(0, lc), :] = jnp.zeros((lc, wi_w), BF16)
                    for cp in dz_copies(s):
                        cp.start()

        def dz_wait(sl):
            for s in range(NCHIP):
                @pl.when(sl == s)
                def _():
                    for cp in dz_copies(s):
                        cp.wait()

        l_pk = local(0, rows_half(pk_hbm, c, hp), pk_mine)
        start(copy(0, rows_half(pk_hbm, 1 - c, hp), pk_recv, sib))
        col = lambda r, cc: r.at[:, pl.ds(_mo(cc * 512, 128), 512)]
        do_loads = [local(7, col(do_hbm, c), do_mine), local(14, col(do_hbm, 1 - c), do_other)]
        hn_loads = [local(2, col(hn_hbm, c), hn_mine), local(4, col(hn_hbm, 1 - c), hn_other)]
        y_copy = lambda s: pltpu.make_async_copy(col(y_hbm, slabs[s]), y_blk, local_sems.at[1])
        y_copy(0).start()
        dz_load(slabs[0])

        def pair_sum(mine, recv, send, nrows, keep, relayed=None):
            def step(i, carry):
                rows = pl.ds(_mo(i * RCHUNK, RCHUNK), RCHUNK)
                s = mine[rows, :] + recv[rows, :].astype(F32)
                if relayed is not None:
                    s = s + relayed[rows, :].astype(F32)
                if keep:
                    mine[rows, :] = s
                if send is not None:
                    send[rows, :] = s.astype(BF16)
                return carry
            lax.fori_loop(0, nrows // RCHUNK, step, 0)

        def chip_sum(own, rb, nrows, terms=(0, 1, 2)):
            def step(i, carry):
                rows = pl.ds(_mo(i * RCHUNK, RCHUNK), RCHUNK)
                acc = own[rows, :]
                for q in terms:
                    acc = acc + rb[q, rows, :].astype(F32)
                own[rows, :] = acc
                return carry
            lax.fori_loop(0, nrows // RCHUNK, step, 0)

        w_in_g = dict(other=wi_other, mine=wi_mine, recv=wi_recv, send=wi_send, rb=wi_rb, p1_sems=(2, 3, 4, 5), p2_sem=12,
                      p1=[None] * NCHIP, wait_load=lambda s: dz_wait(slabs[s]), load=lambda s: dz_load(slabs[s]),
                      dot_other=lambda: _dot_tn(hn_other[...], dzbuf[...]), dot_mine=lambda: _dot_tn(hn_mine[...], dzbuf[...]))
        w_out_g = dict(other=wo_other, mine=wo_mine, recv=wo_recv, send=wo_send, rb=wo_rb, p1_sems=(1, 24, 25, 26), p2_sem=9,
                       p1=[None] * NCHIP, wait_load=lambda s: y_copy(s).wait(), load=lambda s: y_copy(s).start(),
                       dot_other=lambda: _dot_tn(y_blk[...], do_other[...]), dot_mine=lambda: _dot_tn(y_blk[...], do_mine[...]))

        def piece_matmuls(g, s):
            if s >= 2:
                g["p1"][s - 2].wait_send()
            g["wait_load"](s)
            g["other"][s % 2] = g["dot_other"]().astype(BF16)
            g["p1"][s] = copy(g["p1_sems"][s], g["other"].at[s % 2], g["recv"].at[s], sib)
            g["p1"][s].start()
            g["mine"][s % 2] = g["dot_mine"]()
            if s + 1 < NCHIP:
                g["load"](s + 1)

        def piece_finish(g, s):
            mine, recv, send, rb, p2 = g["mine"].at[s % 2], g["recv"].at[s], g["send"], g["rb"], g["p2_sem"]
            nrows = mine.shape[0]
            copy(g["p1_sems"][s], recv, recv, sib).wait_recv()
            if s == 3:
                pair_sum(mine, recv, None, nrows, True)
                return
            if s == 0:
                pair_sum(mine, recv, send.at[0], nrows, False)
                start(copy(p2, send.at[0], rb.at[0], near))
                return
            adds_relayed = c == (1 if s == 1 else 0)

            @pl.when(adds_relayed)
            def _():
                copy(p2, rb.at[0], rb.at[0], sib).wait_recv()
                pair_sum(mine, recv, send.at[s], nrows, False, rb.at[0])

            @pl.when(jnp.logical_not(adds_relayed))
            def _():
                pair_sum(mine, recv, send.at[s], nrows, False)
            start(copy(p2 + s, send.at[s], rb.at[s], (*chips[s - 1], c)))

        def piece_total(g):
            for k in (1, 2):
                copy(g["p2_sem"] + k, g["rb"].at[k], g["rb"].at[k], sib).wait_recv()
            chip_sum(g["mine"].at[1], g["rb"], g["mine"].shape[1], (1, 2))

        for cp in do_loads:
            cp.wait()
        piece_matmuls(w_out_g, 0)
        piece_matmuls(w_out_g, 1)
        piece_finish(w_out_g, 0)
        piece_matmuls(w_out_g, 2)
        piece_finish(w_out_g, 1)
        piece_matmuls(w_out_g, 3)
        piece_finish(w_out_g, 2)

        for cp in hn_loads:
            cp.wait()
        piece_matmuls(w_in_g, 0)

        l_pk.wait()
        copy(0, pk_recv, pk_recv, sib).wait_recv()
        pair_sum(pk_mine, pk_recv, pk_send, hp, True)
        for k in range(3):
            start(copy(6 + k, pk_send.at[:, pl.ds(_mo(cj[k] * 128, 128), 128)], pk_rb.at[k], (*chips[k], c)))
        l_pk_own = local(6, pk_mine.at[:, pl.ds(_mo(j * 128, 128), 128)], pk_own)

        piece_matmuls(w_in_g, 1)
        piece_finish(w_in_g, 0)

        l_pk_own.wait()
        for k in range(3):
            copy(6 + k, pk_rb.at[k], pk_rb.at[k], sib).wait_recv()
        chip_sum(pk_own, pk_rb, hp)
        l_pk_out = local(8, pk_own, pk_piece(pk_out, c, j))
        start(copy(15, pk_own, pk_piece(pk_out, c, j), sib))
        for k in range(2):
            start(copy(16 + k, pk_own, pk_piece(pk_out, c, j), (*chips[k], c)))

        piece_matmuls(w_in_g, 2)
        piece_finish(w_in_g, 1)
        piece_matmuls(w_in_g, 3)
        piece_finish(w_in_g, 2)

        piece_finish(w_out_g, 3)
        piece_total(w_out_g)
        l_wo_out = local(9, wo_mine.at[1], cols_half(wo_out, c, 512))
        start(copy(22, wo_mine.at[1], cols_half(wo_out, c, 512), sib))

        far = (jnp.where(c == 0, x, 1 - x), jnp.where(c == 0, 1 - y, y), c)
        for step, k in enumerate([c, 1 - c, 2]):
            reg = pk_piece(pk_out, c, jnp.where(k == 0, cj[0], jnp.where(k == 1, cj[1], cj[2])))
            copy(16 + k, reg, reg, sib).wait_recv()
            if step == 0:
                start(copy(18, reg, reg, far))
            start(copy(19 + k, reg, reg, sib))

        piece_finish(w_in_g, 3)
        piece_total(w_in_g)
        l_wi_out = local(10, wi_mine.at[1], rows_half(wi_out, c, 512))
        start(copy(23, wi_mine.at[1], rows_half(wi_out, c, 512), sib))

        reg = pk_piece(pk_out, 1 - c, j)
        copy(15, reg, reg, sib).wait_recv()
        for k in range(3):
            reg = pk_piece(pk_out, 1 - c, cj[k])
            copy(19 + k, reg, reg, sib).wait_recv()
        reg = cols_half(wo_out, 1 - c, 512)
        copy(22, reg, reg, sib).wait_recv()
        reg = rows_half(wi_out, 1 - c, 512)
        copy(23, reg, reg, sib).wait_recv()
        for cp in sends + w_in_g["p1"][2:] + w_out_g["p1"][2:]:
            cp.wait_send()
        for cp in (l_pk_out, l_wo_out, l_wi_out):
            cp.wait()

    return pl.pallas_call(
        body, name="grads_reduce",
        out_shape=(jax.ShapeDtypeStruct((D, wi_w), F32), jax.ShapeDtypeStruct((512, D), F32),
                   jax.ShapeDtypeStruct(pack.shape, F32)),
        in_specs=[ANY] * (4 + n_dz), out_specs=(ANY,) * 3,
        scratch_shapes=[
            pltpu.VMEM((lt, 512), BF16), pltpu.VMEM((lt, 512), BF16), pltpu.VMEM((lt, wi_w), BF16),
            pltpu.VMEM((2, 512, wi_w), BF16), pltpu.VMEM((2, 512, wi_w), F32), pltpu.VMEM((4, 512, wi_w), BF16),
            pltpu.VMEM((3, 512, wi_w), BF16), pltpu.VMEM((3, 512, wi_w), BF16),
            pltpu.VMEM((lx, 512), BF16), pltpu.VMEM((lx, 512), BF16), pltpu.VMEM((lx, 512), BF16),
            pltpu.VMEM((2, 512, 512), BF16), pltpu.VMEM((2, 512, 512), F32), pltpu.VMEM((4, 512, 512), BF16),
            pltpu.VMEM((3, 512, 512), BF16), pltpu.VMEM((3, 512, 512), BF16),
            pltpu.VMEM((hp, 512), F32), pltpu.VMEM((hp, 512), F32), pltpu.VMEM((hp, 512), BF16),
            pltpu.VMEM((3, hp, 128), BF16), pltpu.VMEM((hp, 128), F32),
            pltpu.SemaphoreType.DMA((27,)), pltpu.SemaphoreType.DMA((27,)), pltpu.SemaphoreType.DMA((15,))],
        compiler_params=_cp(vmem_mb=56),
    )(hn, *dzs, y, do, pack)


def _ada_bwd(c_all, dmx_all_j, dmc_j, dmx_all, dmc, c_ctx, ada_w_full):
    def body(c_ref, dmxj_ref, dmcj_ref, dmx_ref, dmc_ref, cc_ref, w_hbm, gw_ref, gb_ref, gc_ref, lhs, rhs, dm8,
             w_v, w_sem):
        w_loads = [pltpu.make_async_copy(w_hbm.at[:, pl.ds(D * k, D)], w_v.at[k], w_sem.at[k]) for k in range(3)]
        for cp in w_loads:
            cp.start()
        lhs[...] = jnp.zeros_like(lhs)
        rhs[...] = jnp.zeros_like(rhs)
        cv = c_ref[...]
        lhs[0:8, :] = cv * _sigmoid(cv)
        cc = cc_ref[...]
        a_c, da_c = _silu_and_grad(cc)
        lhs[8:9, :] = a_c
        rhs[0:8, :] = dmxj_ref[...]
        rhs[8:9, :] = dmcj_ref[...]
        gw_ref[...] = _dot_tn(lhs[...].astype(BF16), rhs[...].astype(BF16))
        gb_ref[...] = jnp.sum(dmx_ref[...], axis=0, keepdims=True) + dmc_ref[...]
        dm8[...] = jnp.zeros_like(dm8)
        dm8[0:1, :] = dmc_ref[...]
        da = jnp.zeros((8, D), F32)
        for k in range(3):
            w_loads[k].wait()
            da = da + _dot_nt(dm8[:, D * k:D * k + D].astype(BF16), w_v[k])
        gc_ref[...] = da[0:1, :] * da_c

    return pl.pallas_call(
        body, name="ada_bwd",
        out_shape=(jax.ShapeDtypeStruct((D, 768), F32), jax.ShapeDtypeStruct((1, 3 * D), F32),
                   jax.ShapeDtypeStruct((1, D), F32)),
        in_specs=[VMEM] * 6 + [ANY], out_specs=(VMEM,) * 3,
        scratch_shapes=[pltpu.VMEM((16, D), F32), pltpu.VMEM((16, 768), F32), pltpu.VMEM((8, 3 * D), F32),
                        pltpu.VMEM((3, D, D), BF16), pltpu.SemaphoreType.DMA((3,))],
        compiler_params=_cp(vmem_mb=32),
    )(c_all, dmx_all_j, dmc_j, dmx_all, dmc, c_ctx, _in_hbm(ada_w_full))


def _proj(x, ctx, mods, norm_g, w_full, sgu, after):
    lx, lc = x.shape[0], ctx.shape[0]
    assert lc == T
    n = 1 + lx // T

    def body(x_ref, c_ref, sh_ref, sc_ref, ng_ref, w0, w1, w2, w3, w4, g_ref, b_ref, sw_ref, bt_ref, after_ref,
             hn_ref, xa_ref, ga_ref, u_ref, v_ref, gb_ref, ys_ref, mixed_s):
        i = pl.program_id(0)
        is_ctx = i == 0
        xv = jnp.where(is_ctx, c_ref[...], x_ref[...])
        sc = jnp.where(is_ctx, sc_ref[1:2, :], sc_ref[0:1, :])
        sh = jnp.where(is_ctx, sh_ref[1:2, :], sh_ref[0:1, :])
        r = lax.rsqrt(jnp.mean(xv * xv, axis=-1, keepdims=True) + NORM_EPS)
        hb = ((xv * r) * ng_ref[...] * (1.0 + sc) + sh).astype(BF16)
        hn_ref[...] = hb
        xa_ref[...] = _dot(hb, w0[...])

        @pl.when(i > 0)
        def _():
            u_ref[...] = _dot(hb, w2[...])
            v_ref[...] = _dot(hb, w3[...])
            for ch in range(T // HD):
                rows = slice(HD * ch, HD * ch + HD)
                ug = _sgu_parts(u_ref[rows, :], v_ref[rows, :], g_ref[...], b_ref[...], sw_ref, bt_ref, mixed_s)[0]
                ys_ref[rows, :] = ug * mixed_s[...]
            ga_ref[...] = _dot(hb, w1[...])
            gb_ref[...] = _dot(hb, w4[...])

    every = pl.BlockSpec((T, D), lambda i: (i, 0))
    lat = pl.BlockSpec((T, D), lambda i: (jnp.maximum(i - 1, 0), 0))
    vec = pl.BlockSpec((1, D), lambda i: (0, 0))
    in_specs = [lat, pl.BlockSpec((T, D), lambda i: (0, 0)), pl.BlockSpec((8, D), lambda i: (0, 0)),
                pl.BlockSpec((8, D), lambda i: (0, 1)), vec]
    in_specs += [pl.BlockSpec((D, D), lambda i, k=k: (0, k)) for k in range(5)]
    in_specs += [vec, vec, pl.BlockSpec((NH, HD, HD), lambda i: (0, 0, 0)), pl.BlockSpec((HD, NH), lambda i: (0, 0))]
    in_specs += [ANY]
    full_s = jax.ShapeDtypeStruct((lc + lx, D), F32)
    lat_s = jax.ShapeDtypeStruct((lx, D), F32)
    return pl.pallas_call(
        body, name="proj", grid=(n,),
        out_shape=(jax.ShapeDtypeStruct((lc + lx, D), BF16), full_s, lat_s, lat_s, lat_s, lat_s, lat_s),
        in_specs=in_specs, out_specs=(every, every, lat, lat, lat, lat, lat),
        scratch_shapes=[pltpu.VMEM((HD, D), F32)], compiler_params=_cp(1, vmem_mb=56),
    )(x, ctx, mods, mods, norm_g, *([w_full] * 5), *sgu, after)


def _tile_specs(rows_per_pos, width, n_tiles, tile):
    last = n_tiles * (T // 8) - 1
    r = rows_per_pos
    return [pl.BlockSpec((T * r, width), lambda i: (tile(i), 0)),
            pl.BlockSpec((8 * r, width), lambda i: (jnp.maximum(tile(i) * (T // 8) - 1, 0), 0)),
            pl.BlockSpec((8 * r, width), lambda i: (jnp.minimum((tile(i) + 1) * (T // 8), last), 0))]


def _has_prev(tile):
    return tile >= 2


def _has_next(tile, nt):
    return jnp.logical_and(tile >= 1, tile < nt - 1)


ZT = pl.BlockSpec((T * NH, HD), lambda i: (i, 0))
CONV_CHUNK = 32


SCAN_SUB = 8


def _scan_tile(chains, post, carry_ref):
    blk = T // SCAN_SUB

    def step(k, state):
        new = []
        for ci, (a_ref, x_ref, o_ref, q_ref, reverse, xscale) in enumerate(chains):
            for q in range(SCAN_SUB):
                s, p = state[ci * SCAN_SUB + q]
                t = (q + 1) * blk - 1 - k if reverse else q * blk + k
                r = pl.ds(_mo(t * NH, NH), NH)
                a = a_ref[r, :]
                x = x_ref[r, :] if xscale is None else x_ref[r, :] * xscale
                if post:
                    o = x + s
                    o_ref[r, :] = o
                    q_ref[r, :] = p
                    new.append((a * o, a * p))
                else:
                    o = a * s + x
                    p = a * p
                    o_ref[r, :] = o
                    q_ref[r, :] = p
                    new.append((o, p))
        return tuple(new)

    zero = jnp.zeros((NH, HD), F32)
    one = jnp.ones((NH, HD), F32)
    final = lax.fori_loop(0, blk, step, tuple((zero, one) for _ in range(len(chains) * SCAN_SUB)))
    for ci, (a_ref, x_ref, o_ref, q_ref, reverse, xscale) in enumerate(chains):
        carry = carry_ref[ci]
        for q in (range(SCAN_SUB - 1, -1, -1) if reverse else range(SCAN_SUB)):
            rows = pl.ds(q * blk * NH, blk * NH)
            fixed = o_ref[rows, :].reshape(blk, NH, HD) + q_ref[rows, :].reshape(blk, NH, HD) * carry[None]
            o_ref[rows, :] = fixed.reshape(blk * NH, HD)
            s_loc, p_loc = final[ci * SCAN_SUB + q]
            carry = s_loc + p_loc * carry
        carry_ref[ci] = carry


def _lru_fwd(xa, conv_wz, conv_bz, wcat, bcat, lamcat, name):
    lx = xa.shape[0]
    n = lx // T
    tile_u = lambda i: i
    tile_d = lambda i: jnp.where(i == 0, 0, n - i)

    def body(xm_u, xp_u, xn_u, xm_d, xp_d, xn_d, cw, cb, w_ref, b_ref, lam_ref,
             xcz_o, af_o, ab_o, hf_o, hb_o, gf_o, gb_o, fu, fd, pad, xc_d, x_u, x_d, q_u, q_d, carry):
        i = pl.program_id(0)

        @pl.when(i == 0)
        def _():
            carry[...] = jnp.zeros_like(carry)

        def conv_gates(xm, xp, xn, tile, d, xc_ref, a_ref, x_ref, g_ref):
            pmask = jnp.where(_has_prev(tile), 1.0, 0.0)
            nmask = jnp.where(_has_next(tile, n), 1.0, 0.0)
            for h in range(NH):
                cols = slice(HD * h, HD * h + HD)
                pad[_zrows(h, 8), :] = xp[:, cols] * pmask
                pad[pl.ds(8 * NH + h, T, stride=NH), :] = xm[:, cols]
                pad[pl.ds((T + 8) * NH + h, 8, stride=NH), :] = xn[:, cols] * nmask

            def conv_chunk(ci, c_):
                base = pl.multiple_of(ci * (CONV_CHUNK * NH), CONV_CHUNK * NH)
                acc = None
                for k in range(4):
                    sl = pad[pl.ds(base + (7 + k) * NH, CONV_CHUNK * NH), :].reshape(CONV_CHUNK, NH, HD)
                    term = sl * cw[k][None]
                    acc = term if acc is None else acc + term
                acc = acc + cb[...][None]
                xc_ref[pl.ds(base, CONV_CHUNK * NH), :] = acc.reshape(CONV_CHUNK * NH, HD)
                return c_
            lax.fori_loop(0, T // CONV_CHUNK, conv_chunk, 0)

            for h in range(NH):
                xch = xc_ref[_zrows(h, T), :]
                pre = _dot(xch.astype(BF16), w_ref[h, :, 256 * d:256 * d + 256]) + b_ref[h:h + 1, 256 * d:256 * d + 256]
                r, gi, _, _, a, mult = _lru_gate(pre, lam_ref[h:h + 1, :], d, 0)
                a_ref[_zrows(h, T), :] = a
                x_ref[_zrows(h, T), :] = mult * gi * xch
                for q, val in enumerate((r, gi, mult)):
                    g_ref[:, q * D + HD * h:q * D + HD * h + HD] = val

        conv_gates(xm_u, xp_u, xn_u, tile_u(i), 0, xcz_o, af_o, x_u, gf_o)
        conv_gates(xm_d, xp_d, xn_d, tile_d(i), 1, xc_d, ab_o, x_d, gb_o)

        _scan_tile([(af_o, x_u, hf_o, q_u, False, None), (ab_o, x_d, hb_o, q_d, True, None)], False, carry)

        @pl.when(i == 0)
        def _():
            fu[...] = carry[0]
            fd[...] = carry[1]

    full = lambda shape: pl.BlockSpec(shape, lambda i: (0,) * len(shape))
    st = full((NH, HD))
    in_specs = _tile_specs(1, D, n, tile_u) + _tile_specs(1, D, n, tile_d)
    in_specs += [full((4, NH, HD)), st, full((NH, HD, 4 * HD)), full((NH, 4 * HD)), full((NH, 2 * HD))]
    up = pl.BlockSpec((T * NH, HD), lambda i: (tile_u(i), 0))
    dn = pl.BlockSpec((T * NH, HD), lambda i: (tile_d(i), 0))
    zs = jax.ShapeDtypeStruct((lx * NH, HD), F32)
    ss = jax.ShapeDtypeStruct((NH, HD), F32)
    zbuf = pltpu.VMEM((T * NH, HD), F32)
    gs = jax.ShapeDtypeStruct((lx, 3 * D), F32)
    g_up = pl.BlockSpec((T, 3 * D), lambda i: (tile_u(i), 0))
    g_dn = pl.BlockSpec((T, 3 * D), lambda i: (tile_d(i), 0))
    return pl.pallas_call(
        body, name=name, grid=(n,), out_shape=(zs,) * 5 + (gs, gs, ss, ss), in_specs=in_specs,
        out_specs=(up, up, dn, up, dn, g_up, g_dn, st, st),
        scratch_shapes=[pltpu.VMEM(((T + 16) * NH, HD), F32), zbuf, zbuf, zbuf, zbuf, zbuf,
                        pltpu.VMEM((2, NH, HD), F32)],
        compiler_params=_cp(1, vmem_mb=48),
    )(xa, xa, xa, xa, xa, xa, conv_wz, conv_bz, wcat, bcat, lamcat)


def _sgu_parts(u, v, lng, lnb, w_ref, bt_ref, mixed_s):
    ug, dug = _gelu_and_grad(u)
    vg, dvg = _gelu_and_grad(v)
    mu = jnp.mean(vg, axis=-1, keepdims=True)
    vc = vg - mu
    rstd = lax.rsqrt(jnp.mean(vc * vc, axis=-1, keepdims=True) + LN_EPS)
    vh = vc * rstd
    vn = (vh * lng + lnb).astype(BF16)
    for g in range(NH):
        cols = slice(HD * g, HD * g + HD)
        mixed_s[:, cols] = _dot(w_ref[g], vn[:, cols]) + bt_ref[:, g:g + 1]
    return ug, dug, dvg, rstd, vh, vn


def _sgu_bwd_chunk(u, v, dys_v, lng, lnb, w_ref, bt_ref, mixed_s, dvn_s, dw_ref, db_ref, dg_ref, dbl_ref):
    ug, dug, dvg, rstd, vh, vn = _sgu_parts(u, v, lng, lnb, w_ref, bt_ref, mixed_s)
    du = (dys_v * mixed_s[...] * dug).astype(BF16)
    dmix = dys_v * ug
    ones = jnp.ones((8, HD), BF16)
    for g in range(NH):
        cols = slice(HD * g, HD * g + HD)
        dm = dmix[:, cols]
        hi = dm.astype(BF16)
        lo = (dm - hi.astype(F32)).astype(BF16)
        dw_ref[g] += _dot_nt(hi, vn[:, cols])
        db_ref[g:g + 1, :] += (_dot_nt(ones, hi) + _dot_nt(ones, lo))[0:1, :]
        dvn_s[:, cols] = _dot_tn(w_ref[g], hi)
    dvn = dvn_s[...]
    dg_ref[...] += jnp.sum(dvn * vh, axis=0, keepdims=True)
    dbl_ref[...] += jnp.sum(dvn, axis=0, keepdims=True)
    dvh = dvn * lng
    dvg_in = rstd * (dvh - jnp.mean(dvh, axis=-1, keepdims=True) - vh * jnp.mean(dvh * vh, axis=-1, keepdims=True))
    return du, (dvg_in * dvg).astype(BF16)


def _out_fwd_bwd(hf_z, hb_z, ga, gb, ys, x, tgt, mods, final_g, w_out_full):
    lx = x.shape[0]
    n = lx // T

    def body(hf_ref, hb_ref, ga_ref, gb_ref, ys_ref, x_ref, t_ref, gx_ref, fg_ref, w_ref,
             loss_ref, dfg_ref, dgx_ref, dxn_ref, y_ref, do_ref, dga_ref, dgb_ref, dyl_ref, dys_ref, yl_s):
        i = pl.program_id(0)

        @pl.when(i == 0)
        def _():
            loss_ref[...] = jnp.zeros_like(loss_ref)
            dfg_ref[...] = jnp.zeros_like(dfg_ref)
            dgx_ref[...] = jnp.zeros_like(dgx_ref)

        for h in range(NH):
            yl_s[:, HD * h:HD * h + HD] = hf_ref[_zrows(h, T), :] + hb_ref[_zrows(h, T), :]
        yl = yl_s[...]
        gav = ga_ref[...]
        gbv = gb_ref[...]
        sa, dsa = _silu_and_grad(gav)
        sb, dsb = _silu_and_grad(gbv)
        ysv = ys_ref[...]
        y_ref[:, 0:D] = (yl * sa).astype(BF16)
        y_ref[:, D:2 * D] = (ysv * sb).astype(BF16)
        o = _dot(y_ref[...], w_ref[...])
        gx = gx_ref[0:1, :]
        xnew = x_ref[...] + gx * o
        r2 = lax.rsqrt(jnp.mean(xnew * xnew, axis=-1, keepdims=True) + NORM_EPS)
        xh = xnew * r2
        fg = fg_ref[...]
        err = xh * fg - t_ref[...]
        loss_ref[...] += 0.5 * jnp.sum(jnp.mean(err * err, axis=-1, keepdims=True), axis=0, keepdims=True)

        @pl.when(i == n - 1)
        def _():
            lp = loss_ref[...]
            lp1 = lp.astype(BF16).astype(F32)
            lp2 = (lp - lp1).astype(BF16).astype(F32)
            lp3 = (lp - lp1 - lp2).astype(BF16).astype(F32)
            lane = lax.broadcasted_iota(jnp.int32, lp.shape, 1)
            loss_ref[...] = jnp.where(lane == 0, lp1, jnp.where(lane == 1, lp2, jnp.where(lane == 2, lp3, 0.0)))
        dout = err * (1.0 / D)
        dfg_ref[...] += jnp.sum(dout * xh, axis=0, keepdims=True)
        dxh = dout * fg
        dxn = r2 * (dxh - xh * jnp.mean(dxh * xh, axis=-1, keepdims=True))
        dxn_ref[...] = dxn
        dgx_ref[...] += jnp.sum(dxn * o, axis=0, keepdims=True)
        do = (dxn * gx).astype(BF16)
        do_ref[...] = do
        dy = _dot_nt(do, w_ref[...])
        dy1 = dy[:, 0:D]
        dy2 = dy[:, D:2 * D]
        dga_ref[...] = (dy1 * yl * dsa).astype(BF16)
        dgb_ref[...] = (dy2 * ysv * dsb).astype(BF16)
        dys_ref[...] = dy2 * sb
        yl_s[...] = dy1 * sa
        for h in range(NH):
            dyl_ref[_zrows(h, T), :] = yl_s[:, HD * h:HD * h + HD]

    row = pl.BlockSpec((T, D), lambda i: (i, 0))
    vec = pl.BlockSpec((1, D), lambda i: (0, 0))
    zlat = pl.BlockSpec((T * NH, HD), lambda i: (i + 1, 0))
    in_specs = [zlat, zlat, row, row, row, row, row, pl.BlockSpec((8, D), lambda i: (0, 2)), vec,
                pl.BlockSpec((2 * D, D), lambda i: (0, 0))]
    out_shape = (jax.ShapeDtypeStruct((1, D), F32), jax.ShapeDtypeStruct((1, D), F32), jax.ShapeDtypeStruct((1, D), F32),
                 jax.ShapeDtypeStruct((lx, D), F32), jax.ShapeDtypeStruct((lx, 2 * D), BF16),
                 jax.ShapeDtypeStruct((lx, D), BF16), jax.ShapeDtypeStruct((lx, D), BF16),
                 jax.ShapeDtypeStruct((lx, D), BF16), jax.ShapeDtypeStruct((lx * NH, HD), F32),
                 jax.ShapeDtypeStruct((lx, D), F32))
    out_specs = (vec, vec, vec, row, pl.BlockSpec((T, 2 * D), lambda i: (i, 0)),
                 row, row, row, ZT, row)
    return pl.pallas_call(
        body, name="out_fwd_bwd", grid=(n,), out_shape=out_shape, in_specs=in_specs, out_specs=out_specs,
        scratch_shapes=[pltpu.VMEM((T, D), F32)],
        compiler_params=_cp(1, vmem_mb=56),
    )(hf_z, hb_z, ga, gb, ys, x, tgt, mods, final_g, w_out_full)


def _lru_bwd(xc_z, dy_z, hf_z, hb_z, af_z, ab_z, gf, gb, s_b, wcat, lamcat, name):
    lx = xc_z.shape[0] // NH
    n = lx // T
    tile_u = lambda i: jnp.where(i == n - 1, 0, i + 1)
    tile_d = lambda i: n - 1 - i

    def body(xc_u, dy_u, hb_ref, hbn_ref, ab_ref, gb_ref, xc_d, dy_d, hf_ref, hfp_ref, af_ref, gf_ref,
             sb_ref, w_ref, lam_ref, dxcb_ref, dxcf_ref, dw_ref, db_ref, dl_ref,
             lb_s, lf_s, q_u, q_d, pf_s, pb_s, dpre_s, carry):
        i = pl.program_id(0)
        tu, td = tile_u(i), tile_d(i)

        @pl.when(i == 0)
        def _():
            dw_ref[...] = jnp.zeros_like(dw_ref)
            db_ref[...] = jnp.zeros_like(db_ref)
            dl_ref[...] = jnp.zeros_like(dl_ref)
            carry[...] = jnp.zeros_like(carry)

        _scan_tile([(ab_ref, dy_u, lb_s, q_u, False, jnp.where(tu == 0, 0.0, 1.0)),
                    (af_ref, dy_d, lf_s, q_d, True, jnp.where(td == 0, 0.0, 1.0))], True, carry)
        zero = jnp.zeros((NH, HD), F32)
        pb_s[pl.ds(0, T * NH), :] = hb_ref[...]
        pb_s[pl.ds(T * NH, NH), :] = jnp.where(tu == n - 1, sb_ref[...], jnp.where(tu == 0, zero, hbn_ref[pl.ds(0, NH), :]))
        pf_s[pl.ds(0, NH), :] = jnp.where(td == 0, zero, hfp_ref[pl.ds(7 * NH, NH), :])
        pf_s[pl.ds(NH, T * NH), :] = hf_ref[...]
        sides = ((1, xc_u, lb_s, pb_s, NH, ab_ref, gb_ref, dxcb_ref), (0, xc_d, lf_s, pf_s, 0, af_ref, gf_ref, dxcf_ref))
        for d, xc_ref, adj_s, prev_s, prev_off, a_ref, g_ref, dxc_ref in sides:
            wcols = slice(256 * d, 256 * d + 256)
            for h in range(NH):
                xch = xc_ref[_zrows(h, T), :]
                xcb = xch.astype(BF16)
                r, gi, mult = (g_ref[:, q * D + HD * h:q * D + HD * h + HD] for q in range(3))
                a = a_ref[_zrows(h, T), :]
                lam = lam_ref[h:h + 1, HD * d:HD * d + HD]
                sp = _softplus(-lam)
                du = adj_s[_zrows(h, T), :]
                da = du * prev_s[pl.ds(prev_off + h, T, stride=NH), :]
                dgi = du * mult * xch
                dmult = du * gi * xch
                dla = da * a - dmult * (a * a) / mult
                dr = dla * ((-LRU_C) * sp)
                dsp = jnp.sum(dla * ((-LRU_C) * r), axis=0, keepdims=True)
                dl_ref[h:h + 1, HD * d:HD * d + HD] += dsp * (-_sigmoid(-lam))
                dpre_s[:, 0:HD] = dr * r * (1.0 - r)
                dpre_s[:, HD:2 * HD] = dgi * gi * (1.0 - gi)
                dpre = dpre_s[...]
                dpb = dpre.astype(BF16)
                dw_ref[h, :, wcols] += _dot_tn(xcb, dpb)
                db_ref[h:h + 1, wcols] += jnp.sum(dpre, axis=0, keepdims=True)
                dxc_ref[_zrows(h, T), :] = du * mult * gi + _dot_nt(dpb, w_ref[h, :, wcols])

    full = lambda shape: pl.BlockSpec(shape, lambda i: (0,) * len(shape))
    wsp, bsp, lsp = full((NH, HD, 4 * HD)), full((NH, 4 * HD)), full((NH, 2 * HD))
    st = full((NH, HD))
    up = pl.BlockSpec((T * NH, HD), lambda i: (tile_u(i), 0))
    dn = pl.BlockSpec((T * NH, HD), lambda i: (tile_d(i), 0))
    dy_up = pl.BlockSpec((T * NH, HD), lambda i: (jnp.maximum(tile_u(i) - 1, 0), 0))
    dy_dn = pl.BlockSpec((T * NH, HD), lambda i: (jnp.maximum(tile_d(i) - 1, 0), 0))
    nxt = _tile_specs(NH, HD, n, tile_u)[2]
    prv = _tile_specs(NH, HD, n, tile_d)[1]
    g_up = pl.BlockSpec((T, 3 * D), lambda i: (tile_u(i), 0))
    g_dn = pl.BlockSpec((T, 3 * D), lambda i: (tile_d(i), 0))
    zs = jax.ShapeDtypeStruct((lx * NH, HD), F32)
    zbuf = pltpu.VMEM((T * NH, HD), F32)
    zbuf1 = pltpu.VMEM(((T + 1) * NH, HD), F32)
    return pl.pallas_call(
        body, name=name, grid=(n,),
        out_shape=(zs, zs, jax.ShapeDtypeStruct((NH, HD, 4 * HD), F32), jax.ShapeDtypeStruct((NH, 4 * HD), F32),
                   jax.ShapeDtypeStruct((NH, 2 * HD), F32)),
        in_specs=[up, dy_up, up, nxt, up, g_up, dn, dy_dn, dn, prv, dn, g_dn, st, wsp, lsp],
        out_specs=(up, dn, wsp, bsp, lsp),
        scratch_shapes=[zbuf, zbuf, zbuf, zbuf, zbuf1, zbuf1, pltpu.VMEM((T, 2 * HD), F32),
                        pltpu.VMEM((2, NH, HD), F32)],
        compiler_params=_cp(1, vmem_mb=56),
    )(xc_z, dy_z, hb_z, hb_z, ab_z, gb, xc_z, dy_z, hf_z, hf_z, af_z, gf, s_b, wcat, lamcat)


def _conv_bwd(dxc_a, dxc_b, xa, conv_wz, dcw0, dcb0, name):
    lx = dxc_a.shape[0] // NH
    n = lx // T

    def body(dm_a, dp_a, dn_a, dm_b, dp_b, dn_b, xan_ref, cw, dcw0_ref, dcb0_ref, dxa_ref, dcw_ref, dcb_ref,
             pad, dxa_s, xa_ref):
        i = pl.program_id(0)

        @pl.when(i == 0)
        def _():
            dcw_ref[...] = dcw0_ref[...]
            dcb_ref[...] = dcb0_ref[...]

        for h in range(NH):
            xa_ref[_zrows(h, T), :] = xan_ref[:, HD * h:HD * h + HD]

        pmask = jnp.where(_has_prev(i), 1.0, 0.0)
        nmask = jnp.where(_has_next(i, n), 1.0, 0.0)
        pad[pl.ds(0, 8 * NH), :] = (dp_a[...] + dp_b[...]) * pmask
        pad[pl.ds(8 * NH, T * NH), :] = dm_a[...] + dm_b[...]
        pad[pl.ds((T + 8) * NH, 8 * NH), :] = (dn_a[...] + dn_b[...]) * nmask

        def chunk(ci, carry):
            base = pl.multiple_of(ci * (CONV_CHUNK * NH), CONV_CHUNK * NH)
            xav = xa_ref[pl.ds(base, CONV_CHUNK * NH), :].reshape(CONV_CHUNK, NH, HD)
            acc = None
            for k in range(4):
                sl = pad[pl.ds(base + (9 - k) * NH, CONV_CHUNK * NH), :].reshape(CONV_CHUNK, NH, HD)
                term = sl * cw[k][None]
                acc = term if acc is None else acc + term
                dcw_ref[k] += jnp.sum(sl * xav, axis=0)
                if k == 1:
                    dcb_ref[...] += jnp.sum(sl, axis=0)
            dxa_s[pl.ds(base, CONV_CHUNK * NH), :] = acc.reshape(CONV_CHUNK * NH, HD)
            return carry
        lax.fori_loop(0, T // CONV_CHUNK, chunk, 0)
        for h in range(NH):
            dxa_ref[:, HD * h:HD * h + HD] = dxa_s[_zrows(h, T), :].astype(BF16)

    full = lambda shape: pl.BlockSpec(shape, lambda i: (0,) * len(shape))
    return pl.pallas_call(
        body, name=name, grid=(n,),
        out_shape=(jax.ShapeDtypeStruct((lx, D), BF16), jax.ShapeDtypeStruct((4, NH, HD), F32),
                   jax.ShapeDtypeStruct((NH, HD), F32)),
        in_specs=_tile_specs(NH, HD, n, lambda i: i) * 2 + [pl.BlockSpec((T, D), lambda i: (i, 0)), full((4, NH, HD)),
                                                            full((4, NH, HD)), full((NH, HD))],
        out_specs=(pl.BlockSpec((T, D), lambda i: (i, 0)), full((4, NH, HD)), full((NH, HD))),
        scratch_shapes=[pltpu.VMEM(((T + 16) * NH, HD), F32), pltpu.VMEM((T * NH, HD), F32),
                        pltpu.VMEM((T * NH, HD), F32)],
        compiler_params=_cp(1, vmem_mb=48),
    )(dxc_a, dxc_a, dxc_a, dxc_b, dxc_b, dxc_b, xa, conv_wz, dcw0, dcb0)


def _proj_bwd(dxa, dga, dgb, x, ctx, dxn, mods, norm_g, w_full, sgu):
    lx, lc = x.shape[0], ctx.shape[0]
    assert lc == T
    n = 1 + lx // T

    def body(dxa_ref, dga_ref, dgb_ref, w0, w1, w4, w2, w3, x_ref, c_ref, sc_ref, ng_ref, dxn_ref,
             u_ref, v_ref, dy_ref, g_ref, b_ref, sw_ref, bt_ref,
             gx_ref, dng_ref, dscx_ref, dshx_ref, dscc_ref, dshc_ref, du_ref, dv_ref, dws_ref, dbs_ref, dlg_ref, dlb_ref,
             mixed_s, dvn_s):
        i = pl.program_id(0)
        is_ctx = i == 0

        @pl.when(is_ctx)
        def _():
            for acc in (dng_ref, dscx_ref, dshx_ref, dscc_ref, dshc_ref, dws_ref, dbs_ref, dlg_ref, dlb_ref):
                acc[...] = jnp.zeros_like(acc)

        xv = jnp.where(is_ctx, c_ref[...], x_ref[...])
        sc1 = 1.0 + jnp.where(is_ctx, sc_ref[1:2, :], sc_ref[0:1, :])
        r = lax.rsqrt(jnp.mean(xv * xv, axis=-1, keepdims=True) + NORM_EPS)
        xn = xv * r
        ng = ng_ref[...]

        def norm_bwd(dhn, dsc_ref, dsh_ref, with_x):
            t = dhn * xn
            dng_ref[...] += jnp.sum(t * sc1, axis=0, keepdims=True)
            dsc_ref[...] += jnp.sum(t * ng, axis=0, keepdims=True)
            dsh_ref[...] += jnp.sum(dhn, axis=0, keepdims=True)
            if with_x:
                dxh = dhn * (ng * sc1)
                gx_ref[...] = dxn_ref[...] + r * (dxh - xn * jnp.mean(dxh * xn, axis=-1, keepdims=True))

        @pl.when(is_ctx)
        def _():
            norm_bwd(_dot_nt(dxa_ref[...], w0[...]), dscc_ref, dshc_ref, False)

        @pl.when(i > 0)
        def _():
            def sgu_chunk(ch):
                rows = slice(HD * ch, HD * ch + HD)
                du, dv = _sgu_bwd_chunk(u_ref[rows, :], v_ref[rows, :], dy_ref[rows, :], g_ref[...], b_ref[...],
                                        sw_ref, bt_ref, mixed_s, dvn_s, dws_ref, dbs_ref, dlg_ref, dlb_ref)
                du_ref[rows, :] = du
                dv_ref[rows, :] = dv

            dhn = _dot_nt(dxa_ref[...], w0[...])
            sgu_chunk(0)
            dhn = dhn + _dot_nt(dga_ref[...], w1[...])
            for ch in range(1, T // HD):
                sgu_chunk(ch)
            dhn = dhn + _dot_nt(dgb_ref[...], w4[...])
            dhn = dhn + _dot_nt(du_ref[...], w2[...]) + _dot_nt(dv_ref[...], w3[...])
            norm_bwd(dhn, dscx_ref, dshx_ref, True)

    every = pl.BlockSpec((T, D), lambda i: (i, 0))
    lat = pl.BlockSpec((T, D), lambda i: (jnp.maximum(i - 1, 0), 0))
    vec = pl.BlockSpec((1, D), lambda i: (0, 0))
    wsp = pl.BlockSpec((NH, HD, HD), lambda i: (0, 0, 0))
    bsp = pl.BlockSpec((NH, HD), lambda i: (0, 0))
    in_specs = [every, lat, lat] + [pl.BlockSpec((D, D), lambda i, k=k: (0, k)) for k in (0, 1, 4, 2, 3)]
    in_specs += [lat, pl.BlockSpec((T, D), lambda i: (0, 0)), pl.BlockSpec((8, D), lambda i: (0, 1)), vec, lat]
    in_specs += [lat, lat, lat, vec, vec, wsp, pl.BlockSpec((HD, NH), lambda i: (0, 0))]
    vs = jax.ShapeDtypeStruct((1, D), F32)
    zb = jax.ShapeDtypeStruct((lx, D), BF16)
    return pl.pallas_call(
        body, name="proj_bwd", grid=(n,),
        out_shape=(jax.ShapeDtypeStruct((lx, D), F32), vs, vs, vs, vs, vs, zb, zb,
                   jax.ShapeDtypeStruct((NH, HD, HD), F32), jax.ShapeDtypeStruct((NH, HD), F32), vs, vs),
        in_specs=in_specs, out_specs=(lat, vec, vec, vec, vec, vec, lat, lat, wsp, bsp, vec, vec),
        scratch_shapes=[pltpu.VMEM((HD, D), F32), pltpu.VMEM((HD, D), F32)], compiler_params=_cp(1, vmem_mb=56),
    )(dxa, dga, dgb, *([w_full] * 5), x, ctx, mods, norm_g, dxn, *sgu)


def _adam_math(w, g, m, v):
    m = ADAM_B1 * m + (1.0 - ADAM_B1) * g
    v = ADAM_B2 * v + (1.0 - ADAM_B2) * (g * g)
    m_hat = m / (1.0 - ADAM_B1 ** ADAM_STEP)
    v_hat = v / (1.0 - ADAM_B2 ** ADAM_STEP)
    delta = -ADAM_LR * (m_hat / (jnp.sqrt(v_hat) + ADAM_EPS) + ADAM_WD * w)
    return delta, m, v


def _adam_big(w, g, m, v, name):
    rows, cols = w.shape
    tr = 256

    def body(w_ref, g_ref, m_ref, v_ref, d_o, m_o, v_o):
        d, mm, vv = _adam_math(w_ref[...], g_ref[...], m_ref[...], v_ref[...])
        d_o[...] = d
        m_o[...] = mm
        v_o[...] = vv

    blk = pl.BlockSpec((tr, cols), lambda i: (i, 0))
    s = jax.ShapeDtypeStruct((rows, cols), F32)
    return pl.pallas_call(
        body, name=name, grid=(rows // tr,), out_shape=(s, s, s), in_specs=[blk] * 4, out_specs=(blk,) * 3,
        compiler_params=_cp(1, vmem_mb=48),
    )(w, g, m, v)


def _adam_small(items, tot):
    ni = len(items)
    pieces = [it[1] if isinstance(it[1], list) else None for it in items]
    flat = [a for it, pc in zip(items, pieces) for a in ((it[0], it[2], it[3]) if pc is not None else it)]
    n_in = len(flat) + 1
    out_shape = tuple(jax.ShapeDtypeStruct(it[0].shape, F32) for it, pc in zip(items, pieces)
                      for _ in range(4 if pc is not None else 3))
    n_out = len(out_shape)
    n_loads = sum(3 + (len(pc) if pc is not None else 1) for pc in pieces)

    def body(*refs):
        ins, tot_ref, outs = refs[:n_in - 1], refs[n_in - 1], refs[n_in:n_in + n_out]
        bufs = refs[n_in + n_out:n_in + n_out + 7 * ni]
        sem_in, sem_out = refs[n_in + n_out + 7 * ni:]
        loads, q_in, q_sem = [], 0, 0
        for k, pc in enumerate(pieces):
            w_b, g_b, m_b, v_b = bufs[7 * k:7 * k + 4]
            srcs = [(ins[q_in], w_b)]
            if pc is None:
                srcs.append((ins[q_in + 1], g_b))
                q_in += 1
            else:
                srcs += [(tot_ref.at[pl.ds(r0, nr), pl.ds(c0, nc)], g_b.at[pl.ds(d0, nr), :]) for r0, nr, c0, nc, d0 in pc]
            srcs += [(ins[q_in + 1], m_b), (ins[q_in + 2], v_b)]
            q_in += 3
            mine = []
            for src, dst in srcs:
                mine.append(pltpu.make_async_copy(src, dst, sem_in.at[q_sem]))
                q_sem += 1
            loads.append(mine)
        for mine in loads:
            for cp in mine:
                cp.start()
        stores, q_out = [], 0
        for k, pc in enumerate(pieces):
            for cp in loads[k]:
                cp.wait()
            w_b, g_b, m_b, v_b = bufs[7 * k:7 * k + 4]
            res = _adam_math(w_b[...], g_b[...], m_b[...], v_b[...])
            srcs = []
            for q in range(3):
                bufs[7 * k + 4 + q][...] = res[q]
                srcs.append(bufs[7 * k + 4 + q])
            if pc is not None:
                srcs.append(g_b)
            for src in srcs:
                cp = pltpu.make_async_copy(src, outs[q_out], sem_out.at[q_out])
                cp.start()
                stores.append(cp)
                q_out += 1
        for cp in stores:
            cp.wait()

    scratch = [pltpu.VMEM(it[0].shape, F32) for it in items for _ in range(7)]
    scratch += [pltpu.SemaphoreType.DMA((n_loads,)), pltpu.SemaphoreType.DMA((n_out,))]
    res = pl.pallas_call(
        body, name="adam_small", out_shape=out_shape, in_specs=[HBM] * n_in, out_specs=(HBM,) * n_out,
        scratch_shapes=scratch, compiler_params=_cp(vmem_mb=40),
    )(*[_in_hbm(a) for a in flat], _in_hbm(tot))
    outs, q = [], 0
    for pc in pieces:
        outs.append(tuple(res[q:q + 3]) + ((res[q + 3],) if pc is not None else (None,)))
        q += 4 if pc is not None else 3
    return outs


def kernel(x, c, ctx, c_ctx, ada_w, ada_b, norm_g, w_in, conv_w, conv_b, lru_wa, lru_ba, lru_wx, lru_bx, lru_lambda, sgu_ln_g, sgu_ln_b, sgu_w, sgu_b, w_out, final_g, loss_target, m_c_ctx, m_ada_w, m_ada_b, m_norm_g, m_w_in, m_conv_w, m_conv_b, m_lru_wa, m_lru_ba, m_lru_wx, m_lru_bx, m_lru_lambda, m_sgu_ln_g, m_sgu_ln_b, m_sgu_w, m_sgu_b, m_w_out, m_final_g, v_c_ctx, v_ada_w, v_ada_b, v_norm_g, v_w_in, v_conv_w, v_conv_b, v_lru_wa, v_lru_ba, v_lru_wx, v_lru_bx, v_lru_lambda, v_sgu_ln_g, v_sgu_ln_b, v_sgu_w, v_sgu_b, v_w_out, v_final_g):
    ix, iy, ic = lax.axis_index("x"), lax.axis_index("y"), lax.axis_index("c")
    chip = 2 * ix + iy
    dev = 2 * chip + ic
    lx = x.shape[1]
    lc = ctx.shape[1]

    smalls = jnp.concatenate([conv_w[0], lru_lambda[0], jnp.zeros((10, 256), F32)], axis=0)
    c_ctx2 = c_ctx.reshape(1, D)
    mods, c_slots, _, w_in_full, wo_land, ada_land, wcat, bcat, sgu_wb, conv_wz, lamcat, sgu_bt = _gather_in(
        c, c_ctx2, ada_w[0], ada_b, w_in[0], w_out[0], smalls,
        (lru_wa[0], lru_wx[0], lru_ba[0], lru_bx[0], sgu_w[0], sgu_b[0]))
    wo_ss, wo_rs, ada_ss, ada_rs, wo_land, ada_land, token = _late_gather_start(wo_land, ada_land)
    conv_bz = conv_b.reshape(NH, HD)
    final_g2 = final_g.reshape(1, D)

    zero_s = jnp.zeros((NH, HD), F32)
    hn, xa_all, ga, u, v, gb, ys = _proj(x[0], ctx[0], mods, norm_g, w_in_full, (sgu_ln_g, sgu_ln_b, sgu_wb, sgu_bt),
                                         token)
    xcz, af, ab, hf, hb, gf, gb_l, _, hb0 = _lru_fwd(xa_all, conv_wz, conv_bz, wcat, bcat, lamcat, "lru_fwd")

    w_out_full = _late_gather_wait(wo_land, wo_ss, wo_rs, "w_out", hf, "late_gather_wait_w_out")
    (loss_part, dfg, dgx, dxn, y, do, dga, dgb, dyl_z, dys) = _out_fwd_bwd(
        hf, hb, ga, gb, ys, x[0], loss_target[0], mods, final_g2, w_out_full)

    dxc_b, dxc_f, dwc, dbc, dlc = _lru_bwd(xcz, dyl_z, hf, hb, af, ab, gf, gb_l, hb0, wcat, lamcat, "lru_bwd")
    dxa, dcw, dcb = _conv_bwd(dxc_b, dxc_f, xa_all, conv_wz, jnp.zeros((4, NH, HD), F32), zero_s, "conv_bwd")
    dxa = _in_hbm(dxa)

    grad_x, dng, dsc_x, dsh_x, dsc_c, dsh_c, du, dv, d_sgu_w, d_sgu_b, d_ln_g, d_ln_b = _proj_bwd(
        dxa, dga, dgb, x[0], ctx[0], dxn, mods, norm_g, w_in_full, (u, v, dys, sgu_ln_g, sgu_ln_b, sgu_wb, sgu_bt))
    dzs = [dxa, dga, du, dv, dgb]

    dmx = jnp.concatenate([dsh_x, dsc_x, dgx], axis=0)
    dmc = jnp.concatenate([dsh_c, dsc_c, jnp.zeros((1, D), F32)], axis=0)
    slot = jnp.concatenate([dmx, loss_part], axis=0)
    slots = lax.dynamic_update_slice(jnp.zeros((32, D), F32), slot, (4 * dev, 0))
    vecs = jnp.concatenate([dfg, dng, dcb.reshape(1, D), d_ln_g, d_ln_b, dcw.reshape(4, D), dmc,
                            jnp.zeros((4, D), F32), slots], axis=0)
    d_sgu_w4 = d_sgu_w.reshape(4, 256, HD).transpose(1, 0, 2).reshape(256, 4 * HD)
    pad8 = lambda a: jnp.pad(a, ((0, 8 - a.shape[0]), (0, 4 * HD - a.shape[1])))
    pack = jnp.concatenate([dwc.reshape(NH * HD, 4 * HD), pad8(dbc), pad8(dlc), d_sgu_w4, pad8(d_sgu_b),
                            vecs.reshape(96, 4 * HD), jnp.zeros((8, 4 * HD), F32)], axis=0)
    g_w_in, g_w_out, tot = _grads_reduce(hn, dzs, lc, y, do, _in_hbm(pack))

    n_w = NH * HD
    g_lru_wa = [(0, n_w, 2 * HD * d, HD, n_w * d) for d in range(2)]
    g_lru_wx = [(0, n_w, 2 * HD * d + HD, HD, n_w * d) for d in range(2)]
    g_lru_ba = [(n_w, NH, 2 * HD * d, HD, NH * d) for d in range(2)]
    g_lru_bx = [(n_w, NH, 2 * HD * d + HD, HD, NH * d) for d in range(2)]
    g_sgu_w = [(1040, 256, HD * q, HD, 256 * q) for q in range(4)]
    g_sgu_b = [(1296, NH, 0, HD, 0)]
    g_lc = tot[1032:1040, 0:2 * HD]
    tv = tot[1304:1400].reshape(48, D)
    g_final_g, g_norm_g, g_conv_b, g_ln_g, g_ln_b = tv[0:1], tv[1:2], tv[2:3], tv[3:4], tv[4:5]
    g_conv_w_full = tv[5:9]
    dmc_tot = tv[9:12].reshape(1, 3 * D)
    slots_all = tv[16:48].reshape(8, 4, D)
    dmx_all = slots_all[:, 0:3, :].reshape(8, 3 * D)
    c_all = c_slots.reshape(8, 8, D)[:, 0, :]
    g_lam_full = jnp.stack([g_lc[:, 0:HD], g_lc[:, HD:2 * HD]]).reshape(2, D)
    g_conv_w = lax.dynamic_slice(g_conv_w_full, (0, 256 * chip), (4, 256))
    g_lam = lax.dynamic_slice(g_lam_full, (0, 256 * chip), (2, 256))
    dmx_all_j = lax.dynamic_slice(dmx_all, (0, 768 * chip), (8, 768))
    dmc_j = lax.dynamic_slice(dmc_tot, (0, 768 * chip), (1, 768))
    ada_full = _late_gather_wait(ada_land, ada_ss, ada_rs, "ada_w", _in_hbm(tot), "late_gather_wait_ada_w")
    g_ada_w, g_ada_b, g_c_ctx = _ada_bwd(c_all, dmx_all_j, dmc_j, dmx_all, dmc_tot, c_ctx2, ada_full)

    big = {
        "ada_w": _adam_big(ada_w[0], g_ada_w, m_ada_w[0], v_ada_w[0], "adam_ada_w"),
        "w_in": _adam_big(w_in[0], g_w_in, m_w_in[0], v_w_in[0], "adam_w_in"),
        "w_out": _adam_big(w_out[0], g_w_out, m_w_out[0], v_w_out[0], "adam_w_out"),
    }
    small_in = {
        "c_ctx": (c_ctx, g_c_ctx, m_c_ctx, v_c_ctx, (1, D)),
        "ada_b": (ada_b, g_ada_b, m_ada_b, v_ada_b, (1, 3 * D)),
        "norm_g": (norm_g, g_norm_g, m_norm_g, v_norm_g, (1, D)),
        "conv_w": (conv_w, g_conv_w, m_conv_w, v_conv_w, (4, 256)),
        "conv_b": (conv_b, g_conv_b, m_conv_b, v_conv_b, (1, D)),
        "lru_wa": (lru_wa, g_lru_wa, m_lru_wa, v_lru_wa, (2 * NH * HD, HD)),
        "lru_ba": (lru_ba, g_lru_ba, m_lru_ba, v_lru_ba, (2 * NH, HD)),
        "lru_wx": (lru_wx, g_lru_wx, m_lru_wx, v_lru_wx, (2 * NH * HD, HD)),
        "lru_bx": (lru_bx, g_lru_bx, m_lru_bx, v_lru_bx, (2 * NH, HD)),
        "lru_lambda": (lru_lambda, g_lam, m_lru_lambda, v_lru_lambda, (2, 256)),
        "sgu_ln_g": (sgu_ln_g, g_ln_g, m_sgu_ln_g, v_sgu_ln_g, (1, D)),
        "sgu_ln_b": (sgu_ln_b, g_ln_b, m_sgu_ln_b, v_sgu_ln_b, (1, D)),
        "sgu_w": (sgu_w, g_sgu_w, m_sgu_w, v_sgu_w, (NH * HD, HD)),
        "sgu_b": (sgu_b, g_sgu_b, m_sgu_b, v_sgu_b, (NH, HD)),
        "final_g": (final_g, g_final_g, m_final_g, v_final_g, (1, D)),
    }
    names_small = list(small_in)
    res_small = _adam_small([tuple(a if isinstance(a, list) else a.reshape(small_in[k][4]) for a in small_in[k][:4])
                             for k in names_small], tot)
    full_shapes = {"ada_w": ada_w.shape, "w_in": w_in.shape, "w_out": w_out.shape}
    grads, deltas, new_m, new_v = {}, {}, {}, {}
    for k in ("ada_w", "w_in", "w_out"):
        g = {"ada_w": g_ada_w, "w_in": g_w_in, "w_out": g_w_out}[k]
        grads[k] = g.reshape(full_shapes[k])
        deltas[k], new_m[k], new_v[k] = (a.reshape(full_shapes[k]) for a in big[k])
    for k, res in zip(names_small, res_small):
        shape = small_in[k][0].shape
        grads[k] = (small_in[k][1] if res[3] is None else res[3]).reshape(shape)
        deltas[k], new_m[k], new_v[k] = (a.reshape(shape) for a in res[:3])

    loss = jnp.sum(slots_all[:, 3, 0:3])
    order = ["c_ctx", "ada_w", "ada_b", "norm_g", "w_in", "conv_w", "conv_b", "lru_wa", "lru_ba", "lru_wx", "lru_bx",
             "lru_lambda", "sgu_ln_g", "sgu_ln_b", "sgu_w", "sgu_b", "w_out", "final_g"]
    return (loss, grad_x.reshape(x.shape), *[grads[k] for k in order], *[deltas[k] for k in order],
            *[new_m[k] for k in order], *[new_v[k] for k in order])
```

```python
import jax
import jax.numpy as jnp
from jax import lax
from jax.experimental import pallas as pl
from jax.experimental.pallas import tpu as pltpu

F32 = jnp.float32
BF16 = jnp.bfloat16

D = 1024
NH = 8
HD = 128
NCHIP = 4
T = 256
NORM_EPS = 1e-6
LN_EPS = 1e-5
LRU_C = 8.0
ADAM_LR = 0.001
ADAM_B1 = 0.9
ADAM_B2 = 0.999
ADAM_EPS = 1e-08
ADAM_WD = 0.01
ADAM_STEP = 10

VMEM = pl.BlockSpec(memory_space=pltpu.VMEM)
ANY = pl.BlockSpec(memory_space=pl.ANY)
MESH = pl.DeviceIdType.MESH


def _cp(n_grid=0, vmem_mb=None):
    kw = {}
    if n_grid:
        kw["dimension_semantics"] = ("arbitrary",) * n_grid
    if vmem_mb:
        kw["vmem_limit_bytes"] = vmem_mb << 20
    return pltpu.CompilerParams(**kw)


def _sigmoid(x):
    return 0.5 * jnp.tanh(0.5 * x) + 0.5


def _silu_and_grad(x):
    s = _sigmoid(x)
    return x * s, s * (1.0 + x * (1.0 - s))


_GELU_K = 0.7978845608028654
_GELU_C = 0.044715


def _gelu_and_grad(x):
    x2 = x * x
    th = jnp.tanh(x * (_GELU_K + (_GELU_K * _GELU_C) * x2))
    p = 0.5 + 0.5 * th
    g = x * p
    dg = p + g * (1.0 - th) * (_GELU_K + (3.0 * _GELU_K * _GELU_C) * x2)
    return g, dg


def _softplus(x):
    return jnp.maximum(x, 0.0) + jnp.log1p(jnp.exp(-jnp.abs(x)))


def _lru_gate(pre, lam_row, d, off=None):
    off = 256 * d if off is None else off
    r = _sigmoid(pre[:, off:off + HD])
    gi = _sigmoid(pre[:, off + HD:off + 2 * HD])
    lam = lam_row[:, HD * d:HD * d + HD]
    sp = _softplus(-lam)
    la = (-LRU_C) * r * sp
    a = jnp.exp(la)
    x2 = 2.0 * la
    m2 = jnp.where(x2 > -1e-3, -x2 * (1.0 + 0.5 * x2), 1.0 - a * a)
    mult = jnp.sqrt(m2)
    return r, gi, lam, sp, a, mult


def _dot(a, b):
    return jnp.dot(a, b, preferred_element_type=F32)


def _dot_tn(a, b):
    return lax.dot_general(a, b, (((0,), (0,)), ((), ())), preferred_element_type=F32)


def _dot_nt(a, b):
    return lax.dot_general(a, b, (((1,), (1,)), ((), ())), preferred_element_type=F32)


def _mo(v, m):
    return v if isinstance(v, int) else pl.multiple_of(v, m)


def _zrows(h, n):
    return pl.ds(h, n, stride=NH)


def _gather_in(c, c_ctx, ada_w, ada_b, w_in, w_out, smalls, lru_sgu):
    nch = [1, 4]
    wrows = lambda cc, q: (pl.ds(_mo(512 * cc, 16), 512) if q is None
                           else pl.ds(_mo(512 * cc + (512 // nch[1]) * q, 16), 512 // nch[1]))
    specs = [
        ((64, 256), F32, lambda r, jj, cc, q=None: r.at[pl.ds(_mo(16 * jj + 8 * cc, 8), 8), :]),
        ((D, 5120), BF16, lambda r, jj, cc, q=None: r.at[wrows(cc, q), pl.ds(_mo(1280 * jj, 128), 1280)]),
    ]
    halves = [lambda r, cc, q=None: r.at[pl.ds(_mo(8 * cc, 8), 8), :],
              lambda r, cc, q=None: r.at[wrows(cc, q), :]]
    na = len(specs)
    sem_base = [0, 6 * nch[0]]
    sidx = lambda a, q, k: sem_base[a] + 6 * q + k
    n_tiny = 6 * sum(nch)
    n_sem = n_tiny + 10

    def body(c_ref, cc_ref, ada_ref, adab_ref, win_ref, wout_ref, sm_ref, wa_ref, wx_ref, ba_ref, bx_ref, sw_ref, sb_ref,
             mods_o, call_o, sm_o, win_o, wol_o, adal_o, wcat_o, bcat_o, swb_o, cwz_o, lam_o, sbt_o,
             s_win, s_ada, s_wout, f_win, f_ada, f_wout, cslot, lhs, mbuf,
             send_sems, recv_sems, local_sems, load_sems, f_w, f_sw, s_wcat, s_swb, prep_sems, f_sb):
        x, y, c = lax.axis_index("x"), lax.axis_index("y"), lax.axis_index("c")
        j = 2 * x + y
        dev = 2 * j + c
        sib = (x, y, 1 - c)
        chips = [(1 - x, y), (x, 1 - y), (1 - x, 1 - y)]
        cj = [2 * cx + cy for cx, cy in chips]
        outs = [sm_o, win_o]
        srcs = [sm_ref, s_win]

        def copy(idx, src, dst, to):
            return pltpu.make_async_remote_copy(src_ref=src, dst_ref=dst, send_sem=send_sems.at[idx],
                                                recv_sem=recv_sems.at[idx], device_id=to, device_id_type=MESH)

        sends = []

        def start(cp):
            cp.start()
            sends.append(cp)

        cslot[...] = jnp.zeros_like(cslot)
        cslot[0:1, :] = c_ref[...]
        my_slot = pl.ds(_mo(8 * dev, 8), 8)
        others = [sib] + [(*chips[k], c) for k in range(3)] + [(*chips[k], 1 - c) for k in range(3)]
        other_dev = [dev + 1 - 2 * c] + [2 * cj[k] + c for k in range(3)] + [2 * cj[k] + 1 - c for k in range(3)]
        base = n_tiny
        for r in range(7):
            start(copy(base + r, cslot, call_o.at[my_slot, :], others[r]))
        call_o[my_slot, :] = cslot[...]

        crow = 512 // nch[1]
        loads = []
        for cc in (c, 1 - c):
            for q in range(nch[1]):
                rows = pl.ds(_mo(512 * cc + crow * q, 16), crow)
                loads.append(pltpu.make_async_copy(win_ref.at[rows, :], f_win.at[rows, :], load_sems.at[len(loads)]))
        loads.append(pltpu.make_async_copy(ada_ref, f_ada, load_sems.at[len(loads)]))
        loads.append(pltpu.make_async_copy(wout_ref, f_wout, load_sems.at[len(loads)]))
        for ld in loads:
            ld.start()
        prep = []
        for d in range(2):
            for q, (w_src, b_src) in enumerate([(wa_ref, ba_ref), (wx_ref, bx_ref)]):
                prep.append(pltpu.make_async_copy(w_src.at[d], f_w.at[2 * d + q], prep_sems.at[len(prep)]))
                prep.append(pltpu.make_async_copy(b_src.at[d], bcat_o.at[:, pl.ds(HD * (2 * d + q), HD)],
                                                  prep_sems.at[len(prep)]))
        prep.append(pltpu.make_async_copy(sw_ref, f_sw, prep_sems.at[len(prep)]))
        prep.append(pltpu.make_async_copy(sb_ref, f_sb, prep_sems.at[len(prep)]))
        for cp in prep:
            cp.start()
        for k in range(2):
            start(copy(sidx(0, 0, k), halves[0](srcs[0], c), specs[0][2](outs[0], j, c), (*chips[k], c)))
        for q in range(nch[1]):
            loads[q].wait()
            rows = pl.ds(_mo(512 * c + crow * q, 16), crow)
            s_win[rows, :] = f_win[rows, :].astype(BF16)
            for k in range(2):
                start(copy(sidx(1, q, k), halves[1](s_win, c, q), specs[1][2](win_o, j, c, q), (*chips[k], c)))
        for q in range(nch[1]):
            loads[nch[1] + q].wait()
            rows = pl.ds(_mo(512 * (1 - c) + crow * q, 16), crow)
            s_win[rows, :] = f_win[rows, :].astype(BF16)
        local = []
        for a in range(na):
            for cc in range(2):
                lc = pltpu.make_async_copy(halves[a](srcs[a], cc), specs[a][2](outs[a], j, cc), local_sems.at[2 * a + cc])
                lc.start()
                local.append(lc)
        loads[2 * nch[1]].wait()
        s_ada[...] = f_ada[...].astype(BF16)
        loads[2 * nch[1] + 1].wait()
        s_wout[...] = f_wout[...].astype(BF16)
        for q, (src, dst) in enumerate([(s_wout, wol_o.at[pl.ds(_mo(512 * j, 16), 512), :]),
                                        (s_ada, adal_o.at[:, pl.ds(_mo(768 * j, 128), 768)])]):
            lc = pltpu.make_async_copy(src, dst, local_sems.at[2 * na + q])
            lc.start()
            local.append(lc)
        for cp in prep:
            cp.wait()
        for q in range(4):
            s_wcat[:, :, HD * q:HD * q + HD] = f_w[q].astype(BF16)
        s_swb[...] = f_sw[...].astype(BF16)
        sbt_o[...] = f_sb[...].T
        for n, (src, dst) in enumerate([(s_wcat, wcat_o), (s_swb, swb_o)]):
            lc = pltpu.make_async_copy(src, dst, prep_sems.at[len(prep) + n])
            lc.start()
            local.append(lc)

        for r in range(7):
            slot = call_o.at[pl.ds(_mo(8 * other_dev[r], 8), 8), :]
            copy(base + r, slot, slot, sib).wait_recv()
        lhs[...] = jnp.zeros_like(lhs)
        for b in range(8):
            cv = call_o[8 * b:8 * b + 1, :]
            lhs[b:b + 1, :] = cv * _sigmoid(cv)
        cv = cc_ref[...]
        lhs[8:9, :] = cv * _sigmoid(cv)
        adab = adab_ref[:, 0:768]
        for jj in range(1, NCHIP):
            adab = jnp.where(j == jj, adab_ref[:, 768 * jj:768 * jj + 768], adab)
        mbuf[j] = _dot(lhs[...].astype(BF16), s_ada[...]) + adab
        for k in range(3):
            start(copy(base + 7 + k, mbuf.at[j], mbuf.at[j], (*chips[k], c)))
        for k in range(3):
            copy(base + 7 + k, mbuf.at[cj[k]], mbuf.at[cj[k]], sib).wait_recv()
        mods_o[...] = jnp.zeros_like(mods_o)
        for jj in range(NCHIP):
            mods_o[0:1, 768 * jj:768 * jj + 768] = mbuf[jj, pl.ds(dev, 1), :]
            mods_o[1:2, 768 * jj:768 * jj + 768] = mbuf[jj, 8:9, :]

        kx = [1 - x, x, 1 - x]
        ky = [y, 1 - y, 1 - y]
        pick = lambda k, lst: jnp.where(k == 0, lst[0], jnp.where(k == 1, lst[1], lst[2]))
        for a in range(na):
            for q in range(nch[a]):
                for step, k in enumerate([c, 1 - c]):
                    reg = specs[a][2](outs[a], pick(k, cj), c, q)
                    copy(sidx(a, q, k), reg, reg, sib).wait_recv()
                    if step == 0:
                        start(copy(sidx(a, q, 2), reg, reg, (pick(1 - c, kx), pick(1 - c, ky), c)))
                    start(copy(sidx(a, q, 3 + k), reg, reg, sib))
        for a in range(na):
            for q in range(nch[a]):
                reg = specs[a][2](outs[a], cj[2], c, q)
                copy(sidx(a, q, 2), reg, reg, sib).wait_recv()
                start(copy(sidx(a, q, 5), reg, reg, sib))
        for a in range(na):
            for q in range(nch[a]):
                for k in range(3):
                    reg = specs[a][2](outs[a], cj[k], 1 - c, q)
                    copy(sidx(a, q, 3 + k), reg, reg, sib).wait_recv()
        for cp in sends:
            cp.wait_send()
        for lc in local:
            lc.wait()
        for jj in range(NCHIP):
            for mh in range(2):
                h = 2 * jj + mh
                cols = slice(HD * mh, HD * mh + HD)
                for r in range(4):
                    cwz_o[r, h:h + 1, :] = sm_o[16 * jj + r:16 * jj + r + 1, cols]
                for d in range(2):
                    lam_o[h:h + 1, HD * d:HD * d + HD] = sm_o[16 * jj + 4 + d:16 * jj + 5 + d, cols]

    out_shape = (jax.ShapeDtypeStruct((8, 3 * D), F32), jax.ShapeDtypeStruct((64, D), F32),
                 jax.ShapeDtypeStruct(specs[0][0], F32), jax.ShapeDtypeStruct(specs[1][0], BF16),
                 jax.ShapeDtypeStruct((2048, D), BF16), jax.ShapeDtypeStruct((D, 3 * D), BF16),
                 jax.ShapeDtypeStruct((NH, HD, 4 * HD), BF16), jax.ShapeDtypeStruct((NH, 4 * HD), F32),
                 jax.ShapeDtypeStruct((NH, HD, HD), BF16), jax.ShapeDtypeStruct((4, NH, HD), F32),
                 jax.ShapeDtypeStruct((NH, 2 * HD), F32), jax.ShapeDtypeStruct((HD, NH), F32))
    return pl.pallas_call(
        body, name="gather_in", out_shape=out_shape,
        in_specs=[VMEM, VMEM, ANY, VMEM, ANY, ANY, VMEM] + [ANY] * 6,
        out_specs=(VMEM, VMEM, VMEM, ANY, ANY, ANY, ANY, ANY, ANY, VMEM, VMEM, VMEM),
        scratch_shapes=[pltpu.VMEM((D, 1280), BF16), pltpu.VMEM((D, 768), BF16), pltpu.VMEM((512, D), BF16),
                        pltpu.VMEM((D, 1280), F32), pltpu.VMEM((D, 768), F32), pltpu.VMEM((512, D), F32),
                        pltpu.VMEM((8, D), F32), pltpu.VMEM((16, D), F32), pltpu.VMEM((NCHIP, 16, 768), F32),
                        pltpu.SemaphoreType.DMA((n_sem,)), pltpu.SemaphoreType.DMA((n_sem,)),
                        pltpu.SemaphoreType.DMA((2 * na + 2,)), pltpu.SemaphoreType.DMA((2 * nch[1] + 2,)),
                        pltpu.VMEM((4, NH, HD, HD), F32), pltpu.VMEM((NH, HD, HD), F32),
                        pltpu.VMEM((NH, HD, 4 * HD), BF16), pltpu.VMEM((NH, HD, HD), BF16),
                        pltpu.SemaphoreType.DMA((12,)), pltpu.VMEM((NH, HD), F32)],
        compiler_params=_cp(vmem_mb=56),
    )(c, c_ctx, ada_w, ada_b, w_in, w_out, smalls, *[_in_hbm(a) for a in lru_sgu])


HBM = pl.BlockSpec(memory_space=pltpu.HBM)
SEM = pl.BlockSpec(memory_space=pltpu.SEMAPHORE)


def _in_hbm(a):
    return pltpu.with_memory_space_constraint(a, pltpu.HBM)


def _late_gather_regions(x, y, c):
    chips = [(1 - x, y), (x, 1 - y), (1 - x, 1 - y)]
    wo_reg = lambda r, jj, cc: r.at[pl.ds(_mo(512 * jj + 256 * cc, 16), 256), :]
    ada_reg = lambda r, jj, cc: r.at[pl.ds(_mo(512 * cc, 16), 512), pl.ds(_mo(768 * jj, 128), 768)]
    return chips, wo_reg, ada_reg


def _late_gather_start(wo_land, ada_land):
    def body(wol_ref, adal_ref, wo_ss, wo_rs, ada_ss, ada_rs, wol_thru, adal_thru, token):
        x, y, c = lax.axis_index("x"), lax.axis_index("y"), lax.axis_index("c")
        j = 2 * x + y
        chips, wo_reg, ada_reg = _late_gather_regions(x, y, c)
        for k in range(3):
            for cc in range(2):
                pltpu.make_async_remote_copy(src_ref=wo_reg(wol_ref, j, c), dst_ref=wo_reg(wol_ref, j, c),
                                             send_sem=wo_ss.at[2 * k + cc], recv_sem=wo_rs.at[2 * k + c],
                                             device_id=(*chips[k], cc), device_id_type=MESH).start()
        for k in range(3):
            for cc in range(2):
                pltpu.make_async_remote_copy(src_ref=ada_reg(adal_ref, j, c), dst_ref=ada_reg(adal_ref, j, c),
                                             send_sem=ada_ss.at[2 * k + cc], recv_sem=ada_rs.at[2 * k + c],
                                             device_id=(*chips[k], cc), device_id_type=MESH).start()
        token[...] = jnp.zeros_like(token)

    sems = pltpu.SemaphoreType.DMA((6,))
    return pl.pallas_call(
        body, name="late_gather_start",
        out_shape=(sems, sems, sems, sems, pltpu.HBM(wo_land.shape, BF16), pltpu.HBM(ada_land.shape, BF16),
                   jax.ShapeDtypeStruct((8, 128), F32)),
        in_specs=(HBM, HBM), out_specs=(SEM, SEM, SEM, SEM, HBM, HBM, VMEM), input_output_aliases={0: 4, 1: 5},
        compiler_params=pltpu.CompilerParams(has_side_effects=pltpu.SideEffectType.DATAFLOW_SIDE_EFFECTING),
    )(_in_hbm(wo_land), _in_hbm(ada_land))


def _late_gather_wait(land, send_sems, recv_sems, which, after, name):
    def body(land_ref, ss, rs, after_ref, land_out):
        x, y, c = lax.axis_index("x"), lax.axis_index("y"), lax.axis_index("c")
        j = 2 * x + y
        chips, wo_reg, ada_reg = _late_gather_regions(x, y, c)
        reg = wo_reg if which == "w_out" else ada_reg
        for k in range(3):
            kj = 2 * chips[k][0] + chips[k][1]
            for cc in range(2):
                cp = pltpu.make_async_remote_copy(src_ref=reg(land_ref, j, c), dst_ref=reg(land_ref, kj, cc),
                                                  send_sem=ss.at[2 * k + cc], recv_sem=rs.at[2 * k + cc],
                                                  device_id=(*chips[k], cc), device_id_type=MESH)
                cp.wait_send()
                cp.wait_recv()

    return pl.pallas_call(
        body, name=name, out_shape=pltpu.HBM(land.shape, land.dtype),
        in_specs=(HBM, SEM, SEM, ANY), out_specs=HBM, input_output_aliases={0: 0},
        compiler_params=pltpu.CompilerParams(has_side_effects=pltpu.SideEffectType.DATAFLOW_SIDE_EFFECTING),
    )(land, send_sems, recv_sems, after)


RCHUNK = 16


def _grads_reduce(hn, dzs, lc, y, do, pack):
    rp = pack.shape[0]
    hp = rp // 2
    assert hp % RCHUNK == 0
    wi_w = 1280
    lx = hn.shape[0] - lc
    lt = lx + lc
    n_dz = len(dzs)

    def body(*refs):
        hn_hbm, dz_hbm = refs[0], refs[1:1 + n_dz]
        y_hbm, do_hbm, pk_hbm, wi_out, wo_out, pk_out = refs[1 + n_dz:7 + n_dz]
        (hn_mine, hn_other, dzbuf, wi_other, wi_mine, wi_recv, wi_send, wi_rb,
         y_blk, do_mine, do_other, wo_other, wo_mine, wo_recv, wo_send, wo_rb,
         pk_mine, pk_recv, pk_send, pk_rb, pk_own, send_sems, recv_sems, local_sems) = refs[7 + n_dz:]
        x, y, c = lax.axis_index("x"), lax.axis_index("y"), lax.axis_index("c")
        j = 2 * x + y
        sib = (x, y, 1 - c)
        chips = [(1 - x, y), (x, 1 - y), (1 - x, 1 - y)]
        cj = [2 * cx + cy for cx, cy in chips]
        near = (jnp.where(c == 0, 1 - x, x), jnp.where(c == 0, y, 1 - y), c)
        slabs = [cj[2], cj[0], cj[1], j]

        def copy(k, src, dst, to):
            return pltpu.make_async_remote_copy(src_ref=src, dst_ref=dst, send_sem=send_sems.at[k],
                                                recv_sem=recv_sems.at[k], device_id=to, device_id_type=MESH)

        def local(k, src, dst):
            cp = pltpu.make_async_copy(src, dst, local_sems.at[k])
            cp.start()
            return cp

        rows_half = lambda r, cc, n: r.at[pl.ds(_mo(cc * n, 16), n), :]
        cols_half = lambda r, cc, n: r.at[:, pl.ds(_mo(cc * n, 128), n)]
        pk_piece = lambda r, cc, jj: r.at[pl.ds(_mo(cc * hp, 16), hp), pl.ds(_mo(jj * 128, 128), 128)]

        sends = []

        def start(cp):
            cp.start()
            sends.append(cp)

        def dz_pieces(s):
            g0 = wi_w * s
            k0, off0 = g0 // D, g0 % D
            w0 = min(D - off0, wi_w)
            pieces = [(k0, off0, w0, 0)]
            if w0 < wi_w:
                pieces.append((k0 + 1, 0, wi_w - w0, w0))
            return pieces

        def dz_copies(s):
            cps = []
            for q, (k, off, w, dst) in enumerate(dz_pieces(s)):
                cps.append(pltpu.make_async_copy(dz_hbm[k].at[pl.ds(lc if k == 0 else 0, lx), pl.ds(off, w)],
                                                 dzbuf.at[pl.ds(lc, lx), pl.ds(dst, w)], local_sems.at[11 + q]))
            if s == 0:
                cps.append(pltpu.make_async_copy(dz_hbm[0].at[pl.ds(0, lc), :], dzbuf.at[pl.ds(0, lc), pl.ds(0, D)],
                                                 local_sems.at[13]))
            return cps

        def dz_load(sl):
            for s in range(NCHIP):
                @pl.when(sl == s)
                def _():
                    if s == 0:
                        dzbuf[pl.ds(0, lc), pl.ds(D, wi_w - D)] = jnp.zeros((lc, wi_w - D), BF16)
                    else:
                        dzbuf[pl.ds(0, lc), :] = jnp.zeros((lc, wi_w), BF16)
                    for cp in dz_copies(s):
                        cp.start()

        def dz_wait(sl):
            for s in range(NCHIP):
                @pl.when(sl == s)
                def _():
                    for cp in dz_copies(s):
                        cp.wait()

        l_pk = local(0, rows_half(pk_hbm, c, hp), pk_mine)
        start(copy(0, rows_half(pk_hbm, 1 - c, hp), pk_recv, sib))
        col = lambda r, cc: r.at[:, pl.ds(_mo(cc * 512, 128), 512)]
        do_loads = [local(7, col(do_hbm, c), do_mine), local(14, col(do_hbm, 1 - c), do_other)]
        hn_loads = [local(2, col(hn_hbm, c), hn_mine), local(4, col(hn_hbm, 1 - c), hn_other)]
        y_copy = lambda s: pltpu.make_async_copy(col(y_hbm, slabs[s]), y_blk, local_sems.at[1])
        y_copy(0).start()
        dz_load(slabs[0])

        def pair_sum(mine, recv, send, nrows, keep, relayed=None):
            def step(i, carry):
                rows = pl.ds(_mo(i * RCHUNK, RCHUNK), RCHUNK)
                s = mine[rows, :] + recv[rows, :].astype(F32)
                if relayed is not None:
                    s = s + relayed[rows, :].astype(F32)
                if keep:
                    mine[rows, :] = s
                if send is not None:
                    send[rows, :] = s.astype(BF16)
                return carry
            lax.fori_loop(0, nrows // RCHUNK, step, 0)

        def chip_sum(own, rb, nrows, terms=(0, 1, 2)):
            def step(i, carry):
                rows = pl.ds(_mo(i * RCHUNK, RCHUNK), RCHUNK)
                acc = own[rows, :]
                for q in terms:
                    acc = acc + rb[q, rows, :].astype(F32)
                own[rows, :] = acc
                return carry
            lax.fori_loop(0, nrows // RCHUNK, step, 0)

        w_in_g = dict(other=wi_other, mine=wi_mine, recv=wi_recv, send=wi_send, rb=wi_rb, p1_sems=(2, 3, 4, 5), p2_sem=12,
                      p1=[None] * NCHIP, wait_load=lambda s: dz_wait(slabs[s]), load=lambda s: dz_load(slabs[s]),
                      dot_other=lambda: _dot_tn(hn_other[...], dzbuf[...]), dot_mine=lambda: _dot_tn(hn_mine[...], dzbuf[...]))
        w_out_g = dict(other=wo_other, mine=wo_mine, recv=wo_recv, send=wo_send, rb=wo_rb, p1_sems=(1, 24, 25, 26), p2_sem=9,
                       p1=[None] * NCHIP, wait_load=lambda s: y_copy(s).wait(), load=lambda s: y_copy(s).start(),
                       dot_other=lambda: _dot_tn(y_blk[...], do_other[...]), dot_mine=lambda: _dot_tn(y_blk[...], do_mine[...]))

        def piece_matmuls(g, s):
            if s >= 2:
                g["p1"][s - 2].wait_send()
            g["wait_load"](s)
            g["other"][s % 2] = g["dot_other"]().astype(BF16)
            g["p1"][s] = copy(g["p1_sems"][s], g["other"].at[s % 2], g["recv"].at[s], sib)
            g["p1"][s].start()
            g["mine"][s % 2] = g["dot_mine"]()
            if s + 1 < NCHIP:
                g["load"](s + 1)

        def piece_finish(g, s):
            mine, recv, send, rb, p2 = g["mine"].at[s % 2], g["recv"].at[s], g["send"], g["rb"], g["p2_sem"]
            nrows = mine.shape[0]
            copy(g["p1_sems"][s], recv, recv, sib).wait_recv()
            if s == 3:
                pair_sum(mine, recv, None, nrows, True)
                return
            if s == 0:
                pair_sum(mine, recv, send.at[0], nrows, False)
                start(copy(p2, send.at[0], rb.at[0], near))
                return
            adds_relayed = c == (1 if s == 1 else 0)

            @pl.when(adds_relayed)
            def _():
                copy(p2, rb.at[0], rb.at[0], sib).wait_recv()
                pair_sum(mine, recv, send.at[s], nrows, False, rb.at[0])

            @pl.when(jnp.logical_not(adds_relayed))
            def _():
                pair_sum(mine, recv, send.at[s], nrows, False)
            start(copy(p2 + s, send.at[s], rb.at[s], (*chips[s - 1], c)))

        def piece_total(g):
            for k in (1, 2):
                copy(g["p2_sem"] + k, g["rb"].at[k], g["rb"].at[k], sib).wait_recv()
            chip_sum(g["mine"].at[1], g["rb"], g["mine"].shape[1], (1, 2))

        for cp in do_loads:
            cp.wait()
        piece_matmuls(w_out_g, 0)
        piece_matmuls(w_out_g, 1)
        piece_finish(w_out_g, 0)
        piece_matmuls(w_out_g, 2)
        piece_finish(w_out_g, 1)
        piece_matmuls(w_out_g, 3)
        piece_finish(w_out_g, 2)

        for cp in hn_loads:
            cp.wait()
        piece_matmuls(w_in_g, 0)

        l_pk.wait()
        copy(0, pk_recv, pk_recv, sib).wait_recv()
        pair_sum(pk_mine, pk_recv, pk_send, hp, True)
        for k in range(3):
            start(copy(6 + k, pk_send.at[:, pl.ds(_mo(cj[k] * 128, 128), 128)], pk_rb.at[k], (*chips[k], c)))
        l_pk_own = local(6, pk_mine.at[:, pl.ds(_mo(j * 128, 128), 128)], pk_own)

        piece_matmuls(w_in_g, 1)
        piece_finish(w_in_g, 0)

        l_pk_own.wait()
        for k in range(3):
            copy(6 + k, pk_rb.at[k], pk_rb.at[k], sib).wait_recv()
        chip_sum(pk_own, pk_rb, hp)
        l_pk_out = local(8, pk_own, pk_piece(pk_out, c, j))
        start(copy(15, pk_own, pk_piece(pk_out, c, j), sib))
        for k in range(2):
            start(copy(16 + k, pk_own, pk_piece(pk_out, c, j), (*chips[k], c)))

        piece_matmuls(w_in_g, 2)
        piece_finish(w_in_g, 1)
        piece_matmuls(w_in_g, 3)
        piece_finish(w_in_g, 2)

        piece_finish(w_out_g, 3)
        piece_total(w_out_g)
        l_wo_out = local(9, wo_mine.at[1], cols_half(wo_out, c, 512))
        start(copy(22, wo_mine.at[1], cols_half(wo_out, c, 512), sib))

        far = (jnp.where(c == 0, x, 1 - x), jnp.where(c == 0, 1 - y, y), c)
        for step, k in enumerate([c, 1 - c, 2]):
            reg = pk_piece(pk_out, c, jnp.where(k == 0, cj[0], jnp.where(k == 1, cj[1], cj[2])))
            copy(16 + k, reg, reg, sib).wait_recv()
            if step == 0:
                start(copy(18, reg, reg, far))
            start(copy(19 + k, reg, reg, sib))

        piece_finish(w_in_g, 3)
        piece_total(w_in_g)
        l_wi_out = local(10, wi_mine.at[1], rows_half(wi_out, c, 512))
        start(copy(23, wi_mine.at[1], rows_half(wi_out, c, 512), sib))

        reg = pk_piece(pk_out, 1 - c, j)
        copy(15, reg, reg, sib).wait_recv()
        for k in range(3):
            reg = pk_piece(pk_out, 1 - c, cj[k])
            copy(19 + k, reg, reg, sib).wait_recv()
        reg = cols_half(wo_out, 1 - c, 512)
        copy(22, reg, reg, sib).wait_recv()
        reg = rows_half(wi_out, 1 - c, 512)
        copy(23, reg, reg, sib).wait_recv()
        for cp in sends + w_in_g["p1"][2:] + w_out_g["p1"][2:]:
            cp.wait_send()
        for cp in (l_pk_out, l_wo_out, l_wi_out):
            cp.wait()

    return pl.pallas_call(
        body, name="grads_reduce",
        out_shape=(jax.ShapeDtypeStruct((D, wi_w), F32), jax.ShapeDtypeStruct((512, D), F32),
                   jax.ShapeDtypeStruct(pack.shape, F32)),
        in_specs=[ANY] * (4 + n_dz), out_specs=(ANY,) * 3,
        scratch_shapes=[
            pltpu.VMEM((lt, 512), BF16), pltpu.VMEM((lt, 512), BF16), pltpu.VMEM((lt, wi_w), BF16),
            pltpu.VMEM((2, 512, wi_w), BF16), pltpu.VMEM((2, 512, wi_w), F32), pltpu.VMEM((4, 512, wi_w), BF16),
            pltpu.VMEM((3, 512, wi_w), BF16), pltpu.VMEM((3, 512, wi_w), BF16),
            pltpu.VMEM((lx, 512), BF16), pltpu.VMEM((lx, 512), BF16), pltpu.VMEM((lx, 512), BF16),
            pltpu.VMEM((2, 512, 512), BF16), pltpu.VMEM((2, 512, 512), F32), pltpu.VMEM((4, 512, 512), BF16),
            pltpu.VMEM((3, 512, 512), BF16), pltpu.VMEM((3, 512, 512), BF16),
            pltpu.VMEM((hp, 512), F32), pltpu.VMEM((hp, 512), F32), pltpu.VMEM((hp, 512), BF16),
            pltpu.VMEM((3, hp, 128), BF16), pltpu.VMEM((hp, 128), F32),
            pltpu.SemaphoreType.DMA((27,)), pltpu.SemaphoreType.DMA((27,)), pltpu.SemaphoreType.DMA((15,))],
        compiler_params=_cp(vmem_mb=56),
    )(hn, *dzs, y, do, pack)


def _ada_bwd(c_all, dmx_all_j, dmc_j, dmx_all, dmc, c_ctx, ada_w_full):
    def body(c_ref, dmxj_ref, dmcj_ref, dmx_ref, dmc_ref, cc_ref, w_hbm, gw_ref, gb_ref, gc_ref, lhs, rhs, dm8,
             w_v, w_sem):
        w_loads = [pltpu.make_async_copy(w_hbm.at[:, pl.ds(D * k, D)], w_v.at[k], w_sem.at[k]) for k in range(3)]
        for cp in w_loads:
            cp.start()
        lhs[...] = jnp.zeros_like(lhs)
        rhs[...] = jnp.zeros_like(rhs)
        cv = c_ref[...]
        lhs[0:8, :] = cv * _sigmoid(cv)
        cc = cc_ref[...]
        a_c, da_c = _silu_and_grad(cc)
        lhs[8:9, :] = a_c
        rhs[0:8, :] = dmxj_ref[...]
        rhs[8:9, :] = dmcj_ref[...]
        gw_ref[...] = _dot_tn(lhs[...].astype(BF16), rhs[...].astype(BF16))
        gb_ref[...] = jnp.sum(dmx_ref[...], axis=0, keepdims=True) + dmc_ref[...]
        dm8[...] = jnp.zeros_like(dm8)
        dm8[0:1, :] = dmc_ref[...]
        da = jnp.zeros((8, D), F32)
        for k in range(3):
            w_loads[k].wait()
            da = da + _dot_nt(dm8[:, D * k:D * k + D].astype(BF16), w_v[k])
        gc_ref[...] = da[0:1, :] * da_c

    return pl.pallas_call(
        body, name="ada_bwd",
        out_shape=(jax.ShapeDtypeStruct((D, 768), F32), jax.ShapeDtypeStruct((1, 3 * D), F32),
                   jax.ShapeDtypeStruct((1, D), F32)),
        in_specs=[VMEM] * 6 + [ANY], out_specs=(VMEM,) * 3,
        scratch_shapes=[pltpu.VMEM((16, D), F32), pltpu.VMEM((16, 768), F32), pltpu.VMEM((8, 3 * D), F32),
                        pltpu.VMEM((3, D, D), BF16), pltpu.SemaphoreType.DMA((3,))],
        compiler_params=_cp(vmem_mb=32),
    )(c_all, dmx_all_j, dmc_j, dmx_all, dmc, c_ctx, _in_hbm(ada_w_full))


def _proj(x, ctx, mods, norm_g, w_full, sgu, after):
    lx, lc = x.shape[0], ctx.shape[0]
    assert lc == T
    n = 1 + lx // T

    def body(x_ref, c_ref, sh_ref, sc_ref, ng_ref, w0, w1, w2, w3, w4, g_ref, b_ref, sw_ref, bt_ref, after_ref,
             hn_ref, xa_ref, ga_ref, u_ref, v_ref, gb_ref, ys_ref, mixed_s):
        i = pl.program_id(0)
        is_ctx = i == 0
        xv = jnp.where(is_ctx, c_ref[...], x_ref[...])
        sc = jnp.where(is_ctx, sc_ref[1:2, :], sc_ref[0:1, :])
        sh = jnp.where(is_ctx, sh_ref[1:2, :], sh_ref[0:1, :])
        r = lax.rsqrt(jnp.mean(xv * xv, axis=-1, keepdims=True) + NORM_EPS)
        hb = ((xv * r) * ng_ref[...] * (1.0 + sc) + sh).astype(BF16)
        hn_ref[...] = hb
        xa_ref[...] = _dot(hb, w0[...])

        @pl.when(i > 0)
        def _():
            u_ref[...] = _dot(hb, w2[...])
            v_ref[...] = _dot(hb, w3[...])
            for ch in range(T // HD):
                rows = slice(HD * ch, HD * ch + HD)
                ug = _sgu_parts(u_ref[rows, :], v_ref[rows, :], g_ref[...], b_ref[...], sw_ref, bt_ref, mixed_s)[0]
                ys_ref[rows, :] = ug * mixed_s[...]
            ga_ref[...] = _dot(hb, w1[...])
            gb_ref[...] = _dot(hb, w4[...])

    every = pl.BlockSpec((T, D), lambda i: (i, 0))
    lat = pl.BlockSpec((T, D), lambda i: (jnp.maximum(i - 1, 0), 0))
    vec = pl.BlockSpec((1, D), lambda i: (0, 0))
    in_specs = [lat, pl.BlockSpec((T, D), lambda i: (0, 0)), pl.BlockSpec((8, D), lambda i: (0, 0)),
                pl.BlockSpec((8, D), lambda i: (0, 1)), vec]
    in_specs += [pl.BlockSpec((D, D), lambda i, k=k: (0, k)) for k in range(5)]
    in_specs += [vec, vec, pl.BlockSpec((NH, HD, HD), lambda i: (0, 0, 0)), pl.BlockSpec((HD, NH), lambda i: (0, 0))]
    in_specs += [ANY]
    full_s = jax.ShapeDtypeStruct((lc + lx, D), F32)
    lat_s = jax.ShapeDtypeStruct((lx, D), F32)
    return pl.pallas_call(
        body, name="proj", grid=(n,),
        out_shape=(jax.ShapeDtypeStruct((lc + lx, D), BF16), full_s, lat_s, lat_s, lat_s, lat_s, lat_s),
        in_specs=in_specs, out_specs=(every, every, lat, lat, lat, lat, lat),
        scratch_shapes=[pltpu.VMEM((HD, D), F32)], compiler_params=_cp(1, vmem_mb=56),
    )(x, ctx, mods, mods, norm_g, *([w_full] * 5), *sgu, after)


def _tile_specs(rows_per_pos, width, n_tiles, tile):
    last = n_tiles * (T // 8) - 1
    r = rows_per_pos
    return [pl.BlockSpec((T * r, width), lambda i: (tile(i), 0)),
            pl.BlockSpec((8 * r, width), lambda i: (jnp.maximum(tile(i) * (T // 8) - 1, 0), 0)),
            pl.BlockSpec((8 * r, width), lambda i: (jnp.minimum((tile(i) + 1) * (T // 8), last), 0))]


def _has_prev(tile):
    return tile >= 2


def _has_next(tile, nt):
    return jnp.logical_and(tile >= 1, tile < nt - 1)


ZT = pl.BlockSpec((T * NH, HD), lambda i: (i, 0))
CONV_CHUNK = 32


SCAN_SUB = 8


def _scan_tile(chains, post, carry_ref):
    blk = T // SCAN_SUB

    def step(k, state):
        new = []
        for ci, (a_ref, x_ref, o_ref, q_ref, reverse, xscale) in enumerate(chains):
            for q in range(SCAN_SUB):
                s, p = state[ci * SCAN_SUB + q]
                t = (q + 1) * blk - 1 - k if reverse else q * blk + k
                r = pl.ds(_mo(t * NH, NH), NH)
                a = a_ref[r, :]
                x = x_ref[r, :] if xscale is None else x_ref[r, :] * xscale
                if post:
                    o = x + s
                    o_ref[r, :] = o
                    q_ref[r, :] = p
                    new.append((a * o, a * p))
                else:
                    o = a * s + x
                    p = a * p
                    o_ref[r, :] = o
                    q_ref[r, :] = p
                    new.append((o, p))
        return tuple(new)

    zero = jnp.zeros((NH, HD), F32)
    one = jnp.ones((NH, HD), F32)
    final = lax.fori_loop(0, blk, step, tuple((zero, one) for _ in range(len(chains) * SCAN_SUB)))
    for ci, (a_ref, x_ref, o_ref, q_ref, reverse, xscale) in enumerate(chains):
        carry = carry_ref[ci]
        for q in (range(SCAN_SUB - 1, -1, -1) if reverse else range(SCAN_SUB)):
            rows = pl.ds(q * blk * NH, blk * NH)
            fixed = o_ref[rows, :].reshape(blk, NH, HD) + q_ref[rows, :].reshape(blk, NH, HD) * carry[None]
            o_ref[rows, :] = fixed.reshape(blk * NH, HD)
            s_loc, p_loc = final[ci * SCAN_SUB + q]
            carry = s_loc + p_loc * carry
        carry_ref[ci] = carry


def _lru_fwd(xa, conv_wz, conv_bz, wcat, bcat, lamcat, name):
    lx = xa.shape[0]
    n = lx // T
    tile_u = lambda i: i
    tile_d = lambda i: jnp.where(i == 0, 0, n - i)

    def body(xm_u, xp_u, xn_u, xm_d, xp_d, xn_d, cw, cb, w_ref, b_ref, lam_ref,
             xcz_o, af_o, ab_o, hf_o, hb_o, gf_o, gb_o, fu, fd, pad, xc_d, x_u, x_d, q_u, q_d, carry):
        i = pl.program_id(0)

        @pl.when(i == 0)
        def _():
            carry[...] = jnp.zeros_like(carry)

        def conv_gates(xm, xp, xn, tile, d, xc_ref, a_ref, x_ref, g_ref):
            pmask = jnp.where(_has_prev(tile), 1.0, 0.0)
            nmask = jnp.where(_has_next(tile, n), 1.0, 0.0)
            for h in range(NH):
                cols = slice(HD * h, HD * h + HD)
                pad[_zrows(h, 8), :] = xp[:, cols] * pmask
                pad[pl.ds(8 * NH + h, T, stride=NH), :] = xm[:, cols]
                pad[pl.ds((T + 8) * NH + h, 8, stride=NH), :] = xn[:, cols] * nmask

            def conv_chunk(ci, c_):
                base = pl.multiple_of(ci * (CONV_CHUNK * NH), CONV_CHUNK * NH)
                acc = None
                for k in range(4):
                    sl = pad[pl.ds(base + (7 + k) * NH, CONV_CHUNK * NH), :].reshape(CONV_CHUNK, NH, HD)
                    term = sl * cw[k][None]
                    acc = term if acc is None else acc + term
                acc = acc + cb[...][None]
                xc_ref[pl.ds(base, CONV_CHUNK * NH), :] = acc.reshape(CONV_CHUNK * NH, HD)
                return c_
            lax.fori_loop(0, T // CONV_CHUNK, conv_chunk, 0)

            for h in range(NH):
                xch = xc_ref[_zrows(h, T), :]
                pre = _dot(xch.astype(BF16), w_ref[h, :, 256 * d:256 * d + 256]) + b_ref[h:h + 1, 256 * d:256 * d + 256]
                r, gi, _, _, a, mult = _lru_gate(pre, lam_ref[h:h + 1, :], d, 0)
                a_ref[_zrows(h, T), :] = a
                x_ref[_zrows(h, T), :] = mult * gi * xch
                for q, val in enumerate((r, gi, mult)):
                    g_ref[:, q * D + HD * h:q * D + HD * h + HD] = val

        conv_gates(xm_u, xp_u, xn_u, tile_u(i), 0, xcz_o, af_o, x_u, gf_o)
        conv_gates(xm_d, xp_d, xn_d, tile_d(i), 1, xc_d, ab_o, x_d, gb_o)

        _scan_tile([(af_o, x_u, hf_o, q_u, False, None), (ab_o, x_d, hb_o, q_d, True, None)], False, carry)

        @pl.when(i == 0)
        def _():
            fu[...] = carry[0]
            fd[...] = carry[1]

    full = lambda shape: pl.BlockSpec(shape, lambda i: (0,) * len(shape))
    st = full((NH, HD))
    in_specs = _tile_specs(1, D, n, tile_u) + _tile_specs(1, D, n, tile_d)
    in_specs += [full((4, NH, HD)), st, full((NH, HD, 4 * HD)), full((NH, 4 * HD)), full((NH, 2 * HD))]
    up = pl.BlockSpec((T * NH, HD), lambda i: (tile_u(i), 0))
    dn = pl.BlockSpec((T * NH, HD), lambda i: (tile_d(i), 0))
    zs = jax.ShapeDtypeStruct((lx * NH, HD), F32)
    ss = jax.ShapeDtypeStruct((NH, HD), F32)
    zbuf = pltpu.VMEM((T * NH, HD), F32)
    gs = jax.ShapeDtypeStruct((lx, 3 * D), F32)
    g_up = pl.BlockSpec((T, 3 * D), lambda i: (tile_u(i), 0))
    g_dn = pl.BlockSpec((T, 3 * D), lambda i: (tile_d(i), 0))
    return pl.pallas_call(
        body, name=name, grid=(n,), out_shape=(zs,) * 5 + (gs, gs, ss, ss), in_specs=in_specs,
        out_specs=(up, up, dn, up, dn, g_up, g_dn, st, st),
        scratch_shapes=[pltpu.VMEM(((T + 16) * NH, HD), F32), zbuf, zbuf, zbuf, zbuf, zbuf,
                        pltpu.VMEM((2, NH, HD), F32)],
        compiler_params=_cp(1, vmem_mb=48),
    )(xa, xa, xa, xa, xa, xa, conv_wz, conv_bz, wcat, bcat, lamcat)


def _sgu_parts(u, v, lng, lnb, w_ref, bt_ref, mixed_s):
    ug, dug = _gelu_and_grad(u)
    vg, dvg = _gelu_and_grad(v)
    mu = jnp.mean(vg, axis=-1, keepdims=True)
    vc = vg - mu
    rstd = lax.rsqrt(jnp.mean(vc * vc, axis=-1, keepdims=True) + LN_EPS)
    vh = vc * rstd
    vn = (vh * lng + lnb).astype(BF16)
    for g in range(NH):
        cols = slice(HD * g, HD * g + HD)
        mixed_s[:, cols] = _dot(w_ref[g], vn[:, cols]) + bt_ref[:, g:g + 1]
    return ug, dug, dvg, rstd, vh, vn


def _sgu_bwd_chunk(u, v, dys_v, lng, lnb, w_ref, bt_ref, mixed_s, dvn_s, dw_ref, db_ref, dg_ref, dbl_ref):
    ug, dug, dvg, rstd, vh, vn = _sgu_parts(u, v, lng, lnb, w_ref, bt_ref, mixed_s)
    du = (dys_v * mixed_s[...] * dug).astype(BF16)
    dmix = dys_v * ug
    ones = jnp.ones((8, HD), BF16)
    for g in range(NH):
        cols = slice(HD * g, HD * g + HD)
        dm = dmix[:, cols]
        hi = dm.astype(BF16)
        lo = (dm - hi.astype(F32)).astype(BF16)
        dw_ref[g] += _dot_nt(hi, vn[:, cols])
        db_ref[g:g + 1, :] += (_dot_nt(ones, hi) + _dot_nt(ones, lo))[0:1, :]
        dvn_s[:, cols] = _dot_tn(w_ref[g], hi)
    dvn = dvn_s[...]
    dg_ref[...] += jnp.sum(dvn * vh, axis=0, keepdims=True)
    dbl_ref[...] += jnp.sum(dvn, axis=0, keepdims=True)
    dvh = dvn * lng
    dvg_in = rstd * (dvh - jnp.mean(dvh, axis=-1, keepdims=True) - vh * jnp.mean(dvh * vh, axis=-1, keepdims=True))
    return du, (dvg_in * dvg).astype(BF16)


def _out_fwd_bwd(hf_z, hb_z, ga, gb, ys, x, tgt, mods, final_g, w_out_full):
    lx = x.shape[0]
    n = lx // T

    def body(hf_ref, hb_ref, ga_ref, gb_ref, ys_ref, x_ref, t_ref, gx_ref, fg_ref, w_ref,
             loss_ref, dfg_ref, dgx_ref, dxn_ref, y_ref, do_ref, dga_ref, dgb_ref, dyl_ref, dys_ref, yl_s):
        i = pl.program_id(0)

        @pl.when(i == 0)
        def _():
            loss_ref[...] = jnp.zeros_like(loss_ref)
            dfg_ref[...] = jnp.zeros_like(dfg_ref)
            dgx_ref[...] = jnp.zeros_like(dgx_ref)

        for h in range(NH):
            yl_s[:, HD * h:HD * h + HD] = hf_ref[_zrows(h, T), :] + hb_ref[_zrows(h, T), :]
        yl = yl_s[...]
        gav = ga_ref[...]
        gbv = gb_ref[...]
        sa, dsa = _silu_and_grad(gav)
        sb, dsb = _silu_and_grad(gbv)
        ysv = ys_ref[...]
        y_ref[:, 0:D] = (yl * sa).astype(BF16)
        y_ref[:, D:2 * D] = (ysv * sb).astype(BF16)
        o = _dot(y_ref[...], w_ref[...])
        gx = gx_ref[0:1, :]
        xnew = x_ref[...] + gx * o
        r2 = lax.rsqrt(jnp.mean(xnew * xnew, axis=-1, keepdims=True) + NORM_EPS)
        xh = xnew * r2
        fg = fg_ref[...]
        err = xh * fg - t_ref[...]
        loss_ref[...] += 0.5 * jnp.sum(jnp.mean(err * err, axis=-1, keepdims=True), axis=0, keepdims=True)

        @pl.when(i == n - 1)
        def _():
            lp = loss_ref[...]
            lp1 = lp.astype(BF16).astype(F32)
            lp2 = (lp - lp1).astype(BF16).astype(F32)
            lp3 = (lp - lp1 - lp2).astype(BF16).astype(F32)
            lane = lax.broadcasted_iota(jnp.int32, lp.shape, 1)
            loss_ref[...] = jnp.where(lane == 0, lp1, jnp.where(lane == 1, lp2, jnp.where(lane == 2, lp3, 0.0)))
        dout = err * (1.0 / D)
        dfg_ref[...] += jnp.sum(dout * xh, axis=0, keepdims=True)
        dxh = dout * fg
        dxn = r2 * (dxh - xh * jnp.mean(dxh * xh, axis=-1, keepdims=True))
        dxn_ref[...] = dxn
        dgx_ref[...] += jnp.sum(dxn * o, axis=0, keepdims=True)
        do = (dxn * gx).astype(BF16)
        do_ref[...] = do
        dy = _dot_nt(do, w_ref[...])
        dy1 = dy[:, 0:D]
        dy2 = dy[:, D:2 * D]
        dga_ref[...] = (dy1 * yl * dsa).astype(BF16)
        dgb_ref[...] = (dy2 * ysv * dsb).astype(BF16)
        dys_ref[...] = dy2 * sb
        yl_s[...] = dy1 * sa
        for h in range(NH):
            dyl_ref[_zrows(h, T), :] = yl_s[:, HD * h:HD * h + HD]

    row = pl.BlockSpec((T, D), lambda i: (i, 0))
    vec = pl.BlockSpec((1, D), lambda i: (0, 0))
    zlat = pl.BlockSpec((T * NH, HD), lambda i: (i + 1, 0))
    in_specs = [zlat, zlat, row, row, row, row, row, pl.BlockSpec((8, D), lambda i: (0, 2)), vec,
                pl.BlockSpec((2 * D, D), lambda i: (0, 0))]
    out_shape = (jax.ShapeDtypeStruct((1, D), F32), jax.ShapeDtypeStruct((1, D), F32), jax.ShapeDtypeStruct((1, D), F32),
                 jax.ShapeDtypeStruct((lx, D), F32), jax.ShapeDtypeStruct((lx, 2 * D), BF16),
                 jax.ShapeDtypeStruct((lx, D), BF16), jax.ShapeDtypeStruct((lx, D), BF16),
                 jax.ShapeDtypeStruct((lx, D), BF16), jax.ShapeDtypeStruct((lx * NH, HD), F32),
                 jax.ShapeDtypeStruct((lx, D), F32))
    out_specs = (vec, vec, vec, row, pl.BlockSpec((T, 2 * D), lambda i: (i, 0)),
                 row, row, row, ZT, row)
    return pl.pallas_call(
        body, name="out_fwd_bwd", grid=(n,), out_shape=out_shape, in_specs=in_specs, out_specs=out_specs,
        scratch_shapes=[pltpu.VMEM((T, D), F32)],
        compiler_params=_cp(1, vmem_mb=56),
    )(hf_z, hb_z, ga, gb, ys, x, tgt, mods, final_g, w_out_full)


def _lru_bwd(xc_z, dy_z, hf_z, hb_z, af_z, ab_z, gf, gb, s_b, wcat, lamcat, name):
    lx = xc_z.shape[0] // NH
    n = lx // T
    tile_u = lambda i: jnp.where(i == n - 1, 0, i + 1)
    tile_d = lambda i: n - 1 - i

    def body(xc_u, dy_u, hb_ref, hbn_ref, ab_ref, gb_ref, xc_d, dy_d, hf_ref, hfp_ref, af_ref, gf_ref,
             sb_ref, w_ref, lam_ref, dxcb_ref, dxcf_ref, dw_ref, db_ref, dl_ref,
             lb_s, lf_s, q_u, q_d, pf_s, pb_s, dpre_s, carry):
        i = pl.program_id(0)
        tu, td = tile_u(i), tile_d(i)

        @pl.when(i == 0)
        def _():
            dw_ref[...] = jnp.zeros_like(dw_ref)
            db_ref[...] = jnp.zeros_like(db_ref)
            dl_ref[...] = jnp.zeros_like(dl_ref)
            carry[...] = jnp.zeros_like(carry)

        _scan_tile([(ab_ref, dy_u, lb_s, q_u, False, jnp.where(tu == 0, 0.0, 1.0)),
                    (af_ref, dy_d, lf_s, q_d, True, jnp.where(td == 0, 0.0, 1.0))], True, carry)
        zero = jnp.zeros((NH, HD), F32)
        pb_s[pl.ds(0, T * NH), :] = hb_ref[...]
        pb_s[pl.ds(T * NH, NH), :] = jnp.where(tu == n - 1, sb_ref[...], jnp.where(tu == 0, zero, hbn_ref[pl.ds(0, NH), :]))
        pf_s[pl.ds(0, NH), :] = jnp.where(td == 0, zero, hfp_ref[pl.ds(7 * NH, NH), :])
        pf_s[pl.ds(NH, T * NH), :] = hf_ref[...]
        sides = ((1, xc_u, lb_s, pb_s, NH, ab_ref, gb_ref, dxcb_ref), (0, xc_d, lf_s, pf_s, 0, af_ref, gf_ref, dxcf_ref))
        for d, xc_ref, adj_s, prev_s, prev_off, a_ref, g_ref, dxc_ref in sides:
            wcols = slice(256 * d, 256 * d + 256)
            for h in range(NH):
                xch = xc_ref[_zrows(h, T), :]
                xcb = xch.astype(BF16)
                r, gi, mult = (g_ref[:, q * D + HD * h:q * D + HD * h + HD] for q in range(3))
                a = a_ref[_zrows(h, T), :]
                lam = lam_ref[h:h + 1, HD * d:HD * d + HD]
                sp = _softplus(-lam)
                du = adj_s[_zrows(h, T), :]
                da = du * prev_s[pl.ds(prev_off + h, T, stride=NH), :]
                dgi = du * mult * xch
                dmult = du * gi * xch
                dla = da * a - dmult * (a * a) / mult
                dr = dla * ((-LRU_C) * sp)
                dsp = jnp.sum(dla * ((-LRU_C) * r), axis=0, keepdims=True)
                dl_ref[h:h + 1, HD * d:HD * d + HD] += dsp * (-_sigmoid(-lam))
                dpre_s[:, 0:HD] = dr * r * (1.0 - r)
                dpre_s[:, HD:2 * HD] = dgi * gi * (1.0 - gi)
                dpre = dpre_s[...]
                dpb = dpre.astype(BF16)
                dw_ref[h, :, wcols] += _dot_tn(xcb, dpb)
                db_ref[h:h + 1, wcols] += jnp.sum(dpre, axis=0, keepdims=True)
                dxc_ref[_zrows(h, T), :] = du * mult * gi + _dot_nt(dpb, w_ref[h, :, wcols])

    full = lambda shape: pl.BlockSpec(shape, lambda i: (0,) * len(shape))
    wsp, bsp, lsp = full((NH, HD, 4 * HD)), full((NH, 4 * HD)), full((NH, 2 * HD))
    st = full((NH, HD))
    up = pl.BlockSpec((T * NH, HD), lambda i: (tile_u(i), 0))
    dn = pl.BlockSpec((T * NH, HD), lambda i: (tile_d(i), 0))
    dy_up = pl.BlockSpec((T * NH, HD), lambda i: (jnp.maximum(tile_u(i) - 1, 0), 0))
    dy_dn = pl.BlockSpec((T * NH, HD), lambda i: (jnp.maximum(tile_d(i) - 1, 0), 0))
    nxt = _tile_specs(NH, HD, n, tile_u)[2]
    prv = _tile_specs(NH, HD, n, tile_d)[1]
    g_up = pl.BlockSpec((T, 3 * D), lambda i: (tile_u(i), 0))
    g_dn = pl.BlockSpec((T, 3 * D), lambda i: (tile_d(i), 0))
    zs = jax.ShapeDtypeStruct((lx * NH, HD), F32)
    zbuf = pltpu.VMEM((T * NH, HD), F32)
    zbuf1 = pltpu.VMEM(((T + 1) * NH, HD), F32)
    return pl.pallas_call(
        body, name=name, grid=(n,),
        out_shape=(zs, zs, jax.ShapeDtypeStruct((NH, HD, 4 * HD), F32), jax.ShapeDtypeStruct((NH, 4 * HD), F32),
                   jax.ShapeDtypeStruct((NH, 2 * HD), F32)),
        in_specs=[up, dy_up, up, nxt, up, g_up, dn, dy_dn, dn, prv, dn, g_dn, st, wsp, lsp],
        out_specs=(up, dn, wsp, bsp, lsp),
        scratch_shapes=[zbuf, zbuf, zbuf, zbuf, zbuf1, zbuf1, pltpu.VMEM((T, 2 * HD), F32),
                        pltpu.VMEM((2, NH, HD), F32)],
        compiler_params=_cp(1, vmem_mb=56),
    )(xc_z, dy_z, hb_z, hb_z, ab_z, gb, xc_z, dy_z, hf_z, hf_z, af_z, gf, s_b, wcat, lamcat)


def _conv_bwd(dxc_a, dxc_b, xa, conv_wz, dcw0, dcb0, name):
    lx = dxc_a.shape[0] // NH
    n = lx // T

    def body(dm_a, dp_a, dn_a, dm_b, dp_b, dn_b, xan_ref, cw, dcw0_ref, dcb0_ref, dxa_ref, dcw_ref, dcb_ref,
             pad, dxa_s, xa_ref):
        i = pl.program_id(0)

        @pl.when(i == 0)
        def _():
            dcw_ref[...] = dcw0_ref[...]
            dcb_ref[...] = dcb0_ref[...]

        for h in range(NH):
            xa_ref[_zrows(h, T), :] = xan_ref[:, HD * h:HD * h + HD]

        pmask = jnp.where(_has_prev(i), 1.0, 0.0)
        nmask = jnp.where(_has_next(i, n), 1.0, 0.0)
        pad[pl.ds(0, 8 * NH), :] = (dp_a[...] + dp_b[...]) * pmask
        pad[pl.ds(8 * NH, T * NH), :] = dm_a[...] + dm_b[...]
        pad[pl.ds((T + 8) * NH, 8 * NH), :] = (dn_a[...] + dn_b[...]) * nmask

        def chunk(ci, carry):
            base = pl.multiple_of(ci * (CONV_CHUNK * NH), CONV_CHUNK * NH)
            xav = xa_ref[pl.ds(base, CONV_CHUNK * NH), :].reshape(CONV_CHUNK, NH, HD)
            acc = None
            for k in range(4):
                sl = pad[pl.ds(base + (9 - k) * NH, CONV_CHUNK * NH), :].reshape(CONV_CHUNK, NH, HD)
                term = sl * cw[k][None]
                acc = term if acc is None else acc + term
                dcw_ref[k] += jnp.sum(sl * xav, axis=0)
                if k == 1:
                    dcb_ref[...] += jnp.sum(sl, axis=0)
            dxa_s[pl.ds(base, CONV_CHUNK * NH), :] = acc.reshape(CONV_CHUNK * NH, HD)
            return carry
        lax.fori_loop(0, T // CONV_CHUNK, chunk, 0)
        for h in range(NH):
            dxa_ref[:, HD * h:HD * h + HD] = dxa_s[_zrows(h, T), :].astype(BF16)

    full = lambda shape: pl.BlockSpec(shape, lambda i: (0,) * len(shape))
    return pl.pallas_call(
        body, name=name, grid=(n,),
        out_shape=(jax.ShapeDtypeStruct((lx, D), BF16), jax.ShapeDtypeStruct((4, NH, HD), F32),
                   jax.ShapeDtypeStruct((NH, HD), F32)),
        in_specs=_tile_specs(NH, HD, n, lambda i: i) * 2 + [pl.BlockSpec((T, D), lambda i: (i, 0)), full((4, NH, HD)),
                                                            full((4, NH, HD)), full((NH, HD))],
        out_specs=(pl.BlockSpec((T, D), lambda i: (i, 0)), full((4, NH, HD)), full((NH, HD))),
        scratch_shapes=[pltpu.VMEM(((T + 16) * NH, HD), F32), pltpu.VMEM((T * NH, HD), F32),
                        pltpu.VMEM((T * NH, HD), F32)],
        compiler_params=_cp(1, vmem_mb=48),
    )(dxc_a, dxc_a, dxc_a, dxc_b, dxc_b, dxc_b, xa, conv_wz, dcw0, dcb0)


def _proj_bwd(dxa, dga, dgb, x, ctx, dxn, mods, norm_g, w_full, sgu):
    lx, lc = x.shape[0], ctx.shape[0]
    assert lc == T
    n = 1 + lx // T

    def body(dxa_ref, dga_ref, dgb_ref, w0, w1, w4, w2, w3, x_ref, c_ref, sc_ref, ng_ref, dxn_ref,
             u_ref, v_ref, dy_ref, g_ref, b_ref, sw_ref, bt_ref,
             gx_ref, dng_ref, dscx_ref, dshx_ref, dscc_ref, dshc_ref, du_ref, dv_ref, dws_ref, dbs_ref, dlg_ref, dlb_ref,
             mixed_s, dvn_s):
        i = pl.program_id(0)
        is_ctx = i == 0

        @pl.when(is_ctx)
        def _():
            for acc in (dng_ref, dscx_ref, dshx_ref, dscc_ref, dshc_ref, dws_ref, dbs_ref, dlg_ref, dlb_ref):
                acc[...] = jnp.zeros_like(acc)

        xv = jnp.where(is_ctx, c_ref[...], x_ref[...])
        sc1 = 1.0 + jnp.where(is_ctx, sc_ref[1:2, :], sc_ref[0:1, :])
        r = lax.rsqrt(jnp.mean(xv * xv, axis=-1, keepdims=True) + NORM_EPS)
        xn = xv * r
        ng = ng_ref[...]

        def norm_bwd(dhn, dsc_ref, dsh_ref, with_x):
            t = dhn * xn
            dng_ref[...] += jnp.sum(t * sc1, axis=0, keepdims=True)
            dsc_ref[...] += jnp.sum(t * ng, axis=0, keepdims=True)
            dsh_ref[...] += jnp.sum(dhn, axis=0, keepdims=True)
            if with_x:
                dxh = dhn * (ng * sc1)
                gx_ref[...] = dxn_ref[...] + r * (dxh - xn * jnp.mean(dxh * xn, axis=-1, keepdims=True))

        @pl.when(is_ctx)
        def _():
            norm_bwd(_dot_nt(dxa_ref[...], w0[...]), dscc_ref, dshc_ref, False)

        @pl.when(i > 0)
        def _():
            def sgu_chunk(ch):
                rows = slice(HD * ch, HD * ch + HD)
                du, dv = _sgu_bwd_chunk(u_ref[rows, :], v_ref[rows, :], dy_ref[rows, :], g_ref[...], b_ref[...],
                                        sw_ref, bt_ref, mixed_s, dvn_s, dws_ref, dbs_ref, dlg_ref, dlb_ref)
                du_ref[rows, :] = du
                dv_ref[rows, :] = dv

            dhn = _dot_nt(dxa_ref[...], w0[...])
            sgu_chunk(0)
            dhn = dhn + _dot_nt(dga_ref[...], w1[...])
            for ch in range(1, T // HD):
                sgu_chunk(ch)
            dhn = dhn + _dot_nt(dgb_ref[...], w4[...])
            dhn = dhn + _dot_nt(du_ref[...], w2[...]) + _dot_nt(dv_ref[...], w3[...])
            norm_bwd(dhn, dscx_ref, dshx_ref, True)

    every = pl.BlockSpec((T, D), lambda i: (i, 0))
    lat = pl.BlockSpec((T, D), lambda i: (jnp.maximum(i - 1, 0), 0))
    vec = pl.BlockSpec((1, D), lambda i: (0, 0))
    wsp = pl.BlockSpec((NH, HD, HD), lambda i: (0, 0, 0))
    bsp = pl.BlockSpec((NH, HD), lambda i: (0, 0))
    in_specs = [every, lat, lat] + [pl.BlockSpec((D, D), lambda i, k=k: (0, k)) for k in (0, 1, 4, 2, 3)]
    in_specs += [lat, pl.BlockSpec((T, D), lambda i: (0, 0)), pl.BlockSpec((8, D), lambda i: (0, 1)), vec, lat]
    in_specs += [lat, lat, lat, vec, vec, wsp, pl.BlockSpec((HD, NH), lambda i: (0, 0))]
    vs = jax.ShapeDtypeStruct((1, D), F32)
    zb = jax.ShapeDtypeStruct((lx, D), BF16)
    return pl.pallas_call(
        body, name="proj_bwd", grid=(n,),
        out_shape=(jax.ShapeDtypeStruct((lx, D), F32), vs, vs, vs, vs, vs, zb, zb,
                   jax.ShapeDtypeStruct((NH, HD, HD), F32), jax.ShapeDtypeStruct((NH, HD), F32), vs, vs),
        in_specs=in_specs, out_specs=(lat, vec, vec, vec, vec, vec, lat, lat, wsp, bsp, vec, vec),
        scratch_shapes=[pltpu.VMEM((HD, D), F32), pltpu.VMEM((HD, D), F32)], compiler_params=_cp(1, vmem_mb=56),
    )(dxa, dga, dgb, *([w_full] * 5), x, ctx, mods, norm_g, dxn, *sgu)


def _adam_math(w, g, m, v):
    m = ADAM_B1 * m + (1.0 - ADAM_B1) * g
    v = ADAM_B2 * v + (1.0 - ADAM_B2) * (g * g)
    m_hat = m / (1.0 - ADAM_B1 ** ADAM_STEP)
    v_hat = v / (1.0 - ADAM_B2 ** ADAM_STEP)
    delta = -ADAM_LR * (m_hat / (jnp.sqrt(v_hat) + ADAM_EPS) + ADAM_WD * w)
    return delta, m, v


def _adam_big(w, g, m, v, name):
    rows, cols = w.shape
    tr = 256

    def body(w_ref, g_ref, m_ref, v_ref, d_o, m_o, v_o):
        d, mm, vv = _adam_math(w_ref[...], g_ref[...], m_ref[...], v_ref[...])
        d_o[...] = d
        m_o[...] = mm
        v_o[...] = vv

    blk = pl.BlockSpec((tr, cols), lambda i: (i, 0))
    s = jax.ShapeDtypeStruct((rows, cols), F32)
    return pl.pallas_call(
        body, name=name, grid=(rows // tr,), out_shape=(s, s, s), in_specs=[blk] * 4, out_specs=(blk,) * 3,
        compiler_params=_cp(1, vmem_mb=48),
    )(w, g, m, v)


def _adam_small(items, tot):
    ni = len(items)
    pieces = [it[1] if isinstance(it[1], list) else None for it in items]
    flat = [a for it, pc in zip(items, pieces) for a in ((it[0], it[2], it[3]) if pc is not None else it)]
    n_in = len(flat) + 1
    out_shape = tuple(jax.ShapeDtypeStruct(it[0].shape, F32) for it, pc in zip(items, pieces)
                      for _ in range(4 if pc is not None else 3))
    n_out = len(out_shape)
    n_loads = sum(3 + (len(pc) if pc is not None else 1) for pc in pieces)

    def body(*refs):
        ins, tot_ref, outs = refs[:n_in - 1], refs[n_in - 1], refs[n_in:n_in + n_out]
        bufs = refs[n_in + n_out:n_in + n_out + 7 * ni]
        sem_in, sem_out = refs[n_in + n_out + 7 * ni:]
        loads, q_in, q_sem = [], 0, 0
        for k, pc in enumerate(pieces):
            w_b, g_b, m_b, v_b = bufs[7 * k:7 * k + 4]
            srcs = [(ins[q_in], w_b)]
            if pc is None:
                srcs.append((ins[q_in + 1], g_b))
                q_in += 1
            else:
                srcs += [(tot_ref.at[pl.ds(r0, nr), pl.ds(c0, nc)], g_b.at[pl.ds(d0, nr), :]) for r0, nr, c0, nc, d0 in pc]
            srcs += [(ins[q_in + 1], m_b), (ins[q_in + 2], v_b)]
            q_in += 3
            mine = []
            for src, dst in srcs:
                mine.append(pltpu.make_async_copy(src, dst, sem_in.at[q_sem]))
                q_sem += 1
            loads.append(mine)
        for n, cp in enumerate([cp for mine in loads for cp in mine]):
            cp.start(priority=n % 2)
        stores, q_out = [], 0
        for k, pc in enumerate(pieces):
            for cp in loads[k]:
                cp.wait()
            w_b, g_b, m_b, v_b = bufs[7 * k:7 * k + 4]
            res = _adam_math(w_b[...], g_b[...], m_b[...], v_b[...])
            srcs = []
            for q in range(3):
                bufs[7 * k + 4 + q][...] = res[q]
                srcs.append(bufs[7 * k + 4 + q])
            if pc is not None:
                srcs.append(g_b)
            for src in srcs:
                cp = pltpu.make_async_copy(src, outs[q_out], sem_out.at[q_out])
                cp.start(priority=q_out % 2)
                stores.append(cp)
                q_out += 1
        for cp in stores:
            cp.wait()

    scratch = [pltpu.VMEM(it[0].shape, F32) for it in items for _ in range(7)]
    scratch += [pltpu.SemaphoreType.DMA((n_loads,)), pltpu.SemaphoreType.DMA((n_out,))]
    res = pl.pallas_call(
        body, name="adam_small", out_shape=out_shape, in_specs=[HBM] * n_in, out_specs=(HBM,) * n_out,
        scratch_shapes=scratch, compiler_params=_cp(vmem_mb=40),
    )(*[_in_hbm(a) for a in flat], _in_hbm(tot))
    outs, q = [], 0
    for pc in pieces:
        outs.append(tuple(res[q:q + 3]) + ((res[q + 3],) if pc is not None else (None,)))
        q += 4 if pc is not None else 3
    return outs


def kernel(x, c, ctx, c_ctx, ada_w, ada_b, norm_g, w_in, conv_w, conv_b, lru_wa, lru_ba, lru_wx, lru_bx, lru_lambda, sgu_ln_g, sgu_ln_b, sgu_w, sgu_b, w_out, final_g, loss_target, m_c_ctx, m_ada_w, m_ada_b, m_norm_g, m_w_in, m_conv_w, m_conv_b, m_lru_wa, m_lru_ba, m_lru_wx, m_lru_bx, m_lru_lambda, m_sgu_ln_g, m_sgu_ln_b, m_sgu_w, m_sgu_b, m_w_out, m_final_g, v_c_ctx, v_ada_w, v_ada_b, v_norm_g, v_w_in, v_conv_w, v_conv_b, v_lru_wa, v_lru_ba, v_lru_wx, v_lru_bx, v_lru_lambda, v_sgu_ln_g, v_sgu_ln_b, v_sgu_w, v_sgu_b, v_w_out, v_final_g):
    ix, iy, ic = lax.axis_index("x"), lax.axis_index("y"), lax.axis_index("c")
    chip = 2 * ix + iy
    dev = 2 * chip + ic
    lx = x.shape[1]
    lc = ctx.shape[1]

    smalls = jnp.concatenate([conv_w[0], lru_lambda[0], jnp.zeros((10, 256), F32)], axis=0)
    c_ctx2 = c_ctx.reshape(1, D)
    mods, c_slots, _, w_in_full, wo_land, ada_land, wcat, bcat, sgu_wb, conv_wz, lamcat, sgu_bt = _gather_in(
        c, c_ctx2, ada_w[0], ada_b, w_in[0], w_out[0], smalls,
        (lru_wa[0], lru_wx[0], lru_ba[0], lru_bx[0], sgu_w[0], sgu_b[0]))
    wo_ss, wo_rs, ada_ss, ada_rs, wo_land, ada_land, token = _late_gather_start(wo_land, ada_land)
    conv_bz = conv_b.reshape(NH, HD)
    final_g2 = final_g.reshape(1, D)

    zero_s = jnp.zeros((NH, HD), F32)
    hn, xa_all, ga, u, v, gb, ys = _proj(x[0], ctx[0], mods, norm_g, w_in_full, (sgu_ln_g, sgu_ln_b, sgu_wb, sgu_bt),
                                         token)
    xcz, af, ab, hf, hb, gf, gb_l, _, hb0 = _lru_fwd(xa_all, conv_wz, conv_bz, wcat, bcat, lamcat, "lru_fwd")

    w_out_full = _late_gather_wait(wo_land, wo_ss, wo_rs, "w_out", hf, "late_gather_wait_w_out")
    (loss_part, dfg, dgx, dxn, y, do, dga, dgb, dyl_z, dys) = _out_fwd_bwd(
        hf, hb, ga, gb, ys, x[0], loss_target[0], mods, final_g2, w_out_full)

    dxc_b, dxc_f, dwc, dbc, dlc = _lru_bwd(xcz, dyl_z, hf, hb, af, ab, gf, gb_l, hb0, wcat, lamcat, "lru_bwd")
    dxa, dcw, dcb = _conv_bwd(dxc_b, dxc_f, xa_all, conv_wz, jnp.zeros((4, NH, HD), F32), zero_s, "conv_bwd")
    dxa = _in_hbm(dxa)

    grad_x, dng, dsc_x, dsh_x, dsc_c, dsh_c, du, dv, d_sgu_w, d_sgu_b, d_ln_g, d_ln_b = _proj_bwd(
        dxa, dga, dgb, x[0], ctx[0], dxn, mods, norm_g, w_in_full, (u, v, dys, sgu_ln_g, sgu_ln_b, sgu_wb, sgu_bt))
    dzs = [dxa, dga, du, dv, dgb]

    dmx = jnp.concatenate([dsh_x, dsc_x, dgx], axis=0)
    dmc = jnp.concatenate([dsh_c, dsc_c, jnp.zeros((1, D), F32)], axis=0)
    slot = jnp.concatenate([dmx, loss_part], axis=0)
    slots = lax.dynamic_update_slice(jnp.zeros((32, D), F32), slot, (4 * dev, 0))
    vecs = jnp.concatenate([dfg, dng, dcb.reshape(1, D), d_ln_g, d_ln_b, dcw.reshape(4, D), dmc,
                            jnp.zeros((4, D), F32), slots], axis=0)
    d_sgu_w4 = d_sgu_w.reshape(4, 256, HD).transpose(1, 0, 2).reshape(256, 4 * HD)
    pad8 = lambda a: jnp.pad(a, ((0, 8 - a.shape[0]), (0, 4 * HD - a.shape[1])))
    pack = jnp.concatenate([dwc.reshape(NH * HD, 4 * HD), pad8(dbc), pad8(dlc), d_sgu_w4, pad8(d_sgu_b),
                            vecs.reshape(96, 4 * HD), jnp.zeros((8, 4 * HD), F32)], axis=0)
    g_w_in, g_w_out, tot = _grads_reduce(hn, dzs, lc, y, do, _in_hbm(pack))

    n_w = NH * HD
    g_lru_wa = [(0, n_w, 2 * HD * d, HD, n_w * d) for d in range(2)]
    g_lru_wx = [(0, n_w, 2 * HD * d + HD, HD, n_w * d) for d in range(2)]
    g_lru_ba = [(n_w, NH, 2 * HD * d, HD, NH * d) for d in range(2)]
    g_lru_bx = [(n_w, NH, 2 * HD * d + HD, HD, NH * d) for d in range(2)]
    g_sgu_w = [(1040, 256, HD * q, HD, 256 * q) for q in range(4)]
    g_sgu_b = [(1296, NH, 0, HD, 0)]
    g_lc = tot[1032:1040, 0:2 * HD]
    tv = tot[1304:1400].reshape(48, D)
    g_final_g, g_norm_g, g_conv_b, g_ln_g, g_ln_b = tv[0:1], tv[1:2], tv[2:3], tv[3:4], tv[4:5]
    g_conv_w_full = tv[5:9]
    dmc_tot = tv[9:12].reshape(1, 3 * D)
    slots_all = tv[16:48].reshape(8, 4, D)
    dmx_all = slots_all[:, 0:3, :].reshape(8, 3 * D)
    c_all = c_slots.reshape(8, 8, D)[:, 0, :]
    g_lam_full = jnp.stack([g_lc[:, 0:HD], g_lc[:, HD:2 * HD]]).reshape(2, D)
    g_conv_w = lax.dynamic_slice(g_conv_w_full, (0, 256 * chip), (4, 256))
    g_lam = lax.dynamic_slice(g_lam_full, (0, 256 * chip), (2, 256))
    dmx_all_j = lax.dynamic_slice(dmx_all, (0, 768 * chip), (8, 768))
    dmc_j = lax.dynamic_slice(dmc_tot, (0, 768 * chip), (1, 768))
    ada_full = _late_gather_wait(ada_land, ada_ss, ada_rs, "ada_w", _in_hbm(tot), "late_gather_wait_ada_w")
    g_ada_w, g_ada_b, g_c_ctx = _ada_bwd(c_all, dmx_all_j, dmc_j, dmx_all, dmc_tot, c_ctx2, ada_full)

    big = {
        "ada_w": _adam_big(ada_w[0], g_ada_w, m_ada_w[0], v_ada_w[0], "adam_ada_w"),
        "w_in": _adam_big(w_in[0], g_w_in, m_w_in[0], v_w_in[0], "adam_w_in"),
        "w_out": _adam_big(w_out[0], g_w_out, m_w_out[0], v_w_out[0], "adam_w_out"),
    }
    small_in = {
        "c_ctx": (c_ctx, g_c_ctx, m_c_ctx, v_c_ctx, (1, D)),
        "ada_b": (ada_b, g_ada_b, m_ada_b, v_ada_b, (1, 3 * D)),
        "norm_g": (norm_g, g_norm_g, m_norm_g, v_norm_g, (1, D)),
        "conv_w": (conv_w, g_conv_w, m_conv_w, v_conv_w, (4, 256)),
        "conv_b": (conv_b, g_conv_b, m_conv_b, v_conv_b, (1, D)),
        "lru_wa": (lru_wa, g_lru_wa, m_lru_wa, v_lru_wa, (2 * NH * HD, HD)),
        "lru_ba": (lru_ba, g_lru_ba, m_lru_ba, v_lru_ba, (2 * NH, HD)),
        "lru_wx": (lru_wx, g_lru_wx, m_lru_wx, v_lru_wx, (2 * NH * HD, HD)),
        "lru_bx": (lru_bx, g_lru_bx, m_lru_bx, v_lru_bx, (2 * NH, HD)),
        "lru_lambda": (lru_lambda, g_lam, m_lru_lambda, v_lru_lambda, (2, 256)),
        "sgu_ln_g": (sgu_ln_g, g_ln_g, m_sgu_ln_g, v_sgu_ln_g, (1, D)),
        "sgu_ln_b": (sgu_ln_b, g_ln_b, m_sgu_ln_b, v_sgu_ln_b, (1, D)),
        "sgu_w": (sgu_w, g_sgu_w, m_sgu_w, v_sgu_w, (NH * HD, HD)),
        "sgu_b": (sgu_b, g_sgu_b, m_sgu_b, v_sgu_b, (NH, HD)),
        "final_g": (final_g, g_final_g, m_final_g, v_final_g, (1, D)),
    }
    names_small = list(small_in)
    res_small = _adam_small([tuple(a if isinstance(a, list) else a.reshape(small_in[k][4]) for a in small_in[k][:4])
                             for k in names_small], tot)
    full_shapes = {"ada_w": ada_w.shape, "w_in": w_in.shape, "w_out": w_out.shape}
    grads, deltas, new_m, new_v = {}, {}, {}, {}
    for k in ("ada_w", "w_in", "w_out"):
        g = {"ada_w": g_ada_w, "w_in": g_w_in, "w_out": g_w_out}[k]
        grads[k] = g.reshape(full_shapes[k])
        deltas[k], new_m[k], new_v[k] = (a.reshape(full_shapes[k]) for a in big[k])
    for k, res in zip(names_small, res_small):
        shape = small_in[k][0].shape
        grads[k] = (small_in[k][1] if res[3] is None else res[3]).reshape(shape)
        deltas[k], new_m[k], new_v[k] = (a.reshape(shape) for a in res[:3])

    loss = jnp.sum(slots_all[:, 3, 0:3])
    order = ["c_ctx", "ada_w", "ada_b", "norm_g", "w_in", "conv_w", "conv_b", "lru_wa", "lru_ba", "lru_wx", "lru_bx",
             "lru_lambda", "sgu_ln_g", "sgu_ln_b", "sgu_w", "sgu_b", "w_out", "final_g"]
    return (loss, grad_x.reshape(x.shape), *[grads[k] for k in order], *[deltas[k] for k in order],
            *[new_m[k] for k in order], *[new_v[k] for k in order])
```

```python
import jax
import jax.numpy as jnp
from jax import lax
from jax.experimental import pallas as pl
from jax.experimental.pallas import tpu as pltpu

F32 = jnp.float32
BF16 = jnp.bfloat16

D = 1024
NH = 8
HD = 128
NCHIP = 4
T = 256
NORM_EPS = 1e-6
LN_EPS = 1e-5
LRU_C = 8.0
ADAM_LR = 0.001
ADAM_B1 = 0.9
ADAM_B2 = 0.999
ADAM_EPS = 1e-08
ADAM_WD = 0.01
ADAM_STEP = 10

VMEM = pl.BlockSpec(memory_space=pltpu.VMEM)
ANY = pl.BlockSpec(memory_space=pl.ANY)
MESH = pl.DeviceIdType.MESH


def _cp(n_grid=0, vmem_mb=None):
    kw = {}
    if n_grid:
        kw["dimension_semantics"] = ("arbitrary",) * n_grid
    if vmem_mb:
        kw["vmem_limit_bytes"] = vmem_mb << 20
    return pltpu.CompilerParams(**kw)


def _sigmoid(x):
    return 0.5 * jnp.tanh(0.5 * x) + 0.5


def _silu_and_grad(x):
    s = _sigmoid(x)
    return x * s, s * (1.0 + x * (1.0 - s))


_GELU_K = 0.7978845608028654
_GELU_C = 0.044715


def _gelu_and_grad(x):
    x2 = x * x
    th = jnp.tanh(x * (_GELU_K + (_GELU_K * _GELU_C) * x2))
    p = 0.5 + 0.5 * th
    g = x * p
    dg = p + g * (1.0 - th) * (_GELU_K + (3.0 * _GELU_K * _GELU_C) * x2)
    return g, dg


def _softplus(x):
    return jnp.maximum(x, 0.0) + jnp.log1p(jnp.exp(-jnp.abs(x)))


def _lru_gate(pre, lam_row, d, off=None):
    off = 256 * d if off is None else off
    r = _sigmoid(pre[:, off:off + HD])
    gi = _sigmoid(pre[:, off + HD:off + 2 * HD])
    lam = lam_row[:, HD * d:HD * d + HD]
    sp = _softplus(-lam)
    la = (-LRU_C) * r * sp
    a = jnp.exp(la)
    x2 = 2.0 * la
    m2 = jnp.where(x2 > -1e-3, -x2 * (1.0 + 0.5 * x2), 1.0 - a * a)
    mult = jnp.sqrt(m2)
    return r, gi, lam, sp, a, mult


def _dot(a, b):
    return jnp.dot(a, b, preferred_element_type=F32)


def _dot_tn(a, b):
    return lax.dot_general(a, b, (((0,), (0,)), ((), ())), preferred_element_type=F32)


def _dot_nt(a, b):
    return lax.dot_general(a, b, (((1,), (1,)), ((), ())), preferred_element_type=F32)


def _mo(v, m):
    return v if isinstance(v, int) else pl.multiple_of(v, m)


def _zrows(h, n):
    return pl.ds(h, n, stride=NH)


def _gather_in(c, c_ctx, ada_w, ada_b, w_in, w_out, smalls, lru_sgu):
    nch = [1, 4]
    wrows = lambda cc, q: (pl.ds(_mo(512 * cc, 16), 512) if q is None
                           else pl.ds(_mo(512 * cc + (512 // nch[1]) * q, 16), 512 // nch[1]))
    specs = [
        ((64, 256), F32, lambda r, jj, cc, q=None: r.at[pl.ds(_mo(16 * jj + 8 * cc, 8), 8), :]),
        ((D, 5120), BF16, lambda r, jj, cc, q=None: r.at[wrows(cc, q), pl.ds(_mo(1280 * jj, 128), 1280)]),
    ]
    halves = [lambda r, cc, q=None: r.at[pl.ds(_mo(8 * cc, 8), 8), :],
              lambda r, cc, q=None: r.at[wrows(cc, q), :]]
    na = len(specs)
    sem_base = [0, 6 * nch[0]]
    sidx = lambda a, q, k: sem_base[a] + 6 * q + k
    n_tiny = 6 * sum(nch)
    n_sem = n_tiny + 10

    def body(c_ref, cc_ref, ada_ref, adab_ref, win_ref, wout_ref, sm_ref, wa_ref, wx_ref, ba_ref, bx_ref, sw_ref, sb_ref,
             mods_o, call_o, sm_o, win_o, wol_o, adal_o, wcat_o, bcat_o, swb_o, cwz_o, lam_o, sbt_o,
             s_win, s_ada, s_wout, f_win, f_ada, f_wout, cslot, lhs, mbuf,
             send_sems, recv_sems, local_sems, load_sems, f_w, f_sw, s_wcat, s_swb, prep_sems, f_sb):
        x, y, c = lax.axis_index("x"), lax.axis_index("y"), lax.axis_index("c")
        j = 2 * x + y
        dev = 2 * j + c
        sib = (x, y, 1 - c)
        chips = [(1 - x, y), (x, 1 - y), (1 - x, 1 - y)]
        cj = [2 * cx + cy for cx, cy in chips]
        outs = [sm_o, win_o]
        srcs = [sm_ref, s_win]

        def copy(idx, src, dst, to):
            return pltpu.make_async_remote_copy(src_ref=src, dst_ref=dst, send_sem=send_sems.at[idx],
                                                recv_sem=recv_sems.at[idx], device_id=to, device_id_type=MESH)

        sends = []

        def start(cp):
            cp.start()
            sends.append(cp)

        cslot[...] = jnp.zeros_like(cslot)
        cslot[0:1, :] = c_ref[...]
        my_slot = pl.ds(_mo(8 * dev, 8), 8)
        others = [sib] + [(*chips[k], c) for k in range(3)] + [(*chips[k], 1 - c) for k in range(3)]
        other_dev = [dev + 1 - 2 * c] + [2 * cj[k] + c for k in range(3)] + [2 * cj[k] + 1 - c for k in range(3)]
        base = n_tiny
        for r in range(7):
            start(copy(base + r, cslot, call_o.at[my_slot, :], others[r]))
        call_o[my_slot, :] = cslot[...]

        crow = 512 // nch[1]
        loads = []
        for cc in (c, 1 - c):
            for q in range(nch[1]):
                rows = pl.ds(_mo(512 * cc + crow * q, 16), crow)
                loads.append(pltpu.make_async_copy(win_ref.at[rows, :], f_win.at[rows, :], load_sems.at[len(loads)]))
        loads.append(pltpu.make_async_copy(ada_ref, f_ada, load_sems.at[len(loads)]))
        loads.append(pltpu.make_async_copy(wout_ref, f_wout, load_sems.at[len(loads)]))
        for ld in loads:
            ld.start()
        prep = []
        for d in range(2):
            for q, (w_src, b_src) in enumerate([(wa_ref, ba_ref), (wx_ref, bx_ref)]):
                prep.append(pltpu.make_async_copy(w_src.at[d], f_w.at[2 * d + q], prep_sems.at[len(prep)]))
                prep.append(pltpu.make_async_copy(b_src.at[d], bcat_o.at[:, pl.ds(HD * (2 * d + q), HD)],
                                                  prep_sems.at[len(prep)]))
        prep.append(pltpu.make_async_copy(sw_ref, f_sw, prep_sems.at[len(prep)]))
        prep.append(pltpu.make_async_copy(sb_ref, f_sb, prep_sems.at[len(prep)]))
        for cp in prep:
            cp.start()
        for k in range(2):
            start(copy(sidx(0, 0, k), halves[0](srcs[0], c), specs[0][2](outs[0], j, c), (*chips[k], c)))
        for q in range(nch[1]):
            loads[q].wait()
            rows = pl.ds(_mo(512 * c + crow * q, 16), crow)
            s_win[rows, :] = f_win[rows, :].astype(BF16)
            for k in range(2):
                start(copy(sidx(1, q, k), halves[1](s_win, c, q), specs[1][2](win_o, j, c, q), (*chips[k], c)))
        for q in range(nch[1]):
            loads[nch[1] + q].wait()
            rows = pl.ds(_mo(512 * (1 - c) + crow * q, 16), crow)
            s_win[rows, :] = f_win[rows, :].astype(BF16)
        local = []
        for a in range(na):
            for cc in range(2):
                lc = pltpu.make_async_copy(halves[a](srcs[a], cc), specs[a][2](outs[a], j, cc), local_sems.at[2 * a + cc])
                lc.start()
                local.append(lc)
        loads[2 * nch[1]].wait()
        s_ada[...] = f_ada[...].astype(BF16)
        loads[2 * nch[1] + 1].wait()
        s_wout[...] = f_wout[...].astype(BF16)
        for q, (src, dst) in enumerate([(s_wout, wol_o.at[pl.ds(_mo(512 * j, 16), 512), :]),
                                        (s_ada, adal_o.at[:, pl.ds(_mo(768 * j, 128), 768)])]):
            lc = pltpu.make_async_copy(src, dst, local_sems.at[2 * na + q])
            lc.start()
            local.append(lc)
        for cp in prep:
            cp.wait()
        for q in range(4):
            s_wcat[:, :, HD * q:HD * q + HD] = f_w[q].astype(BF16)
        s_swb[...] = f_sw[...].astype(BF16)
        sbt_o[...] = f_sb[...].T
        for n, (src, dst) in enumerate([(s_wcat, wcat_o), (s_swb, swb_o)]):
            lc = pltpu.make_async_copy(src, dst, prep_sems.at[len(prep) + n])
            lc.start()
            local.append(lc)

        for r in range(7):
            slot = call_o.at[pl.ds(_mo(8 * other_dev[r], 8), 8), :]
            copy(base + r, slot, slot, sib).wait_recv()
        lhs[...] = jnp.zeros_like(lhs)
        for b in range(8):
            cv = call_o[8 * b:8 * b + 1, :]
            lhs[b:b + 1, :] = cv * _sigmoid(cv)
        cv = cc_ref[...]
        lhs[8:9, :] = cv * _sigmoid(cv)
        adab = adab_ref[:, 0:768]
        for jj in range(1, NCHIP):
            adab = jnp.where(j == jj, adab_ref[:, 768 * jj:768 * jj + 768], adab)
        mbuf[j] = _dot(lhs[...].astype(BF16), s_ada[...]) + adab
        for k in range(3):
            start(copy(base + 7 + k, mbuf.at[j], mbuf.at[j], (*chips[k], c)))
        for k in range(3):
            copy(base + 7 + k, mbuf.at[cj[k]], mbuf.at[cj[k]], sib).wait_recv()
        mods_o[...] = jnp.zeros_like(mods_o)
        for jj in range(NCHIP):
            mods_o[0:1, 768 * jj:768 * jj + 768] = mbuf[jj, pl.ds(dev, 1), :]
            mods_o[1:2, 768 * jj:768 * jj + 768] = mbuf[jj, 8:9, :]

        kx = [1 - x, x, 1 - x]
        ky = [y, 1 - y, 1 - y]
        pick = lambda k, lst: jnp.where(k == 0, lst[0], jnp.where(k == 1, lst[1], lst[2]))
        for a in range(na):
            for q in range(nch[a]):
                for step, k in enumerate([c, 1 - c]):
                    reg = specs[a][2](outs[a], pick(k, cj), c, q)
                    copy(sidx(a, q, k), reg, reg, sib).wait_recv()
                    if step == 0:
                        start(copy(sidx(a, q, 2), reg, reg, (pick(1 - c, kx), pick(1 - c, ky), c)))
                    start(copy(sidx(a, q, 3 + k), reg, reg, sib))
        for a in range(na):
            for q in range(nch[a]):
                reg = specs[a][2](outs[a], cj[2], c, q)
                copy(sidx(a, q, 2), reg, reg, sib).wait_recv()
                start(copy(sidx(a, q, 5), reg, reg, sib))
        for a in range(na):
            for q in range(nch[a]):
                for k in range(3):
                    reg = specs[a][2](outs[a], cj[k], 1 - c, q)
                    copy(sidx(a, q, 3 + k), reg, reg, sib).wait_recv()
        for cp in sends:
            cp.wait_send()
        for lc in local:
            lc.wait()
        for jj in range(NCHIP):
            for mh in range(2):
                h = 2 * jj + mh
                cols = slice(HD * mh, HD * mh + HD)
                for r in range(4):
                    cwz_o[r, h:h + 1, :] = sm_o[16 * jj + r:16 * jj + r + 1, cols]
                for d in range(2):
                    lam_o[h:h + 1, HD * d:HD * d + HD] = sm_o[16 * jj + 4 + d:16 * jj + 5 + d, cols]

    out_shape = (jax.ShapeDtypeStruct((8, 3 * D), F32), jax.ShapeDtypeStruct((64, D), F32),
                 jax.ShapeDtypeStruct(specs[0][0], F32), jax.ShapeDtypeStruct(specs[1][0], BF16),
                 jax.ShapeDtypeStruct((2048, D), BF16), jax.ShapeDtypeStruct((D, 3 * D), BF16),
                 jax.ShapeDtypeStruct((NH, HD, 4 * HD), BF16), jax.ShapeDtypeStruct((NH, 4 * HD), F32),
                 jax.ShapeDtypeStruct((NH, HD, HD), BF16), jax.ShapeDtypeStruct((4, NH, HD), F32),
                 jax.ShapeDtypeStruct((NH, 2 * HD), F32), jax.ShapeDtypeStruct((HD, NH), F32))
    return pl.pallas_call(
        body, name="gather_in", out_shape=out_shape,
        in_specs=[VMEM, VMEM, ANY, VMEM, ANY, ANY, VMEM] + [ANY] * 6,
        out_specs=(VMEM, VMEM, VMEM, ANY, ANY, ANY, ANY, ANY, ANY, VMEM, VMEM, VMEM),
        scratch_shapes=[pltpu.VMEM((D, 1280), BF16), pltpu.VMEM((D, 768), BF16), pltpu.VMEM((512, D), BF16),
                        pltpu.VMEM((D, 1280), F32), pltpu.VMEM((D, 768), F32), pltpu.VMEM((512, D), F32),
                        pltpu.VMEM((8, D), F32), pltpu.VMEM((16, D), F32), pltpu.VMEM((NCHIP, 16, 768), F32),
                        pltpu.SemaphoreType.DMA((n_sem,)), pltpu.SemaphoreType.DMA((n_sem,)),
                        pltpu.SemaphoreType.DMA((2 * na + 2,)), pltpu.SemaphoreType.DMA((2 * nch[1] + 2,)),
                        pltpu.VMEM((4, NH, HD, HD), F32), pltpu.VMEM((NH, HD, HD), F32),
                        pltpu.VMEM((NH, HD, 4 * HD), BF16), pltpu.VMEM((NH, HD, HD), BF16),
                        pltpu.SemaphoreType.DMA((12,)), pltpu.VMEM((NH, HD), F32)],
        compiler_params=_cp(vmem_mb=56),
    )(c, c_ctx, ada_w, ada_b, w_in, w_out, smalls, *[_in_hbm(a) for a in lru_sgu])


HBM = pl.BlockSpec(memory_space=pltpu.HBM)
SEM = pl.BlockSpec(memory_space=pltpu.SEMAPHORE)


def _in_hbm(a):
    return pltpu.with_memory_space_constraint(a, pltpu.HBM)


def _late_gather_regions(x, y, c):
    chips = [(1 - x, y), (x, 1 - y), (1 - x, 1 - y)]
    wo_reg = lambda r, jj, cc: r.at[pl.ds(_mo(512 * jj + 256 * cc, 16), 256), :]
    ada_reg = lambda r, jj, cc: r.at[pl.ds(_mo(512 * cc, 16), 512), pl.ds(_mo(768 * jj, 128), 768)]
    return chips, wo_reg, ada_reg


def _late_gather_start(wo_land, ada_land):
    def body(wol_ref, adal_ref, wo_ss, wo_rs, ada_ss, ada_rs, wol_thru, adal_thru, token):
        x, y, c = lax.axis_index("x"), lax.axis_index("y"), lax.axis_index("c")
        j = 2 * x + y
        chips, wo_reg, ada_reg = _late_gather_regions(x, y, c)
        for k in range(3):
            for cc in range(2):
                pltpu.make_async_remote_copy(src_ref=wo_reg(wol_ref, j, c), dst_ref=wo_reg(wol_ref, j, c),
                                             send_sem=wo_ss.at[2 * k + cc], recv_sem=wo_rs.at[2 * k + c],
                                             device_id=(*chips[k], cc), device_id_type=MESH).start()
        for k in range(3):
            for cc in range(2):
                pltpu.make_async_remote_copy(src_ref=ada_reg(adal_ref, j, c), dst_ref=ada_reg(adal_ref, j, c),
                                             send_sem=ada_ss.at[2 * k + cc], recv_sem=ada_rs.at[2 * k + c],
                                             device_id=(*chips[k], cc), device_id_type=MESH).start()
        token[...] = jnp.zeros_like(token)

    sems = pltpu.SemaphoreType.DMA((6,))
    return pl.pallas_call(
        body, name="late_gather_start",
        out_shape=(sems, sems, sems, sems, pltpu.HBM(wo_land.shape, BF16), pltpu.HBM(ada_land.shape, BF16),
                   jax.ShapeDtypeStruct((8, 128), F32)),
        in_specs=(HBM, HBM), out_specs=(SEM, SEM, SEM, SEM, HBM, HBM, VMEM), input_output_aliases={0: 4, 1: 5},
        compiler_params=pltpu.CompilerParams(has_side_effects=pltpu.SideEffectType.DATAFLOW_SIDE_EFFECTING),
    )(_in_hbm(wo_land), _in_hbm(ada_land))


def _late_gather_wait(land, send_sems, recv_sems, which, after, name):
    def body(land_ref, ss, rs, after_ref, land_out):
        x, y, c = lax.axis_index("x"), lax.axis_index("y"), lax.axis_index("c")
        j = 2 * x + y
        chips, wo_reg, ada_reg = _late_gather_regions(x, y, c)
        reg = wo_reg if which == "w_out" else ada_reg
        for k in range(3):
            kj = 2 * chips[k][0] + chips[k][1]
            for cc in range(2):
                cp = pltpu.make_async_remote_copy(src_ref=reg(land_ref, j, c), dst_ref=reg(land_ref, kj, cc),
                                                  send_sem=ss.at[2 * k + cc], recv_sem=rs.at[2 * k + cc],
                                                  device_id=(*chips[k], cc), device_id_type=MESH)
                cp.wait_send()
                cp.wait_recv()

    return pl.pallas_call(
        body, name=name, out_shape=pltpu.HBM(land.shape, land.dtype),
        in_specs=(HBM, SEM, SEM, ANY), out_specs=HBM, input_output_aliases={0: 0},
        compiler_params=pltpu.CompilerParams(has_side_effects=pltpu.SideEffectType.DATAFLOW_SIDE_EFFECTING),
    )(land, send_sems, recv_sems, after)


RCHUNK = 16


def _grads_reduce(hn, dzs, lc, y, do, pack):
    rp = pack.shape[0]
    hp = rp // 2
    assert hp % RCHUNK == 0
    wi_w = 1280
    lx = hn.shape[0] - lc
    lt = lx + lc
    n_dz = len(dzs)

    def body(*refs):
        hn_hbm, dz_hbm = refs[0], refs[1:1 + n_dz]
        y_hbm, do_hbm, pk_hbm, wi_out, wo_out, pk_out = refs[1 + n_dz:7 + n_dz]
        (hn_mine, hn_other, dzbuf, wi_other, wi_mine, wi_recv, wi_send, wi_rb,
         y_blk, do_mine, do_other, wo_other, wo_mine, wo_recv, wo_send, wo_rb,
         pk_mine, pk_recv, pk_send, pk_rb, pk_own, send_sems, recv_sems, local_sems) = refs[7 + n_dz:]
        x, y, c = lax.axis_index("x"), lax.axis_index("y"), lax.axis_index("c")
        j = 2 * x + y
        sib = (x, y, 1 - c)
        chips = [(1 - x, y), (x, 1 - y), (1 - x, 1 - y)]
        cj = [2 * cx + cy for cx, cy in chips]
        near = (jnp.where(c == 0, 1 - x, x), jnp.where(c == 0, y, 1 - y), c)
        slabs = [cj[2], cj[0], cj[1], j]

        def copy(k, src, dst, to):
            return pltpu.make_async_remote_copy(src_ref=src, dst_ref=dst, send_sem=send_sems.at[k],
                                                recv_sem=recv_sems.at[k], device_id=to, device_id_type=MESH)

        def local(k, src, dst):
            cp = pltpu.make_async_copy(src, dst, local_sems.at[k])
            cp.start()
            return cp

        rows_half = lambda r, cc, n: r.at[pl.ds(_mo(cc * n, 16), n), :]
        cols_half = lambda r, cc, n: r.at[:, pl.ds(_mo(cc * n, 128), n)]
        pk_piece = lambda r, cc, jj: r.at[pl.ds(_mo(cc * hp, 16), hp), pl.ds(_mo(jj * 128, 128), 128)]

        sends = []

        def start(cp):
            cp.start()
            sends.append(cp)

        def dz_pieces(s):
            g0 = wi_w * s
            k0, off0 = g0 // D, g0 % D
            w0 = min(D - off0, wi_w)
            pieces = [(k0, off0, w0, 0)]
            if w0 < wi_w:
                pieces.append((k0 + 1, 0, wi_w - w0, w0))
            return pieces

        def dz_copies(s):
            cps = []
            for q, (k, off, w, dst) in enumerate(dz_pieces(s)):
                cps.append(pltpu.make_async_copy(dz_hbm[k].at[pl.ds(lc if k == 0 else 0, lx), pl.ds(off, w)],
                                                 dzbuf.at[pl.ds(lc, lx), pl.ds(dst, w)], local_sems.at[11 + q]))
            if s == 0:
                cps.append(pltpu.make_async_copy(dz_hbm[0].at[pl.ds(0, lc), :], dzbuf.at[pl.ds(0, lc), pl.ds(0, D)],
                                                 local_sems.at[13]))
            return cps

        def dz_load(sl):
            for s in range(NCHIP):
                @pl.when(sl == s)
                def _():
                    if s == 0:
                        dzbuf[pl.ds(0, lc), pl.ds(D, wi_w - D)] = jnp.zeros((lc, wi_w - D), BF16)
                    else:
                        dzbuf[pl.ds(0, lc), :] = jnp.zeros((lc, wi_w), BF16)
                    for cp in dz_copies(s):
                        cp.start()

        def dz_wait(sl):
            for s in range(NCHIP):
                @pl.when(sl == s)
                def _():
                    for cp in dz_copies(s):
                        cp.wait()

        l_pk = local(0, rows_half(pk_hbm, c, hp), pk_mine)
        start(copy(0, rows_half(pk_hbm, 1 - c, hp), pk_recv, sib))
        col = lambda r, cc: r.at[:, pl.ds(_mo(cc * 512, 128), 512)]
        do_loads = [local(7, col(do_hbm, c), do_mine), local(14, col(do_hbm, 1 - c), do_other)]
        hn_loads = [local(2, col(hn_hbm, c), hn_mine), local(4, col(hn_hbm, 1 - c), hn_other)]
        y_copy = lambda s: pltpu.make_async_copy(col(y_hbm, slabs[s]), y_blk, local_sems.at[1])
        y_copy(0).start()
        dz_load(slabs[0])

        def pair_sum(mine, recv, send, nrows, keep, relayed=None):
            def step(i, carry):
                rows = pl.ds(_mo(i * RCHUNK, RCHUNK), RCHUNK)
                s = mine[rows, :] + recv[rows, :].astype(F32)
                if relayed is not None:
                    s = s + relayed[rows, :].astype(F32)
                if keep:
                    mine[rows, :] = s
                if send is not None:
                    send[rows, :] = s.astype(BF16)
                return carry
            lax.fori_loop(0, nrows // RCHUNK, step, 0)

        def chip_sum(own, rb, nrows, terms=(0, 1, 2)):
            def step(i, carry):
                rows = pl.ds(_mo(i * RCHUNK, RCHUNK), RCHUNK)
                acc = own[rows, :]
                for q in terms:
                    acc = acc + rb[q, rows, :].astype(F32)
                own[rows, :] = acc
                return carry
            lax.fori_loop(0, nrows // RCHUNK, step, 0)

        w_in_g = dict(other=wi_other, mine=wi_mine, recv=wi_recv, send=wi_send, rb=wi_rb, p1_sems=(2, 3, 4, 5), p2_sem=12,
                      p1=[None] * NCHIP, wait_load=lambda s: dz_wait(slabs[s]), load=lambda s: dz_load(slabs[s]),
                      dot_other=lambda: _dot_tn(hn_other[...], dzbuf[...]), dot_mine=lambda: _dot_tn(hn_mine[...], dzbuf[...]))
        w_out_g = dict(other=wo_other, mine=wo_mine, recv=wo_recv, send=wo_send, rb=wo_rb, p1_sems=(1, 24, 25, 26), p2_sem=9,
                       p1=[None] * NCHIP, wait_load=lambda s: y_copy(s).wait(), load=lambda s: y_copy(s).start(),
                       dot_other=lambda: _dot_tn(y_blk[...], do_other[...]), dot_mine=lambda: _dot_tn(y_blk[...], do_mine[...]))

        def piece_matmuls(g, s):
            if s >= 2:
                g["p1"][s - 2].wait_send()
            g["wait_load"](s)
            g["other"][s % 2] = g["dot_other"]().astype(BF16)
            g["p1"][s] = copy(g["p1_sems"][s], g["other"].at[s % 2], g["recv"].at[s], sib)
            g["p1"][s].start()
            g["mine"][s % 2] = g["dot_mine"]()
            if s + 1 < NCHIP:
                g["load"](s + 1)

        def piece_finish(g, s):
            mine, recv, send, rb, p2 = g["mine"].at[s % 2], g["recv"].at[s], g["send"], g["rb"], g["p2_sem"]
            nrows = mine.shape[0]
            copy(g["p1_sems"][s], recv, recv, sib).wait_recv()
            if s == 3:
                pair_sum(mine, recv, None, nrows, True)
                return
            if s == 0:
                pair_sum(mine, recv, send.at[0], nrows, False)
                start(copy(p2, send.at[0], rb.at[0], near))
                return
            adds_relayed = c == (1 if s == 1 else 0)

            @pl.when(adds_relayed)
            def _():
                copy(p2, rb.at[0], rb.at[0], sib).wait_recv()
                pair_sum(mine, recv, send.at[s], nrows, False, rb.at[0])

            @pl.when(jnp.logical_not(adds_relayed))
            def _():
                pair_sum(mine, recv, send.at[s], nrows, False)
            start(copy(p2 + s, send.at[s], rb.at[s], (*chips[s - 1], c)))

        def piece_total(g):
            for k in (1, 2):
                copy(g["p2_sem"] + k, g["rb"].at[k], g["rb"].at[k], sib).wait_recv()
            chip_sum(g["mine"].at[1], g["rb"], g["mine"].shape[1], (1, 2))

        for cp in do_loads:
            cp.wait()
        piece_matmuls(w_out_g, 0)
        piece_matmuls(w_out_g, 1)
        piece_finish(w_out_g, 0)
        piece_matmuls(w_out_g, 2)
        piece_finish(w_out_g, 1)
        piece_matmuls(w_out_g, 3)
        piece_finish(w_out_g, 2)

        for cp in hn_loads:
            cp.wait()
        piece_matmuls(w_in_g, 0)

        l_pk.wait()
        copy(0, pk_recv, pk_recv, sib).wait_recv()
        pair_sum(pk_mine, pk_recv, pk_send, hp, True)
        for k in range(3):
            start(copy(6 + k, pk_send.at[:, pl.ds(_mo(cj[k] * 128, 128), 128)], pk_rb.at[k], (*chips[k], c)))
        l_pk_own = local(6, pk_mine.at[:, pl.ds(_mo(j * 128, 128), 128)], pk_own)

        piece_matmuls(w_in_g, 1)
        piece_finish(w_in_g, 0)

        l_pk_own.wait()
        for k in range(3):
            copy(6 + k, pk_rb.at[k], pk_rb.at[k], sib).wait_recv()
        chip_sum(pk_own, pk_rb, hp)
        l_pk_out = local(8, pk_own, pk_piece(pk_out, c, j))
        start(copy(15, pk_own, pk_piece(pk_out, c, j), sib))
        for k in range(2):
            start(copy(16 + k, pk_own, pk_piece(pk_out, c, j), (*chips[k], c)))

        piece_matmuls(w_in_g, 2)
        piece_finish(w_in_g, 1)
        piece_matmuls(w_in_g, 3)
        piece_finish(w_in_g, 2)

        piece_finish(w_out_g, 3)
        piece_total(w_out_g)
        l_wo_out = local(9, wo_mine.at[1], cols_half(wo_out, c, 512))
        start(copy(22, wo_mine.at[1], cols_half(wo_out, c, 512), sib))

        far = (jnp.where(c == 0, x, 1 - x), jnp.where(c == 0, 1 - y, y), c)
        for step, k in enumerate([c, 1 - c, 2]):
            reg = pk_piece(pk_out, c, jnp.where(k == 0, cj[0], jnp.where(k == 1, cj[1], cj[2])))
            copy(16 + k, reg, reg, sib).wait_recv()
            if step == 0:
                start(copy(18, reg, reg, far))
            start(copy(19 + k, reg, reg, sib))

        piece_finish(w_in_g, 3)
        piece_total(w_in_g)
        l_wi_out = local(10, wi_mine.at[1], rows_half(wi_out, c, 512))
        start(copy(23, wi_mine.at[1], rows_half(wi_out, c, 512), sib))

        reg = pk_piece(pk_out, 1 - c, j)
        copy(15, reg, reg, sib).wait_recv()
        for k in range(3):
            reg = pk_piece(pk_out, 1 - c, cj[k])
            copy(19 + k, reg, reg, sib).wait_recv()
        reg = cols_half(wo_out, 1 - c, 512)
        copy(22, reg, reg, sib).wait_recv()
        reg = rows_half(wi_out, 1 - c, 512)
        copy(23, reg, reg, sib).wait_recv()
        for cp in sends + w_in_g["p1"][2:] + w_out_g["p1"][2:]:
            cp.wait_send()
        for cp in (l_pk_out, l_wo_out, l_wi_out):
            cp.wait()

    return pl.pallas_call(
        body, name="grads_reduce",
        out_shape=(jax.ShapeDtypeStruct((D, wi_w), F32), jax.ShapeDtypeStruct((512, D), F32),
                   jax.ShapeDtypeStruct(pack.shape, F32)),
        in_specs=[ANY] * (4 + n_dz), out_specs=(ANY,) * 3,
        scratch_shapes=[
            pltpu.VMEM((lt, 512), BF16), pltpu.VMEM((lt, 512), BF16), pltpu.VMEM((lt, wi_w), BF16),
            pltpu.VMEM((2, 512, wi_w), BF16), pltpu.VMEM((2, 512, wi_w), F32), pltpu.VMEM((4, 512, wi_w), BF16),
            pltpu.VMEM((3, 512, wi_w), BF16), pltpu.VMEM((3, 512, wi_w), BF16),
            pltpu.VMEM((lx, 512), BF16), pltpu.VMEM((lx, 512), BF16), pltpu.VMEM((lx, 512), BF16),
            pltpu.VMEM((2, 512, 512), BF16), pltpu.VMEM((2, 512, 512), F32), pltpu.VMEM((4, 512, 512), BF16),
            pltpu.VMEM((3, 512, 512), BF16), pltpu.VMEM((3, 512, 512), BF16),
            pltpu.VMEM((hp, 512), F32), pltpu.VMEM((hp, 512), F32), pltpu.VMEM((hp, 512), BF16),
            pltpu.VMEM((3, hp, 128), BF16), pltpu.VMEM((hp, 128), F32),
            pltpu.SemaphoreType.DMA((27,)), pltpu.SemaphoreType.DMA((27,)), pltpu.SemaphoreType.DMA((15,))],
        compiler_params=_cp(vmem_mb=56),
    )(hn, *dzs, y, do, pack)


def _ada_bwd(c_all, dmx_all_j, dmc_j, dmx_all, dmc, c_ctx, ada_w_full):
    def body(c_ref, dmxj_ref, dmcj_ref, dmx_ref, dmc_ref, cc_ref, w_hbm, gw_ref, gb_ref, gc_ref, lhs, rhs, dm8,
             w_v, w_sem):
        w_loads = [pltpu.make_async_copy(w_hbm.at[:, pl.ds(D * k, D)], w_v.at[k], w_sem.at[k]) for k in range(3)]
        for cp in w_loads:
            cp.start()
        lhs[...] = jnp.zeros_like(lhs)
        rhs[...] = jnp.zeros_like(rhs)
        cv = c_ref[...]
        lhs[0:8, :] = cv * _sigmoid(cv)
        cc = cc_ref[...]
        a_c, da_c = _silu_and_grad(cc)
        lhs[8:9, :] = a_c
        rhs[0:8, :] = dmxj_ref[...]
        rhs[8:9, :] = dmcj_ref[...]
        gw_ref[...] = _dot_tn(lhs[...].astype(BF16), rhs[...].astype(BF16))
        gb_ref[...] = jnp.sum(dmx_ref[...], axis=0, keepdims=True) + dmc_ref[...]
        dm8[...] = jnp.zeros_like(dm8)
        dm8[0:1, :] = dmc_ref[...]
        da = jnp.zeros((8, D), F32)
        for k in range(3):
            w_loads[k].wait()
            da = da + _dot_nt(dm8[:, D * k:D * k + D].astype(BF16), w_v[k])
        gc_ref[...] = da[0:1, :] * da_c

    return pl.pallas_call(
        body, name="ada_bwd",
        out_shape=(jax.ShapeDtypeStruct((D, 768), F32), jax.ShapeDtypeStruct((1, 3 * D), F32),
                   jax.ShapeDtypeStruct((1, D), F32)),
        in_specs=[VMEM] * 6 + [ANY], out_specs=(VMEM,) * 3,
        scratch_shapes=[pltpu.VMEM((16, D), F32), pltpu.VMEM((16, 768), F32), pltpu.VMEM((8, 3 * D), F32),
                        pltpu.VMEM((3, D, D), BF16), pltpu.SemaphoreType.DMA((3,))],
        compiler_params=_cp(vmem_mb=32),
    )(c_all, dmx_all_j, dmc_j, dmx_all, dmc, c_ctx, _in_hbm(ada_w_full))


def _proj(x, ctx, mods, norm_g, w_full, sgu, after):
    lx, lc = x.shape[0], ctx.shape[0]
    assert lc == T
    n = 1 + lx // T

    def body(x_ref, c_ref, sh_ref, sc_ref, ng_ref, w0, w1, w2, w3, w4, g_ref, b_ref, sw_ref, bt_ref, after_ref,
             hn_ref, xa_ref, ga_ref, u_ref, v_ref, gb_ref, ys_ref, mixed_s):
        i = pl.program_id(0)
        is_ctx = i == 0
        xv = jnp.where(is_ctx, c_ref[...], x_ref[...])
        sc = jnp.where(is_ctx, sc_ref[1:2, :], sc_ref[0:1, :])
        sh = jnp.where(is_ctx, sh_ref[1:2, :], sh_ref[0:1, :])
        r = lax.rsqrt(jnp.mean(xv * xv, axis=-1, keepdims=True) + NORM_EPS)
        hb = ((xv * r) * ng_ref[...] * (1.0 + sc) + sh).astype(BF16)
        hn_ref[...] = hb
        xa_ref[...] = _dot(hb, w0[...])

        @pl.when(i > 0)
        def _():
            u_ref[...] = _dot(hb, w2[...])
            v_ref[...] = _dot(hb, w3[...])
            for ch in range(T // HD):
                rows = slice(HD * ch, HD * ch + HD)
                ug = _sgu_parts(u_ref[rows, :], v_ref[rows, :], g_ref[...], b_ref[...], sw_ref, bt_ref, mixed_s)[0]
                ys_ref[rows, :] = ug * mixed_s[...]
            ga_ref[...] = _dot(hb, w1[...])
            gb_ref[...] = _dot(hb, w4[...])

    every = pl.BlockSpec((T, D), lambda i: (i, 0))
    lat = pl.BlockSpec((T, D), lambda i: (jnp.maximum(i - 1, 0), 0))
    vec = pl.BlockSpec((1, D), lambda i: (0, 0))
    in_specs = [lat, pl.BlockSpec((T, D), lambda i: (0, 0)), pl.BlockSpec((8, D), lambda i: (0, 0)),
                pl.BlockSpec((8, D), lambda i: (0, 1)), vec]
    in_specs += [pl.BlockSpec((D, D), lambda i, k=k: (0, k)) for k in range(5)]
    in_specs += [vec, vec, pl.BlockSpec((NH, HD, HD), lambda i: (0, 0, 0)), pl.BlockSpec((HD, NH), lambda i: (0, 0))]
    in_specs += [ANY]
    full_s = jax.ShapeDtypeStruct((lc + lx, D), F32)
    lat_s = jax.ShapeDtypeStruct((lx, D), F32)
    return pl.pallas_call(
        body, name="proj", grid=(n,),
        out_shape=(jax.ShapeDtypeStruct((lc + lx, D), BF16), full_s, lat_s, lat_s, lat_s, lat_s, lat_s),
        in_specs=in_specs, out_specs=(every, every, lat, lat, lat, lat, lat),
        scratch_shapes=[pltpu.VMEM((HD, D), F32)], compiler_params=_cp(1, vmem_mb=56),
    )(x, ctx, mods, mods, norm_g, *([w_full] * 5), *sgu, after)


def _tile_specs(rows_per_pos, width, n_tiles, tile):
    last = n_tiles * (T // 8) - 1
    r = rows_per_pos
    return [pl.BlockSpec((T * r, width), lambda i: (tile(i), 0)),
            pl.BlockSpec((8 * r, width), lambda i: (jnp.maximum(tile(i) * (T // 8) - 1, 0), 0)),
            pl.BlockSpec((8 * r, width), lambda i: (jnp.minimum((tile(i) + 1) * (T // 8), last), 0))]


def _has_prev(tile):
    return tile >= 2


def _has_next(tile, nt):
    return jnp.logical_and(tile >= 1, tile < nt - 1)


ZT = pl.BlockSpec((T * NH, HD), lambda i: (i, 0))
CONV_CHUNK = 32


SCAN_SUB = 8


def _scan_tile(chains, post, carry_ref):
    blk = T // SCAN_SUB

    def step(k, state):
        new = []
        for ci, (a_ref, x_ref, o_ref, q_ref, reverse, xscale) in enumerate(chains):
            for q in range(SCAN_SUB):
                s, p = state[ci * SCAN_SUB + q]
                t = (q + 1) * blk - 1 - k if reverse else q * blk + k
                r = pl.ds(_mo(t * NH, NH), NH)
                a = a_ref[r, :]
                x = x_ref[r, :] if xscale is None else x_ref[r, :] * xscale
                if post:
                    o = x + s
                    o_ref[r, :] = o
                    q_ref[r, :] = p
                    new.append((a * o, a * p))
                else:
                    o = a * s + x
                    p = a * p
                    o_ref[r, :] = o
                    q_ref[r, :] = p
                    new.append((o, p))
        return tuple(new)

    zero = jnp.zeros((NH, HD), F32)
    one = jnp.ones((NH, HD), F32)
    final = lax.fori_loop(0, blk, step, tuple((zero, one) for _ in range(len(chains) * SCAN_SUB)))
    for ci, (a_ref, x_ref, o_ref, q_ref, reverse, xscale) in enumerate(chains):
        carry = carry_ref[ci]
        for q in (range(SCAN_SUB - 1, -1, -1) if reverse else range(SCAN_SUB)):
            rows = pl.ds(q * blk * NH, blk * NH)
            fixed = o_ref[rows, :].reshape(blk, NH, HD) + q_ref[rows, :].reshape(blk, NH, HD) * carry[None]
            o_ref[rows, :] = fixed.reshape(blk * NH, HD)
            s_loc, p_loc = final[ci * SCAN_SUB + q]
            carry = s_loc + p_loc * carry
        carry_ref[ci] = carry


def _lru_fwd(xa, conv_wz, conv_bz, wcat, bcat, lamcat, name):
    lx = xa.shape[0]
    n = lx // T
    tile_u = lambda i: i
    tile_d = lambda i: jnp.where(i == 0, 0, n - i)

    def body(xm_u, xp_u, xn_u, xm_d, xp_d, xn_d, cw, cb, w_ref, b_ref, lam_ref,
             xcz_o, af_o, ab_o, hf_o, hb_o, gf_o, gb_o, fu, fd, pad, xc_d, x_u, x_d, q_u, q_d, carry):
        i = pl.program_id(0)

        @pl.when(i == 0)
        def _():
            carry[...] = jnp.zeros_like(carry)

        def conv_gates(xm, xp, xn, tile, d, xc_ref, a_ref, x_ref, g_ref):
            pmask = jnp.where(_has_prev(tile), 1.0, 0.0)
            nmask = jnp.where(_has_next(tile, n), 1.0, 0.0)
            for h in range(NH):
                cols = slice(HD * h, HD * h + HD)
                pad[_zrows(h, 8), :] = xp[:, cols] * pmask
                pad[pl.ds(8 * NH + h, T, stride=NH), :] = xm[:, cols]
                pad[pl.ds((T + 8) * NH + h, 8, stride=NH), :] = xn[:, cols] * nmask

            def conv_chunk(ci, c_):
                base = pl.multiple_of(ci * (CONV_CHUNK * NH), CONV_CHUNK * NH)
                acc = None
                for k in range(4):
                    sl = pad[pl.ds(base + (7 + k) * NH, CONV_CHUNK * NH), :].reshape(CONV_CHUNK, NH, HD)
                    term = sl * cw[k][None]
                    acc = term if acc is None else acc + term
                acc = acc + cb[...][None]
                xc_ref[pl.ds(base, CONV_CHUNK * NH), :] = acc.reshape(CONV_CHUNK * NH, HD)
                return c_
            lax.fori_loop(0, T // CONV_CHUNK, conv_chunk, 0)

            for h in range(NH):
                xch = xc_ref[_zrows(h, T), :]
                pre = _dot(xch.astype(BF16), w_ref[h, :, 256 * d:256 * d + 256]) + b_ref[h:h + 1, 256 * d:256 * d + 256]
                r, gi, _, _, a, mult = _lru_gate(pre, lam_ref[h:h + 1, :], d, 0)
                a_ref[_zrows(h, T), :] = a
                x_ref[_zrows(h, T), :] = mult * gi * xch
                for q, val in enumerate((r, gi, mult)):
                    g_ref[:, q * D + HD * h:q * D + HD * h + HD] = val

        conv_gates(xm_u, xp_u, xn_u, tile_u(i), 0, xcz_o, af_o, x_u, gf_o)
        conv_gates(xm_d, xp_d, xn_d, tile_d(i), 1, xc_d, ab_o, x_d, gb_o)

        _scan_tile([(af_o, x_u, hf_o, q_u, False, None), (ab_o, x_d, hb_o, q_d, True, None)], False, carry)

        @pl.when(i == 0)
        def _():
            fu[...] = carry[0]
            fd[...] = carry[1]

    full = lambda shape: pl.BlockSpec(shape, lambda i: (0,) * len(shape))
    st = full((NH, HD))
    in_specs = _tile_specs(1, D, n, tile_u) + _tile_specs(1, D, n, tile_d)
    in_specs += [full((4, NH, HD)), st, full((NH, HD, 4 * HD)), full((NH, 4 * HD)), full((NH, 2 * HD))]
    up = pl.BlockSpec((T * NH, HD), lambda i: (tile_u(i), 0))
    dn = pl.BlockSpec((T * NH, HD), lambda i: (tile_d(i), 0))
    zs = jax.ShapeDtypeStruct((lx * NH, HD), F32)
    ss = jax.ShapeDtypeStruct((NH, HD), F32)
    zbuf = pltpu.VMEM((T * NH, HD), F32)
    gs = jax.ShapeDtypeStruct((lx, 3 * D), F32)
    g_up = pl.BlockSpec((T, 3 * D), lambda i: (tile_u(i), 0))
    g_dn = pl.BlockSpec((T, 3 * D), lambda i: (tile_d(i), 0))
    return pl.pallas_call(
        body, name=name, grid=(n,), out_shape=(zs,) * 5 + (gs, gs, ss, ss), in_specs=in_specs,
        out_specs=(up, up, dn, up, dn, g_up, g_dn, st, st),
        scratch_shapes=[pltpu.VMEM(((T + 16) * NH, HD), F32), zbuf, zbuf, zbuf, zbuf, zbuf,
                        pltpu.VMEM((2, NH, HD), F32)],
        compiler_params=_cp(1, vmem_mb=48),
    )(xa, xa, xa, xa, xa, xa, conv_wz, conv_bz, wcat, bcat, lamcat)


def _sgu_parts(u, v, lng, lnb, w_ref, bt_ref, mixed_s):
    ug, dug = _gelu_and_grad(u)
    vg, dvg = _gelu_and_grad(v)
    mu = jnp.mean(vg, axis=-1, keepdims=True)
    vc = vg - mu
    rstd = lax.rsqrt(jnp.mean(vc * vc, axis=-1, keepdims=True) + LN_EPS)
    vh = vc * rstd
    vn = (vh * lng + lnb).astype(BF16)
    for g in range(NH):
        cols = slice(HD * g, HD * g + HD)
        mixed_s[:, cols] = _dot(w_ref[g], vn[:, cols]) + bt_ref[:, g:g + 1]
    return ug, dug, dvg, rstd, vh, vn


def _sgu_bwd_chunk(u, v, dys_v, lng, lnb, w_ref, bt_ref, mixed_s, dvn_s, dw_ref, db_ref, dg_ref, dbl_ref):
    ug, dug, dvg, rstd, vh, vn = _sgu_parts(u, v, lng, lnb, w_ref, bt_ref, mixed_s)
    du = (dys_v * mixed_s[...] * dug).astype(BF16)
    dmix = dys_v * ug
    ones = jnp.ones((8, HD), BF16)
    for g in range(NH):
        cols = slice(HD * g, HD * g + HD)
        dm = dmix[:, cols]
        hi = dm.astype(BF16)
        lo = (dm - hi.astype(F32)).astype(BF16)
        dw_ref[g] += _dot_nt(hi, vn[:, cols])
        db_ref[g:g + 1, :] += (_dot_nt(ones, hi) + _dot_nt(ones, lo))[0:1, :]
        dvn_s[:, cols] = _dot_tn(w_ref[g], hi)
    dvn = dvn_s[...]
    dg_ref[...] += jnp.sum(dvn * vh, axis=0, keepdims=True)
    dbl_ref[...] += jnp.sum(dvn, axis=0, keepdims=True)
    dvh = dvn * lng
    dvg_in = rstd * (dvh - jnp.mean(dvh, axis=-1, keepdims=True) - vh * jnp.mean(dvh * vh, axis=-1, keepdims=True))
    return du, (dvg_in * dvg).astype(BF16)


def _out_fwd_bwd(hf_z, hb_z, ga, gb, ys, x, tgt, mods, final_g, w_out_full):
    lx = x.shape[0]
    n = lx // T

    def body(hf_ref, hb_ref, ga_ref, gb_ref, ys_ref, x_ref, t_ref, gx_ref, fg_ref, w_ref,
             loss_ref, dfg_ref, dgx_ref, dxn_ref, y_ref, do_ref, dga_ref, dgb_ref, dyl_ref, dys_ref, yl_s):
        i = pl.program_id(0)

        @pl.when(i == 0)
        def _():
            loss_ref[...] = jnp.zeros_like(loss_ref)
            dfg_ref[...] = jnp.zeros_like(dfg_ref)
            dgx_ref[...] = jnp.zeros_like(dgx_ref)

        for h in range(NH):
            yl_s[:, HD * h:HD * h + HD] = hf_ref[_zrows(h, T), :] + hb_ref[_zrows(h, T), :]
        yl = yl_s[...]
        gav = ga_ref[...]
        gbv = gb_ref[...]
        sa, dsa = _silu_and_grad(gav)
        sb, dsb = _silu_and_grad(gbv)
        ysv = ys_ref[...]
        y_ref[:, 0:D] = (yl * sa).astype(BF16)
        y_ref[:, D:2 * D] = (ysv * sb).astype(BF16)
        o = _dot(y_ref[...], w_ref[...])
        gx = gx_ref[0:1, :]
        xnew = x_ref[...] + gx * o
        r2 = lax.rsqrt(jnp.mean(xnew * xnew, axis=-1, keepdims=True) + NORM_EPS)
        xh = xnew * r2
        fg = fg_ref[...]
        err = xh * fg - t_ref[...]
        loss_ref[...] += 0.5 * jnp.sum(jnp.mean(err * err, axis=-1, keepdims=True), axis=0, keepdims=True)

        @pl.when(i == n - 1)
        def _():
            lp = loss_ref[...]
            lp1 = lp.astype(BF16).astype(F32)
            lp2 = (lp - lp1).astype(BF16).astype(F32)
            lp3 = (lp - lp1 - lp2).astype(BF16).astype(F32)
            lane = lax.broadcasted_iota(jnp.int32, lp.shape, 1)
            loss_ref[...] = jnp.where(lane == 0, lp1, jnp.where(lane == 1, lp2, jnp.where(lane == 2, lp3, 0.0)))
        dout = err * (1.0 / D)
        dfg_ref[...] += jnp.sum(dout * xh, axis=0, keepdims=True)
        dxh = dout * fg
        dxn = r2 * (dxh - xh * jnp.mean(dxh * xh, axis=-1, keepdims=True))
        dxn_ref[...] = dxn
        dgx_ref[...] += jnp.sum(dxn * o, axis=0, keepdims=True)
        do = (dxn * gx).astype(BF16)
        do_ref[...] = do
        dy = _dot_nt(do, w_ref[...])
        dy1 = dy[:, 0:D]
        dy2 = dy[:, D:2 * D]
        dga_ref[...] = (dy1 * yl * dsa).astype(BF16)
        dgb_ref[...] = (dy2 * ysv * dsb).astype(BF16)
        dys_ref[...] = dy2 * sb
        yl_s[...] = dy1 * sa
        for h in range(NH):
            dyl_ref[_zrows(h, T), :] = yl_s[:, HD * h:HD * h + HD]

    row = pl.BlockSpec((T, D), lambda i: (i, 0))
    vec = pl.BlockSpec((1, D), lambda i: (0, 0))
    zlat = pl.BlockSpec((T * NH, HD), lambda i: (i + 1, 0))
    in_specs = [zlat, zlat, row, row, row, row, row, pl.BlockSpec((8, D), lambda i: (0, 2)), vec,
                pl.BlockSpec((2 * D, D), lambda i: (0, 0))]
    out_shape = (jax.ShapeDtypeStruct((1, D), F32), jax.ShapeDtypeStruct((1, D), F32), jax.ShapeDtypeStruct((1, D), F32),
                 jax.ShapeDtypeStruct((lx, D), F32), jax.ShapeDtypeStruct((lx, 2 * D), BF16),
                 jax.ShapeDtypeStruct((lx, D), BF16), jax.ShapeDtypeStruct((lx, D), BF16),
                 jax.ShapeDtypeStruct((lx, D), BF16), jax.ShapeDtypeStruct((lx * NH, HD), F32),
                 jax.ShapeDtypeStruct((lx, D), F32))
    out_specs = (vec, vec, vec, row, pl.BlockSpec((T, 2 * D), lambda i: (i, 0)),
                 row, row, row, ZT, row)
    return pl.pallas_call(
        body, name="out_fwd_bwd", grid=(n,), out_shape=out_shape, in_specs=in_specs, out_specs=out_specs,
        scratch_shapes=[pltpu.VMEM((T, D), F32)],
        compiler_params=_cp(1, vmem_mb=56),
    )(hf_z, hb_z, ga, gb, ys, x, tgt, mods, final_g, w_out_full)


def _lru_bwd(xc_z, dy_z, hf_z, hb_z, af_z, ab_z, gf, gb, s_b, wcat, lamcat, name):
    lx = xc_z.shape[0] // NH
    n = lx // T
    tile_u = lambda i: jnp.where(i == n - 1, 0, i + 1)
    tile_d = lambda i: n - 1 - i

    def body(xc_u, dy_u, hb_ref, hbn_ref, ab_ref, gb_ref, xc_d, dy_d, hf_ref, hfp_ref, af_ref, gf_ref,
             sb_ref, w_ref, lam_ref, dxcb_ref, dxcf_ref, dw_ref, db_ref, dl_ref,
             lb_s, lf_s, q_u, q_d, pf_s, pb_s, dpre_s, carry):
        i = pl.program_id(0)
        tu, td = tile_u(i), tile_d(i)

        @pl.when(i == 0)
        def _():
            dw_ref[...] = jnp.zeros_like(dw_ref)
            db_ref[...] = jnp.zeros_like(db_ref)
            dl_ref[...] = jnp.zeros_like(dl_ref)
            carry[...] = jnp.zeros_like(carry)

        _scan_tile([(ab_ref, dy_u, lb_s, q_u, False, jnp.where(tu == 0, 0.0, 1.0)),
                    (af_ref, dy_d, lf_s, q_d, True, jnp.where(td == 0, 0.0, 1.0))], True, carry)
        zero = jnp.zeros((NH, HD), F32)
        pb_s[pl.ds(0, T * NH), :] = hb_ref[...]
        pb_s[pl.ds(T * NH, NH), :] = jnp.where(tu == n - 1, sb_ref[...], jnp.where(tu == 0, zero, hbn_ref[pl.ds(0, NH), :]))
        pf_s[pl.ds(0, NH), :] = jnp.where(td == 0, zero, hfp_ref[pl.ds(7 * NH, NH), :])
        pf_s[pl.ds(NH, T * NH), :] = hf_ref[...]
        sides = ((1, xc_u, lb_s, pb_s, NH, ab_ref, gb_ref, dxcb_ref), (0, xc_d, lf_s, pf_s, 0, af_ref, gf_ref, dxcf_ref))
        for d, xc_ref, adj_s, prev_s, prev_off, a_ref, g_ref, dxc_ref in sides:
            wcols = slice(256 * d, 256 * d + 256)
            for h in range(NH):
                xch = xc_ref[_zrows(h, T), :]
                xcb = xch.astype(BF16)
                r, gi, mult = (g_ref[:, q * D + HD * h:q * D + HD * h + HD] for q in range(3))
                a = a_ref[_zrows(h, T), :]
                lam = lam_ref[h:h + 1, HD * d:HD * d + HD]
                sp = _softplus(-lam)
                du = adj_s[_zrows(h, T), :]
                da = du * prev_s[pl.ds(prev_off + h, T, stride=NH), :]
                dgi = du * mult * xch
                dmult = du * gi * xch
                dla = da * a - dmult * (a * a) / mult
                dr = dla * ((-LRU_C) * sp)
                dsp = jnp.sum(dla * ((-LRU_C) * r), axis=0, keepdims=True)
                dl_ref[h:h + 1, HD * d:HD * d + HD] += dsp * (-_sigmoid(-lam))
                dpre_s[:, 0:HD] = dr * r * (1.0 - r)
                dpre_s[:, HD:2 * HD] = dgi * gi * (1.0 - gi)
                dpre = dpre_s[...]
                dpb = dpre.astype(BF16)
                dw_ref[h, :, wcols] += _dot_tn(xcb, dpb)
                db_ref[h:h + 1, wcols] += jnp.sum(dpre, axis=0, keepdims=True)
                dxc_ref[_zrows(h, T), :] = du * mult * gi + _dot_nt(dpb, w_ref[h, :, wcols])

    full = lambda shape: pl.BlockSpec(shape, lambda i: (0,) * len(shape))
    wsp, bsp, lsp = full((NH, HD, 4 * HD)), full((NH, 4 * HD)), full((NH, 2 * HD))
    st = full((NH, HD))
    up = pl.BlockSpec((T * NH, HD), lambda i: (tile_u(i), 0))
    dn = pl.BlockSpec((T * NH, HD), lambda i: (tile_d(i), 0))
    dy_up = pl.BlockSpec((T * NH, HD), lambda i: (jnp.maximum(tile_u(i) - 1, 0), 0))
    dy_dn = pl.BlockSpec((T * NH, HD), lambda i: (jnp.maximum(tile_d(i) - 1, 0), 0))
    nxt = _tile_specs(NH, HD, n, tile_u)[2]
    prv = _tile_specs(NH, HD, n, tile_d)[1]
    g_up = pl.BlockSpec((T, 3 * D), lambda i: (tile_u(i), 0))
    g_dn = pl.BlockSpec((T, 3 * D), lambda i: (tile_d(i), 0))
    zs = jax.ShapeDtypeStruct((lx * NH, HD), F32)
    zbuf = pltpu.VMEM((T * NH, HD), F32)
    zbuf1 = pltpu.VMEM(((T + 1) * NH, HD), F32)
    return pl.pallas_call(
        body, name=name, grid=(n,),
        out_shape=(zs, zs, jax.ShapeDtypeStruct((NH, HD, 4 * HD), F32), jax.ShapeDtypeStruct((NH, 4 * HD), F32),
                   jax.ShapeDtypeStruct((NH, 2 * HD), F32)),
        in_specs=[up, dy_up, up, nxt, up, g_up, dn, dy_dn, dn, prv, dn, g_dn, st, wsp, lsp],
        out_specs=(up, dn, wsp, bsp, lsp),
        scratch_shapes=[zbuf, zbuf, zbuf, zbuf, zbuf1, zbuf1, pltpu.VMEM((T, 2 * HD), F32),
                        pltpu.VMEM((2, NH, HD), F32)],
        compiler_params=_cp(1, vmem_mb=56),
    )(xc_z, dy_z, hb_z, hb_z, ab_z, gb, xc_z, dy_z, hf_z, hf_z, af_z, gf, s_b, wcat, lamcat)


def _conv_bwd(dxc_a, dxc_b, xa, conv_wz, dcw0, dcb0, name):
    lx = dxc_a.shape[0] // NH
    n = lx // T

    def body(dm_a, dp_a, dn_a, dm_b, dp_b, dn_b, xan_ref, cw, dcw0_ref, dcb0_ref, dxa_ref, dcw_ref, dcb_ref,
             pad, dxa_s, xa_ref):
        i = pl.program_id(0)

        @pl.when(i == 0)
        def _():
            dcw_ref[...] = dcw0_ref[...]
            dcb_ref[...] = dcb0_ref[...]

        for h in range(NH):
            xa_ref[_zrows(h, T), :] = xan_ref[:, HD * h:HD * h + HD]

        pmask = jnp.where(_has_prev(i), 1.0, 0.0)
        nmask = jnp.where(_has_next(i, n), 1.0, 0.0)
        pad[pl.ds(0, 8 * NH), :] = (dp_a[...] + dp_b[...]) * pmask
        pad[pl.ds(8 * NH, T * NH), :] = dm_a[...] + dm_b[...]
        pad[pl.ds((T + 8) * NH, 8 * NH), :] = (dn_a[...] + dn_b[...]) * nmask

        def chunk(ci, carry):
            base = pl.multiple_of(ci * (CONV_CHUNK * NH), CONV_CHUNK * NH)
            xav = xa_ref[pl.ds(base, CONV_CHUNK * NH), :].reshape(CONV_CHUNK, NH, HD)
            acc = None
            for k in range(4):
                sl = pad[pl.ds(base + (9 - k) * NH, CONV_CHUNK * NH), :].reshape(CONV_CHUNK, NH, HD)
                term = sl * cw[k][None]
                acc = term if acc is None else acc + term
                dcw_ref[k] += jnp.sum(sl * xav, axis=0)
                if k == 1:
                    dcb_ref[...] += jnp.sum(sl, axis=0)
            dxa_s[pl.ds(base, CONV_CHUNK * NH), :] = acc.reshape(CONV_CHUNK * NH, HD)
            return carry
        lax.fori_loop(0, T // CONV_CHUNK, chunk, 0)
        for h in range(NH):
            dxa_ref[:, HD * h:HD * h + HD] = dxa_s[_zrows(h, T), :].astype(BF16)

    full = lambda shape: pl.BlockSpec(shape, lambda i: (0,) * len(shape))
    return pl.pallas_call(
        body, name=name, grid=(n,),
        out_shape=(jax.ShapeDtypeStruct((lx, D), BF16), jax.ShapeDtypeStruct((4, NH, HD), F32),
                   jax.ShapeDtypeStruct((NH, HD), F32)),
        in_specs=_tile_specs(NH, HD, n, lambda i: i) * 2 + [pl.BlockSpec((T, D), lambda i: (i, 0)), full((4, NH, HD)),
                                                            full((4, NH, HD)), full((NH, HD))],
        out_specs=(pl.BlockSpec((T, D), lambda i: (i, 0)), full((4, NH, HD)), full((NH, HD))),
        scratch_shapes=[pltpu.VMEM(((T + 16) * NH, HD), F32), pltpu.VMEM((T * NH, HD), F32),
                        pltpu.VMEM((T * NH, HD), F32)],
        compiler_params=_cp(1, vmem_mb=48),
    )(dxc_a, dxc_a, dxc_a, dxc_b, dxc_b, dxc_b, xa, conv_wz, dcw0, dcb0)


def _proj_bwd(dxa, dga, dgb, x, ctx, dxn, mods, norm_g, w_full, sgu):
    lx, lc = x.shape[0], ctx.shape[0]
    assert lc == T
    n = 1 + lx // T

    def body(dxa_ref, dga_ref, dgb_ref, w0, w1, w4, w2, w3, x_ref, c_ref, sc_ref, ng_ref, dxn_ref,
             u_ref, v_ref, dy_ref, g_ref, b_ref, sw_ref, bt_ref,
             gx_ref, dng_ref, dscx_ref, dshx_ref, dscc_ref, dshc_ref, du_ref, dv_ref, dws_ref, dbs_ref, dlg_ref, dlb_ref,
             mixed_s, dvn_s):
        i = pl.program_id(0)
        is_ctx = i == 0

        @pl.when(is_ctx)
        def _():
            for acc in (dng_ref, dscx_ref, dshx_ref, dscc_ref, dshc_ref, dws_ref, dbs_ref, dlg_ref, dlb_ref):
                acc[...] = jnp.zeros_like(acc)

        xv = jnp.where(is_ctx, c_ref[...], x_ref[...])
        sc1 = 1.0 + jnp.where(is_ctx, sc_ref[1:2, :], sc_ref[0:1, :])
        r = lax.rsqrt(jnp.mean(xv * xv, axis=-1, keepdims=True) + NORM_EPS)
        xn = xv * r
        ng = ng_ref[...]

        def norm_bwd(dhn, dsc_ref, dsh_ref, with_x):
            t = dhn * xn
            dng_ref[...] += jnp.sum(t * sc1, axis=0, keepdims=True)
            dsc_ref[...] += jnp.sum(t * ng, axis=0, keepdims=True)
            dsh_ref[...] += jnp.sum(dhn, axis=0, keepdims=True)
            if with_x:
                dxh = dhn * (ng * sc1)
                gx_ref[...] = dxn_ref[...] + r * (dxh - xn * jnp.mean(dxh * xn, axis=-1, keepdims=True))

        @pl.when(is_ctx)
        def _():
            norm_bwd(_dot_nt(dxa_ref[...], w0[...]), dscc_ref, dshc_ref, False)

        @pl.when(i > 0)
        def _():
            def sgu_chunk(ch):
                rows = slice(HD * ch, HD * ch + HD)
                du, dv = _sgu_bwd_chunk(u_ref[rows, :], v_ref[rows, :], dy_ref[rows, :], g_ref[...], b_ref[...],
                                        sw_ref, bt_ref, mixed_s, dvn_s, dws_ref, dbs_ref, dlg_ref, dlb_ref)
                du_ref[rows, :] = du
                dv_ref[rows, :] = dv

            dhn = _dot_nt(dxa_ref[...], w0[...])
            sgu_chunk(0)
            dhn = dhn + _dot_nt(dga_ref[...], w1[...])
            for ch in range(1, T // HD):
                sgu_chunk(ch)
            dhn = dhn + _dot_nt(dgb_ref[...], w4[...])
            dhn = dhn + _dot_nt(du_ref[...], w2[...]) + _dot_nt(dv_ref[...], w3[...])
            norm_bwd(dhn, dscx_ref, dshx_ref, True)

    every = pl.BlockSpec((T, D), lambda i: (i, 0))
    lat = pl.BlockSpec((T, D), lambda i: (jnp.maximum(i - 1, 0), 0))
    vec = pl.BlockSpec((1, D), lambda i: (0, 0))
    wsp = pl.BlockSpec((NH, HD, HD), lambda i: (0, 0, 0))
    bsp = pl.BlockSpec((NH, HD), lambda i: (0, 0))
    in_specs = [every, lat, lat] + [pl.BlockSpec((D, D), lambda i, k=k: (0, k)) for k in (0, 1, 4, 2, 3)]
    in_specs += [lat, pl.BlockSpec((T, D), lambda i: (0, 0)), pl.BlockSpec((8, D), lambda i: (0, 1)), vec, lat]
    in_specs += [lat, lat, lat, vec, vec, wsp, pl.BlockSpec((HD, NH), lambda i: (0, 0))]
    vs = jax.ShapeDtypeStruct((1, D), F32)
    zb = jax.ShapeDtypeStruct((lx, D), BF16)
    return pl.pallas_call(
        body, name="proj_bwd", grid=(n,),
        out_shape=(jax.ShapeDtypeStruct((lx, D), F32), vs, vs, vs, vs, vs, zb, zb,
                   jax.ShapeDtypeStruct((NH, HD, HD), F32), jax.ShapeDtypeStruct((NH, HD), F32), vs, vs),
        in_specs=in_specs, out_specs=(lat, vec, vec, vec, vec, vec, lat, lat, wsp, bsp, vec, vec),
        scratch_shapes=[pltpu.VMEM((HD, D), F32), pltpu.VMEM((HD, D), F32)], compiler_params=_cp(1, vmem_mb=56),
    )(dxa, dga, dgb, *([w_full] * 5), x, ctx, mods, norm_g, dxn, *sgu)


def _adam_math(w, g, m, v):
    m = ADAM_B1 * m + (1.0 - ADAM_B1) * g
    v = ADAM_B2 * v + (1.0 - ADAM_B2) * (g * g)
    m_hat = m / (1.0 - ADAM_B1 ** ADAM_STEP)
    v_hat = v / (1.0 - ADAM_B2 ** ADAM_STEP)
    delta = -ADAM_LR * (m_hat / (jnp.sqrt(v_hat) + ADAM_EPS) + ADAM_WD * w)
    return delta, m, v


def _adam_big(items):
    tr = 256
    n = len(items)
    counts = [it[0].shape[0] // tr for it in items]
    offs = [sum(counts[:k]) for k in range(n)]

    def body(*refs):
        i = pl.program_id(0)
        for k in range(n):
            ins, outs = refs[4 * k:4 * k + 4], refs[4 * n + 3 * k:4 * n + 3 * k + 3]

            @pl.when((i >= offs[k]) & (i < offs[k] + counts[k]))
            def _(ins=ins, outs=outs):
                res = _adam_math(*[r[...] for r in ins])
                for o, r in zip(outs, res):
                    o[...] = r

    def blk(k):
        return pl.BlockSpec((tr, items[k][0].shape[1]),
                            lambda i, k=k: (jnp.clip(i - offs[k], 0, counts[k] - 1), 0))

    res = pl.pallas_call(
        body, name="adam_big", grid=(sum(counts),),
        out_shape=tuple(jax.ShapeDtypeStruct(items[k][0].shape, F32) for k in range(n) for _ in range(3)),
        in_specs=[blk(k) for k in range(n) for _ in range(4)],
        out_specs=tuple(blk(k) for k in range(n) for _ in range(3)),
        compiler_params=_cp(1, vmem_mb=56),
    )(*[a for it in items for a in it])
    return [tuple(res[3 * k:3 * k + 3]) for k in range(n)]


def _adam_small(items, tot):
    ni = len(items)
    pieces = [it[1] if isinstance(it[1], list) else None for it in items]
    flat = [a for it, pc in zip(items, pieces) for a in ((it[0], it[2], it[3]) if pc is not None else it)]
    n_in = len(flat) + 1
    out_shape = tuple(jax.ShapeDtypeStruct(it[0].shape, F32) for it, pc in zip(items, pieces)
                      for _ in range(4 if pc is not None else 3))
    n_out = len(out_shape)
    n_loads = sum(3 + (len(pc) if pc is not None else 1) for pc in pieces)

    def body(*refs):
        ins, tot_ref, outs = refs[:n_in - 1], refs[n_in - 1], refs[n_in:n_in + n_out]
        bufs = refs[n_in + n_out:n_in + n_out + 7 * ni]
        sem_in, sem_out = refs[n_in + n_out + 7 * ni:]
        loads, q_in, q_sem = [], 0, 0
        for k, pc in enumerate(pieces):
            w_b, g_b, m_b, v_b = bufs[7 * k:7 * k + 4]
            srcs = [(ins[q_in], w_b)]
            if pc is None:
                srcs.append((ins[q_in + 1], g_b))
                q_in += 1
            else:
                srcs += [(tot_ref.at[pl.ds(r0, nr), pl.ds(c0, nc)], g_b.at[pl.ds(d0, nr), :]) for r0, nr, c0, nc, d0 in pc]
            srcs += [(ins[q_in + 1], m_b), (ins[q_in + 2], v_b)]
            q_in += 3
            mine = []
            for src, dst in srcs:
                mine.append(pltpu.make_async_copy(src, dst, sem_in.at[q_sem]))
                q_sem += 1
            loads.append(mine)
        for mine in loads:
            for cp in mine:
                cp.start()
        stores, q_out = [], 0
        for k, pc in enumerate(pieces):
            for cp in loads[k]:
                cp.wait()
            w_b, g_b, m_b, v_b = bufs[7 * k:7 * k + 4]
            res = _adam_math(w_b[...], g_b[...], m_b[...], v_b[...])
            srcs = []
            for q in range(3):
                bufs[7 * k + 4 + q][...] = res[q]
                srcs.append(bufs[7 * k + 4 + q])
            if pc is not None:
                srcs.append(g_b)
            for src in srcs:
                cp = pltpu.make_async_copy(src, outs[q_out], sem_out.at[q_out])
                cp.start()
                stores.append(cp)
                q_out += 1
        for cp in stores:
            cp.wait()

    scratch = [pltpu.VMEM(it[0].shape, F32) for it in items for _ in range(7)]
    scratch += [pltpu.SemaphoreType.DMA((n_loads,)), pltpu.SemaphoreType.DMA((n_out,))]
    res = pl.pallas_call(
        body, name="adam_small", out_shape=out_shape, in_specs=[HBM] * n_in, out_specs=(HBM,) * n_out,
        scratch_shapes=scratch, compiler_params=_cp(vmem_mb=40),
    )(*[_in_hbm(a) for a in flat], _in_hbm(tot))
    outs, q = [], 0
    for pc in pieces:
        outs.append(tuple(res[q:q + 3]) + ((res[q + 3],) if pc is not None else (None,)))
        q += 4 if pc is not None else 3
    return outs


def kernel(x, c, ctx, c_ctx, ada_w, ada_b, norm_g, w_in, conv_w, conv_b, lru_wa, lru_ba, lru_wx, lru_bx, lru_lambda, sgu_ln_g, sgu_ln_b, sgu_w, sgu_b, w_out, final_g, loss_target, m_c_ctx, m_ada_w, m_ada_b, m_norm_g, m_w_in, m_conv_w, m_conv_b, m_lru_wa, m_lru_ba, m_lru_wx, m_lru_bx, m_lru_lambda, m_sgu_ln_g, m_sgu_ln_b, m_sgu_w, m_sgu_b, m_w_out, m_final_g, v_c_ctx, v_ada_w, v_ada_b, v_norm_g, v_w_in, v_conv_w, v_conv_b, v_lru_wa, v_lru_ba, v_lru_wx, v_lru_bx, v_lru_lambda, v_sgu_ln_g, v_sgu_ln_b, v_sgu_w, v_sgu_b, v_w_out, v_final_g):
    ix, iy, ic = lax.axis_index("x"), lax.axis_index("y"), lax.axis_index("c")
    chip = 2 * ix + iy
    dev = 2 * chip + ic
    lx = x.shape[1]
    lc = ctx.shape[1]

    smalls = jnp.concatenate([conv_w[0], lru_lambda[0], jnp.zeros((10, 256), F32)], axis=0)
    c_ctx2 = c_ctx.reshape(1, D)
    mods, c_slots, _, w_in_full, wo_land, ada_land, wcat, bcat, sgu_wb, conv_wz, lamcat, sgu_bt = _gather_in(
        c, c_ctx2, ada_w[0], ada_b, w_in[0], w_out[0], smalls,
        (lru_wa[0], lru_wx[0], lru_ba[0], lru_bx[0], sgu_w[0], sgu_b[0]))
    wo_ss, wo_rs, ada_ss, ada_rs, wo_land, ada_land, token = _late_gather_start(wo_land, ada_land)
    conv_bz = conv_b.reshape(NH, HD)
    final_g2 = final_g.reshape(1, D)

    zero_s = jnp.zeros((NH, HD), F32)
    hn, xa_all, ga, u, v, gb, ys = _proj(x[0], ctx[0], mods, norm_g, w_in_full, (sgu_ln_g, sgu_ln_b, sgu_wb, sgu_bt),
                                         token)
    xcz, af, ab, hf, hb, gf, gb_l, _, hb0 = _lru_fwd(xa_all, conv_wz, conv_bz, wcat, bcat, lamcat, "lru_fwd")

    w_out_full = _late_gather_wait(wo_land, wo_ss, wo_rs, "w_out", hf, "late_gather_wait_w_out")
    (loss_part, dfg, dgx, dxn, y, do, dga, dgb, dyl_z, dys) = _out_fwd_bwd(
        hf, hb, ga, gb, ys, x[0], loss_target[0], mods, final_g2, w_out_full)

    dxc_b, dxc_f, dwc, dbc, dlc = _lru_bwd(xcz, dyl_z, hf, hb, af, ab, gf, gb_l, hb0, wcat, lamcat, "lru_bwd")
    dxa, dcw, dcb = _conv_bwd(dxc_b, dxc_f, xa_all, conv_wz, jnp.zeros((4, NH, HD), F32), zero_s, "conv_bwd")
    dxa = _in_hbm(dxa)

    grad_x, dng, dsc_x, dsh_x, dsc_c, dsh_c, du, dv, d_sgu_w, d_sgu_b, d_ln_g, d_ln_b = _proj_bwd(
        dxa, dga, dgb, x[0], ctx[0], dxn, mods, norm_g, w_in_full, (u, v, dys, sgu_ln_g, sgu_ln_b, sgu_wb, sgu_bt))
    dzs = [dxa, dga, du, dv, dgb]

    dmx = jnp.concatenate([dsh_x, dsc_x, dgx], axis=0)
    dmc = jnp.concatenate([dsh_c, dsc_c, jnp.zeros((1, D), F32)], axis=0)
    slot = jnp.concatenate([dmx, loss_part], axis=0)
    slots = lax.dynamic_update_slice(jnp.zeros((32, D), F32), slot, (4 * dev, 0))
    vecs = jnp.concatenate([dfg, dng, dcb.reshape(1, D), d_ln_g, d_ln_b, dcw.reshape(4, D), dmc,
                            jnp.zeros((4, D), F32), slots], axis=0)
    d_sgu_w4 = d_sgu_w.reshape(4, 256, HD).transpose(1, 0, 2).reshape(256, 4 * HD)
    pad8 = lambda a: jnp.pad(a, ((0, 8 - a.shape[0]), (0, 4 * HD - a.shape[1])))
    pack = jnp.concatenate([dwc.reshape(NH * HD, 4 * HD), pad8(dbc), pad8(dlc), d_sgu_w4, pad8(d_sgu_b),
                            vecs.reshape(96, 4 * HD), jnp.zeros((8, 4 * HD), F32)], axis=0)
    g_w_in, g_w_out, tot = _grads_reduce(hn, dzs, lc, y, do, _in_hbm(pack))

    n_w = NH * HD
    g_lru_wa = [(0, n_w, 2 * HD * d, HD, n_w * d) for d in range(2)]
    g_lru_wx = [(0, n_w, 2 * HD * d + HD, HD, n_w * d) for d in range(2)]
    g_lru_ba = [(n_w, NH, 2 * HD * d, HD, NH * d) for d in range(2)]
    g_lru_bx = [(n_w, NH, 2 * HD * d + HD, HD, NH * d) for d in range(2)]
    g_sgu_w = [(1040, 256, HD * q, HD, 256 * q) for q in range(4)]
    g_sgu_b = [(1296, NH, 0, HD, 0)]
    g_lc = tot[1032:1040, 0:2 * HD]
    tv = tot[1304:1400].reshape(48, D)
    g_final_g, g_norm_g, g_conv_b, g_ln_g, g_ln_b = tv[0:1], tv[1:2], tv[2:3], tv[3:4], tv[4:5]
    g_conv_w_full = tv[5:9]
    dmc_tot = tv[9:12].reshape(1, 3 * D)
    slots_all = tv[16:48].reshape(8, 4, D)
    dmx_all = slots_all[:, 0:3, :].reshape(8, 3 * D)
    c_all = c_slots.reshape(8, 8, D)[:, 0, :]
    g_lam_full = jnp.stack([g_lc[:, 0:HD], g_lc[:, HD:2 * HD]]).reshape(2, D)
    g_conv_w = lax.dynamic_slice(g_conv_w_full, (0, 256 * chip), (4, 256))
    g_lam = lax.dynamic_slice(g_lam_full, (0, 256 * chip), (2, 256))
    dmx_all_j = lax.dynamic_slice(dmx_all, (0, 768 * chip), (8, 768))
    dmc_j = lax.dynamic_slice(dmc_tot, (0, 768 * chip), (1, 768))
    ada_full = _late_gather_wait(ada_land, ada_ss, ada_rs, "ada_w", _in_hbm(tot), "late_gather_wait_ada_w")
    g_ada_w, g_ada_b, g_c_ctx = _ada_bwd(c_all, dmx_all_j, dmc_j, dmx_all, dmc_tot, c_ctx2, ada_full)

    big = dict(zip(("w_in", "w_out", "ada_w"), _adam_big([
        (w_in[0], g_w_in, m_w_in[0], v_w_in[0]), (w_out[0], g_w_out, m_w_out[0], v_w_out[0]),
        (ada_w[0], g_ada_w, m_ada_w[0], v_ada_w[0])])))
    small_in = {
        "c_ctx": (c_ctx, g_c_ctx, m_c_ctx, v_c_ctx, (1, D)),
        "ada_b": (ada_b, g_ada_b, m_ada_b, v_ada_b, (1, 3 * D)),
        "norm_g": (norm_g, g_norm_g, m_norm_g, v_norm_g, (1, D)),
        "conv_w": (conv_w, g_conv_w, m_conv_w, v_conv_w, (4, 256)),
        "conv_b": (conv_b, g_conv_b, m_conv_b, v_conv_b, (1, D)),
        "lru_wa": (lru_wa, g_lru_wa, m_lru_wa, v_lru_wa, (2 * NH * HD, HD)),
        "lru_ba": (lru_ba, g_lru_ba, m_lru_ba, v_lru_ba, (2 * NH, HD)),
        "lru_wx": (lru_wx, g_lru_wx, m_lru_wx, v_lru_wx, (2 * NH * HD, HD)),
        "lru_bx": (lru_bx, g_lru_bx, m_lru_bx, v_lru_bx, (2 * NH, HD)),
        "lru_lambda": (lru_lambda, g_lam, m_lru_lambda, v_lru_lambda, (2, 256)),
        "sgu_ln_g": (sgu_ln_g, g_ln_g, m_sgu_ln_g, v_sgu_ln_g, (1, D)),
        "sgu_ln_b": (sgu_ln_b, g_ln_b, m_sgu_ln_b, v_sgu_ln_b, (1, D)),
        "sgu_w": (sgu_w, g_sgu_w, m_sgu_w, v_sgu_w, (NH * HD, HD)),
        "sgu_b": (sgu_b, g_sgu_b, m_sgu_b, v_sgu_b, (NH, HD)),
        "final_g": (final_g, g_final_g, m_final_g, v_final_g, (1, D)),
    }
    names_small = list(small_in)
    res_small = _adam_small([tuple(a if isinstance(a, list) else a.reshape(small_in[k][4]) for a in small_in[k][:4])
                             for k in names_small], tot)
    full_shapes = {"ada_w": ada_w.shape, "w_in": w_in.shape, "w_out": w_out.shape}
    grads, deltas, new_m, new_v = {}, {}, {}, {}
    for k in ("ada_w", "w_in", "w_out"):
        g = {"ada_w": g_ada_w, "w_in": g_w_in, "w_out": g_w_out}[k]
        grads[k] = g.reshape(full_shapes[k])
        deltas[k], new_m[k], new_v[k] = (a.reshape(full_shapes[k]) for a in big[k])
    for k, res in zip(names_small, res_small):
        shape = small_in[k][0].shape
        grads[k] = (small_in[k][1] if res[3] is None else res[3]).reshape(shape)
        deltas[k], new_m[k], new_v[k] = (a.reshape(shape) for a in res[:3])

    loss = jnp.sum(slots_all[:, 3, 0:3])
    order = ["c_ctx", "ada_w", "ada_b", "norm_g", "w_in", "conv_w", "conv_b", "lru_wa", "lru_ba", "lru_wx", "lru_bx",
             "lru_lambda", "sgu_ln_g", "sgu_ln_b", "sgu_w", "sgu_b", "w_out", "final_g"]
    return (loss, grad_x.reshape(x.shape), *[grads[k] for k in order], *[deltas[k] for k in order],
            *[new_m[k] for k in order], *[new_v[k] for k in order])
```
